```python
import jax, jax.numpy as jnp
from jax import lax
import numpy as np

D_MODEL = 1024
BATCH = 8
SEQ = 2048
DEPTH = 2

N_META = 16
FOX_HEADS = 8
FOX_HEAD_DIM = 64
FOX_WIDTH = FOX_HEADS * FOX_HEAD_DIM
Q_BLOCK = 128
CONV_CH = D_MODEL - FOX_WIDTH
CONV_WIDTH = 31
IN_COLS = 3 * FOX_WIDTH + FOX_HEADS + 2 * CONV_CH
POOL_WINDOWS = (2, 4, 8, 16)
N_POOL_GROUPS = len(POOL_WINDOWS)
POOL_GROUP = D_MODEL // N_POOL_GROUPS
D_FF = 2816
FFN_CONV_WIDTH = 3
RMS_EPS = 1e-6
LN_EPS = 1e-5
N_EVEN = (DEPTH + 1) // 2
N_ODD = DEPTH // 2

kernel_name = "fox_conformer_pool_hybrid_block"


def rms_norm(x, g):
    xf = x.astype(jnp.float32)
    y = xf * lax.rsqrt(jnp.mean(xf * xf, axis=-1, keepdims=True) + RMS_EPS)
    return (y * g.astype(jnp.float32)).astype(x.dtype)


def layer_norm(x, g, b):
    xf = x.astype(jnp.float32)
    mu = jnp.mean(xf, axis=-1, keepdims=True)
    var = jnp.mean(jnp.square(xf - mu), axis=-1, keepdims=True)
    y = (xf - mu) * lax.rsqrt(var + LN_EPS)
    return (y * g.astype(jnp.float32) + b.astype(jnp.float32)).astype(x.dtype)


def causal_depthwise_conv(x, w, b):
    K, C = w.shape
    y = lax.conv_general_dilated(
        x, w[:, None, :].astype(x.dtype), window_strides=(1,), padding=[(K - 1, 0)],
        dimension_numbers=('NWC', 'WIO', 'NWC'), feature_group_count=C)
    return y + b.astype(x.dtype)


def forgetting_attention(q, k, v, log_f):
    B, L, H, Dh = q.shape
    n_blk = (L - N_META) // Q_BLOCK
    qf, kf, vf = (a.astype(jnp.float32) for a in (q, k, v))
    c = jnp.transpose(jnp.cumsum(log_f, axis=1), (0, 2, 1))
    scale = Dh ** -0.5
    pos = jnp.arange(L)

    def attend(q_blk, c_q, q_pos):
        s = jnp.einsum('bqhd,bkhd->bhqk', q_blk, kf) * scale
        s = s + c_q[..., :, None] - c[..., None, :]
        s = jnp.where(pos[None, :] <= q_pos[:, None], s, -jnp.inf)
        p = jax.nn.softmax(s, axis=-1)
        return jnp.einsum('bhqk,bkhd->bqhd', p, vf)

    out_meta = attend(qf[:, :N_META], c[:, :, :N_META], pos[:N_META])
    q_r = jnp.transpose(qf[:, N_META:].reshape(B, n_blk, Q_BLOCK, H, Dh), (1, 0, 2, 3, 4))
    c_r = jnp.transpose(c[:, :, N_META:].reshape(B, H, n_blk, Q_BLOCK), (2, 0, 1, 3))
    p_r = pos[N_META:].reshape(n_blk, Q_BLOCK)
    out_r = lax.map(lambda a: attend(*a), (q_r, c_r, p_r))
    out_r = jnp.transpose(out_r, (1, 0, 2, 3, 4)).reshape(B, L - N_META, H, Dh)
    out = jnp.concatenate([out_meta, out_r], axis=1)
    return out.reshape(B, L, H * Dh)


def fox_conformer_mixer(h, w_in, b_f, conv_w, conv_b, ln_g, ln_b, w_out):
    B, L, _ = h.shape
    proj = h @ w_in.astype(h.dtype)
    q, k, v, f_logit, glu = jnp.split(
        proj, [FOX_WIDTH, 2 * FOX_WIDTH, 3 * FOX_WIDTH, 3 * FOX_WIDTH + FOX_HEADS], axis=-1)
    log_f = jax.nn.log_sigmoid(f_logit.astype(jnp.float32) + b_f.astype(jnp.float32))
    shp = (B, L, FOX_HEADS, FOX_HEAD_DIM)
    attn = forgetting_attention(q.reshape(shp), k.reshape(shp), v.reshape(shp), log_f).astype(h.dtype)
    a, g = jnp.split(glu, 2, axis=-1)
    u = a * jax.nn.sigmoid(g)
    u = causal_depthwise_conv(u, conv_w, conv_b)
    u = jax.nn.silu(layer_norm(u, ln_g, ln_b))
    return jnp.concatenate([attn, u], axis=-1) @ w_out.astype(h.dtype)


def multiscale_pool_mixer(h, pool_w, pool_b, pool_scale):
    B, L, D = h.shape
    hf = h.astype(jnp.float32).reshape(B, L, N_POOL_GROUPS, POOL_GROUP)
    cs = jnp.cumsum(hf, axis=1)
    n_seen = jnp.arange(1, L + 1)
    outs = []
    for gi, w in enumerate(POOL_WINDOWS):
        csg = cs[:, :, gi]
        lag = jnp.pad(csg, ((0, 0), (w, 0), (0, 0)))[:, :L]
        cnt = jnp.minimum(n_seen, w).astype(jnp.float32)[None, :, None]
        outs.append((csg - lag) / cnt - hf[:, :, gi])
    d = jnp.stack(outs, axis=2)
    y = jnp.einsum('blgc,gcd->blgd', d, pool_w.astype(jnp.float32)) + pool_b.astype(jnp.float32)
    return (y.reshape(B, L, D) * pool_scale.astype(jnp.float32)).astype(h.dtype)


def conv_glu_ffn(h, w_up, conv_w, conv_b, w_down):
    u = h @ w_up.astype(h.dtype)
    u = causal_depthwise_conv(u, conv_w, conv_b)
    gate, val = jnp.split(u, 2, axis=-1)
    return (jax.nn.silu(gate) * val) @ w_down.astype(h.dtype)


def _fwd_setup_inputs(seed: int = 0) -> dict:
    key = jax.random.key(seed)
    ks = jax.random.split(key, 24)
    nrm = lambda k, shp, s: jax.random.normal(k, shp, jnp.float32) * s
    D = D_MODEL
    return {
        "x": nrm(ks[0], (BATCH, SEQ, D), 1.0),
        "meta_tokens": nrm(ks[1], (N_META, D), 1.0),
        "mix_norm_even": 1.0 + nrm(ks[2], (N_EVEN, D), 0.02),
        "w_in": nrm(ks[3], (N_EVEN, D, IN_COLS), D ** -0.5),
        "b_f": jax.random.uniform(ks[4], (N_EVEN, FOX_HEADS), jnp.float32, 2.0, 5.0),
        "conv_w": nrm(ks[5], (N_EVEN, CONV_WIDTH, CONV_CH), CONV_WIDTH ** -0.5),
        "conv_b": nrm(ks[6], (N_EVEN, CONV_CH), 0.02),
        "ln_g": 1.0 + nrm(ks[7], (N_EVEN, CONV_CH), 0.02),
        "ln_b": nrm(ks[8], (N_EVEN, CONV_CH), 0.02),
        "w_out": nrm(ks[9], (N_EVEN, FOX_WIDTH + CONV_CH, D), (FOX_WIDTH + CONV_CH) ** -0.5),
        "mix_norm_odd": 1.0 + nrm(ks[10], (N_ODD, D), 0.02),
        "pool_w": nrm(ks[11], (N_ODD, N_POOL_GROUPS, POOL_GROUP, POOL_GROUP), POOL_GROUP ** -0.5),
        "pool_b": nrm(ks[12], (N_ODD, N_POOL_GROUPS, POOL_GROUP), 0.02),
        "pool_scale": 0.5 + nrm(ks[13], (N_ODD, D), 0.1),
        "ffn_norm": 1.0 + nrm(ks[14], (DEPTH, D), 0.02),
        "w_up": nrm(ks[15], (DEPTH, D, 2 * D_FF), D ** -0.5),
        "ffn_conv_w": nrm(ks[16], (DEPTH, FFN_CONV_WIDTH, 2 * D_FF), FFN_CONV_WIDTH ** -0.5),
        "ffn_conv_b": nrm(ks[17], (DEPTH, 2 * D_FF), 0.02),
        "w_down": nrm(ks[18], (DEPTH, D_FF, D), D_FF ** -0.5),
        "final_norm": 1.0 + nrm(ks[19], (D,), 0.02),
    }


def _fwd_reference(x, meta_tokens, mix_norm_even, w_in, b_f, conv_w, conv_b, ln_g, ln_b, w_out,
              mix_norm_odd, pool_w, pool_b, pool_scale,
              ffn_norm, w_up, ffn_conv_w, ffn_conv_b, w_down, final_norm):
    B = x.shape[0]
    meta = jnp.broadcast_to(meta_tokens.astype(x.dtype)[None], (B, N_META, x.shape[-1]))
    h = jnp.concatenate([meta, x], axis=1)
    for i in range(DEPTH):
        j = i // 2
        if i % 2 == 0:
            h = h + fox_conformer_mixer(rms_norm(h, mix_norm_even[j]), w_in[j], b_f[j], conv_w[j],
                                        conv_b[j], ln_g[j], ln_b[j], w_out[j])
        else:
            h = h + multiscale_pool_mixer(rms_norm(h, mix_norm_odd[j]), pool_w[j], pool_b[j],
                                          pool_scale[j])
        h = h + conv_glu_ffn(rms_norm(h, ffn_norm[i]), w_up[i], ffn_conv_w[i], ffn_conv_b[i],
                             w_down[i])
    h = rms_norm(h, final_norm)
    return h[:, N_META:]


import jax as _jax
import jax.numpy as _jnp

TWIN_FORMAT = 'train_step'
FWD_PARAMS = ['x', 'meta_tokens', 'mix_norm_even', 'w_in', 'b_f', 'conv_w', 'conv_b', 'ln_g', 'ln_b', 'w_out', 'mix_norm_odd', 'pool_w', 'pool_b', 'pool_scale', 'ffn_norm', 'w_up', 'ffn_conv_w', 'ffn_conv_b', 'w_down', 'final_norm']
TWIN_WEIGHTS = ['meta_tokens', 'mix_norm_even', 'w_in', 'b_f', 'conv_w', 'conv_b', 'ln_g', 'ln_b', 'w_out', 'mix_norm_odd', 'pool_w', 'pool_b', 'pool_scale', 'ffn_norm', 'w_up', 'ffn_conv_w', 'ffn_conv_b', 'w_down', 'final_norm']
TWIN_DIFF_INPUT = 'x'
TWIN_INPUTS = ['x', 'meta_tokens', 'mix_norm_even', 'w_in', 'b_f', 'conv_w', 'conv_b', 'ln_g', 'ln_b', 'w_out', 'mix_norm_odd', 'pool_w', 'pool_b', 'pool_scale', 'ffn_norm', 'w_up', 'ffn_conv_w', 'ffn_conv_b', 'w_down', 'final_norm', 'loss_target', 'm_meta_tokens', 'm_mix_norm_even', 'm_w_in', 'm_b_f', 'm_conv_w', 'm_conv_b', 'm_ln_g', 'm_ln_b', 'm_w_out', 'm_mix_norm_odd', 'm_pool_w', 'm_pool_b', 'm_pool_scale', 'm_ffn_norm', 'm_w_up', 'm_ffn_conv_w', 'm_ffn_conv_b', 'm_w_down', 'm_final_norm', 'v_meta_tokens', 'v_mix_norm_even', 'v_w_in', 'v_b_f', 'v_conv_w', 'v_conv_b', 'v_ln_g', 'v_ln_b', 'v_w_out', 'v_mix_norm_odd', 'v_pool_w', 'v_pool_b', 'v_pool_scale', 'v_ffn_norm', 'v_w_up', 'v_ffn_conv_w', 'v_ffn_conv_b', 'v_w_down', 'v_final_norm']
TWIN_OUTPUTS = ['loss', 'grad_x', 'grad_meta_tokens', 'grad_mix_norm_even', 'grad_w_in', 'grad_b_f', 'grad_conv_w', 'grad_conv_b', 'grad_ln_g', 'grad_ln_b', 'grad_w_out', 'grad_mix_norm_odd', 'grad_pool_w', 'grad_pool_b', 'grad_pool_scale', 'grad_ffn_norm', 'grad_w_up', 'grad_ffn_conv_w', 'grad_ffn_conv_b', 'grad_w_down', 'grad_final_norm', 'delta_meta_tokens', 'delta_mix_norm_even', 'delta_w_in', 'delta_b_f', 'delta_conv_w', 'delta_conv_b', 'delta_ln_g', 'delta_ln_b', 'delta_w_out', 'delta_mix_norm_odd', 'delta_pool_w', 'delta_pool_b', 'delta_pool_scale', 'delta_ffn_norm', 'delta_w_up', 'delta_ffn_conv_w', 'delta_ffn_conv_b', 'delta_w_down', 'delta_final_norm', 'new_m_meta_tokens', 'new_m_mix_norm_even', 'new_m_w_in', 'new_m_b_f', 'new_m_conv_w', 'new_m_conv_b', 'new_m_ln_g', 'new_m_ln_b', 'new_m_w_out', 'new_m_mix_norm_odd', 'new_m_pool_w', 'new_m_pool_b', 'new_m_pool_scale', 'new_m_ffn_norm', 'new_m_w_up', 'new_m_ffn_conv_w', 'new_m_ffn_conv_b', 'new_m_w_down', 'new_m_final_norm', 'new_v_meta_tokens', 'new_v_mix_norm_even', 'new_v_w_in', 'new_v_b_f', 'new_v_conv_w', 'new_v_conv_b', 'new_v_ln_g', 'new_v_ln_b', 'new_v_w_out', 'new_v_mix_norm_odd', 'new_v_pool_w', 'new_v_pool_b', 'new_v_pool_scale', 'new_v_ffn_norm', 'new_v_w_up', 'new_v_ffn_conv_w', 'new_v_ffn_conv_b', 'new_v_w_down', 'new_v_final_norm']
TWIN_LEAF_KINDS = {'loss': 'loss', 'grad_x': 'grad_x', 'grad_meta_tokens': 'grad_w', 'grad_mix_norm_even': 'grad_w', 'grad_w_in': 'grad_w', 'grad_b_f': 'grad_w', 'grad_conv_w': 'grad_w', 'grad_conv_b': 'grad_w', 'grad_ln_g': 'grad_w', 'grad_ln_b': 'grad_w', 'grad_w_out': 'grad_w', 'grad_mix_norm_odd': 'grad_w', 'grad_pool_w': 'grad_w', 'grad_pool_b': 'grad_w', 'grad_pool_scale': 'grad_w', 'grad_ffn_norm': 'grad_w', 'grad_w_up': 'grad_w', 'grad_ffn_conv_w': 'grad_w', 'grad_ffn_conv_b': 'grad_w', 'grad_w_down': 'grad_w', 'grad_final_norm': 'grad_w', 'delta_meta_tokens': 'delta_w', 'delta_mix_norm_even': 'delta_w', 'delta_w_in': 'delta_w', 'delta_b_f': 'delta_w', 'delta_conv_w': 'delta_w', 'delta_conv_b': 'delta_w', 'delta_ln_g': 'delta_w', 'delta_ln_b': 'delta_w', 'delta_w_out': 'delta_w', 'delta_mix_norm_odd': 'delta_w', 'delta_pool_w': 'delta_w', 'delta_pool_b': 'delta_w', 'delta_pool_scale': 'delta_w', 'delta_ffn_norm': 'delta_w', 'delta_w_up': 'delta_w', 'delta_ffn_conv_w': 'delta_w', 'delta_ffn_conv_b': 'delta_w', 'delta_w_down': 'delta_w', 'delta_final_norm': 'delta_w', 'new_m_meta_tokens': 'new_m', 'new_m_mix_norm_even': 'new_m', 'new_m_w_in': 'new_m', 'new_m_b_f': 'new_m', 'new_m_conv_w': 'new_m', 'new_m_conv_b': 'new_m', 'new_m_ln_g': 'new_m', 'new_m_ln_b': 'new_m', 'new_m_w_out': 'new_m', 'new_m_mix_norm_odd': 'new_m', 'new_m_pool_w': 'new_m', 'new_m_pool_b': 'new_m', 'new_m_pool_scale': 'new_m', 'new_m_ffn_norm': 'new_m', 'new_m_w_up': 'new_m', 'new_m_ffn_conv_w': 'new_m', 'new_m_ffn_conv_b': 'new_m', 'new_m_w_down': 'new_m', 'new_m_final_norm': 'new_m', 'new_v_meta_tokens': 'new_v', 'new_v_mix_norm_even': 'new_v', 'new_v_w_in': 'new_v', 'new_v_b_f': 'new_v', 'new_v_conv_w': 'new_v', 'new_v_conv_b': 'new_v', 'new_v_ln_g': 'new_v', 'new_v_ln_b': 'new_v', 'new_v_w_out': 'new_v', 'new_v_mix_norm_odd': 'new_v', 'new_v_pool_w': 'new_v', 'new_v_pool_b': 'new_v', 'new_v_pool_scale': 'new_v', 'new_v_ffn_norm': 'new_v', 'new_v_w_up': 'new_v', 'new_v_ffn_conv_w': 'new_v', 'new_v_ffn_conv_b': 'new_v', 'new_v_w_down': 'new_v', 'new_v_final_norm': 'new_v'}


def _forward(args):
    return _fwd_reference(*[args[k] for k in FWD_PARAMS])


def _output_shape():
    out = _jax.eval_shape(lambda: _forward(_fwd_setup_inputs(0)))
    return out.shape, out.dtype

N_MICROBATCH = 1
ADAM_LR = 0.001
ADAM_B1 = 0.9
ADAM_B2 = 0.999
ADAM_EPS = 1e-08
ADAM_WD = 0.01
ADAM_STEP = 10
PER_EXAMPLE_BATCH_AXIS = {'x': 0, 'loss_target': 0}
SHARED_INPUTS = []
_WEIGHT_DTYPES = {'meta_tokens': _jnp.float32, 'mix_norm_even': _jnp.float32, 'w_in': _jnp.float32, 'b_f': _jnp.float32, 'conv_w': _jnp.float32, 'conv_b': _jnp.float32, 'ln_g': _jnp.float32, 'ln_b': _jnp.float32, 'w_out': _jnp.float32, 'mix_norm_odd': _jnp.float32, 'pool_w': _jnp.float32, 'pool_b': _jnp.float32, 'pool_scale': _jnp.float32, 'ffn_norm': _jnp.float32, 'w_up': _jnp.float32, 'ffn_conv_w': _jnp.float32, 'ffn_conv_b': _jnp.float32, 'w_down': _jnp.float32, 'final_norm': _jnp.float32}
MOMENT_SCALE = {'meta_tokens': 4.819918e-03, 'mix_norm_even': 7.789585e-02, 'w_in': 4.968963e-02, 'b_f': 2.506935e-01, 'conv_w': 8.530188e-02, 'conv_b': 1.438201e-01, 'ln_g': 9.580888e-02, 'ln_b': 7.794114e-02, 'w_out': 6.458120e-02, 'mix_norm_odd': 4.566895e-02, 'pool_w': 4.552459e-02, 'pool_b': 5.297330e-02, 'pool_scale': 1.111346e-01, 'ffn_norm': 8.849685e-02, 'w_up': 3.640254e-02, 'ffn_conv_w': 3.657595e-02, 'ffn_conv_b': 3.528567e-02, 'w_down': 5.955030e-02, 'final_norm': 1.601485e+01}


def _to_microbatches(a, axis):
    t = _jnp.moveaxis(a, axis, 0)
    t = t.reshape((N_MICROBATCH, t.shape[0] // N_MICROBATCH) + t.shape[1:])
    return _jnp.moveaxis(t, 1, axis + 1)


def setup_inputs(seed: int = 0) -> dict:
    inp = _fwd_setup_inputs(seed)
    key = _jax.random.fold_in(_jax.random.key(seed), 7919)
    shape, _ = _output_shape()
    out = dict(inp)
    out["loss_target"] = _jax.random.normal(_jax.random.fold_in(key, 0), shape, _jnp.float32)
    for i, name in enumerate(TWIN_WEIGHTS):
        w = inp[name].astype(_jnp.float32)
        if MOMENT_SCALE is None:
            s = _jnp.sqrt(_jnp.mean(_jnp.square(w)) + 1e-30)
        else:
            s = MOMENT_SCALE[name]
        km, kv = _jax.random.split(_jax.random.fold_in(key, i + 1))
        out[name] = w
        out["m_" + name] = s * _jax.random.normal(km, w.shape, _jnp.float32)
        out["v_" + name] = (s * s) * _jax.random.uniform(kv, w.shape, _jnp.float32, 0.5, 1.5)
    if N_MICROBATCH > 1:
        for name, axis in PER_EXAMPLE_BATCH_AXIS.items():
            out[name] = _to_microbatches(out[name], axis)
    return {'x': out['x'], 'meta_tokens': out['meta_tokens'], 'mix_norm_even': out['mix_norm_even'], 'w_in': out['w_in'], 'b_f': out['b_f'], 'conv_w': out['conv_w'], 'conv_b': out['conv_b'], 'ln_g': out['ln_g'], 'ln_b': out['ln_b'], 'w_out': out['w_out'], 'mix_norm_odd': out['mix_norm_odd'], 'pool_w': out['pool_w'], 'pool_b': out['pool_b'], 'pool_scale': out['pool_scale'], 'ffn_norm': out['ffn_norm'], 'w_up': out['w_up'], 'ffn_conv_w': out['ffn_conv_w'], 'ffn_conv_b': out['ffn_conv_b'], 'w_down': out['w_down'], 'final_norm': out['final_norm'], 'loss_target': out['loss_target'], 'm_meta_tokens': out['m_meta_tokens'], 'm_mix_norm_even': out['m_mix_norm_even'], 'm_w_in': out['m_w_in'], 'm_b_f': out['m_b_f'], 'm_conv_w': out['m_conv_w'], 'm_conv_b': out['m_conv_b'], 'm_ln_g': out['m_ln_g'], 'm_ln_b': out['m_ln_b'], 'm_w_out': out['m_w_out'], 'm_mix_norm_odd': out['m_mix_norm_odd'], 'm_pool_w': out['m_pool_w'], 'm_pool_b': out['m_pool_b'], 'm_pool_scale': out['m_pool_scale'], 'm_ffn_norm': out['m_ffn_norm'], 'm_w_up': out['m_w_up'], 'm_ffn_conv_w': out['m_ffn_conv_w'], 'm_ffn_conv_b': out['m_ffn_conv_b'], 'm_w_down': out['m_w_down'], 'm_final_norm': out['m_final_norm'], 'v_meta_tokens': out['v_meta_tokens'], 'v_mix_norm_even': out['v_mix_norm_even'], 'v_w_in': out['v_w_in'], 'v_b_f': out['v_b_f'], 'v_conv_w': out['v_conv_w'], 'v_conv_b': out['v_conv_b'], 'v_ln_g': out['v_ln_g'], 'v_ln_b': out['v_ln_b'], 'v_w_out': out['v_w_out'], 'v_mix_norm_odd': out['v_mix_norm_odd'], 'v_pool_w': out['v_pool_w'], 'v_pool_b': out['v_pool_b'], 'v_pool_scale': out['v_pool_scale'], 'v_ffn_norm': out['v_ffn_norm'], 'v_w_up': out['v_w_up'], 'v_ffn_conv_w': out['v_ffn_conv_w'], 'v_ffn_conv_b': out['v_ffn_conv_b'], 'v_w_down': out['v_w_down'], 'v_final_norm': out['v_final_norm']}


def _loss(weights, diff, rest, loss_target):
    with _jax.named_scope("forward"):
        args = {**rest, TWIN_DIFF_INPUT: diff, **{k: w.astype(_WEIGHT_DTYPES[k]) for k, w in weights.items()}}
        y = _forward(args)
    with _jax.named_scope("loss_head"):
        err = _jnp.square(y.astype(_jnp.float32) - loss_target)
        return 0.5 * _jnp.sum(_jnp.mean(err, axis=-1)) if err.ndim else 0.5 * err


def _adamw(w, g, m, v):
    m = ADAM_B1 * m + (1.0 - ADAM_B1) * g
    v = ADAM_B2 * v + (1.0 - ADAM_B2) * _jnp.square(g)
    m_hat = m / (1.0 - ADAM_B1 ** ADAM_STEP)
    v_hat = v / (1.0 - ADAM_B2 ** ADAM_STEP)
    delta = -ADAM_LR * (m_hat / (_jnp.sqrt(v_hat) + ADAM_EPS) + ADAM_WD * w)
    return delta, m, v


def reference(x, meta_tokens, mix_norm_even, w_in, b_f, conv_w, conv_b, ln_g, ln_b, w_out, mix_norm_odd, pool_w, pool_b, pool_scale, ffn_norm, w_up, ffn_conv_w, ffn_conv_b, w_down, final_norm, loss_target, m_meta_tokens, m_mix_norm_even, m_w_in, m_b_f, m_conv_w, m_conv_b, m_ln_g, m_ln_b, m_w_out, m_mix_norm_odd, m_pool_w, m_pool_b, m_pool_scale, m_ffn_norm, m_w_up, m_ffn_conv_w, m_ffn_conv_b, m_w_down, m_final_norm, v_meta_tokens, v_mix_norm_even, v_w_in, v_b_f, v_conv_w, v_conv_b, v_ln_g, v_ln_b, v_w_out, v_mix_norm_odd, v_pool_w, v_pool_b, v_pool_scale, v_ffn_norm, v_w_up, v_ffn_conv_w, v_ffn_conv_b, v_w_down, v_final_norm):
    given = dict(x=x, meta_tokens=meta_tokens, mix_norm_even=mix_norm_even, w_in=w_in, b_f=b_f, conv_w=conv_w, conv_b=conv_b, ln_g=ln_g, ln_b=ln_b, w_out=w_out, mix_norm_odd=mix_norm_odd, pool_w=pool_w, pool_b=pool_b, pool_scale=pool_scale, ffn_norm=ffn_norm, w_up=w_up, ffn_conv_w=ffn_conv_w, ffn_conv_b=ffn_conv_b, w_down=w_down, final_norm=final_norm, loss_target=loss_target, m_meta_tokens=m_meta_tokens, m_mix_norm_even=m_mix_norm_even, m_w_in=m_w_in, m_b_f=m_b_f, m_conv_w=m_conv_w, m_conv_b=m_conv_b, m_ln_g=m_ln_g, m_ln_b=m_ln_b, m_w_out=m_w_out, m_mix_norm_odd=m_mix_norm_odd, m_pool_w=m_pool_w, m_pool_b=m_pool_b, m_pool_scale=m_pool_scale, m_ffn_norm=m_ffn_norm, m_w_up=m_w_up, m_ffn_conv_w=m_ffn_conv_w, m_ffn_conv_b=m_ffn_conv_b, m_w_down=m_w_down, m_final_norm=m_final_norm, v_meta_tokens=v_meta_tokens, v_mix_norm_even=v_mix_norm_even, v_w_in=v_w_in, v_b_f=v_b_f, v_conv_w=v_conv_w, v_conv_b=v_conv_b, v_ln_g=v_ln_g, v_ln_b=v_ln_b, v_w_out=v_w_out, v_mix_norm_odd=v_mix_norm_odd, v_pool_w=v_pool_w, v_pool_b=v_pool_b, v_pool_scale=v_pool_scale, v_ffn_norm=v_ffn_norm, v_w_up=v_w_up, v_ffn_conv_w=v_ffn_conv_w, v_ffn_conv_b=v_ffn_conv_b, v_w_down=v_w_down, v_final_norm=v_final_norm)
    weights = {n: given[n] for n in TWIN_WEIGHTS}
    shared = {n: given[n] for n in SHARED_INPUTS}
    per_example = {n: given[n] for n in ['x']}
    grad_fn = _jax.value_and_grad(_loss, argnums=(0, 1))

    def one_microbatch(ex, loss_target):
        ex = dict(ex)
        diff = ex.pop(TWIN_DIFF_INPUT)
        return grad_fn(weights, diff, {**shared, **ex}, loss_target)

    if N_MICROBATCH == 1:
        loss, (grad_w, grad_x) = one_microbatch(per_example, given["loss_target"])
    else:
        def body(carry, xs):
            loss_sum, grad_sum = carry
            l_k, (gw_k, gx_k) = one_microbatch(xs[0], xs[1])
            with _jax.named_scope("update"):
                return (loss_sum + l_k, _jax.tree.map(_jnp.add, grad_sum, gw_k)), gx_k

        init = (_jnp.zeros((), _jnp.float32), _jax.tree.map(_jnp.zeros_like, weights))
        (loss, grad_w), grad_x = _jax.lax.scan(body, init, (per_example, given["loss_target"]))
    with _jax.named_scope("update"):
        delta_w, new_m, new_v = {}, {}, {}
        for n in TWIN_WEIGHTS:
            delta_w[n], new_m[n], new_v[n] = _adamw(weights[n], grad_w[n], given["m_" + n], given["v_" + n])
    return (loss, grad_x, *[grad_w[n] for n in TWIN_WEIGHTS], *[delta_w[n] for n in TWIN_WEIGHTS],
            *[new_m[n] for n in TWIN_WEIGHTS], *[new_v[n] for n in TWIN_WEIGHTS])
```

```python
import functools

import numpy as np
import jax
import jax.numpy as jnp
from jax import lax
from jax.experimental import pallas as pl
from jax.experimental.pallas import tpu as pltpu

F32 = jnp.float32
BF16 = jnp.bfloat16

D_MODEL = 1024
N_META = 16
SEQ = 2048
HEADS = 8
HEAD_DIM = 64
FOX_W = HEADS * HEAD_DIM
CONV_CH = 512
CONV_K = 31
D_FF = 2816
POOL_WINDOWS = (2, 4, 8, 16)
POOL_G = 256
RMS_EPS = 1e-6
LN_EPS = 1e-5
IN_COLS = 3 * FOX_W + HEADS + 2 * CONV_CH
IN_COLS_P = 3 * FOX_W + 2 * CONV_CH + 128
F_COL_BLK = (3 * FOX_W + 2 * CONV_CH) // 128
N_CHIPS = 4
IN_SHARD = IN_COLS // N_CHIPS
UP_SHARD = 2 * D_FF // N_CHIPS
DOWN_SHARD = D_FF // N_CHIPS

ADAM_LR = 0.001
ADAM_B1 = 0.9
ADAM_B2 = 0.999
ADAM_EPS = 1e-08
ADAM_WD = 0.01
ADAM_STEP = 10

LANE = 128
ATT_BLK = 128
VMEM_LIMIT = 48 * 1024 * 1024

NEG = -1e30


def _sigmoid(x):
    return 1.0 / (1.0 + jnp.exp(-x))


def _params(sem=None):
    return pltpu.CompilerParams(dimension_semantics=sem, vmem_limit_bytes=VMEM_LIMIT)


def _sub_rows(tm):
    best = 8
    for s in range(8, 137, 8):
        if tm % s == 0:
            best = s
    return best


def _mm(a, b, mode, out_dtype, tm, tn, name, add=None):
    if mode == "nn":
        (M, K), (K2, N) = a.shape, b.shape
        dims = (((1,), (0,)), ((), ()))
        a_blk, a_idx = (tm, K), (lambda i, j: (i, 0))
        b_blk, b_idx = (K, tn), (lambda i, j: (0, j))
    elif mode == "nt":
        (M, K), (N, K2) = a.shape, b.shape
        dims = (((1,), (1,)), ((), ()))
        a_blk, a_idx = (tm, K), (lambda i, j: (i, 0))
        b_blk, b_idx = (tn, K), (lambda i, j: (j, 0))
    else:
        (K, M), (K2, N) = a.shape, b.shape
        dims = (((0,), (0,)), ((), ()))
        a_blk, a_idx = (K, tm), (lambda i, j: (0, i))
        b_blk, b_idx = (K, tn), (lambda i, j: (0, j))
    assert K == K2 and M % tm == 0 and N % tn == 0, (name, a.shape, b.shape, tm, tn)
    gm, gn = M // tm, N // tn
    a_bytes = a.size * a.dtype.itemsize
    b_bytes = b.size * b.dtype.itemsize
    m_outer = a_bytes + b_bytes * gm <= b_bytes + a_bytes * gn
    if m_outer:
        grid = (gm, gn)
        wrap = lambda f: f
    else:
        grid = (gn, gm)
        wrap = lambda f: (lambda j, i: f(i, j))
    o_idx = lambda i, j: (i, j)
    in_specs = [pl.BlockSpec(a_blk, wrap(a_idx)), pl.BlockSpec(b_blk, wrap(b_idx))]
    args = [a, b]
    if add is not None:
        in_specs.append(pl.BlockSpec((tm, tn), wrap(o_idx)))
        args.append(add)

    def body(a_ref, b_ref, *rest):
        o_ref = rest[-1]
        x = a_ref[...].astype(BF16)
        y = b_ref[...].astype(BF16)
        acc = lax.dot_general(x, y, dims, preferred_element_type=F32)
        if add is not None:
            acc = acc + rest[0][...]
        o_ref[...] = acc.astype(o_ref.dtype)

    return pl.pallas_call(
        body, name=name, out_shape=jax.ShapeDtypeStruct((M, N), out_dtype), grid=grid,
        in_specs=in_specs, out_specs=pl.BlockSpec((tm, tn), wrap(o_idx)),
        compiler_params=_params(("parallel", "parallel")))(*args)


def _rms_fwd(h, g, out_dtype, tm, name):
    LP, Dm = h.shape

    def body(h_ref, g_ref, o_ref):
        x = h_ref[...]
        r = lax.rsqrt(jnp.mean(x * x, axis=1, keepdims=True) + RMS_EPS)
        o_ref[...] = (x * r * g_ref[...]).astype(o_ref.dtype)

    return pl.pallas_call(
        body, name=name, out_shape=jax.ShapeDtypeStruct((LP, Dm), out_dtype), grid=(LP // tm,),
        in_specs=[pl.BlockSpec((tm, Dm), lambda i: (i, 0)), pl.BlockSpec((1, Dm), lambda i: (0, 0))],
        out_specs=pl.BlockSpec((tm, Dm), lambda i: (i, 0)),
        compiler_params=_params(("parallel",)))(h, g)


def _rms_bwd(h, g, dn, dres, tm, name):
    LP, Dm = h.shape

    def body(h_ref, g_ref, dn_ref, dr_ref, dh_ref, dg_ref):
        i = pl.program_id(0)
        x = h_ref[...]
        r = lax.rsqrt(jnp.mean(x * x, axis=1, keepdims=True) + RMS_EPS)
        xhat = x * r
        dy = dn_ref[...]
        dxh = dy * g_ref[...]
        dh = r * (dxh - xhat * jnp.mean(dxh * xhat, axis=1, keepdims=True))
        dh_ref[...] = dr_ref[...] + dh

        @pl.when(i == 0)
        def _():
            dg_ref[...] = jnp.zeros_like(dg_ref)

        dg_ref[...] += jnp.sum(dy * xhat, axis=0, keepdims=True)

    row = pl.BlockSpec((tm, Dm), lambda i: (i, 0))
    vec = pl.BlockSpec((1, Dm), lambda i: (0, 0))
    return pl.pallas_call(
        body, name=name,
        out_shape=(jax.ShapeDtypeStruct((LP, Dm), F32), jax.ShapeDtypeStruct((1, Dm), F32)),
        grid=(LP // tm,), in_specs=[row, vec, row, row], out_specs=(row, vec),
        compiler_params=_params(("arbitrary",)))(h, g, dn, dres)


def _loss_head(h, g, tgt, n_real, tm, name):
    LP, Dm = h.shape

    def body(h_ref, g_ref, t_ref, loss_ref, dh_ref, dg_ref):
        i = pl.program_id(0)
        x = h_ref[...]
        gg = g_ref[...]
        r = lax.rsqrt(jnp.mean(x * x, axis=1, keepdims=True) + RMS_EPS)
        xhat = x * r
        rows = i * tm + lax.broadcasted_iota(jnp.int32, (tm, 1), 0)
        real = jnp.logical_and(rows >= N_META, rows < n_real)
        diff = jnp.where(real, xhat * gg - t_ref[...], 0.0)
        dy = diff * (1.0 / Dm)
        dxh = dy * gg
        dh_ref[...] = r * (dxh - xhat * jnp.mean(dxh * xhat, axis=1, keepdims=True))

        @pl.when(i == 0)
        def _():
            dg_ref[...] = jnp.zeros_like(dg_ref)
            loss_ref[...] = jnp.zeros_like(loss_ref)

        dg_ref[...] += jnp.sum(dy * xhat, axis=0, keepdims=True)
        part = jnp.sum(jnp.sum(diff * diff, axis=1, keepdims=True), axis=0, keepdims=True)
        loss_ref[...] += jnp.broadcast_to(part * (0.5 / Dm), loss_ref.shape)

    row = pl.BlockSpec((tm, Dm), lambda i: (i, 0))
    vec = pl.BlockSpec((1, Dm), lambda i: (0, 0))
    return pl.pallas_call(
        body, name=name,
        out_shape=(jax.ShapeDtypeStruct((1, LANE), F32), jax.ShapeDtypeStruct((LP, Dm), F32),
                   jax.ShapeDtypeStruct((1, Dm), F32)),
        grid=(LP // tm,), in_specs=[row, vec, row],
        out_specs=(pl.BlockSpec((1, LANE), lambda i: (0, 0)), row, vec),
        compiler_params=_params(("arbitrary",)))(h, g, tgt)


def _fgate_fwd(proj, bf_p, name):
    LP = proj.shape[0]
    nb = LP // LANE

    def body(f_ref, b_ref, c_ref, lf_ref):
        x = f_ref[...] + b_ref[...]
        lf_ref[...] = jnp.minimum(x, 0.0) - jnp.log1p(jnp.exp(-jnp.abs(x)))
        ri = lax.broadcasted_iota(jnp.int32, (LANE, LANE), 0)
        ci = lax.broadcasted_iota(jnp.int32, (LANE, LANE), 1)
        tri = jnp.where(ri >= ci, 1.0, 0.0).astype(F32)

        def blk(i, carry):
            rows = pl.ds(pl.multiple_of(i * LANE, LANE), LANE)
            cb = jnp.dot(tri, lf_ref[rows, :], precision=lax.Precision.HIGHEST,
                         preferred_element_type=F32) + carry
            c_ref[rows, :] = cb
            return cb[LANE - 1:LANE, :]

        lax.fori_loop(0, nb, blk, jnp.zeros((1, LANE), F32))

    return pl.pallas_call(
        body, name=name, out_shape=jax.ShapeDtypeStruct((LP, LANE), F32), grid=(1,),
        in_specs=[pl.BlockSpec((LP, LANE), lambda i: (0, F_COL_BLK)),
                  pl.BlockSpec((1, LANE), lambda i: (0, 0))],
        out_specs=pl.BlockSpec((LP, LANE), lambda i: (0, 0)),
        scratch_shapes=[pltpu.VMEM((LP, LANE), F32)],
        compiler_params=_params(("arbitrary",)))(proj, bf_p)


def _fgate_bwd(proj, bf_p, dc, name):
    LP = proj.shape[0]
    nb = LP // LANE

    def body(f_ref, b_ref, dc_ref, dl_ref, db_ref):
        ri = lax.broadcasted_iota(jnp.int32, (LANE, LANE), 0)
        ci = lax.broadcasted_iota(jnp.int32, (LANE, LANE), 1)
        triu = jnp.where(ri <= ci, 1.0, 0.0).astype(F32)
        bb = b_ref[...]

        tail = jnp.zeros((1, LANE), F32)
        dbs = jnp.zeros((1, LANE), F32)
        for i in range(nb - 1, -1, -1):
            rows = slice(i * LANE, (i + 1) * LANE)
            gb = jnp.dot(triu, dc_ref[rows, :], precision=lax.Precision.HIGHEST,
                         preferred_element_type=F32) + tail
            x = f_ref[rows, :] + bb
            dl = gb * _sigmoid(-x)
            dl_ref[rows, :] = dl.astype(dl_ref.dtype)
            tail = gb[0:1, :]
            dbs = dbs + jnp.sum(dl, axis=0, keepdims=True)
        db_ref[...] = dbs

    return pl.pallas_call(
        body, name=name,
        out_shape=(jax.ShapeDtypeStruct((LP, LANE), BF16), jax.ShapeDtypeStruct((1, LANE), F32)),
        grid=(1,),
        in_specs=[pl.BlockSpec((LP, LANE), lambda i: (0, F_COL_BLK)),
                  pl.BlockSpec((1, LANE), lambda i: (0, 0)),
                  pl.BlockSpec((LP, LANE), lambda i: (0, 0))],
        out_specs=(pl.BlockSpec((LP, LANE), lambda i: (0, 0)), pl.BlockSpec((1, LANE), lambda i: (0, 0))),
        compiler_params=_params(("arbitrary",)))(proj, bf_p, dc)


def _attn_fwd(q, k, v, c_col, c_row, name):
    Hh, LP, Dh = q.shape
    nb = LP // ATT_BLK
    scale = Dh ** -0.5
    nt = (((1,), (1,)), ((), ()))

    def body(q_ref, k_ref, v_ref, cc_ref, cr_ref, o_ref, lse_ref):
        i = pl.program_id(1)
        qb = q_ref[...]
        cq = cc_ref[...]
        rows = i * ATT_BLK + lax.broadcasted_iota(jnp.int32, (ATT_BLK, ATT_BLK), 0)
        cols0 = lax.broadcasted_iota(jnp.int32, (ATT_BLK, ATT_BLK), 1)

        def step(j, carry):
            m, l, acc = carry
            ks = pl.ds(pl.multiple_of(j * ATT_BLK, ATT_BLK), ATT_BLK)
            s = lax.dot_general(qb, k_ref[ks, :], nt, preferred_element_type=F32) * scale
            s = s + cq - cr_ref[j]
            s = jnp.where(cols0 + j * ATT_BLK <= rows, s, NEG)
            m_new = jnp.maximum(m, jnp.max(s, axis=1, keepdims=True))
            p = jnp.exp(s - m_new)
            alpha = jnp.exp(m - m_new)
            l = alpha * l + jnp.sum(p, axis=1, keepdims=True)
            acc = alpha * acc + jnp.dot(p.astype(BF16), v_ref[ks, :], preferred_element_type=F32)
            return m_new, l, acc

        init = (jnp.full((ATT_BLK, 1), NEG, F32), jnp.zeros((ATT_BLK, 1), F32),
                jnp.zeros((ATT_BLK, Dh), F32))
        m, l, acc = lax.fori_loop(0, i + 1, step, init)
        o_ref[...] = acc / l
        lse_ref[...] = m + jnp.log(l)

    qspec = pl.BlockSpec((None, ATT_BLK, Dh), lambda h, i: (h, i, 0))
    kspec = pl.BlockSpec((None, LP, Dh), lambda h, i: (h, 0, 0))
    colspec = pl.BlockSpec((None, ATT_BLK, 1), lambda h, i: (h, i, 0))
    rowspec = pl.BlockSpec((None, nb, 1, ATT_BLK), lambda h, i: (h, 0, 0, 0))
    return pl.pallas_call(
        body, name=name,
        out_shape=(jax.ShapeDtypeStruct((Hh, LP, Dh), F32), jax.ShapeDtypeStruct((Hh, LP, 1), F32)),
        grid=(Hh, nb), in_specs=[qspec, kspec, kspec, colspec, rowspec],
        out_specs=(qspec, colspec),
        compiler_params=_params(("parallel", "arbitrary")))(q, k, v, c_col, c_row)


def _attn_bwd(q, k, v, o, do, lse, c_col, c_row, name):
    Hh, LP, Dh = q.shape
    nb = LP // ATT_BLK
    scale = Dh ** -0.5
    nt = (((1,), (1,)), ((), ()))
    tn = (((0,), (0,)), ((), ()))

    def body(q_ref, k_ref, v_ref, o_ref, do_ref, lse_ref, cc_ref, cr_ref,
             dq_ref, dk_ref, dv_ref, dcq_ref, dc_ref, delta_ref):
        j = pl.program_id(1)

        @pl.when(j == 0)
        def _():
            dq_ref[...] = jnp.zeros_like(dq_ref)
            dcq_ref[...] = jnp.zeros_like(dcq_ref)
            dob_all = do_ref[...].astype(BF16).astype(F32)
            delta_ref[...] = jnp.sum(dob_all * o_ref[...], axis=1, keepdims=True)

        kb = k_ref[...]
        vb = v_ref[...]
        ck = cr_ref[j]
        rows0 = lax.broadcasted_iota(jnp.int32, (ATT_BLK, ATT_BLK), 0)
        cols = j * ATT_BLK + lax.broadcasted_iota(jnp.int32, (ATT_BLK, ATT_BLK), 1)

        def step(i, carry):
            dk, dv, dcs = carry
            qs = pl.ds(pl.multiple_of(i * ATT_BLK, ATT_BLK), ATT_BLK)
            qb = q_ref[qs, :]
            dob = do_ref[qs, :].astype(BF16)
            s = lax.dot_general(qb, kb, nt, preferred_element_type=F32) * scale
            s = s + cc_ref[qs, :] - ck
            s = jnp.where(cols <= rows0 + i * ATT_BLK, s, NEG)
            p = jnp.exp(s - lse_ref[qs, :])
            dp = lax.dot_general(dob, vb, nt, preferred_element_type=F32)
            ds = p * (dp - delta_ref[qs, :])
            dsb = ds.astype(BF16)
            dv = dv + lax.dot_general(p.astype(BF16), dob, tn, preferred_element_type=F32)
            dk = dk + lax.dot_general(dsb, qb, tn, preferred_element_type=F32) * scale
            dq_ref[qs, :] += jnp.dot(dsb, kb, preferred_element_type=F32) * scale
            dcq_ref[qs, :] += jnp.sum(ds, axis=1, keepdims=True)
            dcs = dcs - jnp.sum(ds, axis=0, keepdims=True)
            return dk, dv, dcs

        init = (jnp.zeros((ATT_BLK, Dh), F32), jnp.zeros((ATT_BLK, Dh), F32),
                jnp.zeros((1, ATT_BLK), F32))
        dk, dv, dcs = lax.fori_loop(j, nb, step, init)
        dk_ref[...] = dk
        dv_ref[...] = dv
        dc_ref[...] = dcs

    full = pl.BlockSpec((None, LP, Dh), lambda h, j: (h, 0, 0))
    blk = pl.BlockSpec((None, ATT_BLK, Dh), lambda h, j: (h, j, 0))
    col = pl.BlockSpec((None, LP, 1), lambda h, j: (h, 0, 0))
    rowspec = pl.BlockSpec((None, nb, 1, ATT_BLK), lambda h, j: (h, 0, 0, 0))
    return pl.pallas_call(
        body, name=name,
        out_shape=(jax.ShapeDtypeStruct((Hh, LP, Dh), F32), jax.ShapeDtypeStruct((Hh, LP, Dh), F32),
                   jax.ShapeDtypeStruct((Hh, LP, Dh), F32), jax.ShapeDtypeStruct((Hh, LP, 1), F32),
                   jax.ShapeDtypeStruct((Hh, nb, 1, ATT_BLK), F32)),
        grid=(Hh, nb), in_specs=[full, blk, blk, full, full, col, col, rowspec],
        out_specs=(full, blk, blk, col, pl.BlockSpec((None, None, 1, ATT_BLK), lambda h, j: (h, j, 0, 0))),
        scratch_shapes=[pltpu.VMEM((LP, 1), F32)],
        compiler_params=_params(("parallel", "arbitrary")))(q, k, v, o, do, lse, c_col, c_row)


CONV_HALO = 32
A_BLK = 3 * FOX_W // CONV_CH
G_BLK = A_BLK + 1


def _conf_fwd(proj, cw, cb, lg, lb, tm, name):
    LP = proj.shape[0]
    C = CONV_CH
    sub = _sub_rows(tm)
    hpb = tm // CONV_HALO

    def body(a_ref, g_ref, ah_ref, gh_ref, w_ref, cb_ref, lg_ref, lb_ref, u1_ref, u_ref, buf):
        r = pl.program_id(0)
        buf[CONV_HALO:CONV_HALO + tm, :] = a_ref[...] * _sigmoid(g_ref[...])
        buf[0:CONV_HALO, :] = jnp.where(r > 0, ah_ref[...] * _sigmoid(gh_ref[...]), 0.0)
        for s in range(tm // sub):
            for ct in range(C // LANE):
                ln = slice(ct * LANE, (ct + 1) * LANE)
                acc = jnp.broadcast_to(cb_ref[:, ln], (sub, LANE))
                for kk in range(CONV_K):
                    off = CONV_HALO + s * sub - (CONV_K - 1) + kk
                    acc = acc + w_ref[kk:kk + 1, ln] * buf[off:off + sub, ln]
                u1_ref[s * sub:(s + 1) * sub, ln] = acc
        u1 = u1_ref[...]
        mu = jnp.mean(u1, axis=1, keepdims=True)
        xc = u1 - mu
        var = jnp.mean(xc * xc, axis=1, keepdims=True)
        y = xc * lax.rsqrt(var + LN_EPS) * lg_ref[...] + lb_ref[...]
        u_ref[...] = (y * _sigmoid(y)).astype(u_ref.dtype)

    cur = lambda blk: pl.BlockSpec((tm, C), lambda r: (r, blk))
    halo = lambda blk: pl.BlockSpec((CONV_HALO, C), lambda r: (jnp.maximum(r * hpb - 1, 0), blk))
    vec = pl.BlockSpec((1, C), lambda r: (0, 0))
    out = pl.BlockSpec((tm, C), lambda r: (r, 0))
    return pl.pallas_call(
        body, name=name,
        out_shape=(jax.ShapeDtypeStruct((LP, C), F32), jax.ShapeDtypeStruct((LP, C), BF16)),
        grid=(LP // tm,),
        in_specs=[cur(A_BLK), cur(G_BLK), halo(A_BLK), halo(G_BLK),
                  pl.BlockSpec((CONV_HALO, C), lambda r: (0, 0)), vec, vec, vec],
        out_specs=(out, out),
        scratch_shapes=[pltpu.VMEM((CONV_HALO + tm, C), F32)],
        compiler_params=_params(("parallel",)))(proj, proj, proj, proj, cw, cb, lg, lb)


def _conf_bwd(proj, u1, dcat, cw, lg, lb, tm, name):
    LP = proj.shape[0]
    C = CONV_CH
    sub = _sub_rows(tm)
    hpb = tm // CONV_HALO
    nblk = LP // tm
    last_halo = LP // CONV_HALO - 1

    def body(a_ref, g_ref, ah_ref, gh_ref, u1_ref, u1n_ref, du_ref, dun_ref, w_ref, lg_ref, lb_ref,
             dadg_ref, dw_ref, dcb_ref, dlg_ref, dlb_ref, ubuf, dbuf, du0):
        r = pl.program_id(0)
        lgv = lg_ref[...]
        lbv = lb_ref[...]

        def ln_silu_bwd(u1v, duv):
            mu = jnp.mean(u1v, axis=1, keepdims=True)
            xc = u1v - mu
            rstd = lax.rsqrt(jnp.mean(xc * xc, axis=1, keepdims=True) + LN_EPS)
            xhat = xc * rstd
            y = xhat * lgv + lbv
            sg = _sigmoid(y)
            dy = duv * (sg * (1.0 + y * (1.0 - sg)))
            dxh = dy * lgv
            du1 = rstd * (dxh - jnp.mean(dxh, axis=1, keepdims=True)
                          - xhat * jnp.mean(dxh * xhat, axis=1, keepdims=True))
            return du1, dy, xhat

        @pl.when(r == 0)
        def _():
            dw_ref[...] = jnp.zeros_like(dw_ref)
            dcb_ref[...] = jnp.zeros_like(dcb_ref)
            dlg_ref[...] = jnp.zeros_like(dlg_ref)
            dlb_ref[...] = jnp.zeros_like(dlb_ref)

        du1, dy, xhat = ln_silu_bwd(u1_ref[...], du_ref[...])
        dlg_ref[...] += jnp.sum(dy * xhat, axis=0, keepdims=True)
        dlb_ref[...] += jnp.sum(dy, axis=0, keepdims=True)
        dcb_ref[...] += jnp.sum(du1, axis=0, keepdims=True)
        dbuf[0:tm, :] = du1
        du1n, _, _ = ln_silu_bwd(u1n_ref[...], dun_ref[...])
        dbuf[tm:tm + CONV_HALO, :] = jnp.where(r < nblk - 1, du1n, 0.0)
        ubuf[CONV_HALO:CONV_HALO + tm, :] = a_ref[...] * _sigmoid(g_ref[...])
        ubuf[0:CONV_HALO, :] = jnp.where(r > 0, ah_ref[...] * _sigmoid(gh_ref[...]), 0.0)

        for ct in range(C // LANE):
            ln = slice(ct * LANE, (ct + 1) * LANE)
            for s in range(tm // sub):
                d_here = dbuf[s * sub:(s + 1) * sub, ln]
                acc = jnp.zeros((sub, LANE), F32)
                for kk in range(CONV_K):
                    fo = s * sub + (CONV_K - 1) - kk
                    acc = acc + w_ref[kk:kk + 1, ln] * dbuf[fo:fo + sub, ln]
                    bo = CONV_HALO + s * sub - (CONV_K - 1) + kk
                    dw_ref[kk:kk + 1, ln] += jnp.sum(d_here * ubuf[bo:bo + sub, ln], axis=0, keepdims=True)
                du0[s * sub:(s + 1) * sub, ln] = acc
        a = a_ref[...]
        sg = _sigmoid(g_ref[...])
        d0 = du0[...]
        dadg_ref[:, 0:C] = (d0 * sg).astype(dadg_ref.dtype)
        dadg_ref[:, C:2 * C] = (d0 * a * sg * (1.0 - sg)).astype(dadg_ref.dtype)

    cur = lambda blk: pl.BlockSpec((tm, C), lambda r: (r, blk))
    prev = lambda blk: pl.BlockSpec((CONV_HALO, C), lambda r: (jnp.maximum(r * hpb - 1, 0), blk))
    nxt = lambda blk: pl.BlockSpec((CONV_HALO, C), lambda r: (jnp.minimum((r + 1) * hpb, last_halo), blk))
    vec = pl.BlockSpec((1, C), lambda r: (0, 0))
    wspec = pl.BlockSpec((CONV_HALO, C), lambda r: (0, 0))
    return pl.pallas_call(
        body, name=name,
        out_shape=(jax.ShapeDtypeStruct((LP, 2 * C), BF16), jax.ShapeDtypeStruct((CONV_HALO, C), F32),
                   jax.ShapeDtypeStruct((1, C), F32), jax.ShapeDtypeStruct((1, C), F32),
                   jax.ShapeDtypeStruct((1, C), F32)),
        grid=(nblk,),
        in_specs=[cur(A_BLK), cur(G_BLK), prev(A_BLK), prev(G_BLK), cur(0), nxt(0), cur(1), nxt(1),
                  wspec, vec, vec],
        out_specs=(pl.BlockSpec((tm, 2 * C), lambda r: (r, 0)), wspec, vec, vec, vec),
        scratch_shapes=[pltpu.VMEM((CONV_HALO + tm, C), F32), pltpu.VMEM((tm + CONV_HALO, C), F32),
                        pltpu.VMEM((tm, C), F32)],
        compiler_params=_params(("arbitrary",)))(proj, proj, proj, proj, u1, u1, dcat, dcat, cw, lg, lb)


FFN_HALO = 8
FFN_TC = 256
FFN_K = 3


def _ffn_conv(buf, w_ref, b_ref, s, sub, ln):
    acc = jnp.broadcast_to(b_ref[:, ln], (sub, LANE))
    for kk in range(FFN_K):
        off = FFN_HALO + s * sub - (FFN_K - 1) + kk
        acc = acc + w_ref[kk:kk + 1, ln] * buf[off:off + sub, ln]
    return acc


def _ffn_act_fwd(upg, upv, w, b, tm, name):
    LP, F = upg.shape
    nct = F // FFN_TC
    sub = _sub_rows(tm)
    hpb = tm // FFN_HALO

    def body(g_ref, v_ref, gh_ref, vh_ref, wg_ref, wv_ref, bg_ref, bv_ref, act_ref, gbuf, vbuf):
        r = pl.program_id(1)
        gbuf[FFN_HALO:FFN_HALO + tm, :] = g_ref[...]
        vbuf[FFN_HALO:FFN_HALO + tm, :] = v_ref[...]
        gbuf[0:FFN_HALO, :] = jnp.where(r > 0, gh_ref[...], 0.0)
        vbuf[0:FFN_HALO, :] = jnp.where(r > 0, vh_ref[...], 0.0)
        for s in range(tm // sub):
            for ct in range(FFN_TC // LANE):
                ln = slice(ct * LANE, (ct + 1) * LANE)
                gc = _ffn_conv(gbuf, wg_ref, bg_ref, s, sub, ln)
                vc = _ffn_conv(vbuf, wv_ref, bv_ref, s, sub, ln)
                act_ref[s * sub:(s + 1) * sub, ln] = (gc * _sigmoid(gc) * vc).astype(act_ref.dtype)

    cur = pl.BlockSpec((tm, FFN_TC), lambda c, r: (r, c))
    halo = pl.BlockSpec((FFN_HALO, FFN_TC), lambda c, r: (jnp.maximum(r * hpb - 1, 0), c))
    wg = pl.BlockSpec((8, FFN_TC), lambda c, r: (0, c))
    wv = pl.BlockSpec((8, FFN_TC), lambda c, r: (0, nct + c))
    bg = pl.BlockSpec((1, FFN_TC), lambda c, r: (0, c))
    bv = pl.BlockSpec((1, FFN_TC), lambda c, r: (0, nct + c))
    return pl.pallas_call(
        body, name=name, out_shape=jax.ShapeDtypeStruct((LP, F), BF16), grid=(nct, LP // tm),
        in_specs=[cur, cur, halo, halo, wg, wv, bg, bv], out_specs=cur,
        scratch_shapes=[pltpu.VMEM((FFN_HALO + tm, FFN_TC), F32)] * 2,
        compiler_params=_params(("parallel", "parallel")))(upg, upv, upg, upv, w, w, b, b)


def _ffn_act_bwd(upg, upv, dact, w, b, tm, name):
    LP, F = upg.shape
    nct = F // FFN_TC
    sub = _sub_rows(tm)
    hpb = tm // FFN_HALO
    nblk = LP // tm
    last_halo = LP // FFN_HALO - 1
    TB = tm + 2 * FFN_HALO

    def body(g_ref, v_ref, gp_ref, vp_ref, gn_ref, vn_ref, da_ref, dan_ref,
             wg_ref, wv_ref, bg_ref, bv_ref,
             dg_ref, dv_ref, dwg_ref, dwv_ref, dbg_ref, dbv_ref, gbuf, vbuf, dgb, dvb):
        r = pl.program_id(1)
        first = r == 0
        last = r == nblk - 1

        @pl.when(first)
        def _():
            dwg_ref[...] = jnp.zeros_like(dwg_ref)
            dwv_ref[...] = jnp.zeros_like(dwv_ref)
            dbg_ref[...] = jnp.zeros_like(dbg_ref)
            dbv_ref[...] = jnp.zeros_like(dbv_ref)

        for buf, c_ref, p_ref, n_ref in ((gbuf, g_ref, gp_ref, gn_ref), (vbuf, v_ref, vp_ref, vn_ref)):
            buf[0:FFN_HALO, :] = jnp.where(first, 0.0, p_ref[...])
            buf[FFN_HALO:FFN_HALO + tm, :] = c_ref[...]
            buf[FFN_HALO + tm:TB, :] = jnp.where(last, 0.0, n_ref[...])

        def dconv(s0, nrows, ln, dact_v):
            gc = jnp.broadcast_to(bg_ref[:, ln], (nrows, LANE))
            vc = jnp.broadcast_to(bv_ref[:, ln], (nrows, LANE))
            for kk in range(FFN_K):
                off = s0 - (FFN_K - 1) + kk
                gc = gc + wg_ref[kk:kk + 1, ln] * gbuf[off:off + nrows, ln]
                vc = vc + wv_ref[kk:kk + 1, ln] * vbuf[off:off + nrows, ln]
            sg = _sigmoid(gc)
            return dact_v * vc * (sg * (1.0 + gc * (1.0 - sg))), dact_v * (gc * sg)

        for ct in range(FFN_TC // LANE):
            ln = slice(ct * LANE, (ct + 1) * LANE)
            for s in range(tm // sub):
                dgc, dvc = dconv(FFN_HALO + s * sub, sub, ln, da_ref[s * sub:(s + 1) * sub, ln])
                dgb[s * sub:(s + 1) * sub, ln] = dgc
                dvb[s * sub:(s + 1) * sub, ln] = dvc
            dgc, dvc = dconv(FFN_HALO + tm, FFN_HALO, ln, jnp.where(last, 0.0, dan_ref[:, ln]))
            dgb[tm:tm + FFN_HALO, ln] = dgc
            dvb[tm:tm + FFN_HALO, ln] = dvc
            for dbuf, ubuf, w_ref, dw_ref, db_ref, dout in (
                    (dgb, gbuf, wg_ref, dwg_ref, dbg_ref, dg_ref), (dvb, vbuf, wv_ref, dwv_ref, dbv_ref, dv_ref)):
                for s in range(tm // sub):
                    d_here = dbuf[s * sub:(s + 1) * sub, ln]
                    acc = jnp.zeros((sub, LANE), F32)
                    for kk in range(FFN_K):
                        fo = s * sub + (FFN_K - 1) - kk
                        acc = acc + w_ref[kk:kk + 1, ln] * dbuf[fo:fo + sub, ln]
                        bo = FFN_HALO + s * sub - (FFN_K - 1) + kk
                        dw_ref[kk:kk + 1, ln] += jnp.sum(d_here * ubuf[bo:bo + sub, ln], axis=0, keepdims=True)
                    db_ref[:, ln] += jnp.sum(d_here, axis=0, keepdims=True)
                    dout[s * sub:(s + 1) * sub, ln] = acc.astype(dout.dtype)

    cur = pl.BlockSpec((tm, FFN_TC), lambda c, r: (r, c))
    prev = pl.BlockSpec((FFN_HALO, FFN_TC), lambda c, r: (jnp.maximum(r * hpb - 1, 0), c))
    nxt = pl.BlockSpec((FFN_HALO, FFN_TC), lambda c, r: (jnp.minimum((r + 1) * hpb, last_halo), c))
    wg = pl.BlockSpec((8, FFN_TC), lambda c, r: (0, c))
    wv = pl.BlockSpec((8, FFN_TC), lambda c, r: (0, nct + c))
    bg = pl.BlockSpec((1, FFN_TC), lambda c, r: (0, c))
    bv = pl.BlockSpec((1, FFN_TC), lambda c, r: (0, nct + c))
    dupg, dupv, dwg, dwv, dbg, dbv = pl.pallas_call(
        body, name=name,
        out_shape=(jax.ShapeDtypeStruct((LP, F), BF16), jax.ShapeDtypeStruct((LP, F), BF16),
                   jax.ShapeDtypeStruct((8, F), F32), jax.ShapeDtypeStruct((8, F), F32),
                   jax.ShapeDtypeStruct((1, F), F32), jax.ShapeDtypeStruct((1, F), F32)),
        grid=(nct, nblk),
        in_specs=[cur, cur, prev, prev, nxt, nxt, cur, nxt, wg, wv, bg, bv],
        out_specs=(cur, cur, pl.BlockSpec((8, FFN_TC), lambda c, r: (0, c)),
                   pl.BlockSpec((8, FFN_TC), lambda c, r: (0, c)),
                   pl.BlockSpec((1, FFN_TC), lambda c, r: (0, c)),
                   pl.BlockSpec((1, FFN_TC), lambda c, r: (0, c))),
        scratch_shapes=[pltpu.VMEM((TB, FFN_TC), F32), pltpu.VMEM((TB, FFN_TC), F32),
                        pltpu.VMEM((tm + FFN_HALO, FFN_TC), F32), pltpu.VMEM((tm + FFN_HALO, FFN_TC), F32)],
        compiler_params=_params(("parallel", "arbitrary")))(
            upg, upv, upg, upv, upg, upv, dact, dact, w, w, b, b)
    return dupg, dupv, jnp.concatenate([dwg, dwv], axis=1), jnp.concatenate([dbg, dbv], axis=1)


POOL_HALO = 16


def _pool_fwd(h, g, pw, pb, ps, tm, name):
    LP, Dm = h.shape
    sub = _sub_rows(tm)
    hpb = tm // POOL_HALO

    def body(h_ref, hh_ref, g_ref, pw_ref, pb_ref, ps_ref, o_ref, d_ref, buf):
        r = pl.program_id(0)
        gg = g_ref[...]

        def norm(x):
            return x * lax.rsqrt(jnp.mean(x * x, axis=1, keepdims=True) + RMS_EPS) * gg

        x = h_ref[...]
        buf[POOL_HALO:POOL_HALO + tm, :] = norm(x)
        buf[0:POOL_HALO, :] = jnp.where(r > 0, norm(hh_ref[...]), 0.0)
        for gi, w in enumerate(POOL_WINDOWS):
            ln = slice(gi * POOL_G, (gi + 1) * POOL_G)
            for s in range(tm // sub):
                base = POOL_HALO + s * sub
                acc = buf[base:base + sub, ln]
                for jj in range(1, w):
                    acc = acc + buf[base - jj:base - jj + sub, ln]
                t = r * tm + s * sub + lax.broadcasted_iota(jnp.int32, (sub, 1), 0)
                cnt = jnp.minimum(t + 1, w).astype(F32)
                d_ref[s * sub:(s + 1) * sub, ln] = (acc / cnt - buf[base:base + sub, ln]).astype(d_ref.dtype)
            y = jnp.dot(d_ref[:, ln], pw_ref[gi], preferred_element_type=F32) + pb_ref[:, ln]
            o_ref[:, ln] = x[:, ln] + y * ps_ref[:, ln]

    row = pl.BlockSpec((tm, Dm), lambda r: (r, 0))
    halo = pl.BlockSpec((POOL_HALO, Dm), lambda r: (jnp.maximum(r * hpb - 1, 0), 0))
    vec = pl.BlockSpec((1, Dm), lambda r: (0, 0))
    wsp = pl.BlockSpec((len(POOL_WINDOWS), POOL_G, POOL_G), lambda r: (0, 0, 0))
    return pl.pallas_call(
        body, name=name,
        out_shape=(jax.ShapeDtypeStruct((LP, Dm), F32), jax.ShapeDtypeStruct((LP, Dm), BF16)),
        grid=(LP // tm,), in_specs=[row, halo, vec, wsp, vec, vec], out_specs=(row, row),
        scratch_shapes=[pltpu.VMEM((POOL_HALO + tm, Dm), F32)],
        compiler_params=_params(("parallel",)))(h, h, g, pw, pb, ps)


def _pool_bwd(h, g, d, pw, pb, ps, dh_out, tm, name):
    LP, Dm = h.shape
    sub = _sub_rows(tm)
    hpb = tm // POOL_HALO
    nblk = LP // tm
    last_halo = LP // POOL_HALO - 1
    nt = (((1,), (1,)), ((), ()))
    tn = (((0,), (0,)), ((), ()))

    def body(h_ref, g_ref, d_ref, pw_ref, pb_ref, ps_ref, do_ref, don_ref,
             dh_ref, dpw_ref, dpb_ref, dps_ref, dg_ref, ebuf, ddb, dnb):
        r = pl.program_id(0)

        @pl.when(r == 0)
        def _():
            dpw_ref[...] = jnp.zeros_like(dpw_ref)
            dpb_ref[...] = jnp.zeros_like(dpb_ref)
            dps_ref[...] = jnp.zeros_like(dps_ref)
            dg_ref[...] = jnp.zeros_like(dg_ref)

        for gi, w in enumerate(POOL_WINDOWS):
            ln = slice(gi * POOL_G, (gi + 1) * POOL_G)
            wg = pw_ref[gi]
            dog = do_ref[:, ln]
            dg_b = d_ref[:, ln]
            y_pre = jnp.dot(dg_b, wg, preferred_element_type=F32) + pb_ref[:, ln]
            dps_ref[:, ln] += jnp.sum(dog * y_pre, axis=0, keepdims=True)
            dy = dog * ps_ref[:, ln]
            dpb_ref[:, ln] += jnp.sum(dy, axis=0, keepdims=True)
            dyb = dy.astype(BF16)
            dpw_ref[gi] += lax.dot_general(dg_b, dyb, tn, preferred_element_type=F32)
            dd = lax.dot_general(dyb, wg, nt, preferred_element_type=F32)
            ddb[:, ln] = dd
            t = r * tm + lax.broadcasted_iota(jnp.int32, (tm, 1), 0)
            ebuf[0:tm, ln] = dd / jnp.minimum(t + 1, w).astype(F32)
            dyn = (don_ref[:, ln] * ps_ref[:, ln]).astype(BF16)
            ddn = lax.dot_general(dyn, wg, nt, preferred_element_type=F32)
            tn_ = (r + 1) * tm + lax.broadcasted_iota(jnp.int32, (POOL_HALO, 1), 0)
            ebuf[tm:tm + POOL_HALO, ln] = jnp.where(r < nblk - 1, ddn / jnp.minimum(tn_ + 1, w).astype(F32), 0.0)
            for s in range(tm // sub):
                acc = ebuf[s * sub:(s + 1) * sub, ln]
                for jj in range(1, w):
                    acc = acc + ebuf[s * sub + jj:s * sub + jj + sub, ln]
                dnb[s * sub:(s + 1) * sub, ln] = acc - ddb[s * sub:(s + 1) * sub, ln]
        x = h_ref[...]
        rr = lax.rsqrt(jnp.mean(x * x, axis=1, keepdims=True) + RMS_EPS)
        xhat = x * rr
        dn = dnb[...]
        dxh = dn * g_ref[...]
        dh_ref[...] = do_ref[...] + rr * (dxh - xhat * jnp.mean(dxh * xhat, axis=1, keepdims=True))
        dg_ref[...] += jnp.sum(dn * xhat, axis=0, keepdims=True)

    row = pl.BlockSpec((tm, Dm), lambda r: (r, 0))
    nxt = pl.BlockSpec((POOL_HALO, Dm), lambda r: (jnp.minimum((r + 1) * hpb, last_halo), 0))
    vec = pl.BlockSpec((1, Dm), lambda r: (0, 0))
    wsp = pl.BlockSpec((len(POOL_WINDOWS), POOL_G, POOL_G), lambda r: (0, 0, 0))
    return pl.pallas_call(
        body, name=name,
        out_shape=(jax.ShapeDtypeStruct((LP, Dm), F32),
                   jax.ShapeDtypeStruct((len(POOL_WINDOWS), POOL_G, POOL_G), F32),
                   jax.ShapeDtypeStruct((1, Dm), F32), jax.ShapeDtypeStruct((1, Dm), F32),
                   jax.ShapeDtypeStruct((1, Dm), F32)),
        grid=(nblk,), in_specs=[row, vec, row, wsp, vec, vec, row, nxt],
        out_specs=(row, wsp, vec, vec, vec),
        scratch_shapes=[pltpu.VMEM((tm + POOL_HALO, Dm), F32), pltpu.VMEM((tm, Dm), F32),
                        pltpu.VMEM((tm, Dm), F32)],
        compiler_params=_params(("arbitrary",)))(h, g, d, pw, pb, ps, dh_out, dh_out)


def _adamw(w, g, m, v, name):
    shape = w.shape
    cols = shape[-1]
    rows = int(np.prod(shape[:-1])) if len(shape) > 1 else 1
    w2, g2, m2, v2 = (t.reshape(rows, cols) for t in (w, g, m, v))
    tr = rows
    for cand in (256, 128, 64, 32, 16, 8):
        if rows % cand == 0 and rows > cand:
            tr = cand
            break
    c1 = float(1.0 - ADAM_B1 ** ADAM_STEP)
    c2 = float(1.0 - ADAM_B2 ** ADAM_STEP)

    def body(w_ref, g_ref, m_ref, v_ref, d_ref, mo_ref, vo_ref):
        gg = g_ref[...]
        mn = ADAM_B1 * m_ref[...] + (1.0 - ADAM_B1) * gg
        vn = ADAM_B2 * v_ref[...] + (1.0 - ADAM_B2) * (gg * gg)
        m_hat = mn / c1
        v_hat = vn / c2
        d_ref[...] = -ADAM_LR * (m_hat / (jnp.sqrt(v_hat) + ADAM_EPS) + ADAM_WD * w_ref[...])
        mo_ref[...] = mn
        vo_ref[...] = vn

    spec = pl.BlockSpec((tr, cols), lambda i: (i, 0))
    sds = jax.ShapeDtypeStruct((rows, cols), F32)
    d2, mo, vo = pl.pallas_call(
        body, name=name, out_shape=(sds, sds, sds), grid=(rows // tr,),
        in_specs=[spec] * 4, out_specs=(spec,) * 3,
        compiler_params=_params(("parallel",)))(w2, g2, m2, v2)
    return d2.reshape(shape), mo.reshape(shape), vo.reshape(shape)


def _row_tiles(LP):
    tm = LP // 4
    assert LP % 4 == 0 and tm % CONV_HALO == 0 and LP % ATT_BLK == 0, LP
    return tm, LP // 2


def _heads(t, LP):
    return t.reshape(LP, HEADS, HEAD_DIM).transpose(1, 0, 2)


def _unheads(t, LP):
    return t.transpose(1, 0, 2).reshape(LP, FOX_W)


def _ffn_fwd(h, gain, wug, wuv, cw, cb, wd, tm, tmm, tag):
    n = _rms_fwd(h, gain, BF16, tm, f"ffn_norm_{tag}")
    upg = _mm(n, wug, "nn", F32, tmm, 256, f"ffn_up_gate_{tag}")
    upv = _mm(n, wuv, "nn", F32, tmm, 256, f"ffn_up_val_{tag}")
    act = _ffn_act_fwd(upg, upv, cw, cb, tm, f"ffn_act_{tag}")
    out = _mm(act, wd, "nn", F32, tmm, 512, f"ffn_down_{tag}", add=h)
    return out, (n, upg, upv, act)


def _ffn_bwd(h, gain, wug, wuv, cw, cb, wd, saved, dout, tm, tmm, tag):
    n, upg, upv, act = saved
    dact = _mm(dout, wd, "nt", F32, tmm, 256, f"ffn_dact_{tag}")
    dwd = _mm(act, dout, "tn", F32, 256, 512, f"ffn_dwdown_{tag}")
    dupg, dupv, dcw, dcb = _ffn_act_bwd(upg, upv, dact, cw, cb, tm, f"ffn_act_bwd_{tag}")
    dn = _mm(dupg, wug, "nt", F32, tm, 512, f"ffn_dn_gate_{tag}")
    dn = _mm(dupv, wuv, "nt", F32, tm, 512, f"ffn_dn_val_{tag}", add=dn)
    dwug = _mm(n, dupg, "tn", F32, 512, 256, f"ffn_dwup_gate_{tag}")
    dwuv = _mm(n, dupv, "tn", F32, 512, 256, f"ffn_dwup_val_{tag}")
    dh, dgain = _rms_bwd(h, gain, dn, dout, tm, f"ffn_norm_bwd_{tag}")
    return dh, dict(gain=dgain, wug=dwug, wuv=dwuv, cw=dcw[:FFN_K], cb=dcb, wd=dwd)


def _local_step(h0, tgt, W, n_real):
    LP = h0.shape[0]
    tm, tmm = _row_tiles(LP)
    nb = LP // ATT_BLK
    G = {}

    n0 = _rms_fwd(h0, W["mix_norm_even"], BF16, tm, "mix_norm_even")
    proj = _mm(n0, W["w_in_p"], "nn", F32, tmm, 384, "in_proj")
    c = _fgate_fwd(proj, W["b_f_p"], "forget_gate")
    cT = c[:, :HEADS].T
    c_col = cT[:, :, None]
    c_row = cT.reshape(HEADS, nb, 1, ATT_BLK)
    qkv = proj[:, :3 * FOX_W].astype(BF16)
    q, k, v = (_heads(qkv[:, i * FOX_W:(i + 1) * FOX_W], LP) for i in range(3))
    o, lse = _attn_fwd(q, k, v, c_col, c_row, "fox_attention")
    u1, u = _conf_fwd(proj, W["conv_w_p"], W["conv_b"], W["ln_g"], W["ln_b"], tm, "conformer")
    cat = jnp.concatenate([_unheads(o, LP).astype(BF16), u], axis=1)
    h1 = _mm(cat, W["w_out"], "nn", F32, tmm, 512, "out_proj", add=h0)
    h2, ffn0 = _ffn_fwd(h1, W["ffn_norm"][0:1], W["w_up_g"][0], W["w_up_v"][0], W["ffn_conv_w_p"][0],
                        W["ffn_conv_b"][0:1], W["w_down"][0], tm, tmm, "0")
    h3, dpool = _pool_fwd(h2, W["mix_norm_odd"], W["pool_w"], W["pool_b"], W["pool_scale"], tm, "pool_mixer")
    h4, ffn1 = _ffn_fwd(h3, W["ffn_norm"][1:2], W["w_up_g"][1], W["w_up_v"][1], W["ffn_conv_w_p"][1],
                        W["ffn_conv_b"][1:2], W["w_down"][1], tm, tmm, "1")
    loss, dh4, G["final_norm"] = _loss_head(h4, W["final_norm"], tgt, n_real, tm, "loss_head")

    dh3, g1 = _ffn_bwd(h3, W["ffn_norm"][1:2], W["w_up_g"][1], W["w_up_v"][1], W["ffn_conv_w_p"][1],
                       W["ffn_conv_b"][1:2], W["w_down"][1], ffn1, dh4, tm, tmm, "1")
    dh2, G["pool_w"], G["pool_b"], G["pool_scale"], G["mix_norm_odd"] = _pool_bwd(
        h2, W["mix_norm_odd"], dpool, W["pool_w"], W["pool_b"], W["pool_scale"], dh3, tm, "pool_mixer_bwd")
    dh1, g0 = _ffn_bwd(h1, W["ffn_norm"][0:1], W["w_up_g"][0], W["w_up_v"][0], W["ffn_conv_w_p"][0],
                       W["ffn_conv_b"][0:1], W["w_down"][0], ffn0, dh2, tm, tmm, "0")
    for key in ("gain", "wug", "wuv", "cw", "cb", "wd"):
        G["ffn_" + key] = (g0[key], g1[key])

    dcat = _mm(dh1, W["w_out"], "nt", F32, tmm, 512, "out_proj_dx")
    G["w_out"] = _mm(cat, dh1, "tn", F32, 512, 512, "out_proj_dw")
    dadg, dcw, G["conv_b"], G["ln_g"], G["ln_b"] = _conf_bwd(
        proj, u1, dcat, W["conv_w_p"], W["ln_g"], W["ln_b"], tm, "conformer_bwd")
    G["conv_w"] = dcw[:CONV_K]
    do = _heads(dcat[:, :FOX_W], LP)
    dq, dk, dv, dcq, dck = _attn_bwd(q, k, v, o, do, lse, c_col, c_row, "fox_attention_bwd")
    dc = jnp.pad((dcq.reshape(HEADS, LP) + dck.reshape(HEADS, LP)).T, ((0, 0), (0, LANE - HEADS)))
    dfl, dbf = _fgate_bwd(proj, W["b_f_p"], dc, "forget_gate_bwd")
    G["b_f"] = dbf[:, :HEADS]
    dproj = jnp.concatenate([_unheads(t, LP).astype(BF16) for t in (dq, dk, dv)] + [dadg, dfl], axis=1)
    dn0 = _mm(dproj, W["w_in_p"], "nt", F32, tmm, 512, "in_proj_dx")
    G["w_in_p"] = _mm(n0, dproj, "tn", F32, 512, 384, "in_proj_dw")
    dh0, G["mix_norm_even"] = _rms_bwd(h0, W["mix_norm_even"], dn0, dh1, tm, "mix_norm_even_bwd")
    return loss, dh0, G


def _compute_layout(P):
    w_in = P["w_in"].reshape(D_MODEL, IN_COLS)
    qkv, f, ag = w_in[:, :3 * FOX_W], w_in[:, 3 * FOX_W:3 * FOX_W + HEADS], w_in[:, 3 * FOX_W + HEADS:]
    w_in_p = jnp.concatenate([qkv, ag, f, jnp.zeros((D_MODEL, LANE - HEADS), w_in.dtype)], axis=1).astype(BF16)
    w_up = P["w_up"].astype(BF16)
    return dict(
        mix_norm_even=P["mix_norm_even"].reshape(1, D_MODEL).astype(F32),
        w_in_p=w_in_p,
        b_f_p=jnp.pad(P["b_f"].reshape(1, HEADS).astype(F32), ((0, 0), (0, LANE - HEADS))),
        conv_w_p=jnp.pad(P["conv_w"].reshape(CONV_K, CONV_CH).astype(F32), ((0, CONV_HALO - CONV_K), (0, 0))),
        conv_b=P["conv_b"].reshape(1, CONV_CH).astype(F32),
        ln_g=P["ln_g"].reshape(1, CONV_CH).astype(F32),
        ln_b=P["ln_b"].reshape(1, CONV_CH).astype(F32),
        w_out=P["w_out"].reshape(D_MODEL, D_MODEL).astype(BF16),
        mix_norm_odd=P["mix_norm_odd"].reshape(1, D_MODEL).astype(F32),
        pool_w=P["pool_w"].reshape(len(POOL_WINDOWS), POOL_G, POOL_G).astype(BF16),
        pool_b=P["pool_b"].reshape(1, D_MODEL).astype(F32),
        pool_scale=P["pool_scale"].reshape(1, D_MODEL).astype(F32),
        ffn_norm=P["ffn_norm"].astype(F32),
        w_up_g=w_up[:, :, :D_FF],
        w_up_v=w_up[:, :, D_FF:],
        ffn_conv_w_p=jnp.pad(P["ffn_conv_w"].astype(F32), ((0, 0), (0, 8 - FFN_K), (0, 0))),
        ffn_conv_b=P["ffn_conv_b"].astype(F32),
        w_down=P["w_down"].astype(BF16),
        final_norm=P["final_norm"].reshape(1, D_MODEL).astype(F32),
    )


def _reference_layout(G, dh0):
    gp = G["w_in_p"]
    g_w_in = jnp.concatenate([gp[:, :3 * FOX_W], gp[:, 3 * FOX_W + 2 * CONV_CH:3 * FOX_W + 2 * CONV_CH + HEADS],
                              gp[:, 3 * FOX_W:3 * FOX_W + 2 * CONV_CH]], axis=1)
    return dict(
        meta_tokens=dh0[:N_META],
        mix_norm_even=G["mix_norm_even"],
        w_in=g_w_in[None],
        b_f=G["b_f"],
        conv_w=G["conv_w"][None],
        conv_b=G["conv_b"],
        ln_g=G["ln_g"],
        ln_b=G["ln_b"],
        w_out=G["w_out"][None],
        mix_norm_odd=G["mix_norm_odd"],
        pool_w=G["pool_w"][None],
        pool_b=G["pool_b"].reshape(1, len(POOL_WINDOWS), POOL_G),
        pool_scale=G["pool_scale"],
        ffn_norm=jnp.concatenate(G["ffn_gain"], axis=0),
        w_up=jnp.stack([jnp.concatenate([g, v], axis=1) for g, v in zip(G["ffn_wug"], G["ffn_wuv"])]),
        ffn_conv_w=jnp.stack(G["ffn_cw"]),
        ffn_conv_b=jnp.concatenate(G["ffn_cb"], axis=0),
        w_down=jnp.stack(G["ffn_wd"]),
        final_norm=G["final_norm"].reshape(D_MODEL),
    )


MESH = pl.DeviceIdType.MESH
ANY = pl.BlockSpec(memory_space=pl.ANY)
PACK_COLS = 1024


def _coords():
    return lax.axis_index("x"), lax.axis_index("y"), lax.axis_index("c")


def _other_chips(x, y):
    return [(1 - x, y), (x, 1 - y), (1 - x, 1 - y)]


def _allgather_chips(pack):
    R, C = pack.shape
    R2 = R // 2

    def body(x_ref, o_ref, send_sems, recv_sems, local_sem):
        x, y, c = _coords()
        sibling = (x, y, 1 - c)
        chips = _other_chips(x, y)

        def slot(px, py, half):
            return o_ref.at[2 * px + py, pl.ds(half * R2, R2), :]

        def copy(k, src, dst, to):
            return pltpu.make_async_remote_copy(src_ref=src, dst_ref=dst, send_sem=send_sems.at[k],
                                                recv_sem=recv_sems.at[k], device_id=to, device_id_type=MESH)

        mine = pltpu.make_async_copy(x_ref, o_ref.at[2 * x + y], local_sem)
        mine.start()
        my_half = x_ref.at[pl.ds(c * R2, R2), :]
        first = [copy(j, my_half, slot(x, y, c), (*chip, c)) for j, chip in enumerate(chips)]
        for cp in first:
            cp.start()
        passed = [copy(3 + j, slot(*chip, c), slot(*chip, c), sibling) for j, chip in enumerate(chips)]
        for j, chip in enumerate(chips):
            copy(j, my_half, slot(*chip, c), sibling).wait_recv()
            passed[j].start()
        for j, chip in enumerate(chips):
            copy(3 + j, my_half, slot(*chip, 1 - c), sibling).wait_recv()
        for cp in first + passed:
            cp.wait_send()
        mine.wait()

    return pl.pallas_call(
        body, name="allgather_weights", out_shape=jax.ShapeDtypeStruct((N_CHIPS, R, C), pack.dtype),
        in_specs=[ANY], out_specs=ANY,
        scratch_shapes=[pltpu.SemaphoreType.DMA((6,)), pltpu.SemaphoreType.DMA((6,)), pltpu.SemaphoreType.DMA],
    )(pack)


def _pair_exchange(G):
    n, R, C = G.shape
    R2 = R // 2

    def body(g_ref, o_ref, send_sem, recv_sem):
        x, y, c = _coords()
        src = g_ref.at[pl.ds(0, n), pl.ds((1 - c) * R2, R2), :]
        cp = pltpu.make_async_remote_copy(src_ref=src, dst_ref=o_ref, send_sem=send_sem, recv_sem=recv_sem,
                                          device_id=(x, y, 1 - c), device_id_type=MESH)
        cp.start()
        cp.wait()

    return pl.pallas_call(
        body, name="grad_pair_exchange", out_shape=jax.ShapeDtypeStruct((n, R2, C), G.dtype),
        in_specs=[ANY], out_specs=ANY,
        scratch_shapes=[pltpu.SemaphoreType.DMA, pltpu.SemaphoreType.DMA],
    )(G)


def _row_tile(rows, align, cap):
    best = None
    for t in range(align, min(rows, cap) + 1, align):
        if rows % t == 0:
            best = t
    assert best is not None, (rows, align, cap)
    return best


def _pair_sum(G, recv):
    n, R, C = G.shape
    R2 = R // 2
    tr = _row_tile(R2, 16, 704)
    nrb = R2 // tr
    half = lax.axis_index("c").astype(jnp.int32).reshape(1)

    def body(c_ref, g_ref, r_ref, o_ref):
        o_ref[...] = (g_ref[...] + r_ref[...]).astype(o_ref.dtype)

    return pl.pallas_call(
        body, name="grad_pair_sum", out_shape=jax.ShapeDtypeStruct((n, R2, C), BF16),
        grid_spec=pltpu.PrefetchScalarGridSpec(
            num_scalar_prefetch=1, grid=(n, nrb),
            in_specs=[pl.BlockSpec((None, tr, C), lambda j, i, c_ref: (j, c_ref[0] * nrb + i, 0)),
                      pl.BlockSpec((None, tr, C), lambda j, i, c_ref: (j, i, 0))],
            out_specs=pl.BlockSpec((None, tr, C), lambda j, i, c_ref: (j, i, 0))),
        compiler_params=_params(("parallel", "parallel")))(half, G, recv)


def _chip_exchange(P):
    n, R2, C = P.shape

    def body(p_ref, o_ref, send_sems, recv_sems, local_sem):
        x, y, c = _coords()
        me = 2 * x + y
        chips = _other_chips(x, y)
        mine = pltpu.make_async_copy(p_ref.at[me], o_ref.at[me], local_sem)
        mine.start()
        sends = [pltpu.make_async_remote_copy(
            src_ref=p_ref.at[2 * px + py], dst_ref=o_ref.at[me], send_sem=send_sems.at[k],
            recv_sem=recv_sems.at[k], device_id=(px, py, c), device_id_type=MESH)
            for k, (px, py) in enumerate(chips)]
        for cp in sends:
            cp.start()
        for k, (px, py) in enumerate(chips):
            pltpu.make_async_remote_copy(
                src_ref=p_ref.at[me], dst_ref=o_ref.at[2 * px + py], send_sem=send_sems.at[k],
                recv_sem=recv_sems.at[k], device_id=(px, py, c), device_id_type=MESH).wait_recv()
        for cp in sends:
            cp.wait_send()
        mine.wait()

    return pl.pallas_call(
        body, name="grad_chip_exchange", out_shape=jax.ShapeDtypeStruct((n, R2, C), P.dtype),
        in_specs=[ANY], out_specs=ANY,
        scratch_shapes=[pltpu.SemaphoreType.DMA((3,)), pltpu.SemaphoreType.DMA((3,)), pltpu.SemaphoreType.DMA],
    )(P)


def _chip_sum(X):
    n, R2, C = X.shape
    tr = _row_tile(R2, 16, 704)

    def body(x_ref, o_ref):
        acc = x_ref[0].astype(F32)
        for s in range(1, n):
            acc = acc + x_ref[s].astype(F32)
        o_ref[...] = acc

    return pl.pallas_call(
        body, name="grad_chip_sum", out_shape=jax.ShapeDtypeStruct((R2, C), F32), grid=(R2 // tr,),
        in_specs=[pl.BlockSpec((n, tr, C), lambda i: (0, i, 0))],
        out_specs=pl.BlockSpec((tr, C), lambda i: (i, 0)),
        compiler_params=_params(("parallel",)))(X)


def _pair_allgather(Q):
    R2, C = Q.shape

    def body(q_ref, o_ref, send_sem, recv_sem, local_sem):
        x, y, c = _coords()
        mine = pltpu.make_async_copy(q_ref, o_ref.at[c], local_sem)
        mine.start()
        cp = pltpu.make_async_remote_copy(src_ref=q_ref, dst_ref=o_ref.at[c], send_sem=send_sem,
                                          recv_sem=recv_sem, device_id=(x, y, 1 - c), device_id_type=MESH)
        cp.start()
        pltpu.make_async_remote_copy(src_ref=q_ref, dst_ref=o_ref.at[1 - c], send_sem=send_sem,
                                     recv_sem=recv_sem, device_id=(x, y, 1 - c), device_id_type=MESH).wait_recv()
        cp.wait_send()
        mine.wait()

    return pl.pallas_call(
        body, name="grad_pair_allgather", out_shape=jax.ShapeDtypeStruct((2, R2, C), Q.dtype),
        in_specs=[ANY], out_specs=ANY,
        scratch_shapes=[pltpu.SemaphoreType.DMA, pltpu.SemaphoreType.DMA, pltpu.SemaphoreType.DMA],
    )(Q)


def _allreduce_small(pack):
    Rs, C = pack.shape
    n_dev = 8

    def body(x_ref, o_ref, buf, send_sems, recv_sems):
        x, y, c = _coords()
        me = 4 * x + 2 * y + c
        buf[me] = x_ref[...]
        peers = []
        for rel in range(1, n_dev):
            px = 1 - x if rel & 4 else x
            py = 1 - y if rel & 2 else y
            pc = 1 - c if rel & 1 else c
            peers.append((px, py, pc))
        sends = [pltpu.make_async_remote_copy(
            src_ref=x_ref, dst_ref=buf.at[me], send_sem=send_sems.at[k], recv_sem=recv_sems.at[k],
            device_id=peer, device_id_type=MESH) for k, peer in enumerate(peers)]
        for cp in sends:
            cp.start()
        for k, (px, py, pc) in enumerate(peers):
            pltpu.make_async_remote_copy(
                src_ref=x_ref, dst_ref=buf.at[4 * px + 2 * py + pc], send_sem=send_sems.at[k],
                recv_sem=recv_sems.at[k], device_id=(px, py, pc), device_id_type=MESH).wait_recv()
        for cp in sends:
            cp.wait_send()
        acc = buf[0]
        for d in range(1, n_dev):
            acc = acc + buf[d]
        o_ref[...] = acc

    vm = pl.BlockSpec(memory_space=pltpu.VMEM)
    return pl.pallas_call(
        body, name="allreduce_replicated", out_shape=jax.ShapeDtypeStruct((Rs, C), F32),
        in_specs=[vm], out_specs=vm,
        scratch_shapes=[pltpu.VMEM((n_dev, Rs, C), F32), pltpu.SemaphoreType.DMA((n_dev - 1,)),
                        pltpu.SemaphoreType.DMA((n_dev - 1,))],
    )(pack)


SHARDED = (
    ("w_in", 2, True), ("w_out", 1, True), ("pool_w", 2, True), ("w_up", 2, True), ("w_down", 1, True),
    ("meta_tokens", 1, False), ("mix_norm_odd", 1, False), ("pool_b", 2, False), ("pool_scale", 1, False),
    ("conv_w", 2, False), ("ffn_conv_w", 2, False))
REPLICATED = ("mix_norm_even", "b_f", "conv_b", "ln_g", "ln_b", "ffn_norm", "ffn_conv_b", "final_norm")
PACK_ROW_ALIGN = 32


def _pad_rows(flat, align_rows, cols):
    rows = -(-flat.shape[-1] // cols)
    rows = -(-rows // align_rows) * align_rows
    pad = rows * cols - flat.shape[-1]
    flat = jnp.pad(flat, [(0, 0)] * (flat.ndim - 1) + [(0, pad)])
    return flat.reshape(flat.shape[:-1] + (rows, cols))


def _pack_weight_shards(shards):
    parts = []
    for name, _, as_bf16 in SHARDED:
        w = shards[name].astype(F32).reshape(-1)
        parts.append(w.astype(BF16) if as_bf16 else lax.bitcast_convert_type(w, BF16).reshape(-1))
    return _pad_rows(jnp.concatenate(parts), PACK_ROW_ALIGN, PACK_COLS)


def _unpack_weights(gathered, shards):
    flat = gathered.reshape(N_CHIPS, -1)
    out, off = {}, 0
    for name, axis, as_bf16 in SHARDED:
        shp = shards[name].shape
        n = int(np.prod(shp))
        if as_bf16:
            t = flat[:, off:off + n]
            off += n
        else:
            t = lax.bitcast_convert_type(flat[:, off:off + 2 * n].reshape(N_CHIPS, n, 2), F32)
            off += 2 * n
        t = t.reshape((N_CHIPS,) + shp)
        out[name] = jnp.concatenate([t[j] for j in range(N_CHIPS)], axis=axis)
    return out


def _pack_grad_shards(grads, shards):
    parts = []
    for name, axis, _ in SHARDED:
        g = grads[name].reshape(shards[name].shape[:axis] + (N_CHIPS, shards[name].shape[axis])
                                + shards[name].shape[axis + 1:])
        parts.append(jnp.moveaxis(g, axis, 0).reshape(N_CHIPS, -1))
    return _pad_rows(jnp.concatenate(parts, axis=1), PACK_ROW_ALIGN, PACK_COLS)


def _unpack_grad_shard(reduced, shards):
    flat = reduced.reshape(-1)
    out, off = {}, 0
    for name, _, _ in SHARDED:
        shp = shards[name].shape
        n = int(np.prod(shp))
        out[name] = flat[off:off + n].reshape(shp)
        off += n
    return out


def _pack_replicated(grads, loss):
    parts = [_pad_rows(grads[name].astype(F32).reshape(-1), 1, LANE).reshape(-1) for name in REPLICATED]
    parts.append(_pad_rows(loss.reshape(-1)[:1], 1, LANE).reshape(-1))
    return _pad_rows(jnp.concatenate(parts), 8, LANE)


def _unpack_replicated(reduced, shapes):
    flat = reduced.reshape(-1)
    out, off = {}, 0
    for name in REPLICATED:
        n = int(np.prod(shapes[name]))
        out[name] = flat[off:off + n].reshape(shapes[name])
        off += -(-n // LANE) * LANE
    return out, flat[off]


WEIGHT_NAMES = ("meta_tokens", "mix_norm_even", "w_in", "b_f", "conv_w", "conv_b", "ln_g", "ln_b", "w_out",
                "mix_norm_odd", "pool_w", "pool_b", "pool_scale", "ffn_norm", "w_up", "ffn_conv_w",
                "ffn_conv_b", "w_down", "final_norm")


def kernel(x, meta_tokens, mix_norm_even, w_in, b_f, conv_w, conv_b, ln_g, ln_b, w_out, mix_norm_odd, pool_w, pool_b, pool_scale, ffn_norm, w_up, ffn_conv_w, ffn_conv_b, w_down, final_norm, loss_target, m_meta_tokens, m_mix_norm_even, m_w_in, m_b_f, m_conv_w, m_conv_b, m_ln_g, m_ln_b, m_w_out, m_mix_norm_odd, m_pool_w, m_pool_b, m_pool_scale, m_ffn_norm, m_w_up, m_ffn_conv_w, m_ffn_conv_b, m_w_down, m_final_norm, v_meta_tokens, v_mix_norm_even, v_w_in, v_b_f, v_conv_w, v_conv_b, v_ln_g, v_ln_b, v_w_out, v_mix_norm_odd, v_pool_w, v_pool_b, v_pool_scale, v_ffn_norm, v_w_up, v_ffn_conv_w, v_ffn_conv_b, v_w_down, v_final_norm):
    given = dict(locals())
    w_loc = {n: given[n] for n in WEIGHT_NAMES}
    m_loc = {n: given["m_" + n] for n in WEIGHT_NAMES}
    v_loc = {n: given["v_" + n] for n in WEIGHT_NAMES}
    sharded = {name for name, _, _ in SHARDED}

    shards = {n: w_loc[n] for n in sharded}
    gathered = _allgather_chips(_pack_weight_shards(shards))
    full = _unpack_weights(gathered, shards)
    full.update({n: w_loc[n] for n in REPLICATED})
    W = _compute_layout(full)

    seq = x.shape[1]
    n_real = N_META + seq
    LP = -(-n_real // ATT_BLK) * ATT_BLK
    tail = jnp.zeros((LP - n_real, D_MODEL), F32)
    h0 = jnp.concatenate([full["meta_tokens"].astype(F32), x[0], tail], axis=0)
    tgt = jnp.concatenate([jnp.zeros((N_META, D_MODEL), F32), loss_target[0], tail], axis=0)
    loss_loc, dh0, G = _local_step(h0, tgt, W, n_real)
    grads = _reference_layout(G, dh0)
    grad_x = dh0[N_META:n_real][None]

    rep_shapes = {n: w_loc[n].shape for n in REPLICATED}
    rep, loss = _unpack_replicated(_allreduce_small(_pack_replicated(grads, loss_loc)), rep_shapes)

    gpack = _pack_grad_shards(grads, shards)
    pair = _pair_sum(gpack, _pair_exchange(gpack))
    half = _chip_sum(_chip_exchange(pair))
    both = _pair_allgather(half)
    gsh = _unpack_grad_shard(both.reshape(-1, PACK_COLS), shards)

    grad_w = {n: (gsh[n] if n in sharded else rep[n]) for n in WEIGHT_NAMES}
    delta, new_m, new_v = {}, {}, {}
    for n in WEIGHT_NAMES:
        delta[n], new_m[n], new_v[n] = _adamw(w_loc[n], grad_w[n], m_loc[n], v_loc[n], "adamw_" + n)
    return (loss, grad_x, *[grad_w[n] for n in WEIGHT_NAMES], *[delta[n] for n in WEIGHT_NAMES],
            *[new_m[n] for n in WEIGHT_NAMES], *[new_v[n] for n in WEIGHT_NAMES])
```

```python
import functools

import numpy as np
import jax
import jax.numpy as jnp
from jax import lax
from jax.experimental import pallas as pl
from jax.experimental.pallas import tpu as pltpu

F32 = jnp.float32
BF16 = jnp.bfloat16

D_MODEL = 1024
N_META = 16
SEQ = 2048
HEADS = 8
HEAD_DIM = 64
FOX_W = HEADS * HEAD_DIM
CONV_CH = 512
CONV_K = 31
D_FF = 2816
POOL_WINDOWS = (2, 4, 8, 16)
POOL_G = 256
RMS_EPS = 1e-6
LN_EPS = 1e-5
IN_COLS = 3 * FOX_W + HEADS + 2 * CONV_CH
IN_COLS_P = 3 * FOX_W + 2 * CONV_CH + 128
F_COL_BLK = (3 * FOX_W + 2 * CONV_CH) // 128
N_CHIPS = 4
IN_SHARD = IN_COLS // N_CHIPS
UP_SHARD = 2 * D_FF // N_CHIPS
DOWN_SHARD = D_FF // N_CHIPS

ADAM_LR = 0.001
ADAM_B1 = 0.9
ADAM_B2 = 0.999
ADAM_EPS = 1e-08
ADAM_WD = 0.01
ADAM_STEP = 10

LANE = 128
ATT_BLK = 128
VMEM_LIMIT = 48 * 1024 * 1024

NEG = -1e30


def _sigmoid(x):
    return 1.0 / (1.0 + jnp.exp(-x))


def _params(sem=None):
    return pltpu.CompilerParams(dimension_semantics=sem, vmem_limit_bytes=VMEM_LIMIT)


def _sub_rows(tm):
    best = 8
    for s in range(8, 137, 8):
        if tm % s == 0:
            best = s
    return best


def _mm(a, b, mode, out_dtype, tm, tn, name, add=None, a_lead=None, b_lead=None, out=None):
    a_shape = a.shape if a_lead is None else a.shape[1:]
    b_shape = b.shape if b_lead is None else b.shape[1:]
    if mode == "nn":
        (M, K), (K2, N) = a_shape, b_shape
        dims = (((1,), (0,)), ((), ()))
        a_blk, a_idx = (tm, K), (lambda i, j: (i, 0))
        b_blk, b_idx = (K, tn), (lambda i, j: (0, j))
    elif mode == "nt":
        (M, K), (N, K2) = a_shape, b_shape
        dims = (((1,), (1,)), ((), ()))
        a_blk, a_idx = (tm, K), (lambda i, j: (i, 0))
        b_blk, b_idx = (tn, K), (lambda i, j: (j, 0))
    else:
        (K, M), (K2, N) = a_shape, b_shape
        dims = (((0,), (0,)), ((), ()))
        a_blk, a_idx = (K, tm), (lambda i, j: (0, i))
        b_blk, b_idx = (K, tn), (lambda i, j: (0, j))
    assert K == K2 and M % tm == 0 and N % tn == 0, (name, a.shape, b.shape, tm, tn)
    gm, gn = M // tm, N // tn
    a_bytes = M * K * a.dtype.itemsize
    b_bytes = N * K * b.dtype.itemsize
    m_outer = a_bytes + b_bytes * gm <= b_bytes + a_bytes * gn
    if m_outer:
        grid = (gm, gn)
        wrap = lambda f: f
    else:
        grid = (gn, gm)
        wrap = lambda f: (lambda j, i: f(i, j))

    def lead(blk, idx, at):
        if at is None:
            return pl.BlockSpec(blk, wrap(idx))
        return pl.BlockSpec((None,) + blk, wrap(lambda i, j: (at,) + idx(i, j)))

    o_idx = lambda i, j: (i, j)
    in_specs = [lead(a_blk, a_idx, a_lead), lead(b_blk, b_idx, b_lead)]
    args = [a, b]
    if add is not None:
        in_specs.append(pl.BlockSpec((tm, tn), wrap(o_idx)))
        args.append(add)
    aliases = {}
    if out is None:
        out_shape = jax.ShapeDtypeStruct((M, N), out_dtype)
        out_spec = pl.BlockSpec((tm, tn), wrap(o_idx))
    else:
        o_lead, n_lead, into = out
        out_shape = jax.ShapeDtypeStruct((n_lead, M, N), out_dtype)
        out_spec = lead((tm, tn), o_idx, o_lead)
        if into is not None:
            aliases = {len(args): 0}
            in_specs.append(pl.BlockSpec(memory_space=pl.ANY))
            args.append(into)
    has_add = add is not None

    def body(a_ref, b_ref, *rest):
        o_ref = rest[-1]
        x = a_ref[...].astype(BF16)
        y = b_ref[...].astype(BF16)
        acc = lax.dot_general(x, y, dims, preferred_element_type=F32)
        if has_add:
            acc = acc + rest[0][...]
        o_ref[...] = acc.astype(o_ref.dtype)

    return pl.pallas_call(
        body, name=name, out_shape=out_shape, grid=grid, in_specs=in_specs, out_specs=out_spec,
        input_output_aliases=aliases, compiler_params=_params(("parallel", "parallel")))(*args)


def _mm_ffn_dn(dup, w_up, layer, tm, tn, name):
    _, LP, F = dup.shape
    Dm = w_up.shape[1]
    nt = (((1,), (1,)), ((), ()))

    def body(a_ref, b_ref, o_ref):
        acc = lax.dot_general(a_ref[0], b_ref[:, 0:F], nt, preferred_element_type=F32)
        acc = acc + lax.dot_general(a_ref[1], b_ref[:, F:2 * F], nt, preferred_element_type=F32)
        o_ref[...] = acc

    return pl.pallas_call(
        body, name=name, out_shape=jax.ShapeDtypeStruct((LP, Dm), F32), grid=(LP // tm, Dm // tn),
        in_specs=[pl.BlockSpec((2, tm, F), lambda i, j: (0, i, 0)),
                  pl.BlockSpec((None, tn, 2 * F), lambda i, j: (layer, j, 0))],
        out_specs=pl.BlockSpec((tm, tn), lambda i, j: (i, j)),
        compiler_params=_params(("parallel", "parallel")))(dup, w_up)


def _mm_ffn_dwup(n, dup, layer, into, tk, tn, name):
    LP, Dm = n.shape
    F = dup.shape[2]
    nct = F // tn
    tdims = (((0,), (0,)), ((), ()))

    def body(a_ref, b_ref, *rest):
        rest[-1][...] = lax.dot_general(a_ref[...], b_ref[...], tdims, preferred_element_type=F32)

    in_specs = [pl.BlockSpec((LP, tk), lambda i, j: (0, i)),
                pl.BlockSpec((None, LP, tn), lambda i, j: (j // nct, 0, j % nct))]
    args = [n, dup]
    aliases = {}
    if into is not None:
        in_specs.append(pl.BlockSpec(memory_space=pl.ANY))
        args.append(into)
        aliases = {2: 0}
    return pl.pallas_call(
        body, name=name, out_shape=jax.ShapeDtypeStruct((2, Dm, 2 * F), F32), grid=(Dm // tk, 2 * nct),
        in_specs=in_specs, out_specs=pl.BlockSpec((None, tk, tn), lambda i, j: (layer, i, j)),
        input_output_aliases=aliases, compiler_params=_params(("parallel", "parallel")))(*args)


def _rms_fwd(h, g, out_dtype, tm, name):
    LP, Dm = h.shape

    def body(h_ref, g_ref, o_ref):
        x = h_ref[...]
        r = lax.rsqrt(jnp.mean(x * x, axis=1, keepdims=True) + RMS_EPS)
        o_ref[...] = (x * r * g_ref[...]).astype(o_ref.dtype)

    return pl.pallas_call(
        body, name=name, out_shape=jax.ShapeDtypeStruct((LP, Dm), out_dtype), grid=(LP // tm,),
        in_specs=[pl.BlockSpec((tm, Dm), lambda i: (i, 0)), pl.BlockSpec((1, Dm), lambda i: (0, 0))],
        out_specs=pl.BlockSpec((tm, Dm), lambda i: (i, 0)),
        compiler_params=_params(("parallel",)))(h, g)


def _rms_bwd(h, g, dn, dres, tm, name):
    LP, Dm = h.shape

    def body(h_ref, g_ref, dn_ref, dr_ref, dh_ref, dg_ref):
        i = pl.program_id(0)
        x = h_ref[...]
        r = lax.rsqrt(jnp.mean(x * x, axis=1, keepdims=True) + RMS_EPS)
        xhat = x * r
        dy = dn_ref[...]
        dxh = dy * g_ref[...]
        dh = r * (dxh - xhat * jnp.mean(dxh * xhat, axis=1, keepdims=True))
        dh_ref[...] = dr_ref[...] + dh

        @pl.when(i == 0)
        def _():
            dg_ref[...] = jnp.zeros_like(dg_ref)

        dg_ref[...] += jnp.sum(dy * xhat, axis=0, keepdims=True)

    row = pl.BlockSpec((tm, Dm), lambda i: (i, 0))
    vec = pl.BlockSpec((1, Dm), lambda i: (0, 0))
    return pl.pallas_call(
        body, name=name,
        out_shape=(jax.ShapeDtypeStruct((LP, Dm), F32), jax.ShapeDtypeStruct((1, Dm), F32)),
        grid=(LP // tm,), in_specs=[row, vec, row, row], out_specs=(row, vec),
        compiler_params=_params(("arbitrary",)))(h, g, dn, dres)


def _loss_head(h, g, tgt, n_real, tm, name):
    LP, Dm = h.shape

    def body(h_ref, g_ref, t_ref, loss_ref, dh_ref, dg_ref):
        i = pl.program_id(0)
        x = h_ref[...]
        gg = g_ref[...]
        r = lax.rsqrt(jnp.mean(x * x, axis=1, keepdims=True) + RMS_EPS)
        xhat = x * r
        rows = i * tm + lax.broadcasted_iota(jnp.int32, (tm, 1), 0)
        real = jnp.logical_and(rows >= N_META, rows < n_real)
        diff = jnp.where(real, xhat * gg - t_ref[...], 0.0)
        dy = diff * (1.0 / Dm)
        dxh = dy * gg
        dh_ref[...] = r * (dxh - xhat * jnp.mean(dxh * xhat, axis=1, keepdims=True))

        @pl.when(i == 0)
        def _():
            dg_ref[...] = jnp.zeros_like(dg_ref)
            loss_ref[...] = jnp.zeros_like(loss_ref)

        dg_ref[...] += jnp.sum(dy * xhat, axis=0, keepdims=True)
        part = jnp.sum(jnp.sum(diff * diff, axis=1, keepdims=True), axis=0, keepdims=True)
        loss_ref[...] += jnp.broadcast_to(part * (0.5 / Dm), loss_ref.shape)

    row = pl.BlockSpec((tm, Dm), lambda i: (i, 0))
    vec = pl.BlockSpec((1, Dm), lambda i: (0, 0))
    return pl.pallas_call(
        body, name=name,
        out_shape=(jax.ShapeDtypeStruct((1, LANE), F32), jax.ShapeDtypeStruct((LP, Dm), F32),
                   jax.ShapeDtypeStruct((1, Dm), F32)),
        grid=(LP // tm,), in_specs=[row, vec, row],
        out_specs=(pl.BlockSpec((1, LANE), lambda i: (0, 0)), row, vec),
        compiler_params=_params(("arbitrary",)))(h, g, tgt)


def _fgate_fwd(proj, bf_p, name):
    LP = proj.shape[0]
    nb = LP // LANE

    def body(f_ref, b_ref, c_ref, lf_ref):
        x = f_ref[...] + b_ref[...]
        lf_ref[...] = jnp.minimum(x, 0.0) - jnp.log1p(jnp.exp(-jnp.abs(x)))
        ri = lax.broadcasted_iota(jnp.int32, (LANE, LANE), 0)
        ci = lax.broadcasted_iota(jnp.int32, (LANE, LANE), 1)
        tri = jnp.where(ri >= ci, 1.0, 0.0).astype(F32)

        def blk(i, carry):
            rows = pl.ds(pl.multiple_of(i * LANE, LANE), LANE)
            cb = jnp.dot(tri, lf_ref[rows, :], precision=lax.Precision.HIGHEST,
                         preferred_element_type=F32) + carry
            c_ref[rows, :] = cb
            return cb[LANE - 1:LANE, :]

        lax.fori_loop(0, nb, blk, jnp.zeros((1, LANE), F32))

    return pl.pallas_call(
        body, name=name, out_shape=jax.ShapeDtypeStruct((LP, LANE), F32), grid=(1,),
        in_specs=[pl.BlockSpec((LP, LANE), lambda i: (0, F_COL_BLK)),
                  pl.BlockSpec((1, LANE), lambda i: (0, 0))],
        out_specs=pl.BlockSpec((LP, LANE), lambda i: (0, 0)),
        scratch_shapes=[pltpu.VMEM((LP, LANE), F32)],
        compiler_params=_params(("arbitrary",)))(proj, bf_p)


def _fgate_bwd(proj, bf_p, dc, name):
    LP = proj.shape[0]
    nb = LP // LANE

    def body(f_ref, b_ref, dc_ref, dl_ref, db_ref):
        ri = lax.broadcasted_iota(jnp.int32, (LANE, LANE), 0)
        ci = lax.broadcasted_iota(jnp.int32, (LANE, LANE), 1)
        triu = jnp.where(ri <= ci, 1.0, 0.0).astype(F32)
        bb = b_ref[...]

        tail = jnp.zeros((1, LANE), F32)
        dbs = jnp.zeros((1, LANE), F32)
        for i in range(nb - 1, -1, -1):
            rows = slice(i * LANE, (i + 1) * LANE)
            gb = jnp.dot(triu, dc_ref[rows, :], precision=lax.Precision.HIGHEST,
                         preferred_element_type=F32) + tail
            x = f_ref[rows, :] + bb
            dl = gb * _sigmoid(-x)
            dl_ref[rows, :] = dl.astype(dl_ref.dtype)
            tail = gb[0:1, :]
            dbs = dbs + jnp.sum(dl, axis=0, keepdims=True)
        db_ref[...] = dbs

    return pl.pallas_call(
        body, name=name,
        out_shape=(jax.ShapeDtypeStruct((LP, LANE), BF16), jax.ShapeDtypeStruct((1, LANE), F32)),
        grid=(1,),
        in_specs=[pl.BlockSpec((LP, LANE), lambda i: (0, F_COL_BLK)),
                  pl.BlockSpec((1, LANE), lambda i: (0, 0)),
                  pl.BlockSpec((LP, LANE), lambda i: (0, 0))],
        out_specs=(pl.BlockSpec((LP, LANE), lambda i: (0, 0)), pl.BlockSpec((1, LANE), lambda i: (0, 0))),
        compiler_params=_params(("arbitrary",)))(proj, bf_p, dc)


def _attn_fwd(q, k, v, c_col, c_row, name):
    Hh, LP, Dh = q.shape
    nb = LP // ATT_BLK
    scale = Dh ** -0.5
    nt = (((1,), (1,)), ((), ()))

    def body(q_ref, k_ref, v_ref, cc_ref, cr_ref, o_ref, lse_ref):
        i = pl.program_id(1)
        qb = q_ref[...]
        cq = cc_ref[...]
        rows = i * ATT_BLK + lax.broadcasted_iota(jnp.int32, (ATT_BLK, ATT_BLK), 0)
        cols0 = lax.broadcasted_iota(jnp.int32, (ATT_BLK, ATT_BLK), 1)

        def step(j, carry):
            m, l, acc = carry
            ks = pl.ds(pl.multiple_of(j * ATT_BLK, ATT_BLK), ATT_BLK)
            s = lax.dot_general(qb, k_ref[ks, :], nt, preferred_element_type=F32) * scale
            s = s + cq - cr_ref[j]
            s = jnp.where(cols0 + j * ATT_BLK <= rows, s, NEG)
            m_new = jnp.maximum(m, jnp.max(s, axis=1, keepdims=True))
            p = jnp.exp(s - m_new)
            alpha = jnp.exp(m - m_new)
            l = alpha * l + jnp.sum(p, axis=1, keepdims=True)
            acc = alpha * acc + jnp.dot(p.astype(BF16), v_ref[ks, :], preferred_element_type=F32)
            return m_new, l, acc

        init = (jnp.full((ATT_BLK, 1), NEG, F32), jnp.zeros((ATT_BLK, 1), F32),
                jnp.zeros((ATT_BLK, Dh), F32))
        m, l, acc = lax.fori_loop(0, i + 1, step, init)
        o_ref[...] = acc / l
        lse_ref[...] = m + jnp.log(l)

    qspec = pl.BlockSpec((None, ATT_BLK, Dh), lambda h, i: (h, i, 0))
    kspec = pl.BlockSpec((None, LP, Dh), lambda h, i: (h, 0, 0))
    colspec = pl.BlockSpec((None, ATT_BLK, 1), lambda h, i: (h, i, 0))
    rowspec = pl.BlockSpec((None, nb, 1, ATT_BLK), lambda h, i: (h, 0, 0, 0))
    return pl.pallas_call(
        body, name=name,
        out_shape=(jax.ShapeDtypeStruct((Hh, LP, Dh), F32), jax.ShapeDtypeStruct((Hh, LP, 1), F32)),
        grid=(Hh, nb), in_specs=[qspec, kspec, kspec, colspec, rowspec],
        out_specs=(qspec, colspec),
        compiler_params=_params(("parallel", "arbitrary")))(q, k, v, c_col, c_row)


def _attn_bwd(q, k, v, o, do, lse, c_col, c_row, name):
    Hh, LP, Dh = q.shape
    nb = LP // ATT_BLK
    scale = Dh ** -0.5
    nt = (((1,), (1,)), ((), ()))
    tn = (((0,), (0,)), ((), ()))

    def body(q_ref, k_ref, v_ref, o_ref, do_ref, lse_ref, cc_ref, cr_ref,
             dq_ref, dk_ref, dv_ref, dcq_ref, dc_ref, delta_ref):
        j = pl.program_id(1)

        @pl.when(j == 0)
        def _():
            dq_ref[...] = jnp.zeros_like(dq_ref)
            dcq_ref[...] = jnp.zeros_like(dcq_ref)
            dob_all = do_ref[...].astype(BF16).astype(F32)
            delta_ref[...] = jnp.sum(dob_all * o_ref[...], axis=1, keepdims=True)

        kb = k_ref[...]
        vb = v_ref[...]
        ck = cr_ref[j]
        rows0 = lax.broadcasted_iota(jnp.int32, (ATT_BLK, ATT_BLK), 0)
        cols = j * ATT_BLK + lax.broadcasted_iota(jnp.int32, (ATT_BLK, ATT_BLK), 1)

        def step(i, carry):
            dk, dv, dcs = carry
            qs = pl.ds(pl.multiple_of(i * ATT_BLK, ATT_BLK), ATT_BLK)
            qb = q_ref[qs, :]
            dob = do_ref[qs, :].astype(BF16)
            s = lax.dot_general(qb, kb, nt, preferred_element_type=F32) * scale
            s = s + cc_ref[qs, :] - ck
            s = jnp.where(cols <= rows0 + i * ATT_BLK, s, NEG)
            p = jnp.exp(s - lse_ref[qs, :])
            dp = lax.dot_general(dob, vb, nt, preferred_element_type=F32)
            ds = p * (dp - delta_ref[qs, :])
            dsb = ds.astype(BF16)
            dv = dv + lax.dot_general(p.astype(BF16), dob, tn, preferred_element_type=F32)
            dk = dk + lax.dot_general(dsb, qb, tn, preferred_element_type=F32) * scale
            dq_ref[qs, :] += jnp.dot(dsb, kb, preferred_element_type=F32) * scale
            dcq_ref[qs, :] += jnp.sum(ds, axis=1, keepdims=True)
            dcs = dcs - jnp.sum(ds, axis=0, keepdims=True)
            return dk, dv, dcs

        init = (jnp.zeros((ATT_BLK, Dh), F32), jnp.zeros((ATT_BLK, Dh), F32),
                jnp.zeros((1, ATT_BLK), F32))
        dk, dv, dcs = lax.fori_loop(j, nb, step, init)
        dk_ref[...] = dk
        dv_ref[...] = dv
        dc_ref[...] = dcs

    full = pl.BlockSpec((None, LP, Dh), lambda h, j: (h, 0, 0))
    blk = pl.BlockSpec((None, ATT_BLK, Dh), lambda h, j: (h, j, 0))
    col = pl.BlockSpec((None, LP, 1), lambda h, j: (h, 0, 0))
    rowspec = pl.BlockSpec((None, nb, 1, ATT_BLK), lambda h, j: (h, 0, 0, 0))
    return pl.pallas_call(
        body, name=name,
        out_shape=(jax.ShapeDtypeStruct((Hh, LP, Dh), F32), jax.ShapeDtypeStruct((Hh, LP, Dh), F32),
                   jax.ShapeDtypeStruct((Hh, LP, Dh), F32), jax.ShapeDtypeStruct((Hh, LP, 1), F32),
                   jax.ShapeDtypeStruct((Hh, nb, 1, ATT_BLK), F32)),
        grid=(Hh, nb), in_specs=[full, blk, blk, full, full, col, col, rowspec],
        out_specs=(full, blk, blk, col, pl.BlockSpec((None, None, 1, ATT_BLK), lambda h, j: (h, j, 0, 0))),
        scratch_shapes=[pltpu.VMEM((LP, 1), F32)],
        compiler_params=_params(("parallel", "arbitrary")))(q, k, v, o, do, lse, c_col, c_row)


CONV_HALO = 32
A_BLK = 3 * FOX_W // CONV_CH
G_BLK = A_BLK + 1


def _conf_fwd(proj, cw, cb, lg, lb, tm, name):
    LP = proj.shape[0]
    C = CONV_CH
    sub = _sub_rows(tm)
    hpb = tm // CONV_HALO

    def body(a_ref, g_ref, ah_ref, gh_ref, w_ref, cb_ref, lg_ref, lb_ref, u1_ref, u_ref, buf):
        r = pl.program_id(0)
        buf[CONV_HALO:CONV_HALO + tm, :] = a_ref[...] * _sigmoid(g_ref[...])
        buf[0:CONV_HALO, :] = jnp.where(r > 0, ah_ref[...] * _sigmoid(gh_ref[...]), 0.0)
        for s in range(tm // sub):
            for ct in range(C // LANE):
                ln = slice(ct * LANE, (ct + 1) * LANE)
                acc = jnp.broadcast_to(cb_ref[:, ln], (sub, LANE))
                for kk in range(CONV_K):
                    off = CONV_HALO + s * sub - (CONV_K - 1) + kk
                    acc = acc + w_ref[kk:kk + 1, ln] * buf[off:off + sub, ln]
                u1_ref[s * sub:(s + 1) * sub, ln] = acc
        u1 = u1_ref[...]
        mu = jnp.mean(u1, axis=1, keepdims=True)
        xc = u1 - mu
        var = jnp.mean(xc * xc, axis=1, keepdims=True)
        y = xc * lax.rsqrt(var + LN_EPS) * lg_ref[...] + lb_ref[...]
        u_ref[...] = (y * _sigmoid(y)).astype(u_ref.dtype)

    cur = lambda blk: pl.BlockSpec((tm, C), lambda r: (r, blk))
    halo = lambda blk: pl.BlockSpec((CONV_HALO, C), lambda r: (jnp.maximum(r * hpb - 1, 0), blk))
    vec = pl.BlockSpec((1, C), lambda r: (0, 0))
    out = pl.BlockSpec((tm, C), lambda r: (r, 0))
    return pl.pallas_call(
        body, name=name,
        out_shape=(jax.ShapeDtypeStruct((LP, C), F32), jax.ShapeDtypeStruct((LP, C), BF16)),
        grid=(LP // tm,),
        in_specs=[cur(A_BLK), cur(G_BLK), halo(A_BLK), halo(G_BLK),
                  pl.BlockSpec((CONV_HALO, C), lambda r: (0, 0)), vec, vec, vec],
        out_specs=(out, out),
        scratch_shapes=[pltpu.VMEM((CONV_HALO + tm, C), F32)],
        compiler_params=_params(("parallel",)))(proj, proj, proj, proj, cw, cb, lg, lb)


def _conf_bwd(proj, u1, dcat, cw, lg, lb, tm, name):
    LP = proj.shape[0]
    C = CONV_CH
    sub = _sub_rows(tm)
    hpb = tm // CONV_HALO
    nblk = LP // tm
    last_halo = LP // CONV_HALO - 1

    def body(a_ref, g_ref, ah_ref, gh_ref, u1_ref, u1n_ref, du_ref, dun_ref, w_ref, lg_ref, lb_ref,
             dadg_ref, dw_ref, dcb_ref, dlg_ref, dlb_ref, ubuf, dbuf, du0):
        r = pl.program_id(0)
        lgv = lg_ref[...]
        lbv = lb_ref[...]

        def ln_silu_bwd(u1v, duv):
            mu = jnp.mean(u1v, axis=1, keepdims=True)
            xc = u1v - mu
            rstd = lax.rsqrt(jnp.mean(xc * xc, axis=1, keepdims=True) + LN_EPS)
            xhat = xc * rstd
            y = xhat * lgv + lbv
            sg = _sigmoid(y)
            dy = duv * (sg * (1.0 + y * (1.0 - sg)))
            dxh = dy * lgv
            du1 = rstd * (dxh - jnp.mean(dxh, axis=1, keepdims=True)
                          - xhat * jnp.mean(dxh * xhat, axis=1, keepdims=True))
            return du1, dy, xhat

        @pl.when(r == 0)
        def _():
            dw_ref[...] = jnp.zeros_like(dw_ref)
            dcb_ref[...] = jnp.zeros_like(dcb_ref)
            dlg_ref[...] = jnp.zeros_like(dlg_ref)
            dlb_ref[...] = jnp.zeros_like(dlb_ref)

        du1, dy, xhat = ln_silu_bwd(u1_ref[...], du_ref[...])
        dlg_ref[...] += jnp.sum(dy * xhat, axis=0, keepdims=True)
        dlb_ref[...] += jnp.sum(dy, axis=0, keepdims=True)
        dcb_ref[...] += jnp.sum(du1, axis=0, keepdims=True)
        dbuf[0:tm, :] = du1
        du1n, _, _ = ln_silu_bwd(u1n_ref[...], dun_ref[...])
        dbuf[tm:tm + CONV_HALO, :] = jnp.where(r < nblk - 1, du1n, 0.0)
        ubuf[CONV_HALO:CONV_HALO + tm, :] = a_ref[...] * _sigmoid(g_ref[...])
        ubuf[0:CONV_HALO, :] = jnp.where(r > 0, ah_ref[...] * _sigmoid(gh_ref[...]), 0.0)

        for ct in range(C // LANE):
            ln = slice(ct * LANE, (ct + 1) * LANE)
            for s in range(tm // sub):
                d_here = dbuf[s * sub:(s + 1) * sub, ln]
                acc = jnp.zeros((sub, LANE), F32)
                for kk in range(CONV_K):
                    fo = s * sub + (CONV_K - 1) - kk
                    acc = acc + w_ref[kk:kk + 1, ln] * dbuf[fo:fo + sub, ln]
                    bo = CONV_HALO + s * sub - (CONV_K - 1) + kk
                    dw_ref[kk:kk + 1, ln] += jnp.sum(d_here * ubuf[bo:bo + sub, ln], axis=0, keepdims=True)
                du0[s * sub:(s + 1) * sub, ln] = acc
        a = a_ref[...]
        sg = _sigmoid(g_ref[...])
        d0 = du0[...]
        dadg_ref[:, 0:C] = (d0 * sg).astype(dadg_ref.dtype)
        dadg_ref[:, C:2 * C] = (d0 * a * sg * (1.0 - sg)).astype(dadg_ref.dtype)

    cur = lambda blk: pl.BlockSpec((tm, C), lambda r: (r, blk))
    prev = lambda blk: pl.BlockSpec((CONV_HALO, C), lambda r: (jnp.maximum(r * hpb - 1, 0), blk))
    nxt = lambda blk: pl.BlockSpec((CONV_HALO, C), lambda r: (jnp.minimum((r + 1) * hpb, last_halo), blk))
    vec = pl.BlockSpec((1, C), lambda r: (0, 0))
    wspec = pl.BlockSpec((CONV_HALO, C), lambda r: (0, 0))
    return pl.pallas_call(
        body, name=name,
        out_shape=(jax.ShapeDtypeStruct((LP, 2 * C), BF16), jax.ShapeDtypeStruct((CONV_HALO, C), F32),
                   jax.ShapeDtypeStruct((1, C), F32), jax.ShapeDtypeStruct((1, C), F32),
                   jax.ShapeDtypeStruct((1, C), F32)),
        grid=(nblk,),
        in_specs=[cur(A_BLK), cur(G_BLK), prev(A_BLK), prev(G_BLK), cur(0), nxt(0), cur(1), nxt(1),
                  wspec, vec, vec],
        out_specs=(pl.BlockSpec((tm, 2 * C), lambda r: (r, 0)), wspec, vec, vec, vec),
        scratch_shapes=[pltpu.VMEM((CONV_HALO + tm, C), F32), pltpu.VMEM((tm + CONV_HALO, C), F32),
                        pltpu.VMEM((tm, C), F32)],
        compiler_params=_params(("arbitrary",)))(proj, proj, proj, proj, u1, u1, dcat, dcat, cw, lg, lb)


FFN_HALO = 8
FFN_TC = 256
FFN_K = 3


def _ffn_conv(buf, w_ref, b_ref, s, sub, ln):
    acc = jnp.broadcast_to(b_ref[:, ln], (sub, LANE))
    for kk in range(FFN_K):
        off = FFN_HALO + s * sub - (FFN_K - 1) + kk
        acc = acc + w_ref[kk:kk + 1, ln] * buf[off:off + sub, ln]
    return acc


def _ffn_act_fwd(up, w, b, tm, name):
    LP, F = up.shape[0], up.shape[1] // 2
    upg = upv = up
    nct = F // FFN_TC
    sub = _sub_rows(tm)
    hpb = tm // FFN_HALO

    def body(g_ref, v_ref, gh_ref, vh_ref, wg_ref, wv_ref, bg_ref, bv_ref, act_ref, gbuf, vbuf):
        r = pl.program_id(1)
        gbuf[FFN_HALO:FFN_HALO + tm, :] = g_ref[...]
        vbuf[FFN_HALO:FFN_HALO + tm, :] = v_ref[...]
        gbuf[0:FFN_HALO, :] = jnp.where(r > 0, gh_ref[...], 0.0)
        vbuf[0:FFN_HALO, :] = jnp.where(r > 0, vh_ref[...], 0.0)
        for s in range(tm // sub):
            for ct in range(FFN_TC // LANE):
                ln = slice(ct * LANE, (ct + 1) * LANE)
                gc = _ffn_conv(gbuf, wg_ref, bg_ref, s, sub, ln)
                vc = _ffn_conv(vbuf, wv_ref, bv_ref, s, sub, ln)
                act_ref[s * sub:(s + 1) * sub, ln] = (gc * _sigmoid(gc) * vc).astype(act_ref.dtype)

    cur = pl.BlockSpec((tm, FFN_TC), lambda c, r: (r, c))
    halo = pl.BlockSpec((FFN_HALO, FFN_TC), lambda c, r: (jnp.maximum(r * hpb - 1, 0), c))
    wg = pl.BlockSpec((8, FFN_TC), lambda c, r: (0, c))
    wv = pl.BlockSpec((8, FFN_TC), lambda c, r: (0, nct + c))
    bg = pl.BlockSpec((1, FFN_TC), lambda c, r: (0, c))
    bv = pl.BlockSpec((1, FFN_TC), lambda c, r: (0, nct + c))
    curv = pl.BlockSpec((tm, FFN_TC), lambda c, r: (r, nct + c))
    halov = pl.BlockSpec((FFN_HALO, FFN_TC), lambda c, r: (jnp.maximum(r * hpb - 1, 0), nct + c))
    return pl.pallas_call(
        body, name=name, out_shape=jax.ShapeDtypeStruct((LP, F), BF16), grid=(nct, LP // tm),
        in_specs=[cur, curv, halo, halov, wg, wv, bg, bv], out_specs=cur,
        scratch_shapes=[pltpu.VMEM((FFN_HALO + tm, FFN_TC), F32)] * 2,
        compiler_params=_params(("parallel", "parallel")))(upg, upv, upg, upv, w, w, b, b)


def _ffn_act_bwd(up, dact, w, b, tm, name):
    LP, F = up.shape[0], up.shape[1] // 2
    upg = upv = up
    nct = F // FFN_TC
    sub = _sub_rows(tm)
    hpb = tm // FFN_HALO
    nblk = LP // tm
    last_halo = LP // FFN_HALO - 1
    TB = tm + 2 * FFN_HALO

    def body(g_ref, v_ref, gp_ref, vp_ref, gn_ref, vn_ref, da_ref, dan_ref,
             wg_ref, wv_ref, bg_ref, bv_ref,
             dup_ref, dwg_ref, dwv_ref, dbg_ref, dbv_ref, gbuf, vbuf, dgb, dvb):
        r = pl.program_id(1)
        dg_ref = dup_ref.at[0]
        dv_ref = dup_ref.at[1]
        first = r == 0
        last = r == nblk - 1

        @pl.when(first)
        def _():
            dwg_ref[...] = jnp.zeros_like(dwg_ref)
            dwv_ref[...] = jnp.zeros_like(dwv_ref)
            dbg_ref[...] = jnp.zeros_like(dbg_ref)
            dbv_ref[...] = jnp.zeros_like(dbv_ref)

        for buf, c_ref, p_ref, n_ref in ((gbuf, g_ref, gp_ref, gn_ref), (vbuf, v_ref, vp_ref, vn_ref)):
            buf[0:FFN_HALO, :] = jnp.where(first, 0.0, p_ref[...])
            buf[FFN_HALO:FFN_HALO + tm, :] = c_ref[...]
            buf[FFN_HALO + tm:TB, :] = jnp.where(last, 0.0, n_ref[...])

        def dconv(s0, nrows, ln, dact_v):
            gc = jnp.broadcast_to(bg_ref[:, ln], (nrows, LANE))
            vc = jnp.broadcast_to(bv_ref[:, ln], (nrows, LANE))
            for kk in range(FFN_K):
                off = s0 - (FFN_K - 1) + kk
                gc = gc + wg_ref[kk:kk + 1, ln] * gbuf[off:off + nrows, ln]
                vc = vc + wv_ref[kk:kk + 1, ln] * vbuf[off:off + nrows, ln]
            sg = _sigmoid(gc)
            return dact_v * vc * (sg * (1.0 + gc * (1.0 - sg))), dact_v * (gc * sg)

        for ct in range(FFN_TC // LANE):
            ln = slice(ct * LANE, (ct + 1) * LANE)
            for s in range(tm // sub):
                dgc, dvc = dconv(FFN_HALO + s * sub, sub, ln, da_ref[s * sub:(s + 1) * sub, ln])
                dgb[s * sub:(s + 1) * sub, ln] = dgc
                dvb[s * sub:(s + 1) * sub, ln] = dvc
            dgc, dvc = dconv(FFN_HALO + tm, FFN_HALO, ln, jnp.where(last, 0.0, dan_ref[:, ln]))
            dgb[tm:tm + FFN_HALO, ln] = dgc
            dvb[tm:tm + FFN_HALO, ln] = dvc
            for dbuf, ubuf, w_ref, dw_ref, db_ref, dout in (
                    (dgb, gbuf, wg_ref, dwg_ref, dbg_ref, dg_ref), (dvb, vbuf, wv_ref, dwv_ref, dbv_ref, dv_ref)):
                for s in range(tm // sub):
                    d_here = dbuf[s * sub:(s + 1) * sub, ln]
                    acc = jnp.zeros((sub, LANE), F32)
                    for kk in range(FFN_K):
                        fo = s * sub + (FFN_K - 1) - kk
                        acc = acc + w_ref[kk:kk + 1, ln] * dbuf[fo:fo + sub, ln]
                        bo = FFN_HALO + s * sub - (FFN_K - 1) + kk
                        dw_ref[kk:kk + 1, ln] += jnp.sum(d_here * ubuf[bo:bo + sub, ln], axis=0, keepdims=True)
                    db_ref[:, ln] += jnp.sum(d_here, axis=0, keepdims=True)
                    dout[s * sub:(s + 1) * sub, ln] = acc.astype(dout.dtype)

    cur = pl.BlockSpec((tm, FFN_TC), lambda c, r: (r, c))
    prev = pl.BlockSpec((FFN_HALO, FFN_TC), lambda c, r: (jnp.maximum(r * hpb - 1, 0), c))
    nxt = pl.BlockSpec((FFN_HALO, FFN_TC), lambda c, r: (jnp.minimum((r + 1) * hpb, last_halo), c))
    wg = pl.BlockSpec((8, FFN_TC), lambda c, r: (0, c))
    wv = pl.BlockSpec((8, FFN_TC), lambda c, r: (0, nct + c))
    bg = pl.BlockSpec((1, FFN_TC), lambda c, r: (0, c))
    bv = pl.BlockSpec((1, FFN_TC), lambda c, r: (0, nct + c))
    curv = pl.BlockSpec((tm, FFN_TC), lambda c, r: (r, nct + c))
    prevv = pl.BlockSpec((FFN_HALO, FFN_TC), lambda c, r: (jnp.maximum(r * hpb - 1, 0), nct + c))
    nxtv = pl.BlockSpec((FFN_HALO, FFN_TC), lambda c, r: (jnp.minimum((r + 1) * hpb, last_halo), nct + c))
    dup, dwg, dwv, dbg, dbv = pl.pallas_call(
        body, name=name,
        out_shape=(jax.ShapeDtypeStruct((2, LP, F), BF16),
                   jax.ShapeDtypeStruct((8, F), F32), jax.ShapeDtypeStruct((8, F), F32),
                   jax.ShapeDtypeStruct((1, F), F32), jax.ShapeDtypeStruct((1, F), F32)),
        grid=(nct, nblk),
        in_specs=[cur, curv, prev, prevv, nxt, nxtv, cur, nxt, wg, wv, bg, bv],
        out_specs=(pl.BlockSpec((2, tm, FFN_TC), lambda c, r: (0, r, c)),
                   pl.BlockSpec((8, FFN_TC), lambda c, r: (0, c)),
                   pl.BlockSpec((8, FFN_TC), lambda c, r: (0, c)),
                   pl.BlockSpec((1, FFN_TC), lambda c, r: (0, c)),
                   pl.BlockSpec((1, FFN_TC), lambda c, r: (0, c))),
        scratch_shapes=[pltpu.VMEM((TB, FFN_TC), F32), pltpu.VMEM((TB, FFN_TC), F32),
                        pltpu.VMEM((tm + FFN_HALO, FFN_TC), F32), pltpu.VMEM((tm + FFN_HALO, FFN_TC), F32)],
        compiler_params=_params(("parallel", "arbitrary")))(
            upg, upv, upg, upv, upg, upv, dact, dact, w, w, b, b)
    return dup, jnp.concatenate([dwg, dwv], axis=1), jnp.concatenate([dbg, dbv], axis=1)


POOL_HALO = 16


def _pool_fwd(h, g, pw, pb, ps, tm, name):
    LP, Dm = h.shape
    sub = _sub_rows(tm)
    hpb = tm // POOL_HALO

    def body(h_ref, hh_ref, g_ref, pw_ref, pb_ref, ps_ref, o_ref, d_ref, buf):
        r = pl.program_id(0)
        gg = g_ref[...]

        def norm(x):
            return x * lax.rsqrt(jnp.mean(x * x, axis=1, keepdims=True) + RMS_EPS) * gg

        x = h_ref[...]
        buf[POOL_HALO:POOL_HALO + tm, :] = norm(x)
        buf[0:POOL_HALO, :] = jnp.where(r > 0, norm(hh_ref[...]), 0.0)
        for gi, w in enumerate(POOL_WINDOWS):
            ln = slice(gi * POOL_G, (gi + 1) * POOL_G)
            for s in range(tm // sub):
                base = POOL_HALO + s * sub
                acc = buf[base:base + sub, ln]
                for jj in range(1, w):
                    acc = acc + buf[base - jj:base - jj + sub, ln]
                t = r * tm + s * sub + lax.broadcasted_iota(jnp.int32, (sub, 1), 0)
                cnt = jnp.minimum(t + 1, w).astype(F32)
                d_ref[s * sub:(s + 1) * sub, ln] = (acc / cnt - buf[base:base + sub, ln]).astype(d_ref.dtype)
            y = jnp.dot(d_ref[:, ln], pw_ref[gi], preferred_element_type=F32) + pb_ref[:, ln]
            o_ref[:, ln] = x[:, ln] + y * ps_ref[:, ln]

    row = pl.BlockSpec((tm, Dm), lambda r: (r, 0))
    halo = pl.BlockSpec((POOL_HALO, Dm), lambda r: (jnp.maximum(r * hpb - 1, 0), 0))
    vec = pl.BlockSpec((1, Dm), lambda r: (0, 0))
    wsp = pl.BlockSpec((len(POOL_WINDOWS), POOL_G, POOL_G), lambda r: (0, 0, 0))
    return pl.pallas_call(
        body, name=name,
        out_shape=(jax.ShapeDtypeStruct((LP, Dm), F32), jax.ShapeDtypeStruct((LP, Dm), BF16)),
        grid=(LP // tm,), in_specs=[row, halo, vec, wsp, vec, vec], out_specs=(row, row),
        scratch_shapes=[pltpu.VMEM((POOL_HALO + tm, Dm), F32)],
        compiler_params=_params(("parallel",)))(h, h, g, pw, pb, ps)


def _pool_bwd(h, g, d, pw, pb, ps, dh_out, tm, name):
    LP, Dm = h.shape
    sub = _sub_rows(tm)
    hpb = tm // POOL_HALO
    nblk = LP // tm
    last_halo = LP // POOL_HALO - 1
    nt = (((1,), (1,)), ((), ()))
    tn = (((0,), (0,)), ((), ()))

    def body(h_ref, g_ref, d_ref, pw_ref, pb_ref, ps_ref, do_ref, don_ref,
             dh_ref, dpw_ref, dpb_ref, dps_ref, dg_ref, ebuf, ddb, dnb):
        r = pl.program_id(0)

        @pl.when(r == 0)
        def _():
            dpw_ref[...] = jnp.zeros_like(dpw_ref)
            dpb_ref[...] = jnp.zeros_like(dpb_ref)
            dps_ref[...] = jnp.zeros_like(dps_ref)
            dg_ref[...] = jnp.zeros_like(dg_ref)

        for gi, w in enumerate(POOL_WINDOWS):
            ln = slice(gi * POOL_G, (gi + 1) * POOL_G)
            wg = pw_ref[gi]
            dog = do_ref[:, ln]
            dg_b = d_ref[:, ln]
            y_pre = jnp.dot(dg_b, wg, preferred_element_type=F32) + pb_ref[:, ln]
            dps_ref[:, ln] += jnp.sum(dog * y_pre, axis=0, keepdims=True)
            dy = dog * ps_ref[:, ln]
            dpb_ref[:, ln] += jnp.sum(dy, axis=0, keepdims=True)
            dyb = dy.astype(BF16)
            dpw_ref[gi] += lax.dot_general(dg_b, dyb, tn, preferred_element_type=F32)
            dd = lax.dot_general(dyb, wg, nt, preferred_element_type=F32)
            ddb[:, ln] = dd
            t = r * tm + lax.broadcasted_iota(jnp.int32, (tm, 1), 0)
            ebuf[0:tm, ln] = dd / jnp.minimum(t + 1, w).astype(F32)
            dyn = (don_ref[:, ln] * ps_ref[:, ln]).astype(BF16)
            ddn = lax.dot_general(dyn, wg, nt, preferred_element_type=F32)
            tn_ = (r + 1) * tm + lax.broadcasted_iota(jnp.int32, (POOL_HALO, 1), 0)
            ebuf[tm:tm + POOL_HALO, ln] = jnp.where(r < nblk - 1, ddn / jnp.minimum(tn_ + 1, w).astype(F32), 0.0)
            for s in range(tm // sub):
                acc = ebuf[s * sub:(s + 1) * sub, ln]
                for jj in range(1, w):
                    acc = acc + ebuf[s * sub + jj:s * sub + jj + sub, ln]
                dnb[s * sub:(s + 1) * sub, ln] = acc - ddb[s * sub:(s + 1) * sub, ln]
        x = h_ref[...]
        rr = lax.rsqrt(jnp.mean(x * x, axis=1, keepdims=True) + RMS_EPS)
        xhat = x * rr
        dn = dnb[...]
        dxh = dn * g_ref[...]
        dh_ref[...] = do_ref[...] + rr * (dxh - xhat * jnp.mean(dxh * xhat, axis=1, keepdims=True))
        dg_ref[...] += jnp.sum(dn * xhat, axis=0, keepdims=True)

    row = pl.BlockSpec((tm, Dm), lambda r: (r, 0))
    nxt = pl.BlockSpec((POOL_HALO, Dm), lambda r: (jnp.minimum((r + 1) * hpb, last_halo), 0))
    vec = pl.BlockSpec((1, Dm), lambda r: (0, 0))
    wsp = pl.BlockSpec((len(POOL_WINDOWS), POOL_G, POOL_G), lambda r: (0, 0, 0))
    return pl.pallas_call(
        body, name=name,
        out_shape=(jax.ShapeDtypeStruct((LP, Dm), F32),
                   jax.ShapeDtypeStruct((len(POOL_WINDOWS), POOL_G, POOL_G), F32),
                   jax.ShapeDtypeStruct((1, Dm), F32), jax.ShapeDtypeStruct((1, Dm), F32),
                   jax.ShapeDtypeStruct((1, Dm), F32)),
        grid=(nblk,), in_specs=[row, vec, row, wsp, vec, vec, row, nxt],
        out_specs=(row, wsp, vec, vec, vec),
        scratch_shapes=[pltpu.VMEM((tm + POOL_HALO, Dm), F32), pltpu.VMEM((tm, Dm), F32),
                        pltpu.VMEM((tm, Dm), F32)],
        compiler_params=_params(("arbitrary",)))(h, g, d, pw, pb, ps, dh_out, dh_out)


def _adamw(w, g, m, v, name):
    shape = w.shape
    cols = shape[-1]
    rows = int(np.prod(shape[:-1])) if len(shape) > 1 else 1
    w2, g2, m2, v2 = (t.reshape(rows, cols) for t in (w, g, m, v))
    tr = rows
    for cand in (256, 128, 64, 32, 16, 8):
        if rows % cand == 0 and rows > cand:
            tr = cand
            break
    c1 = float(1.0 - ADAM_B1 ** ADAM_STEP)
    c2 = float(1.0 - ADAM_B2 ** ADAM_STEP)

    def body(w_ref, g_ref, m_ref, v_ref, d_ref, mo_ref, vo_ref):
        gg = g_ref[...]
        mn = ADAM_B1 * m_ref[...] + (1.0 - ADAM_B1) * gg
        vn = ADAM_B2 * v_ref[...] + (1.0 - ADAM_B2) * (gg * gg)
        m_hat = mn / c1
        v_hat = vn / c2
        d_ref[...] = -ADAM_LR * (m_hat / (jnp.sqrt(v_hat) + ADAM_EPS) + ADAM_WD * w_ref[...])
        mo_ref[...] = mn
        vo_ref[...] = vn

    spec = pl.BlockSpec((tr, cols), lambda i: (i, 0))
    sds = jax.ShapeDtypeStruct((rows, cols), F32)
    d2, mo, vo = pl.pallas_call(
        body, name=name, out_shape=(sds, sds, sds), grid=(rows // tr,),
        in_specs=[spec] * 4, out_specs=(spec,) * 3,
        compiler_params=_params(("parallel",)))(w2, g2, m2, v2)
    return d2.reshape(shape), mo.reshape(shape), vo.reshape(shape)


def _row_tiles(LP):
    tm = LP // 4
    assert LP % 4 == 0 and tm % CONV_HALO == 0 and LP % ATT_BLK == 0, LP
    return tm, LP // 2


def _heads(t, LP):
    return t.reshape(LP, HEADS, HEAD_DIM).transpose(1, 0, 2)


def _unheads(t, LP):
    return t.transpose(1, 0, 2).reshape(LP, FOX_W)


def _ffn_fwd(h, gain, wug, wuv, cw, cb, wd, tm, tmm, tag):
    n = _rms_fwd(h, gain, BF16, tm, f"ffn_norm_{tag}")
    upg = _mm(n, wug, "nn", F32, tmm, 256, f"ffn_up_gate_{tag}")
    upv = _mm(n, wuv, "nn", F32, tmm, 256, f"ffn_up_val_{tag}")
    act = _ffn_act_fwd(upg, upv, cw, cb, tm, f"ffn_act_{tag}")
    out = _mm(act, wd, "nn", F32, tmm, 512, f"ffn_down_{tag}", add=h)
    return out, (n, upg, upv, act)


def _ffn_bwd(h, gain, wug, wuv, cw, cb, wd, saved, dout, tm, tmm, tag):
    n, upg, upv, act = saved
    dact = _mm(dout, wd, "nt", F32, tmm, 256, f"ffn_dact_{tag}")
    dwd = _mm(act, dout, "tn", F32, 256, 512, f"ffn_dwdown_{tag}")
    dupg, dupv, dcw, dcb = _ffn_act_bwd(upg, upv, dact, cw, cb, tm, f"ffn_act_bwd_{tag}")
    dn = _mm(dupg, wug, "nt", F32, tm, 512, f"ffn_dn_gate_{tag}")
    dn = _mm(dupv, wuv, "nt", F32, tm, 512, f"ffn_dn_val_{tag}", add=dn)
    dwug = _mm(n, dupg, "tn", F32, 512, 256, f"ffn_dwup_gate_{tag}")
    dwuv = _mm(n, dupv, "tn", F32, 512, 256, f"ffn_dwup_val_{tag}")
    dh, dgain = _rms_bwd(h, gain, dn, dout, tm, f"ffn_norm_bwd_{tag}")
    return dh, dict(gain=dgain, wug=dwug, wuv=dwuv, cw=dcw[:FFN_K], cb=dcb, wd=dwd)


def _local_step(h0, tgt, W, n_real):
    LP = h0.shape[0]
    tm, tmm = _row_tiles(LP)
    nb = LP // ATT_BLK
    G = {}

    n0 = _rms_fwd(h0, W["mix_norm_even"], BF16, tm, "mix_norm_even")
    proj = _mm(n0, W["w_in_p"], "nn", F32, tmm, 384, "in_proj")
    c = _fgate_fwd(proj, W["b_f_p"], "forget_gate")
    cT = c[:, :HEADS].T
    c_col = cT[:, :, None]
    c_row = cT.reshape(HEADS, nb, 1, ATT_BLK)
    qkv = proj[:, :3 * FOX_W].astype(BF16)
    q, k, v = (_heads(qkv[:, i * FOX_W:(i + 1) * FOX_W], LP) for i in range(3))
    o, lse = _attn_fwd(q, k, v, c_col, c_row, "fox_attention")
    u1, u = _conf_fwd(proj, W["conv_w_p"], W["conv_b"], W["ln_g"], W["ln_b"], tm, "conformer")
    cat = jnp.concatenate([_unheads(o, LP).astype(BF16), u], axis=1)
    h1 = _mm(cat, W["w_out"], "nn", F32, tmm, 512, "out_proj", add=h0)
    h2, ffn0 = _ffn_fwd(h1, W["ffn_norm"][0:1], W["w_up_g"][0], W["w_up_v"][0], W["ffn_conv_w_p"][0],
                        W["ffn_conv_b"][0:1], W["w_down"][0], tm, tmm, "0")
    h3, dpool = _pool_fwd(h2, W["mix_norm_odd"], W["pool_w"], W["pool_b"], W["pool_scale"], tm, "pool_mixer")
    h4, ffn1 = _ffn_fwd(h3, W["ffn_norm"][1:2], W["w_up_g"][1], W["w_up_v"][1], W["ffn_conv_w_p"][1],
                        W["ffn_conv_b"][1:2], W["w_down"][1], tm, tmm, "1")
    loss, dh4, G["final_norm"] = _loss_head(h4, W["final_norm"], tgt, n_real, tm, "loss_head")

    dh3, g1 = _ffn_bwd(h3, W["ffn_norm"][1:2], W["w_up_g"][1], W["w_up_v"][1], W["ffn_conv_w_p"][1],
                       W["ffn_conv_b"][1:2], W["w_down"][1], ffn1, dh4, tm, tmm, "1")
    dh2, G["pool_w"], G["pool_b"], G["pool_scale"], G["mix_norm_odd"] = _pool_bwd(
        h2, W["mix_norm_odd"], dpool, W["pool_w"], W["pool_b"], W["pool_scale"], dh3, tm, "pool_mixer_bwd")
    dh1, g0 = _ffn_bwd(h1, W["ffn_norm"][0:1], W["w_up_g"][0], W["w_up_v"][0], W["ffn_conv_w_p"][0],
                       W["ffn_conv_b"][0:1], W["w_down"][0], ffn0, dh2, tm, tmm, "0")
    for key in ("gain", "wug", "wuv", "cw", "cb", "wd"):
        G["ffn_" + key] = (g0[key], g1[key])

    dcat = _mm(dh1, W["w_out"], "nt", F32, tmm, 512, "out_proj_dx")
    G["w_out"] = _mm(cat, dh1, "tn", F32, 512, 512, "out_proj_dw")
    dadg, dcw, G["conv_b"], G["ln_g"], G["ln_b"] = _conf_bwd(
        proj, u1, dcat, W["conv_w_p"], W["ln_g"], W["ln_b"], tm, "conformer_bwd")
    G["conv_w"] = dcw[:CONV_K]
    do = _heads(dcat[:, :FOX_W], LP)
    dq, dk, dv, dcq, dck = _attn_bwd(q, k, v, o, do, lse, c_col, c_row, "fox_attention_bwd")
    dc = jnp.pad((dcq.reshape(HEADS, LP) + dck.reshape(HEADS, LP)).T, ((0, 0), (0, LANE - HEADS)))
    dfl, dbf = _fgate_bwd(proj, W["b_f_p"], dc, "forget_gate_bwd")
    G["b_f"] = dbf[:, :HEADS]
    dproj = jnp.concatenate([_unheads(t, LP).astype(BF16) for t in (dq, dk, dv)] + [dadg, dfl], axis=1)
    dn0 = _mm(dproj, W["w_in_p"], "nt", F32, tmm, 512, "in_proj_dx")
    G["w_in_p"] = _mm(n0, dproj, "tn", F32, 512, 384, "in_proj_dw")
    dh0, G["mix_norm_even"] = _rms_bwd(h0, W["mix_norm_even"], dn0, dh1, tm, "mix_norm_even_bwd")
    return loss, dh0, G


def _compute_layout(P):
    w_in = P["w_in"].reshape(D_MODEL, IN_COLS)
    qkv, f, ag = w_in[:, :3 * FOX_W], w_in[:, 3 * FOX_W:3 * FOX_W + HEADS], w_in[:, 3 * FOX_W + HEADS:]
    w_in_p = jnp.concatenate([qkv, ag, f, jnp.zeros((D_MODEL, LANE - HEADS), w_in.dtype)], axis=1).astype(BF16)
    w_up = P["w_up"].astype(BF16)
    return dict(
        mix_norm_even=P["mix_norm_even"].reshape(1, D_MODEL).astype(F32),
        w_in_p=w_in_p,
        b_f_p=jnp.pad(P["b_f"].reshape(1, HEADS).astype(F32), ((0, 0), (0, LANE - HEADS))),
        conv_w_p=jnp.pad(P["conv_w"].reshape(CONV_K, CONV_CH).astype(F32), ((0, CONV_HALO - CONV_K), (0, 0))),
        conv_b=P["conv_b"].reshape(1, CONV_CH).astype(F32),
        ln_g=P["ln_g"].reshape(1, CONV_CH).astype(F32),
        ln_b=P["ln_b"].reshape(1, CONV_CH).astype(F32),
        w_out=P["w_out"].reshape(D_MODEL, D_MODEL).astype(BF16),
        mix_norm_odd=P["mix_norm_odd"].reshape(1, D_MODEL).astype(F32),
        pool_w=P["pool_w"].reshape(len(POOL_WINDOWS), POOL_G, POOL_G).astype(BF16),
        pool_b=P["pool_b"].reshape(1, D_MODEL).astype(F32),
        pool_scale=P["pool_scale"].reshape(1, D_MODEL).astype(F32),
        ffn_norm=P["ffn_norm"].astype(F32),
        w_up_g=w_up[:, :, :D_FF],
        w_up_v=w_up[:, :, D_FF:],
        ffn_conv_w_p=jnp.pad(P["ffn_conv_w"].astype(F32), ((0, 0), (0, 8 - FFN_K), (0, 0))),
        ffn_conv_b=P["ffn_conv_b"].astype(F32),
        w_down=P["w_down"].astype(BF16),
        final_norm=P["final_norm"].reshape(1, D_MODEL).astype(F32),
    )


def _reference_layout(G, dh0):
    gp = G["w_in_p"]
    g_w_in = jnp.concatenate([gp[:, :3 * FOX_W], gp[:, 3 * FOX_W + 2 * CONV_CH:3 * FOX_W + 2 * CONV_CH + HEADS],
                              gp[:, 3 * FOX_W:3 * FOX_W + 2 * CONV_CH]], axis=1)
    return dict(
        meta_tokens=dh0[:N_META],
        mix_norm_even=G["mix_norm_even"],
        w_in=g_w_in[None],
        b_f=G["b_f"],
        conv_w=G["conv_w"][None],
        conv_b=G["conv_b"],
        ln_g=G["ln_g"],
        ln_b=G["ln_b"],
        w_out=G["w_out"][None],
        mix_norm_odd=G["mix_norm_odd"],
        pool_w=G["pool_w"][None],
        pool_b=G["pool_b"].reshape(1, len(POOL_WINDOWS), POOL_G),
        pool_scale=G["pool_scale"],
        ffn_norm=jnp.concatenate(G["ffn_gain"], axis=0),
        w_up=jnp.stack([jnp.concatenate([g, v], axis=1) for g, v in zip(G["ffn_wug"], G["ffn_wuv"])]),
        ffn_conv_w=jnp.stack(G["ffn_cw"]),
        ffn_conv_b=jnp.concatenate(G["ffn_cb"], axis=0),
        w_down=jnp.stack(G["ffn_wd"]),
        final_norm=G["final_norm"].reshape(D_MODEL),
    )


MESH = pl.DeviceIdType.MESH
ANY = pl.BlockSpec(memory_space=pl.ANY)
PACK_COLS = 1024


def _coords():
    return lax.axis_index("x"), lax.axis_index("y"), lax.axis_index("c")


def _other_chips(x, y):
    return [(1 - x, y), (x, 1 - y), (1 - x, 1 - y)]


def _allgather_chips(pack):
    R, C = pack.shape
    R2 = R // 2

    def body(x_ref, o_ref, send_sems, recv_sems, local_sem):
        x, y, c = _coords()
        sibling = (x, y, 1 - c)
        chips = _other_chips(x, y)

        def slot(px, py, half):
            return o_ref.at[2 * px + py, pl.ds(half * R2, R2), :]

        def copy(k, src, dst, to):
            return pltpu.make_async_remote_copy(src_ref=src, dst_ref=dst, send_sem=send_sems.at[k],
                                                recv_sem=recv_sems.at[k], device_id=to, device_id_type=MESH)

        mine = pltpu.make_async_copy(x_ref, o_ref.at[2 * x + y], local_sem)
        mine.start()
        my_half = x_ref.at[pl.ds(c * R2, R2), :]
        first = [copy(j, my_half, slot(x, y, c), (*chip, c)) for j, chip in enumerate(chips)]
        for cp in first:
            cp.start()
        passed = [copy(3 + j, slot(*chip, c), slot(*chip, c), sibling) for j, chip in enumerate(chips)]
        for j, chip in enumerate(chips):
            copy(j, my_half, slot(*chip, c), sibling).wait_recv()
            passed[j].start()
        for j, chip in enumerate(chips):
            copy(3 + j, my_half, slot(*chip, 1 - c), sibling).wait_recv()
        for cp in first + passed:
            cp.wait_send()
        mine.wait()

    return pl.pallas_call(
        body, name="allgather_weights", out_shape=jax.ShapeDtypeStruct((N_CHIPS, R, C), pack.dtype),
        in_specs=[ANY], out_specs=ANY,
        scratch_shapes=[pltpu.SemaphoreType.DMA((6,)), pltpu.SemaphoreType.DMA((6,)), pltpu.SemaphoreType.DMA],
    )(pack)


def _pair_exchange(G):
    n, R, C = G.shape
    R2 = R // 2

    def body(g_ref, o_ref, send_sem, recv_sem):
        x, y, c = _coords()
        src = g_ref.at[pl.ds(0, n), pl.ds((1 - c) * R2, R2), :]
        cp = pltpu.make_async_remote_copy(src_ref=src, dst_ref=o_ref, send_sem=send_sem, recv_sem=recv_sem,
                                          device_id=(x, y, 1 - c), device_id_type=MESH)
        cp.start()
        cp.wait()

    return pl.pallas_call(
        body, name="grad_pair_exchange", out_shape=jax.ShapeDtypeStruct((n, R2, C), G.dtype),
        in_specs=[ANY], out_specs=ANY,
        scratch_shapes=[pltpu.SemaphoreType.DMA, pltpu.SemaphoreType.DMA],
    )(G)


def _row_tile(rows, align, cap):
    best = None
    for t in range(align, min(rows, cap) + 1, align):
        if rows % t == 0:
            best = t
    assert best is not None, (rows, align, cap)
    return best


def _pair_sum(G, recv):
    n, R, C = G.shape
    R2 = R // 2
    tr = _row_tile(R2, 16, 704)
    nrb = R2 // tr
    half = lax.axis_index("c").astype(jnp.int32).reshape(1)

    def body(c_ref, g_ref, r_ref, o_ref):
        o_ref[...] = (g_ref[...] + r_ref[...]).astype(o_ref.dtype)

    return pl.pallas_call(
        body, name="grad_pair_sum", out_shape=jax.ShapeDtypeStruct((n, R2, C), BF16),
        grid_spec=pltpu.PrefetchScalarGridSpec(
            num_scalar_prefetch=1, grid=(n, nrb),
            in_specs=[pl.BlockSpec((None, tr, C), lambda j, i, c_ref: (j, c_ref[0] * nrb + i, 0)),
                      pl.BlockSpec((None, tr, C), lambda j, i, c_ref: (j, i, 0))],
            out_specs=pl.BlockSpec((None, tr, C), lambda j, i, c_ref: (j, i, 0))),
        compiler_params=_params(("parallel", "parallel")))(half, G, recv)


def _chip_exchange(P):
    n, R2, C = P.shape

    def body(p_ref, o_ref, send_sems, recv_sems, local_sem):
        x, y, c = _coords()
        me = 2 * x + y
        chips = _other_chips(x, y)
        mine = pltpu.make_async_copy(p_ref.at[me], o_ref.at[me], local_sem)
        mine.start()
        sends = [pltpu.make_async_remote_copy(
            src_ref=p_ref.at[2 * px + py], dst_ref=o_ref.at[me], send_sem=send_sems.at[k],
            recv_sem=recv_sems.at[k], device_id=(px, py, c), device_id_type=MESH)
            for k, (px, py) in enumerate(chips)]
        for cp in sends:
            cp.start()
        for k, (px, py) in enumerate(chips):
            pltpu.make_async_remote_copy(
                src_ref=p_ref.at[me], dst_ref=o_ref.at[2 * px + py], send_sem=send_sems.at[k],
                recv_sem=recv_sems.at[k], device_id=(px, py, c), device_id_type=MESH).wait_recv()
        for cp in sends:
            cp.wait_send()
        mine.wait()

    return pl.pallas_call(
        body, name="grad_chip_exchange", out_shape=jax.ShapeDtypeStruct((n, R2, C), P.dtype),
        in_specs=[ANY], out_specs=ANY,
        scratch_shapes=[pltpu.SemaphoreType.DMA((3,)), pltpu.SemaphoreType.DMA((3,)), pltpu.SemaphoreType.DMA],
    )(P)


def _chip_sum(X):
    n, R2, C = X.shape
    tr = _row_tile(R2, 16, 704)

    def body(x_ref, o_ref):
        acc = x_ref[0].astype(F32)
        for s in range(1, n):
            acc = acc + x_ref[s].astype(F32)
        o_ref[...] = acc

    return pl.pallas_call(
        body, name="grad_chip_sum", out_shape=jax.ShapeDtypeStruct((R2, C), F32), grid=(R2 // tr,),
        in_specs=[pl.BlockSpec((n, tr, C), lambda i: (0, i, 0))],
        out_specs=pl.BlockSpec((tr, C), lambda i: (i, 0)),
        compiler_params=_params(("parallel",)))(X)


def _pair_allgather(Q):
    R2, C = Q.shape

    def body(q_ref, o_ref, send_sem, recv_sem, local_sem):
        x, y, c = _coords()
        mine = pltpu.make_async_copy(q_ref, o_ref.at[c], local_sem)
        mine.start()
        cp = pltpu.make_async_remote_copy(src_ref=q_ref, dst_ref=o_ref.at[c], send_sem=send_sem,
                                          recv_sem=recv_sem, device_id=(x, y, 1 - c), device_id_type=MESH)
        cp.start()
        pltpu.make_async_remote_copy(src_ref=q_ref, dst_ref=o_ref.at[1 - c], send_sem=send_sem,
                                     recv_sem=recv_sem, device_id=(x, y, 1 - c), device_id_type=MESH).wait_recv()
        cp.wait_send()
        mine.wait()

    return pl.pallas_call(
        body, name="grad_pair_allgather", out_shape=jax.ShapeDtypeStruct((2, R2, C), Q.dtype),
        in_specs=[ANY], out_specs=ANY,
        scratch_shapes=[pltpu.SemaphoreType.DMA, pltpu.SemaphoreType.DMA, pltpu.SemaphoreType.DMA],
    )(Q)


def _allreduce_small(pack):
    Rs, C = pack.shape
    n_dev = 8

    def body(x_ref, o_ref, buf, send_sems, recv_sems):
        x, y, c = _coords()
        me = 4 * x + 2 * y + c
        buf[me] = x_ref[...]
        peers = []
        for rel in range(1, n_dev):
            px = 1 - x if rel & 4 else x
            py = 1 - y if rel & 2 else y
            pc = 1 - c if rel & 1 else c
            peers.append((px, py, pc))
        sends = [pltpu.make_async_remote_copy(
            src_ref=x_ref, dst_ref=buf.at[me], send_sem=send_sems.at[k], recv_sem=recv_sems.at[k],
            device_id=peer, device_id_type=MESH) for k, peer in enumerate(peers)]
        for cp in sends:
            cp.start()
        for k, (px, py, pc) in enumerate(peers):
            pltpu.make_async_remote_copy(
                src_ref=x_ref, dst_ref=buf.at[4 * px + 2 * py + pc], send_sem=send_sems.at[k],
                recv_sem=recv_sems.at[k], device_id=(px, py, pc), device_id_type=MESH).wait_recv()
        for cp in sends:
            cp.wait_send()
        acc = buf[0]
        for d in range(1, n_dev):
            acc = acc + buf[d]
        o_ref[...] = acc

    vm = pl.BlockSpec(memory_space=pltpu.VMEM)
    return pl.pallas_call(
        body, name="allreduce_replicated", out_shape=jax.ShapeDtypeStruct((Rs, C), F32),
        in_specs=[vm], out_specs=vm,
        scratch_shapes=[pltpu.VMEM((n_dev, Rs, C), F32), pltpu.SemaphoreType.DMA((n_dev - 1,)),
                        pltpu.SemaphoreType.DMA((n_dev - 1,))],
    )(pack)


SHARDED = (
    ("w_in", 2, True), ("w_out", 1, True), ("pool_w", 2, True), ("w_up", 2, True), ("w_down", 1, True),
    ("meta_tokens", 1, False), ("mix_norm_odd", 1, False), ("pool_b", 2, False), ("pool_scale", 1, False),
    ("conv_w", 2, False), ("ffn_conv_w", 2, False))
REPLICATED = ("mix_norm_even", "b_f", "conv_b", "ln_g", "ln_b", "ffn_norm", "ffn_conv_b", "final_norm")
PACK_ROW_ALIGN = 32


def _pad_rows(flat, align_rows, cols):
    rows = -(-flat.shape[-1] // cols)
    rows = -(-rows // align_rows) * align_rows
    pad = rows * cols - flat.shape[-1]
    flat = jnp.pad(flat, [(0, 0)] * (flat.ndim - 1) + [(0, pad)])
    return flat.reshape(flat.shape[:-1] + (rows, cols))


def _pack_weight_shards(shards):
    parts = []
    for name, _, as_bf16 in SHARDED:
        w = shards[name].astype(F32).reshape(-1)
        parts.append(w.astype(BF16) if as_bf16 else lax.bitcast_convert_type(w, BF16).reshape(-1))
    return _pad_rows(jnp.concatenate(parts), PACK_ROW_ALIGN, PACK_COLS)


def _unpack_weights(gathered, shards):
    flat = gathered.reshape(N_CHIPS, -1)
    out, off = {}, 0
    for name, axis, as_bf16 in SHARDED:
        shp = shards[name].shape
        n = int(np.prod(shp))
        if as_bf16:
            t = flat[:, off:off + n]
            off += n
        else:
            t = lax.bitcast_convert_type(flat[:, off:off + 2 * n].reshape(N_CHIPS, n, 2), F32)
            off += 2 * n
        t = t.reshape((N_CHIPS,) + shp)
        out[name] = jnp.concatenate([t[j] for j in range(N_CHIPS)], axis=axis)
    return out


def _pack_grad_shards(grads, shards):
    parts = []
    for name, axis, _ in SHARDED:
        g = grads[name].reshape(shards[name].shape[:axis] + (N_CHIPS, shards[name].shape[axis])
                                + shards[name].shape[axis + 1:])
        parts.append(jnp.moveaxis(g, axis, 0).reshape(N_CHIPS, -1))
    return _pad_rows(jnp.concatenate(parts, axis=1), PACK_ROW_ALIGN, PACK_COLS)


def _unpack_grad_shard(reduced, shards):
    flat = reduced.reshape(-1)
    out, off = {}, 0
    for name, _, _ in SHARDED:
        shp = shards[name].shape
        n = int(np.prod(shp))
        out[name] = flat[off:off + n].reshape(shp)
        off += n
    return out


def _pack_replicated(grads, loss):
    parts = [_pad_rows(grads[name].astype(F32).reshape(-1), 1, LANE).reshape(-1) for name in REPLICATED]
    parts.append(_pad_rows(loss.reshape(-1)[:1], 1, LANE).reshape(-1))
    return _pad_rows(jnp.concatenate(parts), 8, LANE)


def _unpack_replicated(reduced, shapes):
    flat = reduced.reshape(-1)
    out, off = {}, 0
    for name in REPLICATED:
        n = int(np.prod(shapes[name]))
        out[name] = flat[off:off + n].reshape(shapes[name])
        off += -(-n // LANE) * LANE
    return out, flat[off]


def _ffn_fwd2(h, W, layer, tm, tmm):
    tag = str(layer)
    n = _rms_fwd(h, W["ffn_norm"][layer:layer + 1], BF16, tm, f"ffn_norm_{tag}")
    up = _mm(n, W["w_up"], "nn", F32, tmm, 512, f"ffn_up_{tag}", b_lead=layer)
    act = _ffn_act_fwd(up, W["ffn_conv_w_p"][layer], W["ffn_conv_b"][layer:layer + 1], tm, f"ffn_act_{tag}")
    out = _mm(act, W["w_down"], "nn", F32, tmm, 512, f"ffn_down_{tag}", add=h, b_lead=layer)
    return out, (n, up, act)


def _ffn_bwd2(h, W, layer, saved, dout, acc, tm, tmm):
    tag = str(layer)
    n, up, act = saved
    dact = _mm(dout, W["w_down"], "nt", F32, tmm, 256, f"ffn_dact_{tag}", b_lead=layer)
    dwd = _mm(act, dout, "tn", F32, 256, 512, f"ffn_dwdown_{tag}",
              out=(layer, 2, None if acc is None else acc[1]))
    dup, dcw, dcb = _ffn_act_bwd(up, dact, W["ffn_conv_w_p"][layer], W["ffn_conv_b"][layer:layer + 1], tm,
                                 f"ffn_act_bwd_{tag}")
    dn = _mm_ffn_dn(dup, W["w_up"], layer, tm, 512, f"ffn_dn_{tag}")
    dwu = _mm_ffn_dwup(n, dup, layer, None if acc is None else acc[0], 512, 256, f"ffn_dwup_{tag}")
    dh, dgain = _rms_bwd(h, W["ffn_norm"][layer:layer + 1], dn, dout, tm, f"ffn_norm_bwd_{tag}")
    return dh, (dwu, dwd), dict(gain=dgain, cw=dcw[:FFN_K], cb=dcb)


def _local_step2(h0, tgt, W, n_real):
    LP = h0.shape[0]
    tm, tmm = _row_tiles(LP)
    nb = LP // ATT_BLK
    G = {}
    n0 = _rms_fwd(h0, W["mix_norm_even"], BF16, tm, "mix_norm_even")
    proj = _mm(n0, W["w_in_p"], "nn", F32, tmm, 384, "in_proj")
    c = _fgate_fwd(proj, W["b_f_p"], "forget_gate")
    cT = c[:, :HEADS].T
    c_col = cT[:, :, None]
    c_row = cT.reshape(HEADS, nb, 1, ATT_BLK)
    qkv = proj[:, :3 * FOX_W].astype(BF16)
    q, k, v = (_heads(qkv[:, i * FOX_W:(i + 1) * FOX_W], LP) for i in range(3))
    o, lse = _attn_fwd(q, k, v, c_col, c_row, "fox_attention")
    u1, u = _conf_fwd(proj, W["conv_w_p"], W["conv_b"], W["ln_g"], W["ln_b"], tm, "conformer")
    cat = jnp.concatenate([_unheads(o, LP).astype(BF16), u], axis=1)
    h1 = _mm(cat, W["w_out"], "nn", F32, tmm, 512, "out_proj", add=h0)
    h2, ffn0 = _ffn_fwd2(h1, W, 0, tm, tmm)
    h3, dpool = _pool_fwd(h2, W["mix_norm_odd"], W["pool_w"], W["pool_b"], W["pool_scale"], tm, "pool_mixer")
    h4, ffn1 = _ffn_fwd2(h3, W, 1, tm, tmm)
    loss, dh4, G["final_norm"] = _loss_head(h4, W["final_norm"], tgt, n_real, tm, "loss_head")

    dh3, acc, g1 = _ffn_bwd2(h3, W, 1, ffn1, dh4, None, tm, tmm)
    dh2, G["pool_w"], G["pool_b"], G["pool_scale"], G["mix_norm_odd"] = _pool_bwd(
        h2, W["mix_norm_odd"], dpool, W["pool_w"], W["pool_b"], W["pool_scale"], dh3, tm, "pool_mixer_bwd")
    dh1, acc, g0 = _ffn_bwd2(h1, W, 0, ffn0, dh2, acc, tm, tmm)
    G["w_up"], G["w_down"] = acc
    G["ffn_norm"] = jnp.concatenate([g0["gain"], g1["gain"]], axis=0)
    G["ffn_conv_w"] = jnp.stack([g0["cw"], g1["cw"]])
    G["ffn_conv_b"] = jnp.concatenate([g0["cb"], g1["cb"]], axis=0)

    dcat = _mm(dh1, W["w_out"], "nt", F32, tmm, 512, "out_proj_dx")
    G["w_out"] = _mm(cat, dh1, "tn", F32, 512, 512, "out_proj_dw")
    dadg, dcw, G["conv_b"], G["ln_g"], G["ln_b"] = _conf_bwd(
        proj, u1, dcat, W["conv_w_p"], W["ln_g"], W["ln_b"], tm, "conformer_bwd")
    G["conv_w"] = dcw[:CONV_K]
    do = _heads(dcat[:, :FOX_W], LP)
    dq, dk, dv, dcq, dck = _attn_bwd(q, k, v, o, do, lse, c_col, c_row, "fox_attention_bwd")
    dc = jnp.pad((dcq.reshape(HEADS, LP) + dck.reshape(HEADS, LP)).T, ((0, 0), (0, LANE - HEADS)))
    dfl, dbf = _fgate_bwd(proj, W["b_f_p"], dc, "forget_gate_bwd")
    G["b_f"] = dbf[:, :HEADS]
    dproj = jnp.concatenate([_unheads(t, LP).astype(BF16) for t in (dq, dk, dv)] + [dadg, dfl], axis=1)
    dn0 = _mm(dproj, W["w_in_p"], "nt", F32, tmm, 512, "in_proj_dx")
    G["w_in_p"] = _mm(n0, dproj, "tn", F32, 512, 384, "in_proj_dw")
    dh0, G["mix_norm_even"] = _rms_bwd(h0, W["mix_norm_even"], dn0, dh1, tm, "mix_norm_even_bwd")
    return loss, dh0, G


class _Cut:
    def __init__(self, full_shape, chip_dim, half_dim):
        self.full = tuple(full_shape)
        self.chip_dim, self.half_dim = chip_dim, half_dim
        self.chip_size = full_shape[chip_dim] // N_CHIPS
        self.half_size = full_shape[half_dim] // 2
        assert chip_dim != half_dim

    def shape(self, chip=False, half=False):
        s = list(self.full)
        if chip:
            s[self.chip_dim] = self.chip_size
        if half:
            s[self.half_dim] = self.half_size
        return tuple(s)

    def region(self, ref, chip=None, half=None):
        idx = [pl.ds(0, n) for n in ref.shape]
        if chip is not None:
            idx[self.chip_dim] = pl.ds(chip * self.chip_size, self.chip_size)
        if half is not None:
            idx[self.half_dim] = pl.ds(half * self.half_size, self.half_size)
        return ref.at[tuple(idx)]


SMALL_SHARDED = ("meta_tokens", "mix_norm_odd", "pool_b", "pool_scale", "conv_w", "ffn_conv_w")
SMALL_ROWS = 144


def _cuts():
    return {
        "w_in": _Cut((N_CHIPS, D_MODEL, IN_SHARD), 0, 1),
        "w_out": _Cut((D_MODEL, D_MODEL), 0, 1),
        "pool_w": _Cut((len(POOL_WINDOWS), POOL_G, POOL_G), 1, 0),
        "w_up": _Cut((2, D_MODEL, 2 * D_FF), 2, 1),
        "w_down": _Cut((2, D_FF, D_MODEL), 1, 2),
        "small": _Cut((N_CHIPS, SMALL_ROWS, LANE), 0, 1),
    }


COMM_ORDER = ("w_in", "w_out", "pool_w", "w_up", "w_down", "small")


def _remote(src, dst, send_sems, recv_sems, k, to):
    return pltpu.make_async_remote_copy(src_ref=src, dst_ref=dst, send_sem=send_sems.at[k],
                                        recv_sem=recv_sems.at[k], device_id=to, device_id_type=MESH)


def _gather_weights(shards, cuts):
    n = len(shards)

    def body(*refs):
        srcs, outs = refs[:n], refs[n:2 * n]
        send_sems, recv_sems = refs[2 * n:]
        x, y, c = _coords()
        me = 2 * x + y
        sibling = (x, y, 1 - c)
        chips = _other_chips(x, y)
        sends = []
        for t, cut in enumerate(cuts):
            push = _remote(srcs[t], cut.region(outs[t], chip=me), send_sems, recv_sems, 7 * t, sibling)
            push.start()
            sends.append(push)
            for kk, chip in enumerate(chips):
                cp = _remote(cut.region(srcs[t], half=c), cut.region(outs[t], chip=me, half=c),
                             send_sems, recv_sems, 7 * t + 1 + kk, (*chip, c))
                cp.start()
                sends.append(cp)
        for t, cut in enumerate(cuts):
            for kk, (px, py) in enumerate(chips):
                landed = cut.region(outs[t], chip=2 * px + py, half=c)
                _remote(landed, landed, send_sems, recv_sems, 7 * t + 1 + kk, sibling).wait_recv()
                fwd = _remote(landed, landed, send_sems, recv_sems, 7 * t + 4 + kk, sibling)
                fwd.start()
                sends.append(fwd)
        for t, cut in enumerate(cuts):
            mine = cut.region(outs[t], chip=me)
            _remote(mine, mine, send_sems, recv_sems, 7 * t, sibling).wait_recv()
            for kk, (px, py) in enumerate(chips):
                other = cut.region(outs[t], chip=2 * px + py, half=1 - c)
                _remote(other, other, send_sems, recv_sems, 7 * t + 4 + kk, sibling).wait_recv()
        for cp in sends:
            cp.wait_send()

    return pl.pallas_call(
        body, name="gather_weights",
        out_shape=tuple(jax.ShapeDtypeStruct(cut.full, s.dtype) for cut, s in zip(cuts, shards)),
        in_specs=[ANY] * n, out_specs=tuple([ANY] * n),
        scratch_shapes=[pltpu.SemaphoreType.DMA((7 * n,)), pltpu.SemaphoreType.DMA((7 * n,))],
    )(*shards)


def _pair_exchange2(fulls, cuts):
    n = len(fulls)

    def body(*refs):
        srcs, outs = refs[:n], refs[n:2 * n]
        send_sems, recv_sems = refs[2 * n:]
        x, y, c = _coords()
        cps = [_remote(cut.region(srcs[t], half=1 - c), outs[t], send_sems, recv_sems, t, (x, y, 1 - c))
               for t, cut in enumerate(cuts)]
        for cp in cps:
            cp.start()
        for cp in cps:
            cp.wait()

    return pl.pallas_call(
        body, name="grad_pair_exchange",
        out_shape=tuple(jax.ShapeDtypeStruct(cut.shape(half=True), f.dtype) for cut, f in zip(cuts, fulls)),
        in_specs=[ANY] * n, out_specs=tuple([ANY] * n),
        scratch_shapes=[pltpu.SemaphoreType.DMA((n,)), pltpu.SemaphoreType.DMA((n,))],
    )(*fulls)


def _grid_of(shape, blk):
    assert all(s % b == 0 for s, b in zip(shape, blk)), (shape, blk)
    return tuple(s // b for s, b in zip(shape, blk))


def _pair_sum2(full, recv, cut, blk, name):
    hshape = cut.shape(half=True)
    grid = _grid_of(hshape, blk)
    hb = cut.half_size // blk[cut.half_dim]
    hd = cut.half_dim
    pos = jnp.stack([lax.axis_index("c")]).astype(jnp.int32)

    def full_idx(*a):
        ids, p = list(a[:-1]), a[-1]
        ids[hd] = ids[hd] + p[0] * hb
        return tuple(ids)

    def body(p_ref, f_ref, r_ref, o_ref):
        o_ref[...] = (f_ref[...] + r_ref[...]).astype(o_ref.dtype)

    return pl.pallas_call(
        body, name=name, out_shape=jax.ShapeDtypeStruct(hshape, BF16),
        grid_spec=pltpu.PrefetchScalarGridSpec(
            num_scalar_prefetch=1, grid=grid,
            in_specs=[pl.BlockSpec(blk, full_idx), pl.BlockSpec(blk, lambda *a: tuple(a[:-1]))],
            out_specs=pl.BlockSpec(blk, lambda *a: tuple(a[:-1]))),
        compiler_params=_params(("parallel",) * len(grid)))(pos, full, recv)


def _chip_exchange2(parts, cuts):
    n = len(parts)

    def body(*refs):
        srcs, outs = refs[:n], refs[n:2 * n]
        send_sems, recv_sems = refs[2 * n:]
        x, y, c = _coords()
        me = 2 * x + y
        chips = _other_chips(x, y)
        sends = []
        for t, cut in enumerate(cuts):
            for kk, (px, py) in enumerate(chips):
                cp = _remote(cut.region(srcs[t], chip=2 * px + py), outs[t].at[me], send_sems, recv_sems,
                             3 * t + kk, (px, py, c))
                cp.start()
                sends.append(cp)
        for t, cut in enumerate(cuts):
            for kk, (px, py) in enumerate(chips):
                slot = outs[t].at[2 * px + py]
                _remote(slot, slot, send_sems, recv_sems, 3 * t + kk, (px, py, c)).wait_recv()
        for cp in sends:
            cp.wait_send()

    return pl.pallas_call(
        body, name="grad_chip_exchange",
        out_shape=tuple(jax.ShapeDtypeStruct((N_CHIPS,) + cut.shape(chip=True, half=True), p.dtype)
                        for cut, p in zip(cuts, parts)),
        in_specs=[ANY] * n, out_specs=tuple([ANY] * n),
        scratch_shapes=[pltpu.SemaphoreType.DMA((3 * n,)), pltpu.SemaphoreType.DMA((3 * n,))],
    )(*parts)


def _chip_sum2(part, recv, cut, blk, name):
    bshape = cut.shape(chip=True, half=True)
    grid = _grid_of(bshape, blk)
    cb = cut.chip_size // blk[cut.chip_dim]
    hb = cut.half_size // blk[cut.half_dim]
    cd, hd = cut.chip_dim, cut.half_dim
    x, y, c = _coords()
    slots = [2 * px + py for px, py in _other_chips(x, y)]
    pos = jnp.stack([c, 2 * x + y] + slots).astype(jnp.int32)

    def part_idx(*a):
        ids, p = list(a[:-1]), a[-1]
        ids[cd] = ids[cd] + p[1] * cb
        return tuple(ids)

    def recv_idx(kk):
        return lambda *a: (a[-1][2 + kk],) + tuple(a[:-1])

    def out_idx(*a):
        ids, p = list(a[:-1]), a[-1]
        ids[hd] = ids[hd] + p[0] * hb
        return tuple(ids)

    def body(p_ref, own_ref, r0_ref, r1_ref, r2_ref, o_ref):
        acc = own_ref[...].astype(F32)
        for r_ref in (r0_ref, r1_ref, r2_ref):
            acc = acc + r_ref[...].astype(F32)
        o_ref[...] = acc

    return pl.pallas_call(
        body, name=name, out_shape=jax.ShapeDtypeStruct(cut.shape(chip=True), F32),
        grid_spec=pltpu.PrefetchScalarGridSpec(
            num_scalar_prefetch=1, grid=grid,
            in_specs=[pl.BlockSpec(blk, part_idx)] + [pl.BlockSpec((None,) + blk, recv_idx(kk)) for kk in range(3)],
            out_specs=pl.BlockSpec(blk, out_idx)),
        compiler_params=_params(("parallel",) * len(grid)))(pos, part, recv, recv, recv)


def _pair_swap2(blocks, cuts):
    n = len(blocks)

    def body(*refs):
        outs = refs[n:2 * n]
        send_sems, recv_sems = refs[2 * n:]
        x, y, c = _coords()
        cps = []
        for t, cut in enumerate(cuts):
            mine = cut.region(outs[t], half=c)
            cp = _remote(mine, mine, send_sems, recv_sems, t, (x, y, 1 - c))
            cp.start()
            cps.append(cp)
        for t, cut in enumerate(cuts):
            theirs = cut.region(outs[t], half=1 - c)
            _remote(theirs, theirs, send_sems, recv_sems, t, (x, y, 1 - c)).wait_recv()
        for cp in cps:
            cp.wait_send()

    return pl.pallas_call(
        body, name="grad_pair_swap",
        out_shape=tuple(jax.ShapeDtypeStruct(b.shape, b.dtype) for b in blocks),
        in_specs=[ANY] * n, out_specs=tuple([ANY] * n),
        input_output_aliases={t: t for t in range(n)},
        scratch_shapes=[pltpu.SemaphoreType.DMA((n,)), pltpu.SemaphoreType.DMA((n,))],
    )(*blocks)


PAIR_SUM_BLOCKS = {"w_in": (1, 512, IN_SHARD), "w_out": (512, 512), "pool_w": (1, POOL_G, POOL_G),
                   "w_up": (1, 64, 2 * D_FF), "w_down": (1, 704, 512), "small": (N_CHIPS, SMALL_ROWS // 2, LANE)}
CHIP_SUM_BLOCKS = {"w_in": (1, 512, IN_SHARD), "w_out": (256, 512), "pool_w": (2, 64, POOL_G),
                   "w_up": (1, 128, UP_SHARD), "w_down": (1, DOWN_SHARD, 512), "small": (1, SMALL_ROWS // 2, LANE)}


def _pack_small(P):
    parts = []
    for name in SMALL_SHARDED:
        t = P[name]
        parts.append(t.astype(F32))
    return parts


def _small_rows(t, lead):
    flat = t.reshape(lead + (-1,))
    pad = -flat.shape[-1] % LANE
    return jnp.pad(flat, [(0, 0)] * len(lead) + [(0, pad)]).reshape(lead + (-1, LANE))


def _pack_small_shards(shards):
    rows = jnp.concatenate([_small_rows(shards[n].astype(F32), ()) for n in SMALL_SHARDED], axis=0)
    return jnp.pad(rows, ((0, SMALL_ROWS - rows.shape[0]), (0, 0)))[None]


def _unpack_small(pack, shards, axes):
    out, off = {}, 0
    nchip = pack.shape[0]
    for name in SMALL_SHARDED:
        shp = shards[name].shape
        cnt = int(np.prod(shp))
        rows = -(-cnt // LANE)
        t = pack[:, off:off + rows].reshape(nchip, -1)[:, :cnt].reshape((nchip,) + shp)
        out[name] = jnp.concatenate([t[j] for j in range(nchip)], axis=axes[name])
        off += rows
    return out


def _pack_small_grads(grads, shards, axes):
    parts = []
    for name in SMALL_SHARDED:
        shp, ax = shards[name].shape, axes[name]
        g = grads[name].reshape(shp[:ax] + (N_CHIPS, shp[ax]) + shp[ax + 1:])
        parts.append(_small_rows(jnp.moveaxis(g, ax, 0), (N_CHIPS,)))
    rows = jnp.concatenate(parts, axis=1)
    return jnp.pad(rows, ((0, 0), (0, SMALL_ROWS - rows.shape[1]), (0, 0)))


SMALL_AXES = {"meta_tokens": 1, "mix_norm_odd": 1, "pool_b": 2, "pool_scale": 1, "conv_w": 2, "ffn_conv_w": 2}


WEIGHT_NAMES = ("meta_tokens", "mix_norm_even", "w_in", "b_f", "conv_w", "conv_b", "ln_g", "ln_b", "w_out",
                "mix_norm_odd", "pool_w", "pool_b", "pool_scale", "ffn_norm", "w_up", "ffn_conv_w",
                "ffn_conv_b", "w_down", "final_norm")


def kernel(x, meta_tokens, mix_norm_even, w_in, b_f, conv_w, conv_b, ln_g, ln_b, w_out, mix_norm_odd, pool_w, pool_b, pool_scale, ffn_norm, w_up, ffn_conv_w, ffn_conv_b, w_down, final_norm, loss_target, m_meta_tokens, m_mix_norm_even, m_w_in, m_b_f, m_conv_w, m_conv_b, m_ln_g, m_ln_b, m_w_out, m_mix_norm_odd, m_pool_w, m_pool_b, m_pool_scale, m_ffn_norm, m_w_up, m_ffn_conv_w, m_ffn_conv_b, m_w_down, m_final_norm, v_meta_tokens, v_mix_norm_even, v_w_in, v_b_f, v_conv_w, v_conv_b, v_ln_g, v_ln_b, v_w_out, v_mix_norm_odd, v_pool_w, v_pool_b, v_pool_scale, v_ffn_norm, v_w_up, v_ffn_conv_w, v_ffn_conv_b, v_w_down, v_final_norm):
    given = dict(locals())
    w_loc = {n: given[n] for n in WEIGHT_NAMES}
    m_loc = {n: given["m_" + n] for n in WEIGHT_NAMES}
    v_loc = {n: given["v_" + n] for n in WEIGHT_NAMES}
    cut_of = _cuts()
    cuts = [cut_of[n] for n in COMM_ORDER]
    big = ("w_in", "w_out", "pool_w", "w_up", "w_down")
    small_shards = {n: w_loc[n] for n in SMALL_SHARDED}

    shards = [w_loc[n].astype(BF16).reshape(cut_of[n].shape(chip=True)) for n in big]
    shards.append(_pack_small_shards(small_shards))
    g_in, g_out, g_pool, g_up, g_down, g_small = _gather_weights(shards, cuts)
    full = _unpack_small(g_small, small_shards, SMALL_AXES)
    full.update({n: w_loc[n] for n in REPLICATED})
    w_in_full = g_in.transpose(1, 0, 2).reshape(D_MODEL, IN_COLS)
    qkv, f, ag = (w_in_full[:, :3 * FOX_W], w_in_full[:, 3 * FOX_W:3 * FOX_W + HEADS],
                  w_in_full[:, 3 * FOX_W + HEADS:])
    W = dict(
        mix_norm_even=full["mix_norm_even"].reshape(1, D_MODEL),
        w_in_p=jnp.concatenate([qkv, ag, f, jnp.zeros((D_MODEL, LANE - HEADS), BF16)], axis=1),
        b_f_p=jnp.pad(full["b_f"].reshape(1, HEADS), ((0, 0), (0, LANE - HEADS))),
        conv_w_p=jnp.pad(full["conv_w"].reshape(CONV_K, CONV_CH), ((0, CONV_HALO - CONV_K), (0, 0))),
        conv_b=full["conv_b"].reshape(1, CONV_CH), ln_g=full["ln_g"].reshape(1, CONV_CH),
        ln_b=full["ln_b"].reshape(1, CONV_CH), w_out=g_out,
        mix_norm_odd=full["mix_norm_odd"].reshape(1, D_MODEL), pool_w=g_pool,
        pool_b=full["pool_b"].reshape(1, D_MODEL), pool_scale=full["pool_scale"].reshape(1, D_MODEL),
        ffn_norm=full["ffn_norm"], w_up=g_up,
        ffn_conv_w_p=jnp.pad(full["ffn_conv_w"], ((0, 0), (0, 8 - FFN_K), (0, 0))),
        ffn_conv_b=full["ffn_conv_b"], w_down=g_down, final_norm=full["final_norm"].reshape(1, D_MODEL))

    seq = x.shape[1]
    n_real = N_META + seq
    LP = -(-n_real // ATT_BLK) * ATT_BLK
    tail = jnp.zeros((LP - n_real, D_MODEL), F32)
    h0 = jnp.concatenate([full["meta_tokens"], x[0], tail], axis=0)
    tgt = jnp.concatenate([jnp.zeros((N_META, D_MODEL), F32), loss_target[0], tail], axis=0)
    loss_loc, dh0, G = _local_step2(h0, tgt, W, n_real)
    grad_x = dh0[N_META:n_real][None]
    G["meta_tokens"] = dh0[:N_META]

    rep_shapes = {n: w_loc[n].shape for n in REPLICATED}
    G["final_norm"] = G["final_norm"].reshape(D_MODEL)
    rep, loss = _unpack_replicated(_allreduce_small(_pack_replicated(G, loss_loc)), rep_shapes)

    gp = G["w_in_p"]
    g_w_in = jnp.concatenate([gp[:, :3 * FOX_W], gp[:, 3 * FOX_W + 2 * CONV_CH:3 * FOX_W + 2 * CONV_CH + HEADS],
                              gp[:, 3 * FOX_W:3 * FOX_W + 2 * CONV_CH]], axis=1)
    fulls = [g_w_in.reshape(D_MODEL, N_CHIPS, IN_SHARD).transpose(1, 0, 2), G["w_out"], G["pool_w"], G["w_up"],
             G["w_down"], _pack_small_grads(G, small_shards, SMALL_AXES)]
    recvs = _pair_exchange2(fulls, cuts)
    parts = [_pair_sum2(f, r, cut, PAIR_SUM_BLOCKS[n], "grad_pair_sum_" + n)
             for f, r, cut, n in zip(fulls, recvs, cuts, COMM_ORDER)]
    others = _chip_exchange2(parts, cuts)
    blocks = [_chip_sum2(p, o, cut, CHIP_SUM_BLOCKS[n], "grad_chip_sum_" + n)
              for p, o, cut, n in zip(parts, others, cuts, COMM_ORDER)]
    blocks = _pair_swap2(blocks, cuts)
    gsh = {n: b.reshape(w_loc[n].shape) for n, b in zip(big, blocks[:5])}
    gsh.update(_unpack_small(blocks[5], small_shards, SMALL_AXES))
    sharded = set(big) | set(SMALL_SHARDED)

    grad_w = {n: (gsh[n] if n in sharded else rep[n]) for n in WEIGHT_NAMES}
    delta, new_m, new_v = {}, {}, {}
    for n in WEIGHT_NAMES:
        delta[n], new_m[n], new_v[n] = _adamw(w_loc[n], grad_w[n], m_loc[n], v_loc[n], "adamw_" + n)
    return (loss, grad_x, *[grad_w[n] for n in WEIGHT_NAMES], *[delta[n] for n in WEIGHT_NAMES],
            *[new_m[n] for n in WEIGHT_NAMES], *[new_v[n] for n in WEIGHT_NAMES])
```

```python
import functools

import numpy as np
import jax
import jax.numpy as jnp
from jax import lax
from jax.experimental import pallas as pl
from jax.experimental.pallas import tpu as pltpu

F32 = jnp.float32
BF16 = jnp.bfloat16

D_MODEL = 1024
N_META = 16
SEQ = 2048
HEADS = 8
HEAD_DIM = 64
FOX_W = HEADS * HEAD_DIM
CONV_CH = 512
CONV_K = 31
D_FF = 2816
POOL_WINDOWS = (2, 4, 8, 16)
POOL_G = 256
RMS_EPS = 1e-6
LN_EPS = 1e-5
IN_COLS = 3 * FOX_W + HEADS + 2 * CONV_CH
IN_COLS_P = 3 * FOX_W + 2 * CONV_CH + 128
F_COL_BLK = (3 * FOX_W + 2 * CONV_CH) // 128
N_CHIPS = 4
IN_SHARD = IN_COLS // N_CHIPS
UP_SHARD = 2 * D_FF // N_CHIPS
DOWN_SHARD = D_FF // N_CHIPS

ADAM_LR = 0.001
ADAM_B1 = 0.9
ADAM_B2 = 0.999
ADAM_EPS = 1e-08
ADAM_WD = 0.01
ADAM_STEP = 10

LANE = 128
ATT_BLK = 128
VMEM_LIMIT = 48 * 1024 * 1024

NEG = -1e30


def _sigmoid(x):
    return 1.0 / (1.0 + jnp.exp(-x))


def _params(sem=None):
    return pltpu.CompilerParams(dimension_semantics=sem, vmem_limit_bytes=VMEM_LIMIT)


def _sub_rows(tm):
    best = 8
    for s in range(8, 137, 8):
        if tm % s == 0:
            best = s
    return best


def _mm(a, b, mode, out_dtype, tm, tn, name, add=None, a_lead=None, b_lead=None, out=None):
    a_shape = a.shape if a_lead is None else a.shape[1:]
    b_shape = b.shape if b_lead is None else b.shape[1:]
    if mode == "nn":
        (M, K), (K2, N) = a_shape, b_shape
        dims = (((1,), (0,)), ((), ()))
        a_blk, a_idx = (tm, K), (lambda i, j: (i, 0))
        b_blk, b_idx = (K, tn), (lambda i, j: (0, j))
    elif mode == "nt":
        (M, K), (N, K2) = a_shape, b_shape
        dims = (((1,), (1,)), ((), ()))
        a_blk, a_idx = (tm, K), (lambda i, j: (i, 0))
        b_blk, b_idx = (tn, K), (lambda i, j: (j, 0))
    else:
        (K, M), (K2, N) = a_shape, b_shape
        dims = (((0,), (0,)), ((), ()))
        a_blk, a_idx = (K, tm), (lambda i, j: (0, i))
        b_blk, b_idx = (K, tn), (lambda i, j: (0, j))
    assert K == K2 and M % tm == 0 and N % tn == 0, (name, a.shape, b.shape, tm, tn)
    gm, gn = M // tm, N // tn
    a_bytes = M * K * a.dtype.itemsize
    b_bytes = N * K * b.dtype.itemsize
    m_outer = a_bytes + b_bytes * gm <= b_bytes + a_bytes * gn
    if m_outer:
        grid = (gm, gn)
        wrap = lambda f: f
    else:
        grid = (gn, gm)
        wrap = lambda f: (lambda j, i: f(i, j))

    def lead(blk, idx, at):
        if at is None:
            return pl.BlockSpec(blk, wrap(idx))
        return pl.BlockSpec((None,) + blk, wrap(lambda i, j: (at,) + idx(i, j)))

    o_idx = lambda i, j: (i, j)
    in_specs = [lead(a_blk, a_idx, a_lead), lead(b_blk, b_idx, b_lead)]
    args = [a, b]
    if add is not None:
        in_specs.append(pl.BlockSpec((tm, tn), wrap(o_idx)))
        args.append(add)
    aliases = {}
    if out is None:
        out_shape = jax.ShapeDtypeStruct((M, N), out_dtype)
        out_spec = pl.BlockSpec((tm, tn), wrap(o_idx))
    else:
        o_lead, n_lead, into = out
        out_shape = jax.ShapeDtypeStruct((n_lead, M, N), out_dtype)
        out_spec = lead((tm, tn), o_idx, o_lead)
        if into is not None:
            aliases = {len(args): 0}
            in_specs.append(pl.BlockSpec(memory_space=pl.ANY))
            args.append(into)
    has_add = add is not None

    def body(a_ref, b_ref, *rest):
        o_ref = rest[-1]
        x = a_ref[...].astype(BF16)
        y = b_ref[...].astype(BF16)
        acc = lax.dot_general(x, y, dims, preferred_element_type=F32)
        if has_add:
            acc = acc + rest[0][...]
        o_ref[...] = acc.astype(o_ref.dtype)

    return pl.pallas_call(
        body, name=name, out_shape=out_shape, grid=grid, in_specs=in_specs, out_specs=out_spec,
        input_output_aliases=aliases, compiler_params=_params(("parallel", "parallel")))(*args)


def _mm_ffn_dn(dup, w_up, layer, tm, tn, name):
    _, LP, F = dup.shape
    Dm = w_up.shape[1]
    nt = (((1,), (1,)), ((), ()))

    def body(a_ref, b_ref, o_ref):
        acc = lax.dot_general(a_ref[0], b_ref[:, 0:F], nt, preferred_element_type=F32)
        acc = acc + lax.dot_general(a_ref[1], b_ref[:, F:2 * F], nt, preferred_element_type=F32)
        o_ref[...] = acc

    return pl.pallas_call(
        body, name=name, out_shape=jax.ShapeDtypeStruct((LP, Dm), F32), grid=(LP // tm, Dm // tn),
        in_specs=[pl.BlockSpec((2, tm, F), lambda i, j: (0, i, 0)),
                  pl.BlockSpec((None, tn, 2 * F), lambda i, j: (layer, j, 0))],
        out_specs=pl.BlockSpec((tm, tn), lambda i, j: (i, j)),
        compiler_params=_params(("parallel", "parallel")))(dup, w_up)


def _mm_ffn_dwup(n, dup, layer, into, tk, tn, name):
    LP, Dm = n.shape
    F = dup.shape[2]
    nct = F // tn
    tdims = (((0,), (0,)), ((), ()))

    def body(a_ref, b_ref, *rest):
        rest[-1][...] = lax.dot_general(a_ref[...], b_ref[...], tdims, preferred_element_type=F32)

    in_specs = [pl.BlockSpec((LP, tk), lambda i, j: (0, i)),
                pl.BlockSpec((None, LP, tn), lambda i, j: (j // nct, 0, j % nct))]
    args = [n, dup]
    aliases = {}
    if into is not None:
        in_specs.append(pl.BlockSpec(memory_space=pl.ANY))
        args.append(into)
        aliases = {2: 0}
    return pl.pallas_call(
        body, name=name, out_shape=jax.ShapeDtypeStruct((2, Dm, 2 * F), F32), grid=(Dm // tk, 2 * nct),
        in_specs=in_specs, out_specs=pl.BlockSpec((None, tk, tn), lambda i, j: (layer, i, j)),
        input_output_aliases=aliases, compiler_params=_params(("parallel", "parallel")))(*args)


def _rms_fwd(h, g, out_dtype, tm, name):
    LP, Dm = h.shape

    def body(h_ref, g_ref, o_ref):
        x = h_ref[...]
        r = lax.rsqrt(jnp.mean(x * x, axis=1, keepdims=True) + RMS_EPS)
        o_ref[...] = (x * r * g_ref[...]).astype(o_ref.dtype)

    return pl.pallas_call(
        body, name=name, out_shape=jax.ShapeDtypeStruct((LP, Dm), out_dtype), grid=(LP // tm,),
        in_specs=[pl.BlockSpec((tm, Dm), lambda i: (i, 0)), pl.BlockSpec((1, Dm), lambda i: (0, 0))],
        out_specs=pl.BlockSpec((tm, Dm), lambda i: (i, 0)),
        compiler_params=_params(("parallel",)))(h, g)


def _rms_bwd(h, g, dn, dres, tm, name):
    LP, Dm = h.shape

    def body(h_ref, g_ref, dn_ref, dr_ref, dh_ref, dg_ref):
        i = pl.program_id(0)
        x = h_ref[...]
        r = lax.rsqrt(jnp.mean(x * x, axis=1, keepdims=True) + RMS_EPS)
        xhat = x * r
        dy = dn_ref[...]
        dxh = dy * g_ref[...]
        dh = r * (dxh - xhat * jnp.mean(dxh * xhat, axis=1, keepdims=True))
        dh_ref[...] = dr_ref[...] + dh

        @pl.when(i == 0)
        def _():
            dg_ref[...] = jnp.zeros_like(dg_ref)

        dg_ref[...] += jnp.sum(dy * xhat, axis=0, keepdims=True)

    row = pl.BlockSpec((tm, Dm), lambda i: (i, 0))
    vec = pl.BlockSpec((1, Dm), lambda i: (0, 0))
    return pl.pallas_call(
        body, name=name,
        out_shape=(jax.ShapeDtypeStruct((LP, Dm), F32), jax.ShapeDtypeStruct((1, Dm), F32)),
        grid=(LP // tm,), in_specs=[row, vec, row, row], out_specs=(row, vec),
        compiler_params=_params(("arbitrary",)))(h, g, dn, dres)


def _loss_head(h, g, tgt, n_real, tm, name):
    LP, Dm = h.shape

    def body(h_ref, g_ref, t_ref, loss_ref, dh_ref, dg_ref):
        i = pl.program_id(0)
        x = h_ref[...]
        gg = g_ref[...]
        r = lax.rsqrt(jnp.mean(x * x, axis=1, keepdims=True) + RMS_EPS)
        xhat = x * r
        rows = i * tm + lax.broadcasted_iota(jnp.int32, (tm, 1), 0)
        real = jnp.logical_and(rows >= N_META, rows < n_real)
        diff = jnp.where(real, xhat * gg - t_ref[...], 0.0)
        dy = diff * (1.0 / Dm)
        dxh = dy * gg
        dh_ref[...] = r * (dxh - xhat * jnp.mean(dxh * xhat, axis=1, keepdims=True))

        @pl.when(i == 0)
        def _():
            dg_ref[...] = jnp.zeros_like(dg_ref)
            loss_ref[...] = jnp.zeros_like(loss_ref)

        dg_ref[...] += jnp.sum(dy * xhat, axis=0, keepdims=True)
        part = jnp.sum(jnp.sum(diff * diff, axis=1, keepdims=True), axis=0, keepdims=True)
        loss_ref[...] += jnp.broadcast_to(part * (0.5 / Dm), loss_ref.shape)

    row = pl.BlockSpec((tm, Dm), lambda i: (i, 0))
    vec = pl.BlockSpec((1, Dm), lambda i: (0, 0))
    return pl.pallas_call(
        body, name=name,
        out_shape=(jax.ShapeDtypeStruct((1, LANE), F32), jax.ShapeDtypeStruct((LP, Dm), F32),
                   jax.ShapeDtypeStruct((1, Dm), F32)),
        grid=(LP // tm,), in_specs=[row, vec, row],
        out_specs=(pl.BlockSpec((1, LANE), lambda i: (0, 0)), row, vec),
        compiler_params=_params(("arbitrary",)))(h, g, tgt)


def _fgate_fwd(proj, bf_p, name):
    LP = proj.shape[0]
    nb = LP // LANE

    def body(f_ref, b_ref, c_ref, lf_ref):
        x = f_ref[...] + b_ref[...]
        lf_ref[...] = jnp.minimum(x, 0.0) - jnp.log1p(jnp.exp(-jnp.abs(x)))
        ri = lax.broadcasted_iota(jnp.int32, (LANE, LANE), 0)
        ci = lax.broadcasted_iota(jnp.int32, (LANE, LANE), 1)
        tri = jnp.where(ri >= ci, 1.0, 0.0).astype(F32)

        def blk(i, carry):
            rows = pl.ds(pl.multiple_of(i * LANE, LANE), LANE)
            cb = jnp.dot(tri, lf_ref[rows, :], precision=lax.Precision.HIGHEST,
                         preferred_element_type=F32) + carry
            c_ref[rows, :] = cb
            return cb[LANE - 1:LANE, :]

        lax.fori_loop(0, nb, blk, jnp.zeros((1, LANE), F32))

    return pl.pallas_call(
        body, name=name, out_shape=jax.ShapeDtypeStruct((LP, LANE), F32), grid=(1,),
        in_specs=[pl.BlockSpec((LP, LANE), lambda i: (0, F_COL_BLK)),
                  pl.BlockSpec((1, LANE), lambda i: (0, 0))],
        out_specs=pl.BlockSpec((LP, LANE), lambda i: (0, 0)),
        scratch_shapes=[pltpu.VMEM((LP, LANE), F32)],
        compiler_params=_params(("arbitrary",)))(proj, bf_p)


def _fgate_bwd(proj, bf_p, dc, name):
    LP = proj.shape[0]
    nb = LP // LANE

    def body(f_ref, b_ref, dc_ref, dl_ref, db_ref):
        ri = lax.broadcasted_iota(jnp.int32, (LANE, LANE), 0)
        ci = lax.broadcasted_iota(jnp.int32, (LANE, LANE), 1)
        triu = jnp.where(ri <= ci, 1.0, 0.0).astype(F32)
        bb = b_ref[...]

        tail = jnp.zeros((1, LANE), F32)
        dbs = jnp.zeros((1, LANE), F32)
        for i in range(nb - 1, -1, -1):
            rows = slice(i * LANE, (i + 1) * LANE)
            gb = jnp.dot(triu, dc_ref[rows, :], precision=lax.Precision.HIGHEST,
                         preferred_element_type=F32) + tail
            x = f_ref[rows, :] + bb
            dl = gb * _sigmoid(-x)
            dl_ref[rows, :] = dl.astype(dl_ref.dtype)
            tail = gb[0:1, :]
            dbs = dbs + jnp.sum(dl, axis=0, keepdims=True)
        db_ref[...] = dbs

    return pl.pallas_call(
        body, name=name,
        out_shape=(jax.ShapeDtypeStruct((LP, LANE), BF16), jax.ShapeDtypeStruct((1, LANE), F32)),
        grid=(1,),
        in_specs=[pl.BlockSpec((LP, LANE), lambda i: (0, F_COL_BLK)),
                  pl.BlockSpec((1, LANE), lambda i: (0, 0)),
                  pl.BlockSpec((LP, LANE), lambda i: (0, 0))],
        out_specs=(pl.BlockSpec((LP, LANE), lambda i: (0, 0)), pl.BlockSpec((1, LANE), lambda i: (0, 0))),
        compiler_params=_params(("arbitrary",)))(proj, bf_p, dc)


def _attn_fwd(q, k, v, c_col, c_row, name):
    Hh, LP, Dh = q.shape
    nb = LP // ATT_BLK
    scale = Dh ** -0.5
    nt = (((1,), (1,)), ((), ()))

    def body(q_ref, k_ref, v_ref, cc_ref, cr_ref, o_ref, lse_ref):
        i = pl.program_id(1)
        qb = q_ref[...]
        cq = cc_ref[...]
        rows = i * ATT_BLK + lax.broadcasted_iota(jnp.int32, (ATT_BLK, ATT_BLK), 0)
        cols0 = lax.broadcasted_iota(jnp.int32, (ATT_BLK, ATT_BLK), 1)

        def step(j, carry):
            m, l, acc = carry
            ks = pl.ds(pl.multiple_of(j * ATT_BLK, ATT_BLK), ATT_BLK)
            s = lax.dot_general(qb, k_ref[ks, :], nt, preferred_element_type=F32) * scale
            s = s + cq - cr_ref[j]
            s = jnp.where(cols0 + j * ATT_BLK <= rows, s, NEG)
            m_new = jnp.maximum(m, jnp.max(s, axis=1, keepdims=True))
            p = jnp.exp(s - m_new)
            alpha = jnp.exp(m - m_new)
            l = alpha * l + jnp.sum(p, axis=1, keepdims=True)
            acc = alpha * acc + jnp.dot(p.astype(BF16), v_ref[ks, :], preferred_element_type=F32)
            return m_new, l, acc

        init = (jnp.full((ATT_BLK, 1), NEG, F32), jnp.zeros((ATT_BLK, 1), F32),
                jnp.zeros((ATT_BLK, Dh), F32))
        m, l, acc = lax.fori_loop(0, i + 1, step, init)
        o_ref[...] = acc / l
        lse_ref[...] = m + jnp.log(l)

    qspec = pl.BlockSpec((None, ATT_BLK, Dh), lambda h, i: (h, i, 0))
    kspec = pl.BlockSpec((None, LP, Dh), lambda h, i: (h, 0, 0))
    colspec = pl.BlockSpec((None, ATT_BLK, 1), lambda h, i: (h, i, 0))
    rowspec = pl.BlockSpec((None, nb, 1, ATT_BLK), lambda h, i: (h, 0, 0, 0))
    return pl.pallas_call(
        body, name=name,
        out_shape=(jax.ShapeDtypeStruct((Hh, LP, Dh), F32), jax.ShapeDtypeStruct((Hh, LP, 1), F32)),
        grid=(Hh, nb), in_specs=[qspec, kspec, kspec, colspec, rowspec],
        out_specs=(qspec, colspec),
        compiler_params=_params(("parallel", "arbitrary")))(q, k, v, c_col, c_row)


def _attn_bwd(q, k, v, o, do, lse, c_col, c_row, name):
    Hh, LP, Dh = q.shape
    nb = LP // ATT_BLK
    scale = Dh ** -0.5
    nt = (((1,), (1,)), ((), ()))
    tn = (((0,), (0,)), ((), ()))

    def body(q_ref, k_ref, v_ref, o_ref, do_ref, lse_ref, cc_ref, cr_ref,
             dq_ref, dk_ref, dv_ref, dcq_ref, dc_ref, delta_ref):
        j = pl.program_id(1)

        @pl.when(j == 0)
        def _():
            dq_ref[...] = jnp.zeros_like(dq_ref)
            dcq_ref[...] = jnp.zeros_like(dcq_ref)
            dob_all = do_ref[...].astype(BF16).astype(F32)
            delta_ref[...] = jnp.sum(dob_all * o_ref[...], axis=1, keepdims=True)

        kb = k_ref[...]
        vb = v_ref[...]
        ck = cr_ref[j]
        rows0 = lax.broadcasted_iota(jnp.int32, (ATT_BLK, ATT_BLK), 0)
        cols = j * ATT_BLK + lax.broadcasted_iota(jnp.int32, (ATT_BLK, ATT_BLK), 1)

        def step(i, carry):
            dk, dv, dcs = carry
            qs = pl.ds(pl.multiple_of(i * ATT_BLK, ATT_BLK), ATT_BLK)
            qb = q_ref[qs, :]
            dob = do_ref[qs, :].astype(BF16)
            s = lax.dot_general(qb, kb, nt, preferred_element_type=F32) * scale
            s = s + cc_ref[qs, :] - ck
            s = jnp.where(cols <= rows0 + i * ATT_BLK, s, NEG)
            p = jnp.exp(s - lse_ref[qs, :])
            dp = lax.dot_general(dob, vb, nt, preferred_element_type=F32)
            ds = p * (dp - delta_ref[qs, :])
            dsb = ds.astype(BF16)
            dv = dv + lax.dot_general(p.astype(BF16), dob, tn, preferred_element_type=F32)
            dk = dk + lax.dot_general(dsb, qb, tn, preferred_element_type=F32) * scale
            dq_ref[qs, :] += jnp.dot(dsb, kb, preferred_element_type=F32) * scale
            dcq_ref[qs, :] += jnp.sum(ds, axis=1, keepdims=True)
            dcs = dcs - jnp.sum(ds, axis=0, keepdims=True)
            return dk, dv, dcs

        init = (jnp.zeros((ATT_BLK, Dh), F32), jnp.zeros((ATT_BLK, Dh), F32),
                jnp.zeros((1, ATT_BLK), F32))
        dk, dv, dcs = lax.fori_loop(j, nb, step, init)
        dk_ref[...] = dk
        dv_ref[...] = dv
        dc_ref[...] = dcs

    full = pl.BlockSpec((None, LP, Dh), lambda h, j: (h, 0, 0))
    blk = pl.BlockSpec((None, ATT_BLK, Dh), lambda h, j: (h, j, 0))
    col = pl.BlockSpec((None, LP, 1), lambda h, j: (h, 0, 0))
    rowspec = pl.BlockSpec((None, nb, 1, ATT_BLK), lambda h, j: (h, 0, 0, 0))
    return pl.pallas_call(
        body, name=name,
        out_shape=(jax.ShapeDtypeStruct((Hh, LP, Dh), F32), jax.ShapeDtypeStruct((Hh, LP, Dh), F32),
                   jax.ShapeDtypeStruct((Hh, LP, Dh), F32), jax.ShapeDtypeStruct((Hh, LP, 1), F32),
                   jax.ShapeDtypeStruct((Hh, nb, 1, ATT_BLK), F32)),
        grid=(Hh, nb), in_specs=[full, blk, blk, full, full, col, col, rowspec],
        out_specs=(full, blk, blk, col, pl.BlockSpec((None, None, 1, ATT_BLK), lambda h, j: (h, j, 0, 0))),
        scratch_shapes=[pltpu.VMEM((LP, 1), F32)],
        compiler_params=_params(("parallel", "arbitrary")))(q, k, v, o, do, lse, c_col, c_row)


AUG = 80
ONES_IN_K = HEAD_DIM
ONES_IN_Q = HEAD_DIM + 3
ATT_HEADS_PER_STEP = 8
ATT_HEADS_PER_STEP_BWD = 4


def _split3(c):
    rnd = lambda t: lax.reduce_precision(t, exponent_bits=8, mantissa_bits=7)
    hi = rnd(c)
    r1 = c - hi
    mid = rnd(r1)
    lo = rnd(r1 - mid)
    return hi.astype(BF16), mid.astype(BF16), lo.astype(BF16)


def _blocks_t(t, LP):
    Hh, _, Rr = t.shape
    return t.reshape(Hh, LP // ATT_BLK, ATT_BLK, Rr).transpose(0, 1, 3, 2)


def _attn_operands(q, k, v, cT, LP):
    hd = lambda t: t.reshape(LP, HEADS, HEAD_DIM).transpose(1, 0, 2)
    parts = [p[:, :, None] for p in _split3(cT)]
    ones = jnp.ones((HEADS, LP, 3), BF16)
    zeros = jnp.zeros((HEADS, LP, AUG - HEAD_DIM - 6), BF16)
    q_aug = jnp.concatenate([hd((q * (HEAD_DIM ** -0.5)).astype(BF16))] + parts + [ones, zeros], axis=2)
    k_aug = jnp.concatenate([hd(k.astype(BF16)), ones] + [-p for p in parts] + [zeros], axis=2)
    vh = hd(v.astype(BF16))
    return _blocks_t(q_aug, LP), _blocks_t(k_aug, LP), k_aug, _blocks_t(vh, LP), vh


def _attn_fwd2(qT, k_aug, vT, name):
    Hh, nb, _, _ = qT.shape
    LP = nb * ATT_BLK
    Dh = vT.shape[2]

    HB = ATT_HEADS_PER_STEP

    def body(q_ref, k_ref, v_ref, o_ref, lse_ref):
        keys = lax.broadcasted_iota(jnp.int32, (ATT_BLK, ATT_BLK), 0)
        qrys = lax.broadcasted_iota(jnp.int32, (ATT_BLK, ATT_BLK), 1)
        causal = keys <= qrys

        def q_block(i, _):
            def tile(j, carry, masked):
                ks = pl.ds(pl.multiple_of(j * ATT_BLK, ATT_BLK), ATT_BLK)
                out = []
                for hh in range(HB):
                    m, l, acc = carry[hh]
                    s = jnp.dot(k_ref[hh, ks, :], q_ref[hh, i], preferred_element_type=F32)
                    if masked:
                        s = jnp.where(causal, s, NEG)
                    m_new = jnp.maximum(m, jnp.max(s, axis=0, keepdims=True))
                    p = jnp.exp(s - m_new)
                    alpha = jnp.exp(m - m_new)
                    l = alpha * l + jnp.sum(p, axis=0, keepdims=True)
                    acc = alpha * acc + jnp.dot(v_ref[hh, j], p.astype(BF16), preferred_element_type=F32)
                    out.append((m_new, l, acc))
                return tuple(out)

            init = tuple((jnp.full((1, ATT_BLK), NEG, F32), jnp.zeros((1, ATT_BLK), F32),
                          jnp.zeros((Dh, ATT_BLK), F32)) for _ in range(HB))
            carry = lax.fori_loop(0, i, lambda j, cr: tile(j, cr, False), init)
            carry = tile(i, carry, True)
            for hh in range(HB):
                m, l, acc = carry[hh]
                o_ref[hh, i] = acc / l
                lse_ref[hh, i] = m + jnp.log(l)
            return 0

        lax.fori_loop(0, nb, q_block, 0)

    blk = lambda r: pl.BlockSpec((HB, nb, r, ATT_BLK), lambda h: (h, 0, 0, 0))
    return pl.pallas_call(
        body, name=name,
        out_shape=(jax.ShapeDtypeStruct((Hh, nb, Dh, ATT_BLK), F32), jax.ShapeDtypeStruct((Hh, nb, 1, ATT_BLK), F32)),
        grid=(Hh // HB,), in_specs=[blk(AUG), pl.BlockSpec((HB, LP, AUG), lambda h: (h, 0, 0)), blk(Dh)],
        out_specs=(blk(Dh), blk(1)),
        compiler_params=_params(("parallel",)))(qT, k_aug, vT)


def _attn_bwd2(qT, kT, k_aug, v, oT, doT, lse, name):
    Hh, nb, _, _ = qT.shape
    LP = nb * ATT_BLK
    Dh = v.shape[2]
    nt = (((1,), (1,)), ((), ()))

    HB = ATT_HEADS_PER_STEP_BWD

    def body(q_ref, kt_ref, k_ref, v_ref, o_ref, do_ref, lse_ref, dq_ref, dk_ref, dv_ref, delta_ref):
        keys = lax.broadcasted_iota(jnp.int32, (ATT_BLK, ATT_BLK), 0)
        qrys = lax.broadcasted_iota(jnp.int32, (ATT_BLK, ATT_BLK), 1)
        causal = keys <= qrys

        def prep(i, _):
            for hh in range(HB):
                delta_ref[hh, i] = jnp.sum(do_ref[hh, i].astype(F32) * o_ref[hh, i], axis=0, keepdims=True)
                dq_ref[hh, i] = jnp.zeros((AUG, ATT_BLK), F32)
            return 0

        lax.fori_loop(0, nb, prep, 0)

        def kv_block(j, _):
            ks = pl.ds(pl.multiple_of(j * ATT_BLK, ATT_BLK), ATT_BLK)

            def tile(i, carry, masked):
                out = []
                for hh in range(HB):
                    dk, dv = carry[hh]
                    qb = q_ref[hh, i]
                    dob = do_ref[hh, i]
                    s = jnp.dot(k_ref[hh, ks, :], qb, preferred_element_type=F32)
                    if masked:
                        s = jnp.where(causal, s, NEG)
                    p = jnp.exp(s - lse_ref[hh, i])
                    dp = jnp.dot(v_ref[hh, ks, :], dob, preferred_element_type=F32)
                    ds = (p * (dp - delta_ref[hh, i])).astype(BF16)
                    dv = dv + lax.dot_general(p.astype(BF16), dob, nt, preferred_element_type=F32)
                    dk = dk + lax.dot_general(ds, qb, nt, preferred_element_type=F32)
                    dq_ref[hh, i] += jnp.dot(kt_ref[hh, j], ds, preferred_element_type=F32)
                    out.append((dk, dv))
                return tuple(out)

            init = tuple((jnp.zeros((ATT_BLK, AUG), F32), jnp.zeros((ATT_BLK, Dh), F32)) for _ in range(HB))
            carry = tile(j, init, True)
            carry = lax.fori_loop(j + 1, nb, lambda i, cr: tile(i, cr, False), carry)
            for hh in range(HB):
                dk_ref[hh, ks, :] = carry[hh][0]
                dv_ref[hh, ks, :] = carry[hh][1]
            return 0

        lax.fori_loop(0, nb, kv_block, 0)

    blk = lambda r: pl.BlockSpec((HB, nb, r, ATT_BLK), lambda h: (h, 0, 0, 0))
    row = lambda cols: pl.BlockSpec((HB, LP, cols), lambda h: (h, 0, 0))
    return pl.pallas_call(
        body, name=name,
        out_shape=(jax.ShapeDtypeStruct((Hh, nb, AUG, ATT_BLK), F32), jax.ShapeDtypeStruct((Hh, LP, AUG), F32),
                   jax.ShapeDtypeStruct((Hh, LP, Dh), F32)),
        grid=(Hh // HB,), in_specs=[blk(AUG), blk(AUG), row(AUG), row(Dh), blk(Dh), blk(Dh), blk(1)],
        out_specs=(blk(AUG), row(AUG), row(Dh)),
        scratch_shapes=[pltpu.VMEM((HB, nb, 1, ATT_BLK), F32)],
        compiler_params=_params(("parallel",)))(qT, kT, k_aug, v, oT, doT, lse)


CONV_HALO = 32
A_BLK = 3 * FOX_W // CONV_CH
G_BLK = A_BLK + 1


def _conf_fwd(proj, cw, cb, lg, lb, tm, name):
    LP = proj.shape[0]
    C = CONV_CH
    sub = _sub_rows(tm)
    hpb = tm // CONV_HALO

    def body(a_ref, g_ref, ah_ref, gh_ref, w_ref, cb_ref, lg_ref, lb_ref, u1_ref, u_ref, buf):
        r = pl.program_id(0)
        buf[CONV_HALO:CONV_HALO + tm, :] = a_ref[...] * _sigmoid(g_ref[...])
        buf[0:CONV_HALO, :] = jnp.where(r > 0, ah_ref[...] * _sigmoid(gh_ref[...]), 0.0)
        for s in range(tm // sub):
            for ct in range(C // LANE):
                ln = slice(ct * LANE, (ct + 1) * LANE)
                acc = jnp.broadcast_to(cb_ref[:, ln], (sub, LANE))
                for kk in range(CONV_K):
                    off = CONV_HALO + s * sub - (CONV_K - 1) + kk
                    acc = acc + w_ref[kk:kk + 1, ln] * buf[off:off + sub, ln]
                u1_ref[s * sub:(s + 1) * sub, ln] = acc
        u1 = u1_ref[...]
        mu = jnp.mean(u1, axis=1, keepdims=True)
        xc = u1 - mu
        var = jnp.mean(xc * xc, axis=1, keepdims=True)
        y = xc * lax.rsqrt(var + LN_EPS) * lg_ref[...] + lb_ref[...]
        u_ref[...] = (y * _sigmoid(y)).astype(u_ref.dtype)

    cur = lambda blk: pl.BlockSpec((tm, C), lambda r: (r, blk))
    halo = lambda blk: pl.BlockSpec((CONV_HALO, C), lambda r: (jnp.maximum(r * hpb - 1, 0), blk))
    vec = pl.BlockSpec((1, C), lambda r: (0, 0))
    out = pl.BlockSpec((tm, C), lambda r: (r, 0))
    return pl.pallas_call(
        body, name=name,
        out_shape=(jax.ShapeDtypeStruct((LP, C), F32), jax.ShapeDtypeStruct((LP, C), BF16)),
        grid=(LP // tm,),
        in_specs=[cur(A_BLK), cur(G_BLK), halo(A_BLK), halo(G_BLK),
                  pl.BlockSpec((CONV_HALO, C), lambda r: (0, 0)), vec, vec, vec],
        out_specs=(out, out),
        scratch_shapes=[pltpu.VMEM((CONV_HALO + tm, C), F32)],
        compiler_params=_params(("parallel",)))(proj, proj, proj, proj, cw, cb, lg, lb)


def _conf_bwd(proj, u1, dcat, cw, lg, lb, tm, name):
    LP = proj.shape[0]
    C = CONV_CH
    sub = _sub_rows(tm)
    hpb = tm // CONV_HALO
    nblk = LP // tm
    last_halo = LP // CONV_HALO - 1

    def body(a_ref, g_ref, ah_ref, gh_ref, u1_ref, u1n_ref, du_ref, dun_ref, w_ref, lg_ref, lb_ref,
             dadg_ref, dw_ref, dcb_ref, dlg_ref, dlb_ref, ubuf, dbuf, du0):
        r = pl.program_id(0)
        lgv = lg_ref[...]
        lbv = lb_ref[...]

        def ln_silu_bwd(u1v, duv):
            mu = jnp.mean(u1v, axis=1, keepdims=True)
            xc = u1v - mu
            rstd = lax.rsqrt(jnp.mean(xc * xc, axis=1, keepdims=True) + LN_EPS)
            xhat = xc * rstd
            y = xhat * lgv + lbv
            sg = _sigmoid(y)
            dy = duv * (sg * (1.0 + y * (1.0 - sg)))
            dxh = dy * lgv
            du1 = rstd * (dxh - jnp.mean(dxh, axis=1, keepdims=True)
                          - xhat * jnp.mean(dxh * xhat, axis=1, keepdims=True))
            return du1, dy, xhat

        @pl.when(r == 0)
        def _():
            dw_ref[...] = jnp.zeros_like(dw_ref)
            dcb_ref[...] = jnp.zeros_like(dcb_ref)
            dlg_ref[...] = jnp.zeros_like(dlg_ref)
            dlb_ref[...] = jnp.zeros_like(dlb_ref)

        du1, dy, xhat = ln_silu_bwd(u1_ref[...], du_ref[...])
        dlg_ref[...] += jnp.sum(dy * xhat, axis=0, keepdims=True)
        dlb_ref[...] += jnp.sum(dy, axis=0, keepdims=True)
        dcb_ref[...] += jnp.sum(du1, axis=0, keepdims=True)
        dbuf[0:tm, :] = du1
        du1n, _, _ = ln_silu_bwd(u1n_ref[...], dun_ref[...])
        dbuf[tm:tm + CONV_HALO, :] = jnp.where(r < nblk - 1, du1n, 0.0)
        ubuf[CONV_HALO:CONV_HALO + tm, :] = a_ref[...] * _sigmoid(g_ref[...])
        ubuf[0:CONV_HALO, :] = jnp.where(r > 0, ah_ref[...] * _sigmoid(gh_ref[...]), 0.0)

        for ct in range(C // LANE):
            ln = slice(ct * LANE, (ct + 1) * LANE)
            for s in range(tm // sub):
                d_here = dbuf[s * sub:(s + 1) * sub, ln]
                acc = jnp.zeros((sub, LANE), F32)
                for kk in range(CONV_K):
                    fo = s * sub + (CONV_K - 1) - kk
                    acc = acc + w_ref[kk:kk + 1, ln] * dbuf[fo:fo + sub, ln]
                    bo = CONV_HALO + s * sub - (CONV_K - 1) + kk
                    dw_ref[kk:kk + 1, ln] += jnp.sum(d_here * ubuf[bo:bo + sub, ln], axis=0, keepdims=True)
                du0[s * sub:(s + 1) * sub, ln] = acc
        a = a_ref[...]
        sg = _sigmoid(g_ref[...])
        d0 = du0[...]
        dadg_ref[:, 0:C] = (d0 * sg).astype(dadg_ref.dtype)
        dadg_ref[:, C:2 * C] = (d0 * a * sg * (1.0 - sg)).astype(dadg_ref.dtype)

    cur = lambda blk: pl.BlockSpec((tm, C), lambda r: (r, blk))
    prev = lambda blk: pl.BlockSpec((CONV_HALO, C), lambda r: (jnp.maximum(r * hpb - 1, 0), blk))
    nxt = lambda blk: pl.BlockSpec((CONV_HALO, C), lambda r: (jnp.minimum((r + 1) * hpb, last_halo), blk))
    vec = pl.BlockSpec((1, C), lambda r: (0, 0))
    wspec = pl.BlockSpec((CONV_HALO, C), lambda r: (0, 0))
    return pl.pallas_call(
        body, name=name,
        out_shape=(jax.ShapeDtypeStruct((LP, 2 * C), BF16), jax.ShapeDtypeStruct((CONV_HALO, C), F32),
                   jax.ShapeDtypeStruct((1, C), F32), jax.ShapeDtypeStruct((1, C), F32),
                   jax.ShapeDtypeStruct((1, C), F32)),
        grid=(nblk,),
        in_specs=[cur(A_BLK), cur(G_BLK), prev(A_BLK), prev(G_BLK), cur(0), nxt(0), cur(1), nxt(1),
                  wspec, vec, vec],
        out_specs=(pl.BlockSpec((tm, 2 * C), lambda r: (r, 0)), wspec, vec, vec, vec),
        scratch_shapes=[pltpu.VMEM((CONV_HALO + tm, C), F32), pltpu.VMEM((tm + CONV_HALO, C), F32),
                        pltpu.VMEM((tm, C), F32)],
        compiler_params=_params(("arbitrary",)))(proj, proj, proj, proj, u1, u1, dcat, dcat, cw, lg, lb)


FFN_HALO = 8
FFN_TC = 256
FFN_K = 3


def _ffn_conv(buf, w_ref, b_ref, s, sub, ln):
    acc = jnp.broadcast_to(b_ref[:, ln], (sub, LANE))
    for kk in range(FFN_K):
        off = FFN_HALO + s * sub - (FFN_K - 1) + kk
        acc = acc + w_ref[kk:kk + 1, ln] * buf[off:off + sub, ln]
    return acc


def _ffn_act_fwd(up, w, b, tm, name):
    LP, F = up.shape[0], up.shape[1] // 2
    upg = upv = up
    nct = F // FFN_TC
    sub = _sub_rows(tm)
    hpb = tm // FFN_HALO

    def body(g_ref, v_ref, gh_ref, vh_ref, wg_ref, wv_ref, bg_ref, bv_ref, act_ref, gbuf, vbuf):
        r = pl.program_id(1)
        gbuf[FFN_HALO:FFN_HALO + tm, :] = g_ref[...]
        vbuf[FFN_HALO:FFN_HALO + tm, :] = v_ref[...]
        gbuf[0:FFN_HALO, :] = jnp.where(r > 0, gh_ref[...], 0.0)
        vbuf[0:FFN_HALO, :] = jnp.where(r > 0, vh_ref[...], 0.0)
        for s in range(tm // sub):
            for ct in range(FFN_TC // LANE):
                ln = slice(ct * LANE, (ct + 1) * LANE)
                gc = _ffn_conv(gbuf, wg_ref, bg_ref, s, sub, ln)
                vc = _ffn_conv(vbuf, wv_ref, bv_ref, s, sub, ln)
                act_ref[s * sub:(s + 1) * sub, ln] = (gc * _sigmoid(gc) * vc).astype(act_ref.dtype)

    cur = pl.BlockSpec((tm, FFN_TC), lambda c, r: (r, c))
    halo = pl.BlockSpec((FFN_HALO, FFN_TC), lambda c, r: (jnp.maximum(r * hpb - 1, 0), c))
    wg = pl.BlockSpec((8, FFN_TC), lambda c, r: (0, c))
    wv = pl.BlockSpec((8, FFN_TC), lambda c, r: (0, nct + c))
    bg = pl.BlockSpec((1, FFN_TC), lambda c, r: (0, c))
    bv = pl.BlockSpec((1, FFN_TC), lambda c, r: (0, nct + c))
    curv = pl.BlockSpec((tm, FFN_TC), lambda c, r: (r, nct + c))
    halov = pl.BlockSpec((FFN_HALO, FFN_TC), lambda c, r: (jnp.maximum(r * hpb - 1, 0), nct + c))
    return pl.pallas_call(
        body, name=name, out_shape=jax.ShapeDtypeStruct((LP, F), BF16), grid=(nct, LP // tm),
        in_specs=[cur, curv, halo, halov, wg, wv, bg, bv], out_specs=cur,
        scratch_shapes=[pltpu.VMEM((FFN_HALO + tm, FFN_TC), F32)] * 2,
        compiler_params=_params(("parallel", "parallel")))(upg, upv, upg, upv, w, w, b, b)


def _ffn_act_bwd(up, dact, w, b, tm, name):
    LP, F = up.shape[0], up.shape[1] // 2
    upg = upv = up
    nct = F // FFN_TC
    sub = _sub_rows(tm)
    hpb = tm // FFN_HALO
    nblk = LP // tm
    last_halo = LP // FFN_HALO - 1
    TB = tm + 2 * FFN_HALO

    def body(g_ref, v_ref, gp_ref, vp_ref, gn_ref, vn_ref, da_ref, dan_ref,
             wg_ref, wv_ref, bg_ref, bv_ref,
             dup_ref, dwg_ref, dwv_ref, dbg_ref, dbv_ref, gbuf, vbuf, dgb, dvb):
        r = pl.program_id(1)
        dg_ref = dup_ref.at[0]
        dv_ref = dup_ref.at[1]
        first = r == 0
        last = r == nblk - 1

        @pl.when(first)
        def _():
            dwg_ref[...] = jnp.zeros_like(dwg_ref)
            dwv_ref[...] = jnp.zeros_like(dwv_ref)
            dbg_ref[...] = jnp.zeros_like(dbg_ref)
            dbv_ref[...] = jnp.zeros_like(dbv_ref)

        for buf, c_ref, p_ref, n_ref in ((gbuf, g_ref, gp_ref, gn_ref), (vbuf, v_ref, vp_ref, vn_ref)):
            buf[0:FFN_HALO, :] = jnp.where(first, 0.0, p_ref[...])
            buf[FFN_HALO:FFN_HALO + tm, :] = c_ref[...]
            buf[FFN_HALO + tm:TB, :] = jnp.where(last, 0.0, n_ref[...])

        def dconv(s0, nrows, ln, dact_v):
            gc = jnp.broadcast_to(bg_ref[:, ln], (nrows, LANE))
            vc = jnp.broadcast_to(bv_ref[:, ln], (nrows, LANE))
            for kk in range(FFN_K):
                off = s0 - (FFN_K - 1) + kk
                gc = gc + wg_ref[kk:kk + 1, ln] * gbuf[off:off + nrows, ln]
                vc = vc + wv_ref[kk:kk + 1, ln] * vbuf[off:off + nrows, ln]
            sg = _sigmoid(gc)
            return dact_v * vc * (sg * (1.0 + gc * (1.0 - sg))), dact_v * (gc * sg)

        for ct in range(FFN_TC // LANE):
            ln = slice(ct * LANE, (ct + 1) * LANE)
            for s in range(tm // sub):
                dgc, dvc = dconv(FFN_HALO + s * sub, sub, ln, da_ref[s * sub:(s + 1) * sub, ln])
                dgb[s * sub:(s + 1) * sub, ln] = dgc
                dvb[s * sub:(s + 1) * sub, ln] = dvc
            dgc, dvc = dconv(FFN_HALO + tm, FFN_HALO, ln, jnp.where(last, 0.0, dan_ref[:, ln]))
            dgb[tm:tm + FFN_HALO, ln] = dgc
            dvb[tm:tm + FFN_HALO, ln] = dvc
            for dbuf, ubuf, w_ref, dw_ref, db_ref, dout in (
                    (dgb, gbuf, wg_ref, dwg_ref, dbg_ref, dg_ref), (dvb, vbuf, wv_ref, dwv_ref, dbv_ref, dv_ref)):
                for s in range(tm // sub):
                    d_here = dbuf[s * sub:(s + 1) * sub, ln]
                    acc = jnp.zeros((sub, LANE), F32)
                    for kk in range(FFN_K):
                        fo = s * sub + (FFN_K - 1) - kk
                        acc = acc + w_ref[kk:kk + 1, ln] * dbuf[fo:fo + sub, ln]
                        bo = FFN_HALO + s * sub - (FFN_K - 1) + kk
                        dw_ref[kk:kk + 1, ln] += jnp.sum(d_here * ubuf[bo:bo + sub, ln], axis=0, keepdims=True)
                    db_ref[:, ln] += jnp.sum(d_here, axis=0, keepdims=True)
                    dout[s * sub:(s + 1) * sub, ln] = acc.astype(dout.dtype)

    cur = pl.BlockSpec((tm, FFN_TC), lambda c, r: (r, c))
    prev = pl.BlockSpec((FFN_HALO, FFN_TC), lambda c, r: (jnp.maximum(r * hpb - 1, 0), c))
    nxt = pl.BlockSpec((FFN_HALO, FFN_TC), lambda c, r: (jnp.minimum((r + 1) * hpb, last_halo), c))
    wg = pl.BlockSpec((8, FFN_TC), lambda c, r: (0, c))
    wv = pl.BlockSpec((8, FFN_TC), lambda c, r: (0, nct + c))
    bg = pl.BlockSpec((1, FFN_TC), lambda c, r: (0, c))
    bv = pl.BlockSpec((1, FFN_TC), lambda c, r: (0, nct + c))
    curv = pl.BlockSpec((tm, FFN_TC), lambda c, r: (r, nct + c))
    prevv = pl.BlockSpec((FFN_HALO, FFN_TC), lambda c, r: (jnp.maximum(r * hpb - 1, 0), nct + c))
    nxtv = pl.BlockSpec((FFN_HALO, FFN_TC), lambda c, r: (jnp.minimum((r + 1) * hpb, last_halo), nct + c))
    dup, dwg, dwv, dbg, dbv = pl.pallas_call(
        body, name=name,
        out_shape=(jax.ShapeDtypeStruct((2, LP, F), BF16),
                   jax.ShapeDtypeStruct((8, F), F32), jax.ShapeDtypeStruct((8, F), F32),
                   jax.ShapeDtypeStruct((1, F), F32), jax.ShapeDtypeStruct((1, F), F32)),
        grid=(nct, nblk),
        in_specs=[cur, curv, prev, prevv, nxt, nxtv, cur, nxt, wg, wv, bg, bv],
        out_specs=(pl.BlockSpec((2, tm, FFN_TC), lambda c, r: (0, r, c)),
                   pl.BlockSpec((8, FFN_TC), lambda c, r: (0, c)),
                   pl.BlockSpec((8, FFN_TC), lambda c, r: (0, c)),
                   pl.BlockSpec((1, FFN_TC), lambda c, r: (0, c)),
                   pl.BlockSpec((1, FFN_TC), lambda c, r: (0, c))),
        scratch_shapes=[pltpu.VMEM((TB, FFN_TC), F32), pltpu.VMEM((TB, FFN_TC), F32),
                        pltpu.VMEM((tm + FFN_HALO, FFN_TC), F32), pltpu.VMEM((tm + FFN_HALO, FFN_TC), F32)],
        compiler_params=_params(("parallel", "arbitrary")))(
            upg, upv, upg, upv, upg, upv, dact, dact, w, w, b, b)
    return dup, jnp.concatenate([dwg, dwv], axis=1), jnp.concatenate([dbg, dbv], axis=1)


POOL_HALO = 16


def _pool_fwd(h, g, pw, pb, ps, tm, name):
    LP, Dm = h.shape
    sub = _sub_rows(tm)
    hpb = tm // POOL_HALO

    def body(h_ref, hh_ref, g_ref, pw_ref, pb_ref, ps_ref, o_ref, d_ref, buf):
        r = pl.program_id(0)
        gg = g_ref[...]

        def norm(x):
            return x * lax.rsqrt(jnp.mean(x * x, axis=1, keepdims=True) + RMS_EPS) * gg

        x = h_ref[...]
        buf[POOL_HALO:POOL_HALO + tm, :] = norm(x)
        buf[0:POOL_HALO, :] = jnp.where(r > 0, norm(hh_ref[...]), 0.0)
        for gi, w in enumerate(POOL_WINDOWS):
            ln = slice(gi * POOL_G, (gi + 1) * POOL_G)
            for s in range(tm // sub):
                base = POOL_HALO + s * sub
                acc = buf[base:base + sub, ln]
                for jj in range(1, w):
                    acc = acc + buf[base - jj:base - jj + sub, ln]
                t = r * tm + s * sub + lax.broadcasted_iota(jnp.int32, (sub, 1), 0)
                cnt = jnp.minimum(t + 1, w).astype(F32)
                d_ref[s * sub:(s + 1) * sub, ln] = (acc / cnt - buf[base:base + sub, ln]).astype(d_ref.dtype)
            y = jnp.dot(d_ref[:, ln], pw_ref[gi], preferred_element_type=F32) + pb_ref[:, ln]
            o_ref[:, ln] = x[:, ln] + y * ps_ref[:, ln]

    row = pl.BlockSpec((tm, Dm), lambda r: (r, 0))
    halo = pl.BlockSpec((POOL_HALO, Dm), lambda r: (jnp.maximum(r * hpb - 1, 0), 0))
    vec = pl.BlockSpec((1, Dm), lambda r: (0, 0))
    wsp = pl.BlockSpec((len(POOL_WINDOWS), POOL_G, POOL_G), lambda r: (0, 0, 0))
    return pl.pallas_call(
        body, name=name,
        out_shape=(jax.ShapeDtypeStruct((LP, Dm), F32), jax.ShapeDtypeStruct((LP, Dm), BF16)),
        grid=(LP // tm,), in_specs=[row, halo, vec, wsp, vec, vec], out_specs=(row, row),
        scratch_shapes=[pltpu.VMEM((POOL_HALO + tm, Dm), F32)],
        compiler_params=_params(("parallel",)))(h, h, g, pw, pb, ps)


def _pool_bwd(h, g, d, pw, pb, ps, dh_out, tm, name):
    LP, Dm = h.shape
    sub = _sub_rows(tm)
    hpb = tm // POOL_HALO
    nblk = LP // tm
    last_halo = LP // POOL_HALO - 1
    nt = (((1,), (1,)), ((), ()))
    tn = (((0,), (0,)), ((), ()))

    def body(h_ref, g_ref, d_ref, pw_ref, pb_ref, ps_ref, do_ref, don_ref,
             dh_ref, dpw_ref, dpb_ref, dps_ref, dg_ref, ebuf, ddb, dnb):
        r = pl.program_id(0)

        @pl.when(r == 0)
        def _():
            dpw_ref[...] = jnp.zeros_like(dpw_ref)
            dpb_ref[...] = jnp.zeros_like(dpb_ref)
            dps_ref[...] = jnp.zeros_like(dps_ref)
            dg_ref[...] = jnp.zeros_like(dg_ref)

        for gi, w in enumerate(POOL_WINDOWS):
            ln = slice(gi * POOL_G, (gi + 1) * POOL_G)
            wg = pw_ref[gi]
            dog = do_ref[:, ln]
            dg_b = d_ref[:, ln]
            y_pre = jnp.dot(dg_b, wg, preferred_element_type=F32) + pb_ref[:, ln]
            dps_ref[:, ln] += jnp.sum(dog * y_pre, axis=0, keepdims=True)
            dy = dog * ps_ref[:, ln]
            dpb_ref[:, ln] += jnp.sum(dy, axis=0, keepdims=True)
            dyb = dy.astype(BF16)
            dpw_ref[gi] += lax.dot_general(dg_b, dyb, tn, preferred_element_type=F32)
            dd = lax.dot_general(dyb, wg, nt, preferred_element_type=F32)
            ddb[:, ln] = dd
            t = r * tm + lax.broadcasted_iota(jnp.int32, (tm, 1), 0)
            ebuf[0:tm, ln] = dd / jnp.minimum(t + 1, w).astype(F32)
            dyn = (don_ref[:, ln] * ps_ref[:, ln]).astype(BF16)
            ddn = lax.dot_general(dyn, wg, nt, preferred_element_type=F32)
            tn_ = (r + 1) * tm + lax.broadcasted_iota(jnp.int32, (POOL_HALO, 1), 0)
            ebuf[tm:tm + POOL_HALO, ln] = jnp.where(r < nblk - 1, ddn / jnp.minimum(tn_ + 1, w).astype(F32), 0.0)
            for s in range(tm // sub):
                acc = ebuf[s * sub:(s + 1) * sub, ln]
                for jj in range(1, w):
                    acc = acc + ebuf[s * sub + jj:s * sub + jj + sub, ln]
                dnb[s * sub:(s + 1) * sub, ln] = acc - ddb[s * sub:(s + 1) * sub, ln]
        x = h_ref[...]
        rr = lax.rsqrt(jnp.mean(x * x, axis=1, keepdims=True) + RMS_EPS)
        xhat = x * rr
        dn = dnb[...]
        dxh = dn * g_ref[...]
        dh_ref[...] = do_ref[...] + rr * (dxh - xhat * jnp.mean(dxh * xhat, axis=1, keepdims=True))
        dg_ref[...] += jnp.sum(dn * xhat, axis=0, keepdims=True)

    row = pl.BlockSpec((tm, Dm), lambda r: (r, 0))
    nxt = pl.BlockSpec((POOL_HALO, Dm), lambda r: (jnp.minimum((r + 1) * hpb, last_halo), 0))
    vec = pl.BlockSpec((1, Dm), lambda r: (0, 0))
    wsp = pl.BlockSpec((len(POOL_WINDOWS), POOL_G, POOL_G), lambda r: (0, 0, 0))
    return pl.pallas_call(
        body, name=name,
        out_shape=(jax.ShapeDtypeStruct((LP, Dm), F32),
                   jax.ShapeDtypeStruct((len(POOL_WINDOWS), POOL_G, POOL_G), F32),
                   jax.ShapeDtypeStruct((1, Dm), F32), jax.ShapeDtypeStruct((1, Dm), F32),
                   jax.ShapeDtypeStruct((1, Dm), F32)),
        grid=(nblk,), in_specs=[row, vec, row, wsp, vec, vec, row, nxt],
        out_specs=(row, wsp, vec, vec, vec),
        scratch_shapes=[pltpu.VMEM((tm + POOL_HALO, Dm), F32), pltpu.VMEM((tm, Dm), F32),
                        pltpu.VMEM((tm, Dm), F32)],
        compiler_params=_params(("arbitrary",)))(h, g, d, pw, pb, ps, dh_out, dh_out)


def _adamw(w, g, m, v, name):
    shape = w.shape
    cols = shape[-1]
    rows = int(np.prod(shape[:-1])) if len(shape) > 1 else 1
    w2, g2, m2, v2 = (t.reshape(rows, cols) for t in (w, g, m, v))
    tr = rows
    for cand in (256, 128, 64, 32, 16, 8):
        if rows % cand == 0 and rows > cand:
            tr = cand
            break
    c1 = float(1.0 - ADAM_B1 ** ADAM_STEP)
    c2 = float(1.0 - ADAM_B2 ** ADAM_STEP)

    def body(w_ref, g_ref, m_ref, v_ref, d_ref, mo_ref, vo_ref):
        gg = g_ref[...]
        mn = ADAM_B1 * m_ref[...] + (1.0 - ADAM_B1) * gg
        vn = ADAM_B2 * v_ref[...] + (1.0 - ADAM_B2) * (gg * gg)
        m_hat = mn / c1
        v_hat = vn / c2
        d_ref[...] = -ADAM_LR * (m_hat / (jnp.sqrt(v_hat) + ADAM_EPS) + ADAM_WD * w_ref[...])
        mo_ref[...] = mn
        vo_ref[...] = vn

    spec = pl.BlockSpec((tr, cols), lambda i: (i, 0))
    sds = jax.ShapeDtypeStruct((rows, cols), F32)
    d2, mo, vo = pl.pallas_call(
        body, name=name, out_shape=(sds, sds, sds), grid=(rows // tr,),
        in_specs=[spec] * 4, out_specs=(spec,) * 3,
        compiler_params=_params(("parallel",)))(w2, g2, m2, v2)
    return d2.reshape(shape), mo.reshape(shape), vo.reshape(shape)


def _row_tiles(LP):
    tm = LP // 4
    assert LP % 4 == 0 and tm % CONV_HALO == 0 and LP % ATT_BLK == 0, LP
    return tm, LP // 2


def _heads(t, LP):
    return t.reshape(LP, HEADS, HEAD_DIM).transpose(1, 0, 2)


def _unheads(t, LP):
    return t.transpose(1, 0, 2).reshape(LP, FOX_W)


def _ffn_fwd(h, gain, wug, wuv, cw, cb, wd, tm, tmm, tag):
    n = _rms_fwd(h, gain, BF16, tm, f"ffn_norm_{tag}")
    upg = _mm(n, wug, "nn", F32, tmm, 256, f"ffn_up_gate_{tag}")
    upv = _mm(n, wuv, "nn", F32, tmm, 256, f"ffn_up_val_{tag}")
    act = _ffn_act_fwd(upg, upv, cw, cb, tm, f"ffn_act_{tag}")
    out = _mm(act, wd, "nn", F32, tmm, 512, f"ffn_down_{tag}", add=h)
    return out, (n, upg, upv, act)


def _ffn_bwd(h, gain, wug, wuv, cw, cb, wd, saved, dout, tm, tmm, tag):
    n, upg, upv, act = saved
    dact = _mm(dout, wd, "nt", F32, tmm, 256, f"ffn_dact_{tag}")
    dwd = _mm(act, dout, "tn", F32, 256, 512, f"ffn_dwdown_{tag}")
    dupg, dupv, dcw, dcb = _ffn_act_bwd(upg, upv, dact, cw, cb, tm, f"ffn_act_bwd_{tag}")
    dn = _mm(dupg, wug, "nt", F32, tm, 512, f"ffn_dn_gate_{tag}")
    dn = _mm(dupv, wuv, "nt", F32, tm, 512, f"ffn_dn_val_{tag}", add=dn)
    dwug = _mm(n, dupg, "tn", F32, 512, 256, f"ffn_dwup_gate_{tag}")
    dwuv = _mm(n, dupv, "tn", F32, 512, 256, f"ffn_dwup_val_{tag}")
    dh, dgain = _rms_bwd(h, gain, dn, dout, tm, f"ffn_norm_bwd_{tag}")
    return dh, dict(gain=dgain, wug=dwug, wuv=dwuv, cw=dcw[:FFN_K], cb=dcb, wd=dwd)


def _local_step(h0, tgt, W, n_real):
    LP = h0.shape[0]
    tm, tmm = _row_tiles(LP)
    nb = LP // ATT_BLK
    G = {}

    n0 = _rms_fwd(h0, W["mix_norm_even"], BF16, tm, "mix_norm_even")
    proj = _mm(n0, W["w_in_p"], "nn", F32, tmm, 384, "in_proj")
    c = _fgate_fwd(proj, W["b_f_p"], "forget_gate")
    cT = c[:, :HEADS].T
    c_col = cT[:, :, None]
    c_row = cT.reshape(HEADS, nb, 1, ATT_BLK)
    qkv = proj[:, :3 * FOX_W].astype(BF16)
    q, k, v = (_heads(qkv[:, i * FOX_W:(i + 1) * FOX_W], LP) for i in range(3))
    o, lse = _attn_fwd(q, k, v, c_col, c_row, "fox_attention")
    u1, u = _conf_fwd(proj, W["conv_w_p"], W["conv_b"], W["ln_g"], W["ln_b"], tm, "conformer")
    cat = jnp.concatenate([_unheads(o, LP).astype(BF16), u], axis=1)
    h1 = _mm(cat, W["w_out"], "nn", F32, tmm, 512, "out_proj", add=h0)
    h2, ffn0 = _ffn_fwd(h1, W["ffn_norm"][0:1], W["w_up_g"][0], W["w_up_v"][0], W["ffn_conv_w_p"][0],
                        W["ffn_conv_b"][0:1], W["w_down"][0], tm, tmm, "0")
    h3, dpool = _pool_fwd(h2, W["mix_norm_odd"], W["pool_w"], W["pool_b"], W["pool_scale"], tm, "pool_mixer")
    h4, ffn1 = _ffn_fwd(h3, W["ffn_norm"][1:2], W["w_up_g"][1], W["w_up_v"][1], W["ffn_conv_w_p"][1],
                        W["ffn_conv_b"][1:2], W["w_down"][1], tm, tmm, "1")
    loss, dh4, G["final_norm"] = _loss_head(h4, W["final_norm"], tgt, n_real, tm, "loss_head")

    dh3, g1 = _ffn_bwd(h3, W["ffn_norm"][1:2], W["w_up_g"][1], W["w_up_v"][1], W["ffn_conv_w_p"][1],
                       W["ffn_conv_b"][1:2], W["w_down"][1], ffn1, dh4, tm, tmm, "1")
    dh2, G["pool_w"], G["pool_b"], G["pool_scale"], G["mix_norm_odd"] = _pool_bwd(
        h2, W["mix_norm_odd"], dpool, W["pool_w"], W["pool_b"], W["pool_scale"], dh3, tm, "pool_mixer_bwd")
    dh1, g0 = _ffn_bwd(h1, W["ffn_norm"][0:1], W["w_up_g"][0], W["w_up_v"][0], W["ffn_conv_w_p"][0],
                       W["ffn_conv_b"][0:1], W["w_down"][0], ffn0, dh2, tm, tmm, "0")
    for key in ("gain", "wug", "wuv", "cw", "cb", "wd"):
        G["ffn_" + key] = (g0[key], g1[key])

    dcat = _mm(dh1, W["w_out"], "nt", F32, tmm, 512, "out_proj_dx")
    G["w_out"] = _mm(cat, dh1, "tn", F32, 512, 512, "out_proj_dw")
    dadg, dcw, G["conv_b"], G["ln_g"], G["ln_b"] = _conf_bwd(
        proj, u1, dcat, W["conv_w_p"], W["ln_g"], W["ln_b"], tm, "conformer_bwd")
    G["conv_w"] = dcw[:CONV_K]
    do = _heads(dcat[:, :FOX_W], LP)
    dq, dk, dv, dcq, dck = _attn_bwd(q, k, v, o, do, lse, c_col, c_row, "fox_attention_bwd")
    dc = jnp.pad((dcq.reshape(HEADS, LP) + dck.reshape(HEADS, LP)).T, ((0, 0), (0, LANE - HEADS)))
    dfl, dbf = _fgate_bwd(proj, W["b_f_p"], dc, "forget_gate_bwd")
    G["b_f"] = dbf[:, :HEADS]
    dproj = jnp.concatenate([_unheads(t, LP).astype(BF16) for t in (dq, dk, dv)] + [dadg, dfl], axis=1)
    dn0 = _mm(dproj, W["w_in_p"], "nt", F32, tmm, 512, "in_proj_dx")
    G["w_in_p"] = _mm(n0, dproj, "tn", F32, 512, 384, "in_proj_dw")
    dh0, G["mix_norm_even"] = _rms_bwd(h0, W["mix_norm_even"], dn0, dh1, tm, "mix_norm_even_bwd")
    return loss, dh0, G


def _compute_layout(P):
    w_in = P["w_in"].reshape(D_MODEL, IN_COLS)
    qkv, f, ag = w_in[:, :3 * FOX_W], w_in[:, 3 * FOX_W:3 * FOX_W + HEADS], w_in[:, 3 * FOX_W + HEADS:]
    w_in_p = jnp.concatenate([qkv, ag, f, jnp.zeros((D_MODEL, LANE - HEADS), w_in.dtype)], axis=1).astype(BF16)
    w_up = P["w_up"].astype(BF16)
    return dict(
        mix_norm_even=P["mix_norm_even"].reshape(1, D_MODEL).astype(F32),
        w_in_p=w_in_p,
        b_f_p=jnp.pad(P["b_f"].reshape(1, HEADS).astype(F32), ((0, 0), (0, LANE - HEADS))),
        conv_w_p=jnp.pad(P["conv_w"].reshape(CONV_K, CONV_CH).astype(F32), ((0, CONV_HALO - CONV_K), (0, 0))),
        conv_b=P["conv_b"].reshape(1, CONV_CH).astype(F32),
        ln_g=P["ln_g"].reshape(1, CONV_CH).astype(F32),
        ln_b=P["ln_b"].reshape(1, CONV_CH).astype(F32),
        w_out=P["w_out"].reshape(D_MODEL, D_MODEL).astype(BF16),
        mix_norm_odd=P["mix_norm_odd"].reshape(1, D_MODEL).astype(F32),
        pool_w=P["pool_w"].reshape(len(POOL_WINDOWS), POOL_G, POOL_G).astype(BF16),
        pool_b=P["pool_b"].reshape(1, D_MODEL).astype(F32),
        pool_scale=P["pool_scale"].reshape(1, D_MODEL).astype(F32),
        ffn_norm=P["ffn_norm"].astype(F32),
        w_up_g=w_up[:, :, :D_FF],
        w_up_v=w_up[:, :, D_FF:],
        ffn_conv_w_p=jnp.pad(P["ffn_conv_w"].astype(F32), ((0, 0), (0, 8 - FFN_K), (0, 0))),
        ffn_conv_b=P["ffn_conv_b"].astype(F32),
        w_down=P["w_down"].astype(BF16),
        final_norm=P["final_norm"].reshape(1, D_MODEL).astype(F32),
    )


def _reference_layout(G, dh0):
    gp = G["w_in_p"]
    g_w_in = jnp.concatenate([gp[:, :3 * FOX_W], gp[:, 3 * FOX_W + 2 * CONV_CH:3 * FOX_W + 2 * CONV_CH + HEADS],
                              gp[:, 3 * FOX_W:3 * FOX_W + 2 * CONV_CH]], axis=1)
    return dict(
        meta_tokens=dh0[:N_META],
        mix_norm_even=G["mix_norm_even"],
        w_in=g_w_in[None],
        b_f=G["b_f"],
        conv_w=G["conv_w"][None],
        conv_b=G["conv_b"],
        ln_g=G["ln_g"],
        ln_b=G["ln_b"],
        w_out=G["w_out"][None],
        mix_norm_odd=G["mix_norm_odd"],
        pool_w=G["pool_w"][None],
        pool_b=G["pool_b"].reshape(1, len(POOL_WINDOWS), POOL_G),
        pool_scale=G["pool_scale"],
        ffn_norm=jnp.concatenate(G["ffn_gain"], axis=0),
        w_up=jnp.stack([jnp.concatenate([g, v], axis=1) for g, v in zip(G["ffn_wug"], G["ffn_wuv"])]),
        ffn_conv_w=jnp.stack(G["ffn_cw"]),
        ffn_conv_b=jnp.concatenate(G["ffn_cb"], axis=0),
        w_down=jnp.stack(G["ffn_wd"]),
        final_norm=G["final_norm"].reshape(D_MODEL),
    )


MESH = pl.DeviceIdType.MESH
ANY = pl.BlockSpec(memory_space=pl.ANY)
PACK_COLS = 1024


def _coords():
    return lax.axis_index("x"), lax.axis_index("y"), lax.axis_index("c")


def _other_chips(x, y):
    return [(1 - x, y), (x, 1 - y), (1 - x, 1 - y)]


def _allgather_chips(pack):
    R, C = pack.shape
    R2 = R // 2

    def body(x_ref, o_ref, send_sems, recv_sems, local_sem):
        x, y, c = _coords()
        sibling = (x, y, 1 - c)
        chips = _other_chips(x, y)

        def slot(px, py, half):
            return o_ref.at[2 * px + py, pl.ds(half * R2, R2), :]

        def copy(k, src, dst, to):
            return pltpu.make_async_remote_copy(src_ref=src, dst_ref=dst, send_sem=send_sems.at[k],
                                                recv_sem=recv_sems.at[k], device_id=to, device_id_type=MESH)

        mine = pltpu.make_async_copy(x_ref, o_ref.at[2 * x + y], local_sem)
        mine.start()
        my_half = x_ref.at[pl.ds(c * R2, R2), :]
        first = [copy(j, my_half, slot(x, y, c), (*chip, c)) for j, chip in enumerate(chips)]
        for cp in first:
            cp.start()
        passed = [copy(3 + j, slot(*chip, c), slot(*chip, c), sibling) for j, chip in enumerate(chips)]
        for j, chip in enumerate(chips):
            copy(j, my_half, slot(*chip, c), sibling).wait_recv()
            passed[j].start()
        for j, chip in enumerate(chips):
            copy(3 + j, my_half, slot(*chip, 1 - c), sibling).wait_recv()
        for cp in first + passed:
            cp.wait_send()
        mine.wait()

    return pl.pallas_call(
        body, name="allgather_weights", out_shape=jax.ShapeDtypeStruct((N_CHIPS, R, C), pack.dtype),
        in_specs=[ANY], out_specs=ANY,
        scratch_shapes=[pltpu.SemaphoreType.DMA((6,)), pltpu.SemaphoreType.DMA((6,)), pltpu.SemaphoreType.DMA],
    )(pack)


def _pair_exchange(G):
    n, R, C = G.shape
    R2 = R // 2

    def body(g_ref, o_ref, send_sem, recv_sem):
        x, y, c = _coords()
        src = g_ref.at[pl.ds(0, n), pl.ds((1 - c) * R2, R2), :]
        cp = pltpu.make_async_remote_copy(src_ref=src, dst_ref=o_ref, send_sem=send_sem, recv_sem=recv_sem,
                                          device_id=(x, y, 1 - c), device_id_type=MESH)
        cp.start()
        cp.wait()

    return pl.pallas_call(
        body, name="grad_pair_exchange", out_shape=jax.ShapeDtypeStruct((n, R2, C), G.dtype),
        in_specs=[ANY], out_specs=ANY,
        scratch_shapes=[pltpu.SemaphoreType.DMA, pltpu.SemaphoreType.DMA],
    )(G)


def _row_tile(rows, align, cap):
    best = None
    for t in range(align, min(rows, cap) + 1, align):
        if rows % t == 0:
            best = t
    assert best is not None, (rows, align, cap)
    return best


def _pair_sum(G, recv):
    n, R, C = G.shape
    R2 = R // 2
    tr = _row_tile(R2, 16, 704)
    nrb = R2 // tr
    half = lax.axis_index("c").astype(jnp.int32).reshape(1)

    def body(c_ref, g_ref, r_ref, o_ref):
        o_ref[...] = (g_ref[...] + r_ref[...]).astype(o_ref.dtype)

    return pl.pallas_call(
        body, name="grad_pair_sum", out_shape=jax.ShapeDtypeStruct((n, R2, C), BF16),
        grid_spec=pltpu.PrefetchScalarGridSpec(
            num_scalar_prefetch=1, grid=(n, nrb),
            in_specs=[pl.BlockSpec((None, tr, C), lambda j, i, c_ref: (j, c_ref[0] * nrb + i, 0)),
                      pl.BlockSpec((None, tr, C), lambda j, i, c_ref: (j, i, 0))],
            out_specs=pl.BlockSpec((None, tr, C), lambda j, i, c_ref: (j, i, 0))),
        compiler_params=_params(("parallel", "parallel")))(half, G, recv)


def _chip_exchange(P):
    n, R2, C = P.shape

    def body(p_ref, o_ref, send_sems, recv_sems, local_sem):
        x, y, c = _coords()
        me = 2 * x + y
        chips = _other_chips(x, y)
        mine = pltpu.make_async_copy(p_ref.at[me], o_ref.at[me], local_sem)
        mine.start()
        sends = [pltpu.make_async_remote_copy(
            src_ref=p_ref.at[2 * px + py], dst_ref=o_ref.at[me], send_sem=send_sems.at[k],
            recv_sem=recv_sems.at[k], device_id=(px, py, c), device_id_type=MESH)
            for k, (px, py) in enumerate(chips)]
        for cp in sends:
            cp.start()
        for k, (px, py) in enumerate(chips):
            pltpu.make_async_remote_copy(
                src_ref=p_ref.at[me], dst_ref=o_ref.at[2 * px + py], send_sem=send_sems.at[k],
                recv_sem=recv_sems.at[k], device_id=(px, py, c), device_id_type=MESH).wait_recv()
        for cp in sends:
            cp.wait_send()
        mine.wait()

    return pl.pallas_call(
        body, name="grad_chip_exchange", out_shape=jax.ShapeDtypeStruct((n, R2, C), P.dtype),
        in_specs=[ANY], out_specs=ANY,
        scratch_shapes=[pltpu.SemaphoreType.DMA((3,)), pltpu.SemaphoreType.DMA((3,)), pltpu.SemaphoreType.DMA],
    )(P)


def _chip_sum(X):
    n, R2, C = X.shape
    tr = _row_tile(R2, 16, 704)

    def body(x_ref, o_ref):
        acc = x_ref[0].astype(F32)
        for s in range(1, n):
            acc = acc + x_ref[s].astype(F32)
        o_ref[...] = acc

    return pl.pallas_call(
        body, name="grad_chip_sum", out_shape=jax.ShapeDtypeStruct((R2, C), F32), grid=(R2 // tr,),
        in_specs=[pl.BlockSpec((n, tr, C), lambda i: (0, i, 0))],
        out_specs=pl.BlockSpec((tr, C), lambda i: (i, 0)),
        compiler_params=_params(("parallel",)))(X)


def _pair_allgather(Q):
    R2, C = Q.shape

    def body(q_ref, o_ref, send_sem, recv_sem, local_sem):
        x, y, c = _coords()
        mine = pltpu.make_async_copy(q_ref, o_ref.at[c], local_sem)
        mine.start()
        cp = pltpu.make_async_remote_copy(src_ref=q_ref, dst_ref=o_ref.at[c], send_sem=send_sem,
                                          recv_sem=recv_sem, device_id=(x, y, 1 - c), device_id_type=MESH)
        cp.start()
        pltpu.make_async_remote_copy(src_ref=q_ref, dst_ref=o_ref.at[1 - c], send_sem=send_sem,
                                     recv_sem=recv_sem, device_id=(x, y, 1 - c), device_id_type=MESH).wait_recv()
        cp.wait_send()
        mine.wait()

    return pl.pallas_call(
        body, name="grad_pair_allgather", out_shape=jax.ShapeDtypeStruct((2, R2, C), Q.dtype),
        in_specs=[ANY], out_specs=ANY,
        scratch_shapes=[pltpu.SemaphoreType.DMA, pltpu.SemaphoreType.DMA, pltpu.SemaphoreType.DMA],
    )(Q)


def _allreduce_small(pack):
    Rs, C = pack.shape
    n_dev = 8

    def body(x_ref, o_ref, buf, send_sems, recv_sems):
        x, y, c = _coords()
        me = 4 * x + 2 * y + c
        buf[me] = x_ref[...]
        peers = []
        for rel in range(1, n_dev):
            px = 1 - x if rel & 4 else x
            py = 1 - y if rel & 2 else y
            pc = 1 - c if rel & 1 else c
            peers.append((px, py, pc))
        sends = [pltpu.make_async_remote_copy(
            src_ref=x_ref, dst_ref=buf.at[me], send_sem=send_sems.at[k], recv_sem=recv_sems.at[k],
            device_id=peer, device_id_type=MESH) for k, peer in enumerate(peers)]
        for cp in sends:
            cp.start()
        for k, (px, py, pc) in enumerate(peers):
            pltpu.make_async_remote_copy(
                src_ref=x_ref, dst_ref=buf.at[4 * px + 2 * py + pc], send_sem=send_sems.at[k],
                recv_sem=recv_sems.at[k], device_id=(px, py, pc), device_id_type=MESH).wait_recv()
        for cp in sends:
            cp.wait_send()
        acc = buf[0]
        for d in range(1, n_dev):
            acc = acc + buf[d]
        o_ref[...] = acc

    vm = pl.BlockSpec(memory_space=pltpu.VMEM)
    return pl.pallas_call(
        body, name="allreduce_replicated", out_shape=jax.ShapeDtypeStruct((Rs, C), F32),
        in_specs=[vm], out_specs=vm,
        scratch_shapes=[pltpu.VMEM((n_dev, Rs, C), F32), pltpu.SemaphoreType.DMA((n_dev - 1,)),
                        pltpu.SemaphoreType.DMA((n_dev - 1,))],
    )(pack)


SHARDED = (
    ("w_in", 2, True), ("w_out", 1, True), ("pool_w", 2, True), ("w_up", 2, True), ("w_down", 1, True),
    ("meta_tokens", 1, False), ("mix_norm_odd", 1, False), ("pool_b", 2, False), ("pool_scale", 1, False),
    ("conv_w", 2, False), ("ffn_conv_w", 2, False))
REPLICATED = ("mix_norm_even", "b_f", "conv_b", "ln_g", "ln_b", "ffn_norm", "ffn_conv_b", "final_norm")
PACK_ROW_ALIGN = 32


def _pad_rows(flat, align_rows, cols):
    rows = -(-flat.shape[-1] // cols)
    rows = -(-rows // align_rows) * align_rows
    pad = rows * cols - flat.shape[-1]
    flat = jnp.pad(flat, [(0, 0)] * (flat.ndim - 1) + [(0, pad)])
    return flat.reshape(flat.shape[:-1] + (rows, cols))


def _pack_weight_shards(shards):
    parts = []
    for name, _, as_bf16 in SHARDED:
        w = shards[name].astype(F32).reshape(-1)
        parts.append(w.astype(BF16) if as_bf16 else lax.bitcast_convert_type(w, BF16).reshape(-1))
    return _pad_rows(jnp.concatenate(parts), PACK_ROW_ALIGN, PACK_COLS)


def _unpack_weights(gathered, shards):
    flat = gathered.reshape(N_CHIPS, -1)
    out, off = {}, 0
    for name, axis, as_bf16 in SHARDED:
        shp = shards[name].shape
        n = int(np.prod(shp))
        if as_bf16:
            t = flat[:, off:off + n]
            off += n
        else:
            t = lax.bitcast_convert_type(flat[:, off:off + 2 * n].reshape(N_CHIPS, n, 2), F32)
            off += 2 * n
        t = t.reshape((N_CHIPS,) + shp)
        out[name] = jnp.concatenate([t[j] for j in range(N_CHIPS)], axis=axis)
    return out


def _pack_grad_shards(grads, shards):
    parts = []
    for name, axis, _ in SHARDED:
        g = grads[name].reshape(shards[name].shape[:axis] + (N_CHIPS, shards[name].shape[axis])
                                + shards[name].shape[axis + 1:])
        parts.append(jnp.moveaxis(g, axis, 0).reshape(N_CHIPS, -1))
    return _pad_rows(jnp.concatenate(parts, axis=1), PACK_ROW_ALIGN, PACK_COLS)


def _unpack_grad_shard(reduced, shards):
    flat = reduced.reshape(-1)
    out, off = {}, 0
    for name, _, _ in SHARDED:
        shp = shards[name].shape
        n = int(np.prod(shp))
        out[name] = flat[off:off + n].reshape(shp)
        off += n
    return out


def _pack_replicated(grads, loss):
    parts = [_pad_rows(grads[name].astype(F32).reshape(-1), 1, LANE).reshape(-1) for name in REPLICATED]
    parts.append(_pad_rows(loss.reshape(-1)[:1], 1, LANE).reshape(-1))
    return _pad_rows(jnp.concatenate(parts), 8, LANE)


def _unpack_replicated(reduced, shapes):
    flat = reduced.reshape(-1)
    out, off = {}, 0
    for name in REPLICATED:
        n = int(np.prod(shapes[name]))
        out[name] = flat[off:off + n].reshape(shapes[name])
        off += -(-n // LANE) * LANE
    return out, flat[off]


def _ffn_fwd2(h, W, layer, tm, tmm):
    tag = str(layer)
    n = _rms_fwd(h, W["ffn_norm"][layer:layer + 1], BF16, tm, f"ffn_norm_{tag}")
    up = _mm(n, W["w_up"], "nn", F32, tmm, 512, f"ffn_up_{tag}", b_lead=layer)
    act = _ffn_act_fwd(up, W["ffn_conv_w_p"][layer], W["ffn_conv_b"][layer:layer + 1], tm, f"ffn_act_{tag}")
    out = _mm(act, W["w_down"], "nn", F32, tmm, 512, f"ffn_down_{tag}", add=h, b_lead=layer)
    return out, (n, up, act)


def _ffn_bwd2(h, W, layer, saved, dout, acc, tm, tmm):
    tag = str(layer)
    n, up, act = saved
    dact = _mm(dout, W["w_down"], "nt", F32, tmm, 256, f"ffn_dact_{tag}", b_lead=layer)
    dwd = _mm(act, dout, "tn", F32, 256, 512, f"ffn_dwdown_{tag}",
              out=(layer, 2, None if acc is None else acc[1]))
    dup, dcw, dcb = _ffn_act_bwd(up, dact, W["ffn_conv_w_p"][layer], W["ffn_conv_b"][layer:layer + 1], tm,
                                 f"ffn_act_bwd_{tag}")
    dn = _mm_ffn_dn(dup, W["w_up"], layer, tm, 512, f"ffn_dn_{tag}")
    dwu = _mm_ffn_dwup(n, dup, layer, None if acc is None else acc[0], 512, 256, f"ffn_dwup_{tag}")
    dh, dgain = _rms_bwd(h, W["ffn_norm"][layer:layer + 1], dn, dout, tm, f"ffn_norm_bwd_{tag}")
    return dh, (dwu, dwd), dict(gain=dgain, cw=dcw[:FFN_K], cb=dcb)


def _local_step2(h0, tgt, W, n_real):
    LP = h0.shape[0]
    tm, tmm = _row_tiles(LP)
    nb = LP // ATT_BLK
    G = {}
    n0 = _rms_fwd(h0, W["mix_norm_even"], BF16, tm, "mix_norm_even")
    proj = _mm(n0, W["w_in_p"], "nn", F32, tmm, 384, "in_proj")
    c = _fgate_fwd(proj, W["b_f_p"], "forget_gate")
    cT = c[:, :HEADS].T
    qT, kT, k_aug, vT, vh = _attn_operands(proj[:, :FOX_W], proj[:, FOX_W:2 * FOX_W],
                                           proj[:, 2 * FOX_W:3 * FOX_W], cT, LP)
    oT, lse = _attn_fwd2(qT, k_aug, vT, "fox_attention")
    from_t = lambda t: t.transpose(1, 3, 0, 2).reshape(LP, FOX_W)
    u1, u = _conf_fwd(proj, W["conv_w_p"], W["conv_b"], W["ln_g"], W["ln_b"], tm, "conformer")
    cat = jnp.concatenate([from_t(oT).astype(BF16), u], axis=1)
    h1 = _mm(cat, W["w_out"], "nn", F32, tmm, 512, "out_proj", add=h0)
    h2, ffn0 = _ffn_fwd2(h1, W, 0, tm, tmm)
    h3, dpool = _pool_fwd(h2, W["mix_norm_odd"], W["pool_w"], W["pool_b"], W["pool_scale"], tm, "pool_mixer")
    h4, ffn1 = _ffn_fwd2(h3, W, 1, tm, tmm)
    loss, dh4, G["final_norm"] = _loss_head(h4, W["final_norm"], tgt, n_real, tm, "loss_head")

    dh3, acc, g1 = _ffn_bwd2(h3, W, 1, ffn1, dh4, None, tm, tmm)
    dh2, G["pool_w"], G["pool_b"], G["pool_scale"], G["mix_norm_odd"] = _pool_bwd(
        h2, W["mix_norm_odd"], dpool, W["pool_w"], W["pool_b"], W["pool_scale"], dh3, tm, "pool_mixer_bwd")
    dh1, acc, g0 = _ffn_bwd2(h1, W, 0, ffn0, dh2, acc, tm, tmm)
    G["w_up"], G["w_down"] = acc
    G["ffn_norm"] = jnp.concatenate([g0["gain"], g1["gain"]], axis=0)
    G["ffn_conv_w"] = jnp.stack([g0["cw"], g1["cw"]])
    G["ffn_conv_b"] = jnp.concatenate([g0["cb"], g1["cb"]], axis=0)

    dcat = _mm(dh1, W["w_out"], "nt", F32, tmm, 512, "out_proj_dx")
    G["w_out"] = _mm(cat, dh1, "tn", F32, 512, 512, "out_proj_dw")
    dadg, dcw, G["conv_b"], G["ln_g"], G["ln_b"] = _conf_bwd(
        proj, u1, dcat, W["conv_w_p"], W["ln_g"], W["ln_b"], tm, "conformer_bwd")
    G["conv_w"] = dcw[:CONV_K]
    doT = dcat[:, :FOX_W].astype(BF16).reshape(nb, ATT_BLK, HEADS, HEAD_DIM).transpose(2, 0, 3, 1)
    dqT, dk_aug, dv = _attn_bwd2(qT, kT, k_aug, vh, oT, doT, lse, "fox_attention_bwd")
    dq = from_t(dqT[:, :, :HEAD_DIM, :] * (HEAD_DIM ** -0.5))
    dk, dv = (t.transpose(1, 0, 2).reshape(LP, FOX_W) for t in (dk_aug[:, :, :HEAD_DIM], dv))
    dc = dqT[:, :, ONES_IN_K, :].reshape(HEADS, LP) - dk_aug[:, :, ONES_IN_Q]
    dfl, dbf = _fgate_bwd(proj, W["b_f_p"], jnp.pad(dc.T, ((0, 0), (0, LANE - HEADS))), "forget_gate_bwd")
    G["b_f"] = dbf[:, :HEADS]
    dproj = jnp.concatenate([t.astype(BF16) for t in (dq, dk, dv)] + [dadg, dfl], axis=1)
    dn0 = _mm(dproj, W["w_in_p"], "nt", F32, tmm, 512, "in_proj_dx")
    G["w_in_p"] = _mm(n0, dproj, "tn", F32, 512, 384, "in_proj_dw")
    dh0, G["mix_norm_even"] = _rms_bwd(h0, W["mix_norm_even"], dn0, dh1, tm, "mix_norm_even_bwd")
    return loss, dh0, G


class _Cut:
    def __init__(self, full_shape, chip_dim, half_dim):
        self.full = tuple(full_shape)
        self.chip_dim, self.half_dim = chip_dim, half_dim
        self.chip_size = full_shape[chip_dim] // N_CHIPS
        self.half_size = full_shape[half_dim] // 2
        assert chip_dim != half_dim

    def shape(self, chip=False, half=False):
        s = list(self.full)
        if chip:
            s[self.chip_dim] = self.chip_size
        if half:
            s[self.half_dim] = self.half_size
        return tuple(s)

    def region(self, ref, chip=None, half=None):
        idx = [pl.ds(0, n) for n in ref.shape]
        if chip is not None:
            idx[self.chip_dim] = pl.ds(chip * self.chip_size, self.chip_size)
        if half is not None:
            idx[self.half_dim] = pl.ds(half * self.half_size, self.half_size)
        return ref.at[tuple(idx)]


SMALL_SHARDED = ("meta_tokens", "mix_norm_odd", "pool_b", "pool_scale", "conv_w", "ffn_conv_w")
SMALL_ROWS = 144


def _cuts():
    return {
        "w_in": _Cut((N_CHIPS, D_MODEL, IN_SHARD), 0, 1),
        "w_out": _Cut((D_MODEL, D_MODEL), 0, 1),
        "pool_w": _Cut((len(POOL_WINDOWS), POOL_G, POOL_G), 1, 0),
        "w_up": _Cut((2, D_MODEL, 2 * D_FF), 2, 1),
        "w_down": _Cut((2, D_FF, D_MODEL), 1, 2),
        "small": _Cut((N_CHIPS, SMALL_ROWS, LANE), 0, 1),
    }


COMM_ORDER = ("w_in", "w_out", "pool_w", "w_up", "w_down", "small")


def _remote(src, dst, send_sems, recv_sems, k, to):
    return pltpu.make_async_remote_copy(src_ref=src, dst_ref=dst, send_sem=send_sems.at[k],
                                        recv_sem=recv_sems.at[k], device_id=to, device_id_type=MESH)


def _gather_weights(shards, cuts):
    n = len(shards)

    def body(*refs):
        srcs, outs = refs[:n], refs[n:2 * n]
        send_sems, recv_sems = refs[2 * n:]
        x, y, c = _coords()
        me = 2 * x + y
        sibling = (x, y, 1 - c)
        chips = _other_chips(x, y)
        sends = []
        for t, cut in enumerate(cuts):
            push = _remote(srcs[t], cut.region(outs[t], chip=me), send_sems, recv_sems, 7 * t, sibling)
            push.start()
            sends.append(push)
            for kk, chip in enumerate(chips):
                cp = _remote(cut.region(srcs[t], half=c), cut.region(outs[t], chip=me, half=c),
                             send_sems, recv_sems, 7 * t + 1 + kk, (*chip, c))
                cp.start()
                sends.append(cp)
        for t, cut in enumerate(cuts):
            for kk, (px, py) in enumerate(chips):
                landed = cut.region(outs[t], chip=2 * px + py, half=c)
                _remote(landed, landed, send_sems, recv_sems, 7 * t + 1 + kk, sibling).wait_recv()
                fwd = _remote(landed, landed, send_sems, recv_sems, 7 * t + 4 + kk, sibling)
                fwd.start()
                sends.append(fwd)
        for t, cut in enumerate(cuts):
            mine = cut.region(outs[t], chip=me)
            _remote(mine, mine, send_sems, recv_sems, 7 * t, sibling).wait_recv()
            for kk, (px, py) in enumerate(chips):
                other = cut.region(outs[t], chip=2 * px + py, half=1 - c)
                _remote(other, other, send_sems, recv_sems, 7 * t + 4 + kk, sibling).wait_recv()
        for cp in sends:
            cp.wait_send()

    return pl.pallas_call(
        body, name="gather_weights",
        out_shape=tuple(jax.ShapeDtypeStruct(cut.full, s.dtype) for cut, s in zip(cuts, shards)),
        in_specs=[ANY] * n, out_specs=tuple([ANY] * n),
        scratch_shapes=[pltpu.SemaphoreType.DMA((7 * n,)), pltpu.SemaphoreType.DMA((7 * n,))],
    )(*shards)


def _pair_exchange2(fulls, cuts):
    n = len(fulls)

    def body(*refs):
        srcs, outs = refs[:n], refs[n:2 * n]
        send_sems, recv_sems = refs[2 * n:]
        x, y, c = _coords()
        cps = [_remote(cut.region(srcs[t], half=1 - c), outs[t], send_sems, recv_sems, t, (x, y, 1 - c))
               for t, cut in enumerate(cuts)]
        for cp in cps:
            cp.start()
        for cp in cps:
            cp.wait()

    return pl.pallas_call(
        body, name="grad_pair_exchange",
        out_shape=tuple(jax.ShapeDtypeStruct(cut.shape(half=True), f.dtype) for cut, f in zip(cuts, fulls)),
        in_specs=[ANY] * n, out_specs=tuple([ANY] * n),
        scratch_shapes=[pltpu.SemaphoreType.DMA((n,)), pltpu.SemaphoreType.DMA((n,))],
    )(*fulls)


def _grid_of(shape, blk):
    assert all(s % b == 0 for s, b in zip(shape, blk)), (shape, blk)
    return tuple(s // b for s, b in zip(shape, blk))


def _pair_sum2(full, recv, cut, blk, name):
    hshape = cut.shape(half=True)
    grid = _grid_of(hshape, blk)
    hb = cut.half_size // blk[cut.half_dim]
    hd = cut.half_dim
    pos = jnp.stack([lax.axis_index("c")]).astype(jnp.int32)

    def full_idx(*a):
        ids, p = list(a[:-1]), a[-1]
        ids[hd] = ids[hd] + p[0] * hb
        return tuple(ids)

    def body(p_ref, f_ref, r_ref, o_ref):
        o_ref[...] = (f_ref[...] + r_ref[...]).astype(o_ref.dtype)

    return pl.pallas_call(
        body, name=name, out_shape=jax.ShapeDtypeStruct(hshape, BF16),
        grid_spec=pltpu.PrefetchScalarGridSpec(
            num_scalar_prefetch=1, grid=grid,
            in_specs=[pl.BlockSpec(blk, full_idx), pl.BlockSpec(blk, lambda *a: tuple(a[:-1]))],
            out_specs=pl.BlockSpec(blk, lambda *a: tuple(a[:-1]))),
        compiler_params=_params(("parallel",) * len(grid)))(pos, full, recv)


def _chip_exchange2(parts, cuts):
    n = len(parts)

    def body(*refs):
        srcs, outs = refs[:n], refs[n:2 * n]
        send_sems, recv_sems = refs[2 * n:]
        x, y, c = _coords()
        me = 2 * x + y
        chips = _other_chips(x, y)
        sends = []
        for t, cut in enumerate(cuts):
            for kk, (px, py) in enumerate(chips):
                cp = _remote(cut.region(srcs[t], chip=2 * px + py), outs[t].at[me], send_sems, recv_sems,
                             3 * t + kk, (px, py, c))
                cp.start()
                sends.append(cp)
        for t, cut in enumerate(cuts):
            for kk, (px, py) in enumerate(chips):
                slot = outs[t].at[2 * px + py]
                _remote(slot, slot, send_sems, recv_sems, 3 * t + kk, (px, py, c)).wait_recv()
        for cp in sends:
            cp.wait_send()

    return pl.pallas_call(
        body, name="grad_chip_exchange",
        out_shape=tuple(jax.ShapeDtypeStruct((N_CHIPS,) + cut.shape(chip=True, half=True), p.dtype)
                        for cut, p in zip(cuts, parts)),
        in_specs=[ANY] * n, out_specs=tuple([ANY] * n),
        scratch_shapes=[pltpu.SemaphoreType.DMA((3 * n,)), pltpu.SemaphoreType.DMA((3 * n,))],
    )(*parts)


def _chip_sum2(part, recv, cut, blk, name):
    bshape = cut.shape(chip=True, half=True)
    grid = _grid_of(bshape, blk)
    cb = cut.chip_size // blk[cut.chip_dim]
    hb = cut.half_size // blk[cut.half_dim]
    cd, hd = cut.chip_dim, cut.half_dim
    x, y, c = _coords()
    slots = [2 * px + py for px, py in _other_chips(x, y)]
    pos = jnp.stack([c, 2 * x + y] + slots).astype(jnp.int32)

    def part_idx(*a):
        ids, p = list(a[:-1]), a[-1]
        ids[cd] = ids[cd] + p[1] * cb
        return tuple(ids)

    def recv_idx(kk):
        return lambda *a: (a[-1][2 + kk],) + tuple(a[:-1])

    def out_idx(*a):
        ids, p = list(a[:-1]), a[-1]
        ids[hd] = ids[hd] + p[0] * hb
        return tuple(ids)

    def body(p_ref, own_ref, r0_ref, r1_ref, r2_ref, o_ref):
        acc = own_ref[...].astype(F32)
        for r_ref in (r0_ref, r1_ref, r2_ref):
            acc = acc + r_ref[...].astype(F32)
        o_ref[...] = acc

    return pl.pallas_call(
        body, name=name, out_shape=jax.ShapeDtypeStruct(cut.shape(chip=True), F32),
        grid_spec=pltpu.PrefetchScalarGridSpec(
            num_scalar_prefetch=1, grid=grid,
            in_specs=[pl.BlockSpec(blk, part_idx)] + [pl.BlockSpec((None,) + blk, recv_idx(kk)) for kk in range(3)],
            out_specs=pl.BlockSpec(blk, out_idx)),
        compiler_params=_params(("parallel",) * len(grid)))(pos, part, recv, recv, recv)


def _pair_swap2(blocks, cuts):
    n = len(blocks)

    def body(*refs):
        outs = refs[n:2 * n]
        send_sems, recv_sems = refs[2 * n:]
        x, y, c = _coords()
        cps = []
        for t, cut in enumerate(cuts):
            mine = cut.region(outs[t], half=c)
            cp = _remote(mine, mine, send_sems, recv_sems, t, (x, y, 1 - c))
            cp.start()
            cps.append(cp)
        for t, cut in enumerate(cuts):
            theirs = cut.region(outs[t], half=1 - c)
            _remote(theirs, theirs, send_sems, recv_sems, t, (x, y, 1 - c)).wait_recv()
        for cp in cps:
            cp.wait_send()

    return pl.pallas_call(
        body, name="grad_pair_swap",
        out_shape=tuple(jax.ShapeDtypeStruct(b.shape, b.dtype) for b in blocks),
        in_specs=[ANY] * n, out_specs=tuple([ANY] * n),
        input_output_aliases={t: t for t in range(n)},
        scratch_shapes=[pltpu.SemaphoreType.DMA((n,)), pltpu.SemaphoreType.DMA((n,))],
    )(*blocks)


PAIR_SUM_BLOCKS = {"w_in": (1, 512, IN_SHARD), "w_out": (512, 512), "pool_w": (1, POOL_G, POOL_G),
                   "w_up": (1, 64, 2 * D_FF), "w_down": (1, 704, 512), "small": (N_CHIPS, SMALL_ROWS // 2, LANE)}
CHIP_SUM_BLOCKS = {"w_in": (1, 512, IN_SHARD), "w_out": (256, 512), "pool_w": (2, 64, POOL_G),
                   "w_up": (1, 128, UP_SHARD), "w_down": (1, DOWN_SHARD, 512), "small": (1, SMALL_ROWS // 2, LANE)}


def _pack_small(P):
    parts = []
    for name in SMALL_SHARDED:
        t = P[name]
        parts.append(t.astype(F32))
    return parts


def _small_rows(t, lead):
    flat = t.reshape(lead + (-1,))
    pad = -flat.shape[-1] % LANE
    return jnp.pad(flat, [(0, 0)] * len(lead) + [(0, pad)]).reshape(lead + (-1, LANE))


def _pack_small_shards(shards):
    rows = jnp.concatenate([_small_rows(shards[n].astype(F32), ()) for n in SMALL_SHARDED], axis=0)
    return jnp.pad(rows, ((0, SMALL_ROWS - rows.shape[0]), (0, 0)))[None]


def _unpack_small(pack, shards, axes):
    out, off = {}, 0
    nchip = pack.shape[0]
    for name in SMALL_SHARDED:
        shp = shards[name].shape
        cnt = int(np.prod(shp))
        rows = -(-cnt // LANE)
        t = pack[:, off:off + rows].reshape(nchip, -1)[:, :cnt].reshape((nchip,) + shp)
        out[name] = jnp.concatenate([t[j] for j in range(nchip)], axis=axes[name])
        off += rows
    return out


def _pack_small_grads(grads, shards, axes):
    parts = []
    for name in SMALL_SHARDED:
        shp, ax = shards[name].shape, axes[name]
        g = grads[name].reshape(shp[:ax] + (N_CHIPS, shp[ax]) + shp[ax + 1:])
        parts.append(_small_rows(jnp.moveaxis(g, ax, 0), (N_CHIPS,)))
    rows = jnp.concatenate(parts, axis=1)
    return jnp.pad(rows, ((0, 0), (0, SMALL_ROWS - rows.shape[1]), (0, 0)))


SMALL_AXES = {"meta_tokens": 1, "mix_norm_odd": 1, "pool_b": 2, "pool_scale": 1, "conv_w": 2, "ffn_conv_w": 2}


WEIGHT_NAMES = ("meta_tokens", "mix_norm_even", "w_in", "b_f", "conv_w", "conv_b", "ln_g", "ln_b", "w_out",
                "mix_norm_odd", "pool_w", "pool_b", "pool_scale", "ffn_norm", "w_up", "ffn_conv_w",
                "ffn_conv_b", "w_down", "final_norm")


def kernel(x, meta_tokens, mix_norm_even, w_in, b_f, conv_w, conv_b, ln_g, ln_b, w_out, mix_norm_odd, pool_w, pool_b, pool_scale, ffn_norm, w_up, ffn_conv_w, ffn_conv_b, w_down, final_norm, loss_target, m_meta_tokens, m_mix_norm_even, m_w_in, m_b_f, m_conv_w, m_conv_b, m_ln_g, m_ln_b, m_w_out, m_mix_norm_odd, m_pool_w, m_pool_b, m_pool_scale, m_ffn_norm, m_w_up, m_ffn_conv_w, m_ffn_conv_b, m_w_down, m_final_norm, v_meta_tokens, v_mix_norm_even, v_w_in, v_b_f, v_conv_w, v_conv_b, v_ln_g, v_ln_b, v_w_out, v_mix_norm_odd, v_pool_w, v_pool_b, v_pool_scale, v_ffn_norm, v_w_up, v_ffn_conv_w, v_ffn_conv_b, v_w_down, v_final_norm):
    given = dict(locals())
    w_loc = {n: given[n] for n in WEIGHT_NAMES}
    m_loc = {n: given["m_" + n] for n in WEIGHT_NAMES}
    v_loc = {n: given["v_" + n] for n in WEIGHT_NAMES}
    cut_of = _cuts()
    cuts = [cut_of[n] for n in COMM_ORDER]
    big = ("w_in", "w_out", "pool_w", "w_up", "w_down")
    small_shards = {n: w_loc[n] for n in SMALL_SHARDED}

    shards = [w_loc[n].astype(BF16).reshape(cut_of[n].shape(chip=True)) for n in big]
    shards.append(_pack_small_shards(small_shards))
    g_in, g_out, g_pool, g_up, g_down, g_small = _gather_weights(shards, cuts)
    full = _unpack_small(g_small, small_shards, SMALL_AXES)
    full.update({n: w_loc[n] for n in REPLICATED})
    w_in_full = g_in.transpose(1, 0, 2).reshape(D_MODEL, IN_COLS)
    qkv, f, ag = (w_in_full[:, :3 * FOX_W], w_in_full[:, 3 * FOX_W:3 * FOX_W + HEADS],
                  w_in_full[:, 3 * FOX_W + HEADS:])
    W = dict(
        mix_norm_even=full["mix_norm_even"].reshape(1, D_MODEL),
        w_in_p=jnp.concatenate([qkv, ag, f, jnp.zeros((D_MODEL, LANE - HEADS), BF16)], axis=1),
        b_f_p=jnp.pad(full["b_f"].reshape(1, HEADS), ((0, 0), (0, LANE - HEADS))),
        conv_w_p=jnp.pad(full["conv_w"].reshape(CONV_K, CONV_CH), ((0, CONV_HALO - CONV_K), (0, 0))),
        conv_b=full["conv_b"].reshape(1, CONV_CH), ln_g=full["ln_g"].reshape(1, CONV_CH),
        ln_b=full["ln_b"].reshape(1, CONV_CH), w_out=g_out,
        mix_norm_odd=full["mix_norm_odd"].reshape(1, D_MODEL), pool_w=g_pool,
        pool_b=full["pool_b"].reshape(1, D_MODEL), pool_scale=full["pool_scale"].reshape(1, D_MODEL),
        ffn_norm=full["ffn_norm"], w_up=g_up,
        ffn_conv_w_p=jnp.pad(full["ffn_conv_w"], ((0, 0), (0, 8 - FFN_K), (0, 0))),
        ffn_conv_b=full["ffn_conv_b"], w_down=g_down, final_norm=full["final_norm"].reshape(1, D_MODEL))

    seq = x.shape[1]
    n_real = N_META + seq
    LP = -(-n_real // ATT_BLK) * ATT_BLK
    tail = jnp.zeros((LP - n_real, D_MODEL), F32)
    h0 = jnp.concatenate([full["meta_tokens"], x[0], tail], axis=0)
    tgt = jnp.concatenate([jnp.zeros((N_META, D_MODEL), F32), loss_target[0], tail], axis=0)
    loss_loc, dh0, G = _local_step2(h0, tgt, W, n_real)
    grad_x = dh0[N_META:n_real][None]
    G["meta_tokens"] = dh0[:N_META]

    rep_shapes = {n: w_loc[n].shape for n in REPLICATED}
    G["final_norm"] = G["final_norm"].reshape(D_MODEL)
    rep, loss = _unpack_replicated(_allreduce_small(_pack_replicated(G, loss_loc)), rep_shapes)

    gp = G["w_in_p"]
    g_w_in = jnp.concatenate([gp[:, :3 * FOX_W], gp[:, 3 * FOX_W + 2 * CONV_CH:3 * FOX_W + 2 * CONV_CH + HEADS],
                              gp[:, 3 * FOX_W:3 * FOX_W + 2 * CONV_CH]], axis=1)
    fulls = [g_w_in.reshape(D_MODEL, N_CHIPS, IN_SHARD).transpose(1, 0, 2), G["w_out"], G["pool_w"], G["w_up"],
             G["w_down"], _pack_small_grads(G, small_shards, SMALL_AXES)]
    recvs = _pair_exchange2(fulls, cuts)
    parts = [_pair_sum2(f, r, cut, PAIR_SUM_BLOCKS[n], "grad_pair_sum_" + n)
             for f, r, cut, n in zip(fulls, recvs, cuts, COMM_ORDER)]
    others = _chip_exchange2(parts, cuts)
    blocks = [_chip_sum2(p, o, cut, CHIP_SUM_BLOCKS[n], "grad_chip_sum_" + n)
              for p, o, cut, n in zip(parts, others, cuts, COMM_ORDER)]
    blocks = _pair_swap2(blocks, cuts)
    gsh = {n: b.reshape(w_loc[n].shape) for n, b in zip(big, blocks[:5])}
    gsh.update(_unpack_small(blocks[5], small_shards, SMALL_AXES))
    sharded = set(big) | set(SMALL_SHARDED)

    grad_w = {n: (gsh[n] if n in sharded else rep[n]) for n in WEIGHT_NAMES}
    delta, new_m, new_v = {}, {}, {}
    for n in WEIGHT_NAMES:
        delta[n], new_m[n], new_v[n] = _adamw(w_loc[n], grad_w[n], m_loc[n], v_loc[n], "adamw_" + n)
    return (loss, grad_x, *[grad_w[n] for n in WEIGHT_NAMES], *[delta[n] for n in WEIGHT_NAMES],
            *[new_m[n] for n in WEIGHT_NAMES], *[new_v[n] for n in WEIGHT_NAMES])
```

```python
import functools

import numpy as np
import jax
import jax.numpy as jnp
from jax import lax
from jax.experimental import pallas as pl
from jax.experimental.pallas import tpu as pltpu

F32 = jnp.float32
BF16 = jnp.bfloat16

D_MODEL = 1024
N_META = 16
SEQ = 2048
HEADS = 8
HEAD_DIM = 64
FOX_W = HEADS * HEAD_DIM
CONV_CH = 512
CONV_K = 31
D_FF = 2816
POOL_WINDOWS = (2, 4, 8, 16)
POOL_G = 256
RMS_EPS = 1e-6
LN_EPS = 1e-5
IN_COLS = 3 * FOX_W + HEADS + 2 * CONV_CH
IN_COLS_P = 3 * FOX_W + 2 * CONV_CH + 128
F_COL_BLK = (3 * FOX_W + 2 * CONV_CH) // 128
N_CHIPS = 4
IN_SHARD = IN_COLS // N_CHIPS
UP_SHARD = 2 * D_FF // N_CHIPS
DOWN_SHARD = D_FF // N_CHIPS

ADAM_LR = 0.001
ADAM_B1 = 0.9
ADAM_B2 = 0.999
ADAM_EPS = 1e-08
ADAM_WD = 0.01
ADAM_STEP = 10

LANE = 128
ATT_BLK = 128
VMEM_LIMIT = 48 * 1024 * 1024

NEG = -1e30


def _sigmoid(x):
    return 1.0 / (1.0 + jnp.exp(-x))


def _params(sem=None):
    return pltpu.CompilerParams(dimension_semantics=sem, vmem_limit_bytes=VMEM_LIMIT)


def _sub_rows(tm):
    best = 8
    for s in range(8, 137, 8):
        if tm % s == 0:
            best = s
    return best


def _mm(a, b, mode, out_dtype, tm, tn, name, add=None, a_lead=None, b_lead=None, out=None):
    a_shape = a.shape if a_lead is None else a.shape[1:]
    b_shape = b.shape if b_lead is None else b.shape[1:]
    if mode == "nn":
        (M, K), (K2, N) = a_shape, b_shape
        dims = (((1,), (0,)), ((), ()))
        a_blk, a_idx = (tm, K), (lambda i, j: (i, 0))
        b_blk, b_idx = (K, tn), (lambda i, j: (0, j))
    elif mode == "nt":
        (M, K), (N, K2) = a_shape, b_shape
        dims = (((1,), (1,)), ((), ()))
        a_blk, a_idx = (tm, K), (lambda i, j: (i, 0))
        b_blk, b_idx = (tn, K), (lambda i, j: (j, 0))
    else:
        (K, M), (K2, N) = a_shape, b_shape
        dims = (((0,), (0,)), ((), ()))
        a_blk, a_idx = (K, tm), (lambda i, j: (0, i))
        b_blk, b_idx = (K, tn), (lambda i, j: (0, j))
    assert K == K2 and M % tm == 0 and N % tn == 0, (name, a.shape, b.shape, tm, tn)
    gm, gn = M // tm, N // tn
    a_bytes = M * K * a.dtype.itemsize
    b_bytes = N * K * b.dtype.itemsize
    m_outer = a_bytes + b_bytes * gm <= b_bytes + a_bytes * gn
    if m_outer:
        grid = (gm, gn)
        wrap = lambda f: f
    else:
        grid = (gn, gm)
        wrap = lambda f: (lambda j, i: f(i, j))

    def lead(blk, idx, at):
        if at is None:
            return pl.BlockSpec(blk, wrap(idx))
        return pl.BlockSpec((None,) + blk, wrap(lambda i, j: (at,) + idx(i, j)))

    o_idx = lambda i, j: (i, j)
    in_specs = [lead(a_blk, a_idx, a_lead), lead(b_blk, b_idx, b_lead)]
    args = [a, b]
    if add is not None:
        in_specs.append(pl.BlockSpec((tm, tn), wrap(o_idx)))
        args.append(add)
    aliases = {}
    if out is None:
        out_shape = jax.ShapeDtypeStruct((M, N), out_dtype)
        out_spec = pl.BlockSpec((tm, tn), wrap(o_idx))
    else:
        o_lead, n_lead, into = out
        out_shape = jax.ShapeDtypeStruct((n_lead, M, N), out_dtype)
        out_spec = lead((tm, tn), o_idx, o_lead)
        if into is not None:
            aliases = {len(args): 0}
            in_specs.append(pl.BlockSpec(memory_space=pl.ANY))
            args.append(into)
    has_add = add is not None

    def body(a_ref, b_ref, *rest):
        o_ref = rest[-1]
        x = a_ref[...].astype(BF16)
        y = b_ref[...].astype(BF16)
        acc = lax.dot_general(x, y, dims, preferred_element_type=F32)
        if has_add:
            acc = acc + rest[0][...]
        o_ref[...] = acc.astype(o_ref.dtype)

    return pl.pallas_call(
        body, name=name, out_shape=out_shape, grid=grid, in_specs=in_specs, out_specs=out_spec,
        input_output_aliases=aliases, compiler_params=_params(("parallel", "parallel")))(*args)


def _mm_ffn_dn(dup, w_up, layer, tm, tn, name):
    _, LP, F = dup.shape
    Dm = w_up.shape[1]
    nt = (((1,), (1,)), ((), ()))

    def body(a_ref, b_ref, o_ref):
        acc = lax.dot_general(a_ref[0], b_ref[:, 0:F], nt, preferred_element_type=F32)
        acc = acc + lax.dot_general(a_ref[1], b_ref[:, F:2 * F], nt, preferred_element_type=F32)
        o_ref[...] = acc

    return pl.pallas_call(
        body, name=name, out_shape=jax.ShapeDtypeStruct((LP, Dm), F32), grid=(LP // tm, Dm // tn),
        in_specs=[pl.BlockSpec((2, tm, F), lambda i, j: (0, i, 0)),
                  pl.BlockSpec((None, tn, 2 * F), lambda i, j: (layer, j, 0))],
        out_specs=pl.BlockSpec((tm, tn), lambda i, j: (i, j)),
        compiler_params=_params(("parallel", "parallel")))(dup, w_up)


def _mm_ffn_dwup(n, dup, layer, into, tk, tn, name):
    LP, Dm = n.shape
    F = dup.shape[2]
    nct = F // tn
    tdims = (((0,), (0,)), ((), ()))

    def body(a_ref, b_ref, *rest):
        rest[-1][...] = lax.dot_general(a_ref[...], b_ref[...], tdims, preferred_element_type=F32)

    in_specs = [pl.BlockSpec((LP, tk), lambda i, j: (0, i)),
                pl.BlockSpec((None, LP, tn), lambda i, j: (j // nct, 0, j % nct))]
    args = [n, dup]
    aliases = {}
    if into is not None:
        in_specs.append(pl.BlockSpec(memory_space=pl.ANY))
        args.append(into)
        aliases = {2: 0}
    return pl.pallas_call(
        body, name=name, out_shape=jax.ShapeDtypeStruct((2, Dm, 2 * F), F32), grid=(Dm // tk, 2 * nct),
        in_specs=in_specs, out_specs=pl.BlockSpec((None, tk, tn), lambda i, j: (layer, i, j)),
        input_output_aliases=aliases, compiler_params=_params(("parallel", "parallel")))(*args)


def _rms_fwd(h, g, out_dtype, tm, name):
    LP, Dm = h.shape

    def body(h_ref, g_ref, o_ref):
        x = h_ref[...]
        r = lax.rsqrt(jnp.mean(x * x, axis=1, keepdims=True) + RMS_EPS)
        o_ref[...] = (x * r * g_ref[...]).astype(o_ref.dtype)

    return pl.pallas_call(
        body, name=name, out_shape=jax.ShapeDtypeStruct((LP, Dm), out_dtype), grid=(LP // tm,),
        in_specs=[pl.BlockSpec((tm, Dm), lambda i: (i, 0)), pl.BlockSpec((1, Dm), lambda i: (0, 0))],
        out_specs=pl.BlockSpec((tm, Dm), lambda i: (i, 0)),
        compiler_params=_params(("parallel",)))(h, g)


def _rms_bwd(h, g, dn, dres, tm, name):
    LP, Dm = h.shape

    def body(h_ref, g_ref, dn_ref, dr_ref, dh_ref, dg_ref):
        i = pl.program_id(0)
        x = h_ref[...]
        r = lax.rsqrt(jnp.mean(x * x, axis=1, keepdims=True) + RMS_EPS)
        xhat = x * r
        dy = dn_ref[...]
        dxh = dy * g_ref[...]
        dh = r * (dxh - xhat * jnp.mean(dxh * xhat, axis=1, keepdims=True))
        dh_ref[...] = dr_ref[...] + dh

        @pl.when(i == 0)
        def _():
            dg_ref[...] = jnp.zeros_like(dg_ref)

        dg_ref[...] += jnp.sum(dy * xhat, axis=0, keepdims=True)

    row = pl.BlockSpec((tm, Dm), lambda i: (i, 0))
    vec = pl.BlockSpec((1, Dm), lambda i: (0, 0))
    return pl.pallas_call(
        body, name=name,
        out_shape=(jax.ShapeDtypeStruct((LP, Dm), F32), jax.ShapeDtypeStruct((1, Dm), F32)),
        grid=(LP // tm,), in_specs=[row, vec, row, row], out_specs=(row, vec),
        compiler_params=_params(("arbitrary",)))(h, g, dn, dres)


def _loss_head(h, g, tgt, n_real, tm, name):
    LP, Dm = h.shape

    def body(h_ref, g_ref, t_ref, loss_ref, dh_ref, dg_ref):
        i = pl.program_id(0)
        x = h_ref[...]
        gg = g_ref[...]
        r = lax.rsqrt(jnp.mean(x * x, axis=1, keepdims=True) + RMS_EPS)
        xhat = x * r
        rows = i * tm + lax.broadcasted_iota(jnp.int32, (tm, 1), 0)
        real = jnp.logical_and(rows >= N_META, rows < n_real)
        diff = jnp.where(real, xhat * gg - t_ref[...], 0.0)
        dy = diff * (1.0 / Dm)
        dxh = dy * gg
        dh_ref[...] = r * (dxh - xhat * jnp.mean(dxh * xhat, axis=1, keepdims=True))

        @pl.when(i == 0)
        def _():
            dg_ref[...] = jnp.zeros_like(dg_ref)
            loss_ref[...] = jnp.zeros_like(loss_ref)

        dg_ref[...] += jnp.sum(dy * xhat, axis=0, keepdims=True)
        part = jnp.sum(jnp.sum(diff * diff, axis=1, keepdims=True), axis=0, keepdims=True)
        loss_ref[...] += jnp.broadcast_to(part * (0.5 / Dm), loss_ref.shape)

    row = pl.BlockSpec((tm, Dm), lambda i: (i, 0))
    vec = pl.BlockSpec((1, Dm), lambda i: (0, 0))
    return pl.pallas_call(
        body, name=name,
        out_shape=(jax.ShapeDtypeStruct((1, LANE), F32), jax.ShapeDtypeStruct((LP, Dm), F32),
                   jax.ShapeDtypeStruct((1, Dm), F32)),
        grid=(LP // tm,), in_specs=[row, vec, row],
        out_specs=(pl.BlockSpec((1, LANE), lambda i: (0, 0)), row, vec),
        compiler_params=_params(("arbitrary",)))(h, g, tgt)


def _fgate_fwd(proj, bf_p, name):
    LP = proj.shape[0]
    nb = LP // LANE

    def body(f_ref, b_ref, c_ref, lf_ref):
        x = f_ref[...] + b_ref[...]
        lf_ref[...] = jnp.minimum(x, 0.0) - jnp.log1p(jnp.exp(-jnp.abs(x)))
        ri = lax.broadcasted_iota(jnp.int32, (LANE, LANE), 0)
        ci = lax.broadcasted_iota(jnp.int32, (LANE, LANE), 1)
        tri = jnp.where(ri >= ci, 1.0, 0.0).astype(F32)

        def blk(i, carry):
            rows = pl.ds(pl.multiple_of(i * LANE, LANE), LANE)
            cb = jnp.dot(tri, lf_ref[rows, :], precision=lax.Precision.HIGHEST,
                         preferred_element_type=F32) + carry
            c_ref[rows, :] = cb
            return cb[LANE - 1:LANE, :]

        lax.fori_loop(0, nb, blk, jnp.zeros((1, LANE), F32))

    return pl.pallas_call(
        body, name=name, out_shape=jax.ShapeDtypeStruct((LP, LANE), F32), grid=(1,),
        in_specs=[pl.BlockSpec((LP, LANE), lambda i: (0, F_COL_BLK)),
                  pl.BlockSpec((1, LANE), lambda i: (0, 0))],
        out_specs=pl.BlockSpec((LP, LANE), lambda i: (0, 0)),
        scratch_shapes=[pltpu.VMEM((LP, LANE), F32)],
        compiler_params=_params(("arbitrary",)))(proj, bf_p)


def _fgate_bwd(proj, bf_p, dc, name):
    LP = proj.shape[0]
    nb = LP // LANE

    def body(f_ref, b_ref, dc_ref, dl_ref, db_ref):
        ri = lax.broadcasted_iota(jnp.int32, (LANE, LANE), 0)
        ci = lax.broadcasted_iota(jnp.int32, (LANE, LANE), 1)
        triu = jnp.where(ri <= ci, 1.0, 0.0).astype(F32)
        bb = b_ref[...]

        tail = jnp.zeros((1, LANE), F32)
        dbs = jnp.zeros((1, LANE), F32)
        for i in range(nb - 1, -1, -1):
            rows = slice(i * LANE, (i + 1) * LANE)
            gb = jnp.dot(triu, dc_ref[rows, :], precision=lax.Precision.HIGHEST,
                         preferred_element_type=F32) + tail
            x = f_ref[rows, :] + bb
            dl = gb * _sigmoid(-x)
            dl_ref[rows, :] = dl.astype(dl_ref.dtype)
            tail = gb[0:1, :]
            dbs = dbs + jnp.sum(dl, axis=0, keepdims=True)
        db_ref[...] = dbs

    return pl.pallas_call(
        body, name=name,
        out_shape=(jax.ShapeDtypeStruct((LP, LANE), BF16), jax.ShapeDtypeStruct((1, LANE), F32)),
        grid=(1,),
        in_specs=[pl.BlockSpec((LP, LANE), lambda i: (0, F_COL_BLK)),
                  pl.BlockSpec((1, LANE), lambda i: (0, 0)),
                  pl.BlockSpec((LP, LANE), lambda i: (0, 0))],
        out_specs=(pl.BlockSpec((LP, LANE), lambda i: (0, 0)), pl.BlockSpec((1, LANE), lambda i: (0, 0))),
        compiler_params=_params(("arbitrary",)))(proj, bf_p, dc)


def _attn_fwd(q, k, v, c_col, c_row, name):
    Hh, LP, Dh = q.shape
    nb = LP // ATT_BLK
    scale = Dh ** -0.5
    nt = (((1,), (1,)), ((), ()))

    def body(q_ref, k_ref, v_ref, cc_ref, cr_ref, o_ref, lse_ref):
        i = pl.program_id(1)
        qb = q_ref[...]
        cq = cc_ref[...]
        rows = i * ATT_BLK + lax.broadcasted_iota(jnp.int32, (ATT_BLK, ATT_BLK), 0)
        cols0 = lax.broadcasted_iota(jnp.int32, (ATT_BLK, ATT_BLK), 1)

        def step(j, carry):
            m, l, acc = carry
            ks = pl.ds(pl.multiple_of(j * ATT_BLK, ATT_BLK), ATT_BLK)
            s = lax.dot_general(qb, k_ref[ks, :], nt, preferred_element_type=F32) * scale
            s = s + cq - cr_ref[j]
            s = jnp.where(cols0 + j * ATT_BLK <= rows, s, NEG)
            m_new = jnp.maximum(m, jnp.max(s, axis=1, keepdims=True))
            p = jnp.exp(s - m_new)
            alpha = jnp.exp(m - m_new)
            l = alpha * l + jnp.sum(p, axis=1, keepdims=True)
            acc = alpha * acc + jnp.dot(p.astype(BF16), v_ref[ks, :], preferred_element_type=F32)
            return m_new, l, acc

        init = (jnp.full((ATT_BLK, 1), NEG, F32), jnp.zeros((ATT_BLK, 1), F32),
                jnp.zeros((ATT_BLK, Dh), F32))
        m, l, acc = lax.fori_loop(0, i + 1, step, init)
        o_ref[...] = acc / l
        lse_ref[...] = m + jnp.log(l)

    qspec = pl.BlockSpec((None, ATT_BLK, Dh), lambda h, i: (h, i, 0))
    kspec = pl.BlockSpec((None, LP, Dh), lambda h, i: (h, 0, 0))
    colspec = pl.BlockSpec((None, ATT_BLK, 1), lambda h, i: (h, i, 0))
    rowspec = pl.BlockSpec((None, nb, 1, ATT_BLK), lambda h, i: (h, 0, 0, 0))
    return pl.pallas_call(
        body, name=name,
        out_shape=(jax.ShapeDtypeStruct((Hh, LP, Dh), F32), jax.ShapeDtypeStruct((Hh, LP, 1), F32)),
        grid=(Hh, nb), in_specs=[qspec, kspec, kspec, colspec, rowspec],
        out_specs=(qspec, colspec),
        compiler_params=_params(("parallel", "arbitrary")))(q, k, v, c_col, c_row)


def _attn_bwd(q, k, v, o, do, lse, c_col, c_row, name):
    Hh, LP, Dh = q.shape
    nb = LP // ATT_BLK
    scale = Dh ** -0.5
    nt = (((1,), (1,)), ((), ()))
    tn = (((0,), (0,)), ((), ()))

    def body(q_ref, k_ref, v_ref, o_ref, do_ref, lse_ref, cc_ref, cr_ref,
             dq_ref, dk_ref, dv_ref, dcq_ref, dc_ref, delta_ref):
        j = pl.program_id(1)

        @pl.when(j == 0)
        def _():
            dq_ref[...] = jnp.zeros_like(dq_ref)
            dcq_ref[...] = jnp.zeros_like(dcq_ref)
            dob_all = do_ref[...].astype(BF16).astype(F32)
            delta_ref[...] = jnp.sum(dob_all * o_ref[...], axis=1, keepdims=True)

        kb = k_ref[...]
        vb = v_ref[...]
        ck = cr_ref[j]
        rows0 = lax.broadcasted_iota(jnp.int32, (ATT_BLK, ATT_BLK), 0)
        cols = j * ATT_BLK + lax.broadcasted_iota(jnp.int32, (ATT_BLK, ATT_BLK), 1)

        def step(i, carry):
            dk, dv, dcs = carry
            qs = pl.ds(pl.multiple_of(i * ATT_BLK, ATT_BLK), ATT_BLK)
            qb = q_ref[qs, :]
            dob = do_ref[qs, :].astype(BF16)
            s = lax.dot_general(qb, kb, nt, preferred_element_type=F32) * scale
            s = s + cc_ref[qs, :] - ck
            s = jnp.where(cols <= rows0 + i * ATT_BLK, s, NEG)
            p = jnp.exp(s - lse_ref[qs, :])
            dp = lax.dot_general(dob, vb, nt, preferred_element_type=F32)
            ds = p * (dp - delta_ref[qs, :])
            dsb = ds.astype(BF16)
            dv = dv + lax.dot_general(p.astype(BF16), dob, tn, preferred_element_type=F32)
            dk = dk + lax.dot_general(dsb, qb, tn, preferred_element_type=F32) * scale
            dq_ref[qs, :] += jnp.dot(dsb, kb, preferred_element_type=F32) * scale
            dcq_ref[qs, :] += jnp.sum(ds, axis=1, keepdims=True)
            dcs = dcs - jnp.sum(ds, axis=0, keepdims=True)
            return dk, dv, dcs

        init = (jnp.zeros((ATT_BLK, Dh), F32), jnp.zeros((ATT_BLK, Dh), F32),
                jnp.zeros((1, ATT_BLK), F32))
        dk, dv, dcs = lax.fori_loop(j, nb, step, init)
        dk_ref[...] = dk
        dv_ref[...] = dv
        dc_ref[...] = dcs

    full = pl.BlockSpec((None, LP, Dh), lambda h, j: (h, 0, 0))
    blk = pl.BlockSpec((None, ATT_BLK, Dh), lambda h, j: (h, j, 0))
    col = pl.BlockSpec((None, LP, 1), lambda h, j: (h, 0, 0))
    rowspec = pl.BlockSpec((None, nb, 1, ATT_BLK), lambda h, j: (h, 0, 0, 0))
    return pl.pallas_call(
        body, name=name,
        out_shape=(jax.ShapeDtypeStruct((Hh, LP, Dh), F32), jax.ShapeDtypeStruct((Hh, LP, Dh), F32),
                   jax.ShapeDtypeStruct((Hh, LP, Dh), F32), jax.ShapeDtypeStruct((Hh, LP, 1), F32),
                   jax.ShapeDtypeStruct((Hh, nb, 1, ATT_BLK), F32)),
        grid=(Hh, nb), in_specs=[full, blk, blk, full, full, col, col, rowspec],
        out_specs=(full, blk, blk, col, pl.BlockSpec((None, None, 1, ATT_BLK), lambda h, j: (h, j, 0, 0))),
        scratch_shapes=[pltpu.VMEM((LP, 1), F32)],
        compiler_params=_params(("parallel", "arbitrary")))(q, k, v, o, do, lse, c_col, c_row)


AUG = 80
ONES_IN_K = HEAD_DIM
ONES_IN_Q = HEAD_DIM + 3
ATT_HEADS_PER_STEP = 8
ATT_HEADS_PER_STEP_BWD = 4


def _split3(c):
    rnd = lambda t: lax.reduce_precision(t, exponent_bits=8, mantissa_bits=7)
    hi = rnd(c)
    r1 = c - hi
    mid = rnd(r1)
    lo = rnd(r1 - mid)
    return hi.astype(BF16), mid.astype(BF16), lo.astype(BF16)


def _blocks_t(t, LP):
    Hh, _, Rr = t.shape
    return t.reshape(Hh, LP // ATT_BLK, ATT_BLK, Rr).transpose(0, 1, 3, 2)


def _attn_operands(q, k, v, cT, LP):
    hd = lambda t: t.reshape(LP, HEADS, HEAD_DIM).transpose(1, 0, 2)
    parts = [p[:, :, None] for p in _split3(cT)]
    ones = jnp.ones((HEADS, LP, 3), BF16)
    zeros = jnp.zeros((HEADS, LP, AUG - HEAD_DIM - 6), BF16)
    q_aug = jnp.concatenate([hd((q * (HEAD_DIM ** -0.5)).astype(BF16))] + parts + [ones, zeros], axis=2)
    k_aug = jnp.concatenate([hd(k.astype(BF16)), ones] + [-p for p in parts] + [zeros], axis=2)
    vh = hd(v.astype(BF16))
    return _blocks_t(q_aug, LP), _blocks_t(k_aug, LP), k_aug, _blocks_t(vh, LP), vh


def _attn_fwd2(qT, k_aug, vT, name, comm=None):
    Hh, nb, _, _ = qT.shape
    LP = nb * ATT_BLK
    Dh = vT.shape[2]
    HB = ATT_HEADS_PER_STEP
    n_comm = 0 if comm is None else len(comm[0])

    def body(*refs):
        q_ref, k_ref, v_ref = refs[:3]
        o_ref, lse_ref = refs[3 + n_comm:5 + n_comm]
        if n_comm:
            start, wait = _gather_first_ops(refs[3:3 + n_comm], refs[5 + n_comm:5 + 2 * n_comm],
                                            refs[5 + 2 * n_comm], refs[6 + 2 * n_comm], comm[1])
            pl.when(pl.program_id(0) == 0)(start)
        keys = lax.broadcasted_iota(jnp.int32, (ATT_BLK, ATT_BLK), 0)
        qrys = lax.broadcasted_iota(jnp.int32, (ATT_BLK, ATT_BLK), 1)
        causal = keys <= qrys

        def q_block(i, _):
            def tile(j, carry, masked):
                ks = pl.ds(pl.multiple_of(j * ATT_BLK, ATT_BLK), ATT_BLK)
                out = []
                for hh in range(HB):
                    m, l, acc = carry[hh]
                    s = jnp.dot(k_ref[hh, ks, :], q_ref[hh, i], preferred_element_type=F32)
                    if masked:
                        s = jnp.where(causal, s, NEG)
                    m_new = jnp.maximum(m, jnp.max(s, axis=0, keepdims=True))
                    p = jnp.exp(s - m_new)
                    alpha = jnp.exp(m - m_new)
                    l = alpha * l + jnp.sum(p, axis=0, keepdims=True)
                    acc = alpha * acc + jnp.dot(v_ref[hh, j], p.astype(BF16), preferred_element_type=F32)
                    out.append((m_new, l, acc))
                return tuple(out)

            init = tuple((jnp.full((1, ATT_BLK), NEG, F32), jnp.zeros((1, ATT_BLK), F32),
                          jnp.zeros((Dh, ATT_BLK), F32)) for _ in range(HB))
            carry = lax.fori_loop(0, i, lambda j, cr: tile(j, cr, False), init)
            carry = tile(i, carry, True)
            for hh in range(HB):
                m, l, acc = carry[hh]
                o_ref[hh, i] = acc / l
                lse_ref[hh, i] = m + jnp.log(l)
            return 0

        lax.fori_loop(0, nb, q_block, 0)
        if n_comm:
            pl.when(pl.program_id(0) == Hh // HB - 1)(wait)

    blk = lambda r: pl.BlockSpec((HB, nb, r, ATT_BLK), lambda h: (h, 0, 0, 0))
    out_shape = (jax.ShapeDtypeStruct((Hh, nb, Dh, ATT_BLK), F32), jax.ShapeDtypeStruct((Hh, nb, 1, ATT_BLK), F32))
    scratch = []
    args = [qT, k_aug, vT]
    if n_comm:
        out_shape += tuple(jax.ShapeDtypeStruct(cut.full, s.dtype) for cut, s in zip(comm[1], comm[0]))
        scratch = [pltpu.SemaphoreType.DMA((4 * n_comm,)), pltpu.SemaphoreType.DMA((4 * n_comm,))]
        args += list(comm[0])
    return pl.pallas_call(
        body, name=name, out_shape=out_shape, grid=(Hh // HB,),
        in_specs=[blk(AUG), pl.BlockSpec((HB, LP, AUG), lambda h: (h, 0, 0)), blk(Dh)] + [ANY] * n_comm,
        out_specs=(blk(Dh), blk(1)) + (ANY,) * n_comm, scratch_shapes=scratch,
        compiler_params=_params(("arbitrary",)))(*args)


def _attn_bwd2(qT, kT, k_aug, v, oT, doT, lse, name, comm=None):
    Hh, nb, _, _ = qT.shape
    LP = nb * ATT_BLK
    Dh = v.shape[2]
    nt = (((1,), (1,)), ((), ()))

    HB = ATT_HEADS_PER_STEP_BWD
    n_comm = 0 if comm is None else len(comm[0])

    def body(*refs):
        q_ref, kt_ref, k_ref, v_ref, o_ref, do_ref, lse_ref = refs[:7]
        parts = refs[7:7 + n_comm]
        dq_ref, dk_ref, dv_ref = refs[7 + n_comm:10 + n_comm]
        others = refs[10 + n_comm:10 + 2 * n_comm]
        delta_ref = refs[10 + 2 * n_comm]
        if n_comm:
            start, wait = _chip_exchange_ops(parts, others, refs[11 + 2 * n_comm], refs[12 + 2 * n_comm], comm[1])
            pl.when(pl.program_id(0) == 0)(start)
        keys = lax.broadcasted_iota(jnp.int32, (ATT_BLK, ATT_BLK), 0)
        qrys = lax.broadcasted_iota(jnp.int32, (ATT_BLK, ATT_BLK), 1)
        causal = keys <= qrys

        def prep(i, _):
            for hh in range(HB):
                delta_ref[hh, i] = jnp.sum(do_ref[hh, i].astype(F32) * o_ref[hh, i], axis=0, keepdims=True)
                dq_ref[hh, i] = jnp.zeros((AUG, ATT_BLK), F32)
            return 0

        lax.fori_loop(0, nb, prep, 0)

        def kv_block(j, _):
            ks = pl.ds(pl.multiple_of(j * ATT_BLK, ATT_BLK), ATT_BLK)

            def tile(i, carry, masked):
                out = []
                for hh in range(HB):
                    dk, dv = carry[hh]
                    qb = q_ref[hh, i]
                    dob = do_ref[hh, i]
                    s = jnp.dot(k_ref[hh, ks, :], qb, preferred_element_type=F32)
                    if masked:
                        s = jnp.where(causal, s, NEG)
                    p = jnp.exp(s - lse_ref[hh, i])
                    dp = jnp.dot(v_ref[hh, ks, :], dob, preferred_element_type=F32)
                    ds = (p * (dp - delta_ref[hh, i])).astype(BF16)
                    dv = dv + lax.dot_general(p.astype(BF16), dob, nt, preferred_element_type=F32)
                    dk = dk + lax.dot_general(ds, qb, nt, preferred_element_type=F32)
                    dq_ref[hh, i] += jnp.dot(kt_ref[hh, j], ds, preferred_element_type=F32)
                    out.append((dk, dv))
                return tuple(out)

            init = tuple((jnp.zeros((ATT_BLK, AUG), F32), jnp.zeros((ATT_BLK, Dh), F32)) for _ in range(HB))
            carry = tile(j, init, True)
            carry = lax.fori_loop(j + 1, nb, lambda i, cr: tile(i, cr, False), carry)
            for hh in range(HB):
                dk_ref[hh, ks, :] = carry[hh][0]
                dv_ref[hh, ks, :] = carry[hh][1]
            return 0

        lax.fori_loop(0, nb, kv_block, 0)
        if n_comm:
            pl.when(pl.program_id(0) == Hh // HB - 1)(wait)

    blk = lambda r: pl.BlockSpec((HB, nb, r, ATT_BLK), lambda h: (h, 0, 0, 0))
    row = lambda cols: pl.BlockSpec((HB, LP, cols), lambda h: (h, 0, 0))
    out_shape = (jax.ShapeDtypeStruct((Hh, nb, AUG, ATT_BLK), F32), jax.ShapeDtypeStruct((Hh, LP, AUG), F32),
                 jax.ShapeDtypeStruct((Hh, LP, Dh), F32))
    scratch = [pltpu.VMEM((HB, nb, 1, ATT_BLK), F32)]
    args = [qT, kT, k_aug, v, oT, doT, lse]
    if n_comm:
        out_shape += _chip_exchange_shapes(*comm)
        scratch += [pltpu.SemaphoreType.DMA((3 * n_comm,)), pltpu.SemaphoreType.DMA((3 * n_comm,))]
        args += list(comm[0])
    return pl.pallas_call(
        body, name=name, out_shape=out_shape, grid=(Hh // HB,),
        in_specs=[blk(AUG), blk(AUG), row(AUG), row(Dh), blk(Dh), blk(Dh), blk(1)] + [ANY] * n_comm,
        out_specs=(blk(AUG), row(AUG), row(Dh)) + (ANY,) * n_comm,
        scratch_shapes=scratch,
        compiler_params=_params(("arbitrary",)))(*args)


CONV_HALO = 32
A_BLK = 3 * FOX_W // CONV_CH
G_BLK = A_BLK + 1


def _conf_fwd(proj, cw, cb, lg, lb, tm, name):
    LP = proj.shape[0]
    C = CONV_CH
    sub = _sub_rows(tm)
    hpb = tm // CONV_HALO

    def body(a_ref, g_ref, ah_ref, gh_ref, w_ref, cb_ref, lg_ref, lb_ref, u1_ref, u_ref, buf):
        r = pl.program_id(0)
        buf[CONV_HALO:CONV_HALO + tm, :] = a_ref[...] * _sigmoid(g_ref[...])
        buf[0:CONV_HALO, :] = jnp.where(r > 0, ah_ref[...] * _sigmoid(gh_ref[...]), 0.0)
        for s in range(tm // sub):
            for ct in range(C // LANE):
                ln = slice(ct * LANE, (ct + 1) * LANE)
                acc = jnp.broadcast_to(cb_ref[:, ln], (sub, LANE))
                for kk in range(CONV_K):
                    off = CONV_HALO + s * sub - (CONV_K - 1) + kk
                    acc = acc + w_ref[kk:kk + 1, ln] * buf[off:off + sub, ln]
                u1_ref[s * sub:(s + 1) * sub, ln] = acc
        u1 = u1_ref[...]
        mu = jnp.mean(u1, axis=1, keepdims=True)
        xc = u1 - mu
        var = jnp.mean(xc * xc, axis=1, keepdims=True)
        y = xc * lax.rsqrt(var + LN_EPS) * lg_ref[...] + lb_ref[...]
        u_ref[...] = (y * _sigmoid(y)).astype(u_ref.dtype)

    cur = lambda blk: pl.BlockSpec((tm, C), lambda r: (r, blk))
    halo = lambda blk: pl.BlockSpec((CONV_HALO, C), lambda r: (jnp.maximum(r * hpb - 1, 0), blk))
    vec = pl.BlockSpec((1, C), lambda r: (0, 0))
    out = pl.BlockSpec((tm, C), lambda r: (r, 0))
    return pl.pallas_call(
        body, name=name,
        out_shape=(jax.ShapeDtypeStruct((LP, C), F32), jax.ShapeDtypeStruct((LP, C), BF16)),
        grid=(LP // tm,),
        in_specs=[cur(A_BLK), cur(G_BLK), halo(A_BLK), halo(G_BLK),
                  pl.BlockSpec((CONV_HALO, C), lambda r: (0, 0)), vec, vec, vec],
        out_specs=(out, out),
        scratch_shapes=[pltpu.VMEM((CONV_HALO + tm, C), F32)],
        compiler_params=_params(("parallel",)))(proj, proj, proj, proj, cw, cb, lg, lb)


def _conf_bwd(proj, u1, dcat, cw, lg, lb, tm, name):
    LP = proj.shape[0]
    C = CONV_CH
    sub = _sub_rows(tm)
    hpb = tm // CONV_HALO
    nblk = LP // tm
    last_halo = LP // CONV_HALO - 1

    def body(a_ref, g_ref, ah_ref, gh_ref, u1_ref, u1n_ref, du_ref, dun_ref, w_ref, lg_ref, lb_ref,
             dadg_ref, dw_ref, dcb_ref, dlg_ref, dlb_ref, ubuf, dbuf, du0):
        r = pl.program_id(0)
        lgv = lg_ref[...]
        lbv = lb_ref[...]

        def ln_silu_bwd(u1v, duv):
            mu = jnp.mean(u1v, axis=1, keepdims=True)
            xc = u1v - mu
            rstd = lax.rsqrt(jnp.mean(xc * xc, axis=1, keepdims=True) + LN_EPS)
            xhat = xc * rstd
            y = xhat * lgv + lbv
            sg = _sigmoid(y)
            dy = duv * (sg * (1.0 + y * (1.0 - sg)))
            dxh = dy * lgv
            du1 = rstd * (dxh - jnp.mean(dxh, axis=1, keepdims=True)
                          - xhat * jnp.mean(dxh * xhat, axis=1, keepdims=True))
            return du1, dy, xhat

        @pl.when(r == 0)
        def _():
            dw_ref[...] = jnp.zeros_like(dw_ref)
            dcb_ref[...] = jnp.zeros_like(dcb_ref)
            dlg_ref[...] = jnp.zeros_like(dlg_ref)
            dlb_ref[...] = jnp.zeros_like(dlb_ref)

        du1, dy, xhat = ln_silu_bwd(u1_ref[...], du_ref[...])
        dlg_ref[...] += jnp.sum(dy * xhat, axis=0, keepdims=True)
        dlb_ref[...] += jnp.sum(dy, axis=0, keepdims=True)
        dcb_ref[...] += jnp.sum(du1, axis=0, keepdims=True)
        dbuf[0:tm, :] = du1
        du1n, _, _ = ln_silu_bwd(u1n_ref[...], dun_ref[...])
        dbuf[tm:tm + CONV_HALO, :] = jnp.where(r < nblk - 1, du1n, 0.0)
        ubuf[CONV_HALO:CONV_HALO + tm, :] = a_ref[...] * _sigmoid(g_ref[...])
        ubuf[0:CONV_HALO, :] = jnp.where(r > 0, ah_ref[...] * _sigmoid(gh_ref[...]), 0.0)

        for ct in range(C // LANE):
            ln = slice(ct * LANE, (ct + 1) * LANE)
            for s in range(tm // sub):
                d_here = dbuf[s * sub:(s + 1) * sub, ln]
                acc = jnp.zeros((sub, LANE), F32)
                for kk in range(CONV_K):
                    fo = s * sub + (CONV_K - 1) - kk
                    acc = acc + w_ref[kk:kk + 1, ln] * dbuf[fo:fo + sub, ln]
                    bo = CONV_HALO + s * sub - (CONV_K - 1) + kk
                    dw_ref[kk:kk + 1, ln] += jnp.sum(d_here * ubuf[bo:bo + sub, ln], axis=0, keepdims=True)
                du0[s * sub:(s + 1) * sub, ln] = acc
        a = a_ref[...]
        sg = _sigmoid(g_ref[...])
        d0 = du0[...]
        dadg_ref[:, 0:C] = (d0 * sg).astype(dadg_ref.dtype)
        dadg_ref[:, C:2 * C] = (d0 * a * sg * (1.0 - sg)).astype(dadg_ref.dtype)

    cur = lambda blk: pl.BlockSpec((tm, C), lambda r: (r, blk))
    prev = lambda blk: pl.BlockSpec((CONV_HALO, C), lambda r: (jnp.maximum(r * hpb - 1, 0), blk))
    nxt = lambda blk: pl.BlockSpec((CONV_HALO, C), lambda r: (jnp.minimum((r + 1) * hpb, last_halo), blk))
    vec = pl.BlockSpec((1, C), lambda r: (0, 0))
    wspec = pl.BlockSpec((CONV_HALO, C), lambda r: (0, 0))
    return pl.pallas_call(
        body, name=name,
        out_shape=(jax.ShapeDtypeStruct((LP, 2 * C), BF16), jax.ShapeDtypeStruct((CONV_HALO, C), F32),
                   jax.ShapeDtypeStruct((1, C), F32), jax.ShapeDtypeStruct((1, C), F32),
                   jax.ShapeDtypeStruct((1, C), F32)),
        grid=(nblk,),
        in_specs=[cur(A_BLK), cur(G_BLK), prev(A_BLK), prev(G_BLK), cur(0), nxt(0), cur(1), nxt(1),
                  wspec, vec, vec],
        out_specs=(pl.BlockSpec((tm, 2 * C), lambda r: (r, 0)), wspec, vec, vec, vec),
        scratch_shapes=[pltpu.VMEM((CONV_HALO + tm, C), F32), pltpu.VMEM((tm + CONV_HALO, C), F32),
                        pltpu.VMEM((tm, C), F32)],
        compiler_params=_params(("arbitrary",)))(proj, proj, proj, proj, u1, u1, dcat, dcat, cw, lg, lb)


FFN_HALO = 8
FFN_TC = 256
FFN_K = 3


def _ffn_conv(buf, w_ref, b_ref, s, sub, ln):
    acc = jnp.broadcast_to(b_ref[:, ln], (sub, LANE))
    for kk in range(FFN_K):
        off = FFN_HALO + s * sub - (FFN_K - 1) + kk
        acc = acc + w_ref[kk:kk + 1, ln] * buf[off:off + sub, ln]
    return acc


def _ffn_act_fwd(up, w, b, tm, name):
    LP, F = up.shape[0], up.shape[1] // 2
    upg = upv = up
    nct = F // FFN_TC
    sub = _sub_rows(tm)
    hpb = tm // FFN_HALO

    def body(g_ref, v_ref, gh_ref, vh_ref, wg_ref, wv_ref, bg_ref, bv_ref, act_ref, gbuf, vbuf):
        r = pl.program_id(1)
        gbuf[FFN_HALO:FFN_HALO + tm, :] = g_ref[...]
        vbuf[FFN_HALO:FFN_HALO + tm, :] = v_ref[...]
        gbuf[0:FFN_HALO, :] = jnp.where(r > 0, gh_ref[...], 0.0)
        vbuf[0:FFN_HALO, :] = jnp.where(r > 0, vh_ref[...], 0.0)
        for s in range(tm // sub):
            for ct in range(FFN_TC // LANE):
                ln = slice(ct * LANE, (ct + 1) * LANE)
                gc = _ffn_conv(gbuf, wg_ref, bg_ref, s, sub, ln)
                vc = _ffn_conv(vbuf, wv_ref, bv_ref, s, sub, ln)
                act_ref[s * sub:(s + 1) * sub, ln] = (gc * _sigmoid(gc) * vc).astype(act_ref.dtype)

    cur = pl.BlockSpec((tm, FFN_TC), lambda c, r: (r, c))
    halo = pl.BlockSpec((FFN_HALO, FFN_TC), lambda c, r: (jnp.maximum(r * hpb - 1, 0), c))
    wg = pl.BlockSpec((8, FFN_TC), lambda c, r: (0, c))
    wv = pl.BlockSpec((8, FFN_TC), lambda c, r: (0, nct + c))
    bg = pl.BlockSpec((1, FFN_TC), lambda c, r: (0, c))
    bv = pl.BlockSpec((1, FFN_TC), lambda c, r: (0, nct + c))
    curv = pl.BlockSpec((tm, FFN_TC), lambda c, r: (r, nct + c))
    halov = pl.BlockSpec((FFN_HALO, FFN_TC), lambda c, r: (jnp.maximum(r * hpb - 1, 0), nct + c))
    return pl.pallas_call(
        body, name=name, out_shape=jax.ShapeDtypeStruct((LP, F), BF16), grid=(nct, LP // tm),
        in_specs=[cur, curv, halo, halov, wg, wv, bg, bv], out_specs=cur,
        scratch_shapes=[pltpu.VMEM((FFN_HALO + tm, FFN_TC), F32)] * 2,
        compiler_params=_params(("parallel", "parallel")))(upg, upv, upg, upv, w, w, b, b)


def _ffn_act_bwd(up, dact, w, b, tm, name):
    LP, F = up.shape[0], up.shape[1] // 2
    upg = upv = up
    nct = F // FFN_TC
    sub = _sub_rows(tm)
    hpb = tm // FFN_HALO
    nblk = LP // tm
    last_halo = LP // FFN_HALO - 1
    TB = tm + 2 * FFN_HALO

    def body(g_ref, v_ref, gp_ref, vp_ref, gn_ref, vn_ref, da_ref, dan_ref,
             wg_ref, wv_ref, bg_ref, bv_ref,
             dup_ref, dwg_ref, dwv_ref, dbg_ref, dbv_ref, gbuf, vbuf, dgb, dvb):
        r = pl.program_id(1)
        dg_ref = dup_ref.at[0]
        dv_ref = dup_ref.at[1]
        first = r == 0
        last = r == nblk - 1

        @pl.when(first)
        def _():
            dwg_ref[...] = jnp.zeros_like(dwg_ref)
            dwv_ref[...] = jnp.zeros_like(dwv_ref)
            dbg_ref[...] = jnp.zeros_like(dbg_ref)
            dbv_ref[...] = jnp.zeros_like(dbv_ref)

        for buf, c_ref, p_ref, n_ref in ((gbuf, g_ref, gp_ref, gn_ref), (vbuf, v_ref, vp_ref, vn_ref)):
            buf[0:FFN_HALO, :] = jnp.where(first, 0.0, p_ref[...])
            buf[FFN_HALO:FFN_HALO + tm, :] = c_ref[...]
            buf[FFN_HALO + tm:TB, :] = jnp.where(last, 0.0, n_ref[...])

        def dconv(s0, nrows, ln, dact_v):
            gc = jnp.broadcast_to(bg_ref[:, ln], (nrows, LANE))
            vc = jnp.broadcast_to(bv_ref[:, ln], (nrows, LANE))
            for kk in range(FFN_K):
                off = s0 - (FFN_K - 1) + kk
                gc = gc + wg_ref[kk:kk + 1, ln] * gbuf[off:off + nrows, ln]
                vc = vc + wv_ref[kk:kk + 1, ln] * vbuf[off:off + nrows, ln]
            sg = _sigmoid(gc)
            return dact_v * vc * (sg * (1.0 + gc * (1.0 - sg))), dact_v * (gc * sg)

        for ct in range(FFN_TC // LANE):
            ln = slice(ct * LANE, (ct + 1) * LANE)
            for s in range(tm // sub):
                dgc, dvc = dconv(FFN_HALO + s * sub, sub, ln, da_ref[s * sub:(s + 1) * sub, ln])
                dgb[s * sub:(s + 1) * sub, ln] = dgc
                dvb[s * sub:(s + 1) * sub, ln] = dvc
            dgc, dvc = dconv(FFN_HALO + tm, FFN_HALO, ln, jnp.where(last, 0.0, dan_ref[:, ln]))
            dgb[tm:tm + FFN_HALO, ln] = dgc
            dvb[tm:tm + FFN_HALO, ln] = dvc
            for dbuf, ubuf, w_ref, dw_ref, db_ref, dout in (
                    (dgb, gbuf, wg_ref, dwg_ref, dbg_ref, dg_ref), (dvb, vbuf, wv_ref, dwv_ref, dbv_ref, dv_ref)):
                for s in range(tm // sub):
                    d_here = dbuf[s * sub:(s + 1) * sub, ln]
                    acc = jnp.zeros((sub, LANE), F32)
                    for kk in range(FFN_K):
                        fo = s * sub + (FFN_K - 1) - kk
                        acc = acc + w_ref[kk:kk + 1, ln] * dbuf[fo:fo + sub, ln]
                        bo = FFN_HALO + s * sub - (FFN_K - 1) + kk
                        dw_ref[kk:kk + 1, ln] += jnp.sum(d_here * ubuf[bo:bo + sub, ln], axis=0, keepdims=True)
                    db_ref[:, ln] += jnp.sum(d_here, axis=0, keepdims=True)
                    dout[s * sub:(s + 1) * sub, ln] = acc.astype(dout.dtype)

    cur = pl.BlockSpec((tm, FFN_TC), lambda c, r: (r, c))
    prev = pl.BlockSpec((FFN_HALO, FFN_TC), lambda c, r: (jnp.maximum(r * hpb - 1, 0), c))
    nxt = pl.BlockSpec((FFN_HALO, FFN_TC), lambda c, r: (jnp.minimum((r + 1) * hpb, last_halo), c))
    wg = pl.BlockSpec((8, FFN_TC), lambda c, r: (0, c))
    wv = pl.BlockSpec((8, FFN_TC), lambda c, r: (0, nct + c))
    bg = pl.BlockSpec((1, FFN_TC), lambda c, r: (0, c))
    bv = pl.BlockSpec((1, FFN_TC), lambda c, r: (0, nct + c))
    curv = pl.BlockSpec((tm, FFN_TC), lambda c, r: (r, nct + c))
    prevv = pl.BlockSpec((FFN_HALO, FFN_TC), lambda c, r: (jnp.maximum(r * hpb - 1, 0), nct + c))
    nxtv = pl.BlockSpec((FFN_HALO, FFN_TC), lambda c, r: (jnp.minimum((r + 1) * hpb, last_halo), nct + c))
    dup, dwg, dwv, dbg, dbv = pl.pallas_call(
        body, name=name,
        out_shape=(jax.ShapeDtypeStruct((2, LP, F), BF16),
                   jax.ShapeDtypeStruct((8, F), F32), jax.ShapeDtypeStruct((8, F), F32),
                   jax.ShapeDtypeStruct((1, F), F32), jax.ShapeDtypeStruct((1, F), F32)),
        grid=(nct, nblk),
        in_specs=[cur, curv, prev, prevv, nxt, nxtv, cur, nxt, wg, wv, bg, bv],
        out_specs=(pl.BlockSpec((2, tm, FFN_TC), lambda c, r: (0, r, c)),
                   pl.BlockSpec((8, FFN_TC), lambda c, r: (0, c)),
                   pl.BlockSpec((8, FFN_TC), lambda c, r: (0, c)),
                   pl.BlockSpec((1, FFN_TC), lambda c, r: (0, c)),
                   pl.BlockSpec((1, FFN_TC), lambda c, r: (0, c))),
        scratch_shapes=[pltpu.VMEM((TB, FFN_TC), F32), pltpu.VMEM((TB, FFN_TC), F32),
                        pltpu.VMEM((tm + FFN_HALO, FFN_TC), F32), pltpu.VMEM((tm + FFN_HALO, FFN_TC), F32)],
        compiler_params=_params(("parallel", "arbitrary")))(
            upg, upv, upg, upv, upg, upv, dact, dact, w, w, b, b)
    return dup, jnp.concatenate([dwg, dwv], axis=1), jnp.concatenate([dbg, dbv], axis=1)


POOL_HALO = 16


def _pool_fwd(h, g, pw, pb, ps, tm, name):
    LP, Dm = h.shape
    sub = _sub_rows(tm)
    hpb = tm // POOL_HALO

    def body(h_ref, hh_ref, g_ref, pw_ref, pb_ref, ps_ref, o_ref, d_ref, buf):
        r = pl.program_id(0)
        gg = g_ref[...]

        def norm(x):
            return x * lax.rsqrt(jnp.mean(x * x, axis=1, keepdims=True) + RMS_EPS) * gg

        x = h_ref[...]
        buf[POOL_HALO:POOL_HALO + tm, :] = norm(x)
        buf[0:POOL_HALO, :] = jnp.where(r > 0, norm(hh_ref[...]), 0.0)
        for gi, w in enumerate(POOL_WINDOWS):
            ln = slice(gi * POOL_G, (gi + 1) * POOL_G)
            for s in range(tm // sub):
                base = POOL_HALO + s * sub
                acc = buf[base:base + sub, ln]
                for jj in range(1, w):
                    acc = acc + buf[base - jj:base - jj + sub, ln]
                t = r * tm + s * sub + lax.broadcasted_iota(jnp.int32, (sub, 1), 0)
                cnt = jnp.minimum(t + 1, w).astype(F32)
                d_ref[s * sub:(s + 1) * sub, ln] = (acc / cnt - buf[base:base + sub, ln]).astype(d_ref.dtype)
            y = jnp.dot(d_ref[:, ln], pw_ref[gi], preferred_element_type=F32) + pb_ref[:, ln]
            o_ref[:, ln] = x[:, ln] + y * ps_ref[:, ln]

    row = pl.BlockSpec((tm, Dm), lambda r: (r, 0))
    halo = pl.BlockSpec((POOL_HALO, Dm), lambda r: (jnp.maximum(r * hpb - 1, 0), 0))
    vec = pl.BlockSpec((1, Dm), lambda r: (0, 0))
    wsp = pl.BlockSpec((len(POOL_WINDOWS), POOL_G, POOL_G), lambda r: (0, 0, 0))
    return pl.pallas_call(
        body, name=name,
        out_shape=(jax.ShapeDtypeStruct((LP, Dm), F32), jax.ShapeDtypeStruct((LP, Dm), BF16)),
        grid=(LP // tm,), in_specs=[row, halo, vec, wsp, vec, vec], out_specs=(row, row),
        scratch_shapes=[pltpu.VMEM((POOL_HALO + tm, Dm), F32)],
        compiler_params=_params(("parallel",)))(h, h, g, pw, pb, ps)


def _pool_bwd(h, g, d, pw, pb, ps, dh_out, tm, name):
    LP, Dm = h.shape
    sub = _sub_rows(tm)
    hpb = tm // POOL_HALO
    nblk = LP // tm
    last_halo = LP // POOL_HALO - 1
    nt = (((1,), (1,)), ((), ()))
    tn = (((0,), (0,)), ((), ()))

    def body(h_ref, g_ref, d_ref, pw_ref, pb_ref, ps_ref, do_ref, don_ref,
             dh_ref, dpw_ref, dpb_ref, dps_ref, dg_ref, ebuf, ddb, dnb):
        r = pl.program_id(0)

        @pl.when(r == 0)
        def _():
            dpw_ref[...] = jnp.zeros_like(dpw_ref)
            dpb_ref[...] = jnp.zeros_like(dpb_ref)
            dps_ref[...] = jnp.zeros_like(dps_ref)
            dg_ref[...] = jnp.zeros_like(dg_ref)

        for gi, w in enumerate(POOL_WINDOWS):
            ln = slice(gi * POOL_G, (gi + 1) * POOL_G)
            wg = pw_ref[gi]
            dog = do_ref[:, ln]
            dg_b = d_ref[:, ln]
            y_pre = jnp.dot(dg_b, wg, preferred_element_type=F32) + pb_ref[:, ln]
            dps_ref[:, ln] += jnp.sum(dog * y_pre, axis=0, keepdims=True)
            dy = dog * ps_ref[:, ln]
            dpb_ref[:, ln] += jnp.sum(dy, axis=0, keepdims=True)
            dyb = dy.astype(BF16)
            dpw_ref[gi] += lax.dot_general(dg_b, dyb, tn, preferred_element_type=F32)
            dd = lax.dot_general(dyb, wg, nt, preferred_element_type=F32)
            ddb[:, ln] = dd
            t = r * tm + lax.broadcasted_iota(jnp.int32, (tm, 1), 0)
            ebuf[0:tm, ln] = dd / jnp.minimum(t + 1, w).astype(F32)
            dyn = (don_ref[:, ln] * ps_ref[:, ln]).astype(BF16)
            ddn = lax.dot_general(dyn, wg, nt, preferred_element_type=F32)
            tn_ = (r + 1) * tm + lax.broadcasted_iota(jnp.int32, (POOL_HALO, 1), 0)
            ebuf[tm:tm + POOL_HALO, ln] = jnp.where(r < nblk - 1, ddn / jnp.minimum(tn_ + 1, w).astype(F32), 0.0)
            for s in range(tm // sub):
                acc = ebuf[s * sub:(s + 1) * sub, ln]
                for jj in range(1, w):
                    acc = acc + ebuf[s * sub + jj:s * sub + jj + sub, ln]
                dnb[s * sub:(s + 1) * sub, ln] = acc - ddb[s * sub:(s + 1) * sub, ln]
        x = h_ref[...]
        rr = lax.rsqrt(jnp.mean(x * x, axis=1, keepdims=True) + RMS_EPS)
        xhat = x * rr
        dn = dnb[...]
        dxh = dn * g_ref[...]
        dh_ref[...] = do_ref[...] + rr * (dxh - xhat * jnp.mean(dxh * xhat, axis=1, keepdims=True))
        dg_ref[...] += jnp.sum(dn * xhat, axis=0, keepdims=True)

    row = pl.BlockSpec((tm, Dm), lambda r: (r, 0))
    nxt = pl.BlockSpec((POOL_HALO, Dm), lambda r: (jnp.minimum((r + 1) * hpb, last_halo), 0))
    vec = pl.BlockSpec((1, Dm), lambda r: (0, 0))
    wsp = pl.BlockSpec((len(POOL_WINDOWS), POOL_G, POOL_G), lambda r: (0, 0, 0))
    return pl.pallas_call(
        body, name=name,
        out_shape=(jax.ShapeDtypeStruct((LP, Dm), F32),
                   jax.ShapeDtypeStruct((len(POOL_WINDOWS), POOL_G, POOL_G), F32),
                   jax.ShapeDtypeStruct((1, Dm), F32), jax.ShapeDtypeStruct((1, Dm), F32),
                   jax.ShapeDtypeStruct((1, Dm), F32)),
        grid=(nblk,), in_specs=[row, vec, row, wsp, vec, vec, row, nxt],
        out_specs=(row, wsp, vec, vec, vec),
        scratch_shapes=[pltpu.VMEM((tm + POOL_HALO, Dm), F32), pltpu.VMEM((tm, Dm), F32),
                        pltpu.VMEM((tm, Dm), F32)],
        compiler_params=_params(("arbitrary",)))(h, g, d, pw, pb, ps, dh_out, dh_out)


def _adamw(w, g, m, v, name):
    shape = w.shape
    cols = shape[-1]
    rows = int(np.prod(shape[:-1])) if len(shape) > 1 else 1
    w2, g2, m2, v2 = (t.reshape(rows, cols) for t in (w, g, m, v))
    tr = rows
    for cand in (256, 128, 64, 32, 16, 8):
        if rows % cand == 0 and rows > cand:
            tr = cand
            break
    c1 = float(1.0 - ADAM_B1 ** ADAM_STEP)
    c2 = float(1.0 - ADAM_B2 ** ADAM_STEP)

    def body(w_ref, g_ref, m_ref, v_ref, d_ref, mo_ref, vo_ref):
        gg = g_ref[...]
        mn = ADAM_B1 * m_ref[...] + (1.0 - ADAM_B1) * gg
        vn = ADAM_B2 * v_ref[...] + (1.0 - ADAM_B2) * (gg * gg)
        m_hat = mn / c1
        v_hat = vn / c2
        d_ref[...] = -ADAM_LR * (m_hat / (jnp.sqrt(v_hat) + ADAM_EPS) + ADAM_WD * w_ref[...])
        mo_ref[...] = mn
        vo_ref[...] = vn

    spec = pl.BlockSpec((tr, cols), lambda i: (i, 0))
    sds = jax.ShapeDtypeStruct((rows, cols), F32)
    d2, mo, vo = pl.pallas_call(
        body, name=name, out_shape=(sds, sds, sds), grid=(rows // tr,),
        in_specs=[spec] * 4, out_specs=(spec,) * 3,
        compiler_params=_params(("parallel",)))(w2, g2, m2, v2)
    return d2.reshape(shape), mo.reshape(shape), vo.reshape(shape)


def _row_tiles(LP):
    tm = LP // 4
    assert LP % 4 == 0 and tm % CONV_HALO == 0 and LP % ATT_BLK == 0, LP
    return tm, LP // 2


def _heads(t, LP):
    return t.reshape(LP, HEADS, HEAD_DIM).transpose(1, 0, 2)


def _unheads(t, LP):
    return t.transpose(1, 0, 2).reshape(LP, FOX_W)


def _ffn_fwd(h, gain, wug, wuv, cw, cb, wd, tm, tmm, tag):
    n = _rms_fwd(h, gain, BF16, tm, f"ffn_norm_{tag}")
    upg = _mm(n, wug, "nn", F32, tmm, 256, f"ffn_up_gate_{tag}")
    upv = _mm(n, wuv, "nn", F32, tmm, 256, f"ffn_up_val_{tag}")
    act = _ffn_act_fwd(upg, upv, cw, cb, tm, f"ffn_act_{tag}")
    out = _mm(act, wd, "nn", F32, tmm, 512, f"ffn_down_{tag}", add=h)
    return out, (n, upg, upv, act)


def _ffn_bwd(h, gain, wug, wuv, cw, cb, wd, saved, dout, tm, tmm, tag):
    n, upg, upv, act = saved
    dact = _mm(dout, wd, "nt", F32, tmm, 256, f"ffn_dact_{tag}")
    dwd = _mm(act, dout, "tn", F32, 256, 512, f"ffn_dwdown_{tag}")
    dupg, dupv, dcw, dcb = _ffn_act_bwd(upg, upv, dact, cw, cb, tm, f"ffn_act_bwd_{tag}")
    dn = _mm(dupg, wug, "nt", F32, tm, 512, f"ffn_dn_gate_{tag}")
    dn = _mm(dupv, wuv, "nt", F32, tm, 512, f"ffn_dn_val_{tag}", add=dn)
    dwug = _mm(n, dupg, "tn", F32, 512, 256, f"ffn_dwup_gate_{tag}")
    dwuv = _mm(n, dupv, "tn", F32, 512, 256, f"ffn_dwup_val_{tag}")
    dh, dgain = _rms_bwd(h, gain, dn, dout, tm, f"ffn_norm_bwd_{tag}")
    return dh, dict(gain=dgain, wug=dwug, wuv=dwuv, cw=dcw[:FFN_K], cb=dcb, wd=dwd)


def _local_step(h0, tgt, W, n_real):
    LP = h0.shape[0]
    tm, tmm = _row_tiles(LP)
    nb = LP // ATT_BLK
    G = {}

    n0 = _rms_fwd(h0, W["mix_norm_even"], BF16, tm, "mix_norm_even")
    proj = _mm(n0, W["w_in_p"], "nn", F32, tmm, 384, "in_proj")
    c = _fgate_fwd(proj, W["b_f_p"], "forget_gate")
    cT = c[:, :HEADS].T
    c_col = cT[:, :, None]
    c_row = cT.reshape(HEADS, nb, 1, ATT_BLK)
    qkv = proj[:, :3 * FOX_W].astype(BF16)
    q, k, v = (_heads(qkv[:, i * FOX_W:(i + 1) * FOX_W], LP) for i in range(3))
    o, lse = _attn_fwd(q, k, v, c_col, c_row, "fox_attention")
    u1, u = _conf_fwd(proj, W["conv_w_p"], W["conv_b"], W["ln_g"], W["ln_b"], tm, "conformer")
    cat = jnp.concatenate([_unheads(o, LP).astype(BF16), u], axis=1)
    h1 = _mm(cat, W["w_out"], "nn", F32, tmm, 512, "out_proj", add=h0)
    h2, ffn0 = _ffn_fwd(h1, W["ffn_norm"][0:1], W["w_up_g"][0], W["w_up_v"][0], W["ffn_conv_w_p"][0],
                        W["ffn_conv_b"][0:1], W["w_down"][0], tm, tmm, "0")
    h3, dpool = _pool_fwd(h2, W["mix_norm_odd"], W["pool_w"], W["pool_b"], W["pool_scale"], tm, "pool_mixer")
    h4, ffn1 = _ffn_fwd(h3, W["ffn_norm"][1:2], W["w_up_g"][1], W["w_up_v"][1], W["ffn_conv_w_p"][1],
                        W["ffn_conv_b"][1:2], W["w_down"][1], tm, tmm, "1")
    loss, dh4, G["final_norm"] = _loss_head(h4, W["final_norm"], tgt, n_real, tm, "loss_head")

    dh3, g1 = _ffn_bwd(h3, W["ffn_norm"][1:2], W["w_up_g"][1], W["w_up_v"][1], W["ffn_conv_w_p"][1],
                       W["ffn_conv_b"][1:2], W["w_down"][1], ffn1, dh4, tm, tmm, "1")
    dh2, G["pool_w"], G["pool_b"], G["pool_scale"], G["mix_norm_odd"] = _pool_bwd(
        h2, W["mix_norm_odd"], dpool, W["pool_w"], W["pool_b"], W["pool_scale"], dh3, tm, "pool_mixer_bwd")
    dh1, g0 = _ffn_bwd(h1, W["ffn_norm"][0:1], W["w_up_g"][0], W["w_up_v"][0], W["ffn_conv_w_p"][0],
                       W["ffn_conv_b"][0:1], W["w_down"][0], ffn0, dh2, tm, tmm, "0")
    for key in ("gain", "wug", "wuv", "cw", "cb", "wd"):
        G["ffn_" + key] = (g0[key], g1[key])

    dcat = _mm(dh1, W["w_out"], "nt", F32, tmm, 512, "out_proj_dx")
    G["w_out"] = _mm(cat, dh1, "tn", F32, 512, 512, "out_proj_dw")
    dadg, dcw, G["conv_b"], G["ln_g"], G["ln_b"] = _conf_bwd(
        proj, u1, dcat, W["conv_w_p"], W["ln_g"], W["ln_b"], tm, "conformer_bwd")
    G["conv_w"] = dcw[:CONV_K]
    do = _heads(dcat[:, :FOX_W], LP)
    dq, dk, dv, dcq, dck = _attn_bwd(q, k, v, o, do, lse, c_col, c_row, "fox_attention_bwd")
    dc = jnp.pad((dcq.reshape(HEADS, LP) + dck.reshape(HEADS, LP)).T, ((0, 0), (0, LANE - HEADS)))
    dfl, dbf = _fgate_bwd(proj, W["b_f_p"], dc, "forget_gate_bwd")
    G["b_f"] = dbf[:, :HEADS]
    dproj = jnp.concatenate([_unheads(t, LP).astype(BF16) for t in (dq, dk, dv)] + [dadg, dfl], axis=1)
    dn0 = _mm(dproj, W["w_in_p"], "nt", F32, tmm, 512, "in_proj_dx")
    G["w_in_p"] = _mm(n0, dproj, "tn", F32, 512, 384, "in_proj_dw")
    dh0, G["mix_norm_even"] = _rms_bwd(h0, W["mix_norm_even"], dn0, dh1, tm, "mix_norm_even_bwd")
    return loss, dh0, G


def _compute_layout(P):
    w_in = P["w_in"].reshape(D_MODEL, IN_COLS)
    qkv, f, ag = w_in[:, :3 * FOX_W], w_in[:, 3 * FOX_W:3 * FOX_W + HEADS], w_in[:, 3 * FOX_W + HEADS:]
    w_in_p = jnp.concatenate([qkv, ag, f, jnp.zeros((D_MODEL, LANE - HEADS), w_in.dtype)], axis=1).astype(BF16)
    w_up = P["w_up"].astype(BF16)
    return dict(
        mix_norm_even=P["mix_norm_even"].reshape(1, D_MODEL).astype(F32),
        w_in_p=w_in_p,
        b_f_p=jnp.pad(P["b_f"].reshape(1, HEADS).astype(F32), ((0, 0), (0, LANE - HEADS))),
        conv_w_p=jnp.pad(P["conv_w"].reshape(CONV_K, CONV_CH).astype(F32), ((0, CONV_HALO - CONV_K), (0, 0))),
        conv_b=P["conv_b"].reshape(1, CONV_CH).astype(F32),
        ln_g=P["ln_g"].reshape(1, CONV_CH).astype(F32),
        ln_b=P["ln_b"].reshape(1, CONV_CH).astype(F32),
        w_out=P["w_out"].reshape(D_MODEL, D_MODEL).astype(BF16),
        mix_norm_odd=P["mix_norm_odd"].reshape(1, D_MODEL).astype(F32),
        pool_w=P["pool_w"].reshape(len(POOL_WINDOWS), POOL_G, POOL_G).astype(BF16),
        pool_b=P["pool_b"].reshape(1, D_MODEL).astype(F32),
        pool_scale=P["pool_scale"].reshape(1, D_MODEL).astype(F32),
        ffn_norm=P["ffn_norm"].astype(F32),
        w_up_g=w_up[:, :, :D_FF],
        w_up_v=w_up[:, :, D_FF:],
        ffn_conv_w_p=jnp.pad(P["ffn_conv_w"].astype(F32), ((0, 0), (0, 8 - FFN_K), (0, 0))),
        ffn_conv_b=P["ffn_conv_b"].astype(F32),
        w_down=P["w_down"].astype(BF16),
        final_norm=P["final_norm"].reshape(1, D_MODEL).astype(F32),
    )


def _reference_layout(G, dh0):
    gp = G["w_in_p"]
    g_w_in = jnp.concatenate([gp[:, :3 * FOX_W], gp[:, 3 * FOX_W + 2 * CONV_CH:3 * FOX_W + 2 * CONV_CH + HEADS],
                              gp[:, 3 * FOX_W:3 * FOX_W + 2 * CONV_CH]], axis=1)
    return dict(
        meta_tokens=dh0[:N_META],
        mix_norm_even=G["mix_norm_even"],
        w_in=g_w_in[None],
        b_f=G["b_f"],
        conv_w=G["conv_w"][None],
        conv_b=G["conv_b"],
        ln_g=G["ln_g"],
        ln_b=G["ln_b"],
        w_out=G["w_out"][None],
        mix_norm_odd=G["mix_norm_odd"],
        pool_w=G["pool_w"][None],
        pool_b=G["pool_b"].reshape(1, len(POOL_WINDOWS), POOL_G),
        pool_scale=G["pool_scale"],
        ffn_norm=jnp.concatenate(G["ffn_gain"], axis=0),
        w_up=jnp.stack([jnp.concatenate([g, v], axis=1) for g, v in zip(G["ffn_wug"], G["ffn_wuv"])]),
        ffn_conv_w=jnp.stack(G["ffn_cw"]),
        ffn_conv_b=jnp.concatenate(G["ffn_cb"], axis=0),
        w_down=jnp.stack(G["ffn_wd"]),
        final_norm=G["final_norm"].reshape(D_MODEL),
    )


MESH = pl.DeviceIdType.MESH
ANY = pl.BlockSpec(memory_space=pl.ANY)
PACK_COLS = 1024


def _coords():
    return lax.axis_index("x"), lax.axis_index("y"), lax.axis_index("c")


def _other_chips(x, y):
    return [(1 - x, y), (x, 1 - y), (1 - x, 1 - y)]


def _allgather_chips(pack):
    R, C = pack.shape
    R2 = R // 2

    def body(x_ref, o_ref, send_sems, recv_sems, local_sem):
        x, y, c = _coords()
        sibling = (x, y, 1 - c)
        chips = _other_chips(x, y)

        def slot(px, py, half):
            return o_ref.at[2 * px + py, pl.ds(half * R2, R2), :]

        def copy(k, src, dst, to):
            return pltpu.make_async_remote_copy(src_ref=src, dst_ref=dst, send_sem=send_sems.at[k],
                                                recv_sem=recv_sems.at[k], device_id=to, device_id_type=MESH)

        mine = pltpu.make_async_copy(x_ref, o_ref.at[2 * x + y], local_sem)
        mine.start()
        my_half = x_ref.at[pl.ds(c * R2, R2), :]
        first = [copy(j, my_half, slot(x, y, c), (*chip, c)) for j, chip in enumerate(chips)]
        for cp in first:
            cp.start()
        passed = [copy(3 + j, slot(*chip, c), slot(*chip, c), sibling) for j, chip in enumerate(chips)]
        for j, chip in enumerate(chips):
            copy(j, my_half, slot(*chip, c), sibling).wait_recv()
            passed[j].start()
        for j, chip in enumerate(chips):
            copy(3 + j, my_half, slot(*chip, 1 - c), sibling).wait_recv()
        for cp in first + passed:
            cp.wait_send()
        mine.wait()

    return pl.pallas_call(
        body, name="allgather_weights", out_shape=jax.ShapeDtypeStruct((N_CHIPS, R, C), pack.dtype),
        in_specs=[ANY], out_specs=ANY,
        scratch_shapes=[pltpu.SemaphoreType.DMA((6,)), pltpu.SemaphoreType.DMA((6,)), pltpu.SemaphoreType.DMA],
    )(pack)


def _pair_exchange(G):
    n, R, C = G.shape
    R2 = R // 2

    def body(g_ref, o_ref, send_sem, recv_sem):
        x, y, c = _coords()
        src = g_ref.at[pl.ds(0, n), pl.ds((1 - c) * R2, R2), :]
        cp = pltpu.make_async_remote_copy(src_ref=src, dst_ref=o_ref, send_sem=send_sem, recv_sem=recv_sem,
                                          device_id=(x, y, 1 - c), device_id_type=MESH)
        cp.start()
        cp.wait()

    return pl.pallas_call(
        body, name="grad_pair_exchange", out_shape=jax.ShapeDtypeStruct((n, R2, C), G.dtype),
        in_specs=[ANY], out_specs=ANY,
        scratch_shapes=[pltpu.SemaphoreType.DMA, pltpu.SemaphoreType.DMA],
    )(G)


def _row_tile(rows, align, cap):
    best = None
    for t in range(align, min(rows, cap) + 1, align):
        if rows % t == 0:
            best = t
    assert best is not None, (rows, align, cap)
    return best


def _pair_sum(G, recv):
    n, R, C = G.shape
    R2 = R // 2
    tr = _row_tile(R2, 16, 704)
    nrb = R2 // tr
    half = lax.axis_index("c").astype(jnp.int32).reshape(1)

    def body(c_ref, g_ref, r_ref, o_ref):
        o_ref[...] = (g_ref[...] + r_ref[...]).astype(o_ref.dtype)

    return pl.pallas_call(
        body, name="grad_pair_sum", out_shape=jax.ShapeDtypeStruct((n, R2, C), BF16),
        grid_spec=pltpu.PrefetchScalarGridSpec(
            num_scalar_prefetch=1, grid=(n, nrb),
            in_specs=[pl.BlockSpec((None, tr, C), lambda j, i, c_ref: (j, c_ref[0] * nrb + i, 0)),
                      pl.BlockSpec((None, tr, C), lambda j, i, c_ref: (j, i, 0))],
            out_specs=pl.BlockSpec((None, tr, C), lambda j, i, c_ref: (j, i, 0))),
        compiler_params=_params(("parallel", "parallel")))(half, G, recv)


def _chip_exchange(P):
    n, R2, C = P.shape

    def body(p_ref, o_ref, send_sems, recv_sems, local_sem):
        x, y, c = _coords()
        me = 2 * x + y
        chips = _other_chips(x, y)
        mine = pltpu.make_async_copy(p_ref.at[me], o_ref.at[me], local_sem)
        mine.start()
        sends = [pltpu.make_async_remote_copy(
            src_ref=p_ref.at[2 * px + py], dst_ref=o_ref.at[me], send_sem=send_sems.at[k],
            recv_sem=recv_sems.at[k], device_id=(px, py, c), device_id_type=MESH)
            for k, (px, py) in enumerate(chips)]
        for cp in sends:
            cp.start()
        for k, (px, py) in enumerate(chips):
            pltpu.make_async_remote_copy(
                src_ref=p_ref.at[me], dst_ref=o_ref.at[2 * px + py], send_sem=send_sems.at[k],
                recv_sem=recv_sems.at[k], device_id=(px, py, c), device_id_type=MESH).wait_recv()
        for cp in sends:
            cp.wait_send()
        mine.wait()

    return pl.pallas_call(
        body, name="grad_chip_exchange", out_shape=jax.ShapeDtypeStruct((n, R2, C), P.dtype),
        in_specs=[ANY], out_specs=ANY,
        scratch_shapes=[pltpu.SemaphoreType.DMA((3,)), pltpu.SemaphoreType.DMA((3,)), pltpu.SemaphoreType.DMA],
    )(P)


def _chip_sum(X):
    n, R2, C = X.shape
    tr = _row_tile(R2, 16, 704)

    def body(x_ref, o_ref):
        acc = x_ref[0].astype(F32)
        for s in range(1, n):
            acc = acc + x_ref[s].astype(F32)
        o_ref[...] = acc

    return pl.pallas_call(
        body, name="grad_chip_sum", out_shape=jax.ShapeDtypeStruct((R2, C), F32), grid=(R2 // tr,),
        in_specs=[pl.BlockSpec((n, tr, C), lambda i: (0, i, 0))],
        out_specs=pl.BlockSpec((tr, C), lambda i: (i, 0)),
        compiler_params=_params(("parallel",)))(X)


def _pair_allgather(Q):
    R2, C = Q.shape

    def body(q_ref, o_ref, send_sem, recv_sem, local_sem):
        x, y, c = _coords()
        mine = pltpu.make_async_copy(q_ref, o_ref.at[c], local_sem)
        mine.start()
        cp = pltpu.make_async_remote_copy(src_ref=q_ref, dst_ref=o_ref.at[c], send_sem=send_sem,
                                          recv_sem=recv_sem, device_id=(x, y, 1 - c), device_id_type=MESH)
        cp.start()
        pltpu.make_async_remote_copy(src_ref=q_ref, dst_ref=o_ref.at[1 - c], send_sem=send_sem,
                                     recv_sem=recv_sem, device_id=(x, y, 1 - c), device_id_type=MESH).wait_recv()
        cp.wait_send()
        mine.wait()

    return pl.pallas_call(
        body, name="grad_pair_allgather", out_shape=jax.ShapeDtypeStruct((2, R2, C), Q.dtype),
        in_specs=[ANY], out_specs=ANY,
        scratch_shapes=[pltpu.SemaphoreType.DMA, pltpu.SemaphoreType.DMA, pltpu.SemaphoreType.DMA],
    )(Q)


def _allreduce_small(pack):
    Rs, C = pack.shape
    n_dev = 8

    def body(x_ref, o_ref, buf, send_sems, recv_sems):
        x, y, c = _coords()
        me = 4 * x + 2 * y + c
        buf[me] = x_ref[...]
        peers = []
        for rel in range(1, n_dev):
            px = 1 - x if rel & 4 else x
            py = 1 - y if rel & 2 else y
            pc = 1 - c if rel & 1 else c
            peers.append((px, py, pc))
        sends = [pltpu.make_async_remote_copy(
            src_ref=x_ref, dst_ref=buf.at[me], send_sem=send_sems.at[k], recv_sem=recv_sems.at[k],
            device_id=peer, device_id_type=MESH) for k, peer in enumerate(peers)]
        for cp in sends:
            cp.start()
        for k, (px, py, pc) in enumerate(peers):
            pltpu.make_async_remote_copy(
                src_ref=x_ref, dst_ref=buf.at[4 * px + 2 * py + pc], send_sem=send_sems.at[k],
                recv_sem=recv_sems.at[k], device_id=(px, py, pc), device_id_type=MESH).wait_recv()
        for cp in sends:
            cp.wait_send()
        acc = buf[0]
        for d in range(1, n_dev):
            acc = acc + buf[d]
        o_ref[...] = acc

    vm = pl.BlockSpec(memory_space=pltpu.VMEM)
    return pl.pallas_call(
        body, name="allreduce_replicated", out_shape=jax.ShapeDtypeStruct((Rs, C), F32),
        in_specs=[vm], out_specs=vm,
        scratch_shapes=[pltpu.VMEM((n_dev, Rs, C), F32), pltpu.SemaphoreType.DMA((n_dev - 1,)),
                        pltpu.SemaphoreType.DMA((n_dev - 1,))],
    )(pack)


SHARDED = (
    ("w_in", 2, True), ("w_out", 1, True), ("pool_w", 2, True), ("w_up", 2, True), ("w_down", 1, True),
    ("meta_tokens", 1, False), ("mix_norm_odd", 1, False), ("pool_b", 2, False), ("pool_scale", 1, False),
    ("conv_w", 2, False), ("ffn_conv_w", 2, False))
REPLICATED = ("mix_norm_even", "b_f", "conv_b", "ln_g", "ln_b", "ffn_norm", "ffn_conv_b", "final_norm")
PACK_ROW_ALIGN = 32


def _pad_rows(flat, align_rows, cols):
    rows = -(-flat.shape[-1] // cols)
    rows = -(-rows // align_rows) * align_rows
    pad = rows * cols - flat.shape[-1]
    flat = jnp.pad(flat, [(0, 0)] * (flat.ndim - 1) + [(0, pad)])
    return flat.reshape(flat.shape[:-1] + (rows, cols))


def _pack_weight_shards(shards):
    parts = []
    for name, _, as_bf16 in SHARDED:
        w = shards[name].astype(F32).reshape(-1)
        parts.append(w.astype(BF16) if as_bf16 else lax.bitcast_convert_type(w, BF16).reshape(-1))
    return _pad_rows(jnp.concatenate(parts), PACK_ROW_ALIGN, PACK_COLS)


def _unpack_weights(gathered, shards):
    flat = gathered.reshape(N_CHIPS, -1)
    out, off = {}, 0
    for name, axis, as_bf16 in SHARDED:
        shp = shards[name].shape
        n = int(np.prod(shp))
        if as_bf16:
            t = flat[:, off:off + n]
            off += n
        else:
            t = lax.bitcast_convert_type(flat[:, off:off + 2 * n].reshape(N_CHIPS, n, 2), F32)
            off += 2 * n
        t = t.reshape((N_CHIPS,) + shp)
        out[name] = jnp.concatenate([t[j] for j in range(N_CHIPS)], axis=axis)
    return out


def _pack_grad_shards(grads, shards):
    parts = []
    for name, axis, _ in SHARDED:
        g = grads[name].reshape(shards[name].shape[:axis] + (N_CHIPS, shards[name].shape[axis])
                                + shards[name].shape[axis + 1:])
        parts.append(jnp.moveaxis(g, axis, 0).reshape(N_CHIPS, -1))
    return _pad_rows(jnp.concatenate(parts, axis=1), PACK_ROW_ALIGN, PACK_COLS)


def _unpack_grad_shard(reduced, shards):
    flat = reduced.reshape(-1)
    out, off = {}, 0
    for name, _, _ in SHARDED:
        shp = shards[name].shape
        n = int(np.prod(shp))
        out[name] = flat[off:off + n].reshape(shp)
        off += n
    return out


def _pack_replicated(grads, loss):
    parts = [_pad_rows(grads[name].astype(F32).reshape(-1), 1, LANE).reshape(-1) for name in REPLICATED]
    parts.append(_pad_rows(loss.reshape(-1)[:1], 1, LANE).reshape(-1))
    return _pad_rows(jnp.concatenate(parts), 8, LANE)


def _unpack_replicated(reduced, shapes):
    flat = reduced.reshape(-1)
    out, off = {}, 0
    for name in REPLICATED:
        n = int(np.prod(shapes[name]))
        out[name] = flat[off:off + n].reshape(shapes[name])
        off += -(-n // LANE) * LANE
    return out, flat[off]


def _ffn_fwd2(h, W, layer, tm, tmm):
    tag = str(layer)
    n = _rms_fwd(h, W["ffn_norm"][layer:layer + 1], BF16, tm, f"ffn_norm_{tag}")
    up = _mm(n, W["w_up"], "nn", F32, tmm, 512, f"ffn_up_{tag}", b_lead=layer)
    act = _ffn_act_fwd(up, W["ffn_conv_w_p"][layer], W["ffn_conv_b"][layer:layer + 1], tm, f"ffn_act_{tag}")
    out = _mm(act, W["w_down"], "nn", F32, tmm, 512, f"ffn_down_{tag}", add=h, b_lead=layer)
    return out, (n, up, act)


def _ffn_bwd2(h, W, layer, saved, dout, acc, tm, tmm):
    tag = str(layer)
    n, up, act = saved
    dact = _mm(dout, W["w_down"], "nt", F32, tmm, 256, f"ffn_dact_{tag}", b_lead=layer)
    dwd = _mm(act, dout, "tn", F32, 256, 512, f"ffn_dwdown_{tag}",
              out=(layer, 2, None if acc is None else acc[1]))
    dup, dcw, dcb = _ffn_act_bwd(up, dact, W["ffn_conv_w_p"][layer], W["ffn_conv_b"][layer:layer + 1], tm,
                                 f"ffn_act_bwd_{tag}")
    dn = _mm_ffn_dn(dup, W["w_up"], layer, tm, 512, f"ffn_dn_{tag}")
    dwu = _mm_ffn_dwup(n, dup, layer, None if acc is None else acc[0], 512, 256, f"ffn_dwup_{tag}")
    dh, dgain = _rms_bwd(h, W["ffn_norm"][layer:layer + 1], dn, dout, tm, f"ffn_norm_bwd_{tag}")
    return dh, (dwu, dwd), dict(gain=dgain, cw=dcw[:FFN_K], cb=dcb)


GATHER_FIRST = ("w_in", "w_out", "small")
GATHER_LATE = ("pool_w", "w_up", "w_down")
HOSTED = ("w_out", "pool_w", "w_up", "w_down")
LATE = ("w_in", "small")


def _local_step2(h0, tgt, W, n_real, cut_of):
    LP = h0.shape[0]
    tm, tmm = _row_tiles(LP)
    nb = LP // ATT_BLK
    G = {}
    n0 = _rms_fwd(h0, W["mix_norm_even"], BF16, tm, "mix_norm_even")
    proj = _mm(n0, W["w_in_p"], "nn", F32, tmm, 384, "in_proj")
    c = _fgate_fwd(proj, W["b_f_p"], "forget_gate")
    cT = c[:, :HEADS].T
    qT, kT, k_aug, vT, vh = _attn_operands(proj[:, :FOX_W], proj[:, FOX_W:2 * FOX_W],
                                           proj[:, 2 * FOX_W:3 * FOX_W], cT, LP)
    gcuts = [cut_of[n] for n in GATHER_LATE]
    oT, lse, *gfull = _attn_fwd2(qT, k_aug, vT, "fox_attention", comm=(W["late_shards"], gcuts))
    W = dict(W)
    W.update(zip(GATHER_LATE, _gather_forward(gfull, gcuts)))
    from_t = lambda t: t.transpose(1, 3, 0, 2).reshape(LP, FOX_W)
    u1, u = _conf_fwd(proj, W["conv_w_p"], W["conv_b"], W["ln_g"], W["ln_b"], tm, "conformer")
    cat = jnp.concatenate([from_t(oT).astype(BF16), u], axis=1)
    h1 = _mm(cat, W["w_out"], "nn", F32, tmm, 512, "out_proj", add=h0)
    h2, ffn0 = _ffn_fwd2(h1, W, 0, tm, tmm)
    h3, dpool = _pool_fwd(h2, W["mix_norm_odd"], W["pool_w"], W["pool_b"], W["pool_scale"], tm, "pool_mixer")
    h4, ffn1 = _ffn_fwd2(h3, W, 1, tm, tmm)
    loss, dh4, G["final_norm"] = _loss_head(h4, W["final_norm"], tgt, n_real, tm, "loss_head")

    dh3, acc, g1 = _ffn_bwd2(h3, W, 1, ffn1, dh4, None, tm, tmm)
    dh2, G["pool_w"], G["pool_b"], G["pool_scale"], G["mix_norm_odd"] = _pool_bwd(
        h2, W["mix_norm_odd"], dpool, W["pool_w"], W["pool_b"], W["pool_scale"], dh3, tm, "pool_mixer_bwd")
    dh1, acc, g0 = _ffn_bwd2(h1, W, 0, ffn0, dh2, acc, tm, tmm)
    G["w_up"], G["w_down"] = acc
    G["ffn_norm"] = jnp.concatenate([g0["gain"], g1["gain"]], axis=0)
    G["ffn_conv_w"] = jnp.stack([g0["cw"], g1["cw"]])
    G["ffn_conv_b"] = jnp.concatenate([g0["cb"], g1["cb"]], axis=0)

    dcat = _mm(dh1, W["w_out"], "nt", F32, tmm, 512, "out_proj_dx")
    G["w_out"] = _mm(cat, dh1, "tn", F32, 512, 512, "out_proj_dw")
    dadg, dcw, G["conv_b"], G["ln_g"], G["ln_b"] = _conf_bwd(
        proj, u1, dcat, W["conv_w_p"], W["ln_g"], W["ln_b"], tm, "conformer_bwd")
    G["conv_w"] = dcw[:CONV_K]
    doT = dcat[:, :FOX_W].astype(BF16).reshape(nb, ATT_BLK, HEADS, HEAD_DIM).transpose(2, 0, 3, 1)
    hcuts = [cut_of[n] for n in HOSTED]
    hfull = [G[n] for n in HOSTED]
    hrecv = _pair_exchange2(hfull, hcuts, "grad_pair_exchange_early")
    hparts = [_pair_sum2(f, r, cut, PAIR_SUM_BLOCKS[n], "grad_pair_sum_" + n)
              for f, r, cut, n in zip(hfull, hrecv, hcuts, HOSTED)]
    dqT, dk_aug, dv, *hothers = _attn_bwd2(qT, kT, k_aug, vh, oT, doT, lse, "fox_attention_bwd",
                                           comm=(hparts, hcuts))
    dq = from_t(dqT[:, :, :HEAD_DIM, :] * (HEAD_DIM ** -0.5))
    dk, dv = (t.transpose(1, 0, 2).reshape(LP, FOX_W) for t in (dk_aug[:, :, :HEAD_DIM], dv))
    dc = dqT[:, :, ONES_IN_K, :].reshape(HEADS, LP) - dk_aug[:, :, ONES_IN_Q]
    dfl, dbf = _fgate_bwd(proj, W["b_f_p"], jnp.pad(dc.T, ((0, 0), (0, LANE - HEADS))), "forget_gate_bwd")
    G["b_f"] = dbf[:, :HEADS]
    dproj = jnp.concatenate([t.astype(BF16) for t in (dq, dk, dv)] + [dadg, dfl], axis=1)
    dn0 = _mm(dproj, W["w_in_p"], "nt", F32, tmm, 512, "in_proj_dx")
    G["w_in_p"] = _mm(n0, dproj, "tn", F32, 512, 384, "in_proj_dw")
    dh0, G["mix_norm_even"] = _rms_bwd(h0, W["mix_norm_even"], dn0, dh1, tm, "mix_norm_even_bwd")
    return loss, dh0, G, dict(zip(HOSTED, hparts)), dict(zip(HOSTED, hothers))


class _Cut:
    def __init__(self, full_shape, chip_dim, half_dim):
        self.full = tuple(full_shape)
        self.chip_dim, self.half_dim = chip_dim, half_dim
        self.chip_size = full_shape[chip_dim] // N_CHIPS
        self.half_size = full_shape[half_dim] // 2
        assert chip_dim != half_dim

    def shape(self, chip=False, half=False):
        s = list(self.full)
        if chip:
            s[self.chip_dim] = self.chip_size
        if half:
            s[self.half_dim] = self.half_size
        return tuple(s)

    def region(self, ref, chip=None, half=None):
        idx = [pl.ds(0, n) for n in ref.shape]
        if chip is not None:
            idx[self.chip_dim] = pl.ds(chip * self.chip_size, self.chip_size)
        if half is not None:
            idx[self.half_dim] = pl.ds(half * self.half_size, self.half_size)
        return ref.at[tuple(idx)]


SMALL_SHARDED = ("meta_tokens", "mix_norm_odd", "pool_b", "pool_scale", "conv_w", "ffn_conv_w")
SMALL_ROWS = 144


def _cuts():
    return {
        "w_in": _Cut((N_CHIPS, D_MODEL, IN_SHARD), 0, 1),
        "w_out": _Cut((D_MODEL, D_MODEL), 0, 1),
        "pool_w": _Cut((len(POOL_WINDOWS), POOL_G, POOL_G), 1, 0),
        "w_up": _Cut((2, D_MODEL, 2 * D_FF), 2, 1),
        "w_down": _Cut((2, D_FF, D_MODEL), 1, 2),
        "small": _Cut((N_CHIPS, SMALL_ROWS, LANE), 0, 1),
    }


COMM_ORDER = ("w_in", "w_out", "pool_w", "w_up", "w_down", "small")


def _remote(src, dst, send_sems, recv_sems, k, to):
    return pltpu.make_async_remote_copy(src_ref=src, dst_ref=dst, send_sem=send_sems.at[k],
                                        recv_sem=recv_sems.at[k], device_id=to, device_id_type=MESH)


def _gather_weights(shards, cuts):
    n = len(shards)

    def body(*refs):
        srcs, outs = refs[:n], refs[n:2 * n]
        send_sems, recv_sems = refs[2 * n:]
        x, y, c = _coords()
        me = 2 * x + y
        sibling = (x, y, 1 - c)
        chips = _other_chips(x, y)
        sends = []
        for t, cut in enumerate(cuts):
            push = _remote(srcs[t], cut.region(outs[t], chip=me), send_sems, recv_sems, 7 * t, sibling)
            push.start()
            sends.append(push)
            for kk, chip in enumerate(chips):
                cp = _remote(cut.region(srcs[t], half=c), cut.region(outs[t], chip=me, half=c),
                             send_sems, recv_sems, 7 * t + 1 + kk, (*chip, c))
                cp.start()
                sends.append(cp)
        for t, cut in enumerate(cuts):
            for kk, (px, py) in enumerate(chips):
                landed = cut.region(outs[t], chip=2 * px + py, half=c)
                _remote(landed, landed, send_sems, recv_sems, 7 * t + 1 + kk, sibling).wait_recv()
                fwd = _remote(landed, landed, send_sems, recv_sems, 7 * t + 4 + kk, sibling)
                fwd.start()
                sends.append(fwd)
        for t, cut in enumerate(cuts):
            mine = cut.region(outs[t], chip=me)
            _remote(mine, mine, send_sems, recv_sems, 7 * t, sibling).wait_recv()
            for kk, (px, py) in enumerate(chips):
                other = cut.region(outs[t], chip=2 * px + py, half=1 - c)
                _remote(other, other, send_sems, recv_sems, 7 * t + 4 + kk, sibling).wait_recv()
        for cp in sends:
            cp.wait_send()

    return pl.pallas_call(
        body, name="gather_weights",
        out_shape=tuple(jax.ShapeDtypeStruct(cut.full, s.dtype) for cut, s in zip(cuts, shards)),
        in_specs=[ANY] * n, out_specs=tuple([ANY] * n),
        scratch_shapes=[pltpu.SemaphoreType.DMA((7 * n,)), pltpu.SemaphoreType.DMA((7 * n,))],
    )(*shards)


def _gather_first_ops(srcs, outs, send_sems, recv_sems, cuts):
    x, y, c = _coords()
    me = 2 * x + y
    sibling = (x, y, 1 - c)
    chips = _other_chips(x, y)

    def copies():
        out = []
        for t, cut in enumerate(cuts):
            out.append(_remote(srcs[t], cut.region(outs[t], chip=me), send_sems, recv_sems, 4 * t, sibling))
            for kk, chip in enumerate(chips):
                out.append(_remote(cut.region(srcs[t], half=c), cut.region(outs[t], chip=me, half=c),
                                   send_sems, recv_sems, 4 * t + 1 + kk, (*chip, c)))
        return out

    def start():
        for cp in copies():
            cp.start()

    def wait():
        for t, cut in enumerate(cuts):
            mine = cut.region(outs[t], chip=me)
            _remote(mine, mine, send_sems, recv_sems, 4 * t, sibling).wait_recv()
            for kk, (px, py) in enumerate(chips):
                landed = cut.region(outs[t], chip=2 * px + py, half=c)
                _remote(landed, landed, send_sems, recv_sems, 4 * t + 1 + kk, sibling).wait_recv()
        for cp in copies():
            cp.wait_send()

    return start, wait


def _gather_forward(fulls, cuts):
    n = len(fulls)

    def body(*refs):
        outs = refs[n:2 * n]
        send_sems, recv_sems = refs[2 * n:]
        x, y, c = _coords()
        sibling = (x, y, 1 - c)
        chips = _other_chips(x, y)
        sends = []
        for t, cut in enumerate(cuts):
            for kk, (px, py) in enumerate(chips):
                landed = cut.region(outs[t], chip=2 * px + py, half=c)
                cp = _remote(landed, landed, send_sems, recv_sems, 3 * t + kk, sibling)
                cp.start()
                sends.append(cp)
        for t, cut in enumerate(cuts):
            for kk, (px, py) in enumerate(chips):
                other = cut.region(outs[t], chip=2 * px + py, half=1 - c)
                _remote(other, other, send_sems, recv_sems, 3 * t + kk, sibling).wait_recv()
        for cp in sends:
            cp.wait_send()

    return pl.pallas_call(
        body, name="gather_forward",
        out_shape=tuple(jax.ShapeDtypeStruct(f.shape, f.dtype) for f in fulls),
        in_specs=[ANY] * n, out_specs=tuple([ANY] * n), input_output_aliases={t: t for t in range(n)},
        scratch_shapes=[pltpu.SemaphoreType.DMA((3 * n,)), pltpu.SemaphoreType.DMA((3 * n,))],
    )(*fulls)


def _pair_exchange2(fulls, cuts, name):
    n = len(fulls)

    def body(*refs):
        srcs, outs = refs[:n], refs[n:2 * n]
        send_sems, recv_sems = refs[2 * n:]
        x, y, c = _coords()
        cps = [_remote(cut.region(srcs[t], half=1 - c), outs[t], send_sems, recv_sems, t, (x, y, 1 - c))
               for t, cut in enumerate(cuts)]
        for cp in cps:
            cp.start()
        for cp in cps:
            cp.wait()

    return pl.pallas_call(
        body, name=name,
        out_shape=tuple(jax.ShapeDtypeStruct(cut.shape(half=True), f.dtype) for cut, f in zip(cuts, fulls)),
        in_specs=[ANY] * n, out_specs=tuple([ANY] * n),
        scratch_shapes=[pltpu.SemaphoreType.DMA((n,)), pltpu.SemaphoreType.DMA((n,))],
    )(*fulls)


def _grid_of(shape, blk):
    assert all(s % b == 0 for s, b in zip(shape, blk)), (shape, blk)
    return tuple(s // b for s, b in zip(shape, blk))


def _pair_sum2(full, recv, cut, blk, name):
    hshape = cut.shape(half=True)
    grid = _grid_of(hshape, blk)
    hb = cut.half_size // blk[cut.half_dim]
    hd = cut.half_dim
    pos = jnp.stack([lax.axis_index("c")]).astype(jnp.int32)

    def full_idx(*a):
        ids, p = list(a[:-1]), a[-1]
        ids[hd] = ids[hd] + p[0] * hb
        return tuple(ids)

    def body(p_ref, f_ref, r_ref, o_ref):
        o_ref[...] = (f_ref[...] + r_ref[...]).astype(o_ref.dtype)

    return pl.pallas_call(
        body, name=name, out_shape=jax.ShapeDtypeStruct(hshape, BF16),
        grid_spec=pltpu.PrefetchScalarGridSpec(
            num_scalar_prefetch=1, grid=grid,
            in_specs=[pl.BlockSpec(blk, full_idx), pl.BlockSpec(blk, lambda *a: tuple(a[:-1]))],
            out_specs=pl.BlockSpec(blk, lambda *a: tuple(a[:-1]))),
        compiler_params=_params(("parallel",) * len(grid)))(pos, full, recv)


def _chip_exchange_ops(srcs, outs, send_sems, recv_sems, cuts):
    x, y, c = _coords()
    me = 2 * x + y
    chips = _other_chips(x, y)

    def copies():
        return [_remote(cut.region(srcs[t], chip=2 * px + py), outs[t].at[me], send_sems, recv_sems,
                        3 * t + kk, (px, py, c))
                for t, cut in enumerate(cuts) for kk, (px, py) in enumerate(chips)]

    def start():
        for cp in copies():
            cp.start()

    def wait():
        for t, cut in enumerate(cuts):
            for kk, (px, py) in enumerate(chips):
                slot = outs[t].at[2 * px + py]
                _remote(slot, slot, send_sems, recv_sems, 3 * t + kk, (px, py, c)).wait_recv()
        for cp in copies():
            cp.wait_send()

    return start, wait


def _chip_exchange_shapes(parts, cuts):
    return tuple(jax.ShapeDtypeStruct((N_CHIPS,) + cut.shape(chip=True, half=True), p.dtype)
                 for cut, p in zip(cuts, parts))


def _chip_exchange2(parts, cuts):
    n = len(parts)

    def body(*refs):
        start, wait = _chip_exchange_ops(refs[:n], refs[n:2 * n], refs[2 * n], refs[2 * n + 1], cuts)
        start()
        wait()

    return pl.pallas_call(
        body, name="grad_chip_exchange",
        out_shape=tuple(jax.ShapeDtypeStruct((N_CHIPS,) + cut.shape(chip=True, half=True), p.dtype)
                        for cut, p in zip(cuts, parts)),
        in_specs=[ANY] * n, out_specs=tuple([ANY] * n),
        scratch_shapes=[pltpu.SemaphoreType.DMA((3 * n,)), pltpu.SemaphoreType.DMA((3 * n,))],
    )(*parts)


def _chip_sum2(part, recv, cut, blk, name):
    bshape = cut.shape(chip=True, half=True)
    grid = _grid_of(bshape, blk)
    cb = cut.chip_size // blk[cut.chip_dim]
    hb = cut.half_size // blk[cut.half_dim]
    cd, hd = cut.chip_dim, cut.half_dim
    x, y, c = _coords()
    slots = [2 * px + py for px, py in _other_chips(x, y)]
    pos = jnp.stack([c, 2 * x + y] + slots).astype(jnp.int32)

    def part_idx(*a):
        ids, p = list(a[:-1]), a[-1]
        ids[cd] = ids[cd] + p[1] * cb
        return tuple(ids)

    def recv_idx(kk):
        return lambda *a: (a[-1][2 + kk],) + tuple(a[:-1])

    def out_idx(*a):
        ids, p = list(a[:-1]), a[-1]
        ids[hd] = ids[hd] + p[0] * hb
        return tuple(ids)

    def body(p_ref, own_ref, r0_ref, r1_ref, r2_ref, o_ref):
        acc = own_ref[...].astype(F32)
        for r_ref in (r0_ref, r1_ref, r2_ref):
            acc = acc + r_ref[...].astype(F32)
        o_ref[...] = acc

    return pl.pallas_call(
        body, name=name, out_shape=jax.ShapeDtypeStruct(cut.shape(chip=True), F32),
        grid_spec=pltpu.PrefetchScalarGridSpec(
            num_scalar_prefetch=1, grid=grid,
            in_specs=[pl.BlockSpec(blk, part_idx)] + [pl.BlockSpec((None,) + blk, recv_idx(kk)) for kk in range(3)],
            out_specs=pl.BlockSpec(blk, out_idx)),
        compiler_params=_params(("parallel",) * len(grid)))(pos, part, recv, recv, recv)


def _pair_swap2(blocks, cuts):
    n = len(blocks)

    def body(*refs):
        outs = refs[n:2 * n]
        send_sems, recv_sems = refs[2 * n:]
        x, y, c = _coords()
        cps = []
        for t, cut in enumerate(cuts):
            mine = cut.region(outs[t], half=c)
            cp = _remote(mine, mine, send_sems, recv_sems, t, (x, y, 1 - c))
            cp.start()
            cps.append(cp)
        for t, cut in enumerate(cuts):
            theirs = cut.region(outs[t], half=1 - c)
            _remote(theirs, theirs, send_sems, recv_sems, t, (x, y, 1 - c)).wait_recv()
        for cp in cps:
            cp.wait_send()

    return pl.pallas_call(
        body, name="grad_pair_swap",
        out_shape=tuple(jax.ShapeDtypeStruct(b.shape, b.dtype) for b in blocks),
        in_specs=[ANY] * n, out_specs=tuple([ANY] * n),
        input_output_aliases={t: t for t in range(n)},
        scratch_shapes=[pltpu.SemaphoreType.DMA((n,)), pltpu.SemaphoreType.DMA((n,))],
    )(*blocks)


PAIR_SUM_BLOCKS = {"w_in": (1, 512, IN_SHARD), "w_out": (512, 512), "pool_w": (1, POOL_G, POOL_G),
                   "w_up": (1, 64, 2 * D_FF), "w_down": (1, 704, 512), "small": (N_CHIPS, SMALL_ROWS // 2, LANE)}
CHIP_SUM_BLOCKS = {"w_in": (1, 512, IN_SHARD), "w_out": (256, 512), "pool_w": (2, 64, POOL_G),
                   "w_up": (1, 128, UP_SHARD), "w_down": (1, DOWN_SHARD, 512), "small": (1, SMALL_ROWS // 2, LANE)}


def _pack_small(P):
    parts = []
    for name in SMALL_SHARDED:
        t = P[name]
        parts.append(t.astype(F32))
    return parts


def _small_rows(t, lead):
    flat = t.reshape(lead + (-1,))
    pad = -flat.shape[-1] % LANE
    return jnp.pad(flat, [(0, 0)] * len(lead) + [(0, pad)]).reshape(lead + (-1, LANE))


def _pack_small_shards(shards):
    rows = jnp.concatenate([_small_rows(shards[n].astype(F32), ()) for n in SMALL_SHARDED], axis=0)
    return jnp.pad(rows, ((0, SMALL_ROWS - rows.shape[0]), (0, 0)))[None]


def _unpack_small(pack, shards, axes):
    out, off = {}, 0
    nchip = pack.shape[0]
    for name in SMALL_SHARDED:
        shp = shards[name].shape
        cnt = int(np.prod(shp))
        rows = -(-cnt // LANE)
        t = pack[:, off:off + rows].reshape(nchip, -1)[:, :cnt].reshape((nchip,) + shp)
        out[name] = jnp.concatenate([t[j] for j in range(nchip)], axis=axes[name])
        off += rows
    return out


def _pack_small_grads(grads, shards, axes):
    parts = []
    for name in SMALL_SHARDED:
        shp, ax = shards[name].shape, axes[name]
        g = grads[name].reshape(shp[:ax] + (N_CHIPS, shp[ax]) + shp[ax + 1:])
        parts.append(_small_rows(jnp.moveaxis(g, ax, 0), (N_CHIPS,)))
    rows = jnp.concatenate(parts, axis=1)
    return jnp.pad(rows, ((0, 0), (0, SMALL_ROWS - rows.shape[1]), (0, 0)))


SMALL_AXES = {"meta_tokens": 1, "mix_norm_odd": 1, "pool_b": 2, "pool_scale": 1, "conv_w": 2, "ffn_conv_w": 2}


WEIGHT_NAMES = ("meta_tokens", "mix_norm_even", "w_in", "b_f", "conv_w", "conv_b", "ln_g", "ln_b", "w_out",
                "mix_norm_odd", "pool_w", "pool_b", "pool_scale", "ffn_norm", "w_up", "ffn_conv_w",
                "ffn_conv_b", "w_down", "final_norm")


def kernel(x, meta_tokens, mix_norm_even, w_in, b_f, conv_w, conv_b, ln_g, ln_b, w_out, mix_norm_odd, pool_w, pool_b, pool_scale, ffn_norm, w_up, ffn_conv_w, ffn_conv_b, w_down, final_norm, loss_target, m_meta_tokens, m_mix_norm_even, m_w_in, m_b_f, m_conv_w, m_conv_b, m_ln_g, m_ln_b, m_w_out, m_mix_norm_odd, m_pool_w, m_pool_b, m_pool_scale, m_ffn_norm, m_w_up, m_ffn_conv_w, m_ffn_conv_b, m_w_down, m_final_norm, v_meta_tokens, v_mix_norm_even, v_w_in, v_b_f, v_conv_w, v_conv_b, v_ln_g, v_ln_b, v_w_out, v_mix_norm_odd, v_pool_w, v_pool_b, v_pool_scale, v_ffn_norm, v_w_up, v_ffn_conv_w, v_ffn_conv_b, v_w_down, v_final_norm):
    given = dict(locals())
    w_loc = {n: given[n] for n in WEIGHT_NAMES}
    m_loc = {n: given["m_" + n] for n in WEIGHT_NAMES}
    v_loc = {n: given["v_" + n] for n in WEIGHT_NAMES}
    cut_of = _cuts()
    cuts = [cut_of[n] for n in COMM_ORDER]
    big = ("w_in", "w_out", "pool_w", "w_up", "w_down")
    small_shards = {n: w_loc[n] for n in SMALL_SHARDED}

    shard_of = {n: w_loc[n].astype(BF16).reshape(cut_of[n].shape(chip=True)) for n in big}
    shard_of["small"] = _pack_small_shards(small_shards)
    g_in, g_out, g_small = _gather_weights([shard_of[n] for n in GATHER_FIRST], [cut_of[n] for n in GATHER_FIRST])
    g_pool = g_up = g_down = None
    full = _unpack_small(g_small, small_shards, SMALL_AXES)
    full.update({n: w_loc[n] for n in REPLICATED})
    w_in_full = g_in.transpose(1, 0, 2).reshape(D_MODEL, IN_COLS)
    qkv, f, ag = (w_in_full[:, :3 * FOX_W], w_in_full[:, 3 * FOX_W:3 * FOX_W + HEADS],
                  w_in_full[:, 3 * FOX_W + HEADS:])
    W = dict(
        mix_norm_even=full["mix_norm_even"].reshape(1, D_MODEL),
        w_in_p=jnp.concatenate([qkv, ag, f, jnp.zeros((D_MODEL, LANE - HEADS), BF16)], axis=1),
        b_f_p=jnp.pad(full["b_f"].reshape(1, HEADS), ((0, 0), (0, LANE - HEADS))),
        conv_w_p=jnp.pad(full["conv_w"].reshape(CONV_K, CONV_CH), ((0, CONV_HALO - CONV_K), (0, 0))),
        conv_b=full["conv_b"].reshape(1, CONV_CH), ln_g=full["ln_g"].reshape(1, CONV_CH),
        ln_b=full["ln_b"].reshape(1, CONV_CH), w_out=g_out,
        mix_norm_odd=full["mix_norm_odd"].reshape(1, D_MODEL), pool_w=g_pool,
        pool_b=full["pool_b"].reshape(1, D_MODEL), pool_scale=full["pool_scale"].reshape(1, D_MODEL),
        ffn_norm=full["ffn_norm"], w_up=g_up,
        ffn_conv_w_p=jnp.pad(full["ffn_conv_w"], ((0, 0), (0, 8 - FFN_K), (0, 0))),
        ffn_conv_b=full["ffn_conv_b"], w_down=g_down, final_norm=full["final_norm"].reshape(1, D_MODEL),
        late_shards=[shard_of[n] for n in GATHER_LATE])

    seq = x.shape[1]
    n_real = N_META + seq
    LP = -(-n_real // ATT_BLK) * ATT_BLK
    tail = jnp.zeros((LP - n_real, D_MODEL), F32)
    h0 = jnp.concatenate([full["meta_tokens"], x[0], tail], axis=0)
    tgt = jnp.concatenate([jnp.zeros((N_META, D_MODEL), F32), loss_target[0], tail], axis=0)
    loss_loc, dh0, G, parts, others = _local_step2(h0, tgt, W, n_real, cut_of)
    grad_x = dh0[N_META:n_real][None]
    G["meta_tokens"] = dh0[:N_META]

    rep_shapes = {n: w_loc[n].shape for n in REPLICATED}
    G["final_norm"] = G["final_norm"].reshape(D_MODEL)
    rep, loss = _unpack_replicated(_allreduce_small(_pack_replicated(G, loss_loc)), rep_shapes)

    gp = G["w_in_p"]
    g_w_in = jnp.concatenate([gp[:, :3 * FOX_W], gp[:, 3 * FOX_W + 2 * CONV_CH:3 * FOX_W + 2 * CONV_CH + HEADS],
                              gp[:, 3 * FOX_W:3 * FOX_W + 2 * CONV_CH]], axis=1)
    lcuts = [cut_of[n] for n in LATE]
    lfull = [g_w_in.reshape(D_MODEL, N_CHIPS, IN_SHARD).transpose(1, 0, 2),
             _pack_small_grads(G, small_shards, SMALL_AXES)]
    lrecv = _pair_exchange2(lfull, lcuts, "grad_pair_exchange_late")
    lparts = [_pair_sum2(f, r, cut, PAIR_SUM_BLOCKS[n], "grad_pair_sum_" + n)
              for f, r, cut, n in zip(lfull, lrecv, lcuts, LATE)]
    parts.update(zip(LATE, lparts))
    others.update(zip(LATE, _chip_exchange2(lparts, lcuts)))
    blocks = [_chip_sum2(parts[n], others[n], cut_of[n], CHIP_SUM_BLOCKS[n], "grad_chip_sum_" + n)
              for n in COMM_ORDER]
    blocks = _pair_swap2(blocks, cuts)
    gsh = {n: b.reshape(w_loc[n].shape) for n, b in zip(big, blocks[:5])}
    gsh.update(_unpack_small(blocks[5], small_shards, SMALL_AXES))
    sharded = set(big) | set(SMALL_SHARDED)

    grad_w = {n: (gsh[n] if n in sharded else rep[n]) for n in WEIGHT_NAMES}
    delta, new_m, new_v = {}, {}, {}
    for n in WEIGHT_NAMES:
        delta[n], new_m[n], new_v[n] = _adamw(w_loc[n], grad_w[n], m_loc[n], v_loc[n], "adamw_" + n)
    return (loss, grad_x, *[grad_w[n] for n in WEIGHT_NAMES], *[delta[n] for n in WEIGHT_NAMES],
            *[new_m[n] for n in WEIGHT_NAMES], *[new_v[n] for n in WEIGHT_NAMES])
```

```python
import functools

import numpy as np
import jax
import jax.numpy as jnp
from jax import lax
from jax.experimental import pallas as pl
from jax.experimental.pallas import tpu as pltpu

F32 = jnp.float32
BF16 = jnp.bfloat16

D_MODEL = 1024
N_META = 16
SEQ = 2048
HEADS = 8
HEAD_DIM = 64
FOX_W = HEADS * HEAD_DIM
CONV_CH = 512
CONV_K = 31
D_FF = 2816
POOL_WINDOWS = (2, 4, 8, 16)
POOL_G = 256
RMS_EPS = 1e-6
LN_EPS = 1e-5
IN_COLS = 3 * FOX_W + HEADS + 2 * CONV_CH
IN_COLS_P = 3 * FOX_W + 2 * CONV_CH + 128
F_COL_BLK = (3 * FOX_W + 2 * CONV_CH) // 128
N_CHIPS = 4
IN_SHARD = IN_COLS // N_CHIPS
UP_SHARD = 2 * D_FF // N_CHIPS
DOWN_SHARD = D_FF // N_CHIPS

ADAM_LR = 0.001
ADAM_B1 = 0.9
ADAM_B2 = 0.999
ADAM_EPS = 1e-08
ADAM_WD = 0.01
ADAM_STEP = 10

LANE = 128
ATT_BLK = 128
VMEM_LIMIT = 48 * 1024 * 1024

NEG = -1e30


def _sigmoid(x):
    return 0.5 * jnp.tanh(0.5 * x) + 0.5


def _sigmoid_tail(x):
    return 1.0 / (1.0 + jnp.exp(-x))


def _params(sem=None):
    return pltpu.CompilerParams(dimension_semantics=sem, vmem_limit_bytes=VMEM_LIMIT)


def _sub_rows(tm):
    best = 8
    for s in range(8, 137, 8):
        if tm % s == 0:
            best = s
    return best


def _mm(a, b, mode, out_dtype, tm, tn, name, add=None, a_lead=None, b_lead=None, out=None):
    a_shape = a.shape if a_lead is None else a.shape[1:]
    b_shape = b.shape if b_lead is None else b.shape[1:]
    if mode == "nn":
        (M, K), (K2, N) = a_shape, b_shape
        dims = (((1,), (0,)), ((), ()))
        a_blk, a_idx = (tm, K), (lambda i, j: (i, 0))
        b_blk, b_idx = (K, tn), (lambda i, j: (0, j))
    elif mode == "nt":
        (M, K), (N, K2) = a_shape, b_shape
        dims = (((1,), (1,)), ((), ()))
        a_blk, a_idx = (tm, K), (lambda i, j: (i, 0))
        b_blk, b_idx = (tn, K), (lambda i, j: (j, 0))
    else:
        (K, M), (K2, N) = a_shape, b_shape
        dims = (((0,), (0,)), ((), ()))
        a_blk, a_idx = (K, tm), (lambda i, j: (0, i))
        b_blk, b_idx = (K, tn), (lambda i, j: (0, j))
    assert K == K2 and M % tm == 0 and N % tn == 0, (name, a.shape, b.shape, tm, tn)
    gm, gn = M // tm, N // tn
    a_bytes = M * K * a.dtype.itemsize
    b_bytes = N * K * b.dtype.itemsize
    m_outer = a_bytes + b_bytes * gm <= b_bytes + a_bytes * gn
    if m_outer:
        grid = (gm, gn)
        wrap = lambda f: f
    else:
        grid = (gn, gm)
        wrap = lambda f: (lambda j, i: f(i, j))

    def lead(blk, idx, at):
        if at is None:
            return pl.BlockSpec(blk, wrap(idx))
        return pl.BlockSpec((None,) + blk, wrap(lambda i, j: (at,) + idx(i, j)))

    o_idx = lambda i, j: (i, j)
    in_specs = [lead(a_blk, a_idx, a_lead), lead(b_blk, b_idx, b_lead)]
    args = [a, b]
    if add is not None:
        in_specs.append(pl.BlockSpec((tm, tn), wrap(o_idx)))
        args.append(add)
    aliases = {}
    if out is None:
        out_shape = jax.ShapeDtypeStruct((M, N), out_dtype)
        out_spec = pl.BlockSpec((tm, tn), wrap(o_idx))
    else:
        o_lead, n_lead, into = out
        out_shape = jax.ShapeDtypeStruct((n_lead, M, N), out_dtype)
        out_spec = lead((tm, tn), o_idx, o_lead)
        if into is not None:
            aliases = {len(args): 0}
            in_specs.append(pl.BlockSpec(memory_space=pl.ANY))
            args.append(into)
    has_add = add is not None

    def body(a_ref, b_ref, *rest):
        o_ref = rest[-1]
        x = a_ref[...].astype(BF16)
        y = b_ref[...].astype(BF16)
        acc = lax.dot_general(x, y, dims, preferred_element_type=F32)
        if has_add:
            acc = acc + rest[0][...]
        o_ref[...] = acc.astype(o_ref.dtype)

    return pl.pallas_call(
        body, name=name, out_shape=out_shape, grid=grid, in_specs=in_specs, out_specs=out_spec,
        input_output_aliases=aliases, compiler_params=_params(("parallel", "parallel")))(*args)


def _mm_ffn_dn(dup, w_up, layer, tm, tn, name):
    _, LP, F = dup.shape
    Dm = w_up.shape[1]
    nt = (((1,), (1,)), ((), ()))

    def body(a_ref, b_ref, o_ref):
        acc = lax.dot_general(a_ref[0], b_ref[:, 0:F], nt, preferred_element_type=F32)
        acc = acc + lax.dot_general(a_ref[1], b_ref[:, F:2 * F], nt, preferred_element_type=F32)
        o_ref[...] = acc

    return pl.pallas_call(
        body, name=name, out_shape=jax.ShapeDtypeStruct((LP, Dm), F32), grid=(LP // tm, Dm // tn),
        in_specs=[pl.BlockSpec((2, tm, F), lambda i, j: (0, i, 0)),
                  pl.BlockSpec((None, tn, 2 * F), lambda i, j: (layer, j, 0))],
        out_specs=pl.BlockSpec((tm, tn), lambda i, j: (i, j)),
        compiler_params=_params(("parallel", "parallel")))(dup, w_up)


def _mm_ffn_dwup(n, dup, layer, into, tk, tn, name):
    LP, Dm = n.shape
    F = dup.shape[2]
    nct = F // tn
    tdims = (((0,), (0,)), ((), ()))

    def body(a_ref, b_ref, *rest):
        rest[-1][...] = lax.dot_general(a_ref[...], b_ref[...], tdims, preferred_element_type=F32)

    in_specs = [pl.BlockSpec((LP, tk), lambda i, j: (0, i)),
                pl.BlockSpec((None, LP, tn), lambda i, j: (j // nct, 0, j % nct))]
    args = [n, dup]
    aliases = {}
    if into is not None:
        in_specs.append(pl.BlockSpec(memory_space=pl.ANY))
        args.append(into)
        aliases = {2: 0}
    return pl.pallas_call(
        body, name=name, out_shape=jax.ShapeDtypeStruct((2, Dm, 2 * F), F32), grid=(Dm // tk, 2 * nct),
        in_specs=in_specs, out_specs=pl.BlockSpec((None, tk, tn), lambda i, j: (layer, i, j)),
        input_output_aliases=aliases, compiler_params=_params(("parallel", "parallel")))(*args)


def _rms_fwd(h, g, out_dtype, tm, name):
    LP, Dm = h.shape

    def body(h_ref, g_ref, o_ref):
        x = h_ref[...]
        r = lax.rsqrt(jnp.mean(x * x, axis=1, keepdims=True) + RMS_EPS)
        o_ref[...] = (x * r * g_ref[...]).astype(o_ref.dtype)

    return pl.pallas_call(
        body, name=name, out_shape=jax.ShapeDtypeStruct((LP, Dm), out_dtype), grid=(LP // tm,),
        in_specs=[pl.BlockSpec((tm, Dm), lambda i: (i, 0)), pl.BlockSpec((1, Dm), lambda i: (0, 0))],
        out_specs=pl.BlockSpec((tm, Dm), lambda i: (i, 0)),
        compiler_params=_params(("parallel",)))(h, g)


def _rms_bwd(h, g, dn, dres, tm, name):
    LP, Dm = h.shape

    def body(h_ref, g_ref, dn_ref, dr_ref, dh_ref, dg_ref):
        i = pl.program_id(0)
        x = h_ref[...]
        r = lax.rsqrt(jnp.mean(x * x, axis=1, keepdims=True) + RMS_EPS)
        xhat = x * r
        dy = dn_ref[...]
        dxh = dy * g_ref[...]
        dh = r * (dxh - xhat * jnp.mean(dxh * xhat, axis=1, keepdims=True))
        dh_ref[...] = dr_ref[...] + dh

        @pl.when(i == 0)
        def _():
            dg_ref[...] = jnp.zeros_like(dg_ref)

        dg_ref[...] += jnp.sum(dy * xhat, axis=0, keepdims=True)

    row = pl.BlockSpec((tm, Dm), lambda i: (i, 0))
    vec = pl.BlockSpec((1, Dm), lambda i: (0, 0))
    return pl.pallas_call(
        body, name=name,
        out_shape=(jax.ShapeDtypeStruct((LP, Dm), F32), jax.ShapeDtypeStruct((1, Dm), F32)),
        grid=(LP // tm,), in_specs=[row, vec, row, row], out_specs=(row, vec),
        compiler_params=_params(("arbitrary",)))(h, g, dn, dres)


def _loss_head(h, g, tgt, n_real, tm, name):
    LP, Dm = h.shape

    def body(h_ref, g_ref, t_ref, loss_ref, dh_ref, dg_ref):
        i = pl.program_id(0)
        x = h_ref[...]
        gg = g_ref[...]
        r = lax.rsqrt(jnp.mean(x * x, axis=1, keepdims=True) + RMS_EPS)
        xhat = x * r
        rows = i * tm + lax.broadcasted_iota(jnp.int32, (tm, 1), 0)
        real = jnp.logical_and(rows >= N_META, rows < n_real)
        diff = jnp.where(real, xhat * gg - t_ref[...], 0.0)
        dy = diff * (1.0 / Dm)
        dxh = dy * gg
        dh_ref[...] = r * (dxh - xhat * jnp.mean(dxh * xhat, axis=1, keepdims=True))

        @pl.when(i == 0)
        def _():
            dg_ref[...] = jnp.zeros_like(dg_ref)
            loss_ref[...] = jnp.zeros_like(loss_ref)

        dg_ref[...] += jnp.sum(dy * xhat, axis=0, keepdims=True)
        part = jnp.sum(jnp.sum(diff * diff, axis=1, keepdims=True), axis=0, keepdims=True)
        loss_ref[...] += jnp.broadcast_to(part * (0.5 / Dm), loss_ref.shape)

    row = pl.BlockSpec((tm, Dm), lambda i: (i, 0))
    vec = pl.BlockSpec((1, Dm), lambda i: (0, 0))
    return pl.pallas_call(
        body, name=name,
        out_shape=(jax.ShapeDtypeStruct((1, LANE), F32), jax.ShapeDtypeStruct((LP, Dm), F32),
                   jax.ShapeDtypeStruct((1, Dm), F32)),
        grid=(LP // tm,), in_specs=[row, vec, row],
        out_specs=(pl.BlockSpec((1, LANE), lambda i: (0, 0)), row, vec),
        compiler_params=_params(("arbitrary",)))(h, g, tgt)


def _fgate_fwd(proj, bf_p, name):
    LP = proj.shape[0]
    nb = LP // LANE

    def body(f_ref, b_ref, c_ref, lf_ref):
        x = f_ref[...] + b_ref[...]
        lf_ref[...] = jnp.minimum(x, 0.0) - jnp.log1p(jnp.exp(-jnp.abs(x)))
        ri = lax.broadcasted_iota(jnp.int32, (LANE, LANE), 0)
        ci = lax.broadcasted_iota(jnp.int32, (LANE, LANE), 1)
        tri = jnp.where(ri >= ci, 1.0, 0.0).astype(F32)

        def blk(i, carry):
            rows = pl.ds(pl.multiple_of(i * LANE, LANE), LANE)
            cb = jnp.dot(tri, lf_ref[rows, :], precision=lax.Precision.HIGHEST,
                         preferred_element_type=F32) + carry
            c_ref[rows, :] = cb
            return cb[LANE - 1:LANE, :]

        lax.fori_loop(0, nb, blk, jnp.zeros((1, LANE), F32))

    return pl.pallas_call(
        body, name=name, out_shape=jax.ShapeDtypeStruct((LP, LANE), F32), grid=(1,),
        in_specs=[pl.BlockSpec((LP, LANE), lambda i: (0, F_COL_BLK)),
                  pl.BlockSpec((1, LANE), lambda i: (0, 0))],
        out_specs=pl.BlockSpec((LP, LANE), lambda i: (0, 0)),
        scratch_shapes=[pltpu.VMEM((LP, LANE), F32)],
        compiler_params=_params(("arbitrary",)))(proj, bf_p)


def _fgate_bwd(proj, bf_p, dc, name):
    LP = proj.shape[0]
    nb = LP // LANE

    def body(f_ref, b_ref, dc_ref, dl_ref, db_ref):
        ri = lax.broadcasted_iota(jnp.int32, (LANE, LANE), 0)
        ci = lax.broadcasted_iota(jnp.int32, (LANE, LANE), 1)
        triu = jnp.where(ri <= ci, 1.0, 0.0).astype(F32)
        bb = b_ref[...]

        tail = jnp.zeros((1, LANE), F32)
        dbs = jnp.zeros((1, LANE), F32)
        for i in range(nb - 1, -1, -1):
            rows = slice(i * LANE, (i + 1) * LANE)
            gb = jnp.dot(triu, dc_ref[rows, :], precision=lax.Precision.HIGHEST,
                         preferred_element_type=F32) + tail
            x = f_ref[rows, :] + bb
            dl = gb * _sigmoid_tail(-x)
            dl_ref[rows, :] = dl.astype(dl_ref.dtype)
            tail = gb[0:1, :]
            dbs = dbs + jnp.sum(dl, axis=0, keepdims=True)
        db_ref[...] = dbs

    return pl.pallas_call(
        body, name=name,
        out_shape=(jax.ShapeDtypeStruct((LP, LANE), BF16), jax.ShapeDtypeStruct((1, LANE), F32)),
        grid=(1,),
        in_specs=[pl.BlockSpec((LP, LANE), lambda i: (0, F_COL_BLK)),
                  pl.BlockSpec((1, LANE), lambda i: (0, 0)),
                  pl.BlockSpec((LP, LANE), lambda i: (0, 0))],
        out_specs=(pl.BlockSpec((LP, LANE), lambda i: (0, 0)), pl.BlockSpec((1, LANE), lambda i: (0, 0))),
        compiler_params=_params(("arbitrary",)))(proj, bf_p, dc)


def _attn_fwd(q, k, v, c_col, c_row, name):
    Hh, LP, Dh = q.shape
    nb = LP // ATT_BLK
    scale = Dh ** -0.5
    nt = (((1,), (1,)), ((), ()))

    def body(q_ref, k_ref, v_ref, cc_ref, cr_ref, o_ref, lse_ref):
        i = pl.program_id(1)
        qb = q_ref[...]
        cq = cc_ref[...]
        rows = i * ATT_BLK + lax.broadcasted_iota(jnp.int32, (ATT_BLK, ATT_BLK), 0)
        cols0 = lax.broadcasted_iota(jnp.int32, (ATT_BLK, ATT_BLK), 1)

        def step(j, carry):
            m, l, acc = carry
            ks = pl.ds(pl.multiple_of(j * ATT_BLK, ATT_BLK), ATT_BLK)
            s = lax.dot_general(qb, k_ref[ks, :], nt, preferred_element_type=F32) * scale
            s = s + cq - cr_ref[j]
            s = jnp.where(cols0 + j * ATT_BLK <= rows, s, NEG)
            m_new = jnp.maximum(m, jnp.max(s, axis=1, keepdims=True))
            p = jnp.exp(s - m_new)
            alpha = jnp.exp(m - m_new)
            l = alpha * l + jnp.sum(p, axis=1, keepdims=True)
            acc = alpha * acc + jnp.dot(p.astype(BF16), v_ref[ks, :], preferred_element_type=F32)
            return m_new, l, acc

        init = (jnp.full((ATT_BLK, 1), NEG, F32), jnp.zeros((ATT_BLK, 1), F32),
                jnp.zeros((ATT_BLK, Dh), F32))
        m, l, acc = lax.fori_loop(0, i + 1, step, init)
        o_ref[...] = acc / l
        lse_ref[...] = m + jnp.log(l)

    qspec = pl.BlockSpec((None, ATT_BLK, Dh), lambda h, i: (h, i, 0))
    kspec = pl.BlockSpec((None, LP, Dh), lambda h, i: (h, 0, 0))
    colspec = pl.BlockSpec((None, ATT_BLK, 1), lambda h, i: (h, i, 0))
    rowspec = pl.BlockSpec((None, nb, 1, ATT_BLK), lambda h, i: (h, 0, 0, 0))
    return pl.pallas_call(
        body, name=name,
        out_shape=(jax.ShapeDtypeStruct((Hh, LP, Dh), F32), jax.ShapeDtypeStruct((Hh, LP, 1), F32)),
        grid=(Hh, nb), in_specs=[qspec, kspec, kspec, colspec, rowspec],
        out_specs=(qspec, colspec),
        compiler_params=_params(("parallel", "arbitrary")))(q, k, v, c_col, c_row)


def _attn_bwd(q, k, v, o, do, lse, c_col, c_row, name):
    Hh, LP, Dh = q.shape
    nb = LP // ATT_BLK
    scale = Dh ** -0.5
    nt = (((1,), (1,)), ((), ()))
    tn = (((0,), (0,)), ((), ()))

    def body(q_ref, k_ref, v_ref, o_ref, do_ref, lse_ref, cc_ref, cr_ref,
             dq_ref, dk_ref, dv_ref, dcq_ref, dc_ref, delta_ref):
        j = pl.program_id(1)

        @pl.when(j == 0)
        def _():
            dq_ref[...] = jnp.zeros_like(dq_ref)
            dcq_ref[...] = jnp.zeros_like(dcq_ref)
            dob_all = do_ref[...].astype(BF16).astype(F32)
            delta_ref[...] = jnp.sum(dob_all * o_ref[...], axis=1, keepdims=True)

        kb = k_ref[...]
        vb = v_ref[...]
        ck = cr_ref[j]
        rows0 = lax.broadcasted_iota(jnp.int32, (ATT_BLK, ATT_BLK), 0)
        cols = j * ATT_BLK + lax.broadcasted_iota(jnp.int32, (ATT_BLK, ATT_BLK), 1)

        def step(i, carry):
            dk, dv, dcs = carry
            qs = pl.ds(pl.multiple_of(i * ATT_BLK, ATT_BLK), ATT_BLK)
            qb = q_ref[qs, :]
            dob = do_ref[qs, :].astype(BF16)
            s = lax.dot_general(qb, kb, nt, preferred_element_type=F32) * scale
            s = s + cc_ref[qs, :] - ck
            s = jnp.where(cols <= rows0 + i * ATT_BLK, s, NEG)
            p = jnp.exp(s - lse_ref[qs, :])
            dp = lax.dot_general(dob, vb, nt, preferred_element_type=F32)
            ds = p * (dp - delta_ref[qs, :])
            dsb = ds.astype(BF16)
            dv = dv + lax.dot_general(p.astype(BF16), dob, tn, preferred_element_type=F32)
            dk = dk + lax.dot_general(dsb, qb, tn, preferred_element_type=F32) * scale
            dq_ref[qs, :] += jnp.dot(dsb, kb, preferred_element_type=F32) * scale
            dcq_ref[qs, :] += jnp.sum(ds, axis=1, keepdims=True)
            dcs = dcs - jnp.sum(ds, axis=0, keepdims=True)
            return dk, dv, dcs

        init = (jnp.zeros((ATT_BLK, Dh), F32), jnp.zeros((ATT_BLK, Dh), F32),
                jnp.zeros((1, ATT_BLK), F32))
        dk, dv, dcs = lax.fori_loop(j, nb, step, init)
        dk_ref[...] = dk
        dv_ref[...] = dv
        dc_ref[...] = dcs

    full = pl.BlockSpec((None, LP, Dh), lambda h, j: (h, 0, 0))
    blk = pl.BlockSpec((None, ATT_BLK, Dh), lambda h, j: (h, j, 0))
    col = pl.BlockSpec((None, LP, 1), lambda h, j: (h, 0, 0))
    rowspec = pl.BlockSpec((None, nb, 1, ATT_BLK), lambda h, j: (h, 0, 0, 0))
    return pl.pallas_call(
        body, name=name,
        out_shape=(jax.ShapeDtypeStruct((Hh, LP, Dh), F32), jax.ShapeDtypeStruct((Hh, LP, Dh), F32),
                   jax.ShapeDtypeStruct((Hh, LP, Dh), F32), jax.ShapeDtypeStruct((Hh, LP, 1), F32),
                   jax.ShapeDtypeStruct((Hh, nb, 1, ATT_BLK), F32)),
        grid=(Hh, nb), in_specs=[full, blk, blk, full, full, col, col, rowspec],
        out_specs=(full, blk, blk, col, pl.BlockSpec((None, None, 1, ATT_BLK), lambda h, j: (h, j, 0, 0))),
        scratch_shapes=[pltpu.VMEM((LP, 1), F32)],
        compiler_params=_params(("parallel", "arbitrary")))(q, k, v, o, do, lse, c_col, c_row)


AUG = 128
ONES_IN_K = HEAD_DIM
ONES_IN_Q = HEAD_DIM + 3
ATT_HEADS_PER_STEP = 8
ATT_HEADS_PER_STEP_BWD = 4


def _attn_prep(proj, c, name):
    LP = proj.shape[0]
    nb = LP // ATT_BLK
    tail_rows = AUG - HEAD_DIM

    def body(q_ref, k_ref, v_ref, c_ref, qT_ref, kT_ref, ka_ref, vT_ref):
        qt = (q_ref[...] * (HEAD_DIM ** -0.5)).T
        kt = k_ref[...].T
        vt = v_ref[...].T
        ct = c_ref[...].T
        hi = ct.astype(BF16).astype(F32)
        r1 = ct - hi
        mid = r1.astype(BF16).astype(F32)
        lo = (r1 - mid).astype(BF16).astype(F32)
        row = lax.broadcasted_iota(jnp.int32, (tail_rows, ATT_BLK), 0)
        ones = jnp.where(row < 3, 1.0, 0.0)
        for h in range(HEADS):
            cparts = jnp.where(row == 0, hi[h:h + 1], jnp.where(row == 1, mid[h:h + 1],
                               jnp.where(row == 2, lo[h:h + 1], 0.0)))
            hs = slice(h * HEAD_DIM, (h + 1) * HEAD_DIM)
            q_tail = cparts + pltpu.roll(ones, 3, 0)
            k_tail = ones - pltpu.roll(cparts, 3, 0)
            qT_ref[h] = jnp.concatenate([qt[hs], q_tail], axis=0).astype(BF16)
            kfull = jnp.concatenate([kt[hs], k_tail], axis=0)
            kT_ref[h] = kfull.astype(BF16)
            ka_ref[h] = kfull.T.astype(BF16)
            vT_ref[h] = vt[hs].astype(BF16)

    col = lambda j: pl.BlockSpec((ATT_BLK, FOX_W), lambda i: (i, j))
    blk = lambda r: pl.BlockSpec((HEADS, None, r, ATT_BLK), lambda i: (0, i, 0, 0))
    return pl.pallas_call(
        body, name=name,
        out_shape=(jax.ShapeDtypeStruct((HEADS, nb, AUG, ATT_BLK), BF16),
                   jax.ShapeDtypeStruct((HEADS, nb, AUG, ATT_BLK), BF16),
                   jax.ShapeDtypeStruct((HEADS, LP, AUG), BF16),
                   jax.ShapeDtypeStruct((HEADS, nb, HEAD_DIM, ATT_BLK), BF16)),
        grid=(nb,), in_specs=[col(0), col(1), col(2), pl.BlockSpec((ATT_BLK, LANE), lambda i: (i, 0))],
        out_specs=(blk(AUG), blk(AUG), pl.BlockSpec((HEADS, ATT_BLK, AUG), lambda i: (0, i, 0)), blk(HEAD_DIM)),
        compiler_params=_params(("parallel",)))(proj, proj, proj, c)


def _attn_rows(xT, scale, out_dtype, name):
    Hh, nb, R, _ = xT.shape

    def body(x_ref, o_ref):
        stack = jnp.concatenate([x_ref[h, 0:HEAD_DIM, :] for h in range(Hh)], axis=0)
        o_ref[...] = (stack * scale).T.astype(o_ref.dtype)

    return pl.pallas_call(
        body, name=name, out_shape=jax.ShapeDtypeStruct((nb * ATT_BLK, Hh * HEAD_DIM), out_dtype), grid=(nb,),
        in_specs=[pl.BlockSpec((Hh, None, R, ATT_BLK), lambda i: (0, i, 0, 0))],
        out_specs=pl.BlockSpec((ATT_BLK, Hh * HEAD_DIM), lambda i: (i, 0)),
        compiler_params=_params(("parallel",)))(xT)


def _attn_cols(x, name):
    LP = x.shape[0]
    nb = LP // ATT_BLK

    def body(x_ref, o_ref):
        xt = x_ref[...].T
        for h in range(HEADS):
            o_ref[h] = xt[h * HEAD_DIM:(h + 1) * HEAD_DIM].astype(o_ref.dtype)

    return pl.pallas_call(
        body, name=name, out_shape=jax.ShapeDtypeStruct((HEADS, nb, HEAD_DIM, ATT_BLK), BF16), grid=(nb,),
        in_specs=[pl.BlockSpec((ATT_BLK, FOX_W), lambda i: (i, 0))],
        out_specs=pl.BlockSpec((HEADS, None, HEAD_DIM, ATT_BLK), lambda i: (0, i, 0, 0)),
        compiler_params=_params(("parallel",)))(x)


def _attn_dc(dqT, dkT, name):
    Hh, nb, _, _ = dqT.shape

    def body(q_ref, k_ref, o_ref):
        row = lax.broadcasted_iota(jnp.int32, (LANE, ATT_BLK), 0)
        acc = jnp.zeros((LANE, ATT_BLK), F32)
        for h in range(Hh):
            d = q_ref[h, ONES_IN_K:ONES_IN_K + 1, :] - k_ref[h, ONES_IN_Q:ONES_IN_Q + 1, :]
            acc = jnp.where(row == h, d, acc)
        o_ref[...] = acc.T

    spec = pl.BlockSpec((Hh, None, AUG, ATT_BLK), lambda i: (0, i, 0, 0))
    return pl.pallas_call(
        body, name=name, out_shape=jax.ShapeDtypeStruct((nb * ATT_BLK, LANE), F32), grid=(nb,),
        in_specs=[spec, spec], out_specs=pl.BlockSpec((ATT_BLK, LANE), lambda i: (i, 0)),
        compiler_params=_params(("parallel",)))(dqT, dkT)


def _attn_fwd2(qT, k_aug, vT, name, comm=None):
    Hh, nb, _, _ = qT.shape
    LP = nb * ATT_BLK
    Dh = vT.shape[2]
    HB = ATT_HEADS_PER_STEP
    n_comm = 0 if comm is None else len(comm[0])

    def body(*refs):
        q_ref, k_ref, v_ref = refs[:3]
        o_ref, lse_ref = refs[3 + n_comm:5 + n_comm]
        if n_comm:
            start, wait = _gather_first_ops(refs[3:3 + n_comm], refs[5 + n_comm:5 + 2 * n_comm],
                                            refs[5 + 2 * n_comm], refs[6 + 2 * n_comm], comm[1])
            pl.when(pl.program_id(0) == 0)(start)
        keys = lax.broadcasted_iota(jnp.int32, (ATT_BLK, ATT_BLK), 0)
        qrys = lax.broadcasted_iota(jnp.int32, (ATT_BLK, ATT_BLK), 1)
        causal = keys <= qrys

        def q_block(i, _):
            def tile(j, carry, masked):
                ks = pl.ds(pl.multiple_of(j * ATT_BLK, ATT_BLK), ATT_BLK)
                out = []
                for hh in range(HB):
                    m, l, acc = carry[hh]
                    s = jnp.dot(k_ref[hh, ks, :], q_ref[hh, i], preferred_element_type=F32)
                    if masked:
                        s = jnp.where(causal, s, NEG)
                    m_new = jnp.maximum(m, jnp.max(s, axis=0, keepdims=True))
                    p = jnp.exp(s - m_new)
                    alpha = jnp.exp(m - m_new)
                    l = alpha * l + jnp.sum(p, axis=0, keepdims=True)
                    acc = alpha * acc + jnp.dot(v_ref[hh, j], p.astype(BF16), preferred_element_type=F32)
                    out.append((m_new, l, acc))
                return tuple(out)

            init = tuple((jnp.full((1, ATT_BLK), NEG, F32), jnp.zeros((1, ATT_BLK), F32),
                          jnp.zeros((Dh, ATT_BLK), F32)) for _ in range(HB))
            carry = lax.fori_loop(0, i, lambda j, cr: tile(j, cr, False), init)
            carry = tile(i, carry, True)
            for hh in range(HB):
                m, l, acc = carry[hh]
                o_ref[hh, i] = acc / l
                lse_ref[hh, i] = m + jnp.log(l)
            return 0

        lax.fori_loop(0, nb, q_block, 0)
        if n_comm:
            pl.when(pl.program_id(0) == Hh // HB - 1)(wait)

    blk = lambda r: pl.BlockSpec((HB, nb, r, ATT_BLK), lambda h: (h, 0, 0, 0))
    out_shape = (jax.ShapeDtypeStruct((Hh, nb, Dh, ATT_BLK), F32), jax.ShapeDtypeStruct((Hh, nb, 1, ATT_BLK), F32))
    scratch = []
    args = [qT, k_aug, vT]
    if n_comm:
        out_shape += tuple(jax.ShapeDtypeStruct(cut.full, s.dtype) for cut, s in zip(comm[1], comm[0]))
        scratch = [pltpu.SemaphoreType.DMA((4 * n_comm,)), pltpu.SemaphoreType.DMA((4 * n_comm,))]
        args += list(comm[0])
    return pl.pallas_call(
        body, name=name, out_shape=out_shape, grid=(Hh // HB,),
        in_specs=[blk(AUG), pl.BlockSpec((HB, LP, AUG), lambda h: (h, 0, 0)), blk(Dh)] + [ANY] * n_comm,
        out_specs=(blk(Dh), blk(1)) + (ANY,) * n_comm, scratch_shapes=scratch,
        compiler_params=_params(("arbitrary",)))(*args)


def _attn_bwd2(qT, kT, k_aug, v, oT, doT, lse, name, comm=None):
    Hh, nb, _, _ = qT.shape
    LP = nb * ATT_BLK
    Dh = v.shape[2]
    nt = (((1,), (1,)), ((), ()))
    tn = (((0,), (0,)), ((), ()))

    HB = ATT_HEADS_PER_STEP_BWD
    n_comm = 0 if comm is None else len(comm[0])

    def body(*refs):
        q_ref, kt_ref, k_ref, v_ref, o_ref, do_ref, lse_ref = refs[:7]
        parts = refs[7:7 + n_comm]
        dq_ref, dk_ref, dv_ref = refs[7 + n_comm:10 + n_comm]
        others = refs[10 + n_comm:10 + 2 * n_comm]
        delta_ref = refs[10 + 2 * n_comm]
        if n_comm:
            start, wait = _chip_exchange_ops(parts, others, refs[11 + 2 * n_comm], refs[12 + 2 * n_comm], comm[1])
            pl.when(pl.program_id(0) == 0)(start)
        keys = lax.broadcasted_iota(jnp.int32, (ATT_BLK, ATT_BLK), 0)
        qrys = lax.broadcasted_iota(jnp.int32, (ATT_BLK, ATT_BLK), 1)
        causal = keys <= qrys

        def prep(i, _):
            for hh in range(HB):
                delta_ref[hh, i] = jnp.sum(do_ref[hh, i].astype(F32) * o_ref[hh, i], axis=0, keepdims=True)
                dq_ref[hh, i] = jnp.zeros((AUG, ATT_BLK), F32)
            return 0

        lax.fori_loop(0, nb, prep, 0)

        def kv_block(j, _):
            ks = pl.ds(pl.multiple_of(j * ATT_BLK, ATT_BLK), ATT_BLK)

            def tile(i, carry, masked):
                out = []
                for hh in range(HB):
                    dk, dv = carry[hh]
                    qb = q_ref[hh, i]
                    dob = do_ref[hh, i]
                    s = jnp.dot(k_ref[hh, ks, :], qb, preferred_element_type=F32)
                    if masked:
                        s = jnp.where(causal, s, NEG)
                    p = jnp.exp(s - lse_ref[hh, i])
                    dp = lax.dot_general(v_ref[hh, j], dob, tn, preferred_element_type=F32)
                    ds = (p * (dp - delta_ref[hh, i])).astype(BF16)
                    dv = dv + lax.dot_general(dob, p.astype(BF16), nt, preferred_element_type=F32)
                    dk = dk + lax.dot_general(qb, ds, nt, preferred_element_type=F32)
                    dq_ref[hh, i] += jnp.dot(kt_ref[hh, j], ds, preferred_element_type=F32)
                    out.append((dk, dv))
                return tuple(out)

            init = tuple((jnp.zeros((AUG, ATT_BLK), F32), jnp.zeros((Dh, ATT_BLK), F32)) for _ in range(HB))
            carry = tile(j, init, True)
            carry = lax.fori_loop(j + 1, nb, lambda i, cr: tile(i, cr, False), carry)
            for hh in range(HB):
                dk_ref[hh, j] = carry[hh][0]
                dv_ref[hh, j] = carry[hh][1]
            return 0

        lax.fori_loop(0, nb, kv_block, 0)
        if n_comm:
            pl.when(pl.program_id(0) == Hh // HB - 1)(wait)

    blk = lambda r: pl.BlockSpec((HB, nb, r, ATT_BLK), lambda h: (h, 0, 0, 0))
    row = lambda cols: pl.BlockSpec((HB, LP, cols), lambda h: (h, 0, 0))
    out_shape = (jax.ShapeDtypeStruct((Hh, nb, AUG, ATT_BLK), F32), jax.ShapeDtypeStruct((Hh, nb, AUG, ATT_BLK), F32),
                 jax.ShapeDtypeStruct((Hh, nb, Dh, ATT_BLK), F32))
    scratch = [pltpu.VMEM((HB, nb, 1, ATT_BLK), F32)]
    args = [qT, kT, k_aug, v, oT, doT, lse]
    if n_comm:
        out_shape += _chip_exchange_shapes(*comm)
        scratch += [pltpu.SemaphoreType.DMA((3 * n_comm,)), pltpu.SemaphoreType.DMA((3 * n_comm,))]
        args += list(comm[0])
    return pl.pallas_call(
        body, name=name, out_shape=out_shape, grid=(Hh // HB,),
        in_specs=[blk(AUG), blk(AUG), row(AUG), blk(Dh), blk(Dh), blk(Dh), blk(1)] + [ANY] * n_comm,
        out_specs=(blk(AUG), blk(AUG), blk(Dh)) + (ANY,) * n_comm,
        scratch_shapes=scratch,
        compiler_params=_params(("arbitrary",)))(*args)


CONV_HALO = 32
A_BLK = 3 * FOX_W // CONV_CH
G_BLK = A_BLK + 1


def _conf_fwd(proj, cw, cb, lg, lb, tm, name):
    LP = proj.shape[0]
    C = CONV_CH
    sub = _sub_rows(tm)
    hpb = tm // CONV_HALO

    def body(a_ref, g_ref, ah_ref, gh_ref, w_ref, cb_ref, lg_ref, lb_ref, u1_ref, u_ref, buf):
        r = pl.program_id(0)
        buf[CONV_HALO:CONV_HALO + tm, :] = a_ref[...] * _sigmoid(g_ref[...])
        buf[0:CONV_HALO, :] = jnp.where(r > 0, ah_ref[...] * _sigmoid(gh_ref[...]), 0.0)
        for s in range(tm // sub):
            for ct in range(C // LANE):
                ln = slice(ct * LANE, (ct + 1) * LANE)
                acc = jnp.broadcast_to(cb_ref[:, ln], (sub, LANE))
                for kk in range(CONV_K):
                    off = CONV_HALO + s * sub - (CONV_K - 1) + kk
                    acc = acc + w_ref[kk:kk + 1, ln] * buf[off:off + sub, ln]
                u1_ref[s * sub:(s + 1) * sub, ln] = acc
        u1 = u1_ref[...]
        mu = jnp.mean(u1, axis=1, keepdims=True)
        xc = u1 - mu
        var = jnp.mean(xc * xc, axis=1, keepdims=True)
        y = xc * lax.rsqrt(var + LN_EPS) * lg_ref[...] + lb_ref[...]
        u_ref[...] = (y * _sigmoid(y)).astype(u_ref.dtype)

    cur = lambda blk: pl.BlockSpec((tm, C), lambda r: (r, blk))
    halo = lambda blk: pl.BlockSpec((CONV_HALO, C), lambda r: (jnp.maximum(r * hpb - 1, 0), blk))
    vec = pl.BlockSpec((1, C), lambda r: (0, 0))
    out = pl.BlockSpec((tm, C), lambda r: (r, 0))
    return pl.pallas_call(
        body, name=name,
        out_shape=(jax.ShapeDtypeStruct((LP, C), F32), jax.ShapeDtypeStruct((LP, C), BF16)),
        grid=(LP // tm,),
        in_specs=[cur(A_BLK), cur(G_BLK), halo(A_BLK), halo(G_BLK),
                  pl.BlockSpec((CONV_HALO, C), lambda r: (0, 0)), vec, vec, vec],
        out_specs=(out, out),
        scratch_shapes=[pltpu.VMEM((CONV_HALO + tm, C), F32)],
        compiler_params=_params(("parallel",)))(proj, proj, proj, proj, cw, cb, lg, lb)


def _conf_bwd(proj, u1, dcat, cw, lg, lb, tm, name):
    LP = proj.shape[0]
    C = CONV_CH
    sub = _sub_rows(tm)
    hpb = tm // CONV_HALO
    nblk = LP // tm
    last_halo = LP // CONV_HALO - 1

    def body(a_ref, g_ref, ah_ref, gh_ref, u1_ref, u1n_ref, du_ref, dun_ref, w_ref, lg_ref, lb_ref,
             dadg_ref, dw_ref, dcb_ref, dlg_ref, dlb_ref, ubuf, dbuf, du0):
        r = pl.program_id(0)
        lgv = lg_ref[...]
        lbv = lb_ref[...]

        def ln_silu_bwd(u1v, duv):
            mu = jnp.mean(u1v, axis=1, keepdims=True)
            xc = u1v - mu
            rstd = lax.rsqrt(jnp.mean(xc * xc, axis=1, keepdims=True) + LN_EPS)
            xhat = xc * rstd
            y = xhat * lgv + lbv
            sg = _sigmoid(y)
            dy = duv * (sg * (1.0 + y * (1.0 - sg)))
            dxh = dy * lgv
            du1 = rstd * (dxh - jnp.mean(dxh, axis=1, keepdims=True)
                          - xhat * jnp.mean(dxh * xhat, axis=1, keepdims=True))
            return du1, dy, xhat

        @pl.when(r == 0)
        def _():
            dw_ref[...] = jnp.zeros_like(dw_ref)
            dcb_ref[...] = jnp.zeros_like(dcb_ref)
            dlg_ref[...] = jnp.zeros_like(dlg_ref)
            dlb_ref[...] = jnp.zeros_like(dlb_ref)

        du1, dy, xhat = ln_silu_bwd(u1_ref[...], du_ref[...])
        dlg_ref[...] += jnp.sum(dy * xhat, axis=0, keepdims=True)
        dlb_ref[...] += jnp.sum(dy, axis=0, keepdims=True)
        dcb_ref[...] += jnp.sum(du1, axis=0, keepdims=True)
        dbuf[0:tm, :] = du1
        du1n, _, _ = ln_silu_bwd(u1n_ref[...], dun_ref[...])
        dbuf[tm:tm + CONV_HALO, :] = jnp.where(r < nblk - 1, du1n, 0.0)
        ubuf[CONV_HALO:CONV_HALO + tm, :] = a_ref[...] * _sigmoid(g_ref[...])
        ubuf[0:CONV_HALO, :] = jnp.where(r > 0, ah_ref[...] * _sigmoid(gh_ref[...]), 0.0)

        for ct in range(C // LANE):
            ln = slice(ct * LANE, (ct + 1) * LANE)
            for s in range(tm // sub):
                d_here = dbuf[s * sub:(s + 1) * sub, ln]
                acc = jnp.zeros((sub, LANE), F32)
                for kk in range(CONV_K):
                    fo = s * sub + (CONV_K - 1) - kk
                    acc = acc + w_ref[kk:kk + 1, ln] * dbuf[fo:fo + sub, ln]
                    bo = CONV_HALO + s * sub - (CONV_K - 1) + kk
                    dw_ref[kk:kk + 1, ln] += jnp.sum(d_here * ubuf[bo:bo + sub, ln], axis=0, keepdims=True)
                du0[s * sub:(s + 1) * sub, ln] = acc
        a = a_ref[...]
        sg = _sigmoid(g_ref[...])
        d0 = du0[...]
        dadg_ref[:, 0:C] = (d0 * sg).astype(dadg_ref.dtype)
        dadg_ref[:, C:2 * C] = (d0 * a * sg * (1.0 - sg)).astype(dadg_ref.dtype)

    cur = lambda blk: pl.BlockSpec((tm, C), lambda r: (r, blk))
    prev = lambda blk: pl.BlockSpec((CONV_HALO, C), lambda r: (jnp.maximum(r * hpb - 1, 0), blk))
    nxt = lambda blk: pl.BlockSpec((CONV_HALO, C), lambda r: (jnp.minimum((r + 1) * hpb, last_halo), blk))
    vec = pl.BlockSpec((1, C), lambda r: (0, 0))
    wspec = pl.BlockSpec((CONV_HALO, C), lambda r: (0, 0))
    return pl.pallas_call(
        body, name=name,
        out_shape=(jax.ShapeDtypeStruct((LP, 2 * C), BF16), jax.ShapeDtypeStruct((CONV_HALO, C), F32),
                   jax.ShapeDtypeStruct((1, C), F32), jax.ShapeDtypeStruct((1, C), F32),
                   jax.ShapeDtypeStruct((1, C), F32)),
        grid=(nblk,),
        in_specs=[cur(A_BLK), cur(G_BLK), prev(A_BLK), prev(G_BLK), cur(0), nxt(0), cur(1), nxt(1),
                  wspec, vec, vec],
        out_specs=(pl.BlockSpec((tm, 2 * C), lambda r: (r, 0)), wspec, vec, vec, vec),
        scratch_shapes=[pltpu.VMEM((CONV_HALO + tm, C), F32), pltpu.VMEM((tm + CONV_HALO, C), F32),
                        pltpu.VMEM((tm, C), F32)],
        compiler_params=_params(("arbitrary",)))(proj, proj, proj, proj, u1, u1, dcat, dcat, cw, lg, lb)


FFN_HALO = 8
FFN_TC = 256
FFN_K = 3


def _ffn_conv(buf, w_ref, b_ref, s, sub, ln):
    acc = jnp.broadcast_to(b_ref[:, ln], (sub, LANE))
    for kk in range(FFN_K):
        off = FFN_HALO + s * sub - (FFN_K - 1) + kk
        acc = acc + w_ref[kk:kk + 1, ln] * buf[off:off + sub, ln]
    return acc


def _ffn_act_fwd(up, w, b, tm, name):
    LP, F = up.shape[0], up.shape[1] // 2
    upg = upv = up
    nct = F // FFN_TC
    sub = _sub_rows(tm)
    hpb = tm // FFN_HALO

    def body(g_ref, v_ref, gh_ref, vh_ref, wg_ref, wv_ref, bg_ref, bv_ref, act_ref, gbuf, vbuf):
        r = pl.program_id(1)
        gbuf[FFN_HALO:FFN_HALO + tm, :] = g_ref[...]
        vbuf[FFN_HALO:FFN_HALO + tm, :] = v_ref[...]
        gbuf[0:FFN_HALO, :] = jnp.where(r > 0, gh_ref[...], 0.0)
        vbuf[0:FFN_HALO, :] = jnp.where(r > 0, vh_ref[...], 0.0)
        for s in range(tm // sub):
            for ct in range(FFN_TC // LANE):
                ln = slice(ct * LANE, (ct + 1) * LANE)
                gc = _ffn_conv(gbuf, wg_ref, bg_ref, s, sub, ln)
                vc = _ffn_conv(vbuf, wv_ref, bv_ref, s, sub, ln)
                act_ref[s * sub:(s + 1) * sub, ln] = (gc * _sigmoid(gc) * vc).astype(act_ref.dtype)

    cur = pl.BlockSpec((tm, FFN_TC), lambda c, r: (r, c))
    halo = pl.BlockSpec((FFN_HALO, FFN_TC), lambda c, r: (jnp.maximum(r * hpb - 1, 0), c))
    wg = pl.BlockSpec((8, FFN_TC), lambda c, r: (0, c))
    wv = pl.BlockSpec((8, FFN_TC), lambda c, r: (0, nct + c))
    bg = pl.BlockSpec((1, FFN_TC), lambda c, r: (0, c))
    bv = pl.BlockSpec((1, FFN_TC), lambda c, r: (0, nct + c))
    curv = pl.BlockSpec((tm, FFN_TC), lambda c, r: (r, nct + c))
    halov = pl.BlockSpec((FFN_HALO, FFN_TC), lambda c, r: (jnp.maximum(r * hpb - 1, 0), nct + c))
    return pl.pallas_call(
        body, name=name, out_shape=jax.ShapeDtypeStruct((LP, F), BF16), grid=(nct, LP // tm),
        in_specs=[cur, curv, halo, halov, wg, wv, bg, bv], out_specs=cur,
        scratch_shapes=[pltpu.VMEM((FFN_HALO + tm, FFN_TC), F32)] * 2,
        compiler_params=_params(("parallel", "parallel")))(upg, upv, upg, upv, w, w, b, b)


def _ffn_act_bwd(up, dact, w, b, tm, name):
    LP, F = up.shape[0], up.shape[1] // 2
    upg = upv = up
    nct = F // FFN_TC
    sub = _sub_rows(tm)
    hpb = tm // FFN_HALO
    nblk = LP // tm
    last_halo = LP // FFN_HALO - 1
    TB = tm + 2 * FFN_HALO

    def body(g_ref, v_ref, gp_ref, vp_ref, gn_ref, vn_ref, da_ref, dan_ref,
             wg_ref, wv_ref, bg_ref, bv_ref,
             dup_ref, dwg_ref, dwv_ref, dbg_ref, dbv_ref, gbuf, vbuf, dgb, dvb):
        r = pl.program_id(1)
        dg_ref = dup_ref.at[0]
        dv_ref = dup_ref.at[1]
        first = r == 0
        last = r == nblk - 1

        @pl.when(first)
        def _():
            dwg_ref[...] = jnp.zeros_like(dwg_ref)
            dwv_ref[...] = jnp.zeros_like(dwv_ref)
            dbg_ref[...] = jnp.zeros_like(dbg_ref)
            dbv_ref[...] = jnp.zeros_like(dbv_ref)

        for buf, c_ref, p_ref, n_ref in ((gbuf, g_ref, gp_ref, gn_ref), (vbuf, v_ref, vp_ref, vn_ref)):
            buf[0:FFN_HALO, :] = jnp.where(first, 0.0, p_ref[...])
            buf[FFN_HALO:FFN_HALO + tm, :] = c_ref[...]
            buf[FFN_HALO + tm:TB, :] = jnp.where(last, 0.0, n_ref[...])

        def dconv(s0, nrows, ln, dact_v):
            gc = jnp.broadcast_to(bg_ref[:, ln], (nrows, LANE))
            vc = jnp.broadcast_to(bv_ref[:, ln], (nrows, LANE))
            for kk in range(FFN_K):
                off = s0 - (FFN_K - 1) + kk
                gc = gc + wg_ref[kk:kk + 1, ln] * gbuf[off:off + nrows, ln]
                vc = vc + wv_ref[kk:kk + 1, ln] * vbuf[off:off + nrows, ln]
            sg = _sigmoid(gc)
            return dact_v * vc * (sg * (1.0 + gc * (1.0 - sg))), dact_v * (gc * sg)

        for ct in range(FFN_TC // LANE):
            ln = slice(ct * LANE, (ct + 1) * LANE)
            for s in range(tm // sub):
                dgc, dvc = dconv(FFN_HALO + s * sub, sub, ln, da_ref[s * sub:(s + 1) * sub, ln])
                dgb[s * sub:(s + 1) * sub, ln] = dgc
                dvb[s * sub:(s + 1) * sub, ln] = dvc
            dgc, dvc = dconv(FFN_HALO + tm, FFN_HALO, ln, jnp.where(last, 0.0, dan_ref[:, ln]))
            dgb[tm:tm + FFN_HALO, ln] = dgc
            dvb[tm:tm + FFN_HALO, ln] = dvc
            for dbuf, ubuf, w_ref, dw_ref, db_ref, dout in (
                    (dgb, gbuf, wg_ref, dwg_ref, dbg_ref, dg_ref), (dvb, vbuf, wv_ref, dwv_ref, dbv_ref, dv_ref)):
                for s in range(tm // sub):
                    d_here = dbuf[s * sub:(s + 1) * sub, ln]
                    acc = jnp.zeros((sub, LANE), F32)
                    for kk in range(FFN_K):
                        fo = s * sub + (FFN_K - 1) - kk
                        acc = acc + w_ref[kk:kk + 1, ln] * dbuf[fo:fo + sub, ln]
                        bo = FFN_HALO + s * sub - (FFN_K - 1) + kk
                        dw_ref[kk:kk + 1, ln] += jnp.sum(d_here * ubuf[bo:bo + sub, ln], axis=0, keepdims=True)
                    db_ref[:, ln] += jnp.sum(d_here, axis=0, keepdims=True)
                    dout[s * sub:(s + 1) * sub, ln] = acc.astype(dout.dtype)

    cur = pl.BlockSpec((tm, FFN_TC), lambda c, r: (r, c))
    prev = pl.BlockSpec((FFN_HALO, FFN_TC), lambda c, r: (jnp.maximum(r * hpb - 1, 0), c))
    nxt = pl.BlockSpec((FFN_HALO, FFN_TC), lambda c, r: (jnp.minimum((r + 1) * hpb, last_halo), c))
    wg = pl.BlockSpec((8, FFN_TC), lambda c, r: (0, c))
    wv = pl.BlockSpec((8, FFN_TC), lambda c, r: (0, nct + c))
    bg = pl.BlockSpec((1, FFN_TC), lambda c, r: (0, c))
    bv = pl.BlockSpec((1, FFN_TC), lambda c, r: (0, nct + c))
    curv = pl.BlockSpec((tm, FFN_TC), lambda c, r: (r, nct + c))
    prevv = pl.BlockSpec((FFN_HALO, FFN_TC), lambda c, r: (jnp.maximum(r * hpb - 1, 0), nct + c))
    nxtv = pl.BlockSpec((FFN_HALO, FFN_TC), lambda c, r: (jnp.minimum((r + 1) * hpb, last_halo), nct + c))
    dup, dwg, dwv, dbg, dbv = pl.pallas_call(
        body, name=name,
        out_shape=(jax.ShapeDtypeStruct((2, LP, F), BF16),
                   jax.ShapeDtypeStruct((8, F), F32), jax.ShapeDtypeStruct((8, F), F32),
                   jax.ShapeDtypeStruct((1, F), F32), jax.ShapeDtypeStruct((1, F), F32)),
        grid=(nct, nblk),
        in_specs=[cur, curv, prev, prevv, nxt, nxtv, cur, nxt, wg, wv, bg, bv],
        out_specs=(pl.BlockSpec((2, tm, FFN_TC), lambda c, r: (0, r, c)),
                   pl.BlockSpec((8, FFN_TC), lambda c, r: (0, c)),
                   pl.BlockSpec((8, FFN_TC), lambda c, r: (0, c)),
                   pl.BlockSpec((1, FFN_TC), lambda c, r: (0, c)),
                   pl.BlockSpec((1, FFN_TC), lambda c, r: (0, c))),
        scratch_shapes=[pltpu.VMEM((TB, FFN_TC), F32), pltpu.VMEM((TB, FFN_TC), F32),
                        pltpu.VMEM((tm + FFN_HALO, FFN_TC), F32), pltpu.VMEM((tm + FFN_HALO, FFN_TC), F32)],
        compiler_params=_params(("parallel", "arbitrary")))(
            upg, upv, upg, upv, upg, upv, dact, dact, w, w, b, b)
    return dup, jnp.concatenate([dwg, dwv], axis=1), jnp.concatenate([dbg, dbv], axis=1)


POOL_HALO = 16


def _pool_fwd(h, g, pw, pb, ps, tm, name):
    LP, Dm = h.shape
    sub = _sub_rows(tm)
    hpb = tm // POOL_HALO

    def body(h_ref, hh_ref, g_ref, pw_ref, pb_ref, ps_ref, o_ref, d_ref, buf):
        r = pl.program_id(0)
        gg = g_ref[...]

        def norm(x):
            return x * lax.rsqrt(jnp.mean(x * x, axis=1, keepdims=True) + RMS_EPS) * gg

        x = h_ref[...]
        buf[POOL_HALO:POOL_HALO + tm, :] = norm(x)
        buf[0:POOL_HALO, :] = jnp.where(r > 0, norm(hh_ref[...]), 0.0)
        for gi, w in enumerate(POOL_WINDOWS):
            ln = slice(gi * POOL_G, (gi + 1) * POOL_G)
            for s in range(tm // sub):
                base = POOL_HALO + s * sub
                acc = buf[base:base + sub, ln]
                for jj in range(1, w):
                    acc = acc + buf[base - jj:base - jj + sub, ln]
                t = r * tm + s * sub + lax.broadcasted_iota(jnp.int32, (sub, 1), 0)
                cnt = jnp.minimum(t + 1, w).astype(F32)
                d_ref[s * sub:(s + 1) * sub, ln] = (acc / cnt - buf[base:base + sub, ln]).astype(d_ref.dtype)
            y = jnp.dot(d_ref[:, ln], pw_ref[gi], preferred_element_type=F32) + pb_ref[:, ln]
            o_ref[:, ln] = x[:, ln] + y * ps_ref[:, ln]

    row = pl.BlockSpec((tm, Dm), lambda r: (r, 0))
    halo = pl.BlockSpec((POOL_HALO, Dm), lambda r: (jnp.maximum(r * hpb - 1, 0), 0))
    vec = pl.BlockSpec((1, Dm), lambda r: (0, 0))
    wsp = pl.BlockSpec((len(POOL_WINDOWS), POOL_G, POOL_G), lambda r: (0, 0, 0))
    return pl.pallas_call(
        body, name=name,
        out_shape=(jax.ShapeDtypeStruct((LP, Dm), F32), jax.ShapeDtypeStruct((LP, Dm), BF16)),
        grid=(LP // tm,), in_specs=[row, halo, vec, wsp, vec, vec], out_specs=(row, row),
        scratch_shapes=[pltpu.VMEM((POOL_HALO + tm, Dm), F32)],
        compiler_params=_params(("parallel",)))(h, h, g, pw, pb, ps)


def _pool_bwd(h, g, d, pw, pb, ps, dh_out, tm, name):
    LP, Dm = h.shape
    sub = _sub_rows(tm)
    hpb = tm // POOL_HALO
    nblk = LP // tm
    last_halo = LP // POOL_HALO - 1
    nt = (((1,), (1,)), ((), ()))
    tn = (((0,), (0,)), ((), ()))

    def body(h_ref, g_ref, d_ref, pw_ref, pb_ref, ps_ref, do_ref, don_ref,
             dh_ref, dpw_ref, dpb_ref, dps_ref, dg_ref, ebuf, ddb, dnb):
        r = pl.program_id(0)

        @pl.when(r == 0)
        def _():
            dpw_ref[...] = jnp.zeros_like(dpw_ref)
            dpb_ref[...] = jnp.zeros_like(dpb_ref)
            dps_ref[...] = jnp.zeros_like(dps_ref)
            dg_ref[...] = jnp.zeros_like(dg_ref)

        for gi, w in enumerate(POOL_WINDOWS):
            ln = slice(gi * POOL_G, (gi + 1) * POOL_G)
            wg = pw_ref[gi]
            dog = do_ref[:, ln]
            dg_b = d_ref[:, ln]
            y_pre = jnp.dot(dg_b, wg, preferred_element_type=F32) + pb_ref[:, ln]
            dps_ref[:, ln] += jnp.sum(dog * y_pre, axis=0, keepdims=True)
            dy = dog * ps_ref[:, ln]
            dpb_ref[:, ln] += jnp.sum(dy, axis=0, keepdims=True)
            dyb = dy.astype(BF16)
            dpw_ref[gi] += lax.dot_general(dg_b, dyb, tn, preferred_element_type=F32)
            dd = lax.dot_general(dyb, wg, nt, preferred_element_type=F32)
            ddb[:, ln] = dd
            t = r * tm + lax.broadcasted_iota(jnp.int32, (tm, 1), 0)
            ebuf[0:tm, ln] = dd / jnp.minimum(t + 1, w).astype(F32)
            dyn = (don_ref[:, ln] * ps_ref[:, ln]).astype(BF16)
            ddn = lax.dot_general(dyn, wg, nt, preferred_element_type=F32)
            tn_ = (r + 1) * tm + lax.broadcasted_iota(jnp.int32, (POOL_HALO, 1), 0)
            ebuf[tm:tm + POOL_HALO, ln] = jnp.where(r < nblk - 1, ddn / jnp.minimum(tn_ + 1, w).astype(F32), 0.0)
            for s in range(tm // sub):
                acc = ebuf[s * sub:(s + 1) * sub, ln]
                for jj in range(1, w):
                    acc = acc + ebuf[s * sub + jj:s * sub + jj + sub, ln]
                dnb[s * sub:(s + 1) * sub, ln] = acc - ddb[s * sub:(s + 1) * sub, ln]
        x = h_ref[...]
        rr = lax.rsqrt(jnp.mean(x * x, axis=1, keepdims=True) + RMS_EPS)
        xhat = x * rr
        dn = dnb[...]
        dxh = dn * g_ref[...]
        dh_ref[...] = do_ref[...] + rr * (dxh - xhat * jnp.mean(dxh * xhat, axis=1, keepdims=True))
        dg_ref[...] += jnp.sum(dn * xhat, axis=0, keepdims=True)

    row = pl.BlockSpec((tm, Dm), lambda r: (r, 0))
    nxt = pl.BlockSpec((POOL_HALO, Dm), lambda r: (jnp.minimum((r + 1) * hpb, last_halo), 0))
    vec = pl.BlockSpec((1, Dm), lambda r: (0, 0))
    wsp = pl.BlockSpec((len(POOL_WINDOWS), POOL_G, POOL_G), lambda r: (0, 0, 0))
    return pl.pallas_call(
        body, name=name,
        out_shape=(jax.ShapeDtypeStruct((LP, Dm), F32),
                   jax.ShapeDtypeStruct((len(POOL_WINDOWS), POOL_G, POOL_G), F32),
                   jax.ShapeDtypeStruct((1, Dm), F32), jax.ShapeDtypeStruct((1, Dm), F32),
                   jax.ShapeDtypeStruct((1, Dm), F32)),
        grid=(nblk,), in_specs=[row, vec, row, wsp, vec, vec, row, nxt],
        out_specs=(row, wsp, vec, vec, vec),
        scratch_shapes=[pltpu.VMEM((tm + POOL_HALO, Dm), F32), pltpu.VMEM((tm, Dm), F32),
                        pltpu.VMEM((tm, Dm), F32)],
        compiler_params=_params(("arbitrary",)))(h, g, d, pw, pb, ps, dh_out, dh_out)


def _adamw(w, g, m, v, name):
    shape = w.shape
    cols = shape[-1]
    rows = int(np.prod(shape[:-1])) if len(shape) > 1 else 1
    w2, g2, m2, v2 = (t.reshape(rows, cols) for t in (w, g, m, v))
    tr = rows
    for cand in (256, 128, 64, 32, 16, 8):
        if rows % cand == 0 and rows > cand:
            tr = cand
            break
    c1 = float(1.0 - ADAM_B1 ** ADAM_STEP)
    c2 = float(1.0 - ADAM_B2 ** ADAM_STEP)

    def body(w_ref, g_ref, m_ref, v_ref, d_ref, mo_ref, vo_ref):
        gg = g_ref[...]
        mn = ADAM_B1 * m_ref[...] + (1.0 - ADAM_B1) * gg
        vn = ADAM_B2 * v_ref[...] + (1.0 - ADAM_B2) * (gg * gg)
        m_hat = mn / c1
        v_hat = vn / c2
        d_ref[...] = -ADAM_LR * (m_hat / (jnp.sqrt(v_hat) + ADAM_EPS) + ADAM_WD * w_ref[...])
        mo_ref[...] = mn
        vo_ref[...] = vn

    spec = pl.BlockSpec((tr, cols), lambda i: (i, 0))
    sds = jax.ShapeDtypeStruct((rows, cols), F32)
    d2, mo, vo = pl.pallas_call(
        body, name=name, out_shape=(sds, sds, sds), grid=(rows // tr,),
        in_specs=[spec] * 4, out_specs=(spec,) * 3,
        compiler_params=_params(("parallel",)))(w2, g2, m2, v2)
    return d2.reshape(shape), mo.reshape(shape), vo.reshape(shape)


def _row_tiles(LP):
    tm = LP // 4
    assert LP % 4 == 0 and tm % CONV_HALO == 0 and LP % ATT_BLK == 0, LP
    return tm, LP // 2


def _heads(t, LP):
    return t.reshape(LP, HEADS, HEAD_DIM).transpose(1, 0, 2)


def _unheads(t, LP):
    return t.transpose(1, 0, 2).reshape(LP, FOX_W)


def _ffn_fwd(h, gain, wug, wuv, cw, cb, wd, tm, tmm, tag):
    n = _rms_fwd(h, gain, BF16, tm, f"ffn_norm_{tag}")
    upg = _mm(n, wug, "nn", F32, tmm, 256, f"ffn_up_gate_{tag}")
    upv = _mm(n, wuv, "nn", F32, tmm, 256, f"ffn_up_val_{tag}")
    act = _ffn_act_fwd(upg, upv, cw, cb, tm, f"ffn_act_{tag}")
    out = _mm(act, wd, "nn", F32, tmm, 512, f"ffn_down_{tag}", add=h)
    return out, (n, upg, upv, act)


def _ffn_bwd(h, gain, wug, wuv, cw, cb, wd, saved, dout, tm, tmm, tag):
    n, upg, upv, act = saved
    dact = _mm(dout, wd, "nt", F32, tmm, 256, f"ffn_dact_{tag}")
    dwd = _mm(act, dout, "tn", F32, 256, 512, f"ffn_dwdown_{tag}")
    dupg, dupv, dcw, dcb = _ffn_act_bwd(upg, upv, dact, cw, cb, tm, f"ffn_act_bwd_{tag}")
    dn = _mm(dupg, wug, "nt", F32, tm, 512, f"ffn_dn_gate_{tag}")
    dn = _mm(dupv, wuv, "nt", F32, tm, 512, f"ffn_dn_val_{tag}", add=dn)
    dwug = _mm(n, dupg, "tn", F32, 512, 256, f"ffn_dwup_gate_{tag}")
    dwuv = _mm(n, dupv, "tn", F32, 512, 256, f"ffn_dwup_val_{tag}")
    dh, dgain = _rms_bwd(h, gain, dn, dout, tm, f"ffn_norm_bwd_{tag}")
    return dh, dict(gain=dgain, wug=dwug, wuv=dwuv, cw=dcw[:FFN_K], cb=dcb, wd=dwd)


def _local_step(h0, tgt, W, n_real):
    LP = h0.shape[0]
    tm, tmm = _row_tiles(LP)
    nb = LP // ATT_BLK
    G = {}

    n0 = _rms_fwd(h0, W["mix_norm_even"], BF16, tm, "mix_norm_even")
    proj = _mm(n0, W["w_in_p"], "nn", F32, tmm, 384, "in_proj")
    c = _fgate_fwd(proj, W["b_f_p"], "forget_gate")
    cT = c[:, :HEADS].T
    c_col = cT[:, :, None]
    c_row = cT.reshape(HEADS, nb, 1, ATT_BLK)
    qkv = proj[:, :3 * FOX_W].astype(BF16)
    q, k, v = (_heads(qkv[:, i * FOX_W:(i + 1) * FOX_W], LP) for i in range(3))
    o, lse = _attn_fwd(q, k, v, c_col, c_row, "fox_attention")
    u1, u = _conf_fwd(proj, W["conv_w_p"], W["conv_b"], W["ln_g"], W["ln_b"], tm, "conformer")
    cat = jnp.concatenate([_unheads(o, LP).astype(BF16), u], axis=1)
    h1 = _mm(cat, W["w_out"], "nn", F32, tmm, 512, "out_proj", add=h0)
    h2, ffn0 = _ffn_fwd(h1, W["ffn_norm"][0:1], W["w_up_g"][0], W["w_up_v"][0], W["ffn_conv_w_p"][0],
                        W["ffn_conv_b"][0:1], W["w_down"][0], tm, tmm, "0")
    h3, dpool = _pool_fwd(h2, W["mix_norm_odd"], W["pool_w"], W["pool_b"], W["pool_scale"], tm, "pool_mixer")
    h4, ffn1 = _ffn_fwd(h3, W["ffn_norm"][1:2], W["w_up_g"][1], W["w_up_v"][1], W["ffn_conv_w_p"][1],
                        W["ffn_conv_b"][1:2], W["w_down"][1], tm, tmm, "1")
    loss, dh4, G["final_norm"] = _loss_head(h4, W["final_norm"], tgt, n_real, tm, "loss_head")

    dh3, g1 = _ffn_bwd(h3, W["ffn_norm"][1:2], W["w_up_g"][1], W["w_up_v"][1], W["ffn_conv_w_p"][1],
                       W["ffn_conv_b"][1:2], W["w_down"][1], ffn1, dh4, tm, tmm, "1")
    dh2, G["pool_w"], G["pool_b"], G["pool_scale"], G["mix_norm_odd"] = _pool_bwd(
        h2, W["mix_norm_odd"], dpool, W["pool_w"], W["pool_b"], W["pool_scale"], dh3, tm, "pool_mixer_bwd")
    dh1, g0 = _ffn_bwd(h1, W["ffn_norm"][0:1], W["w_up_g"][0], W["w_up_v"][0], W["ffn_conv_w_p"][0],
                       W["ffn_conv_b"][0:1], W["w_down"][0], ffn0, dh2, tm, tmm, "0")
    for key in ("gain", "wug", "wuv", "cw", "cb", "wd"):
        G["ffn_" + key] = (g0[key], g1[key])

    dcat = _mm(dh1, W["w_out"], "nt", F32, tmm, 512, "out_proj_dx")
    G["w_out"] = _mm(cat, dh1, "tn", F32, 512, 512, "out_proj_dw")
    dadg, dcw, G["conv_b"], G["ln_g"], G["ln_b"] = _conf_bwd(
        proj, u1, dcat, W["conv_w_p"], W["ln_g"], W["ln_b"], tm, "conformer_bwd")
    G["conv_w"] = dcw[:CONV_K]
    do = _heads(dcat[:, :FOX_W], LP)
    dq, dk, dv, dcq, dck = _attn_bwd(q, k, v, o, do, lse, c_col, c_row, "fox_attention_bwd")
    dc = jnp.pad((dcq.reshape(HEADS, LP) + dck.reshape(HEADS, LP)).T, ((0, 0), (0, LANE - HEADS)))
    dfl, dbf = _fgate_bwd(proj, W["b_f_p"], dc, "forget_gate_bwd")
    G["b_f"] = dbf[:, :HEADS]
    dproj = jnp.concatenate([_unheads(t, LP).astype(BF16) for t in (dq, dk, dv)] + [dadg, dfl], axis=1)
    dn0 = _mm(dproj, W["w_in_p"], "nt", F32, tmm, 512, "in_proj_dx")
    G["w_in_p"] = _mm(n0, dproj, "tn", F32, 512, 384, "in_proj_dw")
    dh0, G["mix_norm_even"] = _rms_bwd(h0, W["mix_norm_even"], dn0, dh1, tm, "mix_norm_even_bwd")
    return loss, dh0, G


def _compute_layout(P):
    w_in = P["w_in"].reshape(D_MODEL, IN_COLS)
    qkv, f, ag = w_in[:, :3 * FOX_W], w_in[:, 3 * FOX_W:3 * FOX_W + HEADS], w_in[:, 3 * FOX_W + HEADS:]
    w_in_p = jnp.concatenate([qkv, ag, f, jnp.zeros((D_MODEL, LANE - HEADS), w_in.dtype)], axis=1).astype(BF16)
    w_up = P["w_up"].astype(BF16)
    return dict(
        mix_norm_even=P["mix_norm_even"].reshape(1, D_MODEL).astype(F32),
        w_in_p=w_in_p,
        b_f_p=jnp.pad(P["b_f"].reshape(1, HEADS).astype(F32), ((0, 0), (0, LANE - HEADS))),
        conv_w_p=jnp.pad(P["conv_w"].reshape(CONV_K, CONV_CH).astype(F32), ((0, CONV_HALO - CONV_K), (0, 0))),
        conv_b=P["conv_b"].reshape(1, CONV_CH).astype(F32),
        ln_g=P["ln_g"].reshape(1, CONV_CH).astype(F32),
        ln_b=P["ln_b"].reshape(1, CONV_CH).astype(F32),
        w_out=P["w_out"].reshape(D_MODEL, D_MODEL).astype(BF16),
        mix_norm_odd=P["mix_norm_odd"].reshape(1, D_MODEL).astype(F32),
        pool_w=P["pool_w"].reshape(len(POOL_WINDOWS), POOL_G, POOL_G).astype(BF16),
        pool_b=P["pool_b"].reshape(1, D_MODEL).astype(F32),
        pool_scale=P["pool_scale"].reshape(1, D_MODEL).astype(F32),
        ffn_norm=P["ffn_norm"].astype(F32),
        w_up_g=w_up[:, :, :D_FF],
        w_up_v=w_up[:, :, D_FF:],
        ffn_conv_w_p=jnp.pad(P["ffn_conv_w"].astype(F32), ((0, 0), (0, 8 - FFN_K), (0, 0))),
        ffn_conv_b=P["ffn_conv_b"].astype(F32),
        w_down=P["w_down"].astype(BF16),
        final_norm=P["final_norm"].reshape(1, D_MODEL).astype(F32),
    )


def _reference_layout(G, dh0):
    gp = G["w_in_p"]
    g_w_in = jnp.concatenate([gp[:, :3 * FOX_W], gp[:, 3 * FOX_W + 2 * CONV_CH:3 * FOX_W + 2 * CONV_CH + HEADS],
                              gp[:, 3 * FOX_W:3 * FOX_W + 2 * CONV_CH]], axis=1)
    return dict(
        meta_tokens=dh0[:N_META],
        mix_norm_even=G["mix_norm_even"],
        w_in=g_w_in[None],
        b_f=G["b_f"],
        conv_w=G["conv_w"][None],
        conv_b=G["conv_b"],
        ln_g=G["ln_g"],
        ln_b=G["ln_b"],
        w_out=G["w_out"][None],
        mix_norm_odd=G["mix_norm_odd"],
        pool_w=G["pool_w"][None],
        pool_b=G["pool_b"].reshape(1, len(POOL_WINDOWS), POOL_G),
        pool_scale=G["pool_scale"],
        ffn_norm=jnp.concatenate(G["ffn_gain"], axis=0),
        w_up=jnp.stack([jnp.concatenate([g, v], axis=1) for g, v in zip(G["ffn_wug"], G["ffn_wuv"])]),
        ffn_conv_w=jnp.stack(G["ffn_cw"]),
        ffn_conv_b=jnp.concatenate(G["ffn_cb"], axis=0),
        w_down=jnp.stack(G["ffn_wd"]),
        final_norm=G["final_norm"].reshape(D_MODEL),
    )


MESH = pl.DeviceIdType.MESH
ANY = pl.BlockSpec(memory_space=pl.ANY)
PACK_COLS = 1024


def _coords():
    return lax.axis_index("x"), lax.axis_index("y"), lax.axis_index("c")


def _other_chips(x, y):
    return [(1 - x, y), (x, 1 - y), (1 - x, 1 - y)]


def _allgather_chips(pack):
    R, C = pack.shape
    R2 = R // 2

    def body(x_ref, o_ref, send_sems, recv_sems, local_sem):
        x, y, c = _coords()
        sibling = (x, y, 1 - c)
        chips = _other_chips(x, y)

        def slot(px, py, half):
            return o_ref.at[2 * px + py, pl.ds(half * R2, R2), :]

        def copy(k, src, dst, to):
            return pltpu.make_async_remote_copy(src_ref=src, dst_ref=dst, send_sem=send_sems.at[k],
                                                recv_sem=recv_sems.at[k], device_id=to, device_id_type=MESH)

        mine = pltpu.make_async_copy(x_ref, o_ref.at[2 * x + y], local_sem)
        mine.start()
        my_half = x_ref.at[pl.ds(c * R2, R2), :]
        first = [copy(j, my_half, slot(x, y, c), (*chip, c)) for j, chip in enumerate(chips)]
        for cp in first:
            cp.start()
        passed = [copy(3 + j, slot(*chip, c), slot(*chip, c), sibling) for j, chip in enumerate(chips)]
        for j, chip in enumerate(chips):
            copy(j, my_half, slot(*chip, c), sibling).wait_recv()
            passed[j].start()
        for j, chip in enumerate(chips):
            copy(3 + j, my_half, slot(*chip, 1 - c), sibling).wait_recv()
        for cp in first + passed:
            cp.wait_send()
        mine.wait()

    return pl.pallas_call(
        body, name="allgather_weights", out_shape=jax.ShapeDtypeStruct((N_CHIPS, R, C), pack.dtype),
        in_specs=[ANY], out_specs=ANY,
        scratch_shapes=[pltpu.SemaphoreType.DMA((6,)), pltpu.SemaphoreType.DMA((6,)), pltpu.SemaphoreType.DMA],
    )(pack)


def _pair_exchange(G):
    n, R, C = G.shape
    R2 = R // 2

    def body(g_ref, o_ref, send_sem, recv_sem):
        x, y, c = _coords()
        src = g_ref.at[pl.ds(0, n), pl.ds((1 - c) * R2, R2), :]
        cp = pltpu.make_async_remote_copy(src_ref=src, dst_ref=o_ref, send_sem=send_sem, recv_sem=recv_sem,
                                          device_id=(x, y, 1 - c), device_id_type=MESH)
        cp.start()
        cp.wait()

    return pl.pallas_call(
        body, name="grad_pair_exchange", out_shape=jax.ShapeDtypeStruct((n, R2, C), G.dtype),
        in_specs=[ANY], out_specs=ANY,
        scratch_shapes=[pltpu.SemaphoreType.DMA, pltpu.SemaphoreType.DMA],
    )(G)


def _row_tile(rows, align, cap):
    best = None
    for t in range(align, min(rows, cap) + 1, align):
        if rows % t == 0:
            best = t
    assert best is not None, (rows, align, cap)
    return best


def _pair_sum(G, recv):
    n, R, C = G.shape
    R2 = R // 2
    tr = _row_tile(R2, 16, 704)
    nrb = R2 // tr
    half = lax.axis_index("c").astype(jnp.int32).reshape(1)

    def body(c_ref, g_ref, r_ref, o_ref):
        o_ref[...] = (g_ref[...] + r_ref[...]).astype(o_ref.dtype)

    return pl.pallas_call(
        body, name="grad_pair_sum", out_shape=jax.ShapeDtypeStruct((n, R2, C), BF16),
        grid_spec=pltpu.PrefetchScalarGridSpec(
            num_scalar_prefetch=1, grid=(n, nrb),
            in_specs=[pl.BlockSpec((None, tr, C), lambda j, i, c_ref: (j, c_ref[0] * nrb + i, 0)),
                      pl.BlockSpec((None, tr, C), lambda j, i, c_ref: (j, i, 0))],
            out_specs=pl.BlockSpec((None, tr, C), lambda j, i, c_ref: (j, i, 0))),
        compiler_params=_params(("parallel", "parallel")))(half, G, recv)


def _chip_exchange(P):
    n, R2, C = P.shape

    def body(p_ref, o_ref, send_sems, recv_sems, local_sem):
        x, y, c = _coords()
        me = 2 * x + y
        chips = _other_chips(x, y)
        mine = pltpu.make_async_copy(p_ref.at[me], o_ref.at[me], local_sem)
        mine.start()
        sends = [pltpu.make_async_remote_copy(
            src_ref=p_ref.at[2 * px + py], dst_ref=o_ref.at[me], send_sem=send_sems.at[k],
            recv_sem=recv_sems.at[k], device_id=(px, py, c), device_id_type=MESH)
            for k, (px, py) in enumerate(chips)]
        for cp in sends:
            cp.start()
        for k, (px, py) in enumerate(chips):
            pltpu.make_async_remote_copy(
                src_ref=p_ref.at[me], dst_ref=o_ref.at[2 * px + py], send_sem=send_sems.at[k],
                recv_sem=recv_sems.at[k], device_id=(px, py, c), device_id_type=MESH).wait_recv()
        for cp in sends:
            cp.wait_send()
        mine.wait()

    return pl.pallas_call(
        body, name="grad_chip_exchange", out_shape=jax.ShapeDtypeStruct((n, R2, C), P.dtype),
        in_specs=[ANY], out_specs=ANY,
        scratch_shapes=[pltpu.SemaphoreType.DMA((3,)), pltpu.SemaphoreType.DMA((3,)), pltpu.SemaphoreType.DMA],
    )(P)


def _chip_sum(X):
    n, R2, C = X.shape
    tr = _row_tile(R2, 16, 704)

    def body(x_ref, o_ref):
        acc = x_ref[0].astype(F32)
        for s in range(1, n):
            acc = acc + x_ref[s].astype(F32)
        o_ref[...] = acc

    return pl.pallas_call(
        body, name="grad_chip_sum", out_shape=jax.ShapeDtypeStruct((R2, C), F32), grid=(R2 // tr,),
        in_specs=[pl.BlockSpec((n, tr, C), lambda i: (0, i, 0))],
        out_specs=pl.BlockSpec((tr, C), lambda i: (i, 0)),
        compiler_params=_params(("parallel",)))(X)


def _pair_allgather(Q):
    R2, C = Q.shape

    def body(q_ref, o_ref, send_sem, recv_sem, local_sem):
        x, y, c = _coords()
        mine = pltpu.make_async_copy(q_ref, o_ref.at[c], local_sem)
        mine.start()
        cp = pltpu.make_async_remote_copy(src_ref=q_ref, dst_ref=o_ref.at[c], send_sem=send_sem,
                                          recv_sem=recv_sem, device_id=(x, y, 1 - c), device_id_type=MESH)
        cp.start()
        pltpu.make_async_remote_copy(src_ref=q_ref, dst_ref=o_ref.at[1 - c], send_sem=send_sem,
                                     recv_sem=recv_sem, device_id=(x, y, 1 - c), device_id_type=MESH).wait_recv()
        cp.wait_send()
        mine.wait()

    return pl.pallas_call(
        body, name="grad_pair_allgather", out_shape=jax.ShapeDtypeStruct((2, R2, C), Q.dtype),
        in_specs=[ANY], out_specs=ANY,
        scratch_shapes=[pltpu.SemaphoreType.DMA, pltpu.SemaphoreType.DMA, pltpu.SemaphoreType.DMA],
    )(Q)


def _allreduce_small(pack):
    Rs, C = pack.shape
    n_dev = 8

    def body(x_ref, o_ref, buf, send_sems, recv_sems):
        x, y, c = _coords()
        me = 4 * x + 2 * y + c
        buf[me] = x_ref[...]
        peers = []
        for rel in range(1, n_dev):
            px = 1 - x if rel & 4 else x
            py = 1 - y if rel & 2 else y
            pc = 1 - c if rel & 1 else c
            peers.append((px, py, pc))
        sends = [pltpu.make_async_remote_copy(
            src_ref=x_ref, dst_ref=buf.at[me], send_sem=send_sems.at[k], recv_sem=recv_sems.at[k],
            device_id=peer, device_id_type=MESH) for k, peer in enumerate(peers)]
        for cp in sends:
            cp.start()
        for k, (px, py, pc) in enumerate(peers):
            pltpu.make_async_remote_copy(
                src_ref=x_ref, dst_ref=buf.at[4 * px + 2 * py + pc], send_sem=send_sems.at[k],
                recv_sem=recv_sems.at[k], device_id=(px, py, pc), device_id_type=MESH).wait_recv()
        for cp in sends:
            cp.wait_send()
        acc = buf[0]
        for d in range(1, n_dev):
            acc = acc + buf[d]
        o_ref[...] = acc

    vm = pl.BlockSpec(memory_space=pltpu.VMEM)
    return pl.pallas_call(
        body, name="allreduce_replicated", out_shape=jax.ShapeDtypeStruct((Rs, C), F32),
        in_specs=[vm], out_specs=vm,
        scratch_shapes=[pltpu.VMEM((n_dev, Rs, C), F32), pltpu.SemaphoreType.DMA((n_dev - 1,)),
                        pltpu.SemaphoreType.DMA((n_dev - 1,))],
    )(pack)


SHARDED = (
    ("w_in", 2, True), ("w_out", 1, True), ("pool_w", 2, True), ("w_up", 2, True), ("w_down", 1, True),
    ("meta_tokens", 1, False), ("mix_norm_odd", 1, False), ("pool_b", 2, False), ("pool_scale", 1, False),
    ("conv_w", 2, False), ("ffn_conv_w", 2, False))
REPLICATED = ("mix_norm_even", "b_f", "conv_b", "ln_g", "ln_b", "ffn_norm", "ffn_conv_b", "final_norm")
PACK_ROW_ALIGN = 32


def _pad_rows(flat, align_rows, cols):
    rows = -(-flat.shape[-1] // cols)
    rows = -(-rows // align_rows) * align_rows
    pad = rows * cols - flat.shape[-1]
    flat = jnp.pad(flat, [(0, 0)] * (flat.ndim - 1) + [(0, pad)])
    return flat.reshape(flat.shape[:-1] + (rows, cols))


def _pack_weight_shards(shards):
    parts = []
    for name, _, as_bf16 in SHARDED:
        w = shards[name].astype(F32).reshape(-1)
        parts.append(w.astype(BF16) if as_bf16 else lax.bitcast_convert_type(w, BF16).reshape(-1))
    return _pad_rows(jnp.concatenate(parts), PACK_ROW_ALIGN, PACK_COLS)


def _unpack_weights(gathered, shards):
    flat = gathered.reshape(N_CHIPS, -1)
    out, off = {}, 0
    for name, axis, as_bf16 in SHARDED:
        shp = shards[name].shape
        n = int(np.prod(shp))
        if as_bf16:
            t = flat[:, off:off + n]
            off += n
        else:
            t = lax.bitcast_convert_type(flat[:, off:off + 2 * n].reshape(N_CHIPS, n, 2), F32)
            off += 2 * n
        t = t.reshape((N_CHIPS,) + shp)
        out[name] = jnp.concatenate([t[j] for j in range(N_CHIPS)], axis=axis)
    return out


def _pack_grad_shards(grads, shards):
    parts = []
    for name, axis, _ in SHARDED:
        g = grads[name].reshape(shards[name].shape[:axis] + (N_CHIPS, shards[name].shape[axis])
                                + shards[name].shape[axis + 1:])
        parts.append(jnp.moveaxis(g, axis, 0).reshape(N_CHIPS, -1))
    return _pad_rows(jnp.concatenate(parts, axis=1), PACK_ROW_ALIGN, PACK_COLS)


def _unpack_grad_shard(reduced, shards):
    flat = reduced.reshape(-1)
    out, off = {}, 0
    for name, _, _ in SHARDED:
        shp = shards[name].shape
        n = int(np.prod(shp))
        out[name] = flat[off:off + n].reshape(shp)
        off += n
    return out


def _pack_replicated(grads, loss):
    parts = [_pad_rows(grads[name].astype(F32).reshape(-1), 1, LANE).reshape(-1) for name in REPLICATED]
    parts.append(_pad_rows(loss.reshape(-1)[:1], 1, LANE).reshape(-1))
    return _pad_rows(jnp.concatenate(parts), 8, LANE)


def _unpack_replicated(reduced, shapes):
    flat = reduced.reshape(-1)
    out, off = {}, 0
    for name in REPLICATED:
        n = int(np.prod(shapes[name]))
        out[name] = flat[off:off + n].reshape(shapes[name])
        off += -(-n // LANE) * LANE
    return out, flat[off]


def _ffn_fwd2(h, W, layer, tm, tmm):
    tag = str(layer)
    n = _rms_fwd(h, W["ffn_norm"][layer:layer + 1], BF16, tm, f"ffn_norm_{tag}")
    up = _mm(n, W["w_up"], "nn", F32, tmm, 512, f"ffn_up_{tag}", b_lead=layer)
    act = _ffn_act_fwd(up, W["ffn_conv_w_p"][layer], W["ffn_conv_b"][layer:layer + 1], tm, f"ffn_act_{tag}")
    out = _mm(act, W["w_down"], "nn", F32, tmm, 512, f"ffn_down_{tag}", add=h, b_lead=layer)
    return out, (n, up, act)


def _ffn_bwd2(h, W, layer, saved, dout, acc, tm, tmm):
    tag = str(layer)
    n, up, act = saved
    dact = _mm(dout, W["w_down"], "nt", F32, tmm, 256, f"ffn_dact_{tag}", b_lead=layer)
    dwd = _mm(act, dout, "tn", F32, 256, 512, f"ffn_dwdown_{tag}",
              out=(layer, 2, None if acc is None else acc[1]))
    dup, dcw, dcb = _ffn_act_bwd(up, dact, W["ffn_conv_w_p"][layer], W["ffn_conv_b"][layer:layer + 1], tm,
                                 f"ffn_act_bwd_{tag}")
    dn = _mm_ffn_dn(dup, W["w_up"], layer, tm, 512, f"ffn_dn_{tag}")
    dwu = _mm_ffn_dwup(n, dup, layer, None if acc is None else acc[0], 512, 256, f"ffn_dwup_{tag}")
    dh, dgain = _rms_bwd(h, W["ffn_norm"][layer:layer + 1], dn, dout, tm, f"ffn_norm_bwd_{tag}")
    return dh, (dwu, dwd), dict(gain=dgain, cw=dcw[:FFN_K], cb=dcb)


GATHER_FIRST = ("w_in", "w_out", "small")
GATHER_LATE = ("pool_w", "w_up", "w_down")
HOSTED = ("w_out", "pool_w", "w_up", "w_down")
LATE = ("w_in", "small")


def _local_step2(h0, tgt, W, n_real, cut_of):
    LP = h0.shape[0]
    tm, tmm = _row_tiles(LP)
    nb = LP // ATT_BLK
    G = {}
    n0 = _rms_fwd(h0, W["mix_norm_even"], BF16, tm, "mix_norm_even")
    proj = _mm(n0, W["w_in_p"], "nn", F32, tmm, 384, "in_proj")
    c = _fgate_fwd(proj, W["b_f_p"], "forget_gate")
    qT, kT, k_aug, vT = _attn_prep(proj, c, "attention_operands")
    gcuts = [cut_of[n] for n in GATHER_LATE]
    oT, lse, *gfull = _attn_fwd2(qT, k_aug, vT, "fox_attention", comm=(W["late_shards"], gcuts))
    W = dict(W)
    W.update(zip(GATHER_LATE, _gather_forward(gfull, gcuts)))
    u1, u = _conf_fwd(proj, W["conv_w_p"], W["conv_b"], W["ln_g"], W["ln_b"], tm, "conformer")
    cat = jnp.concatenate([_attn_rows(oT, 1.0, BF16, "attention_rows"), u], axis=1)
    h1 = _mm(cat, W["w_out"], "nn", F32, tmm, 512, "out_proj", add=h0)
    h2, ffn0 = _ffn_fwd2(h1, W, 0, tm, tmm)
    h3, dpool = _pool_fwd(h2, W["mix_norm_odd"], W["pool_w"], W["pool_b"], W["pool_scale"], tm, "pool_mixer")
    h4, ffn1 = _ffn_fwd2(h3, W, 1, tm, tmm)
    loss, dh4, G["final_norm"] = _loss_head(h4, W["final_norm"], tgt, n_real, tm, "loss_head")

    dh3, acc, g1 = _ffn_bwd2(h3, W, 1, ffn1, dh4, None, tm, tmm)
    dh2, G["pool_w"], G["pool_b"], G["pool_scale"], G["mix_norm_odd"] = _pool_bwd(
        h2, W["mix_norm_odd"], dpool, W["pool_w"], W["pool_b"], W["pool_scale"], dh3, tm, "pool_mixer_bwd")
    dh1, acc, g0 = _ffn_bwd2(h1, W, 0, ffn0, dh2, acc, tm, tmm)
    G["w_up"], G["w_down"] = acc
    G["ffn_norm"] = jnp.concatenate([g0["gain"], g1["gain"]], axis=0)
    G["ffn_conv_w"] = jnp.stack([g0["cw"], g1["cw"]])
    G["ffn_conv_b"] = jnp.concatenate([g0["cb"], g1["cb"]], axis=0)

    dcat = _mm(dh1, W["w_out"], "nt", F32, tmm, 512, "out_proj_dx")
    G["w_out"] = _mm(cat, dh1, "tn", F32, 512, 512, "out_proj_dw")
    dadg, dcw, G["conv_b"], G["ln_g"], G["ln_b"] = _conf_bwd(
        proj, u1, dcat, W["conv_w_p"], W["ln_g"], W["ln_b"], tm, "conformer_bwd")
    G["conv_w"] = dcw[:CONV_K]
    doT = _attn_cols(dcat, "attention_do_cols")
    hcuts = [cut_of[n] for n in HOSTED]
    hfull = [G[n] for n in HOSTED]
    hrecv = _pair_exchange2(hfull, hcuts, "grad_pair_exchange_early")
    hparts = [_pair_sum2(f, r, cut, PAIR_SUM_BLOCKS[n], "grad_pair_sum_" + n)
              for f, r, cut, n in zip(hfull, hrecv, hcuts, HOSTED)]
    dqT, dkT, dvT, *hothers = _attn_bwd2(qT, kT, k_aug, vT, oT, doT, lse, "fox_attention_bwd",
                                         comm=(hparts, hcuts))
    dfl, dbf = _fgate_bwd(proj, W["b_f_p"], _attn_dc(dqT, dkT, "attention_dc"), "forget_gate_bwd")
    G["b_f"] = dbf[:, :HEADS]
    dproj = jnp.concatenate([_attn_rows(dqT, HEAD_DIM ** -0.5, BF16, "attention_dq_rows"),
                             _attn_rows(dkT, 1.0, BF16, "attention_dk_rows"),
                             _attn_rows(dvT, 1.0, BF16, "attention_dv_rows"), dadg, dfl], axis=1)
    dn0 = _mm(dproj, W["w_in_p"], "nt", F32, tmm, 512, "in_proj_dx")
    G["w_in_p"] = _mm(n0, dproj, "tn", F32, 512, 384, "in_proj_dw")
    dh0, G["mix_norm_even"] = _rms_bwd(h0, W["mix_norm_even"], dn0, dh1, tm, "mix_norm_even_bwd")
    return loss, dh0, G, dict(zip(HOSTED, hparts)), dict(zip(HOSTED, hothers))


class _Cut:
    def __init__(self, full_shape, chip_dim, half_dim):
        self.full = tuple(full_shape)
        self.chip_dim, self.half_dim = chip_dim, half_dim
        self.chip_size = full_shape[chip_dim] // N_CHIPS
        self.half_size = full_shape[half_dim] // 2
        assert chip_dim != half_dim

    def shape(self, chip=False, half=False):
        s = list(self.full)
        if chip:
            s[self.chip_dim] = self.chip_size
        if half:
            s[self.half_dim] = self.half_size
        return tuple(s)

    def region(self, ref, chip=None, half=None):
        idx = [pl.ds(0, n) for n in ref.shape]
        if chip is not None:
            idx[self.chip_dim] = pl.ds(chip * self.chip_size, self.chip_size)
        if half is not None:
            idx[self.half_dim] = pl.ds(half * self.half_size, self.half_size)
        return ref.at[tuple(idx)]


SMALL_SHARDED = ("meta_tokens", "mix_norm_odd", "pool_b", "pool_scale", "conv_w", "ffn_conv_w")
SMALL_ROWS = 144


def _cuts():
    return {
        "w_in": _Cut((N_CHIPS, D_MODEL, IN_SHARD), 0, 1),
        "w_out": _Cut((D_MODEL, D_MODEL), 0, 1),
        "pool_w": _Cut((len(POOL_WINDOWS), POOL_G, POOL_G), 1, 0),
        "w_up": _Cut((2, D_MODEL, 2 * D_FF), 2, 1),
        "w_down": _Cut((2, D_FF, D_MODEL), 1, 2),
        "small": _Cut((N_CHIPS, SMALL_ROWS, LANE), 0, 1),
    }


COMM_ORDER = ("w_in", "w_out", "pool_w", "w_up", "w_down", "small")


def _remote(src, dst, send_sems, recv_sems, k, to):
    return pltpu.make_async_remote_copy(src_ref=src, dst_ref=dst, send_sem=send_sems.at[k],
                                        recv_sem=recv_sems.at[k], device_id=to, device_id_type=MESH)


def _gather_weights(shards, cuts):
    n = len(shards)

    def body(*refs):
        srcs, outs = refs[:n], refs[n:2 * n]
        send_sems, recv_sems = refs[2 * n:]
        x, y, c = _coords()
        me = 2 * x + y
        sibling = (x, y, 1 - c)
        chips = _other_chips(x, y)
        sends = []
        for t, cut in enumerate(cuts):
            push = _remote(srcs[t], cut.region(outs[t], chip=me), send_sems, recv_sems, 7 * t, sibling)
            push.start()
            sends.append(push)
            for kk, chip in enumerate(chips):
                cp = _remote(cut.region(srcs[t], half=c), cut.region(outs[t], chip=me, half=c),
                             send_sems, recv_sems, 7 * t + 1 + kk, (*chip, c))
                cp.start()
                sends.append(cp)
        for t, cut in enumerate(cuts):
            for kk, (px, py) in enumerate(chips):
                landed = cut.region(outs[t], chip=2 * px + py, half=c)
                _remote(landed, landed, send_sems, recv_sems, 7 * t + 1 + kk, sibling).wait_recv()
                fwd = _remote(landed, landed, send_sems, recv_sems, 7 * t + 4 + kk, sibling)
                fwd.start()
                sends.append(fwd)
        for t, cut in enumerate(cuts):
            mine = cut.region(outs[t], chip=me)
            _remote(mine, mine, send_sems, recv_sems, 7 * t, sibling).wait_recv()
            for kk, (px, py) in enumerate(chips):
                other = cut.region(outs[t], chip=2 * px + py, half=1 - c)
                _remote(other, other, send_sems, recv_sems, 7 * t + 4 + kk, sibling).wait_recv()
        for cp in sends:
            cp.wait_send()

    return pl.pallas_call(
        body, name="gather_weights",
        out_shape=tuple(jax.ShapeDtypeStruct(cut.full, s.dtype) for cut, s in zip(cuts, shards)),
        in_specs=[ANY] * n, out_specs=tuple([ANY] * n),
        scratch_shapes=[pltpu.SemaphoreType.DMA((7 * n,)), pltpu.SemaphoreType.DMA((7 * n,))],
    )(*shards)


def _gather_first_ops(srcs, outs, send_sems, recv_sems, cuts):
    x, y, c = _coords()
    me = 2 * x + y
    sibling = (x, y, 1 - c)
    chips = _other_chips(x, y)

    def copies():
        out = []
        for t, cut in enumerate(cuts):
            out.append(_remote(srcs[t], cut.region(outs[t], chip=me), send_sems, recv_sems, 4 * t, sibling))
            for kk, chip in enumerate(chips):
                out.append(_remote(cut.region(srcs[t], half=c), cut.region(outs[t], chip=me, half=c),
                                   send_sems, recv_sems, 4 * t + 1 + kk, (*chip, c)))
        return out

    def start():
        for cp in copies():
            cp.start()

    def wait():
        for t, cut in enumerate(cuts):
            mine = cut.region(outs[t], chip=me)
            _remote(mine, mine, send_sems, recv_sems, 4 * t, sibling).wait_recv()
            for kk, (px, py) in enumerate(chips):
                landed = cut.region(outs[t], chip=2 * px + py, half=c)
                _remote(landed, landed, send_sems, recv_sems, 4 * t + 1 + kk, sibling).wait_recv()
        for cp in copies():
            cp.wait_send()

    return start, wait


def _gather_forward(fulls, cuts):
    n = len(fulls)

    def body(*refs):
        outs = refs[n:2 * n]
        send_sems, recv_sems = refs[2 * n:]
        x, y, c = _coords()
        sibling = (x, y, 1 - c)
        chips = _other_chips(x, y)
        sends = []
        for t, cut in enumerate(cuts):
            for kk, (px, py) in enumerate(chips):
                landed = cut.region(outs[t], chip=2 * px + py, half=c)
                cp = _remote(landed, landed, send_sems, recv_sems, 3 * t + kk, sibling)
                cp.start()
                sends.append(cp)
        for t, cut in enumerate(cuts):
            for kk, (px, py) in enumerate(chips):
                other = cut.region(outs[t], chip=2 * px + py, half=1 - c)
                _remote(other, other, send_sems, recv_sems, 3 * t + kk, sibling).wait_recv()
        for cp in sends:
            cp.wait_send()

    return pl.pallas_call(
        body, name="gather_forward",
        out_shape=tuple(jax.ShapeDtypeStruct(f.shape, f.dtype) for f in fulls),
        in_specs=[ANY] * n, out_specs=tuple([ANY] * n), input_output_aliases={t: t for t in range(n)},
        scratch_shapes=[pltpu.SemaphoreType.DMA((3 * n,)), pltpu.SemaphoreType.DMA((3 * n,))],
    )(*fulls)


def _pair_exchange2(fulls, cuts, name):
    n = len(fulls)

    def body(*refs):
        srcs, outs = refs[:n], refs[n:2 * n]
        send_sems, recv_sems = refs[2 * n:]
        x, y, c = _coords()
        cps = [_remote(cut.region(srcs[t], half=1 - c), outs[t], send_sems, recv_sems, t, (x, y, 1 - c))
               for t, cut in enumerate(cuts)]
        for cp in cps:
            cp.start()
        for cp in cps:
            cp.wait()

    return pl.pallas_call(
        body, name=name,
        out_shape=tuple(jax.ShapeDtypeStruct(cut.shape(half=True), f.dtype) for cut, f in zip(cuts, fulls)),
        in_specs=[ANY] * n, out_specs=tuple([ANY] * n),
        scratch_shapes=[pltpu.SemaphoreType.DMA((n,)), pltpu.SemaphoreType.DMA((n,))],
    )(*fulls)


def _grid_of(shape, blk):
    assert all(s % b == 0 for s, b in zip(shape, blk)), (shape, blk)
    return tuple(s // b for s, b in zip(shape, blk))


def _pair_sum2(full, recv, cut, blk, name):
    hshape = cut.shape(half=True)
    grid = _grid_of(hshape, blk)
    hb = cut.half_size // blk[cut.half_dim]
    hd = cut.half_dim
    pos = jnp.stack([lax.axis_index("c")]).astype(jnp.int32)

    def full_idx(*a):
        ids, p = list(a[:-1]), a[-1]
        ids[hd] = ids[hd] + p[0] * hb
        return tuple(ids)

    def body(p_ref, f_ref, r_ref, o_ref):
        o_ref[...] = (f_ref[...] + r_ref[...]).astype(o_ref.dtype)

    return pl.pallas_call(
        body, name=name, out_shape=jax.ShapeDtypeStruct(hshape, BF16),
        grid_spec=pltpu.PrefetchScalarGridSpec(
            num_scalar_prefetch=1, grid=grid,
            in_specs=[pl.BlockSpec(blk, full_idx), pl.BlockSpec(blk, lambda *a: tuple(a[:-1]))],
            out_specs=pl.BlockSpec(blk, lambda *a: tuple(a[:-1]))),
        compiler_params=_params(("parallel",) * len(grid)))(pos, full, recv)


def _chip_exchange_ops(srcs, outs, send_sems, recv_sems, cuts):
    x, y, c = _coords()
    me = 2 * x + y
    chips = _other_chips(x, y)

    def copies():
        return [_remote(cut.region(srcs[t], chip=2 * px + py), outs[t].at[me], send_sems, recv_sems,
                        3 * t + kk, (px, py, c))
                for t, cut in enumerate(cuts) for kk, (px, py) in enumerate(chips)]

    def start():
        for cp in copies():
            cp.start()

    def wait():
        for t, cut in enumerate(cuts):
            for kk, (px, py) in enumerate(chips):
                slot = outs[t].at[2 * px + py]
                _remote(slot, slot, send_sems, recv_sems, 3 * t + kk, (px, py, c)).wait_recv()
        for cp in copies():
            cp.wait_send()

    return start, wait


def _chip_exchange_shapes(parts, cuts):
    return tuple(jax.ShapeDtypeStruct((N_CHIPS,) + cut.shape(chip=True, half=True), p.dtype)
                 for cut, p in zip(cuts, parts))


def _chip_exchange2(parts, cuts):
    n = len(parts)

    def body(*refs):
        start, wait = _chip_exchange_ops(refs[:n], refs[n:2 * n], refs[2 * n], refs[2 * n + 1], cuts)
        start()
        wait()

    return pl.pallas_call(
        body, name="grad_chip_exchange",
        out_shape=tuple(jax.ShapeDtypeStruct((N_CHIPS,) + cut.shape(chip=True, half=True), p.dtype)
                        for cut, p in zip(cuts, parts)),
        in_specs=[ANY] * n, out_specs=tuple([ANY] * n),
        scratch_shapes=[pltpu.SemaphoreType.DMA((3 * n,)), pltpu.SemaphoreType.DMA((3 * n,))],
    )(*parts)


def _chip_sum2(part, recv, cut, blk, name):
    bshape = cut.shape(chip=True, half=True)
    grid = _grid_of(bshape, blk)
    cb = cut.chip_size // blk[cut.chip_dim]
    hb = cut.half_size // blk[cut.half_dim]
    cd, hd = cut.chip_dim, cut.half_dim
    x, y, c = _coords()
    slots = [2 * px + py for px, py in _other_chips(x, y)]
    pos = jnp.stack([c, 2 * x + y] + slots).astype(jnp.int32)

    def part_idx(*a):
        ids, p = list(a[:-1]), a[-1]
        ids[cd] = ids[cd] + p[1] * cb
        return tuple(ids)

    def recv_idx(kk):
        return lambda *a: (a[-1][2 + kk],) + tuple(a[:-1])

    def out_idx(*a):
        ids, p = list(a[:-1]), a[-1]
        ids[hd] = ids[hd] + p[0] * hb
        return tuple(ids)

    def body(p_ref, own_ref, r0_ref, r1_ref, r2_ref, o_ref):
        acc = own_ref[...].astype(F32)
        for r_ref in (r0_ref, r1_ref, r2_ref):
            acc = acc + r_ref[...].astype(F32)
        o_ref[...] = acc

    return pl.pallas_call(
        body, name=name, out_shape=jax.ShapeDtypeStruct(cut.shape(chip=True), F32),
        grid_spec=pltpu.PrefetchScalarGridSpec(
            num_scalar_prefetch=1, grid=grid,
            in_specs=[pl.BlockSpec(blk, part_idx)] + [pl.BlockSpec((None,) + blk, recv_idx(kk)) for kk in range(3)],
            out_specs=pl.BlockSpec(blk, out_idx)),
        compiler_params=_params(("parallel",) * len(grid)))(pos, part, recv, recv, recv)


def _pair_swap2(blocks, cuts):
    n = len(blocks)

    def body(*refs):
        outs = refs[n:2 * n]
        send_sems, recv_sems = refs[2 * n:]
        x, y, c = _coords()
        cps = []
        for t, cut in enumerate(cuts):
            mine = cut.region(outs[t], half=c)
            cp = _remote(mine, mine, send_sems, recv_sems, t, (x, y, 1 - c))
            cp.start()
            cps.append(cp)
        for t, cut in enumerate(cuts):
            theirs = cut.region(outs[t], half=1 - c)
            _remote(theirs, theirs, send_sems, recv_sems, t, (x, y, 1 - c)).wait_recv()
        for cp in cps:
            cp.wait_send()

    return pl.pallas_call(
        body, name="grad_pair_swap",
        out_shape=tuple(jax.ShapeDtypeStruct(b.shape, b.dtype) for b in blocks),
        in_specs=[ANY] * n, out_specs=tuple([ANY] * n),
        input_output_aliases={t: t for t in range(n)},
        scratch_shapes=[pltpu.SemaphoreType.DMA((n,)), pltpu.SemaphoreType.DMA((n,))],
    )(*blocks)


PAIR_SUM_BLOCKS = {"w_in": (1, 512, IN_SHARD), "w_out": (512, 512), "pool_w": (1, POOL_G, POOL_G),
                   "w_up": (1, 64, 2 * D_FF), "w_down": (1, 704, 512), "small": (N_CHIPS, SMALL_ROWS // 2, LANE)}
CHIP_SUM_BLOCKS = {"w_in": (1, 512, IN_SHARD), "w_out": (256, 512), "pool_w": (2, 64, POOL_G),
                   "w_up": (1, 128, UP_SHARD), "w_down": (1, DOWN_SHARD, 512), "small": (1, SMALL_ROWS // 2, LANE)}


def _pack_small(P):
    parts = []
    for name in SMALL_SHARDED:
        t = P[name]
        parts.append(t.astype(F32))
    return parts


def _small_rows(t, lead):
    flat = t.reshape(lead + (-1,))
    pad = -flat.shape[-1] % LANE
    return jnp.pad(flat, [(0, 0)] * len(lead) + [(0, pad)]).reshape(lead + (-1, LANE))


def _pack_small_shards(shards):
    rows = jnp.concatenate([_small_rows(shards[n].astype(F32), ()) for n in SMALL_SHARDED], axis=0)
    return jnp.pad(rows, ((0, SMALL_ROWS - rows.shape[0]), (0, 0)))[None]


def _unpack_small(pack, shards, axes):
    out, off = {}, 0
    nchip = pack.shape[0]
    for name in SMALL_SHARDED:
        shp = shards[name].shape
        cnt = int(np.prod(shp))
        rows = -(-cnt // LANE)
        t = pack[:, off:off + rows].reshape(nchip, -1)[:, :cnt].reshape((nchip,) + shp)
        out[name] = jnp.concatenate([t[j] for j in range(nchip)], axis=axes[name])
        off += rows
    return out


def _pack_small_grads(grads, shards, axes):
    parts = []
    for name in SMALL_SHARDED:
        shp, ax = shards[name].shape, axes[name]
        g = grads[name].reshape(shp[:ax] + (N_CHIPS, shp[ax]) + shp[ax + 1:])
        parts.append(_small_rows(jnp.moveaxis(g, ax, 0), (N_CHIPS,)))
    rows = jnp.concatenate(parts, axis=1)
    return jnp.pad(rows, ((0, 0), (0, SMALL_ROWS - rows.shape[1]), (0, 0)))


SMALL_AXES = {"meta_tokens": 1, "mix_norm_odd": 1, "pool_b": 2, "pool_scale": 1, "conv_w": 2, "ffn_conv_w": 2}


WEIGHT_NAMES = ("meta_tokens", "mix_norm_even", "w_in", "b_f", "conv_w", "conv_b", "ln_g", "ln_b", "w_out",
                "mix_norm_odd", "pool_w", "pool_b", "pool_scale", "ffn_norm", "w_up", "ffn_conv_w",
                "ffn_conv_b", "w_down", "final_norm")


def kernel(x, meta_tokens, mix_norm_even, w_in, b_f, conv_w, conv_b, ln_g, ln_b, w_out, mix_norm_odd, pool_w, pool_b, pool_scale, ffn_norm, w_up, ffn_conv_w, ffn_conv_b, w_down, final_norm, loss_target, m_meta_tokens, m_mix_norm_even, m_w_in, m_b_f, m_conv_w, m_conv_b, m_ln_g, m_ln_b, m_w_out, m_mix_norm_odd, m_pool_w, m_pool_b, m_pool_scale, m_ffn_norm, m_w_up, m_ffn_conv_w, m_ffn_conv_b, m_w_down, m_final_norm, v_meta_tokens, v_mix_norm_even, v_w_in, v_b_f, v_conv_w, v_conv_b, v_ln_g, v_ln_b, v_w_out, v_mix_norm_odd, v_pool_w, v_pool_b, v_pool_scale, v_ffn_norm, v_w_up, v_ffn_conv_w, v_ffn_conv_b, v_w_down, v_final_norm):
    given = dict(locals())
    w_loc = {n: given[n] for n in WEIGHT_NAMES}
    m_loc = {n: given["m_" + n] for n in WEIGHT_NAMES}
    v_loc = {n: given["v_" + n] for n in WEIGHT_NAMES}
    cut_of = _cuts()
    cuts = [cut_of[n] for n in COMM_ORDER]
    big = ("w_in", "w_out", "pool_w", "w_up", "w_down")
    small_shards = {n: w_loc[n] for n in SMALL_SHARDED}

    shard_of = {n: w_loc[n].astype(BF16).reshape(cut_of[n].shape(chip=True)) for n in big}
    shard_of["small"] = _pack_small_shards(small_shards)
    g_in, g_out, g_small = _gather_weights([shard_of[n] for n in GATHER_FIRST], [cut_of[n] for n in GATHER_FIRST])
    g_pool = g_up = g_down = None
    full = _unpack_small(g_small, small_shards, SMALL_AXES)
    full.update({n: w_loc[n] for n in REPLICATED})
    w_in_full = g_in.transpose(1, 0, 2).reshape(D_MODEL, IN_COLS)
    qkv, f, ag = (w_in_full[:, :3 * FOX_W], w_in_full[:, 3 * FOX_W:3 * FOX_W + HEADS],
                  w_in_full[:, 3 * FOX_W + HEADS:])
    W = dict(
        mix_norm_even=full["mix_norm_even"].reshape(1, D_MODEL),
        w_in_p=jnp.concatenate([qkv, ag, f, jnp.zeros((D_MODEL, LANE - HEADS), BF16)], axis=1),
        b_f_p=jnp.pad(full["b_f"].reshape(1, HEADS), ((0, 0), (0, LANE - HEADS))),
        conv_w_p=jnp.pad(full["conv_w"].reshape(CONV_K, CONV_CH), ((0, CONV_HALO - CONV_K), (0, 0))),
        conv_b=full["conv_b"].reshape(1, CONV_CH), ln_g=full["ln_g"].reshape(1, CONV_CH),
        ln_b=full["ln_b"].reshape(1, CONV_CH), w_out=g_out,
        mix_norm_odd=full["mix_norm_odd"].reshape(1, D_MODEL), pool_w=g_pool,
        pool_b=full["pool_b"].reshape(1, D_MODEL), pool_scale=full["pool_scale"].reshape(1, D_MODEL),
        ffn_norm=full["ffn_norm"], w_up=g_up,
        ffn_conv_w_p=jnp.pad(full["ffn_conv_w"], ((0, 0), (0, 8 - FFN_K), (0, 0))),
        ffn_conv_b=full["ffn_conv_b"], w_down=g_down, final_norm=full["final_norm"].reshape(1, D_MODEL),
        late_shards=[shard_of[n] for n in GATHER_LATE])

    seq = x.shape[1]
    n_real = N_META + seq
    LP = -(-n_real // ATT_BLK) * ATT_BLK
    tail = jnp.zeros((LP - n_real, D_MODEL), F32)
    h0 = jnp.concatenate([full["meta_tokens"], x[0], tail], axis=0)
    tgt = jnp.concatenate([jnp.zeros((N_META, D_MODEL), F32), loss_target[0], tail], axis=0)
    loss_loc, dh0, G, parts, others = _local_step2(h0, tgt, W, n_real, cut_of)
    grad_x = dh0[N_META:n_real][None]
    G["meta_tokens"] = dh0[:N_META]

    rep_shapes = {n: w_loc[n].shape for n in REPLICATED}
    G["final_norm"] = G["final_norm"].reshape(D_MODEL)
    rep, loss = _unpack_replicated(_allreduce_small(_pack_replicated(G, loss_loc)), rep_shapes)

    gp = G["w_in_p"]
    g_w_in = jnp.concatenate([gp[:, :3 * FOX_W], gp[:, 3 * FOX_W + 2 * CONV_CH:3 * FOX_W + 2 * CONV_CH + HEADS],
                              gp[:, 3 * FOX_W:3 * FOX_W + 2 * CONV_CH]], axis=1)
    lcuts = [cut_of[n] for n in LATE]
    lfull = [g_w_in.reshape(D_MODEL, N_CHIPS, IN_SHARD).transpose(1, 0, 2),
             _pack_small_grads(G, small_shards, SMALL_AXES)]
    lrecv = _pair_exchange2(lfull, lcuts, "grad_pair_exchange_late")
    lparts = [_pair_sum2(f, r, cut, PAIR_SUM_BLOCKS[n], "grad_pair_sum_" + n)
              for f, r, cut, n in zip(lfull, lrecv, lcuts, LATE)]
    parts.update(zip(LATE, lparts))
    others.update(zip(LATE, _chip_exchange2(lparts, lcuts)))
    blocks = [_chip_sum2(parts[n], others[n], cut_of[n], CHIP_SUM_BLOCKS[n], "grad_chip_sum_" + n)
              for n in COMM_ORDER]
    blocks = _pair_swap2(blocks, cuts)
    gsh = {n: b.reshape(w_loc[n].shape) for n, b in zip(big, blocks[:5])}
    gsh.update(_unpack_small(blocks[5], small_shards, SMALL_AXES))
    sharded = set(big) | set(SMALL_SHARDED)

    grad_w = {n: (gsh[n] if n in sharded else rep[n]) for n in WEIGHT_NAMES}
    delta, new_m, new_v = {}, {}, {}
    for n in WEIGHT_NAMES:
        delta[n], new_m[n], new_v[n] = _adamw(w_loc[n], grad_w[n], m_loc[n], v_loc[n], "adamw_" + n)
    return (loss, grad_x, *[grad_w[n] for n in WEIGHT_NAMES], *[delta[n] for n in WEIGHT_NAMES],
            *[new_m[n] for n in WEIGHT_NAMES], *[new_v[n] for n in WEIGHT_NAMES])
```

```python
import functools

import numpy as np
import jax
import jax.numpy as jnp
from jax import lax
from jax.experimental import pallas as pl
from jax.experimental.pallas import tpu as pltpu

F32 = jnp.float32
BF16 = jnp.bfloat16

D_MODEL = 1024
N_META = 16
SEQ = 2048
HEADS = 8
HEAD_DIM = 64
FOX_W = HEADS * HEAD_DIM
CONV_CH = 512
CONV_K = 31
D_FF = 2816
POOL_WINDOWS = (2, 4, 8, 16)
POOL_G = 256
RMS_EPS = 1e-6
LN_EPS = 1e-5
IN_COLS = 3 * FOX_W + HEADS + 2 * CONV_CH
IN_COLS_P = 3 * FOX_W + 2 * CONV_CH + 128
F_COL_BLK = (3 * FOX_W + 2 * CONV_CH) // 128
N_CHIPS = 4
IN_SHARD = IN_COLS // N_CHIPS
UP_SHARD = 2 * D_FF // N_CHIPS
DOWN_SHARD = D_FF // N_CHIPS

ADAM_LR = 0.001
ADAM_B1 = 0.9
ADAM_B2 = 0.999
ADAM_EPS = 1e-08
ADAM_WD = 0.01
ADAM_STEP = 10

LANE = 128
ATT_BLK = 128
VMEM_LIMIT = 48 * 1024 * 1024

NEG = -1e30


def _sigmoid(x):
    return 0.5 * jnp.tanh(0.5 * x) + 0.5


def _sigmoid_tail(x):
    return 1.0 / (1.0 + jnp.exp(-x))


def _params(sem=None):
    return pltpu.CompilerParams(dimension_semantics=sem, vmem_limit_bytes=VMEM_LIMIT)


def _sub_rows(tm):
    best = 8
    for s in range(8, 137, 8):
        if tm % s == 0:
            best = s
    return best


def _mm(a, b, mode, out_dtype, tm, tn, name, add=None, a_lead=None, b_lead=None, out=None):
    a_shape = a.shape if a_lead is None else a.shape[1:]
    b_shape = b.shape if b_lead is None else b.shape[1:]
    if mode == "nn":
        (M, K), (K2, N) = a_shape, b_shape
        dims = (((1,), (0,)), ((), ()))
        a_blk, a_idx = (tm, K), (lambda i, j: (i, 0))
        b_blk, b_idx = (K, tn), (lambda i, j: (0, j))
    elif mode == "nt":
        (M, K), (N, K2) = a_shape, b_shape
        dims = (((1,), (1,)), ((), ()))
        a_blk, a_idx = (tm, K), (lambda i, j: (i, 0))
        b_blk, b_idx = (tn, K), (lambda i, j: (j, 0))
    else:
        (K, M), (K2, N) = a_shape, b_shape
        dims = (((0,), (0,)), ((), ()))
        a_blk, a_idx = (K, tm), (lambda i, j: (0, i))
        b_blk, b_idx = (K, tn), (lambda i, j: (0, j))
    assert K == K2 and M % tm == 0 and N % tn == 0, (name, a.shape, b.shape, tm, tn)
    gm, gn = M // tm, N // tn
    a_bytes = M * K * a.dtype.itemsize
    b_bytes = N * K * b.dtype.itemsize
    m_outer = a_bytes + b_bytes * gm <= b_bytes + a_bytes * gn
    if m_outer:
        grid = (gm, gn)
        wrap = lambda f: f
    else:
        grid = (gn, gm)
        wrap = lambda f: (lambda j, i: f(i, j))

    def lead(blk, idx, at):
        if at is None:
            return pl.BlockSpec(blk, wrap(idx))
        return pl.BlockSpec((None,) + blk, wrap(lambda i, j: (at,) + idx(i, j)))

    o_idx = lambda i, j: (i, j)
    in_specs = [lead(a_blk, a_idx, a_lead), lead(b_blk, b_idx, b_lead)]
    args = [a, b]
    if add is not None:
        in_specs.append(pl.BlockSpec((tm, tn), wrap(o_idx)))
        args.append(add)
    aliases = {}
    if out is None:
        out_shape = jax.ShapeDtypeStruct((M, N), out_dtype)
        out_spec = pl.BlockSpec((tm, tn), wrap(o_idx))
    else:
        o_lead, n_lead, into = out
        out_shape = jax.ShapeDtypeStruct((n_lead, M, N), out_dtype)
        out_spec = lead((tm, tn), o_idx, o_lead)
        if into is not None:
            aliases = {len(args): 0}
            in_specs.append(pl.BlockSpec(memory_space=pl.ANY))
            args.append(into)
    has_add = add is not None

    def body(a_ref, b_ref, *rest):
        o_ref = rest[-1]
        x = a_ref[...].astype(BF16)
        y = b_ref[...].astype(BF16)
        acc = lax.dot_general(x, y, dims, preferred_element_type=F32)
        if has_add:
            acc = acc + rest[0][...]
        o_ref[...] = acc.astype(o_ref.dtype)

    return pl.pallas_call(
        body, name=name, out_shape=out_shape, grid=grid, in_specs=in_specs, out_specs=out_spec,
        input_output_aliases=aliases, compiler_params=_params(("parallel", "parallel")))(*args)


def _mm_ffn_dn(dup, w_up, layer, tm, tn, name):
    _, LP, F = dup.shape
    Dm = w_up.shape[1]
    nt = (((1,), (1,)), ((), ()))

    def body(a_ref, b_ref, o_ref):
        acc = lax.dot_general(a_ref[0], b_ref[:, 0:F], nt, preferred_element_type=F32)
        acc = acc + lax.dot_general(a_ref[1], b_ref[:, F:2 * F], nt, preferred_element_type=F32)
        o_ref[...] = acc

    return pl.pallas_call(
        body, name=name, out_shape=jax.ShapeDtypeStruct((LP, Dm), F32), grid=(LP // tm, Dm // tn),
        in_specs=[pl.BlockSpec((2, tm, F), lambda i, j: (0, i, 0)),
                  pl.BlockSpec((None, tn, 2 * F), lambda i, j: (layer, j, 0))],
        out_specs=pl.BlockSpec((tm, tn), lambda i, j: (i, j)),
        compiler_params=_params(("parallel", "parallel")))(dup, w_up)


def _mm_ffn_dwup(n, dup, layer, into, tk, tn, name):
    LP, Dm = n.shape
    F = dup.shape[2]
    nct = F // tn
    tdims = (((0,), (0,)), ((), ()))

    def body(a_ref, b_ref, *rest):
        rest[-1][...] = lax.dot_general(a_ref[...], b_ref[...], tdims, preferred_element_type=F32)

    in_specs = [pl.BlockSpec((LP, tk), lambda i, j: (0, i)),
                pl.BlockSpec((None, LP, tn), lambda i, j: (j // nct, 0, j % nct))]
    args = [n, dup]
    aliases = {}
    if into is not None:
        in_specs.append(pl.BlockSpec(memory_space=pl.ANY))
        args.append(into)
        aliases = {2: 0}
    return pl.pallas_call(
        body, name=name, out_shape=jax.ShapeDtypeStruct((2, Dm, 2 * F), F32), grid=(Dm // tk, 2 * nct),
        in_specs=in_specs, out_specs=pl.BlockSpec((None, tk, tn), lambda i, j: (layer, i, j)),
        input_output_aliases=aliases, compiler_params=_params(("parallel", "parallel")))(*args)


def _rms_fwd(h, g, out_dtype, tm, name):
    LP, Dm = h.shape

    def body(h_ref, g_ref, o_ref):
        x = h_ref[...]
        r = lax.rsqrt(jnp.mean(x * x, axis=1, keepdims=True) + RMS_EPS)
        o_ref[...] = (x * r * g_ref[...]).astype(o_ref.dtype)

    return pl.pallas_call(
        body, name=name, out_shape=jax.ShapeDtypeStruct((LP, Dm), out_dtype), grid=(LP // tm,),
        in_specs=[pl.BlockSpec((tm, Dm), lambda i: (i, 0)), pl.BlockSpec((1, Dm), lambda i: (0, 0))],
        out_specs=pl.BlockSpec((tm, Dm), lambda i: (i, 0)),
        compiler_params=_params(("parallel",)))(h, g)


def _rms_bwd(h, g, dn, dres, tm, name):
    LP, Dm = h.shape

    def body(h_ref, g_ref, dn_ref, dr_ref, dh_ref, dg_ref):
        i = pl.program_id(0)
        x = h_ref[...]
        r = lax.rsqrt(jnp.mean(x * x, axis=1, keepdims=True) + RMS_EPS)
        xhat = x * r
        dy = dn_ref[...]
        dxh = dy * g_ref[...]
        dh = r * (dxh - xhat * jnp.mean(dxh * xhat, axis=1, keepdims=True))
        dh_ref[...] = dr_ref[...] + dh

        @pl.when(i == 0)
        def _():
            dg_ref[...] = jnp.zeros_like(dg_ref)

        dg_ref[...] += jnp.sum(dy * xhat, axis=0, keepdims=True)

    row = pl.BlockSpec((tm, Dm), lambda i: (i, 0))
    vec = pl.BlockSpec((1, Dm), lambda i: (0, 0))
    return pl.pallas_call(
        body, name=name,
        out_shape=(jax.ShapeDtypeStruct((LP, Dm), F32), jax.ShapeDtypeStruct((1, Dm), F32)),
        grid=(LP // tm,), in_specs=[row, vec, row, row], out_specs=(row, vec),
        compiler_params=_params(("arbitrary",)))(h, g, dn, dres)


def _loss_head(h, g, tgt, n_real, tm, name):
    LP, Dm = h.shape

    def body(h_ref, g_ref, t_ref, loss_ref, dh_ref, dg_ref):
        i = pl.program_id(0)
        x = h_ref[...]
        gg = g_ref[...]
        r = lax.rsqrt(jnp.mean(x * x, axis=1, keepdims=True) + RMS_EPS)
        xhat = x * r
        rows = i * tm + lax.broadcasted_iota(jnp.int32, (tm, 1), 0)
        real = jnp.logical_and(rows >= N_META, rows < n_real)
        diff = jnp.where(real, xhat * gg - t_ref[...], 0.0)
        dy = diff * (1.0 / Dm)
        dxh = dy * gg
        dh_ref[...] = r * (dxh - xhat * jnp.mean(dxh * xhat, axis=1, keepdims=True))

        @pl.when(i == 0)
        def _():
            dg_ref[...] = jnp.zeros_like(dg_ref)
            loss_ref[...] = jnp.zeros_like(loss_ref)

        dg_ref[...] += jnp.sum(dy * xhat, axis=0, keepdims=True)
        part = jnp.sum(jnp.sum(diff * diff, axis=1, keepdims=True), axis=0, keepdims=True)
        loss_ref[...] += jnp.broadcast_to(part * (0.5 / Dm), loss_ref.shape)

    row = pl.BlockSpec((tm, Dm), lambda i: (i, 0))
    vec = pl.BlockSpec((1, Dm), lambda i: (0, 0))
    return pl.pallas_call(
        body, name=name,
        out_shape=(jax.ShapeDtypeStruct((1, LANE), F32), jax.ShapeDtypeStruct((LP, Dm), F32),
                   jax.ShapeDtypeStruct((1, Dm), F32)),
        grid=(LP // tm,), in_specs=[row, vec, row],
        out_specs=(pl.BlockSpec((1, LANE), lambda i: (0, 0)), row, vec),
        compiler_params=_params(("arbitrary",)))(h, g, tgt)


def _fgate_fwd(proj, bf_p, name):
    LP = proj.shape[0]
    nb = LP // LANE

    def body(f_ref, b_ref, c_ref, lf_ref):
        x = f_ref[...] + b_ref[...]
        lf_ref[...] = jnp.minimum(x, 0.0) - jnp.log1p(jnp.exp(-jnp.abs(x)))
        ri = lax.broadcasted_iota(jnp.int32, (LANE, LANE), 0)
        ci = lax.broadcasted_iota(jnp.int32, (LANE, LANE), 1)
        tri = jnp.where(ri >= ci, 1.0, 0.0).astype(F32)

        def blk(i, carry):
            rows = pl.ds(pl.multiple_of(i * LANE, LANE), LANE)
            cb = jnp.dot(tri, lf_ref[rows, :], precision=lax.Precision.HIGHEST,
                         preferred_element_type=F32) + carry
            c_ref[rows, :] = cb
            return cb[LANE - 1:LANE, :]

        lax.fori_loop(0, nb, blk, jnp.zeros((1, LANE), F32))

    return pl.pallas_call(
        body, name=name, out_shape=jax.ShapeDtypeStruct((LP, LANE), F32), grid=(1,),
        in_specs=[pl.BlockSpec((LP, LANE), lambda i: (0, F_COL_BLK)),
                  pl.BlockSpec((1, LANE), lambda i: (0, 0))],
        out_specs=pl.BlockSpec((LP, LANE), lambda i: (0, 0)),
        scratch_shapes=[pltpu.VMEM((LP, LANE), F32)],
        compiler_params=_params(("arbitrary",)))(proj, bf_p)


def _fgate_bwd(proj, bf_p, dc, name):
    LP = proj.shape[0]
    nb = LP // LANE

    def body(f_ref, b_ref, dc_ref, dl_ref, db_ref):
        ri = lax.broadcasted_iota(jnp.int32, (LANE, LANE), 0)
        ci = lax.broadcasted_iota(jnp.int32, (LANE, LANE), 1)
        triu = jnp.where(ri <= ci, 1.0, 0.0).astype(F32)
        bb = b_ref[...]

        tail = jnp.zeros((1, LANE), F32)
        dbs = jnp.zeros((1, LANE), F32)
        for i in range(nb - 1, -1, -1):
            rows = slice(i * LANE, (i + 1) * LANE)
            gb = jnp.dot(triu, dc_ref[rows, :], precision=lax.Precision.HIGHEST,
                         preferred_element_type=F32) + tail
            x = f_ref[rows, :] + bb
            dl = gb * _sigmoid_tail(-x)
            dl_ref[rows, :] = dl.astype(dl_ref.dtype)
            tail = gb[0:1, :]
            dbs = dbs + jnp.sum(dl, axis=0, keepdims=True)
        db_ref[...] = dbs

    return pl.pallas_call(
        body, name=name,
        out_shape=(jax.ShapeDtypeStruct((LP, LANE), BF16), jax.ShapeDtypeStruct((1, LANE), F32)),
        grid=(1,),
        in_specs=[pl.BlockSpec((LP, LANE), lambda i: (0, F_COL_BLK)),
                  pl.BlockSpec((1, LANE), lambda i: (0, 0)),
                  pl.BlockSpec((LP, LANE), lambda i: (0, 0))],
        out_specs=(pl.BlockSpec((LP, LANE), lambda i: (0, 0)), pl.BlockSpec((1, LANE), lambda i: (0, 0))),
        compiler_params=_params(("arbitrary",)))(proj, bf_p, dc)


def _attn_fwd(q, k, v, c_col, c_row, name):
    Hh, LP, Dh = q.shape
    nb = LP // ATT_BLK
    scale = Dh ** -0.5
    nt = (((1,), (1,)), ((), ()))

    def body(q_ref, k_ref, v_ref, cc_ref, cr_ref, o_ref, lse_ref):
        i = pl.program_id(1)
        qb = q_ref[...]
        cq = cc_ref[...]
        rows = i * ATT_BLK + lax.broadcasted_iota(jnp.int32, (ATT_BLK, ATT_BLK), 0)
        cols0 = lax.broadcasted_iota(jnp.int32, (ATT_BLK, ATT_BLK), 1)

        def step(j, carry):
            m, l, acc = carry
            ks = pl.ds(pl.multiple_of(j * ATT_BLK, ATT_BLK), ATT_BLK)
            s = lax.dot_general(qb, k_ref[ks, :], nt, preferred_element_type=F32) * scale
            s = s + cq - cr_ref[j]
            s = jnp.where(cols0 + j * ATT_BLK <= rows, s, NEG)
            m_new = jnp.maximum(m, jnp.max(s, axis=1, keepdims=True))
            p = jnp.exp(s - m_new)
            alpha = jnp.exp(m - m_new)
            l = alpha * l + jnp.sum(p, axis=1, keepdims=True)
            acc = alpha * acc + jnp.dot(p.astype(BF16), v_ref[ks, :], preferred_element_type=F32)
            return m_new, l, acc

        init = (jnp.full((ATT_BLK, 1), NEG, F32), jnp.zeros((ATT_BLK, 1), F32),
                jnp.zeros((ATT_BLK, Dh), F32))
        m, l, acc = lax.fori_loop(0, i + 1, step, init)
        o_ref[...] = acc / l
        lse_ref[...] = m + jnp.log(l)

    qspec = pl.BlockSpec((None, ATT_BLK, Dh), lambda h, i: (h, i, 0))
    kspec = pl.BlockSpec((None, LP, Dh), lambda h, i: (h, 0, 0))
    colspec = pl.BlockSpec((None, ATT_BLK, 1), lambda h, i: (h, i, 0))
    rowspec = pl.BlockSpec((None, nb, 1, ATT_BLK), lambda h, i: (h, 0, 0, 0))
    return pl.pallas_call(
        body, name=name,
        out_shape=(jax.ShapeDtypeStruct((Hh, LP, Dh), F32), jax.ShapeDtypeStruct((Hh, LP, 1), F32)),
        grid=(Hh, nb), in_specs=[qspec, kspec, kspec, colspec, rowspec],
        out_specs=(qspec, colspec),
        compiler_params=_params(("parallel", "arbitrary")))(q, k, v, c_col, c_row)


def _attn_bwd(q, k, v, o, do, lse, c_col, c_row, name):
    Hh, LP, Dh = q.shape
    nb = LP // ATT_BLK
    scale = Dh ** -0.5
    nt = (((1,), (1,)), ((), ()))
    tn = (((0,), (0,)), ((), ()))

    def body(q_ref, k_ref, v_ref, o_ref, do_ref, lse_ref, cc_ref, cr_ref,
             dq_ref, dk_ref, dv_ref, dcq_ref, dc_ref, delta_ref):
        j = pl.program_id(1)

        @pl.when(j == 0)
        def _():
            dq_ref[...] = jnp.zeros_like(dq_ref)
            dcq_ref[...] = jnp.zeros_like(dcq_ref)
            dob_all = do_ref[...].astype(BF16).astype(F32)
            delta_ref[...] = jnp.sum(dob_all * o_ref[...], axis=1, keepdims=True)

        kb = k_ref[...]
        vb = v_ref[...]
        ck = cr_ref[j]
        rows0 = lax.broadcasted_iota(jnp.int32, (ATT_BLK, ATT_BLK), 0)
        cols = j * ATT_BLK + lax.broadcasted_iota(jnp.int32, (ATT_BLK, ATT_BLK), 1)

        def step(i, carry):
            dk, dv, dcs = carry
            qs = pl.ds(pl.multiple_of(i * ATT_BLK, ATT_BLK), ATT_BLK)
            qb = q_ref[qs, :]
            dob = do_ref[qs, :].astype(BF16)
            s = lax.dot_general(qb, kb, nt, preferred_element_type=F32) * scale
            s = s + cc_ref[qs, :] - ck
            s = jnp.where(cols <= rows0 + i * ATT_BLK, s, NEG)
            p = jnp.exp(s - lse_ref[qs, :])
            dp = lax.dot_general(dob, vb, nt, preferred_element_type=F32)
            ds = p * (dp - delta_ref[qs, :])
            dsb = ds.astype(BF16)
            dv = dv + lax.dot_general(p.astype(BF16), dob, tn, preferred_element_type=F32)
            dk = dk + lax.dot_general(dsb, qb, tn, preferred_element_type=F32) * scale
            dq_ref[qs, :] += jnp.dot(dsb, kb, preferred_element_type=F32) * scale
            dcq_ref[qs, :] += jnp.sum(ds, axis=1, keepdims=True)
            dcs = dcs - jnp.sum(ds, axis=0, keepdims=True)
            return dk, dv, dcs

        init = (jnp.zeros((ATT_BLK, Dh), F32), jnp.zeros((ATT_BLK, Dh), F32),
                jnp.zeros((1, ATT_BLK), F32))
        dk, dv, dcs = lax.fori_loop(j, nb, step, init)
        dk_ref[...] = dk
        dv_ref[...] = dv
        dc_ref[...] = dcs

    full = pl.BlockSpec((None, LP, Dh), lambda h, j: (h, 0, 0))
    blk = pl.BlockSpec((None, ATT_BLK, Dh), lambda h, j: (h, j, 0))
    col = pl.BlockSpec((None, LP, 1), lambda h, j: (h, 0, 0))
    rowspec = pl.BlockSpec((None, nb, 1, ATT_BLK), lambda h, j: (h, 0, 0, 0))
    return pl.pallas_call(
        body, name=name,
        out_shape=(jax.ShapeDtypeStruct((Hh, LP, Dh), F32), jax.ShapeDtypeStruct((Hh, LP, Dh), F32),
                   jax.ShapeDtypeStruct((Hh, LP, Dh), F32), jax.ShapeDtypeStruct((Hh, LP, 1), F32),
                   jax.ShapeDtypeStruct((Hh, nb, 1, ATT_BLK), F32)),
        grid=(Hh, nb), in_specs=[full, blk, blk, full, full, col, col, rowspec],
        out_specs=(full, blk, blk, col, pl.BlockSpec((None, None, 1, ATT_BLK), lambda h, j: (h, j, 0, 0))),
        scratch_shapes=[pltpu.VMEM((LP, 1), F32)],
        compiler_params=_params(("parallel", "arbitrary")))(q, k, v, o, do, lse, c_col, c_row)


AUG = 128
ONES_IN_K = HEAD_DIM
ONES_IN_Q = HEAD_DIM + 3
ATT_HEADS_PER_STEP = 8
ATT_HEADS_PER_STEP_BWD = 8


def _attn_prep(proj, c, name):
    LP = proj.shape[0]
    nb = LP // ATT_BLK
    tail_rows = AUG - HEAD_DIM

    def body(q_ref, k_ref, v_ref, c_ref, qT_ref, kT_ref, ka_ref, vT_ref):
        qt = (q_ref[...] * (HEAD_DIM ** -0.5)).T
        kt = k_ref[...].T
        vt = v_ref[...].T
        ct = c_ref[...].T
        hi = ct.astype(BF16).astype(F32)
        r1 = ct - hi
        mid = r1.astype(BF16).astype(F32)
        lo = (r1 - mid).astype(BF16).astype(F32)
        row = lax.broadcasted_iota(jnp.int32, (tail_rows, ATT_BLK), 0)
        ones = jnp.where(row < 3, 1.0, 0.0)
        for h in range(HEADS):
            cparts = jnp.where(row == 0, hi[h:h + 1], jnp.where(row == 1, mid[h:h + 1],
                               jnp.where(row == 2, lo[h:h + 1], 0.0)))
            hs = slice(h * HEAD_DIM, (h + 1) * HEAD_DIM)
            q_tail = cparts + pltpu.roll(ones, 3, 0)
            k_tail = ones - pltpu.roll(cparts, 3, 0)
            qT_ref[h] = jnp.concatenate([qt[hs], q_tail], axis=0).astype(BF16)
            kfull = jnp.concatenate([kt[hs], k_tail], axis=0)
            kT_ref[h] = kfull.astype(BF16)
            ka_ref[h] = kfull.T.astype(BF16)
            vT_ref[h] = vt[hs].astype(BF16)

    col = lambda j: pl.BlockSpec((ATT_BLK, FOX_W), lambda i: (i, j))
    blk = lambda r: pl.BlockSpec((HEADS, None, r, ATT_BLK), lambda i: (0, i, 0, 0))
    return pl.pallas_call(
        body, name=name,
        out_shape=(jax.ShapeDtypeStruct((HEADS, nb, AUG, ATT_BLK), BF16),
                   jax.ShapeDtypeStruct((HEADS, nb, AUG, ATT_BLK), BF16),
                   jax.ShapeDtypeStruct((HEADS, LP, AUG), BF16),
                   jax.ShapeDtypeStruct((HEADS, nb, HEAD_DIM, ATT_BLK), BF16)),
        grid=(nb,), in_specs=[col(0), col(1), col(2), pl.BlockSpec((ATT_BLK, LANE), lambda i: (i, 0))],
        out_specs=(blk(AUG), blk(AUG), pl.BlockSpec((HEADS, ATT_BLK, AUG), lambda i: (0, i, 0)), blk(HEAD_DIM)),
        compiler_params=_params(("parallel",)))(proj, proj, proj, c)


def _attn_rows(xT, scale, out_dtype, name):
    Hh, nb, R, _ = xT.shape

    def body(x_ref, o_ref):
        stack = jnp.concatenate([x_ref[h, 0:HEAD_DIM, :] for h in range(Hh)], axis=0)
        o_ref[...] = (stack * scale).T.astype(o_ref.dtype)

    return pl.pallas_call(
        body, name=name, out_shape=jax.ShapeDtypeStruct((nb * ATT_BLK, Hh * HEAD_DIM), out_dtype), grid=(nb,),
        in_specs=[pl.BlockSpec((Hh, None, R, ATT_BLK), lambda i: (0, i, 0, 0))],
        out_specs=pl.BlockSpec((ATT_BLK, Hh * HEAD_DIM), lambda i: (i, 0)),
        compiler_params=_params(("parallel",)))(xT)


def _attn_cols(x, name):
    LP = x.shape[0]
    nb = LP // ATT_BLK

    def body(x_ref, o_ref):
        xt = x_ref[...].T
        for h in range(HEADS):
            o_ref[h] = xt[h * HEAD_DIM:(h + 1) * HEAD_DIM].astype(o_ref.dtype)

    return pl.pallas_call(
        body, name=name, out_shape=jax.ShapeDtypeStruct((HEADS, nb, HEAD_DIM, ATT_BLK), BF16), grid=(nb,),
        in_specs=[pl.BlockSpec((ATT_BLK, FOX_W), lambda i: (i, 0))],
        out_specs=pl.BlockSpec((HEADS, None, HEAD_DIM, ATT_BLK), lambda i: (0, i, 0, 0)),
        compiler_params=_params(("parallel",)))(x)


def _attn_dc(dqT, dkT, name):
    Hh, nb, _, _ = dqT.shape

    def body(q_ref, k_ref, o_ref):
        row = lax.broadcasted_iota(jnp.int32, (LANE, ATT_BLK), 0)
        acc = jnp.zeros((LANE, ATT_BLK), F32)
        for h in range(Hh):
            d = q_ref[h, ONES_IN_K:ONES_IN_K + 1, :] - k_ref[h, ONES_IN_Q:ONES_IN_Q + 1, :]
            acc = jnp.where(row == h, d, acc)
        o_ref[...] = acc.T

    spec = pl.BlockSpec((Hh, None, AUG, ATT_BLK), lambda i: (0, i, 0, 0))
    return pl.pallas_call(
        body, name=name, out_shape=jax.ShapeDtypeStruct((nb * ATT_BLK, LANE), F32), grid=(nb,),
        in_specs=[spec, spec], out_specs=pl.BlockSpec((ATT_BLK, LANE), lambda i: (i, 0)),
        compiler_params=_params(("parallel",)))(dqT, dkT)


def _attn_fwd2(qT, k_aug, vT, name, comm=None):
    Hh, nb, _, _ = qT.shape
    LP = nb * ATT_BLK
    Dh = vT.shape[2]
    HB = ATT_HEADS_PER_STEP
    n_comm = 0 if comm is None else len(comm[0])

    def body(*refs):
        q_ref, k_ref, v_ref = refs[:3]
        o_ref, lse_ref = refs[3 + n_comm:5 + n_comm]
        if n_comm:
            start, wait = _gather_first_ops(refs[3:3 + n_comm], refs[5 + n_comm:5 + 2 * n_comm],
                                            refs[5 + 2 * n_comm], refs[6 + 2 * n_comm], comm[1])
            pl.when(pl.program_id(0) == 0)(start)
        keys = lax.broadcasted_iota(jnp.int32, (ATT_BLK, ATT_BLK), 0)
        qrys = lax.broadcasted_iota(jnp.int32, (ATT_BLK, ATT_BLK), 1)
        causal = keys <= qrys

        def q_block(i, _):
            def tile(j, carry, masked):
                ks = pl.ds(pl.multiple_of(j * ATT_BLK, ATT_BLK), ATT_BLK)
                out = []
                for hh in range(HB):
                    m, l, acc = carry[hh]
                    s = jnp.dot(k_ref[hh, ks, :], q_ref[hh, i], preferred_element_type=F32)
                    if masked:
                        s = jnp.where(causal, s, NEG)
                    m_new = jnp.maximum(m, jnp.max(s, axis=0, keepdims=True))
                    p = jnp.exp(s - m_new)
                    alpha = jnp.exp(m - m_new)
                    l = alpha * l + jnp.sum(p, axis=0, keepdims=True)
                    acc = alpha * acc + jnp.dot(v_ref[hh, j], p.astype(BF16), preferred_element_type=F32)
                    out.append((m_new, l, acc))
                return tuple(out)

            init = tuple((jnp.full((1, ATT_BLK), NEG, F32), jnp.zeros((1, ATT_BLK), F32),
                          jnp.zeros((Dh, ATT_BLK), F32)) for _ in range(HB))
            carry = lax.fori_loop(0, i, lambda j, cr: tile(j, cr, False), init)
            carry = tile(i, carry, True)
            for hh in range(HB):
                m, l, acc = carry[hh]
                o_ref[hh, i] = acc / l
                lse_ref[hh, i] = m + jnp.log(l)
            return 0

        lax.fori_loop(0, nb, q_block, 0)
        if n_comm:
            pl.when(pl.program_id(0) == Hh // HB - 1)(wait)

    blk = lambda r: pl.BlockSpec((HB, nb, r, ATT_BLK), lambda h: (h, 0, 0, 0))
    out_shape = (jax.ShapeDtypeStruct((Hh, nb, Dh, ATT_BLK), F32), jax.ShapeDtypeStruct((Hh, nb, 1, ATT_BLK), F32))
    scratch = []
    args = [qT, k_aug, vT]
    if n_comm:
        out_shape += tuple(jax.ShapeDtypeStruct(cut.full, s.dtype) for cut, s in zip(comm[1], comm[0]))
        scratch = [pltpu.SemaphoreType.DMA((4 * n_comm,)), pltpu.SemaphoreType.DMA((4 * n_comm,))]
        args += list(comm[0])
    return pl.pallas_call(
        body, name=name, out_shape=out_shape, grid=(Hh // HB,),
        in_specs=[blk(AUG), pl.BlockSpec((HB, LP, AUG), lambda h: (h, 0, 0)), blk(Dh)] + [ANY] * n_comm,
        out_specs=(blk(Dh), blk(1)) + (ANY,) * n_comm, scratch_shapes=scratch,
        compiler_params=_params(("arbitrary",)))(*args)


def _attn_bwd2(qT, kT, k_aug, v, oT, doT, lse, name, comm=None):
    Hh, nb, _, _ = qT.shape
    LP = nb * ATT_BLK
    Dh = v.shape[2]
    nt = (((1,), (1,)), ((), ()))
    tn = (((0,), (0,)), ((), ()))

    HB = ATT_HEADS_PER_STEP_BWD
    n_comm = 0 if comm is None else len(comm[0])

    def body(*refs):
        q_ref, kt_ref, k_ref, v_ref, o_ref, do_ref, lse_ref = refs[:7]
        parts = refs[7:7 + n_comm]
        dq_ref, dk_ref, dv_ref = refs[7 + n_comm:10 + n_comm]
        others = refs[10 + n_comm:10 + 2 * n_comm]
        delta_ref = refs[10 + 2 * n_comm]
        if n_comm:
            start, wait = _chip_exchange_ops(parts, others, refs[11 + 2 * n_comm], refs[12 + 2 * n_comm], comm[1])
            pl.when(pl.program_id(0) == 0)(start)
        keys = lax.broadcasted_iota(jnp.int32, (ATT_BLK, ATT_BLK), 0)
        qrys = lax.broadcasted_iota(jnp.int32, (ATT_BLK, ATT_BLK), 1)
        causal = keys <= qrys

        def prep(i, _):
            for hh in range(HB):
                delta_ref[hh, i] = jnp.sum(do_ref[hh, i].astype(F32) * o_ref[hh, i], axis=0, keepdims=True)
                dq_ref[hh, i] = jnp.zeros((AUG, ATT_BLK), F32)
            return 0

        lax.fori_loop(0, nb, prep, 0)

        def kv_block(j, _):
            ks = pl.ds(pl.multiple_of(j * ATT_BLK, ATT_BLK), ATT_BLK)

            def tile(i, carry, masked):
                out, dq_new = [], []
                for hh in range(HB):
                    dk, dv = carry[hh]
                    qb = q_ref[hh, i]
                    dob = do_ref[hh, i]
                    s = jnp.dot(k_ref[hh, ks, :], qb, preferred_element_type=F32)
                    if masked:
                        s = jnp.where(causal, s, NEG)
                    p = jnp.exp(s - lse_ref[hh, i])
                    dp = lax.dot_general(v_ref[hh, j], dob, tn, preferred_element_type=F32)
                    ds = (p * (dp - delta_ref[hh, i])).astype(BF16)
                    dv = dv + lax.dot_general(dob, p.astype(BF16), nt, preferred_element_type=F32)
                    dk = dk + lax.dot_general(qb, ds, nt, preferred_element_type=F32)
                    dq_new.append(jnp.dot(kt_ref[hh, j], ds, preferred_element_type=F32))
                    out.append((dk, dv))
                for hh in range(HB):
                    dq_ref[hh, i] += dq_new[hh]
                return tuple(out)

            init = tuple((jnp.zeros((AUG, ATT_BLK), F32), jnp.zeros((Dh, ATT_BLK), F32)) for _ in range(HB))
            carry = tile(j, init, True)
            carry = lax.fori_loop(j + 1, nb, lambda i, cr: tile(i, cr, False), carry)
            for hh in range(HB):
                dk_ref[hh, j] = carry[hh][0]
                dv_ref[hh, j] = carry[hh][1]
            return 0

        lax.fori_loop(0, nb, kv_block, 0)
        if n_comm:
            pl.when(pl.program_id(0) == Hh // HB - 1)(wait)

    blk = lambda r: pl.BlockSpec((HB, nb, r, ATT_BLK), lambda h: (h, 0, 0, 0))
    row = lambda cols: pl.BlockSpec((HB, LP, cols), lambda h: (h, 0, 0))
    out_shape = (jax.ShapeDtypeStruct((Hh, nb, AUG, ATT_BLK), F32), jax.ShapeDtypeStruct((Hh, nb, AUG, ATT_BLK), F32),
                 jax.ShapeDtypeStruct((Hh, nb, Dh, ATT_BLK), F32))
    scratch = [pltpu.VMEM((HB, nb, 1, ATT_BLK), F32)]
    args = [qT, kT, k_aug, v, oT, doT, lse]
    if n_comm:
        out_shape += _chip_exchange_shapes(*comm)
        scratch += [pltpu.SemaphoreType.DMA((3 * n_comm,)), pltpu.SemaphoreType.DMA((3 * n_comm,))]
        args += list(comm[0])
    return pl.pallas_call(
        body, name=name, out_shape=out_shape, grid=(Hh // HB,),
        in_specs=[blk(AUG), blk(AUG), row(AUG), blk(Dh), blk(Dh), blk(Dh), blk(1)] + [ANY] * n_comm,
        out_specs=(blk(AUG), blk(AUG), blk(Dh)) + (ANY,) * n_comm,
        scratch_shapes=scratch,
        compiler_params=_params(("arbitrary",)))(*args)


CONV_HALO = 32
A_BLK = 3 * FOX_W // CONV_CH
G_BLK = A_BLK + 1


def _conf_fwd(proj, cw, cb, lg, lb, tm, name):
    LP = proj.shape[0]
    C = CONV_CH
    sub = _sub_rows(tm)
    hpb = tm // CONV_HALO

    def body(a_ref, g_ref, ah_ref, gh_ref, w_ref, cb_ref, lg_ref, lb_ref, u1_ref, u_ref, buf):
        r = pl.program_id(0)
        buf[CONV_HALO:CONV_HALO + tm, :] = a_ref[...] * _sigmoid(g_ref[...])
        buf[0:CONV_HALO, :] = jnp.where(r > 0, ah_ref[...] * _sigmoid(gh_ref[...]), 0.0)
        for s in range(tm // sub):
            for ct in range(C // LANE):
                ln = slice(ct * LANE, (ct + 1) * LANE)
                acc = jnp.broadcast_to(cb_ref[:, ln], (sub, LANE))
                for kk in range(CONV_K):
                    off = CONV_HALO + s * sub - (CONV_K - 1) + kk
                    acc = acc + w_ref[kk:kk + 1, ln] * buf[off:off + sub, ln]
                u1_ref[s * sub:(s + 1) * sub, ln] = acc
        u1 = u1_ref[...]
        mu = jnp.mean(u1, axis=1, keepdims=True)
        xc = u1 - mu
        var = jnp.mean(xc * xc, axis=1, keepdims=True)
        y = xc * lax.rsqrt(var + LN_EPS) * lg_ref[...] + lb_ref[...]
        u_ref[...] = (y * _sigmoid(y)).astype(u_ref.dtype)

    cur = lambda blk: pl.BlockSpec((tm, C), lambda r: (r, blk))
    halo = lambda blk: pl.BlockSpec((CONV_HALO, C), lambda r: (jnp.maximum(r * hpb - 1, 0), blk))
    vec = pl.BlockSpec((1, C), lambda r: (0, 0))
    out = pl.BlockSpec((tm, C), lambda r: (r, 0))
    return pl.pallas_call(
        body, name=name,
        out_shape=(jax.ShapeDtypeStruct((LP, C), F32), jax.ShapeDtypeStruct((LP, C), BF16)),
        grid=(LP // tm,),
        in_specs=[cur(A_BLK), cur(G_BLK), halo(A_BLK), halo(G_BLK),
                  pl.BlockSpec((CONV_HALO, C), lambda r: (0, 0)), vec, vec, vec],
        out_specs=(out, out),
        scratch_shapes=[pltpu.VMEM((CONV_HALO + tm, C), F32)],
        compiler_params=_params(("parallel",)))(proj, proj, proj, proj, cw, cb, lg, lb)


def _conf_bwd(proj, u1, dcat, cw, lg, lb, tm, name):
    LP = proj.shape[0]
    C = CONV_CH
    sub = _sub_rows(tm)
    hpb = tm // CONV_HALO
    nblk = LP // tm
    last_halo = LP // CONV_HALO - 1

    def body(a_ref, g_ref, ah_ref, gh_ref, u1_ref, u1n_ref, du_ref, dun_ref, w_ref, lg_ref, lb_ref,
             dadg_ref, dw_ref, dcb_ref, dlg_ref, dlb_ref, ubuf, dbuf, du0):
        r = pl.program_id(0)
        lgv = lg_ref[...]
        lbv = lb_ref[...]

        def ln_silu_bwd(u1v, duv):
            mu = jnp.mean(u1v, axis=1, keepdims=True)
            xc = u1v - mu
            rstd = lax.rsqrt(jnp.mean(xc * xc, axis=1, keepdims=True) + LN_EPS)
            xhat = xc * rstd
            y = xhat * lgv + lbv
            sg = _sigmoid(y)
            dy = duv * (sg * (1.0 + y * (1.0 - sg)))
            dxh = dy * lgv
            du1 = rstd * (dxh - jnp.mean(dxh, axis=1, keepdims=True)
                          - xhat * jnp.mean(dxh * xhat, axis=1, keepdims=True))
            return du1, dy, xhat

        @pl.when(r == 0)
        def _():
            dw_ref[...] = jnp.zeros_like(dw_ref)
            dcb_ref[...] = jnp.zeros_like(dcb_ref)
            dlg_ref[...] = jnp.zeros_like(dlg_ref)
            dlb_ref[...] = jnp.zeros_like(dlb_ref)

        du1, dy, xhat = ln_silu_bwd(u1_ref[...], du_ref[...])
        dlg_ref[...] += jnp.sum(dy * xhat, axis=0, keepdims=True)
        dlb_ref[...] += jnp.sum(dy, axis=0, keepdims=True)
        dcb_ref[...] += jnp.sum(du1, axis=0, keepdims=True)
        dbuf[0:tm, :] = du1
        du1n, _, _ = ln_silu_bwd(u1n_ref[...], dun_ref[...])
        dbuf[tm:tm + CONV_HALO, :] = jnp.where(r < nblk - 1, du1n, 0.0)
        ubuf[CONV_HALO:CONV_HALO + tm, :] = a_ref[...] * _sigmoid(g_ref[...])
        ubuf[0:CONV_HALO, :] = jnp.where(r > 0, ah_ref[...] * _sigmoid(gh_ref[...]), 0.0)

        for ct in range(C // LANE):
            ln = slice(ct * LANE, (ct + 1) * LANE)
            for s in range(tm // sub):
                d_here = dbuf[s * sub:(s + 1) * sub, ln]
                acc = jnp.zeros((sub, LANE), F32)
                for kk in range(CONV_K):
                    fo = s * sub + (CONV_K - 1) - kk
                    acc = acc + w_ref[kk:kk + 1, ln] * dbuf[fo:fo + sub, ln]
                    bo = CONV_HALO + s * sub - (CONV_K - 1) + kk
                    dw_ref[kk:kk + 1, ln] += jnp.sum(d_here * ubuf[bo:bo + sub, ln], axis=0, keepdims=True)
                du0[s * sub:(s + 1) * sub, ln] = acc
        a = a_ref[...]
        sg = _sigmoid(g_ref[...])
        d0 = du0[...]
        dadg_ref[:, 0:C] = (d0 * sg).astype(dadg_ref.dtype)
        dadg_ref[:, C:2 * C] = (d0 * a * sg * (1.0 - sg)).astype(dadg_ref.dtype)

    cur = lambda blk: pl.BlockSpec((tm, C), lambda r: (r, blk))
    prev = lambda blk: pl.BlockSpec((CONV_HALO, C), lambda r: (jnp.maximum(r * hpb - 1, 0), blk))
    nxt = lambda blk: pl.BlockSpec((CONV_HALO, C), lambda r: (jnp.minimum((r + 1) * hpb, last_halo), blk))
    vec = pl.BlockSpec((1, C), lambda r: (0, 0))
    wspec = pl.BlockSpec((CONV_HALO, C), lambda r: (0, 0))
    return pl.pallas_call(
        body, name=name,
        out_shape=(jax.ShapeDtypeStruct((LP, 2 * C), BF16), jax.ShapeDtypeStruct((CONV_HALO, C), F32),
                   jax.ShapeDtypeStruct((1, C), F32), jax.ShapeDtypeStruct((1, C), F32),
                   jax.ShapeDtypeStruct((1, C), F32)),
        grid=(nblk,),
        in_specs=[cur(A_BLK), cur(G_BLK), prev(A_BLK), prev(G_BLK), cur(0), nxt(0), cur(1), nxt(1),
                  wspec, vec, vec],
        out_specs=(pl.BlockSpec((tm, 2 * C), lambda r: (r, 0)), wspec, vec, vec, vec),
        scratch_shapes=[pltpu.VMEM((CONV_HALO + tm, C), F32), pltpu.VMEM((tm + CONV_HALO, C), F32),
                        pltpu.VMEM((tm, C), F32)],
        compiler_params=_params(("arbitrary",)))(proj, proj, proj, proj, u1, u1, dcat, dcat, cw, lg, lb)


FFN_HALO = 8
FFN_TC = 256
FFN_K = 3


def _ffn_conv(buf, w_ref, b_ref, s, sub, ln):
    acc = jnp.broadcast_to(b_ref[:, ln], (sub, LANE))
    for kk in range(FFN_K):
        off = FFN_HALO + s * sub - (FFN_K - 1) + kk
        acc = acc + w_ref[kk:kk + 1, ln] * buf[off:off + sub, ln]
    return acc


def _ffn_act_fwd(up, w, b, tm, name):
    LP, F = up.shape[0], up.shape[1] // 2
    upg = upv = up
    nct = F // FFN_TC
    sub = _sub_rows(tm)
    hpb = tm // FFN_HALO

    def body(g_ref, v_ref, gh_ref, vh_ref, wg_ref, wv_ref, bg_ref, bv_ref, act_ref, gbuf, vbuf):
        r = pl.program_id(1)
        gbuf[FFN_HALO:FFN_HALO + tm, :] = g_ref[...]
        vbuf[FFN_HALO:FFN_HALO + tm, :] = v_ref[...]
        gbuf[0:FFN_HALO, :] = jnp.where(r > 0, gh_ref[...], 0.0)
        vbuf[0:FFN_HALO, :] = jnp.where(r > 0, vh_ref[...], 0.0)
        for s in range(tm // sub):
            for ct in range(FFN_TC // LANE):
                ln = slice(ct * LANE, (ct + 1) * LANE)
                gc = _ffn_conv(gbuf, wg_ref, bg_ref, s, sub, ln)
                vc = _ffn_conv(vbuf, wv_ref, bv_ref, s, sub, ln)
                act_ref[s * sub:(s + 1) * sub, ln] = (gc * _sigmoid(gc) * vc).astype(act_ref.dtype)

    cur = pl.BlockSpec((tm, FFN_TC), lambda c, r: (r, c))
    halo = pl.BlockSpec((FFN_HALO, FFN_TC), lambda c, r: (jnp.maximum(r * hpb - 1, 0), c))
    wg = pl.BlockSpec((8, FFN_TC), lambda c, r: (0, c))
    wv = pl.BlockSpec((8, FFN_TC), lambda c, r: (0, nct + c))
    bg = pl.BlockSpec((1, FFN_TC), lambda c, r: (0, c))
    bv = pl.BlockSpec((1, FFN_TC), lambda c, r: (0, nct + c))
    curv = pl.BlockSpec((tm, FFN_TC), lambda c, r: (r, nct + c))
    halov = pl.BlockSpec((FFN_HALO, FFN_TC), lambda c, r: (jnp.maximum(r * hpb - 1, 0), nct + c))
    return pl.pallas_call(
        body, name=name, out_shape=jax.ShapeDtypeStruct((LP, F), BF16), grid=(nct, LP // tm),
        in_specs=[cur, curv, halo, halov, wg, wv, bg, bv], out_specs=cur,
        scratch_shapes=[pltpu.VMEM((FFN_HALO + tm, FFN_TC), F32)] * 2,
        compiler_params=_params(("parallel", "parallel")))(upg, upv, upg, upv, w, w, b, b)


def _ffn_act_bwd(up, dact, w, b, tm, name):
    LP, F = up.shape[0], up.shape[1] // 2
    upg = upv = up
    nct = F // FFN_TC
    sub = _sub_rows(tm)
    hpb = tm // FFN_HALO
    nblk = LP // tm
    last_halo = LP // FFN_HALO - 1
    TB = tm + 2 * FFN_HALO

    def body(g_ref, v_ref, gp_ref, vp_ref, gn_ref, vn_ref, da_ref, dan_ref,
             wg_ref, wv_ref, bg_ref, bv_ref,
             dup_ref, dwg_ref, dwv_ref, dbg_ref, dbv_ref, gbuf, vbuf, dgb, dvb):
        r = pl.program_id(1)
        dg_ref = dup_ref.at[0]
        dv_ref = dup_ref.at[1]
        first = r == 0
        last = r == nblk - 1

        @pl.when(first)
        def _():
            dwg_ref[...] = jnp.zeros_like(dwg_ref)
            dwv_ref[...] = jnp.zeros_like(dwv_ref)
            dbg_ref[...] = jnp.zeros_like(dbg_ref)
            dbv_ref[...] = jnp.zeros_like(dbv_ref)

        for buf, c_ref, p_ref, n_ref in ((gbuf, g_ref, gp_ref, gn_ref), (vbuf, v_ref, vp_ref, vn_ref)):
            buf[0:FFN_HALO, :] = jnp.where(first, 0.0, p_ref[...])
            buf[FFN_HALO:FFN_HALO + tm, :] = c_ref[...]
            buf[FFN_HALO + tm:TB, :] = jnp.where(last, 0.0, n_ref[...])

        def dconv(s0, nrows, ln, dact_v):
            xg = [gbuf[s0 - (FFN_K - 1) + kk:s0 - (FFN_K - 1) + kk + nrows, ln] for kk in range(FFN_K)]
            xv = [vbuf[s0 - (FFN_K - 1) + kk:s0 - (FFN_K - 1) + kk + nrows, ln] for kk in range(FFN_K)]
            gc = jnp.broadcast_to(bg_ref[:, ln], (nrows, LANE))
            vc = jnp.broadcast_to(bv_ref[:, ln], (nrows, LANE))
            for kk in range(FFN_K):
                gc = gc + wg_ref[kk:kk + 1, ln] * xg[kk]
                vc = vc + wv_ref[kk:kk + 1, ln] * xv[kk]
            sg = _sigmoid(gc)
            return dact_v * vc * (sg * (1.0 + gc * (1.0 - sg))), dact_v * (gc * sg), xg, xv

        colsum = lambda t: jnp.sum(t, axis=0, keepdims=True)
        for ct in range(FFN_TC // LANE):
            ln = slice(ct * LANE, (ct + 1) * LANE)
            zero = jnp.zeros((1, LANE), F32)
            dwg, dwv, dbg, dbv = [zero] * FFN_K, [zero] * FFN_K, zero, zero
            for s in range(tm // sub):
                dgc, dvc, xg, xv = dconv(FFN_HALO + s * sub, sub, ln, da_ref[s * sub:(s + 1) * sub, ln])
                dgb[s * sub:(s + 1) * sub, ln] = dgc
                dvb[s * sub:(s + 1) * sub, ln] = dvc
                dwg = [dwg[kk] + colsum(dgc * xg[kk]) for kk in range(FFN_K)]
                dwv = [dwv[kk] + colsum(dvc * xv[kk]) for kk in range(FFN_K)]
                dbg, dbv = dbg + colsum(dgc), dbv + colsum(dvc)
            for kk in range(FFN_K):
                dwg_ref[kk:kk + 1, ln] += dwg[kk]
                dwv_ref[kk:kk + 1, ln] += dwv[kk]
            dbg_ref[:, ln] += dbg
            dbv_ref[:, ln] += dbv
            dgc, dvc, _, _ = dconv(FFN_HALO + tm, FFN_HALO, ln, jnp.where(last, 0.0, dan_ref[:, ln]))
            dgb[tm:tm + FFN_HALO, ln] = dgc
            dvb[tm:tm + FFN_HALO, ln] = dvc
            for dbuf, w_ref, dout in ((dgb, wg_ref, dg_ref), (dvb, wv_ref, dv_ref)):
                for s in range(tm // sub):
                    acc = jnp.zeros((sub, LANE), F32)
                    for kk in range(FFN_K):
                        fo = s * sub + (FFN_K - 1) - kk
                        acc = acc + w_ref[kk:kk + 1, ln] * dbuf[fo:fo + sub, ln]
                    dout[s * sub:(s + 1) * sub, ln] = acc.astype(dout.dtype)

    cur = pl.BlockSpec((tm, FFN_TC), lambda c, r: (r, c))
    prev = pl.BlockSpec((FFN_HALO, FFN_TC), lambda c, r: (jnp.maximum(r * hpb - 1, 0), c))
    nxt = pl.BlockSpec((FFN_HALO, FFN_TC), lambda c, r: (jnp.minimum((r + 1) * hpb, last_halo), c))
    wg = pl.BlockSpec((8, FFN_TC), lambda c, r: (0, c))
    wv = pl.BlockSpec((8, FFN_TC), lambda c, r: (0, nct + c))
    bg = pl.BlockSpec((1, FFN_TC), lambda c, r: (0, c))
    bv = pl.BlockSpec((1, FFN_TC), lambda c, r: (0, nct + c))
    curv = pl.BlockSpec((tm, FFN_TC), lambda c, r: (r, nct + c))
    prevv = pl.BlockSpec((FFN_HALO, FFN_TC), lambda c, r: (jnp.maximum(r * hpb - 1, 0), nct + c))
    nxtv = pl.BlockSpec((FFN_HALO, FFN_TC), lambda c, r: (jnp.minimum((r + 1) * hpb, last_halo), nct + c))
    dup, dwg, dwv, dbg, dbv = pl.pallas_call(
        body, name=name,
        out_shape=(jax.ShapeDtypeStruct((2, LP, F), BF16),
                   jax.ShapeDtypeStruct((8, F), F32), jax.ShapeDtypeStruct((8, F), F32),
                   jax.ShapeDtypeStruct((1, F), F32), jax.ShapeDtypeStruct((1, F), F32)),
        grid=(nct, nblk),
        in_specs=[cur, curv, prev, prevv, nxt, nxtv, cur, nxt, wg, wv, bg, bv],
        out_specs=(pl.BlockSpec((2, tm, FFN_TC), lambda c, r: (0, r, c)),
                   pl.BlockSpec((8, FFN_TC), lambda c, r: (0, c)),
                   pl.BlockSpec((8, FFN_TC), lambda c, r: (0, c)),
                   pl.BlockSpec((1, FFN_TC), lambda c, r: (0, c)),
                   pl.BlockSpec((1, FFN_TC), lambda c, r: (0, c))),
        scratch_shapes=[pltpu.VMEM((TB, FFN_TC), F32), pltpu.VMEM((TB, FFN_TC), F32),
                        pltpu.VMEM((tm + FFN_HALO, FFN_TC), F32), pltpu.VMEM((tm + FFN_HALO, FFN_TC), F32)],
        compiler_params=_params(("parallel", "arbitrary")))(
            upg, upv, upg, upv, upg, upv, dact, dact, w, w, b, b)
    return dup, jnp.concatenate([dwg, dwv], axis=1), jnp.concatenate([dbg, dbv], axis=1)


POOL_HALO = 16


def _pool_fwd(h, g, pw, pb, ps, tm, name):
    LP, Dm = h.shape
    sub = _sub_rows(tm)
    hpb = tm // POOL_HALO

    def body(h_ref, hh_ref, g_ref, pw_ref, pb_ref, ps_ref, o_ref, d_ref, buf):
        r = pl.program_id(0)
        gg = g_ref[...]

        def norm(x):
            return x * lax.rsqrt(jnp.mean(x * x, axis=1, keepdims=True) + RMS_EPS) * gg

        x = h_ref[...]
        buf[POOL_HALO:POOL_HALO + tm, :] = norm(x)
        buf[0:POOL_HALO, :] = jnp.where(r > 0, norm(hh_ref[...]), 0.0)
        for gi, w in enumerate(POOL_WINDOWS):
            ln = slice(gi * POOL_G, (gi + 1) * POOL_G)
            for s in range(tm // sub):
                base = POOL_HALO + s * sub
                acc = buf[base:base + sub, ln]
                for jj in range(1, w):
                    acc = acc + buf[base - jj:base - jj + sub, ln]
                t = r * tm + s * sub + lax.broadcasted_iota(jnp.int32, (sub, 1), 0)
                cnt = jnp.minimum(t + 1, w).astype(F32)
                d_ref[s * sub:(s + 1) * sub, ln] = (acc / cnt - buf[base:base + sub, ln]).astype(d_ref.dtype)
            y = jnp.dot(d_ref[:, ln], pw_ref[gi], preferred_element_type=F32) + pb_ref[:, ln]
            o_ref[:, ln] = x[:, ln] + y * ps_ref[:, ln]

    row = pl.BlockSpec((tm, Dm), lambda r: (r, 0))
    halo = pl.BlockSpec((POOL_HALO, Dm), lambda r: (jnp.maximum(r * hpb - 1, 0), 0))
    vec = pl.BlockSpec((1, Dm), lambda r: (0, 0))
    wsp = pl.BlockSpec((len(POOL_WINDOWS), POOL_G, POOL_G), lambda r: (0, 0, 0))
    return pl.pallas_call(
        body, name=name,
        out_shape=(jax.ShapeDtypeStruct((LP, Dm), F32), jax.ShapeDtypeStruct((LP, Dm), BF16)),
        grid=(LP // tm,), in_specs=[row, halo, vec, wsp, vec, vec], out_specs=(row, row),
        scratch_shapes=[pltpu.VMEM((POOL_HALO + tm, Dm), F32)],
        compiler_params=_params(("parallel",)))(h, h, g, pw, pb, ps)


def _pool_bwd(h, g, d, pw, pb, ps, dh_out, tm, name):
    LP, Dm = h.shape
    sub = _sub_rows(tm)
    hpb = tm // POOL_HALO
    nblk = LP // tm
    last_halo = LP // POOL_HALO - 1
    nt = (((1,), (1,)), ((), ()))
    tn = (((0,), (0,)), ((), ()))

    def body(h_ref, g_ref, d_ref, pw_ref, pb_ref, ps_ref, do_ref, don_ref,
             dh_ref, dpw_ref, dpb_ref, dps_ref, dg_ref, ebuf, ddb, dnb):
        r = pl.program_id(0)

        @pl.when(r == 0)
        def _():
            dpw_ref[...] = jnp.zeros_like(dpw_ref)
            dpb_ref[...] = jnp.zeros_like(dpb_ref)
            dps_ref[...] = jnp.zeros_like(dps_ref)
            dg_ref[...] = jnp.zeros_like(dg_ref)

        for gi, w in enumerate(POOL_WINDOWS):
            ln = slice(gi * POOL_G, (gi + 1) * POOL_G)
            wg = pw_ref[gi]
            dog = do_ref[:, ln]
            dg_b = d_ref[:, ln]
            y_pre = jnp.dot(dg_b, wg, preferred_element_type=F32) + pb_ref[:, ln]
            dps_ref[:, ln] += jnp.sum(dog * y_pre, axis=0, keepdims=True)
            dy = dog * ps_ref[:, ln]
            dpb_ref[:, ln] += jnp.sum(dy, axis=0, keepdims=True)
            dyb = dy.astype(BF16)
            dpw_ref[gi] += lax.dot_general(dg_b, dyb, tn, preferred_element_type=F32)
            dd = lax.dot_general(dyb, wg, nt, preferred_element_type=F32)
            ddb[:, ln] = dd
            t = r * tm + lax.broadcasted_iota(jnp.int32, (tm, 1), 0)
            ebuf[0:tm, ln] = dd / jnp.minimum(t + 1, w).astype(F32)
            dyn = (don_ref[:, ln] * ps_ref[:, ln]).astype(BF16)
            ddn = lax.dot_general(dyn, wg, nt, preferred_element_type=F32)
            tn_ = (r + 1) * tm + lax.broadcasted_iota(jnp.int32, (POOL_HALO, 1), 0)
            ebuf[tm:tm + POOL_HALO, ln] = jnp.where(r < nblk - 1, ddn / jnp.minimum(tn_ + 1, w).astype(F32), 0.0)
            for s in range(tm // sub):
                acc = ebuf[s * sub:(s + 1) * sub, ln]
                for jj in range(1, w):
                    acc = acc + ebuf[s * sub + jj:s * sub + jj + sub, ln]
                dnb[s * sub:(s + 1) * sub, ln] = acc - ddb[s * sub:(s + 1) * sub, ln]
        x = h_ref[...]
        rr = lax.rsqrt(jnp.mean(x * x, axis=1, keepdims=True) + RMS_EPS)
        xhat = x * rr
        dn = dnb[...]
        dxh = dn * g_ref[...]
        dh_ref[...] = do_ref[...] + rr * (dxh - xhat * jnp.mean(dxh * xhat, axis=1, keepdims=True))
        dg_ref[...] += jnp.sum(dn * xhat, axis=0, keepdims=True)

    row = pl.BlockSpec((tm, Dm), lambda r: (r, 0))
    nxt = pl.BlockSpec((POOL_HALO, Dm), lambda r: (jnp.minimum((r + 1) * hpb, last_halo), 0))
    vec = pl.BlockSpec((1, Dm), lambda r: (0, 0))
    wsp = pl.BlockSpec((len(POOL_WINDOWS), POOL_G, POOL_G), lambda r: (0, 0, 0))
    return pl.pallas_call(
        body, name=name,
        out_shape=(jax.ShapeDtypeStruct((LP, Dm), F32),
                   jax.ShapeDtypeStruct((len(POOL_WINDOWS), POOL_G, POOL_G), F32),
                   jax.ShapeDtypeStruct((1, Dm), F32), jax.ShapeDtypeStruct((1, Dm), F32),
                   jax.ShapeDtypeStruct((1, Dm), F32)),
        grid=(nblk,), in_specs=[row, vec, row, wsp, vec, vec, row, nxt],
        out_specs=(row, wsp, vec, vec, vec),
        scratch_shapes=[pltpu.VMEM((tm + POOL_HALO, Dm), F32), pltpu.VMEM((tm, Dm), F32),
                        pltpu.VMEM((tm, Dm), F32)],
        compiler_params=_params(("arbitrary",)))(h, g, d, pw, pb, ps, dh_out, dh_out)


def _adamw(w, g, m, v, name):
    shape = w.shape
    cols = shape[-1]
    rows = int(np.prod(shape[:-1])) if len(shape) > 1 else 1
    w2, g2, m2, v2 = (t.reshape(rows, cols) for t in (w, g, m, v))
    tr = rows
    for cand in (256, 128, 64, 32, 16, 8):
        if rows % cand == 0 and rows > cand:
            tr = cand
            break
    c1 = float(1.0 - ADAM_B1 ** ADAM_STEP)
    c2 = float(1.0 - ADAM_B2 ** ADAM_STEP)

    def body(w_ref, g_ref, m_ref, v_ref, d_ref, mo_ref, vo_ref):
        gg = g_ref[...]
        mn = ADAM_B1 * m_ref[...] + (1.0 - ADAM_B1) * gg
        vn = ADAM_B2 * v_ref[...] + (1.0 - ADAM_B2) * (gg * gg)
        m_hat = mn / c1
        v_hat = vn / c2
        d_ref[...] = -ADAM_LR * (m_hat / (jnp.sqrt(v_hat) + ADAM_EPS) + ADAM_WD * w_ref[...])
        mo_ref[...] = mn
        vo_ref[...] = vn

    spec = pl.BlockSpec((tr, cols), lambda i: (i, 0))
    sds = jax.ShapeDtypeStruct((rows, cols), F32)
    d2, mo, vo = pl.pallas_call(
        body, name=name, out_shape=(sds, sds, sds), grid=(rows // tr,),
        in_specs=[spec] * 4, out_specs=(spec,) * 3,
        compiler_params=_params(("parallel",)))(w2, g2, m2, v2)
    return d2.reshape(shape), mo.reshape(shape), vo.reshape(shape)


def _row_tiles(LP):
    tm = LP // 4
    assert LP % 4 == 0 and tm % CONV_HALO == 0 and LP % ATT_BLK == 0, LP
    return tm, LP // 2


def _heads(t, LP):
    return t.reshape(LP, HEADS, HEAD_DIM).transpose(1, 0, 2)


def _unheads(t, LP):
    return t.transpose(1, 0, 2).reshape(LP, FOX_W)


def _ffn_fwd(h, gain, wug, wuv, cw, cb, wd, tm, tmm, tag):
    n = _rms_fwd(h, gain, BF16, tm, f"ffn_norm_{tag}")
    upg = _mm(n, wug, "nn", F32, tmm, 256, f"ffn_up_gate_{tag}")
    upv = _mm(n, wuv, "nn", F32, tmm, 256, f"ffn_up_val_{tag}")
    act = _ffn_act_fwd(upg, upv, cw, cb, tm, f"ffn_act_{tag}")
    out = _mm(act, wd, "nn", F32, tmm, 512, f"ffn_down_{tag}", add=h)
    return out, (n, upg, upv, act)


def _ffn_bwd(h, gain, wug, wuv, cw, cb, wd, saved, dout, tm, tmm, tag):
    n, upg, upv, act = saved
    dact = _mm(dout, wd, "nt", F32, tmm, 256, f"ffn_dact_{tag}")
    dwd = _mm(act, dout, "tn", F32, 256, 512, f"ffn_dwdown_{tag}")
    dupg, dupv, dcw, dcb = _ffn_act_bwd(upg, upv, dact, cw, cb, tm, f"ffn_act_bwd_{tag}")
    dn = _mm(dupg, wug, "nt", F32, tm, 512, f"ffn_dn_gate_{tag}")
    dn = _mm(dupv, wuv, "nt", F32, tm, 512, f"ffn_dn_val_{tag}", add=dn)
    dwug = _mm(n, dupg, "tn", F32, 512, 256, f"ffn_dwup_gate_{tag}")
    dwuv = _mm(n, dupv, "tn", F32, 512, 256, f"ffn_dwup_val_{tag}")
    dh, dgain = _rms_bwd(h, gain, dn, dout, tm, f"ffn_norm_bwd_{tag}")
    return dh, dict(gain=dgain, wug=dwug, wuv=dwuv, cw=dcw[:FFN_K], cb=dcb, wd=dwd)


def _local_step(h0, tgt, W, n_real):
    LP = h0.shape[0]
    tm, tmm = _row_tiles(LP)
    nb = LP // ATT_BLK
    G = {}

    n0 = _rms_fwd(h0, W["mix_norm_even"], BF16, tm, "mix_norm_even")
    proj = _mm(n0, W["w_in_p"], "nn", F32, tmm, 384, "in_proj")
    c = _fgate_fwd(proj, W["b_f_p"], "forget_gate")
    cT = c[:, :HEADS].T
    c_col = cT[:, :, None]
    c_row = cT.reshape(HEADS, nb, 1, ATT_BLK)
    qkv = proj[:, :3 * FOX_W].astype(BF16)
    q, k, v = (_heads(qkv[:, i * FOX_W:(i + 1) * FOX_W], LP) for i in range(3))
    o, lse = _attn_fwd(q, k, v, c_col, c_row, "fox_attention")
    u1, u = _conf_fwd(proj, W["conv_w_p"], W["conv_b"], W["ln_g"], W["ln_b"], tm, "conformer")
    cat = jnp.concatenate([_unheads(o, LP).astype(BF16), u], axis=1)
    h1 = _mm(cat, W["w_out"], "nn", F32, tmm, 512, "out_proj", add=h0)
    h2, ffn0 = _ffn_fwd(h1, W["ffn_norm"][0:1], W["w_up_g"][0], W["w_up_v"][0], W["ffn_conv_w_p"][0],
                        W["ffn_conv_b"][0:1], W["w_down"][0], tm, tmm, "0")
    h3, dpool = _pool_fwd(h2, W["mix_norm_odd"], W["pool_w"], W["pool_b"], W["pool_scale"], tm, "pool_mixer")
    h4, ffn1 = _ffn_fwd(h3, W["ffn_norm"][1:2], W["w_up_g"][1], W["w_up_v"][1], W["ffn_conv_w_p"][1],
                        W["ffn_conv_b"][1:2], W["w_down"][1], tm, tmm, "1")
    loss, dh4, G["final_norm"] = _loss_head(h4, W["final_norm"], tgt, n_real, tm, "loss_head")

    dh3, g1 = _ffn_bwd(h3, W["ffn_norm"][1:2], W["w_up_g"][1], W["w_up_v"][1], W["ffn_conv_w_p"][1],
                       W["ffn_conv_b"][1:2], W["w_down"][1], ffn1, dh4, tm, tmm, "1")
    dh2, G["pool_w"], G["pool_b"], G["pool_scale"], G["mix_norm_odd"] = _pool_bwd(
        h2, W["mix_norm_odd"], dpool, W["pool_w"], W["pool_b"], W["pool_scale"], dh3, tm, "pool_mixer_bwd")
    dh1, g0 = _ffn_bwd(h1, W["ffn_norm"][0:1], W["w_up_g"][0], W["w_up_v"][0], W["ffn_conv_w_p"][0],
                       W["ffn_conv_b"][0:1], W["w_down"][0], ffn0, dh2, tm, tmm, "0")
    for key in ("gain", "wug", "wuv", "cw", "cb", "wd"):
        G["ffn_" + key] = (g0[key], g1[key])

    dcat = _mm(dh1, W["w_out"], "nt", F32, tmm, 512, "out_proj_dx")
    G["w_out"] = _mm(cat, dh1, "tn", F32, 512, 512, "out_proj_dw")
    dadg, dcw, G["conv_b"], G["ln_g"], G["ln_b"] = _conf_bwd(
        proj, u1, dcat, W["conv_w_p"], W["ln_g"], W["ln_b"], tm, "conformer_bwd")
    G["conv_w"] = dcw[:CONV_K]
    do = _heads(dcat[:, :FOX_W], LP)
    dq, dk, dv, dcq, dck = _attn_bwd(q, k, v, o, do, lse, c_col, c_row, "fox_attention_bwd")
    dc = jnp.pad((dcq.reshape(HEADS, LP) + dck.reshape(HEADS, LP)).T, ((0, 0), (0, LANE - HEADS)))
    dfl, dbf = _fgate_bwd(proj, W["b_f_p"], dc, "forget_gate_bwd")
    G["b_f"] = dbf[:, :HEADS]
    dproj = jnp.concatenate([_unheads(t, LP).astype(BF16) for t in (dq, dk, dv)] + [dadg, dfl], axis=1)
    dn0 = _mm(dproj, W["w_in_p"], "nt", F32, tmm, 512, "in_proj_dx")
    G["w_in_p"] = _mm(n0, dproj, "tn", F32, 512, 384, "in_proj_dw")
    dh0, G["mix_norm_even"] = _rms_bwd(h0, W["mix_norm_even"], dn0, dh1, tm, "mix_norm_even_bwd")
    return loss, dh0, G


def _compute_layout(P):
    w_in = P["w_in"].reshape(D_MODEL, IN_COLS)
    qkv, f, ag = w_in[:, :3 * FOX_W], w_in[:, 3 * FOX_W:3 * FOX_W + HEADS], w_in[:, 3 * FOX_W + HEADS:]
    w_in_p = jnp.concatenate([qkv, ag, f, jnp.zeros((D_MODEL, LANE - HEADS), w_in.dtype)], axis=1).astype(BF16)
    w_up = P["w_up"].astype(BF16)
    return dict(
        mix_norm_even=P["mix_norm_even"].reshape(1, D_MODEL).astype(F32),
        w_in_p=w_in_p,
        b_f_p=jnp.pad(P["b_f"].reshape(1, HEADS).astype(F32), ((0, 0), (0, LANE - HEADS))),
        conv_w_p=jnp.pad(P["conv_w"].reshape(CONV_K, CONV_CH).astype(F32), ((0, CONV_HALO - CONV_K), (0, 0))),
        conv_b=P["conv_b"].reshape(1, CONV_CH).astype(F32),
        ln_g=P["ln_g"].reshape(1, CONV_CH).astype(F32),
        ln_b=P["ln_b"].reshape(1, CONV_CH).astype(F32),
        w_out=P["w_out"].reshape(D_MODEL, D_MODEL).astype(BF16),
        mix_norm_odd=P["mix_norm_odd"].reshape(1, D_MODEL).astype(F32),
        pool_w=P["pool_w"].reshape(len(POOL_WINDOWS), POOL_G, POOL_G).astype(BF16),
        pool_b=P["pool_b"].reshape(1, D_MODEL).astype(F32),
        pool_scale=P["pool_scale"].reshape(1, D_MODEL).astype(F32),
        ffn_norm=P["ffn_norm"].astype(F32),
        w_up_g=w_up[:, :, :D_FF],
        w_up_v=w_up[:, :, D_FF:],
        ffn_conv_w_p=jnp.pad(P["ffn_conv_w"].astype(F32), ((0, 0), (0, 8 - FFN_K), (0, 0))),
        ffn_conv_b=P["ffn_conv_b"].astype(F32),
        w_down=P["w_down"].astype(BF16),
        final_norm=P["final_norm"].reshape(1, D_MODEL).astype(F32),
    )


def _reference_layout(G, dh0):
    gp = G["w_in_p"]
    g_w_in = jnp.concatenate([gp[:, :3 * FOX_W], gp[:, 3 * FOX_W + 2 * CONV_CH:3 * FOX_W + 2 * CONV_CH + HEADS],
                              gp[:, 3 * FOX_W:3 * FOX_W + 2 * CONV_CH]], axis=1)
    return dict(
        meta_tokens=dh0[:N_META],
        mix_norm_even=G["mix_norm_even"],
        w_in=g_w_in[None],
        b_f=G["b_f"],
        conv_w=G["conv_w"][None],
        conv_b=G["conv_b"],
        ln_g=G["ln_g"],
        ln_b=G["ln_b"],
        w_out=G["w_out"][None],
        mix_norm_odd=G["mix_norm_odd"],
        pool_w=G["pool_w"][None],
        pool_b=G["pool_b"].reshape(1, len(POOL_WINDOWS), POOL_G),
        pool_scale=G["pool_scale"],
        ffn_norm=jnp.concatenate(G["ffn_gain"], axis=0),
        w_up=jnp.stack([jnp.concatenate([g, v], axis=1) for g, v in zip(G["ffn_wug"], G["ffn_wuv"])]),
        ffn_conv_w=jnp.stack(G["ffn_cw"]),
        ffn_conv_b=jnp.concatenate(G["ffn_cb"], axis=0),
        w_down=jnp.stack(G["ffn_wd"]),
        final_norm=G["final_norm"].reshape(D_MODEL),
    )


MESH = pl.DeviceIdType.MESH
ANY = pl.BlockSpec(memory_space=pl.ANY)
PACK_COLS = 1024


def _coords():
    return lax.axis_index("x"), lax.axis_index("y"), lax.axis_index("c")


def _other_chips(x, y):
    return [(1 - x, y), (x, 1 - y), (1 - x, 1 - y)]


def _allgather_chips(pack):
    R, C = pack.shape
    R2 = R // 2

    def body(x_ref, o_ref, send_sems, recv_sems, local_sem):
        x, y, c = _coords()
        sibling = (x, y, 1 - c)
        chips = _other_chips(x, y)

        def slot(px, py, half):
            return o_ref.at[2 * px + py, pl.ds(half * R2, R2), :]

        def copy(k, src, dst, to):
            return pltpu.make_async_remote_copy(src_ref=src, dst_ref=dst, send_sem=send_sems.at[k],
                                                recv_sem=recv_sems.at[k], device_id=to, device_id_type=MESH)

        mine = pltpu.make_async_copy(x_ref, o_ref.at[2 * x + y], local_sem)
        mine.start()
        my_half = x_ref.at[pl.ds(c * R2, R2), :]
        first = [copy(j, my_half, slot(x, y, c), (*chip, c)) for j, chip in enumerate(chips)]
        for cp in first:
            cp.start()
        passed = [copy(3 + j, slot(*chip, c), slot(*chip, c), sibling) for j, chip in enumerate(chips)]
        for j, chip in enumerate(chips):
            copy(j, my_half, slot(*chip, c), sibling).wait_recv()
            passed[j].start()
        for j, chip in enumerate(chips):
            copy(3 + j, my_half, slot(*chip, 1 - c), sibling).wait_recv()
        for cp in first + passed:
            cp.wait_send()
        mine.wait()

    return pl.pallas_call(
        body, name="allgather_weights", out_shape=jax.ShapeDtypeStruct((N_CHIPS, R, C), pack.dtype),
        in_specs=[ANY], out_specs=ANY,
        scratch_shapes=[pltpu.SemaphoreType.DMA((6,)), pltpu.SemaphoreType.DMA((6,)), pltpu.SemaphoreType.DMA],
    )(pack)


def _pair_exchange(G):
    n, R, C = G.shape
    R2 = R // 2

    def body(g_ref, o_ref, send_sem, recv_sem):
        x, y, c = _coords()
        src = g_ref.at[pl.ds(0, n), pl.ds((1 - c) * R2, R2), :]
        cp = pltpu.make_async_remote_copy(src_ref=src, dst_ref=o_ref, send_sem=send_sem, recv_sem=recv_sem,
                                          device_id=(x, y, 1 - c), device_id_type=MESH)
        cp.start()
        cp.wait()

    return pl.pallas_call(
        body, name="grad_pair_exchange", out_shape=jax.ShapeDtypeStruct((n, R2, C), G.dtype),
        in_specs=[ANY], out_specs=ANY,
        scratch_shapes=[pltpu.SemaphoreType.DMA, pltpu.SemaphoreType.DMA],
    )(G)


def _row_tile(rows, align, cap):
    best = None
    for t in range(align, min(rows, cap) + 1, align):
        if rows % t == 0:
            best = t
    assert best is not None, (rows, align, cap)
    return best


def _pair_sum(G, recv):
    n, R, C = G.shape
    R2 = R // 2
    tr = _row_tile(R2, 16, 704)
    nrb = R2 // tr
    half = lax.axis_index("c").astype(jnp.int32).reshape(1)

    def body(c_ref, g_ref, r_ref, o_ref):
        o_ref[...] = (g_ref[...] + r_ref[...]).astype(o_ref.dtype)

    return pl.pallas_call(
        body, name="grad_pair_sum", out_shape=jax.ShapeDtypeStruct((n, R2, C), BF16),
        grid_spec=pltpu.PrefetchScalarGridSpec(
            num_scalar_prefetch=1, grid=(n, nrb),
            in_specs=[pl.BlockSpec((None, tr, C), lambda j, i, c_ref: (j, c_ref[0] * nrb + i, 0)),
                      pl.BlockSpec((None, tr, C), lambda j, i, c_ref: (j, i, 0))],
            out_specs=pl.BlockSpec((None, tr, C), lambda j, i, c_ref: (j, i, 0))),
        compiler_params=_params(("parallel", "parallel")))(half, G, recv)


def _chip_exchange(P):
    n, R2, C = P.shape

    def body(p_ref, o_ref, send_sems, recv_sems, local_sem):
        x, y, c = _coords()
        me = 2 * x + y
        chips = _other_chips(x, y)
        mine = pltpu.make_async_copy(p_ref.at[me], o_ref.at[me], local_sem)
        mine.start()
        sends = [pltpu.make_async_remote_copy(
            src_ref=p_ref.at[2 * px + py], dst_ref=o_ref.at[me], send_sem=send_sems.at[k],
            recv_sem=recv_sems.at[k], device_id=(px, py, c), device_id_type=MESH)
            for k, (px, py) in enumerate(chips)]
        for cp in sends:
            cp.start()
        for k, (px, py) in enumerate(chips):
            pltpu.make_async_remote_copy(
                src_ref=p_ref.at[me], dst_ref=o_ref.at[2 * px + py], send_sem=send_sems.at[k],
                recv_sem=recv_sems.at[k], device_id=(px, py, c), device_id_type=MESH).wait_recv()
        for cp in sends:
            cp.wait_send()
        mine.wait()

    return pl.pallas_call(
        body, name="grad_chip_exchange", out_shape=jax.ShapeDtypeStruct((n, R2, C), P.dtype),
        in_specs=[ANY], out_specs=ANY,
        scratch_shapes=[pltpu.SemaphoreType.DMA((3,)), pltpu.SemaphoreType.DMA((3,)), pltpu.SemaphoreType.DMA],
    )(P)


def _chip_sum(X):
    n, R2, C = X.shape
    tr = _row_tile(R2, 16, 704)

    def body(x_ref, o_ref):
        acc = x_ref[0].astype(F32)
        for s in range(1, n):
            acc = acc + x_ref[s].astype(F32)
        o_ref[...] = acc

    return pl.pallas_call(
        body, name="grad_chip_sum", out_shape=jax.ShapeDtypeStruct((R2, C), F32), grid=(R2 // tr,),
        in_specs=[pl.BlockSpec((n, tr, C), lambda i: (0, i, 0))],
        out_specs=pl.BlockSpec((tr, C), lambda i: (i, 0)),
        compiler_params=_params(("parallel",)))(X)


def _pair_allgather(Q):
    R2, C = Q.shape

    def body(q_ref, o_ref, send_sem, recv_sem, local_sem):
        x, y, c = _coords()
        mine = pltpu.make_async_copy(q_ref, o_ref.at[c], local_sem)
        mine.start()
        cp = pltpu.make_async_remote_copy(src_ref=q_ref, dst_ref=o_ref.at[c], send_sem=send_sem,
                                          recv_sem=recv_sem, device_id=(x, y, 1 - c), device_id_type=MESH)
        cp.start()
        pltpu.make_async_remote_copy(src_ref=q_ref, dst_ref=o_ref.at[1 - c], send_sem=send_sem,
                                     recv_sem=recv_sem, device_id=(x, y, 1 - c), device_id_type=MESH).wait_recv()
        cp.wait_send()
        mine.wait()

    return pl.pallas_call(
        body, name="grad_pair_allgather", out_shape=jax.ShapeDtypeStruct((2, R2, C), Q.dtype),
        in_specs=[ANY], out_specs=ANY,
        scratch_shapes=[pltpu.SemaphoreType.DMA, pltpu.SemaphoreType.DMA, pltpu.SemaphoreType.DMA],
    )(Q)


def _allreduce_small(pack):
    Rs, C = pack.shape
    n_dev = 8

    def body(x_ref, o_ref, buf, send_sems, recv_sems):
        x, y, c = _coords()
        me = 4 * x + 2 * y + c
        buf[me] = x_ref[...]
        peers = []
        for rel in range(1, n_dev):
            px = 1 - x if rel & 4 else x
            py = 1 - y if rel & 2 else y
            pc = 1 - c if rel & 1 else c
            peers.append((px, py, pc))
        sends = [pltpu.make_async_remote_copy(
            src_ref=x_ref, dst_ref=buf.at[me], send_sem=send_sems.at[k], recv_sem=recv_sems.at[k],
            device_id=peer, device_id_type=MESH) for k, peer in enumerate(peers)]
        for cp in sends:
            cp.start()
        for k, (px, py, pc) in enumerate(peers):
            pltpu.make_async_remote_copy(
                src_ref=x_ref, dst_ref=buf.at[4 * px + 2 * py + pc], send_sem=send_sems.at[k],
                recv_sem=recv_sems.at[k], device_id=(px, py, pc), device_id_type=MESH).wait_recv()
        for cp in sends:
            cp.wait_send()
        acc = buf[0]
        for d in range(1, n_dev):
            acc = acc + buf[d]
        o_ref[...] = acc

    vm = pl.BlockSpec(memory_space=pltpu.VMEM)
    return pl.pallas_call(
        body, name="allreduce_replicated", out_shape=jax.ShapeDtypeStruct((Rs, C), F32),
        in_specs=[vm], out_specs=vm,
        scratch_shapes=[pltpu.VMEM((n_dev, Rs, C), F32), pltpu.SemaphoreType.DMA((n_dev - 1,)),
                        pltpu.SemaphoreType.DMA((n_dev - 1,))],
    )(pack)


SHARDED = (
    ("w_in", 2, True), ("w_out", 1, True), ("pool_w", 2, True), ("w_up", 2, True), ("w_down", 1, True),
    ("meta_tokens", 1, False), ("mix_norm_odd", 1, False), ("pool_b", 2, False), ("pool_scale", 1, False),
    ("conv_w", 2, False), ("ffn_conv_w", 2, False))
REPLICATED = ("mix_norm_even", "b_f", "conv_b", "ln_g", "ln_b", "ffn_norm", "ffn_conv_b", "final_norm")
PACK_ROW_ALIGN = 32


def _pad_rows(flat, align_rows, cols):
    rows = -(-flat.shape[-1] // cols)
    rows = -(-rows // align_rows) * align_rows
    pad = rows * cols - flat.shape[-1]
    flat = jnp.pad(flat, [(0, 0)] * (flat.ndim - 1) + [(0, pad)])
    return flat.reshape(flat.shape[:-1] + (rows, cols))


def _pack_weight_shards(shards):
    parts = []
    for name, _, as_bf16 in SHARDED:
        w = shards[name].astype(F32).reshape(-1)
        parts.append(w.astype(BF16) if as_bf16 else lax.bitcast_convert_type(w, BF16).reshape(-1))
    return _pad_rows(jnp.concatenate(parts), PACK_ROW_ALIGN, PACK_COLS)


def _unpack_weights(gathered, shards):
    flat = gathered.reshape(N_CHIPS, -1)
    out, off = {}, 0
    for name, axis, as_bf16 in SHARDED:
        shp = shards[name].shape
        n = int(np.prod(shp))
        if as_bf16:
            t = flat[:, off:off + n]
            off += n
        else:
            t = lax.bitcast_convert_type(flat[:, off:off + 2 * n].reshape(N_CHIPS, n, 2), F32)
            off += 2 * n
        t = t.reshape((N_CHIPS,) + shp)
        out[name] = jnp.concatenate([t[j] for j in range(N_CHIPS)], axis=axis)
    return out


def _pack_grad_shards(grads, shards):
    parts = []
    for name, axis, _ in SHARDED:
        g = grads[name].reshape(shards[name].shape[:axis] + (N_CHIPS, shards[name].shape[axis])
                                + shards[name].shape[axis + 1:])
        parts.append(jnp.moveaxis(g, axis, 0).reshape(N_CHIPS, -1))
    return _pad_rows(jnp.concatenate(parts, axis=1), PACK_ROW_ALIGN, PACK_COLS)


def _unpack_grad_shard(reduced, shards):
    flat = reduced.reshape(-1)
    out, off = {}, 0
    for name, _, _ in SHARDED:
        shp = shards[name].shape
        n = int(np.prod(shp))
        out[name] = flat[off:off + n].reshape(shp)
        off += n
    return out


def _pack_replicated(grads, loss):
    parts = [_pad_rows(grads[name].astype(F32).reshape(-1), 1, LANE).reshape(-1) for name in REPLICATED]
    parts.append(_pad_rows(loss.reshape(-1)[:1], 1, LANE).reshape(-1))
    return _pad_rows(jnp.concatenate(parts), 8, LANE)


def _unpack_replicated(reduced, shapes):
    flat = reduced.reshape(-1)
    out, off = {}, 0
    for name in REPLICATED:
        n = int(np.prod(shapes[name]))
        out[name] = flat[off:off + n].reshape(shapes[name])
        off += -(-n // LANE) * LANE
    return out, flat[off]


def _ffn_fwd2(h, W, layer, tm, tmm):
    tag = str(layer)
    n = _rms_fwd(h, W["ffn_norm"][layer:layer + 1], BF16, tm, f"ffn_norm_{tag}")
    up = _mm(n, W["w_up"], "nn", F32, tmm, 512, f"ffn_up_{tag}", b_lead=layer)
    act = _ffn_act_fwd(up, W["ffn_conv_w_p"][layer], W["ffn_conv_b"][layer:layer + 1], tm, f"ffn_act_{tag}")
    out = _mm(act, W["w_down"], "nn", F32, tmm, 512, f"ffn_down_{tag}", add=h, b_lead=layer)
    return out, (n, up, act)


def _ffn_bwd2(h, W, layer, saved, dout, acc, tm, tmm):
    tag = str(layer)
    n, up, act = saved
    dact = _mm(dout, W["w_down"], "nt", F32, tmm, 256, f"ffn_dact_{tag}", b_lead=layer)
    dwd = _mm(act, dout, "tn", F32, D_FF // 2, 512, f"ffn_dwdown_{tag}",
              out=(layer, 2, None if acc is None else acc[1]))
    dup, dcw, dcb = _ffn_act_bwd(up, dact, W["ffn_conv_w_p"][layer], W["ffn_conv_b"][layer:layer + 1], tm,
                                 f"ffn_act_bwd_{tag}")
    dn = _mm_ffn_dn(dup, W["w_up"], layer, tm, 512, f"ffn_dn_{tag}")
    dwu = _mm_ffn_dwup(n, dup, layer, None if acc is None else acc[0], 512, D_FF // 2, f"ffn_dwup_{tag}")
    dh, dgain = _rms_bwd(h, W["ffn_norm"][layer:layer + 1], dn, dout, tm, f"ffn_norm_bwd_{tag}")
    return dh, (dwu, dwd), dict(gain=dgain, cw=dcw[:FFN_K], cb=dcb)


GATHER_FIRST = ("w_in", "w_out", "small")
GATHER_LATE = ("pool_w", "w_up", "w_down")
HOSTED = ("w_out", "pool_w", "w_up", "w_down")
LATE = ("w_in", "small")


def _local_step2(h0, tgt, W, n_real, cut_of):
    LP = h0.shape[0]
    tm, tmm = _row_tiles(LP)
    nb = LP // ATT_BLK
    G = {}
    n0 = _rms_fwd(h0, W["mix_norm_even"], BF16, tm, "mix_norm_even")
    proj = _mm(n0, W["w_in_p"], "nn", F32, tmm, 384, "in_proj")
    c = _fgate_fwd(proj, W["b_f_p"], "forget_gate")
    qT, kT, k_aug, vT = _attn_prep(proj, c, "attention_operands")
    gcuts = [cut_of[n] for n in GATHER_LATE]
    oT, lse, *gfull = _attn_fwd2(qT, k_aug, vT, "fox_attention", comm=(W["late_shards"], gcuts))
    W = dict(W)
    W.update(zip(GATHER_LATE, _gather_forward(gfull, gcuts)))
    u1, u = _conf_fwd(proj, W["conv_w_p"], W["conv_b"], W["ln_g"], W["ln_b"], tm, "conformer")
    cat = jnp.concatenate([_attn_rows(oT, 1.0, BF16, "attention_rows"), u], axis=1)
    h1 = _mm(cat, W["w_out"], "nn", F32, tmm, 512, "out_proj", add=h0)
    h2, ffn0 = _ffn_fwd2(h1, W, 0, tm, tmm)
    h3, dpool = _pool_fwd(h2, W["mix_norm_odd"], W["pool_w"], W["pool_b"], W["pool_scale"], tm, "pool_mixer")
    h4, ffn1 = _ffn_fwd2(h3, W, 1, tm, tmm)
    loss, dh4, G["final_norm"] = _loss_head(h4, W["final_norm"], tgt, n_real, tm, "loss_head")

    dh3, acc, g1 = _ffn_bwd2(h3, W, 1, ffn1, dh4, None, tm, tmm)
    dh2, G["pool_w"], G["pool_b"], G["pool_scale"], G["mix_norm_odd"] = _pool_bwd(
        h2, W["mix_norm_odd"], dpool, W["pool_w"], W["pool_b"], W["pool_scale"], dh3, tm, "pool_mixer_bwd")
    dh1, acc, g0 = _ffn_bwd2(h1, W, 0, ffn0, dh2, acc, tm, tmm)
    G["w_up"], G["w_down"] = acc
    G["ffn_norm"] = jnp.concatenate([g0["gain"], g1["gain"]], axis=0)
    G["ffn_conv_w"] = jnp.stack([g0["cw"], g1["cw"]])
    G["ffn_conv_b"] = jnp.concatenate([g0["cb"], g1["cb"]], axis=0)

    dcat = _mm(dh1, W["w_out"], "nt", F32, tmm, 512, "out_proj_dx")
    G["w_out"] = _mm(cat, dh1, "tn", F32, 512, 512, "out_proj_dw")
    dadg, dcw, G["conv_b"], G["ln_g"], G["ln_b"] = _conf_bwd(
        proj, u1, dcat, W["conv_w_p"], W["ln_g"], W["ln_b"], tm, "conformer_bwd")
    G["conv_w"] = dcw[:CONV_K]
    doT = _attn_cols(dcat, "attention_do_cols")
    hcuts = [cut_of[n] for n in HOSTED]
    hfull = [G[n] for n in HOSTED]
    hrecv = _pair_exchange2(hfull, hcuts, "grad_pair_exchange_early")
    hparts = [_pair_sum2(f, r, cut, PAIR_SUM_BLOCKS[n], "grad_pair_sum_" + n)
              for f, r, cut, n in zip(hfull, hrecv, hcuts, HOSTED)]
    dqT, dkT, dvT, *hothers = _attn_bwd2(qT, kT, k_aug, vT, oT, doT, lse, "fox_attention_bwd",
                                         comm=(hparts, hcuts))
    dfl, dbf = _fgate_bwd(proj, W["b_f_p"], _attn_dc(dqT, dkT, "attention_dc"), "forget_gate_bwd")
    G["b_f"] = dbf[:, :HEADS]
    dproj = jnp.concatenate([_attn_rows(dqT, HEAD_DIM ** -0.5, BF16, "attention_dq_rows"),
                             _attn_rows(dkT, 1.0, BF16, "attention_dk_rows"),
                             _attn_rows(dvT, 1.0, BF16, "attention_dv_rows"), dadg, dfl], axis=1)
    dn0 = _mm(dproj, W["w_in_p"], "nt", F32, tmm, 512, "in_proj_dx")
    G["w_in_p"] = _mm(n0, dproj, "tn", F32, 512, 896, "in_proj_dw")
    dh0, G["mix_norm_even"] = _rms_bwd(h0, W["mix_norm_even"], dn0, dh1, tm, "mix_norm_even_bwd")
    return loss, dh0, G, dict(zip(HOSTED, hparts)), dict(zip(HOSTED, hothers))


class _Cut:
    def __init__(self, full_shape, chip_dim, half_dim):
        self.full = tuple(full_shape)
        self.chip_dim, self.half_dim = chip_dim, half_dim
        self.chip_size = full_shape[chip_dim] // N_CHIPS
        self.half_size = full_shape[half_dim] // 2
        assert chip_dim != half_dim

    def shape(self, chip=False, half=False):
        s = list(self.full)
        if chip:
            s[self.chip_dim] = self.chip_size
        if half:
            s[self.half_dim] = self.half_size
        return tuple(s)

    def region(self, ref, chip=None, half=None):
        idx = [pl.ds(0, n) for n in ref.shape]
        if chip is not None:
            idx[self.chip_dim] = pl.ds(chip * self.chip_size, self.chip_size)
        if half is not None:
            idx[self.half_dim] = pl.ds(half * self.half_size, self.half_size)
        return ref.at[tuple(idx)]


SMALL_SHARDED = ("meta_tokens", "mix_norm_odd", "pool_b", "pool_scale", "conv_w", "ffn_conv_w")
SMALL_ROWS = 144


def _cuts():
    return {
        "w_in": _Cut((N_CHIPS, D_MODEL, IN_SHARD), 0, 1),
        "w_out": _Cut((D_MODEL, D_MODEL), 0, 1),
        "pool_w": _Cut((len(POOL_WINDOWS), POOL_G, POOL_G), 1, 0),
        "w_up": _Cut((2, D_MODEL, 2 * D_FF), 2, 1),
        "w_down": _Cut((2, D_FF, D_MODEL), 1, 2),
        "small": _Cut((N_CHIPS, SMALL_ROWS, LANE), 0, 1),
    }


COMM_ORDER = ("w_in", "w_out", "pool_w", "w_up", "w_down", "small")


def _remote(src, dst, send_sems, recv_sems, k, to):
    return pltpu.make_async_remote_copy(src_ref=src, dst_ref=dst, send_sem=send_sems.at[k],
                                        recv_sem=recv_sems.at[k], device_id=to, device_id_type=MESH)


def _gather_weights(shards, cuts):
    n = len(shards)

    def body(*refs):
        srcs, outs = refs[:n], refs[n:2 * n]
        send_sems, recv_sems = refs[2 * n:]
        x, y, c = _coords()
        me = 2 * x + y
        sibling = (x, y, 1 - c)
        chips = _other_chips(x, y)
        sends = []
        for t, cut in enumerate(cuts):
            push = _remote(srcs[t], cut.region(outs[t], chip=me), send_sems, recv_sems, 7 * t, sibling)
            push.start()
            sends.append(push)
            for kk, chip in enumerate(chips):
                cp = _remote(cut.region(srcs[t], half=c), cut.region(outs[t], chip=me, half=c),
                             send_sems, recv_sems, 7 * t + 1 + kk, (*chip, c))
                cp.start()
                sends.append(cp)
        for t, cut in enumerate(cuts):
            for kk, (px, py) in enumerate(chips):
                landed = cut.region(outs[t], chip=2 * px + py, half=c)
                _remote(landed, landed, send_sems, recv_sems, 7 * t + 1 + kk, sibling).wait_recv()
                fwd = _remote(landed, landed, send_sems, recv_sems, 7 * t + 4 + kk, sibling)
                fwd.start()
                sends.append(fwd)
        for t, cut in enumerate(cuts):
            mine = cut.region(outs[t], chip=me)
            _remote(mine, mine, send_sems, recv_sems, 7 * t, sibling).wait_recv()
            for kk, (px, py) in enumerate(chips):
                other = cut.region(outs[t], chip=2 * px + py, half=1 - c)
                _remote(other, other, send_sems, recv_sems, 7 * t + 4 + kk, sibling).wait_recv()
        for cp in sends:
            cp.wait_send()

    return pl.pallas_call(
        body, name="gather_weights",
        out_shape=tuple(jax.ShapeDtypeStruct(cut.full, s.dtype) for cut, s in zip(cuts, shards)),
        in_specs=[ANY] * n, out_specs=tuple([ANY] * n),
        scratch_shapes=[pltpu.SemaphoreType.DMA((7 * n,)), pltpu.SemaphoreType.DMA((7 * n,))],
    )(*shards)


def _gather_first_ops(srcs, outs, send_sems, recv_sems, cuts):
    x, y, c = _coords()
    me = 2 * x + y
    sibling = (x, y, 1 - c)
    chips = _other_chips(x, y)

    def copies():
        out = []
        for t, cut in enumerate(cuts):
            out.append(_remote(srcs[t], cut.region(outs[t], chip=me), send_sems, recv_sems, 4 * t, sibling))
            for kk, chip in enumerate(chips):
                out.append(_remote(cut.region(srcs[t], half=c), cut.region(outs[t], chip=me, half=c),
                                   send_sems, recv_sems, 4 * t + 1 + kk, (*chip, c)))
        return out

    def start():
        for cp in copies():
            cp.start()

    def wait():
        for t, cut in enumerate(cuts):
            mine = cut.region(outs[t], chip=me)
            _remote(mine, mine, send_sems, recv_sems, 4 * t, sibling).wait_recv()
            for kk, (px, py) in enumerate(chips):
                landed = cut.region(outs[t], chip=2 * px + py, half=c)
                _remote(landed, landed, send_sems, recv_sems, 4 * t + 1 + kk, sibling).wait_recv()
        for cp in copies():
            cp.wait_send()

    return start, wait


def _gather_forward(fulls, cuts):
    n = len(fulls)

    def body(*refs):
        outs = refs[n:2 * n]
        send_sems, recv_sems = refs[2 * n:]
        x, y, c = _coords()
        sibling = (x, y, 1 - c)
        chips = _other_chips(x, y)
        sends = []
        for t, cut in enumerate(cuts):
            for kk, (px, py) in enumerate(chips):
                landed = cut.region(outs[t], chip=2 * px + py, half=c)
                cp = _remote(landed, landed, send_sems, recv_sems, 3 * t + kk, sibling)
                cp.start()
                sends.append(cp)
        for t, cut in enumerate(cuts):
            for kk, (px, py) in enumerate(chips):
                other = cut.region(outs[t], chip=2 * px + py, half=1 - c)
                _remote(other, other, send_sems, recv_sems, 3 * t + kk, sibling).wait_recv()
        for cp in sends:
            cp.wait_send()

    return pl.pallas_call(
        body, name="gather_forward",
        out_shape=tuple(jax.ShapeDtypeStruct(f.shape, f.dtype) for f in fulls),
        in_specs=[ANY] * n, out_specs=tuple([ANY] * n), input_output_aliases={t: t for t in range(n)},
        scratch_shapes=[pltpu.SemaphoreType.DMA((3 * n,)), pltpu.SemaphoreType.DMA((3 * n,))],
    )(*fulls)


def _pair_exchange2(fulls, cuts, name):
    n = len(fulls)

    def body(*refs):
        srcs, outs = refs[:n], refs[n:2 * n]
        send_sems, recv_sems = refs[2 * n:]
        x, y, c = _coords()
        cps = [_remote(cut.region(srcs[t], half=1 - c), outs[t], send_sems, recv_sems, t, (x, y, 1 - c))
               for t, cut in enumerate(cuts)]
        for cp in cps:
            cp.start()
        for cp in cps:
            cp.wait()

    return pl.pallas_call(
        body, name=name,
        out_shape=tuple(jax.ShapeDtypeStruct(cut.shape(half=True), f.dtype) for cut, f in zip(cuts, fulls)),
        in_specs=[ANY] * n, out_specs=tuple([ANY] * n),
        scratch_shapes=[pltpu.SemaphoreType.DMA((n,)), pltpu.SemaphoreType.DMA((n,))],
    )(*fulls)


def _grid_of(shape, blk):
    assert all(s % b == 0 for s, b in zip(shape, blk)), (shape, blk)
    return tuple(s // b for s, b in zip(shape, blk))


def _pair_sum2(full, recv, cut, blk, name):
    hshape = cut.shape(half=True)
    grid = _grid_of(hshape, blk)
    hb = cut.half_size // blk[cut.half_dim]
    hd = cut.half_dim
    pos = jnp.stack([lax.axis_index("c")]).astype(jnp.int32)

    def full_idx(*a):
        ids, p = list(a[:-1]), a[-1]
        ids[hd] = ids[hd] + p[0] * hb
        return tuple(ids)

    def body(p_ref, f_ref, r_ref, o_ref):
        o_ref[...] = (f_ref[...] + r_ref[...]).astype(o_ref.dtype)

    return pl.pallas_call(
        body, name=name, out_shape=jax.ShapeDtypeStruct(hshape, BF16),
        grid_spec=pltpu.PrefetchScalarGridSpec(
            num_scalar_prefetch=1, grid=grid,
            in_specs=[pl.BlockSpec(blk, full_idx), pl.BlockSpec(blk, lambda *a: tuple(a[:-1]))],
            out_specs=pl.BlockSpec(blk, lambda *a: tuple(a[:-1]))),
        compiler_params=_params(("parallel",) * len(grid)))(pos, full, recv)


def _chip_exchange_ops(srcs, outs, send_sems, recv_sems, cuts):
    x, y, c = _coords()
    me = 2 * x + y
    chips = _other_chips(x, y)

    def copies():
        return [_remote(cut.region(srcs[t], chip=2 * px + py), outs[t].at[me], send_sems, recv_sems,
                        3 * t + kk, (px, py, c))
                for t, cut in enumerate(cuts) for kk, (px, py) in enumerate(chips)]

    def start():
        for cp in copies():
            cp.start()

    def wait():
        for t, cut in enumerate(cuts):
            for kk, (px, py) in enumerate(chips):
                slot = outs[t].at[2 * px + py]
                _remote(slot, slot, send_sems, recv_sems, 3 * t + kk, (px, py, c)).wait_recv()
        for cp in copies():
            cp.wait_send()

    return start, wait


def _chip_exchange_shapes(parts, cuts):
    return tuple(jax.ShapeDtypeStruct((N_CHIPS,) + cut.shape(chip=True, half=True), p.dtype)
                 for cut, p in zip(cuts, parts))


def _chip_exchange2(parts, cuts):
    n = len(parts)

    def body(*refs):
        start, wait = _chip_exchange_ops(refs[:n], refs[n:2 * n], refs[2 * n], refs[2 * n + 1], cuts)
        start()
        wait()

    return pl.pallas_call(
        body, name="grad_chip_exchange",
        out_shape=tuple(jax.ShapeDtypeStruct((N_CHIPS,) + cut.shape(chip=True, half=True), p.dtype)
                        for cut, p in zip(cuts, parts)),
        in_specs=[ANY] * n, out_specs=tuple([ANY] * n),
        scratch_shapes=[pltpu.SemaphoreType.DMA((3 * n,)), pltpu.SemaphoreType.DMA((3 * n,))],
    )(*parts)


def _chip_sum2(part, recv, cut, blk, name):
    bshape = cut.shape(chip=True, half=True)
    grid = _grid_of(bshape, blk)
    cb = cut.chip_size // blk[cut.chip_dim]
    hb = cut.half_size // blk[cut.half_dim]
    cd, hd = cut.chip_dim, cut.half_dim
    x, y, c = _coords()
    slots = [2 * px + py for px, py in _other_chips(x, y)]
    pos = jnp.stack([c, 2 * x + y] + slots).astype(jnp.int32)

    def part_idx(*a):
        ids, p = list(a[:-1]), a[-1]
        ids[cd] = ids[cd] + p[1] * cb
        return tuple(ids)

    def recv_idx(kk):
        return lambda *a: (a[-1][2 + kk],) + tuple(a[:-1])

    def out_idx(*a):
        ids, p = list(a[:-1]), a[-1]
        ids[hd] = ids[hd] + p[0] * hb
        return tuple(ids)

    def body(p_ref, own_ref, r0_ref, r1_ref, r2_ref, o_ref):
        acc = own_ref[...].astype(F32)
        for r_ref in (r0_ref, r1_ref, r2_ref):
            acc = acc + r_ref[...].astype(F32)
        o_ref[...] = acc

    return pl.pallas_call(
        body, name=name, out_shape=jax.ShapeDtypeStruct(cut.shape(chip=True), F32),
        grid_spec=pltpu.PrefetchScalarGridSpec(
            num_scalar_prefetch=1, grid=grid,
            in_specs=[pl.BlockSpec(blk, part_idx)] + [pl.BlockSpec((None,) + blk, recv_idx(kk)) for kk in range(3)],
            out_specs=pl.BlockSpec(blk, out_idx)),
        compiler_params=_params(("parallel",) * len(grid)))(pos, part, recv, recv, recv)


def _pair_swap2(blocks, cuts):
    n = len(blocks)

    def body(*refs):
        outs = refs[n:2 * n]
        send_sems, recv_sems = refs[2 * n:]
        x, y, c = _coords()
        cps = []
        for t, cut in enumerate(cuts):
            mine = cut.region(outs[t], half=c)
            cp = _remote(mine, mine, send_sems, recv_sems, t, (x, y, 1 - c))
            cp.start()
            cps.append(cp)
        for t, cut in enumerate(cuts):
            theirs = cut.region(outs[t], half=1 - c)
            _remote(theirs, theirs, send_sems, recv_sems, t, (x, y, 1 - c)).wait_recv()
        for cp in cps:
            cp.wait_send()

    return pl.pallas_call(
        body, name="grad_pair_swap",
        out_shape=tuple(jax.ShapeDtypeStruct(b.shape, b.dtype) for b in blocks),
        in_specs=[ANY] * n, out_specs=tuple([ANY] * n),
        input_output_aliases={t: t for t in range(n)},
        scratch_shapes=[pltpu.SemaphoreType.DMA((n,)), pltpu.SemaphoreType.DMA((n,))],
    )(*blocks)


PAIR_SUM_BLOCKS = {"w_in": (1, 512, IN_SHARD), "w_out": (512, 512), "pool_w": (1, POOL_G, POOL_G),
                   "w_up": (1, 64, 2 * D_FF), "w_down": (1, 704, 512), "small": (N_CHIPS, SMALL_ROWS // 2, LANE)}
CHIP_SUM_BLOCKS = {"w_in": (1, 512, IN_SHARD), "w_out": (256, 512), "pool_w": (2, 64, POOL_G),
                   "w_up": (1, 128, UP_SHARD), "w_down": (1, DOWN_SHARD, 512), "small": (1, SMALL_ROWS // 2, LANE)}


def _pack_small(P):
    parts = []
    for name in SMALL_SHARDED:
        t = P[name]
        parts.append(t.astype(F32))
    return parts


def _small_rows(t, lead):
    flat = t.reshape(lead + (-1,))
    pad = -flat.shape[-1] % LANE
    return jnp.pad(flat, [(0, 0)] * len(lead) + [(0, pad)]).reshape(lead + (-1, LANE))


def _pack_small_shards(shards):
    rows = jnp.concatenate([_small_rows(shards[n].astype(F32), ()) for n in SMALL_SHARDED], axis=0)
    return jnp.pad(rows, ((0, SMALL_ROWS - rows.shape[0]), (0, 0)))[None]


def _unpack_small(pack, shards, axes):
    out, off = {}, 0
    nchip = pack.shape[0]
    for name in SMALL_SHARDED:
        shp = shards[name].shape
        cnt = int(np.prod(shp))
        rows = -(-cnt // LANE)
        t = pack[:, off:off + rows].reshape(nchip, -1)[:, :cnt].reshape((nchip,) + shp)
        out[name] = jnp.concatenate([t[j] for j in range(nchip)], axis=axes[name])
        off += rows
    return out


def _pack_small_grads(grads, shards, axes):
    parts = []
    for name in SMALL_SHARDED:
        shp, ax = shards[name].shape, axes[name]
        g = grads[name].reshape(shp[:ax] + (N_CHIPS, shp[ax]) + shp[ax + 1:])
        parts.append(_small_rows(jnp.moveaxis(g, ax, 0), (N_CHIPS,)))
    rows = jnp.concatenate(parts, axis=1)
    return jnp.pad(rows, ((0, 0), (0, SMALL_ROWS - rows.shape[1]), (0, 0)))


SMALL_AXES = {"meta_tokens": 1, "mix_norm_odd": 1, "pool_b": 2, "pool_scale": 1, "conv_w": 2, "ffn_conv_w": 2}


WEIGHT_NAMES = ("meta_tokens", "mix_norm_even", "w_in", "b_f", "conv_w", "conv_b", "ln_g", "ln_b", "w_out",
                "mix_norm_odd", "pool_w", "pool_b", "pool_scale", "ffn_norm", "w_up", "ffn_conv_w",
                "ffn_conv_b", "w_down", "final_norm")


def kernel(x, meta_tokens, mix_norm_even, w_in, b_f, conv_w, conv_b, ln_g, ln_b, w_out, mix_norm_odd, pool_w, pool_b, pool_scale, ffn_norm, w_up, ffn_conv_w, ffn_conv_b, w_down, final_norm, loss_target, m_meta_tokens, m_mix_norm_even, m_w_in, m_b_f, m_conv_w, m_conv_b, m_ln_g, m_ln_b, m_w_out, m_mix_norm_odd, m_pool_w, m_pool_b, m_pool_scale, m_ffn_norm, m_w_up, m_ffn_conv_w, m_ffn_conv_b, m_w_down, m_final_norm, v_meta_tokens, v_mix_norm_even, v_w_in, v_b_f, v_conv_w, v_conv_b, v_ln_g, v_ln_b, v_w_out, v_mix_norm_odd, v_pool_w, v_pool_b, v_pool_scale, v_ffn_norm, v_w_up, v_ffn_conv_w, v_ffn_conv_b, v_w_down, v_final_norm):
    given = dict(locals())
    w_loc = {n: given[n] for n in WEIGHT_NAMES}
    m_loc = {n: given["m_" + n] for n in WEIGHT_NAMES}
    v_loc = {n: given["v_" + n] for n in WEIGHT_NAMES}
    cut_of = _cuts()
    cuts = [cut_of[n] for n in COMM_ORDER]
    big = ("w_in", "w_out", "pool_w", "w_up", "w_down")
    small_shards = {n: w_loc[n] for n in SMALL_SHARDED}

    shard_of = {n: w_loc[n].astype(BF16).reshape(cut_of[n].shape(chip=True)) for n in big}
    shard_of["small"] = _pack_small_shards(small_shards)
    g_in, g_out, g_small = _gather_weights([shard_of[n] for n in GATHER_FIRST], [cut_of[n] for n in GATHER_FIRST])
    g_pool = g_up = g_down = None
    full = _unpack_small(g_small, small_shards, SMALL_AXES)
    full.update({n: w_loc[n] for n in REPLICATED})
    w_in_full = g_in.transpose(1, 0, 2).reshape(D_MODEL, IN_COLS)
    qkv, f, ag = (w_in_full[:, :3 * FOX_W], w_in_full[:, 3 * FOX_W:3 * FOX_W + HEADS],
                  w_in_full[:, 3 * FOX_W + HEADS:])
    W = dict(
        mix_norm_even=full["mix_norm_even"].reshape(1, D_MODEL),
        w_in_p=jnp.concatenate([qkv, ag, f, jnp.zeros((D_MODEL, LANE - HEADS), BF16)], axis=1),
        b_f_p=jnp.pad(full["b_f"].reshape(1, HEADS), ((0, 0), (0, LANE - HEADS))),
        conv_w_p=jnp.pad(full["conv_w"].reshape(CONV_K, CONV_CH), ((0, CONV_HALO - CONV_K), (0, 0))),
        conv_b=full["conv_b"].reshape(1, CONV_CH), ln_g=full["ln_g"].reshape(1, CONV_CH),
        ln_b=full["ln_b"].reshape(1, CONV_CH), w_out=g_out,
        mix_norm_odd=full["mix_norm_odd"].reshape(1, D_MODEL), pool_w=g_pool,
        pool_b=full["pool_b"].reshape(1, D_MODEL), pool_scale=full["pool_scale"].reshape(1, D_MODEL),
        ffn_norm=full["ffn_norm"], w_up=g_up,
        ffn_conv_w_p=jnp.pad(full["ffn_conv_w"], ((0, 0), (0, 8 - FFN_K), (0, 0))),
        ffn_conv_b=full["ffn_conv_b"], w_down=g_down, final_norm=full["final_norm"].reshape(1, D_MODEL),
        late_shards=[shard_of[n] for n in GATHER_LATE])

    seq = x.shape[1]
    n_real = N_META + seq
    LP = -(-n_real // ATT_BLK) * ATT_BLK
    tail = jnp.zeros((LP - n_real, D_MODEL), F32)
    h0 = jnp.concatenate([full["meta_tokens"], x[0], tail], axis=0)
    tgt = jnp.concatenate([jnp.zeros((N_META, D_MODEL), F32), loss_target[0], tail], axis=0)
    loss_loc, dh0, G, parts, others = _local_step2(h0, tgt, W, n_real, cut_of)
    grad_x = dh0[N_META:n_real][None]
    G["meta_tokens"] = dh0[:N_META]

    rep_shapes = {n: w_loc[n].shape for n in REPLICATED}
    G["final_norm"] = G["final_norm"].reshape(D_MODEL)
    rep, loss = _unpack_replicated(_allreduce_small(_pack_replicated(G, loss_loc)), rep_shapes)

    gp = G["w_in_p"]
    g_w_in = jnp.concatenate([gp[:, :3 * FOX_W], gp[:, 3 * FOX_W + 2 * CONV_CH:3 * FOX_W + 2 * CONV_CH + HEADS],
                              gp[:, 3 * FOX_W:3 * FOX_W + 2 * CONV_CH]], axis=1)
    lcuts = [cut_of[n] for n in LATE]
    lfull = [g_w_in.reshape(D_MODEL, N_CHIPS, IN_SHARD).transpose(1, 0, 2),
             _pack_small_grads(G, small_shards, SMALL_AXES)]
    lrecv = _pair_exchange2(lfull, lcuts, "grad_pair_exchange_late")
    lparts = [_pair_sum2(f, r, cut, PAIR_SUM_BLOCKS[n], "grad_pair_sum_" + n)
              for f, r, cut, n in zip(lfull, lrecv, lcuts, LATE)]
    parts.update(zip(LATE, lparts))
    others.update(zip(LATE, _chip_exchange2(lparts, lcuts)))
    blocks = [_chip_sum2(parts[n], others[n], cut_of[n], CHIP_SUM_BLOCKS[n], "grad_chip_sum_" + n)
              for n in COMM_ORDER]
    blocks = _pair_swap2(blocks, cuts)
    gsh = {n: b.reshape(w_loc[n].shape) for n, b in zip(big, blocks[:5])}
    gsh.update(_unpack_small(blocks[5], small_shards, SMALL_AXES))
    sharded = set(big) | set(SMALL_SHARDED)

    grad_w = {n: (gsh[n] if n in sharded else rep[n]) for n in WEIGHT_NAMES}
    delta, new_m, new_v = {}, {}, {}
    for n in WEIGHT_NAMES:
        delta[n], new_m[n], new_v[n] = _adamw(w_loc[n], grad_w[n], m_loc[n], v_loc[n], "adamw_" + n)
    return (loss, grad_x, *[grad_w[n] for n in WEIGHT_NAMES], *[delta[n] for n in WEIGHT_NAMES],
            *[new_m[n] for n in WEIGHT_NAMES], *[new_v[n] for n in WEIGHT_NAMES])
```

```python
import functools

import numpy as np
import jax
import jax.numpy as jnp
from jax import lax
from jax.experimental import pallas as pl
from jax.experimental.pallas import tpu as pltpu

F32 = jnp.float32
BF16 = jnp.bfloat16

D_MODEL = 1024
N_META = 16
SEQ = 2048
HEADS = 8
HEAD_DIM = 64
FOX_W = HEADS * HEAD_DIM
CONV_CH = 512
CONV_K = 31
D_FF = 2816
POOL_WINDOWS = (2, 4, 8, 16)
POOL_G = 256
RMS_EPS = 1e-6
LN_EPS = 1e-5
IN_COLS = 3 * FOX_W + HEADS + 2 * CONV_CH
IN_COLS_P = 3 * FOX_W + 2 * CONV_CH + 128
F_COL_BLK = (3 * FOX_W + 2 * CONV_CH) // 128
N_CHIPS = 4
IN_SHARD = IN_COLS // N_CHIPS
UP_SHARD = 2 * D_FF // N_CHIPS
DOWN_SHARD = D_FF // N_CHIPS

ADAM_LR = 0.001
ADAM_B1 = 0.9
ADAM_B2 = 0.999
ADAM_EPS = 1e-08
ADAM_WD = 0.01
ADAM_STEP = 10

LANE = 128
ATT_BLK = 128
VMEM_LIMIT = 56 * 1024 * 1024

NEG = -1e30


def _sigmoid(x):
    return 0.5 * jnp.tanh(0.5 * x) + 0.5


def _sigmoid_tail(x):
    return 1.0 / (1.0 + jnp.exp(-x))


def _params(sem=None):
    return pltpu.CompilerParams(dimension_semantics=sem, vmem_limit_bytes=VMEM_LIMIT)


def _sub_rows(tm):
    best = 8
    for s in range(8, 137, 8):
        if tm % s == 0:
            best = s
    return best


def _mm(a, b, mode, out_dtype, tm, tn, name, add=None, a_lead=None, b_lead=None, out=None, host=None):
    a_shape = a.shape if a_lead is None else a.shape[1:]
    b_shape = b.shape if b_lead is None else b.shape[1:]
    if mode == "nn":
        (M, K), (K2, N) = a_shape, b_shape
        dims = (((1,), (0,)), ((), ()))
        a_blk, a_idx = (tm, K), (lambda i, j: (i, 0))
        b_blk, b_idx = (K, tn), (lambda i, j: (0, j))
    elif mode == "nt":
        (M, K), (N, K2) = a_shape, b_shape
        dims = (((1,), (1,)), ((), ()))
        a_blk, a_idx = (tm, K), (lambda i, j: (i, 0))
        b_blk, b_idx = (tn, K), (lambda i, j: (j, 0))
    else:
        (K, M), (K2, N) = a_shape, b_shape
        dims = (((0,), (0,)), ((), ()))
        a_blk, a_idx = (K, tm), (lambda i, j: (0, i))
        b_blk, b_idx = (K, tn), (lambda i, j: (0, j))
    assert K == K2 and M % tm == 0 and N % tn == 0, (name, a.shape, b.shape, tm, tn)
    gm, gn = M // tm, N // tn
    a_bytes = M * K * a.dtype.itemsize
    b_bytes = N * K * b.dtype.itemsize
    m_outer = a_bytes + b_bytes * gm <= b_bytes + a_bytes * gn
    if m_outer:
        grid = (gm, gn)
        wrap = lambda f: f
    else:
        grid = (gn, gm)
        wrap = lambda f: (lambda j, i: f(i, j))

    def lead(blk, idx, at):
        if at is None:
            return pl.BlockSpec(blk, wrap(idx))
        return pl.BlockSpec((None,) + blk, wrap(lambda i, j: (at,) + idx(i, j)))

    o_idx = lambda i, j: (i, j)
    in_specs = [lead(a_blk, a_idx, a_lead), lead(b_blk, b_idx, b_lead)]
    args = [a, b]
    if add is not None:
        in_specs.append(pl.BlockSpec((tm, tn), wrap(o_idx)))
        args.append(add)
    aliases = {}
    if out is None:
        out_shape = jax.ShapeDtypeStruct((M, N), out_dtype)
        out_spec = pl.BlockSpec((tm, tn), wrap(o_idx))
    else:
        o_lead, n_lead, into = out
        out_shape = jax.ShapeDtypeStruct((n_lead, M, N), out_dtype)
        out_spec = lead((tm, tn), o_idx, o_lead)
        if into is not None:
            aliases = {len(args): 0}
            in_specs.append(pl.BlockSpec(memory_space=pl.ANY))
            args.append(into)
    has_add = add is not None
    n_host = 0 if host is None else len(host[0])
    n_in = len(args)
    scratch = []
    if n_host:
        in_specs = in_specs + [pl.BlockSpec(memory_space=pl.ANY)] * n_host
        args = args + list(host[0])
        out_shape = (out_shape,) + tuple(jax.ShapeDtypeStruct(cut.full, s.dtype) for cut, s in zip(host[1], host[0]))
        out_spec = (out_spec,) + (pl.BlockSpec(memory_space=pl.ANY),) * n_host
        scratch = [pltpu.SemaphoreType.DMA((4 * n_host,)), pltpu.SemaphoreType.DMA((4 * n_host,))]

    def body(*refs):
        a_ref, b_ref = refs[0], refs[1]
        o_ref = refs[n_in + n_host]
        if n_host:
            start, wait = _gather_first_ops(refs[n_in:n_in + n_host], refs[n_in + n_host + 1:n_in + 2 * n_host + 1],
                                            refs[n_in + 2 * n_host + 1], refs[n_in + 2 * n_host + 2], host[1])
            pl.when(jnp.logical_and(pl.program_id(0) == 0, pl.program_id(1) == 0))(start)
        x = a_ref[...].astype(BF16)
        y = b_ref[...].astype(BF16)
        acc = lax.dot_general(x, y, dims, preferred_element_type=F32)
        if has_add:
            acc = acc + refs[2][...]
        o_ref[...] = acc.astype(o_ref.dtype)
        if n_host:
            pl.when(jnp.logical_and(pl.program_id(0) == grid[0] - 1, pl.program_id(1) == grid[1] - 1))(wait)

    sem = ("arbitrary", "arbitrary") if n_host else ("parallel", "parallel")
    return pl.pallas_call(
        body, name=name, out_shape=out_shape, grid=grid, in_specs=in_specs, out_specs=out_spec,
        scratch_shapes=scratch, input_output_aliases=aliases, compiler_params=_params(sem))(*args)


def _mm_ffn_dn(dup, w_up, tm, tn, name):
    _, LP, F = dup.shape
    Dm = w_up.shape[0]
    nt = (((1,), (1,)), ((), ()))

    def body(a_ref, b_ref, o_ref):
        acc = lax.dot_general(a_ref[0], b_ref[:, 0:F], nt, preferred_element_type=F32)
        acc = acc + lax.dot_general(a_ref[1], b_ref[:, F:2 * F], nt, preferred_element_type=F32)
        o_ref[...] = acc

    return pl.pallas_call(
        body, name=name, out_shape=jax.ShapeDtypeStruct((LP, Dm), F32), grid=(LP // tm, Dm // tn),
        in_specs=[pl.BlockSpec((2, tm, F), lambda i, j: (0, i, 0)),
                  pl.BlockSpec((tn, 2 * F), lambda i, j: (j, 0))],
        out_specs=pl.BlockSpec((tm, tn), lambda i, j: (i, j)),
        compiler_params=_params(("parallel", "parallel")))(dup, w_up)


def _mm_ffn_dwup(n, dup, layer, into, tk, tn, name):
    LP, Dm = n.shape
    F = dup.shape[2]
    nct = F // tn
    tdims = (((0,), (0,)), ((), ()))

    def body(a_ref, b_ref, *rest):
        rest[-1][...] = lax.dot_general(a_ref[...], b_ref[...], tdims, preferred_element_type=F32)

    in_specs = [pl.BlockSpec((LP, tk), lambda i, j: (0, i)),
                pl.BlockSpec((None, LP, tn), lambda i, j: (j // nct, 0, j % nct))]
    args = [n, dup]
    aliases = {}
    if into is not None:
        in_specs.append(pl.BlockSpec(memory_space=pl.ANY))
        args.append(into)
        aliases = {2: 0}
    return pl.pallas_call(
        body, name=name, out_shape=jax.ShapeDtypeStruct((2, Dm, 2 * F), F32), grid=(Dm // tk, 2 * nct),
        in_specs=in_specs, out_specs=pl.BlockSpec((None, tk, tn), lambda i, j: (layer, i, j)),
        input_output_aliases=aliases, compiler_params=_params(("parallel", "parallel")))(*args)


def _rms_fwd(h, g, out_dtype, tm, name):
    LP, Dm = h.shape

    def body(h_ref, g_ref, o_ref):
        x = h_ref[...]
        r = lax.rsqrt(jnp.mean(x * x, axis=1, keepdims=True) + RMS_EPS)
        o_ref[...] = (x * r * g_ref[...]).astype(o_ref.dtype)

    return pl.pallas_call(
        body, name=name, out_shape=jax.ShapeDtypeStruct((LP, Dm), out_dtype), grid=(LP // tm,),
        in_specs=[pl.BlockSpec((tm, Dm), lambda i: (i, 0)), pl.BlockSpec((1, Dm), lambda i: (0, 0))],
        out_specs=pl.BlockSpec((tm, Dm), lambda i: (i, 0)),
        compiler_params=_params(("parallel",)))(h, g)


def _rms_bwd(h, g, dn, dres, tm, name):
    LP, Dm = h.shape

    def body(h_ref, g_ref, dn_ref, dr_ref, dh_ref, dg_ref):
        i = pl.program_id(0)
        x = h_ref[...]
        r = lax.rsqrt(jnp.mean(x * x, axis=1, keepdims=True) + RMS_EPS)
        xhat = x * r
        dy = dn_ref[...]
        dxh = dy * g_ref[...]
        dh = r * (dxh - xhat * jnp.mean(dxh * xhat, axis=1, keepdims=True))
        dh_ref[...] = dr_ref[...] + dh

        @pl.when(i == 0)
        def _():
            dg_ref[...] = jnp.zeros_like(dg_ref)

        dg_ref[...] += jnp.sum(dy * xhat, axis=0, keepdims=True)

    row = pl.BlockSpec((tm, Dm), lambda i: (i, 0))
    vec = pl.BlockSpec((1, Dm), lambda i: (0, 0))
    return pl.pallas_call(
        body, name=name,
        out_shape=(jax.ShapeDtypeStruct((LP, Dm), F32), jax.ShapeDtypeStruct((1, Dm), F32)),
        grid=(LP // tm,), in_specs=[row, vec, row, row], out_specs=(row, vec),
        compiler_params=_params(("arbitrary",)))(h, g, dn, dres)


def _loss_head(h, g, tgt, n_real, tm, name):
    LP, Dm = h.shape

    def body(h_ref, g_ref, t_ref, loss_ref, dh_ref, dg_ref):
        i = pl.program_id(0)
        x = h_ref[...]
        gg = g_ref[...]
        r = lax.rsqrt(jnp.mean(x * x, axis=1, keepdims=True) + RMS_EPS)
        xhat = x * r
        rows = i * tm + lax.broadcasted_iota(jnp.int32, (tm, 1), 0)
        real = jnp.logical_and(rows >= N_META, rows < n_real)
        diff = jnp.where(real, xhat * gg - t_ref[...], 0.0)
        dy = diff * (1.0 / Dm)
        dxh = dy * gg
        dh_ref[...] = r * (dxh - xhat * jnp.mean(dxh * xhat, axis=1, keepdims=True))

        @pl.when(i == 0)
        def _():
            dg_ref[...] = jnp.zeros_like(dg_ref)
            loss_ref[...] = jnp.zeros_like(loss_ref)

        dg_ref[...] += jnp.sum(dy * xhat, axis=0, keepdims=True)
        part = jnp.sum(jnp.sum(diff * diff, axis=1, keepdims=True), axis=0, keepdims=True)
        loss_ref[...] += jnp.broadcast_to(part * (0.5 / Dm), loss_ref.shape)

    row = pl.BlockSpec((tm, Dm), lambda i: (i, 0))
    vec = pl.BlockSpec((1, Dm), lambda i: (0, 0))
    return pl.pallas_call(
        body, name=name,
        out_shape=(jax.ShapeDtypeStruct((1, LANE), F32), jax.ShapeDtypeStruct((LP, Dm), F32),
                   jax.ShapeDtypeStruct((1, Dm), F32)),
        grid=(LP // tm,), in_specs=[row, vec, row],
        out_specs=(pl.BlockSpec((1, LANE), lambda i: (0, 0)), row, vec),
        compiler_params=_params(("arbitrary",)))(h, g, tgt)


def _fgate_fwd(proj, bf_p, name):
    LP = proj.shape[0]
    nb = LP // LANE

    def body(f_ref, b_ref, c_ref, lf_ref):
        x = f_ref[...] + b_ref[...]
        lf_ref[...] = jnp.minimum(x, 0.0) - jnp.log1p(jnp.exp(-jnp.abs(x)))
        ri = lax.broadcasted_iota(jnp.int32, (LANE, LANE), 0)
        ci = lax.broadcasted_iota(jnp.int32, (LANE, LANE), 1)
        tri = jnp.where(ri >= ci, 1.0, 0.0).astype(F32)

        def blk(i, carry):
            rows = pl.ds(pl.multiple_of(i * LANE, LANE), LANE)
            cb = jnp.dot(tri, lf_ref[rows, :], precision=lax.Precision.HIGHEST,
                         preferred_element_type=F32) + carry
            c_ref[rows, :] = cb
            return cb[LANE - 1:LANE, :]

        lax.fori_loop(0, nb, blk, jnp.zeros((1, LANE), F32))

    return pl.pallas_call(
        body, name=name, out_shape=jax.ShapeDtypeStruct((LP, LANE), F32), grid=(1,),
        in_specs=[pl.BlockSpec((LP, LANE), lambda i: (0, F_COL_BLK)),
                  pl.BlockSpec((1, LANE), lambda i: (0, 0))],
        out_specs=pl.BlockSpec((LP, LANE), lambda i: (0, 0)),
        scratch_shapes=[pltpu.VMEM((LP, LANE), F32)],
        compiler_params=_params(("arbitrary",)))(proj, bf_p)


def _fgate_bwd(proj, bf_p, dc, name):
    LP = proj.shape[0]
    nb = LP // LANE

    def body(f_ref, b_ref, dc_ref, dl_ref, db_ref):
        ri = lax.broadcasted_iota(jnp.int32, (LANE, LANE), 0)
        ci = lax.broadcasted_iota(jnp.int32, (LANE, LANE), 1)
        triu = jnp.where(ri <= ci, 1.0, 0.0).astype(F32)
        bb = b_ref[...]

        tail = jnp.zeros((1, LANE), F32)
        dbs = jnp.zeros((1, LANE), F32)
        for i in range(nb - 1, -1, -1):
            rows = slice(i * LANE, (i + 1) * LANE)
            gb = jnp.dot(triu, dc_ref[rows, :], precision=lax.Precision.HIGHEST,
                         preferred_element_type=F32) + tail
            x = f_ref[rows, :] + bb
            dl = gb * _sigmoid_tail(-x)
            dl_ref[rows, :] = dl.astype(dl_ref.dtype)
            tail = gb[0:1, :]
            dbs = dbs + jnp.sum(dl, axis=0, keepdims=True)
        db_ref[...] = dbs

    return pl.pallas_call(
        body, name=name,
        out_shape=(jax.ShapeDtypeStruct((LP, LANE), BF16), jax.ShapeDtypeStruct((1, LANE), F32)),
        grid=(1,),
        in_specs=[pl.BlockSpec((LP, LANE), lambda i: (0, F_COL_BLK)),
                  pl.BlockSpec((1, LANE), lambda i: (0, 0)),
                  pl.BlockSpec((LP, LANE), lambda i: (0, 0))],
        out_specs=(pl.BlockSpec((LP, LANE), lambda i: (0, 0)), pl.BlockSpec((1, LANE), lambda i: (0, 0))),
        compiler_params=_params(("arbitrary",)))(proj, bf_p, dc)


def _attn_fwd(q, k, v, c_col, c_row, name):
    Hh, LP, Dh = q.shape
    nb = LP // ATT_BLK
    scale = Dh ** -0.5
    nt = (((1,), (1,)), ((), ()))

    def body(q_ref, k_ref, v_ref, cc_ref, cr_ref, o_ref, lse_ref):
        i = pl.program_id(1)
        qb = q_ref[...]
        cq = cc_ref[...]
        rows = i * ATT_BLK + lax.broadcasted_iota(jnp.int32, (ATT_BLK, ATT_BLK), 0)
        cols0 = lax.broadcasted_iota(jnp.int32, (ATT_BLK, ATT_BLK), 1)

        def step(j, carry):
            m, l, acc = carry
            ks = pl.ds(pl.multiple_of(j * ATT_BLK, ATT_BLK), ATT_BLK)
            s = lax.dot_general(qb, k_ref[ks, :], nt, preferred_element_type=F32) * scale
            s = s + cq - cr_ref[j]
            s = jnp.where(cols0 + j * ATT_BLK <= rows, s, NEG)
            m_new = jnp.maximum(m, jnp.max(s, axis=1, keepdims=True))
            p = jnp.exp(s - m_new)
            alpha = jnp.exp(m - m_new)
            l = alpha * l + jnp.sum(p, axis=1, keepdims=True)
            acc = alpha * acc + jnp.dot(p.astype(BF16), v_ref[ks, :], preferred_element_type=F32)
            return m_new, l, acc

        init = (jnp.full((ATT_BLK, 1), NEG, F32), jnp.zeros((ATT_BLK, 1), F32),
                jnp.zeros((ATT_BLK, Dh), F32))
        m, l, acc = lax.fori_loop(0, i + 1, step, init)
        o_ref[...] = acc / l
        lse_ref[...] = m + jnp.log(l)

    qspec = pl.BlockSpec((None, ATT_BLK, Dh), lambda h, i: (h, i, 0))
    kspec = pl.BlockSpec((None, LP, Dh), lambda h, i: (h, 0, 0))
    colspec = pl.BlockSpec((None, ATT_BLK, 1), lambda h, i: (h, i, 0))
    rowspec = pl.BlockSpec((None, nb, 1, ATT_BLK), lambda h, i: (h, 0, 0, 0))
    return pl.pallas_call(
        body, name=name,
        out_shape=(jax.ShapeDtypeStruct((Hh, LP, Dh), F32), jax.ShapeDtypeStruct((Hh, LP, 1), F32)),
        grid=(Hh, nb), in_specs=[qspec, kspec, kspec, colspec, rowspec],
        out_specs=(qspec, colspec),
        compiler_params=_params(("parallel", "arbitrary")))(q, k, v, c_col, c_row)


def _attn_bwd(q, k, v, o, do, lse, c_col, c_row, name):
    Hh, LP, Dh = q.shape
    nb = LP // ATT_BLK
    scale = Dh ** -0.5
    nt = (((1,), (1,)), ((), ()))
    tn = (((0,), (0,)), ((), ()))

    def body(q_ref, k_ref, v_ref, o_ref, do_ref, lse_ref, cc_ref, cr_ref,
             dq_ref, dk_ref, dv_ref, dcq_ref, dc_ref, delta_ref):
        j = pl.program_id(1)

        @pl.when(j == 0)
        def _():
            dq_ref[...] = jnp.zeros_like(dq_ref)
            dcq_ref[...] = jnp.zeros_like(dcq_ref)
            dob_all = do_ref[...].astype(BF16).astype(F32)
            delta_ref[...] = jnp.sum(dob_all * o_ref[...], axis=1, keepdims=True)

        kb = k_ref[...]
        vb = v_ref[...]
        ck = cr_ref[j]
        rows0 = lax.broadcasted_iota(jnp.int32, (ATT_BLK, ATT_BLK), 0)
        cols = j * ATT_BLK + lax.broadcasted_iota(jnp.int32, (ATT_BLK, ATT_BLK), 1)

        def step(i, carry):
            dk, dv, dcs = carry
            qs = pl.ds(pl.multiple_of(i * ATT_BLK, ATT_BLK), ATT_BLK)
            qb = q_ref[qs, :]
            dob = do_ref[qs, :].astype(BF16)
            s = lax.dot_general(qb, kb, nt, preferred_element_type=F32) * scale
            s = s + cc_ref[qs, :] - ck
            s = jnp.where(cols <= rows0 + i * ATT_BLK, s, NEG)
            p = jnp.exp(s - lse_ref[qs, :])
            dp = lax.dot_general(dob, vb, nt, preferred_element_type=F32)
            ds = p * (dp - delta_ref[qs, :])
            dsb = ds.astype(BF16)
            dv = dv + lax.dot_general(p.astype(BF16), dob, tn, preferred_element_type=F32)
            dk = dk + lax.dot_general(dsb, qb, tn, preferred_element_type=F32) * scale
            dq_ref[qs, :] += jnp.dot(dsb, kb, preferred_element_type=F32) * scale
            dcq_ref[qs, :] += jnp.sum(ds, axis=1, keepdims=True)
            dcs = dcs - jnp.sum(ds, axis=0, keepdims=True)
            return dk, dv, dcs

        init = (jnp.zeros((ATT_BLK, Dh), F32), jnp.zeros((ATT_BLK, Dh), F32),
                jnp.zeros((1, ATT_BLK), F32))
        dk, dv, dcs = lax.fori_loop(j, nb, step, init)
        dk_ref[...] = dk
        dv_ref[...] = dv
        dc_ref[...] = dcs

    full = pl.BlockSpec((None, LP, Dh), lambda h, j: (h, 0, 0))
    blk = pl.BlockSpec((None, ATT_BLK, Dh), lambda h, j: (h, j, 0))
    col = pl.BlockSpec((None, LP, 1), lambda h, j: (h, 0, 0))
    rowspec = pl.BlockSpec((None, nb, 1, ATT_BLK), lambda h, j: (h, 0, 0, 0))
    return pl.pallas_call(
        body, name=name,
        out_shape=(jax.ShapeDtypeStruct((Hh, LP, Dh), F32), jax.ShapeDtypeStruct((Hh, LP, Dh), F32),
                   jax.ShapeDtypeStruct((Hh, LP, Dh), F32), jax.ShapeDtypeStruct((Hh, LP, 1), F32),
                   jax.ShapeDtypeStruct((Hh, nb, 1, ATT_BLK), F32)),
        grid=(Hh, nb), in_specs=[full, blk, blk, full, full, col, col, rowspec],
        out_specs=(full, blk, blk, col, pl.BlockSpec((None, None, 1, ATT_BLK), lambda h, j: (h, j, 0, 0))),
        scratch_shapes=[pltpu.VMEM((LP, 1), F32)],
        compiler_params=_params(("parallel", "arbitrary")))(q, k, v, o, do, lse, c_col, c_row)


AUG = 128
ONES_IN_K = HEAD_DIM
ONES_IN_Q = HEAD_DIM + 3
ATT_HEADS_PER_STEP = 8
ATT_HEADS_PER_STEP_BWD = 8


def _attn_prep(proj, c, name):
    LP = proj.shape[0]
    nb = LP // ATT_BLK
    tail_rows = AUG - HEAD_DIM

    def body(q_ref, k_ref, v_ref, c_ref, qT_ref, kT_ref, ka_ref, vT_ref):
        qt = (q_ref[...] * (HEAD_DIM ** -0.5)).T
        kt = k_ref[...].T
        vt = v_ref[...].T
        ct = c_ref[...].T
        hi = ct.astype(BF16).astype(F32)
        r1 = ct - hi
        mid = r1.astype(BF16).astype(F32)
        lo = (r1 - mid).astype(BF16).astype(F32)
        row = lax.broadcasted_iota(jnp.int32, (tail_rows, ATT_BLK), 0)
        ones = jnp.where(row < 3, 1.0, 0.0)
        for h in range(HEADS):
            cparts = jnp.where(row == 0, hi[h:h + 1], jnp.where(row == 1, mid[h:h + 1],
                               jnp.where(row == 2, lo[h:h + 1], 0.0)))
            hs = slice(h * HEAD_DIM, (h + 1) * HEAD_DIM)
            q_tail = cparts + pltpu.roll(ones, 3, 0)
            k_tail = ones - pltpu.roll(cparts, 3, 0)
            qT_ref[h] = jnp.concatenate([qt[hs], q_tail], axis=0).astype(BF16)
            kfull = jnp.concatenate([kt[hs], k_tail], axis=0)
            kT_ref[h] = kfull.astype(BF16)
            ka_ref[h] = kfull.T.astype(BF16)
            vT_ref[h] = vt[hs].astype(BF16)

    col = lambda j: pl.BlockSpec((ATT_BLK, FOX_W), lambda i: (i, j))
    blk = lambda r: pl.BlockSpec((HEADS, None, r, ATT_BLK), lambda i: (0, i, 0, 0))
    return pl.pallas_call(
        body, name=name,
        out_shape=(jax.ShapeDtypeStruct((HEADS, nb, AUG, ATT_BLK), BF16),
                   jax.ShapeDtypeStruct((HEADS, nb, AUG, ATT_BLK), BF16),
                   jax.ShapeDtypeStruct((HEADS, LP, AUG), BF16),
                   jax.ShapeDtypeStruct((HEADS, nb, HEAD_DIM, ATT_BLK), BF16)),
        grid=(nb,), in_specs=[col(0), col(1), col(2), pl.BlockSpec((ATT_BLK, LANE), lambda i: (i, 0))],
        out_specs=(blk(AUG), blk(AUG), pl.BlockSpec((HEADS, ATT_BLK, AUG), lambda i: (0, i, 0)), blk(HEAD_DIM)),
        compiler_params=_params(("parallel",)))(proj, proj, proj, c)


def _attn_rows(xT, scale, out_dtype, name):
    Hh, nb, R, _ = xT.shape

    def body(x_ref, o_ref):
        stack = jnp.concatenate([x_ref[h, 0:HEAD_DIM, :] for h in range(Hh)], axis=0)
        o_ref[...] = (stack * scale).T.astype(o_ref.dtype)

    return pl.pallas_call(
        body, name=name, out_shape=jax.ShapeDtypeStruct((nb * ATT_BLK, Hh * HEAD_DIM), out_dtype), grid=(nb,),
        in_specs=[pl.BlockSpec((Hh, None, R, ATT_BLK), lambda i: (0, i, 0, 0))],
        out_specs=pl.BlockSpec((ATT_BLK, Hh * HEAD_DIM), lambda i: (i, 0)),
        compiler_params=_params(("parallel",)))(xT)


def _attn_cols(x, name):
    LP = x.shape[0]
    nb = LP // ATT_BLK

    def body(x_ref, o_ref):
        xt = x_ref[...].T
        for h in range(HEADS):
            o_ref[h] = xt[h * HEAD_DIM:(h + 1) * HEAD_DIM].astype(o_ref.dtype)

    return pl.pallas_call(
        body, name=name, out_shape=jax.ShapeDtypeStruct((HEADS, nb, HEAD_DIM, ATT_BLK), BF16), grid=(nb,),
        in_specs=[pl.BlockSpec((ATT_BLK, FOX_W), lambda i: (i, 0))],
        out_specs=pl.BlockSpec((HEADS, None, HEAD_DIM, ATT_BLK), lambda i: (0, i, 0, 0)),
        compiler_params=_params(("parallel",)))(x)


def _attn_dc(dqT, dkT, name):
    Hh, nb, _, _ = dqT.shape

    def body(q_ref, k_ref, o_ref):
        row = lax.broadcasted_iota(jnp.int32, (LANE, ATT_BLK), 0)
        acc = jnp.zeros((LANE, ATT_BLK), F32)
        for h in range(Hh):
            d = q_ref[h, ONES_IN_K:ONES_IN_K + 1, :] - k_ref[h, ONES_IN_Q:ONES_IN_Q + 1, :]
            acc = jnp.where(row == h, d, acc)
        o_ref[...] = acc.T

    spec = pl.BlockSpec((Hh, None, AUG, ATT_BLK), lambda i: (0, i, 0, 0))
    return pl.pallas_call(
        body, name=name, out_shape=jax.ShapeDtypeStruct((nb * ATT_BLK, LANE), F32), grid=(nb,),
        in_specs=[spec, spec], out_specs=pl.BlockSpec((ATT_BLK, LANE), lambda i: (i, 0)),
        compiler_params=_params(("parallel",)))(dqT, dkT)


def _attn_fwd2(qT, k_aug, vT, name, comm=None):
    Hh, nb, _, _ = qT.shape
    LP = nb * ATT_BLK
    Dh = vT.shape[2]
    HB = ATT_HEADS_PER_STEP
    n_comm = 0 if comm is None else len(comm[0])

    def body(*refs):
        q_ref, k_ref, v_ref = refs[:3]
        o_ref, lse_ref = refs[3 + n_comm:5 + n_comm]
        if n_comm:
            start, wait = _gather_first_ops(refs[3:3 + n_comm], refs[5 + n_comm:5 + 2 * n_comm],
                                            refs[5 + 2 * n_comm], refs[6 + 2 * n_comm], comm[1])
            pl.when(pl.program_id(0) == 0)(start)
        keys = lax.broadcasted_iota(jnp.int32, (ATT_BLK, ATT_BLK), 0)
        qrys = lax.broadcasted_iota(jnp.int32, (ATT_BLK, ATT_BLK), 1)
        causal = keys <= qrys

        def q_block(i, _):
            def tile(j, carry, masked):
                ks = pl.ds(pl.multiple_of(j * ATT_BLK, ATT_BLK), ATT_BLK)
                out = []
                for hh in range(HB):
                    m, l, acc = carry[hh]
                    s = jnp.dot(k_ref[hh, ks, :], q_ref[hh, i], preferred_element_type=F32)
                    if masked:
                        s = jnp.where(causal, s, NEG)
                    m_new = jnp.maximum(m, jnp.max(s, axis=0, keepdims=True))
                    p = jnp.exp(s - m_new)
                    alpha = jnp.exp(m - m_new)
                    l = alpha * l + jnp.sum(p, axis=0, keepdims=True)
                    acc = alpha * acc + jnp.dot(v_ref[hh, j], p.astype(BF16), preferred_element_type=F32)
                    out.append((m_new, l, acc))
                return tuple(out)

            init = tuple((jnp.full((1, ATT_BLK), NEG, F32), jnp.zeros((1, ATT_BLK), F32),
                          jnp.zeros((Dh, ATT_BLK), F32)) for _ in range(HB))
            carry = lax.fori_loop(0, i, lambda j, cr: tile(j, cr, False), init)
            carry = tile(i, carry, True)
            for hh in range(HB):
                m, l, acc = carry[hh]
                o_ref[hh, i] = acc / l
                lse_ref[hh, i] = m + jnp.log(l)
            return 0

        lax.fori_loop(0, nb, q_block, 0)
        if n_comm:
            pl.when(pl.program_id(0) == Hh // HB - 1)(wait)

    blk = lambda r: pl.BlockSpec((HB, nb, r, ATT_BLK), lambda h: (h, 0, 0, 0))
    out_shape = (jax.ShapeDtypeStruct((Hh, nb, Dh, ATT_BLK), F32), jax.ShapeDtypeStruct((Hh, nb, 1, ATT_BLK), F32))
    scratch = []
    args = [qT, k_aug, vT]
    if n_comm:
        out_shape += tuple(jax.ShapeDtypeStruct(cut.full, s.dtype) for cut, s in zip(comm[1], comm[0]))
        scratch = [pltpu.SemaphoreType.DMA((4 * n_comm,)), pltpu.SemaphoreType.DMA((4 * n_comm,))]
        args += list(comm[0])
    return pl.pallas_call(
        body, name=name, out_shape=out_shape, grid=(Hh // HB,),
        in_specs=[blk(AUG), pl.BlockSpec((HB, LP, AUG), lambda h: (h, 0, 0)), blk(Dh)] + [ANY] * n_comm,
        out_specs=(blk(Dh), blk(1)) + (ANY,) * n_comm, scratch_shapes=scratch,
        compiler_params=_params(("arbitrary",)))(*args)


def _attn_bwd2(qT, kT, k_aug, v, oT, doT, lse, name, comm=None):
    Hh, nb, _, _ = qT.shape
    LP = nb * ATT_BLK
    Dh = v.shape[2]
    nt = (((1,), (1,)), ((), ()))
    tn = (((0,), (0,)), ((), ()))

    HB = ATT_HEADS_PER_STEP_BWD
    n_comm = 0 if comm is None else len(comm[0])

    def body(*refs):
        q_ref, kt_ref, k_ref, v_ref, o_ref, do_ref, lse_ref = refs[:7]
        parts = refs[7:7 + n_comm]
        dq_ref, dk_ref, dv_ref = refs[7 + n_comm:10 + n_comm]
        others = refs[10 + n_comm:10 + 2 * n_comm]
        delta_ref = refs[10 + 2 * n_comm]
        if n_comm:
            start, wait = _chip_exchange_ops(parts, others, refs[11 + 2 * n_comm], refs[12 + 2 * n_comm], comm[1])
            pl.when(pl.program_id(0) == 0)(start)
        keys = lax.broadcasted_iota(jnp.int32, (ATT_BLK, ATT_BLK), 0)
        qrys = lax.broadcasted_iota(jnp.int32, (ATT_BLK, ATT_BLK), 1)
        causal = keys <= qrys

        def prep(i, _):
            for hh in range(HB):
                delta_ref[hh, i] = jnp.sum(do_ref[hh, i].astype(F32) * o_ref[hh, i], axis=0, keepdims=True)
                dq_ref[hh, i] = jnp.zeros((AUG, ATT_BLK), F32)
            return 0

        lax.fori_loop(0, nb, prep, 0)

        def kv_block(j, _):
            ks = pl.ds(pl.multiple_of(j * ATT_BLK, ATT_BLK), ATT_BLK)

            def tile(i, carry, masked):
                out, dq_new = [], []
                for hh in range(HB):
                    dk, dv = carry[hh]
                    qb = q_ref[hh, i]
                    dob = do_ref[hh, i]
                    s = jnp.dot(k_ref[hh, ks, :], qb, preferred_element_type=F32)
                    if masked:
                        s = jnp.where(causal, s, NEG)
                    p = jnp.exp(s - lse_ref[hh, i])
                    dp = lax.dot_general(v_ref[hh, j], dob, tn, preferred_element_type=F32)
                    ds = (p * (dp - delta_ref[hh, i])).astype(BF16)
                    dv = dv + lax.dot_general(dob, p.astype(BF16), nt, preferred_element_type=F32)
                    dk = dk + lax.dot_general(qb, ds, nt, preferred_element_type=F32)
                    dq_new.append(jnp.dot(kt_ref[hh, j], ds, preferred_element_type=F32))
                    out.append((dk, dv))
                for hh in range(HB):
                    dq_ref[hh, i] += dq_new[hh]
                return tuple(out)

            init = tuple((jnp.zeros((AUG, ATT_BLK), F32), jnp.zeros((Dh, ATT_BLK), F32)) for _ in range(HB))
            carry = tile(j, init, True)
            carry = lax.fori_loop(j + 1, nb, lambda i, cr: tile(i, cr, False), carry)
            for hh in range(HB):
                dk_ref[hh, j] = carry[hh][0]
                dv_ref[hh, j] = carry[hh][1]
            return 0

        lax.fori_loop(0, nb, kv_block, 0)
        if n_comm:
            pl.when(pl.program_id(0) == Hh // HB - 1)(wait)

    blk = lambda r: pl.BlockSpec((HB, nb, r, ATT_BLK), lambda h: (h, 0, 0, 0))
    row = lambda cols: pl.BlockSpec((HB, LP, cols), lambda h: (h, 0, 0))
    out_shape = (jax.ShapeDtypeStruct((Hh, nb, AUG, ATT_BLK), F32), jax.ShapeDtypeStruct((Hh, nb, AUG, ATT_BLK), F32),
                 jax.ShapeDtypeStruct((Hh, nb, Dh, ATT_BLK), F32))
    scratch = [pltpu.VMEM((HB, nb, 1, ATT_BLK), F32)]
    args = [qT, kT, k_aug, v, oT, doT, lse]
    if n_comm:
        out_shape += _chip_exchange_shapes(*comm)
        scratch += [pltpu.SemaphoreType.DMA((3 * n_comm,)), pltpu.SemaphoreType.DMA((3 * n_comm,))]
        args += list(comm[0])
    return pl.pallas_call(
        body, name=name, out_shape=out_shape, grid=(Hh // HB,),
        in_specs=[blk(AUG), blk(AUG), row(AUG), blk(Dh), blk(Dh), blk(Dh), blk(1)] + [ANY] * n_comm,
        out_specs=(blk(AUG), blk(AUG), blk(Dh)) + (ANY,) * n_comm,
        scratch_shapes=scratch,
        compiler_params=_params(("arbitrary",)))(*args)


CONV_HALO = 32
A_BLK = 3 * FOX_W // CONV_CH
G_BLK = A_BLK + 1


def _conf_fwd(proj, cw, cb, lg, lb, tm, name):
    LP = proj.shape[0]
    C = CONV_CH
    sub = _sub_rows(tm)
    hpb = tm // CONV_HALO

    def body(a_ref, g_ref, ah_ref, gh_ref, w_ref, cb_ref, lg_ref, lb_ref, u1_ref, u_ref, buf):
        r = pl.program_id(0)
        buf[CONV_HALO:CONV_HALO + tm, :] = a_ref[...] * _sigmoid(g_ref[...])
        buf[0:CONV_HALO, :] = jnp.where(r > 0, ah_ref[...] * _sigmoid(gh_ref[...]), 0.0)
        for s in range(tm // sub):
            for ct in range(C // LANE):
                ln = slice(ct * LANE, (ct + 1) * LANE)
                acc = jnp.broadcast_to(cb_ref[:, ln], (sub, LANE))
                for kk in range(CONV_K):
                    off = CONV_HALO + s * sub - (CONV_K - 1) + kk
                    acc = acc + w_ref[kk:kk + 1, ln] * buf[off:off + sub, ln]
                u1_ref[s * sub:(s + 1) * sub, ln] = acc
        u1 = u1_ref[...]
        mu = jnp.mean(u1, axis=1, keepdims=True)
        xc = u1 - mu
        var = jnp.mean(xc * xc, axis=1, keepdims=True)
        y = xc * lax.rsqrt(var + LN_EPS) * lg_ref[...] + lb_ref[...]
        u_ref[...] = (y * _sigmoid(y)).astype(u_ref.dtype)

    cur = lambda blk: pl.BlockSpec((tm, C), lambda r: (r, blk))
    halo = lambda blk: pl.BlockSpec((CONV_HALO, C), lambda r: (jnp.maximum(r * hpb - 1, 0), blk))
    vec = pl.BlockSpec((1, C), lambda r: (0, 0))
    out = pl.BlockSpec((tm, C), lambda r: (r, 0))
    return pl.pallas_call(
        body, name=name,
        out_shape=(jax.ShapeDtypeStruct((LP, C), F32), jax.ShapeDtypeStruct((LP, C), BF16)),
        grid=(LP // tm,),
        in_specs=[cur(A_BLK), cur(G_BLK), halo(A_BLK), halo(G_BLK),
                  pl.BlockSpec((CONV_HALO, C), lambda r: (0, 0)), vec, vec, vec],
        out_specs=(out, out),
        scratch_shapes=[pltpu.VMEM((CONV_HALO + tm, C), F32)],
        compiler_params=_params(("parallel",)))(proj, proj, proj, proj, cw, cb, lg, lb)


def _conf_bwd(proj, u1, dcat, cw, lg, lb, tm, name):
    LP = proj.shape[0]
    C = CONV_CH
    sub = _sub_rows(tm)
    hpb = tm // CONV_HALO
    nblk = LP // tm
    last_halo = LP // CONV_HALO - 1

    def body(a_ref, g_ref, ah_ref, gh_ref, u1_ref, u1n_ref, du_ref, dun_ref, w_ref, lg_ref, lb_ref,
             dadg_ref, dw_ref, dcb_ref, dlg_ref, dlb_ref, ubuf, dbuf, du0):
        r = pl.program_id(0)
        lgv = lg_ref[...]
        lbv = lb_ref[...]

        def ln_silu_bwd(u1v, duv):
            mu = jnp.mean(u1v, axis=1, keepdims=True)
            xc = u1v - mu
            rstd = lax.rsqrt(jnp.mean(xc * xc, axis=1, keepdims=True) + LN_EPS)
            xhat = xc * rstd
            y = xhat * lgv + lbv
            sg = _sigmoid(y)
            dy = duv * (sg * (1.0 + y * (1.0 - sg)))
            dxh = dy * lgv
            du1 = rstd * (dxh - jnp.mean(dxh, axis=1, keepdims=True)
                          - xhat * jnp.mean(dxh * xhat, axis=1, keepdims=True))
            return du1, dy, xhat

        @pl.when(r == 0)
        def _():
            dw_ref[...] = jnp.zeros_like(dw_ref)
            dcb_ref[...] = jnp.zeros_like(dcb_ref)
            dlg_ref[...] = jnp.zeros_like(dlg_ref)
            dlb_ref[...] = jnp.zeros_like(dlb_ref)

        du1, dy, xhat = ln_silu_bwd(u1_ref[...], du_ref[...])
        dlg_ref[...] += jnp.sum(dy * xhat, axis=0, keepdims=True)
        dlb_ref[...] += jnp.sum(dy, axis=0, keepdims=True)
        dcb_ref[...] += jnp.sum(du1, axis=0, keepdims=True)
        dbuf[0:tm, :] = du1
        du1n, _, _ = ln_silu_bwd(u1n_ref[...], dun_ref[...])
        dbuf[tm:tm + CONV_HALO, :] = jnp.where(r < nblk - 1, du1n, 0.0)
        ubuf[CONV_HALO:CONV_HALO + tm, :] = a_ref[...] * _sigmoid(g_ref[...])
        ubuf[0:CONV_HALO, :] = jnp.where(r > 0, ah_ref[...] * _sigmoid(gh_ref[...]), 0.0)

        for ct in range(C // LANE):
            ln = slice(ct * LANE, (ct + 1) * LANE)
            for s in range(tm // sub):
                d_here = dbuf[s * sub:(s + 1) * sub, ln]
                acc = jnp.zeros((sub, LANE), F32)
                for kk in range(CONV_K):
                    fo = s * sub + (CONV_K - 1) - kk
                    acc = acc + w_ref[kk:kk + 1, ln] * dbuf[fo:fo + sub, ln]
                    bo = CONV_HALO + s * sub - (CONV_K - 1) + kk
                    dw_ref[kk:kk + 1, ln] += jnp.sum(d_here * ubuf[bo:bo + sub, ln], axis=0, keepdims=True)
                du0[s * sub:(s + 1) * sub, ln] = acc
        a = a_ref[...]
        sg = _sigmoid(g_ref[...])
        d0 = du0[...]
        dadg_ref[:, 0:C] = (d0 * sg).astype(dadg_ref.dtype)
        dadg_ref[:, C:2 * C] = (d0 * a * sg * (1.0 - sg)).astype(dadg_ref.dtype)

    cur = lambda blk: pl.BlockSpec((tm, C), lambda r: (r, blk))
    prev = lambda blk: pl.BlockSpec((CONV_HALO, C), lambda r: (jnp.maximum(r * hpb - 1, 0), blk))
    nxt = lambda blk: pl.BlockSpec((CONV_HALO, C), lambda r: (jnp.minimum((r + 1) * hpb, last_halo), blk))
    vec = pl.BlockSpec((1, C), lambda r: (0, 0))
    wspec = pl.BlockSpec((CONV_HALO, C), lambda r: (0, 0))
    return pl.pallas_call(
        body, name=name,
        out_shape=(jax.ShapeDtypeStruct((LP, 2 * C), BF16), jax.ShapeDtypeStruct((CONV_HALO, C), F32),
                   jax.ShapeDtypeStruct((1, C), F32), jax.ShapeDtypeStruct((1, C), F32),
                   jax.ShapeDtypeStruct((1, C), F32)),
        grid=(nblk,),
        in_specs=[cur(A_BLK), cur(G_BLK), prev(A_BLK), prev(G_BLK), cur(0), nxt(0), cur(1), nxt(1),
                  wspec, vec, vec],
        out_specs=(pl.BlockSpec((tm, 2 * C), lambda r: (r, 0)), wspec, vec, vec, vec),
        scratch_shapes=[pltpu.VMEM((CONV_HALO + tm, C), F32), pltpu.VMEM((tm + CONV_HALO, C), F32),
                        pltpu.VMEM((tm, C), F32)],
        compiler_params=_params(("arbitrary",)))(proj, proj, proj, proj, u1, u1, dcat, dcat, cw, lg, lb)


FFN_HALO = 8
FFN_TC = 256
FFN_K = 3


def _ffn_conv(buf, w_ref, b_ref, s, sub, ln):
    acc = jnp.broadcast_to(b_ref[:, ln], (sub, LANE))
    for kk in range(FFN_K):
        off = FFN_HALO + s * sub - (FFN_K - 1) + kk
        acc = acc + w_ref[kk:kk + 1, ln] * buf[off:off + sub, ln]
    return acc


def _ffn_act_fwd(up, w, b, tm, name, host=None):
    LP, F = up.shape[0], up.shape[1] // 2
    upg = upv = up
    nct = F // FFN_TC
    sub = _sub_rows(tm)
    hpb = tm // FFN_HALO
    n_host = 0 if host is None else len(host[0])
    nrb = LP // tm

    def body(*refs):
        g_ref, v_ref, gh_ref, vh_ref, wg_ref, wv_ref, bg_ref, bv_ref = refs[:8]
        act_ref = refs[8 + n_host]
        gbuf, vbuf = refs[9 + 2 * n_host:11 + 2 * n_host]
        if n_host:
            start, wait = _gather_first_ops(refs[8:8 + n_host], refs[9 + n_host:9 + 2 * n_host],
                                            refs[11 + 2 * n_host], refs[12 + 2 * n_host], host[1])
            pl.when(jnp.logical_and(pl.program_id(0) == 0, pl.program_id(1) == 0))(start)
        r = pl.program_id(1)
        gbuf[FFN_HALO:FFN_HALO + tm, :] = g_ref[...]
        vbuf[FFN_HALO:FFN_HALO + tm, :] = v_ref[...]
        gbuf[0:FFN_HALO, :] = jnp.where(r > 0, gh_ref[...], 0.0)
        vbuf[0:FFN_HALO, :] = jnp.where(r > 0, vh_ref[...], 0.0)
        for s in range(tm // sub):
            for ct in range(FFN_TC // LANE):
                ln = slice(ct * LANE, (ct + 1) * LANE)
                gc = _ffn_conv(gbuf, wg_ref, bg_ref, s, sub, ln)
                vc = _ffn_conv(vbuf, wv_ref, bv_ref, s, sub, ln)
                act_ref[s * sub:(s + 1) * sub, ln] = (gc * _sigmoid(gc) * vc).astype(act_ref.dtype)
        if n_host:
            pl.when(jnp.logical_and(pl.program_id(0) == nct - 1, pl.program_id(1) == nrb - 1))(wait)

    cur = pl.BlockSpec((tm, FFN_TC), lambda c, r: (r, c))
    halo = pl.BlockSpec((FFN_HALO, FFN_TC), lambda c, r: (jnp.maximum(r * hpb - 1, 0), c))
    wg = pl.BlockSpec((8, FFN_TC), lambda c, r: (0, c))
    wv = pl.BlockSpec((8, FFN_TC), lambda c, r: (0, nct + c))
    bg = pl.BlockSpec((1, FFN_TC), lambda c, r: (0, c))
    bv = pl.BlockSpec((1, FFN_TC), lambda c, r: (0, nct + c))
    curv = pl.BlockSpec((tm, FFN_TC), lambda c, r: (r, nct + c))
    halov = pl.BlockSpec((FFN_HALO, FFN_TC), lambda c, r: (jnp.maximum(r * hpb - 1, 0), nct + c))
    out_shape = jax.ShapeDtypeStruct((LP, F), BF16)
    out_specs = cur
    in_specs = [cur, curv, halo, halov, wg, wv, bg, bv]
    args = [upg, upv, upg, upv, w, w, b, b]
    scratch = [pltpu.VMEM((FFN_HALO + tm, FFN_TC), F32)] * 2
    if n_host:
        anyspec = pl.BlockSpec(memory_space=pl.ANY)
        in_specs += [anyspec] * n_host
        args += list(host[0])
        out_shape = (out_shape,) + tuple(jax.ShapeDtypeStruct(cut.full, s.dtype) for cut, s in zip(host[1], host[0]))
        out_specs = (cur,) + (anyspec,) * n_host
        scratch += [pltpu.SemaphoreType.DMA((4 * n_host,)), pltpu.SemaphoreType.DMA((4 * n_host,))]
    sem = ("arbitrary", "arbitrary") if n_host else ("parallel", "parallel")
    return pl.pallas_call(
        body, name=name, out_shape=out_shape, grid=(nct, nrb), in_specs=in_specs, out_specs=out_specs,
        scratch_shapes=scratch, compiler_params=_params(sem))(*args)


def _ffn_act_bwd(up, dact, w, b, tm, name):
    LP, F = up.shape[0], up.shape[1] // 2
    upg = upv = up
    nct = F // FFN_TC
    sub = _sub_rows(tm)
    hpb = tm // FFN_HALO
    nblk = LP // tm
    last_halo = LP // FFN_HALO - 1
    TB = tm + 2 * FFN_HALO

    def body(g_ref, v_ref, gp_ref, vp_ref, gn_ref, vn_ref, da_ref, dan_ref,
             wg_ref, wv_ref, bg_ref, bv_ref,
             dup_ref, dwg_ref, dwv_ref, dbg_ref, dbv_ref, gbuf, vbuf, dgb, dvb):
        r = pl.program_id(1)
        dg_ref = dup_ref.at[0]
        dv_ref = dup_ref.at[1]
        first = r == 0
        last = r == nblk - 1

        @pl.when(first)
        def _():
            dwg_ref[...] = jnp.zeros_like(dwg_ref)
            dwv_ref[...] = jnp.zeros_like(dwv_ref)
            dbg_ref[...] = jnp.zeros_like(dbg_ref)
            dbv_ref[...] = jnp.zeros_like(dbv_ref)

        for buf, c_ref, p_ref, n_ref in ((gbuf, g_ref, gp_ref, gn_ref), (vbuf, v_ref, vp_ref, vn_ref)):
            buf[0:FFN_HALO, :] = jnp.where(first, 0.0, p_ref[...])
            buf[FFN_HALO:FFN_HALO + tm, :] = c_ref[...]
            buf[FFN_HALO + tm:TB, :] = jnp.where(last, 0.0, n_ref[...])

        def dconv(s0, nrows, ln, dact_v):
            xg = [gbuf[s0 - (FFN_K - 1) + kk:s0 - (FFN_K - 1) + kk + nrows, ln] for kk in range(FFN_K)]
            xv = [vbuf[s0 - (FFN_K - 1) + kk:s0 - (FFN_K - 1) + kk + nrows, ln] for kk in range(FFN_K)]
            gc = jnp.broadcast_to(bg_ref[:, ln], (nrows, LANE))
            vc = jnp.broadcast_to(bv_ref[:, ln], (nrows, LANE))
            for kk in range(FFN_K):
                gc = gc + wg_ref[kk:kk + 1, ln] * xg[kk]
                vc = vc + wv_ref[kk:kk + 1, ln] * xv[kk]
            sg = _sigmoid(gc)
            return dact_v * vc * (sg * (1.0 + gc * (1.0 - sg))), dact_v * (gc * sg), xg, xv

        colsum = lambda t: jnp.sum(t, axis=0, keepdims=True)
        for ct in range(FFN_TC // LANE):
            ln = slice(ct * LANE, (ct + 1) * LANE)
            zero = jnp.zeros((1, LANE), F32)
            dwg, dwv, dbg, dbv = [zero] * FFN_K, [zero] * FFN_K, zero, zero
            for s in range(tm // sub):
                dgc, dvc, xg, xv = dconv(FFN_HALO + s * sub, sub, ln, da_ref[s * sub:(s + 1) * sub, ln])
                dgb[s * sub:(s + 1) * sub, ln] = dgc
                dvb[s * sub:(s + 1) * sub, ln] = dvc
                dwg = [dwg[kk] + colsum(dgc * xg[kk]) for kk in range(FFN_K)]
                dwv = [dwv[kk] + colsum(dvc * xv[kk]) for kk in range(FFN_K)]
                dbg, dbv = dbg + colsum(dgc), dbv + colsum(dvc)
            for kk in range(FFN_K):
                dwg_ref[kk:kk + 1, ln] += dwg[kk]
                dwv_ref[kk:kk + 1, ln] += dwv[kk]
            dbg_ref[:, ln] += dbg
            dbv_ref[:, ln] += dbv
            dgc, dvc, _, _ = dconv(FFN_HALO + tm, FFN_HALO, ln, jnp.where(last, 0.0, dan_ref[:, ln]))
            dgb[tm:tm + FFN_HALO, ln] = dgc
            dvb[tm:tm + FFN_HALO, ln] = dvc
            for dbuf, w_ref, dout in ((dgb, wg_ref, dg_ref), (dvb, wv_ref, dv_ref)):
                for s in range(tm // sub):
                    acc = jnp.zeros((sub, LANE), F32)
                    for kk in range(FFN_K):
                        fo = s * sub + (FFN_K - 1) - kk
                        acc = acc + w_ref[kk:kk + 1, ln] * dbuf[fo:fo + sub, ln]
                    dout[s * sub:(s + 1) * sub, ln] = acc.astype(dout.dtype)

    cur = pl.BlockSpec((tm, FFN_TC), lambda c, r: (r, c))
    prev = pl.BlockSpec((FFN_HALO, FFN_TC), lambda c, r: (jnp.maximum(r * hpb - 1, 0), c))
    nxt = pl.BlockSpec((FFN_HALO, FFN_TC), lambda c, r: (jnp.minimum((r + 1) * hpb, last_halo), c))
    wg = pl.BlockSpec((8, FFN_TC), lambda c, r: (0, c))
    wv = pl.BlockSpec((8, FFN_TC), lambda c, r: (0, nct + c))
    bg = pl.BlockSpec((1, FFN_TC), lambda c, r: (0, c))
    bv = pl.BlockSpec((1, FFN_TC), lambda c, r: (0, nct + c))
    curv = pl.BlockSpec((tm, FFN_TC), lambda c, r: (r, nct + c))
    prevv = pl.BlockSpec((FFN_HALO, FFN_TC), lambda c, r: (jnp.maximum(r * hpb - 1, 0), nct + c))
    nxtv = pl.BlockSpec((FFN_HALO, FFN_TC), lambda c, r: (jnp.minimum((r + 1) * hpb, last_halo), nct + c))
    dup, dwg, dwv, dbg, dbv = pl.pallas_call(
        body, name=name,
        out_shape=(jax.ShapeDtypeStruct((2, LP, F), BF16),
                   jax.ShapeDtypeStruct((8, F), F32), jax.ShapeDtypeStruct((8, F), F32),
                   jax.ShapeDtypeStruct((1, F), F32), jax.ShapeDtypeStruct((1, F), F32)),
        grid=(nct, nblk),
        in_specs=[cur, curv, prev, prevv, nxt, nxtv, cur, nxt, wg, wv, bg, bv],
        out_specs=(pl.BlockSpec((2, tm, FFN_TC), lambda c, r: (0, r, c)),
                   pl.BlockSpec((8, FFN_TC), lambda c, r: (0, c)),
                   pl.BlockSpec((8, FFN_TC), lambda c, r: (0, c)),
                   pl.BlockSpec((1, FFN_TC), lambda c, r: (0, c)),
                   pl.BlockSpec((1, FFN_TC), lambda c, r: (0, c))),
        scratch_shapes=[pltpu.VMEM((TB, FFN_TC), F32), pltpu.VMEM((TB, FFN_TC), F32),
                        pltpu.VMEM((tm + FFN_HALO, FFN_TC), F32), pltpu.VMEM((tm + FFN_HALO, FFN_TC), F32)],
        compiler_params=_params(("parallel", "arbitrary")))(
            upg, upv, upg, upv, upg, upv, dact, dact, w, w, b, b)
    return dup, jnp.concatenate([dwg, dwv], axis=1), jnp.concatenate([dbg, dbv], axis=1)


POOL_HALO = 16


def _pool_fwd(h, g, pw, pb, ps, tm, name):
    LP, Dm = h.shape
    sub = _sub_rows(tm)
    hpb = tm // POOL_HALO

    def body(h_ref, hh_ref, g_ref, pw_ref, pb_ref, ps_ref, o_ref, d_ref, buf):
        r = pl.program_id(0)
        gg = g_ref[...]

        def norm(x):
            return x * lax.rsqrt(jnp.mean(x * x, axis=1, keepdims=True) + RMS_EPS) * gg

        x = h_ref[...]
        buf[POOL_HALO:POOL_HALO + tm, :] = norm(x)
        buf[0:POOL_HALO, :] = jnp.where(r > 0, norm(hh_ref[...]), 0.0)
        for gi, w in enumerate(POOL_WINDOWS):
            ln = slice(gi * POOL_G, (gi + 1) * POOL_G)
            for s in range(tm // sub):
                base = POOL_HALO + s * sub
                acc = buf[base:base + sub, ln]
                for jj in range(1, w):
                    acc = acc + buf[base - jj:base - jj + sub, ln]
                t = r * tm + s * sub + lax.broadcasted_iota(jnp.int32, (sub, 1), 0)
                cnt = jnp.minimum(t + 1, w).astype(F32)
                d_ref[s * sub:(s + 1) * sub, ln] = (acc / cnt - buf[base:base + sub, ln]).astype(d_ref.dtype)
            y = jnp.dot(d_ref[:, ln], pw_ref[gi], preferred_element_type=F32) + pb_ref[:, ln]
            o_ref[:, ln] = x[:, ln] + y * ps_ref[:, ln]

    row = pl.BlockSpec((tm, Dm), lambda r: (r, 0))
    halo = pl.BlockSpec((POOL_HALO, Dm), lambda r: (jnp.maximum(r * hpb - 1, 0), 0))
    vec = pl.BlockSpec((1, Dm), lambda r: (0, 0))
    wsp = pl.BlockSpec((len(POOL_WINDOWS), POOL_G, POOL_G), lambda r: (0, 0, 0))
    return pl.pallas_call(
        body, name=name,
        out_shape=(jax.ShapeDtypeStruct((LP, Dm), F32), jax.ShapeDtypeStruct((LP, Dm), BF16)),
        grid=(LP // tm,), in_specs=[row, halo, vec, wsp, vec, vec], out_specs=(row, row),
        scratch_shapes=[pltpu.VMEM((POOL_HALO + tm, Dm), F32)],
        compiler_params=_params(("parallel",)))(h, h, g, pw, pb, ps)


def _pool_bwd(h, g, d, pw, pb, ps, dh_out, tm, name):
    LP, Dm = h.shape
    sub = _sub_rows(tm)
    hpb = tm // POOL_HALO
    nblk = LP // tm
    last_halo = LP // POOL_HALO - 1
    nt = (((1,), (1,)), ((), ()))
    tn = (((0,), (0,)), ((), ()))

    def body(h_ref, g_ref, d_ref, pw_ref, pb_ref, ps_ref, do_ref, don_ref,
             dh_ref, dpw_ref, dpb_ref, dps_ref, dg_ref, ebuf, ddb, dnb):
        r = pl.program_id(0)

        @pl.when(r == 0)
        def _():
            dpw_ref[...] = jnp.zeros_like(dpw_ref)
            dpb_ref[...] = jnp.zeros_like(dpb_ref)
            dps_ref[...] = jnp.zeros_like(dps_ref)
            dg_ref[...] = jnp.zeros_like(dg_ref)

        for gi, w in enumerate(POOL_WINDOWS):
            ln = slice(gi * POOL_G, (gi + 1) * POOL_G)
            wg = pw_ref[gi]
            dog = do_ref[:, ln]
            dg_b = d_ref[:, ln]
            y_pre = jnp.dot(dg_b, wg, preferred_element_type=F32) + pb_ref[:, ln]
            dps_ref[:, ln] += jnp.sum(dog * y_pre, axis=0, keepdims=True)
            dy = dog * ps_ref[:, ln]
            dpb_ref[:, ln] += jnp.sum(dy, axis=0, keepdims=True)
            dyb = dy.astype(BF16)
            dpw_ref[gi] += lax.dot_general(dg_b, dyb, tn, preferred_element_type=F32)
            dd = lax.dot_general(dyb, wg, nt, preferred_element_type=F32)
            ddb[:, ln] = dd
            t = r * tm + lax.broadcasted_iota(jnp.int32, (tm, 1), 0)
            ebuf[0:tm, ln] = dd / jnp.minimum(t + 1, w).astype(F32)
            dyn = (don_ref[:, ln] * ps_ref[:, ln]).astype(BF16)
            ddn = lax.dot_general(dyn, wg, nt, preferred_element_type=F32)
            tn_ = (r + 1) * tm + lax.broadcasted_iota(jnp.int32, (POOL_HALO, 1), 0)
            ebuf[tm:tm + POOL_HALO, ln] = jnp.where(r < nblk - 1, ddn / jnp.minimum(tn_ + 1, w).astype(F32), 0.0)
            for s in range(tm // sub):
                acc = ebuf[s * sub:(s + 1) * sub, ln]
                for jj in range(1, w):
                    acc = acc + ebuf[s * sub + jj:s * sub + jj + sub, ln]
                dnb[s * sub:(s + 1) * sub, ln] = acc - ddb[s * sub:(s + 1) * sub, ln]
        x = h_ref[...]
        rr = lax.rsqrt(jnp.mean(x * x, axis=1, keepdims=True) + RMS_EPS)
        xhat = x * rr
        dn = dnb[...]
        dxh = dn * g_ref[...]
        dh_ref[...] = do_ref[...] + rr * (dxh - xhat * jnp.mean(dxh * xhat, axis=1, keepdims=True))
        dg_ref[...] += jnp.sum(dn * xhat, axis=0, keepdims=True)

    row = pl.BlockSpec((tm, Dm), lambda r: (r, 0))
    nxt = pl.BlockSpec((POOL_HALO, Dm), lambda r: (jnp.minimum((r + 1) * hpb, last_halo), 0))
    vec = pl.BlockSpec((1, Dm), lambda r: (0, 0))
    wsp = pl.BlockSpec((len(POOL_WINDOWS), POOL_G, POOL_G), lambda r: (0, 0, 0))
    return pl.pallas_call(
        body, name=name,
        out_shape=(jax.ShapeDtypeStruct((LP, Dm), F32),
                   jax.ShapeDtypeStruct((len(POOL_WINDOWS), POOL_G, POOL_G), F32),
                   jax.ShapeDtypeStruct((1, Dm), F32), jax.ShapeDtypeStruct((1, Dm), F32),
                   jax.ShapeDtypeStruct((1, Dm), F32)),
        grid=(nblk,), in_specs=[row, vec, row, wsp, vec, vec, row, nxt],
        out_specs=(row, wsp, vec, vec, vec),
        scratch_shapes=[pltpu.VMEM((tm + POOL_HALO, Dm), F32), pltpu.VMEM((tm, Dm), F32),
                        pltpu.VMEM((tm, Dm), F32)],
        compiler_params=_params(("arbitrary",)))(h, g, d, pw, pb, ps, dh_out, dh_out)


def _adamw(w, g, m, v, name):
    shape = w.shape
    cols = shape[-1]
    rows = int(np.prod(shape[:-1])) if len(shape) > 1 else 1
    w2, g2, m2, v2 = (t.reshape(rows, cols) for t in (w, g, m, v))
    tr = rows
    for cand in (256, 128, 64, 32, 16, 8):
        if rows % cand == 0 and rows > cand:
            tr = cand
            break
    c1 = float(1.0 - ADAM_B1 ** ADAM_STEP)
    c2 = float(1.0 - ADAM_B2 ** ADAM_STEP)

    def body(w_ref, g_ref, m_ref, v_ref, d_ref, mo_ref, vo_ref):
        gg = g_ref[...]
        mn = ADAM_B1 * m_ref[...] + (1.0 - ADAM_B1) * gg
        vn = ADAM_B2 * v_ref[...] + (1.0 - ADAM_B2) * (gg * gg)
        m_hat = mn / c1
        v_hat = vn / c2
        d_ref[...] = -ADAM_LR * (m_hat / (jnp.sqrt(v_hat) + ADAM_EPS) + ADAM_WD * w_ref[...])
        mo_ref[...] = mn
        vo_ref[...] = vn

    spec = pl.BlockSpec((tr, cols), lambda i: (i, 0))
    sds = jax.ShapeDtypeStruct((rows, cols), F32)
    d2, mo, vo = pl.pallas_call(
        body, name=name, out_shape=(sds, sds, sds), grid=(rows // tr,),
        in_specs=[spec] * 4, out_specs=(spec,) * 3,
        compiler_params=_params(("parallel",)))(w2, g2, m2, v2)
    return d2.reshape(shape), mo.reshape(shape), vo.reshape(shape)


def _row_tiles(LP):
    tm = LP // 4
    assert LP % 4 == 0 and tm % CONV_HALO == 0 and LP % ATT_BLK == 0, LP
    return tm, LP // 2


def _heads(t, LP):
    return t.reshape(LP, HEADS, HEAD_DIM).transpose(1, 0, 2)


def _unheads(t, LP):
    return t.transpose(1, 0, 2).reshape(LP, FOX_W)


def _ffn_fwd(h, gain, wug, wuv, cw, cb, wd, tm, tmm, tag):
    n = _rms_fwd(h, gain, BF16, tm, f"ffn_norm_{tag}")
    upg = _mm(n, wug, "nn", F32, tmm, 256, f"ffn_up_gate_{tag}")
    upv = _mm(n, wuv, "nn", F32, tmm, 256, f"ffn_up_val_{tag}")
    act = _ffn_act_fwd(upg, upv, cw, cb, tm, f"ffn_act_{tag}")
    out = _mm(act, wd, "nn", F32, tmm, 512, f"ffn_down_{tag}", add=h)
    return out, (n, upg, upv, act)


def _ffn_bwd(h, gain, wug, wuv, cw, cb, wd, saved, dout, tm, tmm, tag):
    n, upg, upv, act = saved
    dact = _mm(dout, wd, "nt", F32, tmm, 256, f"ffn_dact_{tag}")
    dwd = _mm(act, dout, "tn", F32, 256, 512, f"ffn_dwdown_{tag}")
    dupg, dupv, dcw, dcb = _ffn_act_bwd(upg, upv, dact, cw, cb, tm, f"ffn_act_bwd_{tag}")
    dn = _mm(dupg, wug, "nt", F32, tm, 512, f"ffn_dn_gate_{tag}")
    dn = _mm(dupv, wuv, "nt", F32, tm, 512, f"ffn_dn_val_{tag}", add=dn)
    dwug = _mm(n, dupg, "tn", F32, 512, 256, f"ffn_dwup_gate_{tag}")
    dwuv = _mm(n, dupv, "tn", F32, 512, 256, f"ffn_dwup_val_{tag}")
    dh, dgain = _rms_bwd(h, gain, dn, dout, tm, f"ffn_norm_bwd_{tag}")
    return dh, dict(gain=dgain, wug=dwug, wuv=dwuv, cw=dcw[:FFN_K], cb=dcb, wd=dwd)


def _local_step(h0, tgt, W, n_real):
    LP = h0.shape[0]
    tm, tmm = _row_tiles(LP)
    nb = LP // ATT_BLK
    G = {}

    n0 = _rms_fwd(h0, W["mix_norm_even"], BF16, tm, "mix_norm_even")
    proj = _mm(n0, W["w_in_p"], "nn", F32, tmm, 384, "in_proj")
    c = _fgate_fwd(proj, W["b_f_p"], "forget_gate")
    cT = c[:, :HEADS].T
    c_col = cT[:, :, None]
    c_row = cT.reshape(HEADS, nb, 1, ATT_BLK)
    qkv = proj[:, :3 * FOX_W].astype(BF16)
    q, k, v = (_heads(qkv[:, i * FOX_W:(i + 1) * FOX_W], LP) for i in range(3))
    o, lse = _attn_fwd(q, k, v, c_col, c_row, "fox_attention")
    u1, u = _conf_fwd(proj, W["conv_w_p"], W["conv_b"], W["ln_g"], W["ln_b"], tm, "conformer")
    cat = jnp.concatenate([_unheads(o, LP).astype(BF16), u], axis=1)
    h1 = _mm(cat, W["w_out"], "nn", F32, tmm, 512, "out_proj", add=h0)
    h2, ffn0 = _ffn_fwd(h1, W["ffn_norm"][0:1], W["w_up_g"][0], W["w_up_v"][0], W["ffn_conv_w_p"][0],
                        W["ffn_conv_b"][0:1], W["w_down"][0], tm, tmm, "0")
    h3, dpool = _pool_fwd(h2, W["mix_norm_odd"], W["pool_w"], W["pool_b"], W["pool_scale"], tm, "pool_mixer")
    h4, ffn1 = _ffn_fwd(h3, W["ffn_norm"][1:2], W["w_up_g"][1], W["w_up_v"][1], W["ffn_conv_w_p"][1],
                        W["ffn_conv_b"][1:2], W["w_down"][1], tm, tmm, "1")
    loss, dh4, G["final_norm"] = _loss_head(h4, W["final_norm"], tgt, n_real, tm, "loss_head")

    dh3, g1 = _ffn_bwd(h3, W["ffn_norm"][1:2], W["w_up_g"][1], W["w_up_v"][1], W["ffn_conv_w_p"][1],
                       W["ffn_conv_b"][1:2], W["w_down"][1], ffn1, dh4, tm, tmm, "1")
    dh2, G["pool_w"], G["pool_b"], G["pool_scale"], G["mix_norm_odd"] = _pool_bwd(
        h2, W["mix_norm_odd"], dpool, W["pool_w"], W["pool_b"], W["pool_scale"], dh3, tm, "pool_mixer_bwd")
    dh1, g0 = _ffn_bwd(h1, W["ffn_norm"][0:1], W["w_up_g"][0], W["w_up_v"][0], W["ffn_conv_w_p"][0],
                       W["ffn_conv_b"][0:1], W["w_down"][0], ffn0, dh2, tm, tmm, "0")
    for key in ("gain", "wug", "wuv", "cw", "cb", "wd"):
        G["ffn_" + key] = (g0[key], g1[key])

    dcat = _mm(dh1, W["w_out"], "nt", F32, tmm, 512, "out_proj_dx")
    G["w_out"] = _mm(cat, dh1, "tn", F32, 512, 512, "out_proj_dw")
    dadg, dcw, G["conv_b"], G["ln_g"], G["ln_b"] = _conf_bwd(
        proj, u1, dcat, W["conv_w_p"], W["ln_g"], W["ln_b"], tm, "conformer_bwd")
    G["conv_w"] = dcw[:CONV_K]
    do = _heads(dcat[:, :FOX_W], LP)
    dq, dk, dv, dcq, dck = _attn_bwd(q, k, v, o, do, lse, c_col, c_row, "fox_attention_bwd")
    dc = jnp.pad((dcq.reshape(HEADS, LP) + dck.reshape(HEADS, LP)).T, ((0, 0), (0, LANE - HEADS)))
    dfl, dbf = _fgate_bwd(proj, W["b_f_p"], dc, "forget_gate_bwd")
    G["b_f"] = dbf[:, :HEADS]
    dproj = jnp.concatenate([_unheads(t, LP).astype(BF16) for t in (dq, dk, dv)] + [dadg, dfl], axis=1)
    dn0 = _mm(dproj, W["w_in_p"], "nt", F32, tmm, 512, "in_proj_dx")
    G["w_in_p"] = _mm(n0, dproj, "tn", F32, 512, 384, "in_proj_dw")
    dh0, G["mix_norm_even"] = _rms_bwd(h0, W["mix_norm_even"], dn0, dh1, tm, "mix_norm_even_bwd")
    return loss, dh0, G


def _compute_layout(P):
    w_in = P["w_in"].reshape(D_MODEL, IN_COLS)
    qkv, f, ag = w_in[:, :3 * FOX_W], w_in[:, 3 * FOX_W:3 * FOX_W + HEADS], w_in[:, 3 * FOX_W + HEADS:]
    w_in_p = jnp.concatenate([qkv, ag, f, jnp.zeros((D_MODEL, LANE - HEADS), w_in.dtype)], axis=1).astype(BF16)
    w_up = P["w_up"].astype(BF16)
    return dict(
        mix_norm_even=P["mix_norm_even"].reshape(1, D_MODEL).astype(F32),
        w_in_p=w_in_p,
        b_f_p=jnp.pad(P["b_f"].reshape(1, HEADS).astype(F32), ((0, 0), (0, LANE - HEADS))),
        conv_w_p=jnp.pad(P["conv_w"].reshape(CONV_K, CONV_CH).astype(F32), ((0, CONV_HALO - CONV_K), (0, 0))),
        conv_b=P["conv_b"].reshape(1, CONV_CH).astype(F32),
        ln_g=P["ln_g"].reshape(1, CONV_CH).astype(F32),
        ln_b=P["ln_b"].reshape(1, CONV_CH).astype(F32),
        w_out=P["w_out"].reshape(D_MODEL, D_MODEL).astype(BF16),
        mix_norm_odd=P["mix_norm_odd"].reshape(1, D_MODEL).astype(F32),
        pool_w=P["pool_w"].reshape(len(POOL_WINDOWS), POOL_G, POOL_G).astype(BF16),
        pool_b=P["pool_b"].reshape(1, D_MODEL).astype(F32),
        pool_scale=P["pool_scale"].reshape(1, D_MODEL).astype(F32),
        ffn_norm=P["ffn_norm"].astype(F32),
        w_up_g=w_up[:, :, :D_FF],
        w_up_v=w_up[:, :, D_FF:],
        ffn_conv_w_p=jnp.pad(P["ffn_conv_w"].astype(F32), ((0, 0), (0, 8 - FFN_K), (0, 0))),
        ffn_conv_b=P["ffn_conv_b"].astype(F32),
        w_down=P["w_down"].astype(BF16),
        final_norm=P["final_norm"].reshape(1, D_MODEL).astype(F32),
    )


def _reference_layout(G, dh0):
    gp = G["w_in_p"]
    g_w_in = jnp.concatenate([gp[:, :3 * FOX_W], gp[:, 3 * FOX_W + 2 * CONV_CH:3 * FOX_W + 2 * CONV_CH + HEADS],
                              gp[:, 3 * FOX_W:3 * FOX_W + 2 * CONV_CH]], axis=1)
    return dict(
        meta_tokens=dh0[:N_META],
        mix_norm_even=G["mix_norm_even"],
        w_in=g_w_in[None],
        b_f=G["b_f"],
        conv_w=G["conv_w"][None],
        conv_b=G["conv_b"],
        ln_g=G["ln_g"],
        ln_b=G["ln_b"],
        w_out=G["w_out"][None],
        mix_norm_odd=G["mix_norm_odd"],
        pool_w=G["pool_w"][None],
        pool_b=G["pool_b"].reshape(1, len(POOL_WINDOWS), POOL_G),
        pool_scale=G["pool_scale"],
        ffn_norm=jnp.concatenate(G["ffn_gain"], axis=0),
        w_up=jnp.stack([jnp.concatenate([g, v], axis=1) for g, v in zip(G["ffn_wug"], G["ffn_wuv"])]),
        ffn_conv_w=jnp.stack(G["ffn_cw"]),
        ffn_conv_b=jnp.concatenate(G["ffn_cb"], axis=0),
        w_down=jnp.stack(G["ffn_wd"]),
        final_norm=G["final_norm"].reshape(D_MODEL),
    )


MESH = pl.DeviceIdType.MESH
ANY = pl.BlockSpec(memory_space=pl.ANY)
PACK_COLS = 1024


def _coords():
    return lax.axis_index("x"), lax.axis_index("y"), lax.axis_index("c")


def _other_chips(x, y):
    return [(1 - x, y), (x, 1 - y), (1 - x, 1 - y)]


def _allgather_chips(pack):
    R, C = pack.shape
    R2 = R // 2

    def body(x_ref, o_ref, send_sems, recv_sems, local_sem):
        x, y, c = _coords()
        sibling = (x, y, 1 - c)
        chips = _other_chips(x, y)

        def slot(px, py, half):
            return o_ref.at[2 * px + py, pl.ds(half * R2, R2), :]

        def copy(k, src, dst, to):
            return pltpu.make_async_remote_copy(src_ref=src, dst_ref=dst, send_sem=send_sems.at[k],
                                                recv_sem=recv_sems.at[k], device_id=to, device_id_type=MESH)

        mine = pltpu.make_async_copy(x_ref, o_ref.at[2 * x + y], local_sem)
        mine.start()
        my_half = x_ref.at[pl.ds(c * R2, R2), :]
        first = [copy(j, my_half, slot(x, y, c), (*chip, c)) for j, chip in enumerate(chips)]
        for cp in first:
            cp.start()
        passed = [copy(3 + j, slot(*chip, c), slot(*chip, c), sibling) for j, chip in enumerate(chips)]
        for j, chip in enumerate(chips):
            copy(j, my_half, slot(*chip, c), sibling).wait_recv()
            passed[j].start()
        for j, chip in enumerate(chips):
            copy(3 + j, my_half, slot(*chip, 1 - c), sibling).wait_recv()
        for cp in first + passed:
            cp.wait_send()
        mine.wait()

    return pl.pallas_call(
        body, name="allgather_weights", out_shape=jax.ShapeDtypeStruct((N_CHIPS, R, C), pack.dtype),
        in_specs=[ANY], out_specs=ANY,
        scratch_shapes=[pltpu.SemaphoreType.DMA((6,)), pltpu.SemaphoreType.DMA((6,)), pltpu.SemaphoreType.DMA],
    )(pack)


def _pair_exchange(G):
    n, R, C = G.shape
    R2 = R // 2

    def body(g_ref, o_ref, send_sem, recv_sem):
        x, y, c = _coords()
        src = g_ref.at[pl.ds(0, n), pl.ds((1 - c) * R2, R2), :]
        cp = pltpu.make_async_remote_copy(src_ref=src, dst_ref=o_ref, send_sem=send_sem, recv_sem=recv_sem,
                                          device_id=(x, y, 1 - c), device_id_type=MESH)
        cp.start()
        cp.wait()

    return pl.pallas_call(
        body, name="grad_pair_exchange", out_shape=jax.ShapeDtypeStruct((n, R2, C), G.dtype),
        in_specs=[ANY], out_specs=ANY,
        scratch_shapes=[pltpu.SemaphoreType.DMA, pltpu.SemaphoreType.DMA],
    )(G)


def _row_tile(rows, align, cap):
    best = None
    for t in range(align, min(rows, cap) + 1, align):
        if rows % t == 0:
            best = t
    assert best is not None, (rows, align, cap)
    return best


def _pair_sum(G, recv):
    n, R, C = G.shape
    R2 = R // 2
    tr = _row_tile(R2, 16, 704)
    nrb = R2 // tr
    half = lax.axis_index("c").astype(jnp.int32).reshape(1)

    def body(c_ref, g_ref, r_ref, o_ref):
        o_ref[...] = (g_ref[...] + r_ref[...]).astype(o_ref.dtype)

    return pl.pallas_call(
        body, name="grad_pair_sum", out_shape=jax.ShapeDtypeStruct((n, R2, C), BF16),
        grid_spec=pltpu.PrefetchScalarGridSpec(
            num_scalar_prefetch=1, grid=(n, nrb),
            in_specs=[pl.BlockSpec((None, tr, C), lambda j, i, c_ref: (j, c_ref[0] * nrb + i, 0)),
                      pl.BlockSpec((None, tr, C), lambda j, i, c_ref: (j, i, 0))],
            out_specs=pl.BlockSpec((None, tr, C), lambda j, i, c_ref: (j, i, 0))),
        compiler_params=_params(("parallel", "parallel")))(half, G, recv)


def _chip_exchange(P):
    n, R2, C = P.shape

    def body(p_ref, o_ref, send_sems, recv_sems, local_sem):
        x, y, c = _coords()
        me = 2 * x + y
        chips = _other_chips(x, y)
        mine = pltpu.make_async_copy(p_ref.at[me], o_ref.at[me], local_sem)
        mine.start()
        sends = [pltpu.make_async_remote_copy(
            src_ref=p_ref.at[2 * px + py], dst_ref=o_ref.at[me], send_sem=send_sems.at[k],
            recv_sem=recv_sems.at[k], device_id=(px, py, c), device_id_type=MESH)
            for k, (px, py) in enumerate(chips)]
        for cp in sends:
            cp.start()
        for k, (px, py) in enumerate(chips):
            pltpu.make_async_remote_copy(
                src_ref=p_ref.at[me], dst_ref=o_ref.at[2 * px + py], send_sem=send_sems.at[k],
                recv_sem=recv_sems.at[k], device_id=(px, py, c), device_id_type=MESH).wait_recv()
        for cp in sends:
            cp.wait_send()
        mine.wait()

    return pl.pallas_call(
        body, name="grad_chip_exchange", out_shape=jax.ShapeDtypeStruct((n, R2, C), P.dtype),
        in_specs=[ANY], out_specs=ANY,
        scratch_shapes=[pltpu.SemaphoreType.DMA((3,)), pltpu.SemaphoreType.DMA((3,)), pltpu.SemaphoreType.DMA],
    )(P)


def _chip_sum(X):
    n, R2, C = X.shape
    tr = _row_tile(R2, 16, 704)

    def body(x_ref, o_ref):
        acc = x_ref[0].astype(F32)
        for s in range(1, n):
            acc = acc + x_ref[s].astype(F32)
        o_ref[...] = acc

    return pl.pallas_call(
        body, name="grad_chip_sum", out_shape=jax.ShapeDtypeStruct((R2, C), F32), grid=(R2 // tr,),
        in_specs=[pl.BlockSpec((n, tr, C), lambda i: (0, i, 0))],
        out_specs=pl.BlockSpec((tr, C), lambda i: (i, 0)),
        compiler_params=_params(("parallel",)))(X)


def _pair_allgather(Q):
    R2, C = Q.shape

    def body(q_ref, o_ref, send_sem, recv_sem, local_sem):
        x, y, c = _coords()
        mine = pltpu.make_async_copy(q_ref, o_ref.at[c], local_sem)
        mine.start()
        cp = pltpu.make_async_remote_copy(src_ref=q_ref, dst_ref=o_ref.at[c], send_sem=send_sem,
                                          recv_sem=recv_sem, device_id=(x, y, 1 - c), device_id_type=MESH)
        cp.start()
        pltpu.make_async_remote_copy(src_ref=q_ref, dst_ref=o_ref.at[1 - c], send_sem=send_sem,
                                     recv_sem=recv_sem, device_id=(x, y, 1 - c), device_id_type=MESH).wait_recv()
        cp.wait_send()
        mine.wait()

    return pl.pallas_call(
        body, name="grad_pair_allgather", out_shape=jax.ShapeDtypeStruct((2, R2, C), Q.dtype),
        in_specs=[ANY], out_specs=ANY,
        scratch_shapes=[pltpu.SemaphoreType.DMA, pltpu.SemaphoreType.DMA, pltpu.SemaphoreType.DMA],
    )(Q)


def _allreduce_small(pack):
    Rs, C = pack.shape
    n_dev = 8

    def body(x_ref, o_ref, buf, send_sems, recv_sems):
        x, y, c = _coords()
        me = 4 * x + 2 * y + c
        buf[me] = x_ref[...]
        peers = []
        for rel in range(1, n_dev):
            px = 1 - x if rel & 4 else x
            py = 1 - y if rel & 2 else y
            pc = 1 - c if rel & 1 else c
            peers.append((px, py, pc))
        sends = [pltpu.make_async_remote_copy(
            src_ref=x_ref, dst_ref=buf.at[me], send_sem=send_sems.at[k], recv_sem=recv_sems.at[k],
            device_id=peer, device_id_type=MESH) for k, peer in enumerate(peers)]
        for cp in sends:
            cp.start()
        for k, (px, py, pc) in enumerate(peers):
            pltpu.make_async_remote_copy(
                src_ref=x_ref, dst_ref=buf.at[4 * px + 2 * py + pc], send_sem=send_sems.at[k],
                recv_sem=recv_sems.at[k], device_id=(px, py, pc), device_id_type=MESH).wait_recv()
        for cp in sends:
            cp.wait_send()
        acc = buf[0]
        for d in range(1, n_dev):
            acc = acc + buf[d]
        o_ref[...] = acc

    vm = pl.BlockSpec(memory_space=pltpu.VMEM)
    return pl.pallas_call(
        body, name="allreduce_replicated", out_shape=jax.ShapeDtypeStruct((Rs, C), F32),
        in_specs=[vm], out_specs=vm,
        scratch_shapes=[pltpu.VMEM((n_dev, Rs, C), F32), pltpu.SemaphoreType.DMA((n_dev - 1,)),
                        pltpu.SemaphoreType.DMA((n_dev - 1,))],
    )(pack)


SHARDED = (
    ("w_in", 2, True), ("w_out", 1, True), ("pool_w", 2, True), ("w_up", 2, True), ("w_down", 1, True),
    ("meta_tokens", 1, False), ("mix_norm_odd", 1, False), ("pool_b", 2, False), ("pool_scale", 1, False),
    ("conv_w", 2, False), ("ffn_conv_w", 2, False))
REPLICATED = ("mix_norm_even", "b_f", "conv_b", "ln_g", "ln_b", "ffn_norm", "ffn_conv_b", "final_norm")
PACK_ROW_ALIGN = 32


def _pad_rows(flat, align_rows, cols):
    rows = -(-flat.shape[-1] // cols)
    rows = -(-rows // align_rows) * align_rows
    pad = rows * cols - flat.shape[-1]
    flat = jnp.pad(flat, [(0, 0)] * (flat.ndim - 1) + [(0, pad)])
    return flat.reshape(flat.shape[:-1] + (rows, cols))


def _pack_weight_shards(shards):
    parts = []
    for name, _, as_bf16 in SHARDED:
        w = shards[name].astype(F32).reshape(-1)
        parts.append(w.astype(BF16) if as_bf16 else lax.bitcast_convert_type(w, BF16).reshape(-1))
    return _pad_rows(jnp.concatenate(parts), PACK_ROW_ALIGN, PACK_COLS)


def _unpack_weights(gathered, shards):
    flat = gathered.reshape(N_CHIPS, -1)
    out, off = {}, 0
    for name, axis, as_bf16 in SHARDED:
        shp = shards[name].shape
        n = int(np.prod(shp))
        if as_bf16:
            t = flat[:, off:off + n]
            off += n
        else:
            t = lax.bitcast_convert_type(flat[:, off:off + 2 * n].reshape(N_CHIPS, n, 2), F32)
            off += 2 * n
        t = t.reshape((N_CHIPS,) + shp)
        out[name] = jnp.concatenate([t[j] for j in range(N_CHIPS)], axis=axis)
    return out


def _pack_grad_shards(grads, shards):
    parts = []
    for name, axis, _ in SHARDED:
        g = grads[name].reshape(shards[name].shape[:axis] + (N_CHIPS, shards[name].shape[axis])
                                + shards[name].shape[axis + 1:])
        parts.append(jnp.moveaxis(g, axis, 0).reshape(N_CHIPS, -1))
    return _pad_rows(jnp.concatenate(parts, axis=1), PACK_ROW_ALIGN, PACK_COLS)


def _unpack_grad_shard(reduced, shards):
    flat = reduced.reshape(-1)
    out, off = {}, 0
    for name, _, _ in SHARDED:
        shp = shards[name].shape
        n = int(np.prod(shp))
        out[name] = flat[off:off + n].reshape(shp)
        off += n
    return out


def _pack_replicated(grads, loss):
    parts = [_pad_rows(grads[name].astype(F32).reshape(-1), 1, LANE).reshape(-1) for name in REPLICATED]
    parts.append(_pad_rows(loss.reshape(-1)[:1], 1, LANE).reshape(-1))
    return _pad_rows(jnp.concatenate(parts), 8, LANE)


def _unpack_replicated(reduced, shapes):
    flat = reduced.reshape(-1)
    out, off = {}, 0
    for name in REPLICATED:
        n = int(np.prod(shapes[name]))
        out[name] = flat[off:off + n].reshape(shapes[name])
        off += -(-n // LANE) * LANE
    return out, flat[off]


def _ffn_fwd2(h, W, layer, tm, tmm, host_up=None, host_act=None):
    tag = str(layer)
    n = _rms_fwd(h, W["ffn_norm"][layer:layer + 1], BF16, tm, f"ffn_norm_{tag}")
    up, *g_up = _mm(n, W["w_up"][layer], "nn", F32, tmm, UP_SHARD, f"ffn_up_{tag}", host=host_up) \
        if host_up else (_mm(n, W["w_up"][layer], "nn", F32, tmm, UP_SHARD, f"ffn_up_{tag}"),)
    act, *g_act = _ffn_act_fwd(up, W["ffn_conv_w_p"][layer], W["ffn_conv_b"][layer:layer + 1], tm,
                               f"ffn_act_{tag}", host=host_act) \
        if host_act else (_ffn_act_fwd(up, W["ffn_conv_w_p"][layer], W["ffn_conv_b"][layer:layer + 1], tm,
                                       f"ffn_act_{tag}"),)
    out = _mm(act, W["w_down"][layer], "nn", F32, tm, D_MODEL, f"ffn_down_{tag}", add=h)
    return out, (n, up, act), g_up + g_act


def _ffn_bwd2(h, W, layer, saved, dout, acc, tm, tmm):
    tag = str(layer)
    n, up, act = saved
    dact = _mm(dout, W["w_down"][layer], "nt", F32, tmm, UP_SHARD, f"ffn_dact_{tag}")
    dwd = _mm(act, dout, "tn", F32, D_FF // 2, 512, f"ffn_dwdown_{tag}",
              out=(layer, 2, None if acc is None else acc[1]))
    dup, dcw, dcb = _ffn_act_bwd(up, dact, W["ffn_conv_w_p"][layer], W["ffn_conv_b"][layer:layer + 1], tm,
                                 f"ffn_act_bwd_{tag}")
    dn = _mm_ffn_dn(dup, W["w_up"][layer], tm, D_MODEL, f"ffn_dn_{tag}")
    dwu = _mm_ffn_dwup(n, dup, layer, None if acc is None else acc[0], 512, D_FF // 2, f"ffn_dwup_{tag}")
    dh, dgain = _rms_bwd(h, W["ffn_norm"][layer:layer + 1], dn, dout, tm, f"ffn_norm_bwd_{tag}")
    return dh, (dwu, dwd), dict(gain=dgain, cw=dcw[:FFN_K], cb=dcb)


GATHER_FIRST = ("w_in", "w_out", "small")
GATHER_LATE = ("pool_w", "w_up", "w_down")
HOSTED = ("w_out", "pool_w", "w_up", "w_down")
LATE = ("w_in", "small")


def _local_step2(h0, tgt, W, n_real, cut_of):
    LP = h0.shape[0]
    tm, tmm = _row_tiles(LP)
    nb = LP // ATT_BLK
    G = {}
    n0 = _rms_fwd(h0, W["mix_norm_even"], BF16, tm, "mix_norm_even")
    sh = W["late_shards"]
    stage = lambda *names: ([sh[n] for n in names], [cut_of[n] for n in names])
    proj, g_down0 = _mm(n0, W["w_in_p"], "nn", F32, tmm, 896, "in_proj", host=stage("w_down0"))
    c = _fgate_fwd(proj, W["b_f_p"], "forget_gate")
    qT, kT, k_aug, vT = _attn_prep(proj, c, "attention_operands")
    oT, lse, g_pool, g_up0 = _attn_fwd2(qT, k_aug, vT, "fox_attention", comm=stage("pool_w", "w_up0"))
    g_down0, g_pool, g_up0 = _gather_forward([g_down0, g_pool, g_up0], stage("w_down0", "pool_w", "w_up0")[1],
                                             "gather_forward_0")
    W = dict(W)
    W.update(pool_w=g_pool, w_up=[g_up0, None], w_down=[g_down0, None])
    u1, u = _conf_fwd(proj, W["conv_w_p"], W["conv_b"], W["ln_g"], W["ln_b"], tm, "conformer")
    cat = jnp.concatenate([_attn_rows(oT, 1.0, BF16, "attention_rows"), u], axis=1)
    h1 = _mm(cat, W["w_out"], "nn", F32, tmm, D_MODEL, "out_proj", add=h0)
    h2, ffn0, (g_down1, g_up1) = _ffn_fwd2(h1, W, 0, tm, tmm, host_up=stage("w_down1"), host_act=stage("w_up1"))
    g_down1, g_up1 = _gather_forward([g_down1, g_up1], stage("w_down1", "w_up1")[1], "gather_forward_1")
    W.update(w_up=[g_up0, g_up1], w_down=[g_down0, g_down1])
    h3, dpool = _pool_fwd(h2, W["mix_norm_odd"], W["pool_w"], W["pool_b"], W["pool_scale"], tm, "pool_mixer")
    h4, ffn1, _ = _ffn_fwd2(h3, W, 1, tm, tmm)
    loss, dh4, G["final_norm"] = _loss_head(h4, W["final_norm"], tgt, n_real, tm, "loss_head")

    dh3, acc, g1 = _ffn_bwd2(h3, W, 1, ffn1, dh4, None, tm, tmm)
    dh2, G["pool_w"], G["pool_b"], G["pool_scale"], G["mix_norm_odd"] = _pool_bwd(
        h2, W["mix_norm_odd"], dpool, W["pool_w"], W["pool_b"], W["pool_scale"], dh3, tm, "pool_mixer_bwd")
    dh1, acc, g0 = _ffn_bwd2(h1, W, 0, ffn0, dh2, acc, tm, tmm)
    G["w_up"], G["w_down"] = acc
    G["ffn_norm"] = jnp.concatenate([g0["gain"], g1["gain"]], axis=0)
    G["ffn_conv_w"] = jnp.stack([g0["cw"], g1["cw"]])
    G["ffn_conv_b"] = jnp.concatenate([g0["cb"], g1["cb"]], axis=0)

    dcat = _mm(dh1, W["w_out"], "nt", F32, tmm, D_MODEL, "out_proj_dx")
    G["w_out"] = _mm(cat, dh1, "tn", F32, 512, D_MODEL, "out_proj_dw")
    dadg, dcw, G["conv_b"], G["ln_g"], G["ln_b"] = _conf_bwd(
        proj, u1, dcat, W["conv_w_p"], W["ln_g"], W["ln_b"], tm, "conformer_bwd")
    G["conv_w"] = dcw[:CONV_K]
    doT = _attn_cols(dcat, "attention_do_cols")
    hcuts = [cut_of[n] for n in HOSTED]
    hfull = [G[n] for n in HOSTED]
    hrecv = _pair_exchange2(hfull, hcuts, "grad_pair_exchange_early")
    hparts = [_pair_sum2(f, r, cut, PAIR_SUM_BLOCKS[n], "grad_pair_sum_" + n)
              for f, r, cut, n in zip(hfull, hrecv, hcuts, HOSTED)]
    dqT, dkT, dvT, *hothers = _attn_bwd2(qT, kT, k_aug, vT, oT, doT, lse, "fox_attention_bwd",
                                         comm=(hparts, hcuts))
    dfl, dbf = _fgate_bwd(proj, W["b_f_p"], _attn_dc(dqT, dkT, "attention_dc"), "forget_gate_bwd")
    G["b_f"] = dbf[:, :HEADS]
    dproj = jnp.concatenate([_attn_rows(dqT, HEAD_DIM ** -0.5, BF16, "attention_dq_rows"),
                             _attn_rows(dkT, 1.0, BF16, "attention_dk_rows"),
                             _attn_rows(dvT, 1.0, BF16, "attention_dv_rows"), dadg, dfl], axis=1)
    dn0 = _mm(dproj, W["w_in_p"], "nt", F32, tmm, D_MODEL, "in_proj_dx")
    G["w_in_p"] = _mm(n0, dproj, "tn", F32, 512, 896, "in_proj_dw")
    dh0, G["mix_norm_even"] = _rms_bwd(h0, W["mix_norm_even"], dn0, dh1, tm, "mix_norm_even_bwd")
    return loss, dh0, G, dict(zip(HOSTED, hparts)), dict(zip(HOSTED, hothers))


class _Cut:
    def __init__(self, full_shape, chip_dim, half_dim):
        self.full = tuple(full_shape)
        self.chip_dim, self.half_dim = chip_dim, half_dim
        self.chip_size = full_shape[chip_dim] // N_CHIPS
        self.half_size = full_shape[half_dim] // 2
        assert chip_dim != half_dim

    def shape(self, chip=False, half=False):
        s = list(self.full)
        if chip:
            s[self.chip_dim] = self.chip_size
        if half:
            s[self.half_dim] = self.half_size
        return tuple(s)

    def region(self, ref, chip=None, half=None):
        idx = [pl.ds(0, n) for n in ref.shape]
        if chip is not None:
            idx[self.chip_dim] = pl.ds(chip * self.chip_size, self.chip_size)
        if half is not None:
            idx[self.half_dim] = pl.ds(half * self.half_size, self.half_size)
        return ref.at[tuple(idx)]


SMALL_SHARDED = ("meta_tokens", "mix_norm_odd", "pool_b", "pool_scale", "conv_w", "ffn_conv_w")
SMALL_ROWS = 144


def _cuts():
    return {
        "w_in": _Cut((N_CHIPS, D_MODEL, IN_SHARD), 0, 1),
        "w_out": _Cut((D_MODEL, D_MODEL), 0, 1),
        "pool_w": _Cut((len(POOL_WINDOWS), POOL_G, POOL_G), 1, 0),
        "w_up": _Cut((2, D_MODEL, 2 * D_FF), 2, 1),
        "w_down": _Cut((2, D_FF, D_MODEL), 1, 2),
        "small": _Cut((N_CHIPS, SMALL_ROWS, LANE), 0, 1),
        "w_up0": _Cut((D_MODEL, 2 * D_FF), 1, 0), "w_up1": _Cut((D_MODEL, 2 * D_FF), 1, 0),
        "w_down0": _Cut((D_FF, D_MODEL), 0, 1), "w_down1": _Cut((D_FF, D_MODEL), 0, 1),
    }


COMM_ORDER = ("w_in", "w_out", "pool_w", "w_up", "w_down", "small")


def _remote(src, dst, send_sems, recv_sems, k, to):
    return pltpu.make_async_remote_copy(src_ref=src, dst_ref=dst, send_sem=send_sems.at[k],
                                        recv_sem=recv_sems.at[k], device_id=to, device_id_type=MESH)


def _gather_weights(shards, cuts):
    n = len(shards)

    def body(*refs):
        srcs, outs = refs[:n], refs[n:2 * n]
        send_sems, recv_sems = refs[2 * n:]
        x, y, c = _coords()
        me = 2 * x + y
        sibling = (x, y, 1 - c)
        chips = _other_chips(x, y)
        sends = []
        for t, cut in enumerate(cuts):
            push = _remote(srcs[t], cut.region(outs[t], chip=me), send_sems, recv_sems, 7 * t, sibling)
            push.start()
            sends.append(push)
            for kk, chip in enumerate(chips):
                cp = _remote(cut.region(srcs[t], half=c), cut.region(outs[t], chip=me, half=c),
                             send_sems, recv_sems, 7 * t + 1 + kk, (*chip, c))
                cp.start()
                sends.append(cp)
        for t, cut in enumerate(cuts):
            for kk, (px, py) in enumerate(chips):
                landed = cut.region(outs[t], chip=2 * px + py, half=c)
                _remote(landed, landed, send_sems, recv_sems, 7 * t + 1 + kk, sibling).wait_recv()
                fwd = _remote(landed, landed, send_sems, recv_sems, 7 * t + 4 + kk, sibling)
                fwd.start()
                sends.append(fwd)
        for t, cut in enumerate(cuts):
            mine = cut.region(outs[t], chip=me)
            _remote(mine, mine, send_sems, recv_sems, 7 * t, sibling).wait_recv()
            for kk, (px, py) in enumerate(chips):
                other = cut.region(outs[t], chip=2 * px + py, half=1 - c)
                _remote(other, other, send_sems, recv_sems, 7 * t + 4 + kk, sibling).wait_recv()
        for cp in sends:
            cp.wait_send()

    return pl.pallas_call(
        body, name="gather_weights",
        out_shape=tuple(jax.ShapeDtypeStruct(cut.full, s.dtype) for cut, s in zip(cuts, shards)),
        in_specs=[ANY] * n, out_specs=tuple([ANY] * n),
        scratch_shapes=[pltpu.SemaphoreType.DMA((7 * n,)), pltpu.SemaphoreType.DMA((7 * n,))],
    )(*shards)


def _gather_first_ops(srcs, outs, send_sems, recv_sems, cuts):
    x, y, c = _coords()
    me = 2 * x + y
    sibling = (x, y, 1 - c)
    chips = _other_chips(x, y)

    def copies():
        out = []
        for t, cut in enumerate(cuts):
            out.append(_remote(srcs[t], cut.region(outs[t], chip=me), send_sems, recv_sems, 4 * t, sibling))
            for kk, chip in enumerate(chips):
                out.append(_remote(cut.region(srcs[t], half=c), cut.region(outs[t], chip=me, half=c),
                                   send_sems, recv_sems, 4 * t + 1 + kk, (*chip, c)))
        return out

    def start():
        for cp in copies():
            cp.start()

    def wait():
        for t, cut in enumerate(cuts):
            mine = cut.region(outs[t], chip=me)
            _remote(mine, mine, send_sems, recv_sems, 4 * t, sibling).wait_recv()
            for kk, (px, py) in enumerate(chips):
                landed = cut.region(outs[t], chip=2 * px + py, half=c)
                _remote(landed, landed, send_sems, recv_sems, 4 * t + 1 + kk, sibling).wait_recv()
        for cp in copies():
            cp.wait_send()

    return start, wait


def _gather_forward(fulls, cuts, name):
    n = len(fulls)

    def body(*refs):
        outs = refs[n:2 * n]
        send_sems, recv_sems = refs[2 * n:]
        x, y, c = _coords()
        sibling = (x, y, 1 - c)
        chips = _other_chips(x, y)
        sends = []
        for t, cut in enumerate(cuts):
            for kk, (px, py) in enumerate(chips):
                landed = cut.region(outs[t], chip=2 * px + py, half=c)
                cp = _remote(landed, landed, send_sems, recv_sems, 3 * t + kk, sibling)
                cp.start()
                sends.append(cp)
        for t, cut in enumerate(cuts):
            for kk, (px, py) in enumerate(chips):
                other = cut.region(outs[t], chip=2 * px + py, half=1 - c)
                _remote(other, other, send_sems, recv_sems, 3 * t + kk, sibling).wait_recv()
        for cp in sends:
            cp.wait_send()

    return pl.pallas_call(
        body, name=name,
        out_shape=tuple(jax.ShapeDtypeStruct(f.shape, f.dtype) for f in fulls),
        in_specs=[ANY] * n, out_specs=tuple([ANY] * n), input_output_aliases={t: t for t in range(n)},
        scratch_shapes=[pltpu.SemaphoreType.DMA((3 * n,)), pltpu.SemaphoreType.DMA((3 * n,))],
    )(*fulls)


def _pair_exchange2(fulls, cuts, name):
    n = len(fulls)

    def body(*refs):
        srcs, outs = refs[:n], refs[n:2 * n]
        send_sems, recv_sems = refs[2 * n:]
        x, y, c = _coords()
        cps = [_remote(cut.region(srcs[t], half=1 - c), outs[t], send_sems, recv_sems, t, (x, y, 1 - c))
               for t, cut in enumerate(cuts)]
        for cp in cps:
            cp.start()
        for cp in cps:
            cp.wait()

    return pl.pallas_call(
        body, name=name,
        out_shape=tuple(jax.ShapeDtypeStruct(cut.shape(half=True), f.dtype) for cut, f in zip(cuts, fulls)),
        in_specs=[ANY] * n, out_specs=tuple([ANY] * n),
        scratch_shapes=[pltpu.SemaphoreType.DMA((n,)), pltpu.SemaphoreType.DMA((n,))],
    )(*fulls)


def _grid_of(shape, blk):
    assert all(s % b == 0 for s, b in zip(shape, blk)), (shape, blk)
    return tuple(s // b for s, b in zip(shape, blk))


def _pair_sum2(full, recv, cut, blk, name):
    hshape = cut.shape(half=True)
    grid = _grid_of(hshape, blk)
    hb = cut.half_size // blk[cut.half_dim]
    hd = cut.half_dim
    pos = jnp.stack([lax.axis_index("c")]).astype(jnp.int32)

    def full_idx(*a):
        ids, p = list(a[:-1]), a[-1]
        ids[hd] = ids[hd] + p[0] * hb
        return tuple(ids)

    def body(p_ref, f_ref, r_ref, o_ref):
        o_ref[...] = (f_ref[...] + r_ref[...]).astype(o_ref.dtype)

    return pl.pallas_call(
        body, name=name, out_shape=jax.ShapeDtypeStruct(hshape, BF16),
        grid_spec=pltpu.PrefetchScalarGridSpec(
            num_scalar_prefetch=1, grid=grid,
            in_specs=[pl.BlockSpec(blk, full_idx), pl.BlockSpec(blk, lambda *a: tuple(a[:-1]))],
            out_specs=pl.BlockSpec(blk, lambda *a: tuple(a[:-1]))),
        compiler_params=_params(("parallel",) * len(grid)))(pos, full, recv)


def _chip_exchange_ops(srcs, outs, send_sems, recv_sems, cuts):
    x, y, c = _coords()
    me = 2 * x + y
    chips = _other_chips(x, y)

    def copies():
        return [_remote(cut.region(srcs[t], chip=2 * px + py), outs[t].at[me], send_sems, recv_sems,
                        3 * t + kk, (px, py, c))
                for t, cut in enumerate(cuts) for kk, (px, py) in enumerate(chips)]

    def start():
        for cp in copies():
            cp.start()

    def wait():
        for t, cut in enumerate(cuts):
            for kk, (px, py) in enumerate(chips):
                slot = outs[t].at[2 * px + py]
                _remote(slot, slot, send_sems, recv_sems, 3 * t + kk, (px, py, c)).wait_recv()
        for cp in copies():
            cp.wait_send()

    return start, wait


def _chip_exchange_shapes(parts, cuts):
    return tuple(jax.ShapeDtypeStruct((N_CHIPS,) + cut.shape(chip=True, half=True), p.dtype)
                 for cut, p in zip(cuts, parts))


def _chip_exchange2(parts, cuts):
    n = len(parts)

    def body(*refs):
        start, wait = _chip_exchange_ops(refs[:n], refs[n:2 * n], refs[2 * n], refs[2 * n + 1], cuts)
        start()
        wait()

    return pl.pallas_call(
        body, name="grad_chip_exchange",
        out_shape=tuple(jax.ShapeDtypeStruct((N_CHIPS,) + cut.shape(chip=True, half=True), p.dtype)
                        for cut, p in zip(cuts, parts)),
        in_specs=[ANY] * n, out_specs=tuple([ANY] * n),
        scratch_shapes=[pltpu.SemaphoreType.DMA((3 * n,)), pltpu.SemaphoreType.DMA((3 * n,))],
    )(*parts)


def _chip_sum2(part, recv, cut, blk, name):
    bshape = cut.shape(chip=True, half=True)
    grid = _grid_of(bshape, blk)
    cb = cut.chip_size // blk[cut.chip_dim]
    hb = cut.half_size // blk[cut.half_dim]
    cd, hd = cut.chip_dim, cut.half_dim
    x, y, c = _coords()
    slots = [2 * px + py for px, py in _other_chips(x, y)]
    pos = jnp.stack([c, 2 * x + y] + slots).astype(jnp.int32)

    def part_idx(*a):
        ids, p = list(a[:-1]), a[-1]
        ids[cd] = ids[cd] + p[1] * cb
        return tuple(ids)

    def recv_idx(kk):
        return lambda *a: (a[-1][2 + kk],) + tuple(a[:-1])

    def out_idx(*a):
        ids, p = list(a[:-1]), a[-1]
        ids[hd] = ids[hd] + p[0] * hb
        return tuple(ids)

    def body(p_ref, own_ref, r0_ref, r1_ref, r2_ref, o_ref):
        acc = own_ref[...].astype(F32)
        for r_ref in (r0_ref, r1_ref, r2_ref):
            acc = acc + r_ref[...].astype(F32)
        o_ref[...] = acc

    return pl.pallas_call(
        body, name=name, out_shape=jax.ShapeDtypeStruct(cut.shape(chip=True), F32),
        grid_spec=pltpu.PrefetchScalarGridSpec(
            num_scalar_prefetch=1, grid=grid,
            in_specs=[pl.BlockSpec(blk, part_idx)] + [pl.BlockSpec((None,) + blk, recv_idx(kk)) for kk in range(3)],
            out_specs=pl.BlockSpec(blk, out_idx)),
        compiler_params=_params(("parallel",) * len(grid)))(pos, part, recv, recv, recv)


def _pair_swap2(blocks, cuts):
    n = len(blocks)

    def body(*refs):
        outs = refs[n:2 * n]
        send_sems, recv_sems = refs[2 * n:]
        x, y, c = _coords()
        cps = []
        for t, cut in enumerate(cuts):
            mine = cut.region(outs[t], half=c)
            cp = _remote(mine, mine, send_sems, recv_sems, t, (x, y, 1 - c))
            cp.start()
            cps.append(cp)
        for t, cut in enumerate(cuts):
            theirs = cut.region(outs[t], half=1 - c)
            _remote(theirs, theirs, send_sems, recv_sems, t, (x, y, 1 - c)).wait_recv()
        for cp in cps:
            cp.wait_send()

    return pl.pallas_call(
        body, name="grad_pair_swap",
        out_shape=tuple(jax.ShapeDtypeStruct(b.shape, b.dtype) for b in blocks),
        in_specs=[ANY] * n, out_specs=tuple([ANY] * n),
        input_output_aliases={t: t for t in range(n)},
        scratch_shapes=[pltpu.SemaphoreType.DMA((n,)), pltpu.SemaphoreType.DMA((n,))],
    )(*blocks)


PAIR_SUM_BLOCKS = {"w_in": (1, 512, IN_SHARD), "w_out": (512, 512), "pool_w": (1, POOL_G, POOL_G),
                   "w_up": (1, 64, 2 * D_FF), "w_down": (1, 704, 512), "small": (N_CHIPS, SMALL_ROWS // 2, LANE)}
CHIP_SUM_BLOCKS = {"w_in": (1, 512, IN_SHARD), "w_out": (256, 512), "pool_w": (2, 64, POOL_G),
                   "w_up": (1, 128, UP_SHARD), "w_down": (1, DOWN_SHARD, 512), "small": (1, SMALL_ROWS // 2, LANE)}


def _pack_small(P):
    parts = []
    for name in SMALL_SHARDED:
        t = P[name]
        parts.append(t.astype(F32))
    return parts


def _small_rows(t, lead):
    flat = t.reshape(lead + (-1,))
    pad = -flat.shape[-1] % LANE
    return jnp.pad(flat, [(0, 0)] * len(lead) + [(0, pad)]).reshape(lead + (-1, LANE))


def _pack_small_shards(shards):
    rows = jnp.concatenate([_small_rows(shards[n].astype(F32), ()) for n in SMALL_SHARDED], axis=0)
    return jnp.pad(rows, ((0, SMALL_ROWS - rows.shape[0]), (0, 0)))[None]


def _unpack_small(pack, shards, axes):
    out, off = {}, 0
    nchip = pack.shape[0]
    for name in SMALL_SHARDED:
        shp = shards[name].shape
        cnt = int(np.prod(shp))
        rows = -(-cnt // LANE)
        t = pack[:, off:off + rows].reshape(nchip, -1)[:, :cnt].reshape((nchip,) + shp)
        out[name] = jnp.concatenate([t[j] for j in range(nchip)], axis=axes[name])
        off += rows
    return out


def _pack_small_grads(grads, shards, axes):
    parts = []
    for name in SMALL_SHARDED:
        shp, ax = shards[name].shape, axes[name]
        g = grads[name].reshape(shp[:ax] + (N_CHIPS, shp[ax]) + shp[ax + 1:])
        parts.append(_small_rows(jnp.moveaxis(g, ax, 0), (N_CHIPS,)))
    rows = jnp.concatenate(parts, axis=1)
    return jnp.pad(rows, ((0, 0), (0, SMALL_ROWS - rows.shape[1]), (0, 0)))


SMALL_AXES = {"meta_tokens": 1, "mix_norm_odd": 1, "pool_b": 2, "pool_scale": 1, "conv_w": 2, "ffn_conv_w": 2}


WEIGHT_NAMES = ("meta_tokens", "mix_norm_even", "w_in", "b_f", "conv_w", "conv_b", "ln_g", "ln_b", "w_out",
                "mix_norm_odd", "pool_w", "pool_b", "pool_scale", "ffn_norm", "w_up", "ffn_conv_w",
                "ffn_conv_b", "w_down", "final_norm")


def kernel(x, meta_tokens, mix_norm_even, w_in, b_f, conv_w, conv_b, ln_g, ln_b, w_out, mix_norm_odd, pool_w, pool_b, pool_scale, ffn_norm, w_up, ffn_conv_w, ffn_conv_b, w_down, final_norm, loss_target, m_meta_tokens, m_mix_norm_even, m_w_in, m_b_f, m_conv_w, m_conv_b, m_ln_g, m_ln_b, m_w_out, m_mix_norm_odd, m_pool_w, m_pool_b, m_pool_scale, m_ffn_norm, m_w_up, m_ffn_conv_w, m_ffn_conv_b, m_w_down, m_final_norm, v_meta_tokens, v_mix_norm_even, v_w_in, v_b_f, v_conv_w, v_conv_b, v_ln_g, v_ln_b, v_w_out, v_mix_norm_odd, v_pool_w, v_pool_b, v_pool_scale, v_ffn_norm, v_w_up, v_ffn_conv_w, v_ffn_conv_b, v_w_down, v_final_norm):
    given = dict(locals())
    w_loc = {n: given[n] for n in WEIGHT_NAMES}
    m_loc = {n: given["m_" + n] for n in WEIGHT_NAMES}
    v_loc = {n: given["v_" + n] for n in WEIGHT_NAMES}
    cut_of = _cuts()
    cuts = [cut_of[n] for n in COMM_ORDER]
    big = ("w_in", "w_out", "pool_w", "w_up", "w_down")
    small_shards = {n: w_loc[n] for n in SMALL_SHARDED}

    shard_of = {n: w_loc[n].astype(BF16).reshape(cut_of[n].shape(chip=True)) for n in big}
    shard_of["small"] = _pack_small_shards(small_shards)
    g_in, g_out, g_small = _gather_weights([shard_of[n] for n in GATHER_FIRST], [cut_of[n] for n in GATHER_FIRST])
    g_pool = g_up = g_down = None
    full = _unpack_small(g_small, small_shards, SMALL_AXES)
    full.update({n: w_loc[n] for n in REPLICATED})
    w_in_full = g_in.transpose(1, 0, 2).reshape(D_MODEL, IN_COLS)
    qkv, f, ag = (w_in_full[:, :3 * FOX_W], w_in_full[:, 3 * FOX_W:3 * FOX_W + HEADS],
                  w_in_full[:, 3 * FOX_W + HEADS:])
    W = dict(
        mix_norm_even=full["mix_norm_even"].reshape(1, D_MODEL),
        w_in_p=jnp.concatenate([qkv, ag, f, jnp.zeros((D_MODEL, LANE - HEADS), BF16)], axis=1),
        b_f_p=jnp.pad(full["b_f"].reshape(1, HEADS), ((0, 0), (0, LANE - HEADS))),
        conv_w_p=jnp.pad(full["conv_w"].reshape(CONV_K, CONV_CH), ((0, CONV_HALO - CONV_K), (0, 0))),
        conv_b=full["conv_b"].reshape(1, CONV_CH), ln_g=full["ln_g"].reshape(1, CONV_CH),
        ln_b=full["ln_b"].reshape(1, CONV_CH), w_out=g_out,
        mix_norm_odd=full["mix_norm_odd"].reshape(1, D_MODEL), pool_w=g_pool,
        pool_b=full["pool_b"].reshape(1, D_MODEL), pool_scale=full["pool_scale"].reshape(1, D_MODEL),
        ffn_norm=full["ffn_norm"], w_up=g_up,
        ffn_conv_w_p=jnp.pad(full["ffn_conv_w"], ((0, 0), (0, 8 - FFN_K), (0, 0))),
        ffn_conv_b=full["ffn_conv_b"], w_down=g_down, final_norm=full["final_norm"].reshape(1, D_MODEL),
        late_shards=dict(pool_w=shard_of["pool_w"], w_up0=shard_of["w_up"][0], w_up1=shard_of["w_up"][1],
                         w_down0=shard_of["w_down"][0], w_down1=shard_of["w_down"][1]))

    seq = x.shape[1]
    n_real = N_META + seq
    LP = -(-n_real // ATT_BLK) * ATT_BLK
    tail = jnp.zeros((LP - n_real, D_MODEL), F32)
    h0 = jnp.concatenate([full["meta_tokens"], x[0], tail], axis=0)
    tgt = jnp.concatenate([jnp.zeros((N_META, D_MODEL), F32), loss_target[0], tail], axis=0)
    loss_loc, dh0, G, parts, others = _local_step2(h0, tgt, W, n_real, cut_of)
    grad_x = dh0[N_META:n_real][None]
    G["meta_tokens"] = dh0[:N_META]

    rep_shapes = {n: w_loc[n].shape for n in REPLICATED}
    G["final_norm"] = G["final_norm"].reshape(D_MODEL)
    rep, loss = _unpack_replicated(_allreduce_small(_pack_replicated(G, loss_loc)), rep_shapes)

    gp = G["w_in_p"]
    g_w_in = jnp.concatenate([gp[:, :3 * FOX_W], gp[:, 3 * FOX_W + 2 * CONV_CH:3 * FOX_W + 2 * CONV_CH + HEADS],
                              gp[:, 3 * FOX_W:3 * FOX_W + 2 * CONV_CH]], axis=1)
    lcuts = [cut_of[n] for n in LATE]
    lfull = [g_w_in.reshape(D_MODEL, N_CHIPS, IN_SHARD).transpose(1, 0, 2),
             _pack_small_grads(G, small_shards, SMALL_AXES)]
    lrecv = _pair_exchange2(lfull, lcuts, "grad_pair_exchange_late")
    lparts = [_pair_sum2(f, r, cut, PAIR_SUM_BLOCKS[n], "grad_pair_sum_" + n)
              for f, r, cut, n in zip(lfull, lrecv, lcuts, LATE)]
    parts.update(zip(LATE, lparts))
    others.update(zip(LATE, _chip_exchange2(lparts, lcuts)))
    blocks = [_chip_sum2(parts[n], others[n], cut_of[n], CHIP_SUM_BLOCKS[n], "grad_chip_sum_" + n)
              for n in COMM_ORDER]
    blocks = _pair_swap2(blocks, cuts)
    gsh = {n: b.reshape(w_loc[n].shape) for n, b in zip(big, blocks[:5])}
    gsh.update(_unpack_small(blocks[5], small_shards, SMALL_AXES))
    sharded = set(big) | set(SMALL_SHARDED)

    grad_w = {n: (gsh[n] if n in sharded else rep[n]) for n in WEIGHT_NAMES}
    delta, new_m, new_v = {}, {}, {}
    for n in WEIGHT_NAMES:
        delta[n], new_m[n], new_v[n] = _adamw(w_loc[n], grad_w[n], m_loc[n], v_loc[n], "adamw_" + n)
    return (loss, grad_x, *[grad_w[n] for n in WEIGHT_NAMES], *[delta[n] for n in WEIGHT_NAMES],
            *[new_m[n] for n in WEIGHT_NAMES], *[new_v[n] for n in WEIGHT_NAMES])
```

```python
import functools

import numpy as np
import jax
import jax.numpy as jnp
from jax import lax
from jax.experimental import pallas as pl
from jax.experimental.pallas import tpu as pltpu

F32 = jnp.float32
BF16 = jnp.bfloat16

D_MODEL = 1024
N_META = 16
SEQ = 2048
HEADS = 8
HEAD_DIM = 64
FOX_W = HEADS * HEAD_DIM
CONV_CH = 512
CONV_K = 31
D_FF = 2816
POOL_WINDOWS = (2, 4, 8, 16)
POOL_G = 256
RMS_EPS = 1e-6
LN_EPS = 1e-5
IN_COLS = 3 * FOX_W + HEADS + 2 * CONV_CH
IN_COLS_P = 3 * FOX_W + 2 * CONV_CH + 128
F_COL_BLK = (3 * FOX_W + 2 * CONV_CH) // 128
N_CHIPS = 4
IN_SHARD = IN_COLS // N_CHIPS
UP_SHARD = 2 * D_FF // N_CHIPS
DOWN_SHARD = D_FF // N_CHIPS

ADAM_LR = 0.001
ADAM_B1 = 0.9
ADAM_B2 = 0.999
ADAM_EPS = 1e-08
ADAM_WD = 0.01
ADAM_STEP = 10

LANE = 128
ATT_BLK = 128
VMEM_LIMIT = 56 * 1024 * 1024

NEG = -1e30


def _sigmoid(x):
    return 0.5 * jnp.tanh(0.5 * x) + 0.5


def _sigmoid_tail(x):
    return 1.0 / (1.0 + jnp.exp(-x))


def _params(sem=None):
    return pltpu.CompilerParams(dimension_semantics=sem, vmem_limit_bytes=VMEM_LIMIT)


def _sub_rows(tm):
    best = 8
    for s in range(8, 137, 8):
        if tm % s == 0:
            best = s
    return best


def _mm(a, b, mode, out_dtype, tm, tn, name, add=None, a_lead=None, b_lead=None, out=None, host=None):
    a_shape = a.shape if a_lead is None else a.shape[1:]
    b_shape = b.shape if b_lead is None else b.shape[1:]
    if mode == "nn":
        (M, K), (K2, N) = a_shape, b_shape
        dims = (((1,), (0,)), ((), ()))
        a_blk, a_idx = (tm, K), (lambda i, j: (i, 0))
        b_blk, b_idx = (K, tn), (lambda i, j: (0, j))
    elif mode == "nt":
        (M, K), (N, K2) = a_shape, b_shape
        dims = (((1,), (1,)), ((), ()))
        a_blk, a_idx = (tm, K), (lambda i, j: (i, 0))
        b_blk, b_idx = (tn, K), (lambda i, j: (j, 0))
    else:
        (K, M), (K2, N) = a_shape, b_shape
        dims = (((0,), (0,)), ((), ()))
        a_blk, a_idx = (K, tm), (lambda i, j: (0, i))
        b_blk, b_idx = (K, tn), (lambda i, j: (0, j))
    assert K == K2 and M % tm == 0 and N % tn == 0, (name, a.shape, b.shape, tm, tn)
    gm, gn = M // tm, N // tn
    a_bytes = M * K * a.dtype.itemsize
    b_bytes = N * K * b.dtype.itemsize
    m_outer = a_bytes + b_bytes * gm <= b_bytes + a_bytes * gn
    if m_outer:
        grid = (gm, gn)
        wrap = lambda f: f
    else:
        grid = (gn, gm)
        wrap = lambda f: (lambda j, i: f(i, j))

    def lead(blk, idx, at):
        if at is None:
            return pl.BlockSpec(blk, wrap(idx))
        return pl.BlockSpec((None,) + blk, wrap(lambda i, j: (at,) + idx(i, j)))

    o_idx = lambda i, j: (i, j)
    in_specs = [lead(a_blk, a_idx, a_lead), lead(b_blk, b_idx, b_lead)]
    args = [a, b]
    if add is not None:
        in_specs.append(pl.BlockSpec((tm, tn), wrap(o_idx)))
        args.append(add)
    aliases = {}
    if out is None:
        out_shape = jax.ShapeDtypeStruct((M, N), out_dtype)
        out_spec = pl.BlockSpec((tm, tn), wrap(o_idx))
    else:
        o_lead, n_lead, into = out
        out_shape = jax.ShapeDtypeStruct((n_lead, M, N), out_dtype)
        out_spec = lead((tm, tn), o_idx, o_lead)
        if into is not None:
            aliases = {len(args): 0}
            in_specs.append(pl.BlockSpec(memory_space=pl.ANY))
            args.append(into)
    has_add = add is not None
    n_host = 0 if host is None else len(host[0])
    n_in = len(args)
    scratch = []
    if n_host:
        in_specs = in_specs + [pl.BlockSpec(memory_space=pl.ANY)] * n_host
        args = args + list(host[0])
        out_shape = (out_shape,) + tuple(jax.ShapeDtypeStruct(cut.full, s.dtype) for cut, s in zip(host[1], host[0]))
        out_spec = (out_spec,) + (pl.BlockSpec(memory_space=pl.ANY),) * n_host
        scratch = [pltpu.SemaphoreType.DMA((4 * n_host,)), pltpu.SemaphoreType.DMA((4 * n_host,))]

    def body(*refs):
        a_ref, b_ref = refs[0], refs[1]
        o_ref = refs[n_in + n_host]
        if n_host:
            start, wait = _gather_first_ops(refs[n_in:n_in + n_host], refs[n_in + n_host + 1:n_in + 2 * n_host + 1],
                                            refs[n_in + 2 * n_host + 1], refs[n_in + 2 * n_host + 2], host[1])
            pl.when(jnp.logical_and(pl.program_id(0) == 0, pl.program_id(1) == 0))(start)
        x = a_ref[...].astype(BF16)
        y = b_ref[...].astype(BF16)
        acc = lax.dot_general(x, y, dims, preferred_element_type=F32)
        if has_add:
            acc = acc + refs[2][...]
        o_ref[...] = acc.astype(o_ref.dtype)
        if n_host:
            pl.when(jnp.logical_and(pl.program_id(0) == grid[0] - 1, pl.program_id(1) == grid[1] - 1))(wait)

    sem = ("arbitrary", "arbitrary") if n_host else ("parallel", "parallel")
    return pl.pallas_call(
        body, name=name, out_shape=out_shape, grid=grid, in_specs=in_specs, out_specs=out_spec,
        scratch_shapes=scratch, input_output_aliases=aliases, compiler_params=_params(sem))(*args)


def _mm_ffn_dn(dup, w_up, tm, tn, name):
    _, LP, F = dup.shape
    Dm = w_up.shape[0]
    nt = (((1,), (1,)), ((), ()))

    def body(a_ref, b_ref, o_ref):
        acc = lax.dot_general(a_ref[0], b_ref[:, 0:F], nt, preferred_element_type=F32)
        acc = acc + lax.dot_general(a_ref[1], b_ref[:, F:2 * F], nt, preferred_element_type=F32)
        o_ref[...] = acc

    return pl.pallas_call(
        body, name=name, out_shape=jax.ShapeDtypeStruct((LP, Dm), F32), grid=(LP // tm, Dm // tn),
        in_specs=[pl.BlockSpec((2, tm, F), lambda i, j: (0, i, 0)),
                  pl.BlockSpec((tn, 2 * F), lambda i, j: (j, 0))],
        out_specs=pl.BlockSpec((tm, tn), lambda i, j: (i, j)),
        compiler_params=_params(("parallel", "parallel")))(dup, w_up)


def _mm_ffn_dwup(n, dup, layer, into, tk, tn, name):
    LP, Dm = n.shape
    F = dup.shape[2]
    nct = F // tn
    tdims = (((0,), (0,)), ((), ()))

    def body(a_ref, b_ref, *rest):
        rest[-1][...] = lax.dot_general(a_ref[...], b_ref[...], tdims, preferred_element_type=F32)

    in_specs = [pl.BlockSpec((LP, tk), lambda i, j: (0, i)),
                pl.BlockSpec((None, LP, tn), lambda i, j: (j // nct, 0, j % nct))]
    args = [n, dup]
    aliases = {}
    if into is not None:
        in_specs.append(pl.BlockSpec(memory_space=pl.ANY))
        args.append(into)
        aliases = {2: 0}
    return pl.pallas_call(
        body, name=name, out_shape=jax.ShapeDtypeStruct((2, Dm, 2 * F), F32), grid=(Dm // tk, 2 * nct),
        in_specs=in_specs, out_specs=pl.BlockSpec((None, tk, tn), lambda i, j: (layer, i, j)),
        input_output_aliases=aliases, compiler_params=_params(("parallel", "parallel")))(*args)


def _rms_fwd(h, g, out_dtype, tm, name):
    LP, Dm = h.shape

    def body(h_ref, g_ref, o_ref):
        x = h_ref[...]
        r = lax.rsqrt(jnp.mean(x * x, axis=1, keepdims=True) + RMS_EPS)
        o_ref[...] = (x * r * g_ref[...]).astype(o_ref.dtype)

    return pl.pallas_call(
        body, name=name, out_shape=jax.ShapeDtypeStruct((LP, Dm), out_dtype), grid=(LP // tm,),
        in_specs=[pl.BlockSpec((tm, Dm), lambda i: (i, 0)), pl.BlockSpec((1, Dm), lambda i: (0, 0))],
        out_specs=pl.BlockSpec((tm, Dm), lambda i: (i, 0)),
        compiler_params=_params(("parallel",)))(h, g)


def _rms_bwd(h, g, dn, dres, tm, name):
    LP, Dm = h.shape

    def body(h_ref, g_ref, dn_ref, dr_ref, dh_ref, dg_ref):
        i = pl.program_id(0)
        x = h_ref[...]
        r = lax.rsqrt(jnp.mean(x * x, axis=1, keepdims=True) + RMS_EPS)
        xhat = x * r
        dy = dn_ref[...]
        dxh = dy * g_ref[...]
        dh = r * (dxh - xhat * jnp.mean(dxh * xhat, axis=1, keepdims=True))
        dh_ref[...] = dr_ref[...] + dh

        @pl.when(i == 0)
        def _():
            dg_ref[...] = jnp.zeros_like(dg_ref)

        dg_ref[...] += jnp.sum(dy * xhat, axis=0, keepdims=True)

    row = pl.BlockSpec((tm, Dm), lambda i: (i, 0))
    vec = pl.BlockSpec((1, Dm), lambda i: (0, 0))
    return pl.pallas_call(
        body, name=name,
        out_shape=(jax.ShapeDtypeStruct((LP, Dm), F32), jax.ShapeDtypeStruct((1, Dm), F32)),
        grid=(LP // tm,), in_specs=[row, vec, row, row], out_specs=(row, vec),
        compiler_params=_params(("arbitrary",)))(h, g, dn, dres)


def _loss_head(h, g, tgt, n_real, tm, name):
    LP, Dm = h.shape

    def body(h_ref, g_ref, t_ref, loss_ref, dh_ref, dg_ref):
        i = pl.program_id(0)
        x = h_ref[...]
        gg = g_ref[...]
        r = lax.rsqrt(jnp.mean(x * x, axis=1, keepdims=True) + RMS_EPS)
        xhat = x * r
        rows = i * tm + lax.broadcasted_iota(jnp.int32, (tm, 1), 0)
        real = jnp.logical_and(rows >= N_META, rows < n_real)
        diff = jnp.where(real, xhat * gg - t_ref[...], 0.0)
        dy = diff * (1.0 / Dm)
        dxh = dy * gg
        dh_ref[...] = r * (dxh - xhat * jnp.mean(dxh * xhat, axis=1, keepdims=True))

        @pl.when(i == 0)
        def _():
            dg_ref[...] = jnp.zeros_like(dg_ref)
            loss_ref[...] = jnp.zeros_like(loss_ref)

        dg_ref[...] += jnp.sum(dy * xhat, axis=0, keepdims=True)
        part = jnp.sum(jnp.sum(diff * diff, axis=1, keepdims=True), axis=0, keepdims=True)
        loss_ref[...] += jnp.broadcast_to(part * (0.5 / Dm), loss_ref.shape)

    row = pl.BlockSpec((tm, Dm), lambda i: (i, 0))
    vec = pl.BlockSpec((1, Dm), lambda i: (0, 0))
    return pl.pallas_call(
        body, name=name,
        out_shape=(jax.ShapeDtypeStruct((1, LANE), F32), jax.ShapeDtypeStruct((LP, Dm), F32),
                   jax.ShapeDtypeStruct((1, Dm), F32)),
        grid=(LP // tm,), in_specs=[row, vec, row],
        out_specs=(pl.BlockSpec((1, LANE), lambda i: (0, 0)), row, vec),
        compiler_params=_params(("arbitrary",)))(h, g, tgt)


def _fgate_fwd(proj, bf_p, name):
    LP = proj.shape[0]
    nb = LP // LANE

    def body(f_ref, b_ref, c_ref, lf_ref):
        x = f_ref[...] + b_ref[...]
        lf_ref[...] = jnp.minimum(x, 0.0) - jnp.log1p(jnp.exp(-jnp.abs(x)))
        ri = lax.broadcasted_iota(jnp.int32, (LANE, LANE), 0)
        ci = lax.broadcasted_iota(jnp.int32, (LANE, LANE), 1)
        tri = jnp.where(ri >= ci, 1.0, 0.0).astype(F32)

        def blk(i, carry):
            rows = pl.ds(pl.multiple_of(i * LANE, LANE), LANE)
            cb = jnp.dot(tri, lf_ref[rows, :], precision=lax.Precision.HIGHEST,
                         preferred_element_type=F32) + carry
            c_ref[rows, :] = cb
            return cb[LANE - 1:LANE, :]

        lax.fori_loop(0, nb, blk, jnp.zeros((1, LANE), F32))

    return pl.pallas_call(
        body, name=name, out_shape=jax.ShapeDtypeStruct((LP, LANE), F32), grid=(1,),
        in_specs=[pl.BlockSpec((LP, LANE), lambda i: (0, F_COL_BLK)),
                  pl.BlockSpec((1, LANE), lambda i: (0, 0))],
        out_specs=pl.BlockSpec((LP, LANE), lambda i: (0, 0)),
        scratch_shapes=[pltpu.VMEM((LP, LANE), F32)],
        compiler_params=_params(("arbitrary",)))(proj, bf_p)


def _fgate_bwd(proj, bf_p, dc, name):
    LP = proj.shape[0]
    nb = LP // LANE

    def body(f_ref, b_ref, dc_ref, dl_ref, db_ref):
        ri = lax.broadcasted_iota(jnp.int32, (LANE, LANE), 0)
        ci = lax.broadcasted_iota(jnp.int32, (LANE, LANE), 1)
        triu = jnp.where(ri <= ci, 1.0, 0.0).astype(F32)
        bb = b_ref[...]

        tail = jnp.zeros((1, LANE), F32)
        dbs = jnp.zeros((1, LANE), F32)
        for i in range(nb - 1, -1, -1):
            rows = slice(i * LANE, (i + 1) * LANE)
            gb = jnp.dot(triu, dc_ref[rows, :], precision=lax.Precision.HIGHEST,
                         preferred_element_type=F32) + tail
            x = f_ref[rows, :] + bb
            dl = gb * _sigmoid_tail(-x)
            dl_ref[rows, :] = dl.astype(dl_ref.dtype)
            tail = gb[0:1, :]
            dbs = dbs + jnp.sum(dl, axis=0, keepdims=True)
        db_ref[...] = dbs

    return pl.pallas_call(
        body, name=name,
        out_shape=(jax.ShapeDtypeStruct((LP, LANE), BF16), jax.ShapeDtypeStruct((1, LANE), F32)),
        grid=(1,),
        in_specs=[pl.BlockSpec((LP, LANE), lambda i: (0, F_COL_BLK)),
                  pl.BlockSpec((1, LANE), lambda i: (0, 0)),
                  pl.BlockSpec((LP, LANE), lambda i: (0, 0))],
        out_specs=(pl.BlockSpec((LP, LANE), lambda i: (0, 0)), pl.BlockSpec((1, LANE), lambda i: (0, 0))),
        compiler_params=_params(("arbitrary",)))(proj, bf_p, dc)


def _attn_fwd(q, k, v, c_col, c_row, name):
    Hh, LP, Dh = q.shape
    nb = LP // ATT_BLK
    scale = Dh ** -0.5
    nt = (((1,), (1,)), ((), ()))

    def body(q_ref, k_ref, v_ref, cc_ref, cr_ref, o_ref, lse_ref):
        i = pl.program_id(1)
        qb = q_ref[...]
        cq = cc_ref[...]
        rows = i * ATT_BLK + lax.broadcasted_iota(jnp.int32, (ATT_BLK, ATT_BLK), 0)
        cols0 = lax.broadcasted_iota(jnp.int32, (ATT_BLK, ATT_BLK), 1)

        def step(j, carry):
            m, l, acc = carry
            ks = pl.ds(pl.multiple_of(j * ATT_BLK, ATT_BLK), ATT_BLK)
            s = lax.dot_general(qb, k_ref[ks, :], nt, preferred_element_type=F32) * scale
            s = s + cq - cr_ref[j]
            s = jnp.where(cols0 + j * ATT_BLK <= rows, s, NEG)
            m_new = jnp.maximum(m, jnp.max(s, axis=1, keepdims=True))
            p = jnp.exp(s - m_new)
            alpha = jnp.exp(m - m_new)
            l = alpha * l + jnp.sum(p, axis=1, keepdims=True)
            acc = alpha * acc + jnp.dot(p.astype(BF16), v_ref[ks, :], preferred_element_type=F32)
            return m_new, l, acc

        init = (jnp.full((ATT_BLK, 1), NEG, F32), jnp.zeros((ATT_BLK, 1), F32),
                jnp.zeros((ATT_BLK, Dh), F32))
        m, l, acc = lax.fori_loop(0, i + 1, step, init)
        o_ref[...] = acc / l
        lse_ref[...] = m + jnp.log(l)

    qspec = pl.BlockSpec((None, ATT_BLK, Dh), lambda h, i: (h, i, 0))
    kspec = pl.BlockSpec((None, LP, Dh), lambda h, i: (h, 0, 0))
    colspec = pl.BlockSpec((None, ATT_BLK, 1), lambda h, i: (h, i, 0))
    rowspec = pl.BlockSpec((None, nb, 1, ATT_BLK), lambda h, i: (h, 0, 0, 0))
    return pl.pallas_call(
        body, name=name,
        out_shape=(jax.ShapeDtypeStruct((Hh, LP, Dh), F32), jax.ShapeDtypeStruct((Hh, LP, 1), F32)),
        grid=(Hh, nb), in_specs=[qspec, kspec, kspec, colspec, rowspec],
        out_specs=(qspec, colspec),
        compiler_params=_params(("parallel", "arbitrary")))(q, k, v, c_col, c_row)


def _attn_bwd(q, k, v, o, do, lse, c_col, c_row, name):
    Hh, LP, Dh = q.shape
    nb = LP // ATT_BLK
    scale = Dh ** -0.5
    nt = (((1,), (1,)), ((), ()))
    tn = (((0,), (0,)), ((), ()))

    def body(q_ref, k_ref, v_ref, o_ref, do_ref, lse_ref, cc_ref, cr_ref,
             dq_ref, dk_ref, dv_ref, dcq_ref, dc_ref, delta_ref):
        j = pl.program_id(1)

        @pl.when(j == 0)
        def _():
            dq_ref[...] = jnp.zeros_like(dq_ref)
            dcq_ref[...] = jnp.zeros_like(dcq_ref)
            dob_all = do_ref[...].astype(BF16).astype(F32)
            delta_ref[...] = jnp.sum(dob_all * o_ref[...], axis=1, keepdims=True)

        kb = k_ref[...]
        vb = v_ref[...]
        ck = cr_ref[j]
        rows0 = lax.broadcasted_iota(jnp.int32, (ATT_BLK, ATT_BLK), 0)
        cols = j * ATT_BLK + lax.broadcasted_iota(jnp.int32, (ATT_BLK, ATT_BLK), 1)

        def step(i, carry):
            dk, dv, dcs = carry
            qs = pl.ds(pl.multiple_of(i * ATT_BLK, ATT_BLK), ATT_BLK)
            qb = q_ref[qs, :]
            dob = do_ref[qs, :].astype(BF16)
            s = lax.dot_general(qb, kb, nt, preferred_element_type=F32) * scale
            s = s + cc_ref[qs, :] - ck
            s = jnp.where(cols <= rows0 + i * ATT_BLK, s, NEG)
            p = jnp.exp(s - lse_ref[qs, :])
            dp = lax.dot_general(dob, vb, nt, preferred_element_type=F32)
            ds = p * (dp - delta_ref[qs, :])
            dsb = ds.astype(BF16)
            dv = dv + lax.dot_general(p.astype(BF16), dob, tn, preferred_element_type=F32)
            dk = dk + lax.dot_general(dsb, qb, tn, preferred_element_type=F32) * scale
            dq_ref[qs, :] += jnp.dot(dsb, kb, preferred_element_type=F32) * scale
            dcq_ref[qs, :] += jnp.sum(ds, axis=1, keepdims=True)
            dcs = dcs - jnp.sum(ds, axis=0, keepdims=True)
            return dk, dv, dcs

        init = (jnp.zeros((ATT_BLK, Dh), F32), jnp.zeros((ATT_BLK, Dh), F32),
                jnp.zeros((1, ATT_BLK), F32))
        dk, dv, dcs = lax.fori_loop(j, nb, step, init)
        dk_ref[...] = dk
        dv_ref[...] = dv
        dc_ref[...] = dcs

    full = pl.BlockSpec((None, LP, Dh), lambda h, j: (h, 0, 0))
    blk = pl.BlockSpec((None, ATT_BLK, Dh), lambda h, j: (h, j, 0))
    col = pl.BlockSpec((None, LP, 1), lambda h, j: (h, 0, 0))
    rowspec = pl.BlockSpec((None, nb, 1, ATT_BLK), lambda h, j: (h, 0, 0, 0))
    return pl.pallas_call(
        body, name=name,
        out_shape=(jax.ShapeDtypeStruct((Hh, LP, Dh), F32), jax.ShapeDtypeStruct((Hh, LP, Dh), F32),
                   jax.ShapeDtypeStruct((Hh, LP, Dh), F32), jax.ShapeDtypeStruct((Hh, LP, 1), F32),
                   jax.ShapeDtypeStruct((Hh, nb, 1, ATT_BLK), F32)),
        grid=(Hh, nb), in_specs=[full, blk, blk, full, full, col, col, rowspec],
        out_specs=(full, blk, blk, col, pl.BlockSpec((None, None, 1, ATT_BLK), lambda h, j: (h, j, 0, 0))),
        scratch_shapes=[pltpu.VMEM((LP, 1), F32)],
        compiler_params=_params(("parallel", "arbitrary")))(q, k, v, o, do, lse, c_col, c_row)


AUG = 128
ONES_IN_K = HEAD_DIM
ONES_IN_Q = HEAD_DIM + 3
ATT_HEADS_PER_STEP = 8
ATT_HEADS_PER_STEP_BWD = 8


def _attn_prep(proj, c, name):
    LP = proj.shape[0]
    nb = LP // ATT_BLK
    tail_rows = AUG - HEAD_DIM

    def body(q_ref, k_ref, v_ref, c_ref, qT_ref, kT_ref, ka_ref, vT_ref):
        qt = (q_ref[...] * (HEAD_DIM ** -0.5)).T
        kt = k_ref[...].T
        vt = v_ref[...].T
        ct = c_ref[...].T
        hi = ct.astype(BF16).astype(F32)
        r1 = ct - hi
        mid = r1.astype(BF16).astype(F32)
        lo = (r1 - mid).astype(BF16).astype(F32)
        row = lax.broadcasted_iota(jnp.int32, (tail_rows, ATT_BLK), 0)
        ones = jnp.where(row < 3, 1.0, 0.0)
        for h in range(HEADS):
            cparts = jnp.where(row == 0, hi[h:h + 1], jnp.where(row == 1, mid[h:h + 1],
                               jnp.where(row == 2, lo[h:h + 1], 0.0)))
            hs = slice(h * HEAD_DIM, (h + 1) * HEAD_DIM)
            q_tail = cparts + pltpu.roll(ones, 3, 0)
            k_tail = ones - pltpu.roll(cparts, 3, 0)
            qT_ref[h] = jnp.concatenate([qt[hs], q_tail], axis=0).astype(BF16)
            kfull = jnp.concatenate([kt[hs], k_tail], axis=0)
            kT_ref[h] = kfull.astype(BF16)
            ka_ref[h] = kfull.T.astype(BF16)
            vT_ref[h] = vt[hs].astype(BF16)

    col = lambda j: pl.BlockSpec((ATT_BLK, FOX_W), lambda i: (i, j))
    blk = lambda r: pl.BlockSpec((HEADS, None, r, ATT_BLK), lambda i: (0, i, 0, 0))
    return pl.pallas_call(
        body, name=name,
        out_shape=(jax.ShapeDtypeStruct((HEADS, nb, AUG, ATT_BLK), BF16),
                   jax.ShapeDtypeStruct((HEADS, nb, AUG, ATT_BLK), BF16),
                   jax.ShapeDtypeStruct((HEADS, LP, AUG), BF16),
                   jax.ShapeDtypeStruct((HEADS, nb, HEAD_DIM, ATT_BLK), BF16)),
        grid=(nb,), in_specs=[col(0), col(1), col(2), pl.BlockSpec((ATT_BLK, LANE), lambda i: (i, 0))],
        out_specs=(blk(AUG), blk(AUG), pl.BlockSpec((HEADS, ATT_BLK, AUG), lambda i: (0, i, 0)), blk(HEAD_DIM)),
        compiler_params=_params(("parallel",)))(proj, proj, proj, c)


def _attn_rows(xT, scale, out_dtype, name):
    Hh, nb, R, _ = xT.shape

    def body(x_ref, o_ref):
        stack = jnp.concatenate([x_ref[h, 0:HEAD_DIM, :] for h in range(Hh)], axis=0)
        o_ref[...] = (stack * scale).T.astype(o_ref.dtype)

    return pl.pallas_call(
        body, name=name, out_shape=jax.ShapeDtypeStruct((nb * ATT_BLK, Hh * HEAD_DIM), out_dtype), grid=(nb,),
        in_specs=[pl.BlockSpec((Hh, None, R, ATT_BLK), lambda i: (0, i, 0, 0))],
        out_specs=pl.BlockSpec((ATT_BLK, Hh * HEAD_DIM), lambda i: (i, 0)),
        compiler_params=_params(("parallel",)))(xT)


def _attn_cols(x, name):
    LP = x.shape[0]
    nb = LP // ATT_BLK

    def body(x_ref, o_ref):
        xt = x_ref[...].T
        for h in range(HEADS):
            o_ref[h] = xt[h * HEAD_DIM:(h + 1) * HEAD_DIM].astype(o_ref.dtype)

    return pl.pallas_call(
        body, name=name, out_shape=jax.ShapeDtypeStruct((HEADS, nb, HEAD_DIM, ATT_BLK), BF16), grid=(nb,),
        in_specs=[pl.BlockSpec((ATT_BLK, FOX_W), lambda i: (i, 0))],
        out_specs=pl.BlockSpec((HEADS, None, HEAD_DIM, ATT_BLK), lambda i: (0, i, 0, 0)),
        compiler_params=_params(("parallel",)))(x)


def _attn_dc(dqT, dkT, name):
    Hh, nb, _, _ = dqT.shape

    def body(q_ref, k_ref, o_ref):
        row = lax.broadcasted_iota(jnp.int32, (LANE, ATT_BLK), 0)
        acc = jnp.zeros((LANE, ATT_BLK), F32)
        for h in range(Hh):
            d = q_ref[h, ONES_IN_K:ONES_IN_K + 1, :] - k_ref[h, ONES_IN_Q:ONES_IN_Q + 1, :]
            acc = jnp.where(row == h, d, acc)
        o_ref[...] = acc.T

    spec = pl.BlockSpec((Hh, None, AUG, ATT_BLK), lambda i: (0, i, 0, 0))
    return pl.pallas_call(
        body, name=name, out_shape=jax.ShapeDtypeStruct((nb * ATT_BLK, LANE), F32), grid=(nb,),
        in_specs=[spec, spec], out_specs=pl.BlockSpec((ATT_BLK, LANE), lambda i: (i, 0)),
        compiler_params=_params(("parallel",)))(dqT, dkT)


def _attn_fwd2(qT, k_aug, vT, name, comm=None):
    Hh, nb, _, _ = qT.shape
    LP = nb * ATT_BLK
    Dh = vT.shape[2]
    HB = ATT_HEADS_PER_STEP
    n_comm = 0 if comm is None else len(comm[0])

    def body(*refs):
        q_ref, k_ref, v_ref = refs[:3]
        o_ref, lse_ref = refs[3 + n_comm:5 + n_comm]
        if n_comm:
            start, wait = _gather_first_ops(refs[3:3 + n_comm], refs[5 + n_comm:5 + 2 * n_comm],
                                            refs[5 + 2 * n_comm], refs[6 + 2 * n_comm], comm[1])
            pl.when(pl.program_id(0) == 0)(start)
        keys = lax.broadcasted_iota(jnp.int32, (ATT_BLK, ATT_BLK), 0)
        qrys = lax.broadcasted_iota(jnp.int32, (ATT_BLK, ATT_BLK), 1)
        causal = keys <= qrys

        def q_block(i, _):
            def tile(j, carry, masked):
                ks = pl.ds(pl.multiple_of(j * ATT_BLK, ATT_BLK), ATT_BLK)
                s_all = [jnp.dot(k_ref[hh, ks, :], q_ref[hh, i], preferred_element_type=F32) for hh in range(HB)]
                stats, p_all = [], []
                for hh in range(HB):
                    m, l, _ = carry[hh]
                    s = jnp.where(causal, s_all[hh], NEG) if masked else s_all[hh]
                    m_new = jnp.maximum(m, jnp.max(s, axis=0, keepdims=True))
                    p = jnp.exp(s - m_new)
                    alpha = jnp.exp(m - m_new)
                    stats.append((m_new, alpha * l + jnp.sum(p, axis=0, keepdims=True), alpha))
                    p_all.append(p.astype(BF16))
                out = []
                for hh in range(HB):
                    m_new, l, alpha = stats[hh]
                    acc = alpha * carry[hh][2] + jnp.dot(v_ref[hh, j], p_all[hh], preferred_element_type=F32)
                    out.append((m_new, l, acc))
                return tuple(out)

            init = tuple((jnp.full((1, ATT_BLK), NEG, F32), jnp.zeros((1, ATT_BLK), F32),
                          jnp.zeros((Dh, ATT_BLK), F32)) for _ in range(HB))
            carry = lax.fori_loop(0, i, lambda j, cr: tile(j, cr, False), init)
            carry = tile(i, carry, True)
            for hh in range(HB):
                m, l, acc = carry[hh]
                o_ref[hh, i] = acc / l
                lse_ref[hh, i] = m + jnp.log(l)
            return 0

        lax.fori_loop(0, nb, q_block, 0)
        if n_comm:
            pl.when(pl.program_id(0) == Hh // HB - 1)(wait)

    blk = lambda r: pl.BlockSpec((HB, nb, r, ATT_BLK), lambda h: (h, 0, 0, 0))
    out_shape = (jax.ShapeDtypeStruct((Hh, nb, Dh, ATT_BLK), F32), jax.ShapeDtypeStruct((Hh, nb, 1, ATT_BLK), F32))
    scratch = []
    args = [qT, k_aug, vT]
    if n_comm:
        out_shape += tuple(jax.ShapeDtypeStruct(cut.full, s.dtype) for cut, s in zip(comm[1], comm[0]))
        scratch = [pltpu.SemaphoreType.DMA((4 * n_comm,)), pltpu.SemaphoreType.DMA((4 * n_comm,))]
        args += list(comm[0])
    return pl.pallas_call(
        body, name=name, out_shape=out_shape, grid=(Hh // HB,),
        in_specs=[blk(AUG), pl.BlockSpec((HB, LP, AUG), lambda h: (h, 0, 0)), blk(Dh)] + [ANY] * n_comm,
        out_specs=(blk(Dh), blk(1)) + (ANY,) * n_comm, scratch_shapes=scratch,
        compiler_params=_params(("arbitrary",)))(*args)


def _attn_bwd2(qT, kT, k_aug, v, oT, doT, lse, name, comm=None):
    Hh, nb, _, _ = qT.shape
    LP = nb * ATT_BLK
    Dh = v.shape[2]
    nt = (((1,), (1,)), ((), ()))
    tn = (((0,), (0,)), ((), ()))

    HB = ATT_HEADS_PER_STEP_BWD
    n_comm = 0 if comm is None else len(comm[0])

    def body(*refs):
        q_ref, kt_ref, k_ref, v_ref, o_ref, do_ref, lse_ref = refs[:7]
        parts = refs[7:7 + n_comm]
        dq_ref, dk_ref, dv_ref = refs[7 + n_comm:10 + n_comm]
        others = refs[10 + n_comm:10 + 2 * n_comm]
        delta_ref = refs[10 + 2 * n_comm]
        if n_comm:
            start, wait = _chip_exchange_ops(parts, others, refs[11 + 2 * n_comm], refs[12 + 2 * n_comm], comm[1])
            pl.when(pl.program_id(0) == 0)(start)
        keys = lax.broadcasted_iota(jnp.int32, (ATT_BLK, ATT_BLK), 0)
        qrys = lax.broadcasted_iota(jnp.int32, (ATT_BLK, ATT_BLK), 1)
        causal = keys <= qrys

        def prep(i, _):
            for hh in range(HB):
                delta_ref[hh, i] = jnp.sum(do_ref[hh, i].astype(F32) * o_ref[hh, i], axis=0, keepdims=True)
                dq_ref[hh, i] = jnp.zeros((AUG, ATT_BLK), F32)
            return 0

        lax.fori_loop(0, nb, prep, 0)

        def kv_block(j, _):
            ks = pl.ds(pl.multiple_of(j * ATT_BLK, ATT_BLK), ATT_BLK)

            def tile(i, carry, masked):
                s_all = [jnp.dot(k_ref[hh, ks, :], q_ref[hh, i], preferred_element_type=F32) for hh in range(HB)]
                dp_all = [lax.dot_general(v_ref[hh, j], do_ref[hh, i], tn, preferred_element_type=F32)
                          for hh in range(HB)]
                p_all, ds_all = [], []
                for hh in range(HB):
                    s = jnp.where(causal, s_all[hh], NEG) if masked else s_all[hh]
                    p = jnp.exp(s - lse_ref[hh, i])
                    ds_all.append((p * (dp_all[hh] - delta_ref[hh, i])).astype(BF16))
                    p_all.append(p.astype(BF16))
                out = []
                for hh in range(HB):
                    dk, dv = carry[hh]
                    dv = dv + lax.dot_general(do_ref[hh, i], p_all[hh], nt, preferred_element_type=F32)
                    dk = dk + lax.dot_general(q_ref[hh, i], ds_all[hh], nt, preferred_element_type=F32)
                    out.append((dk, dv))
                dq_new = [jnp.dot(kt_ref[hh, j], ds_all[hh], preferred_element_type=F32) for hh in range(HB)]
                for hh in range(HB):
                    dq_ref[hh, i] += dq_new[hh]
                return tuple(out)

            init = tuple((jnp.zeros((AUG, ATT_BLK), F32), jnp.zeros((Dh, ATT_BLK), F32)) for _ in range(HB))
            carry = tile(j, init, True)
            carry = lax.fori_loop(j + 1, nb, lambda i, cr: tile(i, cr, False), carry)
            for hh in range(HB):
                dk_ref[hh, j] = carry[hh][0]
                dv_ref[hh, j] = carry[hh][1]
            return 0

        lax.fori_loop(0, nb, kv_block, 0)
        if n_comm:
            pl.when(pl.program_id(0) == Hh // HB - 1)(wait)

    blk = lambda r: pl.BlockSpec((HB, nb, r, ATT_BLK), lambda h: (h, 0, 0, 0))
    row = lambda cols: pl.BlockSpec((HB, LP, cols), lambda h: (h, 0, 0))
    out_shape = (jax.ShapeDtypeStruct((Hh, nb, AUG, ATT_BLK), F32), jax.ShapeDtypeStruct((Hh, nb, AUG, ATT_BLK), F32),
                 jax.ShapeDtypeStruct((Hh, nb, Dh, ATT_BLK), F32))
    scratch = [pltpu.VMEM((HB, nb, 1, ATT_BLK), F32)]
    args = [qT, kT, k_aug, v, oT, doT, lse]
    if n_comm:
        out_shape += _chip_exchange_shapes(*comm)
        scratch += [pltpu.SemaphoreType.DMA((3 * n_comm,)), pltpu.SemaphoreType.DMA((3 * n_comm,))]
        args += list(comm[0])
    return pl.pallas_call(
        body, name=name, out_shape=out_shape, grid=(Hh // HB,),
        in_specs=[blk(AUG), blk(AUG), row(AUG), blk(Dh), blk(Dh), blk(Dh), blk(1)] + [ANY] * n_comm,
        out_specs=(blk(AUG), blk(AUG), blk(Dh)) + (ANY,) * n_comm,
        scratch_shapes=scratch,
        compiler_params=_params(("arbitrary",)))(*args)


CONV_HALO = 32
A_BLK = 3 * FOX_W // CONV_CH
G_BLK = A_BLK + 1


def _conf_fwd(proj, cw, cb, lg, lb, tm, name):
    LP = proj.shape[0]
    C = CONV_CH
    sub = _sub_rows(tm)
    hpb = tm // CONV_HALO

    def body(a_ref, g_ref, ah_ref, gh_ref, w_ref, cb_ref, lg_ref, lb_ref, u1_ref, u_ref, buf):
        r = pl.program_id(0)
        buf[CONV_HALO:CONV_HALO + tm, :] = a_ref[...] * _sigmoid(g_ref[...])
        buf[0:CONV_HALO, :] = jnp.where(r > 0, ah_ref[...] * _sigmoid(gh_ref[...]), 0.0)
        for s in range(tm // sub):
            for ct in range(C // LANE):
                ln = slice(ct * LANE, (ct + 1) * LANE)
                acc = jnp.broadcast_to(cb_ref[:, ln], (sub, LANE))
                for kk in range(CONV_K):
                    off = CONV_HALO + s * sub - (CONV_K - 1) + kk
                    acc = acc + w_ref[kk:kk + 1, ln] * buf[off:off + sub, ln]
                u1_ref[s * sub:(s + 1) * sub, ln] = acc
        u1 = u1_ref[...]
        mu = jnp.mean(u1, axis=1, keepdims=True)
        xc = u1 - mu
        var = jnp.mean(xc * xc, axis=1, keepdims=True)
        y = xc * lax.rsqrt(var + LN_EPS) * lg_ref[...] + lb_ref[...]
        u_ref[...] = (y * _sigmoid(y)).astype(u_ref.dtype)

    cur = lambda blk: pl.BlockSpec((tm, C), lambda r: (r, blk))
    halo = lambda blk: pl.BlockSpec((CONV_HALO, C), lambda r: (jnp.maximum(r * hpb - 1, 0), blk))
    vec = pl.BlockSpec((1, C), lambda r: (0, 0))
    out = pl.BlockSpec((tm, C), lambda r: (r, 0))
    return pl.pallas_call(
        body, name=name,
        out_shape=(jax.ShapeDtypeStruct((LP, C), F32), jax.ShapeDtypeStruct((LP, C), BF16)),
        grid=(LP // tm,),
        in_specs=[cur(A_BLK), cur(G_BLK), halo(A_BLK), halo(G_BLK),
                  pl.BlockSpec((CONV_HALO, C), lambda r: (0, 0)), vec, vec, vec],
        out_specs=(out, out),
        scratch_shapes=[pltpu.VMEM((CONV_HALO + tm, C), F32)],
        compiler_params=_params(("parallel",)))(proj, proj, proj, proj, cw, cb, lg, lb)


def _conf_bwd(proj, u1, dcat, cw, lg, lb, tm, name):
    LP = proj.shape[0]
    C = CONV_CH
    sub = _sub_rows(tm)
    hpb = tm // CONV_HALO
    nblk = LP // tm
    last_halo = LP // CONV_HALO - 1

    def body(a_ref, g_ref, ah_ref, gh_ref, u1_ref, u1n_ref, du_ref, dun_ref, w_ref, lg_ref, lb_ref,
             dadg_ref, dw_ref, dcb_ref, dlg_ref, dlb_ref, ubuf, dbuf, du0):
        r = pl.program_id(0)
        lgv = lg_ref[...]
        lbv = lb_ref[...]

        def ln_silu_bwd(u1v, duv):
            mu = jnp.mean(u1v, axis=1, keepdims=True)
            xc = u1v - mu
            rstd = lax.rsqrt(jnp.mean(xc * xc, axis=1, keepdims=True) + LN_EPS)
            xhat = xc * rstd
            y = xhat * lgv + lbv
            sg = _sigmoid(y)
            dy = duv * (sg * (1.0 + y * (1.0 - sg)))
            dxh = dy * lgv
            du1 = rstd * (dxh - jnp.mean(dxh, axis=1, keepdims=True)
                          - xhat * jnp.mean(dxh * xhat, axis=1, keepdims=True))
            return du1, dy, xhat

        @pl.when(r == 0)
        def _():
            dw_ref[...] = jnp.zeros_like(dw_ref)
            dcb_ref[...] = jnp.zeros_like(dcb_ref)
            dlg_ref[...] = jnp.zeros_like(dlg_ref)
            dlb_ref[...] = jnp.zeros_like(dlb_ref)

        du1, dy, xhat = ln_silu_bwd(u1_ref[...], du_ref[...])
        dlg_ref[...] += jnp.sum(dy * xhat, axis=0, keepdims=True)
        dlb_ref[...] += jnp.sum(dy, axis=0, keepdims=True)
        dcb_ref[...] += jnp.sum(du1, axis=0, keepdims=True)
        dbuf[0:tm, :] = du1
        du1n, _, _ = ln_silu_bwd(u1n_ref[...], dun_ref[...])
        dbuf[tm:tm + CONV_HALO, :] = jnp.where(r < nblk - 1, du1n, 0.0)
        ubuf[CONV_HALO:CONV_HALO + tm, :] = a_ref[...] * _sigmoid(g_ref[...])
        ubuf[0:CONV_HALO, :] = jnp.where(r > 0, ah_ref[...] * _sigmoid(gh_ref[...]), 0.0)

        for ct in range(C // LANE):
            ln = slice(ct * LANE, (ct + 1) * LANE)
            for s in range(tm // sub):
                d_here = dbuf[s * sub:(s + 1) * sub, ln]
                acc = jnp.zeros((sub, LANE), F32)
                for kk in range(CONV_K):
                    fo = s * sub + (CONV_K - 1) - kk
                    acc = acc + w_ref[kk:kk + 1, ln] * dbuf[fo:fo + sub, ln]
                    bo = CONV_HALO + s * sub - (CONV_K - 1) + kk
                    dw_ref[kk:kk + 1, ln] += jnp.sum(d_here * ubuf[bo:bo + sub, ln], axis=0, keepdims=True)
                du0[s * sub:(s + 1) * sub, ln] = acc
        a = a_ref[...]
        sg = _sigmoid(g_ref[...])
        d0 = du0[...]
        dadg_ref[:, 0:C] = (d0 * sg).astype(dadg_ref.dtype)
        dadg_ref[:, C:2 * C] = (d0 * a * sg * (1.0 - sg)).astype(dadg_ref.dtype)

    cur = lambda blk: pl.BlockSpec((tm, C), lambda r: (r, blk))
    prev = lambda blk: pl.BlockSpec((CONV_HALO, C), lambda r: (jnp.maximum(r * hpb - 1, 0), blk))
    nxt = lambda blk: pl.BlockSpec((CONV_HALO, C), lambda r: (jnp.minimum((r + 1) * hpb, last_halo), blk))
    vec = pl.BlockSpec((1, C), lambda r: (0, 0))
    wspec = pl.BlockSpec((CONV_HALO, C), lambda r: (0, 0))
    return pl.pallas_call(
        body, name=name,
        out_shape=(jax.ShapeDtypeStruct((LP, 2 * C), BF16), jax.ShapeDtypeStruct((CONV_HALO, C), F32),
                   jax.ShapeDtypeStruct((1, C), F32), jax.ShapeDtypeStruct((1, C), F32),
                   jax.ShapeDtypeStruct((1, C), F32)),
        grid=(nblk,),
        in_specs=[cur(A_BLK), cur(G_BLK), prev(A_BLK), prev(G_BLK), cur(0), nxt(0), cur(1), nxt(1),
                  wspec, vec, vec],
        out_specs=(pl.BlockSpec((tm, 2 * C), lambda r: (r, 0)), wspec, vec, vec, vec),
        scratch_shapes=[pltpu.VMEM((CONV_HALO + tm, C), F32), pltpu.VMEM((tm + CONV_HALO, C), F32),
                        pltpu.VMEM((tm, C), F32)],
        compiler_params=_params(("arbitrary",)))(proj, proj, proj, proj, u1, u1, dcat, dcat, cw, lg, lb)


FFN_HALO = 8
FFN_TC = 256
FFN_K = 3


def _ffn_conv(buf, w_ref, b_ref, s, sub, ln):
    acc = jnp.broadcast_to(b_ref[:, ln], (sub, LANE))
    for kk in range(FFN_K):
        off = FFN_HALO + s * sub - (FFN_K - 1) + kk
        acc = acc + w_ref[kk:kk + 1, ln] * buf[off:off + sub, ln]
    return acc


def _ffn_act_fwd(up, w, b, tm, name, host=None):
    LP, F = up.shape[0], up.shape[1] // 2
    upg = upv = up
    nct = F // FFN_TC
    sub = _sub_rows(tm)
    hpb = tm // FFN_HALO
    n_host = 0 if host is None else len(host[0])
    nrb = LP // tm

    def body(*refs):
        g_ref, v_ref, gh_ref, vh_ref, wg_ref, wv_ref, bg_ref, bv_ref = refs[:8]
        act_ref = refs[8 + n_host]
        gbuf, vbuf = refs[9 + 2 * n_host:11 + 2 * n_host]
        if n_host:
            start, wait = _gather_first_ops(refs[8:8 + n_host], refs[9 + n_host:9 + 2 * n_host],
                                            refs[11 + 2 * n_host], refs[12 + 2 * n_host], host[1])
            pl.when(jnp.logical_and(pl.program_id(0) == 0, pl.program_id(1) == 0))(start)
        r = pl.program_id(1)
        gbuf[FFN_HALO:FFN_HALO + tm, :] = g_ref[...]
        vbuf[FFN_HALO:FFN_HALO + tm, :] = v_ref[...]
        gbuf[0:FFN_HALO, :] = jnp.where(r > 0, gh_ref[...], 0.0)
        vbuf[0:FFN_HALO, :] = jnp.where(r > 0, vh_ref[...], 0.0)
        for s in range(tm // sub):
            for ct in range(FFN_TC // LANE):
                ln = slice(ct * LANE, (ct + 1) * LANE)
                gc = _ffn_conv(gbuf, wg_ref, bg_ref, s, sub, ln)
                vc = _ffn_conv(vbuf, wv_ref, bv_ref, s, sub, ln)
                act_ref[s * sub:(s + 1) * sub, ln] = (gc * _sigmoid(gc) * vc).astype(act_ref.dtype)
        if n_host:
            pl.when(jnp.logical_and(pl.program_id(0) == nct - 1, pl.program_id(1) == nrb - 1))(wait)

    cur = pl.BlockSpec((tm, FFN_TC), lambda c, r: (r, c))
    halo = pl.BlockSpec((FFN_HALO, FFN_TC), lambda c, r: (jnp.maximum(r * hpb - 1, 0), c))
    wg = pl.BlockSpec((8, FFN_TC), lambda c, r: (0, c))
    wv = pl.BlockSpec((8, FFN_TC), lambda c, r: (0, nct + c))
    bg = pl.BlockSpec((1, FFN_TC), lambda c, r: (0, c))
    bv = pl.BlockSpec((1, FFN_TC), lambda c, r: (0, nct + c))
    curv = pl.BlockSpec((tm, FFN_TC), lambda c, r: (r, nct + c))
    halov = pl.BlockSpec((FFN_HALO, FFN_TC), lambda c, r: (jnp.maximum(r * hpb - 1, 0), nct + c))
    out_shape = jax.ShapeDtypeStruct((LP, F), BF16)
    out_specs = cur
    in_specs = [cur, curv, halo, halov, wg, wv, bg, bv]
    args = [upg, upv, upg, upv, w, w, b, b]
    scratch = [pltpu.VMEM((FFN_HALO + tm, FFN_TC), F32)] * 2
    if n_host:
        anyspec = pl.BlockSpec(memory_space=pl.ANY)
        in_specs += [anyspec] * n_host
        args += list(host[0])
        out_shape = (out_shape,) + tuple(jax.ShapeDtypeStruct(cut.full, s.dtype) for cut, s in zip(host[1], host[0]))
        out_specs = (cur,) + (anyspec,) * n_host
        scratch += [pltpu.SemaphoreType.DMA((4 * n_host,)), pltpu.SemaphoreType.DMA((4 * n_host,))]
    sem = ("arbitrary", "arbitrary") if n_host else ("parallel", "parallel")
    return pl.pallas_call(
        body, name=name, out_shape=out_shape, grid=(nct, nrb), in_specs=in_specs, out_specs=out_specs,
        scratch_shapes=scratch, compiler_params=_params(sem))(*args)


def _ffn_act_bwd(up, dact, w, b, tm, name):
    LP, F = up.shape[0], up.shape[1] // 2
    upg = upv = up
    nct = F // FFN_TC
    sub = _sub_rows(tm)
    hpb = tm // FFN_HALO
    nblk = LP // tm
    last_halo = LP // FFN_HALO - 1
    TB = tm + 2 * FFN_HALO

    def body(g_ref, v_ref, gp_ref, vp_ref, gn_ref, vn_ref, da_ref, dan_ref,
             wg_ref, wv_ref, bg_ref, bv_ref,
             dup_ref, dwg_ref, dwv_ref, dbg_ref, dbv_ref, gbuf, vbuf, dgb, dvb):
        r = pl.program_id(1)
        dg_ref = dup_ref.at[0]
        dv_ref = dup_ref.at[1]
        first = r == 0
        last = r == nblk - 1

        @pl.when(first)
        def _():
            dwg_ref[...] = jnp.zeros_like(dwg_ref)
            dwv_ref[...] = jnp.zeros_like(dwv_ref)
            dbg_ref[...] = jnp.zeros_like(dbg_ref)
            dbv_ref[...] = jnp.zeros_like(dbv_ref)

        for buf, c_ref, p_ref, n_ref in ((gbuf, g_ref, gp_ref, gn_ref), (vbuf, v_ref, vp_ref, vn_ref)):
            buf[0:FFN_HALO, :] = jnp.where(first, 0.0, p_ref[...])
            buf[FFN_HALO:FFN_HALO + tm, :] = c_ref[...]
            buf[FFN_HALO + tm:TB, :] = jnp.where(last, 0.0, n_ref[...])

        def dconv(s0, nrows, ln, dact_v):
            xg = [gbuf[s0 - (FFN_K - 1) + kk:s0 - (FFN_K - 1) + kk + nrows, ln] for kk in range(FFN_K)]
            xv = [vbuf[s0 - (FFN_K - 1) + kk:s0 - (FFN_K - 1) + kk + nrows, ln] for kk in range(FFN_K)]
            gc = jnp.broadcast_to(bg_ref[:, ln], (nrows, LANE))
            vc = jnp.broadcast_to(bv_ref[:, ln], (nrows, LANE))
            for kk in range(FFN_K):
                gc = gc + wg_ref[kk:kk + 1, ln] * xg[kk]
                vc = vc + wv_ref[kk:kk + 1, ln] * xv[kk]
            sg = _sigmoid(gc)
            return dact_v * vc * (sg * (1.0 + gc * (1.0 - sg))), dact_v * (gc * sg), xg, xv

        colsum = lambda t: jnp.sum(t, axis=0, keepdims=True)
        for ct in range(FFN_TC // LANE):
            ln = slice(ct * LANE, (ct + 1) * LANE)
            zero = jnp.zeros((1, LANE), F32)
            dwg, dwv, dbg, dbv = [zero] * FFN_K, [zero] * FFN_K, zero, zero
            for s in range(tm // sub):
                dgc, dvc, xg, xv = dconv(FFN_HALO + s * sub, sub, ln, da_ref[s * sub:(s + 1) * sub, ln])
                dgb[s * sub:(s + 1) * sub, ln] = dgc
                dvb[s * sub:(s + 1) * sub, ln] = dvc
                dwg = [dwg[kk] + colsum(dgc * xg[kk]) for kk in range(FFN_K)]
                dwv = [dwv[kk] + colsum(dvc * xv[kk]) for kk in range(FFN_K)]
                dbg, dbv = dbg + colsum(dgc), dbv + colsum(dvc)
            for kk in range(FFN_K):
                dwg_ref[kk:kk + 1, ln] += dwg[kk]
                dwv_ref[kk:kk + 1, ln] += dwv[kk]
            dbg_ref[:, ln] += dbg
            dbv_ref[:, ln] += dbv
            dgc, dvc, _, _ = dconv(FFN_HALO + tm, FFN_HALO, ln, jnp.where(last, 0.0, dan_ref[:, ln]))
            dgb[tm:tm + FFN_HALO, ln] = dgc
            dvb[tm:tm + FFN_HALO, ln] = dvc
            for dbuf, w_ref, dout in ((dgb, wg_ref, dg_ref), (dvb, wv_ref, dv_ref)):
                for s in range(tm // sub):
                    acc = jnp.zeros((sub, LANE), F32)
                    for kk in range(FFN_K):
                        fo = s * sub + (FFN_K - 1) - kk
                        acc = acc + w_ref[kk:kk + 1, ln] * dbuf[fo:fo + sub, ln]
                    dout[s * sub:(s + 1) * sub, ln] = acc.astype(dout.dtype)

    cur = pl.BlockSpec((tm, FFN_TC), lambda c, r: (r, c))
    prev = pl.BlockSpec((FFN_HALO, FFN_TC), lambda c, r: (jnp.maximum(r * hpb - 1, 0), c))
    nxt = pl.BlockSpec((FFN_HALO, FFN_TC), lambda c, r: (jnp.minimum((r + 1) * hpb, last_halo), c))
    wg = pl.BlockSpec((8, FFN_TC), lambda c, r: (0, c))
    wv = pl.BlockSpec((8, FFN_TC), lambda c, r: (0, nct + c))
    bg = pl.BlockSpec((1, FFN_TC), lambda c, r: (0, c))
    bv = pl.BlockSpec((1, FFN_TC), lambda c, r: (0, nct + c))
    curv = pl.BlockSpec((tm, FFN_TC), lambda c, r: (r, nct + c))
    prevv = pl.BlockSpec((FFN_HALO, FFN_TC), lambda c, r: (jnp.maximum(r * hpb - 1, 0), nct + c))
    nxtv = pl.BlockSpec((FFN_HALO, FFN_TC), lambda c, r: (jnp.minimum((r + 1) * hpb, last_halo), nct + c))
    dup, dwg, dwv, dbg, dbv = pl.pallas_call(
        body, name=name,
        out_shape=(jax.ShapeDtypeStruct((2, LP, F), BF16),
                   jax.ShapeDtypeStruct((8, F), F32), jax.ShapeDtypeStruct((8, F), F32),
                   jax.ShapeDtypeStruct((1, F), F32), jax.ShapeDtypeStruct((1, F), F32)),
        grid=(nct, nblk),
        in_specs=[cur, curv, prev, prevv, nxt, nxtv, cur, nxt, wg, wv, bg, bv],
        out_specs=(pl.BlockSpec((2, tm, FFN_TC), lambda c, r: (0, r, c)),
                   pl.BlockSpec((8, FFN_TC), lambda c, r: (0, c)),
                   pl.BlockSpec((8, FFN_TC), lambda c, r: (0, c)),
                   pl.BlockSpec((1, FFN_TC), lambda c, r: (0, c)),
                   pl.BlockSpec((1, FFN_TC), lambda c, r: (0, c))),
        scratch_shapes=[pltpu.VMEM((TB, FFN_TC), F32), pltpu.VMEM((TB, FFN_TC), F32),
                        pltpu.VMEM((tm + FFN_HALO, FFN_TC), F32), pltpu.VMEM((tm + FFN_HALO, FFN_TC), F32)],
        compiler_params=_params(("parallel", "arbitrary")))(
            upg, upv, upg, upv, upg, upv, dact, dact, w, w, b, b)
    return dup, jnp.concatenate([dwg, dwv], axis=1), jnp.concatenate([dbg, dbv], axis=1)


POOL_HALO = 16


def _pool_fwd(h, g, pw, pb, ps, tm, name):
    LP, Dm = h.shape
    sub = _sub_rows(tm)
    hpb = tm // POOL_HALO

    def body(h_ref, hh_ref, g_ref, pw_ref, pb_ref, ps_ref, o_ref, d_ref, buf):
        r = pl.program_id(0)
        gg = g_ref[...]

        def norm(x):
            return x * lax.rsqrt(jnp.mean(x * x, axis=1, keepdims=True) + RMS_EPS) * gg

        x = h_ref[...]
        buf[POOL_HALO:POOL_HALO + tm, :] = norm(x)
        buf[0:POOL_HALO, :] = jnp.where(r > 0, norm(hh_ref[...]), 0.0)
        for gi, w in enumerate(POOL_WINDOWS):
            ln = slice(gi * POOL_G, (gi + 1) * POOL_G)
            for s in range(tm // sub):
                base = POOL_HALO + s * sub
                acc = buf[base:base + sub, ln]
                for jj in range(1, w):
                    acc = acc + buf[base - jj:base - jj + sub, ln]
                t = r * tm + s * sub + lax.broadcasted_iota(jnp.int32, (sub, 1), 0)
                cnt = jnp.minimum(t + 1, w).astype(F32)
                d_ref[s * sub:(s + 1) * sub, ln] = (acc / cnt - buf[base:base + sub, ln]).astype(d_ref.dtype)
            y = jnp.dot(d_ref[:, ln], pw_ref[gi], preferred_element_type=F32) + pb_ref[:, ln]
            o_ref[:, ln] = x[:, ln] + y * ps_ref[:, ln]

    row = pl.BlockSpec((tm, Dm), lambda r: (r, 0))
    halo = pl.BlockSpec((POOL_HALO, Dm), lambda r: (jnp.maximum(r * hpb - 1, 0), 0))
    vec = pl.BlockSpec((1, Dm), lambda r: (0, 0))
    wsp = pl.BlockSpec((len(POOL_WINDOWS), POOL_G, POOL_G), lambda r: (0, 0, 0))
    return pl.pallas_call(
        body, name=name,
        out_shape=(jax.ShapeDtypeStruct((LP, Dm), F32), jax.ShapeDtypeStruct((LP, Dm), BF16)),
        grid=(LP // tm,), in_specs=[row, halo, vec, wsp, vec, vec], out_specs=(row, row),
        scratch_shapes=[pltpu.VMEM((POOL_HALO + tm, Dm), F32)],
        compiler_params=_params(("parallel",)))(h, h, g, pw, pb, ps)


def _pool_bwd(h, g, d, pw, pb, ps, dh_out, tm, name):
    LP, Dm = h.shape
    sub = _sub_rows(tm)
    hpb = tm // POOL_HALO
    nblk = LP // tm
    last_halo = LP // POOL_HALO - 1
    nt = (((1,), (1,)), ((), ()))
    tn = (((0,), (0,)), ((), ()))

    def body(h_ref, g_ref, d_ref, pw_ref, pb_ref, ps_ref, do_ref, don_ref,
             dh_ref, dpw_ref, dpb_ref, dps_ref, dg_ref, ebuf, ddb, dnb):
        r = pl.program_id(0)

        @pl.when(r == 0)
        def _():
            dpw_ref[...] = jnp.zeros_like(dpw_ref)
            dpb_ref[...] = jnp.zeros_like(dpb_ref)
            dps_ref[...] = jnp.zeros_like(dps_ref)
            dg_ref[...] = jnp.zeros_like(dg_ref)

        for gi, w in enumerate(POOL_WINDOWS):
            ln = slice(gi * POOL_G, (gi + 1) * POOL_G)
            wg = pw_ref[gi]
            dog = do_ref[:, ln]
            dg_b = d_ref[:, ln]
            y_pre = jnp.dot(dg_b, wg, preferred_element_type=F32) + pb_ref[:, ln]
            dps_ref[:, ln] += jnp.sum(dog * y_pre, axis=0, keepdims=True)
            dy = dog * ps_ref[:, ln]
            dpb_ref[:, ln] += jnp.sum(dy, axis=0, keepdims=True)
            dyb = dy.astype(BF16)
            dpw_ref[gi] += lax.dot_general(dg_b, dyb, tn, preferred_element_type=F32)
            dd = lax.dot_general(dyb, wg, nt, preferred_element_type=F32)
            ddb[:, ln] = dd
            t = r * tm + lax.broadcasted_iota(jnp.int32, (tm, 1), 0)
            ebuf[0:tm, ln] = dd / jnp.minimum(t + 1, w).astype(F32)
            dyn = (don_ref[:, ln] * ps_ref[:, ln]).astype(BF16)
            ddn = lax.dot_general(dyn, wg, nt, preferred_element_type=F32)
            tn_ = (r + 1) * tm + lax.broadcasted_iota(jnp.int32, (POOL_HALO, 1), 0)
            ebuf[tm:tm + POOL_HALO, ln] = jnp.where(r < nblk - 1, ddn / jnp.minimum(tn_ + 1, w).astype(F32), 0.0)
            for s in range(tm // sub):
                acc = ebuf[s * sub:(s + 1) * sub, ln]
                for jj in range(1, w):
                    acc = acc + ebuf[s * sub + jj:s * sub + jj + sub, ln]
                dnb[s * sub:(s + 1) * sub, ln] = acc - ddb[s * sub:(s + 1) * sub, ln]
        x = h_ref[...]
        rr = lax.rsqrt(jnp.mean(x * x, axis=1, keepdims=True) + RMS_EPS)
        xhat = x * rr
        dn = dnb[...]
        dxh = dn * g_ref[...]
        dh_ref[...] = do_ref[...] + rr * (dxh - xhat * jnp.mean(dxh * xhat, axis=1, keepdims=True))
        dg_ref[...] += jnp.sum(dn * xhat, axis=0, keepdims=True)

    row = pl.BlockSpec((tm, Dm), lambda r: (r, 0))
    nxt = pl.BlockSpec((POOL_HALO, Dm), lambda r: (jnp.minimum((r + 1) * hpb, last_halo), 0))
    vec = pl.BlockSpec((1, Dm), lambda r: (0, 0))
    wsp = pl.BlockSpec((len(POOL_WINDOWS), POOL_G, POOL_G), lambda r: (0, 0, 0))
    return pl.pallas_call(
        body, name=name,
        out_shape=(jax.ShapeDtypeStruct((LP, Dm), F32),
                   jax.ShapeDtypeStruct((len(POOL_WINDOWS), POOL_G, POOL_G), F32),
                   jax.ShapeDtypeStruct((1, Dm), F32), jax.ShapeDtypeStruct((1, Dm), F32),
                   jax.ShapeDtypeStruct((1, Dm), F32)),
        grid=(nblk,), in_specs=[row, vec, row, wsp, vec, vec, row, nxt],
        out_specs=(row, wsp, vec, vec, vec),
        scratch_shapes=[pltpu.VMEM((tm + POOL_HALO, Dm), F32), pltpu.VMEM((tm, Dm), F32),
                        pltpu.VMEM((tm, Dm), F32)],
        compiler_params=_params(("arbitrary",)))(h, g, d, pw, pb, ps, dh_out, dh_out)


def _adamw(w, g, m, v, name):
    shape = w.shape
    cols = shape[-1]
    rows = int(np.prod(shape[:-1])) if len(shape) > 1 else 1
    w2, g2, m2, v2 = (t.reshape(rows, cols) for t in (w, g, m, v))
    tr = rows
    for cand in (256, 128, 64, 32, 16, 8):
        if rows % cand == 0 and rows > cand:
            tr = cand
            break
    c1 = float(1.0 - ADAM_B1 ** ADAM_STEP)
    c2 = float(1.0 - ADAM_B2 ** ADAM_STEP)

    def body(w_ref, g_ref, m_ref, v_ref, d_ref, mo_ref, vo_ref):
        gg = g_ref[...]
        mn = ADAM_B1 * m_ref[...] + (1.0 - ADAM_B1) * gg
        vn = ADAM_B2 * v_ref[...] + (1.0 - ADAM_B2) * (gg * gg)
        m_hat = mn / c1
        v_hat = vn / c2
        d_ref[...] = -ADAM_LR * (m_hat / (jnp.sqrt(v_hat) + ADAM_EPS) + ADAM_WD * w_ref[...])
        mo_ref[...] = mn
        vo_ref[...] = vn

    spec = pl.BlockSpec((tr, cols), lambda i: (i, 0))
    sds = jax.ShapeDtypeStruct((rows, cols), F32)
    d2, mo, vo = pl.pallas_call(
        body, name=name, out_shape=(sds, sds, sds), grid=(rows // tr,),
        in_specs=[spec] * 4, out_specs=(spec,) * 3,
        compiler_params=_params(("parallel",)))(w2, g2, m2, v2)
    return d2.reshape(shape), mo.reshape(shape), vo.reshape(shape)


def _row_tiles(LP):
    tm = LP // 4
    assert LP % 4 == 0 and tm % CONV_HALO == 0 and LP % ATT_BLK == 0, LP
    return tm, LP // 2


def _heads(t, LP):
    return t.reshape(LP, HEADS, HEAD_DIM).transpose(1, 0, 2)


def _unheads(t, LP):
    return t.transpose(1, 0, 2).reshape(LP, FOX_W)


def _ffn_fwd(h, gain, wug, wuv, cw, cb, wd, tm, tmm, tag):
    n = _rms_fwd(h, gain, BF16, tm, f"ffn_norm_{tag}")
    upg = _mm(n, wug, "nn", F32, tmm, 256, f"ffn_up_gate_{tag}")
    upv = _mm(n, wuv, "nn", F32, tmm, 256, f"ffn_up_val_{tag}")
    act = _ffn_act_fwd(upg, upv, cw, cb, tm, f"ffn_act_{tag}")
    out = _mm(act, wd, "nn", F32, tmm, 512, f"ffn_down_{tag}", add=h)
    return out, (n, upg, upv, act)


def _ffn_bwd(h, gain, wug, wuv, cw, cb, wd, saved, dout, tm, tmm, tag):
    n, upg, upv, act = saved
    dact = _mm(dout, wd, "nt", F32, tmm, 256, f"ffn_dact_{tag}")
    dwd = _mm(act, dout, "tn", F32, 256, 512, f"ffn_dwdown_{tag}")
    dupg, dupv, dcw, dcb = _ffn_act_bwd(upg, upv, dact, cw, cb, tm, f"ffn_act_bwd_{tag}")
    dn = _mm(dupg, wug, "nt", F32, tm, 512, f"ffn_dn_gate_{tag}")
    dn = _mm(dupv, wuv, "nt", F32, tm, 512, f"ffn_dn_val_{tag}", add=dn)
    dwug = _mm(n, dupg, "tn", F32, 512, 256, f"ffn_dwup_gate_{tag}")
    dwuv = _mm(n, dupv, "tn", F32, 512, 256, f"ffn_dwup_val_{tag}")
    dh, dgain = _rms_bwd(h, gain, dn, dout, tm, f"ffn_norm_bwd_{tag}")
    return dh, dict(gain=dgain, wug=dwug, wuv=dwuv, cw=dcw[:FFN_K], cb=dcb, wd=dwd)


def _local_step(h0, tgt, W, n_real):
    LP = h0.shape[0]
    tm, tmm = _row_tiles(LP)
    nb = LP // ATT_BLK
    G = {}

    n0 = _rms_fwd(h0, W["mix_norm_even"], BF16, tm, "mix_norm_even")
    proj = _mm(n0, W["w_in_p"], "nn", F32, tmm, 384, "in_proj")
    c = _fgate_fwd(proj, W["b_f_p"], "forget_gate")
    cT = c[:, :HEADS].T
    c_col = cT[:, :, None]
    c_row = cT.reshape(HEADS, nb, 1, ATT_BLK)
    qkv = proj[:, :3 * FOX_W].astype(BF16)
    q, k, v = (_heads(qkv[:, i * FOX_W:(i + 1) * FOX_W], LP) for i in range(3))
    o, lse = _attn_fwd(q, k, v, c_col, c_row, "fox_attention")
    u1, u = _conf_fwd(proj, W["conv_w_p"], W["conv_b"], W["ln_g"], W["ln_b"], tm, "conformer")
    cat = jnp.concatenate([_unheads(o, LP).astype(BF16), u], axis=1)
    h1 = _mm(cat, W["w_out"], "nn", F32, tmm, 512, "out_proj", add=h0)
    h2, ffn0 = _ffn_fwd(h1, W["ffn_norm"][0:1], W["w_up_g"][0], W["w_up_v"][0], W["ffn_conv_w_p"][0],
                        W["ffn_conv_b"][0:1], W["w_down"][0], tm, tmm, "0")
    h3, dpool = _pool_fwd(h2, W["mix_norm_odd"], W["pool_w"], W["pool_b"], W["pool_scale"], tm, "pool_mixer")
    h4, ffn1 = _ffn_fwd(h3, W["ffn_norm"][1:2], W["w_up_g"][1], W["w_up_v"][1], W["ffn_conv_w_p"][1],
                        W["ffn_conv_b"][1:2], W["w_down"][1], tm, tmm, "1")
    loss, dh4, G["final_norm"] = _loss_head(h4, W["final_norm"], tgt, n_real, tm, "loss_head")

    dh3, g1 = _ffn_bwd(h3, W["ffn_norm"][1:2], W["w_up_g"][1], W["w_up_v"][1], W["ffn_conv_w_p"][1],
                       W["ffn_conv_b"][1:2], W["w_down"][1], ffn1, dh4, tm, tmm, "1")
    dh2, G["pool_w"], G["pool_b"], G["pool_scale"], G["mix_norm_odd"] = _pool_bwd(
        h2, W["mix_norm_odd"], dpool, W["pool_w"], W["pool_b"], W["pool_scale"], dh3, tm, "pool_mixer_bwd")
    dh1, g0 = _ffn_bwd(h1, W["ffn_norm"][0:1], W["w_up_g"][0], W["w_up_v"][0], W["ffn_conv_w_p"][0],
                       W["ffn_conv_b"][0:1], W["w_down"][0], ffn0, dh2, tm, tmm, "0")
    for key in ("gain", "wug", "wuv", "cw", "cb", "wd"):
        G["ffn_" + key] = (g0[key], g1[key])

    dcat = _mm(dh1, W["w_out"], "nt", F32, tmm, 512, "out_proj_dx")
    G["w_out"] = _mm(cat, dh1, "tn", F32, 512, 512, "out_proj_dw")
    dadg, dcw, G["conv_b"], G["ln_g"], G["ln_b"] = _conf_bwd(
        proj, u1, dcat, W["conv_w_p"], W["ln_g"], W["ln_b"], tm, "conformer_bwd")
    G["conv_w"] = dcw[:CONV_K]
    do = _heads(dcat[:, :FOX_W], LP)
    dq, dk, dv, dcq, dck = _attn_bwd(q, k, v, o, do, lse, c_col, c_row, "fox_attention_bwd")
    dc = jnp.pad((dcq.reshape(HEADS, LP) + dck.reshape(HEADS, LP)).T, ((0, 0), (0, LANE - HEADS)))
    dfl, dbf = _fgate_bwd(proj, W["b_f_p"], dc, "forget_gate_bwd")
    G["b_f"] = dbf[:, :HEADS]
    dproj = jnp.concatenate([_unheads(t, LP).astype(BF16) for t in (dq, dk, dv)] + [dadg, dfl], axis=1)
    dn0 = _mm(dproj, W["w_in_p"], "nt", F32, tmm, 512, "in_proj_dx")
    G["w_in_p"] = _mm(n0, dproj, "tn", F32, 512, 384, "in_proj_dw")
    dh0, G["mix_norm_even"] = _rms_bwd(h0, W["mix_norm_even"], dn0, dh1, tm, "mix_norm_even_bwd")
    return loss, dh0, G


def _compute_layout(P):
    w_in = P["w_in"].reshape(D_MODEL, IN_COLS)
    qkv, f, ag = w_in[:, :3 * FOX_W], w_in[:, 3 * FOX_W:3 * FOX_W + HEADS], w_in[:, 3 * FOX_W + HEADS:]
    w_in_p = jnp.concatenate([qkv, ag, f, jnp.zeros((D_MODEL, LANE - HEADS), w_in.dtype)], axis=1).astype(BF16)
    w_up = P["w_up"].astype(BF16)
    return dict(
        mix_norm_even=P["mix_norm_even"].reshape(1, D_MODEL).astype(F32),
        w_in_p=w_in_p,
        b_f_p=jnp.pad(P["b_f"].reshape(1, HEADS).astype(F32), ((0, 0), (0, LANE - HEADS))),
        conv_w_p=jnp.pad(P["conv_w"].reshape(CONV_K, CONV_CH).astype(F32), ((0, CONV_HALO - CONV_K), (0, 0))),
        conv_b=P["conv_b"].reshape(1, CONV_CH).astype(F32),
        ln_g=P["ln_g"].reshape(1, CONV_CH).astype(F32),
        ln_b=P["ln_b"].reshape(1, CONV_CH).astype(F32),
        w_out=P["w_out"].reshape(D_MODEL, D_MODEL).astype(BF16),
        mix_norm_odd=P["mix_norm_odd"].reshape(1, D_MODEL).astype(F32),
        pool_w=P["pool_w"].reshape(len(POOL_WINDOWS), POOL_G, POOL_G).astype(BF16),
        pool_b=P["pool_b"].reshape(1, D_MODEL).astype(F32),
        pool_scale=P["pool_scale"].reshape(1, D_MODEL).astype(F32),
        ffn_norm=P["ffn_norm"].astype(F32),
        w_up_g=w_up[:, :, :D_FF],
        w_up_v=w_up[:, :, D_FF:],
        ffn_conv_w_p=jnp.pad(P["ffn_conv_w"].astype(F32), ((0, 0), (0, 8 - FFN_K), (0, 0))),
        ffn_conv_b=P["ffn_conv_b"].astype(F32),
        w_down=P["w_down"].astype(BF16),
        final_norm=P["final_norm"].reshape(1, D_MODEL).astype(F32),
    )


def _reference_layout(G, dh0):
    gp = G["w_in_p"]
    g_w_in = jnp.concatenate([gp[:, :3 * FOX_W], gp[:, 3 * FOX_W + 2 * CONV_CH:3 * FOX_W + 2 * CONV_CH + HEADS],
                              gp[:, 3 * FOX_W:3 * FOX_W + 2 * CONV_CH]], axis=1)
    return dict(
        meta_tokens=dh0[:N_META],
        mix_norm_even=G["mix_norm_even"],
        w_in=g_w_in[None],
        b_f=G["b_f"],
        conv_w=G["conv_w"][None],
        conv_b=G["conv_b"],
        ln_g=G["ln_g"],
        ln_b=G["ln_b"],
        w_out=G["w_out"][None],
        mix_norm_odd=G["mix_norm_odd"],
        pool_w=G["pool_w"][None],
        pool_b=G["pool_b"].reshape(1, len(POOL_WINDOWS), POOL_G),
        pool_scale=G["pool_scale"],
        ffn_norm=jnp.concatenate(G["ffn_gain"], axis=0),
        w_up=jnp.stack([jnp.concatenate([g, v], axis=1) for g, v in zip(G["ffn_wug"], G["ffn_wuv"])]),
        ffn_conv_w=jnp.stack(G["ffn_cw"]),
        ffn_conv_b=jnp.concatenate(G["ffn_cb"], axis=0),
        w_down=jnp.stack(G["ffn_wd"]),
        final_norm=G["final_norm"].reshape(D_MODEL),
    )


MESH = pl.DeviceIdType.MESH
ANY = pl.BlockSpec(memory_space=pl.ANY)
PACK_COLS = 1024


def _coords():
    return lax.axis_index("x"), lax.axis_index("y"), lax.axis_index("c")


def _other_chips(x, y):
    return [(1 - x, y), (x, 1 - y), (1 - x, 1 - y)]


def _allgather_chips(pack):
    R, C = pack.shape
    R2 = R // 2

    def body(x_ref, o_ref, send_sems, recv_sems, local_sem):
        x, y, c = _coords()
        sibling = (x, y, 1 - c)
        chips = _other_chips(x, y)

        def slot(px, py, half):
            return o_ref.at[2 * px + py, pl.ds(half * R2, R2), :]

        def copy(k, src, dst, to):
            return pltpu.make_async_remote_copy(src_ref=src, dst_ref=dst, send_sem=send_sems.at[k],
                                                recv_sem=recv_sems.at[k], device_id=to, device_id_type=MESH)

        mine = pltpu.make_async_copy(x_ref, o_ref.at[2 * x + y], local_sem)
        mine.start()
        my_half = x_ref.at[pl.ds(c * R2, R2), :]
        first = [copy(j, my_half, slot(x, y, c), (*chip, c)) for j, chip in enumerate(chips)]
        for cp in first:
            cp.start()
        passed = [copy(3 + j, slot(*chip, c), slot(*chip, c), sibling) for j, chip in enumerate(chips)]
        for j, chip in enumerate(chips):
            copy(j, my_half, slot(*chip, c), sibling).wait_recv()
            passed[j].start()
        for j, chip in enumerate(chips):
            copy(3 + j, my_half, slot(*chip, 1 - c), sibling).wait_recv()
        for cp in first + passed:
            cp.wait_send()
        mine.wait()

    return pl.pallas_call(
        body, name="allgather_weights", out_shape=jax.ShapeDtypeStruct((N_CHIPS, R, C), pack.dtype),
        in_specs=[ANY], out_specs=ANY,
        scratch_shapes=[pltpu.SemaphoreType.DMA((6,)), pltpu.SemaphoreType.DMA((6,)), pltpu.SemaphoreType.DMA],
    )(pack)


def _pair_exchange(G):
    n, R, C = G.shape
    R2 = R // 2

    def body(g_ref, o_ref, send_sem, recv_sem):
        x, y, c = _coords()
        src = g_ref.at[pl.ds(0, n), pl.ds((1 - c) * R2, R2), :]
        cp = pltpu.make_async_remote_copy(src_ref=src, dst_ref=o_ref, send_sem=send_sem, recv_sem=recv_sem,
                                          device_id=(x, y, 1 - c), device_id_type=MESH)
        cp.start()
        cp.wait()

    return pl.pallas_call(
        body, name="grad_pair_exchange", out_shape=jax.ShapeDtypeStruct((n, R2, C), G.dtype),
        in_specs=[ANY], out_specs=ANY,
        scratch_shapes=[pltpu.SemaphoreType.DMA, pltpu.SemaphoreType.DMA],
    )(G)


def _row_tile(rows, align, cap):
    best = None
    for t in range(align, min(rows, cap) + 1, align):
        if rows % t == 0:
            best = t
    assert best is not None, (rows, align, cap)
    return best


def _pair_sum(G, recv):
    n, R, C = G.shape
    R2 = R // 2
    tr = _row_tile(R2, 16, 704)
    nrb = R2 // tr
    half = lax.axis_index("c").astype(jnp.int32).reshape(1)

    def body(c_ref, g_ref, r_ref, o_ref):
        o_ref[...] = (g_ref[...] + r_ref[...]).astype(o_ref.dtype)

    return pl.pallas_call(
        body, name="grad_pair_sum", out_shape=jax.ShapeDtypeStruct((n, R2, C), BF16),
        grid_spec=pltpu.PrefetchScalarGridSpec(
            num_scalar_prefetch=1, grid=(n, nrb),
            in_specs=[pl.BlockSpec((None, tr, C), lambda j, i, c_ref: (j, c_ref[0] * nrb + i, 0)),
                      pl.BlockSpec((None, tr, C), lambda j, i, c_ref: (j, i, 0))],
            out_specs=pl.BlockSpec((None, tr, C), lambda j, i, c_ref: (j, i, 0))),
        compiler_params=_params(("parallel", "parallel")))(half, G, recv)


def _chip_exchange(P):
    n, R2, C = P.shape

    def body(p_ref, o_ref, send_sems, recv_sems, local_sem):
        x, y, c = _coords()
        me = 2 * x + y
        chips = _other_chips(x, y)
        mine = pltpu.make_async_copy(p_ref.at[me], o_ref.at[me], local_sem)
        mine.start()
        sends = [pltpu.make_async_remote_copy(
            src_ref=p_ref.at[2 * px + py], dst_ref=o_ref.at[me], send_sem=send_sems.at[k],
            recv_sem=recv_sems.at[k], device_id=(px, py, c), device_id_type=MESH)
            for k, (px, py) in enumerate(chips)]
        for cp in sends:
            cp.start()
        for k, (px, py) in enumerate(chips):
            pltpu.make_async_remote_copy(
                src_ref=p_ref.at[me], dst_ref=o_ref.at[2 * px + py], send_sem=send_sems.at[k],
                recv_sem=recv_sems.at[k], device_id=(px, py, c), device_id_type=MESH).wait_recv()
        for cp in sends:
            cp.wait_send()
        mine.wait()

    return pl.pallas_call(
        body, name="grad_chip_exchange", out_shape=jax.ShapeDtypeStruct((n, R2, C), P.dtype),
        in_specs=[ANY], out_specs=ANY,
        scratch_shapes=[pltpu.SemaphoreType.DMA((3,)), pltpu.SemaphoreType.DMA((3,)), pltpu.SemaphoreType.DMA],
    )(P)


def _chip_sum(X):
    n, R2, C = X.shape
    tr = _row_tile(R2, 16, 704)

    def body(x_ref, o_ref):
        acc = x_ref[0].astype(F32)
        for s in range(1, n):
            acc = acc + x_ref[s].astype(F32)
        o_ref[...] = acc

    return pl.pallas_call(
        body, name="grad_chip_sum", out_shape=jax.ShapeDtypeStruct((R2, C), F32), grid=(R2 // tr,),
        in_specs=[pl.BlockSpec((n, tr, C), lambda i: (0, i, 0))],
        out_specs=pl.BlockSpec((tr, C), lambda i: (i, 0)),
        compiler_params=_params(("parallel",)))(X)


def _pair_allgather(Q):
    R2, C = Q.shape

    def body(q_ref, o_ref, send_sem, recv_sem, local_sem):
        x, y, c = _coords()
        mine = pltpu.make_async_copy(q_ref, o_ref.at[c], local_sem)
        mine.start()
        cp = pltpu.make_async_remote_copy(src_ref=q_ref, dst_ref=o_ref.at[c], send_sem=send_sem,
                                          recv_sem=recv_sem, device_id=(x, y, 1 - c), device_id_type=MESH)
        cp.start()
        pltpu.make_async_remote_copy(src_ref=q_ref, dst_ref=o_ref.at[1 - c], send_sem=send_sem,
                                     recv_sem=recv_sem, device_id=(x, y, 1 - c), device_id_type=MESH).wait_recv()
        cp.wait_send()
        mine.wait()

    return pl.pallas_call(
        body, name="grad_pair_allgather", out_shape=jax.ShapeDtypeStruct((2, R2, C), Q.dtype),
        in_specs=[ANY], out_specs=ANY,
        scratch_shapes=[pltpu.SemaphoreType.DMA, pltpu.SemaphoreType.DMA, pltpu.SemaphoreType.DMA],
    )(Q)


def _allreduce_small(pack):
    Rs, C = pack.shape
    n_dev = 8

    def body(x_ref, o_ref, buf, send_sems, recv_sems):
        x, y, c = _coords()
        me = 4 * x + 2 * y + c
        buf[me] = x_ref[...]
        peers = []
        for rel in range(1, n_dev):
            px = 1 - x if rel & 4 else x
            py = 1 - y if rel & 2 else y
            pc = 1 - c if rel & 1 else c
            peers.append((px, py, pc))
        sends = [pltpu.make_async_remote_copy(
            src_ref=x_ref, dst_ref=buf.at[me], send_sem=send_sems.at[k], recv_sem=recv_sems.at[k],
            device_id=peer, device_id_type=MESH) for k, peer in enumerate(peers)]
        for cp in sends:
            cp.start()
        for k, (px, py, pc) in enumerate(peers):
            pltpu.make_async_remote_copy(
                src_ref=x_ref, dst_ref=buf.at[4 * px + 2 * py + pc], send_sem=send_sems.at[k],
                recv_sem=recv_sems.at[k], device_id=(px, py, pc), device_id_type=MESH).wait_recv()
        for cp in sends:
            cp.wait_send()
        acc = buf[0]
        for d in range(1, n_dev):
            acc = acc + buf[d]
        o_ref[...] = acc

    vm = pl.BlockSpec(memory_space=pltpu.VMEM)
    return pl.pallas_call(
        body, name="allreduce_replicated", out_shape=jax.ShapeDtypeStruct((Rs, C), F32),
        in_specs=[vm], out_specs=vm,
        scratch_shapes=[pltpu.VMEM((n_dev, Rs, C), F32), pltpu.SemaphoreType.DMA((n_dev - 1,)),
                        pltpu.SemaphoreType.DMA((n_dev - 1,))],
    )(pack)


SHARDED = (
    ("w_in", 2, True), ("w_out", 1, True), ("pool_w", 2, True), ("w_up", 2, True), ("w_down", 1, True),
    ("meta_tokens", 1, False), ("mix_norm_odd", 1, False), ("pool_b", 2, False), ("pool_scale", 1, False),
    ("conv_w", 2, False), ("ffn_conv_w", 2, False))
REPLICATED = ("mix_norm_even", "b_f", "conv_b", "ln_g", "ln_b", "ffn_norm", "ffn_conv_b", "final_norm")
PACK_ROW_ALIGN = 32


def _pad_rows(flat, align_rows, cols):
    rows = -(-flat.shape[-1] // cols)
    rows = -(-rows // align_rows) * align_rows
    pad = rows * cols - flat.shape[-1]
    flat = jnp.pad(flat, [(0, 0)] * (flat.ndim - 1) + [(0, pad)])
    return flat.reshape(flat.shape[:-1] + (rows, cols))


def _pack_weight_shards(shards):
    parts = []
    for name, _, as_bf16 in SHARDED:
        w = shards[name].astype(F32).reshape(-1)
        parts.append(w.astype(BF16) if as_bf16 else lax.bitcast_convert_type(w, BF16).reshape(-1))
    return _pad_rows(jnp.concatenate(parts), PACK_ROW_ALIGN, PACK_COLS)


def _unpack_weights(gathered, shards):
    flat = gathered.reshape(N_CHIPS, -1)
    out, off = {}, 0
    for name, axis, as_bf16 in SHARDED:
        shp = shards[name].shape
        n = int(np.prod(shp))
        if as_bf16:
            t = flat[:, off:off + n]
            off += n
        else:
            t = lax.bitcast_convert_type(flat[:, off:off + 2 * n].reshape(N_CHIPS, n, 2), F32)
            off += 2 * n
        t = t.reshape((N_CHIPS,) + shp)
        out[name] = jnp.concatenate([t[j] for j in range(N_CHIPS)], axis=axis)
    return out


def _pack_grad_shards(grads, shards):
    parts = []
    for name, axis, _ in SHARDED:
        g = grads[name].reshape(shards[name].shape[:axis] + (N_CHIPS, shards[name].shape[axis])
                                + shards[name].shape[axis + 1:])
        parts.append(jnp.moveaxis(g, axis, 0).reshape(N_CHIPS, -1))
    return _pad_rows(jnp.concatenate(parts, axis=1), PACK_ROW_ALIGN, PACK_COLS)


def _unpack_grad_shard(reduced, shards):
    flat = reduced.reshape(-1)
    out, off = {}, 0
    for name, _, _ in SHARDED:
        shp = shards[name].shape
        n = int(np.prod(shp))
        out[name] = flat[off:off + n].reshape(shp)
        off += n
    return out


def _pack_replicated(grads, loss):
    parts = [_pad_rows(grads[name].astype(F32).reshape(-1), 1, LANE).reshape(-1) for name in REPLICATED]
    parts.append(_pad_rows(loss.reshape(-1)[:1], 1, LANE).reshape(-1))
    return _pad_rows(jnp.concatenate(parts), 8, LANE)


def _unpack_replicated(reduced, shapes):
    flat = reduced.reshape(-1)
    out, off = {}, 0
    for name in REPLICATED:
        n = int(np.prod(shapes[name]))
        out[name] = flat[off:off + n].reshape(shapes[name])
        off += -(-n // LANE) * LANE
    return out, flat[off]


def _ffn_fwd2(h, W, layer, tm, tmm, host_up=None, host_act=None):
    tag = str(layer)
    n = _rms_fwd(h, W["ffn_norm"][layer:layer + 1], BF16, tm, f"ffn_norm_{tag}")
    up, *g_up = _mm(n, W["w_up"][layer], "nn", F32, tmm, UP_SHARD, f"ffn_up_{tag}", host=host_up) \
        if host_up else (_mm(n, W["w_up"][layer], "nn", F32, tmm, UP_SHARD, f"ffn_up_{tag}"),)
    act, *g_act = _ffn_act_fwd(up, W["ffn_conv_w_p"][layer], W["ffn_conv_b"][layer:layer + 1], tm,
                               f"ffn_act_{tag}", host=host_act) \
        if host_act else (_ffn_act_fwd(up, W["ffn_conv_w_p"][layer], W["ffn_conv_b"][layer:layer + 1], tm,
                                       f"ffn_act_{tag}"),)
    out = _mm(act, W["w_down"][layer], "nn", F32, tm, D_MODEL, f"ffn_down_{tag}", add=h)
    return out, (n, up, act), g_up + g_act


def _ffn_bwd2(h, W, layer, saved, dout, acc, tm, tmm):
    tag = str(layer)
    n, up, act = saved
    dact = _mm(dout, W["w_down"][layer], "nt", F32, tmm, UP_SHARD, f"ffn_dact_{tag}")
    dwd = _mm(act, dout, "tn", F32, D_FF // 2, 512, f"ffn_dwdown_{tag}",
              out=(layer, 2, None if acc is None else acc[1]))
    dup, dcw, dcb = _ffn_act_bwd(up, dact, W["ffn_conv_w_p"][layer], W["ffn_conv_b"][layer:layer + 1], tm,
                                 f"ffn_act_bwd_{tag}")
    dn = _mm_ffn_dn(dup, W["w_up"][layer], tm, D_MODEL, f"ffn_dn_{tag}")
    dwu = _mm_ffn_dwup(n, dup, layer, None if acc is None else acc[0], 512, D_FF // 2, f"ffn_dwup_{tag}")
    dh, dgain = _rms_bwd(h, W["ffn_norm"][layer:layer + 1], dn, dout, tm, f"ffn_norm_bwd_{tag}")
    return dh, (dwu, dwd), dict(gain=dgain, cw=dcw[:FFN_K], cb=dcb)


GATHER_FIRST = ("w_in", "small")
GATHER_LATE = ("pool_w", "w_up", "w_down")
HOSTED = ("w_out", "pool_w", "w_up", "w_down")
LATE = ("w_in", "small")


def _local_step2(h0, tgt, W, n_real, cut_of):
    LP = h0.shape[0]
    tm, tmm = _row_tiles(LP)
    nb = LP // ATT_BLK
    G = {}
    n0 = _rms_fwd(h0, W["mix_norm_even"], BF16, tm, "mix_norm_even")
    sh = W["late_shards"]
    stage = lambda *names: ([sh[n] for n in names], [cut_of[n] for n in names])
    proj, g_down0 = _mm(n0, W["w_in_p"], "nn", F32, tmm, 896, "in_proj", host=stage("w_down0"))
    c = _fgate_fwd(proj, W["b_f_p"], "forget_gate")
    qT, kT, k_aug, vT = _attn_prep(proj, c, "attention_operands")
    oT, lse, g_pool, g_up0, g_out = _attn_fwd2(qT, k_aug, vT, "fox_attention",
                                               comm=stage("pool_w", "w_up0", "w_out"))
    g_down0, g_pool, g_up0, g_out = _gather_forward(
        [g_down0, g_pool, g_up0, g_out], stage("w_down0", "pool_w", "w_up0", "w_out")[1], "gather_forward_0")
    W = dict(W)
    W.update(pool_w=g_pool, w_up=[g_up0, None], w_down=[g_down0, None], w_out=g_out)
    u1, u = _conf_fwd(proj, W["conv_w_p"], W["conv_b"], W["ln_g"], W["ln_b"], tm, "conformer")
    cat = jnp.concatenate([_attn_rows(oT, 1.0, BF16, "attention_rows"), u], axis=1)
    h1 = _mm(cat, W["w_out"], "nn", F32, tmm, D_MODEL, "out_proj", add=h0)
    h2, ffn0, (g_down1, g_up1) = _ffn_fwd2(h1, W, 0, tm, tmm, host_up=stage("w_down1"), host_act=stage("w_up1"))
    g_down1, g_up1 = _gather_forward([g_down1, g_up1], stage("w_down1", "w_up1")[1], "gather_forward_1")
    W.update(w_up=[g_up0, g_up1], w_down=[g_down0, g_down1])
    h3, dpool = _pool_fwd(h2, W["mix_norm_odd"], W["pool_w"], W["pool_b"], W["pool_scale"], tm, "pool_mixer")
    h4, ffn1, _ = _ffn_fwd2(h3, W, 1, tm, tmm)
    loss, dh4, G["final_norm"] = _loss_head(h4, W["final_norm"], tgt, n_real, tm, "loss_head")

    dh3, acc, g1 = _ffn_bwd2(h3, W, 1, ffn1, dh4, None, tm, tmm)
    dh2, G["pool_w"], G["pool_b"], G["pool_scale"], G["mix_norm_odd"] = _pool_bwd(
        h2, W["mix_norm_odd"], dpool, W["pool_w"], W["pool_b"], W["pool_scale"], dh3, tm, "pool_mixer_bwd")
    dh1, acc, g0 = _ffn_bwd2(h1, W, 0, ffn0, dh2, acc, tm, tmm)
    G["w_up"], G["w_down"] = acc
    G["ffn_norm"] = jnp.concatenate([g0["gain"], g1["gain"]], axis=0)
    G["ffn_conv_w"] = jnp.stack([g0["cw"], g1["cw"]])
    G["ffn_conv_b"] = jnp.concatenate([g0["cb"], g1["cb"]], axis=0)

    dcat = _mm(dh1, W["w_out"], "nt", F32, tmm, D_MODEL, "out_proj_dx")
    G["w_out"] = _mm(cat, dh1, "tn", F32, 512, D_MODEL, "out_proj_dw")
    dadg, dcw, G["conv_b"], G["ln_g"], G["ln_b"] = _conf_bwd(
        proj, u1, dcat, W["conv_w_p"], W["ln_g"], W["ln_b"], tm, "conformer_bwd")
    G["conv_w"] = dcw[:CONV_K]
    doT = _attn_cols(dcat, "attention_do_cols")
    hcuts = [cut_of[n] for n in HOSTED]
    hfull = [G[n] for n in HOSTED]
    hrecv = _pair_exchange2(hfull, hcuts, "grad_pair_exchange_early")
    hparts = [_pair_sum2(f, r, cut, PAIR_SUM_BLOCKS[n], "grad_pair_sum_" + n)
              for f, r, cut, n in zip(hfull, hrecv, hcuts, HOSTED)]
    dqT, dkT, dvT, *hothers = _attn_bwd2(qT, kT, k_aug, vT, oT, doT, lse, "fox_attention_bwd",
                                         comm=(hparts, hcuts))
    dfl, dbf = _fgate_bwd(proj, W["b_f_p"], _attn_dc(dqT, dkT, "attention_dc"), "forget_gate_bwd")
    G["b_f"] = dbf[:, :HEADS]
    dproj = jnp.concatenate([_attn_rows(dqT, HEAD_DIM ** -0.5, BF16, "attention_dq_rows"),
                             _attn_rows(dkT, 1.0, BF16, "attention_dk_rows"),
                             _attn_rows(dvT, 1.0, BF16, "attention_dv_rows"), dadg, dfl], axis=1)
    dn0 = _mm(dproj, W["w_in_p"], "nt", F32, tmm, D_MODEL, "in_proj_dx")
    G["w_in_p"] = _mm(n0, dproj, "tn", F32, 512, 896, "in_proj_dw")
    dh0, G["mix_norm_even"] = _rms_bwd(h0, W["mix_norm_even"], dn0, dh1, tm, "mix_norm_even_bwd")
    return loss, dh0, G, dict(zip(HOSTED, hparts)), dict(zip(HOSTED, hothers))


class _Cut:
    def __init__(self, full_shape, chip_dim, half_dim):
        self.full = tuple(full_shape)
        self.chip_dim, self.half_dim = chip_dim, half_dim
        self.chip_size = full_shape[chip_dim] // N_CHIPS
        self.half_size = full_shape[half_dim] // 2
        assert chip_dim != half_dim

    def shape(self, chip=False, half=False):
        s = list(self.full)
        if chip:
            s[self.chip_dim] = self.chip_size
        if half:
            s[self.half_dim] = self.half_size
        return tuple(s)

    def region(self, ref, chip=None, half=None):
        idx = [pl.ds(0, n) for n in ref.shape]
        if chip is not None:
            idx[self.chip_dim] = pl.ds(chip * self.chip_size, self.chip_size)
        if half is not None:
            idx[self.half_dim] = pl.ds(half * self.half_size, self.half_size)
        return ref.at[tuple(idx)]


SMALL_SHARDED = ("meta_tokens", "mix_norm_odd", "pool_b", "pool_scale", "conv_w", "ffn_conv_w")
SMALL_ROWS = 144


def _cuts():
    return {
        "w_in": _Cut((N_CHIPS, D_MODEL, IN_SHARD), 0, 1),
        "w_out": _Cut((D_MODEL, D_MODEL), 0, 1),
        "pool_w": _Cut((len(POOL_WINDOWS), POOL_G, POOL_G), 1, 0),
        "w_up": _Cut((2, D_MODEL, 2 * D_FF), 2, 1),
        "w_down": _Cut((2, D_FF, D_MODEL), 1, 2),
        "small": _Cut((N_CHIPS, SMALL_ROWS, LANE), 0, 1),
        "w_up0": _Cut((D_MODEL, 2 * D_FF), 1, 0), "w_up1": _Cut((D_MODEL, 2 * D_FF), 1, 0),
        "w_down0": _Cut((D_FF, D_MODEL), 0, 1), "w_down1": _Cut((D_FF, D_MODEL), 0, 1),
    }


COMM_ORDER = ("w_in", "w_out", "pool_w", "w_up", "w_down", "small")


def _remote(src, dst, send_sems, recv_sems, k, to):
    return pltpu.make_async_remote_copy(src_ref=src, dst_ref=dst, send_sem=send_sems.at[k],
                                        recv_sem=recv_sems.at[k], device_id=to, device_id_type=MESH)


def _gather_weights(shards, cuts):
    n = len(shards)

    def body(*refs):
        srcs, outs = refs[:n], refs[n:2 * n]
        send_sems, recv_sems = refs[2 * n:]
        x, y, c = _coords()
        me = 2 * x + y
        sibling = (x, y, 1 - c)
        chips = _other_chips(x, y)
        sends = []
        for t, cut in enumerate(cuts):
            push = _remote(srcs[t], cut.region(outs[t], chip=me), send_sems, recv_sems, 7 * t, sibling)
            push.start()
            sends.append(push)
            for kk, chip in enumerate(chips):
                cp = _remote(cut.region(srcs[t], half=c), cut.region(outs[t], chip=me, half=c),
                             send_sems, recv_sems, 7 * t + 1 + kk, (*chip, c))
                cp.start()
                sends.append(cp)
        for t, cut in enumerate(cuts):
            for kk, (px, py) in enumerate(chips):
                landed = cut.region(outs[t], chip=2 * px + py, half=c)
                _remote(landed, landed, send_sems, recv_sems, 7 * t + 1 + kk, sibling).wait_recv()
                fwd = _remote(landed, landed, send_sems, recv_sems, 7 * t + 4 + kk, sibling)
                fwd.start()
                sends.append(fwd)
        for t, cut in enumerate(cuts):
            mine = cut.region(outs[t], chip=me)
            _remote(mine, mine, send_sems, recv_sems, 7 * t, sibling).wait_recv()
            for kk, (px, py) in enumerate(chips):
                other = cut.region(outs[t], chip=2 * px + py, half=1 - c)
                _remote(other, other, send_sems, recv_sems, 7 * t + 4 + kk, sibling).wait_recv()
        for cp in sends:
            cp.wait_send()

    return pl.pallas_call(
        body, name="gather_weights",
        out_shape=tuple(jax.ShapeDtypeStruct(cut.full, s.dtype) for cut, s in zip(cuts, shards)),
        in_specs=[ANY] * n, out_specs=tuple([ANY] * n),
        scratch_shapes=[pltpu.SemaphoreType.DMA((7 * n,)), pltpu.SemaphoreType.DMA((7 * n,))],
    )(*shards)


def _gather_first_ops(srcs, outs, send_sems, recv_sems, cuts):
    x, y, c = _coords()
    me = 2 * x + y
    sibling = (x, y, 1 - c)
    chips = _other_chips(x, y)

    def copies():
        out = []
        for t, cut in enumerate(cuts):
            out.append(_remote(srcs[t], cut.region(outs[t], chip=me), send_sems, recv_sems, 4 * t, sibling))
            for kk, chip in enumerate(chips):
                out.append(_remote(cut.region(srcs[t], half=c), cut.region(outs[t], chip=me, half=c),
                                   send_sems, recv_sems, 4 * t + 1 + kk, (*chip, c)))
        return out

    def start():
        for cp in copies():
            cp.start()

    def wait():
        for t, cut in enumerate(cuts):
            mine = cut.region(outs[t], chip=me)
            _remote(mine, mine, send_sems, recv_sems, 4 * t, sibling).wait_recv()
            for kk, (px, py) in enumerate(chips):
                landed = cut.region(outs[t], chip=2 * px + py, half=c)
                _remote(landed, landed, send_sems, recv_sems, 4 * t + 1 + kk, sibling).wait_recv()
        for cp in copies():
            cp.wait_send()

    return start, wait


def _gather_forward(fulls, cuts, name):
    n = len(fulls)

    def body(*refs):
        outs = refs[n:2 * n]
        send_sems, recv_sems = refs[2 * n:]
        x, y, c = _coords()
        sibling = (x, y, 1 - c)
        chips = _other_chips(x, y)
        sends = []
        for t, cut in enumerate(cuts):
            for kk, (px, py) in enumerate(chips):
                landed = cut.region(outs[t], chip=2 * px + py, half=c)
                cp = _remote(landed, landed, send_sems, recv_sems, 3 * t + kk, sibling)
                cp.start()
                sends.append(cp)
        for t, cut in enumerate(cuts):
            for kk, (px, py) in enumerate(chips):
                other = cut.region(outs[t], chip=2 * px + py, half=1 - c)
                _remote(other, other, send_sems, recv_sems, 3 * t + kk, sibling).wait_recv()
        for cp in sends:
            cp.wait_send()

    return pl.pallas_call(
        body, name=name,
        out_shape=tuple(jax.ShapeDtypeStruct(f.shape, f.dtype) for f in fulls),
        in_specs=[ANY] * n, out_specs=tuple([ANY] * n), input_output_aliases={t: t for t in range(n)},
        scratch_shapes=[pltpu.SemaphoreType.DMA((3 * n,)), pltpu.SemaphoreType.DMA((3 * n,))],
    )(*fulls)


def _pair_exchange2(fulls, cuts, name):
    n = len(fulls)

    def body(*refs):
        srcs, outs = refs[:n], refs[n:2 * n]
        send_sems, recv_sems = refs[2 * n:]
        x, y, c = _coords()
        cps = [_remote(cut.region(srcs[t], half=1 - c), outs[t], send_sems, recv_sems, t, (x, y, 1 - c))
               for t, cut in enumerate(cuts)]
        for cp in cps:
            cp.start()
        for cp in cps:
            cp.wait()

    return pl.pallas_call(
        body, name=name,
        out_shape=tuple(jax.ShapeDtypeStruct(cut.shape(half=True), f.dtype) for cut, f in zip(cuts, fulls)),
        in_specs=[ANY] * n, out_specs=tuple([ANY] * n),
        scratch_shapes=[pltpu.SemaphoreType.DMA((n,)), pltpu.SemaphoreType.DMA((n,))],
    )(*fulls)


def _grid_of(shape, blk):
    assert all(s % b == 0 for s, b in zip(shape, blk)), (shape, blk)
    return tuple(s // b for s, b in zip(shape, blk))


def _pair_sum2(full, recv, cut, blk, name):
    hshape = cut.shape(half=True)
    grid = _grid_of(hshape, blk)
    hb = cut.half_size // blk[cut.half_dim]
    hd = cut.half_dim
    pos = jnp.stack([lax.axis_index("c")]).astype(jnp.int32)

    def full_idx(*a):
        ids, p = list(a[:-1]), a[-1]
        ids[hd] = ids[hd] + p[0] * hb
        return tuple(ids)

    def body(p_ref, f_ref, r_ref, o_ref):
        o_ref[...] = (f_ref[...] + r_ref[...]).astype(o_ref.dtype)

    return pl.pallas_call(
        body, name=name, out_shape=jax.ShapeDtypeStruct(hshape, BF16),
        grid_spec=pltpu.PrefetchScalarGridSpec(
            num_scalar_prefetch=1, grid=grid,
            in_specs=[pl.BlockSpec(blk, full_idx), pl.BlockSpec(blk, lambda *a: tuple(a[:-1]))],
            out_specs=pl.BlockSpec(blk, lambda *a: tuple(a[:-1]))),
        compiler_params=_params(("parallel",) * len(grid)))(pos, full, recv)


def _chip_exchange_ops(srcs, outs, send_sems, recv_sems, cuts):
    x, y, c = _coords()
    me = 2 * x + y
    chips = _other_chips(x, y)

    def copies():
        return [_remote(cut.region(srcs[t], chip=2 * px + py), outs[t].at[me], send_sems, recv_sems,
                        3 * t + kk, (px, py, c))
                for t, cut in enumerate(cuts) for kk, (px, py) in enumerate(chips)]

    def start():
        for cp in copies():
            cp.start()

    def wait():
        for t, cut in enumerate(cuts):
            for kk, (px, py) in enumerate(chips):
                slot = outs[t].at[2 * px + py]
                _remote(slot, slot, send_sems, recv_sems, 3 * t + kk, (px, py, c)).wait_recv()
        for cp in copies():
            cp.wait_send()

    return start, wait


def _chip_exchange_shapes(parts, cuts):
    return tuple(jax.ShapeDtypeStruct((N_CHIPS,) + cut.shape(chip=True, half=True), p.dtype)
                 for cut, p in zip(cuts, parts))


def _chip_exchange2(parts, cuts):
    n = len(parts)

    def body(*refs):
        start, wait = _chip_exchange_ops(refs[:n], refs[n:2 * n], refs[2 * n], refs[2 * n + 1], cuts)
        start()
        wait()

    return pl.pallas_call(
        body, name="grad_chip_exchange",
        out_shape=tuple(jax.ShapeDtypeStruct((N_CHIPS,) + cut.shape(chip=True, half=True), p.dtype)
                        for cut, p in zip(cuts, parts)),
        in_specs=[ANY] * n, out_specs=tuple([ANY] * n),
        scratch_shapes=[pltpu.SemaphoreType.DMA((3 * n,)), pltpu.SemaphoreType.DMA((3 * n,))],
    )(*parts)


def _chip_sum2(part, recv, cut, blk, name):
    bshape = cut.shape(chip=True, half=True)
    grid = _grid_of(bshape, blk)
    cb = cut.chip_size // blk[cut.chip_dim]
    hb = cut.half_size // blk[cut.half_dim]
    cd, hd = cut.chip_dim, cut.half_dim
    x, y, c = _coords()
    slots = [2 * px + py for px, py in _other_chips(x, y)]
    pos = jnp.stack([c, 2 * x + y] + slots).astype(jnp.int32)

    def part_idx(*a):
        ids, p = list(a[:-1]), a[-1]
        ids[cd] = ids[cd] + p[1] * cb
        return tuple(ids)

    def recv_idx(kk):
        return lambda *a: (a[-1][2 + kk],) + tuple(a[:-1])

    def out_idx(*a):
        ids, p = list(a[:-1]), a[-1]
        ids[hd] = ids[hd] + p[0] * hb
        return tuple(ids)

    def body(p_ref, own_ref, r0_ref, r1_ref, r2_ref, o_ref):
        acc = own_ref[...].astype(F32)
        for r_ref in (r0_ref, r1_ref, r2_ref):
            acc = acc + r_ref[...].astype(F32)
        o_ref[...] = acc

    return pl.pallas_call(
        body, name=name, out_shape=jax.ShapeDtypeStruct(cut.shape(chip=True), F32),
        grid_spec=pltpu.PrefetchScalarGridSpec(
            num_scalar_prefetch=1, grid=grid,
            in_specs=[pl.BlockSpec(blk, part_idx)] + [pl.BlockSpec((None,) + blk, recv_idx(kk)) for kk in range(3)],
            out_specs=pl.BlockSpec(blk, out_idx)),
        compiler_params=_params(("parallel",) * len(grid)))(pos, part, recv, recv, recv)


def _pair_swap2(blocks, cuts):
    n = len(blocks)

    def body(*refs):
        outs = refs[n:2 * n]
        send_sems, recv_sems = refs[2 * n:]
        x, y, c = _coords()
        cps = []
        for t, cut in enumerate(cuts):
            mine = cut.region(outs[t], half=c)
            cp = _remote(mine, mine, send_sems, recv_sems, t, (x, y, 1 - c))
            cp.start()
            cps.append(cp)
        for t, cut in enumerate(cuts):
            theirs = cut.region(outs[t], half=1 - c)
            _remote(theirs, theirs, send_sems, recv_sems, t, (x, y, 1 - c)).wait_recv()
        for cp in cps:
            cp.wait_send()

    return pl.pallas_call(
        body, name="grad_pair_swap",
        out_shape=tuple(jax.ShapeDtypeStruct(b.shape, b.dtype) for b in blocks),
        in_specs=[ANY] * n, out_specs=tuple([ANY] * n),
        input_output_aliases={t: t for t in range(n)},
        scratch_shapes=[pltpu.SemaphoreType.DMA((n,)), pltpu.SemaphoreType.DMA((n,))],
    )(*blocks)


PAIR_SUM_BLOCKS = {"w_in": (1, 512, IN_SHARD), "w_out": (512, 512), "pool_w": (1, POOL_G, POOL_G),
                   "w_up": (1, 64, 2 * D_FF), "w_down": (1, 704, 512), "small": (N_CHIPS, SMALL_ROWS // 2, LANE)}
CHIP_SUM_BLOCKS = {"w_in": (1, 512, IN_SHARD), "w_out": (256, 512), "pool_w": (2, 64, POOL_G),
                   "w_up": (1, 128, UP_SHARD), "w_down": (1, DOWN_SHARD, 512), "small": (1, SMALL_ROWS // 2, LANE)}


def _pack_small(P):
    parts = []
    for name in SMALL_SHARDED:
        t = P[name]
        parts.append(t.astype(F32))
    return parts


def _small_rows(t, lead):
    flat = t.reshape(lead + (-1,))
    pad = -flat.shape[-1] % LANE
    return jnp.pad(flat, [(0, 0)] * len(lead) + [(0, pad)]).reshape(lead + (-1, LANE))


def _pack_small_shards(shards):
    rows = jnp.concatenate([_small_rows(shards[n].astype(F32), ()) for n in SMALL_SHARDED], axis=0)
    return jnp.pad(rows, ((0, SMALL_ROWS - rows.shape[0]), (0, 0)))[None]


def _unpack_small(pack, shards, axes):
    out, off = {}, 0
    nchip = pack.shape[0]
    for name in SMALL_SHARDED:
        shp = shards[name].shape
        cnt = int(np.prod(shp))
        rows = -(-cnt // LANE)
        t = pack[:, off:off + rows].reshape(nchip, -1)[:, :cnt].reshape((nchip,) + shp)
        out[name] = jnp.concatenate([t[j] for j in range(nchip)], axis=axes[name])
        off += rows
    return out


def _pack_small_grads(grads, shards, axes):
    parts = []
    for name in SMALL_SHARDED:
        shp, ax = shards[name].shape, axes[name]
        g = grads[name].reshape(shp[:ax] + (N_CHIPS, shp[ax]) + shp[ax + 1:])
        parts.append(_small_rows(jnp.moveaxis(g, ax, 0), (N_CHIPS,)))
    rows = jnp.concatenate(parts, axis=1)
    return jnp.pad(rows, ((0, 0), (0, SMALL_ROWS - rows.shape[1]), (0, 0)))


SMALL_AXES = {"meta_tokens": 1, "mix_norm_odd": 1, "pool_b": 2, "pool_scale": 1, "conv_w": 2, "ffn_conv_w": 2}


WEIGHT_NAMES = ("meta_tokens", "mix_norm_even", "w_in", "b_f", "conv_w", "conv_b", "ln_g", "ln_b", "w_out",
                "mix_norm_odd", "pool_w", "pool_b", "pool_scale", "ffn_norm", "w_up", "ffn_conv_w",
                "ffn_conv_b", "w_down", "final_norm")


def kernel(x, meta_tokens, mix_norm_even, w_in, b_f, conv_w, conv_b, ln_g, ln_b, w_out, mix_norm_odd, pool_w, pool_b, pool_scale, ffn_norm, w_up, ffn_conv_w, ffn_conv_b, w_down, final_norm, loss_target, m_meta_tokens, m_mix_norm_even, m_w_in, m_b_f, m_conv_w, m_conv_b, m_ln_g, m_ln_b, m_w_out, m_mix_norm_odd, m_pool_w, m_pool_b, m_pool_scale, m_ffn_norm, m_w_up, m_ffn_conv_w, m_ffn_conv_b, m_w_down, m_final_norm, v_meta_tokens, v_mix_norm_even, v_w_in, v_b_f, v_conv_w, v_conv_b, v_ln_g, v_ln_b, v_w_out, v_mix_norm_odd, v_pool_w, v_pool_b, v_pool_scale, v_ffn_norm, v_w_up, v_ffn_conv_w, v_ffn_conv_b, v_w_down, v_final_norm):
    given = dict(locals())
    w_loc = {n: given[n] for n in WEIGHT_NAMES}
    m_loc = {n: given["m_" + n] for n in WEIGHT_NAMES}
    v_loc = {n: given["v_" + n] for n in WEIGHT_NAMES}
    cut_of = _cuts()
    cuts = [cut_of[n] for n in COMM_ORDER]
    big = ("w_in", "w_out", "pool_w", "w_up", "w_down")
    small_shards = {n: w_loc[n] for n in SMALL_SHARDED}

    shard_of = {n: w_loc[n].astype(BF16).reshape(cut_of[n].shape(chip=True)) for n in big}
    shard_of["small"] = _pack_small_shards(small_shards)
    g_in, g_small = _gather_weights([shard_of[n] for n in GATHER_FIRST], [cut_of[n] for n in GATHER_FIRST])
    g_out = None
    g_pool = g_up = g_down = None
    full = _unpack_small(g_small, small_shards, SMALL_AXES)
    full.update({n: w_loc[n] for n in REPLICATED})
    w_in_full = g_in.transpose(1, 0, 2).reshape(D_MODEL, IN_COLS)
    qkv, f, ag = (w_in_full[:, :3 * FOX_W], w_in_full[:, 3 * FOX_W:3 * FOX_W + HEADS],
                  w_in_full[:, 3 * FOX_W + HEADS:])
    W = dict(
        mix_norm_even=full["mix_norm_even"].reshape(1, D_MODEL),
        w_in_p=jnp.concatenate([qkv, ag, f, jnp.zeros((D_MODEL, LANE - HEADS), BF16)], axis=1),
        b_f_p=jnp.pad(full["b_f"].reshape(1, HEADS), ((0, 0), (0, LANE - HEADS))),
        conv_w_p=jnp.pad(full["conv_w"].reshape(CONV_K, CONV_CH), ((0, CONV_HALO - CONV_K), (0, 0))),
        conv_b=full["conv_b"].reshape(1, CONV_CH), ln_g=full["ln_g"].reshape(1, CONV_CH),
        ln_b=full["ln_b"].reshape(1, CONV_CH), w_out=g_out,
        mix_norm_odd=full["mix_norm_odd"].reshape(1, D_MODEL), pool_w=g_pool,
        pool_b=full["pool_b"].reshape(1, D_MODEL), pool_scale=full["pool_scale"].reshape(1, D_MODEL),
        ffn_norm=full["ffn_norm"], w_up=g_up,
        ffn_conv_w_p=jnp.pad(full["ffn_conv_w"], ((0, 0), (0, 8 - FFN_K), (0, 0))),
        ffn_conv_b=full["ffn_conv_b"], w_down=g_down, final_norm=full["final_norm"].reshape(1, D_MODEL),
        late_shards=dict(pool_w=shard_of["pool_w"], w_out=shard_of["w_out"],
                         w_up0=shard_of["w_up"][0], w_up1=shard_of["w_up"][1],
                         w_down0=shard_of["w_down"][0], w_down1=shard_of["w_down"][1]))

    seq = x.shape[1]
    n_real = N_META + seq
    LP = -(-n_real // ATT_BLK) * ATT_BLK
    tail = jnp.zeros((LP - n_real, D_MODEL), F32)
    h0 = jnp.concatenate([full["meta_tokens"], x[0], tail], axis=0)
    tgt = jnp.concatenate([jnp.zeros((N_META, D_MODEL), F32), loss_target[0], tail], axis=0)
    loss_loc, dh0, G, parts, others = _local_step2(h0, tgt, W, n_real, cut_of)
    grad_x = dh0[N_META:n_real][None]
    G["meta_tokens"] = dh0[:N_META]

    rep_shapes = {n: w_loc[n].shape for n in REPLICATED}
    G["final_norm"] = G["final_norm"].reshape(D_MODEL)
    rep, loss = _unpack_replicated(_allreduce_small(_pack_replicated(G, loss_loc)), rep_shapes)

    gp = G["w_in_p"]
    g_w_in = jnp.concatenate([gp[:, :3 * FOX_W], gp[:, 3 * FOX_W + 2 * CONV_CH:3 * FOX_W + 2 * CONV_CH + HEADS],
                              gp[:, 3 * FOX_W:3 * FOX_W + 2 * CONV_CH]], axis=1)
    lcuts = [cut_of[n] for n in LATE]
    lfull = [g_w_in.reshape(D_MODEL, N_CHIPS, IN_SHARD).transpose(1, 0, 2),
             _pack_small_grads(G, small_shards, SMALL_AXES)]
    lrecv = _pair_exchange2(lfull, lcuts, "grad_pair_exchange_late")
    lparts = [_pair_sum2(f, r, cut, PAIR_SUM_BLOCKS[n], "grad_pair_sum_" + n)
              for f, r, cut, n in zip(lfull, lrecv, lcuts, LATE)]
    parts.update(zip(LATE, lparts))
    others.update(zip(LATE, _chip_exchange2(lparts, lcuts)))
    blocks = [_chip_sum2(parts[n], others[n], cut_of[n], CHIP_SUM_BLOCKS[n], "grad_chip_sum_" + n)
              for n in COMM_ORDER]
    blocks = _pair_swap2(blocks, cuts)
    gsh = {n: b.reshape(w_loc[n].shape) for n, b in zip(big, blocks[:5])}
    gsh.update(_unpack_small(blocks[5], small_shards, SMALL_AXES))
    sharded = set(big) | set(SMALL_SHARDED)

    grad_w = {n: (gsh[n] if n in sharded else rep[n]) for n in WEIGHT_NAMES}
    delta, new_m, new_v = {}, {}, {}
    for n in WEIGHT_NAMES:
        delta[n], new_m[n], new_v[n] = _adamw(w_loc[n], grad_w[n], m_loc[n], v_loc[n], "adamw_" + n)
    return (loss, grad_x, *[grad_w[n] for n in WEIGHT_NAMES], *[delta[n] for n in WEIGHT_NAMES],
            *[new_m[n] for n in WEIGHT_NAMES], *[new_v[n] for n in WEIGHT_NAMES])
```

```python
import functools

import numpy as np
import jax
import jax.numpy as jnp
from jax import lax
from jax.experimental import pallas as pl
from jax.experimental.pallas import tpu as pltpu

F32 = jnp.float32
BF16 = jnp.bfloat16

D_MODEL = 1024
N_META = 16
SEQ = 2048
HEADS = 8
HEAD_DIM = 64
FOX_W = HEADS * HEAD_DIM
CONV_CH = 512
CONV_K = 31
D_FF = 2816
POOL_WINDOWS = (2, 4, 8, 16)
POOL_G = 256
RMS_EPS = 1e-6
LN_EPS = 1e-5
IN_COLS = 3 * FOX_W + HEADS + 2 * CONV_CH
IN_COLS_P = 3 * FOX_W + 2 * CONV_CH + 128
F_COL_BLK = (3 * FOX_W + 2 * CONV_CH) // 128
N_CHIPS = 4
IN_SHARD = IN_COLS // N_CHIPS
UP_SHARD = 2 * D_FF // N_CHIPS
DOWN_SHARD = D_FF // N_CHIPS

ADAM_LR = 0.001
ADAM_B1 = 0.9
ADAM_B2 = 0.999
ADAM_EPS = 1e-08
ADAM_WD = 0.01
ADAM_STEP = 10

LANE = 128
ATT_BLK = 128
VMEM_LIMIT = 56 * 1024 * 1024

NEG = -1e30


def _sigmoid(x):
    return 0.5 * jnp.tanh(0.5 * x) + 0.5


def _sigmoid_tail(x):
    return 1.0 / (1.0 + jnp.exp(-x))


def _params(sem=None):
    return pltpu.CompilerParams(dimension_semantics=sem, vmem_limit_bytes=VMEM_LIMIT)


def _sub_rows(tm):
    best = 8
    for s in range(8, 137, 8):
        if tm % s == 0:
            best = s
    return best


def _mm(a, b, mode, out_dtype, tm, tn, name, add=None, a_lead=None, b_lead=None, out=None, host=None):
    a_shape = a.shape if a_lead is None else a.shape[1:]
    b_shape = b.shape if b_lead is None else b.shape[1:]
    if mode == "nn":
        (M, K), (K2, N) = a_shape, b_shape
        dims = (((1,), (0,)), ((), ()))
        a_blk, a_idx = (tm, K), (lambda i, j: (i, 0))
        b_blk, b_idx = (K, tn), (lambda i, j: (0, j))
    elif mode == "nt":
        (M, K), (N, K2) = a_shape, b_shape
        dims = (((1,), (1,)), ((), ()))
        a_blk, a_idx = (tm, K), (lambda i, j: (i, 0))
        b_blk, b_idx = (tn, K), (lambda i, j: (j, 0))
    else:
        (K, M), (K2, N) = a_shape, b_shape
        dims = (((0,), (0,)), ((), ()))
        a_blk, a_idx = (K, tm), (lambda i, j: (0, i))
        b_blk, b_idx = (K, tn), (lambda i, j: (0, j))
    assert K == K2 and M % tm == 0 and N % tn == 0, (name, a.shape, b.shape, tm, tn)
    gm, gn = M // tm, N // tn
    a_bytes = M * K * a.dtype.itemsize
    b_bytes = N * K * b.dtype.itemsize
    m_outer = a_bytes + b_bytes * gm <= b_bytes + a_bytes * gn
    if m_outer:
        grid = (gm, gn)
        wrap = lambda f: f
    else:
        grid = (gn, gm)
        wrap = lambda f: (lambda j, i: f(i, j))

    def lead(blk, idx, at):
        if at is None:
            return pl.BlockSpec(blk, wrap(idx))
        return pl.BlockSpec((None,) + blk, wrap(lambda i, j: (at,) + idx(i, j)))

    o_idx = lambda i, j: (i, j)
    in_specs = [lead(a_blk, a_idx, a_lead), lead(b_blk, b_idx, b_lead)]
    args = [a, b]
    if add is not None:
        in_specs.append(pl.BlockSpec((tm, tn), wrap(o_idx)))
        args.append(add)
    aliases = {}
    if out is None:
        out_shape = jax.ShapeDtypeStruct((M, N), out_dtype)
        out_spec = pl.BlockSpec((tm, tn), wrap(o_idx))
    else:
        o_lead, n_lead, into = out
        out_shape = jax.ShapeDtypeStruct((n_lead, M, N), out_dtype)
        out_spec = lead((tm, tn), o_idx, o_lead)
        if into is not None:
            aliases = {len(args): 0}
            in_specs.append(pl.BlockSpec(memory_space=pl.ANY))
            args.append(into)
    has_add = add is not None
    n_host = 0 if host is None else len(host[0])
    n_in = len(args)
    scratch = []
    if n_host:
        in_specs = in_specs + [pl.BlockSpec(memory_space=pl.ANY)] * n_host
        args = args + list(host[0])
        out_shape = (out_shape,) + tuple(jax.ShapeDtypeStruct(cut.full, s.dtype) for cut, s in zip(host[1], host[0]))
        out_spec = (out_spec,) + (pl.BlockSpec(memory_space=pl.ANY),) * n_host
        scratch = [pltpu.SemaphoreType.DMA((4 * n_host,)), pltpu.SemaphoreType.DMA((4 * n_host,))]

    def body(*refs):
        a_ref, b_ref = refs[0], refs[1]
        o_ref = refs[n_in + n_host]
        if n_host:
            start, wait = _gather_first_ops(refs[n_in:n_in + n_host], refs[n_in + n_host + 1:n_in + 2 * n_host + 1],
                                            refs[n_in + 2 * n_host + 1], refs[n_in + 2 * n_host + 2], host[1])
            pl.when(jnp.logical_and(pl.program_id(0) == 0, pl.program_id(1) == 0))(start)
        x = a_ref[...].astype(BF16)
        y = b_ref[...].astype(BF16)
        acc = lax.dot_general(x, y, dims, preferred_element_type=F32)
        if has_add:
            acc = acc + refs[2][...]
        o_ref[...] = acc.astype(o_ref.dtype)
        if n_host:
            pl.when(jnp.logical_and(pl.program_id(0) == grid[0] - 1, pl.program_id(1) == grid[1] - 1))(wait)

    sem = ("arbitrary", "arbitrary") if n_host else ("parallel", "parallel")
    return pl.pallas_call(
        body, name=name, out_shape=out_shape, grid=grid, in_specs=in_specs, out_specs=out_spec,
        scratch_shapes=scratch, input_output_aliases=aliases, compiler_params=_params(sem))(*args)


def _mm_ffn_dn(dup, w_up, tm, tn, name):
    _, LP, F = dup.shape
    Dm = w_up.shape[0]
    nt = (((1,), (1,)), ((), ()))

    def body(a_ref, b_ref, o_ref):
        acc = lax.dot_general(a_ref[0], b_ref[:, 0:F], nt, preferred_element_type=F32)
        acc = acc + lax.dot_general(a_ref[1], b_ref[:, F:2 * F], nt, preferred_element_type=F32)
        o_ref[...] = acc

    return pl.pallas_call(
        body, name=name, out_shape=jax.ShapeDtypeStruct((LP, Dm), F32), grid=(LP // tm, Dm // tn),
        in_specs=[pl.BlockSpec((2, tm, F), lambda i, j: (0, i, 0)),
                  pl.BlockSpec((tn, 2 * F), lambda i, j: (j, 0))],
        out_specs=pl.BlockSpec((tm, tn), lambda i, j: (i, j)),
        compiler_params=_params(("parallel", "parallel")))(dup, w_up)


def _mm_ffn_dwup(n, dup, tk, tn, name):
    LP, Dm = n.shape
    F = dup.shape[2]
    nct = F // tn
    tdims = (((0,), (0,)), ((), ()))

    def body(a_ref, b_ref, o_ref):
        o_ref[...] = lax.dot_general(a_ref[...], b_ref[...], tdims, preferred_element_type=F32)

    return pl.pallas_call(
        body, name=name, out_shape=jax.ShapeDtypeStruct((Dm, 2 * F), F32), grid=(Dm // tk, 2 * nct),
        in_specs=[pl.BlockSpec((LP, tk), lambda i, j: (0, i)),
                  pl.BlockSpec((None, LP, tn), lambda i, j: (j // nct, 0, j % nct))],
        out_specs=pl.BlockSpec((tk, tn), lambda i, j: (i, j)),
        compiler_params=_params(("parallel", "parallel")))(n, dup)


def _rms_fwd(h, g, out_dtype, tm, name):
    LP, Dm = h.shape

    def body(h_ref, g_ref, o_ref):
        x = h_ref[...]
        r = lax.rsqrt(jnp.mean(x * x, axis=1, keepdims=True) + RMS_EPS)
        o_ref[...] = (x * r * g_ref[...]).astype(o_ref.dtype)

    return pl.pallas_call(
        body, name=name, out_shape=jax.ShapeDtypeStruct((LP, Dm), out_dtype), grid=(LP // tm,),
        in_specs=[pl.BlockSpec((tm, Dm), lambda i: (i, 0)), pl.BlockSpec((1, Dm), lambda i: (0, 0))],
        out_specs=pl.BlockSpec((tm, Dm), lambda i: (i, 0)),
        compiler_params=_params(("parallel",)))(h, g)


def _rms_bwd(h, g, dn, dres, tm, name):
    LP, Dm = h.shape

    def body(h_ref, g_ref, dn_ref, dr_ref, dh_ref, dg_ref):
        i = pl.program_id(0)
        x = h_ref[...]
        r = lax.rsqrt(jnp.mean(x * x, axis=1, keepdims=True) + RMS_EPS)
        xhat = x * r
        dy = dn_ref[...]
        dxh = dy * g_ref[...]
        dh = r * (dxh - xhat * jnp.mean(dxh * xhat, axis=1, keepdims=True))
        dh_ref[...] = dr_ref[...] + dh

        @pl.when(i == 0)
        def _():
            dg_ref[...] = jnp.zeros_like(dg_ref)

        dg_ref[...] += jnp.sum(dy * xhat, axis=0, keepdims=True)

    row = pl.BlockSpec((tm, Dm), lambda i: (i, 0))
    vec = pl.BlockSpec((1, Dm), lambda i: (0, 0))
    return pl.pallas_call(
        body, name=name,
        out_shape=(jax.ShapeDtypeStruct((LP, Dm), F32), jax.ShapeDtypeStruct((1, Dm), F32)),
        grid=(LP // tm,), in_specs=[row, vec, row, row], out_specs=(row, vec),
        compiler_params=_params(("arbitrary",)))(h, g, dn, dres)


def _loss_head(h, g, tgt, n_real, tm, name):
    LP, Dm = h.shape

    def body(h_ref, g_ref, t_ref, loss_ref, dh_ref, dg_ref):
        i = pl.program_id(0)
        x = h_ref[...]
        gg = g_ref[...]
        r = lax.rsqrt(jnp.mean(x * x, axis=1, keepdims=True) + RMS_EPS)
        xhat = x * r
        rows = i * tm + lax.broadcasted_iota(jnp.int32, (tm, 1), 0)
        real = jnp.logical_and(rows >= N_META, rows < n_real)
        diff = jnp.where(real, xhat * gg - t_ref[...], 0.0)
        dy = diff * (1.0 / Dm)
        dxh = dy * gg
        dh_ref[...] = r * (dxh - xhat * jnp.mean(dxh * xhat, axis=1, keepdims=True))

        @pl.when(i == 0)
        def _():
            dg_ref[...] = jnp.zeros_like(dg_ref)
            loss_ref[...] = jnp.zeros_like(loss_ref)

        dg_ref[...] += jnp.sum(dy * xhat, axis=0, keepdims=True)
        part = jnp.sum(jnp.sum(diff * diff, axis=1, keepdims=True), axis=0, keepdims=True)
        loss_ref[...] += jnp.broadcast_to(part * (0.5 / Dm), loss_ref.shape)

    row = pl.BlockSpec((tm, Dm), lambda i: (i, 0))
    vec = pl.BlockSpec((1, Dm), lambda i: (0, 0))
    return pl.pallas_call(
        body, name=name,
        out_shape=(jax.ShapeDtypeStruct((1, LANE), F32), jax.ShapeDtypeStruct((LP, Dm), F32),
                   jax.ShapeDtypeStruct((1, Dm), F32)),
        grid=(LP // tm,), in_specs=[row, vec, row],
        out_specs=(pl.BlockSpec((1, LANE), lambda i: (0, 0)), row, vec),
        compiler_params=_params(("arbitrary",)))(h, g, tgt)


def _fgate_fwd(proj, bf_p, name):
    LP = proj.shape[0]
    nb = LP // LANE

    def body(f_ref, b_ref, c_ref, lf_ref):
        x = f_ref[...] + b_ref[...]
        lf_ref[...] = jnp.minimum(x, 0.0) - jnp.log1p(jnp.exp(-jnp.abs(x)))
        ri = lax.broadcasted_iota(jnp.int32, (LANE, LANE), 0)
        ci = lax.broadcasted_iota(jnp.int32, (LANE, LANE), 1)
        tri = jnp.where(ri >= ci, 1.0, 0.0).astype(F32)

        def blk(i, carry):
            rows = pl.ds(pl.multiple_of(i * LANE, LANE), LANE)
            cb = jnp.dot(tri, lf_ref[rows, :], precision=lax.Precision.HIGHEST,
                         preferred_element_type=F32) + carry
            c_ref[rows, :] = cb
            return cb[LANE - 1:LANE, :]

        lax.fori_loop(0, nb, blk, jnp.zeros((1, LANE), F32))

    return pl.pallas_call(
        body, name=name, out_shape=jax.ShapeDtypeStruct((LP, LANE), F32), grid=(1,),
        in_specs=[pl.BlockSpec((LP, LANE), lambda i: (0, F_COL_BLK)),
                  pl.BlockSpec((1, LANE), lambda i: (0, 0))],
        out_specs=pl.BlockSpec((LP, LANE), lambda i: (0, 0)),
        scratch_shapes=[pltpu.VMEM((LP, LANE), F32)],
        compiler_params=_params(("arbitrary",)))(proj, bf_p)


def _fgate_bwd(proj, bf_p, dc, name):
    LP = proj.shape[0]
    nb = LP // LANE

    def body(f_ref, b_ref, dc_ref, dl_ref, db_ref):
        ri = lax.broadcasted_iota(jnp.int32, (LANE, LANE), 0)
        ci = lax.broadcasted_iota(jnp.int32, (LANE, LANE), 1)
        triu = jnp.where(ri <= ci, 1.0, 0.0).astype(F32)
        bb = b_ref[...]

        tail = jnp.zeros((1, LANE), F32)
        dbs = jnp.zeros((1, LANE), F32)
        for i in range(nb - 1, -1, -1):
            rows = slice(i * LANE, (i + 1) * LANE)
            gb = jnp.dot(triu, dc_ref[rows, :], precision=lax.Precision.HIGHEST,
                         preferred_element_type=F32) + tail
            x = f_ref[rows, :] + bb
            dl = gb * _sigmoid_tail(-x)
            dl_ref[rows, :] = dl.astype(dl_ref.dtype)
            tail = gb[0:1, :]
            dbs = dbs + jnp.sum(dl, axis=0, keepdims=True)
        db_ref[...] = dbs

    return pl.pallas_call(
        body, name=name,
        out_shape=(jax.ShapeDtypeStruct((LP, LANE), BF16), jax.ShapeDtypeStruct((1, LANE), F32)),
        grid=(1,),
        in_specs=[pl.BlockSpec((LP, LANE), lambda i: (0, F_COL_BLK)),
                  pl.BlockSpec((1, LANE), lambda i: (0, 0)),
                  pl.BlockSpec((LP, LANE), lambda i: (0, 0))],
        out_specs=(pl.BlockSpec((LP, LANE), lambda i: (0, 0)), pl.BlockSpec((1, LANE), lambda i: (0, 0))),
        compiler_params=_params(("arbitrary",)))(proj, bf_p, dc)


def _attn_fwd(q, k, v, c_col, c_row, name):
    Hh, LP, Dh = q.shape
    nb = LP // ATT_BLK
    scale = Dh ** -0.5
    nt = (((1,), (1,)), ((), ()))

    def body(q_ref, k_ref, v_ref, cc_ref, cr_ref, o_ref, lse_ref):
        i = pl.program_id(1)
        qb = q_ref[...]
        cq = cc_ref[...]
        rows = i * ATT_BLK + lax.broadcasted_iota(jnp.int32, (ATT_BLK, ATT_BLK), 0)
        cols0 = lax.broadcasted_iota(jnp.int32, (ATT_BLK, ATT_BLK), 1)

        def step(j, carry):
            m, l, acc = carry
            ks = pl.ds(pl.multiple_of(j * ATT_BLK, ATT_BLK), ATT_BLK)
            s = lax.dot_general(qb, k_ref[ks, :], nt, preferred_element_type=F32) * scale
            s = s + cq - cr_ref[j]
            s = jnp.where(cols0 + j * ATT_BLK <= rows, s, NEG)
            m_new = jnp.maximum(m, jnp.max(s, axis=1, keepdims=True))
            p = jnp.exp(s - m_new)
            alpha = jnp.exp(m - m_new)
            l = alpha * l + jnp.sum(p, axis=1, keepdims=True)
            acc = alpha * acc + jnp.dot(p.astype(BF16), v_ref[ks, :], preferred_element_type=F32)
            return m_new, l, acc

        init = (jnp.full((ATT_BLK, 1), NEG, F32), jnp.zeros((ATT_BLK, 1), F32),
                jnp.zeros((ATT_BLK, Dh), F32))
        m, l, acc = lax.fori_loop(0, i + 1, step, init)
        o_ref[...] = acc / l
        lse_ref[...] = m + jnp.log(l)

    qspec = pl.BlockSpec((None, ATT_BLK, Dh), lambda h, i: (h, i, 0))
    kspec = pl.BlockSpec((None, LP, Dh), lambda h, i: (h, 0, 0))
    colspec = pl.BlockSpec((None, ATT_BLK, 1), lambda h, i: (h, i, 0))
    rowspec = pl.BlockSpec((None, nb, 1, ATT_BLK), lambda h, i: (h, 0, 0, 0))
    return pl.pallas_call(
        body, name=name,
        out_shape=(jax.ShapeDtypeStruct((Hh, LP, Dh), F32), jax.ShapeDtypeStruct((Hh, LP, 1), F32)),
        grid=(Hh, nb), in_specs=[qspec, kspec, kspec, colspec, rowspec],
        out_specs=(qspec, colspec),
        compiler_params=_params(("parallel", "arbitrary")))(q, k, v, c_col, c_row)


def _attn_bwd(q, k, v, o, do, lse, c_col, c_row, name):
    Hh, LP, Dh = q.shape
    nb = LP // ATT_BLK
    scale = Dh ** -0.5
    nt = (((1,), (1,)), ((), ()))
    tn = (((0,), (0,)), ((), ()))

    def body(q_ref, k_ref, v_ref, o_ref, do_ref, lse_ref, cc_ref, cr_ref,
             dq_ref, dk_ref, dv_ref, dcq_ref, dc_ref, delta_ref):
        j = pl.program_id(1)

        @pl.when(j == 0)
        def _():
            dq_ref[...] = jnp.zeros_like(dq_ref)
            dcq_ref[...] = jnp.zeros_like(dcq_ref)
            dob_all = do_ref[...].astype(BF16).astype(F32)
            delta_ref[...] = jnp.sum(dob_all * o_ref[...], axis=1, keepdims=True)

        kb = k_ref[...]
        vb = v_ref[...]
        ck = cr_ref[j]
        rows0 = lax.broadcasted_iota(jnp.int32, (ATT_BLK, ATT_BLK), 0)
        cols = j * ATT_BLK + lax.broadcasted_iota(jnp.int32, (ATT_BLK, ATT_BLK), 1)

        def step(i, carry):
            dk, dv, dcs = carry
            qs = pl.ds(pl.multiple_of(i * ATT_BLK, ATT_BLK), ATT_BLK)
            qb = q_ref[qs, :]
            dob = do_ref[qs, :].astype(BF16)
            s = lax.dot_general(qb, kb, nt, preferred_element_type=F32) * scale
            s = s + cc_ref[qs, :] - ck
            s = jnp.where(cols <= rows0 + i * ATT_BLK, s, NEG)
            p = jnp.exp(s - lse_ref[qs, :])
            dp = lax.dot_general(dob, vb, nt, preferred_element_type=F32)
            ds = p * (dp - delta_ref[qs, :])
            dsb = ds.astype(BF16)
            dv = dv + lax.dot_general(p.astype(BF16), dob, tn, preferred_element_type=F32)
            dk = dk + lax.dot_general(dsb, qb, tn, preferred_element_type=F32) * scale
            dq_ref[qs, :] += jnp.dot(dsb, kb, preferred_element_type=F32) * scale
            dcq_ref[qs, :] += jnp.sum(ds, axis=1, keepdims=True)
            dcs = dcs - jnp.sum(ds, axis=0, keepdims=True)
            return dk, dv, dcs

        init = (jnp.zeros((ATT_BLK, Dh), F32), jnp.zeros((ATT_BLK, Dh), F32),
                jnp.zeros((1, ATT_BLK), F32))
        dk, dv, dcs = lax.fori_loop(j, nb, step, init)
        dk_ref[...] = dk
        dv_ref[...] = dv
        dc_ref[...] = dcs

    full = pl.BlockSpec((None, LP, Dh), lambda h, j: (h, 0, 0))
    blk = pl.BlockSpec((None, ATT_BLK, Dh), lambda h, j: (h, j, 0))
    col = pl.BlockSpec((None, LP, 1), lambda h, j: (h, 0, 0))
    rowspec = pl.BlockSpec((None, nb, 1, ATT_BLK), lambda h, j: (h, 0, 0, 0))
    return pl.pallas_call(
        body, name=name,
        out_shape=(jax.ShapeDtypeStruct((Hh, LP, Dh), F32), jax.ShapeDtypeStruct((Hh, LP, Dh), F32),
                   jax.ShapeDtypeStruct((Hh, LP, Dh), F32), jax.ShapeDtypeStruct((Hh, LP, 1), F32),
                   jax.ShapeDtypeStruct((Hh, nb, 1, ATT_BLK), F32)),
        grid=(Hh, nb), in_specs=[full, blk, blk, full, full, col, col, rowspec],
        out_specs=(full, blk, blk, col, pl.BlockSpec((None, None, 1, ATT_BLK), lambda h, j: (h, j, 0, 0))),
        scratch_shapes=[pltpu.VMEM((LP, 1), F32)],
        compiler_params=_params(("parallel", "arbitrary")))(q, k, v, o, do, lse, c_col, c_row)


AUG = 128
ONES_IN_K = HEAD_DIM
ONES_IN_Q = HEAD_DIM + 3
ATT_HEADS_PER_STEP = 8
ATT_HEADS_PER_STEP_BWD = 8


def _attn_prep(proj, c, name):
    LP = proj.shape[0]
    nb = LP // ATT_BLK
    tail_rows = AUG - HEAD_DIM

    def body(q_ref, k_ref, v_ref, c_ref, qT_ref, kT_ref, ka_ref, vT_ref):
        qt = (q_ref[...] * (HEAD_DIM ** -0.5)).T
        kt = k_ref[...].T
        vt = v_ref[...].T
        ct = c_ref[...].T
        hi = ct.astype(BF16).astype(F32)
        r1 = ct - hi
        mid = r1.astype(BF16).astype(F32)
        lo = (r1 - mid).astype(BF16).astype(F32)
        row = lax.broadcasted_iota(jnp.int32, (tail_rows, ATT_BLK), 0)
        ones = jnp.where(row < 3, 1.0, 0.0)
        for h in range(HEADS):
            cparts = jnp.where(row == 0, hi[h:h + 1], jnp.where(row == 1, mid[h:h + 1],
                               jnp.where(row == 2, lo[h:h + 1], 0.0)))
            hs = slice(h * HEAD_DIM, (h + 1) * HEAD_DIM)
            q_tail = cparts + pltpu.roll(ones, 3, 0)
            k_tail = ones - pltpu.roll(cparts, 3, 0)
            qT_ref[h] = jnp.concatenate([qt[hs], q_tail], axis=0).astype(BF16)
            kfull = jnp.concatenate([kt[hs], k_tail], axis=0)
            kT_ref[h] = kfull.astype(BF16)
            ka_ref[h] = kfull.T.astype(BF16)
            vT_ref[h] = vt[hs].astype(BF16)

    col = lambda j: pl.BlockSpec((ATT_BLK, FOX_W), lambda i: (i, j))
    blk = lambda r: pl.BlockSpec((HEADS, None, r, ATT_BLK), lambda i: (0, i, 0, 0))
    return pl.pallas_call(
        body, name=name,
        out_shape=(jax.ShapeDtypeStruct((HEADS, nb, AUG, ATT_BLK), BF16),
                   jax.ShapeDtypeStruct((HEADS, nb, AUG, ATT_BLK), BF16),
                   jax.ShapeDtypeStruct((HEADS, LP, AUG), BF16),
                   jax.ShapeDtypeStruct((HEADS, nb, HEAD_DIM, ATT_BLK), BF16)),
        grid=(nb,), in_specs=[col(0), col(1), col(2), pl.BlockSpec((ATT_BLK, LANE), lambda i: (i, 0))],
        out_specs=(blk(AUG), blk(AUG), pl.BlockSpec((HEADS, ATT_BLK, AUG), lambda i: (0, i, 0)), blk(HEAD_DIM)),
        compiler_params=_params(("parallel",)))(proj, proj, proj, c)


def _attn_rows(xT, scale, out_dtype, name):
    Hh, nb, R, _ = xT.shape

    def body(x_ref, o_ref):
        stack = jnp.concatenate([x_ref[h, 0:HEAD_DIM, :] for h in range(Hh)], axis=0)
        o_ref[...] = (stack * scale).T.astype(o_ref.dtype)

    return pl.pallas_call(
        body, name=name, out_shape=jax.ShapeDtypeStruct((nb * ATT_BLK, Hh * HEAD_DIM), out_dtype), grid=(nb,),
        in_specs=[pl.BlockSpec((Hh, None, R, ATT_BLK), lambda i: (0, i, 0, 0))],
        out_specs=pl.BlockSpec((ATT_BLK, Hh * HEAD_DIM), lambda i: (i, 0)),
        compiler_params=_params(("parallel",)))(xT)


def _attn_cols(x, name):
    LP = x.shape[0]
    nb = LP // ATT_BLK

    def body(x_ref, o_ref):
        xt = x_ref[...].T
        for h in range(HEADS):
            o_ref[h] = xt[h * HEAD_DIM:(h + 1) * HEAD_DIM].astype(o_ref.dtype)

    return pl.pallas_call(
        body, name=name, out_shape=jax.ShapeDtypeStruct((HEADS, nb, HEAD_DIM, ATT_BLK), BF16), grid=(nb,),
        in_specs=[pl.BlockSpec((ATT_BLK, FOX_W), lambda i: (i, 0))],
        out_specs=pl.BlockSpec((HEADS, None, HEAD_DIM, ATT_BLK), lambda i: (0, i, 0, 0)),
        compiler_params=_params(("parallel",)))(x)


def _attn_dc(dqT, dkT, name):
    Hh, nb, _, _ = dqT.shape

    def body(q_ref, k_ref, o_ref):
        row = lax.broadcasted_iota(jnp.int32, (LANE, ATT_BLK), 0)
        acc = jnp.zeros((LANE, ATT_BLK), F32)
        for h in range(Hh):
            d = q_ref[h, ONES_IN_K:ONES_IN_K + 1, :] - k_ref[h, ONES_IN_Q:ONES_IN_Q + 1, :]
            acc = jnp.where(row == h, d, acc)
        o_ref[...] = acc.T

    spec = pl.BlockSpec((Hh, None, AUG, ATT_BLK), lambda i: (0, i, 0, 0))
    return pl.pallas_call(
        body, name=name, out_shape=jax.ShapeDtypeStruct((nb * ATT_BLK, LANE), F32), grid=(nb,),
        in_specs=[spec, spec], out_specs=pl.BlockSpec((ATT_BLK, LANE), lambda i: (i, 0)),
        compiler_params=_params(("parallel",)))(dqT, dkT)


def _attn_fwd2(qT, k_aug, vT, name, comm=None):
    Hh, nb, _, _ = qT.shape
    LP = nb * ATT_BLK
    Dh = vT.shape[2]
    HB = ATT_HEADS_PER_STEP
    n_comm = 0 if comm is None else len(comm[0])

    def body(*refs):
        q_ref, k_ref, v_ref = refs[:3]
        o_ref, lse_ref = refs[3 + n_comm:5 + n_comm]
        if n_comm:
            start, wait = _gather_first_ops(refs[3:3 + n_comm], refs[5 + n_comm:5 + 2 * n_comm],
                                            refs[5 + 2 * n_comm], refs[6 + 2 * n_comm], comm[1])
            pl.when(pl.program_id(0) == 0)(start)
        keys = lax.broadcasted_iota(jnp.int32, (ATT_BLK, ATT_BLK), 0)
        qrys = lax.broadcasted_iota(jnp.int32, (ATT_BLK, ATT_BLK), 1)
        causal = keys <= qrys

        def q_block(i, _):
            def tile(j, carry, masked):
                ks = pl.ds(pl.multiple_of(j * ATT_BLK, ATT_BLK), ATT_BLK)
                s_all = [jnp.dot(k_ref[hh, ks, :], q_ref[hh, i], preferred_element_type=F32) for hh in range(HB)]
                stats, p_all = [], []
                for hh in range(HB):
                    m, l, _ = carry[hh]
                    s = jnp.where(causal, s_all[hh], NEG) if masked else s_all[hh]
                    m_new = jnp.maximum(m, jnp.max(s, axis=0, keepdims=True))
                    p = jnp.exp(s - m_new)
                    alpha = jnp.exp(m - m_new)
                    stats.append((m_new, alpha * l + jnp.sum(p, axis=0, keepdims=True), alpha))
                    p_all.append(p.astype(BF16))
                out = []
                for hh in range(HB):
                    m_new, l, alpha = stats[hh]
                    acc = alpha * carry[hh][2] + jnp.dot(v_ref[hh, j], p_all[hh], preferred_element_type=F32)
                    out.append((m_new, l, acc))
                return tuple(out)

            init = tuple((jnp.full((1, ATT_BLK), NEG, F32), jnp.zeros((1, ATT_BLK), F32),
                          jnp.zeros((Dh, ATT_BLK), F32)) for _ in range(HB))
            carry = lax.fori_loop(0, i, lambda j, cr: tile(j, cr, False), init)
            carry = tile(i, carry, True)
            for hh in range(HB):
                m, l, acc = carry[hh]
                o_ref[hh, i] = acc / l
                lse_ref[hh, i] = m + jnp.log(l)
            return 0

        lax.fori_loop(0, nb, q_block, 0)
        if n_comm:
            pl.when(pl.program_id(0) == Hh // HB - 1)(wait)

    blk = lambda r: pl.BlockSpec((HB, nb, r, ATT_BLK), lambda h: (h, 0, 0, 0))
    out_shape = (jax.ShapeDtypeStruct((Hh, nb, Dh, ATT_BLK), F32), jax.ShapeDtypeStruct((Hh, nb, 1, ATT_BLK), F32))
    scratch = []
    args = [qT, k_aug, vT]
    if n_comm:
        out_shape += tuple(jax.ShapeDtypeStruct(cut.full, s.dtype) for cut, s in zip(comm[1], comm[0]))
        scratch = [pltpu.SemaphoreType.DMA((4 * n_comm,)), pltpu.SemaphoreType.DMA((4 * n_comm,))]
        args += list(comm[0])
    return pl.pallas_call(
        body, name=name, out_shape=out_shape, grid=(Hh // HB,),
        in_specs=[blk(AUG), pl.BlockSpec((HB, LP, AUG), lambda h: (h, 0, 0)), blk(Dh)] + [ANY] * n_comm,
        out_specs=(blk(Dh), blk(1)) + (ANY,) * n_comm, scratch_shapes=scratch,
        compiler_params=_params(("arbitrary",)))(*args)


def _attn_bwd2(qT, kT, k_aug, v, oT, doT, lse, name, comm=None):
    Hh, nb, _, _ = qT.shape
    LP = nb * ATT_BLK
    Dh = v.shape[2]
    nt = (((1,), (1,)), ((), ()))
    tn = (((0,), (0,)), ((), ()))

    HB = ATT_HEADS_PER_STEP_BWD
    n_comm = 0 if comm is None else len(comm[0])

    def body(*refs):
        q_ref, kt_ref, k_ref, v_ref, o_ref, do_ref, lse_ref = refs[:7]
        parts = refs[7:7 + n_comm]
        dq_ref, dk_ref, dv_ref = refs[7 + n_comm:10 + n_comm]
        others = refs[10 + n_comm:10 + 2 * n_comm]
        delta_ref = refs[10 + 2 * n_comm]
        if n_comm:
            start, wait = _chip_exchange_ops(parts, others, refs[11 + 2 * n_comm], refs[12 + 2 * n_comm], comm[1])
            pl.when(pl.program_id(0) == 0)(start)
        keys = lax.broadcasted_iota(jnp.int32, (ATT_BLK, ATT_BLK), 0)
        qrys = lax.broadcasted_iota(jnp.int32, (ATT_BLK, ATT_BLK), 1)
        causal = keys <= qrys

        def prep(i, _):
            for hh in range(HB):
                delta_ref[hh, i] = jnp.sum(do_ref[hh, i].astype(F32) * o_ref[hh, i], axis=0, keepdims=True)
                dq_ref[hh, i] = jnp.zeros((AUG, ATT_BLK), F32)
            return 0

        lax.fori_loop(0, nb, prep, 0)

        def kv_block(j, _):
            ks = pl.ds(pl.multiple_of(j * ATT_BLK, ATT_BLK), ATT_BLK)

            def tile(i, carry, masked):
                s_all = [jnp.dot(k_ref[hh, ks, :], q_ref[hh, i], preferred_element_type=F32) for hh in range(HB)]
                dp_all = [lax.dot_general(v_ref[hh, j], do_ref[hh, i], tn, preferred_element_type=F32)
                          for hh in range(HB)]
                p_all, ds_all = [], []
                for hh in range(HB):
                    s = jnp.where(causal, s_all[hh], NEG) if masked else s_all[hh]
                    p = jnp.exp(s - lse_ref[hh, i])
                    ds_all.append((p * (dp_all[hh] - delta_ref[hh, i])).astype(BF16))
                    p_all.append(p.astype(BF16))
                out = []
                for hh in range(HB):
                    dk, dv = carry[hh]
                    dv = dv + lax.dot_general(do_ref[hh, i], p_all[hh], nt, preferred_element_type=F32)
                    dk = dk + lax.dot_general(q_ref[hh, i], ds_all[hh], nt, preferred_element_type=F32)
                    out.append((dk, dv))
                dq_new = [jnp.dot(kt_ref[hh, j], ds_all[hh], preferred_element_type=F32) for hh in range(HB)]
                for hh in range(HB):
                    dq_ref[hh, i] += dq_new[hh]
                return tuple(out)

            init = tuple((jnp.zeros((AUG, ATT_BLK), F32), jnp.zeros((Dh, ATT_BLK), F32)) for _ in range(HB))
            carry = tile(j, init, True)
            carry = lax.fori_loop(j + 1, nb, lambda i, cr: tile(i, cr, False), carry)
            for hh in range(HB):
                dk_ref[hh, j] = carry[hh][0]
                dv_ref[hh, j] = carry[hh][1]
            return 0

        lax.fori_loop(0, nb, kv_block, 0)
        if n_comm:
            pl.when(pl.program_id(0) == Hh // HB - 1)(wait)

    blk = lambda r: pl.BlockSpec((HB, nb, r, ATT_BLK), lambda h: (h, 0, 0, 0))
    row = lambda cols: pl.BlockSpec((HB, LP, cols), lambda h: (h, 0, 0))
    out_shape = (jax.ShapeDtypeStruct((Hh, nb, AUG, ATT_BLK), F32), jax.ShapeDtypeStruct((Hh, nb, AUG, ATT_BLK), F32),
                 jax.ShapeDtypeStruct((Hh, nb, Dh, ATT_BLK), F32))
    scratch = [pltpu.VMEM((HB, nb, 1, ATT_BLK), F32)]
    args = [qT, kT, k_aug, v, oT, doT, lse]
    if n_comm:
        out_shape += _chip_exchange_shapes(*comm)
        scratch += [pltpu.SemaphoreType.DMA((3 * n_comm,)), pltpu.SemaphoreType.DMA((3 * n_comm,))]
        args += list(comm[0])
    return pl.pallas_call(
        body, name=name, out_shape=out_shape, grid=(Hh // HB,),
        in_specs=[blk(AUG), blk(AUG), row(AUG), blk(Dh), blk(Dh), blk(Dh), blk(1)] + [ANY] * n_comm,
        out_specs=(blk(AUG), blk(AUG), blk(Dh)) + (ANY,) * n_comm,
        scratch_shapes=scratch,
        compiler_params=_params(("arbitrary",)))(*args)


CONV_HALO = 32
A_BLK = 3 * FOX_W // CONV_CH
G_BLK = A_BLK + 1


def _conf_fwd(proj, cw, cb, lg, lb, tm, name):
    LP = proj.shape[0]
    C = CONV_CH
    sub = _sub_rows(tm)
    hpb = tm // CONV_HALO

    def body(a_ref, g_ref, ah_ref, gh_ref, w_ref, cb_ref, lg_ref, lb_ref, u1_ref, u_ref, buf):
        r = pl.program_id(0)
        buf[CONV_HALO:CONV_HALO + tm, :] = a_ref[...] * _sigmoid(g_ref[...])
        buf[0:CONV_HALO, :] = jnp.where(r > 0, ah_ref[...] * _sigmoid(gh_ref[...]), 0.0)
        for s in range(tm // sub):
            for ct in range(C // LANE):
                ln = slice(ct * LANE, (ct + 1) * LANE)
                acc = jnp.broadcast_to(cb_ref[:, ln], (sub, LANE))
                for kk in range(CONV_K):
                    off = CONV_HALO + s * sub - (CONV_K - 1) + kk
                    acc = acc + w_ref[kk:kk + 1, ln] * buf[off:off + sub, ln]
                u1_ref[s * sub:(s + 1) * sub, ln] = acc
        u1 = u1_ref[...]
        mu = jnp.mean(u1, axis=1, keepdims=True)
        xc = u1 - mu
        var = jnp.mean(xc * xc, axis=1, keepdims=True)
        y = xc * lax.rsqrt(var + LN_EPS) * lg_ref[...] + lb_ref[...]
        u_ref[...] = (y * _sigmoid(y)).astype(u_ref.dtype)

    cur = lambda blk: pl.BlockSpec((tm, C), lambda r: (r, blk))
    halo = lambda blk: pl.BlockSpec((CONV_HALO, C), lambda r: (jnp.maximum(r * hpb - 1, 0), blk))
    vec = pl.BlockSpec((1, C), lambda r: (0, 0))
    out = pl.BlockSpec((tm, C), lambda r: (r, 0))
    return pl.pallas_call(
        body, name=name,
        out_shape=(jax.ShapeDtypeStruct((LP, C), F32), jax.ShapeDtypeStruct((LP, C), BF16)),
        grid=(LP // tm,),
        in_specs=[cur(A_BLK), cur(G_BLK), halo(A_BLK), halo(G_BLK),
                  pl.BlockSpec((CONV_HALO, C), lambda r: (0, 0)), vec, vec, vec],
        out_specs=(out, out),
        scratch_shapes=[pltpu.VMEM((CONV_HALO + tm, C), F32)],
        compiler_params=_params(("parallel",)))(proj, proj, proj, proj, cw, cb, lg, lb)


def _conf_bwd(proj, u1, dcat, cw, lg, lb, tm, name):
    LP = proj.shape[0]
    C = CONV_CH
    sub = _sub_rows(tm)
    hpb = tm // CONV_HALO
    nblk = LP // tm
    last_halo = LP // CONV_HALO - 1

    def body(a_ref, g_ref, ah_ref, gh_ref, u1_ref, u1n_ref, du_ref, dun_ref, w_ref, lg_ref, lb_ref,
             dadg_ref, dw_ref, dcb_ref, dlg_ref, dlb_ref, ubuf, dbuf, du0):
        r = pl.program_id(0)
        lgv = lg_ref[...]
        lbv = lb_ref[...]

        def ln_silu_bwd(u1v, duv):
            mu = jnp.mean(u1v, axis=1, keepdims=True)
            xc = u1v - mu
            rstd = lax.rsqrt(jnp.mean(xc * xc, axis=1, keepdims=True) + LN_EPS)
            xhat = xc * rstd
            y = xhat * lgv + lbv
            sg = _sigmoid(y)
            dy = duv * (sg * (1.0 + y * (1.0 - sg)))
            dxh = dy * lgv
            du1 = rstd * (dxh - jnp.mean(dxh, axis=1, keepdims=True)
                          - xhat * jnp.mean(dxh * xhat, axis=1, keepdims=True))
            return du1, dy, xhat

        @pl.when(r == 0)
        def _():
            dw_ref[...] = jnp.zeros_like(dw_ref)
            dcb_ref[...] = jnp.zeros_like(dcb_ref)
            dlg_ref[...] = jnp.zeros_like(dlg_ref)
            dlb_ref[...] = jnp.zeros_like(dlb_ref)

        du1, dy, xhat = ln_silu_bwd(u1_ref[...], du_ref[...])
        dlg_ref[...] += jnp.sum(dy * xhat, axis=0, keepdims=True)
        dlb_ref[...] += jnp.sum(dy, axis=0, keepdims=True)
        dcb_ref[...] += jnp.sum(du1, axis=0, keepdims=True)
        dbuf[0:tm, :] = du1
        du1n, _, _ = ln_silu_bwd(u1n_ref[...], dun_ref[...])
        dbuf[tm:tm + CONV_HALO, :] = jnp.where(r < nblk - 1, du1n, 0.0)
        ubuf[CONV_HALO:CONV_HALO + tm, :] = a_ref[...] * _sigmoid(g_ref[...])
        ubuf[0:CONV_HALO, :] = jnp.where(r > 0, ah_ref[...] * _sigmoid(gh_ref[...]), 0.0)

        for ct in range(C // LANE):
            ln = slice(ct * LANE, (ct + 1) * LANE)
            for s in range(tm // sub):
                d_here = dbuf[s * sub:(s + 1) * sub, ln]
                acc = jnp.zeros((sub, LANE), F32)
                for kk in range(CONV_K):
                    fo = s * sub + (CONV_K - 1) - kk
                    acc = acc + w_ref[kk:kk + 1, ln] * dbuf[fo:fo + sub, ln]
                    bo = CONV_HALO + s * sub - (CONV_K - 1) + kk
                    dw_ref[kk:kk + 1, ln] += jnp.sum(d_here * ubuf[bo:bo + sub, ln], axis=0, keepdims=True)
                du0[s * sub:(s + 1) * sub, ln] = acc
        a = a_ref[...]
        sg = _sigmoid(g_ref[...])
        d0 = du0[...]
        dadg_ref[:, 0:C] = (d0 * sg).astype(dadg_ref.dtype)
        dadg_ref[:, C:2 * C] = (d0 * a * sg * (1.0 - sg)).astype(dadg_ref.dtype)

    cur = lambda blk: pl.BlockSpec((tm, C), lambda r: (r, blk))
    prev = lambda blk: pl.BlockSpec((CONV_HALO, C), lambda r: (jnp.maximum(r * hpb - 1, 0), blk))
    nxt = lambda blk: pl.BlockSpec((CONV_HALO, C), lambda r: (jnp.minimum((r + 1) * hpb, last_halo), blk))
    vec = pl.BlockSpec((1, C), lambda r: (0, 0))
    wspec = pl.BlockSpec((CONV_HALO, C), lambda r: (0, 0))
    return pl.pallas_call(
        body, name=name,
        out_shape=(jax.ShapeDtypeStruct((LP, 2 * C), BF16), jax.ShapeDtypeStruct((CONV_HALO, C), F32),
                   jax.ShapeDtypeStruct((1, C), F32), jax.ShapeDtypeStruct((1, C), F32),
                   jax.ShapeDtypeStruct((1, C), F32)),
        grid=(nblk,),
        in_specs=[cur(A_BLK), cur(G_BLK), prev(A_BLK), prev(G_BLK), cur(0), nxt(0), cur(1), nxt(1),
                  wspec, vec, vec],
        out_specs=(pl.BlockSpec((tm, 2 * C), lambda r: (r, 0)), wspec, vec, vec, vec),
        scratch_shapes=[pltpu.VMEM((CONV_HALO + tm, C), F32), pltpu.VMEM((tm + CONV_HALO, C), F32),
                        pltpu.VMEM((tm, C), F32)],
        compiler_params=_params(("arbitrary",)))(proj, proj, proj, proj, u1, u1, dcat, dcat, cw, lg, lb)


FFN_HALO = 8
FFN_TC = 256
FFN_K = 3


def _ffn_conv(buf, w_ref, b_ref, s, sub, ln):
    acc = jnp.broadcast_to(b_ref[:, ln], (sub, LANE))
    for kk in range(FFN_K):
        off = FFN_HALO + s * sub - (FFN_K - 1) + kk
        acc = acc + w_ref[kk:kk + 1, ln] * buf[off:off + sub, ln]
    return acc


def _ffn_act_fwd(up, w, b, tm, name, host=None):
    LP, F = up.shape[0], up.shape[1] // 2
    upg = upv = up
    nct = F // FFN_TC
    sub = _sub_rows(tm)
    hpb = tm // FFN_HALO
    n_host = 0 if host is None else len(host[0])
    nrb = LP // tm

    def body(*refs):
        g_ref, v_ref, gh_ref, vh_ref, wg_ref, wv_ref, bg_ref, bv_ref = refs[:8]
        act_ref = refs[8 + n_host]
        gbuf, vbuf = refs[9 + 2 * n_host:11 + 2 * n_host]
        if n_host:
            start, wait = _gather_first_ops(refs[8:8 + n_host], refs[9 + n_host:9 + 2 * n_host],
                                            refs[11 + 2 * n_host], refs[12 + 2 * n_host], host[1])
            pl.when(jnp.logical_and(pl.program_id(0) == 0, pl.program_id(1) == 0))(start)
        r = pl.program_id(1)
        gbuf[FFN_HALO:FFN_HALO + tm, :] = g_ref[...]
        vbuf[FFN_HALO:FFN_HALO + tm, :] = v_ref[...]
        gbuf[0:FFN_HALO, :] = jnp.where(r > 0, gh_ref[...], 0.0)
        vbuf[0:FFN_HALO, :] = jnp.where(r > 0, vh_ref[...], 0.0)
        for s in range(tm // sub):
            for ct in range(FFN_TC // LANE):
                ln = slice(ct * LANE, (ct + 1) * LANE)
                gc = _ffn_conv(gbuf, wg_ref, bg_ref, s, sub, ln)
                vc = _ffn_conv(vbuf, wv_ref, bv_ref, s, sub, ln)
                act_ref[s * sub:(s + 1) * sub, ln] = (gc * _sigmoid(gc) * vc).astype(act_ref.dtype)
        if n_host:
            pl.when(jnp.logical_and(pl.program_id(0) == nct - 1, pl.program_id(1) == nrb - 1))(wait)

    cur = pl.BlockSpec((tm, FFN_TC), lambda c, r: (r, c))
    halo = pl.BlockSpec((FFN_HALO, FFN_TC), lambda c, r: (jnp.maximum(r * hpb - 1, 0), c))
    wg = pl.BlockSpec((8, FFN_TC), lambda c, r: (0, c))
    wv = pl.BlockSpec((8, FFN_TC), lambda c, r: (0, nct + c))
    bg = pl.BlockSpec((1, FFN_TC), lambda c, r: (0, c))
    bv = pl.BlockSpec((1, FFN_TC), lambda c, r: (0, nct + c))
    curv = pl.BlockSpec((tm, FFN_TC), lambda c, r: (r, nct + c))
    halov = pl.BlockSpec((FFN_HALO, FFN_TC), lambda c, r: (jnp.maximum(r * hpb - 1, 0), nct + c))
    out_shape = jax.ShapeDtypeStruct((LP, F), BF16)
    out_specs = cur
    in_specs = [cur, curv, halo, halov, wg, wv, bg, bv]
    args = [upg, upv, upg, upv, w, w, b, b]
    scratch = [pltpu.VMEM((FFN_HALO + tm, FFN_TC), F32)] * 2
    if n_host:
        anyspec = pl.BlockSpec(memory_space=pl.ANY)
        in_specs += [anyspec] * n_host
        args += list(host[0])
        out_shape = (out_shape,) + tuple(jax.ShapeDtypeStruct(cut.full, s.dtype) for cut, s in zip(host[1], host[0]))
        out_specs = (cur,) + (anyspec,) * n_host
        scratch += [pltpu.SemaphoreType.DMA((4 * n_host,)), pltpu.SemaphoreType.DMA((4 * n_host,))]
    sem = ("arbitrary", "arbitrary") if n_host else ("parallel", "parallel")
    return pl.pallas_call(
        body, name=name, out_shape=out_shape, grid=(nct, nrb), in_specs=in_specs, out_specs=out_specs,
        scratch_shapes=scratch, compiler_params=_params(sem))(*args)


def _ffn_act_bwd(up, dact, w, b, tm, name, comm=None):
    LP, F = up.shape[0], up.shape[1] // 2
    upg = upv = up
    nct = F // FFN_TC
    sub = _sub_rows(tm)
    hpb = tm // FFN_HALO
    nblk = LP // tm
    last_halo = LP // FFN_HALO - 1
    TB = tm + 2 * FFN_HALO
    n_comm = 0 if comm is None else len(comm[0])

    def body(*refs):
        (g_ref, v_ref, gp_ref, vp_ref, gn_ref, vn_ref, da_ref, dan_ref,
         wg_ref, wv_ref, bg_ref, bv_ref) = refs[:12]
        dup_ref, dwg_ref, dwv_ref, dbg_ref, dbv_ref = refs[12 + n_comm:17 + n_comm]
        gbuf, vbuf, dgb, dvb = refs[17 + 2 * n_comm:21 + 2 * n_comm]
        if n_comm:
            start, wait = _chip_exchange_ops(refs[12:12 + n_comm], refs[17 + n_comm:17 + 2 * n_comm],
                                             refs[21 + 2 * n_comm], refs[22 + 2 * n_comm], comm[1])
            pl.when(jnp.logical_and(pl.program_id(0) == 0, pl.program_id(1) == 0))(start)
        r = pl.program_id(1)
        dg_ref = dup_ref.at[0]
        dv_ref = dup_ref.at[1]
        first = r == 0
        last = r == nblk - 1

        @pl.when(first)
        def _():
            dwg_ref[...] = jnp.zeros_like(dwg_ref)
            dwv_ref[...] = jnp.zeros_like(dwv_ref)
            dbg_ref[...] = jnp.zeros_like(dbg_ref)
            dbv_ref[...] = jnp.zeros_like(dbv_ref)

        for buf, c_ref, p_ref, n_ref in ((gbuf, g_ref, gp_ref, gn_ref), (vbuf, v_ref, vp_ref, vn_ref)):
            buf[0:FFN_HALO, :] = jnp.where(first, 0.0, p_ref[...])
            buf[FFN_HALO:FFN_HALO + tm, :] = c_ref[...]
            buf[FFN_HALO + tm:TB, :] = jnp.where(last, 0.0, n_ref[...])

        def dconv(s0, nrows, ln, dact_v):
            xg = [gbuf[s0 - (FFN_K - 1) + kk:s0 - (FFN_K - 1) + kk + nrows, ln] for kk in range(FFN_K)]
            xv = [vbuf[s0 - (FFN_K - 1) + kk:s0 - (FFN_K - 1) + kk + nrows, ln] for kk in range(FFN_K)]
            gc = jnp.broadcast_to(bg_ref[:, ln], (nrows, LANE))
            vc = jnp.broadcast_to(bv_ref[:, ln], (nrows, LANE))
            for kk in range(FFN_K):
                gc = gc + wg_ref[kk:kk + 1, ln] * xg[kk]
                vc = vc + wv_ref[kk:kk + 1, ln] * xv[kk]
            sg = _sigmoid(gc)
            return dact_v * vc * (sg * (1.0 + gc * (1.0 - sg))), dact_v * (gc * sg), xg, xv

        colsum = lambda t: jnp.sum(t, axis=0, keepdims=True)
        for ct in range(FFN_TC // LANE):
            ln = slice(ct * LANE, (ct + 1) * LANE)
            zero = jnp.zeros((1, LANE), F32)
            dwg, dwv, dbg, dbv = [zero] * FFN_K, [zero] * FFN_K, zero, zero
            for s in range(tm // sub):
                dgc, dvc, xg, xv = dconv(FFN_HALO + s * sub, sub, ln, da_ref[s * sub:(s + 1) * sub, ln])
                dgb[s * sub:(s + 1) * sub, ln] = dgc
                dvb[s * sub:(s + 1) * sub, ln] = dvc
                dwg = [dwg[kk] + colsum(dgc * xg[kk]) for kk in range(FFN_K)]
                dwv = [dwv[kk] + colsum(dvc * xv[kk]) for kk in range(FFN_K)]
                dbg, dbv = dbg + colsum(dgc), dbv + colsum(dvc)
            for kk in range(FFN_K):
                dwg_ref[kk:kk + 1, ln] += dwg[kk]
                dwv_ref[kk:kk + 1, ln] += dwv[kk]
            dbg_ref[:, ln] += dbg
            dbv_ref[:, ln] += dbv
            dgc, dvc, _, _ = dconv(FFN_HALO + tm, FFN_HALO, ln, jnp.where(last, 0.0, dan_ref[:, ln]))
            dgb[tm:tm + FFN_HALO, ln] = dgc
            dvb[tm:tm + FFN_HALO, ln] = dvc
            for dbuf, w_ref, dout in ((dgb, wg_ref, dg_ref), (dvb, wv_ref, dv_ref)):
                for s in range(tm // sub):
                    acc = jnp.zeros((sub, LANE), F32)
                    for kk in range(FFN_K):
                        fo = s * sub + (FFN_K - 1) - kk
                        acc = acc + w_ref[kk:kk + 1, ln] * dbuf[fo:fo + sub, ln]
                    dout[s * sub:(s + 1) * sub, ln] = acc.astype(dout.dtype)
        if n_comm:
            pl.when(jnp.logical_and(pl.program_id(0) == nct - 1, pl.program_id(1) == nblk - 1))(wait)

    cur = pl.BlockSpec((tm, FFN_TC), lambda c, r: (r, c))
    prev = pl.BlockSpec((FFN_HALO, FFN_TC), lambda c, r: (jnp.maximum(r * hpb - 1, 0), c))
    nxt = pl.BlockSpec((FFN_HALO, FFN_TC), lambda c, r: (jnp.minimum((r + 1) * hpb, last_halo), c))
    wg = pl.BlockSpec((8, FFN_TC), lambda c, r: (0, c))
    wv = pl.BlockSpec((8, FFN_TC), lambda c, r: (0, nct + c))
    bg = pl.BlockSpec((1, FFN_TC), lambda c, r: (0, c))
    bv = pl.BlockSpec((1, FFN_TC), lambda c, r: (0, nct + c))
    curv = pl.BlockSpec((tm, FFN_TC), lambda c, r: (r, nct + c))
    prevv = pl.BlockSpec((FFN_HALO, FFN_TC), lambda c, r: (jnp.maximum(r * hpb - 1, 0), nct + c))
    nxtv = pl.BlockSpec((FFN_HALO, FFN_TC), lambda c, r: (jnp.minimum((r + 1) * hpb, last_halo), nct + c))
    out_shape = (jax.ShapeDtypeStruct((2, LP, F), BF16),
                 jax.ShapeDtypeStruct((8, F), F32), jax.ShapeDtypeStruct((8, F), F32),
                 jax.ShapeDtypeStruct((1, F), F32), jax.ShapeDtypeStruct((1, F), F32))
    out_specs = (pl.BlockSpec((2, tm, FFN_TC), lambda c, r: (0, r, c)),
                 pl.BlockSpec((8, FFN_TC), lambda c, r: (0, c)),
                 pl.BlockSpec((8, FFN_TC), lambda c, r: (0, c)),
                 pl.BlockSpec((1, FFN_TC), lambda c, r: (0, c)),
                 pl.BlockSpec((1, FFN_TC), lambda c, r: (0, c)))
    in_specs = [cur, curv, prev, prevv, nxt, nxtv, cur, nxt, wg, wv, bg, bv]
    args = [upg, upv, upg, upv, upg, upv, dact, dact, w, w, b, b]
    scratch = [pltpu.VMEM((TB, FFN_TC), F32), pltpu.VMEM((TB, FFN_TC), F32),
               pltpu.VMEM((tm + FFN_HALO, FFN_TC), F32), pltpu.VMEM((tm + FFN_HALO, FFN_TC), F32)]
    if n_comm:
        anyspec = pl.BlockSpec(memory_space=pl.ANY)
        in_specs += [anyspec] * n_comm
        args += list(comm[0])
        out_shape += _chip_exchange_shapes(*comm)
        out_specs += (anyspec,) * n_comm
        scratch += [pltpu.SemaphoreType.DMA((3 * n_comm,)), pltpu.SemaphoreType.DMA((3 * n_comm,))]
    sem = ("arbitrary", "arbitrary") if n_comm else ("parallel", "arbitrary")
    dup, dwg, dwv, dbg, dbv, *others = pl.pallas_call(
        body, name=name, out_shape=out_shape, grid=(nct, nblk), in_specs=in_specs, out_specs=out_specs,
        scratch_shapes=scratch, compiler_params=_params(sem))(*args)
    return (dup, jnp.concatenate([dwg, dwv], axis=1), jnp.concatenate([dbg, dbv], axis=1)) + tuple(others)


POOL_HALO = 16


def _pool_fwd(h, g, pw, pb, ps, tm, name):
    LP, Dm = h.shape
    sub = _sub_rows(tm)
    hpb = tm // POOL_HALO

    def body(h_ref, hh_ref, g_ref, pw_ref, pb_ref, ps_ref, o_ref, d_ref, buf):
        r = pl.program_id(0)
        gg = g_ref[...]

        def norm(x):
            return x * lax.rsqrt(jnp.mean(x * x, axis=1, keepdims=True) + RMS_EPS) * gg

        x = h_ref[...]
        buf[POOL_HALO:POOL_HALO + tm, :] = norm(x)
        buf[0:POOL_HALO, :] = jnp.where(r > 0, norm(hh_ref[...]), 0.0)
        for gi, w in enumerate(POOL_WINDOWS):
            ln = slice(gi * POOL_G, (gi + 1) * POOL_G)
            for s in range(tm // sub):
                base = POOL_HALO + s * sub
                acc = buf[base:base + sub, ln]
                for jj in range(1, w):
                    acc = acc + buf[base - jj:base - jj + sub, ln]
                t = r * tm + s * sub + lax.broadcasted_iota(jnp.int32, (sub, 1), 0)
                cnt = jnp.minimum(t + 1, w).astype(F32)
                d_ref[s * sub:(s + 1) * sub, ln] = (acc / cnt - buf[base:base + sub, ln]).astype(d_ref.dtype)
            y = jnp.dot(d_ref[:, ln], pw_ref[gi], preferred_element_type=F32) + pb_ref[:, ln]
            o_ref[:, ln] = x[:, ln] + y * ps_ref[:, ln]

    row = pl.BlockSpec((tm, Dm), lambda r: (r, 0))
    halo = pl.BlockSpec((POOL_HALO, Dm), lambda r: (jnp.maximum(r * hpb - 1, 0), 0))
    vec = pl.BlockSpec((1, Dm), lambda r: (0, 0))
    wsp = pl.BlockSpec((len(POOL_WINDOWS), POOL_G, POOL_G), lambda r: (0, 0, 0))
    return pl.pallas_call(
        body, name=name,
        out_shape=(jax.ShapeDtypeStruct((LP, Dm), F32), jax.ShapeDtypeStruct((LP, Dm), BF16)),
        grid=(LP // tm,), in_specs=[row, halo, vec, wsp, vec, vec], out_specs=(row, row),
        scratch_shapes=[pltpu.VMEM((POOL_HALO + tm, Dm), F32)],
        compiler_params=_params(("parallel",)))(h, h, g, pw, pb, ps)


def _pool_bwd(h, g, d, pw, pb, ps, dh_out, tm, name):
    LP, Dm = h.shape
    sub = _sub_rows(tm)
    hpb = tm // POOL_HALO
    nblk = LP // tm
    last_halo = LP // POOL_HALO - 1
    nt = (((1,), (1,)), ((), ()))
    tn = (((0,), (0,)), ((), ()))

    def body(h_ref, g_ref, d_ref, pw_ref, pb_ref, ps_ref, do_ref, don_ref,
             dh_ref, dpw_ref, dpb_ref, dps_ref, dg_ref, ebuf, ddb, dnb):
        r = pl.program_id(0)

        @pl.when(r == 0)
        def _():
            dpw_ref[...] = jnp.zeros_like(dpw_ref)
            dpb_ref[...] = jnp.zeros_like(dpb_ref)
            dps_ref[...] = jnp.zeros_like(dps_ref)
            dg_ref[...] = jnp.zeros_like(dg_ref)

        for gi, w in enumerate(POOL_WINDOWS):
            ln = slice(gi * POOL_G, (gi + 1) * POOL_G)
            wg = pw_ref[gi]
            dog = do_ref[:, ln]
            dg_b = d_ref[:, ln]
            y_pre = jnp.dot(dg_b, wg, preferred_element_type=F32) + pb_ref[:, ln]
            dps_ref[:, ln] += jnp.sum(dog * y_pre, axis=0, keepdims=True)
            dy = dog * ps_ref[:, ln]
            dpb_ref[:, ln] += jnp.sum(dy, axis=0, keepdims=True)
            dyb = dy.astype(BF16)
            dpw_ref[gi] += lax.dot_general(dg_b, dyb, tn, preferred_element_type=F32)
            dd = lax.dot_general(dyb, wg, nt, preferred_element_type=F32)
            ddb[:, ln] = dd
            t = r * tm + lax.broadcasted_iota(jnp.int32, (tm, 1), 0)
            ebuf[0:tm, ln] = dd / jnp.minimum(t + 1, w).astype(F32)
            dyn = (don_ref[:, ln] * ps_ref[:, ln]).astype(BF16)
            ddn = lax.dot_general(dyn, wg, nt, preferred_element_type=F32)
            tn_ = (r + 1) * tm + lax.broadcasted_iota(jnp.int32, (POOL_HALO, 1), 0)
            ebuf[tm:tm + POOL_HALO, ln] = jnp.where(r < nblk - 1, ddn / jnp.minimum(tn_ + 1, w).astype(F32), 0.0)
            for s in range(tm // sub):
                acc = ebuf[s * sub:(s + 1) * sub, ln]
                for jj in range(1, w):
                    acc = acc + ebuf[s * sub + jj:s * sub + jj + sub, ln]
                dnb[s * sub:(s + 1) * sub, ln] = acc - ddb[s * sub:(s + 1) * sub, ln]
        x = h_ref[...]
        rr = lax.rsqrt(jnp.mean(x * x, axis=1, keepdims=True) + RMS_EPS)
        xhat = x * rr
        dn = dnb[...]
        dxh = dn * g_ref[...]
        dh_ref[...] = do_ref[...] + rr * (dxh - xhat * jnp.mean(dxh * xhat, axis=1, keepdims=True))
        dg_ref[...] += jnp.sum(dn * xhat, axis=0, keepdims=True)

    row = pl.BlockSpec((tm, Dm), lambda r: (r, 0))
    nxt = pl.BlockSpec((POOL_HALO, Dm), lambda r: (jnp.minimum((r + 1) * hpb, last_halo), 0))
    vec = pl.BlockSpec((1, Dm), lambda r: (0, 0))
    wsp = pl.BlockSpec((len(POOL_WINDOWS), POOL_G, POOL_G), lambda r: (0, 0, 0))
    return pl.pallas_call(
        body, name=name,
        out_shape=(jax.ShapeDtypeStruct((LP, Dm), F32),
                   jax.ShapeDtypeStruct((len(POOL_WINDOWS), POOL_G, POOL_G), F32),
                   jax.ShapeDtypeStruct((1, Dm), F32), jax.ShapeDtypeStruct((1, Dm), F32),
                   jax.ShapeDtypeStruct((1, Dm), F32)),
        grid=(nblk,), in_specs=[row, vec, row, wsp, vec, vec, row, nxt],
        out_specs=(row, wsp, vec, vec, vec),
        scratch_shapes=[pltpu.VMEM((tm + POOL_HALO, Dm), F32), pltpu.VMEM((tm, Dm), F32),
                        pltpu.VMEM((tm, Dm), F32)],
        compiler_params=_params(("arbitrary",)))(h, g, d, pw, pb, ps, dh_out, dh_out)


def _adamw(w, g, m, v, name):
    shape = w.shape
    cols = shape[-1]
    rows = int(np.prod(shape[:-1])) if len(shape) > 1 else 1
    w2, g2, m2, v2 = (t.reshape(rows, cols) for t in (w, g, m, v))
    tr = rows
    for cand in (256, 128, 64, 32, 16, 8):
        if rows % cand == 0 and rows > cand:
            tr = cand
            break
    c1 = float(1.0 - ADAM_B1 ** ADAM_STEP)
    c2 = float(1.0 - ADAM_B2 ** ADAM_STEP)

    def body(w_ref, g_ref, m_ref, v_ref, d_ref, mo_ref, vo_ref):
        gg = g_ref[...]
        mn = ADAM_B1 * m_ref[...] + (1.0 - ADAM_B1) * gg
        vn = ADAM_B2 * v_ref[...] + (1.0 - ADAM_B2) * (gg * gg)
        m_hat = mn / c1
        v_hat = vn / c2
        d_ref[...] = -ADAM_LR * (m_hat / (jnp.sqrt(v_hat) + ADAM_EPS) + ADAM_WD * w_ref[...])
        mo_ref[...] = mn
        vo_ref[...] = vn

    spec = pl.BlockSpec((tr, cols), lambda i: (i, 0))
    sds = jax.ShapeDtypeStruct((rows, cols), F32)
    d2, mo, vo = pl.pallas_call(
        body, name=name, out_shape=(sds, sds, sds), grid=(rows // tr,),
        in_specs=[spec] * 4, out_specs=(spec,) * 3,
        compiler_params=_params(("parallel",)))(w2, g2, m2, v2)
    return d2.reshape(shape), mo.reshape(shape), vo.reshape(shape)


def _row_tiles(LP):
    tm = LP // 4
    assert LP % 4 == 0 and tm % CONV_HALO == 0 and LP % ATT_BLK == 0, LP
    return tm, LP // 2


def _heads(t, LP):
    return t.reshape(LP, HEADS, HEAD_DIM).transpose(1, 0, 2)


def _unheads(t, LP):
    return t.transpose(1, 0, 2).reshape(LP, FOX_W)


def _ffn_fwd(h, gain, wug, wuv, cw, cb, wd, tm, tmm, tag):
    n = _rms_fwd(h, gain, BF16, tm, f"ffn_norm_{tag}")
    upg = _mm(n, wug, "nn", F32, tmm, 256, f"ffn_up_gate_{tag}")
    upv = _mm(n, wuv, "nn", F32, tmm, 256, f"ffn_up_val_{tag}")
    act = _ffn_act_fwd(upg, upv, cw, cb, tm, f"ffn_act_{tag}")
    out = _mm(act, wd, "nn", F32, tmm, 512, f"ffn_down_{tag}", add=h)
    return out, (n, upg, upv, act)


def _ffn_bwd(h, gain, wug, wuv, cw, cb, wd, saved, dout, tm, tmm, tag):
    n, upg, upv, act = saved
    dact = _mm(dout, wd, "nt", F32, tmm, 256, f"ffn_dact_{tag}")
    dwd = _mm(act, dout, "tn", F32, 256, 512, f"ffn_dwdown_{tag}")
    dupg, dupv, dcw, dcb = _ffn_act_bwd(upg, upv, dact, cw, cb, tm, f"ffn_act_bwd_{tag}")
    dn = _mm(dupg, wug, "nt", F32, tm, 512, f"ffn_dn_gate_{tag}")
    dn = _mm(dupv, wuv, "nt", F32, tm, 512, f"ffn_dn_val_{tag}", add=dn)
    dwug = _mm(n, dupg, "tn", F32, 512, 256, f"ffn_dwup_gate_{tag}")
    dwuv = _mm(n, dupv, "tn", F32, 512, 256, f"ffn_dwup_val_{tag}")
    dh, dgain = _rms_bwd(h, gain, dn, dout, tm, f"ffn_norm_bwd_{tag}")
    return dh, dict(gain=dgain, wug=dwug, wuv=dwuv, cw=dcw[:FFN_K], cb=dcb, wd=dwd)


def _local_step(h0, tgt, W, n_real):
    LP = h0.shape[0]
    tm, tmm = _row_tiles(LP)
    nb = LP // ATT_BLK
    G = {}

    n0 = _rms_fwd(h0, W["mix_norm_even"], BF16, tm, "mix_norm_even")
    proj = _mm(n0, W["w_in_p"], "nn", F32, tmm, 384, "in_proj")
    c = _fgate_fwd(proj, W["b_f_p"], "forget_gate")
    cT = c[:, :HEADS].T
    c_col = cT[:, :, None]
    c_row = cT.reshape(HEADS, nb, 1, ATT_BLK)
    qkv = proj[:, :3 * FOX_W].astype(BF16)
    q, k, v = (_heads(qkv[:, i * FOX_W:(i + 1) * FOX_W], LP) for i in range(3))
    o, lse = _attn_fwd(q, k, v, c_col, c_row, "fox_attention")
    u1, u = _conf_fwd(proj, W["conv_w_p"], W["conv_b"], W["ln_g"], W["ln_b"], tm, "conformer")
    cat = jnp.concatenate([_unheads(o, LP).astype(BF16), u], axis=1)
    h1 = _mm(cat, W["w_out"], "nn", F32, tmm, 512, "out_proj", add=h0)
    h2, ffn0 = _ffn_fwd(h1, W["ffn_norm"][0:1], W["w_up_g"][0], W["w_up_v"][0], W["ffn_conv_w_p"][0],
                        W["ffn_conv_b"][0:1], W["w_down"][0], tm, tmm, "0")
    h3, dpool = _pool_fwd(h2, W["mix_norm_odd"], W["pool_w"], W["pool_b"], W["pool_scale"], tm, "pool_mixer")
    h4, ffn1 = _ffn_fwd(h3, W["ffn_norm"][1:2], W["w_up_g"][1], W["w_up_v"][1], W["ffn_conv_w_p"][1],
                        W["ffn_conv_b"][1:2], W["w_down"][1], tm, tmm, "1")
    loss, dh4, G["final_norm"] = _loss_head(h4, W["final_norm"], tgt, n_real, tm, "loss_head")

    dh3, g1 = _ffn_bwd(h3, W["ffn_norm"][1:2], W["w_up_g"][1], W["w_up_v"][1], W["ffn_conv_w_p"][1],
                       W["ffn_conv_b"][1:2], W["w_down"][1], ffn1, dh4, tm, tmm, "1")
    dh2, G["pool_w"], G["pool_b"], G["pool_scale"], G["mix_norm_odd"] = _pool_bwd(
        h2, W["mix_norm_odd"], dpool, W["pool_w"], W["pool_b"], W["pool_scale"], dh3, tm, "pool_mixer_bwd")
    dh1, g0 = _ffn_bwd(h1, W["ffn_norm"][0:1], W["w_up_g"][0], W["w_up_v"][0], W["ffn_conv_w_p"][0],
                       W["ffn_conv_b"][0:1], W["w_down"][0], ffn0, dh2, tm, tmm, "0")
    for key in ("gain", "wug", "wuv", "cw", "cb", "wd"):
        G["ffn_" + key] = (g0[key], g1[key])

    dcat = _mm(dh1, W["w_out"], "nt", F32, tmm, 512, "out_proj_dx")
    G["w_out"] = _mm(cat, dh1, "tn", F32, 512, 512, "out_proj_dw")
    dadg, dcw, G["conv_b"], G["ln_g"], G["ln_b"] = _conf_bwd(
        proj, u1, dcat, W["conv_w_p"], W["ln_g"], W["ln_b"], tm, "conformer_bwd")
    G["conv_w"] = dcw[:CONV_K]
    do = _heads(dcat[:, :FOX_W], LP)
    dq, dk, dv, dcq, dck = _attn_bwd(q, k, v, o, do, lse, c_col, c_row, "fox_attention_bwd")
    dc = jnp.pad((dcq.reshape(HEADS, LP) + dck.reshape(HEADS, LP)).T, ((0, 0), (0, LANE - HEADS)))
    dfl, dbf = _fgate_bwd(proj, W["b_f_p"], dc, "forget_gate_bwd")
    G["b_f"] = dbf[:, :HEADS]
    dproj = jnp.concatenate([_unheads(t, LP).astype(BF16) for t in (dq, dk, dv)] + [dadg, dfl], axis=1)
    dn0 = _mm(dproj, W["w_in_p"], "nt", F32, tmm, 512, "in_proj_dx")
    G["w_in_p"] = _mm(n0, dproj, "tn", F32, 512, 384, "in_proj_dw")
    dh0, G["mix_norm_even"] = _rms_bwd(h0, W["mix_norm_even"], dn0, dh1, tm, "mix_norm_even_bwd")
    return loss, dh0, G


def _compute_layout(P):
    w_in = P["w_in"].reshape(D_MODEL, IN_COLS)
    qkv, f, ag = w_in[:, :3 * FOX_W], w_in[:, 3 * FOX_W:3 * FOX_W + HEADS], w_in[:, 3 * FOX_W + HEADS:]
    w_in_p = jnp.concatenate([qkv, ag, f, jnp.zeros((D_MODEL, LANE - HEADS), w_in.dtype)], axis=1).astype(BF16)
    w_up = P["w_up"].astype(BF16)
    return dict(
        mix_norm_even=P["mix_norm_even"].reshape(1, D_MODEL).astype(F32),
        w_in_p=w_in_p,
        b_f_p=jnp.pad(P["b_f"].reshape(1, HEADS).astype(F32), ((0, 0), (0, LANE - HEADS))),
        conv_w_p=jnp.pad(P["conv_w"].reshape(CONV_K, CONV_CH).astype(F32), ((0, CONV_HALO - CONV_K), (0, 0))),
        conv_b=P["conv_b"].reshape(1, CONV_CH).astype(F32),
        ln_g=P["ln_g"].reshape(1, CONV_CH).astype(F32),
        ln_b=P["ln_b"].reshape(1, CONV_CH).astype(F32),
        w_out=P["w_out"].reshape(D_MODEL, D_MODEL).astype(BF16),
        mix_norm_odd=P["mix_norm_odd"].reshape(1, D_MODEL).astype(F32),
        pool_w=P["pool_w"].reshape(len(POOL_WINDOWS), POOL_G, POOL_G).astype(BF16),
        pool_b=P["pool_b"].reshape(1, D_MODEL).astype(F32),
        pool_scale=P["pool_scale"].reshape(1, D_MODEL).astype(F32),
        ffn_norm=P["ffn_norm"].astype(F32),
        w_up_g=w_up[:, :, :D_FF],
        w_up_v=w_up[:, :, D_FF:],
        ffn_conv_w_p=jnp.pad(P["ffn_conv_w"].astype(F32), ((0, 0), (0, 8 - FFN_K), (0, 0))),
        ffn_conv_b=P["ffn_conv_b"].astype(F32),
        w_down=P["w_down"].astype(BF16),
        final_norm=P["final_norm"].reshape(1, D_MODEL).astype(F32),
    )


def _reference_layout(G, dh0):
    gp = G["w_in_p"]
    g_w_in = jnp.concatenate([gp[:, :3 * FOX_W], gp[:, 3 * FOX_W + 2 * CONV_CH:3 * FOX_W + 2 * CONV_CH + HEADS],
                              gp[:, 3 * FOX_W:3 * FOX_W + 2 * CONV_CH]], axis=1)
    return dict(
        meta_tokens=dh0[:N_META],
        mix_norm_even=G["mix_norm_even"],
        w_in=g_w_in[None],
        b_f=G["b_f"],
        conv_w=G["conv_w"][None],
        conv_b=G["conv_b"],
        ln_g=G["ln_g"],
        ln_b=G["ln_b"],
        w_out=G["w_out"][None],
        mix_norm_odd=G["mix_norm_odd"],
        pool_w=G["pool_w"][None],
        pool_b=G["pool_b"].reshape(1, len(POOL_WINDOWS), POOL_G),
        pool_scale=G["pool_scale"],
        ffn_norm=jnp.concatenate(G["ffn_gain"], axis=0),
        w_up=jnp.stack([jnp.concatenate([g, v], axis=1) for g, v in zip(G["ffn_wug"], G["ffn_wuv"])]),
        ffn_conv_w=jnp.stack(G["ffn_cw"]),
        ffn_conv_b=jnp.concatenate(G["ffn_cb"], axis=0),
        w_down=jnp.stack(G["ffn_wd"]),
        final_norm=G["final_norm"].reshape(D_MODEL),
    )


MESH = pl.DeviceIdType.MESH
ANY = pl.BlockSpec(memory_space=pl.ANY)
PACK_COLS = 1024


def _coords():
    return lax.axis_index("x"), lax.axis_index("y"), lax.axis_index("c")


def _other_chips(x, y):
    return [(1 - x, y), (x, 1 - y), (1 - x, 1 - y)]


def _allgather_chips(pack):
    R, C = pack.shape
    R2 = R // 2

    def body(x_ref, o_ref, send_sems, recv_sems, local_sem):
        x, y, c = _coords()
        sibling = (x, y, 1 - c)
        chips = _other_chips(x, y)

        def slot(px, py, half):
            return o_ref.at[2 * px + py, pl.ds(half * R2, R2), :]

        def copy(k, src, dst, to):
            return pltpu.make_async_remote_copy(src_ref=src, dst_ref=dst, send_sem=send_sems.at[k],
                                                recv_sem=recv_sems.at[k], device_id=to, device_id_type=MESH)

        mine = pltpu.make_async_copy(x_ref, o_ref.at[2 * x + y], local_sem)
        mine.start()
        my_half = x_ref.at[pl.ds(c * R2, R2), :]
        first = [copy(j, my_half, slot(x, y, c), (*chip, c)) for j, chip in enumerate(chips)]
        for cp in first:
            cp.start()
        passed = [copy(3 + j, slot(*chip, c), slot(*chip, c), sibling) for j, chip in enumerate(chips)]
        for j, chip in enumerate(chips):
            copy(j, my_half, slot(*chip, c), sibling).wait_recv()
            passed[j].start()
        for j, chip in enumerate(chips):
            copy(3 + j, my_half, slot(*chip, 1 - c), sibling).wait_recv()
        for cp in first + passed:
            cp.wait_send()
        mine.wait()

    return pl.pallas_call(
        body, name="allgather_weights", out_shape=jax.ShapeDtypeStruct((N_CHIPS, R, C), pack.dtype),
        in_specs=[ANY], out_specs=ANY,
        scratch_shapes=[pltpu.SemaphoreType.DMA((6,)), pltpu.SemaphoreType.DMA((6,)), pltpu.SemaphoreType.DMA],
    )(pack)


def _pair_exchange(G):
    n, R, C = G.shape
    R2 = R // 2

    def body(g_ref, o_ref, send_sem, recv_sem):
        x, y, c = _coords()
        src = g_ref.at[pl.ds(0, n), pl.ds((1 - c) * R2, R2), :]
        cp = pltpu.make_async_remote_copy(src_ref=src, dst_ref=o_ref, send_sem=send_sem, recv_sem=recv_sem,
                                          device_id=(x, y, 1 - c), device_id_type=MESH)
        cp.start()
        cp.wait()

    return pl.pallas_call(
        body, name="grad_pair_exchange", out_shape=jax.ShapeDtypeStruct((n, R2, C), G.dtype),
        in_specs=[ANY], out_specs=ANY,
        scratch_shapes=[pltpu.SemaphoreType.DMA, pltpu.SemaphoreType.DMA],
    )(G)


def _row_tile(rows, align, cap):
    best = None
    for t in range(align, min(rows, cap) + 1, align):
        if rows % t == 0:
            best = t
    assert best is not None, (rows, align, cap)
    return best


def _pair_sum(G, recv):
    n, R, C = G.shape
    R2 = R // 2
    tr = _row_tile(R2, 16, 704)
    nrb = R2 // tr
    half = lax.axis_index("c").astype(jnp.int32).reshape(1)

    def body(c_ref, g_ref, r_ref, o_ref):
        o_ref[...] = (g_ref[...] + r_ref[...]).astype(o_ref.dtype)

    return pl.pallas_call(
        body, name="grad_pair_sum", out_shape=jax.ShapeDtypeStruct((n, R2, C), BF16),
        grid_spec=pltpu.PrefetchScalarGridSpec(
            num_scalar_prefetch=1, grid=(n, nrb),
            in_specs=[pl.BlockSpec((None, tr, C), lambda j, i, c_ref: (j, c_ref[0] * nrb + i, 0)),
                      pl.BlockSpec((None, tr, C), lambda j, i, c_ref: (j, i, 0))],
            out_specs=pl.BlockSpec((None, tr, C), lambda j, i, c_ref: (j, i, 0))),
        compiler_params=_params(("parallel", "parallel")))(half, G, recv)


def _chip_exchange(P):
    n, R2, C = P.shape

    def body(p_ref, o_ref, send_sems, recv_sems, local_sem):
        x, y, c = _coords()
        me = 2 * x + y
        chips = _other_chips(x, y)
        mine = pltpu.make_async_copy(p_ref.at[me], o_ref.at[me], local_sem)
        mine.start()
        sends = [pltpu.make_async_remote_copy(
            src_ref=p_ref.at[2 * px + py], dst_ref=o_ref.at[me], send_sem=send_sems.at[k],
            recv_sem=recv_sems.at[k], device_id=(px, py, c), device_id_type=MESH)
            for k, (px, py) in enumerate(chips)]
        for cp in sends:
            cp.start()
        for k, (px, py) in enumerate(chips):
            pltpu.make_async_remote_copy(
                src_ref=p_ref.at[me], dst_ref=o_ref.at[2 * px + py], send_sem=send_sems.at[k],
                recv_sem=recv_sems.at[k], device_id=(px, py, c), device_id_type=MESH).wait_recv()
        for cp in sends:
            cp.wait_send()
        mine.wait()

    return pl.pallas_call(
        body, name="grad_chip_exchange", out_shape=jax.ShapeDtypeStruct((n, R2, C), P.dtype),
        in_specs=[ANY], out_specs=ANY,
        scratch_shapes=[pltpu.SemaphoreType.DMA((3,)), pltpu.SemaphoreType.DMA((3,)), pltpu.SemaphoreType.DMA],
    )(P)


def _chip_sum(X):
    n, R2, C = X.shape
    tr = _row_tile(R2, 16, 704)

    def body(x_ref, o_ref):
        acc = x_ref[0].astype(F32)
        for s in range(1, n):
            acc = acc + x_ref[s].astype(F32)
        o_ref[...] = acc

    return pl.pallas_call(
        body, name="grad_chip_sum", out_shape=jax.ShapeDtypeStruct((R2, C), F32), grid=(R2 // tr,),
        in_specs=[pl.BlockSpec((n, tr, C), lambda i: (0, i, 0))],
        out_specs=pl.BlockSpec((tr, C), lambda i: (i, 0)),
        compiler_params=_params(("parallel",)))(X)


def _pair_allgather(Q):
    R2, C = Q.shape

    def body(q_ref, o_ref, send_sem, recv_sem, local_sem):
        x, y, c = _coords()
        mine = pltpu.make_async_copy(q_ref, o_ref.at[c], local_sem)
        mine.start()
        cp = pltpu.make_async_remote_copy(src_ref=q_ref, dst_ref=o_ref.at[c], send_sem=send_sem,
                                          recv_sem=recv_sem, device_id=(x, y, 1 - c), device_id_type=MESH)
        cp.start()
        pltpu.make_async_remote_copy(src_ref=q_ref, dst_ref=o_ref.at[1 - c], send_sem=send_sem,
                                     recv_sem=recv_sem, device_id=(x, y, 1 - c), device_id_type=MESH).wait_recv()
        cp.wait_send()
        mine.wait()

    return pl.pallas_call(
        body, name="grad_pair_allgather", out_shape=jax.ShapeDtypeStruct((2, R2, C), Q.dtype),
        in_specs=[ANY], out_specs=ANY,
        scratch_shapes=[pltpu.SemaphoreType.DMA, pltpu.SemaphoreType.DMA, pltpu.SemaphoreType.DMA],
    )(Q)


def _allreduce_small(pack):
    Rs, C = pack.shape
    n_dev = 8

    def body(x_ref, o_ref, buf, send_sems, recv_sems):
        x, y, c = _coords()
        me = 4 * x + 2 * y + c
        buf[me] = x_ref[...]
        peers = []
        for rel in range(1, n_dev):
            px = 1 - x if rel & 4 else x
            py = 1 - y if rel & 2 else y
            pc = 1 - c if rel & 1 else c
            peers.append((px, py, pc))
        sends = [pltpu.make_async_remote_copy(
            src_ref=x_ref, dst_ref=buf.at[me], send_sem=send_sems.at[k], recv_sem=recv_sems.at[k],
            device_id=peer, device_id_type=MESH) for k, peer in enumerate(peers)]
        for cp in sends:
            cp.start()
        for k, (px, py, pc) in enumerate(peers):
            pltpu.make_async_remote_copy(
                src_ref=x_ref, dst_ref=buf.at[4 * px + 2 * py + pc], send_sem=send_sems.at[k],
                recv_sem=recv_sems.at[k], device_id=(px, py, pc), device_id_type=MESH).wait_recv()
        for cp in sends:
            cp.wait_send()
        acc = buf[0]
        for d in range(1, n_dev):
            acc = acc + buf[d]
        o_ref[...] = acc

    vm = pl.BlockSpec(memory_space=pltpu.VMEM)
    return pl.pallas_call(
        body, name="allreduce_replicated", out_shape=jax.ShapeDtypeStruct((Rs, C), F32),
        in_specs=[vm], out_specs=vm,
        scratch_shapes=[pltpu.VMEM((n_dev, Rs, C), F32), pltpu.SemaphoreType.DMA((n_dev - 1,)),
                        pltpu.SemaphoreType.DMA((n_dev - 1,))],
    )(pack)


SHARDED = (
    ("w_in", 2, True), ("w_out", 1, True), ("pool_w", 2, True), ("w_up", 2, True), ("w_down", 1, True),
    ("meta_tokens", 1, False), ("mix_norm_odd", 1, False), ("pool_b", 2, False), ("pool_scale", 1, False),
    ("conv_w", 2, False), ("ffn_conv_w", 2, False))
REPLICATED = ("mix_norm_even", "b_f", "conv_b", "ln_g", "ln_b", "ffn_norm", "ffn_conv_b", "final_norm")
PACK_ROW_ALIGN = 32


def _pad_rows(flat, align_rows, cols):
    rows = -(-flat.shape[-1] // cols)
    rows = -(-rows // align_rows) * align_rows
    pad = rows * cols - flat.shape[-1]
    flat = jnp.pad(flat, [(0, 0)] * (flat.ndim - 1) + [(0, pad)])
    return flat.reshape(flat.shape[:-1] + (rows, cols))


def _pack_weight_shards(shards):
    parts = []
    for name, _, as_bf16 in SHARDED:
        w = shards[name].astype(F32).reshape(-1)
        parts.append(w.astype(BF16) if as_bf16 else lax.bitcast_convert_type(w, BF16).reshape(-1))
    return _pad_rows(jnp.concatenate(parts), PACK_ROW_ALIGN, PACK_COLS)


def _unpack_weights(gathered, shards):
    flat = gathered.reshape(N_CHIPS, -1)
    out, off = {}, 0
    for name, axis, as_bf16 in SHARDED:
        shp = shards[name].shape
        n = int(np.prod(shp))
        if as_bf16:
            t = flat[:, off:off + n]
            off += n
        else:
            t = lax.bitcast_convert_type(flat[:, off:off + 2 * n].reshape(N_CHIPS, n, 2), F32)
            off += 2 * n
        t = t.reshape((N_CHIPS,) + shp)
        out[name] = jnp.concatenate([t[j] for j in range(N_CHIPS)], axis=axis)
    return out


def _pack_grad_shards(grads, shards):
    parts = []
    for name, axis, _ in SHARDED:
        g = grads[name].reshape(shards[name].shape[:axis] + (N_CHIPS, shards[name].shape[axis])
                                + shards[name].shape[axis + 1:])
        parts.append(jnp.moveaxis(g, axis, 0).reshape(N_CHIPS, -1))
    return _pad_rows(jnp.concatenate(parts, axis=1), PACK_ROW_ALIGN, PACK_COLS)


def _unpack_grad_shard(reduced, shards):
    flat = reduced.reshape(-1)
    out, off = {}, 0
    for name, _, _ in SHARDED:
        shp = shards[name].shape
        n = int(np.prod(shp))
        out[name] = flat[off:off + n].reshape(shp)
        off += n
    return out


def _pack_replicated(grads, loss):
    parts = [_pad_rows(grads[name].astype(F32).reshape(-1), 1, LANE).reshape(-1) for name in REPLICATED]
    parts.append(_pad_rows(loss.reshape(-1)[:1], 1, LANE).reshape(-1))
    return _pad_rows(jnp.concatenate(parts), 8, LANE)


def _unpack_replicated(reduced, shapes):
    flat = reduced.reshape(-1)
    out, off = {}, 0
    for name in REPLICATED:
        n = int(np.prod(shapes[name]))
        out[name] = flat[off:off + n].reshape(shapes[name])
        off += -(-n // LANE) * LANE
    return out, flat[off]


def _ffn_fwd2(h, W, layer, tm, tmm, host_up=None, host_act=None):
    tag = str(layer)
    n = _rms_fwd(h, W["ffn_norm"][layer:layer + 1], BF16, tm, f"ffn_norm_{tag}")
    up, *g_up = _mm(n, W["w_up"][layer], "nn", F32, tmm, UP_SHARD, f"ffn_up_{tag}", host=host_up) \
        if host_up else (_mm(n, W["w_up"][layer], "nn", F32, tmm, UP_SHARD, f"ffn_up_{tag}"),)
    act, *g_act = _ffn_act_fwd(up, W["ffn_conv_w_p"][layer], W["ffn_conv_b"][layer:layer + 1], tm,
                               f"ffn_act_{tag}", host=host_act) \
        if host_act else (_ffn_act_fwd(up, W["ffn_conv_w_p"][layer], W["ffn_conv_b"][layer:layer + 1], tm,
                                       f"ffn_act_{tag}"),)
    out = _mm(act, W["w_down"][layer], "nn", F32, tm, D_MODEL, f"ffn_down_{tag}", add=h)
    return out, (n, up, act), g_up + g_act


def _ffn_bwd2(h, W, layer, saved, dout, tm, tmm, comm=None):
    tag = str(layer)
    n, up, act = saved
    dact = _mm(dout, W["w_down"][layer], "nt", F32, tmm, UP_SHARD, f"ffn_dact_{tag}")
    dwd = _mm(act, dout, "tn", F32, D_FF // 2, 512, f"ffn_dwdown_{tag}")
    dup, dcw, dcb, *others = _ffn_act_bwd(up, dact, W["ffn_conv_w_p"][layer], W["ffn_conv_b"][layer:layer + 1],
                                          tm, f"ffn_act_bwd_{tag}", comm=comm)
    dn = _mm_ffn_dn(dup, W["w_up"][layer], tm, D_MODEL, f"ffn_dn_{tag}")
    dwu = _mm_ffn_dwup(n, dup, 512, D_FF // 2, f"ffn_dwup_{tag}")
    dh, dgain = _rms_bwd(h, W["ffn_norm"][layer:layer + 1], dn, dout, tm, f"ffn_norm_bwd_{tag}")
    return dh, (dwu, dwd), dict(gain=dgain, cw=dcw[:FFN_K], cb=dcb), others


GATHER_FIRST = ("w_in", "small")
GATHER_LATE = ("pool_w", "w_up", "w_down")
HOSTED_FFN = ("w_up1", "w_down1")
HOSTED = ("w_out", "pool_w", "w_up0", "w_down0")
LATE = ("w_in", "small")


def _local_step2(h0, tgt, W, n_real, cut_of):
    LP = h0.shape[0]
    tm, tmm = _row_tiles(LP)
    nb = LP // ATT_BLK
    G = {}
    n0 = _rms_fwd(h0, W["mix_norm_even"], BF16, tm, "mix_norm_even")
    sh = W["late_shards"]
    stage = lambda *names: ([sh[n] for n in names], [cut_of[n] for n in names])
    proj, g_down0 = _mm(n0, W["w_in_p"], "nn", F32, tmm, 896, "in_proj", host=stage("w_down0"))
    c = _fgate_fwd(proj, W["b_f_p"], "forget_gate")
    qT, kT, k_aug, vT = _attn_prep(proj, c, "attention_operands")
    oT, lse, g_pool, g_up0, g_out = _attn_fwd2(qT, k_aug, vT, "fox_attention",
                                               comm=stage("pool_w", "w_up0", "w_out"))
    g_down0, g_pool, g_up0, g_out = _gather_forward(
        [g_down0, g_pool, g_up0, g_out], stage("w_down0", "pool_w", "w_up0", "w_out")[1], "gather_forward_0")
    W = dict(W)
    W.update(pool_w=g_pool, w_up=[g_up0, None], w_down=[g_down0, None], w_out=g_out)
    u1, u = _conf_fwd(proj, W["conv_w_p"], W["conv_b"], W["ln_g"], W["ln_b"], tm, "conformer")
    cat = jnp.concatenate([_attn_rows(oT, 1.0, BF16, "attention_rows"), u], axis=1)
    h1 = _mm(cat, W["w_out"], "nn", F32, tmm, D_MODEL, "out_proj", add=h0)
    h2, ffn0, (g_down1, g_up1) = _ffn_fwd2(h1, W, 0, tm, tmm, host_up=stage("w_down1"), host_act=stage("w_up1"))
    g_down1, g_up1 = _gather_forward([g_down1, g_up1], stage("w_down1", "w_up1")[1], "gather_forward_1")
    W.update(w_up=[g_up0, g_up1], w_down=[g_down0, g_down1])
    h3, dpool = _pool_fwd(h2, W["mix_norm_odd"], W["pool_w"], W["pool_b"], W["pool_scale"], tm, "pool_mixer")
    h4, ffn1, _ = _ffn_fwd2(h3, W, 1, tm, tmm)
    loss, dh4, G["final_norm"] = _loss_head(h4, W["final_norm"], tgt, n_real, tm, "loss_head")

    def pair_sums(names, tag):
        cs = [cut_of[n] for n in names]
        fulls = [G[n] for n in names]
        recv = _pair_exchange2(fulls, cs, "grad_pair_exchange_" + tag)
        return [_pair_sum2(f, r, cut, PAIR_SUM_BLOCKS[n], "grad_pair_sum_" + n)
                for f, r, cut, n in zip(fulls, recv, cs, names)], cs

    dh3, (G["w_up1"], G["w_down1"]), g1, _ = _ffn_bwd2(h3, W, 1, ffn1, dh4, tm, tmm)
    dh2, G["pool_w"], G["pool_b"], G["pool_scale"], G["mix_norm_odd"] = _pool_bwd(
        h2, W["mix_norm_odd"], dpool, W["pool_w"], W["pool_b"], W["pool_scale"], dh3, tm, "pool_mixer_bwd")
    parts1, cuts1 = pair_sums(HOSTED_FFN, "ffn1")
    dh1, (G["w_up0"], G["w_down0"]), g0, others1 = _ffn_bwd2(h1, W, 0, ffn0, dh2, tm, tmm, comm=(parts1, cuts1))
    G["ffn_norm"] = jnp.concatenate([g0["gain"], g1["gain"]], axis=0)
    G["ffn_conv_w"] = jnp.stack([g0["cw"], g1["cw"]])
    G["ffn_conv_b"] = jnp.concatenate([g0["cb"], g1["cb"]], axis=0)

    dcat = _mm(dh1, W["w_out"], "nt", F32, tmm, D_MODEL, "out_proj_dx")
    G["w_out"] = _mm(cat, dh1, "tn", F32, 512, D_MODEL, "out_proj_dw")
    dadg, dcw, G["conv_b"], G["ln_g"], G["ln_b"] = _conf_bwd(
        proj, u1, dcat, W["conv_w_p"], W["ln_g"], W["ln_b"], tm, "conformer_bwd")
    G["conv_w"] = dcw[:CONV_K]
    doT = _attn_cols(dcat, "attention_do_cols")
    hparts, hcuts = pair_sums(HOSTED, "early")
    dqT, dkT, dvT, *hothers = _attn_bwd2(qT, kT, k_aug, vT, oT, doT, lse, "fox_attention_bwd",
                                         comm=(hparts, hcuts))
    dfl, dbf = _fgate_bwd(proj, W["b_f_p"], _attn_dc(dqT, dkT, "attention_dc"), "forget_gate_bwd")
    G["b_f"] = dbf[:, :HEADS]
    dproj = jnp.concatenate([_attn_rows(dqT, HEAD_DIM ** -0.5, BF16, "attention_dq_rows"),
                             _attn_rows(dkT, 1.0, BF16, "attention_dk_rows"),
                             _attn_rows(dvT, 1.0, BF16, "attention_dv_rows"), dadg, dfl], axis=1)
    dn0 = _mm(dproj, W["w_in_p"], "nt", F32, tmm, D_MODEL, "in_proj_dx")
    G["w_in_p"] = _mm(n0, dproj, "tn", F32, 512, 896, "in_proj_dw")
    dh0, G["mix_norm_even"] = _rms_bwd(h0, W["mix_norm_even"], dn0, dh1, tm, "mix_norm_even_bwd")
    parts = dict(zip(HOSTED_FFN + HOSTED, parts1 + hparts))
    others = dict(zip(HOSTED_FFN + HOSTED, list(others1) + list(hothers)))
    return loss, dh0, G, parts, others


class _Cut:
    def __init__(self, full_shape, chip_dim, half_dim):
        self.full = tuple(full_shape)
        self.chip_dim, self.half_dim = chip_dim, half_dim
        self.chip_size = full_shape[chip_dim] // N_CHIPS
        self.half_size = full_shape[half_dim] // 2
        assert chip_dim != half_dim

    def shape(self, chip=False, half=False):
        s = list(self.full)
        if chip:
            s[self.chip_dim] = self.chip_size
        if half:
            s[self.half_dim] = self.half_size
        return tuple(s)

    def region(self, ref, chip=None, half=None):
        idx = [pl.ds(0, n) for n in ref.shape]
        if chip is not None:
            idx[self.chip_dim] = pl.ds(chip * self.chip_size, self.chip_size)
        if half is not None:
            idx[self.half_dim] = pl.ds(half * self.half_size, self.half_size)
        return ref.at[tuple(idx)]


SMALL_SHARDED = ("meta_tokens", "mix_norm_odd", "pool_b", "pool_scale", "conv_w", "ffn_conv_w")
SMALL_ROWS = 144


def _cuts():
    return {
        "w_in": _Cut((N_CHIPS, D_MODEL, IN_SHARD), 0, 1),
        "w_out": _Cut((D_MODEL, D_MODEL), 0, 1),
        "pool_w": _Cut((len(POOL_WINDOWS), POOL_G, POOL_G), 1, 0),
        "w_up": _Cut((2, D_MODEL, 2 * D_FF), 2, 1),
        "w_down": _Cut((2, D_FF, D_MODEL), 1, 2),
        "small": _Cut((N_CHIPS, SMALL_ROWS, LANE), 0, 1),
        "w_up0": _Cut((D_MODEL, 2 * D_FF), 1, 0), "w_up1": _Cut((D_MODEL, 2 * D_FF), 1, 0),
        "w_down0": _Cut((D_FF, D_MODEL), 0, 1), "w_down1": _Cut((D_FF, D_MODEL), 0, 1),
    }


COMM_ORDER = ("w_in", "w_out", "pool_w", "w_up", "w_down", "small")


def _remote(src, dst, send_sems, recv_sems, k, to):
    return pltpu.make_async_remote_copy(src_ref=src, dst_ref=dst, send_sem=send_sems.at[k],
                                        recv_sem=recv_sems.at[k], device_id=to, device_id_type=MESH)


def _gather_weights(shards, cuts):
    n = len(shards)

    def body(*refs):
        srcs, outs = refs[:n], refs[n:2 * n]
        send_sems, recv_sems = refs[2 * n:]
        x, y, c = _coords()
        me = 2 * x + y
        sibling = (x, y, 1 - c)
        chips = _other_chips(x, y)
        sends = []
        for t, cut in enumerate(cuts):
            push = _remote(srcs[t], cut.region(outs[t], chip=me), send_sems, recv_sems, 7 * t, sibling)
            push.start()
            sends.append(push)
            for kk, chip in enumerate(chips):
                cp = _remote(cut.region(srcs[t], half=c), cut.region(outs[t], chip=me, half=c),
                             send_sems, recv_sems, 7 * t + 1 + kk, (*chip, c))
                cp.start()
                sends.append(cp)
        for t, cut in enumerate(cuts):
            for kk, (px, py) in enumerate(chips):
                landed = cut.region(outs[t], chip=2 * px + py, half=c)
                _remote(landed, landed, send_sems, recv_sems, 7 * t + 1 + kk, sibling).wait_recv()
                fwd = _remote(landed, landed, send_sems, recv_sems, 7 * t + 4 + kk, sibling)
                fwd.start()
                sends.append(fwd)
        for t, cut in enumerate(cuts):
            mine = cut.region(outs[t], chip=me)
            _remote(mine, mine, send_sems, recv_sems, 7 * t, sibling).wait_recv()
            for kk, (px, py) in enumerate(chips):
                other = cut.region(outs[t], chip=2 * px + py, half=1 - c)
                _remote(other, other, send_sems, recv_sems, 7 * t + 4 + kk, sibling).wait_recv()
        for cp in sends:
            cp.wait_send()

    return pl.pallas_call(
        body, name="gather_weights",
        out_shape=tuple(jax.ShapeDtypeStruct(cut.full, s.dtype) for cut, s in zip(cuts, shards)),
        in_specs=[ANY] * n, out_specs=tuple([ANY] * n),
        scratch_shapes=[pltpu.SemaphoreType.DMA((7 * n,)), pltpu.SemaphoreType.DMA((7 * n,))],
    )(*shards)


def _gather_first_ops(srcs, outs, send_sems, recv_sems, cuts):
    x, y, c = _coords()
    me = 2 * x + y
    sibling = (x, y, 1 - c)
    chips = _other_chips(x, y)

    def copies():
        out = []
        for t, cut in enumerate(cuts):
            out.append(_remote(srcs[t], cut.region(outs[t], chip=me), send_sems, recv_sems, 4 * t, sibling))
            for kk, chip in enumerate(chips):
                out.append(_remote(cut.region(srcs[t], half=c), cut.region(outs[t], chip=me, half=c),
                                   send_sems, recv_sems, 4 * t + 1 + kk, (*chip, c)))
        return out

    def start():
        for cp in copies():
            cp.start()

    def wait():
        for t, cut in enumerate(cuts):
            mine = cut.region(outs[t], chip=me)
            _remote(mine, mine, send_sems, recv_sems, 4 * t, sibling).wait_recv()
            for kk, (px, py) in enumerate(chips):
                landed = cut.region(outs[t], chip=2 * px + py, half=c)
                _remote(landed, landed, send_sems, recv_sems, 4 * t + 1 + kk, sibling).wait_recv()
        for cp in copies():
            cp.wait_send()

    return start, wait


def _gather_forward(fulls, cuts, name):
    n = len(fulls)

    def body(*refs):
        outs = refs[n:2 * n]
        send_sems, recv_sems = refs[2 * n:]
        x, y, c = _coords()
        sibling = (x, y, 1 - c)
        chips = _other_chips(x, y)
        sends = []
        for t, cut in enumerate(cuts):
            for kk, (px, py) in enumerate(chips):
                landed = cut.region(outs[t], chip=2 * px + py, half=c)
                cp = _remote(landed, landed, send_sems, recv_sems, 3 * t + kk, sibling)
                cp.start()
                sends.append(cp)
        for t, cut in enumerate(cuts):
            for kk, (px, py) in enumerate(chips):
                other = cut.region(outs[t], chip=2 * px + py, half=1 - c)
                _remote(other, other, send_sems, recv_sems, 3 * t + kk, sibling).wait_recv()
        for cp in sends:
            cp.wait_send()

    return pl.pallas_call(
        body, name=name,
        out_shape=tuple(jax.ShapeDtypeStruct(f.shape, f.dtype) for f in fulls),
        in_specs=[ANY] * n, out_specs=tuple([ANY] * n), input_output_aliases={t: t for t in range(n)},
        scratch_shapes=[pltpu.SemaphoreType.DMA((3 * n,)), pltpu.SemaphoreType.DMA((3 * n,))],
    )(*fulls)


def _pair_exchange2(fulls, cuts, name):
    n = len(fulls)

    def body(*refs):
        srcs, outs = refs[:n], refs[n:2 * n]
        send_sems, recv_sems = refs[2 * n:]
        x, y, c = _coords()
        cps = [_remote(cut.region(srcs[t], half=1 - c), outs[t], send_sems, recv_sems, t, (x, y, 1 - c))
               for t, cut in enumerate(cuts)]
        for cp in cps:
            cp.start()
        for cp in cps:
            cp.wait()

    return pl.pallas_call(
        body, name=name,
        out_shape=tuple(jax.ShapeDtypeStruct(cut.shape(half=True), f.dtype) for cut, f in zip(cuts, fulls)),
        in_specs=[ANY] * n, out_specs=tuple([ANY] * n),
        scratch_shapes=[pltpu.SemaphoreType.DMA((n,)), pltpu.SemaphoreType.DMA((n,))],
    )(*fulls)


def _grid_of(shape, blk):
    assert all(s % b == 0 for s, b in zip(shape, blk)), (shape, blk)
    return tuple(s // b for s, b in zip(shape, blk))


def _pair_sum2(full, recv, cut, blk, name):
    hshape = cut.shape(half=True)
    grid = _grid_of(hshape, blk)
    hb = cut.half_size // blk[cut.half_dim]
    hd = cut.half_dim
    pos = jnp.stack([lax.axis_index("c")]).astype(jnp.int32)

    def full_idx(*a):
        ids, p = list(a[:-1]), a[-1]
        ids[hd] = ids[hd] + p[0] * hb
        return tuple(ids)

    def body(p_ref, f_ref, r_ref, o_ref):
        o_ref[...] = (f_ref[...] + r_ref[...]).astype(o_ref.dtype)

    return pl.pallas_call(
        body, name=name, out_shape=jax.ShapeDtypeStruct(hshape, BF16),
        grid_spec=pltpu.PrefetchScalarGridSpec(
            num_scalar_prefetch=1, grid=grid,
            in_specs=[pl.BlockSpec(blk, full_idx), pl.BlockSpec(blk, lambda *a: tuple(a[:-1]))],
            out_specs=pl.BlockSpec(blk, lambda *a: tuple(a[:-1]))),
        compiler_params=_params(("parallel",) * len(grid)))(pos, full, recv)


def _chip_exchange_ops(srcs, outs, send_sems, recv_sems, cuts):
    x, y, c = _coords()
    me = 2 * x + y
    chips = _other_chips(x, y)

    def copies():
        return [_remote(cut.region(srcs[t], chip=2 * px + py), outs[t].at[me], send_sems, recv_sems,
                        3 * t + kk, (px, py, c))
                for t, cut in enumerate(cuts) for kk, (px, py) in enumerate(chips)]

    def start():
        for cp in copies():
            cp.start()

    def wait():
        for t, cut in enumerate(cuts):
            for kk, (px, py) in enumerate(chips):
                slot = outs[t].at[2 * px + py]
                _remote(slot, slot, send_sems, recv_sems, 3 * t + kk, (px, py, c)).wait_recv()
        for cp in copies():
            cp.wait_send()

    return start, wait


def _chip_exchange_shapes(parts, cuts):
    return tuple(jax.ShapeDtypeStruct((N_CHIPS,) + cut.shape(chip=True, half=True), p.dtype)
                 for cut, p in zip(cuts, parts))


def _chip_exchange2(parts, cuts):
    n = len(parts)

    def body(*refs):
        start, wait = _chip_exchange_ops(refs[:n], refs[n:2 * n], refs[2 * n], refs[2 * n + 1], cuts)
        start()
        wait()

    return pl.pallas_call(
        body, name="grad_chip_exchange",
        out_shape=tuple(jax.ShapeDtypeStruct((N_CHIPS,) + cut.shape(chip=True, half=True), p.dtype)
                        for cut, p in zip(cuts, parts)),
        in_specs=[ANY] * n, out_specs=tuple([ANY] * n),
        scratch_shapes=[pltpu.SemaphoreType.DMA((3 * n,)), pltpu.SemaphoreType.DMA((3 * n,))],
    )(*parts)


def _chip_sum2(part, recv, cut, blk, name, stacked=None):
    bshape = cut.shape(chip=True, half=True)
    grid = _grid_of(bshape, blk)
    cb = cut.chip_size // blk[cut.chip_dim]
    hb = cut.half_size // blk[cut.half_dim]
    cd, hd = cut.chip_dim, cut.half_dim
    x, y, c = _coords()
    slots = [2 * px + py for px, py in _other_chips(x, y)]
    pos = jnp.stack([c, 2 * x + y] + slots).astype(jnp.int32)

    def part_idx(*a):
        ids, p = list(a[:-1]), a[-1]
        ids[cd] = ids[cd] + p[1] * cb
        return tuple(ids)

    def recv_idx(kk):
        return lambda *a: (a[-1][2 + kk],) + tuple(a[:-1])

    def out_idx(*a):
        ids, p = list(a[:-1]), a[-1]
        ids[hd] = ids[hd] + p[0] * hb
        return tuple(ids)

    def body(p_ref, own_ref, r0_ref, r1_ref, r2_ref, *rest):
        acc = own_ref[...].astype(F32)
        for r_ref in (r0_ref, r1_ref, r2_ref):
            acc = acc + r_ref[...].astype(F32)
        rest[-1][...] = acc

    in_specs = [pl.BlockSpec(blk, part_idx)] + [pl.BlockSpec((None,) + blk, recv_idx(kk)) for kk in range(3)]
    args = [pos, part, recv, recv, recv]
    aliases = {}
    if stacked is None:
        out_shape = jax.ShapeDtypeStruct(cut.shape(chip=True), F32)
        out_spec = pl.BlockSpec(blk, out_idx)
    else:
        lead, n_lead, into = stacked
        out_shape = jax.ShapeDtypeStruct((n_lead,) + cut.shape(chip=True), F32)
        out_spec = pl.BlockSpec((None,) + blk, lambda *a: (lead,) + out_idx(*a))
        if into is not None:
            in_specs.append(pl.BlockSpec(memory_space=pl.ANY))
            args.append(into)
            aliases = {5: 0}
    return pl.pallas_call(
        body, name=name, out_shape=out_shape,
        grid_spec=pltpu.PrefetchScalarGridSpec(num_scalar_prefetch=1, grid=grid, in_specs=in_specs,
                                               out_specs=out_spec),
        input_output_aliases=aliases, compiler_params=_params(("parallel",) * len(grid)))(*args)


def _pair_swap2(blocks, cuts):
    n = len(blocks)

    def body(*refs):
        outs = refs[n:2 * n]
        send_sems, recv_sems = refs[2 * n:]
        x, y, c = _coords()
        cps = []
        for t, cut in enumerate(cuts):
            mine = cut.region(outs[t], half=c)
            cp = _remote(mine, mine, send_sems, recv_sems, t, (x, y, 1 - c))
            cp.start()
            cps.append(cp)
        for t, cut in enumerate(cuts):
            theirs = cut.region(outs[t], half=1 - c)
            _remote(theirs, theirs, send_sems, recv_sems, t, (x, y, 1 - c)).wait_recv()
        for cp in cps:
            cp.wait_send()

    return pl.pallas_call(
        body, name="grad_pair_swap",
        out_shape=tuple(jax.ShapeDtypeStruct(b.shape, b.dtype) for b in blocks),
        in_specs=[ANY] * n, out_specs=tuple([ANY] * n),
        input_output_aliases={t: t for t in range(n)},
        scratch_shapes=[pltpu.SemaphoreType.DMA((n,)), pltpu.SemaphoreType.DMA((n,))],
    )(*blocks)


PAIR_SUM_BLOCKS = {"w_in": (1, 512, IN_SHARD), "w_out": (512, 512), "pool_w": (1, POOL_G, POOL_G),
                   "w_up0": (64, 2 * D_FF), "w_up1": (64, 2 * D_FF), "w_down0": (704, 512), "w_down1": (704, 512),
                   "small": (N_CHIPS, SMALL_ROWS // 2, LANE)}
CHIP_SUM_BLOCKS = {"w_in": (1, 512, IN_SHARD), "w_out": (256, 512), "pool_w": (2, 64, POOL_G),
                   "w_up0": (128, UP_SHARD), "w_up1": (128, UP_SHARD),
                   "w_down0": (DOWN_SHARD, 512), "w_down1": (DOWN_SHARD, 512),
                   "small": (1, SMALL_ROWS // 2, LANE)}


def _pack_small(P):
    parts = []
    for name in SMALL_SHARDED:
        t = P[name]
        parts.append(t.astype(F32))
    return parts


def _small_rows(t, lead):
    flat = t.reshape(lead + (-1,))
    pad = -flat.shape[-1] % LANE
    return jnp.pad(flat, [(0, 0)] * len(lead) + [(0, pad)]).reshape(lead + (-1, LANE))


def _pack_small_shards(shards):
    rows = jnp.concatenate([_small_rows(shards[n].astype(F32), ()) for n in SMALL_SHARDED], axis=0)
    return jnp.pad(rows, ((0, SMALL_ROWS - rows.shape[0]), (0, 0)))[None]


def _unpack_small(pack, shards, axes):
    out, off = {}, 0
    nchip = pack.shape[0]
    for name in SMALL_SHARDED:
        shp = shards[name].shape
        cnt = int(np.prod(shp))
        rows = -(-cnt // LANE)
        t = pack[:, off:off + rows].reshape(nchip, -1)[:, :cnt].reshape((nchip,) + shp)
        out[name] = jnp.concatenate([t[j] for j in range(nchip)], axis=axes[name])
        off += rows
    return out


def _pack_small_grads(grads, shards, axes):
    parts = []
    for name in SMALL_SHARDED:
        shp, ax = shards[name].shape, axes[name]
        g = grads[name].reshape(shp[:ax] + (N_CHIPS, shp[ax]) + shp[ax + 1:])
        parts.append(_small_rows(jnp.moveaxis(g, ax, 0), (N_CHIPS,)))
    rows = jnp.concatenate(parts, axis=1)
    return jnp.pad(rows, ((0, 0), (0, SMALL_ROWS - rows.shape[1]), (0, 0)))


SMALL_AXES = {"meta_tokens": 1, "mix_norm_odd": 1, "pool_b": 2, "pool_scale": 1, "conv_w": 2, "ffn_conv_w": 2}


WEIGHT_NAMES = ("meta_tokens", "mix_norm_even", "w_in", "b_f", "conv_w", "conv_b", "ln_g", "ln_b", "w_out",
                "mix_norm_odd", "pool_w", "pool_b", "pool_scale", "ffn_norm", "w_up", "ffn_conv_w",
                "ffn_conv_b", "w_down", "final_norm")


def kernel(x, meta_tokens, mix_norm_even, w_in, b_f, conv_w, conv_b, ln_g, ln_b, w_out, mix_norm_odd, pool_w, pool_b, pool_scale, ffn_norm, w_up, ffn_conv_w, ffn_conv_b, w_down, final_norm, loss_target, m_meta_tokens, m_mix_norm_even, m_w_in, m_b_f, m_conv_w, m_conv_b, m_ln_g, m_ln_b, m_w_out, m_mix_norm_odd, m_pool_w, m_pool_b, m_pool_scale, m_ffn_norm, m_w_up, m_ffn_conv_w, m_ffn_conv_b, m_w_down, m_final_norm, v_meta_tokens, v_mix_norm_even, v_w_in, v_b_f, v_conv_w, v_conv_b, v_ln_g, v_ln_b, v_w_out, v_mix_norm_odd, v_pool_w, v_pool_b, v_pool_scale, v_ffn_norm, v_w_up, v_ffn_conv_w, v_ffn_conv_b, v_w_down, v_final_norm):
    given = dict(locals())
    w_loc = {n: given[n] for n in WEIGHT_NAMES}
    m_loc = {n: given["m_" + n] for n in WEIGHT_NAMES}
    v_loc = {n: given["v_" + n] for n in WEIGHT_NAMES}
    cut_of = _cuts()
    cuts = [cut_of[n] for n in COMM_ORDER]
    big = ("w_in", "w_out", "pool_w", "w_up", "w_down")
    small_shards = {n: w_loc[n] for n in SMALL_SHARDED}

    shard_of = {n: w_loc[n].astype(BF16).reshape(cut_of[n].shape(chip=True)) for n in big}
    shard_of["small"] = _pack_small_shards(small_shards)
    g_in, g_small = _gather_weights([shard_of[n] for n in GATHER_FIRST], [cut_of[n] for n in GATHER_FIRST])
    g_out = None
    g_pool = g_up = g_down = None
    full = _unpack_small(g_small, small_shards, SMALL_AXES)
    full.update({n: w_loc[n] for n in REPLICATED})
    w_in_full = g_in.transpose(1, 0, 2).reshape(D_MODEL, IN_COLS)
    qkv, f, ag = (w_in_full[:, :3 * FOX_W], w_in_full[:, 3 * FOX_W:3 * FOX_W + HEADS],
                  w_in_full[:, 3 * FOX_W + HEADS:])
    W = dict(
        mix_norm_even=full["mix_norm_even"].reshape(1, D_MODEL),
        w_in_p=jnp.concatenate([qkv, ag, f, jnp.zeros((D_MODEL, LANE - HEADS), BF16)], axis=1),
        b_f_p=jnp.pad(full["b_f"].reshape(1, HEADS), ((0, 0), (0, LANE - HEADS))),
        conv_w_p=jnp.pad(full["conv_w"].reshape(CONV_K, CONV_CH), ((0, CONV_HALO - CONV_K), (0, 0))),
        conv_b=full["conv_b"].reshape(1, CONV_CH), ln_g=full["ln_g"].reshape(1, CONV_CH),
        ln_b=full["ln_b"].reshape(1, CONV_CH), w_out=g_out,
        mix_norm_odd=full["mix_norm_odd"].reshape(1, D_MODEL), pool_w=g_pool,
        pool_b=full["pool_b"].reshape(1, D_MODEL), pool_scale=full["pool_scale"].reshape(1, D_MODEL),
        ffn_norm=full["ffn_norm"], w_up=g_up,
        ffn_conv_w_p=jnp.pad(full["ffn_conv_w"], ((0, 0), (0, 8 - FFN_K), (0, 0))),
        ffn_conv_b=full["ffn_conv_b"], w_down=g_down, final_norm=full["final_norm"].reshape(1, D_MODEL),
        late_shards=dict(pool_w=shard_of["pool_w"], w_out=shard_of["w_out"],
                         w_up0=shard_of["w_up"][0], w_up1=shard_of["w_up"][1],
                         w_down0=shard_of["w_down"][0], w_down1=shard_of["w_down"][1]))

    seq = x.shape[1]
    n_real = N_META + seq
    LP = -(-n_real // ATT_BLK) * ATT_BLK
    tail = jnp.zeros((LP - n_real, D_MODEL), F32)
    h0 = jnp.concatenate([full["meta_tokens"], x[0], tail], axis=0)
    tgt = jnp.concatenate([jnp.zeros((N_META, D_MODEL), F32), loss_target[0], tail], axis=0)
    loss_loc, dh0, G, parts, others = _local_step2(h0, tgt, W, n_real, cut_of)
    grad_x = dh0[N_META:n_real][None]
    G["meta_tokens"] = dh0[:N_META]

    rep_shapes = {n: w_loc[n].shape for n in REPLICATED}
    G["final_norm"] = G["final_norm"].reshape(D_MODEL)
    rep, loss = _unpack_replicated(_allreduce_small(_pack_replicated(G, loss_loc)), rep_shapes)

    gp = G["w_in_p"]
    g_w_in = jnp.concatenate([gp[:, :3 * FOX_W], gp[:, 3 * FOX_W + 2 * CONV_CH:3 * FOX_W + 2 * CONV_CH + HEADS],
                              gp[:, 3 * FOX_W:3 * FOX_W + 2 * CONV_CH]], axis=1)
    lcuts = [cut_of[n] for n in LATE]
    lfull = [g_w_in.reshape(D_MODEL, N_CHIPS, IN_SHARD).transpose(1, 0, 2),
             _pack_small_grads(G, small_shards, SMALL_AXES)]
    lrecv = _pair_exchange2(lfull, lcuts, "grad_pair_exchange_late")
    lparts = [_pair_sum2(f, r, cut, PAIR_SUM_BLOCKS[n], "grad_pair_sum_" + n)
              for f, r, cut, n in zip(lfull, lrecv, lcuts, LATE)]
    parts.update(zip(LATE, lparts))
    others.update(zip(LATE, _chip_exchange2(lparts, lcuts)))
    def chip_sum(n, stacked=None):
        return _chip_sum2(parts[n], others[n], cut_of[n], CHIP_SUM_BLOCKS[n], "grad_chip_sum_" + n, stacked=stacked)

    blocks = []
    for n in COMM_ORDER:
        if n in ("w_up", "w_down"):
            blocks.append(chip_sum(n + "1", stacked=(1, 2, chip_sum(n + "0", stacked=(0, 2, None)))))
        else:
            blocks.append(chip_sum(n))
    blocks = _pair_swap2(blocks, cuts)
    gsh = {n: b.reshape(w_loc[n].shape) for n, b in zip(big, blocks[:5])}
    gsh.update(_unpack_small(blocks[5], small_shards, SMALL_AXES))
    sharded = set(big) | set(SMALL_SHARDED)

    grad_w = {n: (gsh[n] if n in sharded else rep[n]) for n in WEIGHT_NAMES}
    delta, new_m, new_v = {}, {}, {}
    for n in WEIGHT_NAMES:
        delta[n], new_m[n], new_v[n] = _adamw(w_loc[n], grad_w[n], m_loc[n], v_loc[n], "adamw_" + n)
    return (loss, grad_x, *[grad_w[n] for n in WEIGHT_NAMES], *[delta[n] for n in WEIGHT_NAMES],
            *[new_m[n] for n in WEIGHT_NAMES], *[new_v[n] for n in WEIGHT_NAMES])
```

```python
import functools

import numpy as np
import jax
import jax.numpy as jnp
from jax import lax
from jax.experimental import pallas as pl
from jax.experimental.pallas import tpu as pltpu

F32 = jnp.float32
BF16 = jnp.bfloat16

D_MODEL = 1024
N_META = 16
SEQ = 2048
HEADS = 8
HEAD_DIM = 64
FOX_W = HEADS * HEAD_DIM
CONV_CH = 512
CONV_K = 31
D_FF = 2816
POOL_WINDOWS = (2, 4, 8, 16)
POOL_G = 256
RMS_EPS = 1e-6
LN_EPS = 1e-5
IN_COLS = 3 * FOX_W + HEADS + 2 * CONV_CH
IN_COLS_P = 3 * FOX_W + 2 * CONV_CH + 128
F_COL_BLK = (3 * FOX_W + 2 * CONV_CH) // 128
N_CHIPS = 4
IN_SHARD = IN_COLS // N_CHIPS
UP_SHARD = 2 * D_FF // N_CHIPS
DOWN_SHARD = D_FF // N_CHIPS

ADAM_LR = 0.001
ADAM_B1 = 0.9
ADAM_B2 = 0.999
ADAM_EPS = 1e-08
ADAM_WD = 0.01
ADAM_STEP = 10

LANE = 128
ATT_BLK = 128
VMEM_LIMIT = 56 * 1024 * 1024

NEG = -1e30


def _sigmoid(x):
    return 0.5 * jnp.tanh(0.5 * x) + 0.5


def _sigmoid_tail(x):
    return 1.0 / (1.0 + jnp.exp(-x))


def _params(sem=None):
    return pltpu.CompilerParams(dimension_semantics=sem, vmem_limit_bytes=VMEM_LIMIT)


def _sub_rows(tm):
    best = 8
    for s in range(8, 137, 8):
        if tm % s == 0:
            best = s
    return best


def _mm(a, b, mode, out_dtype, tm, tn, name, add=None, a_lead=None, b_lead=None, out=None, host=None):
    a_shape = a.shape if a_lead is None else a.shape[1:]
    b_shape = b.shape if b_lead is None else b.shape[1:]
    if mode == "nn":
        (M, K), (K2, N) = a_shape, b_shape
        dims = (((1,), (0,)), ((), ()))
        a_blk, a_idx = (tm, K), (lambda i, j: (i, 0))
        b_blk, b_idx = (K, tn), (lambda i, j: (0, j))
    elif mode == "nt":
        (M, K), (N, K2) = a_shape, b_shape
        dims = (((1,), (1,)), ((), ()))
        a_blk, a_idx = (tm, K), (lambda i, j: (i, 0))
        b_blk, b_idx = (tn, K), (lambda i, j: (j, 0))
    else:
        (K, M), (K2, N) = a_shape, b_shape
        dims = (((0,), (0,)), ((), ()))
        a_blk, a_idx = (K, tm), (lambda i, j: (0, i))
        b_blk, b_idx = (K, tn), (lambda i, j: (0, j))
    assert K == K2 and M % tm == 0 and N % tn == 0, (name, a.shape, b.shape, tm, tn)
    gm, gn = M // tm, N // tn
    a_bytes = M * K * a.dtype.itemsize
    b_bytes = N * K * b.dtype.itemsize
    m_outer = a_bytes + b_bytes * gm <= b_bytes + a_bytes * gn
    if m_outer:
        grid = (gm, gn)
        wrap = lambda f: f
    else:
        grid = (gn, gm)
        wrap = lambda f: (lambda j, i: f(i, j))

    def lead(blk, idx, at):
        if at is None:
            return pl.BlockSpec(blk, wrap(idx))
        return pl.BlockSpec((None,) + blk, wrap(lambda i, j: (at,) + idx(i, j)))

    o_idx = lambda i, j: (i, j)
    in_specs = [lead(a_blk, a_idx, a_lead), lead(b_blk, b_idx, b_lead)]
    args = [a, b]
    if add is not None:
        in_specs.append(pl.BlockSpec((tm, tn), wrap(o_idx)))
        args.append(add)
    aliases = {}
    if out is None:
        out_shape = jax.ShapeDtypeStruct((M, N), out_dtype)
        out_spec = pl.BlockSpec((tm, tn), wrap(o_idx))
    else:
        o_lead, n_lead, into = out
        out_shape = jax.ShapeDtypeStruct((n_lead, M, N), out_dtype)
        out_spec = lead((tm, tn), o_idx, o_lead)
        if into is not None:
            aliases = {len(args): 0}
            in_specs.append(pl.BlockSpec(memory_space=pl.ANY))
            args.append(into)
    has_add = add is not None
    n_host = 0 if host is None else len(host[0])
    n_in = len(args)
    scratch = []
    if n_host:
        host_shapes, host_sems, host_ops = _host_plan(host)
        in_specs = in_specs + [pl.BlockSpec(memory_space=pl.ANY)] * n_host
        args = args + list(host[0])
        out_shape = (out_shape,) + host_shapes
        out_spec = (out_spec,) + (pl.BlockSpec(memory_space=pl.ANY),) * n_host
        scratch = [pltpu.SemaphoreType.DMA((host_sems,)), pltpu.SemaphoreType.DMA((host_sems,))]

    def body(*refs):
        a_ref, b_ref = refs[0], refs[1]
        o_ref = refs[n_in + n_host]
        if n_host:
            start, wait = host_ops(refs[n_in:n_in + n_host], refs[n_in + n_host + 1:n_in + 2 * n_host + 1],
                                   refs[n_in + 2 * n_host + 1], refs[n_in + 2 * n_host + 2], host[1])
            pl.when(jnp.logical_and(pl.program_id(0) == 0, pl.program_id(1) == 0))(start)
        x = a_ref[...].astype(BF16)
        y = b_ref[...].astype(BF16)
        acc = lax.dot_general(x, y, dims, preferred_element_type=F32)
        if has_add:
            acc = acc + refs[2][...]
        o_ref[...] = acc.astype(o_ref.dtype)
        if n_host:
            pl.when(jnp.logical_and(pl.program_id(0) == grid[0] - 1, pl.program_id(1) == grid[1] - 1))(wait)

    sem = ("arbitrary", "arbitrary") if n_host else ("parallel", "parallel")
    return pl.pallas_call(
        body, name=name, out_shape=out_shape, grid=grid, in_specs=in_specs, out_specs=out_spec,
        scratch_shapes=scratch, input_output_aliases=aliases, compiler_params=_params(sem))(*args)


def _mm_ffn_dn(dup, w_up, tm, tn, name):
    _, LP, F = dup.shape
    Dm = w_up.shape[0]
    nt = (((1,), (1,)), ((), ()))

    def body(a_ref, b_ref, o_ref):
        acc = lax.dot_general(a_ref[0], b_ref[:, 0:F], nt, preferred_element_type=F32)
        acc = acc + lax.dot_general(a_ref[1], b_ref[:, F:2 * F], nt, preferred_element_type=F32)
        o_ref[...] = acc

    return pl.pallas_call(
        body, name=name, out_shape=jax.ShapeDtypeStruct((LP, Dm), F32), grid=(LP // tm, Dm // tn),
        in_specs=[pl.BlockSpec((2, tm, F), lambda i, j: (0, i, 0)),
                  pl.BlockSpec((tn, 2 * F), lambda i, j: (j, 0))],
        out_specs=pl.BlockSpec((tm, tn), lambda i, j: (i, j)),
        compiler_params=_params(("parallel", "parallel")))(dup, w_up)


def _mm_ffn_dwup(n, dup, tk, tn, name):
    LP, Dm = n.shape
    F = dup.shape[2]
    nct = F // tn
    tdims = (((0,), (0,)), ((), ()))

    def body(a_ref, b_ref, o_ref):
        o_ref[...] = lax.dot_general(a_ref[...], b_ref[...], tdims, preferred_element_type=F32)

    return pl.pallas_call(
        body, name=name, out_shape=jax.ShapeDtypeStruct((Dm, 2 * F), F32), grid=(Dm // tk, 2 * nct),
        in_specs=[pl.BlockSpec((LP, tk), lambda i, j: (0, i)),
                  pl.BlockSpec((None, LP, tn), lambda i, j: (j // nct, 0, j % nct))],
        out_specs=pl.BlockSpec((tk, tn), lambda i, j: (i, j)),
        compiler_params=_params(("parallel", "parallel")))(n, dup)


def _rms_fwd(h, g, out_dtype, tm, name):
    LP, Dm = h.shape

    def body(h_ref, g_ref, o_ref):
        x = h_ref[...]
        r = lax.rsqrt(jnp.mean(x * x, axis=1, keepdims=True) + RMS_EPS)
        o_ref[...] = (x * r * g_ref[...]).astype(o_ref.dtype)

    return pl.pallas_call(
        body, name=name, out_shape=jax.ShapeDtypeStruct((LP, Dm), out_dtype), grid=(LP // tm,),
        in_specs=[pl.BlockSpec((tm, Dm), lambda i: (i, 0)), pl.BlockSpec((1, Dm), lambda i: (0, 0))],
        out_specs=pl.BlockSpec((tm, Dm), lambda i: (i, 0)),
        compiler_params=_params(("parallel",)))(h, g)


def _rms_bwd(h, g, dn, dres, tm, name):
    LP, Dm = h.shape

    def body(h_ref, g_ref, dn_ref, dr_ref, dh_ref, dg_ref):
        i = pl.program_id(0)
        x = h_ref[...]
        r = lax.rsqrt(jnp.mean(x * x, axis=1, keepdims=True) + RMS_EPS)
        xhat = x * r
        dy = dn_ref[...]
        dxh = dy * g_ref[...]
        dh = r * (dxh - xhat * jnp.mean(dxh * xhat, axis=1, keepdims=True))
        dh_ref[...] = dr_ref[...] + dh

        @pl.when(i == 0)
        def _():
            dg_ref[...] = jnp.zeros_like(dg_ref)

        dg_ref[...] += jnp.sum(dy * xhat, axis=0, keepdims=True)

    row = pl.BlockSpec((tm, Dm), lambda i: (i, 0))
    vec = pl.BlockSpec((1, Dm), lambda i: (0, 0))
    return pl.pallas_call(
        body, name=name,
        out_shape=(jax.ShapeDtypeStruct((LP, Dm), F32), jax.ShapeDtypeStruct((1, Dm), F32)),
        grid=(LP // tm,), in_specs=[row, vec, row, row], out_specs=(row, vec),
        compiler_params=_params(("arbitrary",)))(h, g, dn, dres)


def _loss_head(h, g, tgt, n_real, tm, name):
    LP, Dm = h.shape

    def body(h_ref, g_ref, t_ref, loss_ref, dh_ref, dg_ref):
        i = pl.program_id(0)
        x = h_ref[...]
        gg = g_ref[...]
        r = lax.rsqrt(jnp.mean(x * x, axis=1, keepdims=True) + RMS_EPS)
        xhat = x * r
        rows = i * tm + lax.broadcasted_iota(jnp.int32, (tm, 1), 0)
        real = jnp.logical_and(rows >= N_META, rows < n_real)
        diff = jnp.where(real, xhat * gg - t_ref[...], 0.0)
        dy = diff * (1.0 / Dm)
        dxh = dy * gg
        dh_ref[...] = r * (dxh - xhat * jnp.mean(dxh * xhat, axis=1, keepdims=True))

        @pl.when(i == 0)
        def _():
            dg_ref[...] = jnp.zeros_like(dg_ref)
            loss_ref[...] = jnp.zeros_like(loss_ref)

        dg_ref[...] += jnp.sum(dy * xhat, axis=0, keepdims=True)
        part = jnp.sum(jnp.sum(diff * diff, axis=1, keepdims=True), axis=0, keepdims=True)
        loss_ref[...] += jnp.broadcast_to(part * (0.5 / Dm), loss_ref.shape)

    row = pl.BlockSpec((tm, Dm), lambda i: (i, 0))
    vec = pl.BlockSpec((1, Dm), lambda i: (0, 0))
    return pl.pallas_call(
        body, name=name,
        out_shape=(jax.ShapeDtypeStruct((1, LANE), F32), jax.ShapeDtypeStruct((LP, Dm), F32),
                   jax.ShapeDtypeStruct((1, Dm), F32)),
        grid=(LP // tm,), in_specs=[row, vec, row],
        out_specs=(pl.BlockSpec((1, LANE), lambda i: (0, 0)), row, vec),
        compiler_params=_params(("arbitrary",)))(h, g, tgt)


def _fgate_fwd(proj, bf_p, name):
    LP = proj.shape[0]
    nb = LP // LANE

    def body(f_ref, b_ref, c_ref, lf_ref):
        x = f_ref[...] + b_ref[...]
        lf_ref[...] = jnp.minimum(x, 0.0) - jnp.log1p(jnp.exp(-jnp.abs(x)))
        ri = lax.broadcasted_iota(jnp.int32, (LANE, LANE), 0)
        ci = lax.broadcasted_iota(jnp.int32, (LANE, LANE), 1)
        tri = jnp.where(ri >= ci, 1.0, 0.0).astype(F32)

        def blk(i, carry):
            rows = pl.ds(pl.multiple_of(i * LANE, LANE), LANE)
            cb = jnp.dot(tri, lf_ref[rows, :], precision=lax.Precision.HIGHEST,
                         preferred_element_type=F32) + carry
            c_ref[rows, :] = cb
            return cb[LANE - 1:LANE, :]

        lax.fori_loop(0, nb, blk, jnp.zeros((1, LANE), F32))

    return pl.pallas_call(
        body, name=name, out_shape=jax.ShapeDtypeStruct((LP, LANE), F32), grid=(1,),
        in_specs=[pl.BlockSpec((LP, LANE), lambda i: (0, F_COL_BLK)),
                  pl.BlockSpec((1, LANE), lambda i: (0, 0))],
        out_specs=pl.BlockSpec((LP, LANE), lambda i: (0, 0)),
        scratch_shapes=[pltpu.VMEM((LP, LANE), F32)],
        compiler_params=_params(("arbitrary",)))(proj, bf_p)


def _fgate_bwd(proj, bf_p, dc, name):
    LP = proj.shape[0]
    nb = LP // LANE

    def body(f_ref, b_ref, dc_ref, dl_ref, db_ref):
        ri = lax.broadcasted_iota(jnp.int32, (LANE, LANE), 0)
        ci = lax.broadcasted_iota(jnp.int32, (LANE, LANE), 1)
        triu = jnp.where(ri <= ci, 1.0, 0.0).astype(F32)
        bb = b_ref[...]

        tail = jnp.zeros((1, LANE), F32)
        dbs = jnp.zeros((1, LANE), F32)
        for i in range(nb - 1, -1, -1):
            rows = slice(i * LANE, (i + 1) * LANE)
            gb = jnp.dot(triu, dc_ref[rows, :], precision=lax.Precision.HIGHEST,
                         preferred_element_type=F32) + tail
            x = f_ref[rows, :] + bb
            dl = gb * _sigmoid_tail(-x)
            dl_ref[rows, :] = dl.astype(dl_ref.dtype)
            tail = gb[0:1, :]
            dbs = dbs + jnp.sum(dl, axis=0, keepdims=True)
        db_ref[...] = dbs

    return pl.pallas_call(
        body, name=name,
        out_shape=(jax.ShapeDtypeStruct((LP, LANE), BF16), jax.ShapeDtypeStruct((1, LANE), F32)),
        grid=(1,),
        in_specs=[pl.BlockSpec((LP, LANE), lambda i: (0, F_COL_BLK)),
                  pl.BlockSpec((1, LANE), lambda i: (0, 0)),
                  pl.BlockSpec((LP, LANE), lambda i: (0, 0))],
        out_specs=(pl.BlockSpec((LP, LANE), lambda i: (0, 0)), pl.BlockSpec((1, LANE), lambda i: (0, 0))),
        compiler_params=_params(("arbitrary",)))(proj, bf_p, dc)


def _attn_fwd(q, k, v, c_col, c_row, name):
    Hh, LP, Dh = q.shape
    nb = LP // ATT_BLK
    scale = Dh ** -0.5
    nt = (((1,), (1,)), ((), ()))

    def body(q_ref, k_ref, v_ref, cc_ref, cr_ref, o_ref, lse_ref):
        i = pl.program_id(1)
        qb = q_ref[...]
        cq = cc_ref[...]
        rows = i * ATT_BLK + lax.broadcasted_iota(jnp.int32, (ATT_BLK, ATT_BLK), 0)
        cols0 = lax.broadcasted_iota(jnp.int32, (ATT_BLK, ATT_BLK), 1)

        def step(j, carry):
            m, l, acc = carry
            ks = pl.ds(pl.multiple_of(j * ATT_BLK, ATT_BLK), ATT_BLK)
            s = lax.dot_general(qb, k_ref[ks, :], nt, preferred_element_type=F32) * scale
            s = s + cq - cr_ref[j]
            s = jnp.where(cols0 + j * ATT_BLK <= rows, s, NEG)
            m_new = jnp.maximum(m, jnp.max(s, axis=1, keepdims=True))
            p = jnp.exp(s - m_new)
            alpha = jnp.exp(m - m_new)
            l = alpha * l + jnp.sum(p, axis=1, keepdims=True)
            acc = alpha * acc + jnp.dot(p.astype(BF16), v_ref[ks, :], preferred_element_type=F32)
            return m_new, l, acc

        init = (jnp.full((ATT_BLK, 1), NEG, F32), jnp.zeros((ATT_BLK, 1), F32),
                jnp.zeros((ATT_BLK, Dh), F32))
        m, l, acc = lax.fori_loop(0, i + 1, step, init)
        o_ref[...] = acc / l
        lse_ref[...] = m + jnp.log(l)

    qspec = pl.BlockSpec((None, ATT_BLK, Dh), lambda h, i: (h, i, 0))
    kspec = pl.BlockSpec((None, LP, Dh), lambda h, i: (h, 0, 0))
    colspec = pl.BlockSpec((None, ATT_BLK, 1), lambda h, i: (h, i, 0))
    rowspec = pl.BlockSpec((None, nb, 1, ATT_BLK), lambda h, i: (h, 0, 0, 0))
    return pl.pallas_call(
        body, name=name,
        out_shape=(jax.ShapeDtypeStruct((Hh, LP, Dh), F32), jax.ShapeDtypeStruct((Hh, LP, 1), F32)),
        grid=(Hh, nb), in_specs=[qspec, kspec, kspec, colspec, rowspec],
        out_specs=(qspec, colspec),
        compiler_params=_params(("parallel", "arbitrary")))(q, k, v, c_col, c_row)


def _attn_bwd(q, k, v, o, do, lse, c_col, c_row, name):
    Hh, LP, Dh = q.shape
    nb = LP // ATT_BLK
    scale = Dh ** -0.5
    nt = (((1,), (1,)), ((), ()))
    tn = (((0,), (0,)), ((), ()))

    def body(q_ref, k_ref, v_ref, o_ref, do_ref, lse_ref, cc_ref, cr_ref,
             dq_ref, dk_ref, dv_ref, dcq_ref, dc_ref, delta_ref):
        j = pl.program_id(1)

        @pl.when(j == 0)
        def _():
            dq_ref[...] = jnp.zeros_like(dq_ref)
            dcq_ref[...] = jnp.zeros_like(dcq_ref)
            dob_all = do_ref[...].astype(BF16).astype(F32)
            delta_ref[...] = jnp.sum(dob_all * o_ref[...], axis=1, keepdims=True)

        kb = k_ref[...]
        vb = v_ref[...]
        ck = cr_ref[j]
        rows0 = lax.broadcasted_iota(jnp.int32, (ATT_BLK, ATT_BLK), 0)
        cols = j * ATT_BLK + lax.broadcasted_iota(jnp.int32, (ATT_BLK, ATT_BLK), 1)

        def step(i, carry):
            dk, dv, dcs = carry
            qs = pl.ds(pl.multiple_of(i * ATT_BLK, ATT_BLK), ATT_BLK)
            qb = q_ref[qs, :]
            dob = do_ref[qs, :].astype(BF16)
            s = lax.dot_general(qb, kb, nt, preferred_element_type=F32) * scale
            s = s + cc_ref[qs, :] - ck
            s = jnp.where(cols <= rows0 + i * ATT_BLK, s, NEG)
            p = jnp.exp(s - lse_ref[qs, :])
            dp = lax.dot_general(dob, vb, nt, preferred_element_type=F32)
            ds = p * (dp - delta_ref[qs, :])
            dsb = ds.astype(BF16)
            dv = dv + lax.dot_general(p.astype(BF16), dob, tn, preferred_element_type=F32)
            dk = dk + lax.dot_general(dsb, qb, tn, preferred_element_type=F32) * scale
            dq_ref[qs, :] += jnp.dot(dsb, kb, preferred_element_type=F32) * scale
            dcq_ref[qs, :] += jnp.sum(ds, axis=1, keepdims=True)
            dcs = dcs - jnp.sum(ds, axis=0, keepdims=True)
            return dk, dv, dcs

        init = (jnp.zeros((ATT_BLK, Dh), F32), jnp.zeros((ATT_BLK, Dh), F32),
                jnp.zeros((1, ATT_BLK), F32))
        dk, dv, dcs = lax.fori_loop(j, nb, step, init)
        dk_ref[...] = dk
        dv_ref[...] = dv
        dc_ref[...] = dcs

    full = pl.BlockSpec((None, LP, Dh), lambda h, j: (h, 0, 0))
    blk = pl.BlockSpec((None, ATT_BLK, Dh), lambda h, j: (h, j, 0))
    col = pl.BlockSpec((None, LP, 1), lambda h, j: (h, 0, 0))
    rowspec = pl.BlockSpec((None, nb, 1, ATT_BLK), lambda h, j: (h, 0, 0, 0))
    return pl.pallas_call(
        body, name=name,
        out_shape=(jax.ShapeDtypeStruct((Hh, LP, Dh), F32), jax.ShapeDtypeStruct((Hh, LP, Dh), F32),
                   jax.ShapeDtypeStruct((Hh, LP, Dh), F32), jax.ShapeDtypeStruct((Hh, LP, 1), F32),
                   jax.ShapeDtypeStruct((Hh, nb, 1, ATT_BLK), F32)),
        grid=(Hh, nb), in_specs=[full, blk, blk, full, full, col, col, rowspec],
        out_specs=(full, blk, blk, col, pl.BlockSpec((None, None, 1, ATT_BLK), lambda h, j: (h, j, 0, 0))),
        scratch_shapes=[pltpu.VMEM((LP, 1), F32)],
        compiler_params=_params(("parallel", "arbitrary")))(q, k, v, o, do, lse, c_col, c_row)


AUG = 128
ONES_IN_K = HEAD_DIM
ONES_IN_Q = HEAD_DIM + 3
ATT_HEADS_PER_STEP = 8
ATT_HEADS_PER_STEP_BWD = 8


def _attn_prep(proj, c, name):
    LP = proj.shape[0]
    nb = LP // ATT_BLK
    tail_rows = AUG - HEAD_DIM

    def body(q_ref, k_ref, v_ref, c_ref, qT_ref, kT_ref, ka_ref, vT_ref):
        qt = (q_ref[...] * (HEAD_DIM ** -0.5)).T
        kt = k_ref[...].T
        vt = v_ref[...].T
        ct = c_ref[...].T
        hi = ct.astype(BF16).astype(F32)
        r1 = ct - hi
        mid = r1.astype(BF16).astype(F32)
        lo = (r1 - mid).astype(BF16).astype(F32)
        row = lax.broadcasted_iota(jnp.int32, (tail_rows, ATT_BLK), 0)
        ones = jnp.where(row < 3, 1.0, 0.0)
        for h in range(HEADS):
            cparts = jnp.where(row == 0, hi[h:h + 1], jnp.where(row == 1, mid[h:h + 1],
                               jnp.where(row == 2, lo[h:h + 1], 0.0)))
            hs = slice(h * HEAD_DIM, (h + 1) * HEAD_DIM)
            q_tail = cparts + pltpu.roll(ones, 3, 0)
            k_tail = ones - pltpu.roll(cparts, 3, 0)
            qT_ref[h] = jnp.concatenate([qt[hs], q_tail], axis=0).astype(BF16)
            kfull = jnp.concatenate([kt[hs], k_tail], axis=0)
            kT_ref[h] = kfull.astype(BF16)
            ka_ref[h] = kfull.T.astype(BF16)
            vT_ref[h] = vt[hs].astype(BF16)

    col = lambda j: pl.BlockSpec((ATT_BLK, FOX_W), lambda i: (i, j))
    blk = lambda r: pl.BlockSpec((HEADS, None, r, ATT_BLK), lambda i: (0, i, 0, 0))
    return pl.pallas_call(
        body, name=name,
        out_shape=(jax.ShapeDtypeStruct((HEADS, nb, AUG, ATT_BLK), BF16),
                   jax.ShapeDtypeStruct((HEADS, nb, AUG, ATT_BLK), BF16),
                   jax.ShapeDtypeStruct((HEADS, LP, AUG), BF16),
                   jax.ShapeDtypeStruct((HEADS, nb, HEAD_DIM, ATT_BLK), BF16)),
        grid=(nb,), in_specs=[col(0), col(1), col(2), pl.BlockSpec((ATT_BLK, LANE), lambda i: (i, 0))],
        out_specs=(blk(AUG), blk(AUG), pl.BlockSpec((HEADS, ATT_BLK, AUG), lambda i: (0, i, 0)), blk(HEAD_DIM)),
        compiler_params=_params(("parallel",)))(proj, proj, proj, c)


def _attn_rows(xT, scale, out_dtype, name):
    Hh, nb, R, _ = xT.shape

    def body(x_ref, o_ref):
        stack = jnp.concatenate([x_ref[h, 0:HEAD_DIM, :] for h in range(Hh)], axis=0)
        o_ref[...] = (stack * scale).T.astype(o_ref.dtype)

    return pl.pallas_call(
        body, name=name, out_shape=jax.ShapeDtypeStruct((nb * ATT_BLK, Hh * HEAD_DIM), out_dtype), grid=(nb,),
        in_specs=[pl.BlockSpec((Hh, None, R, ATT_BLK), lambda i: (0, i, 0, 0))],
        out_specs=pl.BlockSpec((ATT_BLK, Hh * HEAD_DIM), lambda i: (i, 0)),
        compiler_params=_params(("parallel",)))(xT)


def _attn_cols(x, name):
    LP = x.shape[0]
    nb = LP // ATT_BLK

    def body(x_ref, o_ref):
        xt = x_ref[...].T
        for h in range(HEADS):
            o_ref[h] = xt[h * HEAD_DIM:(h + 1) * HEAD_DIM].astype(o_ref.dtype)

    return pl.pallas_call(
        body, name=name, out_shape=jax.ShapeDtypeStruct((HEADS, nb, HEAD_DIM, ATT_BLK), BF16), grid=(nb,),
        in_specs=[pl.BlockSpec((ATT_BLK, FOX_W), lambda i: (i, 0))],
        out_specs=pl.BlockSpec((HEADS, None, HEAD_DIM, ATT_BLK), lambda i: (0, i, 0, 0)),
        compiler_params=_params(("parallel",)))(x)


def _attn_dc(dqT, dkT, name):
    Hh, nb, _, _ = dqT.shape

    def body(q_ref, k_ref, o_ref):
        row = lax.broadcasted_iota(jnp.int32, (LANE, ATT_BLK), 0)
        acc = jnp.zeros((LANE, ATT_BLK), F32)
        for h in range(Hh):
            d = q_ref[h, ONES_IN_K:ONES_IN_K + 1, :] - k_ref[h, ONES_IN_Q:ONES_IN_Q + 1, :]
            acc = jnp.where(row == h, d, acc)
        o_ref[...] = acc.T

    spec = pl.BlockSpec((Hh, None, AUG, ATT_BLK), lambda i: (0, i, 0, 0))
    return pl.pallas_call(
        body, name=name, out_shape=jax.ShapeDtypeStruct((nb * ATT_BLK, LANE), F32), grid=(nb,),
        in_specs=[spec, spec], out_specs=pl.BlockSpec((ATT_BLK, LANE), lambda i: (i, 0)),
        compiler_params=_params(("parallel",)))(dqT, dkT)


def _attn_fwd2(qT, k_aug, vT, name, comm=None):
    Hh, nb, _, _ = qT.shape
    LP = nb * ATT_BLK
    Dh = vT.shape[2]
    HB = ATT_HEADS_PER_STEP
    n_comm = 0 if comm is None else len(comm[0])

    def body(*refs):
        q_ref, k_ref, v_ref = refs[:3]
        o_ref, lse_ref = refs[3 + n_comm:5 + n_comm]
        if n_comm:
            start, wait = _gather_first_ops(refs[3:3 + n_comm], refs[5 + n_comm:5 + 2 * n_comm],
                                            refs[5 + 2 * n_comm], refs[6 + 2 * n_comm], comm[1])
            pl.when(pl.program_id(0) == 0)(start)
        keys = lax.broadcasted_iota(jnp.int32, (ATT_BLK, ATT_BLK), 0)
        qrys = lax.broadcasted_iota(jnp.int32, (ATT_BLK, ATT_BLK), 1)
        causal = keys <= qrys

        def q_block(i, _):
            def tile(j, carry, masked):
                ks = pl.ds(pl.multiple_of(j * ATT_BLK, ATT_BLK), ATT_BLK)
                s_all = [jnp.dot(k_ref[hh, ks, :], q_ref[hh, i], preferred_element_type=F32) for hh in range(HB)]
                stats, p_all = [], []
                for hh in range(HB):
                    m, l, _ = carry[hh]
                    s = jnp.where(causal, s_all[hh], NEG) if masked else s_all[hh]
                    m_new = jnp.maximum(m, jnp.max(s, axis=0, keepdims=True))
                    p = jnp.exp(s - m_new)
                    alpha = jnp.exp(m - m_new)
                    stats.append((m_new, alpha * l + jnp.sum(p, axis=0, keepdims=True), alpha))
                    p_all.append(p.astype(BF16))
                out = []
                for hh in range(HB):
                    m_new, l, alpha = stats[hh]
                    acc = alpha * carry[hh][2] + jnp.dot(v_ref[hh, j], p_all[hh], preferred_element_type=F32)
                    out.append((m_new, l, acc))
                return tuple(out)

            init = tuple((jnp.full((1, ATT_BLK), NEG, F32), jnp.zeros((1, ATT_BLK), F32),
                          jnp.zeros((Dh, ATT_BLK), F32)) for _ in range(HB))
            carry = lax.fori_loop(0, i, lambda j, cr: tile(j, cr, False), init)
            carry = tile(i, carry, True)
            for hh in range(HB):
                m, l, acc = carry[hh]
                o_ref[hh, i] = acc / l
                lse_ref[hh, i] = m + jnp.log(l)
            return 0

        lax.fori_loop(0, nb, q_block, 0)
        if n_comm:
            pl.when(pl.program_id(0) == Hh // HB - 1)(wait)

    blk = lambda r: pl.BlockSpec((HB, nb, r, ATT_BLK), lambda h: (h, 0, 0, 0))
    out_shape = (jax.ShapeDtypeStruct((Hh, nb, Dh, ATT_BLK), F32), jax.ShapeDtypeStruct((Hh, nb, 1, ATT_BLK), F32))
    scratch = []
    args = [qT, k_aug, vT]
    if n_comm:
        out_shape += tuple(jax.ShapeDtypeStruct(cut.full, s.dtype) for cut, s in zip(comm[1], comm[0]))
        scratch = [pltpu.SemaphoreType.DMA((4 * n_comm,)), pltpu.SemaphoreType.DMA((4 * n_comm,))]
        args += list(comm[0])
    return pl.pallas_call(
        body, name=name, out_shape=out_shape, grid=(Hh // HB,),
        in_specs=[blk(AUG), pl.BlockSpec((HB, LP, AUG), lambda h: (h, 0, 0)), blk(Dh)] + [ANY] * n_comm,
        out_specs=(blk(Dh), blk(1)) + (ANY,) * n_comm, scratch_shapes=scratch,
        compiler_params=_params(("arbitrary",)))(*args)


def _attn_bwd2(qT, kT, k_aug, v, oT, doT, lse, name, comm=None):
    Hh, nb, _, _ = qT.shape
    LP = nb * ATT_BLK
    Dh = v.shape[2]
    nt = (((1,), (1,)), ((), ()))
    tn = (((0,), (0,)), ((), ()))

    HB = ATT_HEADS_PER_STEP_BWD
    n_comm = 0 if comm is None else len(comm[0])

    def body(*refs):
        q_ref, kt_ref, k_ref, v_ref, o_ref, do_ref, lse_ref = refs[:7]
        parts = refs[7:7 + n_comm]
        dq_ref, dk_ref, dv_ref = refs[7 + n_comm:10 + n_comm]
        others = refs[10 + n_comm:10 + 2 * n_comm]
        delta_ref = refs[10 + 2 * n_comm]
        if n_comm:
            start, wait = _chip_exchange_ops(parts, others, refs[11 + 2 * n_comm], refs[12 + 2 * n_comm], comm[1])
            pl.when(pl.program_id(0) == 0)(start)
        keys = lax.broadcasted_iota(jnp.int32, (ATT_BLK, ATT_BLK), 0)
        qrys = lax.broadcasted_iota(jnp.int32, (ATT_BLK, ATT_BLK), 1)
        causal = keys <= qrys

        def prep(i, _):
            for hh in range(HB):
                delta_ref[hh, i] = jnp.sum(do_ref[hh, i].astype(F32) * o_ref[hh, i], axis=0, keepdims=True)
                dq_ref[hh, i] = jnp.zeros((AUG, ATT_BLK), F32)
            return 0

        lax.fori_loop(0, nb, prep, 0)

        def kv_block(j, _):
            ks = pl.ds(pl.multiple_of(j * ATT_BLK, ATT_BLK), ATT_BLK)

            def tile(i, carry, masked):
                s_all = [jnp.dot(k_ref[hh, ks, :], q_ref[hh, i], preferred_element_type=F32) for hh in range(HB)]
                dp_all = [lax.dot_general(v_ref[hh, j], do_ref[hh, i], tn, preferred_element_type=F32)
                          for hh in range(HB)]
                p_all, ds_all = [], []
                for hh in range(HB):
                    s = jnp.where(causal, s_all[hh], NEG) if masked else s_all[hh]
                    p = jnp.exp(s - lse_ref[hh, i])
                    ds_all.append((p * (dp_all[hh] - delta_ref[hh, i])).astype(BF16))
                    p_all.append(p.astype(BF16))
                out = []
                for hh in range(HB):
                    dk, dv = carry[hh]
                    dv = dv + lax.dot_general(do_ref[hh, i], p_all[hh], nt, preferred_element_type=F32)
                    dk = dk + lax.dot_general(q_ref[hh, i], ds_all[hh], nt, preferred_element_type=F32)
                    out.append((dk, dv))
                dq_new = [jnp.dot(kt_ref[hh, j], ds_all[hh], preferred_element_type=F32) for hh in range(HB)]
                for hh in range(HB):
                    dq_ref[hh, i] += dq_new[hh]
                return tuple(out)

            init = tuple((jnp.zeros((AUG, ATT_BLK), F32), jnp.zeros((Dh, ATT_BLK), F32)) for _ in range(HB))
            carry = tile(j, init, True)
            carry = lax.fori_loop(j + 1, nb, lambda i, cr: tile(i, cr, False), carry)
            for hh in range(HB):
                dk_ref[hh, j] = carry[hh][0]
                dv_ref[hh, j] = carry[hh][1]
            return 0

        lax.fori_loop(0, nb, kv_block, 0)
        if n_comm:
            pl.when(pl.program_id(0) == Hh // HB - 1)(wait)

    blk = lambda r: pl.BlockSpec((HB, nb, r, ATT_BLK), lambda h: (h, 0, 0, 0))
    row = lambda cols: pl.BlockSpec((HB, LP, cols), lambda h: (h, 0, 0))
    out_shape = (jax.ShapeDtypeStruct((Hh, nb, AUG, ATT_BLK), F32), jax.ShapeDtypeStruct((Hh, nb, AUG, ATT_BLK), F32),
                 jax.ShapeDtypeStruct((Hh, nb, Dh, ATT_BLK), F32))
    scratch = [pltpu.VMEM((HB, nb, 1, ATT_BLK), F32)]
    args = [qT, kT, k_aug, v, oT, doT, lse]
    if n_comm:
        out_shape += _chip_exchange_shapes(*comm)
        scratch += [pltpu.SemaphoreType.DMA((3 * n_comm,)), pltpu.SemaphoreType.DMA((3 * n_comm,))]
        args += list(comm[0])
    return pl.pallas_call(
        body, name=name, out_shape=out_shape, grid=(Hh // HB,),
        in_specs=[blk(AUG), blk(AUG), row(AUG), blk(Dh), blk(Dh), blk(Dh), blk(1)] + [ANY] * n_comm,
        out_specs=(blk(AUG), blk(AUG), blk(Dh)) + (ANY,) * n_comm,
        scratch_shapes=scratch,
        compiler_params=_params(("arbitrary",)))(*args)


CONV_HALO = 32
A_BLK = 3 * FOX_W // CONV_CH
G_BLK = A_BLK + 1


def _conf_fwd(proj, cw, cb, lg, lb, tm, name):
    LP = proj.shape[0]
    C = CONV_CH
    sub = _sub_rows(tm)
    hpb = tm // CONV_HALO

    def body(a_ref, g_ref, ah_ref, gh_ref, w_ref, cb_ref, lg_ref, lb_ref, u1_ref, u_ref, buf):
        r = pl.program_id(0)
        buf[CONV_HALO:CONV_HALO + tm, :] = a_ref[...] * _sigmoid(g_ref[...])
        buf[0:CONV_HALO, :] = jnp.where(r > 0, ah_ref[...] * _sigmoid(gh_ref[...]), 0.0)
        for s in range(tm // sub):
            for ct in range(C // LANE):
                ln = slice(ct * LANE, (ct + 1) * LANE)
                acc = jnp.broadcast_to(cb_ref[:, ln], (sub, LANE))
                for kk in range(CONV_K):
                    off = CONV_HALO + s * sub - (CONV_K - 1) + kk
                    acc = acc + w_ref[kk:kk + 1, ln] * buf[off:off + sub, ln]
                u1_ref[s * sub:(s + 1) * sub, ln] = acc
        u1 = u1_ref[...]
        mu = jnp.mean(u1, axis=1, keepdims=True)
        xc = u1 - mu
        var = jnp.mean(xc * xc, axis=1, keepdims=True)
        y = xc * lax.rsqrt(var + LN_EPS) * lg_ref[...] + lb_ref[...]
        u_ref[...] = (y * _sigmoid(y)).astype(u_ref.dtype)

    cur = lambda blk: pl.BlockSpec((tm, C), lambda r: (r, blk))
    halo = lambda blk: pl.BlockSpec((CONV_HALO, C), lambda r: (jnp.maximum(r * hpb - 1, 0), blk))
    vec = pl.BlockSpec((1, C), lambda r: (0, 0))
    out = pl.BlockSpec((tm, C), lambda r: (r, 0))
    return pl.pallas_call(
        body, name=name,
        out_shape=(jax.ShapeDtypeStruct((LP, C), F32), jax.ShapeDtypeStruct((LP, C), BF16)),
        grid=(LP // tm,),
        in_specs=[cur(A_BLK), cur(G_BLK), halo(A_BLK), halo(G_BLK),
                  pl.BlockSpec((CONV_HALO, C), lambda r: (0, 0)), vec, vec, vec],
        out_specs=(out, out),
        scratch_shapes=[pltpu.VMEM((CONV_HALO + tm, C), F32)],
        compiler_params=_params(("parallel",)))(proj, proj, proj, proj, cw, cb, lg, lb)


def _conf_bwd(proj, u1, dcat, cw, lg, lb, tm, name, host=None):
    LP = proj.shape[0]
    C = CONV_CH
    sub = _sub_rows(tm)
    hpb = tm // CONV_HALO
    nblk = LP // tm
    last_halo = LP // CONV_HALO - 1

    n_host = 0 if host is None else len(host[0])
    if n_host:
        host_shapes, host_sems, host_ops = _host_plan(host)

    def body(*refs):
        a_ref, g_ref, ah_ref, gh_ref, u1_ref, u1n_ref, du_ref, dun_ref, w_ref, lg_ref, lb_ref = refs[:11]
        dadg_ref, dw_ref, dcb_ref, dlg_ref, dlb_ref = refs[11 + n_host:16 + n_host]
        ubuf, dbuf, du0 = refs[16 + 2 * n_host:19 + 2 * n_host]
        if n_host:
            start, wait = host_ops(refs[11:11 + n_host], refs[16 + n_host:16 + 2 * n_host],
                                   refs[19 + 2 * n_host], refs[20 + 2 * n_host], host[1])
            pl.when(pl.program_id(0) == 0)(start)
        r = pl.program_id(0)
        lgv = lg_ref[...]
        lbv = lb_ref[...]

        def ln_silu_bwd(u1v, duv):
            mu = jnp.mean(u1v, axis=1, keepdims=True)
            xc = u1v - mu
            rstd = lax.rsqrt(jnp.mean(xc * xc, axis=1, keepdims=True) + LN_EPS)
            xhat = xc * rstd
            y = xhat * lgv + lbv
            sg = _sigmoid(y)
            dy = duv * (sg * (1.0 + y * (1.0 - sg)))
            dxh = dy * lgv
            du1 = rstd * (dxh - jnp.mean(dxh, axis=1, keepdims=True)
                          - xhat * jnp.mean(dxh * xhat, axis=1, keepdims=True))
            return du1, dy, xhat

        @pl.when(r == 0)
        def _():
            dw_ref[...] = jnp.zeros_like(dw_ref)
            dcb_ref[...] = jnp.zeros_like(dcb_ref)
            dlg_ref[...] = jnp.zeros_like(dlg_ref)
            dlb_ref[...] = jnp.zeros_like(dlb_ref)

        du1, dy, xhat = ln_silu_bwd(u1_ref[...], du_ref[...])
        dlg_ref[...] += jnp.sum(dy * xhat, axis=0, keepdims=True)
        dlb_ref[...] += jnp.sum(dy, axis=0, keepdims=True)
        dcb_ref[...] += jnp.sum(du1, axis=0, keepdims=True)
        dbuf[0:tm, :] = du1
        du1n, _, _ = ln_silu_bwd(u1n_ref[...], dun_ref[...])
        dbuf[tm:tm + CONV_HALO, :] = jnp.where(r < nblk - 1, du1n, 0.0)
        ubuf[CONV_HALO:CONV_HALO + tm, :] = a_ref[...] * _sigmoid(g_ref[...])
        ubuf[0:CONV_HALO, :] = jnp.where(r > 0, ah_ref[...] * _sigmoid(gh_ref[...]), 0.0)

        for ct in range(C // LANE):
            ln = slice(ct * LANE, (ct + 1) * LANE)
            for s in range(tm // sub):
                d_here = dbuf[s * sub:(s + 1) * sub, ln]
                acc = jnp.zeros((sub, LANE), F32)
                for kk in range(CONV_K):
                    fo = s * sub + (CONV_K - 1) - kk
                    acc = acc + w_ref[kk:kk + 1, ln] * dbuf[fo:fo + sub, ln]
                    bo = CONV_HALO + s * sub - (CONV_K - 1) + kk
                    dw_ref[kk:kk + 1, ln] += jnp.sum(d_here * ubuf[bo:bo + sub, ln], axis=0, keepdims=True)
                du0[s * sub:(s + 1) * sub, ln] = acc
        a = a_ref[...]
        sg = _sigmoid(g_ref[...])
        d0 = du0[...]
        dadg_ref[:, 0:C] = (d0 * sg).astype(dadg_ref.dtype)
        dadg_ref[:, C:2 * C] = (d0 * a * sg * (1.0 - sg)).astype(dadg_ref.dtype)
        if n_host:
            pl.when(pl.program_id(0) == nblk - 1)(wait)

    cur = lambda blk: pl.BlockSpec((tm, C), lambda r: (r, blk))
    prev = lambda blk: pl.BlockSpec((CONV_HALO, C), lambda r: (jnp.maximum(r * hpb - 1, 0), blk))
    nxt = lambda blk: pl.BlockSpec((CONV_HALO, C), lambda r: (jnp.minimum((r + 1) * hpb, last_halo), blk))
    vec = pl.BlockSpec((1, C), lambda r: (0, 0))
    wspec = pl.BlockSpec((CONV_HALO, C), lambda r: (0, 0))
    out_shape = (jax.ShapeDtypeStruct((LP, 2 * C), BF16), jax.ShapeDtypeStruct((CONV_HALO, C), F32),
                 jax.ShapeDtypeStruct((1, C), F32), jax.ShapeDtypeStruct((1, C), F32),
                 jax.ShapeDtypeStruct((1, C), F32))
    out_specs = (pl.BlockSpec((tm, 2 * C), lambda r: (r, 0)), wspec, vec, vec, vec)
    in_specs = [cur(A_BLK), cur(G_BLK), prev(A_BLK), prev(G_BLK), cur(0), nxt(0), cur(1), nxt(1), wspec, vec, vec]
    args = [proj, proj, proj, proj, u1, u1, dcat, dcat, cw, lg, lb]
    scratch = [pltpu.VMEM((CONV_HALO + tm, C), F32), pltpu.VMEM((tm + CONV_HALO, C), F32),
               pltpu.VMEM((tm, C), F32)]
    if n_host:
        anyspec = pl.BlockSpec(memory_space=pl.ANY)
        in_specs += [anyspec] * n_host
        args += list(host[0])
        out_shape += host_shapes
        out_specs += (anyspec,) * n_host
        scratch += [pltpu.SemaphoreType.DMA((host_sems,)), pltpu.SemaphoreType.DMA((host_sems,))]
    return pl.pallas_call(
        body, name=name, out_shape=out_shape, grid=(nblk,), in_specs=in_specs, out_specs=out_specs,
        scratch_shapes=scratch, compiler_params=_params(("arbitrary",)))(*args)


FFN_HALO = 8
FFN_TC = 256
FFN_K = 3


def _ffn_conv(buf, w_ref, b_ref, s, sub, ln):
    acc = jnp.broadcast_to(b_ref[:, ln], (sub, LANE))
    for kk in range(FFN_K):
        off = FFN_HALO + s * sub - (FFN_K - 1) + kk
        acc = acc + w_ref[kk:kk + 1, ln] * buf[off:off + sub, ln]
    return acc


def _ffn_act_fwd(up, w, b, tm, name, host=None):
    LP, F = up.shape[0], up.shape[1] // 2
    upg = upv = up
    nct = F // FFN_TC
    sub = _sub_rows(tm)
    hpb = tm // FFN_HALO
    n_host = 0 if host is None else len(host[0])
    nrb = LP // tm

    def body(*refs):
        g_ref, v_ref, gh_ref, vh_ref, wg_ref, wv_ref, bg_ref, bv_ref = refs[:8]
        act_ref = refs[8 + n_host]
        gbuf, vbuf = refs[9 + 2 * n_host:11 + 2 * n_host]
        if n_host:
            start, wait = _gather_first_ops(refs[8:8 + n_host], refs[9 + n_host:9 + 2 * n_host],
                                            refs[11 + 2 * n_host], refs[12 + 2 * n_host], host[1])
            pl.when(jnp.logical_and(pl.program_id(0) == 0, pl.program_id(1) == 0))(start)
        r = pl.program_id(1)
        gbuf[FFN_HALO:FFN_HALO + tm, :] = g_ref[...]
        vbuf[FFN_HALO:FFN_HALO + tm, :] = v_ref[...]
        gbuf[0:FFN_HALO, :] = jnp.where(r > 0, gh_ref[...], 0.0)
        vbuf[0:FFN_HALO, :] = jnp.where(r > 0, vh_ref[...], 0.0)
        for s in range(tm // sub):
            for ct in range(FFN_TC // LANE):
                ln = slice(ct * LANE, (ct + 1) * LANE)
                gc = _ffn_conv(gbuf, wg_ref, bg_ref, s, sub, ln)
                vc = _ffn_conv(vbuf, wv_ref, bv_ref, s, sub, ln)
                act_ref[s * sub:(s + 1) * sub, ln] = (gc * _sigmoid(gc) * vc).astype(act_ref.dtype)
        if n_host:
            pl.when(jnp.logical_and(pl.program_id(0) == nct - 1, pl.program_id(1) == nrb - 1))(wait)

    cur = pl.BlockSpec((tm, FFN_TC), lambda c, r: (r, c))
    halo = pl.BlockSpec((FFN_HALO, FFN_TC), lambda c, r: (jnp.maximum(r * hpb - 1, 0), c))
    wg = pl.BlockSpec((8, FFN_TC), lambda c, r: (0, c))
    wv = pl.BlockSpec((8, FFN_TC), lambda c, r: (0, nct + c))
    bg = pl.BlockSpec((1, FFN_TC), lambda c, r: (0, c))
    bv = pl.BlockSpec((1, FFN_TC), lambda c, r: (0, nct + c))
    curv = pl.BlockSpec((tm, FFN_TC), lambda c, r: (r, nct + c))
    halov = pl.BlockSpec((FFN_HALO, FFN_TC), lambda c, r: (jnp.maximum(r * hpb - 1, 0), nct + c))
    out_shape = jax.ShapeDtypeStruct((LP, F), BF16)
    out_specs = cur
    in_specs = [cur, curv, halo, halov, wg, wv, bg, bv]
    args = [upg, upv, upg, upv, w, w, b, b]
    scratch = [pltpu.VMEM((FFN_HALO + tm, FFN_TC), F32)] * 2
    if n_host:
        anyspec = pl.BlockSpec(memory_space=pl.ANY)
        in_specs += [anyspec] * n_host
        args += list(host[0])
        out_shape = (out_shape,) + tuple(jax.ShapeDtypeStruct(cut.full, s.dtype) for cut, s in zip(host[1], host[0]))
        out_specs = (cur,) + (anyspec,) * n_host
        scratch += [pltpu.SemaphoreType.DMA((4 * n_host,)), pltpu.SemaphoreType.DMA((4 * n_host,))]
    sem = ("arbitrary", "arbitrary") if n_host else ("parallel", "parallel")
    return pl.pallas_call(
        body, name=name, out_shape=out_shape, grid=(nct, nrb), in_specs=in_specs, out_specs=out_specs,
        scratch_shapes=scratch, compiler_params=_params(sem))(*args)


def _ffn_act_bwd(up, dact, w, b, tm, name, comm=None):
    LP, F = up.shape[0], up.shape[1] // 2
    upg = upv = up
    nct = F // FFN_TC
    sub = _sub_rows(tm)
    hpb = tm // FFN_HALO
    nblk = LP // tm
    last_halo = LP // FFN_HALO - 1
    TB = tm + 2 * FFN_HALO
    n_comm = 0 if comm is None else len(comm[0])

    def body(*refs):
        (g_ref, v_ref, gp_ref, vp_ref, gn_ref, vn_ref, da_ref, dan_ref,
         wg_ref, wv_ref, bg_ref, bv_ref) = refs[:12]
        dup_ref, dwg_ref, dwv_ref, dbg_ref, dbv_ref = refs[12 + n_comm:17 + n_comm]
        gbuf, vbuf, dgb, dvb = refs[17 + 2 * n_comm:21 + 2 * n_comm]
        if n_comm:
            start, wait = _chip_exchange_ops(refs[12:12 + n_comm], refs[17 + n_comm:17 + 2 * n_comm],
                                             refs[21 + 2 * n_comm], refs[22 + 2 * n_comm], comm[1])
            pl.when(jnp.logical_and(pl.program_id(0) == 0, pl.program_id(1) == 0))(start)
        r = pl.program_id(1)
        dg_ref = dup_ref.at[0]
        dv_ref = dup_ref.at[1]
        first = r == 0
        last = r == nblk - 1

        @pl.when(first)
        def _():
            dwg_ref[...] = jnp.zeros_like(dwg_ref)
            dwv_ref[...] = jnp.zeros_like(dwv_ref)
            dbg_ref[...] = jnp.zeros_like(dbg_ref)
            dbv_ref[...] = jnp.zeros_like(dbv_ref)

        for buf, c_ref, p_ref, n_ref in ((gbuf, g_ref, gp_ref, gn_ref), (vbuf, v_ref, vp_ref, vn_ref)):
            buf[0:FFN_HALO, :] = jnp.where(first, 0.0, p_ref[...])
            buf[FFN_HALO:FFN_HALO + tm, :] = c_ref[...]
            buf[FFN_HALO + tm:TB, :] = jnp.where(last, 0.0, n_ref[...])

        def dconv(s0, nrows, ln, dact_v):
            xg = [gbuf[s0 - (FFN_K - 1) + kk:s0 - (FFN_K - 1) + kk + nrows, ln] for kk in range(FFN_K)]
            xv = [vbuf[s0 - (FFN_K - 1) + kk:s0 - (FFN_K - 1) + kk + nrows, ln] for kk in range(FFN_K)]
            gc = jnp.broadcast_to(bg_ref[:, ln], (nrows, LANE))
            vc = jnp.broadcast_to(bv_ref[:, ln], (nrows, LANE))
            for kk in range(FFN_K):
                gc = gc + wg_ref[kk:kk + 1, ln] * xg[kk]
                vc = vc + wv_ref[kk:kk + 1, ln] * xv[kk]
            sg = _sigmoid(gc)
            return dact_v * vc * (sg * (1.0 + gc * (1.0 - sg))), dact_v * (gc * sg), xg, xv

        colsum = lambda t: jnp.sum(t, axis=0, keepdims=True)
        for ct in range(FFN_TC // LANE):
            ln = slice(ct * LANE, (ct + 1) * LANE)
            zero = jnp.zeros((1, LANE), F32)
            dwg, dwv, dbg, dbv = [zero] * FFN_K, [zero] * FFN_K, zero, zero
            for s in range(tm // sub):
                dgc, dvc, xg, xv = dconv(FFN_HALO + s * sub, sub, ln, da_ref[s * sub:(s + 1) * sub, ln])
                dgb[s * sub:(s + 1) * sub, ln] = dgc
                dvb[s * sub:(s + 1) * sub, ln] = dvc
                dwg = [dwg[kk] + colsum(dgc * xg[kk]) for kk in range(FFN_K)]
                dwv = [dwv[kk] + colsum(dvc * xv[kk]) for kk in range(FFN_K)]
                dbg, dbv = dbg + colsum(dgc), dbv + colsum(dvc)
            for kk in range(FFN_K):
                dwg_ref[kk:kk + 1, ln] += dwg[kk]
                dwv_ref[kk:kk + 1, ln] += dwv[kk]
            dbg_ref[:, ln] += dbg
            dbv_ref[:, ln] += dbv
            dgc, dvc, _, _ = dconv(FFN_HALO + tm, FFN_HALO, ln, jnp.where(last, 0.0, dan_ref[:, ln]))
            dgb[tm:tm + FFN_HALO, ln] = dgc
            dvb[tm:tm + FFN_HALO, ln] = dvc
            for dbuf, w_ref, dout in ((dgb, wg_ref, dg_ref), (dvb, wv_ref, dv_ref)):
                for s in range(tm // sub):
                    acc = jnp.zeros((sub, LANE), F32)
                    for kk in range(FFN_K):
                        fo = s * sub + (FFN_K - 1) - kk
                        acc = acc + w_ref[kk:kk + 1, ln] * dbuf[fo:fo + sub, ln]
                    dout[s * sub:(s + 1) * sub, ln] = acc.astype(dout.dtype)
        if n_comm:
            pl.when(jnp.logical_and(pl.program_id(0) == nct - 1, pl.program_id(1) == nblk - 1))(wait)

    cur = pl.BlockSpec((tm, FFN_TC), lambda c, r: (r, c))
    prev = pl.BlockSpec((FFN_HALO, FFN_TC), lambda c, r: (jnp.maximum(r * hpb - 1, 0), c))
    nxt = pl.BlockSpec((FFN_HALO, FFN_TC), lambda c, r: (jnp.minimum((r + 1) * hpb, last_halo), c))
    wg = pl.BlockSpec((8, FFN_TC), lambda c, r: (0, c))
    wv = pl.BlockSpec((8, FFN_TC), lambda c, r: (0, nct + c))
    bg = pl.BlockSpec((1, FFN_TC), lambda c, r: (0, c))
    bv = pl.BlockSpec((1, FFN_TC), lambda c, r: (0, nct + c))
    curv = pl.BlockSpec((tm, FFN_TC), lambda c, r: (r, nct + c))
    prevv = pl.BlockSpec((FFN_HALO, FFN_TC), lambda c, r: (jnp.maximum(r * hpb - 1, 0), nct + c))
    nxtv = pl.BlockSpec((FFN_HALO, FFN_TC), lambda c, r: (jnp.minimum((r + 1) * hpb, last_halo), nct + c))
    out_shape = (jax.ShapeDtypeStruct((2, LP, F), BF16),
                 jax.ShapeDtypeStruct((8, F), F32), jax.ShapeDtypeStruct((8, F), F32),
                 jax.ShapeDtypeStruct((1, F), F32), jax.ShapeDtypeStruct((1, F), F32))
    out_specs = (pl.BlockSpec((2, tm, FFN_TC), lambda c, r: (0, r, c)),
                 pl.BlockSpec((8, FFN_TC), lambda c, r: (0, c)),
                 pl.BlockSpec((8, FFN_TC), lambda c, r: (0, c)),
                 pl.BlockSpec((1, FFN_TC), lambda c, r: (0, c)),
                 pl.BlockSpec((1, FFN_TC), lambda c, r: (0, c)))
    in_specs = [cur, curv, prev, prevv, nxt, nxtv, cur, nxt, wg, wv, bg, bv]
    args = [upg, upv, upg, upv, upg, upv, dact, dact, w, w, b, b]
    scratch = [pltpu.VMEM((TB, FFN_TC), F32), pltpu.VMEM((TB, FFN_TC), F32),
               pltpu.VMEM((tm + FFN_HALO, FFN_TC), F32), pltpu.VMEM((tm + FFN_HALO, FFN_TC), F32)]
    if n_comm:
        anyspec = pl.BlockSpec(memory_space=pl.ANY)
        in_specs += [anyspec] * n_comm
        args += list(comm[0])
        out_shape += _chip_exchange_shapes(*comm)
        out_specs += (anyspec,) * n_comm
        scratch += [pltpu.SemaphoreType.DMA((3 * n_comm,)), pltpu.SemaphoreType.DMA((3 * n_comm,))]
    sem = ("arbitrary", "arbitrary") if n_comm else ("parallel", "arbitrary")
    dup, dwg, dwv, dbg, dbv, *others = pl.pallas_call(
        body, name=name, out_shape=out_shape, grid=(nct, nblk), in_specs=in_specs, out_specs=out_specs,
        scratch_shapes=scratch, compiler_params=_params(sem))(*args)
    return (dup, jnp.concatenate([dwg, dwv], axis=1), jnp.concatenate([dbg, dbv], axis=1)) + tuple(others)


POOL_HALO = 16


def _pool_fwd(h, g, pw, pb, ps, tm, name):
    LP, Dm = h.shape
    sub = _sub_rows(tm)
    hpb = tm // POOL_HALO

    def body(h_ref, hh_ref, g_ref, pw_ref, pb_ref, ps_ref, o_ref, d_ref, buf):
        r = pl.program_id(0)
        gg = g_ref[...]

        def norm(x):
            return x * lax.rsqrt(jnp.mean(x * x, axis=1, keepdims=True) + RMS_EPS) * gg

        x = h_ref[...]
        buf[POOL_HALO:POOL_HALO + tm, :] = norm(x)
        buf[0:POOL_HALO, :] = jnp.where(r > 0, norm(hh_ref[...]), 0.0)
        for gi, w in enumerate(POOL_WINDOWS):
            ln = slice(gi * POOL_G, (gi + 1) * POOL_G)
            for s in range(tm // sub):
                base = POOL_HALO + s * sub
                acc = buf[base:base + sub, ln]
                for jj in range(1, w):
                    acc = acc + buf[base - jj:base - jj + sub, ln]
                t = r * tm + s * sub + lax.broadcasted_iota(jnp.int32, (sub, 1), 0)
                cnt = jnp.minimum(t + 1, w).astype(F32)
                d_ref[s * sub:(s + 1) * sub, ln] = (acc / cnt - buf[base:base + sub, ln]).astype(d_ref.dtype)
            y = jnp.dot(d_ref[:, ln], pw_ref[gi], preferred_element_type=F32) + pb_ref[:, ln]
            o_ref[:, ln] = x[:, ln] + y * ps_ref[:, ln]

    row = pl.BlockSpec((tm, Dm), lambda r: (r, 0))
    halo = pl.BlockSpec((POOL_HALO, Dm), lambda r: (jnp.maximum(r * hpb - 1, 0), 0))
    vec = pl.BlockSpec((1, Dm), lambda r: (0, 0))
    wsp = pl.BlockSpec((len(POOL_WINDOWS), POOL_G, POOL_G), lambda r: (0, 0, 0))
    return pl.pallas_call(
        body, name=name,
        out_shape=(jax.ShapeDtypeStruct((LP, Dm), F32), jax.ShapeDtypeStruct((LP, Dm), BF16)),
        grid=(LP // tm,), in_specs=[row, halo, vec, wsp, vec, vec], out_specs=(row, row),
        scratch_shapes=[pltpu.VMEM((POOL_HALO + tm, Dm), F32)],
        compiler_params=_params(("parallel",)))(h, h, g, pw, pb, ps)


def _pool_bwd(h, g, d, pw, pb, ps, dh_out, tm, name):
    LP, Dm = h.shape
    sub = _sub_rows(tm)
    hpb = tm // POOL_HALO
    nblk = LP // tm
    last_halo = LP // POOL_HALO - 1
    nt = (((1,), (1,)), ((), ()))
    tn = (((0,), (0,)), ((), ()))

    def body(h_ref, g_ref, d_ref, pw_ref, pb_ref, ps_ref, do_ref, don_ref,
             dh_ref, dpw_ref, dpb_ref, dps_ref, dg_ref, ebuf, ddb, dnb):
        r = pl.program_id(0)

        @pl.when(r == 0)
        def _():
            dpw_ref[...] = jnp.zeros_like(dpw_ref)
            dpb_ref[...] = jnp.zeros_like(dpb_ref)
            dps_ref[...] = jnp.zeros_like(dps_ref)
            dg_ref[...] = jnp.zeros_like(dg_ref)

        for gi, w in enumerate(POOL_WINDOWS):
            ln = slice(gi * POOL_G, (gi + 1) * POOL_G)
            wg = pw_ref[gi]
            dog = do_ref[:, ln]
            dg_b = d_ref[:, ln]
            y_pre = jnp.dot(dg_b, wg, preferred_element_type=F32) + pb_ref[:, ln]
            dps_ref[:, ln] += jnp.sum(dog * y_pre, axis=0, keepdims=True)
            dy = dog * ps_ref[:, ln]
            dpb_ref[:, ln] += jnp.sum(dy, axis=0, keepdims=True)
            dyb = dy.astype(BF16)
            dpw_ref[gi] += lax.dot_general(dg_b, dyb, tn, preferred_element_type=F32)
            dd = lax.dot_general(dyb, wg, nt, preferred_element_type=F32)
            ddb[:, ln] = dd
            t = r * tm + lax.broadcasted_iota(jnp.int32, (tm, 1), 0)
            ebuf[0:tm, ln] = dd / jnp.minimum(t + 1, w).astype(F32)
            dyn = (don_ref[:, ln] * ps_ref[:, ln]).astype(BF16)
            ddn = lax.dot_general(dyn, wg, nt, preferred_element_type=F32)
            tn_ = (r + 1) * tm + lax.broadcasted_iota(jnp.int32, (POOL_HALO, 1), 0)
            ebuf[tm:tm + POOL_HALO, ln] = jnp.where(r < nblk - 1, ddn / jnp.minimum(tn_ + 1, w).astype(F32), 0.0)
            for s in range(tm // sub):
                acc = ebuf[s * sub:(s + 1) * sub, ln]
                for jj in range(1, w):
                    acc = acc + ebuf[s * sub + jj:s * sub + jj + sub, ln]
                dnb[s * sub:(s + 1) * sub, ln] = acc - ddb[s * sub:(s + 1) * sub, ln]
        x = h_ref[...]
        rr = lax.rsqrt(jnp.mean(x * x, axis=1, keepdims=True) + RMS_EPS)
        xhat = x * rr
        dn = dnb[...]
        dxh = dn * g_ref[...]
        dh_ref[...] = do_ref[...] + rr * (dxh - xhat * jnp.mean(dxh * xhat, axis=1, keepdims=True))
        dg_ref[...] += jnp.sum(dn * xhat, axis=0, keepdims=True)

    row = pl.BlockSpec((tm, Dm), lambda r: (r, 0))
    nxt = pl.BlockSpec((POOL_HALO, Dm), lambda r: (jnp.minimum((r + 1) * hpb, last_halo), 0))
    vec = pl.BlockSpec((1, Dm), lambda r: (0, 0))
    wsp = pl.BlockSpec((len(POOL_WINDOWS), POOL_G, POOL_G), lambda r: (0, 0, 0))
    return pl.pallas_call(
        body, name=name,
        out_shape=(jax.ShapeDtypeStruct((LP, Dm), F32),
                   jax.ShapeDtypeStruct((len(POOL_WINDOWS), POOL_G, POOL_G), F32),
                   jax.ShapeDtypeStruct((1, Dm), F32), jax.ShapeDtypeStruct((1, Dm), F32),
                   jax.ShapeDtypeStruct((1, Dm), F32)),
        grid=(nblk,), in_specs=[row, vec, row, wsp, vec, vec, row, nxt],
        out_specs=(row, wsp, vec, vec, vec),
        scratch_shapes=[pltpu.VMEM((tm + POOL_HALO, Dm), F32), pltpu.VMEM((tm, Dm), F32),
                        pltpu.VMEM((tm, Dm), F32)],
        compiler_params=_params(("arbitrary",)))(h, g, d, pw, pb, ps, dh_out, dh_out)


def _adamw(w, g, m, v, name):
    shape = w.shape
    cols = shape[-1]
    rows = int(np.prod(shape[:-1])) if len(shape) > 1 else 1
    w2, g2, m2, v2 = (t.reshape(rows, cols) for t in (w, g, m, v))
    tr = rows
    for cand in (256, 128, 64, 32, 16, 8):
        if rows % cand == 0 and rows > cand:
            tr = cand
            break
    c1 = float(1.0 - ADAM_B1 ** ADAM_STEP)
    c2 = float(1.0 - ADAM_B2 ** ADAM_STEP)

    def body(w_ref, g_ref, m_ref, v_ref, d_ref, mo_ref, vo_ref):
        gg = g_ref[...]
        mn = ADAM_B1 * m_ref[...] + (1.0 - ADAM_B1) * gg
        vn = ADAM_B2 * v_ref[...] + (1.0 - ADAM_B2) * (gg * gg)
        m_hat = mn / c1
        v_hat = vn / c2
        d_ref[...] = -ADAM_LR * (m_hat / (jnp.sqrt(v_hat) + ADAM_EPS) + ADAM_WD * w_ref[...])
        mo_ref[...] = mn
        vo_ref[...] = vn

    spec = pl.BlockSpec((tr, cols), lambda i: (i, 0))
    sds = jax.ShapeDtypeStruct((rows, cols), F32)
    d2, mo, vo = pl.pallas_call(
        body, name=name, out_shape=(sds, sds, sds), grid=(rows // tr,),
        in_specs=[spec] * 4, out_specs=(spec,) * 3,
        compiler_params=_params(("parallel",)))(w2, g2, m2, v2)
    return d2.reshape(shape), mo.reshape(shape), vo.reshape(shape)


def _row_tiles(LP):
    tm = LP // 4
    assert LP % 4 == 0 and tm % CONV_HALO == 0 and LP % ATT_BLK == 0, LP
    return tm, LP // 2


def _heads(t, LP):
    return t.reshape(LP, HEADS, HEAD_DIM).transpose(1, 0, 2)


def _unheads(t, LP):
    return t.transpose(1, 0, 2).reshape(LP, FOX_W)


def _ffn_fwd(h, gain, wug, wuv, cw, cb, wd, tm, tmm, tag):
    n = _rms_fwd(h, gain, BF16, tm, f"ffn_norm_{tag}")
    upg = _mm(n, wug, "nn", F32, tmm, 256, f"ffn_up_gate_{tag}")
    upv = _mm(n, wuv, "nn", F32, tmm, 256, f"ffn_up_val_{tag}")
    act = _ffn_act_fwd(upg, upv, cw, cb, tm, f"ffn_act_{tag}")
    out = _mm(act, wd, "nn", F32, tmm, 512, f"ffn_down_{tag}", add=h)
    return out, (n, upg, upv, act)


def _ffn_bwd(h, gain, wug, wuv, cw, cb, wd, saved, dout, tm, tmm, tag):
    n, upg, upv, act = saved
    dact = _mm(dout, wd, "nt", F32, tmm, 256, f"ffn_dact_{tag}")
    dwd = _mm(act, dout, "tn", F32, 256, 512, f"ffn_dwdown_{tag}")
    dupg, dupv, dcw, dcb = _ffn_act_bwd(upg, upv, dact, cw, cb, tm, f"ffn_act_bwd_{tag}")
    dn = _mm(dupg, wug, "nt", F32, tm, 512, f"ffn_dn_gate_{tag}")
    dn = _mm(dupv, wuv, "nt", F32, tm, 512, f"ffn_dn_val_{tag}", add=dn)
    dwug = _mm(n, dupg, "tn", F32, 512, 256, f"ffn_dwup_gate_{tag}")
    dwuv = _mm(n, dupv, "tn", F32, 512, 256, f"ffn_dwup_val_{tag}")
    dh, dgain = _rms_bwd(h, gain, dn, dout, tm, f"ffn_norm_bwd_{tag}")
    return dh, dict(gain=dgain, wug=dwug, wuv=dwuv, cw=dcw[:FFN_K], cb=dcb, wd=dwd)


def _local_step(h0, tgt, W, n_real):
    LP = h0.shape[0]
    tm, tmm = _row_tiles(LP)
    nb = LP // ATT_BLK
    G = {}

    n0 = _rms_fwd(h0, W["mix_norm_even"], BF16, tm, "mix_norm_even")
    proj = _mm(n0, W["w_in_p"], "nn", F32, tmm, 384, "in_proj")
    c = _fgate_fwd(proj, W["b_f_p"], "forget_gate")
    cT = c[:, :HEADS].T
    c_col = cT[:, :, None]
    c_row = cT.reshape(HEADS, nb, 1, ATT_BLK)
    qkv = proj[:, :3 * FOX_W].astype(BF16)
    q, k, v = (_heads(qkv[:, i * FOX_W:(i + 1) * FOX_W], LP) for i in range(3))
    o, lse = _attn_fwd(q, k, v, c_col, c_row, "fox_attention")
    u1, u = _conf_fwd(proj, W["conv_w_p"], W["conv_b"], W["ln_g"], W["ln_b"], tm, "conformer")
    cat = jnp.concatenate([_unheads(o, LP).astype(BF16), u], axis=1)
    h1 = _mm(cat, W["w_out"], "nn", F32, tmm, 512, "out_proj", add=h0)
    h2, ffn0 = _ffn_fwd(h1, W["ffn_norm"][0:1], W["w_up_g"][0], W["w_up_v"][0], W["ffn_conv_w_p"][0],
                        W["ffn_conv_b"][0:1], W["w_down"][0], tm, tmm, "0")
    h3, dpool = _pool_fwd(h2, W["mix_norm_odd"], W["pool_w"], W["pool_b"], W["pool_scale"], tm, "pool_mixer")
    h4, ffn1 = _ffn_fwd(h3, W["ffn_norm"][1:2], W["w_up_g"][1], W["w_up_v"][1], W["ffn_conv_w_p"][1],
                        W["ffn_conv_b"][1:2], W["w_down"][1], tm, tmm, "1")
    loss, dh4, G["final_norm"] = _loss_head(h4, W["final_norm"], tgt, n_real, tm, "loss_head")

    dh3, g1 = _ffn_bwd(h3, W["ffn_norm"][1:2], W["w_up_g"][1], W["w_up_v"][1], W["ffn_conv_w_p"][1],
                       W["ffn_conv_b"][1:2], W["w_down"][1], ffn1, dh4, tm, tmm, "1")
    dh2, G["pool_w"], G["pool_b"], G["pool_scale"], G["mix_norm_odd"] = _pool_bwd(
        h2, W["mix_norm_odd"], dpool, W["pool_w"], W["pool_b"], W["pool_scale"], dh3, tm, "pool_mixer_bwd")
    dh1, g0 = _ffn_bwd(h1, W["ffn_norm"][0:1], W["w_up_g"][0], W["w_up_v"][0], W["ffn_conv_w_p"][0],
                       W["ffn_conv_b"][0:1], W["w_down"][0], ffn0, dh2, tm, tmm, "0")
    for key in ("gain", "wug", "wuv", "cw", "cb", "wd"):
        G["ffn_" + key] = (g0[key], g1[key])

    dcat = _mm(dh1, W["w_out"], "nt", F32, tmm, 512, "out_proj_dx")
    G["w_out"] = _mm(cat, dh1, "tn", F32, 512, 512, "out_proj_dw")
    dadg, dcw, G["conv_b"], G["ln_g"], G["ln_b"] = _conf_bwd(
        proj, u1, dcat, W["conv_w_p"], W["ln_g"], W["ln_b"], tm, "conformer_bwd")
    G["conv_w"] = dcw[:CONV_K]
    do = _heads(dcat[:, :FOX_W], LP)
    dq, dk, dv, dcq, dck = _attn_bwd(q, k, v, o, do, lse, c_col, c_row, "fox_attention_bwd")
    dc = jnp.pad((dcq.reshape(HEADS, LP) + dck.reshape(HEADS, LP)).T, ((0, 0), (0, LANE - HEADS)))
    dfl, dbf = _fgate_bwd(proj, W["b_f_p"], dc, "forget_gate_bwd")
    G["b_f"] = dbf[:, :HEADS]
    dproj = jnp.concatenate([_unheads(t, LP).astype(BF16) for t in (dq, dk, dv)] + [dadg, dfl], axis=1)
    dn0 = _mm(dproj, W["w_in_p"], "nt", F32, tmm, 512, "in_proj_dx")
    G["w_in_p"] = _mm(n0, dproj, "tn", F32, 512, 384, "in_proj_dw")
    dh0, G["mix_norm_even"] = _rms_bwd(h0, W["mix_norm_even"], dn0, dh1, tm, "mix_norm_even_bwd")
    return loss, dh0, G


def _compute_layout(P):
    w_in = P["w_in"].reshape(D_MODEL, IN_COLS)
    qkv, f, ag = w_in[:, :3 * FOX_W], w_in[:, 3 * FOX_W:3 * FOX_W + HEADS], w_in[:, 3 * FOX_W + HEADS:]
    w_in_p = jnp.concatenate([qkv, ag, f, jnp.zeros((D_MODEL, LANE - HEADS), w_in.dtype)], axis=1).astype(BF16)
    w_up = P["w_up"].astype(BF16)
    return dict(
        mix_norm_even=P["mix_norm_even"].reshape(1, D_MODEL).astype(F32),
        w_in_p=w_in_p,
        b_f_p=jnp.pad(P["b_f"].reshape(1, HEADS).astype(F32), ((0, 0), (0, LANE - HEADS))),
        conv_w_p=jnp.pad(P["conv_w"].reshape(CONV_K, CONV_CH).astype(F32), ((0, CONV_HALO - CONV_K), (0, 0))),
        conv_b=P["conv_b"].reshape(1, CONV_CH).astype(F32),
        ln_g=P["ln_g"].reshape(1, CONV_CH).astype(F32),
        ln_b=P["ln_b"].reshape(1, CONV_CH).astype(F32),
        w_out=P["w_out"].reshape(D_MODEL, D_MODEL).astype(BF16),
        mix_norm_odd=P["mix_norm_odd"].reshape(1, D_MODEL).astype(F32),
        pool_w=P["pool_w"].reshape(len(POOL_WINDOWS), POOL_G, POOL_G).astype(BF16),
        pool_b=P["pool_b"].reshape(1, D_MODEL).astype(F32),
        pool_scale=P["pool_scale"].reshape(1, D_MODEL).astype(F32),
        ffn_norm=P["ffn_norm"].astype(F32),
        w_up_g=w_up[:, :, :D_FF],
        w_up_v=w_up[:, :, D_FF:],
        ffn_conv_w_p=jnp.pad(P["ffn_conv_w"].astype(F32), ((0, 0), (0, 8 - FFN_K), (0, 0))),
        ffn_conv_b=P["ffn_conv_b"].astype(F32),
        w_down=P["w_down"].astype(BF16),
        final_norm=P["final_norm"].reshape(1, D_MODEL).astype(F32),
    )


def _reference_layout(G, dh0):
    gp = G["w_in_p"]
    g_w_in = jnp.concatenate([gp[:, :3 * FOX_W], gp[:, 3 * FOX_W + 2 * CONV_CH:3 * FOX_W + 2 * CONV_CH + HEADS],
                              gp[:, 3 * FOX_W:3 * FOX_W + 2 * CONV_CH]], axis=1)
    return dict(
        meta_tokens=dh0[:N_META],
        mix_norm_even=G["mix_norm_even"],
        w_in=g_w_in[None],
        b_f=G["b_f"],
        conv_w=G["conv_w"][None],
        conv_b=G["conv_b"],
        ln_g=G["ln_g"],
        ln_b=G["ln_b"],
        w_out=G["w_out"][None],
        mix_norm_odd=G["mix_norm_odd"],
        pool_w=G["pool_w"][None],
        pool_b=G["pool_b"].reshape(1, len(POOL_WINDOWS), POOL_G),
        pool_scale=G["pool_scale"],
        ffn_norm=jnp.concatenate(G["ffn_gain"], axis=0),
        w_up=jnp.stack([jnp.concatenate([g, v], axis=1) for g, v in zip(G["ffn_wug"], G["ffn_wuv"])]),
        ffn_conv_w=jnp.stack(G["ffn_cw"]),
        ffn_conv_b=jnp.concatenate(G["ffn_cb"], axis=0),
        w_down=jnp.stack(G["ffn_wd"]),
        final_norm=G["final_norm"].reshape(D_MODEL),
    )


MESH = pl.DeviceIdType.MESH
ANY = pl.BlockSpec(memory_space=pl.ANY)
PACK_COLS = 1024


def _coords():
    return lax.axis_index("x"), lax.axis_index("y"), lax.axis_index("c")


def _other_chips(x, y):
    return [(1 - x, y), (x, 1 - y), (1 - x, 1 - y)]


def _allgather_chips(pack):
    R, C = pack.shape
    R2 = R // 2

    def body(x_ref, o_ref, send_sems, recv_sems, local_sem):
        x, y, c = _coords()
        sibling = (x, y, 1 - c)
        chips = _other_chips(x, y)

        def slot(px, py, half):
            return o_ref.at[2 * px + py, pl.ds(half * R2, R2), :]

        def copy(k, src, dst, to):
            return pltpu.make_async_remote_copy(src_ref=src, dst_ref=dst, send_sem=send_sems.at[k],
                                                recv_sem=recv_sems.at[k], device_id=to, device_id_type=MESH)

        mine = pltpu.make_async_copy(x_ref, o_ref.at[2 * x + y], local_sem)
        mine.start()
        my_half = x_ref.at[pl.ds(c * R2, R2), :]
        first = [copy(j, my_half, slot(x, y, c), (*chip, c)) for j, chip in enumerate(chips)]
        for cp in first:
            cp.start()
        passed = [copy(3 + j, slot(*chip, c), slot(*chip, c), sibling) for j, chip in enumerate(chips)]
        for j, chip in enumerate(chips):
            copy(j, my_half, slot(*chip, c), sibling).wait_recv()
            passed[j].start()
        for j, chip in enumerate(chips):
            copy(3 + j, my_half, slot(*chip, 1 - c), sibling).wait_recv()
        for cp in first + passed:
            cp.wait_send()
        mine.wait()

    return pl.pallas_call(
        body, name="allgather_weights", out_shape=jax.ShapeDtypeStruct((N_CHIPS, R, C), pack.dtype),
        in_specs=[ANY], out_specs=ANY,
        scratch_shapes=[pltpu.SemaphoreType.DMA((6,)), pltpu.SemaphoreType.DMA((6,)), pltpu.SemaphoreType.DMA],
    )(pack)


def _pair_exchange(G):
    n, R, C = G.shape
    R2 = R // 2

    def body(g_ref, o_ref, send_sem, recv_sem):
        x, y, c = _coords()
        src = g_ref.at[pl.ds(0, n), pl.ds((1 - c) * R2, R2), :]
        cp = pltpu.make_async_remote_copy(src_ref=src, dst_ref=o_ref, send_sem=send_sem, recv_sem=recv_sem,
                                          device_id=(x, y, 1 - c), device_id_type=MESH)
        cp.start()
        cp.wait()

    return pl.pallas_call(
        body, name="grad_pair_exchange", out_shape=jax.ShapeDtypeStruct((n, R2, C), G.dtype),
        in_specs=[ANY], out_specs=ANY,
        scratch_shapes=[pltpu.SemaphoreType.DMA, pltpu.SemaphoreType.DMA],
    )(G)


def _row_tile(rows, align, cap):
    best = None
    for t in range(align, min(rows, cap) + 1, align):
        if rows % t == 0:
            best = t
    assert best is not None, (rows, align, cap)
    return best


def _pair_sum(G, recv):
    n, R, C = G.shape
    R2 = R // 2
    tr = _row_tile(R2, 16, 704)
    nrb = R2 // tr
    half = lax.axis_index("c").astype(jnp.int32).reshape(1)

    def body(c_ref, g_ref, r_ref, o_ref):
        o_ref[...] = (g_ref[...] + r_ref[...]).astype(o_ref.dtype)

    return pl.pallas_call(
        body, name="grad_pair_sum", out_shape=jax.ShapeDtypeStruct((n, R2, C), BF16),
        grid_spec=pltpu.PrefetchScalarGridSpec(
            num_scalar_prefetch=1, grid=(n, nrb),
            in_specs=[pl.BlockSpec((None, tr, C), lambda j, i, c_ref: (j, c_ref[0] * nrb + i, 0)),
                      pl.BlockSpec((None, tr, C), lambda j, i, c_ref: (j, i, 0))],
            out_specs=pl.BlockSpec((None, tr, C), lambda j, i, c_ref: (j, i, 0))),
        compiler_params=_params(("parallel", "parallel")))(half, G, recv)


def _chip_exchange(P):
    n, R2, C = P.shape

    def body(p_ref, o_ref, send_sems, recv_sems, local_sem):
        x, y, c = _coords()
        me = 2 * x + y
        chips = _other_chips(x, y)
        mine = pltpu.make_async_copy(p_ref.at[me], o_ref.at[me], local_sem)
        mine.start()
        sends = [pltpu.make_async_remote_copy(
            src_ref=p_ref.at[2 * px + py], dst_ref=o_ref.at[me], send_sem=send_sems.at[k],
            recv_sem=recv_sems.at[k], device_id=(px, py, c), device_id_type=MESH)
            for k, (px, py) in enumerate(chips)]
        for cp in sends:
            cp.start()
        for k, (px, py) in enumerate(chips):
            pltpu.make_async_remote_copy(
                src_ref=p_ref.at[me], dst_ref=o_ref.at[2 * px + py], send_sem=send_sems.at[k],
                recv_sem=recv_sems.at[k], device_id=(px, py, c), device_id_type=MESH).wait_recv()
        for cp in sends:
            cp.wait_send()
        mine.wait()

    return pl.pallas_call(
        body, name="grad_chip_exchange", out_shape=jax.ShapeDtypeStruct((n, R2, C), P.dtype),
        in_specs=[ANY], out_specs=ANY,
        scratch_shapes=[pltpu.SemaphoreType.DMA((3,)), pltpu.SemaphoreType.DMA((3,)), pltpu.SemaphoreType.DMA],
    )(P)


def _chip_sum(X):
    n, R2, C = X.shape
    tr = _row_tile(R2, 16, 704)

    def body(x_ref, o_ref):
        acc = x_ref[0].astype(F32)
        for s in range(1, n):
            acc = acc + x_ref[s].astype(F32)
        o_ref[...] = acc

    return pl.pallas_call(
        body, name="grad_chip_sum", out_shape=jax.ShapeDtypeStruct((R2, C), F32), grid=(R2 // tr,),
        in_specs=[pl.BlockSpec((n, tr, C), lambda i: (0, i, 0))],
        out_specs=pl.BlockSpec((tr, C), lambda i: (i, 0)),
        compiler_params=_params(("parallel",)))(X)


def _pair_allgather(Q):
    R2, C = Q.shape

    def body(q_ref, o_ref, send_sem, recv_sem, local_sem):
        x, y, c = _coords()
        mine = pltpu.make_async_copy(q_ref, o_ref.at[c], local_sem)
        mine.start()
        cp = pltpu.make_async_remote_copy(src_ref=q_ref, dst_ref=o_ref.at[c], send_sem=send_sem,
                                          recv_sem=recv_sem, device_id=(x, y, 1 - c), device_id_type=MESH)
        cp.start()
        pltpu.make_async_remote_copy(src_ref=q_ref, dst_ref=o_ref.at[1 - c], send_sem=send_sem,
                                     recv_sem=recv_sem, device_id=(x, y, 1 - c), device_id_type=MESH).wait_recv()
        cp.wait_send()
        mine.wait()

    return pl.pallas_call(
        body, name="grad_pair_allgather", out_shape=jax.ShapeDtypeStruct((2, R2, C), Q.dtype),
        in_specs=[ANY], out_specs=ANY,
        scratch_shapes=[pltpu.SemaphoreType.DMA, pltpu.SemaphoreType.DMA, pltpu.SemaphoreType.DMA],
    )(Q)


def _allreduce_small(pack):
    Rs, C = pack.shape
    n_dev = 8

    def body(x_ref, o_ref, buf, send_sems, recv_sems):
        x, y, c = _coords()
        me = 4 * x + 2 * y + c
        buf[me] = x_ref[...]
        peers = []
        for rel in range(1, n_dev):
            px = 1 - x if rel & 4 else x
            py = 1 - y if rel & 2 else y
            pc = 1 - c if rel & 1 else c
            peers.append((px, py, pc))
        sends = [pltpu.make_async_remote_copy(
            src_ref=x_ref, dst_ref=buf.at[me], send_sem=send_sems.at[k], recv_sem=recv_sems.at[k],
            device_id=peer, device_id_type=MESH) for k, peer in enumerate(peers)]
        for cp in sends:
            cp.start()
        for k, (px, py, pc) in enumerate(peers):
            pltpu.make_async_remote_copy(
                src_ref=x_ref, dst_ref=buf.at[4 * px + 2 * py + pc], send_sem=send_sems.at[k],
                recv_sem=recv_sems.at[k], device_id=(px, py, pc), device_id_type=MESH).wait_recv()
        for cp in sends:
            cp.wait_send()
        acc = buf[0]
        for d in range(1, n_dev):
            acc = acc + buf[d]
        o_ref[...] = acc

    vm = pl.BlockSpec(memory_space=pltpu.VMEM)
    return pl.pallas_call(
        body, name="allreduce_replicated", out_shape=jax.ShapeDtypeStruct((Rs, C), F32),
        in_specs=[vm], out_specs=vm,
        scratch_shapes=[pltpu.VMEM((n_dev, Rs, C), F32), pltpu.SemaphoreType.DMA((n_dev - 1,)),
                        pltpu.SemaphoreType.DMA((n_dev - 1,))],
    )(pack)


SHARDED = (
    ("w_in", 2, True), ("w_out", 1, True), ("pool_w", 2, True), ("w_up", 2, True), ("w_down", 1, True),
    ("meta_tokens", 1, False), ("mix_norm_odd", 1, False), ("pool_b", 2, False), ("pool_scale", 1, False),
    ("conv_w", 2, False), ("ffn_conv_w", 2, False))
REPLICATED = ("mix_norm_even", "b_f", "conv_b", "ln_g", "ln_b", "ffn_norm", "ffn_conv_b", "final_norm")
PACK_ROW_ALIGN = 32


def _pad_rows(flat, align_rows, cols):
    rows = -(-flat.shape[-1] // cols)
    rows = -(-rows // align_rows) * align_rows
    pad = rows * cols - flat.shape[-1]
    flat = jnp.pad(flat, [(0, 0)] * (flat.ndim - 1) + [(0, pad)])
    return flat.reshape(flat.shape[:-1] + (rows, cols))


def _pack_weight_shards(shards):
    parts = []
    for name, _, as_bf16 in SHARDED:
        w = shards[name].astype(F32).reshape(-1)
        parts.append(w.astype(BF16) if as_bf16 else lax.bitcast_convert_type(w, BF16).reshape(-1))
    return _pad_rows(jnp.concatenate(parts), PACK_ROW_ALIGN, PACK_COLS)


def _unpack_weights(gathered, shards):
    flat = gathered.reshape(N_CHIPS, -1)
    out, off = {}, 0
    for name, axis, as_bf16 in SHARDED:
        shp = shards[name].shape
        n = int(np.prod(shp))
        if as_bf16:
            t = flat[:, off:off + n]
            off += n
        else:
            t = lax.bitcast_convert_type(flat[:, off:off + 2 * n].reshape(N_CHIPS, n, 2), F32)
            off += 2 * n
        t = t.reshape((N_CHIPS,) + shp)
        out[name] = jnp.concatenate([t[j] for j in range(N_CHIPS)], axis=axis)
    return out


def _pack_grad_shards(grads, shards):
    parts = []
    for name, axis, _ in SHARDED:
        g = grads[name].reshape(shards[name].shape[:axis] + (N_CHIPS, shards[name].shape[axis])
                                + shards[name].shape[axis + 1:])
        parts.append(jnp.moveaxis(g, axis, 0).reshape(N_CHIPS, -1))
    return _pad_rows(jnp.concatenate(parts, axis=1), PACK_ROW_ALIGN, PACK_COLS)


def _unpack_grad_shard(reduced, shards):
    flat = reduced.reshape(-1)
    out, off = {}, 0
    for name, _, _ in SHARDED:
        shp = shards[name].shape
        n = int(np.prod(shp))
        out[name] = flat[off:off + n].reshape(shp)
        off += n
    return out


def _pack_replicated(grads, loss):
    parts = [_pad_rows(grads[name].astype(F32).reshape(-1), 1, LANE).reshape(-1) for name in REPLICATED]
    parts.append(_pad_rows(loss.reshape(-1)[:1], 1, LANE).reshape(-1))
    return _pad_rows(jnp.concatenate(parts), 8, LANE)


def _unpack_replicated(reduced, shapes):
    flat = reduced.reshape(-1)
    out, off = {}, 0
    for name in REPLICATED:
        n = int(np.prod(shapes[name]))
        out[name] = flat[off:off + n].reshape(shapes[name])
        off += -(-n // LANE) * LANE
    return out, flat[off]


def _ffn_fwd2(h, W, layer, tm, tmm, host_up=None, host_act=None):
    tag = str(layer)
    n = _rms_fwd(h, W["ffn_norm"][layer:layer + 1], BF16, tm, f"ffn_norm_{tag}")
    up, *g_up = _mm(n, W["w_up"][layer], "nn", F32, tmm, UP_SHARD, f"ffn_up_{tag}", host=host_up) \
        if host_up else (_mm(n, W["w_up"][layer], "nn", F32, tmm, UP_SHARD, f"ffn_up_{tag}"),)
    act, *g_act = _ffn_act_fwd(up, W["ffn_conv_w_p"][layer], W["ffn_conv_b"][layer:layer + 1], tm,
                               f"ffn_act_{tag}", host=host_act) \
        if host_act else (_ffn_act_fwd(up, W["ffn_conv_w_p"][layer], W["ffn_conv_b"][layer:layer + 1], tm,
                                       f"ffn_act_{tag}"),)
    out = _mm(act, W["w_down"][layer], "nn", F32, tm, D_MODEL, f"ffn_down_{tag}", add=h)
    return out, (n, up, act), g_up + g_act


def _ffn_bwd2(h, W, layer, saved, dout, tm, tmm, reduce=None):
    tag = str(layer)
    n, up, act = saved
    parts, comm = [], None
    if reduce is None:
        dact = _mm(dout, W["w_down"][layer], "nt", F32, tmm, UP_SHARD, f"ffn_dact_{tag}")
    else:
        names, fulls, cuts = reduce
        dact, *recv = _mm(dout, W["w_down"][layer], "nt", F32, tmm, UP_SHARD, f"ffn_dact_{tag}",
                          host=(fulls, cuts, "pairx"))
        parts = [_pair_sum2(f, r, cut, PAIR_SUM_BLOCKS[nm], "grad_pair_sum_" + nm)
                 for f, r, cut, nm in zip(fulls, recv, cuts, names)]
        comm = (parts, cuts)
    dwd = _mm(act, dout, "tn", F32, D_FF // 2, 512, f"ffn_dwdown_{tag}")
    dup, dcw, dcb, *others = _ffn_act_bwd(up, dact, W["ffn_conv_w_p"][layer], W["ffn_conv_b"][layer:layer + 1],
                                          tm, f"ffn_act_bwd_{tag}", comm=comm)
    dn = _mm_ffn_dn(dup, W["w_up"][layer], tm, D_MODEL, f"ffn_dn_{tag}")
    dwu = _mm_ffn_dwup(n, dup, 512, D_FF // 2, f"ffn_dwup_{tag}")
    dh, dgain = _rms_bwd(h, W["ffn_norm"][layer:layer + 1], dn, dout, tm, f"ffn_norm_bwd_{tag}")
    return dh, (dwu, dwd), dict(gain=dgain, cw=dcw[:FFN_K], cb=dcb), parts, others


GATHER_FIRST = ("w_in", "small")
GATHER_LATE = ("pool_w", "w_up", "w_down")
HOSTED_FFN = ("w_up1", "w_down1")
HOSTED = ("w_out", "pool_w", "w_up0", "w_down0")
LATE = ("w_in", "small")


def _local_step2(h0, tgt, W, n_real, cut_of):
    LP = h0.shape[0]
    tm, tmm = _row_tiles(LP)
    nb = LP // ATT_BLK
    G = {}
    n0 = _rms_fwd(h0, W["mix_norm_even"], BF16, tm, "mix_norm_even")
    sh = W["late_shards"]
    stage = lambda *names: ([sh[n] for n in names], [cut_of[n] for n in names])
    proj, g_down0 = _mm(n0, W["w_in_p"], "nn", F32, tmm, 896, "in_proj", host=stage("w_down0"))
    c = _fgate_fwd(proj, W["b_f_p"], "forget_gate")
    qT, kT, k_aug, vT = _attn_prep(proj, c, "attention_operands")
    oT, lse, g_pool, g_up0, g_out = _attn_fwd2(qT, k_aug, vT, "fox_attention",
                                               comm=stage("pool_w", "w_up0", "w_out"))
    g_down0, g_pool, g_up0, g_out = _gather_forward(
        [g_down0, g_pool, g_up0, g_out], stage("w_down0", "pool_w", "w_up0", "w_out")[1], "gather_forward_0")
    W = dict(W)
    W.update(pool_w=g_pool, w_up=[g_up0, None], w_down=[g_down0, None], w_out=g_out)
    u1, u = _conf_fwd(proj, W["conv_w_p"], W["conv_b"], W["ln_g"], W["ln_b"], tm, "conformer")
    cat = jnp.concatenate([_attn_rows(oT, 1.0, BF16, "attention_rows"), u], axis=1)
    h1 = _mm(cat, W["w_out"], "nn", F32, tmm, D_MODEL, "out_proj", add=h0)
    h2, ffn0, (g_down1, g_up1) = _ffn_fwd2(h1, W, 0, tm, tmm, host_up=stage("w_down1"), host_act=stage("w_up1"))
    g_down1, g_up1 = _gather_forward([g_down1, g_up1], stage("w_down1", "w_up1")[1], "gather_forward_1")
    W.update(w_up=[g_up0, g_up1], w_down=[g_down0, g_down1])
    h3, dpool = _pool_fwd(h2, W["mix_norm_odd"], W["pool_w"], W["pool_b"], W["pool_scale"], tm, "pool_mixer")
    h4, ffn1, _ = _ffn_fwd2(h3, W, 1, tm, tmm)
    loss, dh4, G["final_norm"] = _loss_head(h4, W["final_norm"], tgt, n_real, tm, "loss_head")

    dh3, (G["w_up1"], G["w_down1"]), g1, _, _ = _ffn_bwd2(h3, W, 1, ffn1, dh4, tm, tmm)
    dh2, G["pool_w"], G["pool_b"], G["pool_scale"], G["mix_norm_odd"] = _pool_bwd(
        h2, W["mix_norm_odd"], dpool, W["pool_w"], W["pool_b"], W["pool_scale"], dh3, tm, "pool_mixer_bwd")
    cuts1 = [cut_of[n] for n in HOSTED_FFN]
    dh1, (G["w_up0"], G["w_down0"]), g0, parts1, others1 = _ffn_bwd2(
        h1, W, 0, ffn0, dh2, tm, tmm, reduce=(HOSTED_FFN, [G[n] for n in HOSTED_FFN], cuts1))
    G["ffn_norm"] = jnp.concatenate([g0["gain"], g1["gain"]], axis=0)
    G["ffn_conv_w"] = jnp.stack([g0["cw"], g1["cw"]])
    G["ffn_conv_b"] = jnp.concatenate([g0["cb"], g1["cb"]], axis=0)

    dcat = _mm(dh1, W["w_out"], "nt", F32, tmm, D_MODEL, "out_proj_dx")
    G["w_out"] = _mm(cat, dh1, "tn", F32, 512, D_MODEL, "out_proj_dw")
    hcuts = [cut_of[n] for n in HOSTED]
    hfull = [G[n] for n in HOSTED]
    dadg, dcw, G["conv_b"], G["ln_g"], G["ln_b"], *hrecv = _conf_bwd(
        proj, u1, dcat, W["conv_w_p"], W["ln_g"], W["ln_b"], tm, "conformer_bwd", host=(hfull, hcuts, "pairx"))
    G["conv_w"] = dcw[:CONV_K]
    doT = _attn_cols(dcat, "attention_do_cols")
    hparts = [_pair_sum2(f, r, cut, PAIR_SUM_BLOCKS[n], "grad_pair_sum_" + n)
              for f, r, cut, n in zip(hfull, hrecv, hcuts, HOSTED)]
    dqT, dkT, dvT, *hothers = _attn_bwd2(qT, kT, k_aug, vT, oT, doT, lse, "fox_attention_bwd",
                                         comm=(hparts, hcuts))
    dfl, dbf = _fgate_bwd(proj, W["b_f_p"], _attn_dc(dqT, dkT, "attention_dc"), "forget_gate_bwd")
    G["b_f"] = dbf[:, :HEADS]
    dproj = jnp.concatenate([_attn_rows(dqT, HEAD_DIM ** -0.5, BF16, "attention_dq_rows"),
                             _attn_rows(dkT, 1.0, BF16, "attention_dk_rows"),
                             _attn_rows(dvT, 1.0, BF16, "attention_dv_rows"), dadg, dfl], axis=1)
    dn0 = _mm(dproj, W["w_in_p"], "nt", F32, tmm, D_MODEL, "in_proj_dx")
    G["w_in_p"] = _mm(n0, dproj, "tn", F32, 512, 896, "in_proj_dw")
    dh0, G["mix_norm_even"] = _rms_bwd(h0, W["mix_norm_even"], dn0, dh1, tm, "mix_norm_even_bwd")
    parts = dict(zip(HOSTED_FFN + HOSTED, parts1 + hparts))
    others = dict(zip(HOSTED_FFN + HOSTED, list(others1) + list(hothers)))
    return loss, dh0, G, parts, others


class _Cut:
    def __init__(self, full_shape, chip_dim, half_dim):
        self.full = tuple(full_shape)
        self.chip_dim, self.half_dim = chip_dim, half_dim
        self.chip_size = full_shape[chip_dim] // N_CHIPS
        self.half_size = full_shape[half_dim] // 2
        assert chip_dim != half_dim

    def shape(self, chip=False, half=False):
        s = list(self.full)
        if chip:
            s[self.chip_dim] = self.chip_size
        if half:
            s[self.half_dim] = self.half_size
        return tuple(s)

    def region(self, ref, chip=None, half=None):
        idx = [pl.ds(0, n) for n in ref.shape]
        if chip is not None:
            idx[self.chip_dim] = pl.ds(chip * self.chip_size, self.chip_size)
        if half is not None:
            idx[self.half_dim] = pl.ds(half * self.half_size, self.half_size)
        return ref.at[tuple(idx)]


SMALL_SHARDED = ("meta_tokens", "mix_norm_odd", "pool_b", "pool_scale", "conv_w", "ffn_conv_w")
SMALL_ROWS = 144


def _cuts():
    return {
        "w_in": _Cut((N_CHIPS, D_MODEL, IN_SHARD), 0, 1),
        "w_out": _Cut((D_MODEL, D_MODEL), 0, 1),
        "pool_w": _Cut((len(POOL_WINDOWS), POOL_G, POOL_G), 1, 0),
        "w_up": _Cut((2, D_MODEL, 2 * D_FF), 2, 1),
        "w_down": _Cut((2, D_FF, D_MODEL), 1, 2),
        "small": _Cut((N_CHIPS, SMALL_ROWS, LANE), 0, 1),
        "w_up0": _Cut((D_MODEL, 2 * D_FF), 1, 0), "w_up1": _Cut((D_MODEL, 2 * D_FF), 1, 0),
        "w_down0": _Cut((D_FF, D_MODEL), 0, 1), "w_down1": _Cut((D_FF, D_MODEL), 0, 1),
    }


COMM_ORDER = ("w_in", "w_out", "pool_w", "w_up", "w_down", "small")


def _remote(src, dst, send_sems, recv_sems, k, to):
    return pltpu.make_async_remote_copy(src_ref=src, dst_ref=dst, send_sem=send_sems.at[k],
                                        recv_sem=recv_sems.at[k], device_id=to, device_id_type=MESH)


def _gather_weights(shards, cuts):
    n = len(shards)

    def body(*refs):
        srcs, outs = refs[:n], refs[n:2 * n]
        send_sems, recv_sems = refs[2 * n:]
        x, y, c = _coords()
        me = 2 * x + y
        sibling = (x, y, 1 - c)
        chips = _other_chips(x, y)
        sends = []
        for t, cut in enumerate(cuts):
            push = _remote(srcs[t], cut.region(outs[t], chip=me), send_sems, recv_sems, 7 * t, sibling)
            push.start()
            sends.append(push)
            for kk, chip in enumerate(chips):
                cp = _remote(cut.region(srcs[t], half=c), cut.region(outs[t], chip=me, half=c),
                             send_sems, recv_sems, 7 * t + 1 + kk, (*chip, c))
                cp.start()
                sends.append(cp)
        for t, cut in enumerate(cuts):
            for kk, (px, py) in enumerate(chips):
                landed = cut.region(outs[t], chip=2 * px + py, half=c)
                _remote(landed, landed, send_sems, recv_sems, 7 * t + 1 + kk, sibling).wait_recv()
                fwd = _remote(landed, landed, send_sems, recv_sems, 7 * t + 4 + kk, sibling)
                fwd.start()
                sends.append(fwd)
        for t, cut in enumerate(cuts):
            mine = cut.region(outs[t], chip=me)
            _remote(mine, mine, send_sems, recv_sems, 7 * t, sibling).wait_recv()
            for kk, (px, py) in enumerate(chips):
                other = cut.region(outs[t], chip=2 * px + py, half=1 - c)
                _remote(other, other, send_sems, recv_sems, 7 * t + 4 + kk, sibling).wait_recv()
        for cp in sends:
            cp.wait_send()

    return pl.pallas_call(
        body, name="gather_weights",
        out_shape=tuple(jax.ShapeDtypeStruct(cut.full, s.dtype) for cut, s in zip(cuts, shards)),
        in_specs=[ANY] * n, out_specs=tuple([ANY] * n),
        scratch_shapes=[pltpu.SemaphoreType.DMA((7 * n,)), pltpu.SemaphoreType.DMA((7 * n,))],
    )(*shards)


def _gather_first_ops(srcs, outs, send_sems, recv_sems, cuts):
    x, y, c = _coords()
    me = 2 * x + y
    sibling = (x, y, 1 - c)
    chips = _other_chips(x, y)

    def copies():
        out = []
        for t, cut in enumerate(cuts):
            out.append(_remote(srcs[t], cut.region(outs[t], chip=me), send_sems, recv_sems, 4 * t, sibling))
            for kk, chip in enumerate(chips):
                out.append(_remote(cut.region(srcs[t], half=c), cut.region(outs[t], chip=me, half=c),
                                   send_sems, recv_sems, 4 * t + 1 + kk, (*chip, c)))
        return out

    def start():
        for cp in copies():
            cp.start()

    def wait():
        for t, cut in enumerate(cuts):
            mine = cut.region(outs[t], chip=me)
            _remote(mine, mine, send_sems, recv_sems, 4 * t, sibling).wait_recv()
            for kk, (px, py) in enumerate(chips):
                landed = cut.region(outs[t], chip=2 * px + py, half=c)
                _remote(landed, landed, send_sems, recv_sems, 4 * t + 1 + kk, sibling).wait_recv()
        for cp in copies():
            cp.wait_send()

    return start, wait


def _pair_exchange_ops(srcs, outs, send_sems, recv_sems, cuts):
    x, y, c = _coords()

    def copies():
        return [_remote(cut.region(srcs[t], half=1 - c), outs[t], send_sems, recv_sems, t, (x, y, 1 - c))
                for t, cut in enumerate(cuts)]

    def start():
        for cp in copies():
            cp.start()

    def wait():
        for cp in copies():
            cp.wait()

    return start, wait


def _host_plan(host):
    arrays, cuts = host[0], host[1]
    if len(host) > 2 and host[2] == "pairx":
        return (tuple(jax.ShapeDtypeStruct(cut.shape(half=True), a.dtype) for cut, a in zip(cuts, arrays)),
                len(arrays), _pair_exchange_ops)
    return (tuple(jax.ShapeDtypeStruct(cut.full, a.dtype) for cut, a in zip(cuts, arrays)),
            4 * len(arrays), _gather_first_ops)


def _gather_forward(fulls, cuts, name):
    n = len(fulls)

    def body(*refs):
        outs = refs[n:2 * n]
        send_sems, recv_sems = refs[2 * n:]
        x, y, c = _coords()
        sibling = (x, y, 1 - c)
        chips = _other_chips(x, y)
        sends = []
        for t, cut in enumerate(cuts):
            for kk, (px, py) in enumerate(chips):
                landed = cut.region(outs[t], chip=2 * px + py, half=c)
                cp = _remote(landed, landed, send_sems, recv_sems, 3 * t + kk, sibling)
                cp.start()
                sends.append(cp)
        for t, cut in enumerate(cuts):
            for kk, (px, py) in enumerate(chips):
                other = cut.region(outs[t], chip=2 * px + py, half=1 - c)
                _remote(other, other, send_sems, recv_sems, 3 * t + kk, sibling).wait_recv()
        for cp in sends:
            cp.wait_send()

    return pl.pallas_call(
        body, name=name,
        out_shape=tuple(jax.ShapeDtypeStruct(f.shape, f.dtype) for f in fulls),
        in_specs=[ANY] * n, out_specs=tuple([ANY] * n), input_output_aliases={t: t for t in range(n)},
        scratch_shapes=[pltpu.SemaphoreType.DMA((3 * n,)), pltpu.SemaphoreType.DMA((3 * n,))],
    )(*fulls)


def _pair_exchange2(fulls, cuts, name):
    n = len(fulls)

    def body(*refs):
        srcs, outs = refs[:n], refs[n:2 * n]
        send_sems, recv_sems = refs[2 * n:]
        x, y, c = _coords()
        cps = [_remote(cut.region(srcs[t], half=1 - c), outs[t], send_sems, recv_sems, t, (x, y, 1 - c))
               for t, cut in enumerate(cuts)]
        for cp in cps:
            cp.start()
        for cp in cps:
            cp.wait()

    return pl.pallas_call(
        body, name=name,
        out_shape=tuple(jax.ShapeDtypeStruct(cut.shape(half=True), f.dtype) for cut, f in zip(cuts, fulls)),
        in_specs=[ANY] * n, out_specs=tuple([ANY] * n),
        scratch_shapes=[pltpu.SemaphoreType.DMA((n,)), pltpu.SemaphoreType.DMA((n,))],
    )(*fulls)


def _grid_of(shape, blk):
    assert all(s % b == 0 for s, b in zip(shape, blk)), (shape, blk)
    return tuple(s // b for s, b in zip(shape, blk))


def _pair_sum2(full, recv, cut, blk, name):
    hshape = cut.shape(half=True)
    grid = _grid_of(hshape, blk)
    hb = cut.half_size // blk[cut.half_dim]
    hd = cut.half_dim
    pos = jnp.stack([lax.axis_index("c")]).astype(jnp.int32)

    def full_idx(*a):
        ids, p = list(a[:-1]), a[-1]
        ids[hd] = ids[hd] + p[0] * hb
        return tuple(ids)

    def body(p_ref, f_ref, r_ref, o_ref):
        o_ref[...] = (f_ref[...] + r_ref[...]).astype(o_ref.dtype)

    return pl.pallas_call(
        body, name=name, out_shape=jax.ShapeDtypeStruct(hshape, BF16),
        grid_spec=pltpu.PrefetchScalarGridSpec(
            num_scalar_prefetch=1, grid=grid,
            in_specs=[pl.BlockSpec(blk, full_idx), pl.BlockSpec(blk, lambda *a: tuple(a[:-1]))],
            out_specs=pl.BlockSpec(blk, lambda *a: tuple(a[:-1]))),
        compiler_params=_params(("parallel",) * len(grid)))(pos, full, recv)


def _chip_exchange_ops(srcs, outs, send_sems, recv_sems, cuts):
    x, y, c = _coords()
    me = 2 * x + y
    chips = _other_chips(x, y)

    def copies():
        return [_remote(cut.region(srcs[t], chip=2 * px + py), outs[t].at[me], send_sems, recv_sems,
                        3 * t + kk, (px, py, c))
                for t, cut in enumerate(cuts) for kk, (px, py) in enumerate(chips)]

    def start():
        for cp in copies():
            cp.start()

    def wait():
        for t, cut in enumerate(cuts):
            for kk, (px, py) in enumerate(chips):
                slot = outs[t].at[2 * px + py]
                _remote(slot, slot, send_sems, recv_sems, 3 * t + kk, (px, py, c)).wait_recv()
        for cp in copies():
            cp.wait_send()

    return start, wait


def _chip_exchange_shapes(parts, cuts):
    return tuple(jax.ShapeDtypeStruct((N_CHIPS,) + cut.shape(chip=True, half=True), p.dtype)
                 for cut, p in zip(cuts, parts))


def _chip_exchange2(parts, cuts):
    n = len(parts)

    def body(*refs):
        start, wait = _chip_exchange_ops(refs[:n], refs[n:2 * n], refs[2 * n], refs[2 * n + 1], cuts)
        start()
        wait()

    return pl.pallas_call(
        body, name="grad_chip_exchange",
        out_shape=tuple(jax.ShapeDtypeStruct((N_CHIPS,) + cut.shape(chip=True, half=True), p.dtype)
                        for cut, p in zip(cuts, parts)),
        in_specs=[ANY] * n, out_specs=tuple([ANY] * n),
        scratch_shapes=[pltpu.SemaphoreType.DMA((3 * n,)), pltpu.SemaphoreType.DMA((3 * n,))],
    )(*parts)


def _chip_sum2(part, recv, cut, blk, name, stacked=None):
    bshape = cut.shape(chip=True, half=True)
    grid = _grid_of(bshape, blk)
    cb = cut.chip_size // blk[cut.chip_dim]
    hb = cut.half_size // blk[cut.half_dim]
    cd, hd = cut.chip_dim, cut.half_dim
    x, y, c = _coords()
    slots = [2 * px + py for px, py in _other_chips(x, y)]
    pos = jnp.stack([c, 2 * x + y] + slots).astype(jnp.int32)

    def part_idx(*a):
        ids, p = list(a[:-1]), a[-1]
        ids[cd] = ids[cd] + p[1] * cb
        return tuple(ids)

    def recv_idx(kk):
        return lambda *a: (a[-1][2 + kk],) + tuple(a[:-1])

    def out_idx(*a):
        ids, p = list(a[:-1]), a[-1]
        ids[hd] = ids[hd] + p[0] * hb
        return tuple(ids)

    def body(p_ref, own_ref, r0_ref, r1_ref, r2_ref, *rest):
        acc = own_ref[...].astype(F32)
        for r_ref in (r0_ref, r1_ref, r2_ref):
            acc = acc + r_ref[...].astype(F32)
        rest[-1][...] = acc

    in_specs = [pl.BlockSpec(blk, part_idx)] + [pl.BlockSpec((None,) + blk, recv_idx(kk)) for kk in range(3)]
    args = [pos, part, recv, recv, recv]
    aliases = {}
    if stacked is None:
        out_shape = jax.ShapeDtypeStruct(cut.shape(chip=True), F32)
        out_spec = pl.BlockSpec(blk, out_idx)
    else:
        lead, n_lead, into = stacked
        out_shape = jax.ShapeDtypeStruct((n_lead,) + cut.shape(chip=True), F32)
        out_spec = pl.BlockSpec((None,) + blk, lambda *a: (lead,) + out_idx(*a))
        if into is not None:
            in_specs.append(pl.BlockSpec(memory_space=pl.ANY))
            args.append(into)
            aliases = {5: 0}
    return pl.pallas_call(
        body, name=name, out_shape=out_shape,
        grid_spec=pltpu.PrefetchScalarGridSpec(num_scalar_prefetch=1, grid=grid, in_specs=in_specs,
                                               out_specs=out_spec),
        input_output_aliases=aliases, compiler_params=_params(("parallel",) * len(grid)))(*args)


def _pair_swap2(blocks, cuts):
    n = len(blocks)

    def body(*refs):
        outs = refs[n:2 * n]
        send_sems, recv_sems = refs[2 * n:]
        x, y, c = _coords()
        cps = []
        for t, cut in enumerate(cuts):
            mine = cut.region(outs[t], half=c)
            cp = _remote(mine, mine, send_sems, recv_sems, t, (x, y, 1 - c))
            cp.start()
            cps.append(cp)
        for t, cut in enumerate(cuts):
            theirs = cut.region(outs[t], half=1 - c)
            _remote(theirs, theirs, send_sems, recv_sems, t, (x, y, 1 - c)).wait_recv()
        for cp in cps:
            cp.wait_send()

    return pl.pallas_call(
        body, name="grad_pair_swap",
        out_shape=tuple(jax.ShapeDtypeStruct(b.shape, b.dtype) for b in blocks),
        in_specs=[ANY] * n, out_specs=tuple([ANY] * n),
        input_output_aliases={t: t for t in range(n)},
        scratch_shapes=[pltpu.SemaphoreType.DMA((n,)), pltpu.SemaphoreType.DMA((n,))],
    )(*blocks)


PAIR_SUM_BLOCKS = {"w_in": (1, 512, IN_SHARD), "w_out": (512, 512), "pool_w": (1, POOL_G, POOL_G),
                   "w_up0": (64, 2 * D_FF), "w_up1": (64, 2 * D_FF), "w_down0": (704, 512), "w_down1": (704, 512),
                   "small": (N_CHIPS, SMALL_ROWS // 2, LANE)}
CHIP_SUM_BLOCKS = {"w_in": (1, 512, IN_SHARD), "w_out": (256, 512), "pool_w": (2, 64, POOL_G),
                   "w_up0": (128, UP_SHARD), "w_up1": (128, UP_SHARD),
                   "w_down0": (DOWN_SHARD, 512), "w_down1": (DOWN_SHARD, 512),
                   "small": (1, SMALL_ROWS // 2, LANE)}


def _pack_small(P):
    parts = []
    for name in SMALL_SHARDED:
        t = P[name]
        parts.append(t.astype(F32))
    return parts


def _small_rows(t, lead):
    flat = t.reshape(lead + (-1,))
    pad = -flat.shape[-1] % LANE
    return jnp.pad(flat, [(0, 0)] * len(lead) + [(0, pad)]).reshape(lead + (-1, LANE))


def _pack_small_shards(shards):
    rows = jnp.concatenate([_small_rows(shards[n].astype(F32), ()) for n in SMALL_SHARDED], axis=0)
    return jnp.pad(rows, ((0, SMALL_ROWS - rows.shape[0]), (0, 0)))[None]


def _unpack_small(pack, shards, axes):
    out, off = {}, 0
    nchip = pack.shape[0]
    for name in SMALL_SHARDED:
        shp = shards[name].shape
        cnt = int(np.prod(shp))
        rows = -(-cnt // LANE)
        t = pack[:, off:off + rows].reshape(nchip, -1)[:, :cnt].reshape((nchip,) + shp)
        out[name] = jnp.concatenate([t[j] for j in range(nchip)], axis=axes[name])
        off += rows
    return out


def _pack_small_grads(grads, shards, axes):
    parts = []
    for name in SMALL_SHARDED:
        shp, ax = shards[name].shape, axes[name]
        g = grads[name].reshape(shp[:ax] + (N_CHIPS, shp[ax]) + shp[ax + 1:])
        parts.append(_small_rows(jnp.moveaxis(g, ax, 0), (N_CHIPS,)))
    rows = jnp.concatenate(parts, axis=1)
    return jnp.pad(rows, ((0, 0), (0, SMALL_ROWS - rows.shape[1]), (0, 0)))


SMALL_AXES = {"meta_tokens": 1, "mix_norm_odd": 1, "pool_b": 2, "pool_scale": 1, "conv_w": 2, "ffn_conv_w": 2}


WEIGHT_NAMES = ("meta_tokens", "mix_norm_even", "w_in", "b_f", "conv_w", "conv_b", "ln_g", "ln_b", "w_out",
                "mix_norm_odd", "pool_w", "pool_b", "pool_scale", "ffn_norm", "w_up", "ffn_conv_w",
                "ffn_conv_b", "w_down", "final_norm")


def kernel(x, meta_tokens, mix_norm_even, w_in, b_f, conv_w, conv_b, ln_g, ln_b, w_out, mix_norm_odd, pool_w, pool_b, pool_scale, ffn_norm, w_up, ffn_conv_w, ffn_conv_b, w_down, final_norm, loss_target, m_meta_tokens, m_mix_norm_even, m_w_in, m_b_f, m_conv_w, m_conv_b, m_ln_g, m_ln_b, m_w_out, m_mix_norm_odd, m_pool_w, m_pool_b, m_pool_scale, m_ffn_norm, m_w_up, m_ffn_conv_w, m_ffn_conv_b, m_w_down, m_final_norm, v_meta_tokens, v_mix_norm_even, v_w_in, v_b_f, v_conv_w, v_conv_b, v_ln_g, v_ln_b, v_w_out, v_mix_norm_odd, v_pool_w, v_pool_b, v_pool_scale, v_ffn_norm, v_w_up, v_ffn_conv_w, v_ffn_conv_b, v_w_down, v_final_norm):
    given = dict(locals())
    w_loc = {n: given[n] for n in WEIGHT_NAMES}
    m_loc = {n: given["m_" + n] for n in WEIGHT_NAMES}
    v_loc = {n: given["v_" + n] for n in WEIGHT_NAMES}
    cut_of = _cuts()
    cuts = [cut_of[n] for n in COMM_ORDER]
    big = ("w_in", "w_out", "pool_w", "w_up", "w_down")
    small_shards = {n: w_loc[n] for n in SMALL_SHARDED}

    shard_of = {n: w_loc[n].astype(BF16).reshape(cut_of[n].shape(chip=True)) for n in big}
    shard_of["small"] = _pack_small_shards(small_shards)
    g_in, g_small = _gather_weights([shard_of[n] for n in GATHER_FIRST], [cut_of[n] for n in GATHER_FIRST])
    g_out = None
    g_pool = g_up = g_down = None
    full = _unpack_small(g_small, small_shards, SMALL_AXES)
    full.update({n: w_loc[n] for n in REPLICATED})
    w_in_full = g_in.transpose(1, 0, 2).reshape(D_MODEL, IN_COLS)
    qkv, f, ag = (w_in_full[:, :3 * FOX_W], w_in_full[:, 3 * FOX_W:3 * FOX_W + HEADS],
                  w_in_full[:, 3 * FOX_W + HEADS:])
    W = dict(
        mix_norm_even=full["mix_norm_even"].reshape(1, D_MODEL),
        w_in_p=jnp.concatenate([qkv, ag, f, jnp.zeros((D_MODEL, LANE - HEADS), BF16)], axis=1),
        b_f_p=jnp.pad(full["b_f"].reshape(1, HEADS), ((0, 0), (0, LANE - HEADS))),
        conv_w_p=jnp.pad(full["conv_w"].reshape(CONV_K, CONV_CH), ((0, CONV_HALO - CONV_K), (0, 0))),
        conv_b=full["conv_b"].reshape(1, CONV_CH), ln_g=full["ln_g"].reshape(1, CONV_CH),
        ln_b=full["ln_b"].reshape(1, CONV_CH), w_out=g_out,
        mix_norm_odd=full["mix_norm_odd"].reshape(1, D_MODEL), pool_w=g_pool,
        pool_b=full["pool_b"].reshape(1, D_MODEL), pool_scale=full["pool_scale"].reshape(1, D_MODEL),
        ffn_norm=full["ffn_norm"], w_up=g_up,
        ffn_conv_w_p=jnp.pad(full["ffn_conv_w"], ((0, 0), (0, 8 - FFN_K), (0, 0))),
        ffn_conv_b=full["ffn_conv_b"], w_down=g_down, final_norm=full["final_norm"].reshape(1, D_MODEL),
        late_shards=dict(pool_w=shard_of["pool_w"], w_out=shard_of["w_out"],
                         w_up0=shard_of["w_up"][0], w_up1=shard_of["w_up"][1],
                         w_down0=shard_of["w_down"][0], w_down1=shard_of["w_down"][1]))

    seq = x.shape[1]
    n_real = N_META + seq
    LP = -(-n_real // ATT_BLK) * ATT_BLK
    tail = jnp.zeros((LP - n_real, D_MODEL), F32)
    h0 = jnp.concatenate([full["meta_tokens"], x[0], tail], axis=0)
    tgt = jnp.concatenate([jnp.zeros((N_META, D_MODEL), F32), loss_target[0], tail], axis=0)
    loss_loc, dh0, G, parts, others = _local_step2(h0, tgt, W, n_real, cut_of)
    grad_x = dh0[N_META:n_real][None]
    G["meta_tokens"] = dh0[:N_META]

    rep_shapes = {n: w_loc[n].shape for n in REPLICATED}
    G["final_norm"] = G["final_norm"].reshape(D_MODEL)
    rep, loss = _unpack_replicated(_allreduce_small(_pack_replicated(G, loss_loc)), rep_shapes)

    gp = G["w_in_p"]
    g_w_in = jnp.concatenate([gp[:, :3 * FOX_W], gp[:, 3 * FOX_W + 2 * CONV_CH:3 * FOX_W + 2 * CONV_CH + HEADS],
                              gp[:, 3 * FOX_W:3 * FOX_W + 2 * CONV_CH]], axis=1)
    lcuts = [cut_of[n] for n in LATE]
    lfull = [g_w_in.reshape(D_MODEL, N_CHIPS, IN_SHARD).transpose(1, 0, 2),
             _pack_small_grads(G, small_shards, SMALL_AXES)]
    lrecv = _pair_exchange2(lfull, lcuts, "grad_pair_exchange_late")
    lparts = [_pair_sum2(f, r, cut, PAIR_SUM_BLOCKS[n], "grad_pair_sum_" + n)
              for f, r, cut, n in zip(lfull, lrecv, lcuts, LATE)]
    parts.update(zip(LATE, lparts))
    others.update(zip(LATE, _chip_exchange2(lparts, lcuts)))
    def chip_sum(n, stacked=None):
        return _chip_sum2(parts[n], others[n], cut_of[n], CHIP_SUM_BLOCKS[n], "grad_chip_sum_" + n, stacked=stacked)

    blocks = []
    for n in COMM_ORDER:
        if n in ("w_up", "w_down"):
            blocks.append(chip_sum(n + "1", stacked=(1, 2, chip_sum(n + "0", stacked=(0, 2, None)))))
        else:
            blocks.append(chip_sum(n))
    blocks = _pair_swap2(blocks, cuts)
    gsh = {n: b.reshape(w_loc[n].shape) for n, b in zip(big, blocks[:5])}
    gsh.update(_unpack_small(blocks[5], small_shards, SMALL_AXES))
    sharded = set(big) | set(SMALL_SHARDED)

    grad_w = {n: (gsh[n] if n in sharded else rep[n]) for n in WEIGHT_NAMES}
    delta, new_m, new_v = {}, {}, {}
    for n in WEIGHT_NAMES:
        delta[n], new_m[n], new_v[n] = _adamw(w_loc[n], grad_w[n], m_loc[n], v_loc[n], "adamw_" + n)
    return (loss, grad_x, *[grad_w[n] for n in WEIGHT_NAMES], *[delta[n] for n in WEIGHT_NAMES],
            *[new_m[n] for n in WEIGHT_NAMES], *[new_v[n] for n in WEIGHT_NAMES])
```

```python
import functools

import numpy as np
import jax
import jax.numpy as jnp
from jax import lax
from jax.experimental import pallas as pl
from jax.experimental.pallas import tpu as pltpu

F32 = jnp.float32
BF16 = jnp.bfloat16

D_MODEL = 1024
N_META = 16
SEQ = 2048
HEADS = 8
HEAD_DIM = 64
FOX_W = HEADS * HEAD_DIM
CONV_CH = 512
CONV_K = 31
D_FF = 2816
POOL_WINDOWS = (2, 4, 8, 16)
POOL_G = 256
RMS_EPS = 1e-6
LN_EPS = 1e-5
IN_COLS = 3 * FOX_W + HEADS + 2 * CONV_CH
IN_COLS_P = 3 * FOX_W + 2 * CONV_CH + 128
F_COL_BLK = (3 * FOX_W + 2 * CONV_CH) // 128
N_CHIPS = 4
IN_SHARD = IN_COLS // N_CHIPS
UP_SHARD = 2 * D_FF // N_CHIPS
DOWN_SHARD = D_FF // N_CHIPS

ADAM_LR = 0.001
ADAM_B1 = 0.9
ADAM_B2 = 0.999
ADAM_EPS = 1e-08
ADAM_WD = 0.01
ADAM_STEP = 10

LANE = 128
ATT_BLK = 128
VMEM_LIMIT = 56 * 1024 * 1024

NEG = -1e30


def _sigmoid(x):
    return 0.5 * jnp.tanh(0.5 * x) + 0.5


def _sigmoid_tail(x):
    return 1.0 / (1.0 + jnp.exp(-x))


def _params(sem=None):
    return pltpu.CompilerParams(dimension_semantics=sem, vmem_limit_bytes=VMEM_LIMIT)


def _sub_rows(tm):
    best = 8
    for s in range(8, 137, 8):
        if tm % s == 0:
            best = s
    return best


def _mm(a, b, mode, out_dtype, tm, tn, name, add=None, a_lead=None, b_lead=None, out=None, host=None):
    a_shape = a.shape if a_lead is None else a.shape[1:]
    b_shape = b.shape if b_lead is None else b.shape[1:]
    if mode == "nn":
        (M, K), (K2, N) = a_shape, b_shape
        dims = (((1,), (0,)), ((), ()))
        a_blk, a_idx = (tm, K), (lambda i, j: (i, 0))
        b_blk, b_idx = (K, tn), (lambda i, j: (0, j))
    elif mode == "nt":
        (M, K), (N, K2) = a_shape, b_shape
        dims = (((1,), (1,)), ((), ()))
        a_blk, a_idx = (tm, K), (lambda i, j: (i, 0))
        b_blk, b_idx = (tn, K), (lambda i, j: (j, 0))
    else:
        (K, M), (K2, N) = a_shape, b_shape
        dims = (((0,), (0,)), ((), ()))
        a_blk, a_idx = (K, tm), (lambda i, j: (0, i))
        b_blk, b_idx = (K, tn), (lambda i, j: (0, j))
    assert K == K2 and M % tm == 0 and N % tn == 0, (name, a.shape, b.shape, tm, tn)
    gm, gn = M // tm, N // tn
    a_bytes = M * K * a.dtype.itemsize
    b_bytes = N * K * b.dtype.itemsize
    m_outer = a_bytes + b_bytes * gm <= b_bytes + a_bytes * gn
    if m_outer:
        grid = (gm, gn)
        wrap = lambda f: f
    else:
        grid = (gn, gm)
        wrap = lambda f: (lambda j, i: f(i, j))

    def lead(blk, idx, at):
        if at is None:
            return pl.BlockSpec(blk, wrap(idx))
        return pl.BlockSpec((None,) + blk, wrap(lambda i, j: (at,) + idx(i, j)))

    o_idx = lambda i, j: (i, j)
    in_specs = [lead(a_blk, a_idx, a_lead), lead(b_blk, b_idx, b_lead)]
    args = [a, b]
    if add is not None:
        in_specs.append(pl.BlockSpec((tm, tn), wrap(o_idx)))
        args.append(add)
    aliases = {}
    if out is None:
        out_shape = jax.ShapeDtypeStruct((M, N), out_dtype)
        out_spec = pl.BlockSpec((tm, tn), wrap(o_idx))
    else:
        o_lead, n_lead, into = out
        out_shape = jax.ShapeDtypeStruct((n_lead, M, N), out_dtype)
        out_spec = lead((tm, tn), o_idx, o_lead)
        if into is not None:
            aliases = {len(args): 0}
            in_specs.append(pl.BlockSpec(memory_space=pl.ANY))
            args.append(into)
    has_add = add is not None
    n_host = 0 if host is None else len(host[0])
    n_in = len(args)
    scratch = []
    if n_host:
        host_shapes, host_sems, host_ops = _host_plan(host)
        in_specs = in_specs + [pl.BlockSpec(memory_space=pl.ANY)] * n_host
        args = args + list(host[0])
        out_shape = (out_shape,) + host_shapes
        out_spec = (out_spec,) + (pl.BlockSpec(memory_space=pl.ANY),) * n_host
        scratch = [pltpu.SemaphoreType.DMA((host_sems,)), pltpu.SemaphoreType.DMA((host_sems,))]

    def body(*refs):
        a_ref, b_ref = refs[0], refs[1]
        o_ref = refs[n_in + n_host]
        if n_host:
            start, wait = host_ops(refs[n_in:n_in + n_host], refs[n_in + n_host + 1:n_in + 2 * n_host + 1],
                                   refs[n_in + 2 * n_host + 1], refs[n_in + 2 * n_host + 2], host[1])
            pl.when(jnp.logical_and(pl.program_id(0) == 0, pl.program_id(1) == 0))(start)
        x = a_ref[...].astype(BF16)
        y = b_ref[...].astype(BF16)
        acc = lax.dot_general(x, y, dims, preferred_element_type=F32)
        if has_add:
            acc = acc + refs[2][...]
        o_ref[...] = acc.astype(o_ref.dtype)
        if n_host:
            pl.when(jnp.logical_and(pl.program_id(0) == grid[0] - 1, pl.program_id(1) == grid[1] - 1))(wait)

    sem = ("arbitrary", "arbitrary") if n_host else ("parallel", "parallel")
    return pl.pallas_call(
        body, name=name, out_shape=out_shape, grid=grid, in_specs=in_specs, out_specs=out_spec,
        scratch_shapes=scratch, input_output_aliases=aliases, compiler_params=_params(sem))(*args)


def _mm_ffn_dn(dup, w_up, tm, tn, name):
    _, LP, F = dup.shape
    Dm = w_up.shape[0]
    nt = (((1,), (1,)), ((), ()))

    def body(a_ref, b_ref, o_ref):
        acc = lax.dot_general(a_ref[0], b_ref[:, 0:F], nt, preferred_element_type=F32)
        acc = acc + lax.dot_general(a_ref[1], b_ref[:, F:2 * F], nt, preferred_element_type=F32)
        o_ref[...] = acc

    return pl.pallas_call(
        body, name=name, out_shape=jax.ShapeDtypeStruct((LP, Dm), F32), grid=(LP // tm, Dm // tn),
        in_specs=[pl.BlockSpec((2, tm, F), lambda i, j: (0, i, 0)),
                  pl.BlockSpec((tn, 2 * F), lambda i, j: (j, 0))],
        out_specs=pl.BlockSpec((tm, tn), lambda i, j: (i, j)),
        compiler_params=_params(("parallel", "parallel")))(dup, w_up)


def _mm_ffn_dwup(n, dup, tk, tn, name):
    LP, Dm = n.shape
    F = dup.shape[2]
    nct = F // tn
    tdims = (((0,), (0,)), ((), ()))

    def body(a_ref, b_ref, o_ref):
        o_ref[...] = lax.dot_general(a_ref[...], b_ref[...], tdims, preferred_element_type=F32)

    return pl.pallas_call(
        body, name=name, out_shape=jax.ShapeDtypeStruct((Dm, 2 * F), F32), grid=(Dm // tk, 2 * nct),
        in_specs=[pl.BlockSpec((LP, tk), lambda i, j: (0, i)),
                  pl.BlockSpec((None, LP, tn), lambda i, j: (j // nct, 0, j % nct))],
        out_specs=pl.BlockSpec((tk, tn), lambda i, j: (i, j)),
        compiler_params=_params(("parallel", "parallel")))(n, dup)


def _rms_fwd(h, g, out_dtype, tm, name):
    LP, Dm = h.shape

    def body(h_ref, g_ref, o_ref):
        x = h_ref[...]
        r = lax.rsqrt(jnp.mean(x * x, axis=1, keepdims=True) + RMS_EPS)
        o_ref[...] = (x * r * g_ref[...]).astype(o_ref.dtype)

    return pl.pallas_call(
        body, name=name, out_shape=jax.ShapeDtypeStruct((LP, Dm), out_dtype), grid=(LP // tm,),
        in_specs=[pl.BlockSpec((tm, Dm), lambda i: (i, 0)), pl.BlockSpec((1, Dm), lambda i: (0, 0))],
        out_specs=pl.BlockSpec((tm, Dm), lambda i: (i, 0)),
        compiler_params=_params(("parallel",)))(h, g)


def _rms_bwd(h, g, dn, dres, tm, name, host=None):
    LP, Dm = h.shape
    n_host = 0 if host is None else len(host[0])
    if n_host:
        host_shapes, host_sems, host_ops = _host_plan(host)
    nblk = LP // tm

    def body(*refs):
        h_ref, g_ref, dn_ref, dr_ref = refs[:4]
        dh_ref, dg_ref = refs[4 + n_host:6 + n_host]
        if n_host:
            start, wait = host_ops(refs[4:4 + n_host], refs[6 + n_host:6 + 2 * n_host],
                                   refs[6 + 2 * n_host], refs[7 + 2 * n_host], host[1])
            pl.when(pl.program_id(0) == 0)(start)
        i = pl.program_id(0)
        x = h_ref[...]
        r = lax.rsqrt(jnp.mean(x * x, axis=1, keepdims=True) + RMS_EPS)
        xhat = x * r
        dy = dn_ref[...]
        dxh = dy * g_ref[...]
        dh = r * (dxh - xhat * jnp.mean(dxh * xhat, axis=1, keepdims=True))
        dh_ref[...] = dr_ref[...] + dh

        @pl.when(i == 0)
        def _():
            dg_ref[...] = jnp.zeros_like(dg_ref)

        dg_ref[...] += jnp.sum(dy * xhat, axis=0, keepdims=True)
        if n_host:
            pl.when(pl.program_id(0) == nblk - 1)(wait)

    row = pl.BlockSpec((tm, Dm), lambda i: (i, 0))
    vec = pl.BlockSpec((1, Dm), lambda i: (0, 0))
    out_shape = (jax.ShapeDtypeStruct((LP, Dm), F32), jax.ShapeDtypeStruct((1, Dm), F32))
    out_specs = (row, vec)
    in_specs = [row, vec, row, row]
    args = [h, g, dn, dres]
    scratch = []
    if n_host:
        anyspec = pl.BlockSpec(memory_space=pl.ANY)
        in_specs += [anyspec] * n_host
        args += list(host[0])
        out_shape += host_shapes
        out_specs += (anyspec,) * n_host
        scratch = [pltpu.SemaphoreType.DMA((host_sems,)), pltpu.SemaphoreType.DMA((host_sems,))]
    return pl.pallas_call(
        body, name=name, out_shape=out_shape, grid=(nblk,), in_specs=in_specs, out_specs=out_specs,
        scratch_shapes=scratch, compiler_params=_params(("arbitrary",)))(*args)


def _loss_head(h, g, tgt, n_real, tm, name):
    LP, Dm = h.shape

    def body(h_ref, g_ref, t_ref, loss_ref, dh_ref, dg_ref):
        i = pl.program_id(0)
        x = h_ref[...]
        gg = g_ref[...]
        r = lax.rsqrt(jnp.mean(x * x, axis=1, keepdims=True) + RMS_EPS)
        xhat = x * r
        rows = i * tm + lax.broadcasted_iota(jnp.int32, (tm, 1), 0)
        real = jnp.logical_and(rows >= N_META, rows < n_real)
        diff = jnp.where(real, xhat * gg - t_ref[...], 0.0)
        dy = diff * (1.0 / Dm)
        dxh = dy * gg
        dh_ref[...] = r * (dxh - xhat * jnp.mean(dxh * xhat, axis=1, keepdims=True))

        @pl.when(i == 0)
        def _():
            dg_ref[...] = jnp.zeros_like(dg_ref)
            loss_ref[...] = jnp.zeros_like(loss_ref)

        dg_ref[...] += jnp.sum(dy * xhat, axis=0, keepdims=True)
        part = jnp.sum(jnp.sum(diff * diff, axis=1, keepdims=True), axis=0, keepdims=True)
        loss_ref[...] += jnp.broadcast_to(part * (0.5 / Dm), loss_ref.shape)

    row = pl.BlockSpec((tm, Dm), lambda i: (i, 0))
    vec = pl.BlockSpec((1, Dm), lambda i: (0, 0))
    return pl.pallas_call(
        body, name=name,
        out_shape=(jax.ShapeDtypeStruct((1, LANE), F32), jax.ShapeDtypeStruct((LP, Dm), F32),
                   jax.ShapeDtypeStruct((1, Dm), F32)),
        grid=(LP // tm,), in_specs=[row, vec, row],
        out_specs=(pl.BlockSpec((1, LANE), lambda i: (0, 0)), row, vec),
        compiler_params=_params(("arbitrary",)))(h, g, tgt)


def _fgate_fwd(proj, bf_p, name):
    LP = proj.shape[0]
    nb = LP // LANE

    def body(f_ref, b_ref, c_ref, lf_ref):
        x = f_ref[...] + b_ref[...]
        lf_ref[...] = jnp.minimum(x, 0.0) - jnp.log1p(jnp.exp(-jnp.abs(x)))
        ri = lax.broadcasted_iota(jnp.int32, (LANE, LANE), 0)
        ci = lax.broadcasted_iota(jnp.int32, (LANE, LANE), 1)
        tri = jnp.where(ri >= ci, 1.0, 0.0).astype(F32)

        def blk(i, carry):
            rows = pl.ds(pl.multiple_of(i * LANE, LANE), LANE)
            cb = jnp.dot(tri, lf_ref[rows, :], precision=lax.Precision.HIGHEST,
                         preferred_element_type=F32) + carry
            c_ref[rows, :] = cb
            return cb[LANE - 1:LANE, :]

        lax.fori_loop(0, nb, blk, jnp.zeros((1, LANE), F32))

    return pl.pallas_call(
        body, name=name, out_shape=jax.ShapeDtypeStruct((LP, LANE), F32), grid=(1,),
        in_specs=[pl.BlockSpec((LP, LANE), lambda i: (0, F_COL_BLK)),
                  pl.BlockSpec((1, LANE), lambda i: (0, 0))],
        out_specs=pl.BlockSpec((LP, LANE), lambda i: (0, 0)),
        scratch_shapes=[pltpu.VMEM((LP, LANE), F32)],
        compiler_params=_params(("arbitrary",)))(proj, bf_p)


def _fgate_bwd(proj, bf_p, dc, name):
    LP = proj.shape[0]
    nb = LP // LANE

    def body(f_ref, b_ref, dc_ref, dl_ref, db_ref):
        ri = lax.broadcasted_iota(jnp.int32, (LANE, LANE), 0)
        ci = lax.broadcasted_iota(jnp.int32, (LANE, LANE), 1)
        triu = jnp.where(ri <= ci, 1.0, 0.0).astype(F32)
        bb = b_ref[...]

        tail = jnp.zeros((1, LANE), F32)
        dbs = jnp.zeros((1, LANE), F32)
        for i in range(nb - 1, -1, -1):
            rows = slice(i * LANE, (i + 1) * LANE)
            gb = jnp.dot(triu, dc_ref[rows, :], precision=lax.Precision.HIGHEST,
                         preferred_element_type=F32) + tail
            x = f_ref[rows, :] + bb
            dl = gb * _sigmoid_tail(-x)
            dl_ref[rows, :] = dl.astype(dl_ref.dtype)
            tail = gb[0:1, :]
            dbs = dbs + jnp.sum(dl, axis=0, keepdims=True)
        db_ref[...] = dbs

    return pl.pallas_call(
        body, name=name,
        out_shape=(jax.ShapeDtypeStruct((LP, LANE), BF16), jax.ShapeDtypeStruct((1, LANE), F32)),
        grid=(1,),
        in_specs=[pl.BlockSpec((LP, LANE), lambda i: (0, F_COL_BLK)),
                  pl.BlockSpec((1, LANE), lambda i: (0, 0)),
                  pl.BlockSpec((LP, LANE), lambda i: (0, 0))],
        out_specs=(pl.BlockSpec((LP, LANE), lambda i: (0, 0)), pl.BlockSpec((1, LANE), lambda i: (0, 0))),
        compiler_params=_params(("arbitrary",)))(proj, bf_p, dc)


def _attn_fwd(q, k, v, c_col, c_row, name):
    Hh, LP, Dh = q.shape
    nb = LP // ATT_BLK
    scale = Dh ** -0.5
    nt = (((1,), (1,)), ((), ()))

    def body(q_ref, k_ref, v_ref, cc_ref, cr_ref, o_ref, lse_ref):
        i = pl.program_id(1)
        qb = q_ref[...]
        cq = cc_ref[...]
        rows = i * ATT_BLK + lax.broadcasted_iota(jnp.int32, (ATT_BLK, ATT_BLK), 0)
        cols0 = lax.broadcasted_iota(jnp.int32, (ATT_BLK, ATT_BLK), 1)

        def step(j, carry):
            m, l, acc = carry
            ks = pl.ds(pl.multiple_of(j * ATT_BLK, ATT_BLK), ATT_BLK)
            s = lax.dot_general(qb, k_ref[ks, :], nt, preferred_element_type=F32) * scale
            s = s + cq - cr_ref[j]
            s = jnp.where(cols0 + j * ATT_BLK <= rows, s, NEG)
            m_new = jnp.maximum(m, jnp.max(s, axis=1, keepdims=True))
            p = jnp.exp(s - m_new)
            alpha = jnp.exp(m - m_new)
            l = alpha * l + jnp.sum(p, axis=1, keepdims=True)
            acc = alpha * acc + jnp.dot(p.astype(BF16), v_ref[ks, :], preferred_element_type=F32)
            return m_new, l, acc

        init = (jnp.full((ATT_BLK, 1), NEG, F32), jnp.zeros((ATT_BLK, 1), F32),
                jnp.zeros((ATT_BLK, Dh), F32))
        m, l, acc = lax.fori_loop(0, i + 1, step, init)
        o_ref[...] = acc / l
        lse_ref[...] = m + jnp.log(l)

    qspec = pl.BlockSpec((None, ATT_BLK, Dh), lambda h, i: (h, i, 0))
    kspec = pl.BlockSpec((None, LP, Dh), lambda h, i: (h, 0, 0))
    colspec = pl.BlockSpec((None, ATT_BLK, 1), lambda h, i: (h, i, 0))
    rowspec = pl.BlockSpec((None, nb, 1, ATT_BLK), lambda h, i: (h, 0, 0, 0))
    return pl.pallas_call(
        body, name=name,
        out_shape=(jax.ShapeDtypeStruct((Hh, LP, Dh), F32), jax.ShapeDtypeStruct((Hh, LP, 1), F32)),
        grid=(Hh, nb), in_specs=[qspec, kspec, kspec, colspec, rowspec],
        out_specs=(qspec, colspec),
        compiler_params=_params(("parallel", "arbitrary")))(q, k, v, c_col, c_row)


def _attn_bwd(q, k, v, o, do, lse, c_col, c_row, name):
    Hh, LP, Dh = q.shape
    nb = LP // ATT_BLK
    scale = Dh ** -0.5
    nt = (((1,), (1,)), ((), ()))
    tn = (((0,), (0,)), ((), ()))

    def body(q_ref, k_ref, v_ref, o_ref, do_ref, lse_ref, cc_ref, cr_ref,
             dq_ref, dk_ref, dv_ref, dcq_ref, dc_ref, delta_ref):
        j = pl.program_id(1)

        @pl.when(j == 0)
        def _():
            dq_ref[...] = jnp.zeros_like(dq_ref)
            dcq_ref[...] = jnp.zeros_like(dcq_ref)
            dob_all = do_ref[...].astype(BF16).astype(F32)
            delta_ref[...] = jnp.sum(dob_all * o_ref[...], axis=1, keepdims=True)

        kb = k_ref[...]
        vb = v_ref[...]
        ck = cr_ref[j]
        rows0 = lax.broadcasted_iota(jnp.int32, (ATT_BLK, ATT_BLK), 0)
        cols = j * ATT_BLK + lax.broadcasted_iota(jnp.int32, (ATT_BLK, ATT_BLK), 1)

        def step(i, carry):
            dk, dv, dcs = carry
            qs = pl.ds(pl.multiple_of(i * ATT_BLK, ATT_BLK), ATT_BLK)
            qb = q_ref[qs, :]
            dob = do_ref[qs, :].astype(BF16)
            s = lax.dot_general(qb, kb, nt, preferred_element_type=F32) * scale
            s = s + cc_ref[qs, :] - ck
            s = jnp.where(cols <= rows0 + i * ATT_BLK, s, NEG)
            p = jnp.exp(s - lse_ref[qs, :])
            dp = lax.dot_general(dob, vb, nt, preferred_element_type=F32)
            ds = p * (dp - delta_ref[qs, :])
            dsb = ds.astype(BF16)
            dv = dv + lax.dot_general(p.astype(BF16), dob, tn, preferred_element_type=F32)
            dk = dk + lax.dot_general(dsb, qb, tn, preferred_element_type=F32) * scale
            dq_ref[qs, :] += jnp.dot(dsb, kb, preferred_element_type=F32) * scale
            dcq_ref[qs, :] += jnp.sum(ds, axis=1, keepdims=True)
            dcs = dcs - jnp.sum(ds, axis=0, keepdims=True)
            return dk, dv, dcs

        init = (jnp.zeros((ATT_BLK, Dh), F32), jnp.zeros((ATT_BLK, Dh), F32),
                jnp.zeros((1, ATT_BLK), F32))
        dk, dv, dcs = lax.fori_loop(j, nb, step, init)
        dk_ref[...] = dk
        dv_ref[...] = dv
        dc_ref[...] = dcs

    full = pl.BlockSpec((None, LP, Dh), lambda h, j: (h, 0, 0))
    blk = pl.BlockSpec((None, ATT_BLK, Dh), lambda h, j: (h, j, 0))
    col = pl.BlockSpec((None, LP, 1), lambda h, j: (h, 0, 0))
    rowspec = pl.BlockSpec((None, nb, 1, ATT_BLK), lambda h, j: (h, 0, 0, 0))
    return pl.pallas_call(
        body, name=name,
        out_shape=(jax.ShapeDtypeStruct((Hh, LP, Dh), F32), jax.ShapeDtypeStruct((Hh, LP, Dh), F32),
                   jax.ShapeDtypeStruct((Hh, LP, Dh), F32), jax.ShapeDtypeStruct((Hh, LP, 1), F32),
                   jax.ShapeDtypeStruct((Hh, nb, 1, ATT_BLK), F32)),
        grid=(Hh, nb), in_specs=[full, blk, blk, full, full, col, col, rowspec],
        out_specs=(full, blk, blk, col, pl.BlockSpec((None, None, 1, ATT_BLK), lambda h, j: (h, j, 0, 0))),
        scratch_shapes=[pltpu.VMEM((LP, 1), F32)],
        compiler_params=_params(("parallel", "arbitrary")))(q, k, v, o, do, lse, c_col, c_row)


AUG = 128
ONES_IN_K = HEAD_DIM
ONES_IN_Q = HEAD_DIM + 3
ATT_HEADS_PER_STEP = 8
ATT_HEADS_PER_STEP_BWD = 8


def _attn_prep(proj, c, name):
    LP = proj.shape[0]
    nb = LP // ATT_BLK
    tail_rows = AUG - HEAD_DIM

    def body(q_ref, k_ref, v_ref, c_ref, qT_ref, kT_ref, ka_ref, vT_ref):
        qt = (q_ref[...] * (HEAD_DIM ** -0.5)).T
        kt = k_ref[...].T
        vt = v_ref[...].T
        ct = c_ref[...].T
        hi = ct.astype(BF16).astype(F32)
        r1 = ct - hi
        mid = r1.astype(BF16).astype(F32)
        lo = (r1 - mid).astype(BF16).astype(F32)
        row = lax.broadcasted_iota(jnp.int32, (tail_rows, ATT_BLK), 0)
        ones = jnp.where(row < 3, 1.0, 0.0)
        for h in range(HEADS):
            cparts = jnp.where(row == 0, hi[h:h + 1], jnp.where(row == 1, mid[h:h + 1],
                               jnp.where(row == 2, lo[h:h + 1], 0.0)))
            hs = slice(h * HEAD_DIM, (h + 1) * HEAD_DIM)
            q_tail = cparts + pltpu.roll(ones, 3, 0)
            k_tail = ones - pltpu.roll(cparts, 3, 0)
            qT_ref[h] = jnp.concatenate([qt[hs], q_tail], axis=0).astype(BF16)
            kfull = jnp.concatenate([kt[hs], k_tail], axis=0)
            kT_ref[h] = kfull.astype(BF16)
            ka_ref[h] = kfull.T.astype(BF16)
            vT_ref[h] = vt[hs].astype(BF16)

    col = lambda j: pl.BlockSpec((ATT_BLK, FOX_W), lambda i: (i, j))
    blk = lambda r: pl.BlockSpec((HEADS, None, r, ATT_BLK), lambda i: (0, i, 0, 0))
    return pl.pallas_call(
        body, name=name,
        out_shape=(jax.ShapeDtypeStruct((HEADS, nb, AUG, ATT_BLK), BF16),
                   jax.ShapeDtypeStruct((HEADS, nb, AUG, ATT_BLK), BF16),
                   jax.ShapeDtypeStruct((HEADS, LP, AUG), BF16),
                   jax.ShapeDtypeStruct((HEADS, nb, HEAD_DIM, ATT_BLK), BF16)),
        grid=(nb,), in_specs=[col(0), col(1), col(2), pl.BlockSpec((ATT_BLK, LANE), lambda i: (i, 0))],
        out_specs=(blk(AUG), blk(AUG), pl.BlockSpec((HEADS, ATT_BLK, AUG), lambda i: (0, i, 0)), blk(HEAD_DIM)),
        compiler_params=_params(("parallel",)))(proj, proj, proj, c)


def _attn_rows(xT, scale, out_dtype, name):
    Hh, nb, R, _ = xT.shape

    def body(x_ref, o_ref):
        stack = jnp.concatenate([x_ref[h, 0:HEAD_DIM, :] for h in range(Hh)], axis=0)
        o_ref[...] = (stack * scale).T.astype(o_ref.dtype)

    return pl.pallas_call(
        body, name=name, out_shape=jax.ShapeDtypeStruct((nb * ATT_BLK, Hh * HEAD_DIM), out_dtype), grid=(nb,),
        in_specs=[pl.BlockSpec((Hh, None, R, ATT_BLK), lambda i: (0, i, 0, 0))],
        out_specs=pl.BlockSpec((ATT_BLK, Hh * HEAD_DIM), lambda i: (i, 0)),
        compiler_params=_params(("parallel",)))(xT)


def _attn_cols(x, name):
    LP = x.shape[0]
    nb = LP // ATT_BLK

    def body(x_ref, o_ref):
        xt = x_ref[...].T
        for h in range(HEADS):
            o_ref[h] = xt[h * HEAD_DIM:(h + 1) * HEAD_DIM].astype(o_ref.dtype)

    return pl.pallas_call(
        body, name=name, out_shape=jax.ShapeDtypeStruct((HEADS, nb, HEAD_DIM, ATT_BLK), BF16), grid=(nb,),
        in_specs=[pl.BlockSpec((ATT_BLK, FOX_W), lambda i: (i, 0))],
        out_specs=pl.BlockSpec((HEADS, None, HEAD_DIM, ATT_BLK), lambda i: (0, i, 0, 0)),
        compiler_params=_params(("parallel",)))(x)


def _attn_dc(dqT, dkT, name):
    Hh, nb, _, _ = dqT.shape

    def body(q_ref, k_ref, o_ref):
        row = lax.broadcasted_iota(jnp.int32, (LANE, ATT_BLK), 0)
        acc = jnp.zeros((LANE, ATT_BLK), F32)
        for h in range(Hh):
            d = q_ref[h, ONES_IN_K:ONES_IN_K + 1, :] - k_ref[h, ONES_IN_Q:ONES_IN_Q + 1, :]
            acc = jnp.where(row == h, d, acc)
        o_ref[...] = acc.T

    spec = pl.BlockSpec((Hh, None, AUG, ATT_BLK), lambda i: (0, i, 0, 0))
    return pl.pallas_call(
        body, name=name, out_shape=jax.ShapeDtypeStruct((nb * ATT_BLK, LANE), F32), grid=(nb,),
        in_specs=[spec, spec], out_specs=pl.BlockSpec((ATT_BLK, LANE), lambda i: (i, 0)),
        compiler_params=_params(("parallel",)))(dqT, dkT)


def _attn_fwd2(qT, k_aug, vT, name, comm=None):
    Hh, nb, _, _ = qT.shape
    LP = nb * ATT_BLK
    Dh = vT.shape[2]
    HB = ATT_HEADS_PER_STEP
    n_comm = 0 if comm is None else len(comm[0])

    def body(*refs):
        q_ref, k_ref, v_ref = refs[:3]
        o_ref, lse_ref = refs[3 + n_comm:5 + n_comm]
        if n_comm:
            start, wait = _gather_first_ops(refs[3:3 + n_comm], refs[5 + n_comm:5 + 2 * n_comm],
                                            refs[5 + 2 * n_comm], refs[6 + 2 * n_comm], comm[1])
            pl.when(pl.program_id(0) == 0)(start)
        keys = lax.broadcasted_iota(jnp.int32, (ATT_BLK, ATT_BLK), 0)
        qrys = lax.broadcasted_iota(jnp.int32, (ATT_BLK, ATT_BLK), 1)
        causal = keys <= qrys

        def q_block(i, _):
            def tile(j, carry, masked):
                ks = pl.ds(pl.multiple_of(j * ATT_BLK, ATT_BLK), ATT_BLK)
                s_all = [jnp.dot(k_ref[hh, ks, :], q_ref[hh, i], preferred_element_type=F32) for hh in range(HB)]
                stats, p_all = [], []
                for hh in range(HB):
                    m, l, _ = carry[hh]
                    s = jnp.where(causal, s_all[hh], NEG) if masked else s_all[hh]
                    m_new = jnp.maximum(m, jnp.max(s, axis=0, keepdims=True))
                    p = jnp.exp(s - m_new)
                    alpha = jnp.exp(m - m_new)
                    stats.append((m_new, alpha * l + jnp.sum(p, axis=0, keepdims=True), alpha))
                    p_all.append(p.astype(BF16))
                out = []
                for hh in range(HB):
                    m_new, l, alpha = stats[hh]
                    acc = alpha * carry[hh][2] + jnp.dot(v_ref[hh, j], p_all[hh], preferred_element_type=F32)
                    out.append((m_new, l, acc))
                return tuple(out)

            init = tuple((jnp.full((1, ATT_BLK), NEG, F32), jnp.zeros((1, ATT_BLK), F32),
                          jnp.zeros((Dh, ATT_BLK), F32)) for _ in range(HB))
            carry = lax.fori_loop(0, i, lambda j, cr: tile(j, cr, False), init)
            carry = tile(i, carry, True)
            for hh in range(HB):
                m, l, acc = carry[hh]
                o_ref[hh, i] = acc / l
                lse_ref[hh, i] = m + jnp.log(l)
            return 0

        lax.fori_loop(0, nb, q_block, 0)
        if n_comm:
            pl.when(pl.program_id(0) == Hh // HB - 1)(wait)

    blk = lambda r: pl.BlockSpec((HB, nb, r, ATT_BLK), lambda h: (h, 0, 0, 0))
    out_shape = (jax.ShapeDtypeStruct((Hh, nb, Dh, ATT_BLK), F32), jax.ShapeDtypeStruct((Hh, nb, 1, ATT_BLK), F32))
    scratch = []
    args = [qT, k_aug, vT]
    if n_comm:
        out_shape += tuple(jax.ShapeDtypeStruct(cut.full, s.dtype) for cut, s in zip(comm[1], comm[0]))
        scratch = [pltpu.SemaphoreType.DMA((4 * n_comm,)), pltpu.SemaphoreType.DMA((4 * n_comm,))]
        args += list(comm[0])
    return pl.pallas_call(
        body, name=name, out_shape=out_shape, grid=(Hh // HB,),
        in_specs=[blk(AUG), pl.BlockSpec((HB, LP, AUG), lambda h: (h, 0, 0)), blk(Dh)] + [ANY] * n_comm,
        out_specs=(blk(Dh), blk(1)) + (ANY,) * n_comm, scratch_shapes=scratch,
        compiler_params=_params(("arbitrary",)))(*args)


def _attn_bwd2(qT, kT, k_aug, v, oT, doT, lse, name, comm=None):
    Hh, nb, _, _ = qT.shape
    LP = nb * ATT_BLK
    Dh = v.shape[2]
    nt = (((1,), (1,)), ((), ()))
    tn = (((0,), (0,)), ((), ()))

    HB = ATT_HEADS_PER_STEP_BWD
    n_comm = 0 if comm is None else len(comm[0])

    def body(*refs):
        q_ref, kt_ref, k_ref, v_ref, o_ref, do_ref, lse_ref = refs[:7]
        parts = refs[7:7 + n_comm]
        dq_ref, dk_ref, dv_ref = refs[7 + n_comm:10 + n_comm]
        others = refs[10 + n_comm:10 + 2 * n_comm]
        delta_ref = refs[10 + 2 * n_comm]
        if n_comm:
            start, wait = _chip_exchange_ops(parts, others, refs[11 + 2 * n_comm], refs[12 + 2 * n_comm], comm[1])
            pl.when(pl.program_id(0) == 0)(start)
        keys = lax.broadcasted_iota(jnp.int32, (ATT_BLK, ATT_BLK), 0)
        qrys = lax.broadcasted_iota(jnp.int32, (ATT_BLK, ATT_BLK), 1)
        causal = keys <= qrys

        def prep(i, _):
            for hh in range(HB):
                delta_ref[hh, i] = jnp.sum(do_ref[hh, i].astype(F32) * o_ref[hh, i], axis=0, keepdims=True)
                dq_ref[hh, i] = jnp.zeros((AUG, ATT_BLK), F32)
            return 0

        lax.fori_loop(0, nb, prep, 0)

        def kv_block(j, _):
            ks = pl.ds(pl.multiple_of(j * ATT_BLK, ATT_BLK), ATT_BLK)

            def tile(i, carry, masked):
                s_all = [jnp.dot(k_ref[hh, ks, :], q_ref[hh, i], preferred_element_type=F32) for hh in range(HB)]
                dp_all = [lax.dot_general(v_ref[hh, j], do_ref[hh, i], tn, preferred_element_type=F32)
                          for hh in range(HB)]
                p_all, ds_all = [], []
                for hh in range(HB):
                    s = jnp.where(causal, s_all[hh], NEG) if masked else s_all[hh]
                    p = jnp.exp(s - lse_ref[hh, i])
                    ds_all.append((p * (dp_all[hh] - delta_ref[hh, i])).astype(BF16))
                    p_all.append(p.astype(BF16))
                out = []
                for hh in range(HB):
                    dk, dv = carry[hh]
                    dv = dv + lax.dot_general(do_ref[hh, i], p_all[hh], nt, preferred_element_type=F32)
                    dk = dk + lax.dot_general(q_ref[hh, i], ds_all[hh], nt, preferred_element_type=F32)
                    out.append((dk, dv))
                dq_new = [jnp.dot(kt_ref[hh, j], ds_all[hh], preferred_element_type=F32) for hh in range(HB)]
                for hh in range(HB):
                    dq_ref[hh, i] += dq_new[hh]
                return tuple(out)

            init = tuple((jnp.zeros((AUG, ATT_BLK), F32), jnp.zeros((Dh, ATT_BLK), F32)) for _ in range(HB))
            carry = tile(j, init, True)
            carry = lax.fori_loop(j + 1, nb, lambda i, cr: tile(i, cr, False), carry)
            for hh in range(HB):
                dk_ref[hh, j] = carry[hh][0]
                dv_ref[hh, j] = carry[hh][1]
            return 0

        lax.fori_loop(0, nb, kv_block, 0)
        if n_comm:
            pl.when(pl.program_id(0) == Hh // HB - 1)(wait)

    blk = lambda r: pl.BlockSpec((HB, nb, r, ATT_BLK), lambda h: (h, 0, 0, 0))
    row = lambda cols: pl.BlockSpec((HB, LP, cols), lambda h: (h, 0, 0))
    out_shape = (jax.ShapeDtypeStruct((Hh, nb, AUG, ATT_BLK), F32), jax.ShapeDtypeStruct((Hh, nb, AUG, ATT_BLK), F32),
                 jax.ShapeDtypeStruct((Hh, nb, Dh, ATT_BLK), F32))
    scratch = [pltpu.VMEM((HB, nb, 1, ATT_BLK), F32)]
    args = [qT, kT, k_aug, v, oT, doT, lse]
    if n_comm:
        out_shape += _chip_exchange_shapes(*comm)
        scratch += [pltpu.SemaphoreType.DMA((3 * n_comm,)), pltpu.SemaphoreType.DMA((3 * n_comm,))]
        args += list(comm[0])
    return pl.pallas_call(
        body, name=name, out_shape=out_shape, grid=(Hh // HB,),
        in_specs=[blk(AUG), blk(AUG), row(AUG), blk(Dh), blk(Dh), blk(Dh), blk(1)] + [ANY] * n_comm,
        out_specs=(blk(AUG), blk(AUG), blk(Dh)) + (ANY,) * n_comm,
        scratch_shapes=scratch,
        compiler_params=_params(("arbitrary",)))(*args)


CONV_HALO = 32
A_BLK = 3 * FOX_W // CONV_CH
G_BLK = A_BLK + 1


def _conf_fwd(proj, cw, cb, lg, lb, tm, name):
    LP = proj.shape[0]
    C = CONV_CH
    sub = _sub_rows(tm)
    hpb = tm // CONV_HALO

    def body(a_ref, g_ref, ah_ref, gh_ref, w_ref, cb_ref, lg_ref, lb_ref, u1_ref, u_ref, buf):
        r = pl.program_id(0)
        buf[CONV_HALO:CONV_HALO + tm, :] = a_ref[...] * _sigmoid(g_ref[...])
        buf[0:CONV_HALO, :] = jnp.where(r > 0, ah_ref[...] * _sigmoid(gh_ref[...]), 0.0)
        for s in range(tm // sub):
            for ct in range(C // LANE):
                ln = slice(ct * LANE, (ct + 1) * LANE)
                acc = jnp.broadcast_to(cb_ref[:, ln], (sub, LANE))
                for kk in range(CONV_K):
                    off = CONV_HALO + s * sub - (CONV_K - 1) + kk
                    acc = acc + w_ref[kk:kk + 1, ln] * buf[off:off + sub, ln]
                u1_ref[s * sub:(s + 1) * sub, ln] = acc
        u1 = u1_ref[...]
        mu = jnp.mean(u1, axis=1, keepdims=True)
        xc = u1 - mu
        var = jnp.mean(xc * xc, axis=1, keepdims=True)
        y = xc * lax.rsqrt(var + LN_EPS) * lg_ref[...] + lb_ref[...]
        u_ref[...] = (y * _sigmoid(y)).astype(u_ref.dtype)

    cur = lambda blk: pl.BlockSpec((tm, C), lambda r: (r, blk))
    halo = lambda blk: pl.BlockSpec((CONV_HALO, C), lambda r: (jnp.maximum(r * hpb - 1, 0), blk))
    vec = pl.BlockSpec((1, C), lambda r: (0, 0))
    out = pl.BlockSpec((tm, C), lambda r: (r, 0))
    return pl.pallas_call(
        body, name=name,
        out_shape=(jax.ShapeDtypeStruct((LP, C), F32), jax.ShapeDtypeStruct((LP, C), BF16)),
        grid=(LP // tm,),
        in_specs=[cur(A_BLK), cur(G_BLK), halo(A_BLK), halo(G_BLK),
                  pl.BlockSpec((CONV_HALO, C), lambda r: (0, 0)), vec, vec, vec],
        out_specs=(out, out),
        scratch_shapes=[pltpu.VMEM((CONV_HALO + tm, C), F32)],
        compiler_params=_params(("parallel",)))(proj, proj, proj, proj, cw, cb, lg, lb)


def _conf_bwd(proj, u1, dcat, cw, lg, lb, tm, name, host=None):
    LP = proj.shape[0]
    C = CONV_CH
    sub = _sub_rows(tm)
    hpb = tm // CONV_HALO
    nblk = LP // tm
    last_halo = LP // CONV_HALO - 1

    n_host = 0 if host is None else len(host[0])
    if n_host:
        host_shapes, host_sems, host_ops = _host_plan(host)

    def body(*refs):
        a_ref, g_ref, ah_ref, gh_ref, u1_ref, u1n_ref, du_ref, dun_ref, w_ref, lg_ref, lb_ref = refs[:11]
        dadg_ref, dw_ref, dcb_ref, dlg_ref, dlb_ref = refs[11 + n_host:16 + n_host]
        ubuf, dbuf, du0 = refs[16 + 2 * n_host:19 + 2 * n_host]
        if n_host:
            start, wait = host_ops(refs[11:11 + n_host], refs[16 + n_host:16 + 2 * n_host],
                                   refs[19 + 2 * n_host], refs[20 + 2 * n_host], host[1])
            pl.when(pl.program_id(0) == 0)(start)
        r = pl.program_id(0)
        lgv = lg_ref[...]
        lbv = lb_ref[...]

        def ln_silu_bwd(u1v, duv):
            mu = jnp.mean(u1v, axis=1, keepdims=True)
            xc = u1v - mu
            rstd = lax.rsqrt(jnp.mean(xc * xc, axis=1, keepdims=True) + LN_EPS)
            xhat = xc * rstd
            y = xhat * lgv + lbv
            sg = _sigmoid(y)
            dy = duv * (sg * (1.0 + y * (1.0 - sg)))
            dxh = dy * lgv
            du1 = rstd * (dxh - jnp.mean(dxh, axis=1, keepdims=True)
                          - xhat * jnp.mean(dxh * xhat, axis=1, keepdims=True))
            return du1, dy, xhat

        @pl.when(r == 0)
        def _():
            dw_ref[...] = jnp.zeros_like(dw_ref)
            dcb_ref[...] = jnp.zeros_like(dcb_ref)
            dlg_ref[...] = jnp.zeros_like(dlg_ref)
            dlb_ref[...] = jnp.zeros_like(dlb_ref)

        du1, dy, xhat = ln_silu_bwd(u1_ref[...], du_ref[...])
        dlg_ref[...] += jnp.sum(dy * xhat, axis=0, keepdims=True)
        dlb_ref[...] += jnp.sum(dy, axis=0, keepdims=True)
        dcb_ref[...] += jnp.sum(du1, axis=0, keepdims=True)
        dbuf[0:tm, :] = du1
        du1n, _, _ = ln_silu_bwd(u1n_ref[...], dun_ref[...])
        dbuf[tm:tm + CONV_HALO, :] = jnp.where(r < nblk - 1, du1n, 0.0)
        ubuf[CONV_HALO:CONV_HALO + tm, :] = a_ref[...] * _sigmoid(g_ref[...])
        ubuf[0:CONV_HALO, :] = jnp.where(r > 0, ah_ref[...] * _sigmoid(gh_ref[...]), 0.0)

        for ct in range(C // LANE):
            ln = slice(ct * LANE, (ct + 1) * LANE)
            for s in range(tm // sub):
                d_here = dbuf[s * sub:(s + 1) * sub, ln]
                acc = jnp.zeros((sub, LANE), F32)
                for kk in range(CONV_K):
                    fo = s * sub + (CONV_K - 1) - kk
                    acc = acc + w_ref[kk:kk + 1, ln] * dbuf[fo:fo + sub, ln]
                    bo = CONV_HALO + s * sub - (CONV_K - 1) + kk
                    dw_ref[kk:kk + 1, ln] += jnp.sum(d_here * ubuf[bo:bo + sub, ln], axis=0, keepdims=True)
                du0[s * sub:(s + 1) * sub, ln] = acc
        a = a_ref[...]
        sg = _sigmoid(g_ref[...])
        d0 = du0[...]
        dadg_ref[:, 0:C] = (d0 * sg).astype(dadg_ref.dtype)
        dadg_ref[:, C:2 * C] = (d0 * a * sg * (1.0 - sg)).astype(dadg_ref.dtype)
        if n_host:
            pl.when(pl.program_id(0) == nblk - 1)(wait)

    cur = lambda blk: pl.BlockSpec((tm, C), lambda r: (r, blk))
    prev = lambda blk: pl.BlockSpec((CONV_HALO, C), lambda r: (jnp.maximum(r * hpb - 1, 0), blk))
    nxt = lambda blk: pl.BlockSpec((CONV_HALO, C), lambda r: (jnp.minimum((r + 1) * hpb, last_halo), blk))
    vec = pl.BlockSpec((1, C), lambda r: (0, 0))
    wspec = pl.BlockSpec((CONV_HALO, C), lambda r: (0, 0))
    out_shape = (jax.ShapeDtypeStruct((LP, 2 * C), BF16), jax.ShapeDtypeStruct((CONV_HALO, C), F32),
                 jax.ShapeDtypeStruct((1, C), F32), jax.ShapeDtypeStruct((1, C), F32),
                 jax.ShapeDtypeStruct((1, C), F32))
    out_specs = (pl.BlockSpec((tm, 2 * C), lambda r: (r, 0)), wspec, vec, vec, vec)
    in_specs = [cur(A_BLK), cur(G_BLK), prev(A_BLK), prev(G_BLK), cur(0), nxt(0), cur(1), nxt(1), wspec, vec, vec]
    args = [proj, proj, proj, proj, u1, u1, dcat, dcat, cw, lg, lb]
    scratch = [pltpu.VMEM((CONV_HALO + tm, C), F32), pltpu.VMEM((tm + CONV_HALO, C), F32),
               pltpu.VMEM((tm, C), F32)]
    if n_host:
        anyspec = pl.BlockSpec(memory_space=pl.ANY)
        in_specs += [anyspec] * n_host
        args += list(host[0])
        out_shape += host_shapes
        out_specs += (anyspec,) * n_host
        scratch += [pltpu.SemaphoreType.DMA((host_sems,)), pltpu.SemaphoreType.DMA((host_sems,))]
    return pl.pallas_call(
        body, name=name, out_shape=out_shape, grid=(nblk,), in_specs=in_specs, out_specs=out_specs,
        scratch_shapes=scratch, compiler_params=_params(("arbitrary",)))(*args)


FFN_HALO = 8
FFN_TC = 256
FFN_K = 3


def _ffn_conv(buf, w_ref, b_ref, s, sub, ln):
    acc = jnp.broadcast_to(b_ref[:, ln], (sub, LANE))
    for kk in range(FFN_K):
        off = FFN_HALO + s * sub - (FFN_K - 1) + kk
        acc = acc + w_ref[kk:kk + 1, ln] * buf[off:off + sub, ln]
    return acc


def _ffn_act_fwd(up, w, b, tm, name, host=None):
    LP, F = up.shape[0], up.shape[1] // 2
    upg = upv = up
    nct = F // FFN_TC
    sub = _sub_rows(tm)
    hpb = tm // FFN_HALO
    n_host = 0 if host is None else len(host[0])
    nrb = LP // tm

    def body(*refs):
        g_ref, v_ref, gh_ref, vh_ref, wg_ref, wv_ref, bg_ref, bv_ref = refs[:8]
        act_ref = refs[8 + n_host]
        gbuf, vbuf = refs[9 + 2 * n_host:11 + 2 * n_host]
        if n_host:
            start, wait = _gather_first_ops(refs[8:8 + n_host], refs[9 + n_host:9 + 2 * n_host],
                                            refs[11 + 2 * n_host], refs[12 + 2 * n_host], host[1])
            pl.when(jnp.logical_and(pl.program_id(0) == 0, pl.program_id(1) == 0))(start)
        r = pl.program_id(1)
        gbuf[FFN_HALO:FFN_HALO + tm, :] = g_ref[...]
        vbuf[FFN_HALO:FFN_HALO + tm, :] = v_ref[...]
        gbuf[0:FFN_HALO, :] = jnp.where(r > 0, gh_ref[...], 0.0)
        vbuf[0:FFN_HALO, :] = jnp.where(r > 0, vh_ref[...], 0.0)
        for s in range(tm // sub):
            for ct in range(FFN_TC // LANE):
                ln = slice(ct * LANE, (ct + 1) * LANE)
                gc = _ffn_conv(gbuf, wg_ref, bg_ref, s, sub, ln)
                vc = _ffn_conv(vbuf, wv_ref, bv_ref, s, sub, ln)
                act_ref[s * sub:(s + 1) * sub, ln] = (gc * _sigmoid(gc) * vc).astype(act_ref.dtype)
        if n_host:
            pl.when(jnp.logical_and(pl.program_id(0) == nct - 1, pl.program_id(1) == nrb - 1))(wait)

    cur = pl.BlockSpec((tm, FFN_TC), lambda c, r: (r, c))
    halo = pl.BlockSpec((FFN_HALO, FFN_TC), lambda c, r: (jnp.maximum(r * hpb - 1, 0), c))
    wg = pl.BlockSpec((8, FFN_TC), lambda c, r: (0, c))
    wv = pl.BlockSpec((8, FFN_TC), lambda c, r: (0, nct + c))
    bg = pl.BlockSpec((1, FFN_TC), lambda c, r: (0, c))
    bv = pl.BlockSpec((1, FFN_TC), lambda c, r: (0, nct + c))
    curv = pl.BlockSpec((tm, FFN_TC), lambda c, r: (r, nct + c))
    halov = pl.BlockSpec((FFN_HALO, FFN_TC), lambda c, r: (jnp.maximum(r * hpb - 1, 0), nct + c))
    out_shape = jax.ShapeDtypeStruct((LP, F), BF16)
    out_specs = cur
    in_specs = [cur, curv, halo, halov, wg, wv, bg, bv]
    args = [upg, upv, upg, upv, w, w, b, b]
    scratch = [pltpu.VMEM((FFN_HALO + tm, FFN_TC), F32)] * 2
    if n_host:
        anyspec = pl.BlockSpec(memory_space=pl.ANY)
        in_specs += [anyspec] * n_host
        args += list(host[0])
        out_shape = (out_shape,) + tuple(jax.ShapeDtypeStruct(cut.full, s.dtype) for cut, s in zip(host[1], host[0]))
        out_specs = (cur,) + (anyspec,) * n_host
        scratch += [pltpu.SemaphoreType.DMA((4 * n_host,)), pltpu.SemaphoreType.DMA((4 * n_host,))]
    sem = ("arbitrary", "arbitrary") if n_host else ("parallel", "parallel")
    return pl.pallas_call(
        body, name=name, out_shape=out_shape, grid=(nct, nrb), in_specs=in_specs, out_specs=out_specs,
        scratch_shapes=scratch, compiler_params=_params(sem))(*args)


def _ffn_act_bwd(up, dact, w, b, tm, name, comm=None):
    LP, F = up.shape[0], up.shape[1] // 2
    upg = upv = up
    nct = F // FFN_TC
    sub = _sub_rows(tm)
    hpb = tm // FFN_HALO
    nblk = LP // tm
    last_halo = LP // FFN_HALO - 1
    TB = tm + 2 * FFN_HALO
    n_comm = 0 if comm is None else len(comm[0])

    def body(*refs):
        (g_ref, v_ref, gp_ref, vp_ref, gn_ref, vn_ref, da_ref, dan_ref,
         wg_ref, wv_ref, bg_ref, bv_ref) = refs[:12]
        dup_ref, dwg_ref, dwv_ref, dbg_ref, dbv_ref = refs[12 + n_comm:17 + n_comm]
        gbuf, vbuf, dgb, dvb = refs[17 + 2 * n_comm:21 + 2 * n_comm]
        if n_comm:
            start, wait = _chip_exchange_ops(refs[12:12 + n_comm], refs[17 + n_comm:17 + 2 * n_comm],
                                             refs[21 + 2 * n_comm], refs[22 + 2 * n_comm], comm[1])
            pl.when(jnp.logical_and(pl.program_id(0) == 0, pl.program_id(1) == 0))(start)
        r = pl.program_id(1)
        dg_ref = dup_ref.at[0]
        dv_ref = dup_ref.at[1]
        first = r == 0
        last = r == nblk - 1

        @pl.when(first)
        def _():
            dwg_ref[...] = jnp.zeros_like(dwg_ref)
            dwv_ref[...] = jnp.zeros_like(dwv_ref)
            dbg_ref[...] = jnp.zeros_like(dbg_ref)
            dbv_ref[...] = jnp.zeros_like(dbv_ref)

        for buf, c_ref, p_ref, n_ref in ((gbuf, g_ref, gp_ref, gn_ref), (vbuf, v_ref, vp_ref, vn_ref)):
            buf[0:FFN_HALO, :] = jnp.where(first, 0.0, p_ref[...])
            buf[FFN_HALO:FFN_HALO + tm, :] = c_ref[...]
            buf[FFN_HALO + tm:TB, :] = jnp.where(last, 0.0, n_ref[...])

        def dconv(s0, nrows, ln, dact_v):
            xg = [gbuf[s0 - (FFN_K - 1) + kk:s0 - (FFN_K - 1) + kk + nrows, ln] for kk in range(FFN_K)]
            xv = [vbuf[s0 - (FFN_K - 1) + kk:s0 - (FFN_K - 1) + kk + nrows, ln] for kk in range(FFN_K)]
            gc = jnp.broadcast_to(bg_ref[:, ln], (nrows, LANE))
            vc = jnp.broadcast_to(bv_ref[:, ln], (nrows, LANE))
            for kk in range(FFN_K):
                gc = gc + wg_ref[kk:kk + 1, ln] * xg[kk]
                vc = vc + wv_ref[kk:kk + 1, ln] * xv[kk]
            sg = _sigmoid(gc)
            return dact_v * vc * (sg * (1.0 + gc * (1.0 - sg))), dact_v * (gc * sg), xg, xv

        colsum = lambda t: jnp.sum(t, axis=0, keepdims=True)
        for ct in range(FFN_TC // LANE):
            ln = slice(ct * LANE, (ct + 1) * LANE)
            zero = jnp.zeros((1, LANE), F32)
            dwg, dwv, dbg, dbv = [zero] * FFN_K, [zero] * FFN_K, zero, zero
            for s in range(tm // sub):
                dgc, dvc, xg, xv = dconv(FFN_HALO + s * sub, sub, ln, da_ref[s * sub:(s + 1) * sub, ln])
                dgb[s * sub:(s + 1) * sub, ln] = dgc
                dvb[s * sub:(s + 1) * sub, ln] = dvc
                dwg = [dwg[kk] + colsum(dgc * xg[kk]) for kk in range(FFN_K)]
                dwv = [dwv[kk] + colsum(dvc * xv[kk]) for kk in range(FFN_K)]
                dbg, dbv = dbg + colsum(dgc), dbv + colsum(dvc)
            for kk in range(FFN_K):
                dwg_ref[kk:kk + 1, ln] += dwg[kk]
                dwv_ref[kk:kk + 1, ln] += dwv[kk]
            dbg_ref[:, ln] += dbg
            dbv_ref[:, ln] += dbv
            dgc, dvc, _, _ = dconv(FFN_HALO + tm, FFN_HALO, ln, jnp.where(last, 0.0, dan_ref[:, ln]))
            dgb[tm:tm + FFN_HALO, ln] = dgc
            dvb[tm:tm + FFN_HALO, ln] = dvc
            for dbuf, w_ref, dout in ((dgb, wg_ref, dg_ref), (dvb, wv_ref, dv_ref)):
                for s in range(tm // sub):
                    acc = jnp.zeros((sub, LANE), F32)
                    for kk in range(FFN_K):
                        fo = s * sub + (FFN_K - 1) - kk
                        acc = acc + w_ref[kk:kk + 1, ln] * dbuf[fo:fo + sub, ln]
                    dout[s * sub:(s + 1) * sub, ln] = acc.astype(dout.dtype)
        if n_comm:
            pl.when(jnp.logical_and(pl.program_id(0) == nct - 1, pl.program_id(1) == nblk - 1))(wait)

    cur = pl.BlockSpec((tm, FFN_TC), lambda c, r: (r, c))
    prev = pl.BlockSpec((FFN_HALO, FFN_TC), lambda c, r: (jnp.maximum(r * hpb - 1, 0), c))
    nxt = pl.BlockSpec((FFN_HALO, FFN_TC), lambda c, r: (jnp.minimum((r + 1) * hpb, last_halo), c))
    wg = pl.BlockSpec((8, FFN_TC), lambda c, r: (0, c))
    wv = pl.BlockSpec((8, FFN_TC), lambda c, r: (0, nct + c))
    bg = pl.BlockSpec((1, FFN_TC), lambda c, r: (0, c))
    bv = pl.BlockSpec((1, FFN_TC), lambda c, r: (0, nct + c))
    curv = pl.BlockSpec((tm, FFN_TC), lambda c, r: (r, nct + c))
    prevv = pl.BlockSpec((FFN_HALO, FFN_TC), lambda c, r: (jnp.maximum(r * hpb - 1, 0), nct + c))
    nxtv = pl.BlockSpec((FFN_HALO, FFN_TC), lambda c, r: (jnp.minimum((r + 1) * hpb, last_halo), nct + c))
    out_shape = (jax.ShapeDtypeStruct((2, LP, F), BF16),
                 jax.ShapeDtypeStruct((8, F), F32), jax.ShapeDtypeStruct((8, F), F32),
                 jax.ShapeDtypeStruct((1, F), F32), jax.ShapeDtypeStruct((1, F), F32))
    out_specs = (pl.BlockSpec((2, tm, FFN_TC), lambda c, r: (0, r, c)),
                 pl.BlockSpec((8, FFN_TC), lambda c, r: (0, c)),
                 pl.BlockSpec((8, FFN_TC), lambda c, r: (0, c)),
                 pl.BlockSpec((1, FFN_TC), lambda c, r: (0, c)),
                 pl.BlockSpec((1, FFN_TC), lambda c, r: (0, c)))
    in_specs = [cur, curv, prev, prevv, nxt, nxtv, cur, nxt, wg, wv, bg, bv]
    args = [upg, upv, upg, upv, upg, upv, dact, dact, w, w, b, b]
    scratch = [pltpu.VMEM((TB, FFN_TC), F32), pltpu.VMEM((TB, FFN_TC), F32),
               pltpu.VMEM((tm + FFN_HALO, FFN_TC), F32), pltpu.VMEM((tm + FFN_HALO, FFN_TC), F32)]
    if n_comm:
        anyspec = pl.BlockSpec(memory_space=pl.ANY)
        in_specs += [anyspec] * n_comm
        args += list(comm[0])
        out_shape += _chip_exchange_shapes(*comm)
        out_specs += (anyspec,) * n_comm
        scratch += [pltpu.SemaphoreType.DMA((3 * n_comm,)), pltpu.SemaphoreType.DMA((3 * n_comm,))]
    sem = ("arbitrary", "arbitrary") if n_comm else ("parallel", "arbitrary")
    dup, dwg, dwv, dbg, dbv, *others = pl.pallas_call(
        body, name=name, out_shape=out_shape, grid=(nct, nblk), in_specs=in_specs, out_specs=out_specs,
        scratch_shapes=scratch, compiler_params=_params(sem))(*args)
    return (dup, jnp.concatenate([dwg, dwv], axis=1), jnp.concatenate([dbg, dbv], axis=1)) + tuple(others)


POOL_HALO = 16


def _pool_fwd(h, g, pw, pb, ps, tm, name):
    LP, Dm = h.shape
    sub = _sub_rows(tm)
    hpb = tm // POOL_HALO

    def body(h_ref, hh_ref, g_ref, pw_ref, pb_ref, ps_ref, o_ref, d_ref, buf):
        r = pl.program_id(0)
        gg = g_ref[...]

        def norm(x):
            return x * lax.rsqrt(jnp.mean(x * x, axis=1, keepdims=True) + RMS_EPS) * gg

        x = h_ref[...]
        buf[POOL_HALO:POOL_HALO + tm, :] = norm(x)
        buf[0:POOL_HALO, :] = jnp.where(r > 0, norm(hh_ref[...]), 0.0)
        for gi, w in enumerate(POOL_WINDOWS):
            ln = slice(gi * POOL_G, (gi + 1) * POOL_G)
            for s in range(tm // sub):
                base = POOL_HALO + s * sub
                acc = buf[base:base + sub, ln]
                for jj in range(1, w):
                    acc = acc + buf[base - jj:base - jj + sub, ln]
                t = r * tm + s * sub + lax.broadcasted_iota(jnp.int32, (sub, 1), 0)
                cnt = jnp.minimum(t + 1, w).astype(F32)
                d_ref[s * sub:(s + 1) * sub, ln] = (acc / cnt - buf[base:base + sub, ln]).astype(d_ref.dtype)
            y = jnp.dot(d_ref[:, ln], pw_ref[gi], preferred_element_type=F32) + pb_ref[:, ln]
            o_ref[:, ln] = x[:, ln] + y * ps_ref[:, ln]

    row = pl.BlockSpec((tm, Dm), lambda r: (r, 0))
    halo = pl.BlockSpec((POOL_HALO, Dm), lambda r: (jnp.maximum(r * hpb - 1, 0), 0))
    vec = pl.BlockSpec((1, Dm), lambda r: (0, 0))
    wsp = pl.BlockSpec((len(POOL_WINDOWS), POOL_G, POOL_G), lambda r: (0, 0, 0))
    return pl.pallas_call(
        body, name=name,
        out_shape=(jax.ShapeDtypeStruct((LP, Dm), F32), jax.ShapeDtypeStruct((LP, Dm), BF16)),
        grid=(LP // tm,), in_specs=[row, halo, vec, wsp, vec, vec], out_specs=(row, row),
        scratch_shapes=[pltpu.VMEM((POOL_HALO + tm, Dm), F32)],
        compiler_params=_params(("parallel",)))(h, h, g, pw, pb, ps)


def _pool_bwd(h, g, d, pw, pb, ps, dh_out, tm, name):
    LP, Dm = h.shape
    sub = _sub_rows(tm)
    hpb = tm // POOL_HALO
    nblk = LP // tm
    last_halo = LP // POOL_HALO - 1
    nt = (((1,), (1,)), ((), ()))
    tn = (((0,), (0,)), ((), ()))

    def body(h_ref, g_ref, d_ref, pw_ref, pb_ref, ps_ref, do_ref, don_ref,
             dh_ref, dpw_ref, dpb_ref, dps_ref, dg_ref, ebuf, ddb, dnb):
        r = pl.program_id(0)

        @pl.when(r == 0)
        def _():
            dpw_ref[...] = jnp.zeros_like(dpw_ref)
            dpb_ref[...] = jnp.zeros_like(dpb_ref)
            dps_ref[...] = jnp.zeros_like(dps_ref)
            dg_ref[...] = jnp.zeros_like(dg_ref)

        for gi, w in enumerate(POOL_WINDOWS):
            ln = slice(gi * POOL_G, (gi + 1) * POOL_G)
            wg = pw_ref[gi]
            dog = do_ref[:, ln]
            dg_b = d_ref[:, ln]
            y_pre = jnp.dot(dg_b, wg, preferred_element_type=F32) + pb_ref[:, ln]
            dps_ref[:, ln] += jnp.sum(dog * y_pre, axis=0, keepdims=True)
            dy = dog * ps_ref[:, ln]
            dpb_ref[:, ln] += jnp.sum(dy, axis=0, keepdims=True)
            dyb = dy.astype(BF16)
            dpw_ref[gi] += lax.dot_general(dg_b, dyb, tn, preferred_element_type=F32)
            dd = lax.dot_general(dyb, wg, nt, preferred_element_type=F32)
            ddb[:, ln] = dd
            t = r * tm + lax.broadcasted_iota(jnp.int32, (tm, 1), 0)
            ebuf[0:tm, ln] = dd / jnp.minimum(t + 1, w).astype(F32)
            dyn = (don_ref[:, ln] * ps_ref[:, ln]).astype(BF16)
            ddn = lax.dot_general(dyn, wg, nt, preferred_element_type=F32)
            tn_ = (r + 1) * tm + lax.broadcasted_iota(jnp.int32, (POOL_HALO, 1), 0)
            ebuf[tm:tm + POOL_HALO, ln] = jnp.where(r < nblk - 1, ddn / jnp.minimum(tn_ + 1, w).astype(F32), 0.0)
            for s in range(tm // sub):
                acc = ebuf[s * sub:(s + 1) * sub, ln]
                for jj in range(1, w):
                    acc = acc + ebuf[s * sub + jj:s * sub + jj + sub, ln]
                dnb[s * sub:(s + 1) * sub, ln] = acc - ddb[s * sub:(s + 1) * sub, ln]
        x = h_ref[...]
        rr = lax.rsqrt(jnp.mean(x * x, axis=1, keepdims=True) + RMS_EPS)
        xhat = x * rr
        dn = dnb[...]
        dxh = dn * g_ref[...]
        dh_ref[...] = do_ref[...] + rr * (dxh - xhat * jnp.mean(dxh * xhat, axis=1, keepdims=True))
        dg_ref[...] += jnp.sum(dn * xhat, axis=0, keepdims=True)

    row = pl.BlockSpec((tm, Dm), lambda r: (r, 0))
    nxt = pl.BlockSpec((POOL_HALO, Dm), lambda r: (jnp.minimum((r + 1) * hpb, last_halo), 0))
    vec = pl.BlockSpec((1, Dm), lambda r: (0, 0))
    wsp = pl.BlockSpec((len(POOL_WINDOWS), POOL_G, POOL_G), lambda r: (0, 0, 0))
    return pl.pallas_call(
        body, name=name,
        out_shape=(jax.ShapeDtypeStruct((LP, Dm), F32),
                   jax.ShapeDtypeStruct((len(POOL_WINDOWS), POOL_G, POOL_G), F32),
                   jax.ShapeDtypeStruct((1, Dm), F32), jax.ShapeDtypeStruct((1, Dm), F32),
                   jax.ShapeDtypeStruct((1, Dm), F32)),
        grid=(nblk,), in_specs=[row, vec, row, wsp, vec, vec, row, nxt],
        out_specs=(row, wsp, vec, vec, vec),
        scratch_shapes=[pltpu.VMEM((tm + POOL_HALO, Dm), F32), pltpu.VMEM((tm, Dm), F32),
                        pltpu.VMEM((tm, Dm), F32)],
        compiler_params=_params(("arbitrary",)))(h, g, d, pw, pb, ps, dh_out, dh_out)


def _adamw(w, g, m, v, name):
    shape = w.shape
    cols = shape[-1]
    rows = int(np.prod(shape[:-1])) if len(shape) > 1 else 1
    w2, g2, m2, v2 = (t.reshape(rows, cols) for t in (w, g, m, v))
    tr = rows
    for cand in (256, 128, 64, 32, 16, 8):
        if rows % cand == 0 and rows > cand:
            tr = cand
            break
    c1 = float(1.0 - ADAM_B1 ** ADAM_STEP)
    c2 = float(1.0 - ADAM_B2 ** ADAM_STEP)

    def body(w_ref, g_ref, m_ref, v_ref, d_ref, mo_ref, vo_ref):
        gg = g_ref[...]
        mn = ADAM_B1 * m_ref[...] + (1.0 - ADAM_B1) * gg
        vn = ADAM_B2 * v_ref[...] + (1.0 - ADAM_B2) * (gg * gg)
        m_hat = mn / c1
        v_hat = vn / c2
        d_ref[...] = -ADAM_LR * (m_hat / (jnp.sqrt(v_hat) + ADAM_EPS) + ADAM_WD * w_ref[...])
        mo_ref[...] = mn
        vo_ref[...] = vn

    spec = pl.BlockSpec((tr, cols), lambda i: (i, 0))
    sds = jax.ShapeDtypeStruct((rows, cols), F32)
    d2, mo, vo = pl.pallas_call(
        body, name=name, out_shape=(sds, sds, sds), grid=(rows // tr,),
        in_specs=[spec] * 4, out_specs=(spec,) * 3,
        compiler_params=_params(("parallel",)))(w2, g2, m2, v2)
    return d2.reshape(shape), mo.reshape(shape), vo.reshape(shape)


def _row_tiles(LP):
    tm = LP // 4
    assert LP % 4 == 0 and tm % CONV_HALO == 0 and LP % ATT_BLK == 0, LP
    return tm, LP // 2


def _heads(t, LP):
    return t.reshape(LP, HEADS, HEAD_DIM).transpose(1, 0, 2)


def _unheads(t, LP):
    return t.transpose(1, 0, 2).reshape(LP, FOX_W)


def _ffn_fwd(h, gain, wug, wuv, cw, cb, wd, tm, tmm, tag):
    n = _rms_fwd(h, gain, BF16, tm, f"ffn_norm_{tag}")
    upg = _mm(n, wug, "nn", F32, tmm, 256, f"ffn_up_gate_{tag}")
    upv = _mm(n, wuv, "nn", F32, tmm, 256, f"ffn_up_val_{tag}")
    act = _ffn_act_fwd(upg, upv, cw, cb, tm, f"ffn_act_{tag}")
    out = _mm(act, wd, "nn", F32, tmm, 512, f"ffn_down_{tag}", add=h)
    return out, (n, upg, upv, act)


def _ffn_bwd(h, gain, wug, wuv, cw, cb, wd, saved, dout, tm, tmm, tag):
    n, upg, upv, act = saved
    dact = _mm(dout, wd, "nt", F32, tmm, 256, f"ffn_dact_{tag}")
    dwd = _mm(act, dout, "tn", F32, 256, 512, f"ffn_dwdown_{tag}")
    dupg, dupv, dcw, dcb = _ffn_act_bwd(upg, upv, dact, cw, cb, tm, f"ffn_act_bwd_{tag}")
    dn = _mm(dupg, wug, "nt", F32, tm, 512, f"ffn_dn_gate_{tag}")
    dn = _mm(dupv, wuv, "nt", F32, tm, 512, f"ffn_dn_val_{tag}", add=dn)
    dwug = _mm(n, dupg, "tn", F32, 512, 256, f"ffn_dwup_gate_{tag}")
    dwuv = _mm(n, dupv, "tn", F32, 512, 256, f"ffn_dwup_val_{tag}")
    dh, dgain = _rms_bwd(h, gain, dn, dout, tm, f"ffn_norm_bwd_{tag}")
    return dh, dict(gain=dgain, wug=dwug, wuv=dwuv, cw=dcw[:FFN_K], cb=dcb, wd=dwd)


def _local_step(h0, tgt, W, n_real):
    LP = h0.shape[0]
    tm, tmm = _row_tiles(LP)
    nb = LP // ATT_BLK
    G = {}

    n0 = _rms_fwd(h0, W["mix_norm_even"], BF16, tm, "mix_norm_even")
    proj = _mm(n0, W["w_in_p"], "nn", F32, tmm, 384, "in_proj")
    c = _fgate_fwd(proj, W["b_f_p"], "forget_gate")
    cT = c[:, :HEADS].T
    c_col = cT[:, :, None]
    c_row = cT.reshape(HEADS, nb, 1, ATT_BLK)
    qkv = proj[:, :3 * FOX_W].astype(BF16)
    q, k, v = (_heads(qkv[:, i * FOX_W:(i + 1) * FOX_W], LP) for i in range(3))
    o, lse = _attn_fwd(q, k, v, c_col, c_row, "fox_attention")
    u1, u = _conf_fwd(proj, W["conv_w_p"], W["conv_b"], W["ln_g"], W["ln_b"], tm, "conformer")
    cat = jnp.concatenate([_unheads(o, LP).astype(BF16), u], axis=1)
    h1 = _mm(cat, W["w_out"], "nn", F32, tmm, 512, "out_proj", add=h0)
    h2, ffn0 = _ffn_fwd(h1, W["ffn_norm"][0:1], W["w_up_g"][0], W["w_up_v"][0], W["ffn_conv_w_p"][0],
                        W["ffn_conv_b"][0:1], W["w_down"][0], tm, tmm, "0")
    h3, dpool = _pool_fwd(h2, W["mix_norm_odd"], W["pool_w"], W["pool_b"], W["pool_scale"], tm, "pool_mixer")
    h4, ffn1 = _ffn_fwd(h3, W["ffn_norm"][1:2], W["w_up_g"][1], W["w_up_v"][1], W["ffn_conv_w_p"][1],
                        W["ffn_conv_b"][1:2], W["w_down"][1], tm, tmm, "1")
    loss, dh4, G["final_norm"] = _loss_head(h4, W["final_norm"], tgt, n_real, tm, "loss_head")

    dh3, g1 = _ffn_bwd(h3, W["ffn_norm"][1:2], W["w_up_g"][1], W["w_up_v"][1], W["ffn_conv_w_p"][1],
                       W["ffn_conv_b"][1:2], W["w_down"][1], ffn1, dh4, tm, tmm, "1")
    dh2, G["pool_w"], G["pool_b"], G["pool_scale"], G["mix_norm_odd"] = _pool_bwd(
        h2, W["mix_norm_odd"], dpool, W["pool_w"], W["pool_b"], W["pool_scale"], dh3, tm, "pool_mixer_bwd")
    dh1, g0 = _ffn_bwd(h1, W["ffn_norm"][0:1], W["w_up_g"][0], W["w_up_v"][0], W["ffn_conv_w_p"][0],
                       W["ffn_conv_b"][0:1], W["w_down"][0], ffn0, dh2, tm, tmm, "0")
    for key in ("gain", "wug", "wuv", "cw", "cb", "wd"):
        G["ffn_" + key] = (g0[key], g1[key])

    dcat = _mm(dh1, W["w_out"], "nt", F32, tmm, 512, "out_proj_dx")
    G["w_out"] = _mm(cat, dh1, "tn", F32, 512, 512, "out_proj_dw")
    dadg, dcw, G["conv_b"], G["ln_g"], G["ln_b"] = _conf_bwd(
        proj, u1, dcat, W["conv_w_p"], W["ln_g"], W["ln_b"], tm, "conformer_bwd")
    G["conv_w"] = dcw[:CONV_K]
    do = _heads(dcat[:, :FOX_W], LP)
    dq, dk, dv, dcq, dck = _attn_bwd(q, k, v, o, do, lse, c_col, c_row, "fox_attention_bwd")
    dc = jnp.pad((dcq.reshape(HEADS, LP) + dck.reshape(HEADS, LP)).T, ((0, 0), (0, LANE - HEADS)))
    dfl, dbf = _fgate_bwd(proj, W["b_f_p"], dc, "forget_gate_bwd")
    G["b_f"] = dbf[:, :HEADS]
    dproj = jnp.concatenate([_unheads(t, LP).astype(BF16) for t in (dq, dk, dv)] + [dadg, dfl], axis=1)
    dn0 = _mm(dproj, W["w_in_p"], "nt", F32, tmm, 512, "in_proj_dx")
    G["w_in_p"] = _mm(n0, dproj, "tn", F32, 512, 384, "in_proj_dw")
    dh0, G["mix_norm_even"] = _rms_bwd(h0, W["mix_norm_even"], dn0, dh1, tm, "mix_norm_even_bwd")
    return loss, dh0, G


def _compute_layout(P):
    w_in = P["w_in"].reshape(D_MODEL, IN_COLS)
    qkv, f, ag = w_in[:, :3 * FOX_W], w_in[:, 3 * FOX_W:3 * FOX_W + HEADS], w_in[:, 3 * FOX_W + HEADS:]
    w_in_p = jnp.concatenate([qkv, ag, f, jnp.zeros((D_MODEL, LANE - HEADS), w_in.dtype)], axis=1).astype(BF16)
    w_up = P["w_up"].astype(BF16)
    return dict(
        mix_norm_even=P["mix_norm_even"].reshape(1, D_MODEL).astype(F32),
        w_in_p=w_in_p,
        b_f_p=jnp.pad(P["b_f"].reshape(1, HEADS).astype(F32), ((0, 0), (0, LANE - HEADS))),
        conv_w_p=jnp.pad(P["conv_w"].reshape(CONV_K, CONV_CH).astype(F32), ((0, CONV_HALO - CONV_K), (0, 0))),
        conv_b=P["conv_b"].reshape(1, CONV_CH).astype(F32),
        ln_g=P["ln_g"].reshape(1, CONV_CH).astype(F32),
        ln_b=P["ln_b"].reshape(1, CONV_CH).astype(F32),
        w_out=P["w_out"].reshape(D_MODEL, D_MODEL).astype(BF16),
        mix_norm_odd=P["mix_norm_odd"].reshape(1, D_MODEL).astype(F32),
        pool_w=P["pool_w"].reshape(len(POOL_WINDOWS), POOL_G, POOL_G).astype(BF16),
        pool_b=P["pool_b"].reshape(1, D_MODEL).astype(F32),
        pool_scale=P["pool_scale"].reshape(1, D_MODEL).astype(F32),
        ffn_norm=P["ffn_norm"].astype(F32),
        w_up_g=w_up[:, :, :D_FF],
        w_up_v=w_up[:, :, D_FF:],
        ffn_conv_w_p=jnp.pad(P["ffn_conv_w"].astype(F32), ((0, 0), (0, 8 - FFN_K), (0, 0))),
        ffn_conv_b=P["ffn_conv_b"].astype(F32),
        w_down=P["w_down"].astype(BF16),
        final_norm=P["final_norm"].reshape(1, D_MODEL).astype(F32),
    )


def _reference_layout(G, dh0):
    gp = G["w_in_p"]
    g_w_in = jnp.concatenate([gp[:, :3 * FOX_W], gp[:, 3 * FOX_W + 2 * CONV_CH:3 * FOX_W + 2 * CONV_CH + HEADS],
                              gp[:, 3 * FOX_W:3 * FOX_W + 2 * CONV_CH]], axis=1)
    return dict(
        meta_tokens=dh0[:N_META],
        mix_norm_even=G["mix_norm_even"],
        w_in=g_w_in[None],
        b_f=G["b_f"],
        conv_w=G["conv_w"][None],
        conv_b=G["conv_b"],
        ln_g=G["ln_g"],
        ln_b=G["ln_b"],
        w_out=G["w_out"][None],
        mix_norm_odd=G["mix_norm_odd"],
        pool_w=G["pool_w"][None],
        pool_b=G["pool_b"].reshape(1, len(POOL_WINDOWS), POOL_G),
        pool_scale=G["pool_scale"],
        ffn_norm=jnp.concatenate(G["ffn_gain"], axis=0),
        w_up=jnp.stack([jnp.concatenate([g, v], axis=1) for g, v in zip(G["ffn_wug"], G["ffn_wuv"])]),
        ffn_conv_w=jnp.stack(G["ffn_cw"]),
        ffn_conv_b=jnp.concatenate(G["ffn_cb"], axis=0),
        w_down=jnp.stack(G["ffn_wd"]),
        final_norm=G["final_norm"].reshape(D_MODEL),
    )


MESH = pl.DeviceIdType.MESH
ANY = pl.BlockSpec(memory_space=pl.ANY)
PACK_COLS = 1024


def _coords():
    return lax.axis_index("x"), lax.axis_index("y"), lax.axis_index("c")


def _other_chips(x, y):
    return [(1 - x, y), (x, 1 - y), (1 - x, 1 - y)]


def _allgather_chips(pack):
    R, C = pack.shape
    R2 = R // 2

    def body(x_ref, o_ref, send_sems, recv_sems, local_sem):
        x, y, c = _coords()
        sibling = (x, y, 1 - c)
        chips = _other_chips(x, y)

        def slot(px, py, half):
            return o_ref.at[2 * px + py, pl.ds(half * R2, R2), :]

        def copy(k, src, dst, to):
            return pltpu.make_async_remote_copy(src_ref=src, dst_ref=dst, send_sem=send_sems.at[k],
                                                recv_sem=recv_sems.at[k], device_id=to, device_id_type=MESH)

        mine = pltpu.make_async_copy(x_ref, o_ref.at[2 * x + y], local_sem)
        mine.start()
        my_half = x_ref.at[pl.ds(c * R2, R2), :]
        first = [copy(j, my_half, slot(x, y, c), (*chip, c)) for j, chip in enumerate(chips)]
        for cp in first:
            cp.start()
        passed = [copy(3 + j, slot(*chip, c), slot(*chip, c), sibling) for j, chip in enumerate(chips)]
        for j, chip in enumerate(chips):
            copy(j, my_half, slot(*chip, c), sibling).wait_recv()
            passed[j].start()
        for j, chip in enumerate(chips):
            copy(3 + j, my_half, slot(*chip, 1 - c), sibling).wait_recv()
        for cp in first + passed:
            cp.wait_send()
        mine.wait()

    return pl.pallas_call(
        body, name="allgather_weights", out_shape=jax.ShapeDtypeStruct((N_CHIPS, R, C), pack.dtype),
        in_specs=[ANY], out_specs=ANY,
        scratch_shapes=[pltpu.SemaphoreType.DMA((6,)), pltpu.SemaphoreType.DMA((6,)), pltpu.SemaphoreType.DMA],
    )(pack)


def _pair_exchange(G):
    n, R, C = G.shape
    R2 = R // 2

    def body(g_ref, o_ref, send_sem, recv_sem):
        x, y, c = _coords()
        src = g_ref.at[pl.ds(0, n), pl.ds((1 - c) * R2, R2), :]
        cp = pltpu.make_async_remote_copy(src_ref=src, dst_ref=o_ref, send_sem=send_sem, recv_sem=recv_sem,
                                          device_id=(x, y, 1 - c), device_id_type=MESH)
        cp.start()
        cp.wait()

    return pl.pallas_call(
        body, name="grad_pair_exchange", out_shape=jax.ShapeDtypeStruct((n, R2, C), G.dtype),
        in_specs=[ANY], out_specs=ANY,
        scratch_shapes=[pltpu.SemaphoreType.DMA, pltpu.SemaphoreType.DMA],
    )(G)


def _row_tile(rows, align, cap):
    best = None
    for t in range(align, min(rows, cap) + 1, align):
        if rows % t == 0:
            best = t
    assert best is not None, (rows, align, cap)
    return best


def _pair_sum(G, recv):
    n, R, C = G.shape
    R2 = R // 2
    tr = _row_tile(R2, 16, 704)
    nrb = R2 // tr
    half = lax.axis_index("c").astype(jnp.int32).reshape(1)

    def body(c_ref, g_ref, r_ref, o_ref):
        o_ref[...] = (g_ref[...] + r_ref[...]).astype(o_ref.dtype)

    return pl.pallas_call(
        body, name="grad_pair_sum", out_shape=jax.ShapeDtypeStruct((n, R2, C), BF16),
        grid_spec=pltpu.PrefetchScalarGridSpec(
            num_scalar_prefetch=1, grid=(n, nrb),
            in_specs=[pl.BlockSpec((None, tr, C), lambda j, i, c_ref: (j, c_ref[0] * nrb + i, 0)),
                      pl.BlockSpec((None, tr, C), lambda j, i, c_ref: (j, i, 0))],
            out_specs=pl.BlockSpec((None, tr, C), lambda j, i, c_ref: (j, i, 0))),
        compiler_params=_params(("parallel", "parallel")))(half, G, recv)


def _chip_exchange(P):
    n, R2, C = P.shape

    def body(p_ref, o_ref, send_sems, recv_sems, local_sem):
        x, y, c = _coords()
        me = 2 * x + y
        chips = _other_chips(x, y)
        mine = pltpu.make_async_copy(p_ref.at[me], o_ref.at[me], local_sem)
        mine.start()
        sends = [pltpu.make_async_remote_copy(
            src_ref=p_ref.at[2 * px + py], dst_ref=o_ref.at[me], send_sem=send_sems.at[k],
            recv_sem=recv_sems.at[k], device_id=(px, py, c), device_id_type=MESH)
            for k, (px, py) in enumerate(chips)]
        for cp in sends:
            cp.start()
        for k, (px, py) in enumerate(chips):
            pltpu.make_async_remote_copy(
                src_ref=p_ref.at[me], dst_ref=o_ref.at[2 * px + py], send_sem=send_sems.at[k],
                recv_sem=recv_sems.at[k], device_id=(px, py, c), device_id_type=MESH).wait_recv()
        for cp in sends:
            cp.wait_send()
        mine.wait()

    return pl.pallas_call(
        body, name="grad_chip_exchange", out_shape=jax.ShapeDtypeStruct((n, R2, C), P.dtype),
        in_specs=[ANY], out_specs=ANY,
        scratch_shapes=[pltpu.SemaphoreType.DMA((3,)), pltpu.SemaphoreType.DMA((3,)), pltpu.SemaphoreType.DMA],
    )(P)


def _chip_sum(X):
    n, R2, C = X.shape
    tr = _row_tile(R2, 16, 704)

    def body(x_ref, o_ref):
        acc = x_ref[0].astype(F32)
        for s in range(1, n):
            acc = acc + x_ref[s].astype(F32)
        o_ref[...] = acc

    return pl.pallas_call(
        body, name="grad_chip_sum", out_shape=jax.ShapeDtypeStruct((R2, C), F32), grid=(R2 // tr,),
        in_specs=[pl.BlockSpec((n, tr, C), lambda i: (0, i, 0))],
        out_specs=pl.BlockSpec((tr, C), lambda i: (i, 0)),
        compiler_params=_params(("parallel",)))(X)


def _pair_allgather(Q):
    R2, C = Q.shape

    def body(q_ref, o_ref, send_sem, recv_sem, local_sem):
        x, y, c = _coords()
        mine = pltpu.make_async_copy(q_ref, o_ref.at[c], local_sem)
        mine.start()
        cp = pltpu.make_async_remote_copy(src_ref=q_ref, dst_ref=o_ref.at[c], send_sem=send_sem,
                                          recv_sem=recv_sem, device_id=(x, y, 1 - c), device_id_type=MESH)
        cp.start()
        pltpu.make_async_remote_copy(src_ref=q_ref, dst_ref=o_ref.at[1 - c], send_sem=send_sem,
                                     recv_sem=recv_sem, device_id=(x, y, 1 - c), device_id_type=MESH).wait_recv()
        cp.wait_send()
        mine.wait()

    return pl.pallas_call(
        body, name="grad_pair_allgather", out_shape=jax.ShapeDtypeStruct((2, R2, C), Q.dtype),
        in_specs=[ANY], out_specs=ANY,
        scratch_shapes=[pltpu.SemaphoreType.DMA, pltpu.SemaphoreType.DMA, pltpu.SemaphoreType.DMA],
    )(Q)


def _allreduce_small(pack):
    Rs, C = pack.shape
    n_dev = 8

    def body(x_ref, o_ref, buf, send_sems, recv_sems):
        x, y, c = _coords()
        me = 4 * x + 2 * y + c
        buf[me] = x_ref[...]
        peers = []
        for rel in range(1, n_dev):
            px = 1 - x if rel & 4 else x
            py = 1 - y if rel & 2 else y
            pc = 1 - c if rel & 1 else c
            peers.append((px, py, pc))
        sends = [pltpu.make_async_remote_copy(
            src_ref=x_ref, dst_ref=buf.at[me], send_sem=send_sems.at[k], recv_sem=recv_sems.at[k],
            device_id=peer, device_id_type=MESH) for k, peer in enumerate(peers)]
        for cp in sends:
            cp.start()
        for k, (px, py, pc) in enumerate(peers):
            pltpu.make_async_remote_copy(
                src_ref=x_ref, dst_ref=buf.at[4 * px + 2 * py + pc], send_sem=send_sems.at[k],
                recv_sem=recv_sems.at[k], device_id=(px, py, pc), device_id_type=MESH).wait_recv()
        for cp in sends:
            cp.wait_send()
        acc = buf[0]
        for d in range(1, n_dev):
            acc = acc + buf[d]
        o_ref[...] = acc

    vm = pl.BlockSpec(memory_space=pltpu.VMEM)
    return pl.pallas_call(
        body, name="allreduce_replicated", out_shape=jax.ShapeDtypeStruct((Rs, C), F32),
        in_specs=[vm], out_specs=vm,
        scratch_shapes=[pltpu.VMEM((n_dev, Rs, C), F32), pltpu.SemaphoreType.DMA((n_dev - 1,)),
                        pltpu.SemaphoreType.DMA((n_dev - 1,))],
    )(pack)


SHARDED = (
    ("w_in", 2, True), ("w_out", 1, True), ("pool_w", 2, True), ("w_up", 2, True), ("w_down", 1, True),
    ("meta_tokens", 1, False), ("mix_norm_odd", 1, False), ("pool_b", 2, False), ("pool_scale", 1, False),
    ("conv_w", 2, False), ("ffn_conv_w", 2, False))
REPLICATED = ("mix_norm_even", "b_f", "conv_b", "ln_g", "ln_b", "ffn_norm", "ffn_conv_b", "final_norm")
PACK_ROW_ALIGN = 32


def _pad_rows(flat, align_rows, cols):
    rows = -(-flat.shape[-1] // cols)
    rows = -(-rows // align_rows) * align_rows
    pad = rows * cols - flat.shape[-1]
    flat = jnp.pad(flat, [(0, 0)] * (flat.ndim - 1) + [(0, pad)])
    return flat.reshape(flat.shape[:-1] + (rows, cols))


def _pack_weight_shards(shards):
    parts = []
    for name, _, as_bf16 in SHARDED:
        w = shards[name].astype(F32).reshape(-1)
        parts.append(w.astype(BF16) if as_bf16 else lax.bitcast_convert_type(w, BF16).reshape(-1))
    return _pad_rows(jnp.concatenate(parts), PACK_ROW_ALIGN, PACK_COLS)


def _unpack_weights(gathered, shards):
    flat = gathered.reshape(N_CHIPS, -1)
    out, off = {}, 0
    for name, axis, as_bf16 in SHARDED:
        shp = shards[name].shape
        n = int(np.prod(shp))
        if as_bf16:
            t = flat[:, off:off + n]
            off += n
        else:
            t = lax.bitcast_convert_type(flat[:, off:off + 2 * n].reshape(N_CHIPS, n, 2), F32)
            off += 2 * n
        t = t.reshape((N_CHIPS,) + shp)
        out[name] = jnp.concatenate([t[j] for j in range(N_CHIPS)], axis=axis)
    return out


def _pack_grad_shards(grads, shards):
    parts = []
    for name, axis, _ in SHARDED:
        g = grads[name].reshape(shards[name].shape[:axis] + (N_CHIPS, shards[name].shape[axis])
                                + shards[name].shape[axis + 1:])
        parts.append(jnp.moveaxis(g, axis, 0).reshape(N_CHIPS, -1))
    return _pad_rows(jnp.concatenate(parts, axis=1), PACK_ROW_ALIGN, PACK_COLS)


def _unpack_grad_shard(reduced, shards):
    flat = reduced.reshape(-1)
    out, off = {}, 0
    for name, _, _ in SHARDED:
        shp = shards[name].shape
        n = int(np.prod(shp))
        out[name] = flat[off:off + n].reshape(shp)
        off += n
    return out


def _pack_replicated(grads, loss):
    parts = [_pad_rows(grads[name].astype(F32).reshape(-1), 1, LANE).reshape(-1) for name in REPLICATED]
    parts.append(_pad_rows(loss.reshape(-1)[:1], 1, LANE).reshape(-1))
    return _pad_rows(jnp.concatenate(parts), 8, LANE)


def _unpack_replicated(reduced, shapes):
    flat = reduced.reshape(-1)
    out, off = {}, 0
    for name in REPLICATED:
        n = int(np.prod(shapes[name]))
        out[name] = flat[off:off + n].reshape(shapes[name])
        off += -(-n // LANE) * LANE
    return out, flat[off]


def _ffn_fwd2(h, W, layer, tm, tmm, host_up=None, host_act=None):
    tag = str(layer)
    n = _rms_fwd(h, W["ffn_norm"][layer:layer + 1], BF16, tm, f"ffn_norm_{tag}")
    up, *g_up = _mm(n, W["w_up"][layer], "nn", F32, tmm, UP_SHARD, f"ffn_up_{tag}", host=host_up) \
        if host_up else (_mm(n, W["w_up"][layer], "nn", F32, tmm, UP_SHARD, f"ffn_up_{tag}"),)
    act, *g_act = _ffn_act_fwd(up, W["ffn_conv_w_p"][layer], W["ffn_conv_b"][layer:layer + 1], tm,
                               f"ffn_act_{tag}", host=host_act) \
        if host_act else (_ffn_act_fwd(up, W["ffn_conv_w_p"][layer], W["ffn_conv_b"][layer:layer + 1], tm,
                                       f"ffn_act_{tag}"),)
    out = _mm(act, W["w_down"][layer], "nn", F32, tm, D_MODEL, f"ffn_down_{tag}", add=h)
    return out, (n, up, act), g_up + g_act


def _ffn_bwd2(h, W, layer, saved, dout, tm, tmm, reduce=None):
    tag = str(layer)
    n, up, act = saved
    parts, comm = [], None
    if reduce is None:
        dact = _mm(dout, W["w_down"][layer], "nt", F32, tmm, UP_SHARD, f"ffn_dact_{tag}")
    else:
        names, fulls, cuts = reduce
        dact, *recv = _mm(dout, W["w_down"][layer], "nt", F32, tmm, UP_SHARD, f"ffn_dact_{tag}",
                          host=(fulls, cuts, "pairx"))
        parts = [_pair_sum2(f, r, cut, PAIR_SUM_BLOCKS[nm], "grad_pair_sum_" + nm)
                 for f, r, cut, nm in zip(fulls, recv, cuts, names)]
        comm = (parts, cuts)
    dwd = _mm(act, dout, "tn", F32, D_FF // 2, 512, f"ffn_dwdown_{tag}")
    dup, dcw, dcb, *others = _ffn_act_bwd(up, dact, W["ffn_conv_w_p"][layer], W["ffn_conv_b"][layer:layer + 1],
                                          tm, f"ffn_act_bwd_{tag}", comm=comm)
    dn = _mm_ffn_dn(dup, W["w_up"][layer], tm, D_MODEL, f"ffn_dn_{tag}")
    dwu = _mm_ffn_dwup(n, dup, 512, D_FF // 2, f"ffn_dwup_{tag}")
    dh, dgain = _rms_bwd(h, W["ffn_norm"][layer:layer + 1], dn, dout, tm, f"ffn_norm_bwd_{tag}")
    return dh, (dwu, dwd), dict(gain=dgain, cw=dcw[:FFN_K], cb=dcb), parts, others


GATHER_FIRST = ("w_in", "small")
GATHER_LATE = ("pool_w", "w_up", "w_down")
HOSTED_FFN = ("w_up1", "w_down1")
HOSTED = ("w_out", "pool_w", "w_up0", "w_down0")
LATE = ("small",)


def _local_step2(h0, tgt, W, n_real, cut_of):
    LP = h0.shape[0]
    tm, tmm = _row_tiles(LP)
    nb = LP // ATT_BLK
    G = {}
    n0 = _rms_fwd(h0, W["mix_norm_even"], BF16, tm, "mix_norm_even")
    sh = W["late_shards"]
    stage = lambda *names: ([sh[n] for n in names], [cut_of[n] for n in names])
    proj, g_down0 = _mm(n0, W["w_in_p"], "nn", F32, tmm, 896, "in_proj", host=stage("w_down0"))
    c = _fgate_fwd(proj, W["b_f_p"], "forget_gate")
    qT, kT, k_aug, vT = _attn_prep(proj, c, "attention_operands")
    oT, lse, g_pool, g_up0, g_out = _attn_fwd2(qT, k_aug, vT, "fox_attention",
                                               comm=stage("pool_w", "w_up0", "w_out"))
    g_down0, g_pool, g_up0, g_out = _gather_forward(
        [g_down0, g_pool, g_up0, g_out], stage("w_down0", "pool_w", "w_up0", "w_out")[1], "gather_forward_0")
    W = dict(W)
    W.update(pool_w=g_pool, w_up=[g_up0, None], w_down=[g_down0, None], w_out=g_out)
    u1, u = _conf_fwd(proj, W["conv_w_p"], W["conv_b"], W["ln_g"], W["ln_b"], tm, "conformer")
    cat = jnp.concatenate([_attn_rows(oT, 1.0, BF16, "attention_rows"), u], axis=1)
    h1 = _mm(cat, W["w_out"], "nn", F32, tmm, D_MODEL, "out_proj", add=h0)
    h2, ffn0, (g_down1, g_up1) = _ffn_fwd2(h1, W, 0, tm, tmm, host_up=stage("w_down1"), host_act=stage("w_up1"))
    g_down1, g_up1 = _gather_forward([g_down1, g_up1], stage("w_down1", "w_up1")[1], "gather_forward_1")
    W.update(w_up=[g_up0, g_up1], w_down=[g_down0, g_down1])
    h3, dpool = _pool_fwd(h2, W["mix_norm_odd"], W["pool_w"], W["pool_b"], W["pool_scale"], tm, "pool_mixer")
    h4, ffn1, _ = _ffn_fwd2(h3, W, 1, tm, tmm)
    loss, dh4, G["final_norm"] = _loss_head(h4, W["final_norm"], tgt, n_real, tm, "loss_head")

    dh3, (G["w_up1"], G["w_down1"]), g1, _, _ = _ffn_bwd2(h3, W, 1, ffn1, dh4, tm, tmm)
    dh2, G["pool_w"], G["pool_b"], G["pool_scale"], G["mix_norm_odd"] = _pool_bwd(
        h2, W["mix_norm_odd"], dpool, W["pool_w"], W["pool_b"], W["pool_scale"], dh3, tm, "pool_mixer_bwd")
    cuts1 = [cut_of[n] for n in HOSTED_FFN]
    dh1, (G["w_up0"], G["w_down0"]), g0, parts1, others1 = _ffn_bwd2(
        h1, W, 0, ffn0, dh2, tm, tmm, reduce=(HOSTED_FFN, [G[n] for n in HOSTED_FFN], cuts1))
    G["ffn_norm"] = jnp.concatenate([g0["gain"], g1["gain"]], axis=0)
    G["ffn_conv_w"] = jnp.stack([g0["cw"], g1["cw"]])
    G["ffn_conv_b"] = jnp.concatenate([g0["cb"], g1["cb"]], axis=0)

    dcat = _mm(dh1, W["w_out"], "nt", F32, tmm, D_MODEL, "out_proj_dx")
    G["w_out"] = _mm(cat, dh1, "tn", F32, 512, D_MODEL, "out_proj_dw")
    hcuts = [cut_of[n] for n in HOSTED]
    hfull = [G[n] for n in HOSTED]
    dadg, dcw, G["conv_b"], G["ln_g"], G["ln_b"], *hrecv = _conf_bwd(
        proj, u1, dcat, W["conv_w_p"], W["ln_g"], W["ln_b"], tm, "conformer_bwd", host=(hfull, hcuts, "pairx"))
    G["conv_w"] = dcw[:CONV_K]
    doT = _attn_cols(dcat, "attention_do_cols")
    hparts = [_pair_sum2(f, r, cut, PAIR_SUM_BLOCKS[n], "grad_pair_sum_" + n)
              for f, r, cut, n in zip(hfull, hrecv, hcuts, HOSTED)]
    dqT, dkT, dvT, *hothers = _attn_bwd2(qT, kT, k_aug, vT, oT, doT, lse, "fox_attention_bwd",
                                         comm=(hparts, hcuts))
    dfl, dbf = _fgate_bwd(proj, W["b_f_p"], _attn_dc(dqT, dkT, "attention_dc"), "forget_gate_bwd")
    G["b_f"] = dbf[:, :HEADS]
    dproj = jnp.concatenate([_attn_rows(dqT, HEAD_DIM ** -0.5, BF16, "attention_dq_rows"),
                             _attn_rows(dkT, 1.0, BF16, "attention_dk_rows"),
                             _attn_rows(dvT, 1.0, BF16, "attention_dv_rows"), dadg, dfl], axis=1)
    gp = _mm(n0, dproj, "tn", F32, 512, 896, "in_proj_dw")
    g_w_in = jnp.concatenate([gp[:, :3 * FOX_W], gp[:, 3 * FOX_W + 2 * CONV_CH:3 * FOX_W + 2 * CONV_CH + HEADS],
                              gp[:, 3 * FOX_W:3 * FOX_W + 2 * CONV_CH]], axis=1)
    g_w_in = g_w_in.reshape(D_MODEL, N_CHIPS, IN_SHARD).transpose(1, 0, 2)
    icut = [cut_of["w_in"]]
    dn0, irecv = _mm(dproj, W["w_in_p"], "nt", F32, tmm, D_MODEL, "in_proj_dx", host=([g_w_in], icut, "pairx"))
    ipart = _pair_sum2(g_w_in, irecv, icut[0], PAIR_SUM_BLOCKS["w_in"], "grad_pair_sum_w_in")
    dh0, G["mix_norm_even"], iother = _rms_bwd(h0, W["mix_norm_even"], dn0, dh1, tm, "mix_norm_even_bwd",
                                               host=([ipart], icut, "chipx"))
    parts = dict(zip(HOSTED_FFN + HOSTED + ("w_in",), parts1 + hparts + [ipart]))
    others = dict(zip(HOSTED_FFN + HOSTED + ("w_in",), list(others1) + list(hothers) + [iother]))
    return loss, dh0, G, parts, others


class _Cut:
    def __init__(self, full_shape, chip_dim, half_dim):
        self.full = tuple(full_shape)
        self.chip_dim, self.half_dim = chip_dim, half_dim
        self.chip_size = full_shape[chip_dim] // N_CHIPS
        self.half_size = full_shape[half_dim] // 2
        assert chip_dim != half_dim

    def shape(self, chip=False, half=False):
        s = list(self.full)
        if chip:
            s[self.chip_dim] = self.chip_size
        if half:
            s[self.half_dim] = self.half_size
        return tuple(s)

    def region(self, ref, chip=None, half=None):
        idx = [pl.ds(0, n) for n in ref.shape]
        if chip is not None:
            idx[self.chip_dim] = pl.ds(chip * self.chip_size, self.chip_size)
        if half is not None:
            idx[self.half_dim] = pl.ds(half * self.half_size, self.half_size)
        return ref.at[tuple(idx)]


SMALL_SHARDED = ("meta_tokens", "mix_norm_odd", "pool_b", "pool_scale", "conv_w", "ffn_conv_w")
SMALL_ROWS = 144


def _cuts():
    return {
        "w_in": _Cut((N_CHIPS, D_MODEL, IN_SHARD), 0, 1),
        "w_out": _Cut((D_MODEL, D_MODEL), 0, 1),
        "pool_w": _Cut((len(POOL_WINDOWS), POOL_G, POOL_G), 1, 0),
        "w_up": _Cut((2, D_MODEL, 2 * D_FF), 2, 1),
        "w_down": _Cut((2, D_FF, D_MODEL), 1, 2),
        "small": _Cut((N_CHIPS, SMALL_ROWS, LANE), 0, 1),
        "w_up0": _Cut((D_MODEL, 2 * D_FF), 1, 0), "w_up1": _Cut((D_MODEL, 2 * D_FF), 1, 0),
        "w_down0": _Cut((D_FF, D_MODEL), 0, 1), "w_down1": _Cut((D_FF, D_MODEL), 0, 1),
    }


COMM_ORDER = ("w_in", "w_out", "pool_w", "w_up", "w_down", "small")


def _remote(src, dst, send_sems, recv_sems, k, to):
    return pltpu.make_async_remote_copy(src_ref=src, dst_ref=dst, send_sem=send_sems.at[k],
                                        recv_sem=recv_sems.at[k], device_id=to, device_id_type=MESH)


def _gather_weights(shards, cuts):
    n = len(shards)

    def body(*refs):
        srcs, outs = refs[:n], refs[n:2 * n]
        send_sems, recv_sems = refs[2 * n:]
        x, y, c = _coords()
        me = 2 * x + y
        sibling = (x, y, 1 - c)
        chips = _other_chips(x, y)
        sends = []
        for t, cut in enumerate(cuts):
            push = _remote(srcs[t], cut.region(outs[t], chip=me), send_sems, recv_sems, 7 * t, sibling)
            push.start()
            sends.append(push)
            for kk, chip in enumerate(chips):
                cp = _remote(cut.region(srcs[t], half=c), cut.region(outs[t], chip=me, half=c),
                             send_sems, recv_sems, 7 * t + 1 + kk, (*chip, c))
                cp.start()
                sends.append(cp)
        for t, cut in enumerate(cuts):
            for kk, (px, py) in enumerate(chips):
                landed = cut.region(outs[t], chip=2 * px + py, half=c)
                _remote(landed, landed, send_sems, recv_sems, 7 * t + 1 + kk, sibling).wait_recv()
                fwd = _remote(landed, landed, send_sems, recv_sems, 7 * t + 4 + kk, sibling)
                fwd.start()
                sends.append(fwd)
        for t, cut in enumerate(cuts):
            mine = cut.region(outs[t], chip=me)
            _remote(mine, mine, send_sems, recv_sems, 7 * t, sibling).wait_recv()
            for kk, (px, py) in enumerate(chips):
                other = cut.region(outs[t], chip=2 * px + py, half=1 - c)
                _remote(other, other, send_sems, recv_sems, 7 * t + 4 + kk, sibling).wait_recv()
        for cp in sends:
            cp.wait_send()

    return pl.pallas_call(
        body, name="gather_weights",
        out_shape=tuple(jax.ShapeDtypeStruct(cut.full, s.dtype) for cut, s in zip(cuts, shards)),
        in_specs=[ANY] * n, out_specs=tuple([ANY] * n),
        scratch_shapes=[pltpu.SemaphoreType.DMA((7 * n,)), pltpu.SemaphoreType.DMA((7 * n,))],
    )(*shards)


def _gather_first_ops(srcs, outs, send_sems, recv_sems, cuts):
    x, y, c = _coords()
    me = 2 * x + y
    sibling = (x, y, 1 - c)
    chips = _other_chips(x, y)

    def copies():
        out = []
        for t, cut in enumerate(cuts):
            out.append(_remote(srcs[t], cut.region(outs[t], chip=me), send_sems, recv_sems, 4 * t, sibling))
            for kk, chip in enumerate(chips):
                out.append(_remote(cut.region(srcs[t], half=c), cut.region(outs[t], chip=me, half=c),
                                   send_sems, recv_sems, 4 * t + 1 + kk, (*chip, c)))
        return out

    def start():
        for cp in copies():
            cp.start()

    def wait():
        for t, cut in enumerate(cuts):
            mine = cut.region(outs[t], chip=me)
            _remote(mine, mine, send_sems, recv_sems, 4 * t, sibling).wait_recv()
            for kk, (px, py) in enumerate(chips):
                landed = cut.region(outs[t], chip=2 * px + py, half=c)
                _remote(landed, landed, send_sems, recv_sems, 4 * t + 1 + kk, sibling).wait_recv()
        for cp in copies():
            cp.wait_send()

    return start, wait


def _pair_exchange_ops(srcs, outs, send_sems, recv_sems, cuts):
    x, y, c = _coords()

    def copies():
        return [_remote(cut.region(srcs[t], half=1 - c), outs[t], send_sems, recv_sems, t, (x, y, 1 - c))
                for t, cut in enumerate(cuts)]

    def start():
        for cp in copies():
            cp.start()

    def wait():
        for cp in copies():
            cp.wait()

    return start, wait


def _host_plan(host):
    arrays, cuts = host[0], host[1]
    if len(host) > 2 and host[2] == "chipx":
        return _chip_exchange_shapes(arrays, cuts), 3 * len(arrays), _chip_exchange_ops
    if len(host) > 2 and host[2] == "pairx":
        return (tuple(jax.ShapeDtypeStruct(cut.shape(half=True), a.dtype) for cut, a in zip(cuts, arrays)),
                len(arrays), _pair_exchange_ops)
    return (tuple(jax.ShapeDtypeStruct(cut.full, a.dtype) for cut, a in zip(cuts, arrays)),
            4 * len(arrays), _gather_first_ops)


def _gather_forward(fulls, cuts, name):
    n = len(fulls)

    def body(*refs):
        outs = refs[n:2 * n]
        send_sems, recv_sems = refs[2 * n:]
        x, y, c = _coords()
        sibling = (x, y, 1 - c)
        chips = _other_chips(x, y)
        sends = []
        for t, cut in enumerate(cuts):
            for kk, (px, py) in enumerate(chips):
                landed = cut.region(outs[t], chip=2 * px + py, half=c)
                cp = _remote(landed, landed, send_sems, recv_sems, 3 * t + kk, sibling)
                cp.start()
                sends.append(cp)
        for t, cut in enumerate(cuts):
            for kk, (px, py) in enumerate(chips):
                other = cut.region(outs[t], chip=2 * px + py, half=1 - c)
                _remote(other, other, send_sems, recv_sems, 3 * t + kk, sibling).wait_recv()
        for cp in sends:
            cp.wait_send()

    return pl.pallas_call(
        body, name=name,
        out_shape=tuple(jax.ShapeDtypeStruct(f.shape, f.dtype) for f in fulls),
        in_specs=[ANY] * n, out_specs=tuple([ANY] * n), input_output_aliases={t: t for t in range(n)},
        scratch_shapes=[pltpu.SemaphoreType.DMA((3 * n,)), pltpu.SemaphoreType.DMA((3 * n,))],
    )(*fulls)


def _pair_exchange2(fulls, cuts, name):
    n = len(fulls)

    def body(*refs):
        srcs, outs = refs[:n], refs[n:2 * n]
        send_sems, recv_sems = refs[2 * n:]
        x, y, c = _coords()
        cps = [_remote(cut.region(srcs[t], half=1 - c), outs[t], send_sems, recv_sems, t, (x, y, 1 - c))
               for t, cut in enumerate(cuts)]
        for cp in cps:
            cp.start()
        for cp in cps:
            cp.wait()

    return pl.pallas_call(
        body, name=name,
        out_shape=tuple(jax.ShapeDtypeStruct(cut.shape(half=True), f.dtype) for cut, f in zip(cuts, fulls)),
        in_specs=[ANY] * n, out_specs=tuple([ANY] * n),
        scratch_shapes=[pltpu.SemaphoreType.DMA((n,)), pltpu.SemaphoreType.DMA((n,))],
    )(*fulls)


def _grid_of(shape, blk):
    assert all(s % b == 0 for s, b in zip(shape, blk)), (shape, blk)
    return tuple(s // b for s, b in zip(shape, blk))


def _pair_sum2(full, recv, cut, blk, name):
    hshape = cut.shape(half=True)
    grid = _grid_of(hshape, blk)
    hb = cut.half_size // blk[cut.half_dim]
    hd = cut.half_dim
    pos = jnp.stack([lax.axis_index("c")]).astype(jnp.int32)

    def full_idx(*a):
        ids, p = list(a[:-1]), a[-1]
        ids[hd] = ids[hd] + p[0] * hb
        return tuple(ids)

    def body(p_ref, f_ref, r_ref, o_ref):
        o_ref[...] = (f_ref[...] + r_ref[...]).astype(o_ref.dtype)

    return pl.pallas_call(
        body, name=name, out_shape=jax.ShapeDtypeStruct(hshape, BF16),
        grid_spec=pltpu.PrefetchScalarGridSpec(
            num_scalar_prefetch=1, grid=grid,
            in_specs=[pl.BlockSpec(blk, full_idx), pl.BlockSpec(blk, lambda *a: tuple(a[:-1]))],
            out_specs=pl.BlockSpec(blk, lambda *a: tuple(a[:-1]))),
        compiler_params=_params(("parallel",) * len(grid)))(pos, full, recv)


def _chip_exchange_ops(srcs, outs, send_sems, recv_sems, cuts):
    x, y, c = _coords()
    me = 2 * x + y
    chips = _other_chips(x, y)

    def copies():
        return [_remote(cut.region(srcs[t], chip=2 * px + py), outs[t].at[me], send_sems, recv_sems,
                        3 * t + kk, (px, py, c))
                for t, cut in enumerate(cuts) for kk, (px, py) in enumerate(chips)]

    def start():
        for cp in copies():
            cp.start()

    def wait():
        for t, cut in enumerate(cuts):
            for kk, (px, py) in enumerate(chips):
                slot = outs[t].at[2 * px + py]
                _remote(slot, slot, send_sems, recv_sems, 3 * t + kk, (px, py, c)).wait_recv()
        for cp in copies():
            cp.wait_send()

    return start, wait


def _chip_exchange_shapes(parts, cuts):
    return tuple(jax.ShapeDtypeStruct((N_CHIPS,) + cut.shape(chip=True, half=True), p.dtype)
                 for cut, p in zip(cuts, parts))


def _chip_exchange2(parts, cuts):
    n = len(parts)

    def body(*refs):
        start, wait = _chip_exchange_ops(refs[:n], refs[n:2 * n], refs[2 * n], refs[2 * n + 1], cuts)
        start()
        wait()

    return pl.pallas_call(
        body, name="grad_chip_exchange",
        out_shape=tuple(jax.ShapeDtypeStruct((N_CHIPS,) + cut.shape(chip=True, half=True), p.dtype)
                        for cut, p in zip(cuts, parts)),
        in_specs=[ANY] * n, out_specs=tuple([ANY] * n),
        scratch_shapes=[pltpu.SemaphoreType.DMA((3 * n,)), pltpu.SemaphoreType.DMA((3 * n,))],
    )(*parts)


def _chip_sum2(part, recv, cut, blk, name, stacked=None):
    bshape = cut.shape(chip=True, half=True)
    grid = _grid_of(bshape, blk)
    cb = cut.chip_size // blk[cut.chip_dim]
    hb = cut.half_size // blk[cut.half_dim]
    cd, hd = cut.chip_dim, cut.half_dim
    x, y, c = _coords()
    slots = [2 * px + py for px, py in _other_chips(x, y)]
    pos = jnp.stack([c, 2 * x + y] + slots).astype(jnp.int32)

    def part_idx(*a):
        ids, p = list(a[:-1]), a[-1]
        ids[cd] = ids[cd] + p[1] * cb
        return tuple(ids)

    def recv_idx(kk):
        return lambda *a: (a[-1][2 + kk],) + tuple(a[:-1])

    def out_idx(*a):
        ids, p = list(a[:-1]), a[-1]
        ids[hd] = ids[hd] + p[0] * hb
        return tuple(ids)

    def body(p_ref, own_ref, r0_ref, r1_ref, r2_ref, *rest):
        acc = own_ref[...].astype(F32)
        for r_ref in (r0_ref, r1_ref, r2_ref):
            acc = acc + r_ref[...].astype(F32)
        rest[-1][...] = acc

    in_specs = [pl.BlockSpec(blk, part_idx)] + [pl.BlockSpec((None,) + blk, recv_idx(kk)) for kk in range(3)]
    args = [pos, part, recv, recv, recv]
    aliases = {}
    if stacked is None:
        out_shape = jax.ShapeDtypeStruct(cut.shape(chip=True), F32)
        out_spec = pl.BlockSpec(blk, out_idx)
    else:
        lead, n_lead, into = stacked
        out_shape = jax.ShapeDtypeStruct((n_lead,) + cut.shape(chip=True), F32)
        out_spec = pl.BlockSpec((None,) + blk, lambda *a: (lead,) + out_idx(*a))
        if into is not None:
            in_specs.append(pl.BlockSpec(memory_space=pl.ANY))
            args.append(into)
            aliases = {5: 0}
    return pl.pallas_call(
        body, name=name, out_shape=out_shape,
        grid_spec=pltpu.PrefetchScalarGridSpec(num_scalar_prefetch=1, grid=grid, in_specs=in_specs,
                                               out_specs=out_spec),
        input_output_aliases=aliases, compiler_params=_params(("parallel",) * len(grid)))(*args)


def _pair_swap2(blocks, cuts):
    n = len(blocks)

    def body(*refs):
        outs = refs[n:2 * n]
        send_sems, recv_sems = refs[2 * n:]
        x, y, c = _coords()
        cps = []
        for t, cut in enumerate(cuts):
            mine = cut.region(outs[t], half=c)
            cp = _remote(mine, mine, send_sems, recv_sems, t, (x, y, 1 - c))
            cp.start()
            cps.append(cp)
        for t, cut in enumerate(cuts):
            theirs = cut.region(outs[t], half=1 - c)
            _remote(theirs, theirs, send_sems, recv_sems, t, (x, y, 1 - c)).wait_recv()
        for cp in cps:
            cp.wait_send()

    return pl.pallas_call(
        body, name="grad_pair_swap",
        out_shape=tuple(jax.ShapeDtypeStruct(b.shape, b.dtype) for b in blocks),
        in_specs=[ANY] * n, out_specs=tuple([ANY] * n),
        input_output_aliases={t: t for t in range(n)},
        scratch_shapes=[pltpu.SemaphoreType.DMA((n,)), pltpu.SemaphoreType.DMA((n,))],
    )(*blocks)


PAIR_SUM_BLOCKS = {"w_in": (1, 512, IN_SHARD), "w_out": (512, 512), "pool_w": (1, POOL_G, POOL_G),
                   "w_up0": (64, 2 * D_FF), "w_up1": (64, 2 * D_FF), "w_down0": (704, 512), "w_down1": (704, 512),
                   "small": (N_CHIPS, SMALL_ROWS // 2, LANE)}
CHIP_SUM_BLOCKS = {"w_in": (1, 512, IN_SHARD), "w_out": (256, 512), "pool_w": (2, 64, POOL_G),
                   "w_up0": (128, UP_SHARD), "w_up1": (128, UP_SHARD),
                   "w_down0": (DOWN_SHARD, 512), "w_down1": (DOWN_SHARD, 512),
                   "small": (1, SMALL_ROWS // 2, LANE)}


def _pack_small(P):
    parts = []
    for name in SMALL_SHARDED:
        t = P[name]
        parts.append(t.astype(F32))
    return parts


def _small_rows(t, lead):
    flat = t.reshape(lead + (-1,))
    pad = -flat.shape[-1] % LANE
    return jnp.pad(flat, [(0, 0)] * len(lead) + [(0, pad)]).reshape(lead + (-1, LANE))


def _pack_small_shards(shards):
    rows = jnp.concatenate([_small_rows(shards[n].astype(F32), ()) for n in SMALL_SHARDED], axis=0)
    return jnp.pad(rows, ((0, SMALL_ROWS - rows.shape[0]), (0, 0)))[None]


def _unpack_small(pack, shards, axes):
    out, off = {}, 0
    nchip = pack.shape[0]
    for name in SMALL_SHARDED:
        shp = shards[name].shape
        cnt = int(np.prod(shp))
        rows = -(-cnt // LANE)
        t = pack[:, off:off + rows].reshape(nchip, -1)[:, :cnt].reshape((nchip,) + shp)
        out[name] = jnp.concatenate([t[j] for j in range(nchip)], axis=axes[name])
        off += rows
    return out


def _pack_small_grads(grads, shards, axes):
    parts = []
    for name in SMALL_SHARDED:
        shp, ax = shards[name].shape, axes[name]
        g = grads[name].reshape(shp[:ax] + (N_CHIPS, shp[ax]) + shp[ax + 1:])
        parts.append(_small_rows(jnp.moveaxis(g, ax, 0), (N_CHIPS,)))
    rows = jnp.concatenate(parts, axis=1)
    return jnp.pad(rows, ((0, 0), (0, SMALL_ROWS - rows.shape[1]), (0, 0)))


SMALL_AXES = {"meta_tokens": 1, "mix_norm_odd": 1, "pool_b": 2, "pool_scale": 1, "conv_w": 2, "ffn_conv_w": 2}


WEIGHT_NAMES = ("meta_tokens", "mix_norm_even", "w_in", "b_f", "conv_w", "conv_b", "ln_g", "ln_b", "w_out",
                "mix_norm_odd", "pool_w", "pool_b", "pool_scale", "ffn_norm", "w_up", "ffn_conv_w",
                "ffn_conv_b", "w_down", "final_norm")


def kernel(x, meta_tokens, mix_norm_even, w_in, b_f, conv_w, conv_b, ln_g, ln_b, w_out, mix_norm_odd, pool_w, pool_b, pool_scale, ffn_norm, w_up, ffn_conv_w, ffn_conv_b, w_down, final_norm, loss_target, m_meta_tokens, m_mix_norm_even, m_w_in, m_b_f, m_conv_w, m_conv_b, m_ln_g, m_ln_b, m_w_out, m_mix_norm_odd, m_pool_w, m_pool_b, m_pool_scale, m_ffn_norm, m_w_up, m_ffn_conv_w, m_ffn_conv_b, m_w_down, m_final_norm, v_meta_tokens, v_mix_norm_even, v_w_in, v_b_f, v_conv_w, v_conv_b, v_ln_g, v_ln_b, v_w_out, v_mix_norm_odd, v_pool_w, v_pool_b, v_pool_scale, v_ffn_norm, v_w_up, v_ffn_conv_w, v_ffn_conv_b, v_w_down, v_final_norm):
    given = dict(locals())
    w_loc = {n: given[n] for n in WEIGHT_NAMES}
    m_loc = {n: given["m_" + n] for n in WEIGHT_NAMES}
    v_loc = {n: given["v_" + n] for n in WEIGHT_NAMES}
    cut_of = _cuts()
    cuts = [cut_of[n] for n in COMM_ORDER]
    big = ("w_in", "w_out", "pool_w", "w_up", "w_down")
    small_shards = {n: w_loc[n] for n in SMALL_SHARDED}

    shard_of = {n: w_loc[n].astype(BF16).reshape(cut_of[n].shape(chip=True)) for n in big}
    shard_of["small"] = _pack_small_shards(small_shards)
    g_in, g_small = _gather_weights([shard_of[n] for n in GATHER_FIRST], [cut_of[n] for n in GATHER_FIRST])
    g_out = None
    g_pool = g_up = g_down = None
    full = _unpack_small(g_small, small_shards, SMALL_AXES)
    full.update({n: w_loc[n] for n in REPLICATED})
    w_in_full = g_in.transpose(1, 0, 2).reshape(D_MODEL, IN_COLS)
    qkv, f, ag = (w_in_full[:, :3 * FOX_W], w_in_full[:, 3 * FOX_W:3 * FOX_W + HEADS],
                  w_in_full[:, 3 * FOX_W + HEADS:])
    W = dict(
        mix_norm_even=full["mix_norm_even"].reshape(1, D_MODEL),
        w_in_p=jnp.concatenate([qkv, ag, f, jnp.zeros((D_MODEL, LANE - HEADS), BF16)], axis=1),
        b_f_p=jnp.pad(full["b_f"].reshape(1, HEADS), ((0, 0), (0, LANE - HEADS))),
        conv_w_p=jnp.pad(full["conv_w"].reshape(CONV_K, CONV_CH), ((0, CONV_HALO - CONV_K), (0, 0))),
        conv_b=full["conv_b"].reshape(1, CONV_CH), ln_g=full["ln_g"].reshape(1, CONV_CH),
        ln_b=full["ln_b"].reshape(1, CONV_CH), w_out=g_out,
        mix_norm_odd=full["mix_norm_odd"].reshape(1, D_MODEL), pool_w=g_pool,
        pool_b=full["pool_b"].reshape(1, D_MODEL), pool_scale=full["pool_scale"].reshape(1, D_MODEL),
        ffn_norm=full["ffn_norm"], w_up=g_up,
        ffn_conv_w_p=jnp.pad(full["ffn_conv_w"], ((0, 0), (0, 8 - FFN_K), (0, 0))),
        ffn_conv_b=full["ffn_conv_b"], w_down=g_down, final_norm=full["final_norm"].reshape(1, D_MODEL),
        late_shards=dict(pool_w=shard_of["pool_w"], w_out=shard_of["w_out"],
                         w_up0=shard_of["w_up"][0], w_up1=shard_of["w_up"][1],
                         w_down0=shard_of["w_down"][0], w_down1=shard_of["w_down"][1]))

    seq = x.shape[1]
    n_real = N_META + seq
    LP = -(-n_real // ATT_BLK) * ATT_BLK
    tail = jnp.zeros((LP - n_real, D_MODEL), F32)
    h0 = jnp.concatenate([full["meta_tokens"], x[0], tail], axis=0)
    tgt = jnp.concatenate([jnp.zeros((N_META, D_MODEL), F32), loss_target[0], tail], axis=0)
    loss_loc, dh0, G, parts, others = _local_step2(h0, tgt, W, n_real, cut_of)
    grad_x = dh0[N_META:n_real][None]
    G["meta_tokens"] = dh0[:N_META]

    rep_shapes = {n: w_loc[n].shape for n in REPLICATED}
    G["final_norm"] = G["final_norm"].reshape(D_MODEL)
    rep, loss = _unpack_replicated(_allreduce_small(_pack_replicated(G, loss_loc)), rep_shapes)

    lcuts = [cut_of[n] for n in LATE]
    lfull = [_pack_small_grads(G, small_shards, SMALL_AXES)]
    lrecv = _pair_exchange2(lfull, lcuts, "grad_pair_exchange_late")
    lparts = [_pair_sum2(f, r, cut, PAIR_SUM_BLOCKS[n], "grad_pair_sum_" + n)
              for f, r, cut, n in zip(lfull, lrecv, lcuts, LATE)]
    parts.update(zip(LATE, lparts))
    others.update(zip(LATE, _chip_exchange2(lparts, lcuts)))
    def chip_sum(n, stacked=None):
        return _chip_sum2(parts[n], others[n], cut_of[n], CHIP_SUM_BLOCKS[n], "grad_chip_sum_" + n, stacked=stacked)

    blocks = []
    for n in COMM_ORDER:
        if n in ("w_up", "w_down"):
            blocks.append(chip_sum(n + "1", stacked=(1, 2, chip_sum(n + "0", stacked=(0, 2, None)))))
        else:
            blocks.append(chip_sum(n))
    blocks = _pair_swap2(blocks, cuts)
    gsh = {n: b.reshape(w_loc[n].shape) for n, b in zip(big, blocks[:5])}
    gsh.update(_unpack_small(blocks[5], small_shards, SMALL_AXES))
    sharded = set(big) | set(SMALL_SHARDED)

    grad_w = {n: (gsh[n] if n in sharded else rep[n]) for n in WEIGHT_NAMES}
    delta, new_m, new_v = {}, {}, {}
    for n in WEIGHT_NAMES:
        delta[n], new_m[n], new_v[n] = _adamw(w_loc[n], grad_w[n], m_loc[n], v_loc[n], "adamw_" + n)
    return (loss, grad_x, *[grad_w[n] for n in WEIGHT_NAMES], *[delta[n] for n in WEIGHT_NAMES],
            *[new_m[n] for n in WEIGHT_NAMES], *[new_v[n] for n in WEIGHT_NAMES])
```

```python
import functools

import numpy as np
import jax
import jax.numpy as jnp
from jax import lax
from jax.experimental import pallas as pl
from jax.experimental.pallas import tpu as pltpu

F32 = jnp.float32
BF16 = jnp.bfloat16

D_MODEL = 1024
N_META = 16
SEQ = 2048
HEADS = 8
HEAD_DIM = 64
FOX_W = HEADS * HEAD_DIM
CONV_CH = 512
CONV_K = 31
D_FF = 2816
POOL_WINDOWS = (2, 4, 8, 16)
POOL_G = 256
RMS_EPS = 1e-6
LN_EPS = 1e-5
IN_COLS = 3 * FOX_W + HEADS + 2 * CONV_CH
IN_COLS_P = 3 * FOX_W + 2 * CONV_CH + 128
F_COL_BLK = (3 * FOX_W + 2 * CONV_CH) // 128
N_CHIPS = 4
IN_SHARD = IN_COLS // N_CHIPS
UP_SHARD = 2 * D_FF // N_CHIPS
DOWN_SHARD = D_FF // N_CHIPS

ADAM_LR = 0.001
ADAM_B1 = 0.9
ADAM_B2 = 0.999
ADAM_EPS = 1e-08
ADAM_WD = 0.01
ADAM_STEP = 10

LANE = 128
ATT_BLK = 128
VMEM_LIMIT = 56 * 1024 * 1024

NEG = -1e30


def _sigmoid(x):
    return 0.5 * jnp.tanh(0.5 * x) + 0.5


def _sigmoid_tail(x):
    return 1.0 / (1.0 + jnp.exp(-x))


def _params(sem=None):
    return pltpu.CompilerParams(dimension_semantics=sem, vmem_limit_bytes=VMEM_LIMIT)


def _sub_rows(tm):
    best = 8
    for s in range(8, 137, 8):
        if tm % s == 0:
            best = s
    return best


def _mm(a, b, mode, out_dtype, tm, tn, name, add=None, a_lead=None, b_lead=None, out=None, host=None):
    a_shape = a.shape if a_lead is None else a.shape[1:]
    b_shape = b.shape if b_lead is None else b.shape[1:]
    if mode == "nn":
        (M, K), (K2, N) = a_shape, b_shape
        dims = (((1,), (0,)), ((), ()))
        a_blk, a_idx = (tm, K), (lambda i, j: (i, 0))
        b_blk, b_idx = (K, tn), (lambda i, j: (0, j))
    elif mode == "nt":
        (M, K), (N, K2) = a_shape, b_shape
        dims = (((1,), (1,)), ((), ()))
        a_blk, a_idx = (tm, K), (lambda i, j: (i, 0))
        b_blk, b_idx = (tn, K), (lambda i, j: (j, 0))
    else:
        (K, M), (K2, N) = a_shape, b_shape
        dims = (((0,), (0,)), ((), ()))
        a_blk, a_idx = (K, tm), (lambda i, j: (0, i))
        b_blk, b_idx = (K, tn), (lambda i, j: (0, j))
    assert K == K2 and M % tm == 0 and N % tn == 0, (name, a.shape, b.shape, tm, tn)
    gm, gn = M // tm, N // tn
    a_bytes = M * K * a.dtype.itemsize
    b_bytes = N * K * b.dtype.itemsize
    m_outer = a_bytes + b_bytes * gm <= b_bytes + a_bytes * gn
    if m_outer:
        grid = (gm, gn)
        wrap = lambda f: f
    else:
        grid = (gn, gm)
        wrap = lambda f: (lambda j, i: f(i, j))

    def lead(blk, idx, at):
        if at is None:
            return pl.BlockSpec(blk, wrap(idx))
        return pl.BlockSpec((None,) + blk, wrap(lambda i, j: (at,) + idx(i, j)))

    o_idx = lambda i, j: (i, j)
    in_specs = [lead(a_blk, a_idx, a_lead), lead(b_blk, b_idx, b_lead)]
    args = [a, b]
    if add is not None:
        in_specs.append(pl.BlockSpec((tm, tn), wrap(o_idx)))
        args.append(add)
    aliases = {}
    if out is None:
        out_shape = jax.ShapeDtypeStruct((M, N), out_dtype)
        out_spec = pl.BlockSpec((tm, tn), wrap(o_idx))
    else:
        o_lead, n_lead, into = out
        out_shape = jax.ShapeDtypeStruct((n_lead, M, N), out_dtype)
        out_spec = lead((tm, tn), o_idx, o_lead)
        if into is not None:
            aliases = {len(args): 0}
            in_specs.append(pl.BlockSpec(memory_space=pl.ANY))
            args.append(into)
    has_add = add is not None
    n_host = 0 if host is None else len(host[0])
    n_in = len(args)
    scratch = []
    if n_host:
        host_shapes, host_sems, host_ops = _host_plan(host)
        in_specs = in_specs + [pl.BlockSpec(memory_space=pl.ANY)] * n_host
        args = args + list(host[0])
        out_shape = (out_shape,) + host_shapes
        out_spec = (out_spec,) + (pl.BlockSpec(memory_space=pl.ANY),) * n_host
        scratch = [pltpu.SemaphoreType.DMA((host_sems,)), pltpu.SemaphoreType.DMA((host_sems,))]

    def body(*refs):
        a_ref, b_ref = refs[0], refs[1]
        o_ref = refs[n_in + n_host]
        if n_host:
            start, wait = host_ops(refs[n_in:n_in + n_host], refs[n_in + n_host + 1:n_in + 2 * n_host + 1],
                                   refs[n_in + 2 * n_host + 1], refs[n_in + 2 * n_host + 2], host[1])
            pl.when(jnp.logical_and(pl.program_id(0) == 0, pl.program_id(1) == 0))(start)
        x = a_ref[...].astype(BF16)
        y = b_ref[...].astype(BF16)
        acc = lax.dot_general(x, y, dims, preferred_element_type=F32)
        if has_add:
            acc = acc + refs[2][...]
        o_ref[...] = acc.astype(o_ref.dtype)
        if n_host:
            pl.when(jnp.logical_and(pl.program_id(0) == grid[0] - 1, pl.program_id(1) == grid[1] - 1))(wait)

    sem = ("arbitrary", "arbitrary") if n_host else ("parallel", "parallel")
    return pl.pallas_call(
        body, name=name, out_shape=out_shape, grid=grid, in_specs=in_specs, out_specs=out_spec,
        scratch_shapes=scratch, input_output_aliases=aliases, compiler_params=_params(sem))(*args)


def _mm_ffn_dn(dup, w_up, tm, tn, name):
    _, LP, F = dup.shape
    Dm = w_up.shape[0]
    nt = (((1,), (1,)), ((), ()))

    def body(a_ref, b_ref, o_ref):
        acc = lax.dot_general(a_ref[0], b_ref[:, 0:F], nt, preferred_element_type=F32)
        acc = acc + lax.dot_general(a_ref[1], b_ref[:, F:2 * F], nt, preferred_element_type=F32)
        o_ref[...] = acc

    return pl.pallas_call(
        body, name=name, out_shape=jax.ShapeDtypeStruct((LP, Dm), F32), grid=(LP // tm, Dm // tn),
        in_specs=[pl.BlockSpec((2, tm, F), lambda i, j: (0, i, 0)),
                  pl.BlockSpec((tn, 2 * F), lambda i, j: (j, 0))],
        out_specs=pl.BlockSpec((tm, tn), lambda i, j: (i, j)),
        compiler_params=_params(("parallel", "parallel")))(dup, w_up)


def _mm_ffn_dwup(n, dup, tk, tn, name):
    LP, Dm = n.shape
    F = dup.shape[2]
    nct = F // tn
    tdims = (((0,), (0,)), ((), ()))

    def body(a_ref, b_ref, o_ref):
        o_ref[...] = lax.dot_general(a_ref[...], b_ref[...], tdims, preferred_element_type=F32)

    return pl.pallas_call(
        body, name=name, out_shape=jax.ShapeDtypeStruct((Dm, 2 * F), F32), grid=(Dm // tk, 2 * nct),
        in_specs=[pl.BlockSpec((LP, tk), lambda i, j: (0, i)),
                  pl.BlockSpec((None, LP, tn), lambda i, j: (j // nct, 0, j % nct))],
        out_specs=pl.BlockSpec((tk, tn), lambda i, j: (i, j)),
        compiler_params=_params(("parallel", "parallel")))(n, dup)


def _rms_fwd(h, g, out_dtype, tm, name):
    LP, Dm = h.shape

    def body(h_ref, g_ref, o_ref):
        x = h_ref[...]
        r = lax.rsqrt(jnp.mean(x * x, axis=1, keepdims=True) + RMS_EPS)
        o_ref[...] = (x * r * g_ref[...]).astype(o_ref.dtype)

    return pl.pallas_call(
        body, name=name, out_shape=jax.ShapeDtypeStruct((LP, Dm), out_dtype), grid=(LP // tm,),
        in_specs=[pl.BlockSpec((tm, Dm), lambda i: (i, 0)), pl.BlockSpec((1, Dm), lambda i: (0, 0))],
        out_specs=pl.BlockSpec((tm, Dm), lambda i: (i, 0)),
        compiler_params=_params(("parallel",)))(h, g)


def _rms_bwd(h, g, dn, dres, tm, name, host=None):
    LP, Dm = h.shape
    n_host = 0 if host is None else len(host[0])
    if n_host:
        host_shapes, host_sems, host_ops = _host_plan(host)
    nblk = LP // tm

    def body(*refs):
        h_ref, g_ref, dn_ref, dr_ref = refs[:4]
        dh_ref, dg_ref = refs[4 + n_host:6 + n_host]
        if n_host:
            start, wait = host_ops(refs[4:4 + n_host], refs[6 + n_host:6 + 2 * n_host],
                                   refs[6 + 2 * n_host], refs[7 + 2 * n_host], host[1])
            pl.when(pl.program_id(0) == 0)(start)
        i = pl.program_id(0)
        x = h_ref[...]
        r = lax.rsqrt(jnp.mean(x * x, axis=1, keepdims=True) + RMS_EPS)
        xhat = x * r
        dy = dn_ref[...]
        dxh = dy * g_ref[...]
        dh = r * (dxh - xhat * jnp.mean(dxh * xhat, axis=1, keepdims=True))
        dh_ref[...] = dr_ref[...] + dh

        @pl.when(i == 0)
        def _():
            dg_ref[...] = jnp.zeros_like(dg_ref)

        dg_ref[...] += jnp.sum(dy * xhat, axis=0, keepdims=True)
        if n_host:
            pl.when(pl.program_id(0) == nblk - 1)(wait)

    row = pl.BlockSpec((tm, Dm), lambda i: (i, 0))
    vec = pl.BlockSpec((1, Dm), lambda i: (0, 0))
    out_shape = (jax.ShapeDtypeStruct((LP, Dm), F32), jax.ShapeDtypeStruct((1, Dm), F32))
    out_specs = (row, vec)
    in_specs = [row, vec, row, row]
    args = [h, g, dn, dres]
    scratch = []
    if n_host:
        anyspec = pl.BlockSpec(memory_space=pl.ANY)
        in_specs += [anyspec] * n_host
        args += list(host[0])
        out_shape += host_shapes
        out_specs += (anyspec,) * n_host
        scratch = [pltpu.SemaphoreType.DMA((host_sems,)), pltpu.SemaphoreType.DMA((host_sems,))]
    return pl.pallas_call(
        body, name=name, out_shape=out_shape, grid=(nblk,), in_specs=in_specs, out_specs=out_specs,
        scratch_shapes=scratch, compiler_params=_params(("arbitrary",)))(*args)


def _loss_head(h, g, tgt, n_real, tm, name):
    LP, Dm = h.shape

    def body(h_ref, g_ref, t_ref, loss_ref, dh_ref, dg_ref):
        i = pl.program_id(0)
        x = h_ref[...]
        gg = g_ref[...]
        r = lax.rsqrt(jnp.mean(x * x, axis=1, keepdims=True) + RMS_EPS)
        xhat = x * r
        rows = i * tm + lax.broadcasted_iota(jnp.int32, (tm, 1), 0)
        real = jnp.logical_and(rows >= N_META, rows < n_real)
        diff = jnp.where(real, xhat * gg - t_ref[...], 0.0)
        dy = diff * (1.0 / Dm)
        dxh = dy * gg
        dh_ref[...] = r * (dxh - xhat * jnp.mean(dxh * xhat, axis=1, keepdims=True))

        @pl.when(i == 0)
        def _():
            dg_ref[...] = jnp.zeros_like(dg_ref)
            loss_ref[...] = jnp.zeros_like(loss_ref)

        dg_ref[...] += jnp.sum(dy * xhat, axis=0, keepdims=True)
        part = jnp.sum(jnp.sum(diff * diff, axis=1, keepdims=True), axis=0, keepdims=True)
        loss_ref[...] += jnp.broadcast_to(part * (0.5 / Dm), loss_ref.shape)

    row = pl.BlockSpec((tm, Dm), lambda i: (i, 0))
    vec = pl.BlockSpec((1, Dm), lambda i: (0, 0))
    return pl.pallas_call(
        body, name=name,
        out_shape=(jax.ShapeDtypeStruct((1, LANE), F32), jax.ShapeDtypeStruct((LP, Dm), F32),
                   jax.ShapeDtypeStruct((1, Dm), F32)),
        grid=(LP // tm,), in_specs=[row, vec, row],
        out_specs=(pl.BlockSpec((1, LANE), lambda i: (0, 0)), row, vec),
        compiler_params=_params(("arbitrary",)))(h, g, tgt)


def _fgate_fwd(proj, bf_p, name):
    LP = proj.shape[0]
    nb = LP // LANE

    def body(f_ref, b_ref, c_ref, lf_ref):
        x = f_ref[...] + b_ref[...]
        lf_ref[...] = jnp.minimum(x, 0.0) - jnp.log1p(jnp.exp(-jnp.abs(x)))
        ri = lax.broadcasted_iota(jnp.int32, (LANE, LANE), 0)
        ci = lax.broadcasted_iota(jnp.int32, (LANE, LANE), 1)
        tri = jnp.where(ri >= ci, 1.0, 0.0).astype(F32)

        def blk(i, carry):
            rows = pl.ds(pl.multiple_of(i * LANE, LANE), LANE)
            cb = jnp.dot(tri, lf_ref[rows, :], precision=lax.Precision.HIGHEST,
                         preferred_element_type=F32) + carry
            c_ref[rows, :] = cb
            return cb[LANE - 1:LANE, :]

        lax.fori_loop(0, nb, blk, jnp.zeros((1, LANE), F32))

    return pl.pallas_call(
        body, name=name, out_shape=jax.ShapeDtypeStruct((LP, LANE), F32), grid=(1,),
        in_specs=[pl.BlockSpec((LP, LANE), lambda i: (0, F_COL_BLK)),
                  pl.BlockSpec((1, LANE), lambda i: (0, 0))],
        out_specs=pl.BlockSpec((LP, LANE), lambda i: (0, 0)),
        scratch_shapes=[pltpu.VMEM((LP, LANE), F32)],
        compiler_params=_params(("arbitrary",)))(proj, bf_p)


def _fgate_bwd(proj, bf_p, dc, name):
    LP = proj.shape[0]
    nb = LP // LANE

    def body(f_ref, b_ref, dc_ref, dl_ref, db_ref):
        ri = lax.broadcasted_iota(jnp.int32, (LANE, LANE), 0)
        ci = lax.broadcasted_iota(jnp.int32, (LANE, LANE), 1)
        triu = jnp.where(ri <= ci, 1.0, 0.0).astype(F32)
        bb = b_ref[...]

        tail = jnp.zeros((1, LANE), F32)
        dbs = jnp.zeros((1, LANE), F32)
        for i in range(nb - 1, -1, -1):
            rows = slice(i * LANE, (i + 1) * LANE)
            gb = jnp.dot(triu, dc_ref[rows, :], precision=lax.Precision.HIGHEST,
                         preferred_element_type=F32) + tail
            x = f_ref[rows, :] + bb
            dl = gb * _sigmoid_tail(-x)
            dl_ref[rows, :] = dl.astype(dl_ref.dtype)
            tail = gb[0:1, :]
            dbs = dbs + jnp.sum(dl, axis=0, keepdims=True)
        db_ref[...] = dbs

    return pl.pallas_call(
        body, name=name,
        out_shape=(jax.ShapeDtypeStruct((LP, LANE), BF16), jax.ShapeDtypeStruct((1, LANE), F32)),
        grid=(1,),
        in_specs=[pl.BlockSpec((LP, LANE), lambda i: (0, F_COL_BLK)),
                  pl.BlockSpec((1, LANE), lambda i: (0, 0)),
                  pl.BlockSpec((LP, LANE), lambda i: (0, 0))],
        out_specs=(pl.BlockSpec((LP, LANE), lambda i: (0, 0)), pl.BlockSpec((1, LANE), lambda i: (0, 0))),
        compiler_params=_params(("arbitrary",)))(proj, bf_p, dc)


def _attn_fwd(q, k, v, c_col, c_row, name):
    Hh, LP, Dh = q.shape
    nb = LP // ATT_BLK
    scale = Dh ** -0.5
    nt = (((1,), (1,)), ((), ()))

    def body(q_ref, k_ref, v_ref, cc_ref, cr_ref, o_ref, lse_ref):
        i = pl.program_id(1)
        qb = q_ref[...]
        cq = cc_ref[...]
        rows = i * ATT_BLK + lax.broadcasted_iota(jnp.int32, (ATT_BLK, ATT_BLK), 0)
        cols0 = lax.broadcasted_iota(jnp.int32, (ATT_BLK, ATT_BLK), 1)

        def step(j, carry):
            m, l, acc = carry
            ks = pl.ds(pl.multiple_of(j * ATT_BLK, ATT_BLK), ATT_BLK)
            s = lax.dot_general(qb, k_ref[ks, :], nt, preferred_element_type=F32) * scale
            s = s + cq - cr_ref[j]
            s = jnp.where(cols0 + j * ATT_BLK <= rows, s, NEG)
            m_new = jnp.maximum(m, jnp.max(s, axis=1, keepdims=True))
            p = jnp.exp(s - m_new)
            alpha = jnp.exp(m - m_new)
            l = alpha * l + jnp.sum(p, axis=1, keepdims=True)
            acc = alpha * acc + jnp.dot(p.astype(BF16), v_ref[ks, :], preferred_element_type=F32)
            return m_new, l, acc

        init = (jnp.full((ATT_BLK, 1), NEG, F32), jnp.zeros((ATT_BLK, 1), F32),
                jnp.zeros((ATT_BLK, Dh), F32))
        m, l, acc = lax.fori_loop(0, i + 1, step, init)
        o_ref[...] = acc / l
        lse_ref[...] = m + jnp.log(l)

    qspec = pl.BlockSpec((None, ATT_BLK, Dh), lambda h, i: (h, i, 0))
    kspec = pl.BlockSpec((None, LP, Dh), lambda h, i: (h, 0, 0))
    colspec = pl.BlockSpec((None, ATT_BLK, 1), lambda h, i: (h, i, 0))
    rowspec = pl.BlockSpec((None, nb, 1, ATT_BLK), lambda h, i: (h, 0, 0, 0))
    return pl.pallas_call(
        body, name=name,
        out_shape=(jax.ShapeDtypeStruct((Hh, LP, Dh), F32), jax.ShapeDtypeStruct((Hh, LP, 1), F32)),
        grid=(Hh, nb), in_specs=[qspec, kspec, kspec, colspec, rowspec],
        out_specs=(qspec, colspec),
        compiler_params=_params(("parallel", "arbitrary")))(q, k, v, c_col, c_row)


def _attn_bwd(q, k, v, o, do, lse, c_col, c_row, name):
    Hh, LP, Dh = q.shape
    nb = LP // ATT_BLK
    scale = Dh ** -0.5
    nt = (((1,), (1,)), ((), ()))
    tn = (((0,), (0,)), ((), ()))

    def body(q_ref, k_ref, v_ref, o_ref, do_ref, lse_ref, cc_ref, cr_ref,
             dq_ref, dk_ref, dv_ref, dcq_ref, dc_ref, delta_ref):
        j = pl.program_id(1)

        @pl.when(j == 0)
        def _():
            dq_ref[...] = jnp.zeros_like(dq_ref)
            dcq_ref[...] = jnp.zeros_like(dcq_ref)
            dob_all = do_ref[...].astype(BF16).astype(F32)
            delta_ref[...] = jnp.sum(dob_all * o_ref[...], axis=1, keepdims=True)

        kb = k_ref[...]
        vb = v_ref[...]
        ck = cr_ref[j]
        rows0 = lax.broadcasted_iota(jnp.int32, (ATT_BLK, ATT_BLK), 0)
        cols = j * ATT_BLK + lax.broadcasted_iota(jnp.int32, (ATT_BLK, ATT_BLK), 1)

        def step(i, carry):
            dk, dv, dcs = carry
            qs = pl.ds(pl.multiple_of(i * ATT_BLK, ATT_BLK), ATT_BLK)
            qb = q_ref[qs, :]
            dob = do_ref[qs, :].astype(BF16)
            s = lax.dot_general(qb, kb, nt, preferred_element_type=F32) * scale
            s = s + cc_ref[qs, :] - ck
            s = jnp.where(cols <= rows0 + i * ATT_BLK, s, NEG)
            p = jnp.exp(s - lse_ref[qs, :])
            dp = lax.dot_general(dob, vb, nt, preferred_element_type=F32)
            ds = p * (dp - delta_ref[qs, :])
            dsb = ds.astype(BF16)
            dv = dv + lax.dot_general(p.astype(BF16), dob, tn, preferred_element_type=F32)
            dk = dk + lax.dot_general(dsb, qb, tn, preferred_element_type=F32) * scale
            dq_ref[qs, :] += jnp.dot(dsb, kb, preferred_element_type=F32) * scale
            dcq_ref[qs, :] += jnp.sum(ds, axis=1, keepdims=True)
            dcs = dcs - jnp.sum(ds, axis=0, keepdims=True)
            return dk, dv, dcs

        init = (jnp.zeros((ATT_BLK, Dh), F32), jnp.zeros((ATT_BLK, Dh), F32),
                jnp.zeros((1, ATT_BLK), F32))
        dk, dv, dcs = lax.fori_loop(j, nb, step, init)
        dk_ref[...] = dk
        dv_ref[...] = dv
        dc_ref[...] = dcs

    full = pl.BlockSpec((None, LP, Dh), lambda h, j: (h, 0, 0))
    blk = pl.BlockSpec((None, ATT_BLK, Dh), lambda h, j: (h, j, 0))
    col = pl.BlockSpec((None, LP, 1), lambda h, j: (h, 0, 0))
    rowspec = pl.BlockSpec((None, nb, 1, ATT_BLK), lambda h, j: (h, 0, 0, 0))
    return pl.pallas_call(
        body, name=name,
        out_shape=(jax.ShapeDtypeStruct((Hh, LP, Dh), F32), jax.ShapeDtypeStruct((Hh, LP, Dh), F32),
                   jax.ShapeDtypeStruct((Hh, LP, Dh), F32), jax.ShapeDtypeStruct((Hh, LP, 1), F32),
                   jax.ShapeDtypeStruct((Hh, nb, 1, ATT_BLK), F32)),
        grid=(Hh, nb), in_specs=[full, blk, blk, full, full, col, col, rowspec],
        out_specs=(full, blk, blk, col, pl.BlockSpec((None, None, 1, ATT_BLK), lambda h, j: (h, j, 0, 0))),
        scratch_shapes=[pltpu.VMEM((LP, 1), F32)],
        compiler_params=_params(("parallel", "arbitrary")))(q, k, v, o, do, lse, c_col, c_row)


AUG = 128
ONES_IN_K = HEAD_DIM
ONES_IN_Q = HEAD_DIM + 3
ATT_HEADS_PER_STEP = 8
ATT_HEADS_PER_STEP_BWD = 8


def _attn_prep(proj, c, name):
    LP = proj.shape[0]
    nb = LP // ATT_BLK
    tail_rows = AUG - HEAD_DIM

    def body(q_ref, k_ref, v_ref, c_ref, qT_ref, kT_ref, ka_ref, vT_ref):
        qt = (q_ref[...] * (HEAD_DIM ** -0.5)).T
        kt = k_ref[...].T
        vt = v_ref[...].T
        ct = c_ref[...].T
        hi = ct.astype(BF16).astype(F32)
        r1 = ct - hi
        mid = r1.astype(BF16).astype(F32)
        lo = (r1 - mid).astype(BF16).astype(F32)
        row = lax.broadcasted_iota(jnp.int32, (tail_rows, ATT_BLK), 0)
        ones = jnp.where(row < 3, 1.0, 0.0)
        for h in range(HEADS):
            cparts = jnp.where(row == 0, hi[h:h + 1], jnp.where(row == 1, mid[h:h + 1],
                               jnp.where(row == 2, lo[h:h + 1], 0.0)))
            hs = slice(h * HEAD_DIM, (h + 1) * HEAD_DIM)
            q_tail = cparts + pltpu.roll(ones, 3, 0)
            k_tail = ones - pltpu.roll(cparts, 3, 0)
            qT_ref[h] = jnp.concatenate([qt[hs], q_tail], axis=0).astype(BF16)
            kfull = jnp.concatenate([kt[hs], k_tail], axis=0)
            kT_ref[h] = kfull.astype(BF16)
            ka_ref[h] = kfull.T.astype(BF16)
            vT_ref[h] = vt[hs].astype(BF16)

    col = lambda j: pl.BlockSpec((ATT_BLK, FOX_W), lambda i: (i, j))
    blk = lambda r: pl.BlockSpec((HEADS, None, r, ATT_BLK), lambda i: (0, i, 0, 0))
    return pl.pallas_call(
        body, name=name,
        out_shape=(jax.ShapeDtypeStruct((HEADS, nb, AUG, ATT_BLK), BF16),
                   jax.ShapeDtypeStruct((HEADS, nb, AUG, ATT_BLK), BF16),
                   jax.ShapeDtypeStruct((HEADS, LP, AUG), BF16),
                   jax.ShapeDtypeStruct((HEADS, nb, HEAD_DIM, ATT_BLK), BF16)),
        grid=(nb,), in_specs=[col(0), col(1), col(2), pl.BlockSpec((ATT_BLK, LANE), lambda i: (i, 0))],
        out_specs=(blk(AUG), blk(AUG), pl.BlockSpec((HEADS, ATT_BLK, AUG), lambda i: (0, i, 0)), blk(HEAD_DIM)),
        compiler_params=_params(("parallel",)))(proj, proj, proj, c)


def _attn_rows(xT, scale, out_dtype, name):
    Hh, nb, R, _ = xT.shape

    def body(x_ref, o_ref):
        stack = jnp.concatenate([x_ref[h, 0:HEAD_DIM, :] for h in range(Hh)], axis=0)
        o_ref[...] = (stack * scale).T.astype(o_ref.dtype)

    return pl.pallas_call(
        body, name=name, out_shape=jax.ShapeDtypeStruct((nb * ATT_BLK, Hh * HEAD_DIM), out_dtype), grid=(nb,),
        in_specs=[pl.BlockSpec((Hh, None, R, ATT_BLK), lambda i: (0, i, 0, 0))],
        out_specs=pl.BlockSpec((ATT_BLK, Hh * HEAD_DIM), lambda i: (i, 0)),
        compiler_params=_params(("parallel",)))(xT)


def _attn_cols(x, name):
    LP = x.shape[0]
    nb = LP // ATT_BLK

    def body(x_ref, o_ref):
        xt = x_ref[...].T
        for h in range(HEADS):
            o_ref[h] = xt[h * HEAD_DIM:(h + 1) * HEAD_DIM].astype(o_ref.dtype)

    return pl.pallas_call(
        body, name=name, out_shape=jax.ShapeDtypeStruct((HEADS, nb, HEAD_DIM, ATT_BLK), BF16), grid=(nb,),
        in_specs=[pl.BlockSpec((ATT_BLK, FOX_W), lambda i: (i, 0))],
        out_specs=pl.BlockSpec((HEADS, None, HEAD_DIM, ATT_BLK), lambda i: (0, i, 0, 0)),
        compiler_params=_params(("parallel",)))(x)


def _attn_dc(dqT, dkT, name):
    Hh, nb, _, _ = dqT.shape

    def body(q_ref, k_ref, o_ref):
        row = lax.broadcasted_iota(jnp.int32, (LANE, ATT_BLK), 0)
        acc = jnp.zeros((LANE, ATT_BLK), F32)
        for h in range(Hh):
            d = q_ref[h, ONES_IN_K:ONES_IN_K + 1, :] - k_ref[h, ONES_IN_Q:ONES_IN_Q + 1, :]
            acc = jnp.where(row == h, d, acc)
        o_ref[...] = acc.T

    spec = pl.BlockSpec((Hh, None, AUG, ATT_BLK), lambda i: (0, i, 0, 0))
    return pl.pallas_call(
        body, name=name, out_shape=jax.ShapeDtypeStruct((nb * ATT_BLK, LANE), F32), grid=(nb,),
        in_specs=[spec, spec], out_specs=pl.BlockSpec((ATT_BLK, LANE), lambda i: (i, 0)),
        compiler_params=_params(("parallel",)))(dqT, dkT)


def _attn_fwd2(qT, k_aug, vT, name, comm=None):
    Hh, nb, _, _ = qT.shape
    LP = nb * ATT_BLK
    Dh = vT.shape[2]
    HB = ATT_HEADS_PER_STEP
    n_comm = 0 if comm is None else len(comm[0])

    def body(*refs):
        q_ref, k_ref, v_ref = refs[:3]
        o_ref, lse_ref = refs[3 + n_comm:5 + n_comm]
        if n_comm:
            start, wait = _gather_first_ops(refs[3:3 + n_comm], refs[5 + n_comm:5 + 2 * n_comm],
                                            refs[5 + 2 * n_comm], refs[6 + 2 * n_comm], comm[1])
            pl.when(pl.program_id(0) == 0)(start)
        keys = lax.broadcasted_iota(jnp.int32, (ATT_BLK, ATT_BLK), 0)
        qrys = lax.broadcasted_iota(jnp.int32, (ATT_BLK, ATT_BLK), 1)
        causal = keys <= qrys

        def q_block(i, _):
            def tile(j, carry, masked):
                ks = pl.ds(pl.multiple_of(j * ATT_BLK, ATT_BLK), ATT_BLK)
                s_all = [jnp.dot(k_ref[hh, ks, :], q_ref[hh, i], preferred_element_type=F32) for hh in range(HB)]
                stats, p_all = [], []
                for hh in range(HB):
                    m, l, _ = carry[hh]
                    s = jnp.where(causal, s_all[hh], NEG) if masked else s_all[hh]
                    m_new = jnp.maximum(m, jnp.max(s, axis=0, keepdims=True))
                    p = jnp.exp(s - m_new)
                    alpha = jnp.exp(m - m_new)
                    stats.append((m_new, alpha * l + jnp.sum(p, axis=0, keepdims=True), alpha))
                    p_all.append(p.astype(BF16))
                out = []
                for hh in range(HB):
                    m_new, l, alpha = stats[hh]
                    acc = alpha * carry[hh][2] + jnp.dot(v_ref[hh, j], p_all[hh], preferred_element_type=F32)
                    out.append((m_new, l, acc))
                return tuple(out)

            init = tuple((jnp.full((1, ATT_BLK), NEG, F32), jnp.zeros((1, ATT_BLK), F32),
                          jnp.zeros((Dh, ATT_BLK), F32)) for _ in range(HB))
            carry = lax.fori_loop(0, i, lambda j, cr: tile(j, cr, False), init)
            carry = tile(i, carry, True)
            for hh in range(HB):
                m, l, acc = carry[hh]
                o_ref[hh, i] = acc / l
                lse_ref[hh, i] = m + jnp.log(l)
            return 0

        lax.fori_loop(0, nb, q_block, 0)
        if n_comm:
            pl.when(pl.program_id(0) == Hh // HB - 1)(wait)

    blk = lambda r: pl.BlockSpec((HB, nb, r, ATT_BLK), lambda h: (h, 0, 0, 0))
    out_shape = (jax.ShapeDtypeStruct((Hh, nb, Dh, ATT_BLK), F32), jax.ShapeDtypeStruct((Hh, nb, 1, ATT_BLK), F32))
    scratch = []
    args = [qT, k_aug, vT]
    if n_comm:
        out_shape += tuple(jax.ShapeDtypeStruct(cut.full, s.dtype) for cut, s in zip(comm[1], comm[0]))
        scratch = [pltpu.SemaphoreType.DMA((4 * n_comm,)), pltpu.SemaphoreType.DMA((4 * n_comm,))]
        args += list(comm[0])
    return pl.pallas_call(
        body, name=name, out_shape=out_shape, grid=(Hh // HB,),
        in_specs=[blk(AUG), pl.BlockSpec((HB, LP, AUG), lambda h: (h, 0, 0)), blk(Dh)] + [ANY] * n_comm,
        out_specs=(blk(Dh), blk(1)) + (ANY,) * n_comm, scratch_shapes=scratch,
        compiler_params=_params(("arbitrary",)))(*args)


def _attn_bwd2(qT, kT, k_aug, v, oT, doT, lse, name, comm=None):
    Hh, nb, _, _ = qT.shape
    LP = nb * ATT_BLK
    Dh = v.shape[2]
    nt = (((1,), (1,)), ((), ()))
    tn = (((0,), (0,)), ((), ()))

    HB = ATT_HEADS_PER_STEP_BWD
    n_comm = 0 if comm is None else len(comm[0])

    def body(*refs):
        q_ref, kt_ref, k_ref, v_ref, o_ref, do_ref, lse_ref = refs[:7]
        parts = refs[7:7 + n_comm]
        dq_ref, dk_ref, dv_ref = refs[7 + n_comm:10 + n_comm]
        others = refs[10 + n_comm:10 + 2 * n_comm]
        delta_ref = refs[10 + 2 * n_comm]
        if n_comm:
            start, wait = _chip_exchange_ops(parts, others, refs[11 + 2 * n_comm], refs[12 + 2 * n_comm], comm[1])
            pl.when(pl.program_id(0) == 0)(start)
        keys = lax.broadcasted_iota(jnp.int32, (ATT_BLK, ATT_BLK), 0)
        qrys = lax.broadcasted_iota(jnp.int32, (ATT_BLK, ATT_BLK), 1)
        causal = keys <= qrys

        def prep(i, _):
            for hh in range(HB):
                delta_ref[hh, i] = jnp.sum(do_ref[hh, i].astype(F32) * o_ref[hh, i], axis=0, keepdims=True)
                dq_ref[hh, i] = jnp.zeros((AUG, ATT_BLK), F32)
            return 0

        lax.fori_loop(0, nb, prep, 0)

        def kv_block(j, _):
            ks = pl.ds(pl.multiple_of(j * ATT_BLK, ATT_BLK), ATT_BLK)

            def tile(i, carry, masked):
                s_all = [jnp.dot(k_ref[hh, ks, :], q_ref[hh, i], preferred_element_type=F32) for hh in range(HB)]
                dp_all = [lax.dot_general(v_ref[hh, j], do_ref[hh, i], tn, preferred_element_type=F32)
                          for hh in range(HB)]
                p_all, ds_all = [], []
                for hh in range(HB):
                    s = jnp.where(causal, s_all[hh], NEG) if masked else s_all[hh]
                    p = jnp.exp(s - lse_ref[hh, i])
                    ds_all.append((p * (dp_all[hh] - delta_ref[hh, i])).astype(BF16))
                    p_all.append(p.astype(BF16))
                out = []
                for hh in range(HB):
                    dk, dv = carry[hh]
                    dv = dv + lax.dot_general(do_ref[hh, i], p_all[hh], nt, preferred_element_type=F32)
                    dk = dk + lax.dot_general(q_ref[hh, i], ds_all[hh], nt, preferred_element_type=F32)
                    out.append((dk, dv))
                dq_new = [jnp.dot(kt_ref[hh, j], ds_all[hh], preferred_element_type=F32) for hh in range(HB)]
                for hh in range(HB):
                    dq_ref[hh, i] += dq_new[hh]
                return tuple(out)

            init = tuple((jnp.zeros((AUG, ATT_BLK), F32), jnp.zeros((Dh, ATT_BLK), F32)) for _ in range(HB))
            carry = tile(j, init, True)
            carry = lax.fori_loop(j + 1, nb, lambda i, cr: tile(i, cr, False), carry)
            for hh in range(HB):
                dk_ref[hh, j] = carry[hh][0]
                dv_ref[hh, j] = carry[hh][1]
            return 0

        lax.fori_loop(0, nb, kv_block, 0)
        if n_comm:
            pl.when(pl.program_id(0) == Hh // HB - 1)(wait)

    blk = lambda r: pl.BlockSpec((HB, nb, r, ATT_BLK), lambda h: (h, 0, 0, 0))
    row = lambda cols: pl.BlockSpec((HB, LP, cols), lambda h: (h, 0, 0))
    out_shape = (jax.ShapeDtypeStruct((Hh, nb, AUG, ATT_BLK), F32), jax.ShapeDtypeStruct((Hh, nb, AUG, ATT_BLK), F32),
                 jax.ShapeDtypeStruct((Hh, nb, Dh, ATT_BLK), F32))
    scratch = [pltpu.VMEM((HB, nb, 1, ATT_BLK), F32)]
    args = [qT, kT, k_aug, v, oT, doT, lse]
    if n_comm:
        out_shape += _chip_exchange_shapes(*comm)
        scratch += [pltpu.SemaphoreType.DMA((3 * n_comm,)), pltpu.SemaphoreType.DMA((3 * n_comm,))]
        args += list(comm[0])
    return pl.pallas_call(
        body, name=name, out_shape=out_shape, grid=(Hh // HB,),
        in_specs=[blk(AUG), blk(AUG), row(AUG), blk(Dh), blk(Dh), blk(Dh), blk(1)] + [ANY] * n_comm,
        out_specs=(blk(AUG), blk(AUG), blk(Dh)) + (ANY,) * n_comm,
        scratch_shapes=scratch,
        compiler_params=_params(("arbitrary",)))(*args)


CONV_HALO = 32
A_BLK = 3 * FOX_W // CONV_CH
G_BLK = A_BLK + 1


def _conf_fwd(proj, cw, cb, lg, lb, tm, name):
    LP = proj.shape[0]
    C = CONV_CH
    sub = _sub_rows(tm)
    hpb = tm // CONV_HALO

    def body(a_ref, g_ref, ah_ref, gh_ref, w_ref, cb_ref, lg_ref, lb_ref, u1_ref, u_ref, buf):
        r = pl.program_id(0)
        buf[CONV_HALO:CONV_HALO + tm, :] = a_ref[...] * _sigmoid(g_ref[...])
        buf[0:CONV_HALO, :] = jnp.where(r > 0, ah_ref[...] * _sigmoid(gh_ref[...]), 0.0)
        for s in range(tm // sub):
            for ct in range(C // LANE):
                ln = slice(ct * LANE, (ct + 1) * LANE)
                acc = jnp.broadcast_to(cb_ref[:, ln], (sub, LANE))
                for kk in range(CONV_K):
                    off = CONV_HALO + s * sub - (CONV_K - 1) + kk
                    acc = acc + w_ref[kk:kk + 1, ln] * buf[off:off + sub, ln]
                u1_ref[s * sub:(s + 1) * sub, ln] = acc
        u1 = u1_ref[...]
        mu = jnp.mean(u1, axis=1, keepdims=True)
        xc = u1 - mu
        var = jnp.mean(xc * xc, axis=1, keepdims=True)
        y = xc * lax.rsqrt(var + LN_EPS) * lg_ref[...] + lb_ref[...]
        u_ref[...] = (y * _sigmoid(y)).astype(u_ref.dtype)

    cur = lambda blk: pl.BlockSpec((tm, C), lambda r: (r, blk))
    halo = lambda blk: pl.BlockSpec((CONV_HALO, C), lambda r: (jnp.maximum(r * hpb - 1, 0), blk))
    vec = pl.BlockSpec((1, C), lambda r: (0, 0))
    out = pl.BlockSpec((tm, C), lambda r: (r, 0))
    return pl.pallas_call(
        body, name=name,
        out_shape=(jax.ShapeDtypeStruct((LP, C), F32), jax.ShapeDtypeStruct((LP, C), BF16)),
        grid=(LP // tm,),
        in_specs=[cur(A_BLK), cur(G_BLK), halo(A_BLK), halo(G_BLK),
                  pl.BlockSpec((CONV_HALO, C), lambda r: (0, 0)), vec, vec, vec],
        out_specs=(out, out),
        scratch_shapes=[pltpu.VMEM((CONV_HALO + tm, C), F32)],
        compiler_params=_params(("parallel",)))(proj, proj, proj, proj, cw, cb, lg, lb)


def _conf_bwd(proj, u1, dcat, cw, lg, lb, tm, name, host=None):
    LP = proj.shape[0]
    C = CONV_CH
    sub = _sub_rows(tm)
    hpb = tm // CONV_HALO
    nblk = LP // tm
    last_halo = LP // CONV_HALO - 1

    n_host = 0 if host is None else len(host[0])
    if n_host:
        host_shapes, host_sems, host_ops = _host_plan(host)

    def body(*refs):
        a_ref, g_ref, ah_ref, gh_ref, u1_ref, u1n_ref, du_ref, dun_ref, w_ref, lg_ref, lb_ref = refs[:11]
        dadg_ref, dw_ref, dcb_ref, dlg_ref, dlb_ref = refs[11 + n_host:16 + n_host]
        ubuf, dbuf, du0 = refs[16 + 2 * n_host:19 + 2 * n_host]
        if n_host:
            start, wait = host_ops(refs[11:11 + n_host], refs[16 + n_host:16 + 2 * n_host],
                                   refs[19 + 2 * n_host], refs[20 + 2 * n_host], host[1])
            pl.when(pl.program_id(0) == 0)(start)
        r = pl.program_id(0)
        lgv = lg_ref[...]
        lbv = lb_ref[...]

        def ln_silu_bwd(u1v, duv):
            mu = jnp.mean(u1v, axis=1, keepdims=True)
            xc = u1v - mu
            rstd = lax.rsqrt(jnp.mean(xc * xc, axis=1, keepdims=True) + LN_EPS)
            xhat = xc * rstd
            y = xhat * lgv + lbv
            sg = _sigmoid(y)
            dy = duv * (sg * (1.0 + y * (1.0 - sg)))
            dxh = dy * lgv
            du1 = rstd * (dxh - jnp.mean(dxh, axis=1, keepdims=True)
                          - xhat * jnp.mean(dxh * xhat, axis=1, keepdims=True))
            return du1, dy, xhat

        @pl.when(r == 0)
        def _():
            dw_ref[...] = jnp.zeros_like(dw_ref)
            dcb_ref[...] = jnp.zeros_like(dcb_ref)
            dlg_ref[...] = jnp.zeros_like(dlg_ref)
            dlb_ref[...] = jnp.zeros_like(dlb_ref)

        du1, dy, xhat = ln_silu_bwd(u1_ref[...], du_ref[...])
        dlg_ref[...] += jnp.sum(dy * xhat, axis=0, keepdims=True)
        dlb_ref[...] += jnp.sum(dy, axis=0, keepdims=True)
        dcb_ref[...] += jnp.sum(du1, axis=0, keepdims=True)
        dbuf[0:tm, :] = du1
        du1n, _, _ = ln_silu_bwd(u1n_ref[...], dun_ref[...])
        dbuf[tm:tm + CONV_HALO, :] = jnp.where(r < nblk - 1, du1n, 0.0)
        ubuf[CONV_HALO:CONV_HALO + tm, :] = a_ref[...] * _sigmoid(g_ref[...])
        ubuf[0:CONV_HALO, :] = jnp.where(r > 0, ah_ref[...] * _sigmoid(gh_ref[...]), 0.0)

        for ct in range(C // LANE):
            ln = slice(ct * LANE, (ct + 1) * LANE)
            for s in range(tm // sub):
                d_here = dbuf[s * sub:(s + 1) * sub, ln]
                acc = jnp.zeros((sub, LANE), F32)
                for kk in range(CONV_K):
                    fo = s * sub + (CONV_K - 1) - kk
                    acc = acc + w_ref[kk:kk + 1, ln] * dbuf[fo:fo + sub, ln]
                    bo = CONV_HALO + s * sub - (CONV_K - 1) + kk
                    dw_ref[kk:kk + 1, ln] += jnp.sum(d_here * ubuf[bo:bo + sub, ln], axis=0, keepdims=True)
                du0[s * sub:(s + 1) * sub, ln] = acc
        a = a_ref[...]
        sg = _sigmoid(g_ref[...])
        d0 = du0[...]
        dadg_ref[:, 0:C] = (d0 * sg).astype(dadg_ref.dtype)
        dadg_ref[:, C:2 * C] = (d0 * a * sg * (1.0 - sg)).astype(dadg_ref.dtype)
        if n_host:
            pl.when(pl.program_id(0) == nblk - 1)(wait)

    cur = lambda blk: pl.BlockSpec((tm, C), lambda r: (r, blk))
    prev = lambda blk: pl.BlockSpec((CONV_HALO, C), lambda r: (jnp.maximum(r * hpb - 1, 0), blk))
    nxt = lambda blk: pl.BlockSpec((CONV_HALO, C), lambda r: (jnp.minimum((r + 1) * hpb, last_halo), blk))
    vec = pl.BlockSpec((1, C), lambda r: (0, 0))
    wspec = pl.BlockSpec((CONV_HALO, C), lambda r: (0, 0))
    out_shape = (jax.ShapeDtypeStruct((LP, 2 * C), BF16), jax.ShapeDtypeStruct((CONV_HALO, C), F32),
                 jax.ShapeDtypeStruct((1, C), F32), jax.ShapeDtypeStruct((1, C), F32),
                 jax.ShapeDtypeStruct((1, C), F32))
    out_specs = (pl.BlockSpec((tm, 2 * C), lambda r: (r, 0)), wspec, vec, vec, vec)
    in_specs = [cur(A_BLK), cur(G_BLK), prev(A_BLK), prev(G_BLK), cur(0), nxt(0), cur(1), nxt(1), wspec, vec, vec]
    args = [proj, proj, proj, proj, u1, u1, dcat, dcat, cw, lg, lb]
    scratch = [pltpu.VMEM((CONV_HALO + tm, C), F32), pltpu.VMEM((tm + CONV_HALO, C), F32),
               pltpu.VMEM((tm, C), F32)]
    if n_host:
        anyspec = pl.BlockSpec(memory_space=pl.ANY)
        in_specs += [anyspec] * n_host
        args += list(host[0])
        out_shape += host_shapes
        out_specs += (anyspec,) * n_host
        scratch += [pltpu.SemaphoreType.DMA((host_sems,)), pltpu.SemaphoreType.DMA((host_sems,))]
    return pl.pallas_call(
        body, name=name, out_shape=out_shape, grid=(nblk,), in_specs=in_specs, out_specs=out_specs,
        scratch_shapes=scratch, compiler_params=_params(("arbitrary",)))(*args)


FFN_HALO = 8
FFN_TC = 1408
FFN_ROW_SPLIT = 2
FFN_K = 3


def _ffn_conv(buf, w_ref, b_ref, s, sub, ln):
    acc = jnp.broadcast_to(b_ref[:, ln], (sub, LANE))
    for kk in range(FFN_K):
        off = FFN_HALO + s * sub - (FFN_K - 1) + kk
        acc = acc + w_ref[kk:kk + 1, ln] * buf[off:off + sub, ln]
    return acc


def _ffn_act_fwd(up, w, b, tm, name, host=None):
    LP, F = up.shape[0], up.shape[1] // 2
    tm = tm // FFN_ROW_SPLIT
    upg = upv = up
    nct = F // FFN_TC
    sub = _sub_rows(tm)
    hpb = tm // FFN_HALO
    n_host = 0 if host is None else len(host[0])
    nrb = LP // tm

    def body(*refs):
        g_ref, v_ref, gh_ref, vh_ref, wg_ref, wv_ref, bg_ref, bv_ref = refs[:8]
        act_ref = refs[8 + n_host]
        gbuf, vbuf = refs[9 + 2 * n_host:11 + 2 * n_host]
        if n_host:
            start, wait = _gather_first_ops(refs[8:8 + n_host], refs[9 + n_host:9 + 2 * n_host],
                                            refs[11 + 2 * n_host], refs[12 + 2 * n_host], host[1])
            pl.when(jnp.logical_and(pl.program_id(0) == 0, pl.program_id(1) == 0))(start)
        r = pl.program_id(1)
        gbuf[FFN_HALO:FFN_HALO + tm, :] = g_ref[...]
        vbuf[FFN_HALO:FFN_HALO + tm, :] = v_ref[...]
        gbuf[0:FFN_HALO, :] = jnp.where(r > 0, gh_ref[...], 0.0)
        vbuf[0:FFN_HALO, :] = jnp.where(r > 0, vh_ref[...], 0.0)
        for s in range(tm // sub):
            for ct in range(FFN_TC // LANE):
                ln = slice(ct * LANE, (ct + 1) * LANE)
                gc = _ffn_conv(gbuf, wg_ref, bg_ref, s, sub, ln)
                vc = _ffn_conv(vbuf, wv_ref, bv_ref, s, sub, ln)
                act_ref[s * sub:(s + 1) * sub, ln] = (gc * _sigmoid(gc) * vc).astype(act_ref.dtype)
        if n_host:
            pl.when(jnp.logical_and(pl.program_id(0) == nct - 1, pl.program_id(1) == nrb - 1))(wait)

    cur = pl.BlockSpec((tm, FFN_TC), lambda c, r: (r, c))
    halo = pl.BlockSpec((FFN_HALO, FFN_TC), lambda c, r: (jnp.maximum(r * hpb - 1, 0), c))
    wg = pl.BlockSpec((8, FFN_TC), lambda c, r: (0, c))
    wv = pl.BlockSpec((8, FFN_TC), lambda c, r: (0, nct + c))
    bg = pl.BlockSpec((1, FFN_TC), lambda c, r: (0, c))
    bv = pl.BlockSpec((1, FFN_TC), lambda c, r: (0, nct + c))
    curv = pl.BlockSpec((tm, FFN_TC), lambda c, r: (r, nct + c))
    halov = pl.BlockSpec((FFN_HALO, FFN_TC), lambda c, r: (jnp.maximum(r * hpb - 1, 0), nct + c))
    out_shape = jax.ShapeDtypeStruct((LP, F), BF16)
    out_specs = cur
    in_specs = [cur, curv, halo, halov, wg, wv, bg, bv]
    args = [upg, upv, upg, upv, w, w, b, b]
    scratch = [pltpu.VMEM((FFN_HALO + tm, FFN_TC), F32)] * 2
    if n_host:
        anyspec = pl.BlockSpec(memory_space=pl.ANY)
        in_specs += [anyspec] * n_host
        args += list(host[0])
        out_shape = (out_shape,) + tuple(jax.ShapeDtypeStruct(cut.full, s.dtype) for cut, s in zip(host[1], host[0]))
        out_specs = (cur,) + (anyspec,) * n_host
        scratch += [pltpu.SemaphoreType.DMA((4 * n_host,)), pltpu.SemaphoreType.DMA((4 * n_host,))]
    sem = ("arbitrary", "arbitrary") if n_host else ("parallel", "parallel")
    return pl.pallas_call(
        body, name=name, out_shape=out_shape, grid=(nct, nrb), in_specs=in_specs, out_specs=out_specs,
        scratch_shapes=scratch, compiler_params=_params(sem))(*args)


def _ffn_act_bwd(up, dact, w, b, tm, name, comm=None):
    LP, F = up.shape[0], up.shape[1] // 2
    tm = tm // FFN_ROW_SPLIT
    upg = upv = up
    nct = F // FFN_TC
    sub = _sub_rows(tm)
    hpb = tm // FFN_HALO
    nblk = LP // tm
    last_halo = LP // FFN_HALO - 1
    TB = tm + 2 * FFN_HALO
    n_comm = 0 if comm is None else len(comm[0])

    def body(*refs):
        (g_ref, v_ref, gp_ref, vp_ref, gn_ref, vn_ref, da_ref, dan_ref,
         wg_ref, wv_ref, bg_ref, bv_ref) = refs[:12]
        dup_ref, dwg_ref, dwv_ref, dbg_ref, dbv_ref = refs[12 + n_comm:17 + n_comm]
        gbuf, vbuf, dgb, dvb = refs[17 + 2 * n_comm:21 + 2 * n_comm]
        if n_comm:
            start, wait = _chip_exchange_ops(refs[12:12 + n_comm], refs[17 + n_comm:17 + 2 * n_comm],
                                             refs[21 + 2 * n_comm], refs[22 + 2 * n_comm], comm[1])
            pl.when(jnp.logical_and(pl.program_id(0) == 0, pl.program_id(1) == 0))(start)
        r = pl.program_id(1)
        dg_ref = dup_ref.at[0]
        dv_ref = dup_ref.at[1]
        first = r == 0
        last = r == nblk - 1

        @pl.when(first)
        def _():
            dwg_ref[...] = jnp.zeros_like(dwg_ref)
            dwv_ref[...] = jnp.zeros_like(dwv_ref)
            dbg_ref[...] = jnp.zeros_like(dbg_ref)
            dbv_ref[...] = jnp.zeros_like(dbv_ref)

        for buf, c_ref, p_ref, n_ref in ((gbuf, g_ref, gp_ref, gn_ref), (vbuf, v_ref, vp_ref, vn_ref)):
            buf[0:FFN_HALO, :] = jnp.where(first, 0.0, p_ref[...])
            buf[FFN_HALO:FFN_HALO + tm, :] = c_ref[...]
            buf[FFN_HALO + tm:TB, :] = jnp.where(last, 0.0, n_ref[...])

        def dconv(s0, nrows, ln, dact_v):
            xg = [gbuf[s0 - (FFN_K - 1) + kk:s0 - (FFN_K - 1) + kk + nrows, ln] for kk in range(FFN_K)]
            xv = [vbuf[s0 - (FFN_K - 1) + kk:s0 - (FFN_K - 1) + kk + nrows, ln] for kk in range(FFN_K)]
            gc = jnp.broadcast_to(bg_ref[:, ln], (nrows, LANE))
            vc = jnp.broadcast_to(bv_ref[:, ln], (nrows, LANE))
            for kk in range(FFN_K):
                gc = gc + wg_ref[kk:kk + 1, ln] * xg[kk]
                vc = vc + wv_ref[kk:kk + 1, ln] * xv[kk]
            sg = _sigmoid(gc)
            return dact_v * vc * (sg * (1.0 + gc * (1.0 - sg))), dact_v * (gc * sg), xg, xv

        colsum = lambda t: jnp.sum(t, axis=0, keepdims=True)
        for ct in range(FFN_TC // LANE):
            ln = slice(ct * LANE, (ct + 1) * LANE)
            zero = jnp.zeros((1, LANE), F32)
            dwg, dwv, dbg, dbv = [zero] * FFN_K, [zero] * FFN_K, zero, zero
            for s in range(tm // sub):
                dgc, dvc, xg, xv = dconv(FFN_HALO + s * sub, sub, ln, da_ref[s * sub:(s + 1) * sub, ln])
                dgb[s * sub:(s + 1) * sub, ln] = dgc
                dvb[s * sub:(s + 1) * sub, ln] = dvc
                dwg = [dwg[kk] + colsum(dgc * xg[kk]) for kk in range(FFN_K)]
                dwv = [dwv[kk] + colsum(dvc * xv[kk]) for kk in range(FFN_K)]
                dbg, dbv = dbg + colsum(dgc), dbv + colsum(dvc)
            for kk in range(FFN_K):
                dwg_ref[kk:kk + 1, ln] += dwg[kk]
                dwv_ref[kk:kk + 1, ln] += dwv[kk]
            dbg_ref[:, ln] += dbg
            dbv_ref[:, ln] += dbv
            dgc, dvc, _, _ = dconv(FFN_HALO + tm, FFN_HALO, ln, jnp.where(last, 0.0, dan_ref[:, ln]))
            dgb[tm:tm + FFN_HALO, ln] = dgc
            dvb[tm:tm + FFN_HALO, ln] = dvc
            for dbuf, w_ref, dout in ((dgb, wg_ref, dg_ref), (dvb, wv_ref, dv_ref)):
                for s in range(tm // sub):
                    acc = jnp.zeros((sub, LANE), F32)
                    for kk in range(FFN_K):
                        fo = s * sub + (FFN_K - 1) - kk
                        acc = acc + w_ref[kk:kk + 1, ln] * dbuf[fo:fo + sub, ln]
                    dout[s * sub:(s + 1) * sub, ln] = acc.astype(dout.dtype)
        if n_comm:
            pl.when(jnp.logical_and(pl.program_id(0) == nct - 1, pl.program_id(1) == nblk - 1))(wait)

    cur = pl.BlockSpec((tm, FFN_TC), lambda c, r: (r, c))
    prev = pl.BlockSpec((FFN_HALO, FFN_TC), lambda c, r: (jnp.maximum(r * hpb - 1, 0), c))
    nxt = pl.BlockSpec((FFN_HALO, FFN_TC), lambda c, r: (jnp.minimum((r + 1) * hpb, last_halo), c))
    wg = pl.BlockSpec((8, FFN_TC), lambda c, r: (0, c))
    wv = pl.BlockSpec((8, FFN_TC), lambda c, r: (0, nct + c))
    bg = pl.BlockSpec((1, FFN_TC), lambda c, r: (0, c))
    bv = pl.BlockSpec((1, FFN_TC), lambda c, r: (0, nct + c))
    curv = pl.BlockSpec((tm, FFN_TC), lambda c, r: (r, nct + c))
    prevv = pl.BlockSpec((FFN_HALO, FFN_TC), lambda c, r: (jnp.maximum(r * hpb - 1, 0), nct + c))
    nxtv = pl.BlockSpec((FFN_HALO, FFN_TC), lambda c, r: (jnp.minimum((r + 1) * hpb, last_halo), nct + c))
    out_shape = (jax.ShapeDtypeStruct((2, LP, F), BF16),
                 jax.ShapeDtypeStruct((8, F), F32), jax.ShapeDtypeStruct((8, F), F32),
                 jax.ShapeDtypeStruct((1, F), F32), jax.ShapeDtypeStruct((1, F), F32))
    out_specs = (pl.BlockSpec((2, tm, FFN_TC), lambda c, r: (0, r, c)),
                 pl.BlockSpec((8, FFN_TC), lambda c, r: (0, c)),
                 pl.BlockSpec((8, FFN_TC), lambda c, r: (0, c)),
                 pl.BlockSpec((1, FFN_TC), lambda c, r: (0, c)),
                 pl.BlockSpec((1, FFN_TC), lambda c, r: (0, c)))
    in_specs = [cur, curv, prev, prevv, nxt, nxtv, cur, nxt, wg, wv, bg, bv]
    args = [upg, upv, upg, upv, upg, upv, dact, dact, w, w, b, b]
    scratch = [pltpu.VMEM((TB, FFN_TC), F32), pltpu.VMEM((TB, FFN_TC), F32),
               pltpu.VMEM((tm + FFN_HALO, FFN_TC), F32), pltpu.VMEM((tm + FFN_HALO, FFN_TC), F32)]
    if n_comm:
        anyspec = pl.BlockSpec(memory_space=pl.ANY)
        in_specs += [anyspec] * n_comm
        args += list(comm[0])
        out_shape += _chip_exchange_shapes(*comm)
        out_specs += (anyspec,) * n_comm
        scratch += [pltpu.SemaphoreType.DMA((3 * n_comm,)), pltpu.SemaphoreType.DMA((3 * n_comm,))]
    sem = ("arbitrary", "arbitrary") if n_comm else ("parallel", "arbitrary")
    dup, dwg, dwv, dbg, dbv, *others = pl.pallas_call(
        body, name=name, out_shape=out_shape, grid=(nct, nblk), in_specs=in_specs, out_specs=out_specs,
        scratch_shapes=scratch, compiler_params=_params(sem))(*args)
    return (dup, jnp.concatenate([dwg, dwv], axis=1), jnp.concatenate([dbg, dbv], axis=1)) + tuple(others)


POOL_HALO = 16


def _pool_fwd(h, g, pw, pb, ps, tm, name):
    LP, Dm = h.shape
    sub = _sub_rows(tm)
    hpb = tm // POOL_HALO

    def body(h_ref, hh_ref, g_ref, pw_ref, pb_ref, ps_ref, o_ref, d_ref, buf):
        r = pl.program_id(0)
        gg = g_ref[...]

        def norm(x):
            return x * lax.rsqrt(jnp.mean(x * x, axis=1, keepdims=True) + RMS_EPS) * gg

        x = h_ref[...]
        buf[POOL_HALO:POOL_HALO + tm, :] = norm(x)
        buf[0:POOL_HALO, :] = jnp.where(r > 0, norm(hh_ref[...]), 0.0)
        for gi, w in enumerate(POOL_WINDOWS):
            ln = slice(gi * POOL_G, (gi + 1) * POOL_G)
            for s in range(tm // sub):
                base = POOL_HALO + s * sub
                acc = buf[base:base + sub, ln]
                for jj in range(1, w):
                    acc = acc + buf[base - jj:base - jj + sub, ln]
                t = r * tm + s * sub + lax.broadcasted_iota(jnp.int32, (sub, 1), 0)
                cnt = jnp.minimum(t + 1, w).astype(F32)
                d_ref[s * sub:(s + 1) * sub, ln] = (acc / cnt - buf[base:base + sub, ln]).astype(d_ref.dtype)
            y = jnp.dot(d_ref[:, ln], pw_ref[gi], preferred_element_type=F32) + pb_ref[:, ln]
            o_ref[:, ln] = x[:, ln] + y * ps_ref[:, ln]

    row = pl.BlockSpec((tm, Dm), lambda r: (r, 0))
    halo = pl.BlockSpec((POOL_HALO, Dm), lambda r: (jnp.maximum(r * hpb - 1, 0), 0))
    vec = pl.BlockSpec((1, Dm), lambda r: (0, 0))
    wsp = pl.BlockSpec((len(POOL_WINDOWS), POOL_G, POOL_G), lambda r: (0, 0, 0))
    return pl.pallas_call(
        body, name=name,
        out_shape=(jax.ShapeDtypeStruct((LP, Dm), F32), jax.ShapeDtypeStruct((LP, Dm), BF16)),
        grid=(LP // tm,), in_specs=[row, halo, vec, wsp, vec, vec], out_specs=(row, row),
        scratch_shapes=[pltpu.VMEM((POOL_HALO + tm, Dm), F32)],
        compiler_params=_params(("parallel",)))(h, h, g, pw, pb, ps)


def _pool_bwd(h, g, d, pw, pb, ps, dh_out, tm, name):
    LP, Dm = h.shape
    sub = _sub_rows(tm)
    hpb = tm // POOL_HALO
    nblk = LP // tm
    last_halo = LP // POOL_HALO - 1
    nt = (((1,), (1,)), ((), ()))
    tn = (((0,), (0,)), ((), ()))

    def body(h_ref, g_ref, d_ref, pw_ref, pb_ref, ps_ref, do_ref, don_ref,
             dh_ref, dpw_ref, dpb_ref, dps_ref, dg_ref, ebuf, ddb, dnb):
        r = pl.program_id(0)

        @pl.when(r == 0)
        def _():
            dpw_ref[...] = jnp.zeros_like(dpw_ref)
            dpb_ref[...] = jnp.zeros_like(dpb_ref)
            dps_ref[...] = jnp.zeros_like(dps_ref)
            dg_ref[...] = jnp.zeros_like(dg_ref)

        for gi, w in enumerate(POOL_WINDOWS):
            ln = slice(gi * POOL_G, (gi + 1) * POOL_G)
            wg = pw_ref[gi]
            dog = do_ref[:, ln]
            dg_b = d_ref[:, ln]
            y_pre = jnp.dot(dg_b, wg, preferred_element_type=F32) + pb_ref[:, ln]
            dps_ref[:, ln] += jnp.sum(dog * y_pre, axis=0, keepdims=True)
            dy = dog * ps_ref[:, ln]
            dpb_ref[:, ln] += jnp.sum(dy, axis=0, keepdims=True)
            dyb = dy.astype(BF16)
            dpw_ref[gi] += lax.dot_general(dg_b, dyb, tn, preferred_element_type=F32)
            dd = lax.dot_general(dyb, wg, nt, preferred_element_type=F32)
            ddb[:, ln] = dd
            t = r * tm + lax.broadcasted_iota(jnp.int32, (tm, 1), 0)
            ebuf[0:tm, ln] = dd / jnp.minimum(t + 1, w).astype(F32)
            dyn = (don_ref[:, ln] * ps_ref[:, ln]).astype(BF16)
            ddn = lax.dot_general(dyn, wg, nt, preferred_element_type=F32)
            tn_ = (r + 1) * tm + lax.broadcasted_iota(jnp.int32, (POOL_HALO, 1), 0)
            ebuf[tm:tm + POOL_HALO, ln] = jnp.where(r < nblk - 1, ddn / jnp.minimum(tn_ + 1, w).astype(F32), 0.0)
            for s in range(tm // sub):
                acc = ebuf[s * sub:(s + 1) * sub, ln]
                for jj in range(1, w):
                    acc = acc + ebuf[s * sub + jj:s * sub + jj + sub, ln]
                dnb[s * sub:(s + 1) * sub, ln] = acc - ddb[s * sub:(s + 1) * sub, ln]
        x = h_ref[...]
        rr = lax.rsqrt(jnp.mean(x * x, axis=1, keepdims=True) + RMS_EPS)
        xhat = x * rr
        dn = dnb[...]
        dxh = dn * g_ref[...]
        dh_ref[...] = do_ref[...] + rr * (dxh - xhat * jnp.mean(dxh * xhat, axis=1, keepdims=True))
        dg_ref[...] += jnp.sum(dn * xhat, axis=0, keepdims=True)

    row = pl.BlockSpec((tm, Dm), lambda r: (r, 0))
    nxt = pl.BlockSpec((POOL_HALO, Dm), lambda r: (jnp.minimum((r + 1) * hpb, last_halo), 0))
    vec = pl.BlockSpec((1, Dm), lambda r: (0, 0))
    wsp = pl.BlockSpec((len(POOL_WINDOWS), POOL_G, POOL_G), lambda r: (0, 0, 0))
    return pl.pallas_call(
        body, name=name,
        out_shape=(jax.ShapeDtypeStruct((LP, Dm), F32),
                   jax.ShapeDtypeStruct((len(POOL_WINDOWS), POOL_G, POOL_G), F32),
                   jax.ShapeDtypeStruct((1, Dm), F32), jax.ShapeDtypeStruct((1, Dm), F32),
                   jax.ShapeDtypeStruct((1, Dm), F32)),
        grid=(nblk,), in_specs=[row, vec, row, wsp, vec, vec, row, nxt],
        out_specs=(row, wsp, vec, vec, vec),
        scratch_shapes=[pltpu.VMEM((tm + POOL_HALO, Dm), F32), pltpu.VMEM((tm, Dm), F32),
                        pltpu.VMEM((tm, Dm), F32)],
        compiler_params=_params(("arbitrary",)))(h, g, d, pw, pb, ps, dh_out, dh_out)


def _adamw(w, g, m, v, name):
    shape = w.shape
    cols = shape[-1]
    rows = int(np.prod(shape[:-1])) if len(shape) > 1 else 1
    w2, g2, m2, v2 = (t.reshape(rows, cols) for t in (w, g, m, v))
    tr = rows
    for cand in (256, 128, 64, 32, 16, 8):
        if rows % cand == 0 and rows > cand:
            tr = cand
            break
    c1 = float(1.0 - ADAM_B1 ** ADAM_STEP)
    c2 = float(1.0 - ADAM_B2 ** ADAM_STEP)

    def body(w_ref, g_ref, m_ref, v_ref, d_ref, mo_ref, vo_ref):
        gg = g_ref[...]
        mn = ADAM_B1 * m_ref[...] + (1.0 - ADAM_B1) * gg
        vn = ADAM_B2 * v_ref[...] + (1.0 - ADAM_B2) * (gg * gg)
        m_hat = mn / c1
        v_hat = vn / c2
        d_ref[...] = -ADAM_LR * (m_hat / (jnp.sqrt(v_hat) + ADAM_EPS) + ADAM_WD * w_ref[...])
        mo_ref[...] = mn
        vo_ref[...] = vn

    spec = pl.BlockSpec((tr, cols), lambda i: (i, 0))
    sds = jax.ShapeDtypeStruct((rows, cols), F32)
    d2, mo, vo = pl.pallas_call(
        body, name=name, out_shape=(sds, sds, sds), grid=(rows // tr,),
        in_specs=[spec] * 4, out_specs=(spec,) * 3,
        compiler_params=_params(("parallel",)))(w2, g2, m2, v2)
    return d2.reshape(shape), mo.reshape(shape), vo.reshape(shape)


def _row_tiles(LP):
    tm = LP // 4
    assert LP % 4 == 0 and tm % CONV_HALO == 0 and LP % ATT_BLK == 0, LP
    return tm, LP // 2


def _heads(t, LP):
    return t.reshape(LP, HEADS, HEAD_DIM).transpose(1, 0, 2)


def _unheads(t, LP):
    return t.transpose(1, 0, 2).reshape(LP, FOX_W)


def _ffn_fwd(h, gain, wug, wuv, cw, cb, wd, tm, tmm, tag):
    n = _rms_fwd(h, gain, BF16, tm, f"ffn_norm_{tag}")
    upg = _mm(n, wug, "nn", F32, tmm, 256, f"ffn_up_gate_{tag}")
    upv = _mm(n, wuv, "nn", F32, tmm, 256, f"ffn_up_val_{tag}")
    act = _ffn_act_fwd(upg, upv, cw, cb, tm, f"ffn_act_{tag}")
    out = _mm(act, wd, "nn", F32, tmm, 512, f"ffn_down_{tag}", add=h)
    return out, (n, upg, upv, act)


def _ffn_bwd(h, gain, wug, wuv, cw, cb, wd, saved, dout, tm, tmm, tag):
    n, upg, upv, act = saved
    dact = _mm(dout, wd, "nt", F32, tmm, 256, f"ffn_dact_{tag}")
    dwd = _mm(act, dout, "tn", F32, 256, 512, f"ffn_dwdown_{tag}")
    dupg, dupv, dcw, dcb = _ffn_act_bwd(upg, upv, dact, cw, cb, tm, f"ffn_act_bwd_{tag}")
    dn = _mm(dupg, wug, "nt", F32, tm, 512, f"ffn_dn_gate_{tag}")
    dn = _mm(dupv, wuv, "nt", F32, tm, 512, f"ffn_dn_val_{tag}", add=dn)
    dwug = _mm(n, dupg, "tn", F32, 512, 256, f"ffn_dwup_gate_{tag}")
    dwuv = _mm(n, dupv, "tn", F32, 512, 256, f"ffn_dwup_val_{tag}")
    dh, dgain = _rms_bwd(h, gain, dn, dout, tm, f"ffn_norm_bwd_{tag}")
    return dh, dict(gain=dgain, wug=dwug, wuv=dwuv, cw=dcw[:FFN_K], cb=dcb, wd=dwd)


def _local_step(h0, tgt, W, n_real):
    LP = h0.shape[0]
    tm, tmm = _row_tiles(LP)
    nb = LP // ATT_BLK
    G = {}

    n0 = _rms_fwd(h0, W["mix_norm_even"], BF16, tm, "mix_norm_even")
    proj = _mm(n0, W["w_in_p"], "nn", F32, tmm, 384, "in_proj")
    c = _fgate_fwd(proj, W["b_f_p"], "forget_gate")
    cT = c[:, :HEADS].T
    c_col = cT[:, :, None]
    c_row = cT.reshape(HEADS, nb, 1, ATT_BLK)
    qkv = proj[:, :3 * FOX_W].astype(BF16)
    q, k, v = (_heads(qkv[:, i * FOX_W:(i + 1) * FOX_W], LP) for i in range(3))
    o, lse = _attn_fwd(q, k, v, c_col, c_row, "fox_attention")
    u1, u = _conf_fwd(proj, W["conv_w_p"], W["conv_b"], W["ln_g"], W["ln_b"], tm, "conformer")
    cat = jnp.concatenate([_unheads(o, LP).astype(BF16), u], axis=1)
    h1 = _mm(cat, W["w_out"], "nn", F32, tmm, 512, "out_proj", add=h0)
    h2, ffn0 = _ffn_fwd(h1, W["ffn_norm"][0:1], W["w_up_g"][0], W["w_up_v"][0], W["ffn_conv_w_p"][0],
                        W["ffn_conv_b"][0:1], W["w_down"][0], tm, tmm, "0")
    h3, dpool = _pool_fwd(h2, W["mix_norm_odd"], W["pool_w"], W["pool_b"], W["pool_scale"], tm, "pool_mixer")
    h4, ffn1 = _ffn_fwd(h3, W["ffn_norm"][1:2], W["w_up_g"][1], W["w_up_v"][1], W["ffn_conv_w_p"][1],
                        W["ffn_conv_b"][1:2], W["w_down"][1], tm, tmm, "1")
    loss, dh4, G["final_norm"] = _loss_head(h4, W["final_norm"], tgt, n_real, tm, "loss_head")

    dh3, g1 = _ffn_bwd(h3, W["ffn_norm"][1:2], W["w_up_g"][1], W["w_up_v"][1], W["ffn_conv_w_p"][1],
                       W["ffn_conv_b"][1:2], W["w_down"][1], ffn1, dh4, tm, tmm, "1")
    dh2, G["pool_w"], G["pool_b"], G["pool_scale"], G["mix_norm_odd"] = _pool_bwd(
        h2, W["mix_norm_odd"], dpool, W["pool_w"], W["pool_b"], W["pool_scale"], dh3, tm, "pool_mixer_bwd")
    dh1, g0 = _ffn_bwd(h1, W["ffn_norm"][0:1], W["w_up_g"][0], W["w_up_v"][0], W["ffn_conv_w_p"][0],
                       W["ffn_conv_b"][0:1], W["w_down"][0], ffn0, dh2, tm, tmm, "0")
    for key in ("gain", "wug", "wuv", "cw", "cb", "wd"):
        G["ffn_" + key] = (g0[key], g1[key])

    dcat = _mm(dh1, W["w_out"], "nt", F32, tmm, 512, "out_proj_dx")
    G["w_out"] = _mm(cat, dh1, "tn", F32, 512, 512, "out_proj_dw")
    dadg, dcw, G["conv_b"], G["ln_g"], G["ln_b"] = _conf_bwd(
        proj, u1, dcat, W["conv_w_p"], W["ln_g"], W["ln_b"], tm, "conformer_bwd")
    G["conv_w"] = dcw[:CONV_K]
    do = _heads(dcat[:, :FOX_W], LP)
    dq, dk, dv, dcq, dck = _attn_bwd(q, k, v, o, do, lse, c_col, c_row, "fox_attention_bwd")
    dc = jnp.pad((dcq.reshape(HEADS, LP) + dck.reshape(HEADS, LP)).T, ((0, 0), (0, LANE - HEADS)))
    dfl, dbf = _fgate_bwd(proj, W["b_f_p"], dc, "forget_gate_bwd")
    G["b_f"] = dbf[:, :HEADS]
    dproj = jnp.concatenate([_unheads(t, LP).astype(BF16) for t in (dq, dk, dv)] + [dadg, dfl], axis=1)
    dn0 = _mm(dproj, W["w_in_p"], "nt", F32, tmm, 512, "in_proj_dx")
    G["w_in_p"] = _mm(n0, dproj, "tn", F32, 512, 384, "in_proj_dw")
    dh0, G["mix_norm_even"] = _rms_bwd(h0, W["mix_norm_even"], dn0, dh1, tm, "mix_norm_even_bwd")
    return loss, dh0, G


def _compute_layout(P):
    w_in = P["w_in"].reshape(D_MODEL, IN_COLS)
    qkv, f, ag = w_in[:, :3 * FOX_W], w_in[:, 3 * FOX_W:3 * FOX_W + HEADS], w_in[:, 3 * FOX_W + HEADS:]
    w_in_p = jnp.concatenate([qkv, ag, f, jnp.zeros((D_MODEL, LANE - HEADS), w_in.dtype)], axis=1).astype(BF16)
    w_up = P["w_up"].astype(BF16)
    return dict(
        mix_norm_even=P["mix_norm_even"].reshape(1, D_MODEL).astype(F32),
        w_in_p=w_in_p,
        b_f_p=jnp.pad(P["b_f"].reshape(1, HEADS).astype(F32), ((0, 0), (0, LANE - HEADS))),
        conv_w_p=jnp.pad(P["conv_w"].reshape(CONV_K, CONV_CH).astype(F32), ((0, CONV_HALO - CONV_K), (0, 0))),
        conv_b=P["conv_b"].reshape(1, CONV_CH).astype(F32),
        ln_g=P["ln_g"].reshape(1, CONV_CH).astype(F32),
        ln_b=P["ln_b"].reshape(1, CONV_CH).astype(F32),
        w_out=P["w_out"].reshape(D_MODEL, D_MODEL).astype(BF16),
        mix_norm_odd=P["mix_norm_odd"].reshape(1, D_MODEL).astype(F32),
        pool_w=P["pool_w"].reshape(len(POOL_WINDOWS), POOL_G, POOL_G).astype(BF16),
        pool_b=P["pool_b"].reshape(1, D_MODEL).astype(F32),
        pool_scale=P["pool_scale"].reshape(1, D_MODEL).astype(F32),
        ffn_norm=P["ffn_norm"].astype(F32),
        w_up_g=w_up[:, :, :D_FF],
        w_up_v=w_up[:, :, D_FF:],
        ffn_conv_w_p=jnp.pad(P["ffn_conv_w"].astype(F32), ((0, 0), (0, 8 - FFN_K), (0, 0))),
        ffn_conv_b=P["ffn_conv_b"].astype(F32),
        w_down=P["w_down"].astype(BF16),
        final_norm=P["final_norm"].reshape(1, D_MODEL).astype(F32),
    )


def _reference_layout(G, dh0):
    gp = G["w_in_p"]
    g_w_in = jnp.concatenate([gp[:, :3 * FOX_W], gp[:, 3 * FOX_W + 2 * CONV_CH:3 * FOX_W + 2 * CONV_CH + HEADS],
                              gp[:, 3 * FOX_W:3 * FOX_W + 2 * CONV_CH]], axis=1)
    return dict(
        meta_tokens=dh0[:N_META],
        mix_norm_even=G["mix_norm_even"],
        w_in=g_w_in[None],
        b_f=G["b_f"],
        conv_w=G["conv_w"][None],
        conv_b=G["conv_b"],
        ln_g=G["ln_g"],
        ln_b=G["ln_b"],
        w_out=G["w_out"][None],
        mix_norm_odd=G["mix_norm_odd"],
        pool_w=G["pool_w"][None],
        pool_b=G["pool_b"].reshape(1, len(POOL_WINDOWS), POOL_G),
        pool_scale=G["pool_scale"],
        ffn_norm=jnp.concatenate(G["ffn_gain"], axis=0),
        w_up=jnp.stack([jnp.concatenate([g, v], axis=1) for g, v in zip(G["ffn_wug"], G["ffn_wuv"])]),
        ffn_conv_w=jnp.stack(G["ffn_cw"]),
        ffn_conv_b=jnp.concatenate(G["ffn_cb"], axis=0),
        w_down=jnp.stack(G["ffn_wd"]),
        final_norm=G["final_norm"].reshape(D_MODEL),
    )


MESH = pl.DeviceIdType.MESH
ANY = pl.BlockSpec(memory_space=pl.ANY)
PACK_COLS = 1024


def _coords():
    return lax.axis_index("x"), lax.axis_index("y"), lax.axis_index("c")


def _other_chips(x, y):
    return [(1 - x, y), (x, 1 - y), (1 - x, 1 - y)]


def _allgather_chips(pack):
    R, C = pack.shape
    R2 = R // 2

    def body(x_ref, o_ref, send_sems, recv_sems, local_sem):
        x, y, c = _coords()
        sibling = (x, y, 1 - c)
        chips = _other_chips(x, y)

        def slot(px, py, half):
            return o_ref.at[2 * px + py, pl.ds(half * R2, R2), :]

        def copy(k, src, dst, to):
            return pltpu.make_async_remote_copy(src_ref=src, dst_ref=dst, send_sem=send_sems.at[k],
                                                recv_sem=recv_sems.at[k], device_id=to, device_id_type=MESH)

        mine = pltpu.make_async_copy(x_ref, o_ref.at[2 * x + y], local_sem)
        mine.start()
        my_half = x_ref.at[pl.ds(c * R2, R2), :]
        first = [copy(j, my_half, slot(x, y, c), (*chip, c)) for j, chip in enumerate(chips)]
        for cp in first:
            cp.start()
        passed = [copy(3 + j, slot(*chip, c), slot(*chip, c), sibling) for j, chip in enumerate(chips)]
        for j, chip in enumerate(chips):
            copy(j, my_half, slot(*chip, c), sibling).wait_recv()
            passed[j].start()
        for j, chip in enumerate(chips):
            copy(3 + j, my_half, slot(*chip, 1 - c), sibling).wait_recv()
        for cp in first + passed:
            cp.wait_send()
        mine.wait()

    return pl.pallas_call(
        body, name="allgather_weights", out_shape=jax.ShapeDtypeStruct((N_CHIPS, R, C), pack.dtype),
        in_specs=[ANY], out_specs=ANY,
        scratch_shapes=[pltpu.SemaphoreType.DMA((6,)), pltpu.SemaphoreType.DMA((6,)), pltpu.SemaphoreType.DMA],
    )(pack)


def _pair_exchange(G):
    n, R, C = G.shape
    R2 = R // 2

    def body(g_ref, o_ref, send_sem, recv_sem):
        x, y, c = _coords()
        src = g_ref.at[pl.ds(0, n), pl.ds((1 - c) * R2, R2), :]
        cp = pltpu.make_async_remote_copy(src_ref=src, dst_ref=o_ref, send_sem=send_sem, recv_sem=recv_sem,
                                          device_id=(x, y, 1 - c), device_id_type=MESH)
        cp.start()
        cp.wait()

    return pl.pallas_call(
        body, name="grad_pair_exchange", out_shape=jax.ShapeDtypeStruct((n, R2, C), G.dtype),
        in_specs=[ANY], out_specs=ANY,
        scratch_shapes=[pltpu.SemaphoreType.DMA, pltpu.SemaphoreType.DMA],
    )(G)


def _row_tile(rows, align, cap):
    best = None
    for t in range(align, min(rows, cap) + 1, align):
        if rows % t == 0:
            best = t
    assert best is not None, (rows, align, cap)
    return best


def _pair_sum(G, recv):
    n, R, C = G.shape
    R2 = R // 2
    tr = _row_tile(R2, 16, 704)
    nrb = R2 // tr
    half = lax.axis_index("c").astype(jnp.int32).reshape(1)

    def body(c_ref, g_ref, r_ref, o_ref):
        o_ref[...] = (g_ref[...] + r_ref[...]).astype(o_ref.dtype)

    return pl.pallas_call(
        body, name="grad_pair_sum", out_shape=jax.ShapeDtypeStruct((n, R2, C), BF16),
        grid_spec=pltpu.PrefetchScalarGridSpec(
            num_scalar_prefetch=1, grid=(n, nrb),
            in_specs=[pl.BlockSpec((None, tr, C), lambda j, i, c_ref: (j, c_ref[0] * nrb + i, 0)),
                      pl.BlockSpec((None, tr, C), lambda j, i, c_ref: (j, i, 0))],
            out_specs=pl.BlockSpec((None, tr, C), lambda j, i, c_ref: (j, i, 0))),
        compiler_params=_params(("parallel", "parallel")))(half, G, recv)


def _chip_exchange(P):
    n, R2, C = P.shape

    def body(p_ref, o_ref, send_sems, recv_sems, local_sem):
        x, y, c = _coords()
        me = 2 * x + y
        chips = _other_chips(x, y)
        mine = pltpu.make_async_copy(p_ref.at[me], o_ref.at[me], local_sem)
        mine.start()
        sends = [pltpu.make_async_remote_copy(
            src_ref=p_ref.at[2 * px + py], dst_ref=o_ref.at[me], send_sem=send_sems.at[k],
            recv_sem=recv_sems.at[k], device_id=(px, py, c), device_id_type=MESH)
            for k, (px, py) in enumerate(chips)]
        for cp in sends:
            cp.start()
        for k, (px, py) in enumerate(chips):
            pltpu.make_async_remote_copy(
                src_ref=p_ref.at[me], dst_ref=o_ref.at[2 * px + py], send_sem=send_sems.at[k],
                recv_sem=recv_sems.at[k], device_id=(px, py, c), device_id_type=MESH).wait_recv()
        for cp in sends:
            cp.wait_send()
        mine.wait()

    return pl.pallas_call(
        body, name="grad_chip_exchange", out_shape=jax.ShapeDtypeStruct((n, R2, C), P.dtype),
        in_specs=[ANY], out_specs=ANY,
        scratch_shapes=[pltpu.SemaphoreType.DMA((3,)), pltpu.SemaphoreType.DMA((3,)), pltpu.SemaphoreType.DMA],
    )(P)


def _chip_sum(X):
    n, R2, C = X.shape
    tr = _row_tile(R2, 16, 704)

    def body(x_ref, o_ref):
        acc = x_ref[0].astype(F32)
        for s in range(1, n):
            acc = acc + x_ref[s].astype(F32)
        o_ref[...] = acc

    return pl.pallas_call(
        body, name="grad_chip_sum", out_shape=jax.ShapeDtypeStruct((R2, C), F32), grid=(R2 // tr,),
        in_specs=[pl.BlockSpec((n, tr, C), lambda i: (0, i, 0))],
        out_specs=pl.BlockSpec((tr, C), lambda i: (i, 0)),
        compiler_params=_params(("parallel",)))(X)


def _pair_allgather(Q):
    R2, C = Q.shape

    def body(q_ref, o_ref, send_sem, recv_sem, local_sem):
        x, y, c = _coords()
        mine = pltpu.make_async_copy(q_ref, o_ref.at[c], local_sem)
        mine.start()
        cp = pltpu.make_async_remote_copy(src_ref=q_ref, dst_ref=o_ref.at[c], send_sem=send_sem,
                                          recv_sem=recv_sem, device_id=(x, y, 1 - c), device_id_type=MESH)
        cp.start()
        pltpu.make_async_remote_copy(src_ref=q_ref, dst_ref=o_ref.at[1 - c], send_sem=send_sem,
                                     recv_sem=recv_sem, device_id=(x, y, 1 - c), device_id_type=MESH).wait_recv()
        cp.wait_send()
        mine.wait()

    return pl.pallas_call(
        body, name="grad_pair_allgather", out_shape=jax.ShapeDtypeStruct((2, R2, C), Q.dtype),
        in_specs=[ANY], out_specs=ANY,
        scratch_shapes=[pltpu.SemaphoreType.DMA, pltpu.SemaphoreType.DMA, pltpu.SemaphoreType.DMA],
    )(Q)


def _allreduce_small(pack):
    Rs, C = pack.shape
    n_dev = 8

    def body(x_ref, o_ref, buf, send_sems, recv_sems):
        x, y, c = _coords()
        me = 4 * x + 2 * y + c
        buf[me] = x_ref[...]
        peers = []
        for rel in range(1, n_dev):
            px = 1 - x if rel & 4 else x
            py = 1 - y if rel & 2 else y
            pc = 1 - c if rel & 1 else c
            peers.append((px, py, pc))
        sends = [pltpu.make_async_remote_copy(
            src_ref=x_ref, dst_ref=buf.at[me], send_sem=send_sems.at[k], recv_sem=recv_sems.at[k],
            device_id=peer, device_id_type=MESH) for k, peer in enumerate(peers)]
        for cp in sends:
            cp.start()
        for k, (px, py, pc) in enumerate(peers):
            pltpu.make_async_remote_copy(
                src_ref=x_ref, dst_ref=buf.at[4 * px + 2 * py + pc], send_sem=send_sems.at[k],
                recv_sem=recv_sems.at[k], device_id=(px, py, pc), device_id_type=MESH).wait_recv()
        for cp in sends:
            cp.wait_send()
        acc = buf[0]
        for d in range(1, n_dev):
            acc = acc + buf[d]
        o_ref[...] = acc

    vm = pl.BlockSpec(memory_space=pltpu.VMEM)
    return pl.pallas_call(
        body, name="allreduce_replicated", out_shape=jax.ShapeDtypeStruct((Rs, C), F32),
        in_specs=[vm], out_specs=vm,
        scratch_shapes=[pltpu.VMEM((n_dev, Rs, C), F32), pltpu.SemaphoreType.DMA((n_dev - 1,)),
                        pltpu.SemaphoreType.DMA((n_dev - 1,))],
    )(pack)


SHARDED = (
    ("w_in", 2, True), ("w_out", 1, True), ("pool_w", 2, True), ("w_up", 2, True), ("w_down", 1, True),
    ("meta_tokens", 1, False), ("mix_norm_odd", 1, False), ("pool_b", 2, False), ("pool_scale", 1, False),
    ("conv_w", 2, False), ("ffn_conv_w", 2, False))
REPLICATED = ("mix_norm_even", "b_f", "conv_b", "ln_g", "ln_b", "ffn_norm", "ffn_conv_b", "final_norm")
PACK_ROW_ALIGN = 32


def _pad_rows(flat, align_rows, cols):
    rows = -(-flat.shape[-1] // cols)
    rows = -(-rows // align_rows) * align_rows
    pad = rows * cols - flat.shape[-1]
    flat = jnp.pad(flat, [(0, 0)] * (flat.ndim - 1) + [(0, pad)])
    return flat.reshape(flat.shape[:-1] + (rows, cols))


def _pack_weight_shards(shards):
    parts = []
    for name, _, as_bf16 in SHARDED:
        w = shards[name].astype(F32).reshape(-1)
        parts.append(w.astype(BF16) if as_bf16 else lax.bitcast_convert_type(w, BF16).reshape(-1))
    return _pad_rows(jnp.concatenate(parts), PACK_ROW_ALIGN, PACK_COLS)


def _unpack_weights(gathered, shards):
    flat = gathered.reshape(N_CHIPS, -1)
    out, off = {}, 0
    for name, axis, as_bf16 in SHARDED:
        shp = shards[name].shape
        n = int(np.prod(shp))
        if as_bf16:
            t = flat[:, off:off + n]
            off += n
        else:
            t = lax.bitcast_convert_type(flat[:, off:off + 2 * n].reshape(N_CHIPS, n, 2), F32)
            off += 2 * n
        t = t.reshape((N_CHIPS,) + shp)
        out[name] = jnp.concatenate([t[j] for j in range(N_CHIPS)], axis=axis)
    return out


def _pack_grad_shards(grads, shards):
    parts = []
    for name, axis, _ in SHARDED:
        g = grads[name].reshape(shards[name].shape[:axis] + (N_CHIPS, shards[name].shape[axis])
                                + shards[name].shape[axis + 1:])
        parts.append(jnp.moveaxis(g, axis, 0).reshape(N_CHIPS, -1))
    return _pad_rows(jnp.concatenate(parts, axis=1), PACK_ROW_ALIGN, PACK_COLS)


def _unpack_grad_shard(reduced, shards):
    flat = reduced.reshape(-1)
    out, off = {}, 0
    for name, _, _ in SHARDED:
        shp = shards[name].shape
        n = int(np.prod(shp))
        out[name] = flat[off:off + n].reshape(shp)
        off += n
    return out


def _pack_replicated(grads, loss):
    parts = [_pad_rows(grads[name].astype(F32).reshape(-1), 1, LANE).reshape(-1) for name in REPLICATED]
    parts.append(_pad_rows(loss.reshape(-1)[:1], 1, LANE).reshape(-1))
    return _pad_rows(jnp.concatenate(parts), 8, LANE)


def _unpack_replicated(reduced, shapes):
    flat = reduced.reshape(-1)
    out, off = {}, 0
    for name in REPLICATED:
        n = int(np.prod(shapes[name]))
        out[name] = flat[off:off + n].reshape(shapes[name])
        off += -(-n // LANE) * LANE
    return out, flat[off]


def _ffn_fwd2(h, W, layer, tm, tmm, host_up=None, host_act=None):
    tag = str(layer)
    n = _rms_fwd(h, W["ffn_norm"][layer:layer + 1], BF16, tm, f"ffn_norm_{tag}")
    up, *g_up = _mm(n, W["w_up"][layer], "nn", F32, tmm, UP_SHARD, f"ffn_up_{tag}", host=host_up) \
        if host_up else (_mm(n, W["w_up"][layer], "nn", F32, tmm, UP_SHARD, f"ffn_up_{tag}"),)
    act, *g_act = _ffn_act_fwd(up, W["ffn_conv_w_p"][layer], W["ffn_conv_b"][layer:layer + 1], tm,
                               f"ffn_act_{tag}", host=host_act) \
        if host_act else (_ffn_act_fwd(up, W["ffn_conv_w_p"][layer], W["ffn_conv_b"][layer:layer + 1], tm,
                                       f"ffn_act_{tag}"),)
    out = _mm(act, W["w_down"][layer], "nn", F32, tm, D_MODEL, f"ffn_down_{tag}", add=h)
    return out, (n, up, act), g_up + g_act


def _ffn_bwd2(h, W, layer, saved, dout, tm, tmm, reduce=None):
    tag = str(layer)
    n, up, act = saved
    parts, comm = [], None
    if reduce is None:
        dact = _mm(dout, W["w_down"][layer], "nt", F32, tmm, UP_SHARD, f"ffn_dact_{tag}")
    else:
        names, fulls, cuts = reduce
        dact, *recv = _mm(dout, W["w_down"][layer], "nt", F32, tmm, UP_SHARD, f"ffn_dact_{tag}",
                          host=(fulls, cuts, "pairx"))
        parts = [_pair_sum2(f, r, cut, PAIR_SUM_BLOCKS[nm], "grad_pair_sum_" + nm)
                 for f, r, cut, nm in zip(fulls, recv, cuts, names)]
        comm = (parts, cuts)
    dwd = _mm(act, dout, "tn", F32, D_FF // 2, 512, f"ffn_dwdown_{tag}")
    dup, dcw, dcb, *others = _ffn_act_bwd(up, dact, W["ffn_conv_w_p"][layer], W["ffn_conv_b"][layer:layer + 1],
                                          tm, f"ffn_act_bwd_{tag}", comm=comm)
    dn = _mm_ffn_dn(dup, W["w_up"][layer], tm, D_MODEL, f"ffn_dn_{tag}")
    dwu = _mm_ffn_dwup(n, dup, 512, D_FF // 2, f"ffn_dwup_{tag}")
    dh, dgain = _rms_bwd(h, W["ffn_norm"][layer:layer + 1], dn, dout, tm, f"ffn_norm_bwd_{tag}")
    return dh, (dwu, dwd), dict(gain=dgain, cw=dcw[:FFN_K], cb=dcb), parts, others


GATHER_FIRST = ("w_in", "small")
GATHER_LATE = ("pool_w", "w_up", "w_down")
HOSTED_FFN = ("w_up1", "w_down1")
HOSTED = ("w_out", "pool_w", "w_up0", "w_down0")
LATE = ("small",)


def _local_step2(h0, tgt, W, n_real, cut_of):
    LP = h0.shape[0]
    tm, tmm = _row_tiles(LP)
    nb = LP // ATT_BLK
    G = {}
    n0 = _rms_fwd(h0, W["mix_norm_even"], BF16, tm, "mix_norm_even")
    sh = W["late_shards"]
    stage = lambda *names: ([sh[n] for n in names], [cut_of[n] for n in names])
    proj, g_down0 = _mm(n0, W["w_in_p"], "nn", F32, tmm, 896, "in_proj", host=stage("w_down0"))
    c = _fgate_fwd(proj, W["b_f_p"], "forget_gate")
    qT, kT, k_aug, vT = _attn_prep(proj, c, "attention_operands")
    oT, lse, g_pool, g_up0, g_out = _attn_fwd2(qT, k_aug, vT, "fox_attention",
                                               comm=stage("pool_w", "w_up0", "w_out"))
    g_down0, g_pool, g_up0, g_out = _gather_forward(
        [g_down0, g_pool, g_up0, g_out], stage("w_down0", "pool_w", "w_up0", "w_out")[1], "gather_forward_0")
    W = dict(W)
    W.update(pool_w=g_pool, w_up=[g_up0, None], w_down=[g_down0, None], w_out=g_out)
    u1, u = _conf_fwd(proj, W["conv_w_p"], W["conv_b"], W["ln_g"], W["ln_b"], tm, "conformer")
    cat = jnp.concatenate([_attn_rows(oT, 1.0, BF16, "attention_rows"), u], axis=1)
    h1 = _mm(cat, W["w_out"], "nn", F32, tmm, D_MODEL, "out_proj", add=h0)
    h2, ffn0, (g_down1, g_up1) = _ffn_fwd2(h1, W, 0, tm, tmm, host_up=stage("w_down1"), host_act=stage("w_up1"))
    g_down1, g_up1 = _gather_forward([g_down1, g_up1], stage("w_down1", "w_up1")[1], "gather_forward_1")
    W.update(w_up=[g_up0, g_up1], w_down=[g_down0, g_down1])
    h3, dpool = _pool_fwd(h2, W["mix_norm_odd"], W["pool_w"], W["pool_b"], W["pool_scale"], tm, "pool_mixer")
    h4, ffn1, _ = _ffn_fwd2(h3, W, 1, tm, tmm)
    loss, dh4, G["final_norm"] = _loss_head(h4, W["final_norm"], tgt, n_real, tm, "loss_head")

    dh3, (G["w_up1"], G["w_down1"]), g1, _, _ = _ffn_bwd2(h3, W, 1, ffn1, dh4, tm, tmm)
    dh2, G["pool_w"], G["pool_b"], G["pool_scale"], G["mix_norm_odd"] = _pool_bwd(
        h2, W["mix_norm_odd"], dpool, W["pool_w"], W["pool_b"], W["pool_scale"], dh3, tm, "pool_mixer_bwd")
    cuts1 = [cut_of[n] for n in HOSTED_FFN]
    dh1, (G["w_up0"], G["w_down0"]), g0, parts1, others1 = _ffn_bwd2(
        h1, W, 0, ffn0, dh2, tm, tmm, reduce=(HOSTED_FFN, [G[n] for n in HOSTED_FFN], cuts1))
    G["ffn_norm"] = jnp.concatenate([g0["gain"], g1["gain"]], axis=0)
    G["ffn_conv_w"] = jnp.stack([g0["cw"], g1["cw"]])
    G["ffn_conv_b"] = jnp.concatenate([g0["cb"], g1["cb"]], axis=0)

    dcat = _mm(dh1, W["w_out"], "nt", F32, tmm, D_MODEL, "out_proj_dx")
    G["w_out"] = _mm(cat, dh1, "tn", F32, 512, D_MODEL, "out_proj_dw")
    hcuts = [cut_of[n] for n in HOSTED]
    hfull = [G[n] for n in HOSTED]
    dadg, dcw, G["conv_b"], G["ln_g"], G["ln_b"], *hrecv = _conf_bwd(
        proj, u1, dcat, W["conv_w_p"], W["ln_g"], W["ln_b"], tm, "conformer_bwd", host=(hfull, hcuts, "pairx"))
    G["conv_w"] = dcw[:CONV_K]
    doT = _attn_cols(dcat, "attention_do_cols")
    hparts = [_pair_sum2(f, r, cut, PAIR_SUM_BLOCKS[n], "grad_pair_sum_" + n)
              for f, r, cut, n in zip(hfull, hrecv, hcuts, HOSTED)]
    dqT, dkT, dvT, *hothers = _attn_bwd2(qT, kT, k_aug, vT, oT, doT, lse, "fox_attention_bwd",
                                         comm=(hparts, hcuts))
    dfl, dbf = _fgate_bwd(proj, W["b_f_p"], _attn_dc(dqT, dkT, "attention_dc"), "forget_gate_bwd")
    G["b_f"] = dbf[:, :HEADS]
    dproj = jnp.concatenate([_attn_rows(dqT, HEAD_DIM ** -0.5, BF16, "attention_dq_rows"),
                             _attn_rows(dkT, 1.0, BF16, "attention_dk_rows"),
                             _attn_rows(dvT, 1.0, BF16, "attention_dv_rows"), dadg, dfl], axis=1)
    gp = _mm(n0, dproj, "tn", F32, 512, 896, "in_proj_dw")
    g_w_in = jnp.concatenate([gp[:, :3 * FOX_W], gp[:, 3 * FOX_W + 2 * CONV_CH:3 * FOX_W + 2 * CONV_CH + HEADS],
                              gp[:, 3 * FOX_W:3 * FOX_W + 2 * CONV_CH]], axis=1)
    g_w_in = g_w_in.reshape(D_MODEL, N_CHIPS, IN_SHARD).transpose(1, 0, 2)
    icut = [cut_of["w_in"]]
    dn0, irecv = _mm(dproj, W["w_in_p"], "nt", F32, tmm, D_MODEL, "in_proj_dx", host=([g_w_in], icut, "pairx"))
    ipart = _pair_sum2(g_w_in, irecv, icut[0], PAIR_SUM_BLOCKS["w_in"], "grad_pair_sum_w_in")
    dh0, G["mix_norm_even"], iother = _rms_bwd(h0, W["mix_norm_even"], dn0, dh1, tm, "mix_norm_even_bwd",
                                               host=([ipart], icut, "chipx"))
    parts = dict(zip(HOSTED_FFN + HOSTED + ("w_in",), parts1 + hparts + [ipart]))
    others = dict(zip(HOSTED_FFN + HOSTED + ("w_in",), list(others1) + list(hothers) + [iother]))
    return loss, dh0, G, parts, others


class _Cut:
    def __init__(self, full_shape, chip_dim, half_dim):
        self.full = tuple(full_shape)
        self.chip_dim, self.half_dim = chip_dim, half_dim
        self.chip_size = full_shape[chip_dim] // N_CHIPS
        self.half_size = full_shape[half_dim] // 2
        assert chip_dim != half_dim

    def shape(self, chip=False, half=False):
        s = list(self.full)
        if chip:
            s[self.chip_dim] = self.chip_size
        if half:
            s[self.half_dim] = self.half_size
        return tuple(s)

    def region(self, ref, chip=None, half=None):
        idx = [pl.ds(0, n) for n in ref.shape]
        if chip is not None:
            idx[self.chip_dim] = pl.ds(chip * self.chip_size, self.chip_size)
        if half is not None:
            idx[self.half_dim] = pl.ds(half * self.half_size, self.half_size)
        return ref.at[tuple(idx)]


SMALL_SHARDED = ("meta_tokens", "mix_norm_odd", "pool_b", "pool_scale", "conv_w", "ffn_conv_w")
SMALL_ROWS = 144


def _cuts():
    return {
        "w_in": _Cut((N_CHIPS, D_MODEL, IN_SHARD), 0, 1),
        "w_out": _Cut((D_MODEL, D_MODEL), 0, 1),
        "pool_w": _Cut((len(POOL_WINDOWS), POOL_G, POOL_G), 1, 0),
        "w_up": _Cut((2, D_MODEL, 2 * D_FF), 2, 1),
        "w_down": _Cut((2, D_FF, D_MODEL), 1, 2),
        "small": _Cut((N_CHIPS, SMALL_ROWS, LANE), 0, 1),
        "w_up0": _Cut((D_MODEL, 2 * D_FF), 1, 0), "w_up1": _Cut((D_MODEL, 2 * D_FF), 1, 0),
        "w_down0": _Cut((D_FF, D_MODEL), 0, 1), "w_down1": _Cut((D_FF, D_MODEL), 0, 1),
    }


COMM_ORDER = ("w_in", "w_out", "pool_w", "w_up", "w_down", "small")


def _remote(src, dst, send_sems, recv_sems, k, to):
    return pltpu.make_async_remote_copy(src_ref=src, dst_ref=dst, send_sem=send_sems.at[k],
                                        recv_sem=recv_sems.at[k], device_id=to, device_id_type=MESH)


def _gather_weights(shards, cuts):
    n = len(shards)

    def body(*refs):
        srcs, outs = refs[:n], refs[n:2 * n]
        send_sems, recv_sems = refs[2 * n:]
        x, y, c = _coords()
        me = 2 * x + y
        sibling = (x, y, 1 - c)
        chips = _other_chips(x, y)
        sends = []
        for t, cut in enumerate(cuts):
            push = _remote(srcs[t], cut.region(outs[t], chip=me), send_sems, recv_sems, 7 * t, sibling)
            push.start()
            sends.append(push)
            for kk, chip in enumerate(chips):
                cp = _remote(cut.region(srcs[t], half=c), cut.region(outs[t], chip=me, half=c),
                             send_sems, recv_sems, 7 * t + 1 + kk, (*chip, c))
                cp.start()
                sends.append(cp)
        for t, cut in enumerate(cuts):
            for kk, (px, py) in enumerate(chips):
                landed = cut.region(outs[t], chip=2 * px + py, half=c)
                _remote(landed, landed, send_sems, recv_sems, 7 * t + 1 + kk, sibling).wait_recv()
                fwd = _remote(landed, landed, send_sems, recv_sems, 7 * t + 4 + kk, sibling)
                fwd.start()
                sends.append(fwd)
        for t, cut in enumerate(cuts):
            mine = cut.region(outs[t], chip=me)
            _remote(mine, mine, send_sems, recv_sems, 7 * t, sibling).wait_recv()
            for kk, (px, py) in enumerate(chips):
                other = cut.region(outs[t], chip=2 * px + py, half=1 - c)
                _remote(other, other, send_sems, recv_sems, 7 * t + 4 + kk, sibling).wait_recv()
        for cp in sends:
            cp.wait_send()

    return pl.pallas_call(
        body, name="gather_weights",
        out_shape=tuple(jax.ShapeDtypeStruct(cut.full, s.dtype) for cut, s in zip(cuts, shards)),
        in_specs=[ANY] * n, out_specs=tuple([ANY] * n),
        scratch_shapes=[pltpu.SemaphoreType.DMA((7 * n,)), pltpu.SemaphoreType.DMA((7 * n,))],
    )(*shards)


def _gather_first_ops(srcs, outs, send_sems, recv_sems, cuts):
    x, y, c = _coords()
    me = 2 * x + y
    sibling = (x, y, 1 - c)
    chips = _other_chips(x, y)

    def copies():
        out = []
        for t, cut in enumerate(cuts):
            out.append(_remote(srcs[t], cut.region(outs[t], chip=me), send_sems, recv_sems, 4 * t, sibling))
            for kk, chip in enumerate(chips):
                out.append(_remote(cut.region(srcs[t], half=c), cut.region(outs[t], chip=me, half=c),
                                   send_sems, recv_sems, 4 * t + 1 + kk, (*chip, c)))
        return out

    def start():
        for cp in copies():
            cp.start()

    def wait():
        for t, cut in enumerate(cuts):
            mine = cut.region(outs[t], chip=me)
            _remote(mine, mine, send_sems, recv_sems, 4 * t, sibling).wait_recv()
            for kk, (px, py) in enumerate(chips):
                landed = cut.region(outs[t], chip=2 * px + py, half=c)
                _remote(landed, landed, send_sems, recv_sems, 4 * t + 1 + kk, sibling).wait_recv()
        for cp in copies():
            cp.wait_send()

    return start, wait


def _pair_exchange_ops(srcs, outs, send_sems, recv_sems, cuts):
    x, y, c = _coords()

    def copies():
        return [_remote(cut.region(srcs[t], half=1 - c), outs[t], send_sems, recv_sems, t, (x, y, 1 - c))
                for t, cut in enumerate(cuts)]

    def start():
        for cp in copies():
            cp.start()

    def wait():
        for cp in copies():
            cp.wait()

    return start, wait


def _host_plan(host):
    arrays, cuts = host[0], host[1]
    if len(host) > 2 and host[2] == "chipx":
        return _chip_exchange_shapes(arrays, cuts), 3 * len(arrays), _chip_exchange_ops
    if len(host) > 2 and host[2] == "pairx":
        return (tuple(jax.ShapeDtypeStruct(cut.shape(half=True), a.dtype) for cut, a in zip(cuts, arrays)),
                len(arrays), _pair_exchange_ops)
    return (tuple(jax.ShapeDtypeStruct(cut.full, a.dtype) for cut, a in zip(cuts, arrays)),
            4 * len(arrays), _gather_first_ops)


def _gather_forward(fulls, cuts, name):
    n = len(fulls)

    def body(*refs):
        outs = refs[n:2 * n]
        send_sems, recv_sems = refs[2 * n:]
        x, y, c = _coords()
        sibling = (x, y, 1 - c)
        chips = _other_chips(x, y)
        sends = []
        for t, cut in enumerate(cuts):
            for kk, (px, py) in enumerate(chips):
                landed = cut.region(outs[t], chip=2 * px + py, half=c)
                cp = _remote(landed, landed, send_sems, recv_sems, 3 * t + kk, sibling)
                cp.start()
                sends.append(cp)
        for t, cut in enumerate(cuts):
            for kk, (px, py) in enumerate(chips):
                other = cut.region(outs[t], chip=2 * px + py, half=1 - c)
                _remote(other, other, send_sems, recv_sems, 3 * t + kk, sibling).wait_recv()
        for cp in sends:
            cp.wait_send()

    return pl.pallas_call(
        body, name=name,
        out_shape=tuple(jax.ShapeDtypeStruct(f.shape, f.dtype) for f in fulls),
        in_specs=[ANY] * n, out_specs=tuple([ANY] * n), input_output_aliases={t: t for t in range(n)},
        scratch_shapes=[pltpu.SemaphoreType.DMA((3 * n,)), pltpu.SemaphoreType.DMA((3 * n,))],
    )(*fulls)


def _pair_exchange2(fulls, cuts, name):
    n = len(fulls)

    def body(*refs):
        srcs, outs = refs[:n], refs[n:2 * n]
        send_sems, recv_sems = refs[2 * n:]
        x, y, c = _coords()
        cps = [_remote(cut.region(srcs[t], half=1 - c), outs[t], send_sems, recv_sems, t, (x, y, 1 - c))
               for t, cut in enumerate(cuts)]
        for cp in cps:
            cp.start()
        for cp in cps:
            cp.wait()

    return pl.pallas_call(
        body, name=name,
        out_shape=tuple(jax.ShapeDtypeStruct(cut.shape(half=True), f.dtype) for cut, f in zip(cuts, fulls)),
        in_specs=[ANY] * n, out_specs=tuple([ANY] * n),
        scratch_shapes=[pltpu.SemaphoreType.DMA((n,)), pltpu.SemaphoreType.DMA((n,))],
    )(*fulls)


def _grid_of(shape, blk):
    assert all(s % b == 0 for s, b in zip(shape, blk)), (shape, blk)
    return tuple(s // b for s, b in zip(shape, blk))


def _pair_sum2(full, recv, cut, blk, name):
    hshape = cut.shape(half=True)
    grid = _grid_of(hshape, blk)
    hb = cut.half_size // blk[cut.half_dim]
    hd = cut.half_dim
    pos = jnp.stack([lax.axis_index("c")]).astype(jnp.int32)

    def full_idx(*a):
        ids, p = list(a[:-1]), a[-1]
        ids[hd] = ids[hd] + p[0] * hb
        return tuple(ids)

    def body(p_ref, f_ref, r_ref, o_ref):
        o_ref[...] = (f_ref[...] + r_ref[...]).astype(o_ref.dtype)

    return pl.pallas_call(
        body, name=name, out_shape=jax.ShapeDtypeStruct(hshape, BF16),
        grid_spec=pltpu.PrefetchScalarGridSpec(
            num_scalar_prefetch=1, grid=grid,
            in_specs=[pl.BlockSpec(blk, full_idx), pl.BlockSpec(blk, lambda *a: tuple(a[:-1]))],
            out_specs=pl.BlockSpec(blk, lambda *a: tuple(a[:-1]))),
        compiler_params=_params(("parallel",) * len(grid)))(pos, full, recv)


def _chip_exchange_ops(srcs, outs, send_sems, recv_sems, cuts):
    x, y, c = _coords()
    me = 2 * x + y
    chips = _other_chips(x, y)

    def copies():
        return [_remote(cut.region(srcs[t], chip=2 * px + py), outs[t].at[me], send_sems, recv_sems,
                        3 * t + kk, (px, py, c))
                for t, cut in enumerate(cuts) for kk, (px, py) in enumerate(chips)]

    def start():
        for cp in copies():
            cp.start()

    def wait():
        for t, cut in enumerate(cuts):
            for kk, (px, py) in enumerate(chips):
                slot = outs[t].at[2 * px + py]
                _remote(slot, slot, send_sems, recv_sems, 3 * t + kk, (px, py, c)).wait_recv()
        for cp in copies():
            cp.wait_send()

    return start, wait


def _chip_exchange_shapes(parts, cuts):
    return tuple(jax.ShapeDtypeStruct((N_CHIPS,) + cut.shape(chip=True, half=True), p.dtype)
                 for cut, p in zip(cuts, parts))


def _chip_exchange2(parts, cuts):
    n = len(parts)

    def body(*refs):
        start, wait = _chip_exchange_ops(refs[:n], refs[n:2 * n], refs[2 * n], refs[2 * n + 1], cuts)
        start()
        wait()

    return pl.pallas_call(
        body, name="grad_chip_exchange",
        out_shape=tuple(jax.ShapeDtypeStruct((N_CHIPS,) + cut.shape(chip=True, half=True), p.dtype)
                        for cut, p in zip(cuts, parts)),
        in_specs=[ANY] * n, out_specs=tuple([ANY] * n),
        scratch_shapes=[pltpu.SemaphoreType.DMA((3 * n,)), pltpu.SemaphoreType.DMA((3 * n,))],
    )(*parts)


def _chip_sum2(part, recv, cut, blk, name, stacked=None):
    bshape = cut.shape(chip=True, half=True)
    grid = _grid_of(bshape, blk)
    cb = cut.chip_size // blk[cut.chip_dim]
    hb = cut.half_size // blk[cut.half_dim]
    cd, hd = cut.chip_dim, cut.half_dim
    x, y, c = _coords()
    slots = [2 * px + py for px, py in _other_chips(x, y)]
    pos = jnp.stack([c, 2 * x + y] + slots).astype(jnp.int32)

    def part_idx(*a):
        ids, p = list(a[:-1]), a[-1]
        ids[cd] = ids[cd] + p[1] * cb
        return tuple(ids)

    def recv_idx(kk):
        return lambda *a: (a[-1][2 + kk],) + tuple(a[:-1])

    def out_idx(*a):
        ids, p = list(a[:-1]), a[-1]
        ids[hd] = ids[hd] + p[0] * hb
        return tuple(ids)

    def body(p_ref, own_ref, r0_ref, r1_ref, r2_ref, *rest):
        acc = own_ref[...].astype(F32)
        for r_ref in (r0_ref, r1_ref, r2_ref):
            acc = acc + r_ref[...].astype(F32)
        rest[-1][...] = acc

    in_specs = [pl.BlockSpec(blk, part_idx)] + [pl.BlockSpec((None,) + blk, recv_idx(kk)) for kk in range(3)]
    args = [pos, part, recv, recv, recv]
    aliases = {}
    if stacked is None:
        out_shape = jax.ShapeDtypeStruct(cut.shape(chip=True), F32)
        out_spec = pl.BlockSpec(blk, out_idx)
    else:
        lead, n_lead, into = stacked
        out_shape = jax.ShapeDtypeStruct((n_lead,) + cut.shape(chip=True), F32)
        out_spec = pl.BlockSpec((None,) + blk, lambda *a: (lead,) + out_idx(*a))
        if into is not None:
            in_specs.append(pl.BlockSpec(memory_space=pl.ANY))
            args.append(into)
            aliases = {5: 0}
    return pl.pallas_call(
        body, name=name, out_shape=out_shape,
        grid_spec=pltpu.PrefetchScalarGridSpec(num_scalar_prefetch=1, grid=grid, in_specs=in_specs,
                                               out_specs=out_spec),
        input_output_aliases=aliases, compiler_params=_params(("parallel",) * len(grid)))(*args)


def _pair_swap2(blocks, cuts):
    n = len(blocks)

    def body(*refs):
        outs = refs[n:2 * n]
        send_sems, recv_sems = refs[2 * n:]
        x, y, c = _coords()
        cps = []
        for t, cut in enumerate(cuts):
            mine = cut.region(outs[t], half=c)
            cp = _remote(mine, mine, send_sems, recv_sems, t, (x, y, 1 - c))
            cp.start()
            cps.append(cp)
        for t, cut in enumerate(cuts):
            theirs = cut.region(outs[t], half=1 - c)
            _remote(theirs, theirs, send_sems, recv_sems, t, (x, y, 1 - c)).wait_recv()
        for cp in cps:
            cp.wait_send()

    return pl.pallas_call(
        body, name="grad_pair_swap",
        out_shape=tuple(jax.ShapeDtypeStruct(b.shape, b.dtype) for b in blocks),
        in_specs=[ANY] * n, out_specs=tuple([ANY] * n),
        input_output_aliases={t: t for t in range(n)},
        scratch_shapes=[pltpu.SemaphoreType.DMA((n,)), pltpu.SemaphoreType.DMA((n,))],
    )(*blocks)


PAIR_SUM_BLOCKS = {"w_in": (1, 512, IN_SHARD), "w_out": (512, 512), "pool_w": (1, POOL_G, POOL_G),
                   "w_up0": (64, 2 * D_FF), "w_up1": (64, 2 * D_FF), "w_down0": (704, 512), "w_down1": (704, 512),
                   "small": (N_CHIPS, SMALL_ROWS // 2, LANE)}
CHIP_SUM_BLOCKS = {"w_in": (1, 512, IN_SHARD), "w_out": (256, 512), "pool_w": (2, 64, POOL_G),
                   "w_up0": (128, UP_SHARD), "w_up1": (128, UP_SHARD),
                   "w_down0": (DOWN_SHARD, 512), "w_down1": (DOWN_SHARD, 512),
                   "small": (1, SMALL_ROWS // 2, LANE)}


def _pack_small(P):
    parts = []
    for name in SMALL_SHARDED:
        t = P[name]
        parts.append(t.astype(F32))
    return parts


def _small_rows(t, lead):
    flat = t.reshape(lead + (-1,))
    pad = -flat.shape[-1] % LANE
    return jnp.pad(flat, [(0, 0)] * len(lead) + [(0, pad)]).reshape(lead + (-1, LANE))


def _pack_small_shards(shards):
    rows = jnp.concatenate([_small_rows(shards[n].astype(F32), ()) for n in SMALL_SHARDED], axis=0)
    return jnp.pad(rows, ((0, SMALL_ROWS - rows.shape[0]), (0, 0)))[None]


def _unpack_small(pack, shards, axes):
    out, off = {}, 0
    nchip = pack.shape[0]
    for name in SMALL_SHARDED:
        shp = shards[name].shape
        cnt = int(np.prod(shp))
        rows = -(-cnt // LANE)
        t = pack[:, off:off + rows].reshape(nchip, -1)[:, :cnt].reshape((nchip,) + shp)
        out[name] = jnp.concatenate([t[j] for j in range(nchip)], axis=axes[name])
        off += rows
    return out


def _pack_small_grads(grads, shards, axes):
    parts = []
    for name in SMALL_SHARDED:
        shp, ax = shards[name].shape, axes[name]
        g = grads[name].reshape(shp[:ax] + (N_CHIPS, shp[ax]) + shp[ax + 1:])
        parts.append(_small_rows(jnp.moveaxis(g, ax, 0), (N_CHIPS,)))
    rows = jnp.concatenate(parts, axis=1)
    return jnp.pad(rows, ((0, 0), (0, SMALL_ROWS - rows.shape[1]), (0, 0)))


SMALL_AXES = {"meta_tokens": 1, "mix_norm_odd": 1, "pool_b": 2, "pool_scale": 1, "conv_w": 2, "ffn_conv_w": 2}


WEIGHT_NAMES = ("meta_tokens", "mix_norm_even", "w_in", "b_f", "conv_w", "conv_b", "ln_g", "ln_b", "w_out",
                "mix_norm_odd", "pool_w", "pool_b", "pool_scale", "ffn_norm", "w_up", "ffn_conv_w",
                "ffn_conv_b", "w_down", "final_norm")


def kernel(x, meta_tokens, mix_norm_even, w_in, b_f, conv_w, conv_b, ln_g, ln_b, w_out, mix_norm_odd, pool_w, pool_b, pool_scale, ffn_norm, w_up, ffn_conv_w, ffn_conv_b, w_down, final_norm, loss_target, m_meta_tokens, m_mix_norm_even, m_w_in, m_b_f, m_conv_w, m_conv_b, m_ln_g, m_ln_b, m_w_out, m_mix_norm_odd, m_pool_w, m_pool_b, m_pool_scale, m_ffn_norm, m_w_up, m_ffn_conv_w, m_ffn_conv_b, m_w_down, m_final_norm, v_meta_tokens, v_mix_norm_even, v_w_in, v_b_f, v_conv_w, v_conv_b, v_ln_g, v_ln_b, v_w_out, v_mix_norm_odd, v_pool_w, v_pool_b, v_pool_scale, v_ffn_norm, v_w_up, v_ffn_conv_w, v_ffn_conv_b, v_w_down, v_final_norm):
    given = dict(locals())
    w_loc = {n: given[n] for n in WEIGHT_NAMES}
    m_loc = {n: given["m_" + n] for n in WEIGHT_NAMES}
    v_loc = {n: given["v_" + n] for n in WEIGHT_NAMES}
    cut_of = _cuts()
    cuts = [cut_of[n] for n in COMM_ORDER]
    big = ("w_in", "w_out", "pool_w", "w_up", "w_down")
    small_shards = {n: w_loc[n] for n in SMALL_SHARDED}

    shard_of = {n: w_loc[n].astype(BF16).reshape(cut_of[n].shape(chip=True)) for n in big}
    shard_of["small"] = _pack_small_shards(small_shards)
    g_in, g_small = _gather_weights([shard_of[n] for n in GATHER_FIRST], [cut_of[n] for n in GATHER_FIRST])
    g_out = None
    g_pool = g_up = g_down = None
    full = _unpack_small(g_small, small_shards, SMALL_AXES)
    full.update({n: w_loc[n] for n in REPLICATED})
    w_in_full = g_in.transpose(1, 0, 2).reshape(D_MODEL, IN_COLS)
    qkv, f, ag = (w_in_full[:, :3 * FOX_W], w_in_full[:, 3 * FOX_W:3 * FOX_W + HEADS],
                  w_in_full[:, 3 * FOX_W + HEADS:])
    W = dict(
        mix_norm_even=full["mix_norm_even"].reshape(1, D_MODEL),
        w_in_p=jnp.concatenate([qkv, ag, f, jnp.zeros((D_MODEL, LANE - HEADS), BF16)], axis=1),
        b_f_p=jnp.pad(full["b_f"].reshape(1, HEADS), ((0, 0), (0, LANE - HEADS))),
        conv_w_p=jnp.pad(full["conv_w"].reshape(CONV_K, CONV_CH), ((0, CONV_HALO - CONV_K), (0, 0))),
        conv_b=full["conv_b"].reshape(1, CONV_CH), ln_g=full["ln_g"].reshape(1, CONV_CH),
        ln_b=full["ln_b"].reshape(1, CONV_CH), w_out=g_out,
        mix_norm_odd=full["mix_norm_odd"].reshape(1, D_MODEL), pool_w=g_pool,
        pool_b=full["pool_b"].reshape(1, D_MODEL), pool_scale=full["pool_scale"].reshape(1, D_MODEL),
        ffn_norm=full["ffn_norm"], w_up=g_up,
        ffn_conv_w_p=jnp.pad(full["ffn_conv_w"], ((0, 0), (0, 8 - FFN_K), (0, 0))),
        ffn_conv_b=full["ffn_conv_b"], w_down=g_down, final_norm=full["final_norm"].reshape(1, D_MODEL),
        late_shards=dict(pool_w=shard_of["pool_w"], w_out=shard_of["w_out"],
                         w_up0=shard_of["w_up"][0], w_up1=shard_of["w_up"][1],
                         w_down0=shard_of["w_down"][0], w_down1=shard_of["w_down"][1]))

    seq = x.shape[1]
    n_real = N_META + seq
    LP = -(-n_real // ATT_BLK) * ATT_BLK
    tail = jnp.zeros((LP - n_real, D_MODEL), F32)
    h0 = jnp.concatenate([full["meta_tokens"], x[0], tail], axis=0)
    tgt = jnp.concatenate([jnp.zeros((N_META, D_MODEL), F32), loss_target[0], tail], axis=0)
    loss_loc, dh0, G, parts, others = _local_step2(h0, tgt, W, n_real, cut_of)
    grad_x = dh0[N_META:n_real][None]
    G["meta_tokens"] = dh0[:N_META]

    rep_shapes = {n: w_loc[n].shape for n in REPLICATED}
    G["final_norm"] = G["final_norm"].reshape(D_MODEL)
    rep, loss = _unpack_replicated(_allreduce_small(_pack_replicated(G, loss_loc)), rep_shapes)

    lcuts = [cut_of[n] for n in LATE]
    lfull = [_pack_small_grads(G, small_shards, SMALL_AXES)]
    lrecv = _pair_exchange2(lfull, lcuts, "grad_pair_exchange_late")
    lparts = [_pair_sum2(f, r, cut, PAIR_SUM_BLOCKS[n], "grad_pair_sum_" + n)
              for f, r, cut, n in zip(lfull, lrecv, lcuts, LATE)]
    parts.update(zip(LATE, lparts))
    others.update(zip(LATE, _chip_exchange2(lparts, lcuts)))
    def chip_sum(n, stacked=None):
        return _chip_sum2(parts[n], others[n], cut_of[n], CHIP_SUM_BLOCKS[n], "grad_chip_sum_" + n, stacked=stacked)

    blocks = []
    for n in COMM_ORDER:
        if n in ("w_up", "w_down"):
            blocks.append(chip_sum(n + "1", stacked=(1, 2, chip_sum(n + "0", stacked=(0, 2, None)))))
        else:
            blocks.append(chip_sum(n))
    blocks = _pair_swap2(blocks, cuts)
    gsh = {n: b.reshape(w_loc[n].shape) for n, b in zip(big, blocks[:5])}
    gsh.update(_unpack_small(blocks[5], small_shards, SMALL_AXES))
    sharded = set(big) | set(SMALL_SHARDED)

    grad_w = {n: (gsh[n] if n in sharded else rep[n]) for n in WEIGHT_NAMES}
    delta, new_m, new_v = {}, {}, {}
    for n in WEIGHT_NAMES:
        delta[n], new_m[n], new_v[n] = _adamw(w_loc[n], grad_w[n], m_loc[n], v_loc[n], "adamw_" + n)
    return (loss, grad_x, *[grad_w[n] for n in WEIGHT_NAMES], *[delta[n] for n in WEIGHT_NAMES],
            *[new_m[n] for n in WEIGHT_NAMES], *[new_v[n] for n in WEIGHT_NAMES])
```

```python
import numpy as np
import jax
import jax.numpy as jnp
from jax import lax
from jax.experimental import pallas as pl
from jax.experimental.pallas import tpu as pltpu

F32 = jnp.float32
BF16 = jnp.bfloat16

D_MODEL = 1024
N_META = 16
SEQ = 2048
HEADS = 8
HEAD_DIM = 64
FOX_W = HEADS * HEAD_DIM
CONV_CH = 512
CONV_K = 31
D_FF = 2816
POOL_WINDOWS = (2, 4, 8, 16)
POOL_G = 256
RMS_EPS = 1e-6
LN_EPS = 1e-5
IN_COLS = 3 * FOX_W + HEADS + 2 * CONV_CH
IN_COLS_P = 3 * FOX_W + 2 * CONV_CH + 128
F_COL_BLK = (3 * FOX_W + 2 * CONV_CH) // 128
N_CHIPS = 4
IN_SHARD = IN_COLS // N_CHIPS
UP_SHARD = 2 * D_FF // N_CHIPS
DOWN_SHARD = D_FF // N_CHIPS

ADAM_LR = 0.001
ADAM_B1 = 0.9
ADAM_B2 = 0.999
ADAM_EPS = 1e-08
ADAM_WD = 0.01
ADAM_STEP = 10

LANE = 128
ATT_BLK = 128
VMEM_LIMIT = 56 * 1024 * 1024

NEG = -1e30


def _sigmoid(x):
    return 0.5 * jnp.tanh(0.5 * x) + 0.5


def _sigmoid_tail(x):
    return 1.0 / (1.0 + jnp.exp(-x))


def _params(sem=None):
    return pltpu.CompilerParams(dimension_semantics=sem, vmem_limit_bytes=VMEM_LIMIT)


def _sub_rows(tm):
    best = 8
    for s in range(8, 137, 8):
        if tm % s == 0:
            best = s
    return best


def _mm(a, b, mode, out_dtype, tm, tn, name, add=None, a_lead=None, b_lead=None, out=None, host=None):
    a_shape = a.shape if a_lead is None else a.shape[1:]
    b_shape = b.shape if b_lead is None else b.shape[1:]
    if mode == "nn":
        (M, K), (K2, N) = a_shape, b_shape
        dims = (((1,), (0,)), ((), ()))
        a_blk, a_idx = (tm, K), (lambda i, j: (i, 0))
        b_blk, b_idx = (K, tn), (lambda i, j: (0, j))
    elif mode == "nt":
        (M, K), (N, K2) = a_shape, b_shape
        dims = (((1,), (1,)), ((), ()))
        a_blk, a_idx = (tm, K), (lambda i, j: (i, 0))
        b_blk, b_idx = (tn, K), (lambda i, j: (j, 0))
    else:
        (K, M), (K2, N) = a_shape, b_shape
        dims = (((0,), (0,)), ((), ()))
        a_blk, a_idx = (K, tm), (lambda i, j: (0, i))
        b_blk, b_idx = (K, tn), (lambda i, j: (0, j))
    assert K == K2 and M % tm == 0 and N % tn == 0, (name, a.shape, b.shape, tm, tn)
    gm, gn = M // tm, N // tn
    a_bytes = M * K * a.dtype.itemsize
    b_bytes = N * K * b.dtype.itemsize
    m_outer = a_bytes + b_bytes * gm <= b_bytes + a_bytes * gn
    if m_outer:
        grid = (gm, gn)
        wrap = lambda f: f
    else:
        grid = (gn, gm)
        wrap = lambda f: (lambda j, i: f(i, j))

    def lead(blk, idx, at):
        if at is None:
            return pl.BlockSpec(blk, wrap(idx))
        return pl.BlockSpec((None,) + blk, wrap(lambda i, j: (at,) + idx(i, j)))

    o_idx = lambda i, j: (i, j)
    in_specs = [lead(a_blk, a_idx, a_lead), lead(b_blk, b_idx, b_lead)]
    args = [a, b]
    if add is not None:
        in_specs.append(pl.BlockSpec((tm, tn), wrap(o_idx)))
        args.append(add)
    aliases = {}
    if out is None:
        out_shape = jax.ShapeDtypeStruct((M, N), out_dtype)
        out_spec = pl.BlockSpec((tm, tn), wrap(o_idx))
    else:
        o_lead, n_lead, into = out
        out_shape = jax.ShapeDtypeStruct((n_lead, M, N), out_dtype)
        out_spec = lead((tm, tn), o_idx, o_lead)
        if into is not None:
            aliases = {len(args): 0}
            in_specs.append(pl.BlockSpec(memory_space=pl.ANY))
            args.append(into)
    has_add = add is not None
    n_host = 0 if host is None else len(host[0])
    n_in = len(args)
    scratch = []
    if n_host:
        host_shapes, host_sems, host_ops = _host_plan(host)
        in_specs = in_specs + [pl.BlockSpec(memory_space=pl.ANY)] * n_host
        args = args + list(host[0])
        out_shape = (out_shape,) + host_shapes
        out_spec = (out_spec,) + (pl.BlockSpec(memory_space=pl.ANY),) * n_host
        scratch = [pltpu.SemaphoreType.DMA((host_sems,)), pltpu.SemaphoreType.DMA((host_sems,))]

    def body(*refs):
        a_ref, b_ref = refs[0], refs[1]
        o_ref = refs[n_in + n_host]
        if n_host:
            start, wait = host_ops(refs[n_in:n_in + n_host], refs[n_in + n_host + 1:n_in + 2 * n_host + 1],
                                   refs[n_in + 2 * n_host + 1], refs[n_in + 2 * n_host + 2], host[1])
            pl.when(jnp.logical_and(pl.program_id(0) == 0, pl.program_id(1) == 0))(start)
        x = a_ref[...].astype(BF16)
        y = b_ref[...].astype(BF16)
        acc = lax.dot_general(x, y, dims, preferred_element_type=F32)
        if has_add:
            acc = acc + refs[2][...]
        o_ref[...] = acc.astype(o_ref.dtype)
        if n_host:
            pl.when(jnp.logical_and(pl.program_id(0) == grid[0] - 1, pl.program_id(1) == grid[1] - 1))(wait)

    sem = ("arbitrary", "arbitrary") if n_host else ("parallel", "parallel")
    return pl.pallas_call(
        body, name=name, out_shape=out_shape, grid=grid, in_specs=in_specs, out_specs=out_spec,
        scratch_shapes=scratch, input_output_aliases=aliases, compiler_params=_params(sem))(*args)


def _mm_ffn_dn(dup, w_up, tm, tn, name):
    _, LP, F = dup.shape
    Dm = w_up.shape[0]
    nt = (((1,), (1,)), ((), ()))

    def body(a_ref, b_ref, o_ref):
        acc = lax.dot_general(a_ref[0], b_ref[:, 0:F], nt, preferred_element_type=F32)
        acc = acc + lax.dot_general(a_ref[1], b_ref[:, F:2 * F], nt, preferred_element_type=F32)
        o_ref[...] = acc

    return pl.pallas_call(
        body, name=name, out_shape=jax.ShapeDtypeStruct((LP, Dm), F32), grid=(LP // tm, Dm // tn),
        in_specs=[pl.BlockSpec((2, tm, F), lambda i, j: (0, i, 0)),
                  pl.BlockSpec((tn, 2 * F), lambda i, j: (j, 0))],
        out_specs=pl.BlockSpec((tm, tn), lambda i, j: (i, j)),
        compiler_params=_params(("parallel", "parallel")))(dup, w_up)


def _mm_ffn_dwup(n, dup, tk, tn, name):
    LP, Dm = n.shape
    F = dup.shape[2]
    nct = F // tn
    tdims = (((0,), (0,)), ((), ()))

    def body(a_ref, b_ref, o_ref):
        o_ref[...] = lax.dot_general(a_ref[...], b_ref[...], tdims, preferred_element_type=F32)

    return pl.pallas_call(
        body, name=name, out_shape=jax.ShapeDtypeStruct((Dm, 2 * F), F32), grid=(Dm // tk, 2 * nct),
        in_specs=[pl.BlockSpec((LP, tk), lambda i, j: (0, i)),
                  pl.BlockSpec((None, LP, tn), lambda i, j: (j // nct, 0, j % nct))],
        out_specs=pl.BlockSpec((tk, tn), lambda i, j: (i, j)),
        compiler_params=_params(("parallel", "parallel")))(n, dup)


def _rms_fwd(h, g, out_dtype, tm, name):
    LP, Dm = h.shape

    def body(h_ref, g_ref, o_ref):
        x = h_ref[...]
        r = lax.rsqrt(jnp.mean(x * x, axis=1, keepdims=True) + RMS_EPS)
        o_ref[...] = (x * r * g_ref[...]).astype(o_ref.dtype)

    return pl.pallas_call(
        body, name=name, out_shape=jax.ShapeDtypeStruct((LP, Dm), out_dtype), grid=(LP // tm,),
        in_specs=[pl.BlockSpec((tm, Dm), lambda i: (i, 0)), pl.BlockSpec((1, Dm), lambda i: (0, 0))],
        out_specs=pl.BlockSpec((tm, Dm), lambda i: (i, 0)),
        compiler_params=_params(("parallel",)))(h, g)


def _rms_bwd(h, g, dn, dres, tm, name, host=None):
    LP, Dm = h.shape
    n_host = 0 if host is None else len(host[0])
    if n_host:
        host_shapes, host_sems, host_ops = _host_plan(host)
    nblk = LP // tm

    def body(*refs):
        h_ref, g_ref, dn_ref, dr_ref = refs[:4]
        dh_ref, dg_ref = refs[4 + n_host:6 + n_host]
        if n_host:
            start, wait = host_ops(refs[4:4 + n_host], refs[6 + n_host:6 + 2 * n_host],
                                   refs[6 + 2 * n_host], refs[7 + 2 * n_host], host[1])
            pl.when(pl.program_id(0) == 0)(start)
        i = pl.program_id(0)
        x = h_ref[...]
        r = lax.rsqrt(jnp.mean(x * x, axis=1, keepdims=True) + RMS_EPS)
        xhat = x * r
        dy = dn_ref[...]
        dxh = dy * g_ref[...]
        dh = r * (dxh - xhat * jnp.mean(dxh * xhat, axis=1, keepdims=True))
        dh_ref[...] = dr_ref[...] + dh

        @pl.when(i == 0)
        def _():
            dg_ref[...] = jnp.zeros_like(dg_ref)

        dg_ref[...] += jnp.sum(dy * xhat, axis=0, keepdims=True)
        if n_host:
            pl.when(pl.program_id(0) == nblk - 1)(wait)

    row = pl.BlockSpec((tm, Dm), lambda i: (i, 0))
    vec = pl.BlockSpec((1, Dm), lambda i: (0, 0))
    out_shape = (jax.ShapeDtypeStruct((LP, Dm), F32), jax.ShapeDtypeStruct((1, Dm), F32))
    out_specs = (row, vec)
    in_specs = [row, vec, row, row]
    args = [h, g, dn, dres]
    scratch = []
    if n_host:
        anyspec = pl.BlockSpec(memory_space=pl.ANY)
        in_specs += [anyspec] * n_host
        args += list(host[0])
        out_shape += host_shapes
        out_specs += (anyspec,) * n_host
        scratch = [pltpu.SemaphoreType.DMA((host_sems,)), pltpu.SemaphoreType.DMA((host_sems,))]
    return pl.pallas_call(
        body, name=name, out_shape=out_shape, grid=(nblk,), in_specs=in_specs, out_specs=out_specs,
        scratch_shapes=scratch, compiler_params=_params(("arbitrary",)))(*args)


def _loss_head(h, g, tgt, n_real, tm, name):
    LP, Dm = h.shape

    def body(h_ref, g_ref, t_ref, loss_ref, dh_ref, dg_ref):
        i = pl.program_id(0)
        x = h_ref[...]
        gg = g_ref[...]
        r = lax.rsqrt(jnp.mean(x * x, axis=1, keepdims=True) + RMS_EPS)
        xhat = x * r
        rows = i * tm + lax.broadcasted_iota(jnp.int32, (tm, 1), 0)
        real = jnp.logical_and(rows >= N_META, rows < n_real)
        diff = jnp.where(real, xhat * gg - t_ref[...], 0.0)
        dy = diff * (1.0 / Dm)
        dxh = dy * gg
        dh_ref[...] = r * (dxh - xhat * jnp.mean(dxh * xhat, axis=1, keepdims=True))

        @pl.when(i == 0)
        def _():
            dg_ref[...] = jnp.zeros_like(dg_ref)
            loss_ref[...] = jnp.zeros_like(loss_ref)

        dg_ref[...] += jnp.sum(dy * xhat, axis=0, keepdims=True)
        part = jnp.sum(jnp.sum(diff * diff, axis=1, keepdims=True), axis=0, keepdims=True)
        loss_ref[...] += jnp.broadcast_to(part * (0.5 / Dm), loss_ref.shape)

    row = pl.BlockSpec((tm, Dm), lambda i: (i, 0))
    vec = pl.BlockSpec((1, Dm), lambda i: (0, 0))
    return pl.pallas_call(
        body, name=name,
        out_shape=(jax.ShapeDtypeStruct((1, LANE), F32), jax.ShapeDtypeStruct((LP, Dm), F32),
                   jax.ShapeDtypeStruct((1, Dm), F32)),
        grid=(LP // tm,), in_specs=[row, vec, row],
        out_specs=(pl.BlockSpec((1, LANE), lambda i: (0, 0)), row, vec),
        compiler_params=_params(("arbitrary",)))(h, g, tgt)


def _fgate_fwd(proj, bf_p, name):
    LP = proj.shape[0]
    nb = LP // LANE

    def body(f_ref, b_ref, c_ref, lf_ref):
        x = f_ref[...] + b_ref[...]
        lf_ref[...] = jnp.minimum(x, 0.0) - jnp.log1p(jnp.exp(-jnp.abs(x)))
        ri = lax.broadcasted_iota(jnp.int32, (LANE, LANE), 0)
        ci = lax.broadcasted_iota(jnp.int32, (LANE, LANE), 1)
        tri = jnp.where(ri >= ci, 1.0, 0.0).astype(F32)

        def blk(i, carry):
            rows = pl.ds(pl.multiple_of(i * LANE, LANE), LANE)
            cb = jnp.dot(tri, lf_ref[rows, :], precision=lax.Precision.HIGHEST,
                         preferred_element_type=F32) + carry
            c_ref[rows, :] = cb
            return cb[LANE - 1:LANE, :]

        lax.fori_loop(0, nb, blk, jnp.zeros((1, LANE), F32))

    return pl.pallas_call(
        body, name=name, out_shape=jax.ShapeDtypeStruct((LP, LANE), F32), grid=(1,),
        in_specs=[pl.BlockSpec((LP, LANE), lambda i: (0, F_COL_BLK)),
                  pl.BlockSpec((1, LANE), lambda i: (0, 0))],
        out_specs=pl.BlockSpec((LP, LANE), lambda i: (0, 0)),
        scratch_shapes=[pltpu.VMEM((LP, LANE), F32)],
        compiler_params=_params(("arbitrary",)))(proj, bf_p)


def _fgate_bwd(proj, bf_p, dc, name):
    LP = proj.shape[0]
    nb = LP // LANE

    def body(f_ref, b_ref, dc_ref, dl_ref, db_ref):
        ri = lax.broadcasted_iota(jnp.int32, (LANE, LANE), 0)
        ci = lax.broadcasted_iota(jnp.int32, (LANE, LANE), 1)
        triu = jnp.where(ri <= ci, 1.0, 0.0).astype(F32)
        bb = b_ref[...]

        tail = jnp.zeros((1, LANE), F32)
        dbs = jnp.zeros((1, LANE), F32)
        for i in range(nb - 1, -1, -1):
            rows = slice(i * LANE, (i + 1) * LANE)
            gb = jnp.dot(triu, dc_ref[rows, :], precision=lax.Precision.HIGHEST,
                         preferred_element_type=F32) + tail
            x = f_ref[rows, :] + bb
            dl = gb * _sigmoid_tail(-x)
            dl_ref[rows, :] = dl.astype(dl_ref.dtype)
            tail = gb[0:1, :]
            dbs = dbs + jnp.sum(dl, axis=0, keepdims=True)
        db_ref[...] = dbs

    return pl.pallas_call(
        body, name=name,
        out_shape=(jax.ShapeDtypeStruct((LP, LANE), BF16), jax.ShapeDtypeStruct((1, LANE), F32)),
        grid=(1,),
        in_specs=[pl.BlockSpec((LP, LANE), lambda i: (0, F_COL_BLK)),
                  pl.BlockSpec((1, LANE), lambda i: (0, 0)),
                  pl.BlockSpec((LP, LANE), lambda i: (0, 0))],
        out_specs=(pl.BlockSpec((LP, LANE), lambda i: (0, 0)), pl.BlockSpec((1, LANE), lambda i: (0, 0))),
        compiler_params=_params(("arbitrary",)))(proj, bf_p, dc)


AUG = 128
ONES_IN_K = HEAD_DIM
ONES_IN_Q = HEAD_DIM + 3
ATT_HEADS_PER_STEP = 8
ATT_HEADS_PER_STEP_BWD = 8


def _attn_prep(proj, c, name):
    LP = proj.shape[0]
    nb = LP // ATT_BLK
    tail_rows = AUG - HEAD_DIM

    def body(q_ref, k_ref, v_ref, c_ref, qT_ref, kT_ref, ka_ref, vT_ref):
        qt = (q_ref[...] * (HEAD_DIM ** -0.5)).T
        kt = k_ref[...].T
        vt = v_ref[...].T
        ct = c_ref[...].T
        hi = ct.astype(BF16).astype(F32)
        r1 = ct - hi
        mid = r1.astype(BF16).astype(F32)
        lo = (r1 - mid).astype(BF16).astype(F32)
        row = lax.broadcasted_iota(jnp.int32, (tail_rows, ATT_BLK), 0)
        ones = jnp.where(row < 3, 1.0, 0.0)
        for h in range(HEADS):
            cparts = jnp.where(row == 0, hi[h:h + 1], jnp.where(row == 1, mid[h:h + 1],
                               jnp.where(row == 2, lo[h:h + 1], 0.0)))
            hs = slice(h * HEAD_DIM, (h + 1) * HEAD_DIM)
            q_tail = cparts + pltpu.roll(ones, 3, 0)
            k_tail = ones - pltpu.roll(cparts, 3, 0)
            qT_ref[h] = jnp.concatenate([qt[hs], q_tail], axis=0).astype(BF16)
            kfull = jnp.concatenate([kt[hs], k_tail], axis=0)
            kT_ref[h] = kfull.astype(BF16)
            ka_ref[h] = kfull.T.astype(BF16)
            vT_ref[h] = vt[hs].astype(BF16)

    col = lambda j: pl.BlockSpec((ATT_BLK, FOX_W), lambda i: (i, j))
    blk = lambda r: pl.BlockSpec((HEADS, None, r, ATT_BLK), lambda i: (0, i, 0, 0))
    return pl.pallas_call(
        body, name=name,
        out_shape=(jax.ShapeDtypeStruct((HEADS, nb, AUG, ATT_BLK), BF16),
                   jax.ShapeDtypeStruct((HEADS, nb, AUG, ATT_BLK), BF16),
                   jax.ShapeDtypeStruct((HEADS, LP, AUG), BF16),
                   jax.ShapeDtypeStruct((HEADS, nb, HEAD_DIM, ATT_BLK), BF16)),
        grid=(nb,), in_specs=[col(0), col(1), col(2), pl.BlockSpec((ATT_BLK, LANE), lambda i: (i, 0))],
        out_specs=(blk(AUG), blk(AUG), pl.BlockSpec((HEADS, ATT_BLK, AUG), lambda i: (0, i, 0)), blk(HEAD_DIM)),
        compiler_params=_params(("parallel",)))(proj, proj, proj, c)


def _attn_rows(xT, scale, out_dtype, name):
    Hh, nb, R, _ = xT.shape

    def body(x_ref, o_ref):
        stack = jnp.concatenate([x_ref[h, 0:HEAD_DIM, :] for h in range(Hh)], axis=0)
        o_ref[...] = (stack * scale).T.astype(o_ref.dtype)

    return pl.pallas_call(
        body, name=name, out_shape=jax.ShapeDtypeStruct((nb * ATT_BLK, Hh * HEAD_DIM), out_dtype), grid=(nb,),
        in_specs=[pl.BlockSpec((Hh, None, R, ATT_BLK), lambda i: (0, i, 0, 0))],
        out_specs=pl.BlockSpec((ATT_BLK, Hh * HEAD_DIM), lambda i: (i, 0)),
        compiler_params=_params(("parallel",)))(xT)


def _attn_cols(x, name):
    LP = x.shape[0]
    nb = LP // ATT_BLK

    def body(x_ref, o_ref):
        xt = x_ref[...].T
        for h in range(HEADS):
            o_ref[h] = xt[h * HEAD_DIM:(h + 1) * HEAD_DIM].astype(o_ref.dtype)

    return pl.pallas_call(
        body, name=name, out_shape=jax.ShapeDtypeStruct((HEADS, nb, HEAD_DIM, ATT_BLK), BF16), grid=(nb,),
        in_specs=[pl.BlockSpec((ATT_BLK, FOX_W), lambda i: (i, 0))],
        out_specs=pl.BlockSpec((HEADS, None, HEAD_DIM, ATT_BLK), lambda i: (0, i, 0, 0)),
        compiler_params=_params(("parallel",)))(x)


def _attn_grads_rows(dqT, dkT, dvT, name):
    Hh, nb, _, _ = dqT.shape
    W3 = Hh * HEAD_DIM

    def body(q_ref, k_ref, v_ref, o_ref, dc_ref):
        for col, (x_ref, scale) in enumerate(((q_ref, HEAD_DIM ** -0.5), (k_ref, 1.0), (v_ref, 1.0))):
            stack = jnp.concatenate([x_ref[h, 0:HEAD_DIM, :] for h in range(Hh)], axis=0)
            o_ref[:, col * W3:(col + 1) * W3] = (stack * scale).T.astype(o_ref.dtype)
        row = lax.broadcasted_iota(jnp.int32, (LANE, ATT_BLK), 0)
        acc = jnp.zeros((LANE, ATT_BLK), F32)
        for h in range(Hh):
            d = q_ref[h, ONES_IN_K:ONES_IN_K + 1, :] - k_ref[h, ONES_IN_Q:ONES_IN_Q + 1, :]
            acc = jnp.where(row == h, d, acc)
        dc_ref[...] = acc.T

    spec = lambda r: pl.BlockSpec((Hh, None, r, ATT_BLK), lambda i: (0, i, 0, 0))
    return pl.pallas_call(
        body, name=name,
        out_shape=(jax.ShapeDtypeStruct((nb * ATT_BLK, 3 * W3), BF16), jax.ShapeDtypeStruct((nb * ATT_BLK, LANE), F32)),
        grid=(nb,), in_specs=[spec(AUG), spec(AUG), spec(HEAD_DIM)],
        out_specs=(pl.BlockSpec((ATT_BLK, 3 * W3), lambda i: (i, 0)), pl.BlockSpec((ATT_BLK, LANE), lambda i: (i, 0))),
        compiler_params=_params(("parallel",)))(dqT, dkT, dvT)


def _attn_fwd2(qT, k_aug, vT, name, comm=None):
    Hh, nb, _, _ = qT.shape
    LP = nb * ATT_BLK
    Dh = vT.shape[2]
    HB = ATT_HEADS_PER_STEP
    n_comm = 0 if comm is None else len(comm[0])

    def body(*refs):
        q_ref, k_ref, v_ref = refs[:3]
        o_ref, lse_ref = refs[3 + n_comm:5 + n_comm]
        if n_comm:
            start, wait = _gather_first_ops(refs[3:3 + n_comm], refs[5 + n_comm:5 + 2 * n_comm],
                                            refs[5 + 2 * n_comm], refs[6 + 2 * n_comm], comm[1])
            pl.when(pl.program_id(0) == 0)(start)
        keys = lax.broadcasted_iota(jnp.int32, (ATT_BLK, ATT_BLK), 0)
        qrys = lax.broadcasted_iota(jnp.int32, (ATT_BLK, ATT_BLK), 1)
        causal = keys <= qrys

        def q_block(i, _):
            def tile(j, carry, masked):
                ks = pl.ds(pl.multiple_of(j * ATT_BLK, ATT_BLK), ATT_BLK)
                s_all = [jnp.dot(k_ref[hh, ks, :], q_ref[hh, i], preferred_element_type=F32) for hh in range(HB)]
                stats, p_all = [], []
                for hh in range(HB):
                    m, l, _ = carry[hh]
                    s = jnp.where(causal, s_all[hh], NEG) if masked else s_all[hh]
                    m_new = jnp.maximum(m, jnp.max(s, axis=0, keepdims=True))
                    p = jnp.exp(s - m_new)
                    alpha = jnp.exp(m - m_new)
                    stats.append((m_new, alpha * l + jnp.sum(p, axis=0, keepdims=True), alpha))
                    p_all.append(p.astype(BF16))
                out = []
                for hh in range(HB):
                    m_new, l, alpha = stats[hh]
                    acc = alpha * carry[hh][2] + jnp.dot(v_ref[hh, j], p_all[hh], preferred_element_type=F32)
                    out.append((m_new, l, acc))
                return tuple(out)

            init = tuple((jnp.full((1, ATT_BLK), NEG, F32), jnp.zeros((1, ATT_BLK), F32),
                          jnp.zeros((Dh, ATT_BLK), F32)) for _ in range(HB))
            carry = lax.fori_loop(0, i, lambda j, cr: tile(j, cr, False), init)
            carry = tile(i, carry, True)
            for hh in range(HB):
                m, l, acc = carry[hh]
                o_ref[hh, i] = acc / l
                lse_ref[hh, i] = m + jnp.log(l)
            return 0

        lax.fori_loop(0, nb, q_block, 0)
        if n_comm:
            pl.when(pl.program_id(0) == Hh // HB - 1)(wait)

    blk = lambda r: pl.BlockSpec((HB, nb, r, ATT_BLK), lambda h: (h, 0, 0, 0))
    out_shape = (jax.ShapeDtypeStruct((Hh, nb, Dh, ATT_BLK), F32), jax.ShapeDtypeStruct((Hh, nb, 1, ATT_BLK), F32))
    scratch = []
    args = [qT, k_aug, vT]
    if n_comm:
        out_shape += tuple(jax.ShapeDtypeStruct(cut.full, s.dtype) for cut, s in zip(comm[1], comm[0]))
        scratch = [pltpu.SemaphoreType.DMA((4 * n_comm,)), pltpu.SemaphoreType.DMA((4 * n_comm,))]
        args += list(comm[0])
    return pl.pallas_call(
        body, name=name, out_shape=out_shape, grid=(Hh // HB,),
        in_specs=[blk(AUG), pl.BlockSpec((HB, LP, AUG), lambda h: (h, 0, 0)), blk(Dh)] + [ANY] * n_comm,
        out_specs=(blk(Dh), blk(1)) + (ANY,) * n_comm, scratch_shapes=scratch,
        compiler_params=_params(("arbitrary",)))(*args)


def _attn_bwd2(qT, kT, k_aug, v, oT, doT, lse, name, comm=None):
    Hh, nb, _, _ = qT.shape
    LP = nb * ATT_BLK
    Dh = v.shape[2]
    nt = (((1,), (1,)), ((), ()))
    tn = (((0,), (0,)), ((), ()))

    HB = ATT_HEADS_PER_STEP_BWD
    n_comm = 0 if comm is None else len(comm[0])

    def body(*refs):
        q_ref, kt_ref, k_ref, v_ref, o_ref, do_ref, lse_ref = refs[:7]
        parts = refs[7:7 + n_comm]
        dq_ref, dk_ref, dv_ref = refs[7 + n_comm:10 + n_comm]
        others = refs[10 + n_comm:10 + 2 * n_comm]
        delta_ref = refs[10 + 2 * n_comm]
        if n_comm:
            start, wait = _chip_exchange_ops(parts, others, refs[11 + 2 * n_comm], refs[12 + 2 * n_comm], comm[1])
            pl.when(pl.program_id(0) == 0)(start)
        keys = lax.broadcasted_iota(jnp.int32, (ATT_BLK, ATT_BLK), 0)
        qrys = lax.broadcasted_iota(jnp.int32, (ATT_BLK, ATT_BLK), 1)
        causal = keys <= qrys

        def prep(i, _):
            for hh in range(HB):
                delta_ref[hh, i] = jnp.sum(do_ref[hh, i].astype(F32) * o_ref[hh, i], axis=0, keepdims=True)
                dq_ref[hh, i] = jnp.zeros((AUG, ATT_BLK), F32)
            return 0

        lax.fori_loop(0, nb, prep, 0)

        def kv_block(j, _):
            ks = pl.ds(pl.multiple_of(j * ATT_BLK, ATT_BLK), ATT_BLK)

            def tile(i, carry, masked):
                s_all = [jnp.dot(k_ref[hh, ks, :], q_ref[hh, i], preferred_element_type=F32) for hh in range(HB)]
                dp_all = [lax.dot_general(v_ref[hh, j], do_ref[hh, i], tn, preferred_element_type=F32)
                          for hh in range(HB)]
                p_all, ds_all = [], []
                for hh in range(HB):
                    s = jnp.where(causal, s_all[hh], NEG) if masked else s_all[hh]
                    p = jnp.exp(s - lse_ref[hh, i])
                    ds_all.append((p * (dp_all[hh] - delta_ref[hh, i])).astype(BF16))
                    p_all.append(p.astype(BF16))
                out = []
                for hh in range(HB):
                    dk, dv = carry[hh]
                    dv = dv + lax.dot_general(do_ref[hh, i], p_all[hh], nt, preferred_element_type=F32)
                    dk = dk + lax.dot_general(q_ref[hh, i], ds_all[hh], nt, preferred_element_type=F32)
                    out.append((dk, dv))
                dq_new = [jnp.dot(kt_ref[hh, j], ds_all[hh], preferred_element_type=F32) for hh in range(HB)]
                for hh in range(HB):
                    dq_ref[hh, i] += dq_new[hh]
                return tuple(out)

            init = tuple((jnp.zeros((AUG, ATT_BLK), F32), jnp.zeros((Dh, ATT_BLK), F32)) for _ in range(HB))
            carry = tile(j, init, True)
            carry = lax.fori_loop(j + 1, nb, lambda i, cr: tile(i, cr, False), carry)
            for hh in range(HB):
                dk_ref[hh, j] = carry[hh][0]
                dv_ref[hh, j] = carry[hh][1]
            return 0

        lax.fori_loop(0, nb, kv_block, 0)
        if n_comm:
            pl.when(pl.program_id(0) == Hh // HB - 1)(wait)

    blk = lambda r: pl.BlockSpec((HB, nb, r, ATT_BLK), lambda h: (h, 0, 0, 0))
    row = lambda cols: pl.BlockSpec((HB, LP, cols), lambda h: (h, 0, 0))
    out_shape = (jax.ShapeDtypeStruct((Hh, nb, AUG, ATT_BLK), F32), jax.ShapeDtypeStruct((Hh, nb, AUG, ATT_BLK), F32),
                 jax.ShapeDtypeStruct((Hh, nb, Dh, ATT_BLK), F32))
    scratch = [pltpu.VMEM((HB, nb, 1, ATT_BLK), F32)]
    args = [qT, kT, k_aug, v, oT, doT, lse]
    if n_comm:
        out_shape += _chip_exchange_shapes(*comm)
        scratch += [pltpu.SemaphoreType.DMA((3 * n_comm,)), pltpu.SemaphoreType.DMA((3 * n_comm,))]
        args += list(comm[0])
    return pl.pallas_call(
        body, name=name, out_shape=out_shape, grid=(Hh // HB,),
        in_specs=[blk(AUG), blk(AUG), row(AUG), blk(Dh), blk(Dh), blk(Dh), blk(1)] + [ANY] * n_comm,
        out_specs=(blk(AUG), blk(AUG), blk(Dh)) + (ANY,) * n_comm,
        scratch_shapes=scratch,
        compiler_params=_params(("arbitrary",)))(*args)


CONV_HALO = 32
A_BLK = 3 * FOX_W // CONV_CH
G_BLK = A_BLK + 1


def _conf_fwd(proj, cw, cb, lg, lb, tm, name):
    LP = proj.shape[0]
    C = CONV_CH
    sub = _sub_rows(tm)
    hpb = tm // CONV_HALO

    def body(a_ref, g_ref, ah_ref, gh_ref, w_ref, cb_ref, lg_ref, lb_ref, u1_ref, u_ref, buf):
        r = pl.program_id(0)
        buf[CONV_HALO:CONV_HALO + tm, :] = a_ref[...] * _sigmoid(g_ref[...])
        buf[0:CONV_HALO, :] = jnp.where(r > 0, ah_ref[...] * _sigmoid(gh_ref[...]), 0.0)
        for s in range(tm // sub):
            for ct in range(C // LANE):
                ln = slice(ct * LANE, (ct + 1) * LANE)
                acc = jnp.broadcast_to(cb_ref[:, ln], (sub, LANE))
                for kk in range(CONV_K):
                    off = CONV_HALO + s * sub - (CONV_K - 1) + kk
                    acc = acc + w_ref[kk:kk + 1, ln] * buf[off:off + sub, ln]
                u1_ref[s * sub:(s + 1) * sub, ln] = acc
        u1 = u1_ref[...]
        mu = jnp.mean(u1, axis=1, keepdims=True)
        xc = u1 - mu
        var = jnp.mean(xc * xc, axis=1, keepdims=True)
        y = xc * lax.rsqrt(var + LN_EPS) * lg_ref[...] + lb_ref[...]
        u_ref[...] = (y * _sigmoid(y)).astype(u_ref.dtype)

    cur = lambda blk: pl.BlockSpec((tm, C), lambda r: (r, blk))
    halo = lambda blk: pl.BlockSpec((CONV_HALO, C), lambda r: (jnp.maximum(r * hpb - 1, 0), blk))
    vec = pl.BlockSpec((1, C), lambda r: (0, 0))
    out = pl.BlockSpec((tm, C), lambda r: (r, 0))
    return pl.pallas_call(
        body, name=name,
        out_shape=(jax.ShapeDtypeStruct((LP, C), F32), jax.ShapeDtypeStruct((LP, C), BF16)),
        grid=(LP // tm,),
        in_specs=[cur(A_BLK), cur(G_BLK), halo(A_BLK), halo(G_BLK),
                  pl.BlockSpec((CONV_HALO, C), lambda r: (0, 0)), vec, vec, vec],
        out_specs=(out, out),
        scratch_shapes=[pltpu.VMEM((CONV_HALO + tm, C), F32)],
        compiler_params=_params(("parallel",)))(proj, proj, proj, proj, cw, cb, lg, lb)


def _conf_bwd(proj, u1, dcat, cw, lg, lb, tm, name, host=None):
    LP = proj.shape[0]
    C = CONV_CH
    sub = _sub_rows(tm)
    hpb = tm // CONV_HALO
    nblk = LP // tm
    last_halo = LP // CONV_HALO - 1

    n_host = 0 if host is None else len(host[0])
    if n_host:
        host_shapes, host_sems, host_ops = _host_plan(host)

    def body(*refs):
        a_ref, g_ref, ah_ref, gh_ref, u1_ref, u1n_ref, du_ref, dun_ref, w_ref, lg_ref, lb_ref = refs[:11]
        dadg_ref, dw_ref, dcb_ref, dlg_ref, dlb_ref = refs[11 + n_host:16 + n_host]
        ubuf, dbuf, du0 = refs[16 + 2 * n_host:19 + 2 * n_host]
        if n_host:
            start, wait = host_ops(refs[11:11 + n_host], refs[16 + n_host:16 + 2 * n_host],
                                   refs[19 + 2 * n_host], refs[20 + 2 * n_host], host[1])
            pl.when(pl.program_id(0) == 0)(start)
        r = pl.program_id(0)
        lgv = lg_ref[...]
        lbv = lb_ref[...]

        def ln_silu_bwd(u1v, duv):
            mu = jnp.mean(u1v, axis=1, keepdims=True)
            xc = u1v - mu
            rstd = lax.rsqrt(jnp.mean(xc * xc, axis=1, keepdims=True) + LN_EPS)
            xhat = xc * rstd
            y = xhat * lgv + lbv
            sg = _sigmoid(y)
            dy = duv * (sg * (1.0 + y * (1.0 - sg)))
            dxh = dy * lgv
            du1 = rstd * (dxh - jnp.mean(dxh, axis=1, keepdims=True)
                          - xhat * jnp.mean(dxh * xhat, axis=1, keepdims=True))
            return du1, dy, xhat

        @pl.when(r == 0)
        def _():
            dw_ref[...] = jnp.zeros_like(dw_ref)
            dcb_ref[...] = jnp.zeros_like(dcb_ref)
            dlg_ref[...] = jnp.zeros_like(dlg_ref)
            dlb_ref[...] = jnp.zeros_like(dlb_ref)

        du1, dy, xhat = ln_silu_bwd(u1_ref[...], du_ref[...])
        dlg_ref[...] += jnp.sum(dy * xhat, axis=0, keepdims=True)
        dlb_ref[...] += jnp.sum(dy, axis=0, keepdims=True)
        dcb_ref[...] += jnp.sum(du1, axis=0, keepdims=True)
        dbuf[0:tm, :] = du1
        du1n, _, _ = ln_silu_bwd(u1n_ref[...], dun_ref[...])
        dbuf[tm:tm + CONV_HALO, :] = jnp.where(r < nblk - 1, du1n, 0.0)
        ubuf[CONV_HALO:CONV_HALO + tm, :] = a_ref[...] * _sigmoid(g_ref[...])
        ubuf[0:CONV_HALO, :] = jnp.where(r > 0, ah_ref[...] * _sigmoid(gh_ref[...]), 0.0)

        for ct in range(C // LANE):
            ln = slice(ct * LANE, (ct + 1) * LANE)
            for s in range(tm // sub):
                d_here = dbuf[s * sub:(s + 1) * sub, ln]
                acc = jnp.zeros((sub, LANE), F32)
                for kk in range(CONV_K):
                    fo = s * sub + (CONV_K - 1) - kk
                    acc = acc + w_ref[kk:kk + 1, ln] * dbuf[fo:fo + sub, ln]
                    bo = CONV_HALO + s * sub - (CONV_K - 1) + kk
                    dw_ref[kk:kk + 1, ln] += jnp.sum(d_here * ubuf[bo:bo + sub, ln], axis=0, keepdims=True)
                du0[s * sub:(s + 1) * sub, ln] = acc
        a = a_ref[...]
        sg = _sigmoid(g_ref[...])
        d0 = du0[...]
        dadg_ref[:, 0:C] = (d0 * sg).astype(dadg_ref.dtype)
        dadg_ref[:, C:2 * C] = (d0 * a * sg * (1.0 - sg)).astype(dadg_ref.dtype)
        if n_host:
            pl.when(pl.program_id(0) == nblk - 1)(wait)

    cur = lambda blk: pl.BlockSpec((tm, C), lambda r: (r, blk))
    prev = lambda blk: pl.BlockSpec((CONV_HALO, C), lambda r: (jnp.maximum(r * hpb - 1, 0), blk))
    nxt = lambda blk: pl.BlockSpec((CONV_HALO, C), lambda r: (jnp.minimum((r + 1) * hpb, last_halo), blk))
    vec = pl.BlockSpec((1, C), lambda r: (0, 0))
    wspec = pl.BlockSpec((CONV_HALO, C), lambda r: (0, 0))
    out_shape = (jax.ShapeDtypeStruct((LP, 2 * C), BF16), jax.ShapeDtypeStruct((CONV_HALO, C), F32),
                 jax.ShapeDtypeStruct((1, C), F32), jax.ShapeDtypeStruct((1, C), F32),
                 jax.ShapeDtypeStruct((1, C), F32))
    out_specs = (pl.BlockSpec((tm, 2 * C), lambda r: (r, 0)), wspec, vec, vec, vec)
    in_specs = [cur(A_BLK), cur(G_BLK), prev(A_BLK), prev(G_BLK), cur(0), nxt(0), cur(1), nxt(1), wspec, vec, vec]
    args = [proj, proj, proj, proj, u1, u1, dcat, dcat, cw, lg, lb]
    scratch = [pltpu.VMEM((CONV_HALO + tm, C), F32), pltpu.VMEM((tm + CONV_HALO, C), F32),
               pltpu.VMEM((tm, C), F32)]
    if n_host:
        anyspec = pl.BlockSpec(memory_space=pl.ANY)
        in_specs += [anyspec] * n_host
        args += list(host[0])
        out_shape += host_shapes
        out_specs += (anyspec,) * n_host
        scratch += [pltpu.SemaphoreType.DMA((host_sems,)), pltpu.SemaphoreType.DMA((host_sems,))]
    return pl.pallas_call(
        body, name=name, out_shape=out_shape, grid=(nblk,), in_specs=in_specs, out_specs=out_specs,
        scratch_shapes=scratch, compiler_params=_params(("arbitrary",)))(*args)


FFN_HALO = 8
FFN_TC = 1408
FFN_ROW_SPLIT = 2
FFN_K = 3


def _ffn_conv(buf, w_ref, b_ref, s, sub, ln):
    acc = jnp.broadcast_to(b_ref[:, ln], (sub, LANE))
    for kk in range(FFN_K):
        off = FFN_HALO + s * sub - (FFN_K - 1) + kk
        acc = acc + w_ref[kk:kk + 1, ln] * buf[off:off + sub, ln]
    return acc


def _ffn_act_fwd(up, w, b, tm, name, host=None):
    LP, F = up.shape[0], up.shape[1] // 2
    tm = tm // FFN_ROW_SPLIT
    upg = upv = up
    nct = F // FFN_TC
    sub = _sub_rows(tm)
    hpb = tm // FFN_HALO
    n_host = 0 if host is None else len(host[0])
    nrb = LP // tm

    def body(*refs):
        g_ref, v_ref, gh_ref, vh_ref, wg_ref, wv_ref, bg_ref, bv_ref = refs[:8]
        act_ref = refs[8 + n_host]
        gbuf, vbuf = refs[9 + 2 * n_host:11 + 2 * n_host]
        if n_host:
            start, wait = _gather_first_ops(refs[8:8 + n_host], refs[9 + n_host:9 + 2 * n_host],
                                            refs[11 + 2 * n_host], refs[12 + 2 * n_host], host[1])
            pl.when(jnp.logical_and(pl.program_id(0) == 0, pl.program_id(1) == 0))(start)
        r = pl.program_id(1)
        gbuf[FFN_HALO:FFN_HALO + tm, :] = g_ref[...]
        vbuf[FFN_HALO:FFN_HALO + tm, :] = v_ref[...]
        gbuf[0:FFN_HALO, :] = jnp.where(r > 0, gh_ref[...], 0.0)
        vbuf[0:FFN_HALO, :] = jnp.where(r > 0, vh_ref[...], 0.0)
        for s in range(tm // sub):
            for ct in range(FFN_TC // LANE):
                ln = slice(ct * LANE, (ct + 1) * LANE)
                gc = _ffn_conv(gbuf, wg_ref, bg_ref, s, sub, ln)
                vc = _ffn_conv(vbuf, wv_ref, bv_ref, s, sub, ln)
                act_ref[s * sub:(s + 1) * sub, ln] = (gc * _sigmoid(gc) * vc).astype(act_ref.dtype)
        if n_host:
            pl.when(jnp.logical_and(pl.program_id(0) == nct - 1, pl.program_id(1) == nrb - 1))(wait)

    cur = pl.BlockSpec((tm, FFN_TC), lambda c, r: (r, c))
    halo = pl.BlockSpec((FFN_HALO, FFN_TC), lambda c, r: (jnp.maximum(r * hpb - 1, 0), c))
    wg = pl.BlockSpec((8, FFN_TC), lambda c, r: (0, c))
    wv = pl.BlockSpec((8, FFN_TC), lambda c, r: (0, nct + c))
    bg = pl.BlockSpec((1, FFN_TC), lambda c, r: (0, c))
    bv = pl.BlockSpec((1, FFN_TC), lambda c, r: (0, nct + c))
    curv = pl.BlockSpec((tm, FFN_TC), lambda c, r: (r, nct + c))
    halov = pl.BlockSpec((FFN_HALO, FFN_TC), lambda c, r: (jnp.maximum(r * hpb - 1, 0), nct + c))
    out_shape = jax.ShapeDtypeStruct((LP, F), BF16)
    out_specs = cur
    in_specs = [cur, curv, halo, halov, wg, wv, bg, bv]
    args = [upg, upv, upg, upv, w, w, b, b]
    scratch = [pltpu.VMEM((FFN_HALO + tm, FFN_TC), F32)] * 2
    if n_host:
        anyspec = pl.BlockSpec(memory_space=pl.ANY)
        in_specs += [anyspec] * n_host
        args += list(host[0])
        out_shape = (out_shape,) + tuple(jax.ShapeDtypeStruct(cut.full, s.dtype) for cut, s in zip(host[1], host[0]))
        out_specs = (cur,) + (anyspec,) * n_host
        scratch += [pltpu.SemaphoreType.DMA((4 * n_host,)), pltpu.SemaphoreType.DMA((4 * n_host,))]
    sem = ("arbitrary", "arbitrary") if n_host else ("parallel", "parallel")
    return pl.pallas_call(
        body, name=name, out_shape=out_shape, grid=(nct, nrb), in_specs=in_specs, out_specs=out_specs,
        scratch_shapes=scratch, compiler_params=_params(sem))(*args)


def _ffn_act_bwd(up, dact, w, b, tm, name, comm=None):
    LP, F = up.shape[0], up.shape[1] // 2
    tm = tm // FFN_ROW_SPLIT
    upg = upv = up
    nct = F // FFN_TC
    sub = _sub_rows(tm)
    hpb = tm // FFN_HALO
    nblk = LP // tm
    last_halo = LP // FFN_HALO - 1
    TB = tm + 2 * FFN_HALO
    n_comm = 0 if comm is None else len(comm[0])

    def body(*refs):
        (g_ref, v_ref, gp_ref, vp_ref, gn_ref, vn_ref, da_ref, dan_ref,
         wg_ref, wv_ref, bg_ref, bv_ref) = refs[:12]
        dup_ref, dwg_ref, dwv_ref, dbg_ref, dbv_ref = refs[12 + n_comm:17 + n_comm]
        gbuf, vbuf, dgb, dvb = refs[17 + 2 * n_comm:21 + 2 * n_comm]
        if n_comm:
            start, wait = _chip_exchange_ops(refs[12:12 + n_comm], refs[17 + n_comm:17 + 2 * n_comm],
                                             refs[21 + 2 * n_comm], refs[22 + 2 * n_comm], comm[1])
            pl.when(jnp.logical_and(pl.program_id(0) == 0, pl.program_id(1) == 0))(start)
        r = pl.program_id(1)
        dg_ref = dup_ref.at[0]
        dv_ref = dup_ref.at[1]
        first = r == 0
        last = r == nblk - 1

        @pl.when(first)
        def _():
            dwg_ref[...] = jnp.zeros_like(dwg_ref)
            dwv_ref[...] = jnp.zeros_like(dwv_ref)
            dbg_ref[...] = jnp.zeros_like(dbg_ref)
            dbv_ref[...] = jnp.zeros_like(dbv_ref)

        for buf, c_ref, p_ref, n_ref in ((gbuf, g_ref, gp_ref, gn_ref), (vbuf, v_ref, vp_ref, vn_ref)):
            buf[0:FFN_HALO, :] = jnp.where(first, 0.0, p_ref[...])
            buf[FFN_HALO:FFN_HALO + tm, :] = c_ref[...]
            buf[FFN_HALO + tm:TB, :] = jnp.where(last, 0.0, n_ref[...])

        def dconv(s0, nrows, ln, dact_v):
            xg = [gbuf[s0 - (FFN_K - 1) + kk:s0 - (FFN_K - 1) + kk + nrows, ln] for kk in range(FFN_K)]
            xv = [vbuf[s0 - (FFN_K - 1) + kk:s0 - (FFN_K - 1) + kk + nrows, ln] for kk in range(FFN_K)]
            gc = jnp.broadcast_to(bg_ref[:, ln], (nrows, LANE))
            vc = jnp.broadcast_to(bv_ref[:, ln], (nrows, LANE))
            for kk in range(FFN_K):
                gc = gc + wg_ref[kk:kk + 1, ln] * xg[kk]
                vc = vc + wv_ref[kk:kk + 1, ln] * xv[kk]
            sg = _sigmoid(gc)
            return dact_v * vc * (sg * (1.0 + gc * (1.0 - sg))), dact_v * (gc * sg), xg, xv

        colsum = lambda t: jnp.sum(t, axis=0, keepdims=True)
        for ct in range(FFN_TC // LANE):
            ln = slice(ct * LANE, (ct + 1) * LANE)
            zero = jnp.zeros((1, LANE), F32)
            dwg, dwv, dbg, dbv = [zero] * FFN_K, [zero] * FFN_K, zero, zero
            for s in range(tm // sub):
                dgc, dvc, xg, xv = dconv(FFN_HALO + s * sub, sub, ln, da_ref[s * sub:(s + 1) * sub, ln])
                dgb[s * sub:(s + 1) * sub, ln] = dgc
                dvb[s * sub:(s + 1) * sub, ln] = dvc
                dwg = [dwg[kk] + colsum(dgc * xg[kk]) for kk in range(FFN_K)]
                dwv = [dwv[kk] + colsum(dvc * xv[kk]) for kk in range(FFN_K)]
                dbg, dbv = dbg + colsum(dgc), dbv + colsum(dvc)
            for kk in range(FFN_K):
                dwg_ref[kk:kk + 1, ln] += dwg[kk]
                dwv_ref[kk:kk + 1, ln] += dwv[kk]
            dbg_ref[:, ln] += dbg
            dbv_ref[:, ln] += dbv
            dgc, dvc, _, _ = dconv(FFN_HALO + tm, FFN_HALO, ln, jnp.where(last, 0.0, dan_ref[:, ln]))
            dgb[tm:tm + FFN_HALO, ln] = dgc
            dvb[tm:tm + FFN_HALO, ln] = dvc
            for dbuf, w_ref, dout in ((dgb, wg_ref, dg_ref), (dvb, wv_ref, dv_ref)):
                for s in range(tm // sub):
                    acc = jnp.zeros((sub, LANE), F32)
                    for kk in range(FFN_K):
                        fo = s * sub + (FFN_K - 1) - kk
                        acc = acc + w_ref[kk:kk + 1, ln] * dbuf[fo:fo + sub, ln]
                    dout[s * sub:(s + 1) * sub, ln] = acc.astype(dout.dtype)
        if n_comm:
            pl.when(jnp.logical_and(pl.program_id(0) == nct - 1, pl.program_id(1) == nblk - 1))(wait)

    cur = pl.BlockSpec((tm, FFN_TC), lambda c, r: (r, c))
    prev = pl.BlockSpec((FFN_HALO, FFN_TC), lambda c, r: (jnp.maximum(r * hpb - 1, 0), c))
    nxt = pl.BlockSpec((FFN_HALO, FFN_TC), lambda c, r: (jnp.minimum((r + 1) * hpb, last_halo), c))
    wg = pl.BlockSpec((8, FFN_TC), lambda c, r: (0, c))
    wv = pl.BlockSpec((8, FFN_TC), lambda c, r: (0, nct + c))
    bg = pl.BlockSpec((1, FFN_TC), lambda c, r: (0, c))
    bv = pl.BlockSpec((1, FFN_TC), lambda c, r: (0, nct + c))
    curv = pl.BlockSpec((tm, FFN_TC), lambda c, r: (r, nct + c))
    prevv = pl.BlockSpec((FFN_HALO, FFN_TC), lambda c, r: (jnp.maximum(r * hpb - 1, 0), nct + c))
    nxtv = pl.BlockSpec((FFN_HALO, FFN_TC), lambda c, r: (jnp.minimum((r + 1) * hpb, last_halo), nct + c))
    out_shape = (jax.ShapeDtypeStruct((2, LP, F), BF16),
                 jax.ShapeDtypeStruct((8, F), F32), jax.ShapeDtypeStruct((8, F), F32),
                 jax.ShapeDtypeStruct((1, F), F32), jax.ShapeDtypeStruct((1, F), F32))
    out_specs = (pl.BlockSpec((2, tm, FFN_TC), lambda c, r: (0, r, c)),
                 pl.BlockSpec((8, FFN_TC), lambda c, r: (0, c)),
                 pl.BlockSpec((8, FFN_TC), lambda c, r: (0, c)),
                 pl.BlockSpec((1, FFN_TC), lambda c, r: (0, c)),
                 pl.BlockSpec((1, FFN_TC), lambda c, r: (0, c)))
    in_specs = [cur, curv, prev, prevv, nxt, nxtv, cur, nxt, wg, wv, bg, bv]
    args = [upg, upv, upg, upv, upg, upv, dact, dact, w, w, b, b]
    scratch = [pltpu.VMEM((TB, FFN_TC), F32), pltpu.VMEM((TB, FFN_TC), F32),
               pltpu.VMEM((tm + FFN_HALO, FFN_TC), F32), pltpu.VMEM((tm + FFN_HALO, FFN_TC), F32)]
    if n_comm:
        anyspec = pl.BlockSpec(memory_space=pl.ANY)
        in_specs += [anyspec] * n_comm
        args += list(comm[0])
        out_shape += _chip_exchange_shapes(*comm)
        out_specs += (anyspec,) * n_comm
        scratch += [pltpu.SemaphoreType.DMA((3 * n_comm,)), pltpu.SemaphoreType.DMA((3 * n_comm,))]
    sem = ("arbitrary", "arbitrary") if n_comm else ("parallel", "arbitrary")
    dup, dwg, dwv, dbg, dbv, *others = pl.pallas_call(
        body, name=name, out_shape=out_shape, grid=(nct, nblk), in_specs=in_specs, out_specs=out_specs,
        scratch_shapes=scratch, compiler_params=_params(sem))(*args)
    return (dup, jnp.concatenate([dwg, dwv], axis=1), jnp.concatenate([dbg, dbv], axis=1)) + tuple(others)


POOL_HALO = 16


def _pool_fwd(h, g, pw, pb, ps, tm, name):
    LP, Dm = h.shape
    sub = _sub_rows(tm)
    hpb = tm // POOL_HALO

    def body(h_ref, hh_ref, g_ref, pw_ref, pb_ref, ps_ref, o_ref, d_ref, buf):
        r = pl.program_id(0)
        gg = g_ref[...]

        def norm(x):
            return x * lax.rsqrt(jnp.mean(x * x, axis=1, keepdims=True) + RMS_EPS) * gg

        x = h_ref[...]
        buf[POOL_HALO:POOL_HALO + tm, :] = norm(x)
        buf[0:POOL_HALO, :] = jnp.where(r > 0, norm(hh_ref[...]), 0.0)
        for gi, w in enumerate(POOL_WINDOWS):
            ln = slice(gi * POOL_G, (gi + 1) * POOL_G)
            for s in range(tm // sub):
                base = POOL_HALO + s * sub
                acc = buf[base:base + sub, ln]
                for jj in range(1, w):
                    acc = acc + buf[base - jj:base - jj + sub, ln]
                t = r * tm + s * sub + lax.broadcasted_iota(jnp.int32, (sub, 1), 0)
                cnt = jnp.minimum(t + 1, w).astype(F32)
                d_ref[s * sub:(s + 1) * sub, ln] = (acc / cnt - buf[base:base + sub, ln]).astype(d_ref.dtype)
            y = jnp.dot(d_ref[:, ln], pw_ref[gi], preferred_element_type=F32) + pb_ref[:, ln]
            o_ref[:, ln] = x[:, ln] + y * ps_ref[:, ln]

    row = pl.BlockSpec((tm, Dm), lambda r: (r, 0))
    halo = pl.BlockSpec((POOL_HALO, Dm), lambda r: (jnp.maximum(r * hpb - 1, 0), 0))
    vec = pl.BlockSpec((1, Dm), lambda r: (0, 0))
    wsp = pl.BlockSpec((len(POOL_WINDOWS), POOL_G, POOL_G), lambda r: (0, 0, 0))
    return pl.pallas_call(
        body, name=name,
        out_shape=(jax.ShapeDtypeStruct((LP, Dm), F32), jax.ShapeDtypeStruct((LP, Dm), BF16)),
        grid=(LP // tm,), in_specs=[row, halo, vec, wsp, vec, vec], out_specs=(row, row),
        scratch_shapes=[pltpu.VMEM((POOL_HALO + tm, Dm), F32)],
        compiler_params=_params(("parallel",)))(h, h, g, pw, pb, ps)


def _pool_bwd(h, g, d, pw, pb, ps, dh_out, tm, name):
    LP, Dm = h.shape
    sub = _sub_rows(tm)
    hpb = tm // POOL_HALO
    nblk = LP // tm
    last_halo = LP // POOL_HALO - 1
    nt = (((1,), (1,)), ((), ()))
    tn = (((0,), (0,)), ((), ()))

    def body(h_ref, g_ref, d_ref, pw_ref, pb_ref, ps_ref, do_ref, don_ref,
             dh_ref, dpw_ref, dpb_ref, dps_ref, dg_ref, ebuf, ddb, dnb):
        r = pl.program_id(0)

        @pl.when(r == 0)
        def _():
            dpw_ref[...] = jnp.zeros_like(dpw_ref)
            dpb_ref[...] = jnp.zeros_like(dpb_ref)
            dps_ref[...] = jnp.zeros_like(dps_ref)
            dg_ref[...] = jnp.zeros_like(dg_ref)

        for gi, w in enumerate(POOL_WINDOWS):
            ln = slice(gi * POOL_G, (gi + 1) * POOL_G)
            wg = pw_ref[gi]
            dog = do_ref[:, ln]
            dg_b = d_ref[:, ln]
            y_pre = jnp.dot(dg_b, wg, preferred_element_type=F32) + pb_ref[:, ln]
            dps_ref[:, ln] += jnp.sum(dog * y_pre, axis=0, keepdims=True)
            dy = dog * ps_ref[:, ln]
            dpb_ref[:, ln] += jnp.sum(dy, axis=0, keepdims=True)
            dyb = dy.astype(BF16)
            dpw_ref[gi] += lax.dot_general(dg_b, dyb, tn, preferred_element_type=F32)
            dd = lax.dot_general(dyb, wg, nt, preferred_element_type=F32)
            ddb[:, ln] = dd
            t = r * tm + lax.broadcasted_iota(jnp.int32, (tm, 1), 0)
            ebuf[0:tm, ln] = dd / jnp.minimum(t + 1, w).astype(F32)
            dyn = (don_ref[:, ln] * ps_ref[:, ln]).astype(BF16)
            ddn = lax.dot_general(dyn, wg, nt, preferred_element_type=F32)
            tn_ = (r + 1) * tm + lax.broadcasted_iota(jnp.int32, (POOL_HALO, 1), 0)
            ebuf[tm:tm + POOL_HALO, ln] = jnp.where(r < nblk - 1, ddn / jnp.minimum(tn_ + 1, w).astype(F32), 0.0)
            for s in range(tm // sub):
                acc = ebuf[s * sub:(s + 1) * sub, ln]
                for jj in range(1, w):
                    acc = acc + ebuf[s * sub + jj:s * sub + jj + sub, ln]
                dnb[s * sub:(s + 1) * sub, ln] = acc - ddb[s * sub:(s + 1) * sub, ln]
        x = h_ref[...]
        rr = lax.rsqrt(jnp.mean(x * x, axis=1, keepdims=True) + RMS_EPS)
        xhat = x * rr
        dn = dnb[...]
        dxh = dn * g_ref[...]
        dh_ref[...] = do_ref[...] + rr * (dxh - xhat * jnp.mean(dxh * xhat, axis=1, keepdims=True))
        dg_ref[...] += jnp.sum(dn * xhat, axis=0, keepdims=True)

    row = pl.BlockSpec((tm, Dm), lambda r: (r, 0))
    nxt = pl.BlockSpec((POOL_HALO, Dm), lambda r: (jnp.minimum((r + 1) * hpb, last_halo), 0))
    vec = pl.BlockSpec((1, Dm), lambda r: (0, 0))
    wsp = pl.BlockSpec((len(POOL_WINDOWS), POOL_G, POOL_G), lambda r: (0, 0, 0))
    return pl.pallas_call(
        body, name=name,
        out_shape=(jax.ShapeDtypeStruct((LP, Dm), F32),
                   jax.ShapeDtypeStruct((len(POOL_WINDOWS), POOL_G, POOL_G), F32),
                   jax.ShapeDtypeStruct((1, Dm), F32), jax.ShapeDtypeStruct((1, Dm), F32),
                   jax.ShapeDtypeStruct((1, Dm), F32)),
        grid=(nblk,), in_specs=[row, vec, row, wsp, vec, vec, row, nxt],
        out_specs=(row, wsp, vec, vec, vec),
        scratch_shapes=[pltpu.VMEM((tm + POOL_HALO, Dm), F32), pltpu.VMEM((tm, Dm), F32),
                        pltpu.VMEM((tm, Dm), F32)],
        compiler_params=_params(("arbitrary",)))(h, g, d, pw, pb, ps, dh_out, dh_out)


def _adamw(w, g, m, v, name):
    shape = w.shape
    cols = shape[-1]
    rows = int(np.prod(shape[:-1])) if len(shape) > 1 else 1
    w2, g2, m2, v2 = (t.reshape(rows, cols) for t in (w, g, m, v))
    tr = rows
    for cand in (256, 128, 64, 32, 16, 8):
        if rows % cand == 0 and rows > cand:
            tr = cand
            break
    c1 = float(1.0 - ADAM_B1 ** ADAM_STEP)
    c2 = float(1.0 - ADAM_B2 ** ADAM_STEP)

    def body(w_ref, g_ref, m_ref, v_ref, d_ref, mo_ref, vo_ref):
        gg = g_ref[...]
        mn = ADAM_B1 * m_ref[...] + (1.0 - ADAM_B1) * gg
        vn = ADAM_B2 * v_ref[...] + (1.0 - ADAM_B2) * (gg * gg)
        m_hat = mn / c1
        v_hat = vn / c2
        d_ref[...] = -ADAM_LR * (m_hat / (jnp.sqrt(v_hat) + ADAM_EPS) + ADAM_WD * w_ref[...])
        mo_ref[...] = mn
        vo_ref[...] = vn

    spec = pl.BlockSpec((tr, cols), lambda i: (i, 0))
    sds = jax.ShapeDtypeStruct((rows, cols), F32)
    d2, mo, vo = pl.pallas_call(
        body, name=name, out_shape=(sds, sds, sds), grid=(rows // tr,),
        in_specs=[spec] * 4, out_specs=(spec,) * 3,
        compiler_params=_params(("parallel",)))(w2, g2, m2, v2)
    return d2.reshape(shape), mo.reshape(shape), vo.reshape(shape)


def _row_tiles(LP):
    tm = LP // 4
    assert LP % 4 == 0 and tm % CONV_HALO == 0 and LP % ATT_BLK == 0, LP
    return tm, LP // 2


MESH = pl.DeviceIdType.MESH
ANY = pl.BlockSpec(memory_space=pl.ANY)


def _coords():
    return lax.axis_index("x"), lax.axis_index("y"), lax.axis_index("c")


def _other_chips(x, y):
    return [(1 - x, y), (x, 1 - y), (1 - x, 1 - y)]


def _allreduce_small(pack):
    Rs, C = pack.shape
    n_dev = 8

    def body(x_ref, o_ref, buf, send_sems, recv_sems):
        x, y, c = _coords()
        me = 4 * x + 2 * y + c
        buf[me] = x_ref[...]
        peers = []
        for rel in range(1, n_dev):
            px = 1 - x if rel & 4 else x
            py = 1 - y if rel & 2 else y
            pc = 1 - c if rel & 1 else c
            peers.append((px, py, pc))
        sends = [pltpu.make_async_remote_copy(
            src_ref=x_ref, dst_ref=buf.at[me], send_sem=send_sems.at[k], recv_sem=recv_sems.at[k],
            device_id=peer, device_id_type=MESH) for k, peer in enumerate(peers)]
        for cp in sends:
            cp.start()
        for k, (px, py, pc) in enumerate(peers):
            pltpu.make_async_remote_copy(
                src_ref=x_ref, dst_ref=buf.at[4 * px + 2 * py + pc], send_sem=send_sems.at[k],
                recv_sem=recv_sems.at[k], device_id=(px, py, pc), device_id_type=MESH).wait_recv()
        for cp in sends:
            cp.wait_send()
        acc = buf[0]
        for d in range(1, n_dev):
            acc = acc + buf[d]
        o_ref[...] = acc

    vm = pl.BlockSpec(memory_space=pltpu.VMEM)
    return pl.pallas_call(
        body, name="allreduce_replicated", out_shape=jax.ShapeDtypeStruct((Rs, C), F32),
        in_specs=[vm], out_specs=vm,
        scratch_shapes=[pltpu.VMEM((n_dev, Rs, C), F32), pltpu.SemaphoreType.DMA((n_dev - 1,)),
                        pltpu.SemaphoreType.DMA((n_dev - 1,))],
    )(pack)


REPLICATED = ("mix_norm_even", "b_f", "conv_b", "ln_g", "ln_b", "ffn_norm", "ffn_conv_b", "final_norm")


def _pad_rows(flat, align_rows, cols):
    rows = -(-flat.shape[-1] // cols)
    rows = -(-rows // align_rows) * align_rows
    pad = rows * cols - flat.shape[-1]
    flat = jnp.pad(flat, [(0, 0)] * (flat.ndim - 1) + [(0, pad)])
    return flat.reshape(flat.shape[:-1] + (rows, cols))


def _pack_replicated(grads, loss):
    parts = [_pad_rows(grads[name].astype(F32).reshape(-1), 1, LANE).reshape(-1) for name in REPLICATED]
    parts.append(_pad_rows(loss.reshape(-1)[:1], 1, LANE).reshape(-1))
    return _pad_rows(jnp.concatenate(parts), 8, LANE)


def _unpack_replicated(reduced, shapes):
    flat = reduced.reshape(-1)
    out, off = {}, 0
    for name in REPLICATED:
        n = int(np.prod(shapes[name]))
        out[name] = flat[off:off + n].reshape(shapes[name])
        off += -(-n // LANE) * LANE
    return out, flat[off]


def _ffn_fwd2(h, W, layer, tm, tmm, host_up=None, host_act=None):
    tag = str(layer)
    n = _rms_fwd(h, W["ffn_norm"][layer:layer + 1], BF16, tm, f"ffn_norm_{tag}")
    up, *g_up = _mm(n, W["w_up"][layer], "nn", F32, tmm, UP_SHARD, f"ffn_up_{tag}", host=host_up) \
        if host_up else (_mm(n, W["w_up"][layer], "nn", F32, tmm, UP_SHARD, f"ffn_up_{tag}"),)
    act, *g_act = _ffn_act_fwd(up, W["ffn_conv_w_p"][layer], W["ffn_conv_b"][layer:layer + 1], tm,
                               f"ffn_act_{tag}", host=host_act) \
        if host_act else (_ffn_act_fwd(up, W["ffn_conv_w_p"][layer], W["ffn_conv_b"][layer:layer + 1], tm,
                                       f"ffn_act_{tag}"),)
    out = _mm(act, W["w_down"][layer], "nn", F32, tm, D_MODEL, f"ffn_down_{tag}", add=h)
    return out, (n, up, act), g_up + g_act


def _ffn_bwd2(h, W, layer, saved, dout, tm, tmm, reduce=None):
    tag = str(layer)
    n, up, act = saved
    parts, comm = [], None
    if reduce is None:
        dact = _mm(dout, W["w_down"][layer], "nt", F32, tmm, UP_SHARD, f"ffn_dact_{tag}")
    else:
        names, fulls, cuts = reduce
        dact, *recv = _mm(dout, W["w_down"][layer], "nt", F32, tmm, UP_SHARD, f"ffn_dact_{tag}",
                          host=(fulls, cuts, "pairx"))
        parts = [_pair_sum2(f, r, cut, PAIR_SUM_BLOCKS[nm], "grad_pair_sum_" + nm)
                 for f, r, cut, nm in zip(fulls, recv, cuts, names)]
        comm = (parts, cuts)
    dwd = _mm(act, dout, "tn", F32, D_FF // 2, 512, f"ffn_dwdown_{tag}")
    dup, dcw, dcb, *others = _ffn_act_bwd(up, dact, W["ffn_conv_w_p"][layer], W["ffn_conv_b"][layer:layer + 1],
                                          tm, f"ffn_act_bwd_{tag}", comm=comm)
    dn = _mm_ffn_dn(dup, W["w_up"][layer], tm, D_MODEL, f"ffn_dn_{tag}")
    dwu = _mm_ffn_dwup(n, dup, 512, D_FF // 2, f"ffn_dwup_{tag}")
    dh, dgain = _rms_bwd(h, W["ffn_norm"][layer:layer + 1], dn, dout, tm, f"ffn_norm_bwd_{tag}")
    return dh, (dwu, dwd), dict(gain=dgain, cw=dcw[:FFN_K], cb=dcb), parts, others


GATHER_FIRST = ("w_in", "small")
GATHER_LATE = ("pool_w", "w_up", "w_down")
HOSTED_FFN = ("w_up1", "w_down1")
HOSTED = ("w_out", "pool_w", "w_up0", "w_down0")
LATE = ("small",)


def _local_step2(h0, tgt, W, n_real, cut_of):
    LP = h0.shape[0]
    tm, tmm = _row_tiles(LP)
    nb = LP // ATT_BLK
    G = {}
    n0 = _rms_fwd(h0, W["mix_norm_even"], BF16, tm, "mix_norm_even")
    sh = W["late_shards"]
    stage = lambda *names: ([sh[n] for n in names], [cut_of[n] for n in names])
    proj, g_down0 = _mm(n0, W["w_in_p"], "nn", F32, tmm, 896, "in_proj", host=stage("w_down0"))
    c = _fgate_fwd(proj, W["b_f_p"], "forget_gate")
    qT, kT, k_aug, vT = _attn_prep(proj, c, "attention_operands")
    oT, lse, g_pool, g_up0, g_out = _attn_fwd2(qT, k_aug, vT, "fox_attention",
                                               comm=stage("pool_w", "w_up0", "w_out"))
    g_down0, g_pool, g_up0, g_out = _gather_forward(
        [g_down0, g_pool, g_up0, g_out], stage("w_down0", "pool_w", "w_up0", "w_out")[1], "gather_forward_0")
    W = dict(W)
    W.update(pool_w=g_pool, w_up=[g_up0, None], w_down=[g_down0, None], w_out=g_out)
    u1, u = _conf_fwd(proj, W["conv_w_p"], W["conv_b"], W["ln_g"], W["ln_b"], tm, "conformer")
    cat = jnp.concatenate([_attn_rows(oT, 1.0, BF16, "attention_rows"), u], axis=1)
    h1 = _mm(cat, W["w_out"], "nn", F32, tmm, D_MODEL, "out_proj", add=h0)
    h2, ffn0, (g_down1, g_up1) = _ffn_fwd2(h1, W, 0, tm, tmm, host_up=stage("w_down1"), host_act=stage("w_up1"))
    g_down1, g_up1 = _gather_forward([g_down1, g_up1], stage("w_down1", "w_up1")[1], "gather_forward_1")
    W.update(w_up=[g_up0, g_up1], w_down=[g_down0, g_down1])
    h3, dpool = _pool_fwd(h2, W["mix_norm_odd"], W["pool_w"], W["pool_b"], W["pool_scale"], tm, "pool_mixer")
    h4, ffn1, _ = _ffn_fwd2(h3, W, 1, tm, tmm)
    loss, dh4, G["final_norm"] = _loss_head(h4, W["final_norm"], tgt, n_real, tm, "loss_head")

    dh3, (G["w_up1"], G["w_down1"]), g1, _, _ = _ffn_bwd2(h3, W, 1, ffn1, dh4, tm, tmm)
    dh2, G["pool_w"], G["pool_b"], G["pool_scale"], G["mix_norm_odd"] = _pool_bwd(
        h2, W["mix_norm_odd"], dpool, W["pool_w"], W["pool_b"], W["pool_scale"], dh3, tm, "pool_mixer_bwd")
    cuts1 = [cut_of[n] for n in HOSTED_FFN]
    dh1, (G["w_up0"], G["w_down0"]), g0, parts1, others1 = _ffn_bwd2(
        h1, W, 0, ffn0, dh2, tm, tmm, reduce=(HOSTED_FFN, [G[n] for n in HOSTED_FFN], cuts1))
    G["ffn_norm"] = jnp.concatenate([g0["gain"], g1["gain"]], axis=0)
    G["ffn_conv_w"] = jnp.stack([g0["cw"], g1["cw"]])
    G["ffn_conv_b"] = jnp.concatenate([g0["cb"], g1["cb"]], axis=0)

    dcat = _mm(dh1, W["w_out"], "nt", F32, tmm, D_MODEL, "out_proj_dx")
    G["w_out"] = _mm(cat, dh1, "tn", F32, 512, D_MODEL, "out_proj_dw")
    hcuts = [cut_of[n] for n in HOSTED]
    hfull = [G[n] for n in HOSTED]
    dadg, dcw, G["conv_b"], G["ln_g"], G["ln_b"], *hrecv = _conf_bwd(
        proj, u1, dcat, W["conv_w_p"], W["ln_g"], W["ln_b"], tm, "conformer_bwd", host=(hfull, hcuts, "pairx"))
    G["conv_w"] = dcw[:CONV_K]
    doT = _attn_cols(dcat, "attention_do_cols")
    hparts = [_pair_sum2(f, r, cut, PAIR_SUM_BLOCKS[n], "grad_pair_sum_" + n)
              for f, r, cut, n in zip(hfull, hrecv, hcuts, HOSTED)]
    dqT, dkT, dvT, *hothers = _attn_bwd2(qT, kT, k_aug, vT, oT, doT, lse, "fox_attention_bwd",
                                         comm=(hparts, hcuts))
    dqkv, dc = _attn_grads_rows(dqT, dkT, dvT, "attention_grads_rows")
    dfl, dbf = _fgate_bwd(proj, W["b_f_p"], dc, "forget_gate_bwd")
    G["b_f"] = dbf[:, :HEADS]
    dproj = jnp.concatenate([dqkv, dadg, dfl], axis=1)
    gp = _mm(n0, dproj, "tn", F32, 512, 896, "in_proj_dw")
    g_w_in = jnp.concatenate([gp[:, :3 * FOX_W], gp[:, 3 * FOX_W + 2 * CONV_CH:3 * FOX_W + 2 * CONV_CH + HEADS],
                              gp[:, 3 * FOX_W:3 * FOX_W + 2 * CONV_CH]], axis=1)
    g_w_in = g_w_in.reshape(D_MODEL, N_CHIPS, IN_SHARD).transpose(1, 0, 2)
    icut = [cut_of["w_in"]]
    dn0, irecv = _mm(dproj, W["w_in_p"], "nt", F32, tmm, D_MODEL, "in_proj_dx", host=([g_w_in], icut, "pairx"))
    ipart = _pair_sum2(g_w_in, irecv, icut[0], PAIR_SUM_BLOCKS["w_in"], "grad_pair_sum_w_in")
    dh0, G["mix_norm_even"], iother = _rms_bwd(h0, W["mix_norm_even"], dn0, dh1, tm, "mix_norm_even_bwd",
                                               host=([ipart], icut, "chipx"))
    parts = dict(zip(HOSTED_FFN + HOSTED + ("w_in",), parts1 + hparts + [ipart]))
    others = dict(zip(HOSTED_FFN + HOSTED + ("w_in",), list(others1) + list(hothers) + [iother]))
    return loss, dh0, G, parts, others


class _Cut:
    def __init__(self, full_shape, chip_dim, half_dim):
        self.full = tuple(full_shape)
        self.chip_dim, self.half_dim = chip_dim, half_dim
        self.chip_size = full_shape[chip_dim] // N_CHIPS
        self.half_size = full_shape[half_dim] // 2
        assert chip_dim != half_dim

    def shape(self, chip=False, half=False):
        s = list(self.full)
        if chip:
            s[self.chip_dim] = self.chip_size
        if half:
            s[self.half_dim] = self.half_size
        return tuple(s)

    def region(self, ref, chip=None, half=None):
        idx = [pl.ds(0, n) for n in ref.shape]
        if chip is not None:
            idx[self.chip_dim] = pl.ds(chip * self.chip_size, self.chip_size)
        if half is not None:
            idx[self.half_dim] = pl.ds(half * self.half_size, self.half_size)
        return ref.at[tuple(idx)]


SMALL_SHARDED = ("meta_tokens", "mix_norm_odd", "pool_b", "pool_scale", "conv_w", "ffn_conv_w")
SMALL_ROWS = 144


def _cuts():
    return {
        "w_in": _Cut((N_CHIPS, D_MODEL, IN_SHARD), 0, 1),
        "w_out": _Cut((D_MODEL, D_MODEL), 0, 1),
        "pool_w": _Cut((len(POOL_WINDOWS), POOL_G, POOL_G), 1, 0),
        "w_up": _Cut((2, D_MODEL, 2 * D_FF), 2, 1),
        "w_down": _Cut((2, D_FF, D_MODEL), 1, 2),
        "small": _Cut((N_CHIPS, SMALL_ROWS, LANE), 0, 1),
        "w_up0": _Cut((D_MODEL, 2 * D_FF), 1, 0), "w_up1": _Cut((D_MODEL, 2 * D_FF), 1, 0),
        "w_down0": _Cut((D_FF, D_MODEL), 0, 1), "w_down1": _Cut((D_FF, D_MODEL), 0, 1),
    }


COMM_ORDER = ("w_in", "w_out", "pool_w", "w_up", "w_down", "small")


def _remote(src, dst, send_sems, recv_sems, k, to):
    return pltpu.make_async_remote_copy(src_ref=src, dst_ref=dst, send_sem=send_sems.at[k],
                                        recv_sem=recv_sems.at[k], device_id=to, device_id_type=MESH)


def _gather_weights(shards, cuts):
    n = len(shards)

    def body(*refs):
        srcs, outs = refs[:n], refs[n:2 * n]
        send_sems, recv_sems = refs[2 * n:]
        x, y, c = _coords()
        me = 2 * x + y
        sibling = (x, y, 1 - c)
        chips = _other_chips(x, y)
        sends = []
        for t, cut in enumerate(cuts):
            push = _remote(srcs[t], cut.region(outs[t], chip=me), send_sems, recv_sems, 7 * t, sibling)
            push.start()
            sends.append(push)
            for kk, chip in enumerate(chips):
                cp = _remote(cut.region(srcs[t], half=c), cut.region(outs[t], chip=me, half=c),
                             send_sems, recv_sems, 7 * t + 1 + kk, (*chip, c))
                cp.start()
                sends.append(cp)
        for t, cut in enumerate(cuts):
            for kk, (px, py) in enumerate(chips):
                landed = cut.region(outs[t], chip=2 * px + py, half=c)
                _remote(landed, landed, send_sems, recv_sems, 7 * t + 1 + kk, sibling).wait_recv()
                fwd = _remote(landed, landed, send_sems, recv_sems, 7 * t + 4 + kk, sibling)
                fwd.start()
                sends.append(fwd)
        for t, cut in enumerate(cuts):
            mine = cut.region(outs[t], chip=me)
            _remote(mine, mine, send_sems, recv_sems, 7 * t, sibling).wait_recv()
            for kk, (px, py) in enumerate(chips):
                other = cut.region(outs[t], chip=2 * px + py, half=1 - c)
                _remote(other, other, send_sems, recv_sems, 7 * t + 4 + kk, sibling).wait_recv()
        for cp in sends:
            cp.wait_send()

    return pl.pallas_call(
        body, name="gather_weights",
        out_shape=tuple(jax.ShapeDtypeStruct(cut.full, s.dtype) for cut, s in zip(cuts, shards)),
        in_specs=[ANY] * n, out_specs=tuple([ANY] * n),
        scratch_shapes=[pltpu.SemaphoreType.DMA((7 * n,)), pltpu.SemaphoreType.DMA((7 * n,))],
    )(*shards)


def _gather_first_ops(srcs, outs, send_sems, recv_sems, cuts):
    x, y, c = _coords()
    me = 2 * x + y
    sibling = (x, y, 1 - c)
    chips = _other_chips(x, y)

    def copies():
        out = []
        for t, cut in enumerate(cuts):
            out.append(_remote(srcs[t], cut.region(outs[t], chip=me), send_sems, recv_sems, 4 * t, sibling))
            for kk, chip in enumerate(chips):
                out.append(_remote(cut.region(srcs[t], half=c), cut.region(outs[t], chip=me, half=c),
                                   send_sems, recv_sems, 4 * t + 1 + kk, (*chip, c)))
        return out

    def start():
        for cp in copies():
            cp.start()

    def wait():
        for t, cut in enumerate(cuts):
            mine = cut.region(outs[t], chip=me)
            _remote(mine, mine, send_sems, recv_sems, 4 * t, sibling).wait_recv()
            for kk, (px, py) in enumerate(chips):
                landed = cut.region(outs[t], chip=2 * px + py, half=c)
                _remote(landed, landed, send_sems, recv_sems, 4 * t + 1 + kk, sibling).wait_recv()
        for cp in copies():
            cp.wait_send()

    return start, wait


def _pair_exchange_ops(srcs, outs, send_sems, recv_sems, cuts):
    x, y, c = _coords()

    def copies():
        return [_remote(cut.region(srcs[t], half=1 - c), outs[t], send_sems, recv_sems, t, (x, y, 1 - c))
                for t, cut in enumerate(cuts)]

    def start():
        for cp in copies():
            cp.start()

    def wait():
        for cp in copies():
            cp.wait()

    return start, wait


def _host_plan(host):
    arrays, cuts = host[0], host[1]
    if len(host) > 2 and host[2] == "chipx":
        return _chip_exchange_shapes(arrays, cuts), 3 * len(arrays), _chip_exchange_ops
    if len(host) > 2 and host[2] == "pairx":
        return (tuple(jax.ShapeDtypeStruct(cut.shape(half=True), a.dtype) for cut, a in zip(cuts, arrays)),
                len(arrays), _pair_exchange_ops)
    return (tuple(jax.ShapeDtypeStruct(cut.full, a.dtype) for cut, a in zip(cuts, arrays)),
            4 * len(arrays), _gather_first_ops)


def _gather_forward(fulls, cuts, name):
    n = len(fulls)

    def body(*refs):
        outs = refs[n:2 * n]
        send_sems, recv_sems = refs[2 * n:]
        x, y, c = _coords()
        sibling = (x, y, 1 - c)
        chips = _other_chips(x, y)
        sends = []
        for t, cut in enumerate(cuts):
            for kk, (px, py) in enumerate(chips):
                landed = cut.region(outs[t], chip=2 * px + py, half=c)
                cp = _remote(landed, landed, send_sems, recv_sems, 3 * t + kk, sibling)
                cp.start()
                sends.append(cp)
        for t, cut in enumerate(cuts):
            for kk, (px, py) in enumerate(chips):
                other = cut.region(outs[t], chip=2 * px + py, half=1 - c)
                _remote(other, other, send_sems, recv_sems, 3 * t + kk, sibling).wait_recv()
        for cp in sends:
            cp.wait_send()

    return pl.pallas_call(
        body, name=name,
        out_shape=tuple(jax.ShapeDtypeStruct(f.shape, f.dtype) for f in fulls),
        in_specs=[ANY] * n, out_specs=tuple([ANY] * n), input_output_aliases={t: t for t in range(n)},
        scratch_shapes=[pltpu.SemaphoreType.DMA((3 * n,)), pltpu.SemaphoreType.DMA((3 * n,))],
    )(*fulls)


def _pair_exchange2(fulls, cuts, name):
    n = len(fulls)

    def body(*refs):
        srcs, outs = refs[:n], refs[n:2 * n]
        send_sems, recv_sems = refs[2 * n:]
        x, y, c = _coords()
        cps = [_remote(cut.region(srcs[t], half=1 - c), outs[t], send_sems, recv_sems, t, (x, y, 1 - c))
               for t, cut in enumerate(cuts)]
        for cp in cps:
            cp.start()
        for cp in cps:
            cp.wait()

    return pl.pallas_call(
        body, name=name,
        out_shape=tuple(jax.ShapeDtypeStruct(cut.shape(half=True), f.dtype) for cut, f in zip(cuts, fulls)),
        in_specs=[ANY] * n, out_specs=tuple([ANY] * n),
        scratch_shapes=[pltpu.SemaphoreType.DMA((n,)), pltpu.SemaphoreType.DMA((n,))],
    )(*fulls)


def _grid_of(shape, blk):
    assert all(s % b == 0 for s, b in zip(shape, blk)), (shape, blk)
    return tuple(s // b for s, b in zip(shape, blk))


def _pair_sum2(full, recv, cut, blk, name):
    hshape = cut.shape(half=True)
    grid = _grid_of(hshape, blk)
    hb = cut.half_size // blk[cut.half_dim]
    hd = cut.half_dim
    pos = jnp.stack([lax.axis_index("c")]).astype(jnp.int32)

    def full_idx(*a):
        ids, p = list(a[:-1]), a[-1]
        ids[hd] = ids[hd] + p[0] * hb
        return tuple(ids)

    def body(p_ref, f_ref, r_ref, o_ref):
        o_ref[...] = (f_ref[...] + r_ref[...]).astype(o_ref.dtype)

    return pl.pallas_call(
        body, name=name, out_shape=jax.ShapeDtypeStruct(hshape, BF16),
        grid_spec=pltpu.PrefetchScalarGridSpec(
            num_scalar_prefetch=1, grid=grid,
            in_specs=[pl.BlockSpec(blk, full_idx), pl.BlockSpec(blk, lambda *a: tuple(a[:-1]))],
            out_specs=pl.BlockSpec(blk, lambda *a: tuple(a[:-1]))),
        compiler_params=_params(("parallel",) * len(grid)))(pos, full, recv)


def _chip_exchange_ops(srcs, outs, send_sems, recv_sems, cuts):
    x, y, c = _coords()
    me = 2 * x + y
    chips = _other_chips(x, y)

    def copies():
        return [_remote(cut.region(srcs[t], chip=2 * px + py), outs[t].at[me], send_sems, recv_sems,
                        3 * t + kk, (px, py, c))
                for t, cut in enumerate(cuts) for kk, (px, py) in enumerate(chips)]

    def start():
        for cp in copies():
            cp.start()

    def wait():
        for t, cut in enumerate(cuts):
            for kk, (px, py) in enumerate(chips):
                slot = outs[t].at[2 * px + py]
                _remote(slot, slot, send_sems, recv_sems, 3 * t + kk, (px, py, c)).wait_recv()
        for cp in copies():
            cp.wait_send()

    return start, wait


def _chip_exchange_shapes(parts, cuts):
    return tuple(jax.ShapeDtypeStruct((N_CHIPS,) + cut.shape(chip=True, half=True), p.dtype)
                 for cut, p in zip(cuts, parts))


def _chip_exchange2(parts, cuts):
    n = len(parts)

    def body(*refs):
        start, wait = _chip_exchange_ops(refs[:n], refs[n:2 * n], refs[2 * n], refs[2 * n + 1], cuts)
        start()
        wait()

    return pl.pallas_call(
        body, name="grad_chip_exchange",
        out_shape=tuple(jax.ShapeDtypeStruct((N_CHIPS,) + cut.shape(chip=True, half=True), p.dtype)
                        for cut, p in zip(cuts, parts)),
        in_specs=[ANY] * n, out_specs=tuple([ANY] * n),
        scratch_shapes=[pltpu.SemaphoreType.DMA((3 * n,)), pltpu.SemaphoreType.DMA((3 * n,))],
    )(*parts)


def _chip_sum2(part, recv, cut, blk, name, stacked=None):
    bshape = cut.shape(chip=True, half=True)
    grid = _grid_of(bshape, blk)
    cb = cut.chip_size // blk[cut.chip_dim]
    hb = cut.half_size // blk[cut.half_dim]
    cd, hd = cut.chip_dim, cut.half_dim
    x, y, c = _coords()
    slots = [2 * px + py for px, py in _other_chips(x, y)]
    pos = jnp.stack([c, 2 * x + y] + slots).astype(jnp.int32)

    def part_idx(*a):
        ids, p = list(a[:-1]), a[-1]
        ids[cd] = ids[cd] + p[1] * cb
        return tuple(ids)

    def recv_idx(kk):
        return lambda *a: (a[-1][2 + kk],) + tuple(a[:-1])

    def out_idx(*a):
        ids, p = list(a[:-1]), a[-1]
        ids[hd] = ids[hd] + p[0] * hb
        return tuple(ids)

    def body(p_ref, own_ref, r0_ref, r1_ref, r2_ref, *rest):
        acc = own_ref[...].astype(F32)
        for r_ref in (r0_ref, r1_ref, r2_ref):
            acc = acc + r_ref[...].astype(F32)
        rest[-1][...] = acc

    in_specs = [pl.BlockSpec(blk, part_idx)] + [pl.BlockSpec((None,) + blk, recv_idx(kk)) for kk in range(3)]
    args = [pos, part, recv, recv, recv]
    aliases = {}
    if stacked is None:
        out_shape = jax.ShapeDtypeStruct(cut.shape(chip=True), F32)
        out_spec = pl.BlockSpec(blk, out_idx)
    else:
        lead, n_lead, into = stacked
        out_shape = jax.ShapeDtypeStruct((n_lead,) + cut.shape(chip=True), F32)
        out_spec = pl.BlockSpec((None,) + blk, lambda *a: (lead,) + out_idx(*a))
        if into is not None:
            in_specs.append(pl.BlockSpec(memory_space=pl.ANY))
            args.append(into)
            aliases = {5: 0}
    return pl.pallas_call(
        body, name=name, out_shape=out_shape,
        grid_spec=pltpu.PrefetchScalarGridSpec(num_scalar_prefetch=1, grid=grid, in_specs=in_specs,
                                               out_specs=out_spec),
        input_output_aliases=aliases, compiler_params=_params(("parallel",) * len(grid)))(*args)


def _pair_swap2(blocks, cuts):
    n = len(blocks)

    def body(*refs):
        outs = refs[n:2 * n]
        send_sems, recv_sems = refs[2 * n:]
        x, y, c = _coords()
        cps = []
        for t, cut in enumerate(cuts):
            mine = cut.region(outs[t], half=c)
            cp = _remote(mine, mine, send_sems, recv_sems, t, (x, y, 1 - c))
            cp.start()
            cps.append(cp)
        for t, cut in enumerate(cuts):
            theirs = cut.region(outs[t], half=1 - c)
            _remote(theirs, theirs, send_sems, recv_sems, t, (x, y, 1 - c)).wait_recv()
        for cp in cps:
            cp.wait_send()

    return pl.pallas_call(
        body, name="grad_pair_swap",
        out_shape=tuple(jax.ShapeDtypeStruct(b.shape, b.dtype) for b in blocks),
        in_specs=[ANY] * n, out_specs=tuple([ANY] * n),
        input_output_aliases={t: t for t in range(n)},
        scratch_shapes=[pltpu.SemaphoreType.DMA((n,)), pltpu.SemaphoreType.DMA((n,))],
    )(*blocks)


PAIR_SUM_BLOCKS = {"w_in": (1, 512, IN_SHARD), "w_out": (512, 512), "pool_w": (1, POOL_G, POOL_G),
                   "w_up0": (64, 2 * D_FF), "w_up1": (64, 2 * D_FF), "w_down0": (704, 512), "w_down1": (704, 512),
                   "small": (N_CHIPS, SMALL_ROWS // 2, LANE)}
CHIP_SUM_BLOCKS = {"w_in": (1, 512, IN_SHARD), "w_out": (256, 512), "pool_w": (2, 64, POOL_G),
                   "w_up0": (128, UP_SHARD), "w_up1": (128, UP_SHARD),
                   "w_down0": (DOWN_SHARD, 512), "w_down1": (DOWN_SHARD, 512),
                   "small": (1, SMALL_ROWS // 2, LANE)}


def _small_rows(t, lead):
    flat = t.reshape(lead + (-1,))
    pad = -flat.shape[-1] % LANE
    return jnp.pad(flat, [(0, 0)] * len(lead) + [(0, pad)]).reshape(lead + (-1, LANE))


def _pack_small_shards(shards):
    rows = jnp.concatenate([_small_rows(shards[n].astype(F32), ()) for n in SMALL_SHARDED], axis=0)
    return jnp.pad(rows, ((0, SMALL_ROWS - rows.shape[0]), (0, 0)))[None]


def _unpack_small(pack, shards, axes):
    out, off = {}, 0
    nchip = pack.shape[0]
    for name in SMALL_SHARDED:
        shp = shards[name].shape
        cnt = int(np.prod(shp))
        rows = -(-cnt // LANE)
        t = pack[:, off:off + rows].reshape(nchip, -1)[:, :cnt].reshape((nchip,) + shp)
        out[name] = jnp.concatenate([t[j] for j in range(nchip)], axis=axes[name])
        off += rows
    return out


def _pack_small_grads(grads, shards, axes):
    parts = []
    for name in SMALL_SHARDED:
        shp, ax = shards[name].shape, axes[name]
        g = grads[name].reshape(shp[:ax] + (N_CHIPS, shp[ax]) + shp[ax + 1:])
        parts.append(_small_rows(jnp.moveaxis(g, ax, 0), (N_CHIPS,)))
    rows = jnp.concatenate(parts, axis=1)
    return jnp.pad(rows, ((0, 0), (0, SMALL_ROWS - rows.shape[1]), (0, 0)))


SMALL_AXES = {"meta_tokens": 1, "mix_norm_odd": 1, "pool_b": 2, "pool_scale": 1, "conv_w": 2, "ffn_conv_w": 2}


WEIGHT_NAMES = ("meta_tokens", "mix_norm_even", "w_in", "b_f", "conv_w", "conv_b", "ln_g", "ln_b", "w_out",
                "mix_norm_odd", "pool_w", "pool_b", "pool_scale", "ffn_norm", "w_up", "ffn_conv_w",
                "ffn_conv_b", "w_down", "final_norm")


def kernel(x, meta_tokens, mix_norm_even, w_in, b_f, conv_w, conv_b, ln_g, ln_b, w_out, mix_norm_odd, pool_w, pool_b, pool_scale, ffn_norm, w_up, ffn_conv_w, ffn_conv_b, w_down, final_norm, loss_target, m_meta_tokens, m_mix_norm_even, m_w_in, m_b_f, m_conv_w, m_conv_b, m_ln_g, m_ln_b, m_w_out, m_mix_norm_odd, m_pool_w, m_pool_b, m_pool_scale, m_ffn_norm, m_w_up, m_ffn_conv_w, m_ffn_conv_b, m_w_down, m_final_norm, v_meta_tokens, v_mix_norm_even, v_w_in, v_b_f, v_conv_w, v_conv_b, v_ln_g, v_ln_b, v_w_out, v_mix_norm_odd, v_pool_w, v_pool_b, v_pool_scale, v_ffn_norm, v_w_up, v_ffn_conv_w, v_ffn_conv_b, v_w_down, v_final_norm):
    given = dict(locals())
    w_loc = {n: given[n] for n in WEIGHT_NAMES}
    m_loc = {n: given["m_" + n] for n in WEIGHT_NAMES}
    v_loc = {n: given["v_" + n] for n in WEIGHT_NAMES}
    cut_of = _cuts()
    cuts = [cut_of[n] for n in COMM_ORDER]
    big = ("w_in", "w_out", "pool_w", "w_up", "w_down")
    small_shards = {n: w_loc[n] for n in SMALL_SHARDED}

    shard_of = {n: w_loc[n].astype(BF16).reshape(cut_of[n].shape(chip=True)) for n in big}
    shard_of["small"] = _pack_small_shards(small_shards)
    g_in, g_small = _gather_weights([shard_of[n] for n in GATHER_FIRST], [cut_of[n] for n in GATHER_FIRST])
    g_out = None
    g_pool = g_up = g_down = None
    full = _unpack_small(g_small, small_shards, SMALL_AXES)
    full.update({n: w_loc[n] for n in REPLICATED})
    w_in_full = g_in.transpose(1, 0, 2).reshape(D_MODEL, IN_COLS)
    qkv, f, ag = (w_in_full[:, :3 * FOX_W], w_in_full[:, 3 * FOX_W:3 * FOX_W + HEADS],
                  w_in_full[:, 3 * FOX_W + HEADS:])
    W = dict(
        mix_norm_even=full["mix_norm_even"].reshape(1, D_MODEL),
        w_in_p=jnp.concatenate([qkv, ag, f, jnp.zeros((D_MODEL, LANE - HEADS), BF16)], axis=1),
        b_f_p=jnp.pad(full["b_f"].reshape(1, HEADS), ((0, 0), (0, LANE - HEADS))),
        conv_w_p=jnp.pad(full["conv_w"].reshape(CONV_K, CONV_CH), ((0, CONV_HALO - CONV_K), (0, 0))),
        conv_b=full["conv_b"].reshape(1, CONV_CH), ln_g=full["ln_g"].reshape(1, CONV_CH),
        ln_b=full["ln_b"].reshape(1, CONV_CH), w_out=g_out,
        mix_norm_odd=full["mix_norm_odd"].reshape(1, D_MODEL), pool_w=g_pool,
        pool_b=full["pool_b"].reshape(1, D_MODEL), pool_scale=full["pool_scale"].reshape(1, D_MODEL),
        ffn_norm=full["ffn_norm"], w_up=g_up,
        ffn_conv_w_p=jnp.pad(full["ffn_conv_w"], ((0, 0), (0, 8 - FFN_K), (0, 0))),
        ffn_conv_b=full["ffn_conv_b"], w_down=g_down, final_norm=full["final_norm"].reshape(1, D_MODEL),
        late_shards=dict(pool_w=shard_of["pool_w"], w_out=shard_of["w_out"],
                         w_up0=shard_of["w_up"][0], w_up1=shard_of["w_up"][1],
                         w_down0=shard_of["w_down"][0], w_down1=shard_of["w_down"][1]))

    seq = x.shape[1]
    n_real = N_META + seq
    LP = -(-n_real // ATT_BLK) * ATT_BLK
    tail = jnp.zeros((LP - n_real, D_MODEL), F32)
    h0 = jnp.concatenate([full["meta_tokens"], x[0], tail], axis=0)
    tgt = jnp.concatenate([jnp.zeros((N_META, D_MODEL), F32), loss_target[0], tail], axis=0)
    loss_loc, dh0, G, parts, others = _local_step2(h0, tgt, W, n_real, cut_of)
    grad_x = dh0[N_META:n_real][None]
    G["meta_tokens"] = dh0[:N_META]

    rep_shapes = {n: w_loc[n].shape for n in REPLICATED}
    G["final_norm"] = G["final_norm"].reshape(D_MODEL)
    rep, loss = _unpack_replicated(_allreduce_small(_pack_replicated(G, loss_loc)), rep_shapes)

    lcuts = [cut_of[n] for n in LATE]
    lfull = [_pack_small_grads(G, small_shards, SMALL_AXES)]
    lrecv = _pair_exchange2(lfull, lcuts, "grad_pair_exchange_late")
    lparts = [_pair_sum2(f, r, cut, PAIR_SUM_BLOCKS[n], "grad_pair_sum_" + n)
              for f, r, cut, n in zip(lfull, lrecv, lcuts, LATE)]
    parts.update(zip(LATE, lparts))
    others.update(zip(LATE, _chip_exchange2(lparts, lcuts)))
    def chip_sum(n, stacked=None):
        return _chip_sum2(parts[n], others[n], cut_of[n], CHIP_SUM_BLOCKS[n], "grad_chip_sum_" + n, stacked=stacked)

    blocks = []
    for n in COMM_ORDER:
        if n in ("w_up", "w_down"):
            blocks.append(chip_sum(n + "1", stacked=(1, 2, chip_sum(n + "0", stacked=(0, 2, None)))))
        else:
            blocks.append(chip_sum(n))
    blocks = _pair_swap2(blocks, cuts)
    gsh = {n: b.reshape(w_loc[n].shape) for n, b in zip(big, blocks[:5])}
    gsh.update(_unpack_small(blocks[5], small_shards, SMALL_AXES))
    sharded = set(big) | set(SMALL_SHARDED)

    grad_w = {n: (gsh[n] if n in sharded else rep[n]) for n in WEIGHT_NAMES}
    delta, new_m, new_v = {}, {}, {}
    for n in WEIGHT_NAMES:
        delta[n], new_m[n], new_v[n] = _adamw(w_loc[n], grad_w[n], m_loc[n], v_loc[n], "adamw_" + n)
    return (loss, grad_x, *[grad_w[n] for n in WEIGHT_NAMES], *[delta[n] for n in WEIGHT_NAMES],
            *[new_m[n] for n in WEIGHT_NAMES], *[new_v[n] for n in WEIGHT_NAMES])
```

```python
import numpy as np
import jax
import jax.numpy as jnp
from jax import lax
from jax.experimental import pallas as pl
from jax.experimental.pallas import tpu as pltpu

F32 = jnp.float32
BF16 = jnp.bfloat16

D_MODEL = 1024
N_META = 16
SEQ = 2048
HEADS = 8
HEAD_DIM = 64
FOX_W = HEADS * HEAD_DIM
CONV_CH = 512
CONV_K = 31
D_FF = 2816
POOL_WINDOWS = (2, 4, 8, 16)
POOL_G = 256
RMS_EPS = 1e-6
LN_EPS = 1e-5
IN_COLS = 3 * FOX_W + HEADS + 2 * CONV_CH
IN_COLS_P = 3 * FOX_W + 2 * CONV_CH + 128
F_COL_BLK = (3 * FOX_W + 2 * CONV_CH) // 128
N_CHIPS = 4
IN_SHARD = IN_COLS // N_CHIPS
UP_SHARD = 2 * D_FF // N_CHIPS
DOWN_SHARD = D_FF // N_CHIPS

ADAM_LR = 0.001
ADAM_B1 = 0.9
ADAM_B2 = 0.999
ADAM_EPS = 1e-08
ADAM_WD = 0.01
ADAM_STEP = 10

LANE = 128
ATT_BLK = 128
VMEM_LIMIT = 56 * 1024 * 1024

NEG = -1e30


def _sigmoid(x):
    return 0.5 * jnp.tanh(0.5 * x) + 0.5


def _sigmoid_tail(x):
    return 1.0 / (1.0 + jnp.exp(-x))


def _params(sem=None):
    return pltpu.CompilerParams(dimension_semantics=sem, vmem_limit_bytes=VMEM_LIMIT)


def _sub_rows(tm):
    best = 8
    for s in range(8, 137, 8):
        if tm % s == 0:
            best = s
    return best


def _mm(a, b, mode, out_dtype, tm, tn, name, add=None, a_lead=None, b_lead=None, out=None, host=None):
    a_shape = a.shape if a_lead is None else a.shape[1:]
    b_shape = b.shape if b_lead is None else b.shape[1:]
    if mode == "nn":
        (M, K), (K2, N) = a_shape, b_shape
        dims = (((1,), (0,)), ((), ()))
        a_blk, a_idx = (tm, K), (lambda i, j: (i, 0))
        b_blk, b_idx = (K, tn), (lambda i, j: (0, j))
    elif mode == "nt":
        (M, K), (N, K2) = a_shape, b_shape
        dims = (((1,), (1,)), ((), ()))
        a_blk, a_idx = (tm, K), (lambda i, j: (i, 0))
        b_blk, b_idx = (tn, K), (lambda i, j: (j, 0))
    else:
        (K, M), (K2, N) = a_shape, b_shape
        dims = (((0,), (0,)), ((), ()))
        a_blk, a_idx = (K, tm), (lambda i, j: (0, i))
        b_blk, b_idx = (K, tn), (lambda i, j: (0, j))
    assert K == K2 and M % tm == 0 and N % tn == 0, (name, a.shape, b.shape, tm, tn)
    gm, gn = M // tm, N // tn
    a_bytes = M * K * a.dtype.itemsize
    b_bytes = N * K * b.dtype.itemsize
    m_outer = a_bytes + b_bytes * gm <= b_bytes + a_bytes * gn
    if m_outer:
        grid = (gm, gn)
        wrap = lambda f: f
    else:
        grid = (gn, gm)
        wrap = lambda f: (lambda j, i: f(i, j))

    def lead(blk, idx, at):
        if at is None:
            return pl.BlockSpec(blk, wrap(idx))
        return pl.BlockSpec((None,) + blk, wrap(lambda i, j: (at,) + idx(i, j)))

    o_idx = lambda i, j: (i, j)
    in_specs = [lead(a_blk, a_idx, a_lead), lead(b_blk, b_idx, b_lead)]
    args = [a, b]
    if add is not None:
        in_specs.append(pl.BlockSpec((tm, tn), wrap(o_idx)))
        args.append(add)
    aliases = {}
    if out is None:
        out_shape = jax.ShapeDtypeStruct((M, N), out_dtype)
        out_spec = pl.BlockSpec((tm, tn), wrap(o_idx))
    else:
        o_lead, n_lead, into = out
        out_shape = jax.ShapeDtypeStruct((n_lead, M, N), out_dtype)
        out_spec = lead((tm, tn), o_idx, o_lead)
        if into is not None:
            aliases = {len(args): 0}
            in_specs.append(pl.BlockSpec(memory_space=pl.ANY))
            args.append(into)
    has_add = add is not None
    n_host = 0 if host is None else len(host[0])
    n_in = len(args)
    scratch = []
    if n_host:
        host_shapes, host_sems, host_ops = _host_plan(host)
        in_specs = in_specs + [pl.BlockSpec(memory_space=pl.ANY)] * n_host
        args = args + list(host[0])
        out_shape = (out_shape,) + host_shapes
        out_spec = (out_spec,) + (pl.BlockSpec(memory_space=pl.ANY),) * n_host
        scratch = [pltpu.SemaphoreType.DMA((host_sems,)), pltpu.SemaphoreType.DMA((host_sems,))]

    def body(*refs):
        a_ref, b_ref = refs[0], refs[1]
        o_ref = refs[n_in + n_host]
        if n_host:
            start, wait = host_ops(refs[n_in:n_in + n_host], refs[n_in + n_host + 1:n_in + 2 * n_host + 1],
                                   refs[n_in + 2 * n_host + 1], refs[n_in + 2 * n_host + 2], host[1])
            pl.when(jnp.logical_and(pl.program_id(0) == 0, pl.program_id(1) == 0))(start)
        x = a_ref[...].astype(BF16)
        y = b_ref[...].astype(BF16)
        acc = lax.dot_general(x, y, dims, preferred_element_type=F32)
        if has_add:
            acc = acc + refs[2][...]
        o_ref[...] = acc.astype(o_ref.dtype)
        if n_host:
            pl.when(jnp.logical_and(pl.program_id(0) == grid[0] - 1, pl.program_id(1) == grid[1] - 1))(wait)

    sem = ("arbitrary", "arbitrary") if n_host else ("parallel", "parallel")
    return pl.pallas_call(
        body, name=name, out_shape=out_shape, grid=grid, in_specs=in_specs, out_specs=out_spec,
        scratch_shapes=scratch, input_output_aliases=aliases, compiler_params=_params(sem))(*args)


def _mm_ffn_dn(dup, w_up, tm, tn, name):
    _, LP, F = dup.shape
    Dm = w_up.shape[0]
    nt = (((1,), (1,)), ((), ()))

    def body(a_ref, b_ref, o_ref):
        acc = lax.dot_general(a_ref[0], b_ref[:, 0:F], nt, preferred_element_type=F32)
        acc = acc + lax.dot_general(a_ref[1], b_ref[:, F:2 * F], nt, preferred_element_type=F32)
        o_ref[...] = acc

    return pl.pallas_call(
        body, name=name, out_shape=jax.ShapeDtypeStruct((LP, Dm), F32), grid=(LP // tm, Dm // tn),
        in_specs=[pl.BlockSpec((2, tm, F), lambda i, j: (0, i, 0)),
                  pl.BlockSpec((tn, 2 * F), lambda i, j: (j, 0))],
        out_specs=pl.BlockSpec((tm, tn), lambda i, j: (i, j)),
        compiler_params=_params(("parallel", "parallel")))(dup, w_up)


def _mm_ffn_dwup(n, dup, tk, tn, name):
    LP, Dm = n.shape
    F = dup.shape[2]
    nct = F // tn
    tdims = (((0,), (0,)), ((), ()))

    def body(a_ref, b_ref, o_ref):
        o_ref[...] = lax.dot_general(a_ref[...], b_ref[...], tdims, preferred_element_type=F32)

    return pl.pallas_call(
        body, name=name, out_shape=jax.ShapeDtypeStruct((Dm, 2 * F), F32), grid=(Dm // tk, 2 * nct),
        in_specs=[pl.BlockSpec((LP, tk), lambda i, j: (0, i)),
                  pl.BlockSpec((None, LP, tn), lambda i, j: (j // nct, 0, j % nct))],
        out_specs=pl.BlockSpec((tk, tn), lambda i, j: (i, j)),
        compiler_params=_params(("parallel", "parallel")))(n, dup)


def _rms_fwd(h, g, out_dtype, tm, name):
    LP, Dm = h.shape

    def body(h_ref, g_ref, o_ref):
        x = h_ref[...]
        r = lax.rsqrt(jnp.mean(x * x, axis=1, keepdims=True) + RMS_EPS)
        o_ref[...] = (x * r * g_ref[...]).astype(o_ref.dtype)

    return pl.pallas_call(
        body, name=name, out_shape=jax.ShapeDtypeStruct((LP, Dm), out_dtype), grid=(LP // tm,),
        in_specs=[pl.BlockSpec((tm, Dm), lambda i: (i, 0)), pl.BlockSpec((1, Dm), lambda i: (0, 0))],
        out_specs=pl.BlockSpec((tm, Dm), lambda i: (i, 0)),
        compiler_params=_params(("parallel",)))(h, g)


def _rms_bwd(h, g, dn, dres, tm, name, host=None):
    LP, Dm = h.shape
    n_host = 0 if host is None else len(host[0])
    if n_host:
        host_shapes, host_sems, host_ops = _host_plan(host)
    nblk = LP // tm

    def body(*refs):
        h_ref, g_ref, dn_ref, dr_ref = refs[:4]
        dh_ref, dg_ref = refs[4 + n_host:6 + n_host]
        if n_host:
            start, wait = host_ops(refs[4:4 + n_host], refs[6 + n_host:6 + 2 * n_host],
                                   refs[6 + 2 * n_host], refs[7 + 2 * n_host], host[1])
            pl.when(pl.program_id(0) == 0)(start)
        i = pl.program_id(0)
        x = h_ref[...]
        r = lax.rsqrt(jnp.mean(x * x, axis=1, keepdims=True) + RMS_EPS)
        xhat = x * r
        dy = dn_ref[...]
        dxh = dy * g_ref[...]
        dh = r * (dxh - xhat * jnp.mean(dxh * xhat, axis=1, keepdims=True))
        dh_ref[...] = dr_ref[...] + dh

        @pl.when(i == 0)
        def _():
            dg_ref[...] = jnp.zeros_like(dg_ref)

        dg_ref[...] += jnp.sum(dy * xhat, axis=0, keepdims=True)
        if n_host:
            pl.when(pl.program_id(0) == nblk - 1)(wait)

    row = pl.BlockSpec((tm, Dm), lambda i: (i, 0))
    vec = pl.BlockSpec((1, Dm), lambda i: (0, 0))
    out_shape = (jax.ShapeDtypeStruct((LP, Dm), F32), jax.ShapeDtypeStruct((1, Dm), F32))
    out_specs = (row, vec)
    in_specs = [row, vec, row, row]
    args = [h, g, dn, dres]
    scratch = []
    if n_host:
        anyspec = pl.BlockSpec(memory_space=pl.ANY)
        in_specs += [anyspec] * n_host
        args += list(host[0])
        out_shape += host_shapes
        out_specs += (anyspec,) * n_host
        scratch = [pltpu.SemaphoreType.DMA((host_sems,)), pltpu.SemaphoreType.DMA((host_sems,))]
    return pl.pallas_call(
        body, name=name, out_shape=out_shape, grid=(nblk,), in_specs=in_specs, out_specs=out_specs,
        scratch_shapes=scratch, compiler_params=_params(("arbitrary",)))(*args)


def _loss_head(h, g, tgt, n_real, tm, name):
    LP, Dm = h.shape

    def body(h_ref, g_ref, t_ref, loss_ref, dh_ref, dg_ref):
        i = pl.program_id(0)
        x = h_ref[...]
        gg = g_ref[...]
        r = lax.rsqrt(jnp.mean(x * x, axis=1, keepdims=True) + RMS_EPS)
        xhat = x * r
        rows = i * tm + lax.broadcasted_iota(jnp.int32, (tm, 1), 0)
        real = jnp.logical_and(rows >= N_META, rows < n_real)
        diff = jnp.where(real, xhat * gg - t_ref[...], 0.0)
        dy = diff * (1.0 / Dm)
        dxh = dy * gg
        dh_ref[...] = r * (dxh - xhat * jnp.mean(dxh * xhat, axis=1, keepdims=True))

        @pl.when(i == 0)
        def _():
            dg_ref[...] = jnp.zeros_like(dg_ref)
            loss_ref[...] = jnp.zeros_like(loss_ref)

        dg_ref[...] += jnp.sum(dy * xhat, axis=0, keepdims=True)
        part = jnp.sum(jnp.sum(diff * diff, axis=1, keepdims=True), axis=0, keepdims=True)
        loss_ref[...] += jnp.broadcast_to(part * (0.5 / Dm), loss_ref.shape)

    row = pl.BlockSpec((tm, Dm), lambda i: (i, 0))
    vec = pl.BlockSpec((1, Dm), lambda i: (0, 0))
    return pl.pallas_call(
        body, name=name,
        out_shape=(jax.ShapeDtypeStruct((1, LANE), F32), jax.ShapeDtypeStruct((LP, Dm), F32),
                   jax.ShapeDtypeStruct((1, Dm), F32)),
        grid=(LP // tm,), in_specs=[row, vec, row],
        out_specs=(pl.BlockSpec((1, LANE), lambda i: (0, 0)), row, vec),
        compiler_params=_params(("arbitrary",)))(h, g, tgt)


def _fgate_fwd(proj, bf_p, name):
    LP = proj.shape[0]
    nb = LP // LANE

    def body(f_ref, b_ref, c_ref, lf_ref):
        x = f_ref[...] + b_ref[...]
        lf_ref[...] = jnp.minimum(x, 0.0) - jnp.log1p(jnp.exp(-jnp.abs(x)))
        ri = lax.broadcasted_iota(jnp.int32, (LANE, LANE), 0)
        ci = lax.broadcasted_iota(jnp.int32, (LANE, LANE), 1)
        tri = jnp.where(ri >= ci, 1.0, 0.0).astype(F32)

        def blk(i, carry):
            rows = pl.ds(pl.multiple_of(i * LANE, LANE), LANE)
            cb = jnp.dot(tri, lf_ref[rows, :], precision=lax.Precision.HIGHEST,
                         preferred_element_type=F32) + carry
            c_ref[rows, :] = cb
            return cb[LANE - 1:LANE, :]

        lax.fori_loop(0, nb, blk, jnp.zeros((1, LANE), F32))

    return pl.pallas_call(
        body, name=name, out_shape=jax.ShapeDtypeStruct((LP, LANE), F32), grid=(1,),
        in_specs=[pl.BlockSpec((LP, LANE), lambda i: (0, F_COL_BLK)),
                  pl.BlockSpec((1, LANE), lambda i: (0, 0))],
        out_specs=pl.BlockSpec((LP, LANE), lambda i: (0, 0)),
        scratch_shapes=[pltpu.VMEM((LP, LANE), F32)],
        compiler_params=_params(("arbitrary",)))(proj, bf_p)


def _fgate_bwd(proj, bf_p, dc, name):
    LP = proj.shape[0]
    nb = LP // LANE

    def body(f_ref, b_ref, dc_ref, dl_ref, db_ref):
        ri = lax.broadcasted_iota(jnp.int32, (LANE, LANE), 0)
        ci = lax.broadcasted_iota(jnp.int32, (LANE, LANE), 1)
        triu = jnp.where(ri <= ci, 1.0, 0.0).astype(F32)
        bb = b_ref[...]

        tail = jnp.zeros((1, LANE), F32)
        dbs = jnp.zeros((1, LANE), F32)
        for i in range(nb - 1, -1, -1):
            rows = slice(i * LANE, (i + 1) * LANE)
            gb = jnp.dot(triu, dc_ref[rows, :], precision=lax.Precision.HIGHEST,
                         preferred_element_type=F32) + tail
            x = f_ref[rows, :] + bb
            dl = gb * _sigmoid_tail(-x)
            dl_ref[rows, :] = dl.astype(dl_ref.dtype)
            tail = gb[0:1, :]
            dbs = dbs + jnp.sum(dl, axis=0, keepdims=True)
        db_ref[...] = dbs

    return pl.pallas_call(
        body, name=name,
        out_shape=(jax.ShapeDtypeStruct((LP, LANE), BF16), jax.ShapeDtypeStruct((1, LANE), F32)),
        grid=(1,),
        in_specs=[pl.BlockSpec((LP, LANE), lambda i: (0, F_COL_BLK)),
                  pl.BlockSpec((1, LANE), lambda i: (0, 0)),
                  pl.BlockSpec((LP, LANE), lambda i: (0, 0))],
        out_specs=(pl.BlockSpec((LP, LANE), lambda i: (0, 0)), pl.BlockSpec((1, LANE), lambda i: (0, 0))),
        compiler_params=_params(("arbitrary",)))(proj, bf_p, dc)


AUG = 128
ONES_IN_K = HEAD_DIM
ONES_IN_Q = HEAD_DIM + 3
ATT_HEADS_PER_STEP = 8
ATT_HEADS_PER_STEP_BWD = 8


def _attn_prep(proj, c, name):
    LP = proj.shape[0]
    nb = LP // ATT_BLK
    tail_rows = AUG - HEAD_DIM

    def body(q_ref, k_ref, v_ref, c_ref, qT_ref, kT_ref, ka_ref, vT_ref):
        qt = (q_ref[...] * (HEAD_DIM ** -0.5)).T
        kt = k_ref[...].T
        vt = v_ref[...].T
        ct = c_ref[...].T
        hi = ct.astype(BF16).astype(F32)
        r1 = ct - hi
        mid = r1.astype(BF16).astype(F32)
        lo = (r1 - mid).astype(BF16).astype(F32)
        row = lax.broadcasted_iota(jnp.int32, (tail_rows, ATT_BLK), 0)
        ones = jnp.where(row < 3, 1.0, 0.0)
        for h in range(HEADS):
            cparts = jnp.where(row == 0, hi[h:h + 1], jnp.where(row == 1, mid[h:h + 1],
                               jnp.where(row == 2, lo[h:h + 1], 0.0)))
            hs = slice(h * HEAD_DIM, (h + 1) * HEAD_DIM)
            q_tail = cparts + pltpu.roll(ones, 3, 0)
            k_tail = ones - pltpu.roll(cparts, 3, 0)
            qT_ref[h] = jnp.concatenate([qt[hs], q_tail], axis=0).astype(BF16)
            kfull = jnp.concatenate([kt[hs], k_tail], axis=0)
            kT_ref[h] = kfull.astype(BF16)
            ka_ref[h] = kfull.T.astype(BF16)
            vT_ref[h] = vt[hs].astype(BF16)

    col = lambda j: pl.BlockSpec((ATT_BLK, FOX_W), lambda i: (i, j))
    blk = lambda r: pl.BlockSpec((HEADS, None, r, ATT_BLK), lambda i: (0, i, 0, 0))
    return pl.pallas_call(
        body, name=name,
        out_shape=(jax.ShapeDtypeStruct((HEADS, nb, AUG, ATT_BLK), BF16),
                   jax.ShapeDtypeStruct((HEADS, nb, AUG, ATT_BLK), BF16),
                   jax.ShapeDtypeStruct((HEADS, LP, AUG), BF16),
                   jax.ShapeDtypeStruct((HEADS, nb, HEAD_DIM, ATT_BLK), BF16)),
        grid=(nb,), in_specs=[col(0), col(1), col(2), pl.BlockSpec((ATT_BLK, LANE), lambda i: (i, 0))],
        out_specs=(blk(AUG), blk(AUG), pl.BlockSpec((HEADS, ATT_BLK, AUG), lambda i: (0, i, 0)), blk(HEAD_DIM)),
        compiler_params=_params(("parallel",)))(proj, proj, proj, c)


def _attn_rows(xT, scale, out_dtype, name):
    Hh, nb, R, _ = xT.shape

    def body(x_ref, o_ref):
        stack = jnp.concatenate([x_ref[h, 0:HEAD_DIM, :] for h in range(Hh)], axis=0)
        o_ref[...] = (stack * scale).T.astype(o_ref.dtype)

    return pl.pallas_call(
        body, name=name, out_shape=jax.ShapeDtypeStruct((nb * ATT_BLK, Hh * HEAD_DIM), out_dtype), grid=(nb,),
        in_specs=[pl.BlockSpec((Hh, None, R, ATT_BLK), lambda i: (0, i, 0, 0))],
        out_specs=pl.BlockSpec((ATT_BLK, Hh * HEAD_DIM), lambda i: (i, 0)),
        compiler_params=_params(("parallel",)))(xT)


def _attn_cols(x, name):
    LP = x.shape[0]
    nb = LP // ATT_BLK

    def body(x_ref, o_ref):
        xt = x_ref[...].T
        for h in range(HEADS):
            o_ref[h] = xt[h * HEAD_DIM:(h + 1) * HEAD_DIM].astype(o_ref.dtype)

    return pl.pallas_call(
        body, name=name, out_shape=jax.ShapeDtypeStruct((HEADS, nb, HEAD_DIM, ATT_BLK), BF16), grid=(nb,),
        in_specs=[pl.BlockSpec((ATT_BLK, FOX_W), lambda i: (i, 0))],
        out_specs=pl.BlockSpec((HEADS, None, HEAD_DIM, ATT_BLK), lambda i: (0, i, 0, 0)),
        compiler_params=_params(("parallel",)))(x)


def _attn_grads_rows(dqT, dkT, dvT, name):
    Hh, nb, _, _ = dqT.shape
    W3 = Hh * HEAD_DIM

    def body(q_ref, k_ref, v_ref, o_ref, dc_ref):
        for col, (x_ref, scale) in enumerate(((q_ref, HEAD_DIM ** -0.5), (k_ref, 1.0), (v_ref, 1.0))):
            stack = jnp.concatenate([x_ref[h, 0:HEAD_DIM, :] for h in range(Hh)], axis=0)
            o_ref[:, col * W3:(col + 1) * W3] = (stack * scale).T.astype(o_ref.dtype)
        row = lax.broadcasted_iota(jnp.int32, (LANE, ATT_BLK), 0)
        acc = jnp.zeros((LANE, ATT_BLK), F32)
        for h in range(Hh):
            d = q_ref[h, ONES_IN_K:ONES_IN_K + 1, :] - k_ref[h, ONES_IN_Q:ONES_IN_Q + 1, :]
            acc = jnp.where(row == h, d, acc)
        dc_ref[...] = acc.T

    spec = lambda r: pl.BlockSpec((Hh, None, r, ATT_BLK), lambda i: (0, i, 0, 0))
    return pl.pallas_call(
        body, name=name,
        out_shape=(jax.ShapeDtypeStruct((nb * ATT_BLK, 3 * W3), BF16), jax.ShapeDtypeStruct((nb * ATT_BLK, LANE), F32)),
        grid=(nb,), in_specs=[spec(AUG), spec(AUG), spec(HEAD_DIM)],
        out_specs=(pl.BlockSpec((ATT_BLK, 3 * W3), lambda i: (i, 0)), pl.BlockSpec((ATT_BLK, LANE), lambda i: (i, 0))),
        compiler_params=_params(("parallel",)))(dqT, dkT, dvT)


def _attn_fwd2(qT, k_aug, vT, name, comm=None):
    Hh, nb, _, _ = qT.shape
    LP = nb * ATT_BLK
    Dh = vT.shape[2]
    HB = ATT_HEADS_PER_STEP
    n_comm = 0 if comm is None else len(comm[0])

    def body(*refs):
        q_ref, k_ref, v_ref = refs[:3]
        o_ref, lse_ref = refs[3 + n_comm:5 + n_comm]
        if n_comm:
            start, wait = _gather_first_ops(refs[3:3 + n_comm], refs[5 + n_comm:5 + 2 * n_comm],
                                            refs[5 + 2 * n_comm], refs[6 + 2 * n_comm], comm[1])
            pl.when(pl.program_id(0) == 0)(start)
        keys = lax.broadcasted_iota(jnp.int32, (ATT_BLK, ATT_BLK), 0)
        qrys = lax.broadcasted_iota(jnp.int32, (ATT_BLK, ATT_BLK), 1)
        causal = keys <= qrys

        def q_block(i, _):
            def tile(j, carry, masked):
                ks = pl.ds(pl.multiple_of(j * ATT_BLK, ATT_BLK), ATT_BLK)
                s_all = [jnp.dot(k_ref[hh, ks, :], q_ref[hh, i], preferred_element_type=F32) for hh in range(HB)]
                stats, p_all = [], []
                for hh in range(HB):
                    m, l, _ = carry[hh]
                    s = jnp.where(causal, s_all[hh], NEG) if masked else s_all[hh]
                    m_new = jnp.maximum(m, jnp.max(s, axis=0, keepdims=True))
                    p = jnp.exp(s - m_new)
                    alpha = jnp.exp(m - m_new)
                    stats.append((m_new, alpha * l + jnp.sum(p, axis=0, keepdims=True), alpha))
                    p_all.append(p.astype(BF16))
                out = []
                for hh in range(HB):
                    m_new, l, alpha = stats[hh]
                    acc = alpha * carry[hh][2] + jnp.dot(v_ref[hh, j], p_all[hh], preferred_element_type=F32)
                    out.append((m_new, l, acc))
                return tuple(out)

            init = tuple((jnp.full((1, ATT_BLK), NEG, F32), jnp.zeros((1, ATT_BLK), F32),
                          jnp.zeros((Dh, ATT_BLK), F32)) for _ in range(HB))
            carry = lax.fori_loop(0, i, lambda j, cr: tile(j, cr, False), init)
            carry = tile(i, carry, True)
            for hh in range(HB):
                m, l, acc = carry[hh]
                o_ref[hh, i] = acc / l
                lse_ref[hh, i] = m + jnp.log(l)
            return 0

        lax.fori_loop(0, nb, q_block, 0)
        if n_comm:
            pl.when(pl.program_id(0) == Hh // HB - 1)(wait)

    blk = lambda r: pl.BlockSpec((HB, nb, r, ATT_BLK), lambda h: (h, 0, 0, 0))
    out_shape = (jax.ShapeDtypeStruct((Hh, nb, Dh, ATT_BLK), F32), jax.ShapeDtypeStruct((Hh, nb, 1, ATT_BLK), F32))
    scratch = []
    args = [qT, k_aug, vT]
    if n_comm:
        out_shape += tuple(jax.ShapeDtypeStruct(cut.full, s.dtype) for cut, s in zip(comm[1], comm[0]))
        scratch = [pltpu.SemaphoreType.DMA((4 * n_comm,)), pltpu.SemaphoreType.DMA((4 * n_comm,))]
        args += list(comm[0])
    return pl.pallas_call(
        body, name=name, out_shape=out_shape, grid=(Hh // HB,),
        in_specs=[blk(AUG), pl.BlockSpec((HB, LP, AUG), lambda h: (h, 0, 0)), blk(Dh)] + [ANY] * n_comm,
        out_specs=(blk(Dh), blk(1)) + (ANY,) * n_comm, scratch_shapes=scratch,
        compiler_params=_params(("arbitrary",)))(*args)


def _attn_bwd2(qT, kT, k_aug, v, oT, doT, lse, name, comm=None):
    Hh, nb, _, _ = qT.shape
    LP = nb * ATT_BLK
    Dh = v.shape[2]
    nt = (((1,), (1,)), ((), ()))
    tn = (((0,), (0,)), ((), ()))

    HB = ATT_HEADS_PER_STEP_BWD
    n_comm = 0 if comm is None else len(comm[0])

    def body(*refs):
        q_ref, kt_ref, k_ref, v_ref, o_ref, do_ref, lse_ref = refs[:7]
        parts = refs[7:7 + n_comm]
        dq_ref, dk_ref, dv_ref = refs[7 + n_comm:10 + n_comm]
        others = refs[10 + n_comm:10 + 2 * n_comm]
        delta_ref = refs[10 + 2 * n_comm]
        if n_comm:
            start, wait = _chip_exchange_ops(parts, others, refs[11 + 2 * n_comm], refs[12 + 2 * n_comm], comm[1])
            pl.when(pl.program_id(0) == 0)(start)
        keys = lax.broadcasted_iota(jnp.int32, (ATT_BLK, ATT_BLK), 0)
        qrys = lax.broadcasted_iota(jnp.int32, (ATT_BLK, ATT_BLK), 1)
        causal = keys <= qrys

        def prep(i, _):
            for hh in range(HB):
                delta_ref[hh, i] = jnp.sum(do_ref[hh, i].astype(F32) * o_ref[hh, i], axis=0, keepdims=True)
                dq_ref[hh, i] = jnp.zeros((AUG, ATT_BLK), F32)
            return 0

        lax.fori_loop(0, nb, prep, 0)

        def kv_block(j, _):
            ks = pl.ds(pl.multiple_of(j * ATT_BLK, ATT_BLK), ATT_BLK)

            def tile(i, carry, masked):
                s_all = [jnp.dot(k_ref[hh, ks, :], q_ref[hh, i], preferred_element_type=F32) for hh in range(HB)]
                dp_all = [lax.dot_general(v_ref[hh, j], do_ref[hh, i], tn, preferred_element_type=F32)
                          for hh in range(HB)]
                p_all, ds_all = [], []
                for hh in range(HB):
                    s = jnp.where(causal, s_all[hh], NEG) if masked else s_all[hh]
                    p = jnp.exp(s - lse_ref[hh, i])
                    ds_all.append((p * (dp_all[hh] - delta_ref[hh, i])).astype(BF16))
                    p_all.append(p.astype(BF16))
                out = []
                for hh in range(HB):
                    dk, dv = carry[hh]
                    dv = dv + lax.dot_general(do_ref[hh, i], p_all[hh], nt, preferred_element_type=F32)
                    dk = dk + lax.dot_general(q_ref[hh, i], ds_all[hh], nt, preferred_element_type=F32)
                    out.append((dk, dv))
                dq_new = [jnp.dot(kt_ref[hh, j], ds_all[hh], preferred_element_type=F32) for hh in range(HB)]
                for hh in range(HB):
                    dq_ref[hh, i] += dq_new[hh]
                return tuple(out)

            init = tuple((jnp.zeros((AUG, ATT_BLK), F32), jnp.zeros((Dh, ATT_BLK), F32)) for _ in range(HB))
            carry = tile(j, init, True)
            carry = lax.fori_loop(j + 1, nb, lambda i, cr: tile(i, cr, False), carry)
            for hh in range(HB):
                dk_ref[hh, j] = carry[hh][0]
                dv_ref[hh, j] = carry[hh][1]
            return 0

        lax.fori_loop(0, nb, kv_block, 0)
        if n_comm:
            pl.when(pl.program_id(0) == Hh // HB - 1)(wait)

    blk = lambda r: pl.BlockSpec((HB, nb, r, ATT_BLK), lambda h: (h, 0, 0, 0))
    row = lambda cols: pl.BlockSpec((HB, LP, cols), lambda h: (h, 0, 0))
    out_shape = (jax.ShapeDtypeStruct((Hh, nb, AUG, ATT_BLK), F32), jax.ShapeDtypeStruct((Hh, nb, AUG, ATT_BLK), F32),
                 jax.ShapeDtypeStruct((Hh, nb, Dh, ATT_BLK), F32))
    scratch = [pltpu.VMEM((HB, nb, 1, ATT_BLK), F32)]
    args = [qT, kT, k_aug, v, oT, doT, lse]
    if n_comm:
        out_shape += _chip_exchange_shapes(*comm)
        scratch += [pltpu.SemaphoreType.DMA((3 * n_comm,)), pltpu.SemaphoreType.DMA((3 * n_comm,))]
        args += list(comm[0])
    return pl.pallas_call(
        body, name=name, out_shape=out_shape, grid=(Hh // HB,),
        in_specs=[blk(AUG), blk(AUG), row(AUG), blk(Dh), blk(Dh), blk(Dh), blk(1)] + [ANY] * n_comm,
        out_specs=(blk(AUG), blk(AUG), blk(Dh)) + (ANY,) * n_comm,
        scratch_shapes=scratch,
        compiler_params=_params(("arbitrary",)))(*args)


CONV_HALO = 32
A_BLK = 3 * FOX_W // CONV_CH
G_BLK = A_BLK + 1


def _conf_fwd(proj, cw, cb, lg, lb, tm, name):
    LP = proj.shape[0]
    C = CONV_CH
    sub = _sub_rows(tm)
    hpb = tm // CONV_HALO

    def body(a_ref, g_ref, ah_ref, gh_ref, w_ref, cb_ref, lg_ref, lb_ref, u1_ref, u_ref, buf):
        r = pl.program_id(0)
        buf[CONV_HALO:CONV_HALO + tm, :] = a_ref[...] * _sigmoid(g_ref[...])
        buf[0:CONV_HALO, :] = jnp.where(r > 0, ah_ref[...] * _sigmoid(gh_ref[...]), 0.0)
        for s in range(tm // sub):
            for ct in range(C // LANE):
                ln = slice(ct * LANE, (ct + 1) * LANE)
                acc = jnp.broadcast_to(cb_ref[:, ln], (sub, LANE))
                for kk in range(CONV_K):
                    off = CONV_HALO + s * sub - (CONV_K - 1) + kk
                    acc = acc + w_ref[kk:kk + 1, ln] * buf[off:off + sub, ln]
                u1_ref[s * sub:(s + 1) * sub, ln] = acc
        u1 = u1_ref[...]
        mu = jnp.mean(u1, axis=1, keepdims=True)
        xc = u1 - mu
        var = jnp.mean(xc * xc, axis=1, keepdims=True)
        y = xc * lax.rsqrt(var + LN_EPS) * lg_ref[...] + lb_ref[...]
        u_ref[...] = (y * _sigmoid(y)).astype(u_ref.dtype)

    cur = lambda blk: pl.BlockSpec((tm, C), lambda r: (r, blk))
    halo = lambda blk: pl.BlockSpec((CONV_HALO, C), lambda r: (jnp.maximum(r * hpb - 1, 0), blk))
    vec = pl.BlockSpec((1, C), lambda r: (0, 0))
    out = pl.BlockSpec((tm, C), lambda r: (r, 0))
    return pl.pallas_call(
        body, name=name,
        out_shape=(jax.ShapeDtypeStruct((LP, C), F32), jax.ShapeDtypeStruct((LP, C), BF16)),
        grid=(LP // tm,),
        in_specs=[cur(A_BLK), cur(G_BLK), halo(A_BLK), halo(G_BLK),
                  pl.BlockSpec((CONV_HALO, C), lambda r: (0, 0)), vec, vec, vec],
        out_specs=(out, out),
        scratch_shapes=[pltpu.VMEM((CONV_HALO + tm, C), F32)],
        compiler_params=_params(("parallel",)))(proj, proj, proj, proj, cw, cb, lg, lb)


def _conf_bwd(proj, u1, dcat, cw, lg, lb, tm, name, host=None):
    LP = proj.shape[0]
    C = CONV_CH
    sub = _sub_rows(tm)
    hpb = tm // CONV_HALO
    nblk = LP // tm
    last_halo = LP // CONV_HALO - 1

    n_host = 0 if host is None else len(host[0])
    if n_host:
        host_shapes, host_sems, host_ops = _host_plan(host)

    def body(*refs):
        a_ref, g_ref, ah_ref, gh_ref, u1_ref, u1n_ref, du_ref, dun_ref, w_ref, lg_ref, lb_ref = refs[:11]
        dadg_ref, dw_ref, dcb_ref, dlg_ref, dlb_ref = refs[11 + n_host:16 + n_host]
        ubuf, dbuf, du0 = refs[16 + 2 * n_host:19 + 2 * n_host]
        if n_host:
            start, wait = host_ops(refs[11:11 + n_host], refs[16 + n_host:16 + 2 * n_host],
                                   refs[19 + 2 * n_host], refs[20 + 2 * n_host], host[1])
            pl.when(pl.program_id(0) == 0)(start)
        r = pl.program_id(0)
        lgv = lg_ref[...]
        lbv = lb_ref[...]

        def ln_silu_bwd(u1v, duv):
            mu = jnp.mean(u1v, axis=1, keepdims=True)
            xc = u1v - mu
            rstd = lax.rsqrt(jnp.mean(xc * xc, axis=1, keepdims=True) + LN_EPS)
            xhat = xc * rstd
            y = xhat * lgv + lbv
            sg = _sigmoid(y)
            dy = duv * (sg * (1.0 + y * (1.0 - sg)))
            dxh = dy * lgv
            du1 = rstd * (dxh - jnp.mean(dxh, axis=1, keepdims=True)
                          - xhat * jnp.mean(dxh * xhat, axis=1, keepdims=True))
            return du1, dy, xhat

        @pl.when(r == 0)
        def _():
            dw_ref[...] = jnp.zeros_like(dw_ref)
            dcb_ref[...] = jnp.zeros_like(dcb_ref)
            dlg_ref[...] = jnp.zeros_like(dlg_ref)
            dlb_ref[...] = jnp.zeros_like(dlb_ref)

        du1, dy, xhat = ln_silu_bwd(u1_ref[...], du_ref[...])
        dlg_ref[...] += jnp.sum(dy * xhat, axis=0, keepdims=True)
        dlb_ref[...] += jnp.sum(dy, axis=0, keepdims=True)
        dcb_ref[...] += jnp.sum(du1, axis=0, keepdims=True)
        dbuf[0:tm, :] = du1
        du1n, _, _ = ln_silu_bwd(u1n_ref[...], dun_ref[...])
        dbuf[tm:tm + CONV_HALO, :] = jnp.where(r < nblk - 1, du1n, 0.0)
        ubuf[CONV_HALO:CONV_HALO + tm, :] = a_ref[...] * _sigmoid(g_ref[...])
        ubuf[0:CONV_HALO, :] = jnp.where(r > 0, ah_ref[...] * _sigmoid(gh_ref[...]), 0.0)

        for ct in range(C // LANE):
            ln = slice(ct * LANE, (ct + 1) * LANE)
            for s in range(tm // sub):
                d_here = dbuf[s * sub:(s + 1) * sub, ln]
                acc = jnp.zeros((sub, LANE), F32)
                for kk in range(CONV_K):
                    fo = s * sub + (CONV_K - 1) - kk
                    acc = acc + w_ref[kk:kk + 1, ln] * dbuf[fo:fo + sub, ln]
                    bo = CONV_HALO + s * sub - (CONV_K - 1) + kk
                    dw_ref[kk:kk + 1, ln] += jnp.sum(d_here * ubuf[bo:bo + sub, ln], axis=0, keepdims=True)
                du0[s * sub:(s + 1) * sub, ln] = acc
        a = a_ref[...]
        sg = _sigmoid(g_ref[...])
        d0 = du0[...]
        dadg_ref[:, 0:C] = (d0 * sg).astype(dadg_ref.dtype)
        dadg_ref[:, C:2 * C] = (d0 * a * sg * (1.0 - sg)).astype(dadg_ref.dtype)
        if n_host:
            pl.when(pl.program_id(0) == nblk - 1)(wait)

    cur = lambda blk: pl.BlockSpec((tm, C), lambda r: (r, blk))
    prev = lambda blk: pl.BlockSpec((CONV_HALO, C), lambda r: (jnp.maximum(r * hpb - 1, 0), blk))
    nxt = lambda blk: pl.BlockSpec((CONV_HALO, C), lambda r: (jnp.minimum((r + 1) * hpb, last_halo), blk))
    vec = pl.BlockSpec((1, C), lambda r: (0, 0))
    wspec = pl.BlockSpec((CONV_HALO, C), lambda r: (0, 0))
    out_shape = (jax.ShapeDtypeStruct((LP, 2 * C), BF16), jax.ShapeDtypeStruct((CONV_HALO, C), F32),
                 jax.ShapeDtypeStruct((1, C), F32), jax.ShapeDtypeStruct((1, C), F32),
                 jax.ShapeDtypeStruct((1, C), F32))
    out_specs = (pl.BlockSpec((tm, 2 * C), lambda r: (r, 0)), wspec, vec, vec, vec)
    in_specs = [cur(A_BLK), cur(G_BLK), prev(A_BLK), prev(G_BLK), cur(0), nxt(0), cur(1), nxt(1), wspec, vec, vec]
    args = [proj, proj, proj, proj, u1, u1, dcat, dcat, cw, lg, lb]
    scratch = [pltpu.VMEM((CONV_HALO + tm, C), F32), pltpu.VMEM((tm + CONV_HALO, C), F32),
               pltpu.VMEM((tm, C), F32)]
    if n_host:
        anyspec = pl.BlockSpec(memory_space=pl.ANY)
        in_specs += [anyspec] * n_host
        args += list(host[0])
        out_shape += host_shapes
        out_specs += (anyspec,) * n_host
        scratch += [pltpu.SemaphoreType.DMA((host_sems,)), pltpu.SemaphoreType.DMA((host_sems,))]
    return pl.pallas_call(
        body, name=name, out_shape=out_shape, grid=(nblk,), in_specs=in_specs, out_specs=out_specs,
        scratch_shapes=scratch, compiler_params=_params(("arbitrary",)))(*args)


FFN_HALO = 8
FFN_TC = 1408
FFN_ROW_SPLIT = 1
FFN_K = 3


def _ffn_conv(buf, w_ref, b_ref, s, sub, ln):
    acc = jnp.broadcast_to(b_ref[:, ln], (sub, LANE))
    for kk in range(FFN_K):
        off = FFN_HALO + s * sub - (FFN_K - 1) + kk
        acc = acc + w_ref[kk:kk + 1, ln] * buf[off:off + sub, ln]
    return acc


def _ffn_act_fwd(up, w, b, tm, name, host=None):
    LP, F = up.shape[0], up.shape[1] // 2
    tm = tm // FFN_ROW_SPLIT
    upg = upv = up
    nct = F // FFN_TC
    sub = _sub_rows(tm)
    hpb = tm // FFN_HALO
    n_host = 0 if host is None else len(host[0])
    nrb = LP // tm

    def body(*refs):
        g_ref, v_ref, gh_ref, vh_ref, wg_ref, wv_ref, bg_ref, bv_ref = refs[:8]
        act_ref = refs[8 + n_host]
        gbuf, vbuf = refs[9 + 2 * n_host:11 + 2 * n_host]
        if n_host:
            start, wait = _gather_first_ops(refs[8:8 + n_host], refs[9 + n_host:9 + 2 * n_host],
                                            refs[11 + 2 * n_host], refs[12 + 2 * n_host], host[1])
            pl.when(jnp.logical_and(pl.program_id(0) == 0, pl.program_id(1) == 0))(start)
        r = pl.program_id(1)
        gbuf[FFN_HALO:FFN_HALO + tm, :] = g_ref[...]
        vbuf[FFN_HALO:FFN_HALO + tm, :] = v_ref[...]
        gbuf[0:FFN_HALO, :] = jnp.where(r > 0, gh_ref[...], 0.0)
        vbuf[0:FFN_HALO, :] = jnp.where(r > 0, vh_ref[...], 0.0)
        for s in range(tm // sub):
            for ct in range(FFN_TC // LANE):
                ln = slice(ct * LANE, (ct + 1) * LANE)
                gc = _ffn_conv(gbuf, wg_ref, bg_ref, s, sub, ln)
                vc = _ffn_conv(vbuf, wv_ref, bv_ref, s, sub, ln)
                act_ref[s * sub:(s + 1) * sub, ln] = (gc * _sigmoid(gc) * vc).astype(act_ref.dtype)
        if n_host:
            pl.when(jnp.logical_and(pl.program_id(0) == nct - 1, pl.program_id(1) == nrb - 1))(wait)

    cur = pl.BlockSpec((tm, FFN_TC), lambda c, r: (r, c))
    halo = pl.BlockSpec((FFN_HALO, FFN_TC), lambda c, r: (jnp.maximum(r * hpb - 1, 0), c))
    wg = pl.BlockSpec((8, FFN_TC), lambda c, r: (0, c))
    wv = pl.BlockSpec((8, FFN_TC), lambda c, r: (0, nct + c))
    bg = pl.BlockSpec((1, FFN_TC), lambda c, r: (0, c))
    bv = pl.BlockSpec((1, FFN_TC), lambda c, r: (0, nct + c))
    curv = pl.BlockSpec((tm, FFN_TC), lambda c, r: (r, nct + c))
    halov = pl.BlockSpec((FFN_HALO, FFN_TC), lambda c, r: (jnp.maximum(r * hpb - 1, 0), nct + c))
    out_shape = jax.ShapeDtypeStruct((LP, F), BF16)
    out_specs = cur
    in_specs = [cur, curv, halo, halov, wg, wv, bg, bv]
    args = [upg, upv, upg, upv, w, w, b, b]
    scratch = [pltpu.VMEM((FFN_HALO + tm, FFN_TC), F32)] * 2
    if n_host:
        anyspec = pl.BlockSpec(memory_space=pl.ANY)
        in_specs += [anyspec] * n_host
        args += list(host[0])
        out_shape = (out_shape,) + tuple(jax.ShapeDtypeStruct(cut.full, s.dtype) for cut, s in zip(host[1], host[0]))
        out_specs = (cur,) + (anyspec,) * n_host
        scratch += [pltpu.SemaphoreType.DMA((4 * n_host,)), pltpu.SemaphoreType.DMA((4 * n_host,))]
    sem = ("arbitrary", "arbitrary") if n_host else ("parallel", "parallel")
    return pl.pallas_call(
        body, name=name, out_shape=out_shape, grid=(nct, nrb), in_specs=in_specs, out_specs=out_specs,
        scratch_shapes=scratch, compiler_params=_params(sem))(*args)


def _ffn_act_bwd(up, dact, w, b, tm, name, comm=None):
    LP, F = up.shape[0], up.shape[1] // 2
    tm = tm // FFN_ROW_SPLIT
    upg = upv = up
    nct = F // FFN_TC
    sub = _sub_rows(tm)
    hpb = tm // FFN_HALO
    nblk = LP // tm
    last_halo = LP // FFN_HALO - 1
    TB = tm + 2 * FFN_HALO
    n_comm = 0 if comm is None else len(comm[0])

    def body(*refs):
        (g_ref, v_ref, gp_ref, vp_ref, gn_ref, vn_ref, da_ref, dan_ref,
         wg_ref, wv_ref, bg_ref, bv_ref) = refs[:12]
        dup_ref, dwg_ref, dwv_ref, dbg_ref, dbv_ref = refs[12 + n_comm:17 + n_comm]
        gbuf, vbuf, dgb, dvb = refs[17 + 2 * n_comm:21 + 2 * n_comm]
        if n_comm:
            start, wait = _chip_exchange_ops(refs[12:12 + n_comm], refs[17 + n_comm:17 + 2 * n_comm],
                                             refs[21 + 2 * n_comm], refs[22 + 2 * n_comm], comm[1])
            pl.when(jnp.logical_and(pl.program_id(0) == 0, pl.program_id(1) == 0))(start)
        r = pl.program_id(1)
        dg_ref = dup_ref.at[0]
        dv_ref = dup_ref.at[1]
        first = r == 0
        last = r == nblk - 1

        @pl.when(first)
        def _():
            dwg_ref[...] = jnp.zeros_like(dwg_ref)
            dwv_ref[...] = jnp.zeros_like(dwv_ref)
            dbg_ref[...] = jnp.zeros_like(dbg_ref)
            dbv_ref[...] = jnp.zeros_like(dbv_ref)

        for buf, c_ref, p_ref, n_ref in ((gbuf, g_ref, gp_ref, gn_ref), (vbuf, v_ref, vp_ref, vn_ref)):
            buf[0:FFN_HALO, :] = jnp.where(first, 0.0, p_ref[...])
            buf[FFN_HALO:FFN_HALO + tm, :] = c_ref[...]
            buf[FFN_HALO + tm:TB, :] = jnp.where(last, 0.0, n_ref[...])

        def dconv(s0, nrows, ln, dact_v):
            xg = [gbuf[s0 - (FFN_K - 1) + kk:s0 - (FFN_K - 1) + kk + nrows, ln] for kk in range(FFN_K)]
            xv = [vbuf[s0 - (FFN_K - 1) + kk:s0 - (FFN_K - 1) + kk + nrows, ln] for kk in range(FFN_K)]
            gc = jnp.broadcast_to(bg_ref[:, ln], (nrows, LANE))
            vc = jnp.broadcast_to(bv_ref[:, ln], (nrows, LANE))
            for kk in range(FFN_K):
                gc = gc + wg_ref[kk:kk + 1, ln] * xg[kk]
                vc = vc + wv_ref[kk:kk + 1, ln] * xv[kk]
            sg = _sigmoid(gc)
            return dact_v * vc * (sg * (1.0 + gc * (1.0 - sg))), dact_v * (gc * sg), xg, xv

        colsum = lambda t: jnp.sum(t, axis=0, keepdims=True)
        for ct in range(FFN_TC // LANE):
            ln = slice(ct * LANE, (ct + 1) * LANE)
            zero = jnp.zeros((1, LANE), F32)
            dwg, dwv, dbg, dbv = [zero] * FFN_K, [zero] * FFN_K, zero, zero
            for s in range(tm // sub):
                dgc, dvc, xg, xv = dconv(FFN_HALO + s * sub, sub, ln, da_ref[s * sub:(s + 1) * sub, ln])
                dgb[s * sub:(s + 1) * sub, ln] = dgc
                dvb[s * sub:(s + 1) * sub, ln] = dvc
                dwg = [dwg[kk] + colsum(dgc * xg[kk]) for kk in range(FFN_K)]
                dwv = [dwv[kk] + colsum(dvc * xv[kk]) for kk in range(FFN_K)]
                dbg, dbv = dbg + colsum(dgc), dbv + colsum(dvc)
            for kk in range(FFN_K):
                dwg_ref[kk:kk + 1, ln] += dwg[kk]
                dwv_ref[kk:kk + 1, ln] += dwv[kk]
            dbg_ref[:, ln] += dbg
            dbv_ref[:, ln] += dbv
            dgc, dvc, _, _ = dconv(FFN_HALO + tm, FFN_HALO, ln, jnp.where(last, 0.0, dan_ref[:, ln]))
            dgb[tm:tm + FFN_HALO, ln] = dgc
            dvb[tm:tm + FFN_HALO, ln] = dvc
            for dbuf, w_ref, dout in ((dgb, wg_ref, dg_ref), (dvb, wv_ref, dv_ref)):
                for s in range(tm // sub):
                    acc = jnp.zeros((sub, LANE), F32)
                    for kk in range(FFN_K):
                        fo = s * sub + (FFN_K - 1) - kk
                        acc = acc + w_ref[kk:kk + 1, ln] * dbuf[fo:fo + sub, ln]
                    dout[s * sub:(s + 1) * sub, ln] = acc.astype(dout.dtype)
        if n_comm:
            pl.when(jnp.logical_and(pl.program_id(0) == nct - 1, pl.program_id(1) == nblk - 1))(wait)

    cur = pl.BlockSpec((tm, FFN_TC), lambda c, r: (r, c))
    prev = pl.BlockSpec((FFN_HALO, FFN_TC), lambda c, r: (jnp.maximum(r * hpb - 1, 0), c))
    nxt = pl.BlockSpec((FFN_HALO, FFN_TC), lambda c, r: (jnp.minimum((r + 1) * hpb, last_halo), c))
    wg = pl.BlockSpec((8, FFN_TC), lambda c, r: (0, c))
    wv = pl.BlockSpec((8, FFN_TC), lambda c, r: (0, nct + c))
    bg = pl.BlockSpec((1, FFN_TC), lambda c, r: (0, c))
    bv = pl.BlockSpec((1, FFN_TC), lambda c, r: (0, nct + c))
    curv = pl.BlockSpec((tm, FFN_TC), lambda c, r: (r, nct + c))
    prevv = pl.BlockSpec((FFN_HALO, FFN_TC), lambda c, r: (jnp.maximum(r * hpb - 1, 0), nct + c))
    nxtv = pl.BlockSpec((FFN_HALO, FFN_TC), lambda c, r: (jnp.minimum((r + 1) * hpb, last_halo), nct + c))
    out_shape = (jax.ShapeDtypeStruct((2, LP, F), BF16),
                 jax.ShapeDtypeStruct((8, F), F32), jax.ShapeDtypeStruct((8, F), F32),
                 jax.ShapeDtypeStruct((1, F), F32), jax.ShapeDtypeStruct((1, F), F32))
    out_specs = (pl.BlockSpec((2, tm, FFN_TC), lambda c, r: (0, r, c)),
                 pl.BlockSpec((8, FFN_TC), lambda c, r: (0, c)),
                 pl.BlockSpec((8, FFN_TC), lambda c, r: (0, c)),
                 pl.BlockSpec((1, FFN_TC), lambda c, r: (0, c)),
                 pl.BlockSpec((1, FFN_TC), lambda c, r: (0, c)))
    in_specs = [cur, curv, prev, prevv, nxt, nxtv, cur, nxt, wg, wv, bg, bv]
    args = [upg, upv, upg, upv, upg, upv, dact, dact, w, w, b, b]
    scratch = [pltpu.VMEM((TB, FFN_TC), F32), pltpu.VMEM((TB, FFN_TC), F32),
               pltpu.VMEM((tm + FFN_HALO, FFN_TC), F32), pltpu.VMEM((tm + FFN_HALO, FFN_TC), F32)]
    if n_comm:
        anyspec = pl.BlockSpec(memory_space=pl.ANY)
        in_specs += [anyspec] * n_comm
        args += list(comm[0])
        out_shape += _chip_exchange_shapes(*comm)
        out_specs += (anyspec,) * n_comm
        scratch += [pltpu.SemaphoreType.DMA((3 * n_comm,)), pltpu.SemaphoreType.DMA((3 * n_comm,))]
    sem = ("arbitrary", "arbitrary") if n_comm else ("parallel", "arbitrary")
    dup, dwg, dwv, dbg, dbv, *others = pl.pallas_call(
        body, name=name, out_shape=out_shape, grid=(nct, nblk), in_specs=in_specs, out_specs=out_specs,
        scratch_shapes=scratch, compiler_params=_params(sem))(*args)
    return (dup, jnp.concatenate([dwg, dwv], axis=1), jnp.concatenate([dbg, dbv], axis=1)) + tuple(others)


POOL_HALO = 16


def _pool_fwd(h, g, pw, pb, ps, tm, name):
    LP, Dm = h.shape
    sub = _sub_rows(tm)
    hpb = tm // POOL_HALO

    def body(h_ref, hh_ref, g_ref, pw_ref, pb_ref, ps_ref, o_ref, d_ref, buf):
        r = pl.program_id(0)
        gg = g_ref[...]

        def norm(x):
            return x * lax.rsqrt(jnp.mean(x * x, axis=1, keepdims=True) + RMS_EPS) * gg

        x = h_ref[...]
        buf[POOL_HALO:POOL_HALO + tm, :] = norm(x)
        buf[0:POOL_HALO, :] = jnp.where(r > 0, norm(hh_ref[...]), 0.0)
        for gi, w in enumerate(POOL_WINDOWS):
            ln = slice(gi * POOL_G, (gi + 1) * POOL_G)
            for s in range(tm // sub):
                base = POOL_HALO + s * sub
                acc = buf[base:base + sub, ln]
                for jj in range(1, w):
                    acc = acc + buf[base - jj:base - jj + sub, ln]
                t = r * tm + s * sub + lax.broadcasted_iota(jnp.int32, (sub, 1), 0)
                cnt = jnp.minimum(t + 1, w).astype(F32)
                d_ref[s * sub:(s + 1) * sub, ln] = (acc / cnt - buf[base:base + sub, ln]).astype(d_ref.dtype)
            y = jnp.dot(d_ref[:, ln], pw_ref[gi], preferred_element_type=F32) + pb_ref[:, ln]
            o_ref[:, ln] = x[:, ln] + y * ps_ref[:, ln]

    row = pl.BlockSpec((tm, Dm), lambda r: (r, 0))
    halo = pl.BlockSpec((POOL_HALO, Dm), lambda r: (jnp.maximum(r * hpb - 1, 0), 0))
    vec = pl.BlockSpec((1, Dm), lambda r: (0, 0))
    wsp = pl.BlockSpec((len(POOL_WINDOWS), POOL_G, POOL_G), lambda r: (0, 0, 0))
    return pl.pallas_call(
        body, name=name,
        out_shape=(jax.ShapeDtypeStruct((LP, Dm), F32), jax.ShapeDtypeStruct((LP, Dm), BF16)),
        grid=(LP // tm,), in_specs=[row, halo, vec, wsp, vec, vec], out_specs=(row, row),
        scratch_shapes=[pltpu.VMEM((POOL_HALO + tm, Dm), F32)],
        compiler_params=_params(("parallel",)))(h, h, g, pw, pb, ps)


def _pool_bwd(h, g, d, pw, pb, ps, dh_out, tm, name):
    LP, Dm = h.shape
    sub = _sub_rows(tm)
    hpb = tm // POOL_HALO
    nblk = LP // tm
    last_halo = LP // POOL_HALO - 1
    nt = (((1,), (1,)), ((), ()))
    tn = (((0,), (0,)), ((), ()))

    def body(h_ref, g_ref, d_ref, pw_ref, pb_ref, ps_ref, do_ref, don_ref,
             dh_ref, dpw_ref, dpb_ref, dps_ref, dg_ref, ebuf, ddb, dnb):
        r = pl.program_id(0)

        @pl.when(r == 0)
        def _():
            dpw_ref[...] = jnp.zeros_like(dpw_ref)
            dpb_ref[...] = jnp.zeros_like(dpb_ref)
            dps_ref[...] = jnp.zeros_like(dps_ref)
            dg_ref[...] = jnp.zeros_like(dg_ref)

        for gi, w in enumerate(POOL_WINDOWS):
            ln = slice(gi * POOL_G, (gi + 1) * POOL_G)
            wg = pw_ref[gi]
            dog = do_ref[:, ln]
            dg_b = d_ref[:, ln]
            y_pre = jnp.dot(dg_b, wg, preferred_element_type=F32) + pb_ref[:, ln]
            dps_ref[:, ln] += jnp.sum(dog * y_pre, axis=0, keepdims=True)
            dy = dog * ps_ref[:, ln]
            dpb_ref[:, ln] += jnp.sum(dy, axis=0, keepdims=True)
            dyb = dy.astype(BF16)
            dpw_ref[gi] += lax.dot_general(dg_b, dyb, tn, preferred_element_type=F32)
            dd = lax.dot_general(dyb, wg, nt, preferred_element_type=F32)
            ddb[:, ln] = dd
            t = r * tm + lax.broadcasted_iota(jnp.int32, (tm, 1), 0)
            ebuf[0:tm, ln] = dd / jnp.minimum(t + 1, w).astype(F32)
            dyn = (don_ref[:, ln] * ps_ref[:, ln]).astype(BF16)
            ddn = lax.dot_general(dyn, wg, nt, preferred_element_type=F32)
            tn_ = (r + 1) * tm + lax.broadcasted_iota(jnp.int32, (POOL_HALO, 1), 0)
            ebuf[tm:tm + POOL_HALO, ln] = jnp.where(r < nblk - 1, ddn / jnp.minimum(tn_ + 1, w).astype(F32), 0.0)
            for s in range(tm // sub):
                acc = ebuf[s * sub:(s + 1) * sub, ln]
                for jj in range(1, w):
                    acc = acc + ebuf[s * sub + jj:s * sub + jj + sub, ln]
                dnb[s * sub:(s + 1) * sub, ln] = acc - ddb[s * sub:(s + 1) * sub, ln]
        x = h_ref[...]
        rr = lax.rsqrt(jnp.mean(x * x, axis=1, keepdims=True) + RMS_EPS)
        xhat = x * rr
        dn = dnb[...]
        dxh = dn * g_ref[...]
        dh_ref[...] = do_ref[...] + rr * (dxh - xhat * jnp.mean(dxh * xhat, axis=1, keepdims=True))
        dg_ref[...] += jnp.sum(dn * xhat, axis=0, keepdims=True)

    row = pl.BlockSpec((tm, Dm), lambda r: (r, 0))
    nxt = pl.BlockSpec((POOL_HALO, Dm), lambda r: (jnp.minimum((r + 1) * hpb, last_halo), 0))
    vec = pl.BlockSpec((1, Dm), lambda r: (0, 0))
    wsp = pl.BlockSpec((len(POOL_WINDOWS), POOL_G, POOL_G), lambda r: (0, 0, 0))
    return pl.pallas_call(
        body, name=name,
        out_shape=(jax.ShapeDtypeStruct((LP, Dm), F32),
                   jax.ShapeDtypeStruct((len(POOL_WINDOWS), POOL_G, POOL_G), F32),
                   jax.ShapeDtypeStruct((1, Dm), F32), jax.ShapeDtypeStruct((1, Dm), F32),
                   jax.ShapeDtypeStruct((1, Dm), F32)),
        grid=(nblk,), in_specs=[row, vec, row, wsp, vec, vec, row, nxt],
        out_specs=(row, wsp, vec, vec, vec),
        scratch_shapes=[pltpu.VMEM((tm + POOL_HALO, Dm), F32), pltpu.VMEM((tm, Dm), F32),
                        pltpu.VMEM((tm, Dm), F32)],
        compiler_params=_params(("arbitrary",)))(h, g, d, pw, pb, ps, dh_out, dh_out)


def _adamw(w, g, m, v, name):
    shape = w.shape
    cols = shape[-1]
    rows = int(np.prod(shape[:-1])) if len(shape) > 1 else 1
    w2, g2, m2, v2 = (t.reshape(rows, cols) for t in (w, g, m, v))
    tr = rows
    for cand in (256, 128, 64, 32, 16, 8):
        if rows % cand == 0 and rows > cand:
            tr = cand
            break
    c1 = float(1.0 - ADAM_B1 ** ADAM_STEP)
    c2 = float(1.0 - ADAM_B2 ** ADAM_STEP)

    def body(w_ref, g_ref, m_ref, v_ref, d_ref, mo_ref, vo_ref):
        gg = g_ref[...]
        mn = ADAM_B1 * m_ref[...] + (1.0 - ADAM_B1) * gg
        vn = ADAM_B2 * v_ref[...] + (1.0 - ADAM_B2) * (gg * gg)
        m_hat = mn / c1
        v_hat = vn / c2
        d_ref[...] = -ADAM_LR * (m_hat / (jnp.sqrt(v_hat) + ADAM_EPS) + ADAM_WD * w_ref[...])
        mo_ref[...] = mn
        vo_ref[...] = vn

    spec = pl.BlockSpec((tr, cols), lambda i: (i, 0))
    sds = jax.ShapeDtypeStruct((rows, cols), F32)
    d2, mo, vo = pl.pallas_call(
        body, name=name, out_shape=(sds, sds, sds), grid=(rows // tr,),
        in_specs=[spec] * 4, out_specs=(spec,) * 3,
        compiler_params=_params(("parallel",)))(w2, g2, m2, v2)
    return d2.reshape(shape), mo.reshape(shape), vo.reshape(shape)


def _row_tiles(LP):
    tm = LP // 4
    assert LP % 4 == 0 and tm % CONV_HALO == 0 and LP % ATT_BLK == 0, LP
    return tm, LP // 2


MESH = pl.DeviceIdType.MESH
ANY = pl.BlockSpec(memory_space=pl.ANY)


def _coords():
    return lax.axis_index("x"), lax.axis_index("y"), lax.axis_index("c")


def _other_chips(x, y):
    return [(1 - x, y), (x, 1 - y), (1 - x, 1 - y)]


def _allreduce_small(pack):
    Rs, C = pack.shape
    n_dev = 8

    def body(x_ref, o_ref, buf, send_sems, recv_sems):
        x, y, c = _coords()
        me = 4 * x + 2 * y + c
        buf[me] = x_ref[...]
        peers = []
        for rel in range(1, n_dev):
            px = 1 - x if rel & 4 else x
            py = 1 - y if rel & 2 else y
            pc = 1 - c if rel & 1 else c
            peers.append((px, py, pc))
        sends = [pltpu.make_async_remote_copy(
            src_ref=x_ref, dst_ref=buf.at[me], send_sem=send_sems.at[k], recv_sem=recv_sems.at[k],
            device_id=peer, device_id_type=MESH) for k, peer in enumerate(peers)]
        for cp in sends:
            cp.start()
        for k, (px, py, pc) in enumerate(peers):
            pltpu.make_async_remote_copy(
                src_ref=x_ref, dst_ref=buf.at[4 * px + 2 * py + pc], send_sem=send_sems.at[k],
                recv_sem=recv_sems.at[k], device_id=(px, py, pc), device_id_type=MESH).wait_recv()
        for cp in sends:
            cp.wait_send()
        acc = buf[0]
        for d in range(1, n_dev):
            acc = acc + buf[d]
        o_ref[...] = acc

    vm = pl.BlockSpec(memory_space=pltpu.VMEM)
    return pl.pallas_call(
        body, name="allreduce_replicated", out_shape=jax.ShapeDtypeStruct((Rs, C), F32),
        in_specs=[vm], out_specs=vm,
        scratch_shapes=[pltpu.VMEM((n_dev, Rs, C), F32), pltpu.SemaphoreType.DMA((n_dev - 1,)),
                        pltpu.SemaphoreType.DMA((n_dev - 1,))],
    )(pack)


REPLICATED = ("mix_norm_even", "b_f", "conv_b", "ln_g", "ln_b", "ffn_norm", "ffn_conv_b", "final_norm")


def _pad_rows(flat, align_rows, cols):
    rows = -(-flat.shape[-1] // cols)
    rows = -(-rows // align_rows) * align_rows
    pad = rows * cols - flat.shape[-1]
    flat = jnp.pad(flat, [(0, 0)] * (flat.ndim - 1) + [(0, pad)])
    return flat.reshape(flat.shape[:-1] + (rows, cols))


def _pack_replicated(grads, loss):
    parts = [_pad_rows(grads[name].astype(F32).reshape(-1), 1, LANE).reshape(-1) for name in REPLICATED]
    parts.append(_pad_rows(loss.reshape(-1)[:1], 1, LANE).reshape(-1))
    return _pad_rows(jnp.concatenate(parts), 8, LANE)


def _unpack_replicated(reduced, shapes):
    flat = reduced.reshape(-1)
    out, off = {}, 0
    for name in REPLICATED:
        n = int(np.prod(shapes[name]))
        out[name] = flat[off:off + n].reshape(shapes[name])
        off += -(-n // LANE) * LANE
    return out, flat[off]


def _ffn_fwd2(h, W, layer, tm, tmm, host_up=None, host_act=None):
    tag = str(layer)
    n = _rms_fwd(h, W["ffn_norm"][layer:layer + 1], BF16, tm, f"ffn_norm_{tag}")
    up, *g_up = _mm(n, W["w_up"][layer], "nn", F32, tmm, UP_SHARD, f"ffn_up_{tag}", host=host_up) \
        if host_up else (_mm(n, W["w_up"][layer], "nn", F32, tmm, UP_SHARD, f"ffn_up_{tag}"),)
    act, *g_act = _ffn_act_fwd(up, W["ffn_conv_w_p"][layer], W["ffn_conv_b"][layer:layer + 1], tm,
                               f"ffn_act_{tag}", host=host_act) \
        if host_act else (_ffn_act_fwd(up, W["ffn_conv_w_p"][layer], W["ffn_conv_b"][layer:layer + 1], tm,
                                       f"ffn_act_{tag}"),)
    out = _mm(act, W["w_down"][layer], "nn", F32, tm, D_MODEL, f"ffn_down_{tag}", add=h)
    return out, (n, up, act), g_up + g_act


def _ffn_bwd2(h, W, layer, saved, dout, tm, tmm, reduce=None):
    tag = str(layer)
    n, up, act = saved
    parts, comm = [], None
    if reduce is None:
        dact = _mm(dout, W["w_down"][layer], "nt", F32, tmm, UP_SHARD, f"ffn_dact_{tag}")
    else:
        names, fulls, cuts = reduce
        dact, *recv = _mm(dout, W["w_down"][layer], "nt", F32, tmm, UP_SHARD, f"ffn_dact_{tag}",
                          host=(fulls, cuts, "pairx"))
        parts = [_pair_sum2(f, r, cut, PAIR_SUM_BLOCKS[nm], "grad_pair_sum_" + nm)
                 for f, r, cut, nm in zip(fulls, recv, cuts, names)]
        comm = (parts, cuts)
    dwd = _mm(act, dout, "tn", F32, D_FF // 2, 512, f"ffn_dwdown_{tag}")
    dup, dcw, dcb, *others = _ffn_act_bwd(up, dact, W["ffn_conv_w_p"][layer], W["ffn_conv_b"][layer:layer + 1],
                                          tm, f"ffn_act_bwd_{tag}", comm=comm)
    dn = _mm_ffn_dn(dup, W["w_up"][layer], tm, D_MODEL, f"ffn_dn_{tag}")
    dwu = _mm_ffn_dwup(n, dup, 512, D_FF // 2, f"ffn_dwup_{tag}")
    dh, dgain = _rms_bwd(h, W["ffn_norm"][layer:layer + 1], dn, dout, tm, f"ffn_norm_bwd_{tag}")
    return dh, (dwu, dwd), dict(gain=dgain, cw=dcw[:FFN_K], cb=dcb), parts, others


GATHER_FIRST = ("w_in", "small")
GATHER_LATE = ("pool_w", "w_up", "w_down")
HOSTED_FFN = ("w_up1", "w_down1")
HOSTED = ("w_out", "pool_w", "w_up0", "w_down0")
LATE = ("small",)


def _local_step2(h0, tgt, W, n_real, cut_of):
    LP = h0.shape[0]
    tm, tmm = _row_tiles(LP)
    nb = LP // ATT_BLK
    G = {}
    n0 = _rms_fwd(h0, W["mix_norm_even"], BF16, tm, "mix_norm_even")
    sh = W["late_shards"]
    stage = lambda *names: ([sh[n] for n in names], [cut_of[n] for n in names])
    proj, g_down0 = _mm(n0, W["w_in_p"], "nn", F32, tmm, 896, "in_proj", host=stage("w_down0"))
    c = _fgate_fwd(proj, W["b_f_p"], "forget_gate")
    qT, kT, k_aug, vT = _attn_prep(proj, c, "attention_operands")
    oT, lse, g_pool, g_up0, g_out = _attn_fwd2(qT, k_aug, vT, "fox_attention",
                                               comm=stage("pool_w", "w_up0", "w_out"))
    g_down0, g_pool, g_up0, g_out = _gather_forward(
        [g_down0, g_pool, g_up0, g_out], stage("w_down0", "pool_w", "w_up0", "w_out")[1], "gather_forward_0")
    W = dict(W)
    W.update(pool_w=g_pool, w_up=[g_up0, None], w_down=[g_down0, None], w_out=g_out)
    u1, u = _conf_fwd(proj, W["conv_w_p"], W["conv_b"], W["ln_g"], W["ln_b"], tm, "conformer")
    cat = jnp.concatenate([_attn_rows(oT, 1.0, BF16, "attention_rows"), u], axis=1)
    h1 = _mm(cat, W["w_out"], "nn", F32, tmm, D_MODEL, "out_proj", add=h0)
    h2, ffn0, (g_down1, g_up1) = _ffn_fwd2(h1, W, 0, tm, tmm, host_up=stage("w_down1"), host_act=stage("w_up1"))
    g_down1, g_up1 = _gather_forward([g_down1, g_up1], stage("w_down1", "w_up1")[1], "gather_forward_1")
    W.update(w_up=[g_up0, g_up1], w_down=[g_down0, g_down1])
    h3, dpool = _pool_fwd(h2, W["mix_norm_odd"], W["pool_w"], W["pool_b"], W["pool_scale"], tm, "pool_mixer")
    h4, ffn1, _ = _ffn_fwd2(h3, W, 1, tm, tmm)
    loss, dh4, G["final_norm"] = _loss_head(h4, W["final_norm"], tgt, n_real, tm, "loss_head")

    dh3, (G["w_up1"], G["w_down1"]), g1, _, _ = _ffn_bwd2(h3, W, 1, ffn1, dh4, tm, tmm)
    dh2, G["pool_w"], G["pool_b"], G["pool_scale"], G["mix_norm_odd"] = _pool_bwd(
        h2, W["mix_norm_odd"], dpool, W["pool_w"], W["pool_b"], W["pool_scale"], dh3, tm, "pool_mixer_bwd")
    cuts1 = [cut_of[n] for n in HOSTED_FFN]
    dh1, (G["w_up0"], G["w_down0"]), g0, parts1, others1 = _ffn_bwd2(
        h1, W, 0, ffn0, dh2, tm, tmm, reduce=(HOSTED_FFN, [G[n] for n in HOSTED_FFN], cuts1))
    G["ffn_norm"] = jnp.concatenate([g0["gain"], g1["gain"]], axis=0)
    G["ffn_conv_w"] = jnp.stack([g0["cw"], g1["cw"]])
    G["ffn_conv_b"] = jnp.concatenate([g0["cb"], g1["cb"]], axis=0)

    dcat = _mm(dh1, W["w_out"], "nt", F32, tmm, D_MODEL, "out_proj_dx")
    G["w_out"] = _mm(cat, dh1, "tn", F32, 512, D_MODEL, "out_proj_dw")
    hcuts = [cut_of[n] for n in HOSTED]
    hfull = [G[n] for n in HOSTED]
    dadg, dcw, G["conv_b"], G["ln_g"], G["ln_b"], *hrecv = _conf_bwd(
        proj, u1, dcat, W["conv_w_p"], W["ln_g"], W["ln_b"], tm, "conformer_bwd", host=(hfull, hcuts, "pairx"))
    G["conv_w"] = dcw[:CONV_K]
    doT = _attn_cols(dcat, "attention_do_cols")
    hparts = [_pair_sum2(f, r, cut, PAIR_SUM_BLOCKS[n], "grad_pair_sum_" + n)
              for f, r, cut, n in zip(hfull, hrecv, hcuts, HOSTED)]
    dqT, dkT, dvT, *hothers = _attn_bwd2(qT, kT, k_aug, vT, oT, doT, lse, "fox_attention_bwd",
                                         comm=(hparts, hcuts))
    dqkv, dc = _attn_grads_rows(dqT, dkT, dvT, "attention_grads_rows")
    dfl, dbf = _fgate_bwd(proj, W["b_f_p"], dc, "forget_gate_bwd")
    G["b_f"] = dbf[:, :HEADS]
    dproj = jnp.concatenate([dqkv, dadg, dfl], axis=1)
    gp = _mm(n0, dproj, "tn", F32, 512, 896, "in_proj_dw")
    g_w_in = jnp.concatenate([gp[:, :3 * FOX_W], gp[:, 3 * FOX_W + 2 * CONV_CH:3 * FOX_W + 2 * CONV_CH + HEADS],
                              gp[:, 3 * FOX_W:3 * FOX_W + 2 * CONV_CH]], axis=1)
    g_w_in = g_w_in.reshape(D_MODEL, N_CHIPS, IN_SHARD).transpose(1, 0, 2)
    icut = [cut_of["w_in"]]
    dn0, irecv = _mm(dproj, W["w_in_p"], "nt", F32, tmm, D_MODEL, "in_proj_dx", host=([g_w_in], icut, "pairx"))
    ipart = _pair_sum2(g_w_in, irecv, icut[0], PAIR_SUM_BLOCKS["w_in"], "grad_pair_sum_w_in")
    dh0, G["mix_norm_even"], iother = _rms_bwd(h0, W["mix_norm_even"], dn0, dh1, tm, "mix_norm_even_bwd",
                                               host=([ipart], icut, "chipx"))
    parts = dict(zip(HOSTED_FFN + HOSTED + ("w_in",), parts1 + hparts + [ipart]))
    others = dict(zip(HOSTED_FFN + HOSTED + ("w_in",), list(others1) + list(hothers) + [iother]))
    return loss, dh0, G, parts, others


class _Cut:
    def __init__(self, full_shape, chip_dim, half_dim):
        self.full = tuple(full_shape)
        self.chip_dim, self.half_dim = chip_dim, half_dim
        self.chip_size = full_shape[chip_dim] // N_CHIPS
        self.half_size = full_shape[half_dim] // 2
        assert chip_dim != half_dim

    def shape(self, chip=False, half=False):
        s = list(self.full)
        if chip:
            s[self.chip_dim] = self.chip_size
        if half:
            s[self.half_dim] = self.half_size
        return tuple(s)

    def region(self, ref, chip=None, half=None):
        idx = [pl.ds(0, n) for n in ref.shape]
        if chip is not None:
            idx[self.chip_dim] = pl.ds(chip * self.chip_size, self.chip_size)
        if half is not None:
            idx[self.half_dim] = pl.ds(half * self.half_size, self.half_size)
        return ref.at[tuple(idx)]


SMALL_SHARDED = ("meta_tokens", "mix_norm_odd", "pool_b", "pool_scale", "conv_w", "ffn_conv_w")
SMALL_ROWS = 144


def _cuts():
    return {
        "w_in": _Cut((N_CHIPS, D_MODEL, IN_SHARD), 0, 1),
        "w_out": _Cut((D_MODEL, D_MODEL), 0, 1),
        "pool_w": _Cut((len(POOL_WINDOWS), POOL_G, POOL_G), 1, 0),
        "w_up": _Cut((2, D_MODEL, 2 * D_FF), 2, 1),
        "w_down": _Cut((2, D_FF, D_MODEL), 1, 2),
        "small": _Cut((N_CHIPS, SMALL_ROWS, LANE), 0, 1),
        "w_up0": _Cut((D_MODEL, 2 * D_FF), 1, 0), "w_up1": _Cut((D_MODEL, 2 * D_FF), 1, 0),
        "w_down0": _Cut((D_FF, D_MODEL), 0, 1), "w_down1": _Cut((D_FF, D_MODEL), 0, 1),
    }


COMM_ORDER = ("w_in", "w_out", "pool_w", "w_up", "w_down", "small")


def _remote(src, dst, send_sems, recv_sems, k, to):
    return pltpu.make_async_remote_copy(src_ref=src, dst_ref=dst, send_sem=send_sems.at[k],
                                        recv_sem=recv_sems.at[k], device_id=to, device_id_type=MESH)


def _gather_weights(shards, cuts):
    n = len(shards)

    def body(*refs):
        srcs, outs = refs[:n], refs[n:2 * n]
        send_sems, recv_sems = refs[2 * n:]
        x, y, c = _coords()
        me = 2 * x + y
        sibling = (x, y, 1 - c)
        chips = _other_chips(x, y)
        sends = []
        for t, cut in enumerate(cuts):
            push = _remote(srcs[t], cut.region(outs[t], chip=me), send_sems, recv_sems, 7 * t, sibling)
            push.start()
            sends.append(push)
            for kk, chip in enumerate(chips):
                cp = _remote(cut.region(srcs[t], half=c), cut.region(outs[t], chip=me, half=c),
                             send_sems, recv_sems, 7 * t + 1 + kk, (*chip, c))
                cp.start()
                sends.append(cp)
        for t, cut in enumerate(cuts):
            for kk, (px, py) in enumerate(chips):
                landed = cut.region(outs[t], chip=2 * px + py, half=c)
                _remote(landed, landed, send_sems, recv_sems, 7 * t + 1 + kk, sibling).wait_recv()
                fwd = _remote(landed, landed, send_sems, recv_sems, 7 * t + 4 + kk, sibling)
                fwd.start()
                sends.append(fwd)
        for t, cut in enumerate(cuts):
            mine = cut.region(outs[t], chip=me)
            _remote(mine, mine, send_sems, recv_sems, 7 * t, sibling).wait_recv()
            for kk, (px, py) in enumerate(chips):
                other = cut.region(outs[t], chip=2 * px + py, half=1 - c)
                _remote(other, other, send_sems, recv_sems, 7 * t + 4 + kk, sibling).wait_recv()
        for cp in sends:
            cp.wait_send()

    return pl.pallas_call(
        body, name="gather_weights",
        out_shape=tuple(jax.ShapeDtypeStruct(cut.full, s.dtype) for cut, s in zip(cuts, shards)),
        in_specs=[ANY] * n, out_specs=tuple([ANY] * n),
        scratch_shapes=[pltpu.SemaphoreType.DMA((7 * n,)), pltpu.SemaphoreType.DMA((7 * n,))],
    )(*shards)


def _gather_first_ops(srcs, outs, send_sems, recv_sems, cuts):
    x, y, c = _coords()
    me = 2 * x + y
    sibling = (x, y, 1 - c)
    chips = _other_chips(x, y)

    def copies():
        out = []
        for t, cut in enumerate(cuts):
            out.append(_remote(srcs[t], cut.region(outs[t], chip=me), send_sems, recv_sems, 4 * t, sibling))
            for kk, chip in enumerate(chips):
                out.append(_remote(cut.region(srcs[t], half=c), cut.region(outs[t], chip=me, half=c),
                                   send_sems, recv_sems, 4 * t + 1 + kk, (*chip, c)))
        return out

    def start():
        for cp in copies():
            cp.start()

    def wait():
        for t, cut in enumerate(cuts):
            mine = cut.region(outs[t], chip=me)
            _remote(mine, mine, send_sems, recv_sems, 4 * t, sibling).wait_recv()
            for kk, (px, py) in enumerate(chips):
                landed = cut.region(outs[t], chip=2 * px + py, half=c)
                _remote(landed, landed, send_sems, recv_sems, 4 * t + 1 + kk, sibling).wait_recv()
        for cp in copies():
            cp.wait_send()

    return start, wait


def _pair_exchange_ops(srcs, outs, send_sems, recv_sems, cuts):
    x, y, c = _coords()

    def copies():
        return [_remote(cut.region(srcs[t], half=1 - c), outs[t], send_sems, recv_sems, t, (x, y, 1 - c))
                for t, cut in enumerate(cuts)]

    def start():
        for cp in copies():
            cp.start()

    def wait():
        for cp in copies():
            cp.wait()

    return start, wait


def _host_plan(host):
    arrays, cuts = host[0], host[1]
    if len(host) > 2 and host[2] == "chipx":
        return _chip_exchange_shapes(arrays, cuts), 3 * len(arrays), _chip_exchange_ops
    if len(host) > 2 and host[2] == "pairx":
        return (tuple(jax.ShapeDtypeStruct(cut.shape(half=True), a.dtype) for cut, a in zip(cuts, arrays)),
                len(arrays), _pair_exchange_ops)
    return (tuple(jax.ShapeDtypeStruct(cut.full, a.dtype) for cut, a in zip(cuts, arrays)),
            4 * len(arrays), _gather_first_ops)


def _gather_forward(fulls, cuts, name):
    n = len(fulls)

    def body(*refs):
        outs = refs[n:2 * n]
        send_sems, recv_sems = refs[2 * n:]
        x, y, c = _coords()
        sibling = (x, y, 1 - c)
        chips = _other_chips(x, y)
        sends = []
        for t, cut in enumerate(cuts):
            for kk, (px, py) in enumerate(chips):
                landed = cut.region(outs[t], chip=2 * px + py, half=c)
                cp = _remote(landed, landed, send_sems, recv_sems, 3 * t + kk, sibling)
                cp.start()
                sends.append(cp)
        for t, cut in enumerate(cuts):
            for kk, (px, py) in enumerate(chips):
                other = cut.region(outs[t], chip=2 * px + py, half=1 - c)
                _remote(other, other, send_sems, recv_sems, 3 * t + kk, sibling).wait_recv()
        for cp in sends:
            cp.wait_send()

    return pl.pallas_call(
        body, name=name,
        out_shape=tuple(jax.ShapeDtypeStruct(f.shape, f.dtype) for f in fulls),
        in_specs=[ANY] * n, out_specs=tuple([ANY] * n), input_output_aliases={t: t for t in range(n)},
        scratch_shapes=[pltpu.SemaphoreType.DMA((3 * n,)), pltpu.SemaphoreType.DMA((3 * n,))],
    )(*fulls)


def _pair_exchange2(fulls, cuts, name):
    n = len(fulls)

    def body(*refs):
        srcs, outs = refs[:n], refs[n:2 * n]
        send_sems, recv_sems = refs[2 * n:]
        x, y, c = _coords()
        cps = [_remote(cut.region(srcs[t], half=1 - c), outs[t], send_sems, recv_sems, t, (x, y, 1 - c))
               for t, cut in enumerate(cuts)]
        for cp in cps:
            cp.start()
        for cp in cps:
            cp.wait()

    return pl.pallas_call(
        body, name=name,
        out_shape=tuple(jax.ShapeDtypeStruct(cut.shape(half=True), f.dtype) for cut, f in zip(cuts, fulls)),
        in_specs=[ANY] * n, out_specs=tuple([ANY] * n),
        scratch_shapes=[pltpu.SemaphoreType.DMA((n,)), pltpu.SemaphoreType.DMA((n,))],
    )(*fulls)


def _grid_of(shape, blk):
    assert all(s % b == 0 for s, b in zip(shape, blk)), (shape, blk)
    return tuple(s // b for s, b in zip(shape, blk))


def _pair_sum2(full, recv, cut, blk, name):
    hshape = cut.shape(half=True)
    grid = _grid_of(hshape, blk)
    hb = cut.half_size // blk[cut.half_dim]
    hd = cut.half_dim
    pos = jnp.stack([lax.axis_index("c")]).astype(jnp.int32)

    def full_idx(*a):
        ids, p = list(a[:-1]), a[-1]
        ids[hd] = ids[hd] + p[0] * hb
        return tuple(ids)

    def body(p_ref, f_ref, r_ref, o_ref):
        o_ref[...] = (f_ref[...] + r_ref[...]).astype(o_ref.dtype)

    return pl.pallas_call(
        body, name=name, out_shape=jax.ShapeDtypeStruct(hshape, BF16),
        grid_spec=pltpu.PrefetchScalarGridSpec(
            num_scalar_prefetch=1, grid=grid,
            in_specs=[pl.BlockSpec(blk, full_idx), pl.BlockSpec(blk, lambda *a: tuple(a[:-1]))],
            out_specs=pl.BlockSpec(blk, lambda *a: tuple(a[:-1]))),
        compiler_params=_params(("parallel",) * len(grid)))(pos, full, recv)


def _chip_exchange_ops(srcs, outs, send_sems, recv_sems, cuts):
    x, y, c = _coords()
    me = 2 * x + y
    chips = _other_chips(x, y)

    def copies():
        return [_remote(cut.region(srcs[t], chip=2 * px + py), outs[t].at[me], send_sems, recv_sems,
                        3 * t + kk, (px, py, c))
                for t, cut in enumerate(cuts) for kk, (px, py) in enumerate(chips)]

    def start():
        for cp in copies():
            cp.start()

    def wait():
        for t, cut in enumerate(cuts):
            for kk, (px, py) in enumerate(chips):
                slot = outs[t].at[2 * px + py]
                _remote(slot, slot, send_sems, recv_sems, 3 * t + kk, (px, py, c)).wait_recv()
        for cp in copies():
            cp.wait_send()

    return start, wait


def _chip_exchange_shapes(parts, cuts):
    return tuple(jax.ShapeDtypeStruct((N_CHIPS,) + cut.shape(chip=True, half=True), p.dtype)
                 for cut, p in zip(cuts, parts))


def _chip_exchange2(parts, cuts):
    n = len(parts)

    def body(*refs):
        start, wait = _chip_exchange_ops(refs[:n], refs[n:2 * n], refs[2 * n], refs[2 * n + 1], cuts)
        start()
        wait()

    return pl.pallas_call(
        body, name="grad_chip_exchange",
        out_shape=tuple(jax.ShapeDtypeStruct((N_CHIPS,) + cut.shape(chip=True, half=True), p.dtype)
                        for cut, p in zip(cuts, parts)),
        in_specs=[ANY] * n, out_specs=tuple([ANY] * n),
        scratch_shapes=[pltpu.SemaphoreType.DMA((3 * n,)), pltpu.SemaphoreType.DMA((3 * n,))],
    )(*parts)


def _chip_sum2(part, recv, cut, blk, name, stacked=None):
    bshape = cut.shape(chip=True, half=True)
    grid = _grid_of(bshape, blk)
    cb = cut.chip_size // blk[cut.chip_dim]
    hb = cut.half_size // blk[cut.half_dim]
    cd, hd = cut.chip_dim, cut.half_dim
    x, y, c = _coords()
    slots = [2 * px + py for px, py in _other_chips(x, y)]
    pos = jnp.stack([c, 2 * x + y] + slots).astype(jnp.int32)

    def part_idx(*a):
        ids, p = list(a[:-1]), a[-1]
        ids[cd] = ids[cd] + p[1] * cb
        return tuple(ids)

    def recv_idx(kk):
        return lambda *a: (a[-1][2 + kk],) + tuple(a[:-1])

    def out_idx(*a):
        ids, p = list(a[:-1]), a[-1]
        ids[hd] = ids[hd] + p[0] * hb
        return tuple(ids)

    def body(p_ref, own_ref, r0_ref, r1_ref, r2_ref, *rest):
        acc = own_ref[...].astype(F32)
        for r_ref in (r0_ref, r1_ref, r2_ref):
            acc = acc + r_ref[...].astype(F32)
        rest[-1][...] = acc

    in_specs = [pl.BlockSpec(blk, part_idx)] + [pl.BlockSpec((None,) + blk, recv_idx(kk)) for kk in range(3)]
    args = [pos, part, recv, recv, recv]
    aliases = {}
    if stacked is None:
        out_shape = jax.ShapeDtypeStruct(cut.shape(chip=True), F32)
        out_spec = pl.BlockSpec(blk, out_idx)
    else:
        lead, n_lead, into = stacked
        out_shape = jax.ShapeDtypeStruct((n_lead,) + cut.shape(chip=True), F32)
        out_spec = pl.BlockSpec((None,) + blk, lambda *a: (lead,) + out_idx(*a))
        if into is not None:
            in_specs.append(pl.BlockSpec(memory_space=pl.ANY))
            args.append(into)
            aliases = {5: 0}
    return pl.pallas_call(
        body, name=name, out_shape=out_shape,
        grid_spec=pltpu.PrefetchScalarGridSpec(num_scalar_prefetch=1, grid=grid, in_specs=in_specs,
                                               out_specs=out_spec),
        input_output_aliases=aliases, compiler_params=_params(("parallel",) * len(grid)))(*args)


def _pair_swap2(blocks, cuts):
    n = len(blocks)

    def body(*refs):
        outs = refs[n:2 * n]
        send_sems, recv_sems = refs[2 * n:]
        x, y, c = _coords()
        cps = []
        for t, cut in enumerate(cuts):
            mine = cut.region(outs[t], half=c)
            cp = _remote(mine, mine, send_sems, recv_sems, t, (x, y, 1 - c))
            cp.start()
            cps.append(cp)
        for t, cut in enumerate(cuts):
            theirs = cut.region(outs[t], half=1 - c)
            _remote(theirs, theirs, send_sems, recv_sems, t, (x, y, 1 - c)).wait_recv()
        for cp in cps:
            cp.wait_send()

    return pl.pallas_call(
        body, name="grad_pair_swap",
        out_shape=tuple(jax.ShapeDtypeStruct(b.shape, b.dtype) for b in blocks),
        in_specs=[ANY] * n, out_specs=tuple([ANY] * n),
        input_output_aliases={t: t for t in range(n)},
        scratch_shapes=[pltpu.SemaphoreType.DMA((n,)), pltpu.SemaphoreType.DMA((n,))],
    )(*blocks)


PAIR_SUM_BLOCKS = {"w_in": (1, 512, IN_SHARD), "w_out": (512, 512), "pool_w": (1, POOL_G, POOL_G),
                   "w_up0": (64, 2 * D_FF), "w_up1": (64, 2 * D_FF), "w_down0": (704, 512), "w_down1": (704, 512),
                   "small": (N_CHIPS, SMALL_ROWS // 2, LANE)}
CHIP_SUM_BLOCKS = {"w_in": (1, 512, IN_SHARD), "w_out": (256, 512), "pool_w": (2, 64, POOL_G),
                   "w_up0": (128, UP_SHARD), "w_up1": (128, UP_SHARD),
                   "w_down0": (DOWN_SHARD, 512), "w_down1": (DOWN_SHARD, 512),
                   "small": (1, SMALL_ROWS // 2, LANE)}


def _small_rows(t, lead):
    flat = t.reshape(lead + (-1,))
    pad = -flat.shape[-1] % LANE
    return jnp.pad(flat, [(0, 0)] * len(lead) + [(0, pad)]).reshape(lead + (-1, LANE))


def _pack_small_shards(shards):
    rows = jnp.concatenate([_small_rows(shards[n].astype(F32), ()) for n in SMALL_SHARDED], axis=0)
    return jnp.pad(rows, ((0, SMALL_ROWS - rows.shape[0]), (0, 0)))[None]


def _unpack_small(pack, shards, axes):
    out, off = {}, 0
    nchip = pack.shape[0]
    for name in SMALL_SHARDED:
        shp = shards[name].shape
        cnt = int(np.prod(shp))
        rows = -(-cnt // LANE)
        t = pack[:, off:off + rows].reshape(nchip, -1)[:, :cnt].reshape((nchip,) + shp)
        out[name] = jnp.concatenate([t[j] for j in range(nchip)], axis=axes[name])
        off += rows
    return out


def _pack_small_grads(grads, shards, axes):
    parts = []
    for name in SMALL_SHARDED:
        shp, ax = shards[name].shape, axes[name]
        g = grads[name].reshape(shp[:ax] + (N_CHIPS, shp[ax]) + shp[ax + 1:])
        parts.append(_small_rows(jnp.moveaxis(g, ax, 0), (N_CHIPS,)))
    rows = jnp.concatenate(parts, axis=1)
    return jnp.pad(rows, ((0, 0), (0, SMALL_ROWS - rows.shape[1]), (0, 0)))


SMALL_AXES = {"meta_tokens": 1, "mix_norm_odd": 1, "pool_b": 2, "pool_scale": 1, "conv_w": 2, "ffn_conv_w": 2}


WEIGHT_NAMES = ("meta_tokens", "mix_norm_even", "w_in", "b_f", "conv_w", "conv_b", "ln_g", "ln_b", "w_out",
                "mix_norm_odd", "pool_w", "pool_b", "pool_scale", "ffn_norm", "w_up", "ffn_conv_w",
                "ffn_conv_b", "w_down", "final_norm")


def kernel(x, meta_tokens, mix_norm_even, w_in, b_f, conv_w, conv_b, ln_g, ln_b, w_out, mix_norm_odd, pool_w, pool_b, pool_scale, ffn_norm, w_up, ffn_conv_w, ffn_conv_b, w_down, final_norm, loss_target, m_meta_tokens, m_mix_norm_even, m_w_in, m_b_f, m_conv_w, m_conv_b, m_ln_g, m_ln_b, m_w_out, m_mix_norm_odd, m_pool_w, m_pool_b, m_pool_scale, m_ffn_norm, m_w_up, m_ffn_conv_w, m_ffn_conv_b, m_w_down, m_final_norm, v_meta_tokens, v_mix_norm_even, v_w_in, v_b_f, v_conv_w, v_conv_b, v_ln_g, v_ln_b, v_w_out, v_mix_norm_odd, v_pool_w, v_pool_b, v_pool_scale, v_ffn_norm, v_w_up, v_ffn_conv_w, v_ffn_conv_b, v_w_down, v_final_norm):
    given = dict(locals())
    w_loc = {n: given[n] for n in WEIGHT_NAMES}
    m_loc = {n: given["m_" + n] for n in WEIGHT_NAMES}
    v_loc = {n: given["v_" + n] for n in WEIGHT_NAMES}
    cut_of = _cuts()
    cuts = [cut_of[n] for n in COMM_ORDER]
    big = ("w_in", "w_out", "pool_w", "w_up", "w_down")
    small_shards = {n: w_loc[n] for n in SMALL_SHARDED}

    shard_of = {n: w_loc[n].astype(BF16).reshape(cut_of[n].shape(chip=True)) for n in big}
    shard_of["small"] = _pack_small_shards(small_shards)
    g_in, g_small = _gather_weights([shard_of[n] for n in GATHER_FIRST], [cut_of[n] for n in GATHER_FIRST])
    g_out = None
    g_pool = g_up = g_down = None
    full = _unpack_small(g_small, small_shards, SMALL_AXES)
    full.update({n: w_loc[n] for n in REPLICATED})
    w_in_full = g_in.transpose(1, 0, 2).reshape(D_MODEL, IN_COLS)
    qkv, f, ag = (w_in_full[:, :3 * FOX_W], w_in_full[:, 3 * FOX_W:3 * FOX_W + HEADS],
                  w_in_full[:, 3 * FOX_W + HEADS:])
    W = dict(
        mix_norm_even=full["mix_norm_even"].reshape(1, D_MODEL),
        w_in_p=jnp.concatenate([qkv, ag, f, jnp.zeros((D_MODEL, LANE - HEADS), BF16)], axis=1),
        b_f_p=jnp.pad(full["b_f"].reshape(1, HEADS), ((0, 0), (0, LANE - HEADS))),
        conv_w_p=jnp.pad(full["conv_w"].reshape(CONV_K, CONV_CH), ((0, CONV_HALO - CONV_K), (0, 0))),
        conv_b=full["conv_b"].reshape(1, CONV_CH), ln_g=full["ln_g"].reshape(1, CONV_CH),
        ln_b=full["ln_b"].reshape(1, CONV_CH), w_out=g_out,
        mix_norm_odd=full["mix_norm_odd"].reshape(1, D_MODEL), pool_w=g_pool,
        pool_b=full["pool_b"].reshape(1, D_MODEL), pool_scale=full["pool_scale"].reshape(1, D_MODEL),
        ffn_norm=full["ffn_norm"], w_up=g_up,
        ffn_conv_w_p=jnp.pad(full["ffn_conv_w"], ((0, 0), (0, 8 - FFN_K), (0, 0))),
        ffn_conv_b=full["ffn_conv_b"], w_down=g_down, final_norm=full["final_norm"].reshape(1, D_MODEL),
        late_shards=dict(pool_w=shard_of["pool_w"], w_out=shard_of["w_out"],
                         w_up0=shard_of["w_up"][0], w_up1=shard_of["w_up"][1],
                         w_down0=shard_of["w_down"][0], w_down1=shard_of["w_down"][1]))

    seq = x.shape[1]
    n_real = N_META + seq
    LP = -(-n_real // ATT_BLK) * ATT_BLK
    tail = jnp.zeros((LP - n_real, D_MODEL), F32)
    h0 = jnp.concatenate([full["meta_tokens"], x[0], tail], axis=0)
    tgt = jnp.concatenate([jnp.zeros((N_META, D_MODEL), F32), loss_target[0], tail], axis=0)
    loss_loc, dh0, G, parts, others = _local_step2(h0, tgt, W, n_real, cut_of)
    grad_x = dh0[N_META:n_real][None]
    G["meta_tokens"] = dh0[:N_META]

    rep_shapes = {n: w_loc[n].shape for n in REPLICATED}
    G["final_norm"] = G["final_norm"].reshape(D_MODEL)
    rep, loss = _unpack_replicated(_allreduce_small(_pack_replicated(G, loss_loc)), rep_shapes)

    lcuts = [cut_of[n] for n in LATE]
    lfull = [_pack_small_grads(G, small_shards, SMALL_AXES)]
    lrecv = _pair_exchange2(lfull, lcuts, "grad_pair_exchange_late")
    lparts = [_pair_sum2(f, r, cut, PAIR_SUM_BLOCKS[n], "grad_pair_sum_" + n)
              for f, r, cut, n in zip(lfull, lrecv, lcuts, LATE)]
    parts.update(zip(LATE, lparts))
    others.update(zip(LATE, _chip_exchange2(lparts, lcuts)))
    def chip_sum(n, stacked=None):
        return _chip_sum2(parts[n], others[n], cut_of[n], CHIP_SUM_BLOCKS[n], "grad_chip_sum_" + n, stacked=stacked)

    blocks = []
    for n in COMM_ORDER:
        if n in ("w_up", "w_down"):
            blocks.append(chip_sum(n + "1", stacked=(1, 2, chip_sum(n + "0", stacked=(0, 2, None)))))
        else:
            blocks.append(chip_sum(n))
    blocks = _pair_swap2(blocks, cuts)
    gsh = {n: b.reshape(w_loc[n].shape) for n, b in zip(big, blocks[:5])}
    gsh.update(_unpack_small(blocks[5], small_shards, SMALL_AXES))
    sharded = set(big) | set(SMALL_SHARDED)

    grad_w = {n: (gsh[n] if n in sharded else rep[n]) for n in WEIGHT_NAMES}
    delta, new_m, new_v = {}, {}, {}
    for n in WEIGHT_NAMES:
        delta[n], new_m[n], new_v[n] = _adamw(w_loc[n], grad_w[n], m_loc[n], v_loc[n], "adamw_" + n)
    return (loss, grad_x, *[grad_w[n] for n in WEIGHT_NAMES], *[delta[n] for n in WEIGHT_NAMES],
            *[new_m[n] for n in WEIGHT_NAMES], *[new_v[n] for n in WEIGHT_NAMES])
```

```python
import numpy as np
import jax
import jax.numpy as jnp
from jax import lax
from jax.experimental import pallas as pl
from jax.experimental.pallas import tpu as pltpu

F32 = jnp.float32
BF16 = jnp.bfloat16

D_MODEL = 1024
N_META = 16
SEQ = 2048
HEADS = 8
HEAD_DIM = 64
FOX_W = HEADS * HEAD_DIM
CONV_CH = 512
CONV_K = 31
D_FF = 2816
POOL_WINDOWS = (2, 4, 8, 16)
POOL_G = 256
RMS_EPS = 1e-6
LN_EPS = 1e-5
IN_COLS = 3 * FOX_W + HEADS + 2 * CONV_CH
IN_COLS_P = 3 * FOX_W + 2 * CONV_CH + 128
F_COL_BLK = (3 * FOX_W + 2 * CONV_CH) // 128
N_CHIPS = 4
IN_SHARD = IN_COLS // N_CHIPS
UP_SHARD = 2 * D_FF // N_CHIPS
DOWN_SHARD = D_FF // N_CHIPS

ADAM_LR = 0.001
ADAM_B1 = 0.9
ADAM_B2 = 0.999
ADAM_EPS = 1e-08
ADAM_WD = 0.01
ADAM_STEP = 10

LANE = 128
ATT_BLK = 128
VMEM_LIMIT = 56 * 1024 * 1024

NEG = -1e30


def _sigmoid(x):
    return 0.5 * jnp.tanh(0.5 * x) + 0.5


def _sigmoid_tail(x):
    return 1.0 / (1.0 + jnp.exp(-x))


def _params(sem=None):
    return pltpu.CompilerParams(dimension_semantics=sem, vmem_limit_bytes=VMEM_LIMIT)


def _sub_rows(tm):
    best = 8
    for s in range(8, 137, 8):
        if tm % s == 0:
            best = s
    return best


def _mm(a, b, mode, out_dtype, tm, tn, name, add=None, a_lead=None, b_lead=None, out=None, host=None):
    a_shape = a.shape if a_lead is None else a.shape[1:]
    b_shape = b.shape if b_lead is None else b.shape[1:]
    if mode == "nn":
        (M, K), (K2, N) = a_shape, b_shape
        dims = (((1,), (0,)), ((), ()))
        a_blk, a_idx = (tm, K), (lambda i, j: (i, 0))
        b_blk, b_idx = (K, tn), (lambda i, j: (0, j))
    elif mode == "nt":
        (M, K), (N, K2) = a_shape, b_shape
        dims = (((1,), (1,)), ((), ()))
        a_blk, a_idx = (tm, K), (lambda i, j: (i, 0))
        b_blk, b_idx = (tn, K), (lambda i, j: (j, 0))
    else:
        (K, M), (K2, N) = a_shape, b_shape
        dims = (((0,), (0,)), ((), ()))
        a_blk, a_idx = (K, tm), (lambda i, j: (0, i))
        b_blk, b_idx = (K, tn), (lambda i, j: (0, j))
    assert K == K2 and M % tm == 0 and N % tn == 0, (name, a.shape, b.shape, tm, tn)
    gm, gn = M // tm, N // tn
    a_bytes = M * K * a.dtype.itemsize
    b_bytes = N * K * b.dtype.itemsize
    m_outer = a_bytes + b_bytes * gm <= b_bytes + a_bytes * gn
    if m_outer:
        grid = (gm, gn)
        wrap = lambda f: f
    else:
        grid = (gn, gm)
        wrap = lambda f: (lambda j, i: f(i, j))

    def lead(blk, idx, at):
        if at is None:
            return pl.BlockSpec(blk, wrap(idx))
        return pl.BlockSpec((None,) + blk, wrap(lambda i, j: (at,) + idx(i, j)))

    o_idx = lambda i, j: (i, j)
    in_specs = [lead(a_blk, a_idx, a_lead), lead(b_blk, b_idx, b_lead)]
    args = [a, b]
    if add is not None:
        in_specs.append(pl.BlockSpec((tm, tn), wrap(o_idx)))
        args.append(add)
    aliases = {}
    if out is None:
        out_shape = jax.ShapeDtypeStruct((M, N), out_dtype)
        out_spec = pl.BlockSpec((tm, tn), wrap(o_idx))
    else:
        o_lead, n_lead, into = out
        out_shape = jax.ShapeDtypeStruct((n_lead, M, N), out_dtype)
        out_spec = lead((tm, tn), o_idx, o_lead)
        if into is not None:
            aliases = {len(args): 0}
            in_specs.append(pl.BlockSpec(memory_space=pl.ANY))
            args.append(into)
    has_add = add is not None
    n_host = 0 if host is None else len(host[0])
    n_in = len(args)
    scratch = []
    if n_host:
        host_shapes, host_sems, host_ops = _host_plan(host)
        in_specs = in_specs + [pl.BlockSpec(memory_space=pl.ANY)] * n_host
        args = args + list(host[0])
        out_shape = (out_shape,) + host_shapes
        out_spec = (out_spec,) + (pl.BlockSpec(memory_space=pl.ANY),) * n_host
        scratch = [pltpu.SemaphoreType.DMA((host_sems,)), pltpu.SemaphoreType.DMA((host_sems,))]

    def body(*refs):
        a_ref, b_ref = refs[0], refs[1]
        o_ref = refs[n_in + n_host]
        if n_host:
            start, wait = host_ops(refs[n_in:n_in + n_host], refs[n_in + n_host + 1:n_in + 2 * n_host + 1],
                                   refs[n_in + 2 * n_host + 1], refs[n_in + 2 * n_host + 2], host[1])
            pl.when(jnp.logical_and(pl.program_id(0) == 0, pl.program_id(1) == 0))(start)
        x = a_ref[...].astype(BF16)
        y = b_ref[...].astype(BF16)
        acc = lax.dot_general(x, y, dims, preferred_element_type=F32)
        if has_add:
            acc = acc + refs[2][...]
        o_ref[...] = acc.astype(o_ref.dtype)
        if n_host:
            pl.when(jnp.logical_and(pl.program_id(0) == grid[0] - 1, pl.program_id(1) == grid[1] - 1))(wait)

    sem = ("arbitrary", "arbitrary") if n_host else ("parallel", "parallel")
    return pl.pallas_call(
        body, name=name, out_shape=out_shape, grid=grid, in_specs=in_specs, out_specs=out_spec,
        scratch_shapes=scratch, input_output_aliases=aliases, compiler_params=_params(sem))(*args)


def _mm_ffn_dn(dup, w_up, tm, tn, name):
    _, LP, F = dup.shape
    Dm = w_up.shape[0]
    nt = (((1,), (1,)), ((), ()))

    def body(a_ref, b_ref, o_ref):
        acc = lax.dot_general(a_ref[0], b_ref[:, 0:F], nt, preferred_element_type=F32)
        acc = acc + lax.dot_general(a_ref[1], b_ref[:, F:2 * F], nt, preferred_element_type=F32)
        o_ref[...] = acc

    return pl.pallas_call(
        body, name=name, out_shape=jax.ShapeDtypeStruct((LP, Dm), F32), grid=(LP // tm, Dm // tn),
        in_specs=[pl.BlockSpec((2, tm, F), lambda i, j: (0, i, 0)),
                  pl.BlockSpec((tn, 2 * F), lambda i, j: (j, 0))],
        out_specs=pl.BlockSpec((tm, tn), lambda i, j: (i, j)),
        compiler_params=_params(("parallel", "parallel")))(dup, w_up)


def _mm_ffn_dwup(n, dup, tk, tn, name):
    LP, Dm = n.shape
    F = dup.shape[2]
    nct = F // tn
    tdims = (((0,), (0,)), ((), ()))

    def body(a_ref, b_ref, o_ref):
        o_ref[...] = lax.dot_general(a_ref[...], b_ref[...], tdims, preferred_element_type=F32)

    return pl.pallas_call(
        body, name=name, out_shape=jax.ShapeDtypeStruct((Dm, 2 * F), F32), grid=(Dm // tk, 2 * nct),
        in_specs=[pl.BlockSpec((LP, tk), lambda i, j: (0, i)),
                  pl.BlockSpec((None, LP, tn), lambda i, j: (j // nct, 0, j % nct))],
        out_specs=pl.BlockSpec((tk, tn), lambda i, j: (i, j)),
        compiler_params=_params(("parallel", "parallel")))(n, dup)


def _rms_fwd(h, g, out_dtype, tm, name):
    LP, Dm = h.shape

    def body(h_ref, g_ref, o_ref):
        x = h_ref[...]
        r = lax.rsqrt(jnp.mean(x * x, axis=1, keepdims=True) + RMS_EPS)
        o_ref[...] = (x * r * g_ref[...]).astype(o_ref.dtype)

    return pl.pallas_call(
        body, name=name, out_shape=jax.ShapeDtypeStruct((LP, Dm), out_dtype), grid=(LP // tm,),
        in_specs=[pl.BlockSpec((tm, Dm), lambda i: (i, 0)), pl.BlockSpec((1, Dm), lambda i: (0, 0))],
        out_specs=pl.BlockSpec((tm, Dm), lambda i: (i, 0)),
        compiler_params=_params(("parallel",)))(h, g)


def _rms_bwd(h, g, dn, dres, tm, name, host=None):
    LP, Dm = h.shape
    n_host = 0 if host is None else len(host[0])
    if n_host:
        host_shapes, host_sems, host_ops = _host_plan(host)
    nblk = LP // tm

    def body(*refs):
        h_ref, g_ref, dn_ref, dr_ref = refs[:4]
        dh_ref, dg_ref = refs[4 + n_host:6 + n_host]
        if n_host:
            start, wait = host_ops(refs[4:4 + n_host], refs[6 + n_host:6 + 2 * n_host],
                                   refs[6 + 2 * n_host], refs[7 + 2 * n_host], host[1])
            pl.when(pl.program_id(0) == 0)(start)
        i = pl.program_id(0)
        x = h_ref[...]
        r = lax.rsqrt(jnp.mean(x * x, axis=1, keepdims=True) + RMS_EPS)
        xhat = x * r
        dy = dn_ref[...]
        dxh = dy * g_ref[...]
        dh = r * (dxh - xhat * jnp.mean(dxh * xhat, axis=1, keepdims=True))
        dh_ref[...] = dr_ref[...] + dh

        @pl.when(i == 0)
        def _():
            dg_ref[...] = jnp.zeros_like(dg_ref)

        dg_ref[...] += jnp.sum(dy * xhat, axis=0, keepdims=True)
        if n_host:
            pl.when(pl.program_id(0) == nblk - 1)(wait)

    row = pl.BlockSpec((tm, Dm), lambda i: (i, 0))
    vec = pl.BlockSpec((1, Dm), lambda i: (0, 0))
    out_shape = (jax.ShapeDtypeStruct((LP, Dm), F32), jax.ShapeDtypeStruct((1, Dm), F32))
    out_specs = (row, vec)
    in_specs = [row, vec, row, row]
    args = [h, g, dn, dres]
    scratch = []
    if n_host:
        anyspec = pl.BlockSpec(memory_space=pl.ANY)
        in_specs += [anyspec] * n_host
        args += list(host[0])
        out_shape += host_shapes
        out_specs += (anyspec,) * n_host
        scratch = [pltpu.SemaphoreType.DMA((host_sems,)), pltpu.SemaphoreType.DMA((host_sems,))]
    return pl.pallas_call(
        body, name=name, out_shape=out_shape, grid=(nblk,), in_specs=in_specs, out_specs=out_specs,
        scratch_shapes=scratch, compiler_params=_params(("arbitrary",)))(*args)


def _loss_head(h, g, tgt, n_real, tm, name):
    LP, Dm = h.shape

    def body(h_ref, g_ref, t_ref, loss_ref, dh_ref, dg_ref):
        i = pl.program_id(0)
        x = h_ref[...]
        gg = g_ref[...]
        r = lax.rsqrt(jnp.mean(x * x, axis=1, keepdims=True) + RMS_EPS)
        xhat = x * r
        rows = i * tm + lax.broadcasted_iota(jnp.int32, (tm, 1), 0)
        real = jnp.logical_and(rows >= N_META, rows < n_real)
        diff = jnp.where(real, xhat * gg - t_ref[...], 0.0)
        dy = diff * (1.0 / Dm)
        dxh = dy * gg
        dh_ref[...] = r * (dxh - xhat * jnp.mean(dxh * xhat, axis=1, keepdims=True))

        @pl.when(i == 0)
        def _():
            dg_ref[...] = jnp.zeros_like(dg_ref)
            loss_ref[...] = jnp.zeros_like(loss_ref)

        dg_ref[...] += jnp.sum(dy * xhat, axis=0, keepdims=True)
        part = jnp.sum(jnp.sum(diff * diff, axis=1, keepdims=True), axis=0, keepdims=True)
        loss_ref[...] += jnp.broadcast_to(part * (0.5 / Dm), loss_ref.shape)

    row = pl.BlockSpec((tm, Dm), lambda i: (i, 0))
    vec = pl.BlockSpec((1, Dm), lambda i: (0, 0))
    return pl.pallas_call(
        body, name=name,
        out_shape=(jax.ShapeDtypeStruct((1, LANE), F32), jax.ShapeDtypeStruct((LP, Dm), F32),
                   jax.ShapeDtypeStruct((1, Dm), F32)),
        grid=(LP // tm,), in_specs=[row, vec, row],
        out_specs=(pl.BlockSpec((1, LANE), lambda i: (0, 0)), row, vec),
        compiler_params=_params(("arbitrary",)))(h, g, tgt)


def _fgate_fwd(proj, bf_p, name):
    LP = proj.shape[0]
    nb = LP // LANE

    def body(f_ref, b_ref, c_ref, lf_ref):
        x = f_ref[...] + b_ref[...]
        lf_ref[...] = jnp.minimum(x, 0.0) - jnp.log1p(jnp.exp(-jnp.abs(x)))
        ri = lax.broadcasted_iota(jnp.int32, (LANE, LANE), 0)
        ci = lax.broadcasted_iota(jnp.int32, (LANE, LANE), 1)
        tri = jnp.where(ri >= ci, 1.0, 0.0).astype(F32)

        def blk(i, carry):
            rows = pl.ds(pl.multiple_of(i * LANE, LANE), LANE)
            cb = jnp.dot(tri, lf_ref[rows, :], precision=lax.Precision.HIGHEST,
                         preferred_element_type=F32) + carry
            c_ref[rows, :] = cb
            return cb[LANE - 1:LANE, :]

        lax.fori_loop(0, nb, blk, jnp.zeros((1, LANE), F32))

    return pl.pallas_call(
        body, name=name, out_shape=jax.ShapeDtypeStruct((LP, LANE), F32), grid=(1,),
        in_specs=[pl.BlockSpec((LP, LANE), lambda i: (0, F_COL_BLK)),
                  pl.BlockSpec((1, LANE), lambda i: (0, 0))],
        out_specs=pl.BlockSpec((LP, LANE), lambda i: (0, 0)),
        scratch_shapes=[pltpu.VMEM((LP, LANE), F32)],
        compiler_params=_params(("arbitrary",)))(proj, bf_p)


def _fgate_bwd(proj, bf_p, dc, name):
    LP = proj.shape[0]
    nb = LP // LANE

    def body(f_ref, b_ref, dc_ref, dl_ref, db_ref):
        ri = lax.broadcasted_iota(jnp.int32, (LANE, LANE), 0)
        ci = lax.broadcasted_iota(jnp.int32, (LANE, LANE), 1)
        triu = jnp.where(ri <= ci, 1.0, 0.0).astype(F32)
        bb = b_ref[...]

        tail = jnp.zeros((1, LANE), F32)
        dbs = jnp.zeros((1, LANE), F32)
        for i in range(nb - 1, -1, -1):
            rows = slice(i * LANE, (i + 1) * LANE)
            gb = jnp.dot(triu, dc_ref[rows, :], precision=lax.Precision.HIGHEST,
                         preferred_element_type=F32) + tail
            x = f_ref[rows, :] + bb
            dl = gb * _sigmoid_tail(-x)
            dl_ref[rows, :] = dl.astype(dl_ref.dtype)
            tail = gb[0:1, :]
            dbs = dbs + jnp.sum(dl, axis=0, keepdims=True)
        db_ref[...] = dbs

    return pl.pallas_call(
        body, name=name,
        out_shape=(jax.ShapeDtypeStruct((LP, LANE), BF16), jax.ShapeDtypeStruct((1, LANE), F32)),
        grid=(1,),
        in_specs=[pl.BlockSpec((LP, LANE), lambda i: (0, F_COL_BLK)),
                  pl.BlockSpec((1, LANE), lambda i: (0, 0)),
                  pl.BlockSpec((LP, LANE), lambda i: (0, 0))],
        out_specs=(pl.BlockSpec((LP, LANE), lambda i: (0, 0)), pl.BlockSpec((1, LANE), lambda i: (0, 0))),
        compiler_params=_params(("arbitrary",)))(proj, bf_p, dc)


AUG = 128
ONES_IN_K = HEAD_DIM
ONES_IN_Q = HEAD_DIM + 3
ATT_HEADS_PER_STEP = 8
ATT_HEADS_PER_STEP_BWD = 8


def _attn_prep(proj, c, name):
    LP = proj.shape[0]
    nb = LP // ATT_BLK
    tail_rows = AUG - HEAD_DIM

    def body(q_ref, k_ref, v_ref, c_ref, qT_ref, kT_ref, ka_ref, vT_ref):
        qt = (q_ref[...] * (HEAD_DIM ** -0.5)).T
        kt = k_ref[...].T
        vt = v_ref[...].T
        ct = c_ref[...].T
        hi = ct.astype(BF16).astype(F32)
        r1 = ct - hi
        mid = r1.astype(BF16).astype(F32)
        lo = (r1 - mid).astype(BF16).astype(F32)
        row = lax.broadcasted_iota(jnp.int32, (tail_rows, ATT_BLK), 0)
        ones = jnp.where(row < 3, 1.0, 0.0)
        for h in range(HEADS):
            cparts = jnp.where(row == 0, hi[h:h + 1], jnp.where(row == 1, mid[h:h + 1],
                               jnp.where(row == 2, lo[h:h + 1], 0.0)))
            hs = slice(h * HEAD_DIM, (h + 1) * HEAD_DIM)
            q_tail = cparts + pltpu.roll(ones, 3, 0)
            k_tail = ones - pltpu.roll(cparts, 3, 0)
            qT_ref[h] = jnp.concatenate([qt[hs], q_tail], axis=0).astype(BF16)
            kfull = jnp.concatenate([kt[hs], k_tail], axis=0)
            kT_ref[h] = kfull.astype(BF16)
            ka_ref[h] = kfull.T.astype(BF16)
            vT_ref[h] = vt[hs].astype(BF16)

    col = lambda j: pl.BlockSpec((ATT_BLK, FOX_W), lambda i: (i, j))
    blk = lambda r: pl.BlockSpec((HEADS, None, r, ATT_BLK), lambda i: (0, i, 0, 0))
    return pl.pallas_call(
        body, name=name,
        out_shape=(jax.ShapeDtypeStruct((HEADS, nb, AUG, ATT_BLK), BF16),
                   jax.ShapeDtypeStruct((HEADS, nb, AUG, ATT_BLK), BF16),
                   jax.ShapeDtypeStruct((HEADS, LP, AUG), BF16),
                   jax.ShapeDtypeStruct((HEADS, nb, HEAD_DIM, ATT_BLK), BF16)),
        grid=(nb,), in_specs=[col(0), col(1), col(2), pl.BlockSpec((ATT_BLK, LANE), lambda i: (i, 0))],
        out_specs=(blk(AUG), blk(AUG), pl.BlockSpec((HEADS, ATT_BLK, AUG), lambda i: (0, i, 0)), blk(HEAD_DIM)),
        compiler_params=_params(("parallel",)))(proj, proj, proj, c)


def _attn_rows(xT, scale, out_dtype, name):
    Hh, nb, R, _ = xT.shape

    def body(x_ref, o_ref):
        stack = jnp.concatenate([x_ref[h, 0:HEAD_DIM, :] for h in range(Hh)], axis=0)
        o_ref[...] = (stack * scale).T.astype(o_ref.dtype)

    return pl.pallas_call(
        body, name=name, out_shape=jax.ShapeDtypeStruct((nb * ATT_BLK, Hh * HEAD_DIM), out_dtype), grid=(nb,),
        in_specs=[pl.BlockSpec((Hh, None, R, ATT_BLK), lambda i: (0, i, 0, 0))],
        out_specs=pl.BlockSpec((ATT_BLK, Hh * HEAD_DIM), lambda i: (i, 0)),
        compiler_params=_params(("parallel",)))(xT)


def _attn_cols(x, name):
    LP = x.shape[0]
    nb = LP // ATT_BLK

    def body(x_ref, o_ref):
        xt = x_ref[...].T
        for h in range(HEADS):
            o_ref[h] = xt[h * HEAD_DIM:(h + 1) * HEAD_DIM].astype(o_ref.dtype)

    return pl.pallas_call(
        body, name=name, out_shape=jax.ShapeDtypeStruct((HEADS, nb, HEAD_DIM, ATT_BLK), BF16), grid=(nb,),
        in_specs=[pl.BlockSpec((ATT_BLK, FOX_W), lambda i: (i, 0))],
        out_specs=pl.BlockSpec((HEADS, None, HEAD_DIM, ATT_BLK), lambda i: (0, i, 0, 0)),
        compiler_params=_params(("parallel",)))(x)


def _attn_grads_rows(dqT, dkT, dvT, name):
    Hh, nb, _, _ = dqT.shape
    W3 = Hh * HEAD_DIM

    def body(q_ref, k_ref, v_ref, o_ref, dc_ref):
        for col, (x_ref, scale) in enumerate(((q_ref, HEAD_DIM ** -0.5), (k_ref, 1.0), (v_ref, 1.0))):
            stack = jnp.concatenate([x_ref[h, 0:HEAD_DIM, :] for h in range(Hh)], axis=0)
            o_ref[:, col * W3:(col + 1) * W3] = (stack * scale).T.astype(o_ref.dtype)
        row = lax.broadcasted_iota(jnp.int32, (LANE, ATT_BLK), 0)
        acc = jnp.zeros((LANE, ATT_BLK), F32)
        for h in range(Hh):
            d = q_ref[h, ONES_IN_K:ONES_IN_K + 1, :] - k_ref[h, ONES_IN_Q:ONES_IN_Q + 1, :]
            acc = jnp.where(row == h, d, acc)
        dc_ref[...] = acc.T

    spec = lambda r: pl.BlockSpec((Hh, None, r, ATT_BLK), lambda i: (0, i, 0, 0))
    return pl.pallas_call(
        body, name=name,
        out_shape=(jax.ShapeDtypeStruct((nb * ATT_BLK, 3 * W3), BF16), jax.ShapeDtypeStruct((nb * ATT_BLK, LANE), F32)),
        grid=(nb,), in_specs=[spec(AUG), spec(AUG), spec(HEAD_DIM)],
        out_specs=(pl.BlockSpec((ATT_BLK, 3 * W3), lambda i: (i, 0)), pl.BlockSpec((ATT_BLK, LANE), lambda i: (i, 0))),
        compiler_params=_params(("parallel",)))(dqT, dkT, dvT)


def _attn_fwd2(qT, k_aug, vT, name, comm=None):
    Hh, nb, _, _ = qT.shape
    LP = nb * ATT_BLK
    Dh = vT.shape[2]
    HB = ATT_HEADS_PER_STEP
    n_comm = 0 if comm is None else len(comm[0])

    def body(*refs):
        q_ref, k_ref, v_ref = refs[:3]
        o_ref, lse_ref = refs[3 + n_comm:5 + n_comm]
        if n_comm:
            start, wait = _gather_first_ops(refs[3:3 + n_comm], refs[5 + n_comm:5 + 2 * n_comm],
                                            refs[5 + 2 * n_comm], refs[6 + 2 * n_comm], comm[1])
            pl.when(pl.program_id(0) == 0)(start)
        keys = lax.broadcasted_iota(jnp.int32, (ATT_BLK, ATT_BLK), 0)
        qrys = lax.broadcasted_iota(jnp.int32, (ATT_BLK, ATT_BLK), 1)
        causal = keys <= qrys

        def q_block(i, _):
            def tile(j, carry, masked):
                ks = pl.ds(pl.multiple_of(j * ATT_BLK, ATT_BLK), ATT_BLK)
                s_all = [jnp.dot(k_ref[hh, ks, :], q_ref[hh, i], preferred_element_type=F32) for hh in range(HB)]
                stats, p_all = [], []
                for hh in range(HB):
                    m, l, _ = carry[hh]
                    s = jnp.where(causal, s_all[hh], NEG) if masked else s_all[hh]
                    m_new = jnp.maximum(m, jnp.max(s, axis=0, keepdims=True))
                    p = jnp.exp(s - m_new)
                    alpha = jnp.exp(m - m_new)
                    stats.append((m_new, alpha * l + jnp.sum(p, axis=0, keepdims=True), alpha))
                    p_all.append(p.astype(BF16))
                out = []
                for hh in range(HB):
                    m_new, l, alpha = stats[hh]
                    acc = alpha * carry[hh][2] + jnp.dot(v_ref[hh, j], p_all[hh], preferred_element_type=F32)
                    out.append((m_new, l, acc))
                return tuple(out)

            init = tuple((jnp.full((1, ATT_BLK), NEG, F32), jnp.zeros((1, ATT_BLK), F32),
                          jnp.zeros((Dh, ATT_BLK), F32)) for _ in range(HB))
            carry = lax.fori_loop(0, i, lambda j, cr: tile(j, cr, False), init)
            carry = tile(i, carry, True)
            for hh in range(HB):
                m, l, acc = carry[hh]
                o_ref[hh, i] = acc / l
                lse_ref[hh, i] = m + jnp.log(l)
            return 0

        lax.fori_loop(0, nb, q_block, 0)
        if n_comm:
            pl.when(pl.program_id(0) == Hh // HB - 1)(wait)

    blk = lambda r: pl.BlockSpec((HB, nb, r, ATT_BLK), lambda h: (h, 0, 0, 0))
    out_shape = (jax.ShapeDtypeStruct((Hh, nb, Dh, ATT_BLK), F32), jax.ShapeDtypeStruct((Hh, nb, 1, ATT_BLK), F32))
    scratch = []
    args = [qT, k_aug, vT]
    if n_comm:
        out_shape += tuple(jax.ShapeDtypeStruct(cut.full, s.dtype) for cut, s in zip(comm[1], comm[0]))
        scratch = [pltpu.SemaphoreType.DMA((4 * n_comm,)), pltpu.SemaphoreType.DMA((4 * n_comm,))]
        args += list(comm[0])
    return pl.pallas_call(
        body, name=name, out_shape=out_shape, grid=(Hh // HB,),
        in_specs=[blk(AUG), pl.BlockSpec((HB, LP, AUG), lambda h: (h, 0, 0)), blk(Dh)] + [ANY] * n_comm,
        out_specs=(blk(Dh), blk(1)) + (ANY,) * n_comm, scratch_shapes=scratch,
        compiler_params=_params(("arbitrary",)))(*args)


def _attn_bwd2(qT, kT, k_aug, v, oT, doT, lse, name, comm=None):
    Hh, nb, _, _ = qT.shape
    LP = nb * ATT_BLK
    Dh = v.shape[2]
    nt = (((1,), (1,)), ((), ()))
    tn = (((0,), (0,)), ((), ()))

    HB = ATT_HEADS_PER_STEP_BWD
    n_comm = 0 if comm is None else len(comm[0])

    def body(*refs):
        q_ref, kt_ref, k_ref, v_ref, o_ref, do_ref, lse_ref = refs[:7]
        parts = refs[7:7 + n_comm]
        dq_ref, dk_ref, dv_ref = refs[7 + n_comm:10 + n_comm]
        others = refs[10 + n_comm:10 + 2 * n_comm]
        delta_ref = refs[10 + 2 * n_comm]
        if n_comm:
            start, wait = _chip_exchange_ops(parts, others, refs[11 + 2 * n_comm], refs[12 + 2 * n_comm], comm[1])
            pl.when(pl.program_id(0) == 0)(start)
        keys = lax.broadcasted_iota(jnp.int32, (ATT_BLK, ATT_BLK), 0)
        qrys = lax.broadcasted_iota(jnp.int32, (ATT_BLK, ATT_BLK), 1)
        causal = keys <= qrys

        def prep(i, _):
            for hh in range(HB):
                delta_ref[hh, i] = jnp.sum(do_ref[hh, i].astype(F32) * o_ref[hh, i], axis=0, keepdims=True)
                dq_ref[hh, i] = jnp.zeros((AUG, ATT_BLK), F32)
            return 0

        lax.fori_loop(0, nb, prep, 0)

        def kv_block(j, _):
            ks = pl.ds(pl.multiple_of(j * ATT_BLK, ATT_BLK), ATT_BLK)

            def tile(i, carry, masked):
                s_all = [jnp.dot(k_ref[hh, ks, :], q_ref[hh, i], preferred_element_type=F32) for hh in range(HB)]
                dp_all = [lax.dot_general(v_ref[hh, j], do_ref[hh, i], tn, preferred_element_type=F32)
                          for hh in range(HB)]
                p_all, ds_all = [], []
                for hh in range(HB):
                    s = jnp.where(causal, s_all[hh], NEG) if masked else s_all[hh]
                    p = jnp.exp(s - lse_ref[hh, i])
                    ds_all.append((p * (dp_all[hh] - delta_ref[hh, i])).astype(BF16))
                    p_all.append(p.astype(BF16))
                out = []
                for hh in range(HB):
                    dk, dv = carry[hh]
                    dv = dv + lax.dot_general(do_ref[hh, i], p_all[hh], nt, preferred_element_type=F32)
                    dk = dk + lax.dot_general(q_ref[hh, i], ds_all[hh], nt, preferred_element_type=F32)
                    out.append((dk, dv))
                dq_new = [jnp.dot(kt_ref[hh, j], ds_all[hh], preferred_element_type=F32) for hh in range(HB)]
                for hh in range(HB):
                    dq_ref[hh, i] += dq_new[hh]
                return tuple(out)

            init = tuple((jnp.zeros((AUG, ATT_BLK), F32), jnp.zeros((Dh, ATT_BLK), F32)) for _ in range(HB))
            carry = tile(j, init, True)
            carry = lax.fori_loop(j + 1, nb, lambda i, cr: tile(i, cr, False), carry)
            for hh in range(HB):
                dk_ref[hh, j] = carry[hh][0]
                dv_ref[hh, j] = carry[hh][1]
            return 0

        lax.fori_loop(0, nb, kv_block, 0)
        if n_comm:
            pl.when(pl.program_id(0) == Hh // HB - 1)(wait)

    blk = lambda r: pl.BlockSpec((HB, nb, r, ATT_BLK), lambda h: (h, 0, 0, 0))
    row = lambda cols: pl.BlockSpec((HB, LP, cols), lambda h: (h, 0, 0))
    out_shape = (jax.ShapeDtypeStruct((Hh, nb, AUG, ATT_BLK), F32), jax.ShapeDtypeStruct((Hh, nb, AUG, ATT_BLK), F32),
                 jax.ShapeDtypeStruct((Hh, nb, Dh, ATT_BLK), F32))
    scratch = [pltpu.VMEM((HB, nb, 1, ATT_BLK), F32)]
    args = [qT, kT, k_aug, v, oT, doT, lse]
    if n_comm:
        out_shape += _chip_exchange_shapes(*comm)
        scratch += [pltpu.SemaphoreType.DMA((3 * n_comm,)), pltpu.SemaphoreType.DMA((3 * n_comm,))]
        args += list(comm[0])
    return pl.pallas_call(
        body, name=name, out_shape=out_shape, grid=(Hh // HB,),
        in_specs=[blk(AUG), blk(AUG), row(AUG), blk(Dh), blk(Dh), blk(Dh), blk(1)] + [ANY] * n_comm,
        out_specs=(blk(AUG), blk(AUG), blk(Dh)) + (ANY,) * n_comm,
        scratch_shapes=scratch,
        compiler_params=_params(("arbitrary",)))(*args)


CONV_HALO = 32
A_BLK = 3 * FOX_W // CONV_CH
G_BLK = A_BLK + 1


def _conf_fwd(proj, cw, cb, lg, lb, tm, name):
    LP = proj.shape[0]
    C = CONV_CH
    sub = _sub_rows(tm)
    hpb = tm // CONV_HALO

    def body(a_ref, g_ref, ah_ref, gh_ref, w_ref, cb_ref, lg_ref, lb_ref, u1_ref, u_ref, buf):
        r = pl.program_id(0)
        buf[CONV_HALO:CONV_HALO + tm, :] = a_ref[...] * _sigmoid(g_ref[...])
        buf[0:CONV_HALO, :] = jnp.where(r > 0, ah_ref[...] * _sigmoid(gh_ref[...]), 0.0)
        for s in range(tm // sub):
            for ct in range(C // LANE):
                ln = slice(ct * LANE, (ct + 1) * LANE)
                acc = jnp.broadcast_to(cb_ref[:, ln], (sub, LANE))
                for kk in range(CONV_K):
                    off = CONV_HALO + s * sub - (CONV_K - 1) + kk
                    acc = acc + w_ref[kk:kk + 1, ln] * buf[off:off + sub, ln]
                u1_ref[s * sub:(s + 1) * sub, ln] = acc
        u1 = u1_ref[...]
        mu = jnp.mean(u1, axis=1, keepdims=True)
        xc = u1 - mu
        var = jnp.mean(xc * xc, axis=1, keepdims=True)
        y = xc * lax.rsqrt(var + LN_EPS) * lg_ref[...] + lb_ref[...]
        u_ref[...] = (y * _sigmoid(y)).astype(u_ref.dtype)

    cur = lambda blk: pl.BlockSpec((tm, C), lambda r: (r, blk))
    halo = lambda blk: pl.BlockSpec((CONV_HALO, C), lambda r: (jnp.maximum(r * hpb - 1, 0), blk))
    vec = pl.BlockSpec((1, C), lambda r: (0, 0))
    out = pl.BlockSpec((tm, C), lambda r: (r, 0))
    return pl.pallas_call(
        body, name=name,
        out_shape=(jax.ShapeDtypeStruct((LP, C), F32), jax.ShapeDtypeStruct((LP, C), BF16)),
        grid=(LP // tm,),
        in_specs=[cur(A_BLK), cur(G_BLK), halo(A_BLK), halo(G_BLK),
                  pl.BlockSpec((CONV_HALO, C), lambda r: (0, 0)), vec, vec, vec],
        out_specs=(out, out),
        scratch_shapes=[pltpu.VMEM((CONV_HALO + tm, C), F32)],
        compiler_params=_params(("parallel",)))(proj, proj, proj, proj, cw, cb, lg, lb)


def _conf_bwd(proj, u1, dcat, cw, lg, lb, tm, name, host=None):
    LP = proj.shape[0]
    C = CONV_CH
    sub = _sub_rows(tm)
    hpb = tm // CONV_HALO
    nblk = LP // tm
    last_halo = LP // CONV_HALO - 1

    n_host = 0 if host is None else len(host[0])
    if n_host:
        host_shapes, host_sems, host_ops = _host_plan(host)

    def body(*refs):
        a_ref, g_ref, ah_ref, gh_ref, u1_ref, u1n_ref, du_ref, dun_ref, w_ref, lg_ref, lb_ref = refs[:11]
        dadg_ref, dw_ref, dcb_ref, dlg_ref, dlb_ref = refs[11 + n_host:16 + n_host]
        ubuf, dbuf, du0 = refs[16 + 2 * n_host:19 + 2 * n_host]
        if n_host:
            start, wait = host_ops(refs[11:11 + n_host], refs[16 + n_host:16 + 2 * n_host],
                                   refs[19 + 2 * n_host], refs[20 + 2 * n_host], host[1])
            pl.when(pl.program_id(0) == 0)(start)
        r = pl.program_id(0)
        lgv = lg_ref[...]
        lbv = lb_ref[...]

        def ln_silu_bwd(u1v, duv):
            mu = jnp.mean(u1v, axis=1, keepdims=True)
            xc = u1v - mu
            rstd = lax.rsqrt(jnp.mean(xc * xc, axis=1, keepdims=True) + LN_EPS)
            xhat = xc * rstd
            y = xhat * lgv + lbv
            sg = _sigmoid(y)
            dy = duv * (sg * (1.0 + y * (1.0 - sg)))
            dxh = dy * lgv
            du1 = rstd * (dxh - jnp.mean(dxh, axis=1, keepdims=True)
                          - xhat * jnp.mean(dxh * xhat, axis=1, keepdims=True))
            return du1, dy, xhat

        @pl.when(r == 0)
        def _():
            dw_ref[...] = jnp.zeros_like(dw_ref)
            dcb_ref[...] = jnp.zeros_like(dcb_ref)
            dlg_ref[...] = jnp.zeros_like(dlg_ref)
            dlb_ref[...] = jnp.zeros_like(dlb_ref)

        du1, dy, xhat = ln_silu_bwd(u1_ref[...], du_ref[...])
        dlg_ref[...] += jnp.sum(dy * xhat, axis=0, keepdims=True)
        dlb_ref[...] += jnp.sum(dy, axis=0, keepdims=True)
        dcb_ref[...] += jnp.sum(du1, axis=0, keepdims=True)
        dbuf[0:tm, :] = du1
        du1n, _, _ = ln_silu_bwd(u1n_ref[...], dun_ref[...])
        dbuf[tm:tm + CONV_HALO, :] = jnp.where(r < nblk - 1, du1n, 0.0)
        ubuf[CONV_HALO:CONV_HALO + tm, :] = a_ref[...] * _sigmoid(g_ref[...])
        ubuf[0:CONV_HALO, :] = jnp.where(r > 0, ah_ref[...] * _sigmoid(gh_ref[...]), 0.0)

        for ct in range(C // LANE):
            ln = slice(ct * LANE, (ct + 1) * LANE)
            for s in range(tm // sub):
                d_here = dbuf[s * sub:(s + 1) * sub, ln]
                acc = jnp.zeros((sub, LANE), F32)
                for kk in range(CONV_K):
                    fo = s * sub + (CONV_K - 1) - kk
                    acc = acc + w_ref[kk:kk + 1, ln] * dbuf[fo:fo + sub, ln]
                    bo = CONV_HALO + s * sub - (CONV_K - 1) + kk
                    dw_ref[kk:kk + 1, ln] += jnp.sum(d_here * ubuf[bo:bo + sub, ln], axis=0, keepdims=True)
                du0[s * sub:(s + 1) * sub, ln] = acc
        a = a_ref[...]
        sg = _sigmoid(g_ref[...])
        d0 = du0[...]
        dadg_ref[:, 0:C] = (d0 * sg).astype(dadg_ref.dtype)
        dadg_ref[:, C:2 * C] = (d0 * a * sg * (1.0 - sg)).astype(dadg_ref.dtype)
        if n_host:
            pl.when(pl.program_id(0) == nblk - 1)(wait)

    cur = lambda blk: pl.BlockSpec((tm, C), lambda r: (r, blk))
    prev = lambda blk: pl.BlockSpec((CONV_HALO, C), lambda r: (jnp.maximum(r * hpb - 1, 0), blk))
    nxt = lambda blk: pl.BlockSpec((CONV_HALO, C), lambda r: (jnp.minimum((r + 1) * hpb, last_halo), blk))
    vec = pl.BlockSpec((1, C), lambda r: (0, 0))
    wspec = pl.BlockSpec((CONV_HALO, C), lambda r: (0, 0))
    out_shape = (jax.ShapeDtypeStruct((LP, 2 * C), BF16), jax.ShapeDtypeStruct((CONV_HALO, C), F32),
                 jax.ShapeDtypeStruct((1, C), F32), jax.ShapeDtypeStruct((1, C), F32),
                 jax.ShapeDtypeStruct((1, C), F32))
    out_specs = (pl.BlockSpec((tm, 2 * C), lambda r: (r, 0)), wspec, vec, vec, vec)
    in_specs = [cur(A_BLK), cur(G_BLK), prev(A_BLK), prev(G_BLK), cur(0), nxt(0), cur(1), nxt(1), wspec, vec, vec]
    args = [proj, proj, proj, proj, u1, u1, dcat, dcat, cw, lg, lb]
    scratch = [pltpu.VMEM((CONV_HALO + tm, C), F32), pltpu.VMEM((tm + CONV_HALO, C), F32),
               pltpu.VMEM((tm, C), F32)]
    if n_host:
        anyspec = pl.BlockSpec(memory_space=pl.ANY)
        in_specs += [anyspec] * n_host
        args += list(host[0])
        out_shape += host_shapes
        out_specs += (anyspec,) * n_host
        scratch += [pltpu.SemaphoreType.DMA((host_sems,)), pltpu.SemaphoreType.DMA((host_sems,))]
    return pl.pallas_call(
        body, name=name, out_shape=out_shape, grid=(nblk,), in_specs=in_specs, out_specs=out_specs,
        scratch_shapes=scratch, compiler_params=_params(("arbitrary",)))(*args)


FFN_HALO = 8
FFN_TC = 1408
FFN_ROW_SPLIT = 1
FFN_K = 3


def _ffn_conv(buf, w_ref, b_ref, s, sub, ln):
    acc = jnp.broadcast_to(b_ref[:, ln], (sub, LANE))
    for kk in range(FFN_K):
        off = FFN_HALO + s * sub - (FFN_K - 1) + kk
        acc = acc + w_ref[kk:kk + 1, ln] * buf[off:off + sub, ln]
    return acc


def _ffn_act_fwd(up, w, b, tm, name, host=None):
    LP, F = up.shape[0], up.shape[1] // 2
    tm = tm // FFN_ROW_SPLIT
    upg = upv = up
    nct = F // FFN_TC
    sub = _sub_rows(tm)
    hpb = tm // FFN_HALO
    n_host = 0 if host is None else len(host[0])
    nrb = LP // tm

    def body(*refs):
        g_ref, v_ref, gh_ref, vh_ref, wg_ref, wv_ref, bg_ref, bv_ref = refs[:8]
        act_ref = refs[8 + n_host]
        gbuf, vbuf = refs[9 + 2 * n_host:11 + 2 * n_host]
        if n_host:
            start, wait = _gather_first_ops(refs[8:8 + n_host], refs[9 + n_host:9 + 2 * n_host],
                                            refs[11 + 2 * n_host], refs[12 + 2 * n_host], host[1])
            pl.when(jnp.logical_and(pl.program_id(0) == 0, pl.program_id(1) == 0))(start)
        r = pl.program_id(1)
        gbuf[FFN_HALO:FFN_HALO + tm, :] = g_ref[...]
        vbuf[FFN_HALO:FFN_HALO + tm, :] = v_ref[...]
        gbuf[0:FFN_HALO, :] = jnp.where(r > 0, gh_ref[...], 0.0)
        vbuf[0:FFN_HALO, :] = jnp.where(r > 0, vh_ref[...], 0.0)
        for s in range(tm // sub):
            for ct in range(FFN_TC // LANE):
                ln = slice(ct * LANE, (ct + 1) * LANE)
                gc = _ffn_conv(gbuf, wg_ref, bg_ref, s, sub, ln)
                vc = _ffn_conv(vbuf, wv_ref, bv_ref, s, sub, ln)
                act_ref[s * sub:(s + 1) * sub, ln] = (gc * _sigmoid(gc) * vc).astype(act_ref.dtype)
        if n_host:
            pl.when(jnp.logical_and(pl.program_id(0) == nct - 1, pl.program_id(1) == nrb - 1))(wait)

    cur = pl.BlockSpec((tm, FFN_TC), lambda c, r: (r, c))
    halo = pl.BlockSpec((FFN_HALO, FFN_TC), lambda c, r: (jnp.maximum(r * hpb - 1, 0), c))
    wg = pl.BlockSpec((8, FFN_TC), lambda c, r: (0, c))
    wv = pl.BlockSpec((8, FFN_TC), lambda c, r: (0, nct + c))
    bg = pl.BlockSpec((1, FFN_TC), lambda c, r: (0, c))
    bv = pl.BlockSpec((1, FFN_TC), lambda c, r: (0, nct + c))
    curv = pl.BlockSpec((tm, FFN_TC), lambda c, r: (r, nct + c))
    halov = pl.BlockSpec((FFN_HALO, FFN_TC), lambda c, r: (jnp.maximum(r * hpb - 1, 0), nct + c))
    out_shape = jax.ShapeDtypeStruct((LP, F), BF16)
    out_specs = cur
    in_specs = [cur, curv, halo, halov, wg, wv, bg, bv]
    args = [upg, upv, upg, upv, w, w, b, b]
    scratch = [pltpu.VMEM((FFN_HALO + tm, FFN_TC), F32)] * 2
    if n_host:
        anyspec = pl.BlockSpec(memory_space=pl.ANY)
        in_specs += [anyspec] * n_host
        args += list(host[0])
        out_shape = (out_shape,) + tuple(jax.ShapeDtypeStruct(cut.full, s.dtype) for cut, s in zip(host[1], host[0]))
        out_specs = (cur,) + (anyspec,) * n_host
        scratch += [pltpu.SemaphoreType.DMA((4 * n_host,)), pltpu.SemaphoreType.DMA((4 * n_host,))]
    sem = ("arbitrary", "arbitrary") if n_host else ("parallel", "parallel")
    return pl.pallas_call(
        body, name=name, out_shape=out_shape, grid=(nct, nrb), in_specs=in_specs, out_specs=out_specs,
        scratch_shapes=scratch, compiler_params=_params(sem))(*args)


def _ffn_act_bwd(up, dact, w, b, tm, name, comm=None):
    LP, F = up.shape[0], up.shape[1] // 2
    tm = tm // FFN_ROW_SPLIT
    upg = upv = up
    nct = F // FFN_TC
    sub = _sub_rows(tm)
    hpb = tm // FFN_HALO
    nblk = LP // tm
    last_halo = LP // FFN_HALO - 1
    TB = tm + 2 * FFN_HALO
    n_comm = 0 if comm is None else len(comm[0])

    def body(*refs):
        (g_ref, v_ref, gp_ref, vp_ref, gn_ref, vn_ref, da_ref, dan_ref,
         wg_ref, wv_ref, bg_ref, bv_ref) = refs[:12]
        dup_ref, dwg_ref, dwv_ref, dbg_ref, dbv_ref = refs[12 + n_comm:17 + n_comm]
        gbuf, vbuf, dgb, dvb = refs[17 + 2 * n_comm:21 + 2 * n_comm]
        if n_comm:
            start, wait = _chip_exchange_ops(refs[12:12 + n_comm], refs[17 + n_comm:17 + 2 * n_comm],
                                             refs[21 + 2 * n_comm], refs[22 + 2 * n_comm], comm[1])
            pl.when(jnp.logical_and(pl.program_id(0) == 0, pl.program_id(1) == 0))(start)
        r = pl.program_id(1)
        dg_ref = dup_ref.at[0]
        dv_ref = dup_ref.at[1]
        first = r == 0
        last = r == nblk - 1

        @pl.when(first)
        def _():
            dwg_ref[...] = jnp.zeros_like(dwg_ref)
            dwv_ref[...] = jnp.zeros_like(dwv_ref)
            dbg_ref[...] = jnp.zeros_like(dbg_ref)
            dbv_ref[...] = jnp.zeros_like(dbv_ref)

        for buf, c_ref, p_ref, n_ref in ((gbuf, g_ref, gp_ref, gn_ref), (vbuf, v_ref, vp_ref, vn_ref)):
            buf[0:FFN_HALO, :] = jnp.where(first, 0.0, p_ref[...])
            buf[FFN_HALO:FFN_HALO + tm, :] = c_ref[...]
            buf[FFN_HALO + tm:TB, :] = jnp.where(last, 0.0, n_ref[...])

        def dconv(s0, nrows, ln, dact_v):
            xg = [gbuf[s0 - (FFN_K - 1) + kk:s0 - (FFN_K - 1) + kk + nrows, ln] for kk in range(FFN_K)]
            xv = [vbuf[s0 - (FFN_K - 1) + kk:s0 - (FFN_K - 1) + kk + nrows, ln] for kk in range(FFN_K)]
            gc = jnp.broadcast_to(bg_ref[:, ln], (nrows, LANE))
            vc = jnp.broadcast_to(bv_ref[:, ln], (nrows, LANE))
            for kk in range(FFN_K):
                gc = gc + wg_ref[kk:kk + 1, ln] * xg[kk]
                vc = vc + wv_ref[kk:kk + 1, ln] * xv[kk]
            sg = _sigmoid(gc)
            return dact_v * vc * (sg * (1.0 + gc * (1.0 - sg))), dact_v * (gc * sg), xg, xv

        colsum = lambda t: jnp.sum(t, axis=0, keepdims=True)
        for ct in range(FFN_TC // LANE):
            ln = slice(ct * LANE, (ct + 1) * LANE)
            zero = jnp.zeros((1, LANE), F32)
            dwg, dwv, dbg, dbv = [zero] * FFN_K, [zero] * FFN_K, zero, zero
            for s in range(tm // sub):
                dgc, dvc, xg, xv = dconv(FFN_HALO + s * sub, sub, ln, da_ref[s * sub:(s + 1) * sub, ln])
                dgb[s * sub:(s + 1) * sub, ln] = dgc
                dvb[s * sub:(s + 1) * sub, ln] = dvc
                dwg = [dwg[kk] + colsum(dgc * xg[kk]) for kk in range(FFN_K)]
                dwv = [dwv[kk] + colsum(dvc * xv[kk]) for kk in range(FFN_K)]
                dbg, dbv = dbg + colsum(dgc), dbv + colsum(dvc)
            for kk in range(FFN_K):
                dwg_ref[kk:kk + 1, ln] += dwg[kk]
                dwv_ref[kk:kk + 1, ln] += dwv[kk]
            dbg_ref[:, ln] += dbg
            dbv_ref[:, ln] += dbv
            dgc, dvc, _, _ = dconv(FFN_HALO + tm, FFN_HALO, ln, jnp.where(last, 0.0, dan_ref[:, ln]))
            dgb[tm:tm + FFN_HALO, ln] = dgc
            dvb[tm:tm + FFN_HALO, ln] = dvc
            for dbuf, w_ref, dout in ((dgb, wg_ref, dg_ref), (dvb, wv_ref, dv_ref)):
                for s in range(tm // sub):
                    acc = jnp.zeros((sub, LANE), F32)
                    for kk in range(FFN_K):
                        fo = s * sub + (FFN_K - 1) - kk
                        acc = acc + w_ref[kk:kk + 1, ln] * dbuf[fo:fo + sub, ln]
                    dout[s * sub:(s + 1) * sub, ln] = acc.astype(dout.dtype)
        if n_comm:
            pl.when(jnp.logical_and(pl.program_id(0) == nct - 1, pl.program_id(1) == nblk - 1))(wait)

    cur = pl.BlockSpec((tm, FFN_TC), lambda c, r: (r, c))
    prev = pl.BlockSpec((FFN_HALO, FFN_TC), lambda c, r: (jnp.maximum(r * hpb - 1, 0), c))
    nxt = pl.BlockSpec((FFN_HALO, FFN_TC), lambda c, r: (jnp.minimum((r + 1) * hpb, last_halo), c))
    wg = pl.BlockSpec((8, FFN_TC), lambda c, r: (0, c))
    wv = pl.BlockSpec((8, FFN_TC), lambda c, r: (0, nct + c))
    bg = pl.BlockSpec((1, FFN_TC), lambda c, r: (0, c))
    bv = pl.BlockSpec((1, FFN_TC), lambda c, r: (0, nct + c))
    curv = pl.BlockSpec((tm, FFN_TC), lambda c, r: (r, nct + c))
    prevv = pl.BlockSpec((FFN_HALO, FFN_TC), lambda c, r: (jnp.maximum(r * hpb - 1, 0), nct + c))
    nxtv = pl.BlockSpec((FFN_HALO, FFN_TC), lambda c, r: (jnp.minimum((r + 1) * hpb, last_halo), nct + c))
    out_shape = (jax.ShapeDtypeStruct((2, LP, F), BF16),
                 jax.ShapeDtypeStruct((8, F), F32), jax.ShapeDtypeStruct((8, F), F32),
                 jax.ShapeDtypeStruct((1, F), F32), jax.ShapeDtypeStruct((1, F), F32))
    out_specs = (pl.BlockSpec((2, tm, FFN_TC), lambda c, r: (0, r, c)),
                 pl.BlockSpec((8, FFN_TC), lambda c, r: (0, c)),
                 pl.BlockSpec((8, FFN_TC), lambda c, r: (0, c)),
                 pl.BlockSpec((1, FFN_TC), lambda c, r: (0, c)),
                 pl.BlockSpec((1, FFN_TC), lambda c, r: (0, c)))
    in_specs = [cur, curv, prev, prevv, nxt, nxtv, cur, nxt, wg, wv, bg, bv]
    args = [upg, upv, upg, upv, upg, upv, dact, dact, w, w, b, b]
    scratch = [pltpu.VMEM((TB, FFN_TC), F32), pltpu.VMEM((TB, FFN_TC), F32),
               pltpu.VMEM((tm + FFN_HALO, FFN_TC), F32), pltpu.VMEM((tm + FFN_HALO, FFN_TC), F32)]
    if n_comm:
        anyspec = pl.BlockSpec(memory_space=pl.ANY)
        in_specs += [anyspec] * n_comm
        args += list(comm[0])
        out_shape += _chip_exchange_shapes(*comm)
        out_specs += (anyspec,) * n_comm
        scratch += [pltpu.SemaphoreType.DMA((3 * n_comm,)), pltpu.SemaphoreType.DMA((3 * n_comm,))]
    sem = ("arbitrary", "arbitrary") if n_comm else ("parallel", "arbitrary")
    dup, dwg, dwv, dbg, dbv, *others = pl.pallas_call(
        body, name=name, out_shape=out_shape, grid=(nct, nblk), in_specs=in_specs, out_specs=out_specs,
        scratch_shapes=scratch, compiler_params=_params(sem))(*args)
    return (dup, jnp.concatenate([dwg, dwv], axis=1), jnp.concatenate([dbg, dbv], axis=1)) + tuple(others)


POOL_HALO = 16


def _pool_fwd(h, g, pw, pb, ps, tm, name):
    LP, Dm = h.shape
    sub = _sub_rows(tm)
    hpb = tm // POOL_HALO

    def body(h_ref, hh_ref, g_ref, pw_ref, pb_ref, ps_ref, o_ref, d_ref, buf):
        r = pl.program_id(0)
        gg = g_ref[...]

        def norm(x):
            return x * lax.rsqrt(jnp.mean(x * x, axis=1, keepdims=True) + RMS_EPS) * gg

        x = h_ref[...]
        buf[POOL_HALO:POOL_HALO + tm, :] = norm(x)
        buf[0:POOL_HALO, :] = jnp.where(r > 0, norm(hh_ref[...]), 0.0)
        for gi, w in enumerate(POOL_WINDOWS):
            ln = slice(gi * POOL_G, (gi + 1) * POOL_G)
            for s in range(tm // sub):
                base = POOL_HALO + s * sub
                acc = buf[base:base + sub, ln]
                for jj in range(1, w):
                    acc = acc + buf[base - jj:base - jj + sub, ln]
                t = r * tm + s * sub + lax.broadcasted_iota(jnp.int32, (sub, 1), 0)
                cnt = jnp.minimum(t + 1, w).astype(F32)
                d_ref[s * sub:(s + 1) * sub, ln] = (acc / cnt - buf[base:base + sub, ln]).astype(d_ref.dtype)
            y = jnp.dot(d_ref[:, ln], pw_ref[gi], preferred_element_type=F32) + pb_ref[:, ln]
            o_ref[:, ln] = x[:, ln] + y * ps_ref[:, ln]

    row = pl.BlockSpec((tm, Dm), lambda r: (r, 0))
    halo = pl.BlockSpec((POOL_HALO, Dm), lambda r: (jnp.maximum(r * hpb - 1, 0), 0))
    vec = pl.BlockSpec((1, Dm), lambda r: (0, 0))
    wsp = pl.BlockSpec((len(POOL_WINDOWS), POOL_G, POOL_G), lambda r: (0, 0, 0))
    return pl.pallas_call(
        body, name=name,
        out_shape=(jax.ShapeDtypeStruct((LP, Dm), F32), jax.ShapeDtypeStruct((LP, Dm), BF16)),
        grid=(LP // tm,), in_specs=[row, halo, vec, wsp, vec, vec], out_specs=(row, row),
        scratch_shapes=[pltpu.VMEM((POOL_HALO + tm, Dm), F32)],
        compiler_params=_params(("parallel",)))(h, h, g, pw, pb, ps)


def _pool_bwd(h, g, d, pw, pb, ps, dh_out, tm, name):
    LP, Dm = h.shape
    sub = _sub_rows(tm)
    hpb = tm // POOL_HALO
    nblk = LP // tm
    last_halo = LP // POOL_HALO - 1
    nt = (((1,), (1,)), ((), ()))
    tn = (((0,), (0,)), ((), ()))

    def body(h_ref, g_ref, d_ref, pw_ref, pb_ref, ps_ref, do_ref, don_ref,
             dh_ref, dpw_ref, dpb_ref, dps_ref, dg_ref, ebuf, ddb, dnb):
        r = pl.program_id(0)

        @pl.when(r == 0)
        def _():
            dpw_ref[...] = jnp.zeros_like(dpw_ref)
            dpb_ref[...] = jnp.zeros_like(dpb_ref)
            dps_ref[...] = jnp.zeros_like(dps_ref)
            dg_ref[...] = jnp.zeros_like(dg_ref)

        for gi, w in enumerate(POOL_WINDOWS):
            ln = slice(gi * POOL_G, (gi + 1) * POOL_G)
            wg = pw_ref[gi]
            dog = do_ref[:, ln]
            dg_b = d_ref[:, ln]
            y_pre = jnp.dot(dg_b, wg, preferred_element_type=F32) + pb_ref[:, ln]
            dps_ref[:, ln] += jnp.sum(dog * y_pre, axis=0, keepdims=True)
            dy = dog * ps_ref[:, ln]
            dpb_ref[:, ln] += jnp.sum(dy, axis=0, keepdims=True)
            dyb = dy.astype(BF16)
            dpw_ref[gi] += lax.dot_general(dg_b, dyb, tn, preferred_element_type=F32)
            dd = lax.dot_general(dyb, wg, nt, preferred_element_type=F32)
            ddb[:, ln] = dd
            t = r * tm + lax.broadcasted_iota(jnp.int32, (tm, 1), 0)
            ebuf[0:tm, ln] = dd / jnp.minimum(t + 1, w).astype(F32)
            dyn = (don_ref[:, ln] * ps_ref[:, ln]).astype(BF16)
            ddn = lax.dot_general(dyn, wg, nt, preferred_element_type=F32)
            tn_ = (r + 1) * tm + lax.broadcasted_iota(jnp.int32, (POOL_HALO, 1), 0)
            ebuf[tm:tm + POOL_HALO, ln] = jnp.where(r < nblk - 1, ddn / jnp.minimum(tn_ + 1, w).astype(F32), 0.0)
            for s in range(tm // sub):
                acc = ebuf[s * sub:(s + 1) * sub, ln]
                for jj in range(1, w):
                    acc = acc + ebuf[s * sub + jj:s * sub + jj + sub, ln]
                dnb[s * sub:(s + 1) * sub, ln] = acc - ddb[s * sub:(s + 1) * sub, ln]
        x = h_ref[...]
        rr = lax.rsqrt(jnp.mean(x * x, axis=1, keepdims=True) + RMS_EPS)
        xhat = x * rr
        dn = dnb[...]
        dxh = dn * g_ref[...]
        dh_ref[...] = do_ref[...] + rr * (dxh - xhat * jnp.mean(dxh * xhat, axis=1, keepdims=True))
        dg_ref[...] += jnp.sum(dn * xhat, axis=0, keepdims=True)

    row = pl.BlockSpec((tm, Dm), lambda r: (r, 0))
    nxt = pl.BlockSpec((POOL_HALO, Dm), lambda r: (jnp.minimum((r + 1) * hpb, last_halo), 0))
    vec = pl.BlockSpec((1, Dm), lambda r: (0, 0))
    wsp = pl.BlockSpec((len(POOL_WINDOWS), POOL_G, POOL_G), lambda r: (0, 0, 0))
    return pl.pallas_call(
        body, name=name,
        out_shape=(jax.ShapeDtypeStruct((LP, Dm), F32),
                   jax.ShapeDtypeStruct((len(POOL_WINDOWS), POOL_G, POOL_G), F32),
                   jax.ShapeDtypeStruct((1, Dm), F32), jax.ShapeDtypeStruct((1, Dm), F32),
                   jax.ShapeDtypeStruct((1, Dm), F32)),
        grid=(nblk,), in_specs=[row, vec, row, wsp, vec, vec, row, nxt],
        out_specs=(row, wsp, vec, vec, vec),
        scratch_shapes=[pltpu.VMEM((tm + POOL_HALO, Dm), F32), pltpu.VMEM((tm, Dm), F32),
                        pltpu.VMEM((tm, Dm), F32)],
        compiler_params=_params(("arbitrary",)))(h, g, d, pw, pb, ps, dh_out, dh_out)


def _adamw(w, g, m, v, name):
    shape = w.shape
    cols = shape[-1]
    rows = int(np.prod(shape[:-1])) if len(shape) > 1 else 1
    w2, g2, m2, v2 = (t.reshape(rows, cols) for t in (w, g, m, v))
    tr = rows
    for cand in (256, 128, 64, 32, 16, 8):
        if rows % cand == 0 and rows > cand:
            tr = cand
            break
    c1 = float(1.0 - ADAM_B1 ** ADAM_STEP)
    c2 = float(1.0 - ADAM_B2 ** ADAM_STEP)

    def body(w_ref, g_ref, m_ref, v_ref, d_ref, mo_ref, vo_ref):
        gg = g_ref[...]
        mn = ADAM_B1 * m_ref[...] + (1.0 - ADAM_B1) * gg
        vn = ADAM_B2 * v_ref[...] + (1.0 - ADAM_B2) * (gg * gg)
        m_hat = mn / c1
        v_hat = vn / c2
        d_ref[...] = -ADAM_LR * (m_hat / (jnp.sqrt(v_hat) + ADAM_EPS) + ADAM_WD * w_ref[...])
        mo_ref[...] = mn
        vo_ref[...] = vn

    spec = pl.BlockSpec((tr, cols), lambda i: (i, 0))
    sds = jax.ShapeDtypeStruct((rows, cols), F32)
    d2, mo, vo = pl.pallas_call(
        body, name=name, out_shape=(sds, sds, sds), grid=(rows // tr,),
        in_specs=[spec] * 4, out_specs=(spec,) * 3,
        compiler_params=_params(("parallel",)))(w2, g2, m2, v2)
    return d2.reshape(shape), mo.reshape(shape), vo.reshape(shape)


def _row_tiles(LP):
    tm = LP // 4
    assert LP % 4 == 0 and tm % CONV_HALO == 0 and LP % ATT_BLK == 0, LP
    return tm, LP // 2


MESH = pl.DeviceIdType.MESH
ANY = pl.BlockSpec(memory_space=pl.ANY)


def _coords():
    return lax.axis_index("x"), lax.axis_index("y"), lax.axis_index("c")


def _other_chips(x, y):
    return [(1 - x, y), (x, 1 - y), (1 - x, 1 - y)]


def _allreduce_small(pack):
    Rs, C = pack.shape
    n_dev = 8

    def body(x_ref, o_ref, buf, send_sems, recv_sems):
        x, y, c = _coords()
        me = 4 * x + 2 * y + c
        buf[me] = x_ref[...]
        peers = []
        for rel in range(1, n_dev):
            px = 1 - x if rel & 4 else x
            py = 1 - y if rel & 2 else y
            pc = 1 - c if rel & 1 else c
            peers.append((px, py, pc))
        sends = [pltpu.make_async_remote_copy(
            src_ref=x_ref, dst_ref=buf.at[me], send_sem=send_sems.at[k], recv_sem=recv_sems.at[k],
            device_id=peer, device_id_type=MESH) for k, peer in enumerate(peers)]
        for cp in sends:
            cp.start()
        for k, (px, py, pc) in enumerate(peers):
            pltpu.make_async_remote_copy(
                src_ref=x_ref, dst_ref=buf.at[4 * px + 2 * py + pc], send_sem=send_sems.at[k],
                recv_sem=recv_sems.at[k], device_id=(px, py, pc), device_id_type=MESH).wait_recv()
        for cp in sends:
            cp.wait_send()
        acc = buf[0]
        for d in range(1, n_dev):
            acc = acc + buf[d]
        o_ref[...] = acc

    vm = pl.BlockSpec(memory_space=pltpu.VMEM)
    return pl.pallas_call(
        body, name="allreduce_replicated", out_shape=jax.ShapeDtypeStruct((Rs, C), F32),
        in_specs=[vm], out_specs=vm,
        scratch_shapes=[pltpu.VMEM((n_dev, Rs, C), F32), pltpu.SemaphoreType.DMA((n_dev - 1,)),
                        pltpu.SemaphoreType.DMA((n_dev - 1,))],
    )(pack)


REPLICATED = ("mix_norm_even", "b_f", "conv_b", "ln_g", "ln_b", "ffn_norm", "ffn_conv_b", "final_norm")


def _pad_rows(flat, align_rows, cols):
    rows = -(-flat.shape[-1] // cols)
    rows = -(-rows // align_rows) * align_rows
    pad = rows * cols - flat.shape[-1]
    flat = jnp.pad(flat, [(0, 0)] * (flat.ndim - 1) + [(0, pad)])
    return flat.reshape(flat.shape[:-1] + (rows, cols))


def _pack_replicated(grads, loss):
    parts = [_pad_rows(grads[name].astype(F32).reshape(-1), 1, LANE).reshape(-1) for name in REPLICATED]
    parts.append(_pad_rows(loss.reshape(-1)[:1], 1, LANE).reshape(-1))
    return _pad_rows(jnp.concatenate(parts), 8, LANE)


def _unpack_replicated(reduced, shapes):
    flat = reduced.reshape(-1)
    out, off = {}, 0
    for name in REPLICATED:
        n = int(np.prod(shapes[name]))
        out[name] = flat[off:off + n].reshape(shapes[name])
        off += -(-n // LANE) * LANE
    return out, flat[off]


def _ffn_fwd2(h, W, layer, tm, tmm, host_up=None, host_act=None):
    tag = str(layer)
    n = _rms_fwd(h, W["ffn_norm"][layer:layer + 1], BF16, tm, f"ffn_norm_{tag}")
    up, *g_up = _mm(n, W["w_up"][layer], "nn", F32, tmm, UP_SHARD, f"ffn_up_{tag}", host=host_up) \
        if host_up else (_mm(n, W["w_up"][layer], "nn", F32, tmm, UP_SHARD, f"ffn_up_{tag}"),)
    act, *g_act = _ffn_act_fwd(up, W["ffn_conv_w_p"][layer], W["ffn_conv_b"][layer:layer + 1], tm,
                               f"ffn_act_{tag}", host=host_act) \
        if host_act else (_ffn_act_fwd(up, W["ffn_conv_w_p"][layer], W["ffn_conv_b"][layer:layer + 1], tm,
                                       f"ffn_act_{tag}"),)
    out = _mm(act, W["w_down"][layer], "nn", F32, tm, D_MODEL, f"ffn_down_{tag}", add=h)
    return out, (n, up, act), g_up + g_act


def _ffn_bwd2(h, W, layer, saved, dout, tm, tmm, reduce=None):
    tag = str(layer)
    n, up, act = saved
    parts, comm = [], None
    if reduce is None:
        dact = _mm(dout, W["w_down"][layer], "nt", F32, tmm, UP_SHARD, f"ffn_dact_{tag}")
    else:
        names, fulls, cuts = reduce
        dact, *recv = _mm(dout, W["w_down"][layer], "nt", F32, tmm, UP_SHARD, f"ffn_dact_{tag}",
                          host=(fulls, cuts, "pairx"))
        parts = [_pair_sum2(f, r, cut, PAIR_SUM_BLOCKS[nm], "grad_pair_sum_" + nm)
                 for f, r, cut, nm in zip(fulls, recv, cuts, names)]
        comm = (parts, cuts)
    dwd = _mm(act, dout, "tn", F32, D_FF // 2, 512, f"ffn_dwdown_{tag}")
    dup, dcw, dcb, *others = _ffn_act_bwd(up, dact, W["ffn_conv_w_p"][layer], W["ffn_conv_b"][layer:layer + 1],
                                          tm, f"ffn_act_bwd_{tag}", comm=comm)
    dn = _mm_ffn_dn(dup, W["w_up"][layer], tm, D_MODEL, f"ffn_dn_{tag}")
    dwu = _mm_ffn_dwup(n, dup, 512, D_FF // 2, f"ffn_dwup_{tag}")
    dh, dgain = _rms_bwd(h, W["ffn_norm"][layer:layer + 1], dn, dout, tm, f"ffn_norm_bwd_{tag}")
    return dh, (dwu, dwd), dict(gain=dgain, cw=dcw[:FFN_K], cb=dcb), parts, others


GATHER_FIRST = ("w_in", "small")
GATHER_LATE = ("pool_w", "w_up", "w_down")
HOSTED_FFN = ("w_up1", "w_down1")
HOSTED = ("w_out", "pool_w", "w_up0", "w_down0")
LATE = ("small",)


def _local_step2(h0, tgt, W, n_real, cut_of):
    LP = h0.shape[0]
    tm, tmm = _row_tiles(LP)
    nb = LP // ATT_BLK
    G = {}
    n0 = _rms_fwd(h0, W["mix_norm_even"], BF16, tm, "mix_norm_even")
    sh = W["late_shards"]
    stage = lambda *names: ([sh[n] for n in names], [cut_of[n] for n in names])
    proj, g_down0 = _mm(n0, W["w_in_p"], "nn", F32, tmm, 896, "in_proj", host=stage("w_down0"))
    c = _fgate_fwd(proj, W["b_f_p"], "forget_gate")
    qT, kT, k_aug, vT = _attn_prep(proj, c, "attention_operands")
    oT, lse, g_pool, g_up0, g_out = _attn_fwd2(qT, k_aug, vT, "fox_attention",
                                               comm=stage("pool_w", "w_up0", "w_out"))
    g_down0, g_pool, g_up0, g_out = _gather_forward(
        [g_down0, g_pool, g_up0, g_out], stage("w_down0", "pool_w", "w_up0", "w_out")[1], "gather_forward_0")
    W = dict(W)
    W.update(pool_w=g_pool, w_up=[g_up0, None], w_down=[g_down0, None], w_out=g_out)
    u1, u = _conf_fwd(proj, W["conv_w_p"], W["conv_b"], W["ln_g"], W["ln_b"], tm, "conformer")
    cat = jnp.concatenate([_attn_rows(oT, 1.0, BF16, "attention_rows"), u], axis=1)
    h1 = _mm(cat, W["w_out"], "nn", F32, tmm, D_MODEL, "out_proj", add=h0)
    h2, ffn0, (g_down1, g_up1) = _ffn_fwd2(h1, W, 0, tm, tmm, host_up=stage("w_down1"), host_act=stage("w_up1"))
    g_down1, g_up1 = _gather_forward([g_down1, g_up1], stage("w_down1", "w_up1")[1], "gather_forward_1")
    W.update(w_up=[g_up0, g_up1], w_down=[g_down0, g_down1])
    h3, dpool = _pool_fwd(h2, W["mix_norm_odd"], W["pool_w"], W["pool_b"], W["pool_scale"], tm, "pool_mixer")
    h4, ffn1, _ = _ffn_fwd2(h3, W, 1, tm, tmm)
    loss, dh4, G["final_norm"] = _loss_head(h4, W["final_norm"], tgt, n_real, tm, "loss_head")

    dh3, (G["w_up1"], G["w_down1"]), g1, _, _ = _ffn_bwd2(h3, W, 1, ffn1, dh4, tm, tmm)
    dh2, G["pool_w"], G["pool_b"], G["pool_scale"], G["mix_norm_odd"] = _pool_bwd(
        h2, W["mix_norm_odd"], dpool, W["pool_w"], W["pool_b"], W["pool_scale"], dh3, tm, "pool_mixer_bwd")
    cuts1 = [cut_of[n] for n in HOSTED_FFN]
    dh1, (G["w_up0"], G["w_down0"]), g0, parts1, others1 = _ffn_bwd2(
        h1, W, 0, ffn0, dh2, tm, tmm, reduce=(HOSTED_FFN, [G[n] for n in HOSTED_FFN], cuts1))
    G["ffn_norm"] = jnp.concatenate([g0["gain"], g1["gain"]], axis=0)
    G["ffn_conv_w"] = jnp.stack([g0["cw"], g1["cw"]])
    G["ffn_conv_b"] = jnp.concatenate([g0["cb"], g1["cb"]], axis=0)

    dcat = _mm(dh1, W["w_out"], "nt", F32, tmm, D_MODEL, "out_proj_dx")
    G["w_out"] = _mm(cat, dh1, "tn", F32, 512, D_MODEL, "out_proj_dw")
    hcuts = [cut_of[n] for n in HOSTED]
    hfull = [G[n] for n in HOSTED]
    dadg, dcw, G["conv_b"], G["ln_g"], G["ln_b"], *hrecv = _conf_bwd(
        proj, u1, dcat, W["conv_w_p"], W["ln_g"], W["ln_b"], tm, "conformer_bwd", host=(hfull, hcuts, "pairx"))
    G["conv_w"] = dcw[:CONV_K]
    doT = _attn_cols(dcat, "attention_do_cols")
    hparts = [_pair_sum2(f, r, cut, PAIR_SUM_BLOCKS[n], "grad_pair_sum_" + n)
              for f, r, cut, n in zip(hfull, hrecv, hcuts, HOSTED)]
    dqT, dkT, dvT, *hothers = _attn_bwd2(qT, kT, k_aug, vT, oT, doT, lse, "fox_attention_bwd",
                                         comm=(hparts, hcuts))
    dqkv, dc = _attn_grads_rows(dqT, dkT, dvT, "attention_grads_rows")
    dfl, dbf = _fgate_bwd(proj, W["b_f_p"], dc, "forget_gate_bwd")
    G["b_f"] = dbf[:, :HEADS]
    dproj = jnp.concatenate([dqkv, dadg, dfl], axis=1)
    gp = _mm(n0, dproj, "tn", F32, 512, 896, "in_proj_dw")
    g_w_in = jnp.concatenate([gp[:, :3 * FOX_W], gp[:, 3 * FOX_W + 2 * CONV_CH:3 * FOX_W + 2 * CONV_CH + HEADS],
                              gp[:, 3 * FOX_W:3 * FOX_W + 2 * CONV_CH]], axis=1)
    g_w_in = g_w_in.reshape(D_MODEL, N_CHIPS, IN_SHARD).transpose(1, 0, 2)
    icut = [cut_of["w_in"]]
    dn0, irecv = _mm(dproj, W["w_in_p"], "nt", F32, tmm, D_MODEL, "in_proj_dx", host=([g_w_in], icut, "pairx"))
    ipart = _pair_sum2(g_w_in, irecv, icut[0], PAIR_SUM_BLOCKS["w_in"], "grad_pair_sum_w_in")
    dh0, G["mix_norm_even"], iother = _rms_bwd(h0, W["mix_norm_even"], dn0, dh1, tm, "mix_norm_even_bwd",
                                               host=([ipart], icut, "chipx"))
    parts = dict(zip(HOSTED_FFN + HOSTED + ("w_in",), parts1 + hparts + [ipart]))
    others = dict(zip(HOSTED_FFN + HOSTED + ("w_in",), list(others1) + list(hothers) + [iother]))
    return loss, dh0, G, parts, others


class _Cut:
    def __init__(self, full_shape, chip_dim, half_dim):
        self.full = tuple(full_shape)
        self.chip_dim, self.half_dim = chip_dim, half_dim
        self.chip_size = full_shape[chip_dim] // N_CHIPS
        self.half_size = full_shape[half_dim] // 2
        assert chip_dim != half_dim

    def shape(self, chip=False, half=False):
        s = list(self.full)
        if chip:
            s[self.chip_dim] = self.chip_size
        if half:
            s[self.half_dim] = self.half_size
        return tuple(s)

    def region(self, ref, chip=None, half=None):
        idx = [pl.ds(0, n) for n in ref.shape]
        if chip is not None:
            idx[self.chip_dim] = pl.ds(chip * self.chip_size, self.chip_size)
        if half is not None:
            idx[self.half_dim] = pl.ds(half * self.half_size, self.half_size)
        return ref.at[tuple(idx)]


SMALL_SHARDED = ("meta_tokens", "mix_norm_odd", "pool_b", "pool_scale", "conv_w", "ffn_conv_w")
SMALL_ROWS = 144


def _cuts():
    return {
        "w_in": _Cut((N_CHIPS, D_MODEL, IN_SHARD), 0, 1),
        "w_out": _Cut((D_MODEL, D_MODEL), 0, 1),
        "pool_w": _Cut((len(POOL_WINDOWS), POOL_G, POOL_G), 1, 0),
        "w_up": _Cut((2, D_MODEL, 2 * D_FF), 2, 1),
        "w_down": _Cut((2, D_FF, D_MODEL), 1, 2),
        "small": _Cut((N_CHIPS, SMALL_ROWS, LANE), 0, 1),
        "w_up0": _Cut((D_MODEL, 2 * D_FF), 1, 0), "w_up1": _Cut((D_MODEL, 2 * D_FF), 1, 0),
        "w_down0": _Cut((D_FF, D_MODEL), 0, 1), "w_down1": _Cut((D_FF, D_MODEL), 0, 1),
    }


COMM_ORDER = ("w_in", "w_out", "pool_w", "w_up", "w_down", "small")


def _remote(src, dst, send_sems, recv_sems, k, to):
    return pltpu.make_async_remote_copy(src_ref=src, dst_ref=dst, send_sem=send_sems.at[k],
                                        recv_sem=recv_sems.at[k], device_id=to, device_id_type=MESH)


def _gather_weights(shards, cuts):
    n = len(shards)

    def body(*refs):
        srcs, outs = refs[:n], refs[n:2 * n]
        send_sems, recv_sems = refs[2 * n:]
        x, y, c = _coords()
        me = 2 * x + y
        sibling = (x, y, 1 - c)
        chips = _other_chips(x, y)
        sends = []
        for t, cut in enumerate(cuts):
            push = _remote(srcs[t], cut.region(outs[t], chip=me), send_sems, recv_sems, 7 * t, sibling)
            push.start()
            sends.append(push)
            for kk, chip in enumerate(chips):
                cp = _remote(cut.region(srcs[t], half=c), cut.region(outs[t], chip=me, half=c),
                             send_sems, recv_sems, 7 * t + 1 + kk, (*chip, c))
                cp.start()
                sends.append(cp)
        for t, cut in enumerate(cuts):
            for kk, (px, py) in enumerate(chips):
                landed = cut.region(outs[t], chip=2 * px + py, half=c)
                _remote(landed, landed, send_sems, recv_sems, 7 * t + 1 + kk, sibling).wait_recv()
                fwd = _remote(landed, landed, send_sems, recv_sems, 7 * t + 4 + kk, sibling)
                fwd.start()
                sends.append(fwd)
        for t, cut in enumerate(cuts):
            mine = cut.region(outs[t], chip=me)
            _remote(mine, mine, send_sems, recv_sems, 7 * t, sibling).wait_recv()
            for kk, (px, py) in enumerate(chips):
                other = cut.region(outs[t], chip=2 * px + py, half=1 - c)
                _remote(other, other, send_sems, recv_sems, 7 * t + 4 + kk, sibling).wait_recv()
        for cp in sends:
            cp.wait_send()

    return pl.pallas_call(
        body, name="gather_weights",
        out_shape=tuple(jax.ShapeDtypeStruct(cut.full, s.dtype) for cut, s in zip(cuts, shards)),
        in_specs=[ANY] * n, out_specs=tuple([ANY] * n),
        scratch_shapes=[pltpu.SemaphoreType.DMA((7 * n,)), pltpu.SemaphoreType.DMA((7 * n,))],
    )(*shards)


def _gather_first_ops(srcs, outs, send_sems, recv_sems, cuts):
    x, y, c = _coords()
    me = 2 * x + y
    sibling = (x, y, 1 - c)
    chips = _other_chips(x, y)

    def copies():
        out = []
        for t, cut in enumerate(cuts):
            out.append(_remote(srcs[t], cut.region(outs[t], chip=me), send_sems, recv_sems, 4 * t, sibling))
            for kk, chip in enumerate(chips):
                out.append(_remote(cut.region(srcs[t], half=c), cut.region(outs[t], chip=me, half=c),
                                   send_sems, recv_sems, 4 * t + 1 + kk, (*chip, c)))
        return out

    def start():
        for cp in copies():
            cp.start()

    def wait():
        for t, cut in enumerate(cuts):
            mine = cut.region(outs[t], chip=me)
            _remote(mine, mine, send_sems, recv_sems, 4 * t, sibling).wait_recv()
            for kk, (px, py) in enumerate(chips):
                landed = cut.region(outs[t], chip=2 * px + py, half=c)
                _remote(landed, landed, send_sems, recv_sems, 4 * t + 1 + kk, sibling).wait_recv()
        for cp in copies():
            cp.wait_send()

    return start, wait


def _pair_exchange_ops(srcs, outs, send_sems, recv_sems, cuts):
    x, y, c = _coords()

    def copies():
        return [_remote(cut.region(srcs[t], half=1 - c), outs[t], send_sems, recv_sems, t, (x, y, 1 - c))
                for t, cut in enumerate(cuts)]

    def start():
        for cp in copies():
            cp.start()

    def wait():
        for cp in copies():
            cp.wait()

    return start, wait


def _host_plan(host):
    arrays, cuts = host[0], host[1]
    if len(host) > 2 and host[2] == "chipx":
        return _chip_exchange_shapes(arrays, cuts), 3 * len(arrays), _chip_exchange_ops
    if len(host) > 2 and host[2] == "pairx":
        return (tuple(jax.ShapeDtypeStruct(cut.shape(half=True), a.dtype) for cut, a in zip(cuts, arrays)),
                len(arrays), _pair_exchange_ops)
    return (tuple(jax.ShapeDtypeStruct(cut.full, a.dtype) for cut, a in zip(cuts, arrays)),
            4 * len(arrays), _gather_first_ops)


def _gather_forward(fulls, cuts, name):
    n = len(fulls)

    def body(*refs):
        outs = refs[n:2 * n]
        send_sems, recv_sems = refs[2 * n:]
        x, y, c = _coords()
        sibling = (x, y, 1 - c)
        chips = _other_chips(x, y)
        sends = []
        for t, cut in enumerate(cuts):
            for kk, (px, py) in enumerate(chips):
                landed = cut.region(outs[t], chip=2 * px + py, half=c)
                cp = _remote(landed, landed, send_sems, recv_sems, 3 * t + kk, sibling)
                cp.start()
                sends.append(cp)
        for t, cut in enumerate(cuts):
            for kk, (px, py) in enumerate(chips):
                other = cut.region(outs[t], chip=2 * px + py, half=1 - c)
                _remote(other, other, send_sems, recv_sems, 3 * t + kk, sibling).wait_recv()
        for cp in sends:
            cp.wait_send()

    return pl.pallas_call(
        body, name=name,
        out_shape=tuple(jax.ShapeDtypeStruct(f.shape, f.dtype) for f in fulls),
        in_specs=[ANY] * n, out_specs=tuple([ANY] * n), input_output_aliases={t: t for t in range(n)},
        scratch_shapes=[pltpu.SemaphoreType.DMA((3 * n,)), pltpu.SemaphoreType.DMA((3 * n,))],
    )(*fulls)


def _pair_exchange2(fulls, cuts, name):
    n = len(fulls)

    def body(*refs):
        srcs, outs = refs[:n], refs[n:2 * n]
        send_sems, recv_sems = refs[2 * n:]
        x, y, c = _coords()
        cps = [_remote(cut.region(srcs[t], half=1 - c), outs[t], send_sems, recv_sems, t, (x, y, 1 - c))
               for t, cut in enumerate(cuts)]
        for cp in cps:
            cp.start()
        for cp in cps:
            cp.wait()

    return pl.pallas_call(
        body, name=name,
        out_shape=tuple(jax.ShapeDtypeStruct(cut.shape(half=True), f.dtype) for cut, f in zip(cuts, fulls)),
        in_specs=[ANY] * n, out_specs=tuple([ANY] * n),
        scratch_shapes=[pltpu.SemaphoreType.DMA((n,)), pltpu.SemaphoreType.DMA((n,))],
    )(*fulls)


def _grid_of(shape, blk):
    assert all(s % b == 0 for s, b in zip(shape, blk)), (shape, blk)
    return tuple(s // b for s, b in zip(shape, blk))


def _pair_sum2(full, recv, cut, blk, name):
    hshape = cut.shape(half=True)
    grid = _grid_of(hshape, blk)
    hb = cut.half_size // blk[cut.half_dim]
    hd = cut.half_dim
    pos = jnp.stack([lax.axis_index("c")]).astype(jnp.int32)

    def full_idx(*a):
        ids, p = list(a[:-1]), a[-1]
        ids[hd] = ids[hd] + p[0] * hb
        return tuple(ids)

    def body(p_ref, f_ref, r_ref, o_ref):
        o_ref[...] = (f_ref[...] + r_ref[...]).astype(o_ref.dtype)

    return pl.pallas_call(
        body, name=name, out_shape=jax.ShapeDtypeStruct(hshape, BF16),
        grid_spec=pltpu.PrefetchScalarGridSpec(
            num_scalar_prefetch=1, grid=grid,
            in_specs=[pl.BlockSpec(blk, full_idx), pl.BlockSpec(blk, lambda *a: tuple(a[:-1]))],
            out_specs=pl.BlockSpec(blk, lambda *a: tuple(a[:-1]))),
        compiler_params=_params(("parallel",) * len(grid)))(pos, full, recv)


def _chip_exchange_ops(srcs, outs, send_sems, recv_sems, cuts):
    x, y, c = _coords()
    me = 2 * x + y
    chips = _other_chips(x, y)

    def copies():
        return [_remote(cut.region(srcs[t], chip=2 * px + py), outs[t].at[me], send_sems, recv_sems,
                        3 * t + kk, (px, py, c))
                for t, cut in enumerate(cuts) for kk, (px, py) in enumerate(chips)]

    def start():
        for cp in copies():
            cp.start()

    def wait():
        for t, cut in enumerate(cuts):
            for kk, (px, py) in enumerate(chips):
                slot = outs[t].at[2 * px + py]
                _remote(slot, slot, send_sems, recv_sems, 3 * t + kk, (px, py, c)).wait_recv()
        for cp in copies():
            cp.wait_send()

    return start, wait


def _chip_exchange_shapes(parts, cuts):
    return tuple(jax.ShapeDtypeStruct((N_CHIPS,) + cut.shape(chip=True, half=True), p.dtype)
                 for cut, p in zip(cuts, parts))


def _chip_exchange2(parts, cuts):
    n = len(parts)

    def body(*refs):
        start, wait = _chip_exchange_ops(refs[:n], refs[n:2 * n], refs[2 * n], refs[2 * n + 1], cuts)
        start()
        wait()

    return pl.pallas_call(
        body, name="grad_chip_exchange",
        out_shape=tuple(jax.ShapeDtypeStruct((N_CHIPS,) + cut.shape(chip=True, half=True), p.dtype)
                        for cut, p in zip(cuts, parts)),
        in_specs=[ANY] * n, out_specs=tuple([ANY] * n),
        scratch_shapes=[pltpu.SemaphoreType.DMA((3 * n,)), pltpu.SemaphoreType.DMA((3 * n,))],
    )(*parts)


def _chip_sum2(part, recv, cut, blk, name, stacked=None):
    bshape = cut.shape(chip=True, half=True)
    grid = _grid_of(bshape, blk)
    cb = cut.chip_size // blk[cut.chip_dim]
    hb = cut.half_size // blk[cut.half_dim]
    cd, hd = cut.chip_dim, cut.half_dim
    x, y, c = _coords()
    slots = [2 * px + py for px, py in _other_chips(x, y)]
    pos = jnp.stack([c, 2 * x + y] + slots).astype(jnp.int32)

    def part_idx(*a):
        ids, p = list(a[:-1]), a[-1]
        ids[cd] = ids[cd] + p[1] * cb
        return tuple(ids)

    def recv_idx(kk):
        return lambda *a: (a[-1][2 + kk],) + tuple(a[:-1])

    def out_idx(*a):
        ids, p = list(a[:-1]), a[-1]
        ids[hd] = ids[hd] + p[0] * hb
        return tuple(ids)

    def body(p_ref, own_ref, r0_ref, r1_ref, r2_ref, *rest):
        acc = own_ref[...].astype(F32)
        for r_ref in (r0_ref, r1_ref, r2_ref):
            acc = acc + r_ref[...].astype(F32)
        rest[-1][...] = acc

    in_specs = [pl.BlockSpec(blk, part_idx)] + [pl.BlockSpec((None,) + blk, recv_idx(kk)) for kk in range(3)]
    args = [pos, part, recv, recv, recv]
    aliases = {}
    if stacked is None:
        out_shape = jax.ShapeDtypeStruct(cut.shape(chip=True), F32)
        out_spec = pl.BlockSpec(blk, out_idx)
    else:
        lead, n_lead, into = stacked
        out_shape = jax.ShapeDtypeStruct((n_lead,) + cut.shape(chip=True), F32)
        out_spec = pl.BlockSpec((None,) + blk, lambda *a: (lead,) + out_idx(*a))
        if into is not None:
            in_specs.append(pl.BlockSpec(memory_space=pl.ANY))
            args.append(into)
            aliases = {5: 0}
    return pl.pallas_call(
        body, name=name, out_shape=out_shape,
        grid_spec=pltpu.PrefetchScalarGridSpec(num_scalar_prefetch=1, grid=grid, in_specs=in_specs,
                                               out_specs=out_spec),
        input_output_aliases=aliases, compiler_params=_params(("parallel",) * len(grid)))(*args)


def _pair_swap2(blocks, cuts):
    n = len(blocks)

    def body(*refs):
        outs = refs[n:2 * n]
        send_sems, recv_sems = refs[2 * n:]
        x, y, c = _coords()
        cps = []
        for t, cut in enumerate(cuts):
            mine = cut.region(outs[t], half=c)
            cp = _remote(mine, mine, send_sems, recv_sems, t, (x, y, 1 - c))
            cp.start()
            cps.append(cp)
        for t, cut in enumerate(cuts):
            theirs = cut.region(outs[t], half=1 - c)
            _remote(theirs, theirs, send_sems, recv_sems, t, (x, y, 1 - c)).wait_recv()
        for cp in cps:
            cp.wait_send()

    return pl.pallas_call(
        body, name="grad_pair_swap",
        out_shape=tuple(jax.ShapeDtypeStruct(b.shape, b.dtype) for b in blocks),
        in_specs=[ANY] * n, out_specs=tuple([ANY] * n),
        input_output_aliases={t: t for t in range(n)},
        scratch_shapes=[pltpu.SemaphoreType.DMA((n,)), pltpu.SemaphoreType.DMA((n,))],
    )(*blocks)


PAIR_SUM_BLOCKS = {"w_in": (1, 512, IN_SHARD), "w_out": (512, 512), "pool_w": (1, POOL_G, POOL_G),
                   "w_up0": (128, 2 * D_FF), "w_up1": (128, 2 * D_FF), "w_down0": (1408, 512), "w_down1": (1408, 512),
                   "small": (N_CHIPS, SMALL_ROWS // 2, LANE)}
CHIP_SUM_BLOCKS = {"w_in": (1, 512, IN_SHARD), "w_out": (256, 512), "pool_w": (2, 64, POOL_G),
                   "w_up0": (512, UP_SHARD), "w_up1": (512, UP_SHARD),
                   "w_down0": (DOWN_SHARD, 512), "w_down1": (DOWN_SHARD, 512),
                   "small": (1, SMALL_ROWS // 2, LANE)}


def _small_rows(t, lead):
    flat = t.reshape(lead + (-1,))
    pad = -flat.shape[-1] % LANE
    return jnp.pad(flat, [(0, 0)] * len(lead) + [(0, pad)]).reshape(lead + (-1, LANE))


def _pack_small_shards(shards):
    rows = jnp.concatenate([_small_rows(shards[n].astype(F32), ()) for n in SMALL_SHARDED], axis=0)
    return jnp.pad(rows, ((0, SMALL_ROWS - rows.shape[0]), (0, 0)))[None]


def _unpack_small(pack, shards, axes):
    out, off = {}, 0
    nchip = pack.shape[0]
    for name in SMALL_SHARDED:
        shp = shards[name].shape
        cnt = int(np.prod(shp))
        rows = -(-cnt // LANE)
        t = pack[:, off:off + rows].reshape(nchip, -1)[:, :cnt].reshape((nchip,) + shp)
        out[name] = jnp.concatenate([t[j] for j in range(nchip)], axis=axes[name])
        off += rows
    return out


def _pack_small_grads(grads, shards, axes):
    parts = []
    for name in SMALL_SHARDED:
        shp, ax = shards[name].shape, axes[name]
        g = grads[name].reshape(shp[:ax] + (N_CHIPS, shp[ax]) + shp[ax + 1:])
        parts.append(_small_rows(jnp.moveaxis(g, ax, 0), (N_CHIPS,)))
    rows = jnp.concatenate(parts, axis=1)
    return jnp.pad(rows, ((0, 0), (0, SMALL_ROWS - rows.shape[1]), (0, 0)))


SMALL_AXES = {"meta_tokens": 1, "mix_norm_odd": 1, "pool_b": 2, "pool_scale": 1, "conv_w": 2, "ffn_conv_w": 2}


WEIGHT_NAMES = ("meta_tokens", "mix_norm_even", "w_in", "b_f", "conv_w", "conv_b", "ln_g", "ln_b", "w_out",
                "mix_norm_odd", "pool_w", "pool_b", "pool_scale", "ffn_norm", "w_up", "ffn_conv_w",
                "ffn_conv_b", "w_down", "final_norm")


def kernel(x, meta_tokens, mix_norm_even, w_in, b_f, conv_w, conv_b, ln_g, ln_b, w_out, mix_norm_odd, pool_w, pool_b, pool_scale, ffn_norm, w_up, ffn_conv_w, ffn_conv_b, w_down, final_norm, loss_target, m_meta_tokens, m_mix_norm_even, m_w_in, m_b_f, m_conv_w, m_conv_b, m_ln_g, m_ln_b, m_w_out, m_mix_norm_odd, m_pool_w, m_pool_b, m_pool_scale, m_ffn_norm, m_w_up, m_ffn_conv_w, m_ffn_conv_b, m_w_down, m_final_norm, v_meta_tokens, v_mix_norm_even, v_w_in, v_b_f, v_conv_w, v_conv_b, v_ln_g, v_ln_b, v_w_out, v_mix_norm_odd, v_pool_w, v_pool_b, v_pool_scale, v_ffn_norm, v_w_up, v_ffn_conv_w, v_ffn_conv_b, v_w_down, v_final_norm):
    given = dict(locals())
    w_loc = {n: given[n] for n in WEIGHT_NAMES}
    m_loc = {n: given["m_" + n] for n in WEIGHT_NAMES}
    v_loc = {n: given["v_" + n] for n in WEIGHT_NAMES}
    cut_of = _cuts()
    cuts = [cut_of[n] for n in COMM_ORDER]
    big = ("w_in", "w_out", "pool_w", "w_up", "w_down")
    small_shards = {n: w_loc[n] for n in SMALL_SHARDED}

    shard_of = {n: w_loc[n].astype(BF16).reshape(cut_of[n].shape(chip=True)) for n in big}
    shard_of["small"] = _pack_small_shards(small_shards)
    g_in, g_small = _gather_weights([shard_of[n] for n in GATHER_FIRST], [cut_of[n] for n in GATHER_FIRST])
    g_out = None
    g_pool = g_up = g_down = None
    full = _unpack_small(g_small, small_shards, SMALL_AXES)
    full.update({n: w_loc[n] for n in REPLICATED})
    w_in_full = g_in.transpose(1, 0, 2).reshape(D_MODEL, IN_COLS)
    qkv, f, ag = (w_in_full[:, :3 * FOX_W], w_in_full[:, 3 * FOX_W:3 * FOX_W + HEADS],
                  w_in_full[:, 3 * FOX_W + HEADS:])
    W = dict(
        mix_norm_even=full["mix_norm_even"].reshape(1, D_MODEL),
        w_in_p=jnp.concatenate([qkv, ag, f, jnp.zeros((D_MODEL, LANE - HEADS), BF16)], axis=1),
        b_f_p=jnp.pad(full["b_f"].reshape(1, HEADS), ((0, 0), (0, LANE - HEADS))),
        conv_w_p=jnp.pad(full["conv_w"].reshape(CONV_K, CONV_CH), ((0, CONV_HALO - CONV_K), (0, 0))),
        conv_b=full["conv_b"].reshape(1, CONV_CH), ln_g=full["ln_g"].reshape(1, CONV_CH),
        ln_b=full["ln_b"].reshape(1, CONV_CH), w_out=g_out,
        mix_norm_odd=full["mix_norm_odd"].reshape(1, D_MODEL), pool_w=g_pool,
        pool_b=full["pool_b"].reshape(1, D_MODEL), pool_scale=full["pool_scale"].reshape(1, D_MODEL),
        ffn_norm=full["ffn_norm"], w_up=g_up,
        ffn_conv_w_p=jnp.pad(full["ffn_conv_w"], ((0, 0), (0, 8 - FFN_K), (0, 0))),
        ffn_conv_b=full["ffn_conv_b"], w_down=g_down, final_norm=full["final_norm"].reshape(1, D_MODEL),
        late_shards=dict(pool_w=shard_of["pool_w"], w_out=shard_of["w_out"],
                         w_up0=shard_of["w_up"][0], w_up1=shard_of["w_up"][1],
                         w_down0=shard_of["w_down"][0], w_down1=shard_of["w_down"][1]))

    seq = x.shape[1]
    n_real = N_META + seq
    LP = -(-n_real // ATT_BLK) * ATT_BLK
    tail = jnp.zeros((LP - n_real, D_MODEL), F32)
    h0 = jnp.concatenate([full["meta_tokens"], x[0], tail], axis=0)
    tgt = jnp.concatenate([jnp.zeros((N_META, D_MODEL), F32), loss_target[0], tail], axis=0)
    loss_loc, dh0, G, parts, others = _local_step2(h0, tgt, W, n_real, cut_of)
    grad_x = dh0[N_META:n_real][None]
    G["meta_tokens"] = dh0[:N_META]

    rep_shapes = {n: w_loc[n].shape for n in REPLICATED}
    G["final_norm"] = G["final_norm"].reshape(D_MODEL)
    rep, loss = _unpack_replicated(_allreduce_small(_pack_replicated(G, loss_loc)), rep_shapes)

    lcuts = [cut_of[n] for n in LATE]
    lfull = [_pack_small_grads(G, small_shards, SMALL_AXES)]
    lrecv = _pair_exchange2(lfull, lcuts, "grad_pair_exchange_late")
    lparts = [_pair_sum2(f, r, cut, PAIR_SUM_BLOCKS[n], "grad_pair_sum_" + n)
              for f, r, cut, n in zip(lfull, lrecv, lcuts, LATE)]
    parts.update(zip(LATE, lparts))
    others.update(zip(LATE, _chip_exchange2(lparts, lcuts)))
    def chip_sum(n, stacked=None):
        return _chip_sum2(parts[n], others[n], cut_of[n], CHIP_SUM_BLOCKS[n], "grad_chip_sum_" + n, stacked=stacked)

    blocks = []
    for n in COMM_ORDER:
        if n in ("w_up", "w_down"):
            blocks.append(chip_sum(n + "1", stacked=(1, 2, chip_sum(n + "0", stacked=(0, 2, None)))))
        else:
            blocks.append(chip_sum(n))
    blocks = _pair_swap2(blocks, cuts)
    gsh = {n: b.reshape(w_loc[n].shape) for n, b in zip(big, blocks[:5])}
    gsh.update(_unpack_small(blocks[5], small_shards, SMALL_AXES))
    sharded = set(big) | set(SMALL_SHARDED)

    grad_w = {n: (gsh[n] if n in sharded else rep[n]) for n in WEIGHT_NAMES}
    delta, new_m, new_v = {}, {}, {}
    for n in WEIGHT_NAMES:
        delta[n], new_m[n], new_v[n] = _adamw(w_loc[n], grad_w[n], m_loc[n], v_loc[n], "adamw_" + n)
    return (loss, grad_x, *[grad_w[n] for n in WEIGHT_NAMES], *[delta[n] for n in WEIGHT_NAMES],
            *[new_m[n] for n in WEIGHT_NAMES], *[new_v[n] for n in WEIGHT_NAMES])
```

```python
import numpy as np
import jax
import jax.numpy as jnp
from jax import lax
from jax.experimental import pallas as pl
from jax.experimental.pallas import tpu as pltpu

F32 = jnp.float32
BF16 = jnp.bfloat16

D_MODEL = 1024
N_META = 16
SEQ = 2048
HEADS = 8
HEAD_DIM = 64
FOX_W = HEADS * HEAD_DIM
CONV_CH = 512
CONV_K = 31
D_FF = 2816
POOL_WINDOWS = (2, 4, 8, 16)
POOL_G = 256
RMS_EPS = 1e-6
LN_EPS = 1e-5
IN_COLS = 3 * FOX_W + HEADS + 2 * CONV_CH
IN_COLS_P = 3 * FOX_W + 2 * CONV_CH + 128
F_COL_BLK = (3 * FOX_W + 2 * CONV_CH) // 128
N_CHIPS = 4
IN_SHARD = IN_COLS // N_CHIPS
UP_SHARD = 2 * D_FF // N_CHIPS
DOWN_SHARD = D_FF // N_CHIPS

ADAM_LR = 0.001
ADAM_B1 = 0.9
ADAM_B2 = 0.999
ADAM_EPS = 1e-08
ADAM_WD = 0.01
ADAM_STEP = 10

LANE = 128
ATT_BLK = 128
VMEM_LIMIT = 56 * 1024 * 1024

NEG = -1e30


def _sigmoid(x):
    return 0.5 * jnp.tanh(0.5 * x) + 0.5


def _sigmoid_tail(x):
    return 1.0 / (1.0 + jnp.exp(-x))


def _params(sem=None):
    return pltpu.CompilerParams(dimension_semantics=sem, vmem_limit_bytes=VMEM_LIMIT)


def _sub_rows(tm):
    best = 8
    for s in range(8, 137, 8):
        if tm % s == 0:
            best = s
    return best


def _mm(a, b, mode, out_dtype, tm, tn, name, add=None, a_lead=None, b_lead=None, out=None, host=None):
    a_shape = a.shape if a_lead is None else a.shape[1:]
    b_shape = b.shape if b_lead is None else b.shape[1:]
    if mode == "nn":
        (M, K), (K2, N) = a_shape, b_shape
        dims = (((1,), (0,)), ((), ()))
        a_blk, a_idx = (tm, K), (lambda i, j: (i, 0))
        b_blk, b_idx = (K, tn), (lambda i, j: (0, j))
    elif mode == "nt":
        (M, K), (N, K2) = a_shape, b_shape
        dims = (((1,), (1,)), ((), ()))
        a_blk, a_idx = (tm, K), (lambda i, j: (i, 0))
        b_blk, b_idx = (tn, K), (lambda i, j: (j, 0))
    else:
        (K, M), (K2, N) = a_shape, b_shape
        dims = (((0,), (0,)), ((), ()))
        a_blk, a_idx = (K, tm), (lambda i, j: (0, i))
        b_blk, b_idx = (K, tn), (lambda i, j: (0, j))
    assert K == K2 and M % tm == 0 and N % tn == 0, (name, a.shape, b.shape, tm, tn)
    gm, gn = M // tm, N // tn
    a_bytes = M * K * a.dtype.itemsize
    b_bytes = N * K * b.dtype.itemsize
    m_outer = a_bytes + b_bytes * gm <= b_bytes + a_bytes * gn
    if m_outer:
        grid = (gm, gn)
        wrap = lambda f: f
    else:
        grid = (gn, gm)
        wrap = lambda f: (lambda j, i: f(i, j))

    def lead(blk, idx, at):
        if at is None:
            return pl.BlockSpec(blk, wrap(idx))
        return pl.BlockSpec((None,) + blk, wrap(lambda i, j: (at,) + idx(i, j)))

    o_idx = lambda i, j: (i, j)
    in_specs = [lead(a_blk, a_idx, a_lead), lead(b_blk, b_idx, b_lead)]
    args = [a, b]
    if add is not None:
        in_specs.append(pl.BlockSpec((tm, tn), wrap(o_idx)))
        args.append(add)
    aliases = {}
    if out is None:
        out_shape = jax.ShapeDtypeStruct((M, N), out_dtype)
        out_spec = pl.BlockSpec((tm, tn), wrap(o_idx))
    else:
        o_lead, n_lead, into = out
        out_shape = jax.ShapeDtypeStruct((n_lead, M, N), out_dtype)
        out_spec = lead((tm, tn), o_idx, o_lead)
        if into is not None:
            aliases = {len(args): 0}
            in_specs.append(pl.BlockSpec(memory_space=pl.ANY))
            args.append(into)
    has_add = add is not None
    n_host = 0 if host is None else len(host[0])
    n_in = len(args)
    scratch = []
    if n_host:
        host_shapes, host_sems, host_ops = _host_plan(host)
        in_specs = in_specs + [pl.BlockSpec(memory_space=pl.ANY)] * n_host
        args = args + list(host[0])
        out_shape = (out_shape,) + host_shapes
        out_spec = (out_spec,) + (pl.BlockSpec(memory_space=pl.ANY),) * n_host
        scratch = [pltpu.SemaphoreType.DMA((host_sems,)), pltpu.SemaphoreType.DMA((host_sems,))]

    def body(*refs):
        a_ref, b_ref = refs[0], refs[1]
        o_ref = refs[n_in + n_host]
        if n_host:
            start, wait = host_ops(refs[n_in:n_in + n_host], refs[n_in + n_host + 1:n_in + 2 * n_host + 1],
                                   refs[n_in + 2 * n_host + 1], refs[n_in + 2 * n_host + 2], host[1])
            pl.when(jnp.logical_and(pl.program_id(0) == 0, pl.program_id(1) == 0))(start)
        x = a_ref[...].astype(BF16)
        y = b_ref[...].astype(BF16)
        acc = lax.dot_general(x, y, dims, preferred_element_type=F32)
        if has_add:
            acc = acc + refs[2][...]
        o_ref[...] = acc.astype(o_ref.dtype)
        if n_host:
            pl.when(jnp.logical_and(pl.program_id(0) == grid[0] - 1, pl.program_id(1) == grid[1] - 1))(wait)

    sem = ("arbitrary", "arbitrary") if n_host else ("parallel", "parallel")
    return pl.pallas_call(
        body, name=name, out_shape=out_shape, grid=grid, in_specs=in_specs, out_specs=out_spec,
        scratch_shapes=scratch, input_output_aliases=aliases, compiler_params=_params(sem))(*args)


def _mm_ffn_dn(dup, w_up, tm, tn, name):
    _, LP, F = dup.shape
    Dm = w_up.shape[0]
    nt = (((1,), (1,)), ((), ()))

    def body(a_ref, b_ref, o_ref):
        acc = lax.dot_general(a_ref[0], b_ref[:, 0:F], nt, preferred_element_type=F32)
        acc = acc + lax.dot_general(a_ref[1], b_ref[:, F:2 * F], nt, preferred_element_type=F32)
        o_ref[...] = acc

    return pl.pallas_call(
        body, name=name, out_shape=jax.ShapeDtypeStruct((LP, Dm), F32), grid=(LP // tm, Dm // tn),
        in_specs=[pl.BlockSpec((2, tm, F), lambda i, j: (0, i, 0)),
                  pl.BlockSpec((tn, 2 * F), lambda i, j: (j, 0))],
        out_specs=pl.BlockSpec((tm, tn), lambda i, j: (i, j)),
        compiler_params=_params(("parallel", "parallel")))(dup, w_up)


def _mm_ffn_dwup(n, dup, tk, tn, name):
    LP, Dm = n.shape
    F = dup.shape[2]
    nct = F // tn
    tdims = (((0,), (0,)), ((), ()))

    def body(a_ref, b_ref, o_ref):
        o_ref[...] = lax.dot_general(a_ref[...], b_ref[...], tdims, preferred_element_type=F32)

    return pl.pallas_call(
        body, name=name, out_shape=jax.ShapeDtypeStruct((Dm, 2 * F), F32), grid=(Dm // tk, 2 * nct),
        in_specs=[pl.BlockSpec((LP, tk), lambda i, j: (0, i)),
                  pl.BlockSpec((None, LP, tn), lambda i, j: (j // nct, 0, j % nct))],
        out_specs=pl.BlockSpec((tk, tn), lambda i, j: (i, j)),
        compiler_params=_params(("parallel", "parallel")))(n, dup)


def _rms_fwd(h, g, out_dtype, tm, name):
    LP, Dm = h.shape

    def body(h_ref, g_ref, o_ref):
        x = h_ref[...]
        r = lax.rsqrt(jnp.mean(x * x, axis=1, keepdims=True) + RMS_EPS)
        o_ref[...] = (x * r * g_ref[...]).astype(o_ref.dtype)

    return pl.pallas_call(
        body, name=name, out_shape=jax.ShapeDtypeStruct((LP, Dm), out_dtype), grid=(LP // tm,),
        in_specs=[pl.BlockSpec((tm, Dm), lambda i: (i, 0)), pl.BlockSpec((1, Dm), lambda i: (0, 0))],
        out_specs=pl.BlockSpec((tm, Dm), lambda i: (i, 0)),
        compiler_params=_params(("parallel",)))(h, g)


def _rms_bwd(h, g, dn, dres, tm, name, host=None):
    LP, Dm = h.shape
    n_host = 0 if host is None else len(host[0])
    if n_host:
        host_shapes, host_sems, host_ops = _host_plan(host)
    nblk = LP // tm

    def body(*refs):
        h_ref, g_ref, dn_ref, dr_ref = refs[:4]
        dh_ref, dg_ref = refs[4 + n_host:6 + n_host]
        if n_host:
            start, wait = host_ops(refs[4:4 + n_host], refs[6 + n_host:6 + 2 * n_host],
                                   refs[6 + 2 * n_host], refs[7 + 2 * n_host], host[1])
            pl.when(pl.program_id(0) == 0)(start)
        i = pl.program_id(0)
        x = h_ref[...]
        r = lax.rsqrt(jnp.mean(x * x, axis=1, keepdims=True) + RMS_EPS)
        xhat = x * r
        dy = dn_ref[...]
        dxh = dy * g_ref[...]
        dh = r * (dxh - xhat * jnp.mean(dxh * xhat, axis=1, keepdims=True))
        dh_ref[...] = dr_ref[...] + dh

        @pl.when(i == 0)
        def _():
            dg_ref[...] = jnp.zeros_like(dg_ref)

        dg_ref[...] += jnp.sum(dy * xhat, axis=0, keepdims=True)
        if n_host:
            pl.when(pl.program_id(0) == nblk - 1)(wait)

    row = pl.BlockSpec((tm, Dm), lambda i: (i, 0))
    vec = pl.BlockSpec((1, Dm), lambda i: (0, 0))
    out_shape = (jax.ShapeDtypeStruct((LP, Dm), F32), jax.ShapeDtypeStruct((1, Dm), F32))
    out_specs = (row, vec)
    in_specs = [row, vec, row, row]
    args = [h, g, dn, dres]
    scratch = []
    if n_host:
        anyspec = pl.BlockSpec(memory_space=pl.ANY)
        in_specs += [anyspec] * n_host
        args += list(host[0])
        out_shape += host_shapes
        out_specs += (anyspec,) * n_host
        scratch = [pltpu.SemaphoreType.DMA((host_sems,)), pltpu.SemaphoreType.DMA((host_sems,))]
    return pl.pallas_call(
        body, name=name, out_shape=out_shape, grid=(nblk,), in_specs=in_specs, out_specs=out_specs,
        scratch_shapes=scratch, compiler_params=_params(("arbitrary",)))(*args)


def _loss_head(h, g, tgt, n_real, tm, name):
    LP, Dm = h.shape

    def body(h_ref, g_ref, t_ref, loss_ref, dh_ref, dg_ref):
        i = pl.program_id(0)
        x = h_ref[...]
        gg = g_ref[...]
        r = lax.rsqrt(jnp.mean(x * x, axis=1, keepdims=True) + RMS_EPS)
        xhat = x * r
        rows = i * tm + lax.broadcasted_iota(jnp.int32, (tm, 1), 0)
        real = jnp.logical_and(rows >= N_META, rows < n_real)
        diff = jnp.where(real, xhat * gg - t_ref[...], 0.0)
        dy = diff * (1.0 / Dm)
        dxh = dy * gg
        dh_ref[...] = r * (dxh - xhat * jnp.mean(dxh * xhat, axis=1, keepdims=True))

        @pl.when(i == 0)
        def _():
            dg_ref[...] = jnp.zeros_like(dg_ref)
            loss_ref[...] = jnp.zeros_like(loss_ref)

        dg_ref[...] += jnp.sum(dy * xhat, axis=0, keepdims=True)
        part = jnp.sum(jnp.sum(diff * diff, axis=1, keepdims=True), axis=0, keepdims=True)
        loss_ref[...] += jnp.broadcast_to(part * (0.5 / Dm), loss_ref.shape)

    row = pl.BlockSpec((tm, Dm), lambda i: (i, 0))
    vec = pl.BlockSpec((1, Dm), lambda i: (0, 0))
    return pl.pallas_call(
        body, name=name,
        out_shape=(jax.ShapeDtypeStruct((1, LANE), F32), jax.ShapeDtypeStruct((LP, Dm), F32),
                   jax.ShapeDtypeStruct((1, Dm), F32)),
        grid=(LP // tm,), in_specs=[row, vec, row],
        out_specs=(pl.BlockSpec((1, LANE), lambda i: (0, 0)), row, vec),
        compiler_params=_params(("arbitrary",)))(h, g, tgt)


def _fgate_fwd(proj, bf_p, name):
    LP = proj.shape[0]
    nb = LP // LANE

    def body(f_ref, b_ref, c_ref, lf_ref):
        x = f_ref[...] + b_ref[...]
        lf_ref[...] = jnp.minimum(x, 0.0) - jnp.log1p(jnp.exp(-jnp.abs(x)))
        ri = lax.broadcasted_iota(jnp.int32, (LANE, LANE), 0)
        ci = lax.broadcasted_iota(jnp.int32, (LANE, LANE), 1)
        tri = jnp.where(ri >= ci, 1.0, 0.0).astype(F32)

        def blk(i, carry):
            rows = pl.ds(pl.multiple_of(i * LANE, LANE), LANE)
            cb = jnp.dot(tri, lf_ref[rows, :], precision=lax.Precision.HIGHEST,
                         preferred_element_type=F32) + carry
            c_ref[rows, :] = cb
            return cb[LANE - 1:LANE, :]

        lax.fori_loop(0, nb, blk, jnp.zeros((1, LANE), F32))

    return pl.pallas_call(
        body, name=name, out_shape=jax.ShapeDtypeStruct((LP, LANE), F32), grid=(1,),
        in_specs=[pl.BlockSpec((LP, LANE), lambda i: (0, F_COL_BLK)),
                  pl.BlockSpec((1, LANE), lambda i: (0, 0))],
        out_specs=pl.BlockSpec((LP, LANE), lambda i: (0, 0)),
        scratch_shapes=[pltpu.VMEM((LP, LANE), F32)],
        compiler_params=_params(("arbitrary",)))(proj, bf_p)


def _fgate_bwd(proj, bf_p, dc, name):
    LP = proj.shape[0]
    nb = LP // LANE

    def body(f_ref, b_ref, dc_ref, dl_ref, db_ref):
        ri = lax.broadcasted_iota(jnp.int32, (LANE, LANE), 0)
        ci = lax.broadcasted_iota(jnp.int32, (LANE, LANE), 1)
        triu = jnp.where(ri <= ci, 1.0, 0.0).astype(F32)
        bb = b_ref[...]

        tail = jnp.zeros((1, LANE), F32)
        dbs = jnp.zeros((1, LANE), F32)
        for i in range(nb - 1, -1, -1):
            rows = slice(i * LANE, (i + 1) * LANE)
            gb = jnp.dot(triu, dc_ref[rows, :], precision=lax.Precision.HIGHEST,
                         preferred_element_type=F32) + tail
            x = f_ref[rows, :] + bb
            dl = gb * _sigmoid_tail(-x)
            dl_ref[rows, :] = dl.astype(dl_ref.dtype)
            tail = gb[0:1, :]
            dbs = dbs + jnp.sum(dl, axis=0, keepdims=True)
        db_ref[...] = dbs

    return pl.pallas_call(
        body, name=name,
        out_shape=(jax.ShapeDtypeStruct((LP, LANE), BF16), jax.ShapeDtypeStruct((1, LANE), F32)),
        grid=(1,),
        in_specs=[pl.BlockSpec((LP, LANE), lambda i: (0, F_COL_BLK)),
                  pl.BlockSpec((1, LANE), lambda i: (0, 0)),
                  pl.BlockSpec((LP, LANE), lambda i: (0, 0))],
        out_specs=(pl.BlockSpec((LP, LANE), lambda i: (0, 0)), pl.BlockSpec((1, LANE), lambda i: (0, 0))),
        compiler_params=_params(("arbitrary",)))(proj, bf_p, dc)


AUG = 128
ONES_IN_K = HEAD_DIM
ONES_IN_Q = HEAD_DIM + 3
ATT_HEADS_PER_STEP = 8
ATT_HEADS_PER_STEP_BWD = 8


def _attn_prep(proj, c, name):
    LP = proj.shape[0]
    nb = LP // ATT_BLK
    tail_rows = AUG - HEAD_DIM

    def body(q_ref, k_ref, v_ref, c_ref, qT_ref, kT_ref, ka_ref, vT_ref):
        qt = (q_ref[...] * (HEAD_DIM ** -0.5)).T
        kt = k_ref[...].T
        vt = v_ref[...].T
        ct = c_ref[...].T
        hi = ct.astype(BF16).astype(F32)
        r1 = ct - hi
        mid = r1.astype(BF16).astype(F32)
        lo = (r1 - mid).astype(BF16).astype(F32)
        row = lax.broadcasted_iota(jnp.int32, (tail_rows, ATT_BLK), 0)
        ones = jnp.where(row < 3, 1.0, 0.0)
        for h in range(HEADS):
            cparts = jnp.where(row == 0, hi[h:h + 1], jnp.where(row == 1, mid[h:h + 1],
                               jnp.where(row == 2, lo[h:h + 1], 0.0)))
            hs = slice(h * HEAD_DIM, (h + 1) * HEAD_DIM)
            q_tail = cparts + pltpu.roll(ones, 3, 0)
            k_tail = ones - pltpu.roll(cparts, 3, 0)
            qT_ref[h] = jnp.concatenate([qt[hs], q_tail], axis=0).astype(BF16)
            kfull = jnp.concatenate([kt[hs], k_tail], axis=0)
            kT_ref[h] = kfull.astype(BF16)
            ka_ref[h] = kfull.T.astype(BF16)
            vT_ref[h] = vt[hs].astype(BF16)

    col = lambda j: pl.BlockSpec((ATT_BLK, FOX_W), lambda i: (i, j))
    blk = lambda r: pl.BlockSpec((HEADS, None, r, ATT_BLK), lambda i: (0, i, 0, 0))
    return pl.pallas_call(
        body, name=name,
        out_shape=(jax.ShapeDtypeStruct((HEADS, nb, AUG, ATT_BLK), BF16),
                   jax.ShapeDtypeStruct((HEADS, nb, AUG, ATT_BLK), BF16),
                   jax.ShapeDtypeStruct((HEADS, LP, AUG), BF16),
                   jax.ShapeDtypeStruct((HEADS, nb, HEAD_DIM, ATT_BLK), BF16)),
        grid=(nb,), in_specs=[col(0), col(1), col(2), pl.BlockSpec((ATT_BLK, LANE), lambda i: (i, 0))],
        out_specs=(blk(AUG), blk(AUG), pl.BlockSpec((HEADS, ATT_BLK, AUG), lambda i: (0, i, 0)), blk(HEAD_DIM)),
        compiler_params=_params(("parallel",)))(proj, proj, proj, c)


def _attn_rows(xT, scale, out_dtype, name, total_cols=None):
    Hh, nb, R, _ = xT.shape

    def body(x_ref, o_ref):
        stack = jnp.concatenate([x_ref[h, 0:HEAD_DIM, :] for h in range(Hh)], axis=0)
        o_ref[...] = (stack * scale).T.astype(o_ref.dtype)

    return pl.pallas_call(
        body, name=name, out_shape=jax.ShapeDtypeStruct((nb * ATT_BLK, total_cols or Hh * HEAD_DIM), out_dtype),
        grid=(nb,),
        in_specs=[pl.BlockSpec((Hh, None, R, ATT_BLK), lambda i: (0, i, 0, 0))],
        out_specs=pl.BlockSpec((ATT_BLK, Hh * HEAD_DIM), lambda i: (i, 0)),
        compiler_params=_params(("parallel",)))(xT)


def _attn_cols(x, name):
    LP = x.shape[0]
    nb = LP // ATT_BLK

    def body(x_ref, o_ref):
        xt = x_ref[...].T
        for h in range(HEADS):
            o_ref[h] = xt[h * HEAD_DIM:(h + 1) * HEAD_DIM].astype(o_ref.dtype)

    return pl.pallas_call(
        body, name=name, out_shape=jax.ShapeDtypeStruct((HEADS, nb, HEAD_DIM, ATT_BLK), BF16), grid=(nb,),
        in_specs=[pl.BlockSpec((ATT_BLK, FOX_W), lambda i: (i, 0))],
        out_specs=pl.BlockSpec((HEADS, None, HEAD_DIM, ATT_BLK), lambda i: (0, i, 0, 0)),
        compiler_params=_params(("parallel",)))(x)


def _attn_grads_rows(dqT, dkT, dvT, name):
    Hh, nb, _, _ = dqT.shape
    W3 = Hh * HEAD_DIM

    def body(q_ref, k_ref, v_ref, o_ref, dc_ref):
        for col, (x_ref, scale) in enumerate(((q_ref, HEAD_DIM ** -0.5), (k_ref, 1.0), (v_ref, 1.0))):
            stack = jnp.concatenate([x_ref[h, 0:HEAD_DIM, :] for h in range(Hh)], axis=0)
            o_ref[:, col * W3:(col + 1) * W3] = (stack * scale).T.astype(o_ref.dtype)
        row = lax.broadcasted_iota(jnp.int32, (LANE, ATT_BLK), 0)
        acc = jnp.zeros((LANE, ATT_BLK), F32)
        for h in range(Hh):
            d = q_ref[h, ONES_IN_K:ONES_IN_K + 1, :] - k_ref[h, ONES_IN_Q:ONES_IN_Q + 1, :]
            acc = jnp.where(row == h, d, acc)
        dc_ref[...] = acc.T

    spec = lambda r: pl.BlockSpec((Hh, None, r, ATT_BLK), lambda i: (0, i, 0, 0))
    return pl.pallas_call(
        body, name=name,
        out_shape=(jax.ShapeDtypeStruct((nb * ATT_BLK, 3 * W3), BF16), jax.ShapeDtypeStruct((nb * ATT_BLK, LANE), F32)),
        grid=(nb,), in_specs=[spec(AUG), spec(AUG), spec(HEAD_DIM)],
        out_specs=(pl.BlockSpec((ATT_BLK, 3 * W3), lambda i: (i, 0)), pl.BlockSpec((ATT_BLK, LANE), lambda i: (i, 0))),
        compiler_params=_params(("parallel",)))(dqT, dkT, dvT)


def _attn_fwd2(qT, k_aug, vT, name, comm=None):
    Hh, nb, _, _ = qT.shape
    LP = nb * ATT_BLK
    Dh = vT.shape[2]
    HB = ATT_HEADS_PER_STEP
    n_comm = 0 if comm is None else len(comm[0])

    def body(*refs):
        q_ref, k_ref, v_ref = refs[:3]
        o_ref, lse_ref = refs[3 + n_comm:5 + n_comm]
        if n_comm:
            start, wait = _gather_first_ops(refs[3:3 + n_comm], refs[5 + n_comm:5 + 2 * n_comm],
                                            refs[5 + 2 * n_comm], refs[6 + 2 * n_comm], comm[1])
            pl.when(pl.program_id(0) == 0)(start)
        keys = lax.broadcasted_iota(jnp.int32, (ATT_BLK, ATT_BLK), 0)
        qrys = lax.broadcasted_iota(jnp.int32, (ATT_BLK, ATT_BLK), 1)
        causal = keys <= qrys

        def q_block(i, _):
            def tile(j, carry, masked):
                ks = pl.ds(pl.multiple_of(j * ATT_BLK, ATT_BLK), ATT_BLK)
                s_all = [jnp.dot(k_ref[hh, ks, :], q_ref[hh, i], preferred_element_type=F32) for hh in range(HB)]
                stats, p_all = [], []
                for hh in range(HB):
                    m, l, _ = carry[hh]
                    s = jnp.where(causal, s_all[hh], NEG) if masked else s_all[hh]
                    m_new = jnp.maximum(m, jnp.max(s, axis=0, keepdims=True))
                    p = jnp.exp(s - m_new)
                    alpha = jnp.exp(m - m_new)
                    stats.append((m_new, alpha * l + jnp.sum(p, axis=0, keepdims=True), alpha))
                    p_all.append(p.astype(BF16))
                out = []
                for hh in range(HB):
                    m_new, l, alpha = stats[hh]
                    acc = alpha * carry[hh][2] + jnp.dot(v_ref[hh, j], p_all[hh], preferred_element_type=F32)
                    out.append((m_new, l, acc))
                return tuple(out)

            init = tuple((jnp.full((1, ATT_BLK), NEG, F32), jnp.zeros((1, ATT_BLK), F32),
                          jnp.zeros((Dh, ATT_BLK), F32)) for _ in range(HB))
            carry = lax.fori_loop(0, i, lambda j, cr: tile(j, cr, False), init)
            carry = tile(i, carry, True)
            for hh in range(HB):
                m, l, acc = carry[hh]
                o_ref[hh, i] = acc / l
                lse_ref[hh, i] = m + jnp.log(l)
            return 0

        lax.fori_loop(0, nb, q_block, 0)
        if n_comm:
            pl.when(pl.program_id(0) == Hh // HB - 1)(wait)

    blk = lambda r: pl.BlockSpec((HB, nb, r, ATT_BLK), lambda h: (h, 0, 0, 0))
    out_shape = (jax.ShapeDtypeStruct((Hh, nb, Dh, ATT_BLK), F32), jax.ShapeDtypeStruct((Hh, nb, 1, ATT_BLK), F32))
    scratch = []
    args = [qT, k_aug, vT]
    if n_comm:
        out_shape += tuple(jax.ShapeDtypeStruct(cut.full, s.dtype) for cut, s in zip(comm[1], comm[0]))
        scratch = [pltpu.SemaphoreType.DMA((4 * n_comm,)), pltpu.SemaphoreType.DMA((4 * n_comm,))]
        args += list(comm[0])
    return pl.pallas_call(
        body, name=name, out_shape=out_shape, grid=(Hh // HB,),
        in_specs=[blk(AUG), pl.BlockSpec((HB, LP, AUG), lambda h: (h, 0, 0)), blk(Dh)] + [ANY] * n_comm,
        out_specs=(blk(Dh), blk(1)) + (ANY,) * n_comm, scratch_shapes=scratch,
        compiler_params=_params(("arbitrary",)))(*args)


def _attn_bwd2(qT, kT, k_aug, v, oT, doT, lse, name, comm=None):
    Hh, nb, _, _ = qT.shape
    LP = nb * ATT_BLK
    Dh = v.shape[2]
    nt = (((1,), (1,)), ((), ()))
    tn = (((0,), (0,)), ((), ()))

    HB = ATT_HEADS_PER_STEP_BWD
    n_comm = 0 if comm is None else len(comm[0])

    def body(*refs):
        q_ref, kt_ref, k_ref, v_ref, o_ref, do_ref, lse_ref = refs[:7]
        parts = refs[7:7 + n_comm]
        dq_ref, dk_ref, dv_ref = refs[7 + n_comm:10 + n_comm]
        others = refs[10 + n_comm:10 + 2 * n_comm]
        delta_ref = refs[10 + 2 * n_comm]
        if n_comm:
            start, wait = _chip_exchange_ops(parts, others, refs[11 + 2 * n_comm], refs[12 + 2 * n_comm], comm[1])
            pl.when(pl.program_id(0) == 0)(start)
        keys = lax.broadcasted_iota(jnp.int32, (ATT_BLK, ATT_BLK), 0)
        qrys = lax.broadcasted_iota(jnp.int32, (ATT_BLK, ATT_BLK), 1)
        causal = keys <= qrys

        def prep(i, _):
            for hh in range(HB):
                delta_ref[hh, i] = jnp.sum(do_ref[hh, i].astype(F32) * o_ref[hh, i], axis=0, keepdims=True)
                dq_ref[hh, i] = jnp.zeros((AUG, ATT_BLK), F32)
            return 0

        lax.fori_loop(0, nb, prep, 0)

        def kv_block(j, _):
            ks = pl.ds(pl.multiple_of(j * ATT_BLK, ATT_BLK), ATT_BLK)

            def tile(i, carry, masked):
                s_all = [jnp.dot(k_ref[hh, ks, :], q_ref[hh, i], preferred_element_type=F32) for hh in range(HB)]
                dp_all = [lax.dot_general(v_ref[hh, j], do_ref[hh, i], tn, preferred_element_type=F32)
                          for hh in range(HB)]
                p_all, ds_all = [], []
                for hh in range(HB):
                    s = jnp.where(causal, s_all[hh], NEG) if masked else s_all[hh]
                    p = jnp.exp(s - lse_ref[hh, i])
                    ds_all.append((p * (dp_all[hh] - delta_ref[hh, i])).astype(BF16))
                    p_all.append(p.astype(BF16))
                out = []
                for hh in range(HB):
                    dk, dv = carry[hh]
                    dv = dv + lax.dot_general(do_ref[hh, i], p_all[hh], nt, preferred_element_type=F32)
                    dk = dk + lax.dot_general(q_ref[hh, i], ds_all[hh], nt, preferred_element_type=F32)
                    out.append((dk, dv))
                dq_new = [jnp.dot(kt_ref[hh, j], ds_all[hh], preferred_element_type=F32) for hh in range(HB)]
                for hh in range(HB):
                    dq_ref[hh, i] += dq_new[hh]
                return tuple(out)

            init = tuple((jnp.zeros((AUG, ATT_BLK), F32), jnp.zeros((Dh, ATT_BLK), F32)) for _ in range(HB))
            carry = tile(j, init, True)
            carry = lax.fori_loop(j + 1, nb, lambda i, cr: tile(i, cr, False), carry)
            for hh in range(HB):
                dk_ref[hh, j] = carry[hh][0]
                dv_ref[hh, j] = carry[hh][1]
            return 0

        lax.fori_loop(0, nb, kv_block, 0)
        if n_comm:
            pl.when(pl.program_id(0) == Hh // HB - 1)(wait)

    blk = lambda r: pl.BlockSpec((HB, nb, r, ATT_BLK), lambda h: (h, 0, 0, 0))
    row = lambda cols: pl.BlockSpec((HB, LP, cols), lambda h: (h, 0, 0))
    out_shape = (jax.ShapeDtypeStruct((Hh, nb, AUG, ATT_BLK), F32), jax.ShapeDtypeStruct((Hh, nb, AUG, ATT_BLK), F32),
                 jax.ShapeDtypeStruct((Hh, nb, Dh, ATT_BLK), F32))
    scratch = [pltpu.VMEM((HB, nb, 1, ATT_BLK), F32)]
    args = [qT, kT, k_aug, v, oT, doT, lse]
    if n_comm:
        out_shape += _chip_exchange_shapes(*comm)
        scratch += [pltpu.SemaphoreType.DMA((3 * n_comm,)), pltpu.SemaphoreType.DMA((3 * n_comm,))]
        args += list(comm[0])
    return pl.pallas_call(
        body, name=name, out_shape=out_shape, grid=(Hh // HB,),
        in_specs=[blk(AUG), blk(AUG), row(AUG), blk(Dh), blk(Dh), blk(Dh), blk(1)] + [ANY] * n_comm,
        out_specs=(blk(AUG), blk(AUG), blk(Dh)) + (ANY,) * n_comm,
        scratch_shapes=scratch,
        compiler_params=_params(("arbitrary",)))(*args)


CONV_HALO = 32
A_BLK = 3 * FOX_W // CONV_CH
G_BLK = A_BLK + 1


def _conf_fwd(proj, cw, cb, lg, lb, tm, name, cat_into=None):
    LP = proj.shape[0]
    C = CONV_CH
    sub = _sub_rows(tm)
    hpb = tm // CONV_HALO

    def body(a_ref, g_ref, ah_ref, gh_ref, w_ref, cb_ref, lg_ref, lb_ref, *rest):
        u1_ref, u_ref, buf = rest[-3:]
        r = pl.program_id(0)
        buf[CONV_HALO:CONV_HALO + tm, :] = a_ref[...] * _sigmoid(g_ref[...])
        buf[0:CONV_HALO, :] = jnp.where(r > 0, ah_ref[...] * _sigmoid(gh_ref[...]), 0.0)
        for s in range(tm // sub):
            for ct in range(C // LANE):
                ln = slice(ct * LANE, (ct + 1) * LANE)
                acc = jnp.broadcast_to(cb_ref[:, ln], (sub, LANE))
                for kk in range(CONV_K):
                    off = CONV_HALO + s * sub - (CONV_K - 1) + kk
                    acc = acc + w_ref[kk:kk + 1, ln] * buf[off:off + sub, ln]
                u1_ref[s * sub:(s + 1) * sub, ln] = acc
        u1 = u1_ref[...]
        mu = jnp.mean(u1, axis=1, keepdims=True)
        xc = u1 - mu
        var = jnp.mean(xc * xc, axis=1, keepdims=True)
        y = xc * lax.rsqrt(var + LN_EPS) * lg_ref[...] + lb_ref[...]
        u_ref[...] = (y * _sigmoid(y)).astype(u_ref.dtype)

    cur = lambda blk: pl.BlockSpec((tm, C), lambda r: (r, blk))
    halo = lambda blk: pl.BlockSpec((CONV_HALO, C), lambda r: (jnp.maximum(r * hpb - 1, 0), blk))
    vec = pl.BlockSpec((1, C), lambda r: (0, 0))
    out = pl.BlockSpec((tm, C), lambda r: (r, 0))
    in_specs = [cur(A_BLK), cur(G_BLK), halo(A_BLK), halo(G_BLK),
                pl.BlockSpec((CONV_HALO, C), lambda r: (0, 0)), vec, vec, vec]
    args = [proj, proj, proj, proj, cw, cb, lg, lb]
    u_shape, u_spec, aliases = jax.ShapeDtypeStruct((LP, C), BF16), out, {}
    if cat_into is not None:
        in_specs.append(pl.BlockSpec(memory_space=pl.ANY))
        args.append(cat_into)
        u_shape, u_spec, aliases = jax.ShapeDtypeStruct((LP, 2 * C), BF16), cur(1), {8: 1}
    return pl.pallas_call(
        body, name=name, out_shape=(jax.ShapeDtypeStruct((LP, C), F32), u_shape), grid=(LP // tm,),
        in_specs=in_specs, out_specs=(out, u_spec), input_output_aliases=aliases,
        scratch_shapes=[pltpu.VMEM((CONV_HALO + tm, C), F32)],
        compiler_params=_params(("parallel",)))(*args)


def _conf_bwd(proj, u1, dcat, cw, lg, lb, tm, name, host=None):
    LP = proj.shape[0]
    C = CONV_CH
    sub = _sub_rows(tm)
    hpb = tm // CONV_HALO
    nblk = LP // tm
    last_halo = LP // CONV_HALO - 1

    n_host = 0 if host is None else len(host[0])
    if n_host:
        host_shapes, host_sems, host_ops = _host_plan(host)

    def body(*refs):
        a_ref, g_ref, ah_ref, gh_ref, u1_ref, u1n_ref, du_ref, dun_ref, w_ref, lg_ref, lb_ref = refs[:11]
        dadg_ref, dw_ref, dcb_ref, dlg_ref, dlb_ref = refs[11 + n_host:16 + n_host]
        ubuf, dbuf, du0 = refs[16 + 2 * n_host:19 + 2 * n_host]
        if n_host:
            start, wait = host_ops(refs[11:11 + n_host], refs[16 + n_host:16 + 2 * n_host],
                                   refs[19 + 2 * n_host], refs[20 + 2 * n_host], host[1])
            pl.when(pl.program_id(0) == 0)(start)
        r = pl.program_id(0)
        lgv = lg_ref[...]
        lbv = lb_ref[...]

        def ln_silu_bwd(u1v, duv):
            mu = jnp.mean(u1v, axis=1, keepdims=True)
            xc = u1v - mu
            rstd = lax.rsqrt(jnp.mean(xc * xc, axis=1, keepdims=True) + LN_EPS)
            xhat = xc * rstd
            y = xhat * lgv + lbv
            sg = _sigmoid(y)
            dy = duv * (sg * (1.0 + y * (1.0 - sg)))
            dxh = dy * lgv
            du1 = rstd * (dxh - jnp.mean(dxh, axis=1, keepdims=True)
                          - xhat * jnp.mean(dxh * xhat, axis=1, keepdims=True))
            return du1, dy, xhat

        @pl.when(r == 0)
        def _():
            dw_ref[...] = jnp.zeros_like(dw_ref)
            dcb_ref[...] = jnp.zeros_like(dcb_ref)
            dlg_ref[...] = jnp.zeros_like(dlg_ref)
            dlb_ref[...] = jnp.zeros_like(dlb_ref)

        du1, dy, xhat = ln_silu_bwd(u1_ref[...], du_ref[...])
        dlg_ref[...] += jnp.sum(dy * xhat, axis=0, keepdims=True)
        dlb_ref[...] += jnp.sum(dy, axis=0, keepdims=True)
        dcb_ref[...] += jnp.sum(du1, axis=0, keepdims=True)
        dbuf[0:tm, :] = du1
        du1n, _, _ = ln_silu_bwd(u1n_ref[...], dun_ref[...])
        dbuf[tm:tm + CONV_HALO, :] = jnp.where(r < nblk - 1, du1n, 0.0)
        ubuf[CONV_HALO:CONV_HALO + tm, :] = a_ref[...] * _sigmoid(g_ref[...])
        ubuf[0:CONV_HALO, :] = jnp.where(r > 0, ah_ref[...] * _sigmoid(gh_ref[...]), 0.0)

        for ct in range(C // LANE):
            ln = slice(ct * LANE, (ct + 1) * LANE)
            for s in range(tm // sub):
                d_here = dbuf[s * sub:(s + 1) * sub, ln]
                acc = jnp.zeros((sub, LANE), F32)
                for kk in range(CONV_K):
                    fo = s * sub + (CONV_K - 1) - kk
                    acc = acc + w_ref[kk:kk + 1, ln] * dbuf[fo:fo + sub, ln]
                    bo = CONV_HALO + s * sub - (CONV_K - 1) + kk
                    dw_ref[kk:kk + 1, ln] += jnp.sum(d_here * ubuf[bo:bo + sub, ln], axis=0, keepdims=True)
                du0[s * sub:(s + 1) * sub, ln] = acc
        a = a_ref[...]
        sg = _sigmoid(g_ref[...])
        d0 = du0[...]
        dadg_ref[:, 0:C] = (d0 * sg).astype(dadg_ref.dtype)
        dadg_ref[:, C:2 * C] = (d0 * a * sg * (1.0 - sg)).astype(dadg_ref.dtype)
        if n_host:
            pl.when(pl.program_id(0) == nblk - 1)(wait)

    cur = lambda blk: pl.BlockSpec((tm, C), lambda r: (r, blk))
    prev = lambda blk: pl.BlockSpec((CONV_HALO, C), lambda r: (jnp.maximum(r * hpb - 1, 0), blk))
    nxt = lambda blk: pl.BlockSpec((CONV_HALO, C), lambda r: (jnp.minimum((r + 1) * hpb, last_halo), blk))
    vec = pl.BlockSpec((1, C), lambda r: (0, 0))
    wspec = pl.BlockSpec((CONV_HALO, C), lambda r: (0, 0))
    out_shape = (jax.ShapeDtypeStruct((LP, 2 * C), BF16), jax.ShapeDtypeStruct((CONV_HALO, C), F32),
                 jax.ShapeDtypeStruct((1, C), F32), jax.ShapeDtypeStruct((1, C), F32),
                 jax.ShapeDtypeStruct((1, C), F32))
    out_specs = (pl.BlockSpec((tm, 2 * C), lambda r: (r, 0)), wspec, vec, vec, vec)
    in_specs = [cur(A_BLK), cur(G_BLK), prev(A_BLK), prev(G_BLK), cur(0), nxt(0), cur(1), nxt(1), wspec, vec, vec]
    args = [proj, proj, proj, proj, u1, u1, dcat, dcat, cw, lg, lb]
    scratch = [pltpu.VMEM((CONV_HALO + tm, C), F32), pltpu.VMEM((tm + CONV_HALO, C), F32),
               pltpu.VMEM((tm, C), F32)]
    if n_host:
        anyspec = pl.BlockSpec(memory_space=pl.ANY)
        in_specs += [anyspec] * n_host
        args += list(host[0])
        out_shape += host_shapes
        out_specs += (anyspec,) * n_host
        scratch += [pltpu.SemaphoreType.DMA((host_sems,)), pltpu.SemaphoreType.DMA((host_sems,))]
    return pl.pallas_call(
        body, name=name, out_shape=out_shape, grid=(nblk,), in_specs=in_specs, out_specs=out_specs,
        scratch_shapes=scratch, compiler_params=_params(("arbitrary",)))(*args)


FFN_HALO = 8
FFN_TC = 1408
FFN_ROW_SPLIT = 1
FFN_K = 3


def _ffn_conv(buf, w_ref, b_ref, s, sub, ln):
    acc = jnp.broadcast_to(b_ref[:, ln], (sub, LANE))
    for kk in range(FFN_K):
        off = FFN_HALO + s * sub - (FFN_K - 1) + kk
        acc = acc + w_ref[kk:kk + 1, ln] * buf[off:off + sub, ln]
    return acc


def _ffn_act_fwd(up, w, b, tm, name, host=None):
    LP, F = up.shape[0], up.shape[1] // 2
    tm = tm // FFN_ROW_SPLIT
    upg = upv = up
    nct = F // FFN_TC
    sub = _sub_rows(tm)
    hpb = tm // FFN_HALO
    n_host = 0 if host is None else len(host[0])
    nrb = LP // tm

    def body(*refs):
        g_ref, v_ref, gh_ref, vh_ref, wg_ref, wv_ref, bg_ref, bv_ref = refs[:8]
        act_ref = refs[8 + n_host]
        gbuf, vbuf = refs[9 + 2 * n_host:11 + 2 * n_host]
        if n_host:
            start, wait = _gather_first_ops(refs[8:8 + n_host], refs[9 + n_host:9 + 2 * n_host],
                                            refs[11 + 2 * n_host], refs[12 + 2 * n_host], host[1])
            pl.when(jnp.logical_and(pl.program_id(0) == 0, pl.program_id(1) == 0))(start)
        r = pl.program_id(1)
        gbuf[FFN_HALO:FFN_HALO + tm, :] = g_ref[...]
        vbuf[FFN_HALO:FFN_HALO + tm, :] = v_ref[...]
        gbuf[0:FFN_HALO, :] = jnp.where(r > 0, gh_ref[...], 0.0)
        vbuf[0:FFN_HALO, :] = jnp.where(r > 0, vh_ref[...], 0.0)
        for s in range(tm // sub):
            for ct in range(FFN_TC // LANE):
                ln = slice(ct * LANE, (ct + 1) * LANE)
                gc = _ffn_conv(gbuf, wg_ref, bg_ref, s, sub, ln)
                vc = _ffn_conv(vbuf, wv_ref, bv_ref, s, sub, ln)
                act_ref[s * sub:(s + 1) * sub, ln] = (gc * _sigmoid(gc) * vc).astype(act_ref.dtype)
        if n_host:
            pl.when(jnp.logical_and(pl.program_id(0) == nct - 1, pl.program_id(1) == nrb - 1))(wait)

    cur = pl.BlockSpec((tm, FFN_TC), lambda c, r: (r, c))
    halo = pl.BlockSpec((FFN_HALO, FFN_TC), lambda c, r: (jnp.maximum(r * hpb - 1, 0), c))
    wg = pl.BlockSpec((8, FFN_TC), lambda c, r: (0, c))
    wv = pl.BlockSpec((8, FFN_TC), lambda c, r: (0, nct + c))
    bg = pl.BlockSpec((1, FFN_TC), lambda c, r: (0, c))
    bv = pl.BlockSpec((1, FFN_TC), lambda c, r: (0, nct + c))
    curv = pl.BlockSpec((tm, FFN_TC), lambda c, r: (r, nct + c))
    halov = pl.BlockSpec((FFN_HALO, FFN_TC), lambda c, r: (jnp.maximum(r * hpb - 1, 0), nct + c))
    out_shape = jax.ShapeDtypeStruct((LP, F), BF16)
    out_specs = cur
    in_specs = [cur, curv, halo, halov, wg, wv, bg, bv]
    args = [upg, upv, upg, upv, w, w, b, b]
    scratch = [pltpu.VMEM((FFN_HALO + tm, FFN_TC), F32)] * 2
    if n_host:
        anyspec = pl.BlockSpec(memory_space=pl.ANY)
        in_specs += [anyspec] * n_host
        args += list(host[0])
        out_shape = (out_shape,) + tuple(jax.ShapeDtypeStruct(cut.full, s.dtype) for cut, s in zip(host[1], host[0]))
        out_specs = (cur,) + (anyspec,) * n_host
        scratch += [pltpu.SemaphoreType.DMA((4 * n_host,)), pltpu.SemaphoreType.DMA((4 * n_host,))]
    sem = ("arbitrary", "arbitrary") if n_host else ("parallel", "parallel")
    return pl.pallas_call(
        body, name=name, out_shape=out_shape, grid=(nct, nrb), in_specs=in_specs, out_specs=out_specs,
        scratch_shapes=scratch, compiler_params=_params(sem))(*args)


def _ffn_act_bwd(up, dact, w, b, tm, name, comm=None):
    LP, F = up.shape[0], up.shape[1] // 2
    tm = tm // FFN_ROW_SPLIT
    upg = upv = up
    nct = F // FFN_TC
    sub = _sub_rows(tm)
    hpb = tm // FFN_HALO
    nblk = LP // tm
    last_halo = LP // FFN_HALO - 1
    TB = tm + 2 * FFN_HALO
    n_comm = 0 if comm is None else len(comm[0])

    def body(*refs):
        (g_ref, v_ref, gp_ref, vp_ref, gn_ref, vn_ref, da_ref, dan_ref,
         wg_ref, wv_ref, bg_ref, bv_ref) = refs[:12]
        dup_ref, dwg_ref, dwv_ref, dbg_ref, dbv_ref = refs[12 + n_comm:17 + n_comm]
        gbuf, vbuf, dgb, dvb = refs[17 + 2 * n_comm:21 + 2 * n_comm]
        if n_comm:
            start, wait = _chip_exchange_ops(refs[12:12 + n_comm], refs[17 + n_comm:17 + 2 * n_comm],
                                             refs[21 + 2 * n_comm], refs[22 + 2 * n_comm], comm[1])
            pl.when(jnp.logical_and(pl.program_id(0) == 0, pl.program_id(1) == 0))(start)
        r = pl.program_id(1)
        dg_ref = dup_ref.at[0]
        dv_ref = dup_ref.at[1]
        first = r == 0
        last = r == nblk - 1

        @pl.when(first)
        def _():
            dwg_ref[...] = jnp.zeros_like(dwg_ref)
            dwv_ref[...] = jnp.zeros_like(dwv_ref)
            dbg_ref[...] = jnp.zeros_like(dbg_ref)
            dbv_ref[...] = jnp.zeros_like(dbv_ref)

        for buf, c_ref, p_ref, n_ref in ((gbuf, g_ref, gp_ref, gn_ref), (vbuf, v_ref, vp_ref, vn_ref)):
            buf[0:FFN_HALO, :] = jnp.where(first, 0.0, p_ref[...])
            buf[FFN_HALO:FFN_HALO + tm, :] = c_ref[...]
            buf[FFN_HALO + tm:TB, :] = jnp.where(last, 0.0, n_ref[...])

        def dconv(s0, nrows, ln, dact_v):
            xg = [gbuf[s0 - (FFN_K - 1) + kk:s0 - (FFN_K - 1) + kk + nrows, ln] for kk in range(FFN_K)]
            xv = [vbuf[s0 - (FFN_K - 1) + kk:s0 - (FFN_K - 1) + kk + nrows, ln] for kk in range(FFN_K)]
            gc = jnp.broadcast_to(bg_ref[:, ln], (nrows, LANE))
            vc = jnp.broadcast_to(bv_ref[:, ln], (nrows, LANE))
            for kk in range(FFN_K):
                gc = gc + wg_ref[kk:kk + 1, ln] * xg[kk]
                vc = vc + wv_ref[kk:kk + 1, ln] * xv[kk]
            sg = _sigmoid(gc)
            return dact_v * vc * (sg * (1.0 + gc * (1.0 - sg))), dact_v * (gc * sg), xg, xv

        colsum = lambda t: jnp.sum(t, axis=0, keepdims=True)
        for ct in range(FFN_TC // LANE):
            ln = slice(ct * LANE, (ct + 1) * LANE)
            zero = jnp.zeros((1, LANE), F32)
            dwg, dwv, dbg, dbv = [zero] * FFN_K, [zero] * FFN_K, zero, zero
            for s in range(tm // sub):
                dgc, dvc, xg, xv = dconv(FFN_HALO + s * sub, sub, ln, da_ref[s * sub:(s + 1) * sub, ln])
                dgb[s * sub:(s + 1) * sub, ln] = dgc
                dvb[s * sub:(s + 1) * sub, ln] = dvc
                dwg = [dwg[kk] + colsum(dgc * xg[kk]) for kk in range(FFN_K)]
                dwv = [dwv[kk] + colsum(dvc * xv[kk]) for kk in range(FFN_K)]
                dbg, dbv = dbg + colsum(dgc), dbv + colsum(dvc)
            for kk in range(FFN_K):
                dwg_ref[kk:kk + 1, ln] += dwg[kk]
                dwv_ref[kk:kk + 1, ln] += dwv[kk]
            dbg_ref[:, ln] += dbg
            dbv_ref[:, ln] += dbv
            dgc, dvc, _, _ = dconv(FFN_HALO + tm, FFN_HALO, ln, jnp.where(last, 0.0, dan_ref[:, ln]))
            dgb[tm:tm + FFN_HALO, ln] = dgc
            dvb[tm:tm + FFN_HALO, ln] = dvc
            for dbuf, w_ref, dout in ((dgb, wg_ref, dg_ref), (dvb, wv_ref, dv_ref)):
                for s in range(tm // sub):
                    acc = jnp.zeros((sub, LANE), F32)
                    for kk in range(FFN_K):
                        fo = s * sub + (FFN_K - 1) - kk
                        acc = acc + w_ref[kk:kk + 1, ln] * dbuf[fo:fo + sub, ln]
                    dout[s * sub:(s + 1) * sub, ln] = acc.astype(dout.dtype)
        if n_comm:
            pl.when(jnp.logical_and(pl.program_id(0) == nct - 1, pl.program_id(1) == nblk - 1))(wait)

    cur = pl.BlockSpec((tm, FFN_TC), lambda c, r: (r, c))
    prev = pl.BlockSpec((FFN_HALO, FFN_TC), lambda c, r: (jnp.maximum(r * hpb - 1, 0), c))
    nxt = pl.BlockSpec((FFN_HALO, FFN_TC), lambda c, r: (jnp.minimum((r + 1) * hpb, last_halo), c))
    wg = pl.BlockSpec((8, FFN_TC), lambda c, r: (0, c))
    wv = pl.BlockSpec((8, FFN_TC), lambda c, r: (0, nct + c))
    bg = pl.BlockSpec((1, FFN_TC), lambda c, r: (0, c))
    bv = pl.BlockSpec((1, FFN_TC), lambda c, r: (0, nct + c))
    curv = pl.BlockSpec((tm, FFN_TC), lambda c, r: (r, nct + c))
    prevv = pl.BlockSpec((FFN_HALO, FFN_TC), lambda c, r: (jnp.maximum(r * hpb - 1, 0), nct + c))
    nxtv = pl.BlockSpec((FFN_HALO, FFN_TC), lambda c, r: (jnp.minimum((r + 1) * hpb, last_halo), nct + c))
    out_shape = (jax.ShapeDtypeStruct((2, LP, F), BF16),
                 jax.ShapeDtypeStruct((8, F), F32), jax.ShapeDtypeStruct((8, F), F32),
                 jax.ShapeDtypeStruct((1, F), F32), jax.ShapeDtypeStruct((1, F), F32))
    out_specs = (pl.BlockSpec((2, tm, FFN_TC), lambda c, r: (0, r, c)),
                 pl.BlockSpec((8, FFN_TC), lambda c, r: (0, c)),
                 pl.BlockSpec((8, FFN_TC), lambda c, r: (0, c)),
                 pl.BlockSpec((1, FFN_TC), lambda c, r: (0, c)),
                 pl.BlockSpec((1, FFN_TC), lambda c, r: (0, c)))
    in_specs = [cur, curv, prev, prevv, nxt, nxtv, cur, nxt, wg, wv, bg, bv]
    args = [upg, upv, upg, upv, upg, upv, dact, dact, w, w, b, b]
    scratch = [pltpu.VMEM((TB, FFN_TC), F32), pltpu.VMEM((TB, FFN_TC), F32),
               pltpu.VMEM((tm + FFN_HALO, FFN_TC), F32), pltpu.VMEM((tm + FFN_HALO, FFN_TC), F32)]
    if n_comm:
        anyspec = pl.BlockSpec(memory_space=pl.ANY)
        in_specs += [anyspec] * n_comm
        args += list(comm[0])
        out_shape += _chip_exchange_shapes(*comm)
        out_specs += (anyspec,) * n_comm
        scratch += [pltpu.SemaphoreType.DMA((3 * n_comm,)), pltpu.SemaphoreType.DMA((3 * n_comm,))]
    sem = ("arbitrary", "arbitrary") if n_comm else ("parallel", "arbitrary")
    dup, dwg, dwv, dbg, dbv, *others = pl.pallas_call(
        body, name=name, out_shape=out_shape, grid=(nct, nblk), in_specs=in_specs, out_specs=out_specs,
        scratch_shapes=scratch, compiler_params=_params(sem))(*args)
    return (dup, jnp.concatenate([dwg, dwv], axis=1), jnp.concatenate([dbg, dbv], axis=1)) + tuple(others)


POOL_HALO = 16


def _pool_fwd(h, g, pw, pb, ps, tm, name):
    LP, Dm = h.shape
    sub = _sub_rows(tm)
    hpb = tm // POOL_HALO

    def body(h_ref, hh_ref, g_ref, pw_ref, pb_ref, ps_ref, o_ref, d_ref, buf):
        r = pl.program_id(0)
        gg = g_ref[...]

        def norm(x):
            return x * lax.rsqrt(jnp.mean(x * x, axis=1, keepdims=True) + RMS_EPS) * gg

        x = h_ref[...]
        buf[POOL_HALO:POOL_HALO + tm, :] = norm(x)
        buf[0:POOL_HALO, :] = jnp.where(r > 0, norm(hh_ref[...]), 0.0)
        for gi, w in enumerate(POOL_WINDOWS):
            ln = slice(gi * POOL_G, (gi + 1) * POOL_G)
            for s in range(tm // sub):
                base = POOL_HALO + s * sub
                acc = buf[base:base + sub, ln]
                for jj in range(1, w):
                    acc = acc + buf[base - jj:base - jj + sub, ln]
                t = r * tm + s * sub + lax.broadcasted_iota(jnp.int32, (sub, 1), 0)
                cnt = jnp.minimum(t + 1, w).astype(F32)
                d_ref[s * sub:(s + 1) * sub, ln] = (acc / cnt - buf[base:base + sub, ln]).astype(d_ref.dtype)
            y = jnp.dot(d_ref[:, ln], pw_ref[gi], preferred_element_type=F32) + pb_ref[:, ln]
            o_ref[:, ln] = x[:, ln] + y * ps_ref[:, ln]

    row = pl.BlockSpec((tm, Dm), lambda r: (r, 0))
    halo = pl.BlockSpec((POOL_HALO, Dm), lambda r: (jnp.maximum(r * hpb - 1, 0), 0))
    vec = pl.BlockSpec((1, Dm), lambda r: (0, 0))
    wsp = pl.BlockSpec((len(POOL_WINDOWS), POOL_G, POOL_G), lambda r: (0, 0, 0))
    return pl.pallas_call(
        body, name=name,
        out_shape=(jax.ShapeDtypeStruct((LP, Dm), F32), jax.ShapeDtypeStruct((LP, Dm), BF16)),
        grid=(LP // tm,), in_specs=[row, halo, vec, wsp, vec, vec], out_specs=(row, row),
        scratch_shapes=[pltpu.VMEM((POOL_HALO + tm, Dm), F32)],
        compiler_params=_params(("parallel",)))(h, h, g, pw, pb, ps)


def _pool_bwd(h, g, d, pw, pb, ps, dh_out, tm, name):
    LP, Dm = h.shape
    sub = _sub_rows(tm)
    hpb = tm // POOL_HALO
    nblk = LP // tm
    last_halo = LP // POOL_HALO - 1
    nt = (((1,), (1,)), ((), ()))
    tn = (((0,), (0,)), ((), ()))

    def body(h_ref, g_ref, d_ref, pw_ref, pb_ref, ps_ref, do_ref, don_ref,
             dh_ref, dpw_ref, dpb_ref, dps_ref, dg_ref, ebuf, ddb, dnb):
        r = pl.program_id(0)

        @pl.when(r == 0)
        def _():
            dpw_ref[...] = jnp.zeros_like(dpw_ref)
            dpb_ref[...] = jnp.zeros_like(dpb_ref)
            dps_ref[...] = jnp.zeros_like(dps_ref)
            dg_ref[...] = jnp.zeros_like(dg_ref)

        for gi, w in enumerate(POOL_WINDOWS):
            ln = slice(gi * POOL_G, (gi + 1) * POOL_G)
            wg = pw_ref[gi]
            dog = do_ref[:, ln]
            dg_b = d_ref[:, ln]
            y_pre = jnp.dot(dg_b, wg, preferred_element_type=F32) + pb_ref[:, ln]
            dps_ref[:, ln] += jnp.sum(dog * y_pre, axis=0, keepdims=True)
            dy = dog * ps_ref[:, ln]
            dpb_ref[:, ln] += jnp.sum(dy, axis=0, keepdims=True)
            dyb = dy.astype(BF16)
            dpw_ref[gi] += lax.dot_general(dg_b, dyb, tn, preferred_element_type=F32)
            dd = lax.dot_general(dyb, wg, nt, preferred_element_type=F32)
            ddb[:, ln] = dd
            t = r * tm + lax.broadcasted_iota(jnp.int32, (tm, 1), 0)
            ebuf[0:tm, ln] = dd / jnp.minimum(t + 1, w).astype(F32)
            dyn = (don_ref[:, ln] * ps_ref[:, ln]).astype(BF16)
            ddn = lax.dot_general(dyn, wg, nt, preferred_element_type=F32)
            tn_ = (r + 1) * tm + lax.broadcasted_iota(jnp.int32, (POOL_HALO, 1), 0)
            ebuf[tm:tm + POOL_HALO, ln] = jnp.where(r < nblk - 1, ddn / jnp.minimum(tn_ + 1, w).astype(F32), 0.0)
            for s in range(tm // sub):
                acc = ebuf[s * sub:(s + 1) * sub, ln]
                for jj in range(1, w):
                    acc = acc + ebuf[s * sub + jj:s * sub + jj + sub, ln]
                dnb[s * sub:(s + 1) * sub, ln] = acc - ddb[s * sub:(s + 1) * sub, ln]
        x = h_ref[...]
        rr = lax.rsqrt(jnp.mean(x * x, axis=1, keepdims=True) + RMS_EPS)
        xhat = x * rr
        dn = dnb[...]
        dxh = dn * g_ref[...]
        dh_ref[...] = do_ref[...] + rr * (dxh - xhat * jnp.mean(dxh * xhat, axis=1, keepdims=True))
        dg_ref[...] += jnp.sum(dn * xhat, axis=0, keepdims=True)

    row = pl.BlockSpec((tm, Dm), lambda r: (r, 0))
    nxt = pl.BlockSpec((POOL_HALO, Dm), lambda r: (jnp.minimum((r + 1) * hpb, last_halo), 0))
    vec = pl.BlockSpec((1, Dm), lambda r: (0, 0))
    wsp = pl.BlockSpec((len(POOL_WINDOWS), POOL_G, POOL_G), lambda r: (0, 0, 0))
    return pl.pallas_call(
        body, name=name,
        out_shape=(jax.ShapeDtypeStruct((LP, Dm), F32),
                   jax.ShapeDtypeStruct((len(POOL_WINDOWS), POOL_G, POOL_G), F32),
                   jax.ShapeDtypeStruct((1, Dm), F32), jax.ShapeDtypeStruct((1, Dm), F32),
                   jax.ShapeDtypeStruct((1, Dm), F32)),
        grid=(nblk,), in_specs=[row, vec, row, wsp, vec, vec, row, nxt],
        out_specs=(row, wsp, vec, vec, vec),
        scratch_shapes=[pltpu.VMEM((tm + POOL_HALO, Dm), F32), pltpu.VMEM((tm, Dm), F32),
                        pltpu.VMEM((tm, Dm), F32)],
        compiler_params=_params(("arbitrary",)))(h, g, d, pw, pb, ps, dh_out, dh_out)


def _adamw(w, g, m, v, name):
    shape = w.shape
    cols = shape[-1]
    rows = int(np.prod(shape[:-1])) if len(shape) > 1 else 1
    w2, g2, m2, v2 = (t.reshape(rows, cols) for t in (w, g, m, v))
    tr = rows
    for cand in (256, 128, 64, 32, 16, 8):
        if rows % cand == 0 and rows > cand:
            tr = cand
            break
    c1 = float(1.0 - ADAM_B1 ** ADAM_STEP)
    c2 = float(1.0 - ADAM_B2 ** ADAM_STEP)

    def body(w_ref, g_ref, m_ref, v_ref, d_ref, mo_ref, vo_ref):
        gg = g_ref[...]
        mn = ADAM_B1 * m_ref[...] + (1.0 - ADAM_B1) * gg
        vn = ADAM_B2 * v_ref[...] + (1.0 - ADAM_B2) * (gg * gg)
        m_hat = mn / c1
        v_hat = vn / c2
        d_ref[...] = -ADAM_LR * (m_hat / (jnp.sqrt(v_hat) + ADAM_EPS) + ADAM_WD * w_ref[...])
        mo_ref[...] = mn
        vo_ref[...] = vn

    spec = pl.BlockSpec((tr, cols), lambda i: (i, 0))
    sds = jax.ShapeDtypeStruct((rows, cols), F32)
    d2, mo, vo = pl.pallas_call(
        body, name=name, out_shape=(sds, sds, sds), grid=(rows // tr,),
        in_specs=[spec] * 4, out_specs=(spec,) * 3,
        compiler_params=_params(("parallel",)))(w2, g2, m2, v2)
    return d2.reshape(shape), mo.reshape(shape), vo.reshape(shape)


def _row_tiles(LP):
    tm = LP // 4
    assert LP % 4 == 0 and tm % CONV_HALO == 0 and LP % ATT_BLK == 0, LP
    return tm, LP // 2


MESH = pl.DeviceIdType.MESH
ANY = pl.BlockSpec(memory_space=pl.ANY)


def _coords():
    return lax.axis_index("x"), lax.axis_index("y"), lax.axis_index("c")


def _other_chips(x, y):
    return [(1 - x, y), (x, 1 - y), (1 - x, 1 - y)]


def _allreduce_small(pack):
    Rs, C = pack.shape
    n_dev = 8

    def body(x_ref, o_ref, buf, send_sems, recv_sems):
        x, y, c = _coords()
        me = 4 * x + 2 * y + c
        buf[me] = x_ref[...]
        peers = []
        for rel in range(1, n_dev):
            px = 1 - x if rel & 4 else x
            py = 1 - y if rel & 2 else y
            pc = 1 - c if rel & 1 else c
            peers.append((px, py, pc))
        sends = [pltpu.make_async_remote_copy(
            src_ref=x_ref, dst_ref=buf.at[me], send_sem=send_sems.at[k], recv_sem=recv_sems.at[k],
            device_id=peer, device_id_type=MESH) for k, peer in enumerate(peers)]
        for cp in sends:
            cp.start()
        for k, (px, py, pc) in enumerate(peers):
            pltpu.make_async_remote_copy(
                src_ref=x_ref, dst_ref=buf.at[4 * px + 2 * py + pc], send_sem=send_sems.at[k],
                recv_sem=recv_sems.at[k], device_id=(px, py, pc), device_id_type=MESH).wait_recv()
        for cp in sends:
            cp.wait_send()
        acc = buf[0]
        for d in range(1, n_dev):
            acc = acc + buf[d]
        o_ref[...] = acc

    vm = pl.BlockSpec(memory_space=pltpu.VMEM)
    return pl.pallas_call(
        body, name="allreduce_replicated", out_shape=jax.ShapeDtypeStruct((Rs, C), F32),
        in_specs=[vm], out_specs=vm,
        scratch_shapes=[pltpu.VMEM((n_dev, Rs, C), F32), pltpu.SemaphoreType.DMA((n_dev - 1,)),
                        pltpu.SemaphoreType.DMA((n_dev - 1,))],
    )(pack)


REPLICATED = ("mix_norm_even", "b_f", "conv_b", "ln_g", "ln_b", "ffn_norm", "ffn_conv_b", "final_norm")


def _pad_rows(flat, align_rows, cols):
    rows = -(-flat.shape[-1] // cols)
    rows = -(-rows // align_rows) * align_rows
    pad = rows * cols - flat.shape[-1]
    flat = jnp.pad(flat, [(0, 0)] * (flat.ndim - 1) + [(0, pad)])
    return flat.reshape(flat.shape[:-1] + (rows, cols))


def _pack_replicated(grads, loss):
    parts = [_pad_rows(grads[name].astype(F32).reshape(-1), 1, LANE).reshape(-1) for name in REPLICATED]
    parts.append(_pad_rows(loss.reshape(-1)[:1], 1, LANE).reshape(-1))
    return _pad_rows(jnp.concatenate(parts), 8, LANE)


def _unpack_replicated(reduced, shapes):
    flat = reduced.reshape(-1)
    out, off = {}, 0
    for name in REPLICATED:
        n = int(np.prod(shapes[name]))
        out[name] = flat[off:off + n].reshape(shapes[name])
        off += -(-n // LANE) * LANE
    return out, flat[off]


def _ffn_fwd2(h, W, layer, tm, tmm, host_up=None, host_act=None):
    tag = str(layer)
    n = _rms_fwd(h, W["ffn_norm"][layer:layer + 1], BF16, tm, f"ffn_norm_{tag}")
    up, *g_up = _mm(n, W["w_up"][layer], "nn", F32, tmm, UP_SHARD, f"ffn_up_{tag}", host=host_up) \
        if host_up else (_mm(n, W["w_up"][layer], "nn", F32, tmm, UP_SHARD, f"ffn_up_{tag}"),)
    act, *g_act = _ffn_act_fwd(up, W["ffn_conv_w_p"][layer], W["ffn_conv_b"][layer:layer + 1], tm,
                               f"ffn_act_{tag}", host=host_act) \
        if host_act else (_ffn_act_fwd(up, W["ffn_conv_w_p"][layer], W["ffn_conv_b"][layer:layer + 1], tm,
                                       f"ffn_act_{tag}"),)
    out = _mm(act, W["w_down"][layer], "nn", F32, tm, D_MODEL, f"ffn_down_{tag}", add=h)
    return out, (n, up, act), g_up + g_act


def _ffn_bwd2(h, W, layer, saved, dout, tm, tmm, reduce=None):
    tag = str(layer)
    n, up, act = saved
    parts, comm = [], None
    if reduce is None:
        dact = _mm(dout, W["w_down"][layer], "nt", F32, tmm, UP_SHARD, f"ffn_dact_{tag}")
    else:
        names, fulls, cuts = reduce
        dact, *recv = _mm(dout, W["w_down"][layer], "nt", F32, tmm, UP_SHARD, f"ffn_dact_{tag}",
                          host=(fulls, cuts, "pairx"))
        parts = [_pair_sum2(f, r, cut, PAIR_SUM_BLOCKS[nm], "grad_pair_sum_" + nm)
                 for f, r, cut, nm in zip(fulls, recv, cuts, names)]
        comm = (parts, cuts)
    dwd = _mm(act, dout, "tn", F32, D_FF // 2, 512, f"ffn_dwdown_{tag}")
    dup, dcw, dcb, *others = _ffn_act_bwd(up, dact, W["ffn_conv_w_p"][layer], W["ffn_conv_b"][layer:layer + 1],
                                          tm, f"ffn_act_bwd_{tag}", comm=comm)
    dn = _mm_ffn_dn(dup, W["w_up"][layer], tm, D_MODEL, f"ffn_dn_{tag}")
    dwu = _mm_ffn_dwup(n, dup, 512, D_FF // 2, f"ffn_dwup_{tag}")
    dh, dgain = _rms_bwd(h, W["ffn_norm"][layer:layer + 1], dn, dout, tm, f"ffn_norm_bwd_{tag}")
    return dh, (dwu, dwd), dict(gain=dgain, cw=dcw[:FFN_K], cb=dcb), parts, others


GATHER_FIRST = ("w_in", "small")
GATHER_LATE = ("pool_w", "w_up", "w_down")
HOSTED_FFN = ("w_up1", "w_down1")
HOSTED = ("w_out", "pool_w", "w_up0", "w_down0")
LATE = ("small",)


def _local_step2(h0, tgt, W, n_real, cut_of):
    LP = h0.shape[0]
    tm, tmm = _row_tiles(LP)
    nb = LP // ATT_BLK
    G = {}
    n0 = _rms_fwd(h0, W["mix_norm_even"], BF16, tm, "mix_norm_even")
    sh = W["late_shards"]
    stage = lambda *names: ([sh[n] for n in names], [cut_of[n] for n in names])
    proj, g_down0 = _mm(n0, W["w_in_p"], "nn", F32, tmm, 896, "in_proj", host=stage("w_down0"))
    c = _fgate_fwd(proj, W["b_f_p"], "forget_gate")
    qT, kT, k_aug, vT = _attn_prep(proj, c, "attention_operands")
    oT, lse, g_pool, g_up0, g_out = _attn_fwd2(qT, k_aug, vT, "fox_attention",
                                               comm=stage("pool_w", "w_up0", "w_out"))
    g_down0, g_pool, g_up0, g_out = _gather_forward(
        [g_down0, g_pool, g_up0, g_out], stage("w_down0", "pool_w", "w_up0", "w_out")[1], "gather_forward_0")
    W = dict(W)
    W.update(pool_w=g_pool, w_up=[g_up0, None], w_down=[g_down0, None], w_out=g_out)
    u1, cat = _conf_fwd(proj, W["conv_w_p"], W["conv_b"], W["ln_g"], W["ln_b"], tm, "conformer",
                        cat_into=_attn_rows(oT, 1.0, BF16, "attention_rows", total_cols=2 * FOX_W))
    h1 = _mm(cat, W["w_out"], "nn", F32, tmm, D_MODEL, "out_proj", add=h0)
    h2, ffn0, (g_down1, g_up1) = _ffn_fwd2(h1, W, 0, tm, tmm, host_up=stage("w_down1"), host_act=stage("w_up1"))
    g_down1, g_up1 = _gather_forward([g_down1, g_up1], stage("w_down1", "w_up1")[1], "gather_forward_1")
    W.update(w_up=[g_up0, g_up1], w_down=[g_down0, g_down1])
    h3, dpool = _pool_fwd(h2, W["mix_norm_odd"], W["pool_w"], W["pool_b"], W["pool_scale"], tm, "pool_mixer")
    h4, ffn1, _ = _ffn_fwd2(h3, W, 1, tm, tmm)
    loss, dh4, G["final_norm"] = _loss_head(h4, W["final_norm"], tgt, n_real, tm, "loss_head")

    dh3, (G["w_up1"], G["w_down1"]), g1, _, _ = _ffn_bwd2(h3, W, 1, ffn1, dh4, tm, tmm)
    dh2, G["pool_w"], G["pool_b"], G["pool_scale"], G["mix_norm_odd"] = _pool_bwd(
        h2, W["mix_norm_odd"], dpool, W["pool_w"], W["pool_b"], W["pool_scale"], dh3, tm, "pool_mixer_bwd")
    cuts1 = [cut_of[n] for n in HOSTED_FFN]
    dh1, (G["w_up0"], G["w_down0"]), g0, parts1, others1 = _ffn_bwd2(
        h1, W, 0, ffn0, dh2, tm, tmm, reduce=(HOSTED_FFN, [G[n] for n in HOSTED_FFN], cuts1))
    G["ffn_norm"] = jnp.concatenate([g0["gain"], g1["gain"]], axis=0)
    G["ffn_conv_w"] = jnp.stack([g0["cw"], g1["cw"]])
    G["ffn_conv_b"] = jnp.concatenate([g0["cb"], g1["cb"]], axis=0)

    dcat = _mm(dh1, W["w_out"], "nt", F32, tmm, D_MODEL, "out_proj_dx")
    G["w_out"] = _mm(cat, dh1, "tn", F32, 512, D_MODEL, "out_proj_dw")
    hcuts = [cut_of[n] for n in HOSTED]
    hfull = [G[n] for n in HOSTED]
    dadg, dcw, G["conv_b"], G["ln_g"], G["ln_b"], *hrecv = _conf_bwd(
        proj, u1, dcat, W["conv_w_p"], W["ln_g"], W["ln_b"], tm, "conformer_bwd", host=(hfull, hcuts, "pairx"))
    G["conv_w"] = dcw[:CONV_K]
    doT = _attn_cols(dcat, "attention_do_cols")
    hparts = [_pair_sum2(f, r, cut, PAIR_SUM_BLOCKS[n], "grad_pair_sum_" + n)
              for f, r, cut, n in zip(hfull, hrecv, hcuts, HOSTED)]
    dqT, dkT, dvT, *hothers = _attn_bwd2(qT, kT, k_aug, vT, oT, doT, lse, "fox_attention_bwd",
                                         comm=(hparts, hcuts))
    dqkv, dc = _attn_grads_rows(dqT, dkT, dvT, "attention_grads_rows")
    dfl, dbf = _fgate_bwd(proj, W["b_f_p"], dc, "forget_gate_bwd")
    G["b_f"] = dbf[:, :HEADS]
    dproj = jnp.concatenate([dqkv, dadg, dfl], axis=1)
    gp = _mm(n0, dproj, "tn", F32, 512, 896, "in_proj_dw")
    g_w_in = jnp.concatenate([gp[:, :3 * FOX_W], gp[:, 3 * FOX_W + 2 * CONV_CH:3 * FOX_W + 2 * CONV_CH + HEADS],
                              gp[:, 3 * FOX_W:3 * FOX_W + 2 * CONV_CH]], axis=1)
    g_w_in = g_w_in.reshape(D_MODEL, N_CHIPS, IN_SHARD).transpose(1, 0, 2)
    icut = [cut_of["w_in"]]
    dn0, irecv = _mm(dproj, W["w_in_p"], "nt", F32, tmm, D_MODEL, "in_proj_dx", host=([g_w_in], icut, "pairx"))
    ipart = _pair_sum2(g_w_in, irecv, icut[0], PAIR_SUM_BLOCKS["w_in"], "grad_pair_sum_w_in")
    dh0, G["mix_norm_even"], iother = _rms_bwd(h0, W["mix_norm_even"], dn0, dh1, tm, "mix_norm_even_bwd",
                                               host=([ipart], icut, "chipx"))
    parts = dict(zip(HOSTED_FFN + HOSTED + ("w_in",), parts1 + hparts + [ipart]))
    others = dict(zip(HOSTED_FFN + HOSTED + ("w_in",), list(others1) + list(hothers) + [iother]))
    return loss, dh0, G, parts, others


class _Cut:
    def __init__(self, full_shape, chip_dim, half_dim):
        self.full = tuple(full_shape)
        self.chip_dim, self.half_dim = chip_dim, half_dim
        self.chip_size = full_shape[chip_dim] // N_CHIPS
        self.half_size = full_shape[half_dim] // 2
        assert chip_dim != half_dim

    def shape(self, chip=False, half=False):
        s = list(self.full)
        if chip:
            s[self.chip_dim] = self.chip_size
        if half:
            s[self.half_dim] = self.half_size
        return tuple(s)

    def region(self, ref, chip=None, half=None):
        idx = [pl.ds(0, n) for n in ref.shape]
        if chip is not None:
            idx[self.chip_dim] = pl.ds(chip * self.chip_size, self.chip_size)
        if half is not None:
            idx[self.half_dim] = pl.ds(half * self.half_size, self.half_size)
        return ref.at[tuple(idx)]


SMALL_SHARDED = ("meta_tokens", "mix_norm_odd", "pool_b", "pool_scale", "conv_w", "ffn_conv_w")
SMALL_ROWS = 144


def _cuts():
    return {
        "w_in": _Cut((N_CHIPS, D_MODEL, IN_SHARD), 0, 1),
        "w_out": _Cut((D_MODEL, D_MODEL), 0, 1),
        "pool_w": _Cut((len(POOL_WINDOWS), POOL_G, POOL_G), 1, 0),
        "w_up": _Cut((2, D_MODEL, 2 * D_FF), 2, 1),
        "w_down": _Cut((2, D_FF, D_MODEL), 1, 2),
        "small": _Cut((N_CHIPS, SMALL_ROWS, LANE), 0, 1),
        "w_up0": _Cut((D_MODEL, 2 * D_FF), 1, 0), "w_up1": _Cut((D_MODEL, 2 * D_FF), 1, 0),
        "w_down0": _Cut((D_FF, D_MODEL), 0, 1), "w_down1": _Cut((D_FF, D_MODEL), 0, 1),
    }


COMM_ORDER = ("w_in", "w_out", "pool_w", "w_up", "w_down", "small")


def _remote(src, dst, send_sems, recv_sems, k, to):
    return pltpu.make_async_remote_copy(src_ref=src, dst_ref=dst, send_sem=send_sems.at[k],
                                        recv_sem=recv_sems.at[k], device_id=to, device_id_type=MESH)


def _gather_weights(shards, cuts):
    n = len(shards)

    def body(*refs):
        srcs, outs = refs[:n], refs[n:2 * n]
        send_sems, recv_sems = refs[2 * n:]
        x, y, c = _coords()
        me = 2 * x + y
        sibling = (x, y, 1 - c)
        chips = _other_chips(x, y)
        sends = []
        for t, cut in enumerate(cuts):
            push = _remote(srcs[t], cut.region(outs[t], chip=me), send_sems, recv_sems, 7 * t, sibling)
            push.start()
            sends.append(push)
            for kk, chip in enumerate(chips):
                cp = _remote(cut.region(srcs[t], half=c), cut.region(outs[t], chip=me, half=c),
                             send_sems, recv_sems, 7 * t + 1 + kk, (*chip, c))
                cp.start()
                sends.append(cp)
        for t, cut in enumerate(cuts):
            for kk, (px, py) in enumerate(chips):
                landed = cut.region(outs[t], chip=2 * px + py, half=c)
                _remote(landed, landed, send_sems, recv_sems, 7 * t + 1 + kk, sibling).wait_recv()
                fwd = _remote(landed, landed, send_sems, recv_sems, 7 * t + 4 + kk, sibling)
                fwd.start()
                sends.append(fwd)
        for t, cut in enumerate(cuts):
            mine = cut.region(outs[t], chip=me)
            _remote(mine, mine, send_sems, recv_sems, 7 * t, sibling).wait_recv()
            for kk, (px, py) in enumerate(chips):
                other = cut.region(outs[t], chip=2 * px + py, half=1 - c)
                _remote(other, other, send_sems, recv_sems, 7 * t + 4 + kk, sibling).wait_recv()
        for cp in sends:
            cp.wait_send()

    return pl.pallas_call(
        body, name="gather_weights",
        out_shape=tuple(jax.ShapeDtypeStruct(cut.full, s.dtype) for cut, s in zip(cuts, shards)),
        in_specs=[ANY] * n, out_specs=tuple([ANY] * n),
        scratch_shapes=[pltpu.SemaphoreType.DMA((7 * n,)), pltpu.SemaphoreType.DMA((7 * n,))],
    )(*shards)


def _gather_first_ops(srcs, outs, send_sems, recv_sems, cuts):
    x, y, c = _coords()
    me = 2 * x + y
    sibling = (x, y, 1 - c)
    chips = _other_chips(x, y)

    def copies():
        out = []
        for t, cut in enumerate(cuts):
            out.append(_remote(srcs[t], cut.region(outs[t], chip=me), send_sems, recv_sems, 4 * t, sibling))
            for kk, chip in enumerate(chips):
                out.append(_remote(cut.region(srcs[t], half=c), cut.region(outs[t], chip=me, half=c),
                                   send_sems, recv_sems, 4 * t + 1 + kk, (*chip, c)))
        return out

    def start():
        for cp in copies():
            cp.start()

    def wait():
        for t, cut in enumerate(cuts):
            mine = cut.region(outs[t], chip=me)
            _remote(mine, mine, send_sems, recv_sems, 4 * t, sibling).wait_recv()
            for kk, (px, py) in enumerate(chips):
                landed = cut.region(outs[t], chip=2 * px + py, half=c)
                _remote(landed, landed, send_sems, recv_sems, 4 * t + 1 + kk, sibling).wait_recv()
        for cp in copies():
            cp.wait_send()

    return start, wait


def _pair_exchange_ops(srcs, outs, send_sems, recv_sems, cuts):
    x, y, c = _coords()

    def copies():
        return [_remote(cut.region(srcs[t], half=1 - c), outs[t], send_sems, recv_sems, t, (x, y, 1 - c))
                for t, cut in enumerate(cuts)]

    def start():
        for cp in copies():
            cp.start()

    def wait():
        for cp in copies():
            cp.wait()

    return start, wait


def _host_plan(host):
    arrays, cuts = host[0], host[1]
    if len(host) > 2 and host[2] == "chipx":
        return _chip_exchange_shapes(arrays, cuts), 3 * len(arrays), _chip_exchange_ops
    if len(host) > 2 and host[2] == "pairx":
        return (tuple(jax.ShapeDtypeStruct(cut.shape(half=True), a.dtype) for cut, a in zip(cuts, arrays)),
                len(arrays), _pair_exchange_ops)
    return (tuple(jax.ShapeDtypeStruct(cut.full, a.dtype) for cut, a in zip(cuts, arrays)),
            4 * len(arrays), _gather_first_ops)


def _gather_forward(fulls, cuts, name):
    n = len(fulls)

    def body(*refs):
        outs = refs[n:2 * n]
        send_sems, recv_sems = refs[2 * n:]
        x, y, c = _coords()
        sibling = (x, y, 1 - c)
        chips = _other_chips(x, y)
        sends = []
        for t, cut in enumerate(cuts):
            for kk, (px, py) in enumerate(chips):
                landed = cut.region(outs[t], chip=2 * px + py, half=c)
                cp = _remote(landed, landed, send_sems, recv_sems, 3 * t + kk, sibling)
                cp.start()
                sends.append(cp)
        for t, cut in enumerate(cuts):
            for kk, (px, py) in enumerate(chips):
                other = cut.region(outs[t], chip=2 * px + py, half=1 - c)
                _remote(other, other, send_sems, recv_sems, 3 * t + kk, sibling).wait_recv()
        for cp in sends:
            cp.wait_send()

    return pl.pallas_call(
        body, name=name,
        out_shape=tuple(jax.ShapeDtypeStruct(f.shape, f.dtype) for f in fulls),
        in_specs=[ANY] * n, out_specs=tuple([ANY] * n), input_output_aliases={t: t for t in range(n)},
        scratch_shapes=[pltpu.SemaphoreType.DMA((3 * n,)), pltpu.SemaphoreType.DMA((3 * n,))],
    )(*fulls)


def _pair_exchange2(fulls, cuts, name):
    n = len(fulls)

    def body(*refs):
        srcs, outs = refs[:n], refs[n:2 * n]
        send_sems, recv_sems = refs[2 * n:]
        x, y, c = _coords()
        cps = [_remote(cut.region(srcs[t], half=1 - c), outs[t], send_sems, recv_sems, t, (x, y, 1 - c))
               for t, cut in enumerate(cuts)]
        for cp in cps:
            cp.start()
        for cp in cps:
            cp.wait()

    return pl.pallas_call(
        body, name=name,
        out_shape=tuple(jax.ShapeDtypeStruct(cut.shape(half=True), f.dtype) for cut, f in zip(cuts, fulls)),
        in_specs=[ANY] * n, out_specs=tuple([ANY] * n),
        scratch_shapes=[pltpu.SemaphoreType.DMA((n,)), pltpu.SemaphoreType.DMA((n,))],
    )(*fulls)


def _grid_of(shape, blk):
    assert all(s % b == 0 for s, b in zip(shape, blk)), (shape, blk)
    return tuple(s // b for s, b in zip(shape, blk))


def _pair_sum2(full, recv, cut, blk, name):
    hshape = cut.shape(half=True)
    grid = _grid_of(hshape, blk)
    hb = cut.half_size // blk[cut.half_dim]
    hd = cut.half_dim
    pos = jnp.stack([lax.axis_index("c")]).astype(jnp.int32)

    def full_idx(*a):
        ids, p = list(a[:-1]), a[-1]
        ids[hd] = ids[hd] + p[0] * hb
        return tuple(ids)

    def body(p_ref, f_ref, r_ref, o_ref):
        o_ref[...] = (f_ref[...] + r_ref[...]).astype(o_ref.dtype)

    return pl.pallas_call(
        body, name=name, out_shape=jax.ShapeDtypeStruct(hshape, BF16),
        grid_spec=pltpu.PrefetchScalarGridSpec(
            num_scalar_prefetch=1, grid=grid,
            in_specs=[pl.BlockSpec(blk, full_idx), pl.BlockSpec(blk, lambda *a: tuple(a[:-1]))],
            out_specs=pl.BlockSpec(blk, lambda *a: tuple(a[:-1]))),
        compiler_params=_params(("parallel",) * len(grid)))(pos, full, recv)


def _chip_exchange_ops(srcs, outs, send_sems, recv_sems, cuts):
    x, y, c = _coords()
    me = 2 * x + y
    chips = _other_chips(x, y)

    def copies():
        return [_remote(cut.region(srcs[t], chip=2 * px + py), outs[t].at[me], send_sems, recv_sems,
                        3 * t + kk, (px, py, c))
                for t, cut in enumerate(cuts) for kk, (px, py) in enumerate(chips)]

    def start():
        for cp in copies():
            cp.start()

    def wait():
        for t, cut in enumerate(cuts):
            for kk, (px, py) in enumerate(chips):
                slot = outs[t].at[2 * px + py]
                _remote(slot, slot, send_sems, recv_sems, 3 * t + kk, (px, py, c)).wait_recv()
        for cp in copies():
            cp.wait_send()

    return start, wait


def _chip_exchange_shapes(parts, cuts):
    return tuple(jax.ShapeDtypeStruct((N_CHIPS,) + cut.shape(chip=True, half=True), p.dtype)
                 for cut, p in zip(cuts, parts))


def _chip_exchange2(parts, cuts):
    n = len(parts)

    def body(*refs):
        start, wait = _chip_exchange_ops(refs[:n], refs[n:2 * n], refs[2 * n], refs[2 * n + 1], cuts)
        start()
        wait()

    return pl.pallas_call(
        body, name="grad_chip_exchange",
        out_shape=tuple(jax.ShapeDtypeStruct((N_CHIPS,) + cut.shape(chip=True, half=True), p.dtype)
                        for cut, p in zip(cuts, parts)),
        in_specs=[ANY] * n, out_specs=tuple([ANY] * n),
        scratch_shapes=[pltpu.SemaphoreType.DMA((3 * n,)), pltpu.SemaphoreType.DMA((3 * n,))],
    )(*parts)


def _chip_sum2(part, recv, cut, blk, name, stacked=None):
    bshape = cut.shape(chip=True, half=True)
    grid = _grid_of(bshape, blk)
    cb = cut.chip_size // blk[cut.chip_dim]
    hb = cut.half_size // blk[cut.half_dim]
    cd, hd = cut.chip_dim, cut.half_dim
    x, y, c = _coords()
    slots = [2 * px + py for px, py in _other_chips(x, y)]
    pos = jnp.stack([c, 2 * x + y] + slots).astype(jnp.int32)

    def part_idx(*a):
        ids, p = list(a[:-1]), a[-1]
        ids[cd] = ids[cd] + p[1] * cb
        return tuple(ids)

    def recv_idx(kk):
        return lambda *a: (a[-1][2 + kk],) + tuple(a[:-1])

    def out_idx(*a):
        ids, p = list(a[:-1]), a[-1]
        ids[hd] = ids[hd] + p[0] * hb
        return tuple(ids)

    def body(p_ref, own_ref, r0_ref, r1_ref, r2_ref, *rest):
        acc = own_ref[...].astype(F32)
        for r_ref in (r0_ref, r1_ref, r2_ref):
            acc = acc + r_ref[...].astype(F32)
        rest[-1][...] = acc

    in_specs = [pl.BlockSpec(blk, part_idx)] + [pl.BlockSpec((None,) + blk, recv_idx(kk)) for kk in range(3)]
    args = [pos, part, recv, recv, recv]
    aliases = {}
    if stacked is None:
        out_shape = jax.ShapeDtypeStruct(cut.shape(chip=True), F32)
        out_spec = pl.BlockSpec(blk, out_idx)
    else:
        lead, n_lead, into = stacked
        out_shape = jax.ShapeDtypeStruct((n_lead,) + cut.shape(chip=True), F32)
        out_spec = pl.BlockSpec((None,) + blk, lambda *a: (lead,) + out_idx(*a))
        if into is not None:
            in_specs.append(pl.BlockSpec(memory_space=pl.ANY))
            args.append(into)
            aliases = {5: 0}
    return pl.pallas_call(
        body, name=name, out_shape=out_shape,
        grid_spec=pltpu.PrefetchScalarGridSpec(num_scalar_prefetch=1, grid=grid, in_specs=in_specs,
                                               out_specs=out_spec),
        input_output_aliases=aliases, compiler_params=_params(("parallel",) * len(grid)))(*args)


def _pair_swap2(blocks, cuts):
    n = len(blocks)

    def body(*refs):
        outs = refs[n:2 * n]
        send_sems, recv_sems = refs[2 * n:]
        x, y, c = _coords()
        cps = []
        for t, cut in enumerate(cuts):
            mine = cut.region(outs[t], half=c)
            cp = _remote(mine, mine, send_sems, recv_sems, t, (x, y, 1 - c))
            cp.start()
            cps.append(cp)
        for t, cut in enumerate(cuts):
            theirs = cut.region(outs[t], half=1 - c)
            _remote(theirs, theirs, send_sems, recv_sems, t, (x, y, 1 - c)).wait_recv()
        for cp in cps:
            cp.wait_send()

    return pl.pallas_call(
        body, name="grad_pair_swap",
        out_shape=tuple(jax.ShapeDtypeStruct(b.shape, b.dtype) for b in blocks),
        in_specs=[ANY] * n, out_specs=tuple([ANY] * n),
        input_output_aliases={t: t for t in range(n)},
        scratch_shapes=[pltpu.SemaphoreType.DMA((n,)), pltpu.SemaphoreType.DMA((n,))],
    )(*blocks)


PAIR_SUM_BLOCKS = {"w_in": (1, 512, IN_SHARD), "w_out": (512, 512), "pool_w": (1, POOL_G, POOL_G),
                   "w_up0": (128, 2 * D_FF), "w_up1": (128, 2 * D_FF), "w_down0": (1408, 512), "w_down1": (1408, 512),
                   "small": (N_CHIPS, SMALL_ROWS // 2, LANE)}
CHIP_SUM_BLOCKS = {"w_in": (1, 512, IN_SHARD), "w_out": (256, 512), "pool_w": (2, 64, POOL_G),
                   "w_up0": (512, UP_SHARD), "w_up1": (512, UP_SHARD),
                   "w_down0": (DOWN_SHARD, 512), "w_down1": (DOWN_SHARD, 512),
                   "small": (1, SMALL_ROWS // 2, LANE)}


def _small_rows(t, lead):
    flat = t.reshape(lead + (-1,))
    pad = -flat.shape[-1] % LANE
    return jnp.pad(flat, [(0, 0)] * len(lead) + [(0, pad)]).reshape(lead + (-1, LANE))


def _pack_small_shards(shards):
    rows = jnp.concatenate([_small_rows(shards[n].astype(F32), ()) for n in SMALL_SHARDED], axis=0)
    return jnp.pad(rows, ((0, SMALL_ROWS - rows.shape[0]), (0, 0)))[None]


def _unpack_small(pack, shards, axes):
    out, off = {}, 0
    nchip = pack.shape[0]
    for name in SMALL_SHARDED:
        shp = shards[name].shape
        cnt = int(np.prod(shp))
        rows = -(-cnt // LANE)
        t = pack[:, off:off + rows].reshape(nchip, -1)[:, :cnt].reshape((nchip,) + shp)
        out[name] = jnp.concatenate([t[j] for j in range(nchip)], axis=axes[name])
        off += rows
    return out


def _pack_small_grads(grads, shards, axes):
    parts = []
    for name in SMALL_SHARDED:
        shp, ax = shards[name].shape, axes[name]
        g = grads[name].reshape(shp[:ax] + (N_CHIPS, shp[ax]) + shp[ax + 1:])
        parts.append(_small_rows(jnp.moveaxis(g, ax, 0), (N_CHIPS,)))
    rows = jnp.concatenate(parts, axis=1)
    return jnp.pad(rows, ((0, 0), (0, SMALL_ROWS - rows.shape[1]), (0, 0)))


SMALL_AXES = {"meta_tokens": 1, "mix_norm_odd": 1, "pool_b": 2, "pool_scale": 1, "conv_w": 2, "ffn_conv_w": 2}


WEIGHT_NAMES = ("meta_tokens", "mix_norm_even", "w_in", "b_f", "conv_w", "conv_b", "ln_g", "ln_b", "w_out",
                "mix_norm_odd", "pool_w", "pool_b", "pool_scale", "ffn_norm", "w_up", "ffn_conv_w",
                "ffn_conv_b", "w_down", "final_norm")


def kernel(x, meta_tokens, mix_norm_even, w_in, b_f, conv_w, conv_b, ln_g, ln_b, w_out, mix_norm_odd, pool_w, pool_b, pool_scale, ffn_norm, w_up, ffn_conv_w, ffn_conv_b, w_down, final_norm, loss_target, m_meta_tokens, m_mix_norm_even, m_w_in, m_b_f, m_conv_w, m_conv_b, m_ln_g, m_ln_b, m_w_out, m_mix_norm_odd, m_pool_w, m_pool_b, m_pool_scale, m_ffn_norm, m_w_up, m_ffn_conv_w, m_ffn_conv_b, m_w_down, m_final_norm, v_meta_tokens, v_mix_norm_even, v_w_in, v_b_f, v_conv_w, v_conv_b, v_ln_g, v_ln_b, v_w_out, v_mix_norm_odd, v_pool_w, v_pool_b, v_pool_scale, v_ffn_norm, v_w_up, v_ffn_conv_w, v_ffn_conv_b, v_w_down, v_final_norm):
    given = dict(locals())
    w_loc = {n: given[n] for n in WEIGHT_NAMES}
    m_loc = {n: given["m_" + n] for n in WEIGHT_NAMES}
    v_loc = {n: given["v_" + n] for n in WEIGHT_NAMES}
    cut_of = _cuts()
    cuts = [cut_of[n] for n in COMM_ORDER]
    big = ("w_in", "w_out", "pool_w", "w_up", "w_down")
    small_shards = {n: w_loc[n] for n in SMALL_SHARDED}

    shard_of = {n: w_loc[n].astype(BF16).reshape(cut_of[n].shape(chip=True)) for n in big}
    shard_of["small"] = _pack_small_shards(small_shards)
    g_in, g_small = _gather_weights([shard_of[n] for n in GATHER_FIRST], [cut_of[n] for n in GATHER_FIRST])
    g_out = None
    g_pool = g_up = g_down = None
    full = _unpack_small(g_small, small_shards, SMALL_AXES)
    full.update({n: w_loc[n] for n in REPLICATED})
    w_in_full = g_in.transpose(1, 0, 2).reshape(D_MODEL, IN_COLS)
    qkv, f, ag = (w_in_full[:, :3 * FOX_W], w_in_full[:, 3 * FOX_W:3 * FOX_W + HEADS],
                  w_in_full[:, 3 * FOX_W + HEADS:])
    W = dict(
        mix_norm_even=full["mix_norm_even"].reshape(1, D_MODEL),
        w_in_p=jnp.concatenate([qkv, ag, f, jnp.zeros((D_MODEL, LANE - HEADS), BF16)], axis=1),
        b_f_p=jnp.pad(full["b_f"].reshape(1, HEADS), ((0, 0), (0, LANE - HEADS))),
        conv_w_p=jnp.pad(full["conv_w"].reshape(CONV_K, CONV_CH), ((0, CONV_HALO - CONV_K), (0, 0))),
        conv_b=full["conv_b"].reshape(1, CONV_CH), ln_g=full["ln_g"].reshape(1, CONV_CH),
        ln_b=full["ln_b"].reshape(1, CONV_CH), w_out=g_out,
        mix_norm_odd=full["mix_norm_odd"].reshape(1, D_MODEL), pool_w=g_pool,
        pool_b=full["pool_b"].reshape(1, D_MODEL), pool_scale=full["pool_scale"].reshape(1, D_MODEL),
        ffn_norm=full["ffn_norm"], w_up=g_up,
        ffn_conv_w_p=jnp.pad(full["ffn_conv_w"], ((0, 0), (0, 8 - FFN_K), (0, 0))),
        ffn_conv_b=full["ffn_conv_b"], w_down=g_down, final_norm=full["final_norm"].reshape(1, D_MODEL),
        late_shards=dict(pool_w=shard_of["pool_w"], w_out=shard_of["w_out"],
                         w_up0=shard_of["w_up"][0], w_up1=shard_of["w_up"][1],
                         w_down0=shard_of["w_down"][0], w_down1=shard_of["w_down"][1]))

    seq = x.shape[1]
    n_real = N_META + seq
    LP = -(-n_real // ATT_BLK) * ATT_BLK
    tail = jnp.zeros((LP - n_real, D_MODEL), F32)
    h0 = jnp.concatenate([full["meta_tokens"], x[0], tail], axis=0)
    tgt = jnp.concatenate([jnp.zeros((N_META, D_MODEL), F32), loss_target[0], tail], axis=0)
    loss_loc, dh0, G, parts, others = _local_step2(h0, tgt, W, n_real, cut_of)
    grad_x = dh0[N_META:n_real][None]
    G["meta_tokens"] = dh0[:N_META]

    rep_shapes = {n: w_loc[n].shape for n in REPLICATED}
    G["final_norm"] = G["final_norm"].reshape(D_MODEL)
    rep, loss = _unpack_replicated(_allreduce_small(_pack_replicated(G, loss_loc)), rep_shapes)

    lcuts = [cut_of[n] for n in LATE]
    lfull = [_pack_small_grads(G, small_shards, SMALL_AXES)]
    lrecv = _pair_exchange2(lfull, lcuts, "grad_pair_exchange_late")
    lparts = [_pair_sum2(f, r, cut, PAIR_SUM_BLOCKS[n], "grad_pair_sum_" + n)
              for f, r, cut, n in zip(lfull, lrecv, lcuts, LATE)]
    parts.update(zip(LATE, lparts))
    others.update(zip(LATE, _chip_exchange2(lparts, lcuts)))
    def chip_sum(n, stacked=None):
        return _chip_sum2(parts[n], others[n], cut_of[n], CHIP_SUM_BLOCKS[n], "grad_chip_sum_" + n, stacked=stacked)

    blocks = []
    for n in COMM_ORDER:
        if n in ("w_up", "w_down"):
            blocks.append(chip_sum(n + "1", stacked=(1, 2, chip_sum(n + "0", stacked=(0, 2, None)))))
        else:
            blocks.append(chip_sum(n))
    blocks = _pair_swap2(blocks, cuts)
    gsh = {n: b.reshape(w_loc[n].shape) for n, b in zip(big, blocks[:5])}
    gsh.update(_unpack_small(blocks[5], small_shards, SMALL_AXES))
    sharded = set(big) | set(SMALL_SHARDED)

    grad_w = {n: (gsh[n] if n in sharded else rep[n]) for n in WEIGHT_NAMES}
    delta, new_m, new_v = {}, {}, {}
    for n in WEIGHT_NAMES:
        delta[n], new_m[n], new_v[n] = _adamw(w_loc[n], grad_w[n], m_loc[n], v_loc[n], "adamw_" + n)
    return (loss, grad_x, *[grad_w[n] for n in WEIGHT_NAMES], *[delta[n] for n in WEIGHT_NAMES],
            *[new_m[n] for n in WEIGHT_NAMES], *[new_v[n] for n in WEIGHT_NAMES])
```

```python
import numpy as np
import jax
import jax.numpy as jnp
from jax import lax
from jax.experimental import pallas as pl
from jax.experimental.pallas import tpu as pltpu

F32 = jnp.float32
BF16 = jnp.bfloat16

D_MODEL = 1024
N_META = 16
SEQ = 2048
HEADS = 8
HEAD_DIM = 64
FOX_W = HEADS * HEAD_DIM
CONV_CH = 512
CONV_K = 31
D_FF = 2816
POOL_WINDOWS = (2, 4, 8, 16)
POOL_G = 256
RMS_EPS = 1e-6
LN_EPS = 1e-5
IN_COLS = 3 * FOX_W + HEADS + 2 * CONV_CH
IN_COLS_P = 3 * FOX_W + 2 * CONV_CH + 128
F_COL_BLK = (3 * FOX_W + 2 * CONV_CH) // 128
N_CHIPS = 4
IN_SHARD = IN_COLS // N_CHIPS
UP_SHARD = 2 * D_FF // N_CHIPS
DOWN_SHARD = D_FF // N_CHIPS

ADAM_LR = 0.001
ADAM_B1 = 0.9
ADAM_B2 = 0.999
ADAM_EPS = 1e-08
ADAM_WD = 0.01
ADAM_STEP = 10

LANE = 128
ATT_BLK = 128
VMEM_LIMIT = 56 * 1024 * 1024

NEG = -1e30


def _sigmoid(x):
    return 0.5 * jnp.tanh(0.5 * x) + 0.5


def _sigmoid_tail(x):
    return 1.0 / (1.0 + jnp.exp(-x))


def _params(sem=None):
    return pltpu.CompilerParams(dimension_semantics=sem, vmem_limit_bytes=VMEM_LIMIT)


def _sub_rows(tm):
    best = 8
    for s in range(8, 137, 8):
        if tm % s == 0:
            best = s
    return best


def _mm(a, b, mode, out_dtype, tm, tn, name, add=None, a_lead=None, b_lead=None, out=None, host=None):
    a_shape = a.shape if a_lead is None else a.shape[1:]
    b_shape = b.shape if b_lead is None else b.shape[1:]
    if mode == "nn":
        (M, K), (K2, N) = a_shape, b_shape
        dims = (((1,), (0,)), ((), ()))
        a_blk, a_idx = (tm, K), (lambda i, j: (i, 0))
        b_blk, b_idx = (K, tn), (lambda i, j: (0, j))
    elif mode == "nt":
        (M, K), (N, K2) = a_shape, b_shape
        dims = (((1,), (1,)), ((), ()))
        a_blk, a_idx = (tm, K), (lambda i, j: (i, 0))
        b_blk, b_idx = (tn, K), (lambda i, j: (j, 0))
    else:
        (K, M), (K2, N) = a_shape, b_shape
        dims = (((0,), (0,)), ((), ()))
        a_blk, a_idx = (K, tm), (lambda i, j: (0, i))
        b_blk, b_idx = (K, tn), (lambda i, j: (0, j))
    assert K == K2 and M % tm == 0 and N % tn == 0, (name, a.shape, b.shape, tm, tn)
    gm, gn = M // tm, N // tn
    a_bytes = M * K * a.dtype.itemsize
    b_bytes = N * K * b.dtype.itemsize
    m_outer = a_bytes + b_bytes * gm <= b_bytes + a_bytes * gn
    if m_outer:
        grid = (gm, gn)
        wrap = lambda f: f
    else:
        grid = (gn, gm)
        wrap = lambda f: (lambda j, i: f(i, j))

    def lead(blk, idx, at):
        if at is None:
            return pl.BlockSpec(blk, wrap(idx))
        return pl.BlockSpec((None,) + blk, wrap(lambda i, j: (at,) + idx(i, j)))

    o_idx = lambda i, j: (i, j)
    in_specs = [lead(a_blk, a_idx, a_lead), lead(b_blk, b_idx, b_lead)]
    args = [a, b]
    if add is not None:
        in_specs.append(pl.BlockSpec((tm, tn), wrap(o_idx)))
        args.append(add)
    aliases = {}
    if out is None:
        out_shape = jax.ShapeDtypeStruct((M, N), out_dtype)
        out_spec = pl.BlockSpec((tm, tn), wrap(o_idx))
    else:
        o_lead, n_lead, into = out
        out_shape = jax.ShapeDtypeStruct((n_lead, M, N), out_dtype)
        out_spec = lead((tm, tn), o_idx, o_lead)
        if into is not None:
            aliases = {len(args): 0}
            in_specs.append(pl.BlockSpec(memory_space=pl.ANY))
            args.append(into)
    has_add = add is not None
    n_host = 0 if host is None else len(host[0])
    n_in = len(args)
    scratch = []
    if n_host:
        host_shapes, host_sems, host_ops = _host_plan(host)
        in_specs = in_specs + [pl.BlockSpec(memory_space=pl.ANY)] * n_host
        args = args + list(host[0])
        out_shape = (out_shape,) + host_shapes
        out_spec = (out_spec,) + (pl.BlockSpec(memory_space=pl.ANY),) * n_host
        scratch = [pltpu.SemaphoreType.DMA((host_sems,)), pltpu.SemaphoreType.DMA((host_sems,))]

    def body(*refs):
        a_ref, b_ref = refs[0], refs[1]
        o_ref = refs[n_in + n_host]
        if n_host:
            start, wait = host_ops(refs[n_in:n_in + n_host], refs[n_in + n_host + 1:n_in + 2 * n_host + 1],
                                   refs[n_in + 2 * n_host + 1], refs[n_in + 2 * n_host + 2], host[1])
            pl.when(jnp.logical_and(pl.program_id(0) == 0, pl.program_id(1) == 0))(start)
        x = a_ref[...].astype(BF16)
        y = b_ref[...].astype(BF16)
        acc = lax.dot_general(x, y, dims, preferred_element_type=F32)
        if has_add:
            acc = acc + refs[2][...]
        o_ref[...] = acc.astype(o_ref.dtype)
        if n_host:
            pl.when(jnp.logical_and(pl.program_id(0) == grid[0] - 1, pl.program_id(1) == grid[1] - 1))(wait)

    sem = ("arbitrary", "arbitrary") if n_host else ("parallel", "parallel")
    return pl.pallas_call(
        body, name=name, out_shape=out_shape, grid=grid, in_specs=in_specs, out_specs=out_spec,
        scratch_shapes=scratch, input_output_aliases=aliases, compiler_params=_params(sem))(*args)


def _mm_ffn_dn(dup, w_up, tm, tn, name):
    _, LP, F = dup.shape
    Dm = w_up.shape[0]
    nt = (((1,), (1,)), ((), ()))

    def body(a_ref, b_ref, o_ref):
        acc = lax.dot_general(a_ref[0], b_ref[:, 0:F], nt, preferred_element_type=F32)
        acc = acc + lax.dot_general(a_ref[1], b_ref[:, F:2 * F], nt, preferred_element_type=F32)
        o_ref[...] = acc

    return pl.pallas_call(
        body, name=name, out_shape=jax.ShapeDtypeStruct((LP, Dm), F32), grid=(LP // tm, Dm // tn),
        in_specs=[pl.BlockSpec((2, tm, F), lambda i, j: (0, i, 0)),
                  pl.BlockSpec((tn, 2 * F), lambda i, j: (j, 0))],
        out_specs=pl.BlockSpec((tm, tn), lambda i, j: (i, j)),
        compiler_params=_params(("parallel", "parallel")))(dup, w_up)


def _mm_ffn_dwup(n, dup, tk, tn, name):
    LP, Dm = n.shape
    F = dup.shape[2]
    nct = F // tn
    tdims = (((0,), (0,)), ((), ()))

    def body(a_ref, b_ref, o_ref):
        o_ref[...] = lax.dot_general(a_ref[...], b_ref[...], tdims, preferred_element_type=F32)

    return pl.pallas_call(
        body, name=name, out_shape=jax.ShapeDtypeStruct((Dm, 2 * F), F32), grid=(Dm // tk, 2 * nct),
        in_specs=[pl.BlockSpec((LP, tk), lambda i, j: (0, i)),
                  pl.BlockSpec((None, LP, tn), lambda i, j: (j // nct, 0, j % nct))],
        out_specs=pl.BlockSpec((tk, tn), lambda i, j: (i, j)),
        compiler_params=_params(("parallel", "parallel")))(n, dup)


def _rms_fwd(h, g, out_dtype, tm, name):
    LP, Dm = h.shape

    def body(h_ref, g_ref, o_ref):
        x = h_ref[...]
        r = lax.rsqrt(jnp.mean(x * x, axis=1, keepdims=True) + RMS_EPS)
        o_ref[...] = (x * r * g_ref[...]).astype(o_ref.dtype)

    return pl.pallas_call(
        body, name=name, out_shape=jax.ShapeDtypeStruct((LP, Dm), out_dtype), grid=(LP // tm,),
        in_specs=[pl.BlockSpec((tm, Dm), lambda i: (i, 0)), pl.BlockSpec((1, Dm), lambda i: (0, 0))],
        out_specs=pl.BlockSpec((tm, Dm), lambda i: (i, 0)),
        compiler_params=_params(("parallel",)))(h, g)


def _rms_bwd(h, g, dn, dres, tm, name, host=None):
    LP, Dm = h.shape
    n_host = 0 if host is None else len(host[0])
    if n_host:
        host_shapes, host_sems, host_ops = _host_plan(host)
    nblk = LP // tm

    def body(*refs):
        h_ref, g_ref, dn_ref, dr_ref = refs[:4]
        dh_ref, dg_ref = refs[4 + n_host:6 + n_host]
        if n_host:
            start, wait = host_ops(refs[4:4 + n_host], refs[6 + n_host:6 + 2 * n_host],
                                   refs[6 + 2 * n_host], refs[7 + 2 * n_host], host[1])
            pl.when(pl.program_id(0) == 0)(start)
        i = pl.program_id(0)
        x = h_ref[...]
        r = lax.rsqrt(jnp.mean(x * x, axis=1, keepdims=True) + RMS_EPS)
        xhat = x * r
        dy = dn_ref[...]
        dxh = dy * g_ref[...]
        dh = r * (dxh - xhat * jnp.mean(dxh * xhat, axis=1, keepdims=True))
        dh_ref[...] = dr_ref[...] + dh

        @pl.when(i == 0)
        def _():
            dg_ref[...] = jnp.zeros_like(dg_ref)

        dg_ref[...] += jnp.sum(dy * xhat, axis=0, keepdims=True)
        if n_host:
            pl.when(pl.program_id(0) == nblk - 1)(wait)

    row = pl.BlockSpec((tm, Dm), lambda i: (i, 0))
    vec = pl.BlockSpec((1, Dm), lambda i: (0, 0))
    out_shape = (jax.ShapeDtypeStruct((LP, Dm), F32), jax.ShapeDtypeStruct((1, Dm), F32))
    out_specs = (row, vec)
    in_specs = [row, vec, row, row]
    args = [h, g, dn, dres]
    scratch = []
    if n_host:
        anyspec = pl.BlockSpec(memory_space=pl.ANY)
        in_specs += [anyspec] * n_host
        args += list(host[0])
        out_shape += host_shapes
        out_specs += (anyspec,) * n_host
        scratch = [pltpu.SemaphoreType.DMA((host_sems,)), pltpu.SemaphoreType.DMA((host_sems,))]
    return pl.pallas_call(
        body, name=name, out_shape=out_shape, grid=(nblk,), in_specs=in_specs, out_specs=out_specs,
        scratch_shapes=scratch, compiler_params=_params(("arbitrary",)))(*args)


def _loss_head(h, g, tgt, n_real, tm, name):
    LP, Dm = h.shape

    def body(h_ref, g_ref, t_ref, loss_ref, dh_ref, dg_ref, dhb_ref):
        i = pl.program_id(0)
        x = h_ref[...]
        gg = g_ref[...]
        r = lax.rsqrt(jnp.mean(x * x, axis=1, keepdims=True) + RMS_EPS)
        xhat = x * r
        rows = i * tm + lax.broadcasted_iota(jnp.int32, (tm, 1), 0)
        real = jnp.logical_and(rows >= N_META, rows < n_real)
        diff = jnp.where(real, xhat * gg - t_ref[...], 0.0)
        dy = diff * (1.0 / Dm)
        dxh = dy * gg
        dh = r * (dxh - xhat * jnp.mean(dxh * xhat, axis=1, keepdims=True))
        dh_ref[...] = dh
        dhb_ref[...] = dh.astype(BF16)

        @pl.when(i == 0)
        def _():
            dg_ref[...] = jnp.zeros_like(dg_ref)
            loss_ref[...] = jnp.zeros_like(loss_ref)

        dg_ref[...] += jnp.sum(dy * xhat, axis=0, keepdims=True)
        part = jnp.sum(jnp.sum(diff * diff, axis=1, keepdims=True), axis=0, keepdims=True)
        loss_ref[...] += jnp.broadcast_to(part * (0.5 / Dm), loss_ref.shape)

    row = pl.BlockSpec((tm, Dm), lambda i: (i, 0))
    vec = pl.BlockSpec((1, Dm), lambda i: (0, 0))
    return pl.pallas_call(
        body, name=name,
        out_shape=(jax.ShapeDtypeStruct((1, LANE), F32), jax.ShapeDtypeStruct((LP, Dm), F32),
                   jax.ShapeDtypeStruct((1, Dm), F32), jax.ShapeDtypeStruct((LP, Dm), BF16)),
        grid=(LP // tm,), in_specs=[row, vec, row],
        out_specs=(pl.BlockSpec((1, LANE), lambda i: (0, 0)), row, vec, row),
        compiler_params=_params(("arbitrary",)))(h, g, tgt)


def _fgate_fwd(proj, bf_p, name):
    LP = proj.shape[0]
    nb = LP // LANE

    def body(f_ref, b_ref, c_ref, lf_ref):
        x = f_ref[...] + b_ref[...]
        lf_ref[...] = jnp.minimum(x, 0.0) - jnp.log1p(jnp.exp(-jnp.abs(x)))
        ri = lax.broadcasted_iota(jnp.int32, (LANE, LANE), 0)
        ci = lax.broadcasted_iota(jnp.int32, (LANE, LANE), 1)
        tri = jnp.where(ri >= ci, 1.0, 0.0).astype(F32)

        def blk(i, carry):
            rows = pl.ds(pl.multiple_of(i * LANE, LANE), LANE)
            cb = jnp.dot(tri, lf_ref[rows, :], precision=lax.Precision.HIGHEST,
                         preferred_element_type=F32) + carry
            c_ref[rows, :] = cb
            return cb[LANE - 1:LANE, :]

        lax.fori_loop(0, nb, blk, jnp.zeros((1, LANE), F32))

    return pl.pallas_call(
        body, name=name, out_shape=jax.ShapeDtypeStruct((LP, LANE), F32), grid=(1,),
        in_specs=[pl.BlockSpec((LP, LANE), lambda i: (0, F_COL_BLK)),
                  pl.BlockSpec((1, LANE), lambda i: (0, 0))],
        out_specs=pl.BlockSpec((LP, LANE), lambda i: (0, 0)),
        scratch_shapes=[pltpu.VMEM((LP, LANE), F32)],
        compiler_params=_params(("arbitrary",)))(proj, bf_p)


def _fgate_bwd(proj, bf_p, dc, name):
    LP = proj.shape[0]
    nb = LP // LANE

    def body(f_ref, b_ref, dc_ref, dl_ref, db_ref):
        ri = lax.broadcasted_iota(jnp.int32, (LANE, LANE), 0)
        ci = lax.broadcasted_iota(jnp.int32, (LANE, LANE), 1)
        triu = jnp.where(ri <= ci, 1.0, 0.0).astype(F32)
        bb = b_ref[...]

        tail = jnp.zeros((1, LANE), F32)
        dbs = jnp.zeros((1, LANE), F32)
        for i in range(nb - 1, -1, -1):
            rows = slice(i * LANE, (i + 1) * LANE)
            gb = jnp.dot(triu, dc_ref[rows, :], precision=lax.Precision.HIGHEST,
                         preferred_element_type=F32) + tail
            x = f_ref[rows, :] + bb
            dl = gb * _sigmoid_tail(-x)
            dl_ref[rows, :] = dl.astype(dl_ref.dtype)
            tail = gb[0:1, :]
            dbs = dbs + jnp.sum(dl, axis=0, keepdims=True)
        db_ref[...] = dbs

    return pl.pallas_call(
        body, name=name,
        out_shape=(jax.ShapeDtypeStruct((LP, LANE), BF16), jax.ShapeDtypeStruct((1, LANE), F32)),
        grid=(1,),
        in_specs=[pl.BlockSpec((LP, LANE), lambda i: (0, F_COL_BLK)),
                  pl.BlockSpec((1, LANE), lambda i: (0, 0)),
                  pl.BlockSpec((LP, LANE), lambda i: (0, 0))],
        out_specs=(pl.BlockSpec((LP, LANE), lambda i: (0, 0)), pl.BlockSpec((1, LANE), lambda i: (0, 0))),
        compiler_params=_params(("arbitrary",)))(proj, bf_p, dc)


AUG = 128
ONES_IN_K = HEAD_DIM
ONES_IN_Q = HEAD_DIM + 3
ATT_HEADS_PER_STEP = 8
ATT_HEADS_PER_STEP_BWD = 8


def _attn_prep(proj, c, name):
    LP = proj.shape[0]
    nb = LP // ATT_BLK
    tail_rows = AUG - HEAD_DIM

    def body(q_ref, k_ref, v_ref, c_ref, qT_ref, kT_ref, ka_ref, vT_ref):
        qt = (q_ref[...] * (HEAD_DIM ** -0.5)).T
        kt = k_ref[...].T
        vt = v_ref[...].T
        ct = c_ref[...].T
        hi = ct.astype(BF16).astype(F32)
        r1 = ct - hi
        mid = r1.astype(BF16).astype(F32)
        lo = (r1 - mid).astype(BF16).astype(F32)
        row = lax.broadcasted_iota(jnp.int32, (tail_rows, ATT_BLK), 0)
        ones = jnp.where(row < 3, 1.0, 0.0)
        for h in range(HEADS):
            cparts = jnp.where(row == 0, hi[h:h + 1], jnp.where(row == 1, mid[h:h + 1],
                               jnp.where(row == 2, lo[h:h + 1], 0.0)))
            hs = slice(h * HEAD_DIM, (h + 1) * HEAD_DIM)
            q_tail = cparts + pltpu.roll(ones, 3, 0)
            k_tail = ones - pltpu.roll(cparts, 3, 0)
            qT_ref[h] = jnp.concatenate([qt[hs], q_tail], axis=0).astype(BF16)
            kfull = jnp.concatenate([kt[hs], k_tail], axis=0)
            kT_ref[h] = kfull.astype(BF16)
            ka_ref[h] = kfull.T.astype(BF16)
            vT_ref[h] = vt[hs].astype(BF16)

    col = lambda j: pl.BlockSpec((ATT_BLK, FOX_W), lambda i: (i, j))
    blk = lambda r: pl.BlockSpec((HEADS, None, r, ATT_BLK), lambda i: (0, i, 0, 0))
    return pl.pallas_call(
        body, name=name,
        out_shape=(jax.ShapeDtypeStruct((HEADS, nb, AUG, ATT_BLK), BF16),
                   jax.ShapeDtypeStruct((HEADS, nb, AUG, ATT_BLK), BF16),
                   jax.ShapeDtypeStruct((HEADS, LP, AUG), BF16),
                   jax.ShapeDtypeStruct((HEADS, nb, HEAD_DIM, ATT_BLK), BF16)),
        grid=(nb,), in_specs=[col(0), col(1), col(2), pl.BlockSpec((ATT_BLK, LANE), lambda i: (i, 0))],
        out_specs=(blk(AUG), blk(AUG), pl.BlockSpec((HEADS, ATT_BLK, AUG), lambda i: (0, i, 0)), blk(HEAD_DIM)),
        compiler_params=_params(("parallel",)))(proj, proj, proj, c)


def _attn_rows(xT, scale, out_dtype, name):
    Hh, nb, R, _ = xT.shape

    def body(x_ref, o_ref):
        stack = jnp.concatenate([x_ref[h, 0:HEAD_DIM, :] for h in range(Hh)], axis=0)
        o_ref[...] = (stack * scale).T.astype(o_ref.dtype)

    return pl.pallas_call(
        body, name=name, out_shape=jax.ShapeDtypeStruct((nb * ATT_BLK, Hh * HEAD_DIM), out_dtype), grid=(nb,),
        in_specs=[pl.BlockSpec((Hh, None, R, ATT_BLK), lambda i: (0, i, 0, 0))],
        out_specs=pl.BlockSpec((ATT_BLK, Hh * HEAD_DIM), lambda i: (i, 0)),
        compiler_params=_params(("parallel",)))(xT)


def _attn_cols(x, name):
    LP = x.shape[0]
    nb = LP // ATT_BLK

    def body(x_ref, o_ref):
        xt = x_ref[...].T
        for h in range(HEADS):
            o_ref[h] = xt[h * HEAD_DIM:(h + 1) * HEAD_DIM].astype(o_ref.dtype)

    return pl.pallas_call(
        body, name=name, out_shape=jax.ShapeDtypeStruct((HEADS, nb, HEAD_DIM, ATT_BLK), BF16), grid=(nb,),
        in_specs=[pl.BlockSpec((ATT_BLK, FOX_W), lambda i: (i, 0))],
        out_specs=pl.BlockSpec((HEADS, None, HEAD_DIM, ATT_BLK), lambda i: (0, i, 0, 0)),
        compiler_params=_params(("parallel",)))(x)


def _attn_grads_rows(dqT, dkT, dvT, name):
    Hh, nb, _, _ = dqT.shape
    W3 = Hh * HEAD_DIM

    def body(q_ref, k_ref, v_ref, o_ref, dc_ref):
        for col, (x_ref, scale) in enumerate(((q_ref, HEAD_DIM ** -0.5), (k_ref, 1.0), (v_ref, 1.0))):
            stack = jnp.concatenate([x_ref[h, 0:HEAD_DIM, :] for h in range(Hh)], axis=0)
            o_ref[:, col * W3:(col + 1) * W3] = (stack * scale).T.astype(o_ref.dtype)
        row = lax.broadcasted_iota(jnp.int32, (LANE, ATT_BLK), 0)
        acc = jnp.zeros((LANE, ATT_BLK), F32)
        for h in range(Hh):
            d = q_ref[h, ONES_IN_K:ONES_IN_K + 1, :] - k_ref[h, ONES_IN_Q:ONES_IN_Q + 1, :]
            acc = jnp.where(row == h, d, acc)
        dc_ref[...] = acc.T

    spec = lambda r: pl.BlockSpec((Hh, None, r, ATT_BLK), lambda i: (0, i, 0, 0))
    return pl.pallas_call(
        body, name=name,
        out_shape=(jax.ShapeDtypeStruct((nb * ATT_BLK, 3 * W3), BF16), jax.ShapeDtypeStruct((nb * ATT_BLK, LANE), F32)),
        grid=(nb,), in_specs=[spec(AUG), spec(AUG), spec(HEAD_DIM)],
        out_specs=(pl.BlockSpec((ATT_BLK, 3 * W3), lambda i: (i, 0)), pl.BlockSpec((ATT_BLK, LANE), lambda i: (i, 0))),
        compiler_params=_params(("parallel",)))(dqT, dkT, dvT)


def _attn_fwd2(qT, k_aug, vT, name, comm=None):
    Hh, nb, _, _ = qT.shape
    LP = nb * ATT_BLK
    Dh = vT.shape[2]
    HB = ATT_HEADS_PER_STEP
    n_comm = 0 if comm is None else len(comm[0])

    def body(*refs):
        q_ref, k_ref, v_ref = refs[:3]
        o_ref, lse_ref = refs[3 + n_comm:5 + n_comm]
        if n_comm:
            start, wait = _gather_first_ops(refs[3:3 + n_comm], refs[5 + n_comm:5 + 2 * n_comm],
                                            refs[5 + 2 * n_comm], refs[6 + 2 * n_comm], comm[1])
            pl.when(pl.program_id(0) == 0)(start)
        keys = lax.broadcasted_iota(jnp.int32, (ATT_BLK, ATT_BLK), 0)
        qrys = lax.broadcasted_iota(jnp.int32, (ATT_BLK, ATT_BLK), 1)
        causal = keys <= qrys

        def q_block(i, _):
            def tile(j, carry, masked):
                ks = pl.ds(pl.multiple_of(j * ATT_BLK, ATT_BLK), ATT_BLK)
                s_all = [jnp.dot(k_ref[hh, ks, :], q_ref[hh, i], preferred_element_type=F32) for hh in range(HB)]
                stats, p_all = [], []
                for hh in range(HB):
                    m, l, _ = carry[hh]
                    s = jnp.where(causal, s_all[hh], NEG) if masked else s_all[hh]
                    m_new = jnp.maximum(m, jnp.max(s, axis=0, keepdims=True))
                    p = jnp.exp(s - m_new)
                    alpha = jnp.exp(m - m_new)
                    stats.append((m_new, alpha * l + jnp.sum(p, axis=0, keepdims=True), alpha))
                    p_all.append(p.astype(BF16))
                out = []
                for hh in range(HB):
                    m_new, l, alpha = stats[hh]
                    acc = alpha * carry[hh][2] + jnp.dot(v_ref[hh, j], p_all[hh], preferred_element_type=F32)
                    out.append((m_new, l, acc))
                return tuple(out)

            init = tuple((jnp.full((1, ATT_BLK), NEG, F32), jnp.zeros((1, ATT_BLK), F32),
                          jnp.zeros((Dh, ATT_BLK), F32)) for _ in range(HB))
            carry = lax.fori_loop(0, i, lambda j, cr: tile(j, cr, False), init)
            carry = tile(i, carry, True)
            for hh in range(HB):
                m, l, acc = carry[hh]
                o_ref[hh, i] = acc / l
                lse_ref[hh, i] = m + jnp.log(l)
            return 0

        lax.fori_loop(0, nb, q_block, 0)
        if n_comm:
            pl.when(pl.program_id(0) == Hh // HB - 1)(wait)

    blk = lambda r: pl.BlockSpec((HB, nb, r, ATT_BLK), lambda h: (h, 0, 0, 0))
    out_shape = (jax.ShapeDtypeStruct((Hh, nb, Dh, ATT_BLK), F32), jax.ShapeDtypeStruct((Hh, nb, 1, ATT_BLK), F32))
    scratch = []
    args = [qT, k_aug, vT]
    if n_comm:
        out_shape += tuple(jax.ShapeDtypeStruct(cut.full, s.dtype) for cut, s in zip(comm[1], comm[0]))
        scratch = [pltpu.SemaphoreType.DMA((4 * n_comm,)), pltpu.SemaphoreType.DMA((4 * n_comm,))]
        args += list(comm[0])
    return pl.pallas_call(
        body, name=name, out_shape=out_shape, grid=(Hh // HB,),
        in_specs=[blk(AUG), pl.BlockSpec((HB, LP, AUG), lambda h: (h, 0, 0)), blk(Dh)] + [ANY] * n_comm,
        out_specs=(blk(Dh), blk(1)) + (ANY,) * n_comm, scratch_shapes=scratch,
        compiler_params=_params(("arbitrary",)))(*args)


def _attn_bwd2(qT, kT, k_aug, v, oT, doT, lse, name, comm=None):
    Hh, nb, _, _ = qT.shape
    LP = nb * ATT_BLK
    Dh = v.shape[2]
    nt = (((1,), (1,)), ((), ()))
    tn = (((0,), (0,)), ((), ()))

    HB = ATT_HEADS_PER_STEP_BWD
    n_comm = 0 if comm is None else len(comm[0])

    def body(*refs):
        q_ref, kt_ref, k_ref, v_ref, o_ref, do_ref, lse_ref = refs[:7]
        parts = refs[7:7 + n_comm]
        dq_ref, dk_ref, dv_ref = refs[7 + n_comm:10 + n_comm]
        others = refs[10 + n_comm:10 + 2 * n_comm]
        delta_ref = refs[10 + 2 * n_comm]
        if n_comm:
            start, wait = _chip_exchange_ops(parts, others, refs[11 + 2 * n_comm], refs[12 + 2 * n_comm], comm[1])
            pl.when(pl.program_id(0) == 0)(start)
        keys = lax.broadcasted_iota(jnp.int32, (ATT_BLK, ATT_BLK), 0)
        qrys = lax.broadcasted_iota(jnp.int32, (ATT_BLK, ATT_BLK), 1)
        causal = keys <= qrys

        def prep(i, _):
            for hh in range(HB):
                delta_ref[hh, i] = jnp.sum(do_ref[hh, i].astype(F32) * o_ref[hh, i], axis=0, keepdims=True)
                dq_ref[hh, i] = jnp.zeros((AUG, ATT_BLK), F32)
            return 0

        lax.fori_loop(0, nb, prep, 0)

        def kv_block(j, _):
            ks = pl.ds(pl.multiple_of(j * ATT_BLK, ATT_BLK), ATT_BLK)

            def tile(i, carry, masked):
                s_all = [jnp.dot(k_ref[hh, ks, :], q_ref[hh, i], preferred_element_type=F32) for hh in range(HB)]
                dp_all = [lax.dot_general(v_ref[hh, j], do_ref[hh, i], tn, preferred_element_type=F32)
                          for hh in range(HB)]
                p_all, ds_all = [], []
                for hh in range(HB):
                    s = jnp.where(causal, s_all[hh], NEG) if masked else s_all[hh]
                    p = jnp.exp(s - lse_ref[hh, i])
                    ds_all.append((p * (dp_all[hh] - delta_ref[hh, i])).astype(BF16))
                    p_all.append(p.astype(BF16))
                out = []
                for hh in range(HB):
                    dk, dv = carry[hh]
                    dv = dv + lax.dot_general(do_ref[hh, i], p_all[hh], nt, preferred_element_type=F32)
                    dk = dk + lax.dot_general(q_ref[hh, i], ds_all[hh], nt, preferred_element_type=F32)
                    out.append((dk, dv))
                dq_new = [jnp.dot(kt_ref[hh, j], ds_all[hh], preferred_element_type=F32) for hh in range(HB)]
                for hh in range(HB):
                    dq_ref[hh, i] += dq_new[hh]
                return tuple(out)

            init = tuple((jnp.zeros((AUG, ATT_BLK), F32), jnp.zeros((Dh, ATT_BLK), F32)) for _ in range(HB))
            carry = tile(j, init, True)
            carry = lax.fori_loop(j + 1, nb, lambda i, cr: tile(i, cr, False), carry)
            for hh in range(HB):
                dk_ref[hh, j] = carry[hh][0]
                dv_ref[hh, j] = carry[hh][1]
            return 0

        lax.fori_loop(0, nb, kv_block, 0)
        if n_comm:
            pl.when(pl.program_id(0) == Hh // HB - 1)(wait)

    blk = lambda r: pl.BlockSpec((HB, nb, r, ATT_BLK), lambda h: (h, 0, 0, 0))
    row = lambda cols: pl.BlockSpec((HB, LP, cols), lambda h: (h, 0, 0))
    out_shape = (jax.ShapeDtypeStruct((Hh, nb, AUG, ATT_BLK), F32), jax.ShapeDtypeStruct((Hh, nb, AUG, ATT_BLK), F32),
                 jax.ShapeDtypeStruct((Hh, nb, Dh, ATT_BLK), F32))
    scratch = [pltpu.VMEM((HB, nb, 1, ATT_BLK), F32)]
    args = [qT, kT, k_aug, v, oT, doT, lse]
    if n_comm:
        out_shape += _chip_exchange_shapes(*comm)
        scratch += [pltpu.SemaphoreType.DMA((3 * n_comm,)), pltpu.SemaphoreType.DMA((3 * n_comm,))]
        args += list(comm[0])
    return pl.pallas_call(
        body, name=name, out_shape=out_shape, grid=(Hh // HB,),
        in_specs=[blk(AUG), blk(AUG), row(AUG), blk(Dh), blk(Dh), blk(Dh), blk(1)] + [ANY] * n_comm,
        out_specs=(blk(AUG), blk(AUG), blk(Dh)) + (ANY,) * n_comm,
        scratch_shapes=scratch,
        compiler_params=_params(("arbitrary",)))(*args)


CONV_HALO = 32
A_BLK = 3 * FOX_W // CONV_CH
G_BLK = A_BLK + 1


def _conf_fwd(proj, cw, cb, lg, lb, tm, name):
    LP = proj.shape[0]
    C = CONV_CH
    sub = _sub_rows(tm)
    hpb = tm // CONV_HALO

    def body(a_ref, g_ref, ah_ref, gh_ref, w_ref, cb_ref, lg_ref, lb_ref, u1_ref, u_ref, buf):
        r = pl.program_id(0)
        buf[CONV_HALO:CONV_HALO + tm, :] = a_ref[...] * _sigmoid(g_ref[...])
        buf[0:CONV_HALO, :] = jnp.where(r > 0, ah_ref[...] * _sigmoid(gh_ref[...]), 0.0)
        for s in range(tm // sub):
            for ct in range(C // LANE):
                ln = slice(ct * LANE, (ct + 1) * LANE)
                acc = jnp.broadcast_to(cb_ref[:, ln], (sub, LANE))
                for kk in range(CONV_K):
                    off = CONV_HALO + s * sub - (CONV_K - 1) + kk
                    acc = acc + w_ref[kk:kk + 1, ln] * buf[off:off + sub, ln]
                u1_ref[s * sub:(s + 1) * sub, ln] = acc
        u1 = u1_ref[...]
        mu = jnp.mean(u1, axis=1, keepdims=True)
        xc = u1 - mu
        var = jnp.mean(xc * xc, axis=1, keepdims=True)
        y = xc * lax.rsqrt(var + LN_EPS) * lg_ref[...] + lb_ref[...]
        u_ref[...] = (y * _sigmoid(y)).astype(u_ref.dtype)

    cur = lambda blk: pl.BlockSpec((tm, C), lambda r: (r, blk))
    halo = lambda blk: pl.BlockSpec((CONV_HALO, C), lambda r: (jnp.maximum(r * hpb - 1, 0), blk))
    vec = pl.BlockSpec((1, C), lambda r: (0, 0))
    out = pl.BlockSpec((tm, C), lambda r: (r, 0))
    return pl.pallas_call(
        body, name=name,
        out_shape=(jax.ShapeDtypeStruct((LP, C), F32), jax.ShapeDtypeStruct((LP, C), BF16)),
        grid=(LP // tm,),
        in_specs=[cur(A_BLK), cur(G_BLK), halo(A_BLK), halo(G_BLK),
                  pl.BlockSpec((CONV_HALO, C), lambda r: (0, 0)), vec, vec, vec],
        out_specs=(out, out),
        scratch_shapes=[pltpu.VMEM((CONV_HALO + tm, C), F32)],
        compiler_params=_params(("parallel",)))(proj, proj, proj, proj, cw, cb, lg, lb)


def _conf_bwd(proj, u1, dcat, cw, lg, lb, tm, name, host=None):
    LP = proj.shape[0]
    C = CONV_CH
    sub = _sub_rows(tm)
    hpb = tm // CONV_HALO
    nblk = LP // tm
    last_halo = LP // CONV_HALO - 1

    n_host = 0 if host is None else len(host[0])
    if n_host:
        host_shapes, host_sems, host_ops = _host_plan(host)

    def body(*refs):
        a_ref, g_ref, ah_ref, gh_ref, u1_ref, u1n_ref, du_ref, dun_ref, w_ref, lg_ref, lb_ref = refs[:11]
        dadg_ref, dw_ref, dcb_ref, dlg_ref, dlb_ref = refs[11 + n_host:16 + n_host]
        ubuf, dbuf, du0 = refs[16 + 2 * n_host:19 + 2 * n_host]
        if n_host:
            start, wait = host_ops(refs[11:11 + n_host], refs[16 + n_host:16 + 2 * n_host],
                                   refs[19 + 2 * n_host], refs[20 + 2 * n_host], host[1])
            pl.when(pl.program_id(0) == 0)(start)
        r = pl.program_id(0)
        lgv = lg_ref[...]
        lbv = lb_ref[...]

        def ln_silu_bwd(u1v, duv):
            mu = jnp.mean(u1v, axis=1, keepdims=True)
            xc = u1v - mu
            rstd = lax.rsqrt(jnp.mean(xc * xc, axis=1, keepdims=True) + LN_EPS)
            xhat = xc * rstd
            y = xhat * lgv + lbv
            sg = _sigmoid(y)
            dy = duv * (sg * (1.0 + y * (1.0 - sg)))
            dxh = dy * lgv
            du1 = rstd * (dxh - jnp.mean(dxh, axis=1, keepdims=True)
                          - xhat * jnp.mean(dxh * xhat, axis=1, keepdims=True))
            return du1, dy, xhat

        @pl.when(r == 0)
        def _():
            dw_ref[...] = jnp.zeros_like(dw_ref)
            dcb_ref[...] = jnp.zeros_like(dcb_ref)
            dlg_ref[...] = jnp.zeros_like(dlg_ref)
            dlb_ref[...] = jnp.zeros_like(dlb_ref)

        du1, dy, xhat = ln_silu_bwd(u1_ref[...], du_ref[...])
        dlg_ref[...] += jnp.sum(dy * xhat, axis=0, keepdims=True)
        dlb_ref[...] += jnp.sum(dy, axis=0, keepdims=True)
        dcb_ref[...] += jnp.sum(du1, axis=0, keepdims=True)
        dbuf[0:tm, :] = du1
        du1n, _, _ = ln_silu_bwd(u1n_ref[...], dun_ref[...])
        dbuf[tm:tm + CONV_HALO, :] = jnp.where(r < nblk - 1, du1n, 0.0)
        ubuf[CONV_HALO:CONV_HALO + tm, :] = a_ref[...] * _sigmoid(g_ref[...])
        ubuf[0:CONV_HALO, :] = jnp.where(r > 0, ah_ref[...] * _sigmoid(gh_ref[...]), 0.0)

        for ct in range(C // LANE):
            ln = slice(ct * LANE, (ct + 1) * LANE)
            for s in range(tm // sub):
                d_here = dbuf[s * sub:(s + 1) * sub, ln]
                acc = jnp.zeros((sub, LANE), F32)
                for kk in range(CONV_K):
                    fo = s * sub + (CONV_K - 1) - kk
                    acc = acc + w_ref[kk:kk + 1, ln] * dbuf[fo:fo + sub, ln]
                    bo = CONV_HALO + s * sub - (CONV_K - 1) + kk
                    dw_ref[kk:kk + 1, ln] += jnp.sum(d_here * ubuf[bo:bo + sub, ln], axis=0, keepdims=True)
                du0[s * sub:(s + 1) * sub, ln] = acc
        a = a_ref[...]
        sg = _sigmoid(g_ref[...])
        d0 = du0[...]
        dadg_ref[:, 0:C] = (d0 * sg).astype(dadg_ref.dtype)
        dadg_ref[:, C:2 * C] = (d0 * a * sg * (1.0 - sg)).astype(dadg_ref.dtype)
        if n_host:
            pl.when(pl.program_id(0) == nblk - 1)(wait)

    cur = lambda blk: pl.BlockSpec((tm, C), lambda r: (r, blk))
    prev = lambda blk: pl.BlockSpec((CONV_HALO, C), lambda r: (jnp.maximum(r * hpb - 1, 0), blk))
    nxt = lambda blk: pl.BlockSpec((CONV_HALO, C), lambda r: (jnp.minimum((r + 1) * hpb, last_halo), blk))
    vec = pl.BlockSpec((1, C), lambda r: (0, 0))
    wspec = pl.BlockSpec((CONV_HALO, C), lambda r: (0, 0))
    out_shape = (jax.ShapeDtypeStruct((LP, 2 * C), BF16), jax.ShapeDtypeStruct((CONV_HALO, C), F32),
                 jax.ShapeDtypeStruct((1, C), F32), jax.ShapeDtypeStruct((1, C), F32),
                 jax.ShapeDtypeStruct((1, C), F32))
    out_specs = (pl.BlockSpec((tm, 2 * C), lambda r: (r, 0)), wspec, vec, vec, vec)
    in_specs = [cur(A_BLK), cur(G_BLK), prev(A_BLK), prev(G_BLK), cur(0), nxt(0), cur(1), nxt(1), wspec, vec, vec]
    args = [proj, proj, proj, proj, u1, u1, dcat, dcat, cw, lg, lb]
    scratch = [pltpu.VMEM((CONV_HALO + tm, C), F32), pltpu.VMEM((tm + CONV_HALO, C), F32),
               pltpu.VMEM((tm, C), F32)]
    if n_host:
        anyspec = pl.BlockSpec(memory_space=pl.ANY)
        in_specs += [anyspec] * n_host
        args += list(host[0])
        out_shape += host_shapes
        out_specs += (anyspec,) * n_host
        scratch += [pltpu.SemaphoreType.DMA((host_sems,)), pltpu.SemaphoreType.DMA((host_sems,))]
    return pl.pallas_call(
        body, name=name, out_shape=out_shape, grid=(nblk,), in_specs=in_specs, out_specs=out_specs,
        scratch_shapes=scratch, compiler_params=_params(("arbitrary",)))(*args)


FFN_HALO = 8
FFN_TC = 1408
FFN_ROW_SPLIT = 1
FFN_K = 3


def _ffn_conv(buf, w_ref, b_ref, s, sub, ln):
    acc = jnp.broadcast_to(b_ref[:, ln], (sub, LANE))
    for kk in range(FFN_K):
        off = FFN_HALO + s * sub - (FFN_K - 1) + kk
        acc = acc + w_ref[kk:kk + 1, ln] * buf[off:off + sub, ln]
    return acc


def _ffn_act_fwd(up, w, b, tm, name, host=None):
    LP, F = up.shape[0], up.shape[1] // 2
    tm = tm // FFN_ROW_SPLIT
    upg = upv = up
    nct = F // FFN_TC
    sub = _sub_rows(tm)
    hpb = tm // FFN_HALO
    n_host = 0 if host is None else len(host[0])
    nrb = LP // tm

    def body(*refs):
        g_ref, v_ref, gh_ref, vh_ref, wg_ref, wv_ref, bg_ref, bv_ref = refs[:8]
        act_ref = refs[8 + n_host]
        gbuf, vbuf = refs[9 + 2 * n_host:11 + 2 * n_host]
        if n_host:
            start, wait = _gather_first_ops(refs[8:8 + n_host], refs[9 + n_host:9 + 2 * n_host],
                                            refs[11 + 2 * n_host], refs[12 + 2 * n_host], host[1])
            pl.when(jnp.logical_and(pl.program_id(0) == 0, pl.program_id(1) == 0))(start)
        r = pl.program_id(1)
        gbuf[FFN_HALO:FFN_HALO + tm, :] = g_ref[...]
        vbuf[FFN_HALO:FFN_HALO + tm, :] = v_ref[...]
        gbuf[0:FFN_HALO, :] = jnp.where(r > 0, gh_ref[...], 0.0)
        vbuf[0:FFN_HALO, :] = jnp.where(r > 0, vh_ref[...], 0.0)
        for s in range(tm // sub):
            for ct in range(FFN_TC // LANE):
                ln = slice(ct * LANE, (ct + 1) * LANE)
                gc = _ffn_conv(gbuf, wg_ref, bg_ref, s, sub, ln)
                vc = _ffn_conv(vbuf, wv_ref, bv_ref, s, sub, ln)
                act_ref[s * sub:(s + 1) * sub, ln] = (gc * _sigmoid(gc) * vc).astype(act_ref.dtype)
        if n_host:
            pl.when(jnp.logical_and(pl.program_id(0) == nct - 1, pl.program_id(1) == nrb - 1))(wait)

    cur = pl.BlockSpec((tm, FFN_TC), lambda c, r: (r, c))
    halo = pl.BlockSpec((FFN_HALO, FFN_TC), lambda c, r: (jnp.maximum(r * hpb - 1, 0), c))
    wg = pl.BlockSpec((8, FFN_TC), lambda c, r: (0, c))
    wv = pl.BlockSpec((8, FFN_TC), lambda c, r: (0, nct + c))
    bg = pl.BlockSpec((1, FFN_TC), lambda c, r: (0, c))
    bv = pl.BlockSpec((1, FFN_TC), lambda c, r: (0, nct + c))
    curv = pl.BlockSpec((tm, FFN_TC), lambda c, r: (r, nct + c))
    halov = pl.BlockSpec((FFN_HALO, FFN_TC), lambda c, r: (jnp.maximum(r * hpb - 1, 0), nct + c))
    out_shape = jax.ShapeDtypeStruct((LP, F), BF16)
    out_specs = cur
    in_specs = [cur, curv, halo, halov, wg, wv, bg, bv]
    args = [upg, upv, upg, upv, w, w, b, b]
    scratch = [pltpu.VMEM((FFN_HALO + tm, FFN_TC), F32)] * 2
    if n_host:
        anyspec = pl.BlockSpec(memory_space=pl.ANY)
        in_specs += [anyspec] * n_host
        args += list(host[0])
        out_shape = (out_shape,) + tuple(jax.ShapeDtypeStruct(cut.full, s.dtype) for cut, s in zip(host[1], host[0]))
        out_specs = (cur,) + (anyspec,) * n_host
        scratch += [pltpu.SemaphoreType.DMA((4 * n_host,)), pltpu.SemaphoreType.DMA((4 * n_host,))]
    sem = ("arbitrary", "arbitrary") if n_host else ("parallel", "parallel")
    return pl.pallas_call(
        body, name=name, out_shape=out_shape, grid=(nct, nrb), in_specs=in_specs, out_specs=out_specs,
        scratch_shapes=scratch, compiler_params=_params(sem))(*args)


def _ffn_act_bwd(up, dact, w, b, tm, name, comm=None):
    LP, F = up.shape[0], up.shape[1] // 2
    tm = tm // FFN_ROW_SPLIT
    upg = upv = up
    nct = F // FFN_TC
    sub = _sub_rows(tm)
    hpb = tm // FFN_HALO
    nblk = LP // tm
    last_halo = LP // FFN_HALO - 1
    TB = tm + 2 * FFN_HALO
    n_comm = 0 if comm is None else len(comm[0])

    def body(*refs):
        (g_ref, v_ref, gp_ref, vp_ref, gn_ref, vn_ref, da_ref, dan_ref,
         wg_ref, wv_ref, bg_ref, bv_ref) = refs[:12]
        dup_ref, dwg_ref, dwv_ref, dbg_ref, dbv_ref = refs[12 + n_comm:17 + n_comm]
        gbuf, vbuf, dgb, dvb = refs[17 + 2 * n_comm:21 + 2 * n_comm]
        if n_comm:
            start, wait = _chip_exchange_ops(refs[12:12 + n_comm], refs[17 + n_comm:17 + 2 * n_comm],
                                             refs[21 + 2 * n_comm], refs[22 + 2 * n_comm], comm[1])
            pl.when(jnp.logical_and(pl.program_id(0) == 0, pl.program_id(1) == 0))(start)
        r = pl.program_id(1)
        dg_ref = dup_ref.at[0]
        dv_ref = dup_ref.at[1]
        first = r == 0
        last = r == nblk - 1

        @pl.when(first)
        def _():
            dwg_ref[...] = jnp.zeros_like(dwg_ref)
            dwv_ref[...] = jnp.zeros_like(dwv_ref)
            dbg_ref[...] = jnp.zeros_like(dbg_ref)
            dbv_ref[...] = jnp.zeros_like(dbv_ref)

        for buf, c_ref, p_ref, n_ref in ((gbuf, g_ref, gp_ref, gn_ref), (vbuf, v_ref, vp_ref, vn_ref)):
            buf[0:FFN_HALO, :] = jnp.where(first, 0.0, p_ref[...])
            buf[FFN_HALO:FFN_HALO + tm, :] = c_ref[...]
            buf[FFN_HALO + tm:TB, :] = jnp.where(last, 0.0, n_ref[...])

        def dconv(s0, nrows, ln, dact_v):
            xg = [gbuf[s0 - (FFN_K - 1) + kk:s0 - (FFN_K - 1) + kk + nrows, ln] for kk in range(FFN_K)]
            xv = [vbuf[s0 - (FFN_K - 1) + kk:s0 - (FFN_K - 1) + kk + nrows, ln] for kk in range(FFN_K)]
            gc = jnp.broadcast_to(bg_ref[:, ln], (nrows, LANE))
            vc = jnp.broadcast_to(bv_ref[:, ln], (nrows, LANE))
            for kk in range(FFN_K):
                gc = gc + wg_ref[kk:kk + 1, ln] * xg[kk]
                vc = vc + wv_ref[kk:kk + 1, ln] * xv[kk]
            sg = _sigmoid(gc)
            return dact_v * vc * (sg * (1.0 + gc * (1.0 - sg))), dact_v * (gc * sg), xg, xv

        colsum = lambda t: jnp.sum(t, axis=0, keepdims=True)
        for ct in range(FFN_TC // LANE):
            ln = slice(ct * LANE, (ct + 1) * LANE)
            zero = jnp.zeros((1, LANE), F32)
            dwg, dwv, dbg, dbv = [zero] * FFN_K, [zero] * FFN_K, zero, zero
            for s in range(tm // sub):
                dgc, dvc, xg, xv = dconv(FFN_HALO + s * sub, sub, ln, da_ref[s * sub:(s + 1) * sub, ln])
                dgb[s * sub:(s + 1) * sub, ln] = dgc
                dvb[s * sub:(s + 1) * sub, ln] = dvc
                dwg = [dwg[kk] + colsum(dgc * xg[kk]) for kk in range(FFN_K)]
                dwv = [dwv[kk] + colsum(dvc * xv[kk]) for kk in range(FFN_K)]
                dbg, dbv = dbg + colsum(dgc), dbv + colsum(dvc)
            for kk in range(FFN_K):
                dwg_ref[kk:kk + 1, ln] += dwg[kk]
                dwv_ref[kk:kk + 1, ln] += dwv[kk]
            dbg_ref[:, ln] += dbg
            dbv_ref[:, ln] += dbv
            dgc, dvc, _, _ = dconv(FFN_HALO + tm, FFN_HALO, ln, jnp.where(last, 0.0, dan_ref[:, ln]))
            dgb[tm:tm + FFN_HALO, ln] = dgc
            dvb[tm:tm + FFN_HALO, ln] = dvc
            for dbuf, w_ref, dout in ((dgb, wg_ref, dg_ref), (dvb, wv_ref, dv_ref)):
                for s in range(tm // sub):
                    acc = jnp.zeros((sub, LANE), F32)
                    for kk in range(FFN_K):
                        fo = s * sub + (FFN_K - 1) - kk
                        acc = acc + w_ref[kk:kk + 1, ln] * dbuf[fo:fo + sub, ln]
                    dout[s * sub:(s + 1) * sub, ln] = acc.astype(dout.dtype)
        if n_comm:
            pl.when(jnp.logical_and(pl.program_id(0) == nct - 1, pl.program_id(1) == nblk - 1))(wait)

    cur = pl.BlockSpec((tm, FFN_TC), lambda c, r: (r, c))
    prev = pl.BlockSpec((FFN_HALO, FFN_TC), lambda c, r: (jnp.maximum(r * hpb - 1, 0), c))
    nxt = pl.BlockSpec((FFN_HALO, FFN_TC), lambda c, r: (jnp.minimum((r + 1) * hpb, last_halo), c))
    wg = pl.BlockSpec((8, FFN_TC), lambda c, r: (0, c))
    wv = pl.BlockSpec((8, FFN_TC), lambda c, r: (0, nct + c))
    bg = pl.BlockSpec((1, FFN_TC), lambda c, r: (0, c))
    bv = pl.BlockSpec((1, FFN_TC), lambda c, r: (0, nct + c))
    curv = pl.BlockSpec((tm, FFN_TC), lambda c, r: (r, nct + c))
    prevv = pl.BlockSpec((FFN_HALO, FFN_TC), lambda c, r: (jnp.maximum(r * hpb - 1, 0), nct + c))
    nxtv = pl.BlockSpec((FFN_HALO, FFN_TC), lambda c, r: (jnp.minimum((r + 1) * hpb, last_halo), nct + c))
    out_shape = (jax.ShapeDtypeStruct((2, LP, F), BF16),
                 jax.ShapeDtypeStruct((8, F), F32), jax.ShapeDtypeStruct((8, F), F32),
                 jax.ShapeDtypeStruct((1, F), F32), jax.ShapeDtypeStruct((1, F), F32))
    out_specs = (pl.BlockSpec((2, tm, FFN_TC), lambda c, r: (0, r, c)),
                 pl.BlockSpec((8, FFN_TC), lambda c, r: (0, c)),
                 pl.BlockSpec((8, FFN_TC), lambda c, r: (0, c)),
                 pl.BlockSpec((1, FFN_TC), lambda c, r: (0, c)),
                 pl.BlockSpec((1, FFN_TC), lambda c, r: (0, c)))
    in_specs = [cur, curv, prev, prevv, nxt, nxtv, cur, nxt, wg, wv, bg, bv]
    args = [upg, upv, upg, upv, upg, upv, dact, dact, w, w, b, b]
    scratch = [pltpu.VMEM((TB, FFN_TC), F32), pltpu.VMEM((TB, FFN_TC), F32),
               pltpu.VMEM((tm + FFN_HALO, FFN_TC), F32), pltpu.VMEM((tm + FFN_HALO, FFN_TC), F32)]
    if n_comm:
        anyspec = pl.BlockSpec(memory_space=pl.ANY)
        in_specs += [anyspec] * n_comm
        args += list(comm[0])
        out_shape += _chip_exchange_shapes(*comm)
        out_specs += (anyspec,) * n_comm
        scratch += [pltpu.SemaphoreType.DMA((3 * n_comm,)), pltpu.SemaphoreType.DMA((3 * n_comm,))]
    sem = ("arbitrary", "arbitrary") if n_comm else ("parallel", "arbitrary")
    dup, dwg, dwv, dbg, dbv, *others = pl.pallas_call(
        body, name=name, out_shape=out_shape, grid=(nct, nblk), in_specs=in_specs, out_specs=out_specs,
        scratch_shapes=scratch, compiler_params=_params(sem))(*args)
    return (dup, jnp.concatenate([dwg, dwv], axis=1), jnp.concatenate([dbg, dbv], axis=1)) + tuple(others)


POOL_HALO = 16


def _pool_fwd(h, g, pw, pb, ps, tm, name):
    LP, Dm = h.shape
    sub = _sub_rows(tm)
    hpb = tm // POOL_HALO

    def body(h_ref, hh_ref, g_ref, pw_ref, pb_ref, ps_ref, o_ref, d_ref, buf):
        r = pl.program_id(0)
        gg = g_ref[...]

        def norm(x):
            return x * lax.rsqrt(jnp.mean(x * x, axis=1, keepdims=True) + RMS_EPS) * gg

        x = h_ref[...]
        buf[POOL_HALO:POOL_HALO + tm, :] = norm(x)
        buf[0:POOL_HALO, :] = jnp.where(r > 0, norm(hh_ref[...]), 0.0)
        for gi, w in enumerate(POOL_WINDOWS):
            ln = slice(gi * POOL_G, (gi + 1) * POOL_G)
            for s in range(tm // sub):
                base = POOL_HALO + s * sub
                acc = buf[base:base + sub, ln]
                for jj in range(1, w):
                    acc = acc + buf[base - jj:base - jj + sub, ln]
                t = r * tm + s * sub + lax.broadcasted_iota(jnp.int32, (sub, 1), 0)
                cnt = jnp.minimum(t + 1, w).astype(F32)
                d_ref[s * sub:(s + 1) * sub, ln] = (acc / cnt - buf[base:base + sub, ln]).astype(d_ref.dtype)
            y = jnp.dot(d_ref[:, ln], pw_ref[gi], preferred_element_type=F32) + pb_ref[:, ln]
            o_ref[:, ln] = x[:, ln] + y * ps_ref[:, ln]

    row = pl.BlockSpec((tm, Dm), lambda r: (r, 0))
    halo = pl.BlockSpec((POOL_HALO, Dm), lambda r: (jnp.maximum(r * hpb - 1, 0), 0))
    vec = pl.BlockSpec((1, Dm), lambda r: (0, 0))
    wsp = pl.BlockSpec((len(POOL_WINDOWS), POOL_G, POOL_G), lambda r: (0, 0, 0))
    return pl.pallas_call(
        body, name=name,
        out_shape=(jax.ShapeDtypeStruct((LP, Dm), F32), jax.ShapeDtypeStruct((LP, Dm), BF16)),
        grid=(LP // tm,), in_specs=[row, halo, vec, wsp, vec, vec], out_specs=(row, row),
        scratch_shapes=[pltpu.VMEM((POOL_HALO + tm, Dm), F32)],
        compiler_params=_params(("parallel",)))(h, h, g, pw, pb, ps)


def _pool_bwd(h, g, d, pw, pb, ps, dh_out, tm, name):
    LP, Dm = h.shape
    sub = _sub_rows(tm)
    hpb = tm // POOL_HALO
    nblk = LP // tm
    last_halo = LP // POOL_HALO - 1
    nt = (((1,), (1,)), ((), ()))
    tn = (((0,), (0,)), ((), ()))

    def body(h_ref, g_ref, d_ref, pw_ref, pb_ref, ps_ref, do_ref, don_ref,
             dh_ref, dpw_ref, dpb_ref, dps_ref, dg_ref, dhb_ref, ebuf, ddb, dnb):
        r = pl.program_id(0)

        @pl.when(r == 0)
        def _():
            dpw_ref[...] = jnp.zeros_like(dpw_ref)
            dpb_ref[...] = jnp.zeros_like(dpb_ref)
            dps_ref[...] = jnp.zeros_like(dps_ref)
            dg_ref[...] = jnp.zeros_like(dg_ref)

        for gi, w in enumerate(POOL_WINDOWS):
            ln = slice(gi * POOL_G, (gi + 1) * POOL_G)
            wg = pw_ref[gi]
            dog = do_ref[:, ln]
            dg_b = d_ref[:, ln]
            y_pre = jnp.dot(dg_b, wg, preferred_element_type=F32) + pb_ref[:, ln]
            dps_ref[:, ln] += jnp.sum(dog * y_pre, axis=0, keepdims=True)
            dy = dog * ps_ref[:, ln]
            dpb_ref[:, ln] += jnp.sum(dy, axis=0, keepdims=True)
            dyb = dy.astype(BF16)
            dpw_ref[gi] += lax.dot_general(dg_b, dyb, tn, preferred_element_type=F32)
            dd = lax.dot_general(dyb, wg, nt, preferred_element_type=F32)
            ddb[:, ln] = dd
            t = r * tm + lax.broadcasted_iota(jnp.int32, (tm, 1), 0)
            ebuf[0:tm, ln] = dd / jnp.minimum(t + 1, w).astype(F32)
            dyn = (don_ref[:, ln] * ps_ref[:, ln]).astype(BF16)
            ddn = lax.dot_general(dyn, wg, nt, preferred_element_type=F32)
            tn_ = (r + 1) * tm + lax.broadcasted_iota(jnp.int32, (POOL_HALO, 1), 0)
            ebuf[tm:tm + POOL_HALO, ln] = jnp.where(r < nblk - 1, ddn / jnp.minimum(tn_ + 1, w).astype(F32), 0.0)
            for s in range(tm // sub):
                acc = ebuf[s * sub:(s + 1) * sub, ln]
                for jj in range(1, w):
                    acc = acc + ebuf[s * sub + jj:s * sub + jj + sub, ln]
                dnb[s * sub:(s + 1) * sub, ln] = acc - ddb[s * sub:(s + 1) * sub, ln]
        x = h_ref[...]
        rr = lax.rsqrt(jnp.mean(x * x, axis=1, keepdims=True) + RMS_EPS)
        xhat = x * rr
        dn = dnb[...]
        dxh = dn * g_ref[...]
        dh = do_ref[...] + rr * (dxh - xhat * jnp.mean(dxh * xhat, axis=1, keepdims=True))
        dh_ref[...] = dh
        dhb_ref[...] = dh.astype(BF16)
        dg_ref[...] += jnp.sum(dn * xhat, axis=0, keepdims=True)

    row = pl.BlockSpec((tm, Dm), lambda r: (r, 0))
    nxt = pl.BlockSpec((POOL_HALO, Dm), lambda r: (jnp.minimum((r + 1) * hpb, last_halo), 0))
    vec = pl.BlockSpec((1, Dm), lambda r: (0, 0))
    wsp = pl.BlockSpec((len(POOL_WINDOWS), POOL_G, POOL_G), lambda r: (0, 0, 0))
    return pl.pallas_call(
        body, name=name,
        out_shape=(jax.ShapeDtypeStruct((LP, Dm), F32),
                   jax.ShapeDtypeStruct((len(POOL_WINDOWS), POOL_G, POOL_G), F32),
                   jax.ShapeDtypeStruct((1, Dm), F32), jax.ShapeDtypeStruct((1, Dm), F32),
                   jax.ShapeDtypeStruct((1, Dm), F32), jax.ShapeDtypeStruct((LP, Dm), BF16)),
        grid=(nblk,), in_specs=[row, vec, row, wsp, vec, vec, row, nxt],
        out_specs=(row, wsp, vec, vec, vec, row),
        scratch_shapes=[pltpu.VMEM((tm + POOL_HALO, Dm), F32), pltpu.VMEM((tm, Dm), F32),
                        pltpu.VMEM((tm, Dm), F32)],
        compiler_params=_params(("arbitrary",)))(h, g, d, pw, pb, ps, dh_out, dh_out)


def _adamw(w, g, m, v, name):
    shape = w.shape
    cols = shape[-1]
    rows = int(np.prod(shape[:-1])) if len(shape) > 1 else 1
    w2, g2, m2, v2 = (t.reshape(rows, cols) for t in (w, g, m, v))
    tr = rows
    for cand in (256, 128, 64, 32, 16, 8):
        if rows % cand == 0 and rows > cand:
            tr = cand
            break
    c1 = float(1.0 - ADAM_B1 ** ADAM_STEP)
    c2 = float(1.0 - ADAM_B2 ** ADAM_STEP)

    def body(w_ref, g_ref, m_ref, v_ref, d_ref, mo_ref, vo_ref):
        gg = g_ref[...]
        mn = ADAM_B1 * m_ref[...] + (1.0 - ADAM_B1) * gg
        vn = ADAM_B2 * v_ref[...] + (1.0 - ADAM_B2) * (gg * gg)
        m_hat = mn / c1
        v_hat = vn / c2
        d_ref[...] = -ADAM_LR * (m_hat / (jnp.sqrt(v_hat) + ADAM_EPS) + ADAM_WD * w_ref[...])
        mo_ref[...] = mn
        vo_ref[...] = vn

    spec = pl.BlockSpec((tr, cols), lambda i: (i, 0))
    sds = jax.ShapeDtypeStruct((rows, cols), F32)
    d2, mo, vo = pl.pallas_call(
        body, name=name, out_shape=(sds, sds, sds), grid=(rows // tr,),
        in_specs=[spec] * 4, out_specs=(spec,) * 3,
        compiler_params=_params(("parallel",)))(w2, g2, m2, v2)
    return d2.reshape(shape), mo.reshape(shape), vo.reshape(shape)


def _row_tiles(LP):
    tm = LP // 4
    assert LP % 4 == 0 and tm % CONV_HALO == 0 and LP % ATT_BLK == 0, LP
    return tm, LP // 2


MESH = pl.DeviceIdType.MESH
ANY = pl.BlockSpec(memory_space=pl.ANY)


def _coords():
    return lax.axis_index("x"), lax.axis_index("y"), lax.axis_index("c")


def _other_chips(x, y):
    return [(1 - x, y), (x, 1 - y), (1 - x, 1 - y)]


def _allreduce_small(pack):
    Rs, C = pack.shape
    n_dev = 8

    def body(x_ref, o_ref, buf, send_sems, recv_sems):
        x, y, c = _coords()
        me = 4 * x + 2 * y + c
        buf[me] = x_ref[...]
        peers = []
        for rel in range(1, n_dev):
            px = 1 - x if rel & 4 else x
            py = 1 - y if rel & 2 else y
            pc = 1 - c if rel & 1 else c
            peers.append((px, py, pc))
        sends = [pltpu.make_async_remote_copy(
            src_ref=x_ref, dst_ref=buf.at[me], send_sem=send_sems.at[k], recv_sem=recv_sems.at[k],
            device_id=peer, device_id_type=MESH) for k, peer in enumerate(peers)]
        for cp in sends:
            cp.start()
        for k, (px, py, pc) in enumerate(peers):
            pltpu.make_async_remote_copy(
                src_ref=x_ref, dst_ref=buf.at[4 * px + 2 * py + pc], send_sem=send_sems.at[k],
                recv_sem=recv_sems.at[k], device_id=(px, py, pc), device_id_type=MESH).wait_recv()
        for cp in sends:
            cp.wait_send()
        acc = buf[0]
        for d in range(1, n_dev):
            acc = acc + buf[d]
        o_ref[...] = acc

    vm = pl.BlockSpec(memory_space=pltpu.VMEM)
    return pl.pallas_call(
        body, name="allreduce_replicated", out_shape=jax.ShapeDtypeStruct((Rs, C), F32),
        in_specs=[vm], out_specs=vm,
        scratch_shapes=[pltpu.VMEM((n_dev, Rs, C), F32), pltpu.SemaphoreType.DMA((n_dev - 1,)),
                        pltpu.SemaphoreType.DMA((n_dev - 1,))],
    )(pack)


REPLICATED = ("mix_norm_even", "b_f", "conv_b", "ln_g", "ln_b", "ffn_norm", "ffn_conv_b", "final_norm")


def _pad_rows(flat, align_rows, cols):
    rows = -(-flat.shape[-1] // cols)
    rows = -(-rows // align_rows) * align_rows
    pad = rows * cols - flat.shape[-1]
    flat = jnp.pad(flat, [(0, 0)] * (flat.ndim - 1) + [(0, pad)])
    return flat.reshape(flat.shape[:-1] + (rows, cols))


def _pack_replicated(grads, loss):
    parts = [_pad_rows(grads[name].astype(F32).reshape(-1), 1, LANE).reshape(-1) for name in REPLICATED]
    parts.append(_pad_rows(loss.reshape(-1)[:1], 1, LANE).reshape(-1))
    return _pad_rows(jnp.concatenate(parts), 8, LANE)


def _unpack_replicated(reduced, shapes):
    flat = reduced.reshape(-1)
    out, off = {}, 0
    for name in REPLICATED:
        n = int(np.prod(shapes[name]))
        out[name] = flat[off:off + n].reshape(shapes[name])
        off += -(-n // LANE) * LANE
    return out, flat[off]


def _ffn_fwd2(h, W, layer, tm, tmm, host_up=None, host_act=None):
    tag = str(layer)
    n = _rms_fwd(h, W["ffn_norm"][layer:layer + 1], BF16, tm, f"ffn_norm_{tag}")
    up, *g_up = _mm(n, W["w_up"][layer], "nn", F32, tmm, UP_SHARD, f"ffn_up_{tag}", host=host_up) \
        if host_up else (_mm(n, W["w_up"][layer], "nn", F32, tmm, UP_SHARD, f"ffn_up_{tag}"),)
    act, *g_act = _ffn_act_fwd(up, W["ffn_conv_w_p"][layer], W["ffn_conv_b"][layer:layer + 1], tm,
                               f"ffn_act_{tag}", host=host_act) \
        if host_act else (_ffn_act_fwd(up, W["ffn_conv_w_p"][layer], W["ffn_conv_b"][layer:layer + 1], tm,
                                       f"ffn_act_{tag}"),)
    out = _mm(act, W["w_down"][layer], "nn", F32, tm, D_MODEL, f"ffn_down_{tag}", add=h)
    return out, (n, up, act), g_up + g_act


def _ffn_bwd2(h, W, layer, saved, dout, tm, tmm, reduce=None, dout_lp=None):
    tag = str(layer)
    n, up, act = saved
    parts, comm = [], None
    dmm = dout if dout_lp is None else dout_lp
    if reduce is None:
        dact = _mm(dmm, W["w_down"][layer], "nt", F32, tmm, UP_SHARD, f"ffn_dact_{tag}")
    else:
        names, fulls, cuts = reduce
        dact, *recv = _mm(dmm, W["w_down"][layer], "nt", F32, tmm, UP_SHARD, f"ffn_dact_{tag}",
                          host=(fulls, cuts, "pairx"))
        parts = [_pair_sum2(f, r, cut, PAIR_SUM_BLOCKS[nm], "grad_pair_sum_" + nm)
                 for f, r, cut, nm in zip(fulls, recv, cuts, names)]
        comm = (parts, cuts)
    dwd = _mm(act, dmm, "tn", F32, D_FF // 2, 512, f"ffn_dwdown_{tag}")
    dup, dcw, dcb, *others = _ffn_act_bwd(up, dact, W["ffn_conv_w_p"][layer], W["ffn_conv_b"][layer:layer + 1],
                                          tm, f"ffn_act_bwd_{tag}", comm=comm)
    dn = _mm_ffn_dn(dup, W["w_up"][layer], tm, D_MODEL, f"ffn_dn_{tag}")
    dwu = _mm_ffn_dwup(n, dup, 512, D_FF // 2, f"ffn_dwup_{tag}")
    dh, dgain = _rms_bwd(h, W["ffn_norm"][layer:layer + 1], dn, dout, tm, f"ffn_norm_bwd_{tag}")
    return dh, (dwu, dwd), dict(gain=dgain, cw=dcw[:FFN_K], cb=dcb), parts, others


GATHER_FIRST = ("w_in", "small")
GATHER_LATE = ("pool_w", "w_up", "w_down")
HOSTED_FFN = ("w_up1", "w_down1")
HOSTED = ("w_out", "pool_w", "w_up0", "w_down0")
LATE = ("small",)


def _local_step2(h0, tgt, W, n_real, cut_of):
    LP = h0.shape[0]
    tm, tmm = _row_tiles(LP)
    nb = LP // ATT_BLK
    G = {}
    n0 = _rms_fwd(h0, W["mix_norm_even"], BF16, tm, "mix_norm_even")
    sh = W["late_shards"]
    stage = lambda *names: ([sh[n] for n in names], [cut_of[n] for n in names])
    proj, g_down0 = _mm(n0, W["w_in_p"], "nn", F32, tmm, 896, "in_proj", host=stage("w_down0"))
    c = _fgate_fwd(proj, W["b_f_p"], "forget_gate")
    qT, kT, k_aug, vT = _attn_prep(proj, c, "attention_operands")
    oT, lse, g_pool, g_up0, g_out = _attn_fwd2(qT, k_aug, vT, "fox_attention",
                                               comm=stage("pool_w", "w_up0", "w_out"))
    g_down0, g_pool, g_up0, g_out = _gather_forward(
        [g_down0, g_pool, g_up0, g_out], stage("w_down0", "pool_w", "w_up0", "w_out")[1], "gather_forward_0")
    W = dict(W)
    W.update(pool_w=g_pool, w_up=[g_up0, None], w_down=[g_down0, None], w_out=g_out)
    u1, u = _conf_fwd(proj, W["conv_w_p"], W["conv_b"], W["ln_g"], W["ln_b"], tm, "conformer")
    cat = jnp.concatenate([_attn_rows(oT, 1.0, BF16, "attention_rows"), u], axis=1)
    h1 = _mm(cat, W["w_out"], "nn", F32, tmm, D_MODEL, "out_proj", add=h0)
    h2, ffn0, (g_down1, g_up1) = _ffn_fwd2(h1, W, 0, tm, tmm, host_up=stage("w_down1"), host_act=stage("w_up1"))
    g_down1, g_up1 = _gather_forward([g_down1, g_up1], stage("w_down1", "w_up1")[1], "gather_forward_1")
    W.update(w_up=[g_up0, g_up1], w_down=[g_down0, g_down1])
    h3, dpool = _pool_fwd(h2, W["mix_norm_odd"], W["pool_w"], W["pool_b"], W["pool_scale"], tm, "pool_mixer")
    h4, ffn1, _ = _ffn_fwd2(h3, W, 1, tm, tmm)
    loss, dh4, G["final_norm"], dh4b = _loss_head(h4, W["final_norm"], tgt, n_real, tm, "loss_head")

    dh3, (G["w_up1"], G["w_down1"]), g1, _, _ = _ffn_bwd2(h3, W, 1, ffn1, dh4, tm, tmm, dout_lp=dh4b)
    dh2, G["pool_w"], G["pool_b"], G["pool_scale"], G["mix_norm_odd"], dh2b = _pool_bwd(
        h2, W["mix_norm_odd"], dpool, W["pool_w"], W["pool_b"], W["pool_scale"], dh3, tm, "pool_mixer_bwd")
    cuts1 = [cut_of[n] for n in HOSTED_FFN]
    dh1, (G["w_up0"], G["w_down0"]), g0, parts1, others1 = _ffn_bwd2(
        h1, W, 0, ffn0, dh2, tm, tmm, reduce=(HOSTED_FFN, [G[n] for n in HOSTED_FFN], cuts1), dout_lp=dh2b)
    G["ffn_norm"] = jnp.concatenate([g0["gain"], g1["gain"]], axis=0)
    G["ffn_conv_w"] = jnp.stack([g0["cw"], g1["cw"]])
    G["ffn_conv_b"] = jnp.concatenate([g0["cb"], g1["cb"]], axis=0)

    dcat = _mm(dh1, W["w_out"], "nt", F32, tmm, D_MODEL, "out_proj_dx")
    G["w_out"] = _mm(cat, dh1, "tn", F32, 512, D_MODEL, "out_proj_dw")
    hcuts = [cut_of[n] for n in HOSTED]
    hfull = [G[n] for n in HOSTED]
    dadg, dcw, G["conv_b"], G["ln_g"], G["ln_b"], *hrecv = _conf_bwd(
        proj, u1, dcat, W["conv_w_p"], W["ln_g"], W["ln_b"], tm, "conformer_bwd", host=(hfull, hcuts, "pairx"))
    G["conv_w"] = dcw[:CONV_K]
    doT = _attn_cols(dcat, "attention_do_cols")
    hparts = [_pair_sum2(f, r, cut, PAIR_SUM_BLOCKS[n], "grad_pair_sum_" + n)
              for f, r, cut, n in zip(hfull, hrecv, hcuts, HOSTED)]
    dqT, dkT, dvT, *hothers = _attn_bwd2(qT, kT, k_aug, vT, oT, doT, lse, "fox_attention_bwd",
                                         comm=(hparts, hcuts))
    dqkv, dc = _attn_grads_rows(dqT, dkT, dvT, "attention_grads_rows")
    dfl, dbf = _fgate_bwd(proj, W["b_f_p"], dc, "forget_gate_bwd")
    G["b_f"] = dbf[:, :HEADS]
    dproj = jnp.concatenate([dqkv, dadg, dfl], axis=1)
    gp = _mm(n0, dproj, "tn", F32, 512, 896, "in_proj_dw")
    g_w_in = jnp.concatenate([gp[:, :3 * FOX_W], gp[:, 3 * FOX_W + 2 * CONV_CH:3 * FOX_W + 2 * CONV_CH + HEADS],
                              gp[:, 3 * FOX_W:3 * FOX_W + 2 * CONV_CH]], axis=1)
    g_w_in = g_w_in.reshape(D_MODEL, N_CHIPS, IN_SHARD).transpose(1, 0, 2)
    icut = [cut_of["w_in"]]
    dn0, irecv = _mm(dproj, W["w_in_p"], "nt", F32, tmm, D_MODEL, "in_proj_dx", host=([g_w_in], icut, "pairx"))
    ipart = _pair_sum2(g_w_in, irecv, icut[0], PAIR_SUM_BLOCKS["w_in"], "grad_pair_sum_w_in")
    dh0, G["mix_norm_even"], iother = _rms_bwd(h0, W["mix_norm_even"], dn0, dh1, tm, "mix_norm_even_bwd",
                                               host=([ipart], icut, "chipx"))
    parts = dict(zip(HOSTED_FFN + HOSTED + ("w_in",), parts1 + hparts + [ipart]))
    others = dict(zip(HOSTED_FFN + HOSTED + ("w_in",), list(others1) + list(hothers) + [iother]))
    return loss, dh0, G, parts, others


class _Cut:
    def __init__(self, full_shape, chip_dim, half_dim):
        self.full = tuple(full_shape)
        self.chip_dim, self.half_dim = chip_dim, half_dim
        self.chip_size = full_shape[chip_dim] // N_CHIPS
        self.half_size = full_shape[half_dim] // 2
        assert chip_dim != half_dim

    def shape(self, chip=False, half=False):
        s = list(self.full)
        if chip:
            s[self.chip_dim] = self.chip_size
        if half:
            s[self.half_dim] = self.half_size
        return tuple(s)

    def region(self, ref, chip=None, half=None):
        idx = [pl.ds(0, n) for n in ref.shape]
        if chip is not None:
            idx[self.chip_dim] = pl.ds(chip * self.chip_size, self.chip_size)
        if half is not None:
            idx[self.half_dim] = pl.ds(half * self.half_size, self.half_size)
        return ref.at[tuple(idx)]


SMALL_SHARDED = ("meta_tokens", "mix_norm_odd", "pool_b", "pool_scale", "conv_w", "ffn_conv_w")
SMALL_ROWS = 144


def _cuts():
    return {
        "w_in": _Cut((N_CHIPS, D_MODEL, IN_SHARD), 0, 1),
        "w_out": _Cut((D_MODEL, D_MODEL), 0, 1),
        "pool_w": _Cut((len(POOL_WINDOWS), POOL_G, POOL_G), 1, 0),
        "w_up": _Cut((2, D_MODEL, 2 * D_FF), 2, 1),
        "w_down": _Cut((2, D_FF, D_MODEL), 1, 2),
        "small": _Cut((N_CHIPS, SMALL_ROWS, LANE), 0, 1),
        "w_up0": _Cut((D_MODEL, 2 * D_FF), 1, 0), "w_up1": _Cut((D_MODEL, 2 * D_FF), 1, 0),
        "w_down0": _Cut((D_FF, D_MODEL), 0, 1), "w_down1": _Cut((D_FF, D_MODEL), 0, 1),
    }


COMM_ORDER = ("w_in", "w_out", "pool_w", "w_up", "w_down", "small")


def _remote(src, dst, send_sems, recv_sems, k, to):
    return pltpu.make_async_remote_copy(src_ref=src, dst_ref=dst, send_sem=send_sems.at[k],
                                        recv_sem=recv_sems.at[k], device_id=to, device_id_type=MESH)


def _gather_weights(shards, cuts):
    n = len(shards)

    def body(*refs):
        srcs, outs = refs[:n], refs[n:2 * n]
        send_sems, recv_sems = refs[2 * n:]
        x, y, c = _coords()
        me = 2 * x + y
        sibling = (x, y, 1 - c)
        chips = _other_chips(x, y)
        sends = []
        for t, cut in enumerate(cuts):
            push = _remote(srcs[t], cut.region(outs[t], chip=me), send_sems, recv_sems, 7 * t, sibling)
            push.start()
            sends.append(push)
            for kk, chip in enumerate(chips):
                cp = _remote(cut.region(srcs[t], half=c), cut.region(outs[t], chip=me, half=c),
                             send_sems, recv_sems, 7 * t + 1 + kk, (*chip, c))
                cp.start()
                sends.append(cp)
        for t, cut in enumerate(cuts):
            for kk, (px, py) in enumerate(chips):
                landed = cut.region(outs[t], chip=2 * px + py, half=c)
                _remote(landed, landed, send_sems, recv_sems, 7 * t + 1 + kk, sibling).wait_recv()
                fwd = _remote(landed, landed, send_sems, recv_sems, 7 * t + 4 + kk, sibling)
                fwd.start()
                sends.append(fwd)
        for t, cut in enumerate(cuts):
            mine = cut.region(outs[t], chip=me)
            _remote(mine, mine, send_sems, recv_sems, 7 * t, sibling).wait_recv()
            for kk, (px, py) in enumerate(chips):
                other = cut.region(outs[t], chip=2 * px + py, half=1 - c)
                _remote(other, other, send_sems, recv_sems, 7 * t + 4 + kk, sibling).wait_recv()
        for cp in sends:
            cp.wait_send()

    return pl.pallas_call(
        body, name="gather_weights",
        out_shape=tuple(jax.ShapeDtypeStruct(cut.full, s.dtype) for cut, s in zip(cuts, shards)),
        in_specs=[ANY] * n, out_specs=tuple([ANY] * n),
        scratch_shapes=[pltpu.SemaphoreType.DMA((7 * n,)), pltpu.SemaphoreType.DMA((7 * n,))],
    )(*shards)


def _gather_first_ops(srcs, outs, send_sems, recv_sems, cuts):
    x, y, c = _coords()
    me = 2 * x + y
    sibling = (x, y, 1 - c)
    chips = _other_chips(x, y)

    def copies():
        out = []
        for t, cut in enumerate(cuts):
            out.append(_remote(srcs[t], cut.region(outs[t], chip=me), send_sems, recv_sems, 4 * t, sibling))
            for kk, chip in enumerate(chips):
                out.append(_remote(cut.region(srcs[t], half=c), cut.region(outs[t], chip=me, half=c),
                                   send_sems, recv_sems, 4 * t + 1 + kk, (*chip, c)))
        return out

    def start():
        for cp in copies():
            cp.start()

    def wait():
        for t, cut in enumerate(cuts):
            mine = cut.region(outs[t], chip=me)
            _remote(mine, mine, send_sems, recv_sems, 4 * t, sibling).wait_recv()
            for kk, (px, py) in enumerate(chips):
                landed = cut.region(outs[t], chip=2 * px + py, half=c)
                _remote(landed, landed, send_sems, recv_sems, 4 * t + 1 + kk, sibling).wait_recv()
        for cp in copies():
            cp.wait_send()

    return start, wait


def _pair_exchange_ops(srcs, outs, send_sems, recv_sems, cuts):
    x, y, c = _coords()

    def copies():
        return [_remote(cut.region(srcs[t], half=1 - c), outs[t], send_sems, recv_sems, t, (x, y, 1 - c))
                for t, cut in enumerate(cuts)]

    def start():
        for cp in copies():
            cp.start()

    def wait():
        for cp in copies():
            cp.wait()

    return start, wait


def _host_plan(host):
    arrays, cuts = host[0], host[1]
    if len(host) > 2 and host[2] == "chipx":
        return _chip_exchange_shapes(arrays, cuts), 3 * len(arrays), _chip_exchange_ops
    if len(host) > 2 and host[2] == "pairx":
        return (tuple(jax.ShapeDtypeStruct(cut.shape(half=True), a.dtype) for cut, a in zip(cuts, arrays)),
                len(arrays), _pair_exchange_ops)
    return (tuple(jax.ShapeDtypeStruct(cut.full, a.dtype) for cut, a in zip(cuts, arrays)),
            4 * len(arrays), _gather_first_ops)


def _gather_forward(fulls, cuts, name):
    n = len(fulls)

    def body(*refs):
        outs = refs[n:2 * n]
        send_sems, recv_sems = refs[2 * n:]
        x, y, c = _coords()
        sibling = (x, y, 1 - c)
        chips = _other_chips(x, y)
        sends = []
        for t, cut in enumerate(cuts):
            for kk, (px, py) in enumerate(chips):
                landed = cut.region(outs[t], chip=2 * px + py, half=c)
                cp = _remote(landed, landed, send_sems, recv_sems, 3 * t + kk, sibling)
                cp.start()
                sends.append(cp)
        for t, cut in enumerate(cuts):
            for kk, (px, py) in enumerate(chips):
                other = cut.region(outs[t], chip=2 * px + py, half=1 - c)
                _remote(other, other, send_sems, recv_sems, 3 * t + kk, sibling).wait_recv()
        for cp in sends:
            cp.wait_send()

    return pl.pallas_call(
        body, name=name,
        out_shape=tuple(jax.ShapeDtypeStruct(f.shape, f.dtype) for f in fulls),
        in_specs=[ANY] * n, out_specs=tuple([ANY] * n), input_output_aliases={t: t for t in range(n)},
        scratch_shapes=[pltpu.SemaphoreType.DMA((3 * n,)), pltpu.SemaphoreType.DMA((3 * n,))],
    )(*fulls)


def _pair_exchange2(fulls, cuts, name):
    n = len(fulls)

    def body(*refs):
        srcs, outs = refs[:n], refs[n:2 * n]
        send_sems, recv_sems = refs[2 * n:]
        x, y, c = _coords()
        cps = [_remote(cut.region(srcs[t], half=1 - c), outs[t], send_sems, recv_sems, t, (x, y, 1 - c))
               for t, cut in enumerate(cuts)]
        for cp in cps:
            cp.start()
        for cp in cps:
            cp.wait()

    return pl.pallas_call(
        body, name=name,
        out_shape=tuple(jax.ShapeDtypeStruct(cut.shape(half=True), f.dtype) for cut, f in zip(cuts, fulls)),
        in_specs=[ANY] * n, out_specs=tuple([ANY] * n),
        scratch_shapes=[pltpu.SemaphoreType.DMA((n,)), pltpu.SemaphoreType.DMA((n,))],
    )(*fulls)


def _grid_of(shape, blk):
    assert all(s % b == 0 for s, b in zip(shape, blk)), (shape, blk)
    return tuple(s // b for s, b in zip(shape, blk))


def _pair_sum2(full, recv, cut, blk, name):
    hshape = cut.shape(half=True)
    grid = _grid_of(hshape, blk)
    hb = cut.half_size // blk[cut.half_dim]
    hd = cut.half_dim
    pos = jnp.stack([lax.axis_index("c")]).astype(jnp.int32)

    def full_idx(*a):
        ids, p = list(a[:-1]), a[-1]
        ids[hd] = ids[hd] + p[0] * hb
        return tuple(ids)

    def body(p_ref, f_ref, r_ref, o_ref):
        o_ref[...] = (f_ref[...] + r_ref[...]).astype(o_ref.dtype)

    return pl.pallas_call(
        body, name=name, out_shape=jax.ShapeDtypeStruct(hshape, BF16),
        grid_spec=pltpu.PrefetchScalarGridSpec(
            num_scalar_prefetch=1, grid=grid,
            in_specs=[pl.BlockSpec(blk, full_idx), pl.BlockSpec(blk, lambda *a: tuple(a[:-1]))],
            out_specs=pl.BlockSpec(blk, lambda *a: tuple(a[:-1]))),
        compiler_params=_params(("parallel",) * len(grid)))(pos, full, recv)


def _chip_exchange_ops(srcs, outs, send_sems, recv_sems, cuts):
    x, y, c = _coords()
    me = 2 * x + y
    chips = _other_chips(x, y)

    def copies():
        return [_remote(cut.region(srcs[t], chip=2 * px + py), outs[t].at[me], send_sems, recv_sems,
                        3 * t + kk, (px, py, c))
                for t, cut in enumerate(cuts) for kk, (px, py) in enumerate(chips)]

    def start():
        for cp in copies():
            cp.start()

    def wait():
        for t, cut in enumerate(cuts):
            for kk, (px, py) in enumerate(chips):
                slot = outs[t].at[2 * px + py]
                _remote(slot, slot, send_sems, recv_sems, 3 * t + kk, (px, py, c)).wait_recv()
        for cp in copies():
            cp.wait_send()

    return start, wait


def _chip_exchange_shapes(parts, cuts):
    return tuple(jax.ShapeDtypeStruct((N_CHIPS,) + cut.shape(chip=True, half=True), p.dtype)
                 for cut, p in zip(cuts, parts))


def _chip_exchange2(parts, cuts):
    n = len(parts)

    def body(*refs):
        start, wait = _chip_exchange_ops(refs[:n], refs[n:2 * n], refs[2 * n], refs[2 * n + 1], cuts)
        start()
        wait()

    return pl.pallas_call(
        body, name="grad_chip_exchange",
        out_shape=tuple(jax.ShapeDtypeStruct((N_CHIPS,) + cut.shape(chip=True, half=True), p.dtype)
                        for cut, p in zip(cuts, parts)),
        in_specs=[ANY] * n, out_specs=tuple([ANY] * n),
        scratch_shapes=[pltpu.SemaphoreType.DMA((3 * n,)), pltpu.SemaphoreType.DMA((3 * n,))],
    )(*parts)


def _chip_sum2(part, recv, cut, blk, name, stacked=None):
    bshape = cut.shape(chip=True, half=True)
    grid = _grid_of(bshape, blk)
    cb = cut.chip_size // blk[cut.chip_dim]
    hb = cut.half_size // blk[cut.half_dim]
    cd, hd = cut.chip_dim, cut.half_dim
    x, y, c = _coords()
    slots = [2 * px + py for px, py in _other_chips(x, y)]
    pos = jnp.stack([c, 2 * x + y] + slots).astype(jnp.int32)

    def part_idx(*a):
        ids, p = list(a[:-1]), a[-1]
        ids[cd] = ids[cd] + p[1] * cb
        return tuple(ids)

    def recv_idx(kk):
        return lambda *a: (a[-1][2 + kk],) + tuple(a[:-1])

    def out_idx(*a):
        ids, p = list(a[:-1]), a[-1]
        ids[hd] = ids[hd] + p[0] * hb
        return tuple(ids)

    def body(p_ref, own_ref, r0_ref, r1_ref, r2_ref, *rest):
        acc = own_ref[...].astype(F32)
        for r_ref in (r0_ref, r1_ref, r2_ref):
            acc = acc + r_ref[...].astype(F32)
        rest[-1][...] = acc

    in_specs = [pl.BlockSpec(blk, part_idx)] + [pl.BlockSpec((None,) + blk, recv_idx(kk)) for kk in range(3)]
    args = [pos, part, recv, recv, recv]
    aliases = {}
    if stacked is None:
        out_shape = jax.ShapeDtypeStruct(cut.shape(chip=True), F32)
        out_spec = pl.BlockSpec(blk, out_idx)
    else:
        lead, n_lead, into = stacked
        out_shape = jax.ShapeDtypeStruct((n_lead,) + cut.shape(chip=True), F32)
        out_spec = pl.BlockSpec((None,) + blk, lambda *a: (lead,) + out_idx(*a))
        if into is not None:
            in_specs.append(pl.BlockSpec(memory_space=pl.ANY))
            args.append(into)
            aliases = {5: 0}
    return pl.pallas_call(
        body, name=name, out_shape=out_shape,
        grid_spec=pltpu.PrefetchScalarGridSpec(num_scalar_prefetch=1, grid=grid, in_specs=in_specs,
                                               out_specs=out_spec),
        input_output_aliases=aliases, compiler_params=_params(("parallel",) * len(grid)))(*args)


def _pair_swap2(blocks, cuts):
    n = len(blocks)

    def body(*refs):
        outs = refs[n:2 * n]
        send_sems, recv_sems = refs[2 * n:]
        x, y, c = _coords()
        cps = []
        for t, cut in enumerate(cuts):
            mine = cut.region(outs[t], half=c)
            cp = _remote(mine, mine, send_sems, recv_sems, t, (x, y, 1 - c))
            cp.start()
            cps.append(cp)
        for t, cut in enumerate(cuts):
            theirs = cut.region(outs[t], half=1 - c)
            _remote(theirs, theirs, send_sems, recv_sems, t, (x, y, 1 - c)).wait_recv()
        for cp in cps:
            cp.wait_send()

    return pl.pallas_call(
        body, name="grad_pair_swap",
        out_shape=tuple(jax.ShapeDtypeStruct(b.shape, b.dtype) for b in blocks),
        in_specs=[ANY] * n, out_specs=tuple([ANY] * n),
        input_output_aliases={t: t for t in range(n)},
        scratch_shapes=[pltpu.SemaphoreType.DMA((n,)), pltpu.SemaphoreType.DMA((n,))],
    )(*blocks)


PAIR_SUM_BLOCKS = {"w_in": (1, 512, IN_SHARD), "w_out": (512, 512), "pool_w": (1, POOL_G, POOL_G),
                   "w_up0": (128, 2 * D_FF), "w_up1": (128, 2 * D_FF), "w_down0": (1408, 512), "w_down1": (1408, 512),
                   "small": (N_CHIPS, SMALL_ROWS // 2, LANE)}
CHIP_SUM_BLOCKS = {"w_in": (1, 512, IN_SHARD), "w_out": (256, 512), "pool_w": (2, 64, POOL_G),
                   "w_up0": (512, UP_SHARD), "w_up1": (512, UP_SHARD),
                   "w_down0": (DOWN_SHARD, 512), "w_down1": (DOWN_SHARD, 512),
                   "small": (1, SMALL_ROWS // 2, LANE)}


def _small_rows(t, lead):
    flat = t.reshape(lead + (-1,))
    pad = -flat.shape[-1] % LANE
    return jnp.pad(flat, [(0, 0)] * len(lead) + [(0, pad)]).reshape(lead + (-1, LANE))


def _pack_small_shards(shards):
    rows = jnp.concatenate([_small_rows(shards[n].astype(F32), ()) for n in SMALL_SHARDED], axis=0)
    return jnp.pad(rows, ((0, SMALL_ROWS - rows.shape[0]), (0, 0)))[None]


def _unpack_small(pack, shards, axes):
    out, off = {}, 0
    nchip = pack.shape[0]
    for name in SMALL_SHARDED:
        shp = shards[name].shape
        cnt = int(np.prod(shp))
        rows = -(-cnt // LANE)
        t = pack[:, off:off + rows].reshape(nchip, -1)[:, :cnt].reshape((nchip,) + shp)
        out[name] = jnp.concatenate([t[j] for j in range(nchip)], axis=axes[name])
        off += rows
    return out


def _pack_small_grads(grads, shards, axes):
    parts = []
    for name in SMALL_SHARDED:
        shp, ax = shards[name].shape, axes[name]
        g = grads[name].reshape(shp[:ax] + (N_CHIPS, shp[ax]) + shp[ax + 1:])
        parts.append(_small_rows(jnp.moveaxis(g, ax, 0), (N_CHIPS,)))
    rows = jnp.concatenate(parts, axis=1)
    return jnp.pad(rows, ((0, 0), (0, SMALL_ROWS - rows.shape[1]), (0, 0)))


SMALL_AXES = {"meta_tokens": 1, "mix_norm_odd": 1, "pool_b": 2, "pool_scale": 1, "conv_w": 2, "ffn_conv_w": 2}


WEIGHT_NAMES = ("meta_tokens", "mix_norm_even", "w_in", "b_f", "conv_w", "conv_b", "ln_g", "ln_b", "w_out",
                "mix_norm_odd", "pool_w", "pool_b", "pool_scale", "ffn_norm", "w_up", "ffn_conv_w",
                "ffn_conv_b", "w_down", "final_norm")


def kernel(x, meta_tokens, mix_norm_even, w_in, b_f, conv_w, conv_b, ln_g, ln_b, w_out, mix_norm_odd, pool_w, pool_b, pool_scale, ffn_norm, w_up, ffn_conv_w, ffn_conv_b, w_down, final_norm, loss_target, m_meta_tokens, m_mix_norm_even, m_w_in, m_b_f, m_conv_w, m_conv_b, m_ln_g, m_ln_b, m_w_out, m_mix_norm_odd, m_pool_w, m_pool_b, m_pool_scale, m_ffn_norm, m_w_up, m_ffn_conv_w, m_ffn_conv_b, m_w_down, m_final_norm, v_meta_tokens, v_mix_norm_even, v_w_in, v_b_f, v_conv_w, v_conv_b, v_ln_g, v_ln_b, v_w_out, v_mix_norm_odd, v_pool_w, v_pool_b, v_pool_scale, v_ffn_norm, v_w_up, v_ffn_conv_w, v_ffn_conv_b, v_w_down, v_final_norm):
    given = dict(locals())
    w_loc = {n: given[n] for n in WEIGHT_NAMES}
    m_loc = {n: given["m_" + n] for n in WEIGHT_NAMES}
    v_loc = {n: given["v_" + n] for n in WEIGHT_NAMES}
    cut_of = _cuts()
    cuts = [cut_of[n] for n in COMM_ORDER]
    big = ("w_in", "w_out", "pool_w", "w_up", "w_down")
    small_shards = {n: w_loc[n] for n in SMALL_SHARDED}

    shard_of = {n: w_loc[n].astype(BF16).reshape(cut_of[n].shape(chip=True)) for n in big}
    shard_of["small"] = _pack_small_shards(small_shards)
    g_in, g_small = _gather_weights([shard_of[n] for n in GATHER_FIRST], [cut_of[n] for n in GATHER_FIRST])
    g_out = None
    g_pool = g_up = g_down = None
    full = _unpack_small(g_small, small_shards, SMALL_AXES)
    full.update({n: w_loc[n] for n in REPLICATED})
    w_in_full = g_in.transpose(1, 0, 2).reshape(D_MODEL, IN_COLS)
    qkv, f, ag = (w_in_full[:, :3 * FOX_W], w_in_full[:, 3 * FOX_W:3 * FOX_W + HEADS],
                  w_in_full[:, 3 * FOX_W + HEADS:])
    W = dict(
        mix_norm_even=full["mix_norm_even"].reshape(1, D_MODEL),
        w_in_p=jnp.concatenate([qkv, ag, f, jnp.zeros((D_MODEL, LANE - HEADS), BF16)], axis=1),
        b_f_p=jnp.pad(full["b_f"].reshape(1, HEADS), ((0, 0), (0, LANE - HEADS))),
        conv_w_p=jnp.pad(full["conv_w"].reshape(CONV_K, CONV_CH), ((0, CONV_HALO - CONV_K), (0, 0))),
        conv_b=full["conv_b"].reshape(1, CONV_CH), ln_g=full["ln_g"].reshape(1, CONV_CH),
        ln_b=full["ln_b"].reshape(1, CONV_CH), w_out=g_out,
        mix_norm_odd=full["mix_norm_odd"].reshape(1, D_MODEL), pool_w=g_pool,
        pool_b=full["pool_b"].reshape(1, D_MODEL), pool_scale=full["pool_scale"].reshape(1, D_MODEL),
        ffn_norm=full["ffn_norm"], w_up=g_up,
        ffn_conv_w_p=jnp.pad(full["ffn_conv_w"], ((0, 0), (0, 8 - FFN_K), (0, 0))),
        ffn_conv_b=full["ffn_conv_b"], w_down=g_down, final_norm=full["final_norm"].reshape(1, D_MODEL),
        late_shards=dict(pool_w=shard_of["pool_w"], w_out=shard_of["w_out"],
                         w_up0=shard_of["w_up"][0], w_up1=shard_of["w_up"][1],
                         w_down0=shard_of["w_down"][0], w_down1=shard_of["w_down"][1]))

    seq = x.shape[1]
    n_real = N_META + seq
    LP = -(-n_real // ATT_BLK) * ATT_BLK
    tail = jnp.zeros((LP - n_real, D_MODEL), F32)
    h0 = jnp.concatenate([full["meta_tokens"], x[0], tail], axis=0)
    tgt = jnp.concatenate([jnp.zeros((N_META, D_MODEL), F32), loss_target[0], tail], axis=0)
    loss_loc, dh0, G, parts, others = _local_step2(h0, tgt, W, n_real, cut_of)
    grad_x = dh0[N_META:n_real][None]
    G["meta_tokens"] = dh0[:N_META]

    rep_shapes = {n: w_loc[n].shape for n in REPLICATED}
    G["final_norm"] = G["final_norm"].reshape(D_MODEL)
    rep, loss = _unpack_replicated(_allreduce_small(_pack_replicated(G, loss_loc)), rep_shapes)

    lcuts = [cut_of[n] for n in LATE]
    lfull = [_pack_small_grads(G, small_shards, SMALL_AXES)]
    lrecv = _pair_exchange2(lfull, lcuts, "grad_pair_exchange_late")
    lparts = [_pair_sum2(f, r, cut, PAIR_SUM_BLOCKS[n], "grad_pair_sum_" + n)
              for f, r, cut, n in zip(lfull, lrecv, lcuts, LATE)]
    parts.update(zip(LATE, lparts))
    others.update(zip(LATE, _chip_exchange2(lparts, lcuts)))
    def chip_sum(n, stacked=None):
        return _chip_sum2(parts[n], others[n], cut_of[n], CHIP_SUM_BLOCKS[n], "grad_chip_sum_" + n, stacked=stacked)

    blocks = []
    for n in COMM_ORDER:
        if n in ("w_up", "w_down"):
            blocks.append(chip_sum(n + "1", stacked=(1, 2, chip_sum(n + "0", stacked=(0, 2, None)))))
        else:
            blocks.append(chip_sum(n))
    blocks = _pair_swap2(blocks, cuts)
    gsh = {n: b.reshape(w_loc[n].shape) for n, b in zip(big, blocks[:5])}
    gsh.update(_unpack_small(blocks[5], small_shards, SMALL_AXES))
    sharded = set(big) | set(SMALL_SHARDED)

    grad_w = {n: (gsh[n] if n in sharded else rep[n]) for n in WEIGHT_NAMES}
    delta, new_m, new_v = {}, {}, {}
    for n in WEIGHT_NAMES:
        delta[n], new_m[n], new_v[n] = _adamw(w_loc[n], grad_w[n], m_loc[n], v_loc[n], "adamw_" + n)
    return (loss, grad_x, *[grad_w[n] for n in WEIGHT_NAMES], *[delta[n] for n in WEIGHT_NAMES],
            *[new_m[n] for n in WEIGHT_NAMES], *[new_v[n] for n in WEIGHT_NAMES])
```
